```python
import math
import jax, jax.numpy as jnp
from jax import lax
import numpy as np

D_MODEL = 1024
BATCH = 8
SEQ = 4096
DEPTH = 2

MIX_W = D_MODEL
SSD_HEAD_DIM = 64
SSD_INNER = MIX_W // 2
SSD_HEADS = SSD_INNER // SSD_HEAD_DIM
SSD_GROUPS = 2
SSD_HPG = SSD_HEADS // SSD_GROUPS
SSD_STATE = 128
SSD_CONV_K = 4
SSD_CHUNK = 256
SSD_CONV_CH = SSD_INNER + 2 * SSD_GROUPS * SSD_STATE
SSD_PROJ_W = SSD_INNER + SSD_CONV_CH + SSD_HEADS
POOL_W = MIX_W // 4
POOL_WINDOWS = (2, 4, 8, 16)
POOL_GROUPS = len(POOL_WINDOWS)
POOL_CH = POOL_W // POOL_GROUPS
ATT_HEAD_DIM = 64
ATT_W = MIX_W - SSD_INNER - POOL_W
ATT_HEADS = ATT_W // ATT_HEAD_DIM
ATT_PATTERNS = ((128, 1), (512, 4), (2048, 16))
ATT_BLOCK = 128
ROT_DIM = ATT_HEAD_DIM // 4
ROPE_THETA = 500000.0
IN_W = SSD_PROJ_W + POOL_W + 3 * ATT_W
FFN_DIM = 2816
FFN_CONV_K = 3
NORM_EPS = 1e-6

kernel_name = "hybrid_ssd_pool_dilated_attn_trunk"

F32 = jnp.float32


def rmsnorm(x, g):
    xf = x.astype(F32)
    y = xf * lax.rsqrt(jnp.mean(xf * xf, axis=-1, keepdims=True) + NORM_EPS)
    return (y * g.astype(F32)).astype(x.dtype)


def causal_dwconv(x, w, b):
    k, ch = w.shape
    y = lax.conv_general_dilated(x, w.astype(x.dtype)[:, None, :], window_strides=(1,),
                                 padding=[(k - 1, 0)], dimension_numbers=("NWC", "WIO", "NWC"),
                                 feature_group_count=ch)
    return y + b.astype(x.dtype)


def ssd_chunked(xs, da, bm, cm):
    b, s, g, j, p = xs.shape
    n = bm.shape[-1]
    pad = (-s) % SSD_CHUNK
    sp = s + pad
    nc = sp // SSD_CHUNK
    q = SSD_CHUNK
    xs = jnp.pad(xs, ((0, 0), (0, pad), (0, 0), (0, 0), (0, 0))).reshape(b, nc, q, g, j, p)
    da = jnp.pad(da, ((0, 0), (0, pad), (0, 0), (0, 0))).reshape(b, nc, q, g, j)
    bm = jnp.pad(bm, ((0, 0), (0, pad), (0, 0), (0, 0))).reshape(b, nc, q, g, n)
    cm = jnp.pad(cm, ((0, 0), (0, pad), (0, 0), (0, 0))).reshape(b, nc, q, g, n)
    a_cum = jnp.cumsum(da, axis=2)
    acs = jnp.moveaxis(a_cum, 2, -1)
    seg = acs[..., :, None] - acs[..., None, :]
    causal = jnp.tril(jnp.ones((q, q), dtype=bool))
    lmat = jnp.exp(jnp.where(causal, seg, -jnp.inf))
    cb = jnp.einsum("bclgn,bcsgn->bcgls", cm, bm)
    y_diag = jnp.einsum("bcgjls,bcsgjp->bclgjp", cb[:, :, :, None] * lmat, xs)
    decay_states = jnp.exp(a_cum[:, :, -1:] - a_cum)
    states = jnp.einsum("bclgn,bclgj,bclgjp->bcgjpn", bm, decay_states, xs)
    chunk_decay = jnp.exp(a_cum[:, :, -1])

    def step(h, inp):
        st, dec = inp
        return h * dec[..., None, None] + st, h

    h0 = jnp.zeros((b, g, j, p, n), xs.dtype)
    _, h_in = lax.scan(step, h0, (jnp.moveaxis(states, 1, 0), jnp.moveaxis(chunk_decay, 1, 0)))
    h_in = jnp.moveaxis(h_in, 0, 1)
    y_off = jnp.einsum("bclgn,bcgjpn,bclgj->bclgjp", cm, h_in, jnp.exp(a_cum))
    return (y_diag + y_off).reshape(b, sp, g, j, p)[:, :s]


def ssd_mixer(p_in, conv_w, conv_b, dt_bias, a_log, d_skip, norm_g):
    b, s, _ = p_in.shape
    z = p_in[..., :SSD_INNER].astype(F32)
    xbc = p_in[..., SSD_INNER:SSD_INNER + SSD_CONV_CH]
    dt = p_in[..., SSD_INNER + SSD_CONV_CH:].astype(F32)
    xbc = jax.nn.silu(causal_dwconv(xbc, conv_w, conv_b)).astype(F32)
    xs = xbc[..., :SSD_INNER].reshape(b, s, SSD_GROUPS, SSD_HPG, SSD_HEAD_DIM)
    bm = xbc[..., SSD_INNER:SSD_INNER + SSD_GROUPS * SSD_STATE].reshape(b, s, SSD_GROUPS, SSD_STATE)
    cm = xbc[..., SSD_INNER + SSD_GROUPS * SSD_STATE:].reshape(b, s, SSD_GROUPS, SSD_STATE)
    dt = jax.nn.softplus(dt + dt_bias.astype(F32)).reshape(b, s, SSD_GROUPS, SSD_HPG)
    a = -jnp.exp(a_log.astype(F32)).reshape(SSD_GROUPS, SSD_HPG)
    y = ssd_chunked(xs * dt[..., None], dt * a, bm, cm)
    y = y + d_skip.astype(F32).reshape(SSD_GROUPS, SSD_HPG, 1) * xs
    y = y.reshape(b, s, SSD_GROUPS, SSD_HPG * SSD_HEAD_DIM) * jax.nn.silu(z).reshape(b, s, SSD_GROUPS, -1)
    y = y * lax.rsqrt(jnp.mean(y * y, axis=-1, keepdims=True) + NORM_EPS)
    y = y * norm_g.astype(F32).reshape(SSD_GROUPS, -1)
    return y.reshape(b, s, SSD_INNER)


def pool_mixer(u, pool_w, pool_scale):
    b, s, _ = u.shape
    u = u.astype(F32).reshape(b, s, POOL_GROUPS, POOL_CH)
    cs = jnp.cumsum(u, axis=1)
    outs = []
    for gi, w in enumerate(POOL_WINDOWS):
        cg = cs[:, :, gi]
        lag = jnp.pad(cg, ((0, 0), (w, 0), (0, 0)))[:, :s]
        cnt = jnp.minimum(jnp.arange(1, s + 1), w).astype(F32)[None, :, None]
        outs.append((cg - lag) / cnt)
    pooled = jnp.stack(outs, axis=2)
    y = jnp.einsum("bsgc,gcd->bsgd", pooled - u, pool_w.astype(F32))
    return y.reshape(b, s, POOL_W) * pool_scale.astype(F32)


def partial_rope(t, cos, sin):
    half = ROT_DIM // 2
    t1, t2 = t[..., :half], t[..., half:ROT_DIM]
    c, s = cos[:, :, None], sin[:, :, None]
    return jnp.concatenate([t1 * c - t2 * s, t2 * c + t1 * s, t[..., ROT_DIM:]], axis=-1)


def dilated_branch(q, k, v, window, dilation):
    b, s, h, e = q.shape
    L = s // dilation
    steps = window // dilation
    n_prev = -(-steps // ATT_BLOCK)
    nb = -(-L // ATT_BLOCK)
    lp = nb * ATT_BLOCK

    def strided(t):
        t = t.reshape(b, L, dilation, h, e).transpose(0, 2, 3, 1, 4)
        t = jnp.pad(t, ((0, 0), (0, 0), (0, 0), (0, lp - L), (0, 0)))
        return t.reshape(b, dilation, h, nb, ATT_BLOCK, e)

    def band(t):
        tp = jnp.pad(t, ((0, 0), (0, 0), (0, 0), (n_prev, 0), (0, 0), (0, 0)))
        return jnp.concatenate([tp[:, :, :, j:j + nb] for j in range(n_prev + 1)], axis=4)

    qb, kb, vb = strided(q), strided(k), strided(v)
    kband, vband = band(kb), band(vb)
    sc = jnp.einsum("bdhnqe,bdhnke->bdhnqk", qb, kband) * (e ** -0.5)
    qi = jnp.arange(ATT_BLOCK)[:, None] + n_prev * ATT_BLOCK
    kj = jnp.arange((n_prev + 1) * ATT_BLOCK)[None, :]
    rel = qi - kj
    kpos = jnp.arange(nb)[:, None, None] * ATT_BLOCK + kj[None] - n_prev * ATT_BLOCK
    valid = (rel >= 0) & (rel <= steps) & (kpos >= 0)
    sc = jnp.where(valid, sc, -jnp.inf)
    m = jnp.max(sc, axis=-1, keepdims=True)
    p = jnp.exp(sc - m)
    l = jnp.sum(p, axis=-1)
    o = jnp.einsum("bdhnqk,bdhnke->bdhnqe", p, vband) / l[..., None]
    lse = m[..., 0] + jnp.log(l)
    o = o.reshape(b, dilation, h, lp, e)[:, :, :, :L].transpose(0, 3, 1, 2, 4).reshape(b, s, h, e)
    lse = lse.reshape(b, dilation, h, lp)[..., :L].transpose(0, 3, 1, 2).reshape(b, s, h)
    return o, lse


def dilated_attention(qkv, cos, sin):
    b, s, _ = qkv.shape
    qkv = qkv.astype(F32)
    q = partial_rope(qkv[..., :ATT_W].reshape(b, s, ATT_HEADS, ATT_HEAD_DIM), cos, sin)
    k = partial_rope(qkv[..., ATT_W:2 * ATT_W].reshape(b, s, ATT_HEADS, ATT_HEAD_DIM), cos, sin)
    v = qkv[..., 2 * ATT_W:].reshape(b, s, ATT_HEADS, ATT_HEAD_DIM)
    outs, lses = [], []
    for window, dilation in ATT_PATTERNS:
        o, lse = dilated_branch(q, k, v, window, dilation)
        outs.append(o)
        lses.append(lse)
    wts = jax.nn.softmax(jnp.stack(lses, axis=0), axis=0)
    o = jnp.einsum("rbsh,rbshe->bshe", wts, jnp.stack(outs, axis=0))
    return o.reshape(b, s, ATT_W)


def conv_ffn(h, up, conv_w, conv_b, down):
    hid = causal_dwconv(h @ up, conv_w, conv_b)
    g, u = jnp.split(hid, 2, axis=-1)
    return (jax.nn.silu(g) * u) @ down


def _fwd_setup_inputs(seed: int = 0) -> dict:
    key = jax.random.key(seed)
    ks = jax.random.split(key, 24)
    nrm = lambda k, shape, scale: jax.random.normal(k, shape, F32) * scale
    dt = jnp.exp(jax.random.uniform(ks[9], (DEPTH, SSD_HEADS), F32) * (math.log(0.1) - math.log(0.001))
                 + math.log(0.001))
    return {
        "x": nrm(ks[0], (BATCH, SEQ, D_MODEL), 1.0),
        "c": nrm(ks[1], (BATCH, D_MODEL), 1.0),
        "positions": (jax.random.randint(ks[2], (BATCH, 1), 0, 1024, jnp.int32)
                      + jnp.arange(SEQ, dtype=jnp.int32)[None, :]),
        "ada_w": nrm(ks[3], (DEPTH, D_MODEL, 6 * D_MODEL), 0.5 * D_MODEL ** -0.5),
        "ada_b": nrm(ks[4], (DEPTH, 6 * D_MODEL), 0.02),
        "norm1_g": 1.0 + nrm(ks[5], (DEPTH, D_MODEL), 0.02),
        "w_in": nrm(ks[6], (DEPTH, D_MODEL, IN_W), D_MODEL ** -0.5),
        "ssd_conv_w": nrm(ks[7], (DEPTH, SSD_CONV_K, SSD_CONV_CH), SSD_CONV_K ** -0.5),
        "ssd_conv_b": nrm(ks[8], (DEPTH, SSD_CONV_CH), 0.02),
        "ssd_dt_bias": dt + jnp.log(-jnp.expm1(-dt)),
        "ssd_a_log": jnp.log(jax.random.uniform(ks[10], (DEPTH, SSD_HEADS), F32, 1.0, 16.0)),
        "ssd_d": 1.0 + nrm(ks[11], (DEPTH, SSD_HEADS), 0.02),
        "ssd_norm_g": 1.0 + nrm(ks[12], (DEPTH, SSD_INNER), 0.02),
        "pool_w": nrm(ks[13], (DEPTH, POOL_GROUPS, POOL_CH, POOL_CH), POOL_CH ** -0.5),
        "pool_scale": 1.0 + nrm(ks[14], (DEPTH, POOL_W), 0.02),
        "w_out": nrm(ks[15], (DEPTH, MIX_W, D_MODEL), MIX_W ** -0.5),
        "norm2_g": 1.0 + nrm(ks[16], (DEPTH, D_MODEL), 0.02),
        "ffn_up": nrm(ks[17], (DEPTH, D_MODEL, 2 * FFN_DIM), D_MODEL ** -0.5),
        "ffn_conv_w": nrm(ks[18], (DEPTH, FFN_CONV_K, 2 * FFN_DIM), FFN_CONV_K ** -0.5),
        "ffn_conv_b": nrm(ks[19], (DEPTH, 2 * FFN_DIM), 0.02),
        "ffn_down": nrm(ks[20], (DEPTH, FFN_DIM, D_MODEL), FFN_DIM ** -0.5),
        "final_g": 1.0 + nrm(ks[21], (D_MODEL,), 0.02),
    }


def _fwd_reference(x, c, positions, ada_w, ada_b, norm1_g, w_in, ssd_conv_w, ssd_conv_b, ssd_dt_bias,
              ssd_a_log, ssd_d, ssd_norm_g, pool_w, pool_scale, w_out, norm2_g, ffn_up, ffn_conv_w,
              ffn_conv_b, ffn_down, final_g):
    inv_freq = ROPE_THETA ** (-jnp.arange(0, ROT_DIM, 2, dtype=F32) / ROT_DIM)
    ang = positions.astype(F32)[..., None] * inv_freq
    cos, sin = jnp.cos(ang), jnp.sin(ang)
    c_act = jax.nn.silu(c)
    a0 = SSD_PROJ_W
    a1 = SSD_PROJ_W + POOL_W
    for i in range(DEPTH):
        mod = (c_act @ ada_w[i] + ada_b[i])[:, None, :]
        sh1, sc1, g1, sh2, sc2, g2 = jnp.split(mod, 6, axis=-1)
        h = rmsnorm(x, norm1_g[i]) * (1.0 + sc1) + sh1
        proj = h @ w_in[i]
        y_ssd = ssd_mixer(proj[..., :a0], ssd_conv_w[i], ssd_conv_b[i], ssd_dt_bias[i],
                          ssd_a_log[i], ssd_d[i], ssd_norm_g[i])
        y_pool = pool_mixer(proj[..., a0:a1], pool_w[i], pool_scale[i])
        y_att = dilated_attention(proj[..., a1:], cos, sin)
        mix = jnp.concatenate([y_ssd, y_pool, y_att], axis=-1).astype(x.dtype)
        x = x + g2.dtype.type(1) * g1 * (mix @ w_out[i]) if False else x + g1 * (mix @ w_out[i])
        h = rmsnorm(x, norm2_g[i]) * (1.0 + sc2) + sh2
        x = x + g2 * conv_ffn(h, ffn_up[i], ffn_conv_w[i], ffn_conv_b[i], ffn_down[i])
    return rmsnorm(x, final_g)


import jax as _jax
import jax.numpy as _jnp

TWIN_FORMAT = 'train_step'
FWD_PARAMS = ['x', 'c', 'positions', 'ada_w', 'ada_b', 'norm1_g', 'w_in', 'ssd_conv_w', 'ssd_conv_b', 'ssd_dt_bias', 'ssd_a_log', 'ssd_d', 'ssd_norm_g', 'pool_w', 'pool_scale', 'w_out', 'norm2_g', 'ffn_up', 'ffn_conv_w', 'ffn_conv_b', 'ffn_down', 'final_g']
TWIN_WEIGHTS = ['ada_w', 'ada_b', 'norm1_g', 'w_in', 'ssd_conv_w', 'ssd_conv_b', 'ssd_dt_bias', 'ssd_a_log', 'ssd_d', 'ssd_norm_g', 'pool_w', 'pool_scale', 'w_out', 'norm2_g', 'ffn_up', 'ffn_conv_w', 'ffn_conv_b', 'ffn_down', 'final_g']
TWIN_DIFF_INPUT = 'x'
TWIN_INPUTS = ['x', 'c', 'positions', 'ada_w', 'ada_b', 'norm1_g', 'w_in', 'ssd_conv_w', 'ssd_conv_b', 'ssd_dt_bias', 'ssd_a_log', 'ssd_d', 'ssd_norm_g', 'pool_w', 'pool_scale', 'w_out', 'norm2_g', 'ffn_up', 'ffn_conv_w', 'ffn_conv_b', 'ffn_down', 'final_g', 'loss_target', 'm_ada_w', 'm_ada_b', 'm_norm1_g', 'm_w_in', 'm_ssd_conv_w', 'm_ssd_conv_b', 'm_ssd_dt_bias', 'm_ssd_a_log', 'm_ssd_d', 'm_ssd_norm_g', 'm_pool_w', 'm_pool_scale', 'm_w_out', 'm_norm2_g', 'm_ffn_up', 'm_ffn_conv_w', 'm_ffn_conv_b', 'm_ffn_down', 'm_final_g', 'v_ada_w', 'v_ada_b', 'v_norm1_g', 'v_w_in', 'v_ssd_conv_w', 'v_ssd_conv_b', 'v_ssd_dt_bias', 'v_ssd_a_log', 'v_ssd_d', 'v_ssd_norm_g', 'v_pool_w', 'v_pool_scale', 'v_w_out', 'v_norm2_g', 'v_ffn_up', 'v_ffn_conv_w', 'v_ffn_conv_b', 'v_ffn_down', 'v_final_g']
TWIN_OUTPUTS = ['loss', 'grad_x', 'grad_ada_w', 'grad_ada_b', 'grad_norm1_g', 'grad_w_in', 'grad_ssd_conv_w', 'grad_ssd_conv_b', 'grad_ssd_dt_bias', 'grad_ssd_a_log', 'grad_ssd_d', 'grad_ssd_norm_g', 'grad_pool_w', 'grad_pool_scale', 'grad_w_out', 'grad_norm2_g', 'grad_ffn_up', 'grad_ffn_conv_w', 'grad_ffn_conv_b', 'grad_ffn_down', 'grad_final_g', 'delta_ada_w', 'delta_ada_b', 'delta_norm1_g', 'delta_w_in', 'delta_ssd_conv_w', 'delta_ssd_conv_b', 'delta_ssd_dt_bias', 'delta_ssd_a_log', 'delta_ssd_d', 'delta_ssd_norm_g', 'delta_pool_w', 'delta_pool_scale', 'delta_w_out', 'delta_norm2_g', 'delta_ffn_up', 'delta_ffn_conv_w', 'delta_ffn_conv_b', 'delta_ffn_down', 'delta_final_g', 'new_m_ada_w', 'new_m_ada_b', 'new_m_norm1_g', 'new_m_w_in', 'new_m_ssd_conv_w', 'new_m_ssd_conv_b', 'new_m_ssd_dt_bias', 'new_m_ssd_a_log', 'new_m_ssd_d', 'new_m_ssd_norm_g', 'new_m_pool_w', 'new_m_pool_scale', 'new_m_w_out', 'new_m_norm2_g', 'new_m_ffn_up', 'new_m_ffn_conv_w', 'new_m_ffn_conv_b', 'new_m_ffn_down', 'new_m_final_g', 'new_v_ada_w', 'new_v_ada_b', 'new_v_norm1_g', 'new_v_w_in', 'new_v_ssd_conv_w', 'new_v_ssd_conv_b', 'new_v_ssd_dt_bias', 'new_v_ssd_a_log', 'new_v_ssd_d', 'new_v_ssd_norm_g', 'new_v_pool_w', 'new_v_pool_scale', 'new_v_w_out', 'new_v_norm2_g', 'new_v_ffn_up', 'new_v_ffn_conv_w', 'new_v_ffn_conv_b', 'new_v_ffn_down', 'new_v_final_g']
TWIN_LEAF_KINDS = {'loss': 'loss', 'grad_x': 'grad_x', 'grad_ada_w': 'grad_w', 'grad_ada_b': 'grad_w', 'grad_norm1_g': 'grad_w', 'grad_w_in': 'grad_w', 'grad_ssd_conv_w': 'grad_w', 'grad_ssd_conv_b': 'grad_w', 'grad_ssd_dt_bias': 'grad_w', 'grad_ssd_a_log': 'grad_w', 'grad_ssd_d': 'grad_w', 'grad_ssd_norm_g': 'grad_w', 'grad_pool_w': 'grad_w', 'grad_pool_scale': 'grad_w', 'grad_w_out': 'grad_w', 'grad_norm2_g': 'grad_w', 'grad_ffn_up': 'grad_w', 'grad_ffn_conv_w': 'grad_w', 'grad_ffn_conv_b': 'grad_w', 'grad_ffn_down': 'grad_w', 'grad_final_g': 'grad_w', 'delta_ada_w': 'delta_w', 'delta_ada_b': 'delta_w', 'delta_norm1_g': 'delta_w', 'delta_w_in': 'delta_w', 'delta_ssd_conv_w': 'delta_w', 'delta_ssd_conv_b': 'delta_w', 'delta_ssd_dt_bias': 'delta_w', 'delta_ssd_a_log': 'delta_w', 'delta_ssd_d': 'delta_w', 'delta_ssd_norm_g': 'delta_w', 'delta_pool_w': 'delta_w', 'delta_pool_scale': 'delta_w', 'delta_w_out': 'delta_w', 'delta_norm2_g': 'delta_w', 'delta_ffn_up': 'delta_w', 'delta_ffn_conv_w': 'delta_w', 'delta_ffn_conv_b': 'delta_w', 'delta_ffn_down': 'delta_w', 'delta_final_g': 'delta_w', 'new_m_ada_w': 'new_m', 'new_m_ada_b': 'new_m', 'new_m_norm1_g': 'new_m', 'new_m_w_in': 'new_m', 'new_m_ssd_conv_w': 'new_m', 'new_m_ssd_conv_b': 'new_m', 'new_m_ssd_dt_bias': 'new_m', 'new_m_ssd_a_log': 'new_m', 'new_m_ssd_d': 'new_m', 'new_m_ssd_norm_g': 'new_m', 'new_m_pool_w': 'new_m', 'new_m_pool_scale': 'new_m', 'new_m_w_out': 'new_m', 'new_m_norm2_g': 'new_m', 'new_m_ffn_up': 'new_m', 'new_m_ffn_conv_w': 'new_m', 'new_m_ffn_conv_b': 'new_m', 'new_m_ffn_down': 'new_m', 'new_m_final_g': 'new_m', 'new_v_ada_w': 'new_v', 'new_v_ada_b': 'new_v', 'new_v_norm1_g': 'new_v', 'new_v_w_in': 'new_v', 'new_v_ssd_conv_w': 'new_v', 'new_v_ssd_conv_b': 'new_v', 'new_v_ssd_dt_bias': 'new_v', 'new_v_ssd_a_log': 'new_v', 'new_v_ssd_d': 'new_v', 'new_v_ssd_norm_g': 'new_v', 'new_v_pool_w': 'new_v', 'new_v_pool_scale': 'new_v', 'new_v_w_out': 'new_v', 'new_v_norm2_g': 'new_v', 'new_v_ffn_up': 'new_v', 'new_v_ffn_conv_w': 'new_v', 'new_v_ffn_conv_b': 'new_v', 'new_v_ffn_down': 'new_v', 'new_v_final_g': 'new_v'}


def _forward(args):
    return _fwd_reference(*[args[k] for k in FWD_PARAMS])


def _output_shape():
    out = _jax.eval_shape(lambda: _forward(_fwd_setup_inputs(0)))
    return out.shape, out.dtype

N_MICROBATCH = 1
ADAM_LR = 0.001
ADAM_B1 = 0.9
ADAM_B2 = 0.999
ADAM_EPS = 1e-08
ADAM_WD = 0.01
ADAM_STEP = 10
PER_EXAMPLE_BATCH_AXIS = {'x': 0, 'c': 0, 'positions': 0, 'loss_target': 0}
SHARED_INPUTS = []
_WEIGHT_DTYPES = {'ada_w': _jnp.float32, 'ada_b': _jnp.float32, 'norm1_g': _jnp.float32, 'w_in': _jnp.float32, 'ssd_conv_w': _jnp.float32, 'ssd_conv_b': _jnp.float32, 'ssd_dt_bias': _jnp.float32, 'ssd_a_log': _jnp.float32, 'ssd_d': _jnp.float32, 'ssd_norm_g': _jnp.float32, 'pool_w': _jnp.float32, 'pool_scale': _jnp.float32, 'w_out': _jnp.float32, 'norm2_g': _jnp.float32, 'ffn_up': _jnp.float32, 'ffn_conv_w': _jnp.float32, 'ffn_conv_b': _jnp.float32, 'ffn_down': _jnp.float32, 'final_g': _jnp.float32}
MOMENT_SCALE = {'ada_w': 5.872037e-02, 'ada_b': 9.802023e-02, 'norm1_g': 6.100338e-02, 'w_in': 4.096886e-02, 'ssd_conv_w': 4.146015e-02, 'ssd_conv_b': 4.934491e-02, 'ssd_dt_bias': 1.060577e-01, 'ssd_a_log': 2.896337e-01, 'ssd_d': 9.219058e-01, 'ssd_norm_g': 5.225893e-02, 'pool_w': 4.917058e-02, 'pool_scale': 4.809144e-02, 'w_out': 4.658979e-02, 'norm2_g': 5.286345e-02, 'ffn_up': 2.326352e-02, 'ffn_conv_w': 2.364769e-02, 'ffn_conv_b': 2.215115e-02, 'ffn_down': 3.789857e-02, 'final_g': 3.206238e+01}


def _to_microbatches(a, axis):
    t = _jnp.moveaxis(a, axis, 0)
    t = t.reshape((N_MICROBATCH, t.shape[0] // N_MICROBATCH) + t.shape[1:])
    return _jnp.moveaxis(t, 1, axis + 1)


def setup_inputs(seed: int = 0) -> dict:
    inp = _fwd_setup_inputs(seed)
    key = _jax.random.fold_in(_jax.random.key(seed), 7919)
    shape, _ = _output_shape()
    out = dict(inp)
    out["loss_target"] = _jax.random.normal(_jax.random.fold_in(key, 0), shape, _jnp.float32)
    for i, name in enumerate(TWIN_WEIGHTS):
        w = inp[name].astype(_jnp.float32)
        if MOMENT_SCALE is None:
            s = _jnp.sqrt(_jnp.mean(_jnp.square(w)) + 1e-30)
        else:
            s = MOMENT_SCALE[name]
        km, kv = _jax.random.split(_jax.random.fold_in(key, i + 1))
        out[name] = w
        out["m_" + name] = s * _jax.random.normal(km, w.shape, _jnp.float32)
        out["v_" + name] = (s * s) * _jax.random.uniform(kv, w.shape, _jnp.float32, 0.5, 1.5)
    if N_MICROBATCH > 1:
        for name, axis in PER_EXAMPLE_BATCH_AXIS.items():
            out[name] = _to_microbatches(out[name], axis)
    return {'x': out['x'], 'c': out['c'], 'positions': out['positions'], 'ada_w': out['ada_w'], 'ada_b': out['ada_b'], 'norm1_g': out['norm1_g'], 'w_in': out['w_in'], 'ssd_conv_w': out['ssd_conv_w'], 'ssd_conv_b': out['ssd_conv_b'], 'ssd_dt_bias': out['ssd_dt_bias'], 'ssd_a_log': out['ssd_a_log'], 'ssd_d': out['ssd_d'], 'ssd_norm_g': out['ssd_norm_g'], 'pool_w': out['pool_w'], 'pool_scale': out['pool_scale'], 'w_out': out['w_out'], 'norm2_g': out['norm2_g'], 'ffn_up': out['ffn_up'], 'ffn_conv_w': out['ffn_conv_w'], 'ffn_conv_b': out['ffn_conv_b'], 'ffn_down': out['ffn_down'], 'final_g': out['final_g'], 'loss_target': out['loss_target'], 'm_ada_w': out['m_ada_w'], 'm_ada_b': out['m_ada_b'], 'm_norm1_g': out['m_norm1_g'], 'm_w_in': out['m_w_in'], 'm_ssd_conv_w': out['m_ssd_conv_w'], 'm_ssd_conv_b': out['m_ssd_conv_b'], 'm_ssd_dt_bias': out['m_ssd_dt_bias'], 'm_ssd_a_log': out['m_ssd_a_log'], 'm_ssd_d': out['m_ssd_d'], 'm_ssd_norm_g': out['m_ssd_norm_g'], 'm_pool_w': out['m_pool_w'], 'm_pool_scale': out['m_pool_scale'], 'm_w_out': out['m_w_out'], 'm_norm2_g': out['m_norm2_g'], 'm_ffn_up': out['m_ffn_up'], 'm_ffn_conv_w': out['m_ffn_conv_w'], 'm_ffn_conv_b': out['m_ffn_conv_b'], 'm_ffn_down': out['m_ffn_down'], 'm_final_g': out['m_final_g'], 'v_ada_w': out['v_ada_w'], 'v_ada_b': out['v_ada_b'], 'v_norm1_g': out['v_norm1_g'], 'v_w_in': out['v_w_in'], 'v_ssd_conv_w': out['v_ssd_conv_w'], 'v_ssd_conv_b': out['v_ssd_conv_b'], 'v_ssd_dt_bias': out['v_ssd_dt_bias'], 'v_ssd_a_log': out['v_ssd_a_log'], 'v_ssd_d': out['v_ssd_d'], 'v_ssd_norm_g': out['v_ssd_norm_g'], 'v_pool_w': out['v_pool_w'], 'v_pool_scale': out['v_pool_scale'], 'v_w_out': out['v_w_out'], 'v_norm2_g': out['v_norm2_g'], 'v_ffn_up': out['v_ffn_up'], 'v_ffn_conv_w': out['v_ffn_conv_w'], 'v_ffn_conv_b': out['v_ffn_conv_b'], 'v_ffn_down': out['v_ffn_down'], 'v_final_g': out['v_final_g']}


def _loss(weights, diff, rest, loss_target):
    with _jax.named_scope("forward"):
        args = {**rest, TWIN_DIFF_INPUT: diff, **{k: w.astype(_WEIGHT_DTYPES[k]) for k, w in weights.items()}}
        y = _forward(args)
    with _jax.named_scope("loss_head"):
        err = _jnp.square(y.astype(_jnp.float32) - loss_target)
        return 0.5 * _jnp.sum(_jnp.mean(err, axis=-1)) if err.ndim else 0.5 * err


def _adamw(w, g, m, v):
    m = ADAM_B1 * m + (1.0 - ADAM_B1) * g
    v = ADAM_B2 * v + (1.0 - ADAM_B2) * _jnp.square(g)
    m_hat = m / (1.0 - ADAM_B1 ** ADAM_STEP)
    v_hat = v / (1.0 - ADAM_B2 ** ADAM_STEP)
    delta = -ADAM_LR * (m_hat / (_jnp.sqrt(v_hat) + ADAM_EPS) + ADAM_WD * w)
    return delta, m, v


def reference(x, c, positions, ada_w, ada_b, norm1_g, w_in, ssd_conv_w, ssd_conv_b, ssd_dt_bias, ssd_a_log, ssd_d, ssd_norm_g, pool_w, pool_scale, w_out, norm2_g, ffn_up, ffn_conv_w, ffn_conv_b, ffn_down, final_g, loss_target, m_ada_w, m_ada_b, m_norm1_g, m_w_in, m_ssd_conv_w, m_ssd_conv_b, m_ssd_dt_bias, m_ssd_a_log, m_ssd_d, m_ssd_norm_g, m_pool_w, m_pool_scale, m_w_out, m_norm2_g, m_ffn_up, m_ffn_conv_w, m_ffn_conv_b, m_ffn_down, m_final_g, v_ada_w, v_ada_b, v_norm1_g, v_w_in, v_ssd_conv_w, v_ssd_conv_b, v_ssd_dt_bias, v_ssd_a_log, v_ssd_d, v_ssd_norm_g, v_pool_w, v_pool_scale, v_w_out, v_norm2_g, v_ffn_up, v_ffn_conv_w, v_ffn_conv_b, v_ffn_down, v_final_g):
    given = dict(x=x, c=c, positions=positions, ada_w=ada_w, ada_b=ada_b, norm1_g=norm1_g, w_in=w_in, ssd_conv_w=ssd_conv_w, ssd_conv_b=ssd_conv_b, ssd_dt_bias=ssd_dt_bias, ssd_a_log=ssd_a_log, ssd_d=ssd_d, ssd_norm_g=ssd_norm_g, pool_w=pool_w, pool_scale=pool_scale, w_out=w_out, norm2_g=norm2_g, ffn_up=ffn_up, ffn_conv_w=ffn_conv_w, ffn_conv_b=ffn_conv_b, ffn_down=ffn_down, final_g=final_g, loss_target=loss_target, m_ada_w=m_ada_w, m_ada_b=m_ada_b, m_norm1_g=m_norm1_g, m_w_in=m_w_in, m_ssd_conv_w=m_ssd_conv_w, m_ssd_conv_b=m_ssd_conv_b, m_ssd_dt_bias=m_ssd_dt_bias, m_ssd_a_log=m_ssd_a_log, m_ssd_d=m_ssd_d, m_ssd_norm_g=m_ssd_norm_g, m_pool_w=m_pool_w, m_pool_scale=m_pool_scale, m_w_out=m_w_out, m_norm2_g=m_norm2_g, m_ffn_up=m_ffn_up, m_ffn_conv_w=m_ffn_conv_w, m_ffn_conv_b=m_ffn_conv_b, m_ffn_down=m_ffn_down, m_final_g=m_final_g, v_ada_w=v_ada_w, v_ada_b=v_ada_b, v_norm1_g=v_norm1_g, v_w_in=v_w_in, v_ssd_conv_w=v_ssd_conv_w, v_ssd_conv_b=v_ssd_conv_b, v_ssd_dt_bias=v_ssd_dt_bias, v_ssd_a_log=v_ssd_a_log, v_ssd_d=v_ssd_d, v_ssd_norm_g=v_ssd_norm_g, v_pool_w=v_pool_w, v_pool_scale=v_pool_scale, v_w_out=v_w_out, v_norm2_g=v_norm2_g, v_ffn_up=v_ffn_up, v_ffn_conv_w=v_ffn_conv_w, v_ffn_conv_b=v_ffn_conv_b, v_ffn_down=v_ffn_down, v_final_g=v_final_g)
    weights = {n: given[n] for n in TWIN_WEIGHTS}
    shared = {n: given[n] for n in SHARED_INPUTS}
    per_example = {n: given[n] for n in ['x', 'c', 'positions']}
    grad_fn = _jax.value_and_grad(_loss, argnums=(0, 1))

    def one_microbatch(ex, loss_target):
        ex = dict(ex)
        diff = ex.pop(TWIN_DIFF_INPUT)
        return grad_fn(weights, diff, {**shared, **ex}, loss_target)

    if N_MICROBATCH == 1:
        loss, (grad_w, grad_x) = one_microbatch(per_example, given["loss_target"])
    else:
        def body(carry, xs):
            loss_sum, grad_sum = carry
            l_k, (gw_k, gx_k) = one_microbatch(xs[0], xs[1])
            with _jax.named_scope("update"):
                return (loss_sum + l_k, _jax.tree.map(_jnp.add, grad_sum, gw_k)), gx_k

        init = (_jnp.zeros((), _jnp.float32), _jax.tree.map(_jnp.zeros_like, weights))
        (loss, grad_w), grad_x = _jax.lax.scan(body, init, (per_example, given["loss_target"]))
    with _jax.named_scope("update"):
        delta_w, new_m, new_v = {}, {}, {}
        for n in TWIN_WEIGHTS:
            delta_w[n], new_m[n], new_v[n] = _adamw(weights[n], grad_w[n], given["m_" + n], given["v_" + n])
    return (loss, grad_x, *[grad_w[n] for n in TWIN_WEIGHTS], *[delta_w[n] for n in TWIN_WEIGHTS],
            *[new_m[n] for n in TWIN_WEIGHTS], *[new_v[n] for n in TWIN_WEIGHTS])
```

```python
import functools
import math

import jax
import jax.numpy as jnp
from jax import lax
from jax.experimental import pallas as pl
from jax.experimental.pallas import tpu as pltpu

F32 = jnp.float32
BF16 = jnp.bfloat16

N_DEV = 8
D_MODEL = 1024
SEQ = 4096
DEPTH = 2
SSD_INNER = 512
SSD_HEADS = 8
SSD_HPG = 4
SSD_STATE = 128
SSD_CHUNK = 256
SSD_CONV_K = 4
SSD_CONV_CH = 1024
POOL_W = 256
POOL_WINDOWS = (2, 4, 8, 16)
ATT_W = 256
ATT_PATTERNS = ((128, 1), (512, 4), (2048, 16))
ATT_BLOCK = 128
ROT_DIM = 16
ROPE_THETA = 500000.0
IN_W = 2568
FFN_DIM = 2816
FFN_CONV_K = 3
NORM_EPS = 1e-6
HEAD_LANES = 64

ADAM_LR = 0.001
ADAM_B1 = 0.9
ADAM_B2 = 0.999
ADAM_EPS = 1e-08
ADAM_WD = 0.01
ADAM_STEP = 10

PROJ_W = 2688
VMEM_LIMIT = 56 * 1024 * 1024
CONV_HALO = 8
POOL_HALO = 16
ATT_KPAD = ATT_BLOCK * 16
MESH = pl.DeviceIdType.MESH


def _cparams(*sem):
    return pltpu.CompilerParams(dimension_semantics=sem, vmem_limit_bytes=VMEM_LIMIT)


def _silu(x):
    return x * jax.nn.sigmoid(x)


def _pick_lane(v, h):
    lane = lax.broadcasted_iota(jnp.int32, v.shape, 1)
    return jnp.sum(jnp.where(lane == h, v, 0.0), axis=1, keepdims=True)


def _pick_row(v, h):
    row = lax.broadcasted_iota(jnp.int32, v.shape, 0)
    return jnp.sum(jnp.where(row == h, v, 0.0), axis=0, keepdims=True)


def _head_of_lane(width):
    return lax.broadcasted_iota(jnp.int32, (1, width), 1) // HEAD_LANES


@functools.partial(jax.custom_vjp, nondiff_argnums=(1, 2))
def _shift_rows(x_ext, s, halo):
    y = x_ext if s == 0 else pltpu.roll(x_ext, s, 0)
    return y[halo:]


def _shift_rows_fwd(x_ext, s, halo):
    return _shift_rows(x_ext, s, halo), None


def _shift_rows_bwd(s, halo, _, g):
    ge = jnp.concatenate([jnp.zeros((halo, g.shape[1]), g.dtype), g], axis=0)
    return (ge if s == 0 else pltpu.roll(ge, ge.shape[0] - s, 0),)


_shift_rows.defvjp(_shift_rows_fwd, _shift_rows_bwd)


@functools.partial(jax.custom_vjp, nondiff_argnums=(1,))
def _roll_rows(x, s):
    return pltpu.roll(x, s, 0)


def _roll_rows_fwd(x, s):
    return _roll_rows(x, s), None


def _roll_rows_bwd(s, _, g):
    return (pltpu.roll(g, g.shape[0] - s, 0),)


_roll_rows.defvjp(_roll_rows_fwd, _roll_rows_bwd)


def _mm(a, w, *, name, nt=False, tm=512, tn=None, out_dtype=F32):
    t, k = a.shape
    n = w.shape[0] if nt else w.shape[1]
    tn = tn or n

    def body(a_ref, w_ref, o_ref):
        av = a_ref[...].astype(BF16)
        if nt:
            acc = lax.dot_general(av, w_ref[...], (((1,), (1,)), ((), ())), preferred_element_type=F32)
        else:
            acc = jnp.dot(av, w_ref[...], preferred_element_type=F32)
        o_ref[...] = acc.astype(out_dtype)

    w_spec = pl.BlockSpec((tn, k), lambda i, j: (j, 0)) if nt else pl.BlockSpec((k, tn), lambda i, j: (0, j))
    return pl.pallas_call(
        body, grid=(t // tm, n // tn),
        in_specs=[pl.BlockSpec((tm, k), lambda i, j: (i, 0)), w_spec],
        out_specs=pl.BlockSpec((tm, tn), lambda i, j: (i, j)),
        out_shape=jax.ShapeDtypeStruct((t, n), out_dtype), name=name,
        compiler_params=_cparams("parallel", "parallel"))(a, w)


def _wgrad(a, b, *, name, tn=None, tt=512, out_dtype=BF16):
    t, k = a.shape
    n = b.shape[1]
    tn = tn or n
    steps = t // tt

    def body(a_ref, b_ref, o_ref, acc_ref):
        s = pl.program_id(1)

        @pl.when(s == 0)
        def _():
            acc_ref[...] = jnp.zeros_like(acc_ref)

        acc_ref[...] += lax.dot_general(a_ref[...].astype(BF16), b_ref[...].astype(BF16),
                                        (((0,), (0,)), ((), ())), preferred_element_type=F32)

        @pl.when(s == steps - 1)
        def _():
            o_ref[...] = acc_ref[...].astype(out_dtype)

    return pl.pallas_call(
        body, grid=(n // tn, steps),
        in_specs=[pl.BlockSpec((tt, k), lambda j, s: (s, 0)), pl.BlockSpec((tt, tn), lambda j, s: (s, j))],
        out_specs=pl.BlockSpec((k, tn), lambda j, s: (0, j)),
        out_shape=jax.ShapeDtypeStruct((k, n), out_dtype),
        scratch_shapes=[pltpu.VMEM((k, tn), F32)], name=name,
        compiler_params=_cparams("parallel", "arbitrary"))(a, b)


def _norm_mod(x, g, sc, sh, *, name, tm=512):
    s, d = x.shape

    def body(x_ref, g_ref, sc_ref, sh_ref, o_ref):
        xv = x_ref[...]
        r = lax.rsqrt(jnp.mean(xv * xv, axis=-1, keepdims=True) + NORM_EPS)
        o_ref[...] = ((xv * r * g_ref[...]) * (1.0 + sc_ref[...]) + sh_ref[...]).astype(BF16)

    row = pl.BlockSpec((1, d), lambda i: (0, 0))
    return pl.pallas_call(
        body, grid=(s // tm,), in_specs=[pl.BlockSpec((tm, d), lambda i: (i, 0)), row, row, row],
        out_specs=pl.BlockSpec((tm, d), lambda i: (i, 0)),
        out_shape=jax.ShapeDtypeStruct((s, d), BF16), name=name, compiler_params=_cparams("parallel"))(x, g, sc, sh)


def _norm_mod_bwd(x, dh, dres, g, sc, *, name, tm=512):
    s, d = x.shape
    steps = s // tm

    def body(x_ref, dh_ref, dres_ref, g_ref, sc_ref, dx_ref, dg_ref, dsc_ref, dsh_ref, da_acc, dsh_acc):
        i = pl.program_id(0)

        @pl.when(i == 0)
        def _():
            da_acc[...] = jnp.zeros_like(da_acc)
            dsh_acc[...] = jnp.zeros_like(dsh_acc)

        xv = x_ref[...]
        dhv = dh_ref[...].astype(F32)
        r = lax.rsqrt(jnp.mean(xv * xv, axis=-1, keepdims=True) + NORM_EPS)
        xhat = xv * r
        gain = g_ref[...] * (1.0 + sc_ref[...])
        dxhat = dhv * gain
        dx_ref[...] = dres_ref[...] + r * (dxhat - xhat * jnp.mean(dxhat * xhat, axis=-1, keepdims=True))
        da_acc[...] += jnp.sum(dhv * xhat, axis=0, keepdims=True)
        dsh_acc[...] += jnp.sum(dhv, axis=0, keepdims=True)

        @pl.when(i == steps - 1)
        def _():
            dg_ref[...] = da_acc[...] * (1.0 + sc_ref[...])
            dsc_ref[...] = da_acc[...] * g_ref[...]
            dsh_ref[...] = dsh_acc[...]

    row = pl.BlockSpec((1, d), lambda i: (0, 0))
    tile = pl.BlockSpec((tm, d), lambda i: (i, 0))
    row_shape = jax.ShapeDtypeStruct((1, d), F32)
    return pl.pallas_call(
        body, grid=(steps,), in_specs=[tile, tile, tile, row, row],
        out_specs=[tile, row, row, row],
        out_shape=[jax.ShapeDtypeStruct((s, d), F32), row_shape, row_shape, row_shape],
        scratch_shapes=[pltpu.VMEM((1, d), F32), pltpu.VMEM((1, d), F32)], name=name,
        compiler_params=_cparams("arbitrary"))(x, dh, dres, g, sc)


def _residual(x, gate, f, *, name, tm=512):
    s, d = x.shape

    def body(x_ref, g_ref, f_ref, o_ref):
        o_ref[...] = x_ref[...] + g_ref[...] * f_ref[...]

    tile = pl.BlockSpec((tm, d), lambda i: (i, 0))
    return pl.pallas_call(
        body, grid=(s // tm,), in_specs=[tile, pl.BlockSpec((1, d), lambda i: (0, 0)), tile], out_specs=tile,
        out_shape=jax.ShapeDtypeStruct((s, d), F32), name=name, compiler_params=_cparams("parallel"))(x, gate, f)


def _residual_bwd(dx, gate, f, *, name, tm=512):
    s, d = dx.shape
    steps = s // tm

    def body(dx_ref, g_ref, f_ref, df_ref, dg_ref):
        i = pl.program_id(0)

        @pl.when(i == 0)
        def _():
            dg_ref[...] = jnp.zeros_like(dg_ref)

        dxv = dx_ref[...]
        df_ref[...] = (g_ref[...] * dxv).astype(BF16)
        dg_ref[...] += jnp.sum(dxv * f_ref[...], axis=0, keepdims=True)

    tile = pl.BlockSpec((tm, d), lambda i: (i, 0))
    row = pl.BlockSpec((1, d), lambda i: (0, 0))
    return pl.pallas_call(
        body, grid=(steps,), in_specs=[tile, row, tile], out_specs=[tile, row],
        out_shape=[jax.ShapeDtypeStruct((s, d), BF16), jax.ShapeDtypeStruct((1, d), F32)], name=name,
        compiler_params=_cparams("arbitrary"))(dx, gate, f)


def _final_loss(x, g, target, *, name, tm=512):
    s, d = x.shape
    steps = s // tm

    def body(x_ref, g_ref, t_ref, loss_ref, dx_ref, dg_ref, sq_acc):
        i = pl.program_id(0)

        @pl.when(i == 0)
        def _():
            sq_acc[...] = jnp.zeros_like(sq_acc)
            dg_ref[...] = jnp.zeros_like(dg_ref)

        xv = x_ref[...]
        r = lax.rsqrt(jnp.mean(xv * xv, axis=-1, keepdims=True) + NORM_EPS)
        xhat = xv * r
        err = xhat * g_ref[...] - t_ref[...]
        sq_acc[...] += jnp.sum(err * err, axis=0, keepdims=True)
        dy = err * (1.0 / d)
        dg_ref[...] += jnp.sum(dy * xhat, axis=0, keepdims=True)
        dxhat = dy * g_ref[...]
        dx_ref[...] = r * (dxhat - xhat * jnp.mean(dxhat * xhat, axis=-1, keepdims=True))

        @pl.when(i == steps - 1)
        def _():
            total = jnp.sum(sq_acc[...], axis=1, keepdims=True) * (0.5 / d)
            loss_ref[...] = jnp.broadcast_to(total, loss_ref.shape)

    tile = pl.BlockSpec((tm, d), lambda i: (i, 0))
    row = pl.BlockSpec((1, d), lambda i: (0, 0))
    return pl.pallas_call(
        body, grid=(steps,), in_specs=[tile, row, tile],
        out_specs=[pl.BlockSpec((1, 128), lambda i: (0, 0)), tile, row],
        out_shape=[jax.ShapeDtypeStruct((1, 128), F32), jax.ShapeDtypeStruct((s, d), F32),
                   jax.ShapeDtypeStruct((1, d), F32)],
        scratch_shapes=[pltpu.VMEM((1, d), F32)], name=name, compiler_params=_cparams("arbitrary"))(x, g, target)


def _ssd_chunk(z, xbc_ext, dt_raw, conv_w, conv_b, dt_bias, a_log, d_skip, norm_g, h_in):
    q = z.shape[0]
    gw = SSD_HPG * HEAD_LANES
    xc = conv_b
    for k in range(SSD_CONV_K):
        xc = xc + _pick_row(conv_w, k) * _shift_rows(xbc_ext, SSD_CONV_K - 1 - k, CONV_HALO)
    xc = _silu(xc)
    dt = jax.nn.softplus(dt_raw + dt_bias)
    da = dt * (-jnp.exp(a_log))
    ri = lax.broadcasted_iota(jnp.int32, (q, q), 0)
    ci = lax.broadcasted_iota(jnp.int32, (q, q), 1)
    causal = ri >= ci
    tril = causal.astype(F32)
    a_cum = jnp.dot(tril, da, preferred_element_type=F32, precision=lax.Precision.HIGHEST)
    a_cum_t = lax.dot_general(da, tril, (((0,), (1,)), ((), ())), preferred_element_type=F32,
                              precision=lax.Precision.HIGHEST)
    a_last = _pick_row(a_cum, q - 1)
    head = _head_of_lane(gw)
    ys, hs = [], []
    for g in range(2):
        xs = xc[:, gw * g:gw * (g + 1)]
        bm = xc[:, SSD_INNER + SSD_STATE * g:SSD_INNER + SSD_STATE * (g + 1)]
        cm = xc[:, SSD_INNER + 2 * SSD_STATE + SSD_STATE * g:SSD_INNER + 2 * SSD_STATE + SSD_STATE * (g + 1)]
        cb = lax.dot_general(cm.astype(BF16), bm.astype(BF16), (((1,), (1,)), ((), ())), preferred_element_type=F32)
        cols = [_pick_lane(a_cum, SSD_HPG * g + j) for j in range(SSD_HPG)]
        lasts = [_pick_lane(a_last, SSD_HPG * g + j) for j in range(SSD_HPG)]
        dt_exp = sum(jnp.where(head == j, _pick_lane(dt, SSD_HPG * g + j), 0.0) for j in range(SSD_HPG))
        d_exp = sum(jnp.where(head == j, _pick_lane(d_skip, SSD_HPG * g + j), 0.0) for j in range(SSD_HPG))
        e_cum = sum(jnp.where(head == j, jnp.exp(cols[j]), 0.0) for j in range(SSD_HPG))
        c_dec = sum(jnp.where(head == j, jnp.exp(lasts[j]), 0.0) for j in range(SSD_HPG))
        xsdt = (xs * dt_exp).astype(BF16)
        y_diag = jnp.zeros((q, gw), F32)
        st_new = jnp.zeros((SSD_STATE, gw), F32)
        for j in range(SSD_HPG):
            row = _pick_row(a_cum_t, SSD_HPG * g + j)
            lmat = jnp.exp(jnp.where(causal, cols[j] - row, -jnp.inf))
            r = jnp.dot((cb * lmat).astype(BF16), xsdt, preferred_element_type=F32)
            y_diag = y_diag + jnp.where(head == j, r, 0.0)
            bd = (bm * jnp.exp(lasts[j] - cols[j])).astype(BF16)
            st = lax.dot_general(bd, xsdt, (((0,), (0,)), ((), ())), preferred_element_type=F32)
            st_new = st_new + jnp.where(head == j, st, 0.0)
        y_off = jnp.dot(cm.astype(BF16), h_in[g].astype(BF16), preferred_element_type=F32) * e_cum
        hs.append(h_in[g] * c_dec + st_new)
        y = y_diag + y_off + d_exp * xs
        yz = y * _silu(z[:, gw * g:gw * (g + 1)])
        yz = yz * lax.rsqrt(jnp.mean(yz * yz, axis=-1, keepdims=True) + NORM_EPS)
        ys.append(yz * norm_g[:, gw * g:gw * (g + 1)])
    return jnp.concatenate(ys, axis=1), tuple(hs)


_SSD_NCHUNK = SEQ // SSD_CHUNK
_HALO_PER_CHUNK = SSD_CHUNK // CONV_HALO


def _ssd_param_specs(const):
    return [pl.BlockSpec((8, SSD_CONV_CH), const), pl.BlockSpec((1, SSD_CONV_CH), const),
            pl.BlockSpec((1, 128), const), pl.BlockSpec((1, 128), const), pl.BlockSpec((1, 128), const),
            pl.BlockSpec((1, SSD_INNER), const)]


def _ssd_fwd(proj, conv_w, conv_b, dt_bias, a_log, d_skip, norm_g, *, name):
    q = SSD_CHUNK

    def body(z_ref, xbc_ref, halo_ref, dt_ref, cw_ref, cb_ref, db_ref, al_ref, d_ref, ng_ref, y_ref, hs_ref, h_acc):
        i = pl.program_id(0)

        @pl.when(i == 0)
        def _():
            h_acc[...] = jnp.zeros_like(h_acc)

        halo = jnp.where(i == 0, 0.0, halo_ref[...])
        xbc_ext = jnp.concatenate([halo, xbc_ref[...]], axis=0)
        h_in = (h_acc[0], h_acc[1])
        hs_ref[0, 0] = h_in[0]
        hs_ref[0, 1] = h_in[1]
        y, h_out = _ssd_chunk(z_ref[...], xbc_ext, dt_ref[...], cw_ref[...], cb_ref[...], db_ref[...], al_ref[...],
                              d_ref[...], ng_ref[...], h_in)
        y_ref[...] = y
        h_acc[0] = h_out[0]
        h_acc[1] = h_out[1]

    const = lambda i: (0, 0)
    return pl.pallas_call(
        body, grid=(_SSD_NCHUNK,),
        in_specs=[pl.BlockSpec((q, SSD_INNER), lambda i: (i, 2)),
                  pl.BlockSpec((q, SSD_CONV_CH), lambda i: (i, 0)),
                  pl.BlockSpec((CONV_HALO, SSD_CONV_CH), lambda i: (jnp.maximum(i * _HALO_PER_CHUNK - 1, 0), 0)),
                  pl.BlockSpec((q, 128), lambda i: (i, 20))] + _ssd_param_specs(const),
        out_specs=[pl.BlockSpec((q, SSD_INNER), lambda i: (i, 0)),
                   pl.BlockSpec((1, 2, SSD_STATE, 256), lambda i: (i, 0, 0, 0))],
        out_shape=[jax.ShapeDtypeStruct((SEQ, SSD_INNER), F32),
                   jax.ShapeDtypeStruct((_SSD_NCHUNK, 2, SSD_STATE, 256), F32)],
        scratch_shapes=[pltpu.VMEM((2, SSD_STATE, 256), F32)], name=name,
        compiler_params=_cparams("arbitrary"))(proj, proj, proj, proj, conv_w, conv_b, dt_bias, a_log, d_skip, norm_g)


def _ssd_bwd(proj, hstates, dmix, conv_w, conv_b, dt_bias, a_log, d_skip, norm_g, *, name):
    q = SSD_CHUNK
    last = _SSD_NCHUNK - 1

    def body(z_ref, xbc_ref, halo_ref, dt_ref, hs_ref, dy_ref, cw_ref, cb_ref, db_ref, al_ref, d_ref, ng_ref,
             dz_ref, dxbc_ref, ddt_ref, dcw_ref, dcb_ref, ddb_ref, dal_ref, dd_ref, dng_ref, dh_acc, dhalo_acc):
        i = pl.program_id(0)

        @pl.when(i == 0)
        def _():
            dh_acc[...] = jnp.zeros_like(dh_acc)
            dhalo_acc[...] = jnp.zeros_like(dhalo_acc)
            for r in (dcw_ref, dcb_ref, ddb_ref, dal_ref, dd_ref, dng_ref):
                r[...] = jnp.zeros_like(r)

        halo = jnp.where(i == last, 0.0, halo_ref[...])
        xbc_ext = jnp.concatenate([halo, xbc_ref[...]], axis=0)
        _, vjp = jax.vjp(_ssd_chunk, z_ref[...], xbc_ext, dt_ref[...], cw_ref[...], cb_ref[...], db_ref[...],
                         al_ref[...], d_ref[...], ng_ref[...], (hs_ref[0, 0], hs_ref[0, 1]))
        gz, gx, gdt, gcw, gcb, gdb, gal, gd, gng, gh = vjp((dy_ref[...], (dh_acc[0], dh_acc[1])))
        dz_ref[...] = gz
        ddt_ref[...] = gdt
        dxbc_ref[...] = gx[CONV_HALO:]
        dxbc_ref[q - CONV_HALO:, :] += dhalo_acc[...]
        dhalo_acc[...] = gx[:CONV_HALO]
        dh_acc[0] = gh[0]
        dh_acc[1] = gh[1]
        dcw_ref[...] += gcw
        dcb_ref[...] += gcb
        ddb_ref[...] += gdb
        dal_ref[...] += gal
        dd_ref[...] += gd
        dng_ref[...] += gng

    const = lambda i: (0, 0)
    rev = lambda i: last - i
    row = lambda n: jax.ShapeDtypeStruct((1, n), F32)
    return pl.pallas_call(
        body, grid=(_SSD_NCHUNK,),
        in_specs=[pl.BlockSpec((q, SSD_INNER), lambda i: (rev(i), 2)),
                  pl.BlockSpec((q, SSD_CONV_CH), lambda i: (rev(i), 0)),
                  pl.BlockSpec((CONV_HALO, SSD_CONV_CH), lambda i: (jnp.maximum(rev(i) * _HALO_PER_CHUNK - 1, 0), 0)),
                  pl.BlockSpec((q, 128), lambda i: (rev(i), 20)),
                  pl.BlockSpec((1, 2, SSD_STATE, 256), lambda i: (rev(i), 0, 0, 0)),
                  pl.BlockSpec((q, SSD_INNER), lambda i: (rev(i), 0))] + _ssd_param_specs(const),
        out_specs=[pl.BlockSpec((q, SSD_INNER), lambda i: (rev(i), 0)),
                   pl.BlockSpec((q, SSD_CONV_CH), lambda i: (rev(i), 0)),
                   pl.BlockSpec((q, 128), lambda i: (rev(i), 0))] + _ssd_param_specs(const),
        out_shape=[jax.ShapeDtypeStruct((SEQ, SSD_INNER), F32), jax.ShapeDtypeStruct((SEQ, SSD_CONV_CH), F32),
                   jax.ShapeDtypeStruct((SEQ, 128), F32), jax.ShapeDtypeStruct((8, SSD_CONV_CH), F32),
                   row(SSD_CONV_CH), row(128), row(128), row(128), row(SSD_INNER)],
        scratch_shapes=[pltpu.VMEM((2, SSD_STATE, 256), F32), pltpu.VMEM((CONV_HALO, SSD_CONV_CH), F32)], name=name,
        compiler_params=_cparams("arbitrary"))(proj, proj, proj, proj, hstates, dmix, conv_w, conv_b, dt_bias, a_log,
                                                d_skip, norm_g)


def _rope_tables(pos_col, inv_freq_lane, *, name):
    s = pos_col.shape[0]

    def body(p_ref, f_ref, c_ref, s1_ref, s2_ref):
        ang = p_ref[...] * f_ref[...]
        within = lax.broadcasted_iota(jnp.int32, ang.shape, 1) % HEAD_LANES
        half = ROT_DIM // 2
        c_ref[...] = jnp.where(within < ROT_DIM, jnp.cos(ang), 1.0)
        sn = jnp.sin(ang)
        s1_ref[...] = jnp.where(within < half, -sn, 0.0)
        s2_ref[...] = jnp.where((within >= half) & (within < ROT_DIM), sn, 0.0)

    shp = jax.ShapeDtypeStruct((s, 128), F32)
    return pl.pallas_call(body, out_shape=[shp, shp, shp], name=name,
                          compiler_params=pltpu.CompilerParams(vmem_limit_bytes=VMEM_LIMIT))(pos_col, inv_freq_lane)


def _rope(t, c, s1, s2):
    half = ROT_DIM // 2
    return t * c + pltpu.roll(t, 128 - half, 1) * s1 + pltpu.roll(t, half, 1) * s2


def _rope_t(g, c, s1, s2):
    half = ROT_DIM // 2
    return g * c + pltpu.roll(g * s1, half, 1) + pltpu.roll(g * s2, 128 - half, 1)


def _att_valid(b):
    qi = lax.broadcasted_iota(jnp.int32, (ATT_BLOCK, 2 * ATT_BLOCK), 0)
    kj = lax.broadcasted_iota(jnp.int32, (ATT_BLOCK, 2 * ATT_BLOCK), 1)
    rel = qi + ATT_BLOCK - kj
    return (rel >= 0) & (rel <= ATT_BLOCK) & (b * ATT_BLOCK + kj - ATT_BLOCK >= 0)


def _att_slices(i, d):
    if d == 1:
        qstart = pl.multiple_of(i * ATT_BLOCK, ATT_BLOCK)
        return i, pl.ds(qstart, ATT_BLOCK), pl.ds(pl.multiple_of(qstart - ATT_BLOCK + ATT_KPAD, ATT_BLOCK), 2 * ATT_BLOCK)
    r = i % d
    b = i // d
    qstart = r + d * ATT_BLOCK * b
    return b, pl.ds(qstart, ATT_BLOCK, stride=d), pl.ds(qstart - ATT_BLOCK * d + ATT_KPAD, 2 * ATT_BLOCK, stride=d)


_ATT_NBLK = SEQ // ATT_BLOCK
_ATT_SCALE = HEAD_LANES ** -0.5


def _att_fwd(proj, cos, sin1, sin2, *, name):
    s = SEQ

    def body(q_ref, k_ref, v_ref, c_ref, s1_ref, s2_ref, o_ref, lse_ref, qs, ks, vs, acc, m_s, l_s):
        c, s1, s2 = c_ref[...], s1_ref[...], s2_ref[...]
        qs[...] = _rope(q_ref[...], c, s1, s2) * _ATT_SCALE
        zeros = jnp.zeros((ATT_KPAD, 128), F32)
        ks[pl.ds(0, ATT_KPAD), :] = zeros
        vs[pl.ds(0, ATT_KPAD), :] = zeros
        ks[pl.ds(ATT_KPAD, s), :] = _rope(k_ref[...], c, s1, s2)
        vs[pl.ds(ATT_KPAD, s), :] = v_ref[...]
        head0 = _head_of_lane(128) == 0

        for bi, (_, d) in enumerate(ATT_PATTERNS):
            def blk(i, carry, d=d, first=(bi == 0)):
                b, sq, sk = _att_slices(i, d)
                qb = qs[sq, :]
                kw = ks[sk, :].astype(BF16)
                vw = vs[sk, :].astype(BF16)
                valid = _att_valid(b)
                ms, ls, os_ = [], [], []
                for hh in range(2):
                    qh = jnp.where(head0 if hh == 0 else ~head0, qb, 0.0).astype(BF16)
                    sc = lax.dot_general(qh, kw, (((1,), (1,)), ((), ())), preferred_element_type=F32)
                    sc = jnp.where(valid, sc, -jnp.inf)
                    mb = jnp.max(sc, axis=1, keepdims=True)
                    p = jnp.exp(sc - mb)
                    ms.append(mb)
                    ls.append(jnp.sum(p, axis=1, keepdims=True))
                    os_.append(jnp.dot(p.astype(BF16), vw, preferred_element_type=F32))
                m_b = jnp.where(head0, ms[0], ms[1])
                l_b = jnp.where(head0, ls[0], ls[1])
                o_b = jnp.where(head0, os_[0], os_[1])
                if first:
                    m_s[sq, :] = m_b
                    l_s[sq, :] = l_b
                    acc[sq, :] = o_b
                else:
                    m_old = m_s[sq, :]
                    m_new = jnp.maximum(m_old, m_b)
                    a_old = jnp.exp(m_old - m_new)
                    a_b = jnp.exp(m_b - m_new)
                    m_s[sq, :] = m_new
                    l_s[sq, :] = l_s[sq, :] * a_old + l_b * a_b
                    acc[sq, :] = acc[sq, :] * a_old + o_b * a_b
                return carry

            lax.fori_loop(0, _ATT_NBLK, blk, 0)

        o_ref[...] = acc[...] / l_s[...]
        lse_ref[...] = m_s[...] + jnp.log(l_s[...])

    col = lambda base: pl.BlockSpec((s, 128), lambda p: (0, base + p))
    tab = pl.BlockSpec((s, 128), lambda p: (0, 0))
    big = pltpu.VMEM((ATT_KPAD + s, 128), F32)
    tok = pltpu.VMEM((s, 128), F32)
    return pl.pallas_call(
        body, grid=(2,), in_specs=[col(14), col(16), col(18), tab, tab, tab],
        out_specs=[pl.BlockSpec((s, 128), lambda p: (0, p)), pl.BlockSpec((s, 128), lambda p: (0, p))],
        out_shape=[jax.ShapeDtypeStruct((s, ATT_W), F32), jax.ShapeDtypeStruct((s, ATT_W), F32)],
        scratch_shapes=[tok, big, big, tok, tok, tok], name=name,
        compiler_params=_cparams("arbitrary"))(proj, proj, proj, cos, sin1, sin2)


def _att_bwd(proj, cos, sin1, sin2, out, lse, dmix, *, name):
    s = SEQ

    def body(proj_ref, c_hbm, s1_hbm, s2_hbm, out_hbm, lse_hbm, dmix_hbm, dq_hbm, dk_hbm, dv_hbm,
             c_ref, s1_ref, s2_ref, o_ref, lse_ref, do_ref, qs, ks, vs, dqs, dks, dvs):
        pltpu.sync_copy(c_hbm, c_ref)
        pltpu.sync_copy(s1_hbm, s1_ref)
        pltpu.sync_copy(s2_hbm, s2_ref)
        head0 = _head_of_lane(128) == 0
        zeros = jnp.zeros((ATT_KPAD, 128), F32)
        for pair in range(2):
            lanes = pl.ds(128 * pair, 128)
            pltpu.sync_copy(proj_ref.at[:, pl.ds(128 * (14 + pair), 128)], qs)
            pltpu.sync_copy(proj_ref.at[:, pl.ds(128 * (16 + pair), 128)], ks.at[pl.ds(ATT_KPAD, s), :])
            pltpu.sync_copy(proj_ref.at[:, pl.ds(128 * (18 + pair), 128)], vs.at[pl.ds(ATT_KPAD, s), :])
            pltpu.sync_copy(out_hbm.at[:, lanes], o_ref)
            pltpu.sync_copy(lse_hbm.at[:, lanes], lse_ref)
            pltpu.sync_copy(dmix_hbm.at[:, pl.ds(128 * (6 + pair), 128)], do_ref)
            c, s1, s2 = c_ref[...], s1_ref[...], s2_ref[...]
            qs[...] = _rope(qs[...], c, s1, s2) * _ATT_SCALE
            ks[pl.ds(0, ATT_KPAD), :] = zeros
            vs[pl.ds(0, ATT_KPAD), :] = zeros
            ks[pl.ds(ATT_KPAD, s), :] = _rope(ks[pl.ds(ATT_KPAD, s), :], c, s1, s2)
            dqs[...] = jnp.zeros_like(dqs)
            dks[...] = jnp.zeros_like(dks)
            dvs[...] = jnp.zeros_like(dvs)

            for _, d in ATT_PATTERNS:
                def blk(i, carry, d=d):
                    b, sq, sk = _att_slices(i, d)
                    qb = qs[sq, :]
                    kw = ks[sk, :].astype(BF16)
                    vw = vs[sk, :].astype(BF16)
                    dob = do_ref[sq, :]
                    lse_b = lse_ref[sq, :]
                    dd = dob * o_ref[sq, :]
                    valid = _att_valid(b)
                    dq_b = jnp.zeros((ATT_BLOCK, 128), F32)
                    dk_w = jnp.zeros((2 * ATT_BLOCK, 128), F32)
                    dv_w = jnp.zeros((2 * ATT_BLOCK, 128), F32)
                    for hh in range(2):
                        hm = head0 if hh == 0 else ~head0
                        qh = jnp.where(hm, qb, 0.0).astype(BF16)
                        doh = jnp.where(hm, dob, 0.0).astype(BF16)
                        lse_h = _pick_lane(lse_b, hh * HEAD_LANES)
                        d_h = jnp.sum(jnp.where(hm, dd, 0.0), axis=1, keepdims=True)
                        sc = lax.dot_general(qh, kw, (((1,), (1,)), ((), ())), preferred_element_type=F32)
                        p = jnp.where(valid, jnp.exp(sc - lse_h), 0.0)
                        dp = lax.dot_general(doh, vw, (((1,), (1,)), ((), ())), preferred_element_type=F32)
                        ds = (p * (dp - d_h)).astype(BF16)
                        dq_b = dq_b + jnp.where(hm, jnp.dot(ds, kw, preferred_element_type=F32), 0.0)
                        dk_w = dk_w + lax.dot_general(ds, qh, (((0,), (0,)), ((), ())), preferred_element_type=F32)
                        dv_w = dv_w + lax.dot_general(p.astype(BF16), doh, (((0,), (0,)), ((), ())),
                                                      preferred_element_type=F32)
                    dqs[sq, :] += dq_b
                    dks[sk, :] += dk_w
                    dvs[sk, :] += dv_w
                    return carry

                lax.fori_loop(0, _ATT_NBLK, blk, 0)

            dqs[...] = _rope_t(dqs[...] * _ATT_SCALE, c, s1, s2)
            dks[pl.ds(ATT_KPAD, s), :] = _rope_t(dks[pl.ds(ATT_KPAD, s), :], c, s1, s2)
            pltpu.sync_copy(dqs, dq_hbm.at[:, lanes])
            pltpu.sync_copy(dks.at[pl.ds(ATT_KPAD, s), :], dk_hbm.at[:, lanes])
            pltpu.sync_copy(dvs.at[pl.ds(ATT_KPAD, s), :], dv_hbm.at[:, lanes])

    hbm = pl.BlockSpec(memory_space=pl.ANY)
    big = pltpu.VMEM((ATT_KPAD + s, 128), F32)
    tok = pltpu.VMEM((s, 128), F32)
    shp = jax.ShapeDtypeStruct((s, ATT_W), F32)
    return pl.pallas_call(
        body, in_specs=[hbm] * 7, out_specs=[hbm] * 3, out_shape=[shp, shp, shp],
        scratch_shapes=[tok] * 6 + [tok, big, big, tok, big, big], name=name,
        compiler_params=pltpu.CompilerParams(vmem_limit_bytes=VMEM_LIMIT))(proj, cos, sin1, sin2, out, lse, dmix)


_POOL_TM = 512
_POOL_NT = SEQ // _POOL_TM
_POOL_HALO_PER_TILE = _POOL_TM // POOL_HALO


def _pool_tile(u_ext, w_bd, scale, t0):
    s2 = u_ext + _roll_rows(u_ext, 1)
    s4 = s2 + _roll_rows(s2, 2)
    s8 = s4 + _roll_rows(s4, 4)
    s16 = s8 + _roll_rows(s8, 8)
    grp = _head_of_lane(POOL_W)
    sel = jnp.where(grp == 0, s2, jnp.where(grp == 1, s4, jnp.where(grp == 2, s8, s16)))[POOL_HALO:]
    t = sel.shape[0]
    pos = t0 + lax.broadcasted_iota(jnp.int32, (t, POOL_W), 0) + 1
    win = jnp.where(grp == 0, 2, jnp.where(grp == 1, 4, jnp.where(grp == 2, 8, 16)))
    cnt = jnp.minimum(pos, win).astype(F32)
    diff = sel / cnt - u_ext[POOL_HALO:]
    return jnp.dot(diff.astype(BF16), w_bd.astype(BF16), preferred_element_type=F32) * scale


def _pool_fwd(proj, w_bd, scale, *, name):
    tm = _POOL_TM

    def body(u_ref, halo_ref, w_ref, sc_ref, y_ref):
        i = pl.program_id(0)
        halo = jnp.where(i == 0, 0.0, halo_ref[...])
        u_ext = jnp.concatenate([halo, u_ref[...]], axis=0)
        y_ref[...] = _pool_tile(u_ext, w_ref[...], sc_ref[...], i * tm)

    return pl.pallas_call(
        body, grid=(_POOL_NT,),
        in_specs=[pl.BlockSpec((tm, POOL_W), lambda i: (i, 6)),
                  pl.BlockSpec((POOL_HALO, POOL_W), lambda i: (jnp.maximum(i * _POOL_HALO_PER_TILE - 1, 0), 6)),
                  pl.BlockSpec((POOL_W, POOL_W), lambda i: (0, 0)), pl.BlockSpec((1, POOL_W), lambda i: (0, 0))],
        out_specs=pl.BlockSpec((tm, POOL_W), lambda i: (i, 0)),
        out_shape=jax.ShapeDtypeStruct((SEQ, POOL_W), F32), name=name,
        compiler_params=_cparams("parallel"))(proj, proj, w_bd, scale)


def _pool_bwd(proj, dmix, w_bd, scale, *, name):
    tm = _POOL_TM
    last = _POOL_NT - 1

    def body(u_ref, halo_ref, dy_ref, w_ref, sc_ref, du_ref, dw_ref, dsc_ref, dhalo_acc):
        i = pl.program_id(0)

        @pl.when(i == 0)
        def _():
            dhalo_acc[...] = jnp.zeros_like(dhalo_acc)
            dw_ref[...] = jnp.zeros_like(dw_ref)
            dsc_ref[...] = jnp.zeros_like(dsc_ref)

        tile = last - i
        halo = jnp.where(tile == 0, 0.0, halo_ref[...])
        u_ext = jnp.concatenate([halo, u_ref[...]], axis=0)
        _, vjp = jax.vjp(functools.partial(_pool_tile, t0=tile * tm), u_ext, w_ref[...], sc_ref[...])
        gu, gw, gs = vjp(dy_ref[...])
        du_ref[...] = gu[POOL_HALO:]
        du_ref[tm - POOL_HALO:, :] += dhalo_acc[...]
        dhalo_acc[...] = gu[:POOL_HALO]
        dw_ref[...] += gw
        dsc_ref[...] += gs

    rev = lambda i: last - i
    return pl.pallas_call(
        body, grid=(_POOL_NT,),
        in_specs=[pl.BlockSpec((tm, POOL_W), lambda i: (rev(i), 6)),
                  pl.BlockSpec((POOL_HALO, POOL_W), lambda i: (jnp.maximum(rev(i) * _POOL_HALO_PER_TILE - 1, 0), 6)),
                  pl.BlockSpec((tm, POOL_W), lambda i: (rev(i), 2)),
                  pl.BlockSpec((POOL_W, POOL_W), lambda i: (0, 0)), pl.BlockSpec((1, POOL_W), lambda i: (0, 0))],
        out_specs=[pl.BlockSpec((tm, POOL_W), lambda i: (rev(i), 0)),
                   pl.BlockSpec((POOL_W, POOL_W), lambda i: (0, 0)), pl.BlockSpec((1, POOL_W), lambda i: (0, 0))],
        out_shape=[jax.ShapeDtypeStruct((SEQ, POOL_W), F32), jax.ShapeDtypeStruct((POOL_W, POOL_W), F32),
                   jax.ShapeDtypeStruct((1, POOL_W), F32)],
        scratch_shapes=[pltpu.VMEM((POOL_HALO, POOL_W), F32)], name=name,
        compiler_params=_cparams("arbitrary"))(proj, proj, dmix, w_bd, scale)


_FFN_TM = 256
_FFN_NT = SEQ // _FFN_TM
_FFN_HALO_PER_TILE = _FFN_TM // CONV_HALO


def _ffn_act_tile(hid_ext, conv_w, conv_b):
    hc = conv_b
    for k in range(FFN_CONV_K):
        hc = hc + _pick_row(conv_w, k) * _shift_rows(hid_ext, FFN_CONV_K - 1 - k, CONV_HALO)
    return _silu(hc[:, :FFN_DIM]) * hc[:, FFN_DIM:]


def _ffn_act(hid, conv_w, conv_b, *, name):
    tm = _FFN_TM
    w = 2 * FFN_DIM

    def body(h_ref, halo_ref, cw_ref, cb_ref, a_ref):
        i = pl.program_id(0)
        halo = jnp.where(i == 0, 0.0, halo_ref[...])
        hid_ext = jnp.concatenate([halo, h_ref[...]], axis=0)
        a_ref[...] = _ffn_act_tile(hid_ext, cw_ref[...], cb_ref[...]).astype(BF16)

    return pl.pallas_call(
        body, grid=(_FFN_NT,),
        in_specs=[pl.BlockSpec((tm, w), lambda i: (i, 0)),
                  pl.BlockSpec((CONV_HALO, w), lambda i: (jnp.maximum(i * _FFN_HALO_PER_TILE - 1, 0), 0)),
                  pl.BlockSpec((8, w), lambda i: (0, 0)), pl.BlockSpec((1, w), lambda i: (0, 0))],
        out_specs=pl.BlockSpec((tm, FFN_DIM), lambda i: (i, 0)),
        out_shape=jax.ShapeDtypeStruct((SEQ, FFN_DIM), BF16), name=name,
        compiler_params=_cparams("parallel"))(hid, hid, conv_w, conv_b)


def _ffn_act_bwd(hid, dact, conv_w, conv_b, *, name):
    tm = _FFN_TM
    w = 2 * FFN_DIM
    last = _FFN_NT - 1

    def body(h_ref, halo_ref, da_ref, cw_ref, cb_ref, dh_ref, dcw_ref, dcb_ref, dhalo_acc):
        i = pl.program_id(0)

        @pl.when(i == 0)
        def _():
            dhalo_acc[...] = jnp.zeros_like(dhalo_acc)
            dcw_ref[...] = jnp.zeros_like(dcw_ref)
            dcb_ref[...] = jnp.zeros_like(dcb_ref)

        halo = jnp.where(i == last, 0.0, halo_ref[...])
        hid_ext = jnp.concatenate([halo, h_ref[...]], axis=0)
        _, vjp = jax.vjp(_ffn_act_tile, hid_ext, cw_ref[...], cb_ref[...])
        gh, gw, gb = vjp(da_ref[...])
        carry = dhalo_acc[...]
        dhalo_acc[...] = gh[:CONV_HALO]
        dh_ref[...] = jnp.concatenate([gh[CONV_HALO:tm], gh[tm:] + carry], axis=0).astype(BF16)
        dcw_ref[...] += gw
        dcb_ref[...] += gb

    rev = lambda i: last - i
    return pl.pallas_call(
        body, grid=(_FFN_NT,),
        in_specs=[pl.BlockSpec((tm, w), lambda i: (rev(i), 0)),
                  pl.BlockSpec((CONV_HALO, w), lambda i: (jnp.maximum(rev(i) * _FFN_HALO_PER_TILE - 1, 0), 0)),
                  pl.BlockSpec((tm, FFN_DIM), lambda i: (rev(i), 0)),
                  pl.BlockSpec((8, w), lambda i: (0, 0)), pl.BlockSpec((1, w), lambda i: (0, 0))],
        out_specs=[pl.BlockSpec((tm, w), lambda i: (rev(i), 0)),
                   pl.BlockSpec((8, w), lambda i: (0, 0)), pl.BlockSpec((1, w), lambda i: (0, 0))],
        out_shape=[jax.ShapeDtypeStruct((SEQ, w), BF16), jax.ShapeDtypeStruct((8, w), F32),
                   jax.ShapeDtypeStruct((1, w), F32)],
        scratch_shapes=[pltpu.VMEM((CONV_HALO, w), F32)], name=name,
        compiler_params=_cparams("arbitrary"))(hid, hid, dact, conv_w, conv_b)


_HBM = pl.BlockSpec(memory_space=pl.ANY)


def _axes():
    return lax.axis_index("x"), lax.axis_index("y"), lax.axis_index("c")


def _allgather(xs, *, name):
    n = len(xs)

    def body(*refs):
        x_refs, out_refs = refs[:n], refs[n:2 * n]
        send_sems, recv_sems, local_sems = refs[2 * n:]
        x, y, c = _axes()
        me, sibling = (x, y, c), (x, y, 1 - c)
        chips = [(1 - x, y), (x, 1 - y), (1 - x, 1 - y)]

        def slot(a, px, py, pc):
            return out_refs[a].at[4 * px + 2 * py + pc]

        def copy(a, k, block, to, src=None):
            return pltpu.make_async_remote_copy(
                src_ref=slot(a, *block) if src is None else src, dst_ref=slot(a, *block),
                send_sem=send_sems.at[a, k], recv_sem=recv_sems.at[a, k], device_id=to, device_id_type=MESH)

        mines, firsts = [], []
        for a in range(n):
            mines.append(pltpu.make_async_copy(x_refs[a], slot(a, *me), local_sems.at[a]))
            mines[-1].start()
            first = [copy(a, 0, me, sibling, src=x_refs[a])]
            first += [copy(a, 1 + j, me, (*chip, c), src=x_refs[a]) for j, chip in enumerate(chips)]
            for cp in first:
                cp.start()
            firsts += first
        passed = []
        for j, chip in enumerate(chips):
            for a in range(n):
                copy(a, 1 + j, (*chip, c), me).wait_recv()
                passed.append(copy(a, 4 + j, (*chip, c), sibling))
                passed[-1].start()
        for a in range(n):
            copy(a, 0, sibling, me).wait_recv()
        for j, chip in enumerate(chips):
            for a in range(n):
                copy(a, 4 + j, (*chip, 1 - c), me).wait_recv()
        for cp in firsts + passed:
            cp.wait_send()
        for cp in mines:
            cp.wait()

    return pl.pallas_call(
        body, out_shape=[jax.ShapeDtypeStruct((N_DEV,) + xb.shape, xb.dtype) for xb in xs],
        in_specs=[_HBM] * n, out_specs=[_HBM] * n,
        scratch_shapes=[pltpu.SemaphoreType.DMA((n, 7)), pltpu.SemaphoreType.DMA((n, 7)),
                        pltpu.SemaphoreType.DMA((n,))],
        name=name)(*xs)


def _pair_exchange(blocks, *, name):
    n = len(blocks)

    def body(*refs):
        in_refs, out_refs = refs[:n], refs[n:2 * n]
        send_sems, recv_sems = refs[2 * n:]
        x, y, c = _axes()
        copies = [pltpu.make_async_remote_copy(
            src_ref=in_refs[a].at[2 * s + (1 - c)], dst_ref=out_refs[a].at[s], send_sem=send_sems.at[a, s],
            recv_sem=recv_sems.at[a, s], device_id=(x, y, 1 - c), device_id_type=MESH)
            for a in range(n) for s in range(4)]
        for cp in copies:
            cp.start()
        for cp in copies:
            cp.wait_recv()
        for cp in copies:
            cp.wait_send()

    return pl.pallas_call(
        body, out_shape=[jax.ShapeDtypeStruct((4,) + b.shape[1:], b.dtype) for b in blocks],
        in_specs=[_HBM] * n, out_specs=[_HBM] * n,
        scratch_shapes=[pltpu.SemaphoreType.DMA((n, 4)), pltpu.SemaphoreType.DMA((n, 4))], name=name)(*blocks)


def _chip_exchange(parts, *, name):
    n = len(parts)

    def body(*refs):
        in_refs, out_refs = refs[:n], refs[n:2 * n]
        send_sems, recv_sems, local_sems = refs[2 * n:]
        x, y, c = _axes()
        my_chip = 2 * x + y
        chips = [(1 - x, y), (x, 1 - y), (1 - x, 1 - y)]
        locals_ = [pltpu.make_async_copy(in_refs[a].at[my_chip], out_refs[a].at[my_chip], local_sems.at[a])
                   for a in range(n)]
        for cp in locals_:
            cp.start()
        copies = [pltpu.make_async_remote_copy(
            src_ref=in_refs[a].at[2 * px + py], dst_ref=out_refs[a].at[my_chip], send_sem=send_sems.at[a, k],
            recv_sem=recv_sems.at[a, k], device_id=(px, py, c), device_id_type=MESH)
            for a in range(n) for k, (px, py) in enumerate(chips)]
        for cp in copies:
            cp.start()
        for cp in copies:
            cp.wait_recv()
        for cp in copies:
            cp.wait_send()
        for cp in locals_:
            cp.wait()

    return pl.pallas_call(
        body, out_shape=[jax.ShapeDtypeStruct(p.shape, p.dtype) for p in parts],
        in_specs=[_HBM] * n, out_specs=[_HBM] * n,
        scratch_shapes=[pltpu.SemaphoreType.DMA((n, 3)), pltpu.SemaphoreType.DMA((n, 3)),
                        pltpu.SemaphoreType.DMA((n,))],
        name=name)(*parts)


def _add_blocks(a, b, *, name, tr=512):
    n, r, cdim = a.shape

    def body(a_ref, b_ref, o_ref):
        o_ref[...] = (a_ref[...].astype(F32) + b_ref[...].astype(F32)).astype(o_ref.dtype)

    spec = pl.BlockSpec((1, tr, cdim), lambda s, i: (s, i, 0))
    return pl.pallas_call(body, grid=(n, r // tr), in_specs=[spec, spec], out_specs=spec,
                          out_shape=jax.ShapeDtypeStruct(a.shape, a.dtype), name=name,
                          compiler_params=_cparams("parallel", "parallel"))(a, b)


def _sum_blocks(a, *, name, tr=None):
    n, r, cdim = a.shape
    tr = tr or r

    def body(a_ref, o_ref):
        acc = a_ref[0].astype(F32)
        for k in range(1, n):
            acc = acc + a_ref[k].astype(F32)
        o_ref[...] = acc

    return pl.pallas_call(body, grid=(r // tr,), in_specs=[pl.BlockSpec((n, tr, cdim), lambda i: (0, i, 0))],
                          out_specs=pl.BlockSpec((tr, cdim), lambda i: (i, 0)),
                          out_shape=jax.ShapeDtypeStruct((r, cdim), F32), name=name,
                          compiler_params=_cparams("parallel"))(a)


_ADA_SHARD = 6 * D_MODEL // N_DEV


def _ada_mod(c_all, ada_w, *, name):
    def body(c_ref, w_ref, o_ref):
        o_ref[0] = jnp.dot(_silu(c_ref[...]).astype(BF16), w_ref[0].astype(BF16), preferred_element_type=F32)

    return pl.pallas_call(
        body, grid=(DEPTH,),
        in_specs=[pl.BlockSpec((N_DEV, D_MODEL), lambda l: (0, 0)),
                  pl.BlockSpec((1, D_MODEL, _ADA_SHARD), lambda l: (l, 0, 0))],
        out_specs=pl.BlockSpec((1, N_DEV, _ADA_SHARD), lambda l: (l, 0, 0)),
        out_shape=jax.ShapeDtypeStruct((DEPTH, N_DEV, _ADA_SHARD), F32), name=name,
        compiler_params=_cparams("parallel"))(c_all, ada_w)


def _ada_wgrad(c_all, dmod_cols, *, name):
    def body(c_ref, d_ref, o_ref):
        o_ref[0] = lax.dot_general(_silu(c_ref[...]), d_ref[0], (((0,), (0,)), ((), ())),
                                   preferred_element_type=F32, precision=lax.Precision.HIGHEST)

    return pl.pallas_call(
        body, grid=(DEPTH,),
        in_specs=[pl.BlockSpec((N_DEV, D_MODEL), lambda l: (0, 0)),
                  pl.BlockSpec((1, N_DEV, _ADA_SHARD), lambda l: (l, 0, 0))],
        out_specs=pl.BlockSpec((1, D_MODEL, _ADA_SHARD), lambda l: (l, 0, 0)),
        out_shape=jax.ShapeDtypeStruct((DEPTH, D_MODEL, _ADA_SHARD), F32), name=name,
        compiler_params=_cparams("parallel"))(c_all, dmod_cols)


def _add_rows(a, b, *, name):
    def body(a_ref, b_ref, o_ref):
        o_ref[...] = a_ref[...] + b_ref[...]

    return pl.pallas_call(body, out_shape=jax.ShapeDtypeStruct(a.shape, a.dtype), name=name)(a, b)


def _adamw(w, g, m, v, *, name, tr):
    r, cdim = w.shape
    c1 = 1.0 - ADAM_B1 ** ADAM_STEP
    c2 = 1.0 - ADAM_B2 ** ADAM_STEP
    nblk = g.shape[0] if g.ndim == 3 else 0

    def body(w_ref, g_ref, m_ref, v_ref, go_ref, d_ref, mo_ref, vo_ref):
        if nblk:
            gv = g_ref[0].astype(F32)
            for k in range(1, nblk):
                gv = gv + g_ref[k].astype(F32)
        else:
            gv = g_ref[...]
        go_ref[...] = gv
        mn = ADAM_B1 * m_ref[...] + (1.0 - ADAM_B1) * gv
        vn = ADAM_B2 * v_ref[...] + (1.0 - ADAM_B2) * (gv * gv)
        mo_ref[...] = mn
        vo_ref[...] = vn
        d_ref[...] = -ADAM_LR * ((mn / c1) / (jnp.sqrt(vn / c2) + ADAM_EPS) + ADAM_WD * w_ref[...])

    spec = pl.BlockSpec((tr, cdim), lambda i: (i, 0))
    g_spec = pl.BlockSpec((nblk, tr, cdim), lambda i: (0, i, 0)) if nblk else spec
    shp = jax.ShapeDtypeStruct((r, cdim), F32)
    return pl.pallas_call(body, grid=(r // tr,), in_specs=[spec, g_spec, spec, spec], out_specs=[spec] * 4,
                          out_shape=[shp] * 4, name=name, compiler_params=_cparams("parallel"))(w, g, m, v)


def _pad_rows(a, rows):
    return jnp.concatenate([a, jnp.zeros((rows - a.shape[0],) + a.shape[1:], a.dtype)], axis=0)


def _pad_lanes(a, lanes):
    return jnp.concatenate([a, jnp.zeros(a.shape[:-1] + (lanes - a.shape[-1],), a.dtype)], axis=-1)


def _permute_w_in(w):
    return jnp.concatenate([w[:, 512:1536], w[:, :512], w[:, 1544:], w[:, 1536:1544],
                            jnp.zeros((w.shape[0], PROJ_W - IN_W), w.dtype)], axis=1)


def _unpermute_w_in(wp):
    return jnp.concatenate([wp[:, 1024:1536], wp[:, :1024], wp[:, 2560:2568], wp[:, 1536:2560]], axis=1)


def _block_diag(w):
    rows = []
    for g in range(4):
        rows.append(jnp.concatenate([w[g] if k == g else jnp.zeros_like(w[g]) for k in range(4)], axis=1))
    return jnp.concatenate(rows, axis=0)


def _diag_blocks(wbd):
    return jnp.stack([wbd[64 * g:64 * (g + 1), 64 * g:64 * (g + 1)] for g in range(4)], axis=0)


def _layer_params(l, big, small):
    return dict(
        w_in=big["w_in"][l], w_out=big["w_out"][l], up=big["ffn_up"][l], down=big["ffn_down"][l],
        norm1_g=small["norm1_g"][l][None], norm2_g=small["norm2_g"][l][None],
        conv_w=_pad_rows(small["ssd_conv_w"][l], 8), conv_b=small["ssd_conv_b"][l][None],
        dt_bias=_pad_lanes(small["ssd_dt_bias"][l][None], 128), a_log=_pad_lanes(small["ssd_a_log"][l][None], 128),
        d_skip=_pad_lanes(small["ssd_d"][l][None], 128), ssd_norm_g=small["ssd_norm_g"][l][None],
        pool_bd=_block_diag(small["pool_w"][l]), pool_scale=small["pool_scale"][l][None],
        fcw=_pad_rows(small["ffn_conv_w"][l], 8), fcb=small["ffn_conv_b"][l][None])


def _mod_rows(mod_l):
    return [mod_l[None, D_MODEL * i:D_MODEL * (i + 1)] for i in range(6)]


def _layer_fwd(x, mod_l, p, tabs, l):
    sh1, sc1, g1, sh2, sc2, g2 = _mod_rows(mod_l)
    h1 = _norm_mod(x, p["norm1_g"], sc1, sh1, name=f"l{l}_norm1")
    proj = _mm(h1, p["w_in"], name=f"l{l}_proj")
    y_ssd, hst = _ssd_fwd(proj, p["conv_w"], p["conv_b"], p["dt_bias"], p["a_log"], p["d_skip"], p["ssd_norm_g"],
                          name=f"l{l}_ssd")
    y_pool = _pool_fwd(proj, p["pool_bd"], p["pool_scale"], name=f"l{l}_pool")
    y_att, lse = _att_fwd(proj, *tabs, name=f"l{l}_att")
    mix = jnp.concatenate([y_ssd, y_pool, y_att], axis=1).astype(BF16)
    mo = _mm(mix, p["w_out"], name=f"l{l}_out")
    x1 = _residual(x, g1, mo, name=f"l{l}_res1")
    h2 = _norm_mod(x1, p["norm2_g"], sc2, sh2, name=f"l{l}_norm2")
    hid = _mm(h2, p["up"], tn=1408, name=f"l{l}_up")
    act = _ffn_act(hid, p["fcw"], p["fcb"], name=f"l{l}_act")
    f = _mm(act, p["down"], name=f"l{l}_down")
    x2 = _residual(x1, g2, f, name=f"l{l}_res2")
    return x2, dict(x=x, h1=h1, proj=proj, hst=hst, y_att=y_att, lse=lse, mix=mix, mo=mo, x1=x1, h2=h2, hid=hid,
                    act=act, f=f)


def _layer_bwd(dx2, sv, mod_l, p, tabs, l):
    sh1, sc1, g1, sh2, sc2, g2 = _mod_rows(mod_l)
    df, dg2 = _residual_bwd(dx2, g2, sv["f"], name=f"l{l}_res2_b")
    dact = _mm(df, p["down"], nt=True, tn=1408, name=f"l{l}_down_bx")
    d_down = _wgrad(sv["act"], df, name=f"l{l}_down_bw")
    dhid, dfcw, dfcb = _ffn_act_bwd(sv["hid"], dact, p["fcw"], p["fcb"], name=f"l{l}_act_b")
    dh2 = _mm(dhid, p["up"], nt=True, tm=256, name=f"l{l}_up_bx")
    d_up = _wgrad(sv["h2"], dhid, tn=1408, name=f"l{l}_up_bw")
    dx1, dn2, dsc2, dsh2 = _norm_mod_bwd(sv["x1"], dh2, dx2, p["norm2_g"], sc2, name=f"l{l}_norm2_b")
    dmo, dg1 = _residual_bwd(dx1, g1, sv["mo"], name=f"l{l}_res1_b")
    dmix = _mm(dmo, p["w_out"], nt=True, name=f"l{l}_out_bx")
    d_wout = _wgrad(sv["mix"], dmo, name=f"l{l}_out_bw")
    dz, dxbc, ddt, dcw, dcb, ddb, dal, dd, dng = _ssd_bwd(
        sv["proj"], sv["hst"], dmix, p["conv_w"], p["conv_b"], p["dt_bias"], p["a_log"], p["d_skip"],
        p["ssd_norm_g"], name=f"l{l}_ssd_b")
    du, dwbd, dpsc = _pool_bwd(sv["proj"], dmix, p["pool_bd"], p["pool_scale"], name=f"l{l}_pool_b")
    dq, dk, dv = _att_bwd(sv["proj"], *tabs, sv["y_att"], sv["lse"], dmix, name=f"l{l}_att_b")
    dproj = jnp.concatenate([dxbc, dz, du, dq, dk, dv, ddt], axis=1).astype(BF16)
    dh1 = _mm(dproj, p["w_in"], nt=True, name=f"l{l}_proj_bx")
    d_win = _wgrad(sv["h1"], dproj, name=f"l{l}_proj_bw")
    dx0, dn1, dsc1, dsh1 = _norm_mod_bwd(sv["x"], dh1, dx1, p["norm1_g"], sc1, name=f"l{l}_norm1_b")
    dmod = jnp.concatenate([dsh1, dsc1, dg1, dsh2, dsc2, dg2], axis=1)[0]
    big = dict(w_in=_unpermute_w_in(d_win), w_out=d_wout, ffn_up=d_up, ffn_down=d_down)
    small = dict(norm1_g=dn1[0], ssd_conv_w=dcw[:SSD_CONV_K], ssd_conv_b=dcb[0], ssd_dt_bias=ddb[0], ssd_a_log=dal[0],
                 ssd_d=dd[0], ssd_norm_g=dng[0], pool_w=_diag_blocks(dwbd), pool_scale=dpsc[0], norm2_g=dn2[0],
                 ffn_conv_w=dfcw[:FFN_CONV_K], ffn_conv_b=dfcb[0])
    return dx0, dmod, big, small


def _example_step(x, target, pos_col, inv_freq_lane, mod, big, small, final_g):
    tabs = _rope_tables(pos_col, inv_freq_lane, name="rope_tables")
    params = [_layer_params(l, big, small) for l in range(DEPTH)]
    saved = []
    for l in range(DEPTH):
        x, sv = _layer_fwd(x, mod[l], params[l], tabs, l)
        saved.append(sv)
    loss_row, dx, dfg = _final_loss(x, final_g[None], target, name="final_loss")
    dmods, bigs, smalls = [None] * DEPTH, [None] * DEPTH, [None] * DEPTH
    for l in reversed(range(DEPTH)):
        dx, dmods[l], bigs[l], smalls[l] = _layer_bwd(dx, saved[l], mod[l], params[l], tabs, l)
    return loss_row, dx, jnp.stack(dmods, axis=0), bigs, smalls, dfg[0]


_BIG = (("w_in", (D_MODEL, IN_W // N_DEV)), ("w_out", (D_MODEL // N_DEV, D_MODEL)),
        ("ffn_up", (D_MODEL, 2 * FFN_DIM // N_DEV)), ("ffn_down", (FFN_DIM // N_DEV, D_MODEL)))
_PACK_LANES = 1024

_SMALL_GRADS = (("norm1_g", D_MODEL), ("ssd_conv_w", SSD_CONV_K * SSD_CONV_CH), ("ssd_conv_b", SSD_CONV_CH),
                ("ssd_dt_bias", 128), ("ssd_a_log", 128), ("ssd_d", 128), ("ssd_norm_g", SSD_INNER),
                ("pool_w", 4 * 64 * 64), ("pool_scale", POOL_W), ("norm2_g", D_MODEL),
                ("ffn_conv_w", FFN_CONV_K * 2 * FFN_DIM), ("ffn_conv_b", 2 * FFN_DIM))
_SMALL_PARAMS = ("ada_b", "norm1_g", "ssd_conv_w", "ssd_conv_b", "ssd_dt_bias", "ssd_a_log", "ssd_d", "ssd_norm_g",
                 "pool_w", "pool_scale", "norm2_g", "ffn_conv_w", "ffn_conv_b", "final_g")
_WEIGHT_ORDER = ("ada_w", "ada_b", "norm1_g", "w_in", "ssd_conv_w", "ssd_conv_b", "ssd_dt_bias", "ssd_a_log", "ssd_d",
                 "ssd_norm_g", "pool_w", "pool_scale", "w_out", "norm2_g", "ffn_up", "ffn_conv_w", "ffn_conv_b",
                 "ffn_down", "final_g")


def _pack(parts, rows, dtype):
    flat = jnp.concatenate([p.reshape(-1).astype(dtype) for p in parts])
    return jnp.concatenate([flat, jnp.zeros((rows * _PACK_LANES - flat.shape[0],), dtype)]).reshape(rows, _PACK_LANES)


_COLUMN_SHARDED = ("w_in", "ffn_up")


def _unshard_big(gathered):
    out = {}
    for (name, (r, c)), g in zip(_BIG, gathered):
        g = g.reshape(N_DEV, DEPTH, r, c)
        if name in _COLUMN_SHARDED:
            full = g.transpose(1, 2, 0, 3).reshape(DEPTH, r, N_DEV * c)
        else:
            full = g.transpose(1, 0, 2, 3).reshape(DEPTH, N_DEV * r, c)
        out[name] = [_permute_w_in(full[l]) if name == "w_in" else full[l] for l in range(DEPTH)]
    return out


def _shard_big(grads):
    out = []
    for name, (r, c) in _BIG:
        g = jnp.stack([grads[l][name] for l in range(DEPTH)], axis=0)
        if name in _COLUMN_SHARDED:
            g = g.reshape(DEPTH, r, N_DEV, c).transpose(2, 0, 1, 3)
        else:
            g = g.reshape(DEPTH, N_DEV, r, c).transpose(1, 0, 2, 3)
        out.append(g.reshape(N_DEV, DEPTH * r, c))
    return out


def kernel(x, c, positions, ada_w, ada_b, norm1_g, w_in, ssd_conv_w, ssd_conv_b, ssd_dt_bias, ssd_a_log, ssd_d, ssd_norm_g, pool_w, pool_scale, w_out, norm2_g, ffn_up, ffn_conv_w, ffn_conv_b, ffn_down, final_g, loss_target, m_ada_w, m_ada_b, m_norm1_g, m_w_in, m_ssd_conv_w, m_ssd_conv_b, m_ssd_dt_bias, m_ssd_a_log, m_ssd_d, m_ssd_norm_g, m_pool_w, m_pool_scale, m_w_out, m_norm2_g, m_ffn_up, m_ffn_conv_w, m_ffn_conv_b, m_ffn_down, m_final_g, v_ada_w, v_ada_b, v_norm1_g, v_w_in, v_ssd_conv_w, v_ssd_conv_b, v_ssd_dt_bias, v_ssd_a_log, v_ssd_d, v_ssd_norm_g, v_pool_w, v_pool_scale, v_w_out, v_norm2_g, v_ffn_up, v_ffn_conv_w, v_ffn_conv_b, v_ffn_down, v_final_g):
    w = dict(ada_w=ada_w, ada_b=ada_b, norm1_g=norm1_g, w_in=w_in, ssd_conv_w=ssd_conv_w, ssd_conv_b=ssd_conv_b,
             ssd_dt_bias=ssd_dt_bias, ssd_a_log=ssd_a_log, ssd_d=ssd_d, ssd_norm_g=ssd_norm_g, pool_w=pool_w,
             pool_scale=pool_scale, w_out=w_out, norm2_g=norm2_g, ffn_up=ffn_up, ffn_conv_w=ffn_conv_w,
             ffn_conv_b=ffn_conv_b, ffn_down=ffn_down, final_g=final_g)
    m = dict(ada_w=m_ada_w, ada_b=m_ada_b, norm1_g=m_norm1_g, w_in=m_w_in, ssd_conv_w=m_ssd_conv_w,
             ssd_conv_b=m_ssd_conv_b, ssd_dt_bias=m_ssd_dt_bias, ssd_a_log=m_ssd_a_log, ssd_d=m_ssd_d,
             ssd_norm_g=m_ssd_norm_g, pool_w=m_pool_w, pool_scale=m_pool_scale, w_out=m_w_out, norm2_g=m_norm2_g,
             ffn_up=m_ffn_up, ffn_conv_w=m_ffn_conv_w, ffn_conv_b=m_ffn_conv_b, ffn_down=m_ffn_down,
             final_g=m_final_g)
    v = dict(ada_w=v_ada_w, ada_b=v_ada_b, norm1_g=v_norm1_g, w_in=v_w_in, ssd_conv_w=v_ssd_conv_w,
             ssd_conv_b=v_ssd_conv_b, ssd_dt_bias=v_ssd_dt_bias, ssd_a_log=v_ssd_a_log, ssd_d=v_ssd_d,
             ssd_norm_g=v_ssd_norm_g, pool_w=v_pool_w, pool_scale=v_pool_scale, w_out=v_w_out, norm2_g=v_norm2_g,
             ffn_up=v_ffn_up, ffn_conv_w=v_ffn_conv_w, ffn_conv_b=v_ffn_conv_b, ffn_down=v_ffn_down,
             final_g=v_final_g)
    ix, iy, ic = _axes()
    dev = 4 * ix + 2 * iy + ic

    n_scw, n_fcw = ssd_conv_w.size, ffn_conv_w.size
    small_all = _allgather([_pack([c, ssd_conv_w, ffn_conv_w], 8, F32)], name="gather_small")[0].reshape(N_DEV, -1)
    c_all = small_all[:, :D_MODEL]
    scw = small_all[:, D_MODEL:D_MODEL + n_scw].reshape(N_DEV, DEPTH, SSD_CONV_K, -1)
    scw = scw.transpose(1, 2, 0, 3).reshape(DEPTH, SSD_CONV_K, SSD_CONV_CH)
    fcw = small_all[:, D_MODEL + n_scw:D_MODEL + n_scw + n_fcw].reshape(N_DEV, DEPTH, FFN_CONV_K, -1)
    fcw = fcw.transpose(1, 2, 0, 3).reshape(DEPTH, FFN_CONV_K, 2 * FFN_DIM)

    mod_cols = _ada_mod(c_all, ada_w, name="ada_mod")
    mod_all = _allgather([mod_cols.reshape(DEPTH * N_DEV, _ADA_SHARD)], name="gather_mod")[0]
    mod_all = mod_all.reshape(N_DEV, DEPTH, N_DEV, _ADA_SHARD)
    mod_mine = lax.dynamic_index_in_dim(mod_all, dev, axis=2, keepdims=False)
    mod = _add_rows(mod_mine.transpose(1, 0, 2).reshape(DEPTH, 6 * D_MODEL), ada_b, name="ada_bias")

    shares = [w[name].astype(BF16).reshape(DEPTH * r, cdim) for name, (r, cdim) in _BIG]
    big = _unshard_big(_allgather(shares, name="gather_weights"))
    small = dict(norm1_g=norm1_g, norm2_g=norm2_g, ssd_conv_w=scw, ssd_conv_b=ssd_conv_b, ssd_dt_bias=ssd_dt_bias,
                 ssd_a_log=ssd_a_log, ssd_d=ssd_d, ssd_norm_g=ssd_norm_g, pool_w=pool_w, pool_scale=pool_scale,
                 ffn_conv_w=fcw, ffn_conv_b=ffn_conv_b)

    inv_freq = ROPE_THETA ** (-jnp.arange(0, ROT_DIM, 2, dtype=F32) / ROT_DIM)
    lane = jnp.arange(128) % HEAD_LANES
    inv_freq_lane = jnp.where(lane < ROT_DIM, inv_freq[lane % (ROT_DIM // 2)], 0.0)[None, :]
    pos_col = positions.reshape(SEQ, 1).astype(F32)
    loss_row, dx, dmod, g_big, g_small, g_final = _example_step(
        x[0], loss_target[0], pos_col, inv_freq_lane, mod, big, small, final_g)

    blocks = _shard_big(g_big)
    from_sibling = _pair_exchange(blocks, name="grads_pair_exchange")
    parts = []
    for (name, _), b, s in zip(_BIG, blocks, from_sibling):
        mine = lax.dynamic_index_in_dim(b.reshape((4, 2) + b.shape[1:]), ic, axis=1, keepdims=False)
        parts.append(_add_blocks(mine, s, tr=b.shape[1] // 4, name=f"grads_pair_sum_{name}"))
    chip_sums = dict(zip([name for name, _ in _BIG], _chip_exchange(parts, name="grads_chip_exchange")))
    grads = {}

    small_parts = [loss_row, dmod] + [jnp.stack([g_small[l][name] for l in range(DEPTH)], axis=0)
                                      for name, _ in _SMALL_GRADS] + [g_final]
    n_small = sum(p.size for p in small_parts)
    small_rows = -(-n_small // _PACK_LANES)
    gathered = _allgather([_pack(small_parts, small_rows, F32)], name="gather_small_grads")[0]
    total = _sum_blocks(gathered, name="sum_small_grads").reshape(-1)
    loss = total[0]
    off = 128
    grads["ada_b"] = total[off:off + DEPTH * 6 * D_MODEL].reshape(DEPTH, 6 * D_MODEL)
    dmod_all = gathered.reshape(N_DEV, -1)[:, off:off + DEPTH * 6 * D_MODEL].reshape(N_DEV, DEPTH, 6 * D_MODEL)
    off += DEPTH * 6 * D_MODEL
    for name, n in _SMALL_GRADS:
        grads[name] = total[off:off + DEPTH * n].reshape(DEPTH, n)
        off += DEPTH * n
    grads["final_g"] = total[off:off + D_MODEL]
    dmod_cols = lax.dynamic_slice_in_dim(dmod_all, dev * _ADA_SHARD, _ADA_SHARD, axis=2).transpose(1, 0, 2)
    grads["ada_w"] = _ada_wgrad(c_all, dmod_cols, name="ada_wgrad")
    for name in ("ssd_dt_bias", "ssd_a_log", "ssd_d"):
        grads[name] = grads[name][:, :SSD_HEADS]
    grads["pool_w"] = grads["pool_w"].reshape(pool_w.shape)
    grads["ssd_conv_w"] = lax.dynamic_slice_in_dim(
        grads["ssd_conv_w"].reshape(DEPTH, SSD_CONV_K, SSD_CONV_CH), dev * ssd_conv_w.shape[2], ssd_conv_w.shape[2], axis=2)
    grads["ffn_conv_w"] = lax.dynamic_slice_in_dim(
        grads["ffn_conv_w"].reshape(DEPTH, FFN_CONV_K, 2 * FFN_DIM), dev * ffn_conv_w.shape[2], ffn_conv_w.shape[2], axis=2)

    delta, new_m, new_v = {}, {}, {}
    for name, tr in (("ada_w", 512), ("w_in", 512), ("w_out", 256), ("ffn_up", 512), ("ffn_down", 352)):
        shp = w[name].shape
        two_d = lambda a: a.reshape(shp[0] * shp[1], shp[2])
        g_in = two_d(grads[name]) if name == "ada_w" else chip_sums[name]
        g_, d_, m_, v_ = _adamw(two_d(w[name]), g_in, two_d(m[name]), two_d(v[name]), tr=tr, name=f"adamw_{name}")
        grads[name], delta[name], new_m[name], new_v[name] = (t.reshape(shp) for t in (g_, d_, m_, v_))
    n_packed = sum(w[name].size for name in _SMALL_PARAMS)
    rows = -(-n_packed // _PACK_LANES)
    packed = [_pack([t[name] for name in _SMALL_PARAMS], rows, F32) for t in (w, grads, m, v)]
    outs = [o.reshape(-1) for o in _adamw(*packed, tr=rows, name="adamw_small")[1:]]
    off = 0
    for name in _SMALL_PARAMS:
        n = w[name].size
        delta[name], new_m[name], new_v[name] = (o[off:off + n].reshape(w[name].shape) for o in outs)
        off += n

    grad_x = dx[None]
    return (loss, grad_x, *[grads[n].reshape(w[n].shape) for n in _WEIGHT_ORDER],
            *[delta[n] for n in _WEIGHT_ORDER], *[new_m[n] for n in _WEIGHT_ORDER],
            *[new_v[n] for n in _WEIGHT_ORDER])
```

```python
import functools
import math

import jax
import jax.numpy as jnp
from jax import lax
from jax.experimental import pallas as pl
from jax.experimental.pallas import tpu as pltpu

F32 = jnp.float32
BF16 = jnp.bfloat16

N_DEV = 8
D_MODEL = 1024
SEQ = 4096
DEPTH = 2
SSD_INNER = 512
SSD_HEADS = 8
SSD_HPG = 4
SSD_STATE = 128
SSD_CHUNK = 256
SSD_CONV_K = 4
SSD_CONV_CH = 1024
POOL_W = 256
POOL_WINDOWS = (2, 4, 8, 16)
ATT_W = 256
ATT_PATTERNS = ((128, 1), (512, 4), (2048, 16))
ATT_BLOCK = 128
ROT_DIM = 16
ROPE_THETA = 500000.0
IN_W = 2568
FFN_DIM = 2816
FFN_CONV_K = 3
NORM_EPS = 1e-6
HEAD_LANES = 64

ADAM_LR = 0.001
ADAM_B1 = 0.9
ADAM_B2 = 0.999
ADAM_EPS = 1e-08
ADAM_WD = 0.01
ADAM_STEP = 10

PROJ_W = 2688
VMEM_LIMIT = 56 * 1024 * 1024
CONV_HALO = 8
POOL_HALO = 16
ATT_KPAD = ATT_BLOCK * 16
MESH = pl.DeviceIdType.MESH


def _cparams(*sem):
    return pltpu.CompilerParams(dimension_semantics=sem, vmem_limit_bytes=VMEM_LIMIT)


def _silu(x):
    return x * jax.nn.sigmoid(x)


def _pick_lane(v, h):
    lane = lax.broadcasted_iota(jnp.int32, v.shape, 1)
    return jnp.sum(jnp.where(lane == h, v, 0.0), axis=1, keepdims=True)


def _pick_row(v, h):
    row = lax.broadcasted_iota(jnp.int32, v.shape, 0)
    return jnp.sum(jnp.where(row == h, v, 0.0), axis=0, keepdims=True)


def _head_of_lane(width):
    return lax.broadcasted_iota(jnp.int32, (1, width), 1) // HEAD_LANES


@functools.partial(jax.custom_vjp, nondiff_argnums=(1, 2))
def _shift_rows(x_ext, s, halo):
    y = x_ext if s == 0 else pltpu.roll(x_ext, s, 0)
    return y[halo:]


def _shift_rows_fwd(x_ext, s, halo):
    return _shift_rows(x_ext, s, halo), None


def _shift_rows_bwd(s, halo, _, g):
    ge = jnp.concatenate([jnp.zeros((halo, g.shape[1]), g.dtype), g], axis=0)
    return (ge if s == 0 else pltpu.roll(ge, ge.shape[0] - s, 0),)


_shift_rows.defvjp(_shift_rows_fwd, _shift_rows_bwd)


@functools.partial(jax.custom_vjp, nondiff_argnums=(1,))
def _roll_rows(x, s):
    return pltpu.roll(x, s, 0)


def _roll_rows_fwd(x, s):
    return _roll_rows(x, s), None


def _roll_rows_bwd(s, _, g):
    return (pltpu.roll(g, g.shape[0] - s, 0),)


_roll_rows.defvjp(_roll_rows_fwd, _roll_rows_bwd)


def _mm(a, w, *, name, nt=False, tm=512, tn=None, out_dtype=F32):
    t, k = a.shape
    n = w.shape[0] if nt else w.shape[1]
    tn = tn or n

    def body(a_ref, w_ref, o_ref):
        av = a_ref[...].astype(BF16)
        if nt:
            acc = lax.dot_general(av, w_ref[...], (((1,), (1,)), ((), ())), preferred_element_type=F32)
        else:
            acc = jnp.dot(av, w_ref[...], preferred_element_type=F32)
        o_ref[...] = acc.astype(out_dtype)

    w_spec = pl.BlockSpec((tn, k), lambda i, j: (j, 0)) if nt else pl.BlockSpec((k, tn), lambda i, j: (0, j))
    return pl.pallas_call(
        body, grid=(t // tm, n // tn),
        in_specs=[pl.BlockSpec((tm, k), lambda i, j: (i, 0)), w_spec],
        out_specs=pl.BlockSpec((tm, tn), lambda i, j: (i, j)),
        out_shape=jax.ShapeDtypeStruct((t, n), out_dtype), name=name,
        compiler_params=_cparams("parallel", "parallel"))(a, w)


def _wgrad(a, b, *, name, tk=None, tn=None, tt=512, out_dtype=BF16):
    t, k = a.shape
    n = b.shape[1]
    tk = tk or k
    tn = tn or n
    steps = t // tt

    def body(a_ref, b_ref, o_ref, acc_ref):
        s = pl.program_id(2)

        @pl.when(s == 0)
        def _():
            acc_ref[...] = jnp.zeros_like(acc_ref)

        acc_ref[...] += lax.dot_general(a_ref[...].astype(BF16), b_ref[...].astype(BF16),
                                        (((0,), (0,)), ((), ())), preferred_element_type=F32)

        @pl.when(s == steps - 1)
        def _():
            o_ref[...] = acc_ref[...].astype(out_dtype)

    return pl.pallas_call(
        body, grid=(k // tk, n // tn, steps),
        in_specs=[pl.BlockSpec((tt, tk), lambda i, j, s: (s, i)), pl.BlockSpec((tt, tn), lambda i, j, s: (s, j))],
        out_specs=pl.BlockSpec((tk, tn), lambda i, j, s: (i, j)),
        out_shape=jax.ShapeDtypeStruct((k, n), out_dtype),
        scratch_shapes=[pltpu.VMEM((tk, tn), F32)], name=name,
        compiler_params=_cparams("parallel", "parallel", "arbitrary"))(a, b)


def _norm_mod(x, g, sc, sh, *, name, tm=512):
    s, d = x.shape

    def body(x_ref, g_ref, sc_ref, sh_ref, o_ref):
        xv = x_ref[...]
        r = lax.rsqrt(jnp.mean(xv * xv, axis=-1, keepdims=True) + NORM_EPS)
        o_ref[...] = ((xv * r * g_ref[...]) * (1.0 + sc_ref[...]) + sh_ref[...]).astype(BF16)

    row = pl.BlockSpec((1, d), lambda i: (0, 0))
    return pl.pallas_call(
        body, grid=(s // tm,), in_specs=[pl.BlockSpec((tm, d), lambda i: (i, 0)), row, row, row],
        out_specs=pl.BlockSpec((tm, d), lambda i: (i, 0)),
        out_shape=jax.ShapeDtypeStruct((s, d), BF16), name=name, compiler_params=_cparams("parallel"))(x, g, sc, sh)


def _norm_mod_bwd(x, dh, dres, g, sc, *, name, tm=512):
    s, d = x.shape
    steps = s // tm

    def body(x_ref, dh_ref, dres_ref, g_ref, sc_ref, dx_ref, dg_ref, dsc_ref, dsh_ref, da_acc, dsh_acc):
        i = pl.program_id(0)

        @pl.when(i == 0)
        def _():
            da_acc[...] = jnp.zeros_like(da_acc)
            dsh_acc[...] = jnp.zeros_like(dsh_acc)

        xv = x_ref[...]
        dhv = dh_ref[...].astype(F32)
        r = lax.rsqrt(jnp.mean(xv * xv, axis=-1, keepdims=True) + NORM_EPS)
        xhat = xv * r
        gain = g_ref[...] * (1.0 + sc_ref[...])
        dxhat = dhv * gain
        dx_ref[...] = dres_ref[...] + r * (dxhat - xhat * jnp.mean(dxhat * xhat, axis=-1, keepdims=True))
        da_acc[...] += jnp.sum(dhv * xhat, axis=0, keepdims=True)
        dsh_acc[...] += jnp.sum(dhv, axis=0, keepdims=True)

        @pl.when(i == steps - 1)
        def _():
            dg_ref[...] = da_acc[...] * (1.0 + sc_ref[...])
            dsc_ref[...] = da_acc[...] * g_ref[...]
            dsh_ref[...] = dsh_acc[...]

    row = pl.BlockSpec((1, d), lambda i: (0, 0))
    tile = pl.BlockSpec((tm, d), lambda i: (i, 0))
    row_shape = jax.ShapeDtypeStruct((1, d), F32)
    return pl.pallas_call(
        body, grid=(steps,), in_specs=[tile, tile, tile, row, row],
        out_specs=[tile, row, row, row],
        out_shape=[jax.ShapeDtypeStruct((s, d), F32), row_shape, row_shape, row_shape],
        scratch_shapes=[pltpu.VMEM((1, d), F32), pltpu.VMEM((1, d), F32)], name=name,
        compiler_params=_cparams("arbitrary"))(x, dh, dres, g, sc)


def _residual(x, gate, f, *, name, tm=512):
    s, d = x.shape

    def body(x_ref, g_ref, f_ref, o_ref):
        o_ref[...] = x_ref[...] + g_ref[...] * f_ref[...]

    tile = pl.BlockSpec((tm, d), lambda i: (i, 0))
    return pl.pallas_call(
        body, grid=(s // tm,), in_specs=[tile, pl.BlockSpec((1, d), lambda i: (0, 0)), tile], out_specs=tile,
        out_shape=jax.ShapeDtypeStruct((s, d), F32), name=name, compiler_params=_cparams("parallel"))(x, gate, f)


def _residual_bwd(dx, gate, f, *, name, tm=512):
    s, d = dx.shape
    steps = s // tm

    def body(dx_ref, g_ref, f_ref, df_ref, dg_ref):
        i = pl.program_id(0)

        @pl.when(i == 0)
        def _():
            dg_ref[...] = jnp.zeros_like(dg_ref)

        dxv = dx_ref[...]
        df_ref[...] = (g_ref[...] * dxv).astype(BF16)
        dg_ref[...] += jnp.sum(dxv * f_ref[...], axis=0, keepdims=True)

    tile = pl.BlockSpec((tm, d), lambda i: (i, 0))
    row = pl.BlockSpec((1, d), lambda i: (0, 0))
    return pl.pallas_call(
        body, grid=(steps,), in_specs=[tile, row, tile], out_specs=[tile, row],
        out_shape=[jax.ShapeDtypeStruct((s, d), BF16), jax.ShapeDtypeStruct((1, d), F32)], name=name,
        compiler_params=_cparams("arbitrary"))(dx, gate, f)


def _final_loss(x, g, target, *, name, tm=512):
    s, d = x.shape
    steps = s // tm

    def body(x_ref, g_ref, t_ref, loss_ref, dx_ref, dg_ref, sq_acc):
        i = pl.program_id(0)

        @pl.when(i == 0)
        def _():
            sq_acc[...] = jnp.zeros_like(sq_acc)
            dg_ref[...] = jnp.zeros_like(dg_ref)

        xv = x_ref[...]
        r = lax.rsqrt(jnp.mean(xv * xv, axis=-1, keepdims=True) + NORM_EPS)
        xhat = xv * r
        err = xhat * g_ref[...] - t_ref[...]
        sq_acc[...] += jnp.sum(err * err, axis=0, keepdims=True)
        dy = err * (1.0 / d)
        dg_ref[...] += jnp.sum(dy * xhat, axis=0, keepdims=True)
        dxhat = dy * g_ref[...]
        dx_ref[...] = r * (dxhat - xhat * jnp.mean(dxhat * xhat, axis=-1, keepdims=True))

        @pl.when(i == steps - 1)
        def _():
            total = jnp.sum(sq_acc[...], axis=1, keepdims=True) * (0.5 / d)
            loss_ref[...] = jnp.broadcast_to(total, loss_ref.shape)

    tile = pl.BlockSpec((tm, d), lambda i: (i, 0))
    row = pl.BlockSpec((1, d), lambda i: (0, 0))
    return pl.pallas_call(
        body, grid=(steps,), in_specs=[tile, row, tile],
        out_specs=[pl.BlockSpec((1, 128), lambda i: (0, 0)), tile, row],
        out_shape=[jax.ShapeDtypeStruct((1, 128), F32), jax.ShapeDtypeStruct((s, d), F32),
                   jax.ShapeDtypeStruct((1, d), F32)],
        scratch_shapes=[pltpu.VMEM((1, d), F32)], name=name, compiler_params=_cparams("arbitrary"))(x, g, target)


def _ssd_chunk(z, xbc_ext, dt_raw, conv_w, conv_b, dt_bias, a_log, d_skip, norm_g, h_in):
    q = z.shape[0]
    gw = SSD_HPG * HEAD_LANES
    xc = conv_b
    for k in range(SSD_CONV_K):
        xc = xc + _pick_row(conv_w, k) * _shift_rows(xbc_ext, SSD_CONV_K - 1 - k, CONV_HALO)
    xc = _silu(xc)
    dt = jax.nn.softplus(dt_raw + dt_bias)
    da = dt * (-jnp.exp(a_log))
    ri = lax.broadcasted_iota(jnp.int32, (q, q), 0)
    ci = lax.broadcasted_iota(jnp.int32, (q, q), 1)
    causal = ri >= ci
    tril = causal.astype(F32)
    a_cum = jnp.dot(tril, da, preferred_element_type=F32, precision=lax.Precision.HIGHEST)
    a_cum_t = lax.dot_general(da, tril, (((0,), (1,)), ((), ())), preferred_element_type=F32,
                              precision=lax.Precision.HIGHEST)
    a_last = _pick_row(a_cum, q - 1)
    head = _head_of_lane(gw)
    ys, hs = [], []
    for g in range(2):
        xs = xc[:, gw * g:gw * (g + 1)]
        bm = xc[:, SSD_INNER + SSD_STATE * g:SSD_INNER + SSD_STATE * (g + 1)]
        cm = xc[:, SSD_INNER + 2 * SSD_STATE + SSD_STATE * g:SSD_INNER + 2 * SSD_STATE + SSD_STATE * (g + 1)]
        cb = lax.dot_general(cm.astype(BF16), bm.astype(BF16), (((1,), (1,)), ((), ())), preferred_element_type=F32)
        cols = [_pick_lane(a_cum, SSD_HPG * g + j) for j in range(SSD_HPG)]
        lasts = [_pick_lane(a_last, SSD_HPG * g + j) for j in range(SSD_HPG)]
        dt_exp = sum(jnp.where(head == j, _pick_lane(dt, SSD_HPG * g + j), 0.0) for j in range(SSD_HPG))
        d_exp = sum(jnp.where(head == j, _pick_lane(d_skip, SSD_HPG * g + j), 0.0) for j in range(SSD_HPG))
        e_cum = sum(jnp.where(head == j, jnp.exp(cols[j]), 0.0) for j in range(SSD_HPG))
        c_dec = sum(jnp.where(head == j, jnp.exp(lasts[j]), 0.0) for j in range(SSD_HPG))
        xsdt = (xs * dt_exp).astype(BF16)
        y_diag = jnp.zeros((q, gw), F32)
        st_new = jnp.zeros((SSD_STATE, gw), F32)
        for j in range(SSD_HPG):
            row = _pick_row(a_cum_t, SSD_HPG * g + j)
            lmat = jnp.exp(jnp.where(causal, cols[j] - row, -jnp.inf))
            r = jnp.dot((cb * lmat).astype(BF16), xsdt, preferred_element_type=F32)
            y_diag = y_diag + jnp.where(head == j, r, 0.0)
            bd = (bm * jnp.exp(lasts[j] - cols[j])).astype(BF16)
            st = lax.dot_general(bd, xsdt, (((0,), (0,)), ((), ())), preferred_element_type=F32)
            st_new = st_new + jnp.where(head == j, st, 0.0)
        y_off = jnp.dot(cm.astype(BF16), h_in[g].astype(BF16), preferred_element_type=F32) * e_cum
        hs.append(h_in[g] * c_dec + st_new)
        y = y_diag + y_off + d_exp * xs
        yz = y * _silu(z[:, gw * g:gw * (g + 1)])
        yz = yz * lax.rsqrt(jnp.mean(yz * yz, axis=-1, keepdims=True) + NORM_EPS)
        ys.append(yz * norm_g[:, gw * g:gw * (g + 1)])
    return jnp.concatenate(ys, axis=1), tuple(hs)


_SSD_NCHUNK = SEQ // SSD_CHUNK
_HALO_PER_CHUNK = SSD_CHUNK // CONV_HALO


def _ssd_param_specs(const):
    return [pl.BlockSpec((8, SSD_CONV_CH), const), pl.BlockSpec((1, SSD_CONV_CH), const),
            pl.BlockSpec((1, 128), const), pl.BlockSpec((1, 128), const), pl.BlockSpec((1, 128), const),
            pl.BlockSpec((1, SSD_INNER), const)]


def _ssd_fwd(proj, conv_w, conv_b, dt_bias, a_log, d_skip, norm_g, *, name):
    q = SSD_CHUNK

    def body(z_ref, xbc_ref, halo_ref, dt_ref, cw_ref, cb_ref, db_ref, al_ref, d_ref, ng_ref, y_ref, hs_ref, h_acc):
        i = pl.program_id(0)

        @pl.when(i == 0)
        def _():
            h_acc[...] = jnp.zeros_like(h_acc)

        halo = jnp.where(i == 0, 0.0, halo_ref[...])
        xbc_ext = jnp.concatenate([halo, xbc_ref[...]], axis=0)
        h_in = (h_acc[0], h_acc[1])
        hs_ref[0, 0] = h_in[0]
        hs_ref[0, 1] = h_in[1]
        y, h_out = _ssd_chunk(z_ref[...], xbc_ext, dt_ref[...], cw_ref[...], cb_ref[...], db_ref[...], al_ref[...],
                              d_ref[...], ng_ref[...], h_in)
        y_ref[...] = y
        h_acc[0] = h_out[0]
        h_acc[1] = h_out[1]

    const = lambda i: (0, 0)
    return pl.pallas_call(
        body, grid=(_SSD_NCHUNK,),
        in_specs=[pl.BlockSpec((q, SSD_INNER), lambda i: (i, 2)),
                  pl.BlockSpec((q, SSD_CONV_CH), lambda i: (i, 0)),
                  pl.BlockSpec((CONV_HALO, SSD_CONV_CH), lambda i: (jnp.maximum(i * _HALO_PER_CHUNK - 1, 0), 0)),
                  pl.BlockSpec((q, 128), lambda i: (i, 20))] + _ssd_param_specs(const),
        out_specs=[pl.BlockSpec((q, SSD_INNER), lambda i: (i, 0)),
                   pl.BlockSpec((1, 2, SSD_STATE, 256), lambda i: (i, 0, 0, 0))],
        out_shape=[jax.ShapeDtypeStruct((SEQ, SSD_INNER), F32),
                   jax.ShapeDtypeStruct((_SSD_NCHUNK, 2, SSD_STATE, 256), F32)],
        scratch_shapes=[pltpu.VMEM((2, SSD_STATE, 256), F32)], name=name,
        compiler_params=_cparams("arbitrary"))(proj, proj, proj, proj, conv_w, conv_b, dt_bias, a_log, d_skip, norm_g)


def _ssd_bwd(proj, hstates, dmix, conv_w, conv_b, dt_bias, a_log, d_skip, norm_g, *, name):
    q = SSD_CHUNK
    last = _SSD_NCHUNK - 1

    def body(z_ref, xbc_ref, halo_ref, dt_ref, hs_ref, dy_ref, cw_ref, cb_ref, db_ref, al_ref, d_ref, ng_ref,
             dz_ref, dxbc_ref, ddt_ref, dcw_ref, dcb_ref, ddb_ref, dal_ref, dd_ref, dng_ref, dh_acc, dhalo_acc):
        i = pl.program_id(0)

        @pl.when(i == 0)
        def _():
            dh_acc[...] = jnp.zeros_like(dh_acc)
            dhalo_acc[...] = jnp.zeros_like(dhalo_acc)
            for r in (dcw_ref, dcb_ref, ddb_ref, dal_ref, dd_ref, dng_ref):
                r[...] = jnp.zeros_like(r)

        halo = jnp.where(i == last, 0.0, halo_ref[...])
        xbc_ext = jnp.concatenate([halo, xbc_ref[...]], axis=0)
        _, vjp = jax.vjp(_ssd_chunk, z_ref[...], xbc_ext, dt_ref[...], cw_ref[...], cb_ref[...], db_ref[...],
                         al_ref[...], d_ref[...], ng_ref[...], (hs_ref[0, 0], hs_ref[0, 1]))
        gz, gx, gdt, gcw, gcb, gdb, gal, gd, gng, gh = vjp((dy_ref[...], (dh_acc[0], dh_acc[1])))
        dz_ref[...] = gz
        ddt_ref[...] = gdt
        dxbc_ref[...] = gx[CONV_HALO:]
        dxbc_ref[q - CONV_HALO:, :] += dhalo_acc[...]
        dhalo_acc[...] = gx[:CONV_HALO]
        dh_acc[0] = gh[0]
        dh_acc[1] = gh[1]
        dcw_ref[...] += gcw
        dcb_ref[...] += gcb
        ddb_ref[...] += gdb
        dal_ref[...] += gal
        dd_ref[...] += gd
        dng_ref[...] += gng

    const = lambda i: (0, 0)
    rev = lambda i: last - i
    row = lambda n: jax.ShapeDtypeStruct((1, n), F32)
    return pl.pallas_call(
        body, grid=(_SSD_NCHUNK,),
        in_specs=[pl.BlockSpec((q, SSD_INNER), lambda i: (rev(i), 2)),
                  pl.BlockSpec((q, SSD_CONV_CH), lambda i: (rev(i), 0)),
                  pl.BlockSpec((CONV_HALO, SSD_CONV_CH), lambda i: (jnp.maximum(rev(i) * _HALO_PER_CHUNK - 1, 0), 0)),
                  pl.BlockSpec((q, 128), lambda i: (rev(i), 20)),
                  pl.BlockSpec((1, 2, SSD_STATE, 256), lambda i: (rev(i), 0, 0, 0)),
                  pl.BlockSpec((q, SSD_INNER), lambda i: (rev(i), 0))] + _ssd_param_specs(const),
        out_specs=[pl.BlockSpec((q, SSD_INNER), lambda i: (rev(i), 0)),
                   pl.BlockSpec((q, SSD_CONV_CH), lambda i: (rev(i), 0)),
                   pl.BlockSpec((q, 128), lambda i: (rev(i), 0))] + _ssd_param_specs(const),
        out_shape=[jax.ShapeDtypeStruct((SEQ, SSD_INNER), F32), jax.ShapeDtypeStruct((SEQ, SSD_CONV_CH), F32),
                   jax.ShapeDtypeStruct((SEQ, 128), F32), jax.ShapeDtypeStruct((8, SSD_CONV_CH), F32),
                   row(SSD_CONV_CH), row(128), row(128), row(128), row(SSD_INNER)],
        scratch_shapes=[pltpu.VMEM((2, SSD_STATE, 256), F32), pltpu.VMEM((CONV_HALO, SSD_CONV_CH), F32)], name=name,
        compiler_params=_cparams("arbitrary"))(proj, proj, proj, proj, hstates, dmix, conv_w, conv_b, dt_bias, a_log,
                                                d_skip, norm_g)


def _rope_tables(pos_col, inv_freq_lane, *, name):
    s = pos_col.shape[0]

    def body(p_ref, f_ref, c_ref, s1_ref, s2_ref):
        ang = p_ref[...] * f_ref[...]
        within = lax.broadcasted_iota(jnp.int32, ang.shape, 1) % HEAD_LANES
        half = ROT_DIM // 2
        c_ref[...] = jnp.where(within < ROT_DIM, jnp.cos(ang), 1.0)
        sn = jnp.sin(ang)
        s1_ref[...] = jnp.where(within < half, -sn, 0.0)
        s2_ref[...] = jnp.where((within >= half) & (within < ROT_DIM), sn, 0.0)

    shp = jax.ShapeDtypeStruct((s, 128), F32)
    return pl.pallas_call(body, out_shape=[shp, shp, shp], name=name,
                          compiler_params=pltpu.CompilerParams(vmem_limit_bytes=VMEM_LIMIT))(pos_col, inv_freq_lane)


def _rope(t, c, s1, s2):
    half = ROT_DIM // 2
    return t * c + pltpu.roll(t, 128 - half, 1) * s1 + pltpu.roll(t, half, 1) * s2


def _rope_t(g, c, s1, s2):
    half = ROT_DIM // 2
    return g * c + pltpu.roll(g * s1, half, 1) + pltpu.roll(g * s2, 128 - half, 1)


def _att_valid(b):
    qi = lax.broadcasted_iota(jnp.int32, (ATT_BLOCK, 2 * ATT_BLOCK), 0)
    kj = lax.broadcasted_iota(jnp.int32, (ATT_BLOCK, 2 * ATT_BLOCK), 1)
    rel = qi + ATT_BLOCK - kj
    return (rel >= 0) & (rel <= ATT_BLOCK) & (b * ATT_BLOCK + kj - ATT_BLOCK >= 0)


def _att_slices(i, d):
    if d == 1:
        qstart = pl.multiple_of(i * ATT_BLOCK, ATT_BLOCK)
        return i, pl.ds(qstart, ATT_BLOCK), pl.ds(pl.multiple_of(qstart - ATT_BLOCK + ATT_KPAD, ATT_BLOCK), 2 * ATT_BLOCK)
    r = i % d
    b = i // d
    qstart = r + d * ATT_BLOCK * b
    return b, pl.ds(qstart, ATT_BLOCK, stride=d), pl.ds(qstart - ATT_BLOCK * d + ATT_KPAD, 2 * ATT_BLOCK, stride=d)


_ATT_NBLK = SEQ // ATT_BLOCK
_ATT_SCALE = HEAD_LANES ** -0.5
_ATT_UNROLL = 4


def _att_fwd(proj, cos, sin1, sin2, *, name):
    s = SEQ

    def body(q_ref, k_ref, v_ref, c_ref, s1_ref, s2_ref, o_ref, lse_ref, qs, ks, vs, acc, m_s, l_s):
        c, s1, s2 = c_ref[...], s1_ref[...], s2_ref[...]
        qs[...] = _rope(q_ref[...], c, s1, s2) * _ATT_SCALE
        zeros = jnp.zeros((ATT_KPAD, 128), F32)
        ks[pl.ds(0, ATT_KPAD), :] = zeros
        vs[pl.ds(0, ATT_KPAD), :] = zeros
        ks[pl.ds(ATT_KPAD, s), :] = _rope(k_ref[...], c, s1, s2)
        vs[pl.ds(ATT_KPAD, s), :] = v_ref[...]
        head0 = _head_of_lane(128) == 0

        for bi, (_, d) in enumerate(ATT_PATTERNS):
            def blk(i, carry, d=d, first=(bi == 0)):
                b, sq, sk = _att_slices(i, d)
                qb = qs[sq, :]
                kw = ks[sk, :].astype(BF16)
                vw = vs[sk, :].astype(BF16)
                valid = _att_valid(b)
                ms, ls, os_ = [], [], []
                for hh in range(2):
                    qh = jnp.where(head0 if hh == 0 else ~head0, qb, 0.0).astype(BF16)
                    sc = lax.dot_general(qh, kw, (((1,), (1,)), ((), ())), preferred_element_type=F32)
                    sc = jnp.where(valid, sc, -jnp.inf)
                    mb = jnp.max(sc, axis=1, keepdims=True)
                    p = jnp.exp(sc - mb)
                    ms.append(mb)
                    ls.append(jnp.sum(p, axis=1, keepdims=True))
                    os_.append(jnp.dot(p.astype(BF16), vw, preferred_element_type=F32))
                m_b = jnp.where(head0, ms[0], ms[1])
                l_b = jnp.where(head0, ls[0], ls[1])
                o_b = jnp.where(head0, os_[0], os_[1])
                if first:
                    m_s[sq, :] = m_b
                    l_s[sq, :] = l_b
                    acc[sq, :] = o_b
                else:
                    m_old = m_s[sq, :]
                    m_new = jnp.maximum(m_old, m_b)
                    a_old = jnp.exp(m_old - m_new)
                    a_b = jnp.exp(m_b - m_new)
                    m_s[sq, :] = m_new
                    l_s[sq, :] = l_s[sq, :] * a_old + l_b * a_b
                    acc[sq, :] = acc[sq, :] * a_old + o_b * a_b
                return carry

            lax.fori_loop(0, _ATT_NBLK, blk, 0, unroll=_ATT_UNROLL)

        o_ref[...] = acc[...] / l_s[...]
        lse_ref[...] = m_s[...] + jnp.log(l_s[...])

    col = lambda base: pl.BlockSpec((s, 128), lambda p: (0, base + p))
    tab = pl.BlockSpec((s, 128), lambda p: (0, 0))
    big = pltpu.VMEM((ATT_KPAD + s, 128), F32)
    tok = pltpu.VMEM((s, 128), F32)
    return pl.pallas_call(
        body, grid=(2,), in_specs=[col(14), col(16), col(18), tab, tab, tab],
        out_specs=[pl.BlockSpec((s, 128), lambda p: (0, p)), pl.BlockSpec((s, 128), lambda p: (0, p))],
        out_shape=[jax.ShapeDtypeStruct((s, ATT_W), F32), jax.ShapeDtypeStruct((s, ATT_W), F32)],
        scratch_shapes=[tok, big, big, tok, tok, tok], name=name,
        compiler_params=_cparams("arbitrary"))(proj, proj, proj, cos, sin1, sin2)


def _att_bwd(proj, cos, sin1, sin2, out, lse, dmix, *, name):
    s = SEQ

    def body(proj_ref, c_hbm, s1_hbm, s2_hbm, out_hbm, lse_hbm, dmix_hbm, dq_hbm, dk_hbm, dv_hbm,
             c_ref, s1_ref, s2_ref, o_ref, lse_ref, do_ref, qs, ks, vs, dqs, dks, dvs):
        pltpu.sync_copy(c_hbm, c_ref)
        pltpu.sync_copy(s1_hbm, s1_ref)
        pltpu.sync_copy(s2_hbm, s2_ref)
        head0 = _head_of_lane(128) == 0
        zeros = jnp.zeros((ATT_KPAD, 128), F32)
        for pair in range(2):
            lanes = pl.ds(128 * pair, 128)
            pltpu.sync_copy(proj_ref.at[:, pl.ds(128 * (14 + pair), 128)], qs)
            pltpu.sync_copy(proj_ref.at[:, pl.ds(128 * (16 + pair), 128)], ks.at[pl.ds(ATT_KPAD, s), :])
            pltpu.sync_copy(proj_ref.at[:, pl.ds(128 * (18 + pair), 128)], vs.at[pl.ds(ATT_KPAD, s), :])
            pltpu.sync_copy(out_hbm.at[:, lanes], o_ref)
            pltpu.sync_copy(lse_hbm.at[:, lanes], lse_ref)
            pltpu.sync_copy(dmix_hbm.at[:, pl.ds(128 * (6 + pair), 128)], do_ref)
            c, s1, s2 = c_ref[...], s1_ref[...], s2_ref[...]
            qs[...] = _rope(qs[...], c, s1, s2) * _ATT_SCALE
            ks[pl.ds(0, ATT_KPAD), :] = zeros
            vs[pl.ds(0, ATT_KPAD), :] = zeros
            ks[pl.ds(ATT_KPAD, s), :] = _rope(ks[pl.ds(ATT_KPAD, s), :], c, s1, s2)
            dqs[...] = jnp.zeros_like(dqs)
            dks[...] = jnp.zeros_like(dks)
            dvs[...] = jnp.zeros_like(dvs)

            for _, d in ATT_PATTERNS:
                def blk(i, carry, d=d):
                    b, sq, sk = _att_slices(i, d)
                    qb = qs[sq, :]
                    kw = ks[sk, :].astype(BF16)
                    vw = vs[sk, :].astype(BF16)
                    dob = do_ref[sq, :]
                    lse_b = lse_ref[sq, :]
                    dd = dob * o_ref[sq, :]
                    valid = _att_valid(b)
                    dq_b = jnp.zeros((ATT_BLOCK, 128), F32)
                    dk_w = jnp.zeros((2 * ATT_BLOCK, 128), F32)
                    dv_w = jnp.zeros((2 * ATT_BLOCK, 128), F32)
                    for hh in range(2):
                        hm = head0 if hh == 0 else ~head0
                        qh = jnp.where(hm, qb, 0.0).astype(BF16)
                        doh = jnp.where(hm, dob, 0.0).astype(BF16)
                        lse_h = _pick_lane(lse_b, hh * HEAD_LANES)
                        d_h = jnp.sum(jnp.where(hm, dd, 0.0), axis=1, keepdims=True)
                        sc = lax.dot_general(qh, kw, (((1,), (1,)), ((), ())), preferred_element_type=F32)
                        p = jnp.where(valid, jnp.exp(sc - lse_h), 0.0)
                        dp = lax.dot_general(doh, vw, (((1,), (1,)), ((), ())), preferred_element_type=F32)
                        ds = (p * (dp - d_h)).astype(BF16)
                        dq_b = dq_b + jnp.where(hm, jnp.dot(ds, kw, preferred_element_type=F32), 0.0)
                        dk_w = dk_w + lax.dot_general(ds, qh, (((0,), (0,)), ((), ())), preferred_element_type=F32)
                        dv_w = dv_w + lax.dot_general(p.astype(BF16), doh, (((0,), (0,)), ((), ())),
                                                      preferred_element_type=F32)
                    dqs[sq, :] += dq_b
                    dks[sk, :] += dk_w
                    dvs[sk, :] += dv_w
                    return carry

                lax.fori_loop(0, _ATT_NBLK, blk, 0, unroll=_ATT_UNROLL)

            dqs[...] = _rope_t(dqs[...] * _ATT_SCALE, c, s1, s2)
            dks[pl.ds(ATT_KPAD, s), :] = _rope_t(dks[pl.ds(ATT_KPAD, s), :], c, s1, s2)
            pltpu.sync_copy(dqs, dq_hbm.at[:, lanes])
            pltpu.sync_copy(dks.at[pl.ds(ATT_KPAD, s), :], dk_hbm.at[:, lanes])
            pltpu.sync_copy(dvs.at[pl.ds(ATT_KPAD, s), :], dv_hbm.at[:, lanes])

    hbm = pl.BlockSpec(memory_space=pl.ANY)
    big = pltpu.VMEM((ATT_KPAD + s, 128), F32)
    tok = pltpu.VMEM((s, 128), F32)
    shp = jax.ShapeDtypeStruct((s, ATT_W), F32)
    return pl.pallas_call(
        body, in_specs=[hbm] * 7, out_specs=[hbm] * 3, out_shape=[shp, shp, shp],
        scratch_shapes=[tok] * 6 + [tok, big, big, tok, big, big], name=name,
        compiler_params=pltpu.CompilerParams(vmem_limit_bytes=VMEM_LIMIT))(proj, cos, sin1, sin2, out, lse, dmix)


_POOL_TM = 512
_POOL_NT = SEQ // _POOL_TM
_POOL_HALO_PER_TILE = _POOL_TM // POOL_HALO


def _pool_tile(u_ext, w_bd, scale, t0):
    s2 = u_ext + _roll_rows(u_ext, 1)
    s4 = s2 + _roll_rows(s2, 2)
    s8 = s4 + _roll_rows(s4, 4)
    s16 = s8 + _roll_rows(s8, 8)
    grp = _head_of_lane(POOL_W)
    sel = jnp.where(grp == 0, s2, jnp.where(grp == 1, s4, jnp.where(grp == 2, s8, s16)))[POOL_HALO:]
    t = sel.shape[0]
    pos = t0 + lax.broadcasted_iota(jnp.int32, (t, POOL_W), 0) + 1
    win = jnp.where(grp == 0, 2, jnp.where(grp == 1, 4, jnp.where(grp == 2, 8, 16)))
    cnt = jnp.minimum(pos, win).astype(F32)
    diff = sel / cnt - u_ext[POOL_HALO:]
    return jnp.dot(diff.astype(BF16), w_bd.astype(BF16), preferred_element_type=F32) * scale


def _pool_fwd(proj, w_bd, scale, *, name):
    tm = _POOL_TM

    def body(u_ref, halo_ref, w_ref, sc_ref, y_ref):
        i = pl.program_id(0)
        halo = jnp.where(i == 0, 0.0, halo_ref[...])
        u_ext = jnp.concatenate([halo, u_ref[...]], axis=0)
        y_ref[...] = _pool_tile(u_ext, w_ref[...], sc_ref[...], i * tm)

    return pl.pallas_call(
        body, grid=(_POOL_NT,),
        in_specs=[pl.BlockSpec((tm, POOL_W), lambda i: (i, 6)),
                  pl.BlockSpec((POOL_HALO, POOL_W), lambda i: (jnp.maximum(i * _POOL_HALO_PER_TILE - 1, 0), 6)),
                  pl.BlockSpec((POOL_W, POOL_W), lambda i: (0, 0)), pl.BlockSpec((1, POOL_W), lambda i: (0, 0))],
        out_specs=pl.BlockSpec((tm, POOL_W), lambda i: (i, 0)),
        out_shape=jax.ShapeDtypeStruct((SEQ, POOL_W), F32), name=name,
        compiler_params=_cparams("parallel"))(proj, proj, w_bd, scale)


def _pool_bwd(proj, dmix, w_bd, scale, *, name):
    tm = _POOL_TM
    last = _POOL_NT - 1

    def body(u_ref, halo_ref, dy_ref, w_ref, sc_ref, du_ref, dw_ref, dsc_ref, dhalo_acc):
        i = pl.program_id(0)

        @pl.when(i == 0)
        def _():
            dhalo_acc[...] = jnp.zeros_like(dhalo_acc)
            dw_ref[...] = jnp.zeros_like(dw_ref)
            dsc_ref[...] = jnp.zeros_like(dsc_ref)

        tile = last - i
        halo = jnp.where(tile == 0, 0.0, halo_ref[...])
        u_ext = jnp.concatenate([halo, u_ref[...]], axis=0)
        _, vjp = jax.vjp(functools.partial(_pool_tile, t0=tile * tm), u_ext, w_ref[...], sc_ref[...])
        gu, gw, gs = vjp(dy_ref[...])
        du_ref[...] = gu[POOL_HALO:]
        du_ref[tm - POOL_HALO:, :] += dhalo_acc[...]
        dhalo_acc[...] = gu[:POOL_HALO]
        dw_ref[...] += gw
        dsc_ref[...] += gs

    rev = lambda i: last - i
    return pl.pallas_call(
        body, grid=(_POOL_NT,),
        in_specs=[pl.BlockSpec((tm, POOL_W), lambda i: (rev(i), 6)),
                  pl.BlockSpec((POOL_HALO, POOL_W), lambda i: (jnp.maximum(rev(i) * _POOL_HALO_PER_TILE - 1, 0), 6)),
                  pl.BlockSpec((tm, POOL_W), lambda i: (rev(i), 2)),
                  pl.BlockSpec((POOL_W, POOL_W), lambda i: (0, 0)), pl.BlockSpec((1, POOL_W), lambda i: (0, 0))],
        out_specs=[pl.BlockSpec((tm, POOL_W), lambda i: (rev(i), 0)),
                   pl.BlockSpec((POOL_W, POOL_W), lambda i: (0, 0)), pl.BlockSpec((1, POOL_W), lambda i: (0, 0))],
        out_shape=[jax.ShapeDtypeStruct((SEQ, POOL_W), F32), jax.ShapeDtypeStruct((POOL_W, POOL_W), F32),
                   jax.ShapeDtypeStruct((1, POOL_W), F32)],
        scratch_shapes=[pltpu.VMEM((POOL_HALO, POOL_W), F32)], name=name,
        compiler_params=_cparams("arbitrary"))(proj, proj, dmix, w_bd, scale)


_FFN_TM = 256
_FFN_NT = SEQ // _FFN_TM
_FFN_HALO_PER_TILE = _FFN_TM // CONV_HALO


def _ffn_act_tile(hid_ext, conv_w, conv_b):
    hc = conv_b
    for k in range(FFN_CONV_K):
        hc = hc + _pick_row(conv_w, k) * _shift_rows(hid_ext, FFN_CONV_K - 1 - k, CONV_HALO)
    return _silu(hc[:, :FFN_DIM]) * hc[:, FFN_DIM:]


def _ffn_act(hid, conv_w, conv_b, *, name):
    tm = _FFN_TM
    w = 2 * FFN_DIM

    def body(h_ref, halo_ref, cw_ref, cb_ref, a_ref):
        i = pl.program_id(0)
        halo = jnp.where(i == 0, 0.0, halo_ref[...])
        hid_ext = jnp.concatenate([halo, h_ref[...]], axis=0)
        a_ref[...] = _ffn_act_tile(hid_ext, cw_ref[...], cb_ref[...]).astype(BF16)

    return pl.pallas_call(
        body, grid=(_FFN_NT,),
        in_specs=[pl.BlockSpec((tm, w), lambda i: (i, 0)),
                  pl.BlockSpec((CONV_HALO, w), lambda i: (jnp.maximum(i * _FFN_HALO_PER_TILE - 1, 0), 0)),
                  pl.BlockSpec((8, w), lambda i: (0, 0)), pl.BlockSpec((1, w), lambda i: (0, 0))],
        out_specs=pl.BlockSpec((tm, FFN_DIM), lambda i: (i, 0)),
        out_shape=jax.ShapeDtypeStruct((SEQ, FFN_DIM), BF16), name=name,
        compiler_params=_cparams("parallel"))(hid, hid, conv_w, conv_b)


def _ffn_act_bwd(hid, dact, conv_w, conv_b, *, name):
    tm = _FFN_TM
    w = 2 * FFN_DIM
    last = _FFN_NT - 1

    def body(h_ref, halo_ref, da_ref, cw_ref, cb_ref, dh_ref, dcw_ref, dcb_ref, dhalo_acc):
        i = pl.program_id(0)

        @pl.when(i == 0)
        def _():
            dhalo_acc[...] = jnp.zeros_like(dhalo_acc)
            dcw_ref[...] = jnp.zeros_like(dcw_ref)
            dcb_ref[...] = jnp.zeros_like(dcb_ref)

        halo = jnp.where(i == last, 0.0, halo_ref[...])
        hid_ext = jnp.concatenate([halo, h_ref[...]], axis=0)
        _, vjp = jax.vjp(_ffn_act_tile, hid_ext, cw_ref[...], cb_ref[...])
        gh, gw, gb = vjp(da_ref[...])
        carry = dhalo_acc[...]
        dhalo_acc[...] = gh[:CONV_HALO]
        dh_ref[...] = jnp.concatenate([gh[CONV_HALO:tm], gh[tm:] + carry], axis=0).astype(BF16)
        dcw_ref[...] += gw
        dcb_ref[...] += gb

    rev = lambda i: last - i
    return pl.pallas_call(
        body, grid=(_FFN_NT,),
        in_specs=[pl.BlockSpec((tm, w), lambda i: (rev(i), 0)),
                  pl.BlockSpec((CONV_HALO, w), lambda i: (jnp.maximum(rev(i) * _FFN_HALO_PER_TILE - 1, 0), 0)),
                  pl.BlockSpec((tm, FFN_DIM), lambda i: (rev(i), 0)),
                  pl.BlockSpec((8, w), lambda i: (0, 0)), pl.BlockSpec((1, w), lambda i: (0, 0))],
        out_specs=[pl.BlockSpec((tm, w), lambda i: (rev(i), 0)),
                   pl.BlockSpec((8, w), lambda i: (0, 0)), pl.BlockSpec((1, w), lambda i: (0, 0))],
        out_shape=[jax.ShapeDtypeStruct((SEQ, w), BF16), jax.ShapeDtypeStruct((8, w), F32),
                   jax.ShapeDtypeStruct((1, w), F32)],
        scratch_shapes=[pltpu.VMEM((CONV_HALO, w), F32)], name=name,
        compiler_params=_cparams("arbitrary"))(hid, hid, dact, conv_w, conv_b)


_HBM = pl.BlockSpec(memory_space=pl.ANY)


def _axes():
    return lax.axis_index("x"), lax.axis_index("y"), lax.axis_index("c")


def _allgather(xs, *, name):
    n = len(xs)

    def body(*refs):
        x_refs, out_refs = refs[:n], refs[n:2 * n]
        send_sems, recv_sems, local_sems = refs[2 * n:]
        x, y, c = _axes()
        me, sibling = (x, y, c), (x, y, 1 - c)
        chips = [(1 - x, y), (x, 1 - y), (1 - x, 1 - y)]

        def slot(a, px, py, pc):
            return out_refs[a].at[4 * px + 2 * py + pc]

        def copy(a, k, block, to, src=None):
            return pltpu.make_async_remote_copy(
                src_ref=slot(a, *block) if src is None else src, dst_ref=slot(a, *block),
                send_sem=send_sems.at[a, k], recv_sem=recv_sems.at[a, k], device_id=to, device_id_type=MESH)

        mines, firsts = [], []
        for a in range(n):
            mines.append(pltpu.make_async_copy(x_refs[a], slot(a, *me), local_sems.at[a]))
            mines[-1].start()
            first = [copy(a, 0, me, sibling, src=x_refs[a])]
            first += [copy(a, 1 + j, me, (*chip, c), src=x_refs[a]) for j, chip in enumerate(chips)]
            for cp in first:
                cp.start()
            firsts += first
        passed = []
        for j, chip in enumerate(chips):
            for a in range(n):
                copy(a, 1 + j, (*chip, c), me).wait_recv()
                passed.append(copy(a, 4 + j, (*chip, c), sibling))
                passed[-1].start()
        for a in range(n):
            copy(a, 0, sibling, me).wait_recv()
        for j, chip in enumerate(chips):
            for a in range(n):
                copy(a, 4 + j, (*chip, 1 - c), me).wait_recv()
        for cp in firsts + passed:
            cp.wait_send()
        for cp in mines:
            cp.wait()

    return pl.pallas_call(
        body, out_shape=[jax.ShapeDtypeStruct((N_DEV,) + xb.shape, xb.dtype) for xb in xs],
        in_specs=[_HBM] * n, out_specs=[_HBM] * n,
        scratch_shapes=[pltpu.SemaphoreType.DMA((n, 7)), pltpu.SemaphoreType.DMA((n, 7)),
                        pltpu.SemaphoreType.DMA((n,))],
        name=name)(*xs)


def _pair_exchange(blocks, *, name):
    n = len(blocks)

    def body(*refs):
        in_refs, out_refs = refs[:n], refs[n:2 * n]
        send_sems, recv_sems = refs[2 * n:]
        x, y, c = _axes()
        copies = [pltpu.make_async_remote_copy(
            src_ref=in_refs[a].at[2 * s + (1 - c)], dst_ref=out_refs[a].at[s], send_sem=send_sems.at[a, s],
            recv_sem=recv_sems.at[a, s], device_id=(x, y, 1 - c), device_id_type=MESH)
            for a in range(n) for s in range(4)]
        for cp in copies:
            cp.start()
        for cp in copies:
            cp.wait_recv()
        for cp in copies:
            cp.wait_send()

    return pl.pallas_call(
        body, out_shape=[jax.ShapeDtypeStruct((4,) + b.shape[1:], b.dtype) for b in blocks],
        in_specs=[_HBM] * n, out_specs=[_HBM] * n,
        scratch_shapes=[pltpu.SemaphoreType.DMA((n, 4)), pltpu.SemaphoreType.DMA((n, 4))], name=name)(*blocks)


def _chip_exchange(parts, *, name):
    n = len(parts)

    def body(*refs):
        in_refs, out_refs = refs[:n], refs[n:2 * n]
        send_sems, recv_sems, local_sems = refs[2 * n:]
        x, y, c = _axes()
        my_chip = 2 * x + y
        chips = [(1 - x, y), (x, 1 - y), (1 - x, 1 - y)]
        locals_ = [pltpu.make_async_copy(in_refs[a].at[my_chip], out_refs[a].at[my_chip], local_sems.at[a])
                   for a in range(n)]
        for cp in locals_:
            cp.start()
        copies = [pltpu.make_async_remote_copy(
            src_ref=in_refs[a].at[2 * px + py], dst_ref=out_refs[a].at[my_chip], send_sem=send_sems.at[a, k],
            recv_sem=recv_sems.at[a, k], device_id=(px, py, c), device_id_type=MESH)
            for a in range(n) for k, (px, py) in enumerate(chips)]
        for cp in copies:
            cp.start()
        for cp in copies:
            cp.wait_recv()
        for cp in copies:
            cp.wait_send()
        for cp in locals_:
            cp.wait()

    return pl.pallas_call(
        body, out_shape=[jax.ShapeDtypeStruct(p.shape, p.dtype) for p in parts],
        in_specs=[_HBM] * n, out_specs=[_HBM] * n,
        scratch_shapes=[pltpu.SemaphoreType.DMA((n, 3)), pltpu.SemaphoreType.DMA((n, 3)),
                        pltpu.SemaphoreType.DMA((n,))],
        name=name)(*parts)


def _pair_sum(core, blocks, from_sibling, *, name):
    _, r, cdim = blocks.shape

    def body(core_ref, a_ref, b_ref, o_ref):
        o_ref[...] = (a_ref[...].astype(F32) + b_ref[...].astype(F32)).astype(o_ref.dtype)

    return pl.pallas_call(
        body,
        grid_spec=pltpu.PrefetchScalarGridSpec(
            num_scalar_prefetch=1, grid=(4,),
            in_specs=[pl.BlockSpec((1, r, cdim), lambda s, core_ref: (2 * s + core_ref[0], 0, 0)),
                      pl.BlockSpec((1, r, cdim), lambda s, core_ref: (s, 0, 0))],
            out_specs=pl.BlockSpec((1, r, cdim), lambda s, core_ref: (s, 0, 0))),
        out_shape=jax.ShapeDtypeStruct(from_sibling.shape, from_sibling.dtype), name=name,
        compiler_params=_cparams("parallel"))(core, blocks, from_sibling)


def _sum_blocks(a, *, name, tr=None):
    n, r, cdim = a.shape
    tr = tr or r

    def body(a_ref, o_ref):
        acc = a_ref[0].astype(F32)
        for k in range(1, n):
            acc = acc + a_ref[k].astype(F32)
        o_ref[...] = acc

    return pl.pallas_call(body, grid=(r // tr,), in_specs=[pl.BlockSpec((n, tr, cdim), lambda i: (0, i, 0))],
                          out_specs=pl.BlockSpec((tr, cdim), lambda i: (i, 0)),
                          out_shape=jax.ShapeDtypeStruct((r, cdim), F32), name=name,
                          compiler_params=_cparams("parallel"))(a)


_ADA_SHARD = 6 * D_MODEL // N_DEV


def _ada_mod(c_all, ada_w, *, name):
    def body(c_ref, w_ref, o_ref):
        o_ref[0] = jnp.dot(_silu(c_ref[...]).astype(BF16), w_ref[0].astype(BF16), preferred_element_type=F32)

    return pl.pallas_call(
        body, grid=(DEPTH,),
        in_specs=[pl.BlockSpec((N_DEV, D_MODEL), lambda l: (0, 0)),
                  pl.BlockSpec((1, D_MODEL, _ADA_SHARD), lambda l: (l, 0, 0))],
        out_specs=pl.BlockSpec((1, N_DEV, _ADA_SHARD), lambda l: (l, 0, 0)),
        out_shape=jax.ShapeDtypeStruct((DEPTH, N_DEV, _ADA_SHARD), F32), name=name,
        compiler_params=_cparams("parallel"))(c_all, ada_w)


def _ada_wgrad(c_all, dmod_cols, *, name):
    def body(c_ref, d_ref, o_ref):
        o_ref[0] = lax.dot_general(_silu(c_ref[...]), d_ref[0], (((0,), (0,)), ((), ())),
                                   preferred_element_type=F32, precision=lax.Precision.HIGHEST)

    return pl.pallas_call(
        body, grid=(DEPTH,),
        in_specs=[pl.BlockSpec((N_DEV, D_MODEL), lambda l: (0, 0)),
                  pl.BlockSpec((1, N_DEV, _ADA_SHARD), lambda l: (l, 0, 0))],
        out_specs=pl.BlockSpec((1, D_MODEL, _ADA_SHARD), lambda l: (l, 0, 0)),
        out_shape=jax.ShapeDtypeStruct((DEPTH, D_MODEL, _ADA_SHARD), F32), name=name,
        compiler_params=_cparams("parallel"))(c_all, dmod_cols)


def _add_rows(a, b, *, name):
    def body(a_ref, b_ref, o_ref):
        o_ref[...] = a_ref[...] + b_ref[...]

    return pl.pallas_call(body, out_shape=jax.ShapeDtypeStruct(a.shape, a.dtype), name=name)(a, b)


def _adamw(w, g, m, v, *, name, tr):
    r, cdim = w.shape
    c1 = 1.0 - ADAM_B1 ** ADAM_STEP
    c2 = 1.0 - ADAM_B2 ** ADAM_STEP

    def body(w_ref, g_ref, m_ref, v_ref, d_ref, mo_ref, vo_ref):
        gv = g_ref[...]
        mn = ADAM_B1 * m_ref[...] + (1.0 - ADAM_B1) * gv
        vn = ADAM_B2 * v_ref[...] + (1.0 - ADAM_B2) * (gv * gv)
        mo_ref[...] = mn
        vo_ref[...] = vn
        d_ref[...] = -ADAM_LR * ((mn / c1) / (jnp.sqrt(vn / c2) + ADAM_EPS) + ADAM_WD * w_ref[...])

    spec = pl.BlockSpec((tr, cdim), lambda i: (i, 0))
    shp = jax.ShapeDtypeStruct((r, cdim), F32)
    return pl.pallas_call(body, grid=(r // tr,), in_specs=[spec] * 4, out_specs=[spec] * 3, out_shape=[shp] * 3,
                          name=name, compiler_params=_cparams("parallel"))(w, g, m, v)


def _pad_rows(a, rows):
    return jnp.concatenate([a, jnp.zeros((rows - a.shape[0],) + a.shape[1:], a.dtype)], axis=0)


def _pad_lanes(a, lanes):
    return jnp.concatenate([a, jnp.zeros(a.shape[:-1] + (lanes - a.shape[-1],), a.dtype)], axis=-1)


def _permute_w_in(wt):
    return jnp.concatenate([wt[512:1536], wt[:512], wt[1544:], wt[1536:1544],
                            jnp.zeros((PROJ_W - IN_W, wt.shape[1]), wt.dtype)], axis=0)


def _unpermute_w_in(wp):
    return jnp.concatenate([wp[1024:1536], wp[:1024], wp[2560:2568], wp[1536:2560]], axis=0)


def _block_diag(w):
    rows = []
    for g in range(4):
        rows.append(jnp.concatenate([w[g] if k == g else jnp.zeros_like(w[g]) for k in range(4)], axis=1))
    return jnp.concatenate(rows, axis=0)


def _diag_blocks(wbd):
    return jnp.stack([wbd[64 * g:64 * (g + 1), 64 * g:64 * (g + 1)] for g in range(4)], axis=0)


def _layer_params(l, big, small):
    return dict(
        w_in=big[l]["w_in"], w_out=big[l]["w_out"], up=big[l]["ffn_up"], down=big[l]["ffn_down"],
        norm1_g=small["norm1_g"][l][None], norm2_g=small["norm2_g"][l][None],
        conv_w=_pad_rows(small["ssd_conv_w"][l], 8), conv_b=small["ssd_conv_b"][l][None],
        dt_bias=_pad_lanes(small["ssd_dt_bias"][l][None], 128), a_log=_pad_lanes(small["ssd_a_log"][l][None], 128),
        d_skip=_pad_lanes(small["ssd_d"][l][None], 128), ssd_norm_g=small["ssd_norm_g"][l][None],
        pool_bd=_block_diag(small["pool_w"][l]), pool_scale=small["pool_scale"][l][None],
        fcw=_pad_rows(small["ffn_conv_w"][l], 8), fcb=small["ffn_conv_b"][l][None])


def _mod_rows(mod_l):
    return [mod_l[None, D_MODEL * i:D_MODEL * (i + 1)] for i in range(6)]


def _layer_fwd(x, mod_l, p, tabs, l):
    sh1, sc1, g1, sh2, sc2, g2 = _mod_rows(mod_l)
    h1 = _norm_mod(x, p["norm1_g"], sc1, sh1, name=f"l{l}_norm1")
    proj = _mm(h1, p["w_in"], nt=True, name=f"l{l}_proj")
    y_ssd, hst = _ssd_fwd(proj, p["conv_w"], p["conv_b"], p["dt_bias"], p["a_log"], p["d_skip"], p["ssd_norm_g"],
                          name=f"l{l}_ssd")
    y_pool = _pool_fwd(proj, p["pool_bd"], p["pool_scale"], name=f"l{l}_pool")
    y_att, lse = _att_fwd(proj, *tabs, name=f"l{l}_att")
    mix = jnp.concatenate([y_ssd, y_pool, y_att], axis=1).astype(BF16)
    mo = _mm(mix, p["w_out"], name=f"l{l}_out")
    x1 = _residual(x, g1, mo, name=f"l{l}_res1")
    h2 = _norm_mod(x1, p["norm2_g"], sc2, sh2, name=f"l{l}_norm2")
    hid = _mm(h2, p["up"], nt=True, tn=1408, name=f"l{l}_up")
    act = _ffn_act(hid, p["fcw"], p["fcb"], name=f"l{l}_act")
    f = _mm(act, p["down"], name=f"l{l}_down")
    x2 = _residual(x1, g2, f, name=f"l{l}_res2")
    return x2, dict(x=x, h1=h1, proj=proj, hst=hst, y_att=y_att, lse=lse, mix=mix, mo=mo, x1=x1, h2=h2, hid=hid,
                    act=act, f=f)


def _layer_bwd(dx2, sv, mod_l, p, tabs, l):
    sh1, sc1, g1, sh2, sc2, g2 = _mod_rows(mod_l)
    df, dg2 = _residual_bwd(dx2, g2, sv["f"], name=f"l{l}_res2_b")
    dact = _mm(df, p["down"], nt=True, tn=1408, name=f"l{l}_down_bx")
    d_down = _wgrad(sv["act"], df, name=f"l{l}_down_bw")
    dhid, dfcw, dfcb = _ffn_act_bwd(sv["hid"], dact, p["fcw"], p["fcb"], name=f"l{l}_act_b")
    dh2 = _mm(dhid, p["up"], tm=256, tn=512, name=f"l{l}_up_bx")
    d_up = _wgrad(dhid, sv["h2"], tk=1408, name=f"l{l}_up_bw")
    dx1, dn2, dsc2, dsh2 = _norm_mod_bwd(sv["x1"], dh2, dx2, p["norm2_g"], sc2, name=f"l{l}_norm2_b")
    dmo, dg1 = _residual_bwd(dx1, g1, sv["mo"], name=f"l{l}_res1_b")
    dmix = _mm(dmo, p["w_out"], nt=True, name=f"l{l}_out_bx")
    d_wout = _wgrad(sv["mix"], dmo, name=f"l{l}_out_bw")
    dz, dxbc, ddt, dcw, dcb, ddb, dal, dd, dng = _ssd_bwd(
        sv["proj"], sv["hst"], dmix, p["conv_w"], p["conv_b"], p["dt_bias"], p["a_log"], p["d_skip"],
        p["ssd_norm_g"], name=f"l{l}_ssd_b")
    du, dwbd, dpsc = _pool_bwd(sv["proj"], dmix, p["pool_bd"], p["pool_scale"], name=f"l{l}_pool_b")
    dq, dk, dv = _att_bwd(sv["proj"], *tabs, sv["y_att"], sv["lse"], dmix, name=f"l{l}_att_b")
    dproj = jnp.concatenate([dxbc, dz, du, dq, dk, dv, ddt], axis=1).astype(BF16)
    dh1 = _mm(dproj, p["w_in"], name=f"l{l}_proj_bx")
    d_win = _wgrad(dproj, sv["h1"], name=f"l{l}_proj_bw")
    dx0, dn1, dsc1, dsh1 = _norm_mod_bwd(sv["x"], dh1, dx1, p["norm1_g"], sc1, name=f"l{l}_norm1_b")
    dmod = jnp.concatenate([dsh1, dsc1, dg1, dsh2, dsc2, dg2], axis=1)[0]
    big = dict(w_in=d_win, w_out=d_wout, ffn_up=d_up, ffn_down=d_down)
    small = dict(norm1_g=dn1[0], ssd_conv_w=dcw[:SSD_CONV_K], ssd_conv_b=dcb[0], ssd_dt_bias=ddb[0], ssd_a_log=dal[0],
                 ssd_d=dd[0], ssd_norm_g=dng[0], pool_w=_diag_blocks(dwbd), pool_scale=dpsc[0], norm2_g=dn2[0],
                 ffn_conv_w=dfcw[:FFN_CONV_K], ffn_conv_b=dfcb[0])
    return dx0, dmod, big, small


def _example_step(x, target, pos_col, inv_freq_lane, mod, big, small, final_g):
    tabs = _rope_tables(pos_col, inv_freq_lane, name="rope_tables")
    params = [_layer_params(l, big, small) for l in range(DEPTH)]
    saved = []
    for l in range(DEPTH):
        x, sv = _layer_fwd(x, mod[l], params[l], tabs, l)
        saved.append(sv)
    loss_row, dx, dfg = _final_loss(x, final_g[None], target, name="final_loss")
    dmods, bigs, smalls = [None] * DEPTH, [None] * DEPTH, [None] * DEPTH
    for l in reversed(range(DEPTH)):
        dx, dmods[l], bigs[l], smalls[l] = _layer_bwd(dx, saved[l], mod[l], params[l], tabs, l)
    return loss_row, dx, jnp.stack(dmods, axis=0), bigs, smalls, dfg[0]


_BIG = (("w_in", (D_MODEL, IN_W // N_DEV)), ("w_out", (D_MODEL // N_DEV, D_MODEL)),
        ("ffn_up", (D_MODEL, 2 * FFN_DIM // N_DEV)), ("ffn_down", (FFN_DIM // N_DEV, D_MODEL)))
_PACK_LANES = 1024

_SMALL_GRADS = (("norm1_g", D_MODEL), ("ssd_conv_w", SSD_CONV_K * SSD_CONV_CH), ("ssd_conv_b", SSD_CONV_CH),
                ("ssd_dt_bias", 128), ("ssd_a_log", 128), ("ssd_d", 128), ("ssd_norm_g", SSD_INNER),
                ("pool_w", 4 * 64 * 64), ("pool_scale", POOL_W), ("norm2_g", D_MODEL),
                ("ffn_conv_w", FFN_CONV_K * 2 * FFN_DIM), ("ffn_conv_b", 2 * FFN_DIM))
_SMALL_PARAMS = ("ada_b", "norm1_g", "ssd_conv_w", "ssd_conv_b", "ssd_dt_bias", "ssd_a_log", "ssd_d", "ssd_norm_g",
                 "pool_w", "pool_scale", "norm2_g", "ffn_conv_w", "ffn_conv_b", "final_g")
_WEIGHT_ORDER = ("ada_w", "ada_b", "norm1_g", "w_in", "ssd_conv_w", "ssd_conv_b", "ssd_dt_bias", "ssd_a_log", "ssd_d",
                 "ssd_norm_g", "pool_w", "pool_scale", "w_out", "norm2_g", "ffn_up", "ffn_conv_w", "ffn_conv_b",
                 "ffn_down", "final_g")


def _pack(parts, rows, dtype):
    flat = jnp.concatenate([p.reshape(-1).astype(dtype) for p in parts])
    return jnp.concatenate([flat, jnp.zeros((rows * _PACK_LANES - flat.shape[0],), dtype)]).reshape(rows, _PACK_LANES)


_COLUMN_SHARDED = ("w_in", "ffn_up")


def _big_shares(w, l):
    return [(w[name][l].T if name in _COLUMN_SHARDED else w[name][l]).astype(BF16) for name, _ in _BIG]


def _unshard_big(gathered):
    out = {}
    for (name, _), g in zip(_BIG, gathered):
        full = g.reshape(N_DEV * g.shape[1], g.shape[2])
        out[name] = _permute_w_in(full) if name == "w_in" else full
    return out


def _shard_big(grads):
    out = []
    for name, _ in _BIG:
        g = _unpermute_w_in(grads[name]) if name == "w_in" else grads[name]
        out.append(g.reshape(N_DEV, g.shape[0] // N_DEV, g.shape[1]))
    return out


def kernel(x, c, positions, ada_w, ada_b, norm1_g, w_in, ssd_conv_w, ssd_conv_b, ssd_dt_bias, ssd_a_log, ssd_d, ssd_norm_g, pool_w, pool_scale, w_out, norm2_g, ffn_up, ffn_conv_w, ffn_conv_b, ffn_down, final_g, loss_target, m_ada_w, m_ada_b, m_norm1_g, m_w_in, m_ssd_conv_w, m_ssd_conv_b, m_ssd_dt_bias, m_ssd_a_log, m_ssd_d, m_ssd_norm_g, m_pool_w, m_pool_scale, m_w_out, m_norm2_g, m_ffn_up, m_ffn_conv_w, m_ffn_conv_b, m_ffn_down, m_final_g, v_ada_w, v_ada_b, v_norm1_g, v_w_in, v_ssd_conv_w, v_ssd_conv_b, v_ssd_dt_bias, v_ssd_a_log, v_ssd_d, v_ssd_norm_g, v_pool_w, v_pool_scale, v_w_out, v_norm2_g, v_ffn_up, v_ffn_conv_w, v_ffn_conv_b, v_ffn_down, v_final_g):
    w = dict(ada_w=ada_w, ada_b=ada_b, norm1_g=norm1_g, w_in=w_in, ssd_conv_w=ssd_conv_w, ssd_conv_b=ssd_conv_b,
             ssd_dt_bias=ssd_dt_bias, ssd_a_log=ssd_a_log, ssd_d=ssd_d, ssd_norm_g=ssd_norm_g, pool_w=pool_w,
             pool_scale=pool_scale, w_out=w_out, norm2_g=norm2_g, ffn_up=ffn_up, ffn_conv_w=ffn_conv_w,
             ffn_conv_b=ffn_conv_b, ffn_down=ffn_down, final_g=final_g)
    m = dict(ada_w=m_ada_w, ada_b=m_ada_b, norm1_g=m_norm1_g, w_in=m_w_in, ssd_conv_w=m_ssd_conv_w,
             ssd_conv_b=m_ssd_conv_b, ssd_dt_bias=m_ssd_dt_bias, ssd_a_log=m_ssd_a_log, ssd_d=m_ssd_d,
             ssd_norm_g=m_ssd_norm_g, pool_w=m_pool_w, pool_scale=m_pool_scale, w_out=m_w_out, norm2_g=m_norm2_g,
             ffn_up=m_ffn_up, ffn_conv_w=m_ffn_conv_w, ffn_conv_b=m_ffn_conv_b, ffn_down=m_ffn_down,
             final_g=m_final_g)
    v = dict(ada_w=v_ada_w, ada_b=v_ada_b, norm1_g=v_norm1_g, w_in=v_w_in, ssd_conv_w=v_ssd_conv_w,
             ssd_conv_b=v_ssd_conv_b, ssd_dt_bias=v_ssd_dt_bias, ssd_a_log=v_ssd_a_log, ssd_d=v_ssd_d,
             ssd_norm_g=v_ssd_norm_g, pool_w=v_pool_w, pool_scale=v_pool_scale, w_out=v_w_out, norm2_g=v_norm2_g,
             ffn_up=v_ffn_up, ffn_conv_w=v_ffn_conv_w, ffn_conv_b=v_ffn_conv_b, ffn_down=v_ffn_down,
             final_g=v_final_g)
    ix, iy, ic = _axes()
    dev = 4 * ix + 2 * iy + ic

    n_scw, n_fcw = ssd_conv_w.size, ffn_conv_w.size
    small_all = _allgather([_pack([c, ssd_conv_w, ffn_conv_w], 8, F32)], name="gather_small")[0].reshape(N_DEV, -1)
    c_all = small_all[:, :D_MODEL]
    scw = small_all[:, D_MODEL:D_MODEL + n_scw].reshape(N_DEV, DEPTH, SSD_CONV_K, -1)
    scw = scw.transpose(1, 2, 0, 3).reshape(DEPTH, SSD_CONV_K, SSD_CONV_CH)
    fcw = small_all[:, D_MODEL + n_scw:D_MODEL + n_scw + n_fcw].reshape(N_DEV, DEPTH, FFN_CONV_K, -1)
    fcw = fcw.transpose(1, 2, 0, 3).reshape(DEPTH, FFN_CONV_K, 2 * FFN_DIM)

    mod_cols = _ada_mod(c_all, ada_w, name="ada_mod")
    mod_all = _allgather([mod_cols.reshape(DEPTH * N_DEV, _ADA_SHARD)], name="gather_mod")[0]
    mod_all = mod_all.reshape(N_DEV, DEPTH, N_DEV, _ADA_SHARD)
    mod_mine = lax.dynamic_index_in_dim(mod_all, dev, axis=2, keepdims=False)
    mod = _add_rows(mod_mine.transpose(1, 0, 2).reshape(DEPTH, 6 * D_MODEL), ada_b, name="ada_bias")

    n_big = len(_BIG)
    gathered = _allgather(_big_shares(w, 0) + _big_shares(w, 1), name="gather_weights")
    big = [_unshard_big(gathered[n_big * l:n_big * (l + 1)]) for l in range(DEPTH)]
    small = dict(norm1_g=norm1_g, norm2_g=norm2_g, ssd_conv_w=scw, ssd_conv_b=ssd_conv_b, ssd_dt_bias=ssd_dt_bias,
                 ssd_a_log=ssd_a_log, ssd_d=ssd_d, ssd_norm_g=ssd_norm_g, pool_w=pool_w, pool_scale=pool_scale,
                 ffn_conv_w=fcw, ffn_conv_b=ffn_conv_b)

    inv_freq = ROPE_THETA ** (-jnp.arange(0, ROT_DIM, 2, dtype=F32) / ROT_DIM)
    lane = jnp.arange(128) % HEAD_LANES
    inv_freq_lane = jnp.where(lane < ROT_DIM, inv_freq[lane % (ROT_DIM // 2)], 0.0)[None, :]
    pos_col = positions.reshape(SEQ, 1).astype(F32)
    loss_row, dx, dmod, g_big, g_small, g_final = _example_step(
        x[0], loss_target[0], pos_col, inv_freq_lane, mod, big, small, final_g)

    blocks = _shard_big(g_big[0]) + _shard_big(g_big[1])
    from_sibling = _pair_exchange(blocks, name="grads_pair_exchange")
    core = ic.astype(jnp.int32).reshape(1)
    parts = [_pair_sum(core, b, s, name=f"grads_pair_sum_{i}") for i, (b, s) in enumerate(zip(blocks, from_sibling))]
    from_chips = _chip_exchange(parts, name="grads_chip_exchange")
    grads = {}
    for i, (name, _) in enumerate(_BIG):
        per_layer = [_sum_blocks(from_chips[n_big * l + i], name=f"grads_chip_sum_{l}_{name}") for l in range(DEPTH)]
        grads[name] = jnp.stack([g.T if name in _COLUMN_SHARDED else g for g in per_layer], axis=0)

    small_parts = [loss_row, dmod] + [jnp.stack([g_small[l][name] for l in range(DEPTH)], axis=0)
                                      for name, _ in _SMALL_GRADS] + [g_final]
    n_small = sum(p.size for p in small_parts)
    small_rows = -(-n_small // _PACK_LANES)
    gathered = _allgather([_pack(small_parts, small_rows, F32)], name="gather_small_grads")[0]
    total = _sum_blocks(gathered, name="sum_small_grads").reshape(-1)
    loss = total[0]
    off = 128
    grads["ada_b"] = total[off:off + DEPTH * 6 * D_MODEL].reshape(DEPTH, 6 * D_MODEL)
    dmod_all = gathered.reshape(N_DEV, -1)[:, off:off + DEPTH * 6 * D_MODEL].reshape(N_DEV, DEPTH, 6 * D_MODEL)
    off += DEPTH * 6 * D_MODEL
    for name, n in _SMALL_GRADS:
        grads[name] = total[off:off + DEPTH * n].reshape(DEPTH, n)
        off += DEPTH * n
    grads["final_g"] = total[off:off + D_MODEL]
    dmod_cols = lax.dynamic_slice_in_dim(dmod_all, dev * _ADA_SHARD, _ADA_SHARD, axis=2).transpose(1, 0, 2)
    grads["ada_w"] = _ada_wgrad(c_all, dmod_cols, name="ada_wgrad")
    for name in ("ssd_dt_bias", "ssd_a_log", "ssd_d"):
        grads[name] = grads[name][:, :SSD_HEADS]
    grads["pool_w"] = grads["pool_w"].reshape(pool_w.shape)
    grads["ssd_conv_w"] = lax.dynamic_slice_in_dim(
        grads["ssd_conv_w"].reshape(DEPTH, SSD_CONV_K, SSD_CONV_CH), dev * ssd_conv_w.shape[2], ssd_conv_w.shape[2], axis=2)
    grads["ffn_conv_w"] = lax.dynamic_slice_in_dim(
        grads["ffn_conv_w"].reshape(DEPTH, FFN_CONV_K, 2 * FFN_DIM), dev * ffn_conv_w.shape[2], ffn_conv_w.shape[2], axis=2)

    delta, new_m, new_v = {}, {}, {}
    for name, tr in (("ada_w", 512), ("w_in", 512), ("w_out", 256), ("ffn_up", 512), ("ffn_down", 352)):
        shp = w[name].shape
        two_d = lambda a: a.reshape(shp[0] * shp[1], shp[2])
        d_, m_, v_ = _adamw(two_d(w[name]), two_d(grads[name]), two_d(m[name]), two_d(v[name]), tr=tr,
                               name=f"adamw_{name}")
        delta[name], new_m[name], new_v[name] = (t.reshape(shp) for t in (d_, m_, v_))
    n_packed = sum(w[name].size for name in _SMALL_PARAMS)
    rows = -(-n_packed // _PACK_LANES)
    packed = [_pack([t[name] for name in _SMALL_PARAMS], rows, F32) for t in (w, grads, m, v)]
    outs = [o.reshape(-1) for o in _adamw(*packed, tr=rows, name="adamw_small")]
    off = 0
    for name in _SMALL_PARAMS:
        n = w[name].size
        delta[name], new_m[name], new_v[name] = (o[off:off + n].reshape(w[name].shape) for o in outs)
        off += n

    grad_x = dx[None]
    return (loss, grad_x, *[grads[n].reshape(w[n].shape) for n in _WEIGHT_ORDER],
            *[delta[n] for n in _WEIGHT_ORDER], *[new_m[n] for n in _WEIGHT_ORDER],
            *[new_v[n] for n in _WEIGHT_ORDER])
```

```python
import functools
import math

import jax
import jax.numpy as jnp
from jax import lax
from jax.experimental import pallas as pl
from jax.experimental.pallas import tpu as pltpu
from jax.experimental.pallas import tpu_sc as plsc

F32 = jnp.float32
BF16 = jnp.bfloat16

N_DEV = 8
D_MODEL = 1024
SEQ = 4096
DEPTH = 2
SSD_INNER = 512
SSD_HEADS = 8
SSD_HPG = 4
SSD_STATE = 128
SSD_CHUNK = 256
SSD_CONV_K = 4
SSD_CONV_CH = 1024
POOL_W = 256
POOL_WINDOWS = (2, 4, 8, 16)
ATT_W = 256
ATT_PATTERNS = ((128, 1), (512, 4), (2048, 16))
ATT_BLOCK = 128
ROT_DIM = 16
ROPE_THETA = 500000.0
IN_W = 2568
FFN_DIM = 2816
FFN_CONV_K = 3
NORM_EPS = 1e-6
HEAD_LANES = 64

ADAM_LR = 0.001
ADAM_B1 = 0.9
ADAM_B2 = 0.999
ADAM_EPS = 1e-08
ADAM_WD = 0.01
ADAM_STEP = 10

PROJ_W = 2688
VMEM_LIMIT = 56 * 1024 * 1024
CONV_HALO = 8
POOL_HALO = 16
ATT_KPAD = ATT_BLOCK * 16
MESH = pl.DeviceIdType.MESH


def _cparams(*sem):
    return pltpu.CompilerParams(dimension_semantics=sem, vmem_limit_bytes=VMEM_LIMIT)


def _silu(x):
    return x * jax.nn.sigmoid(x)


def _pick_lane(v, h):
    lane = lax.broadcasted_iota(jnp.int32, v.shape, 1)
    return jnp.sum(jnp.where(lane == h, v, 0.0), axis=1, keepdims=True)


def _pick_row(v, h):
    row = lax.broadcasted_iota(jnp.int32, v.shape, 0)
    return jnp.sum(jnp.where(row == h, v, 0.0), axis=0, keepdims=True)


def _head_of_lane(width):
    return lax.broadcasted_iota(jnp.int32, (1, width), 1) // HEAD_LANES


@functools.partial(jax.custom_vjp, nondiff_argnums=(1, 2))
def _shift_rows(x_ext, s, halo):
    y = x_ext if s == 0 else pltpu.roll(x_ext, s, 0)
    return y[halo:]


def _shift_rows_fwd(x_ext, s, halo):
    return _shift_rows(x_ext, s, halo), None


def _shift_rows_bwd(s, halo, _, g):
    ge = jnp.concatenate([jnp.zeros((halo, g.shape[1]), g.dtype), g], axis=0)
    return (ge if s == 0 else pltpu.roll(ge, ge.shape[0] - s, 0),)


_shift_rows.defvjp(_shift_rows_fwd, _shift_rows_bwd)


@functools.partial(jax.custom_vjp, nondiff_argnums=(1,))
def _roll_rows(x, s):
    return pltpu.roll(x, s, 0)


def _roll_rows_fwd(x, s):
    return _roll_rows(x, s), None


def _roll_rows_bwd(s, _, g):
    return (pltpu.roll(g, g.shape[0] - s, 0),)


_roll_rows.defvjp(_roll_rows_fwd, _roll_rows_bwd)


def _mm(a, w, *, name, nt=False, tm=512, tn=None, out_dtype=F32):
    t, k = a.shape
    n = w.shape[0] if nt else w.shape[1]
    tn = tn or n

    def body(a_ref, w_ref, o_ref):
        av = a_ref[...].astype(BF16)
        if nt:
            acc = lax.dot_general(av, w_ref[...], (((1,), (1,)), ((), ())), preferred_element_type=F32)
        else:
            acc = jnp.dot(av, w_ref[...], preferred_element_type=F32)
        o_ref[...] = acc.astype(out_dtype)

    w_spec = pl.BlockSpec((tn, k), lambda i, j: (j, 0)) if nt else pl.BlockSpec((k, tn), lambda i, j: (0, j))
    return pl.pallas_call(
        body, grid=(t // tm, n // tn),
        in_specs=[pl.BlockSpec((tm, k), lambda i, j: (i, 0)), w_spec],
        out_specs=pl.BlockSpec((tm, tn), lambda i, j: (i, j)),
        out_shape=jax.ShapeDtypeStruct((t, n), out_dtype), name=name,
        compiler_params=_cparams("parallel", "parallel"))(a, w)


def _wgrad(a, b, *, name, tk=None, tn=None, tt=512, out_dtype=BF16):
    t, k = a.shape
    n = b.shape[1]
    tk = tk or k
    tn = tn or n
    steps = t // tt

    def body(a_ref, b_ref, o_ref, acc_ref):
        s = pl.program_id(2)

        @pl.when(s == 0)
        def _():
            acc_ref[...] = jnp.zeros_like(acc_ref)

        acc_ref[...] += lax.dot_general(a_ref[...].astype(BF16), b_ref[...].astype(BF16),
                                        (((0,), (0,)), ((), ())), preferred_element_type=F32)

        @pl.when(s == steps - 1)
        def _():
            o_ref[...] = acc_ref[...].astype(out_dtype)

    return pl.pallas_call(
        body, grid=(k // tk, n // tn, steps),
        in_specs=[pl.BlockSpec((tt, tk), lambda i, j, s: (s, i)), pl.BlockSpec((tt, tn), lambda i, j, s: (s, j))],
        out_specs=pl.BlockSpec((tk, tn), lambda i, j, s: (i, j)),
        out_shape=jax.ShapeDtypeStruct((k, n), out_dtype),
        scratch_shapes=[pltpu.VMEM((tk, tn), F32)], name=name,
        compiler_params=_cparams("parallel", "parallel", "arbitrary"))(a, b)


def _norm_mod(x, g, sc, sh, *, name, tm=512):
    s, d = x.shape

    def body(x_ref, g_ref, sc_ref, sh_ref, o_ref):
        xv = x_ref[...]
        r = lax.rsqrt(jnp.mean(xv * xv, axis=-1, keepdims=True) + NORM_EPS)
        o_ref[...] = ((xv * r * g_ref[...]) * (1.0 + sc_ref[...]) + sh_ref[...]).astype(BF16)

    row = pl.BlockSpec((1, d), lambda i: (0, 0))
    return pl.pallas_call(
        body, grid=(s // tm,), in_specs=[pl.BlockSpec((tm, d), lambda i: (i, 0)), row, row, row],
        out_specs=pl.BlockSpec((tm, d), lambda i: (i, 0)),
        out_shape=jax.ShapeDtypeStruct((s, d), BF16), name=name, compiler_params=_cparams("parallel"))(x, g, sc, sh)


def _norm_mod_bwd(x, dh, dres, g, sc, *, name, tm=512):
    s, d = x.shape
    steps = s // tm

    def body(x_ref, dh_ref, dres_ref, g_ref, sc_ref, dx_ref, dg_ref, dsc_ref, dsh_ref, da_acc, dsh_acc):
        i = pl.program_id(0)

        @pl.when(i == 0)
        def _():
            da_acc[...] = jnp.zeros_like(da_acc)
            dsh_acc[...] = jnp.zeros_like(dsh_acc)

        xv = x_ref[...]
        dhv = dh_ref[...].astype(F32)
        r = lax.rsqrt(jnp.mean(xv * xv, axis=-1, keepdims=True) + NORM_EPS)
        xhat = xv * r
        gain = g_ref[...] * (1.0 + sc_ref[...])
        dxhat = dhv * gain
        dx_ref[...] = dres_ref[...] + r * (dxhat - xhat * jnp.mean(dxhat * xhat, axis=-1, keepdims=True))
        da_acc[...] += jnp.sum(dhv * xhat, axis=0, keepdims=True)
        dsh_acc[...] += jnp.sum(dhv, axis=0, keepdims=True)

        @pl.when(i == steps - 1)
        def _():
            dg_ref[...] = da_acc[...] * (1.0 + sc_ref[...])
            dsc_ref[...] = da_acc[...] * g_ref[...]
            dsh_ref[...] = dsh_acc[...]

    row = pl.BlockSpec((1, d), lambda i: (0, 0))
    tile = pl.BlockSpec((tm, d), lambda i: (i, 0))
    row_shape = jax.ShapeDtypeStruct((1, d), F32)
    return pl.pallas_call(
        body, grid=(steps,), in_specs=[tile, tile, tile, row, row],
        out_specs=[tile, row, row, row],
        out_shape=[jax.ShapeDtypeStruct((s, d), F32), row_shape, row_shape, row_shape],
        scratch_shapes=[pltpu.VMEM((1, d), F32), pltpu.VMEM((1, d), F32)], name=name,
        compiler_params=_cparams("arbitrary"))(x, dh, dres, g, sc)


def _residual(x, gate, f, *, name, tm=512):
    s, d = x.shape

    def body(x_ref, g_ref, f_ref, o_ref):
        o_ref[...] = x_ref[...] + g_ref[...] * f_ref[...]

    tile = pl.BlockSpec((tm, d), lambda i: (i, 0))
    return pl.pallas_call(
        body, grid=(s // tm,), in_specs=[tile, pl.BlockSpec((1, d), lambda i: (0, 0)), tile], out_specs=tile,
        out_shape=jax.ShapeDtypeStruct((s, d), F32), name=name, compiler_params=_cparams("parallel"))(x, gate, f)


def _residual_bwd(dx, gate, f, *, name, tm=512):
    s, d = dx.shape
    steps = s // tm

    def body(dx_ref, g_ref, f_ref, df_ref, dg_ref):
        i = pl.program_id(0)

        @pl.when(i == 0)
        def _():
            dg_ref[...] = jnp.zeros_like(dg_ref)

        dxv = dx_ref[...]
        df_ref[...] = (g_ref[...] * dxv).astype(BF16)
        dg_ref[...] += jnp.sum(dxv * f_ref[...], axis=0, keepdims=True)

    tile = pl.BlockSpec((tm, d), lambda i: (i, 0))
    row = pl.BlockSpec((1, d), lambda i: (0, 0))
    return pl.pallas_call(
        body, grid=(steps,), in_specs=[tile, row, tile], out_specs=[tile, row],
        out_shape=[jax.ShapeDtypeStruct((s, d), BF16), jax.ShapeDtypeStruct((1, d), F32)], name=name,
        compiler_params=_cparams("arbitrary"))(dx, gate, f)


def _final_loss(x, g, target, *, name, tm=512):
    s, d = x.shape
    steps = s // tm

    def body(x_ref, g_ref, t_ref, loss_ref, dx_ref, dg_ref, sq_acc):
        i = pl.program_id(0)

        @pl.when(i == 0)
        def _():
            sq_acc[...] = jnp.zeros_like(sq_acc)
            dg_ref[...] = jnp.zeros_like(dg_ref)

        xv = x_ref[...]
        r = lax.rsqrt(jnp.mean(xv * xv, axis=-1, keepdims=True) + NORM_EPS)
        xhat = xv * r
        err = xhat * g_ref[...] - t_ref[...]
        sq_acc[...] += jnp.sum(err * err, axis=0, keepdims=True)
        dy = err * (1.0 / d)
        dg_ref[...] += jnp.sum(dy * xhat, axis=0, keepdims=True)
        dxhat = dy * g_ref[...]
        dx_ref[...] = r * (dxhat - xhat * jnp.mean(dxhat * xhat, axis=-1, keepdims=True))

        @pl.when(i == steps - 1)
        def _():
            total = jnp.sum(sq_acc[...], axis=1, keepdims=True) * (0.5 / d)
            loss_ref[...] = jnp.broadcast_to(total, loss_ref.shape)

    tile = pl.BlockSpec((tm, d), lambda i: (i, 0))
    row = pl.BlockSpec((1, d), lambda i: (0, 0))
    return pl.pallas_call(
        body, grid=(steps,), in_specs=[tile, row, tile],
        out_specs=[pl.BlockSpec((1, 128), lambda i: (0, 0)), tile, row],
        out_shape=[jax.ShapeDtypeStruct((1, 128), F32), jax.ShapeDtypeStruct((s, d), F32),
                   jax.ShapeDtypeStruct((1, d), F32)],
        scratch_shapes=[pltpu.VMEM((1, d), F32)], name=name, compiler_params=_cparams("arbitrary"))(x, g, target)


def _ssd_chunk(z, xbc_ext, dt_raw, conv_w, conv_b, dt_bias, a_log, d_skip, norm_g, h_in):
    q = z.shape[0]
    gw = SSD_HPG * HEAD_LANES
    xc = conv_b
    for k in range(SSD_CONV_K):
        xc = xc + _pick_row(conv_w, k) * _shift_rows(xbc_ext, SSD_CONV_K - 1 - k, CONV_HALO)
    xc = _silu(xc)
    dt = jax.nn.softplus(dt_raw + dt_bias)
    da = dt * (-jnp.exp(a_log))
    ri = lax.broadcasted_iota(jnp.int32, (q, q), 0)
    ci = lax.broadcasted_iota(jnp.int32, (q, q), 1)
    causal = ri >= ci
    tril = causal.astype(F32)
    a_cum = jnp.dot(tril, da, preferred_element_type=F32, precision=lax.Precision.HIGHEST)
    a_cum_t = lax.dot_general(da, tril, (((0,), (1,)), ((), ())), preferred_element_type=F32,
                              precision=lax.Precision.HIGHEST)
    a_last = _pick_row(a_cum, q - 1)
    head = _head_of_lane(gw)
    ys, hs = [], []
    for g in range(2):
        xs = xc[:, gw * g:gw * (g + 1)]
        bm = xc[:, SSD_INNER + SSD_STATE * g:SSD_INNER + SSD_STATE * (g + 1)]
        cm = xc[:, SSD_INNER + 2 * SSD_STATE + SSD_STATE * g:SSD_INNER + 2 * SSD_STATE + SSD_STATE * (g + 1)]
        cb = lax.dot_general(cm.astype(BF16), bm.astype(BF16), (((1,), (1,)), ((), ())), preferred_element_type=F32)
        cols = [_pick_lane(a_cum, SSD_HPG * g + j) for j in range(SSD_HPG)]
        lasts = [_pick_lane(a_last, SSD_HPG * g + j) for j in range(SSD_HPG)]
        dt_exp = sum(jnp.where(head == j, _pick_lane(dt, SSD_HPG * g + j), 0.0) for j in range(SSD_HPG))
        d_exp = sum(jnp.where(head == j, _pick_lane(d_skip, SSD_HPG * g + j), 0.0) for j in range(SSD_HPG))
        e_cum = sum(jnp.where(head == j, jnp.exp(cols[j]), 0.0) for j in range(SSD_HPG))
        c_dec = sum(jnp.where(head == j, jnp.exp(lasts[j]), 0.0) for j in range(SSD_HPG))
        xsdt = (xs * dt_exp).astype(BF16)
        y_diag = jnp.zeros((q, gw), F32)
        st_new = jnp.zeros((SSD_STATE, gw), F32)
        for j in range(SSD_HPG):
            row = _pick_row(a_cum_t, SSD_HPG * g + j)
            lmat = jnp.exp(jnp.where(causal, cols[j] - row, -jnp.inf))
            r = jnp.dot((cb * lmat).astype(BF16), xsdt, preferred_element_type=F32)
            y_diag = y_diag + jnp.where(head == j, r, 0.0)
            bd = (bm * jnp.exp(lasts[j] - cols[j])).astype(BF16)
            st = lax.dot_general(bd, xsdt, (((0,), (0,)), ((), ())), preferred_element_type=F32)
            st_new = st_new + jnp.where(head == j, st, 0.0)
        y_off = jnp.dot(cm.astype(BF16), h_in[g].astype(BF16), preferred_element_type=F32) * e_cum
        hs.append(h_in[g] * c_dec + st_new)
        y = y_diag + y_off + d_exp * xs
        yz = y * _silu(z[:, gw * g:gw * (g + 1)])
        yz = yz * lax.rsqrt(jnp.mean(yz * yz, axis=-1, keepdims=True) + NORM_EPS)
        ys.append(yz * norm_g[:, gw * g:gw * (g + 1)])
    return jnp.concatenate(ys, axis=1), tuple(hs)


_SSD_NCHUNK = SEQ // SSD_CHUNK
_HALO_PER_CHUNK = SSD_CHUNK // CONV_HALO


def _ssd_param_specs(const):
    return [pl.BlockSpec((8, SSD_CONV_CH), const), pl.BlockSpec((1, SSD_CONV_CH), const),
            pl.BlockSpec((1, 128), const), pl.BlockSpec((1, 128), const), pl.BlockSpec((1, 128), const),
            pl.BlockSpec((1, SSD_INNER), const)]


def _ssd_fwd(proj, conv_w, conv_b, dt_bias, a_log, d_skip, norm_g, *, name):
    q = SSD_CHUNK

    def body(z_ref, xbc_ref, halo_ref, dt_ref, cw_ref, cb_ref, db_ref, al_ref, d_ref, ng_ref, y_ref, hs_ref, h_acc):
        i = pl.program_id(0)

        @pl.when(i == 0)
        def _():
            h_acc[...] = jnp.zeros_like(h_acc)

        halo = jnp.where(i == 0, 0.0, halo_ref[...])
        xbc_ext = jnp.concatenate([halo, xbc_ref[...]], axis=0)
        h_in = (h_acc[0], h_acc[1])
        hs_ref[0, 0] = h_in[0]
        hs_ref[0, 1] = h_in[1]
        y, h_out = _ssd_chunk(z_ref[...], xbc_ext, dt_ref[...], cw_ref[...], cb_ref[...], db_ref[...], al_ref[...],
                              d_ref[...], ng_ref[...], h_in)
        y_ref[...] = y
        h_acc[0] = h_out[0]
        h_acc[1] = h_out[1]

    const = lambda i: (0, 0)
    return pl.pallas_call(
        body, grid=(_SSD_NCHUNK,),
        in_specs=[pl.BlockSpec((q, SSD_INNER), lambda i: (i, 2)),
                  pl.BlockSpec((q, SSD_CONV_CH), lambda i: (i, 0)),
                  pl.BlockSpec((CONV_HALO, SSD_CONV_CH), lambda i: (jnp.maximum(i * _HALO_PER_CHUNK - 1, 0), 0)),
                  pl.BlockSpec((q, 128), lambda i: (i, 20))] + _ssd_param_specs(const),
        out_specs=[pl.BlockSpec((q, SSD_INNER), lambda i: (i, 0)),
                   pl.BlockSpec((1, 2, SSD_STATE, 256), lambda i: (i, 0, 0, 0))],
        out_shape=[jax.ShapeDtypeStruct((SEQ, SSD_INNER), F32),
                   jax.ShapeDtypeStruct((_SSD_NCHUNK, 2, SSD_STATE, 256), F32)],
        scratch_shapes=[pltpu.VMEM((2, SSD_STATE, 256), F32)], name=name,
        compiler_params=_cparams("arbitrary"))(proj, proj, proj, proj, conv_w, conv_b, dt_bias, a_log, d_skip, norm_g)


def _ssd_bwd(proj, hstates, dmix, conv_w, conv_b, dt_bias, a_log, d_skip, norm_g, *, name):
    q = SSD_CHUNK
    last = _SSD_NCHUNK - 1

    def body(z_ref, xbc_ref, halo_ref, dt_ref, hs_ref, dy_ref, cw_ref, cb_ref, db_ref, al_ref, d_ref, ng_ref,
             dz_ref, dxbc_ref, ddt_ref, dcw_ref, dcb_ref, ddb_ref, dal_ref, dd_ref, dng_ref, dh_acc, dhalo_acc):
        i = pl.program_id(0)

        @pl.when(i == 0)
        def _():
            dh_acc[...] = jnp.zeros_like(dh_acc)
            dhalo_acc[...] = jnp.zeros_like(dhalo_acc)
            for r in (dcw_ref, dcb_ref, ddb_ref, dal_ref, dd_ref, dng_ref):
                r[...] = jnp.zeros_like(r)

        halo = jnp.where(i == last, 0.0, halo_ref[...])
        xbc_ext = jnp.concatenate([halo, xbc_ref[...]], axis=0)
        _, vjp = jax.vjp(_ssd_chunk, z_ref[...], xbc_ext, dt_ref[...], cw_ref[...], cb_ref[...], db_ref[...],
                         al_ref[...], d_ref[...], ng_ref[...], (hs_ref[0, 0], hs_ref[0, 1]))
        gz, gx, gdt, gcw, gcb, gdb, gal, gd, gng, gh = vjp((dy_ref[...], (dh_acc[0], dh_acc[1])))
        dz_ref[...] = gz
        ddt_ref[...] = gdt
        dxbc_ref[...] = gx[CONV_HALO:]
        dxbc_ref[q - CONV_HALO:, :] += dhalo_acc[...]
        dhalo_acc[...] = gx[:CONV_HALO]
        dh_acc[0] = gh[0]
        dh_acc[1] = gh[1]
        dcw_ref[...] += gcw
        dcb_ref[...] += gcb
        ddb_ref[...] += gdb
        dal_ref[...] += gal
        dd_ref[...] += gd
        dng_ref[...] += gng

    const = lambda i: (0, 0)
    rev = lambda i: last - i
    row = lambda n: jax.ShapeDtypeStruct((1, n), F32)
    return pl.pallas_call(
        body, grid=(_SSD_NCHUNK,),
        in_specs=[pl.BlockSpec((q, SSD_INNER), lambda i: (rev(i), 2)),
                  pl.BlockSpec((q, SSD_CONV_CH), lambda i: (rev(i), 0)),
                  pl.BlockSpec((CONV_HALO, SSD_CONV_CH), lambda i: (jnp.maximum(rev(i) * _HALO_PER_CHUNK - 1, 0), 0)),
                  pl.BlockSpec((q, 128), lambda i: (rev(i), 20)),
                  pl.BlockSpec((1, 2, SSD_STATE, 256), lambda i: (rev(i), 0, 0, 0)),
                  pl.BlockSpec((q, SSD_INNER), lambda i: (rev(i), 0))] + _ssd_param_specs(const),
        out_specs=[pl.BlockSpec((q, SSD_INNER), lambda i: (rev(i), 0)),
                   pl.BlockSpec((q, SSD_CONV_CH), lambda i: (rev(i), 0)),
                   pl.BlockSpec((q, 128), lambda i: (rev(i), 0))] + _ssd_param_specs(const),
        out_shape=[jax.ShapeDtypeStruct((SEQ, SSD_INNER), F32), jax.ShapeDtypeStruct((SEQ, SSD_CONV_CH), F32),
                   jax.ShapeDtypeStruct((SEQ, 128), F32), jax.ShapeDtypeStruct((8, SSD_CONV_CH), F32),
                   row(SSD_CONV_CH), row(128), row(128), row(128), row(SSD_INNER)],
        scratch_shapes=[pltpu.VMEM((2, SSD_STATE, 256), F32), pltpu.VMEM((CONV_HALO, SSD_CONV_CH), F32)], name=name,
        compiler_params=_cparams("arbitrary"))(proj, proj, proj, proj, hstates, dmix, conv_w, conv_b, dt_bias, a_log,
                                                d_skip, norm_g)


def _rope_tables(pos_col, inv_freq_lane, *, name):
    s = pos_col.shape[0]

    def body(p_ref, f_ref, c_ref, s1_ref, s2_ref):
        ang = p_ref[...] * f_ref[...]
        within = lax.broadcasted_iota(jnp.int32, ang.shape, 1) % HEAD_LANES
        half = ROT_DIM // 2
        c_ref[...] = jnp.where(within < ROT_DIM, jnp.cos(ang), 1.0)
        sn = jnp.sin(ang)
        s1_ref[...] = jnp.where(within < half, -sn, 0.0)
        s2_ref[...] = jnp.where((within >= half) & (within < ROT_DIM), sn, 0.0)

    shp = jax.ShapeDtypeStruct((s, 128), F32)
    return pl.pallas_call(body, out_shape=[shp, shp, shp], name=name,
                          compiler_params=pltpu.CompilerParams(vmem_limit_bytes=VMEM_LIMIT))(pos_col, inv_freq_lane)


def _rope(t, c, s1, s2):
    half = ROT_DIM // 2
    return t * c + pltpu.roll(t, 128 - half, 1) * s1 + pltpu.roll(t, half, 1) * s2


def _rope_t(g, c, s1, s2):
    half = ROT_DIM // 2
    return g * c + pltpu.roll(g * s1, half, 1) + pltpu.roll(g * s2, 128 - half, 1)


def _att_valid(b):
    qi = lax.broadcasted_iota(jnp.int32, (ATT_BLOCK, 2 * ATT_BLOCK), 0)
    kj = lax.broadcasted_iota(jnp.int32, (ATT_BLOCK, 2 * ATT_BLOCK), 1)
    rel = qi + ATT_BLOCK - kj
    return (rel >= 0) & (rel <= ATT_BLOCK) & (b * ATT_BLOCK + kj - ATT_BLOCK >= 0)


def _att_slices(i, d):
    if d == 1:
        qstart = pl.multiple_of(i * ATT_BLOCK, ATT_BLOCK)
        return i, pl.ds(qstart, ATT_BLOCK), pl.ds(pl.multiple_of(qstart - ATT_BLOCK + ATT_KPAD, ATT_BLOCK), 2 * ATT_BLOCK)
    r = i % d
    b = i // d
    qstart = r + d * ATT_BLOCK * b
    return b, pl.ds(qstart, ATT_BLOCK, stride=d), pl.ds(qstart - ATT_BLOCK * d + ATT_KPAD, 2 * ATT_BLOCK, stride=d)


_ATT_NBLK = SEQ // ATT_BLOCK
_ATT_SCALE = HEAD_LANES ** -0.5
_ATT_UNROLL = 4


def _att_fwd(proj, cos, sin1, sin2, *, name):
    s = SEQ

    def body(q_ref, k_ref, v_ref, c_ref, s1_ref, s2_ref, o_ref, lse_ref, qs, ks, vs, acc, m_s, l_s):
        c, s1, s2 = c_ref[...], s1_ref[...], s2_ref[...]
        qs[...] = _rope(q_ref[...], c, s1, s2) * _ATT_SCALE
        zeros = jnp.zeros((ATT_KPAD, 128), F32)
        ks[pl.ds(0, ATT_KPAD), :] = zeros
        vs[pl.ds(0, ATT_KPAD), :] = zeros
        ks[pl.ds(ATT_KPAD, s), :] = _rope(k_ref[...], c, s1, s2)
        vs[pl.ds(ATT_KPAD, s), :] = v_ref[...]
        head0 = _head_of_lane(128) == 0

        for bi, (_, d) in enumerate(ATT_PATTERNS):
            def blk(i, carry, d=d, first=(bi == 0)):
                b, sq, sk = _att_slices(i, d)
                qb = qs[sq, :]
                kw = ks[sk, :].astype(BF16)
                vw = vs[sk, :].astype(BF16)
                valid = _att_valid(b)
                ms, ls, os_ = [], [], []
                for hh in range(2):
                    qh = jnp.where(head0 if hh == 0 else ~head0, qb, 0.0).astype(BF16)
                    sc = lax.dot_general(qh, kw, (((1,), (1,)), ((), ())), preferred_element_type=F32)
                    sc = jnp.where(valid, sc, -jnp.inf)
                    mb = jnp.max(sc, axis=1, keepdims=True)
                    p = jnp.exp(sc - mb)
                    ms.append(mb)
                    ls.append(jnp.sum(p, axis=1, keepdims=True))
                    os_.append(jnp.dot(p.astype(BF16), vw, preferred_element_type=F32))
                m_b = jnp.where(head0, ms[0], ms[1])
                l_b = jnp.where(head0, ls[0], ls[1])
                o_b = jnp.where(head0, os_[0], os_[1])
                if first:
                    m_s[sq, :] = m_b
                    l_s[sq, :] = l_b
                    acc[sq, :] = o_b
                else:
                    m_old = m_s[sq, :]
                    m_new = jnp.maximum(m_old, m_b)
                    a_old = jnp.exp(m_old - m_new)
                    a_b = jnp.exp(m_b - m_new)
                    m_s[sq, :] = m_new
                    l_s[sq, :] = l_s[sq, :] * a_old + l_b * a_b
                    acc[sq, :] = acc[sq, :] * a_old + o_b * a_b
                return carry

            lax.fori_loop(0, _ATT_NBLK, blk, 0, unroll=_ATT_UNROLL)

        o_ref[...] = acc[...] / l_s[...]
        lse_ref[...] = m_s[...] + jnp.log(l_s[...])

    col = lambda base: pl.BlockSpec((s, 128), lambda p: (0, base + p))
    tab = pl.BlockSpec((s, 128), lambda p: (0, 0))
    big = pltpu.VMEM((ATT_KPAD + s, 128), F32)
    tok = pltpu.VMEM((s, 128), F32)
    return pl.pallas_call(
        body, grid=(2,), in_specs=[col(14), col(16), col(18), tab, tab, tab],
        out_specs=[pl.BlockSpec((s, 128), lambda p: (0, p)), pl.BlockSpec((s, 128), lambda p: (0, p))],
        out_shape=[jax.ShapeDtypeStruct((s, ATT_W), F32), jax.ShapeDtypeStruct((s, ATT_W), F32)],
        scratch_shapes=[tok, big, big, tok, tok, tok], name=name,
        compiler_params=_cparams("arbitrary"))(proj, proj, proj, cos, sin1, sin2)


def _att_bwd(proj, cos, sin1, sin2, out, lse, dmix, *, name):
    s = SEQ

    def body(proj_ref, c_hbm, s1_hbm, s2_hbm, out_hbm, lse_hbm, dmix_hbm, dq_hbm, dk_hbm, dv_hbm,
             c_ref, s1_ref, s2_ref, o_ref, lse_ref, do_ref, qs, ks, vs, dqs, dks, dvs):
        pltpu.sync_copy(c_hbm, c_ref)
        pltpu.sync_copy(s1_hbm, s1_ref)
        pltpu.sync_copy(s2_hbm, s2_ref)
        head0 = _head_of_lane(128) == 0
        zeros = jnp.zeros((ATT_KPAD, 128), F32)
        for pair in range(2):
            lanes = pl.ds(128 * pair, 128)
            pltpu.sync_copy(proj_ref.at[:, pl.ds(128 * (14 + pair), 128)], qs)
            pltpu.sync_copy(proj_ref.at[:, pl.ds(128 * (16 + pair), 128)], ks.at[pl.ds(ATT_KPAD, s), :])
            pltpu.sync_copy(proj_ref.at[:, pl.ds(128 * (18 + pair), 128)], vs.at[pl.ds(ATT_KPAD, s), :])
            pltpu.sync_copy(out_hbm.at[:, lanes], o_ref)
            pltpu.sync_copy(lse_hbm.at[:, lanes], lse_ref)
            pltpu.sync_copy(dmix_hbm.at[:, pl.ds(128 * (6 + pair), 128)], do_ref)
            c, s1, s2 = c_ref[...], s1_ref[...], s2_ref[...]
            qs[...] = _rope(qs[...], c, s1, s2) * _ATT_SCALE
            ks[pl.ds(0, ATT_KPAD), :] = zeros
            vs[pl.ds(0, ATT_KPAD), :] = zeros
            ks[pl.ds(ATT_KPAD, s), :] = _rope(ks[pl.ds(ATT_KPAD, s), :], c, s1, s2)
            dqs[...] = jnp.zeros_like(dqs)
            dks[...] = jnp.zeros_like(dks)
            dvs[...] = jnp.zeros_like(dvs)

            for _, d in ATT_PATTERNS:
                def blk(i, carry, d=d):
                    b, sq, sk = _att_slices(i, d)
                    qb = qs[sq, :]
                    kw = ks[sk, :].astype(BF16)
                    vw = vs[sk, :].astype(BF16)
                    dob = do_ref[sq, :]
                    lse_b = lse_ref[sq, :]
                    dd = dob * o_ref[sq, :]
                    valid = _att_valid(b)
                    dq_b = jnp.zeros((ATT_BLOCK, 128), F32)
                    dk_w = jnp.zeros((2 * ATT_BLOCK, 128), F32)
                    dv_w = jnp.zeros((2 * ATT_BLOCK, 128), F32)
                    for hh in range(2):
                        hm = head0 if hh == 0 else ~head0
                        qh = jnp.where(hm, qb, 0.0).astype(BF16)
                        doh = jnp.where(hm, dob, 0.0).astype(BF16)
                        lse_h = _pick_lane(lse_b, hh * HEAD_LANES)
                        d_h = jnp.sum(jnp.where(hm, dd, 0.0), axis=1, keepdims=True)
                        sc = lax.dot_general(qh, kw, (((1,), (1,)), ((), ())), preferred_element_type=F32)
                        p = jnp.where(valid, jnp.exp(sc - lse_h), 0.0)
                        dp = lax.dot_general(doh, vw, (((1,), (1,)), ((), ())), preferred_element_type=F32)
                        ds = (p * (dp - d_h)).astype(BF16)
                        dq_b = dq_b + jnp.where(hm, jnp.dot(ds, kw, preferred_element_type=F32), 0.0)
                        dk_w = dk_w + lax.dot_general(ds, qh, (((0,), (0,)), ((), ())), preferred_element_type=F32)
                        dv_w = dv_w + lax.dot_general(p.astype(BF16), doh, (((0,), (0,)), ((), ())),
                                                      preferred_element_type=F32)
                    dqs[sq, :] += dq_b
                    dks[sk, :] += dk_w
                    dvs[sk, :] += dv_w
                    return carry

                lax.fori_loop(0, _ATT_NBLK, blk, 0, unroll=_ATT_UNROLL)

            dqs[...] = _rope_t(dqs[...] * _ATT_SCALE, c, s1, s2)
            dks[pl.ds(ATT_KPAD, s), :] = _rope_t(dks[pl.ds(ATT_KPAD, s), :], c, s1, s2)
            pltpu.sync_copy(dqs, dq_hbm.at[:, lanes])
            pltpu.sync_copy(dks.at[pl.ds(ATT_KPAD, s), :], dk_hbm.at[:, lanes])
            pltpu.sync_copy(dvs.at[pl.ds(ATT_KPAD, s), :], dv_hbm.at[:, lanes])

    hbm = pl.BlockSpec(memory_space=pl.ANY)
    big = pltpu.VMEM((ATT_KPAD + s, 128), F32)
    tok = pltpu.VMEM((s, 128), F32)
    shp = jax.ShapeDtypeStruct((s, ATT_W), F32)
    return pl.pallas_call(
        body, in_specs=[hbm] * 7, out_specs=[hbm] * 3, out_shape=[shp, shp, shp],
        scratch_shapes=[tok] * 6 + [tok, big, big, tok, big, big], name=name,
        compiler_params=pltpu.CompilerParams(vmem_limit_bytes=VMEM_LIMIT))(proj, cos, sin1, sin2, out, lse, dmix)


_POOL_TM = 512
_POOL_NT = SEQ // _POOL_TM
_POOL_HALO_PER_TILE = _POOL_TM // POOL_HALO


def _pool_tile(u_ext, w_bd, scale, t0):
    s2 = u_ext + _roll_rows(u_ext, 1)
    s4 = s2 + _roll_rows(s2, 2)
    s8 = s4 + _roll_rows(s4, 4)
    s16 = s8 + _roll_rows(s8, 8)
    grp = _head_of_lane(POOL_W)
    sel = jnp.where(grp == 0, s2, jnp.where(grp == 1, s4, jnp.where(grp == 2, s8, s16)))[POOL_HALO:]
    t = sel.shape[0]
    pos = t0 + lax.broadcasted_iota(jnp.int32, (t, POOL_W), 0) + 1
    win = jnp.where(grp == 0, 2, jnp.where(grp == 1, 4, jnp.where(grp == 2, 8, 16)))
    cnt = jnp.minimum(pos, win).astype(F32)
    diff = sel / cnt - u_ext[POOL_HALO:]
    return jnp.dot(diff.astype(BF16), w_bd.astype(BF16), preferred_element_type=F32) * scale


def _pool_fwd(proj, w_bd, scale, *, name):
    tm = _POOL_TM

    def body(u_ref, halo_ref, w_ref, sc_ref, y_ref):
        i = pl.program_id(0)
        halo = jnp.where(i == 0, 0.0, halo_ref[...])
        u_ext = jnp.concatenate([halo, u_ref[...]], axis=0)
        y_ref[...] = _pool_tile(u_ext, w_ref[...], sc_ref[...], i * tm)

    return pl.pallas_call(
        body, grid=(_POOL_NT,),
        in_specs=[pl.BlockSpec((tm, POOL_W), lambda i: (i, 6)),
                  pl.BlockSpec((POOL_HALO, POOL_W), lambda i: (jnp.maximum(i * _POOL_HALO_PER_TILE - 1, 0), 6)),
                  pl.BlockSpec((POOL_W, POOL_W), lambda i: (0, 0)), pl.BlockSpec((1, POOL_W), lambda i: (0, 0))],
        out_specs=pl.BlockSpec((tm, POOL_W), lambda i: (i, 0)),
        out_shape=jax.ShapeDtypeStruct((SEQ, POOL_W), F32), name=name,
        compiler_params=_cparams("parallel"))(proj, proj, w_bd, scale)


def _pool_bwd(proj, dmix, w_bd, scale, *, name):
    tm = _POOL_TM
    last = _POOL_NT - 1

    def body(u_ref, halo_ref, dy_ref, w_ref, sc_ref, du_ref, dw_ref, dsc_ref, dhalo_acc):
        i = pl.program_id(0)

        @pl.when(i == 0)
        def _():
            dhalo_acc[...] = jnp.zeros_like(dhalo_acc)
            dw_ref[...] = jnp.zeros_like(dw_ref)
            dsc_ref[...] = jnp.zeros_like(dsc_ref)

        tile = last - i
        halo = jnp.where(tile == 0, 0.0, halo_ref[...])
        u_ext = jnp.concatenate([halo, u_ref[...]], axis=0)
        _, vjp = jax.vjp(functools.partial(_pool_tile, t0=tile * tm), u_ext, w_ref[...], sc_ref[...])
        gu, gw, gs = vjp(dy_ref[...])
        du_ref[...] = gu[POOL_HALO:]
        du_ref[tm - POOL_HALO:, :] += dhalo_acc[...]
        dhalo_acc[...] = gu[:POOL_HALO]
        dw_ref[...] += gw
        dsc_ref[...] += gs

    rev = lambda i: last - i
    return pl.pallas_call(
        body, grid=(_POOL_NT,),
        in_specs=[pl.BlockSpec((tm, POOL_W), lambda i: (rev(i), 6)),
                  pl.BlockSpec((POOL_HALO, POOL_W), lambda i: (jnp.maximum(rev(i) * _POOL_HALO_PER_TILE - 1, 0), 6)),
                  pl.BlockSpec((tm, POOL_W), lambda i: (rev(i), 2)),
                  pl.BlockSpec((POOL_W, POOL_W), lambda i: (0, 0)), pl.BlockSpec((1, POOL_W), lambda i: (0, 0))],
        out_specs=[pl.BlockSpec((tm, POOL_W), lambda i: (rev(i), 0)),
                   pl.BlockSpec((POOL_W, POOL_W), lambda i: (0, 0)), pl.BlockSpec((1, POOL_W), lambda i: (0, 0))],
        out_shape=[jax.ShapeDtypeStruct((SEQ, POOL_W), F32), jax.ShapeDtypeStruct((POOL_W, POOL_W), F32),
                   jax.ShapeDtypeStruct((1, POOL_W), F32)],
        scratch_shapes=[pltpu.VMEM((POOL_HALO, POOL_W), F32)], name=name,
        compiler_params=_cparams("arbitrary"))(proj, proj, dmix, w_bd, scale)


_FFN_TM = 256
_FFN_NT = SEQ // _FFN_TM
_FFN_HALO_PER_TILE = _FFN_TM // CONV_HALO


def _ffn_act_tile(hid_ext, conv_w, conv_b):
    hc = conv_b
    for k in range(FFN_CONV_K):
        hc = hc + _pick_row(conv_w, k) * _shift_rows(hid_ext, FFN_CONV_K - 1 - k, CONV_HALO)
    return _silu(hc[:, :FFN_DIM]) * hc[:, FFN_DIM:]


def _ffn_act(hid, conv_w, conv_b, *, name):
    tm = _FFN_TM
    w = 2 * FFN_DIM

    def body(h_ref, halo_ref, cw_ref, cb_ref, a_ref):
        i = pl.program_id(0)
        halo = jnp.where(i == 0, 0.0, halo_ref[...])
        hid_ext = jnp.concatenate([halo, h_ref[...]], axis=0)
        a_ref[...] = _ffn_act_tile(hid_ext, cw_ref[...], cb_ref[...]).astype(BF16)

    return pl.pallas_call(
        body, grid=(_FFN_NT,),
        in_specs=[pl.BlockSpec((tm, w), lambda i: (i, 0)),
                  pl.BlockSpec((CONV_HALO, w), lambda i: (jnp.maximum(i * _FFN_HALO_PER_TILE - 1, 0), 0)),
                  pl.BlockSpec((8, w), lambda i: (0, 0)), pl.BlockSpec((1, w), lambda i: (0, 0))],
        out_specs=pl.BlockSpec((tm, FFN_DIM), lambda i: (i, 0)),
        out_shape=jax.ShapeDtypeStruct((SEQ, FFN_DIM), BF16), name=name,
        compiler_params=_cparams("parallel"))(hid, hid, conv_w, conv_b)


def _ffn_act_bwd(hid, dact, conv_w, conv_b, *, name):
    tm = _FFN_TM
    w = 2 * FFN_DIM
    last = _FFN_NT - 1

    def body(h_ref, halo_ref, da_ref, cw_ref, cb_ref, dh_ref, dcw_ref, dcb_ref, dhalo_acc):
        i = pl.program_id(0)

        @pl.when(i == 0)
        def _():
            dhalo_acc[...] = jnp.zeros_like(dhalo_acc)
            dcw_ref[...] = jnp.zeros_like(dcw_ref)
            dcb_ref[...] = jnp.zeros_like(dcb_ref)

        halo = jnp.where(i == last, 0.0, halo_ref[...])
        hid_ext = jnp.concatenate([halo, h_ref[...]], axis=0)
        _, vjp = jax.vjp(_ffn_act_tile, hid_ext, cw_ref[...], cb_ref[...])
        gh, gw, gb = vjp(da_ref[...])
        carry = dhalo_acc[...]
        dhalo_acc[...] = gh[:CONV_HALO]
        dh_ref[...] = jnp.concatenate([gh[CONV_HALO:tm], gh[tm:] + carry], axis=0).astype(BF16)
        dcw_ref[...] += gw
        dcb_ref[...] += gb

    rev = lambda i: last - i
    return pl.pallas_call(
        body, grid=(_FFN_NT,),
        in_specs=[pl.BlockSpec((tm, w), lambda i: (rev(i), 0)),
                  pl.BlockSpec((CONV_HALO, w), lambda i: (jnp.maximum(rev(i) * _FFN_HALO_PER_TILE - 1, 0), 0)),
                  pl.BlockSpec((tm, FFN_DIM), lambda i: (rev(i), 0)),
                  pl.BlockSpec((8, w), lambda i: (0, 0)), pl.BlockSpec((1, w), lambda i: (0, 0))],
        out_specs=[pl.BlockSpec((tm, w), lambda i: (rev(i), 0)),
                   pl.BlockSpec((8, w), lambda i: (0, 0)), pl.BlockSpec((1, w), lambda i: (0, 0))],
        out_shape=[jax.ShapeDtypeStruct((SEQ, w), BF16), jax.ShapeDtypeStruct((8, w), F32),
                   jax.ShapeDtypeStruct((1, w), F32)],
        scratch_shapes=[pltpu.VMEM((CONV_HALO, w), F32)], name=name,
        compiler_params=_cparams("arbitrary"))(hid, hid, dact, conv_w, conv_b)


_HBM = pl.BlockSpec(memory_space=pl.ANY)


def _axes():
    return lax.axis_index("x"), lax.axis_index("y"), lax.axis_index("c")


def _handshake(peers):
    barrier = pltpu.get_barrier_semaphore()
    for peer in peers:
        pl.semaphore_signal(barrier, inc=1, device_id=peer, device_id_type=MESH)
    pl.semaphore_wait(barrier, len(peers))


def _allgather_body(x_refs, out_refs, send_sems, recv_sems, local_sems, own_barrier):
    n = len(x_refs)
    x, y, c = _axes()
    me, sibling = (x, y, c), (x, y, 1 - c)
    chips = [(1 - x, y), (x, 1 - y), (1 - x, 1 - y)]
    if own_barrier:
        _handshake([sibling] + [(*chip, c) for chip in chips])

    def slot(a, px, py, pc):
        return out_refs[a].at[4 * px + 2 * py + pc]

    def copy(a, k, block, to, src=None):
        return pltpu.make_async_remote_copy(
            src_ref=slot(a, *block) if src is None else src, dst_ref=slot(a, *block),
            send_sem=send_sems.at[a, k], recv_sem=recv_sems.at[a, k], device_id=to, device_id_type=MESH)

    mines, firsts = [], []
    for a in range(n):
        mines.append(pltpu.make_async_copy(x_refs[a], slot(a, *me), local_sems.at[a]))
        mines[-1].start()
        first = [copy(a, 0, me, sibling, src=x_refs[a])]
        first += [copy(a, 1 + j, me, (*chip, c), src=x_refs[a]) for j, chip in enumerate(chips)]
        for cp in first:
            cp.start()
        firsts += first
    passed = []
    for j, chip in enumerate(chips):
        for a in range(n):
            copy(a, 1 + j, (*chip, c), me).wait_recv()
            passed.append(copy(a, 4 + j, (*chip, c), sibling))
            passed[-1].start()
    for a in range(n):
        copy(a, 0, sibling, me).wait_recv()
    for j, chip in enumerate(chips):
        for a in range(n):
            copy(a, 4 + j, (*chip, 1 - c), me).wait_recv()
    for cp in firsts + passed:
        cp.wait_send()
    for cp in mines:
        cp.wait()


def _allgather_sems(n):
    return [pltpu.SemaphoreType.DMA((n, 7)), pltpu.SemaphoreType.DMA((n, 7)), pltpu.SemaphoreType.DMA((n,))]


def _allgather(xs, *, name):
    n = len(xs)

    def body(*refs):
        _allgather_body(refs[:n], refs[n:2 * n], *refs[2 * n:], own_barrier=False)

    return pl.pallas_call(
        body, out_shape=[jax.ShapeDtypeStruct((N_DEV,) + xb.shape, xb.dtype) for xb in xs],
        in_specs=[_HBM] * n, out_specs=[_HBM] * n, scratch_shapes=_allgather_sems(n), name=name)(*xs)


def _allgather_async(xs, *, name, collective_id):
    n = len(xs)
    x_refs = [jax.new_ref(xb, memory_space=pltpu.MemorySpace.HBM) for xb in xs]
    out_refs = [jax.empty_ref(jax.ShapeDtypeStruct((N_DEV,) + xb.shape, xb.dtype), memory_space=pltpu.MemorySpace.HBM)
                for xb in xs]

    @pl.kernel(mesh=plsc.ScalarSubcoreMesh(axis_name="sequencer", num_cores=1), name=name,
               scratch_types=tuple(_allgather_sems(n)),
               compiler_params=pltpu.CompilerParams(collective_id=collective_id))
    def launch(send_sems, recv_sems, local_sems):
        _allgather_body(x_refs, out_refs, send_sems, recv_sems, local_sems, own_barrier=True)

    launch()
    return [r[...] for r in out_refs]


def _pair_exchange(blocks, *, name, collective_id):
    n = len(blocks)
    hbm = pltpu.MemorySpace.HBM
    in_refs = [jax.new_ref(b, memory_space=hbm) for b in blocks]
    out_refs = [jax.empty_ref(jax.ShapeDtypeStruct((4,) + b.shape[1:], b.dtype), memory_space=hbm) for b in blocks]

    @pl.kernel(mesh=plsc.ScalarSubcoreMesh(axis_name="sequencer", num_cores=1), name=name,
               scratch_types=(pltpu.SemaphoreType.DMA((n, 4)), pltpu.SemaphoreType.DMA((n, 4))),
               compiler_params=pltpu.CompilerParams(collective_id=collective_id))
    def launch(send_sems, recv_sems):
        x, y, c = _axes()
        _handshake([(x, y, 1 - c)])
        copies = [pltpu.make_async_remote_copy(
            src_ref=in_refs[a].at[2 * s + (1 - c)], dst_ref=out_refs[a].at[s], send_sem=send_sems.at[a, s],
            recv_sem=recv_sems.at[a, s], device_id=(x, y, 1 - c), device_id_type=MESH)
            for a in range(n) for s in range(4)]
        for cp in copies:
            cp.start()
        for cp in copies:
            cp.wait_recv()
        for cp in copies:
            cp.wait_send()

    launch()
    return [r[...] for r in out_refs]


def _chip_exchange(parts, *, name, collective_id):
    n = len(parts)
    hbm = pltpu.MemorySpace.HBM
    in_refs = [jax.new_ref(p, memory_space=hbm) for p in parts]
    out_refs = [jax.empty_ref(jax.ShapeDtypeStruct(p.shape, p.dtype), memory_space=hbm) for p in parts]

    @pl.kernel(mesh=plsc.ScalarSubcoreMesh(axis_name="sequencer", num_cores=1), name=name,
               scratch_types=(pltpu.SemaphoreType.DMA((n, 3)), pltpu.SemaphoreType.DMA((n, 3)),
                              pltpu.SemaphoreType.DMA((n,))),
               compiler_params=pltpu.CompilerParams(collective_id=collective_id))
    def launch(send_sems, recv_sems, local_sems):
        x, y, c = _axes()
        my_chip = 2 * x + y
        chips = [(1 - x, y), (x, 1 - y), (1 - x, 1 - y)]
        _handshake([(*chip, c) for chip in chips])
        locals_ = [pltpu.make_async_copy(in_refs[a].at[my_chip], out_refs[a].at[my_chip], local_sems.at[a])
                   for a in range(n)]
        for cp in locals_:
            cp.start()
        copies = [pltpu.make_async_remote_copy(
            src_ref=in_refs[a].at[2 * px + py], dst_ref=out_refs[a].at[my_chip], send_sem=send_sems.at[a, k],
            recv_sem=recv_sems.at[a, k], device_id=(px, py, c), device_id_type=MESH)
            for a in range(n) for k, (px, py) in enumerate(chips)]
        for cp in copies:
            cp.start()
        for cp in copies:
            cp.wait_recv()
        for cp in copies:
            cp.wait_send()
        for cp in locals_:
            cp.wait()

    launch()
    return [r[...] for r in out_refs]


def _pair_sum(core, blocks, from_sibling, *, name):
    _, r, cdim = blocks.shape

    def body(core_ref, a_ref, b_ref, o_ref):
        o_ref[...] = (a_ref[...].astype(F32) + b_ref[...].astype(F32)).astype(o_ref.dtype)

    return pl.pallas_call(
        body,
        grid_spec=pltpu.PrefetchScalarGridSpec(
            num_scalar_prefetch=1, grid=(4,),
            in_specs=[pl.BlockSpec((1, r, cdim), lambda s, core_ref: (2 * s + core_ref[0], 0, 0)),
                      pl.BlockSpec((1, r, cdim), lambda s, core_ref: (s, 0, 0))],
            out_specs=pl.BlockSpec((1, r, cdim), lambda s, core_ref: (s, 0, 0))),
        out_shape=jax.ShapeDtypeStruct(from_sibling.shape, from_sibling.dtype), name=name,
        compiler_params=_cparams("parallel"))(core, blocks, from_sibling)


def _sum_blocks(a, *, name, tr=None):
    n, r, cdim = a.shape
    tr = tr or r

    def body(a_ref, o_ref):
        acc = a_ref[0].astype(F32)
        for k in range(1, n):
            acc = acc + a_ref[k].astype(F32)
        o_ref[...] = acc

    return pl.pallas_call(body, grid=(r // tr,), in_specs=[pl.BlockSpec((n, tr, cdim), lambda i: (0, i, 0))],
                          out_specs=pl.BlockSpec((tr, cdim), lambda i: (i, 0)),
                          out_shape=jax.ShapeDtypeStruct((r, cdim), F32), name=name,
                          compiler_params=_cparams("parallel"))(a)


_ADA_SHARD = 6 * D_MODEL // N_DEV


def _ada_mod(c_all, ada_w, *, name):
    def body(c_ref, w_ref, o_ref):
        o_ref[0] = jnp.dot(_silu(c_ref[...]).astype(BF16), w_ref[0].astype(BF16), preferred_element_type=F32)

    return pl.pallas_call(
        body, grid=(DEPTH,),
        in_specs=[pl.BlockSpec((N_DEV, D_MODEL), lambda l: (0, 0)),
                  pl.BlockSpec((1, D_MODEL, _ADA_SHARD), lambda l: (l, 0, 0))],
        out_specs=pl.BlockSpec((1, N_DEV, _ADA_SHARD), lambda l: (l, 0, 0)),
        out_shape=jax.ShapeDtypeStruct((DEPTH, N_DEV, _ADA_SHARD), F32), name=name,
        compiler_params=_cparams("parallel"))(c_all, ada_w)


def _ada_wgrad(c_all, dmod_cols, *, name):
    def body(c_ref, d_ref, o_ref):
        o_ref[0] = lax.dot_general(_silu(c_ref[...]), d_ref[0], (((0,), (0,)), ((), ())),
                                   preferred_element_type=F32, precision=lax.Precision.HIGHEST)

    return pl.pallas_call(
        body, grid=(DEPTH,),
        in_specs=[pl.BlockSpec((N_DEV, D_MODEL), lambda l: (0, 0)),
                  pl.BlockSpec((1, N_DEV, _ADA_SHARD), lambda l: (l, 0, 0))],
        out_specs=pl.BlockSpec((1, D_MODEL, _ADA_SHARD), lambda l: (l, 0, 0)),
        out_shape=jax.ShapeDtypeStruct((DEPTH, D_MODEL, _ADA_SHARD), F32), name=name,
        compiler_params=_cparams("parallel"))(c_all, dmod_cols)


def _add_rows(a, b, *, name):
    def body(a_ref, b_ref, o_ref):
        o_ref[...] = a_ref[...] + b_ref[...]

    return pl.pallas_call(body, out_shape=jax.ShapeDtypeStruct(a.shape, a.dtype), name=name)(a, b)


def _adamw(w, g, m, v, *, name, tr):
    r, cdim = w.shape
    c1 = 1.0 - ADAM_B1 ** ADAM_STEP
    c2 = 1.0 - ADAM_B2 ** ADAM_STEP

    def body(w_ref, g_ref, m_ref, v_ref, d_ref, mo_ref, vo_ref):
        gv = g_ref[...]
        mn = ADAM_B1 * m_ref[...] + (1.0 - ADAM_B1) * gv
        vn = ADAM_B2 * v_ref[...] + (1.0 - ADAM_B2) * (gv * gv)
        mo_ref[...] = mn
        vo_ref[...] = vn
        d_ref[...] = -ADAM_LR * ((mn / c1) / (jnp.sqrt(vn / c2) + ADAM_EPS) + ADAM_WD * w_ref[...])

    spec = pl.BlockSpec((tr, cdim), lambda i: (i, 0))
    shp = jax.ShapeDtypeStruct((r, cdim), F32)
    return pl.pallas_call(body, grid=(r // tr,), in_specs=[spec] * 4, out_specs=[spec] * 3, out_shape=[shp] * 3,
                          name=name, compiler_params=_cparams("parallel"))(w, g, m, v)


def _pad_rows(a, rows):
    return jnp.concatenate([a, jnp.zeros((rows - a.shape[0],) + a.shape[1:], a.dtype)], axis=0)


def _pad_lanes(a, lanes):
    return jnp.concatenate([a, jnp.zeros(a.shape[:-1] + (lanes - a.shape[-1],), a.dtype)], axis=-1)


def _permute_w_in(wt):
    return jnp.concatenate([wt[512:1536], wt[:512], wt[1544:], wt[1536:1544],
                            jnp.zeros((PROJ_W - IN_W, wt.shape[1]), wt.dtype)], axis=0)


def _unpermute_w_in(wp):
    return jnp.concatenate([wp[1024:1536], wp[:1024], wp[2560:2568], wp[1536:2560]], axis=0)


def _block_diag(w):
    rows = []
    for g in range(4):
        rows.append(jnp.concatenate([w[g] if k == g else jnp.zeros_like(w[g]) for k in range(4)], axis=1))
    return jnp.concatenate(rows, axis=0)


def _diag_blocks(wbd):
    return jnp.stack([wbd[64 * g:64 * (g + 1), 64 * g:64 * (g + 1)] for g in range(4)], axis=0)


def _layer_params(l, big, small):
    return dict(
        w_in=big["w_in"], w_out=big["w_out"], up=big["ffn_up"], down=big["ffn_down"],
        norm1_g=small["norm1_g"][l][None], norm2_g=small["norm2_g"][l][None],
        conv_w=_pad_rows(small["ssd_conv_w"][l], 8), conv_b=small["ssd_conv_b"][l][None],
        dt_bias=_pad_lanes(small["ssd_dt_bias"][l][None], 128), a_log=_pad_lanes(small["ssd_a_log"][l][None], 128),
        d_skip=_pad_lanes(small["ssd_d"][l][None], 128), ssd_norm_g=small["ssd_norm_g"][l][None],
        pool_bd=_block_diag(small["pool_w"][l]), pool_scale=small["pool_scale"][l][None],
        fcw=_pad_rows(small["ffn_conv_w"][l], 8), fcb=small["ffn_conv_b"][l][None])


def _mod_rows(mod_l):
    return [mod_l[None, D_MODEL * i:D_MODEL * (i + 1)] for i in range(6)]


def _layer_fwd(x, mod_l, p, tabs, l):
    sh1, sc1, g1, sh2, sc2, g2 = _mod_rows(mod_l)
    h1 = _norm_mod(x, p["norm1_g"], sc1, sh1, name=f"l{l}_norm1")
    proj = _mm(h1, p["w_in"], nt=True, name=f"l{l}_proj")
    y_ssd, hst = _ssd_fwd(proj, p["conv_w"], p["conv_b"], p["dt_bias"], p["a_log"], p["d_skip"], p["ssd_norm_g"],
                          name=f"l{l}_ssd")
    y_pool = _pool_fwd(proj, p["pool_bd"], p["pool_scale"], name=f"l{l}_pool")
    y_att, lse = _att_fwd(proj, *tabs, name=f"l{l}_att")
    mix = jnp.concatenate([y_ssd, y_pool, y_att], axis=1).astype(BF16)
    mo = _mm(mix, p["w_out"], name=f"l{l}_out")
    x1 = _residual(x, g1, mo, name=f"l{l}_res1")
    h2 = _norm_mod(x1, p["norm2_g"], sc2, sh2, name=f"l{l}_norm2")
    hid = _mm(h2, p["up"], nt=True, tm=256, name=f"l{l}_up")
    act = _ffn_act(hid, p["fcw"], p["fcb"], name=f"l{l}_act")
    f = _mm(act, p["down"], name=f"l{l}_down")
    x2 = _residual(x1, g2, f, name=f"l{l}_res2")
    return x2, dict(x=x, h1=h1, proj=proj, hst=hst, y_att=y_att, lse=lse, mix=mix, mo=mo, x1=x1, h2=h2, hid=hid,
                    act=act, f=f)


def _layer_bwd(dx2, sv, mod_l, p, tabs, l, exchange):
    sh1, sc1, g1, sh2, sc2, g2 = _mod_rows(mod_l)
    df, dg2 = _residual_bwd(dx2, g2, sv["f"], name=f"l{l}_res2_b")
    dact = _mm(df, p["down"], nt=True, tn=1408, name=f"l{l}_down_bx")
    d_down = _wgrad(sv["act"], df, name=f"l{l}_down_bw")
    dhid, dfcw, dfcb = _ffn_act_bwd(sv["hid"], dact, p["fcw"], p["fcb"], name=f"l{l}_act_b")
    dh2 = _mm(dhid, p["up"], tm=256, name=f"l{l}_up_bx")
    d_up = _wgrad(dhid, sv["h2"], tk=1408, name=f"l{l}_up_bw")
    exchange(l, "ffn", dict(ffn_up=d_up, ffn_down=d_down))
    dx1, dn2, dsc2, dsh2 = _norm_mod_bwd(sv["x1"], dh2, dx2, p["norm2_g"], sc2, name=f"l{l}_norm2_b")
    dmo, dg1 = _residual_bwd(dx1, g1, sv["mo"], name=f"l{l}_res1_b")
    dmix = _mm(dmo, p["w_out"], nt=True, name=f"l{l}_out_bx")
    d_wout = _wgrad(sv["mix"], dmo, name=f"l{l}_out_bw")
    dz, dxbc, ddt, dcw, dcb, ddb, dal, dd, dng = _ssd_bwd(
        sv["proj"], sv["hst"], dmix, p["conv_w"], p["conv_b"], p["dt_bias"], p["a_log"], p["d_skip"],
        p["ssd_norm_g"], name=f"l{l}_ssd_b")
    du, dwbd, dpsc = _pool_bwd(sv["proj"], dmix, p["pool_bd"], p["pool_scale"], name=f"l{l}_pool_b")
    dq, dk, dv = _att_bwd(sv["proj"], *tabs, sv["y_att"], sv["lse"], dmix, name=f"l{l}_att_b")
    dproj = jnp.concatenate([dxbc, dz, du, dq, dk, dv, ddt], axis=1).astype(BF16)
    dh1 = _mm(dproj, p["w_in"], name=f"l{l}_proj_bx")
    d_win = _wgrad(dproj, sv["h1"], name=f"l{l}_proj_bw")
    exchange(l, "mix", dict(w_in=d_win, w_out=d_wout))
    dx0, dn1, dsc1, dsh1 = _norm_mod_bwd(sv["x"], dh1, dx1, p["norm1_g"], sc1, name=f"l{l}_norm1_b")
    dmod = jnp.concatenate([dsh1, dsc1, dg1, dsh2, dsc2, dg2], axis=1)[0]
    small = dict(norm1_g=dn1[0], ssd_conv_w=dcw[:SSD_CONV_K], ssd_conv_b=dcb[0], ssd_dt_bias=ddb[0], ssd_a_log=dal[0],
                 ssd_d=dd[0], ssd_norm_g=dng[0], pool_w=_diag_blocks(dwbd), pool_scale=dpsc[0], norm2_g=dn2[0],
                 ffn_conv_w=dfcw[:FFN_CONV_K], ffn_conv_b=dfcb[0])
    return dx0, dmod, small


def _example_step(x, target, pos_col, inv_freq_lane, mod, gather, small, final_g, exchange):
    tabs = _rope_tables(pos_col, inv_freq_lane, name="rope_tables")
    params, saved = [], []
    after = None
    for l in range(DEPTH):
        params.append(_layer_params(l, gather(l, after), small))
        x, sv = _layer_fwd(x, mod[l], params[l], tabs, l)
        saved.append(sv)
        after = sv["proj"]
    loss_row, dx, dfg = _final_loss(x, final_g[None], target, name="final_loss")
    dmods, smalls = [None] * DEPTH, [None] * DEPTH
    for l in reversed(range(DEPTH)):
        dx, dmods[l], smalls[l] = _layer_bwd(dx, saved[l], mod[l], params[l], tabs, l, exchange)
    return loss_row, dx, jnp.stack(dmods, axis=0), smalls, dfg[0]


_BIG = (("w_in", (D_MODEL, IN_W // N_DEV)), ("w_out", (D_MODEL // N_DEV, D_MODEL)),
        ("ffn_up", (D_MODEL, 2 * FFN_DIM // N_DEV)), ("ffn_down", (FFN_DIM // N_DEV, D_MODEL)))
_PACK_LANES = 1024

_SMALL_GRADS = (("norm1_g", D_MODEL), ("ssd_conv_w", SSD_CONV_K * SSD_CONV_CH), ("ssd_conv_b", SSD_CONV_CH),
                ("ssd_dt_bias", 128), ("ssd_a_log", 128), ("ssd_d", 128), ("ssd_norm_g", SSD_INNER),
                ("pool_w", 4 * 64 * 64), ("pool_scale", POOL_W), ("norm2_g", D_MODEL),
                ("ffn_conv_w", FFN_CONV_K * 2 * FFN_DIM), ("ffn_conv_b", 2 * FFN_DIM))
_SMALL_PARAMS = ("ada_b", "norm1_g", "ssd_conv_w", "ssd_conv_b", "ssd_dt_bias", "ssd_a_log", "ssd_d", "ssd_norm_g",
                 "pool_w", "pool_scale", "norm2_g", "ffn_conv_w", "ffn_conv_b", "final_g")
_WEIGHT_ORDER = ("ada_w", "ada_b", "norm1_g", "w_in", "ssd_conv_w", "ssd_conv_b", "ssd_dt_bias", "ssd_a_log", "ssd_d",
                 "ssd_norm_g", "pool_w", "pool_scale", "w_out", "norm2_g", "ffn_up", "ffn_conv_w", "ffn_conv_b",
                 "ffn_down", "final_g")


def _pack(parts, rows, dtype):
    flat = jnp.concatenate([p.reshape(-1).astype(dtype) for p in parts])
    return jnp.concatenate([flat, jnp.zeros((rows * _PACK_LANES - flat.shape[0],), dtype)]).reshape(rows, _PACK_LANES)


_COLUMN_SHARDED = ("w_in", "ffn_up")


def _big_shares(w, l):
    return [(w[name][l].T if name in _COLUMN_SHARDED else w[name][l]).astype(BF16) for name, _ in _BIG]


def _unshard_big(gathered):
    out = {}
    for (name, _), g in zip(_BIG, gathered):
        full = g.reshape(N_DEV * g.shape[1], g.shape[2])
        out[name] = _permute_w_in(full) if name == "w_in" else full
    return out


def _shard_big(grads):
    out = []
    for name, g in grads.items():
        g = _unpermute_w_in(g) if name == "w_in" else g
        out.append(g.reshape(N_DEV, g.shape[0] // N_DEV, g.shape[1]))
    return out


def kernel(x, c, positions, ada_w, ada_b, norm1_g, w_in, ssd_conv_w, ssd_conv_b, ssd_dt_bias, ssd_a_log, ssd_d, ssd_norm_g, pool_w, pool_scale, w_out, norm2_g, ffn_up, ffn_conv_w, ffn_conv_b, ffn_down, final_g, loss_target, m_ada_w, m_ada_b, m_norm1_g, m_w_in, m_ssd_conv_w, m_ssd_conv_b, m_ssd_dt_bias, m_ssd_a_log, m_ssd_d, m_ssd_norm_g, m_pool_w, m_pool_scale, m_w_out, m_norm2_g, m_ffn_up, m_ffn_conv_w, m_ffn_conv_b, m_ffn_down, m_final_g, v_ada_w, v_ada_b, v_norm1_g, v_w_in, v_ssd_conv_w, v_ssd_conv_b, v_ssd_dt_bias, v_ssd_a_log, v_ssd_d, v_ssd_norm_g, v_pool_w, v_pool_scale, v_w_out, v_norm2_g, v_ffn_up, v_ffn_conv_w, v_ffn_conv_b, v_ffn_down, v_final_g):
    w = dict(ada_w=ada_w, ada_b=ada_b, norm1_g=norm1_g, w_in=w_in, ssd_conv_w=ssd_conv_w, ssd_conv_b=ssd_conv_b,
             ssd_dt_bias=ssd_dt_bias, ssd_a_log=ssd_a_log, ssd_d=ssd_d, ssd_norm_g=ssd_norm_g, pool_w=pool_w,
             pool_scale=pool_scale, w_out=w_out, norm2_g=norm2_g, ffn_up=ffn_up, ffn_conv_w=ffn_conv_w,
             ffn_conv_b=ffn_conv_b, ffn_down=ffn_down, final_g=final_g)
    m = dict(ada_w=m_ada_w, ada_b=m_ada_b, norm1_g=m_norm1_g, w_in=m_w_in, ssd_conv_w=m_ssd_conv_w,
             ssd_conv_b=m_ssd_conv_b, ssd_dt_bias=m_ssd_dt_bias, ssd_a_log=m_ssd_a_log, ssd_d=m_ssd_d,
             ssd_norm_g=m_ssd_norm_g, pool_w=m_pool_w, pool_scale=m_pool_scale, w_out=m_w_out, norm2_g=m_norm2_g,
             ffn_up=m_ffn_up, ffn_conv_w=m_ffn_conv_w, ffn_conv_b=m_ffn_conv_b, ffn_down=m_ffn_down,
             final_g=m_final_g)
    v = dict(ada_w=v_ada_w, ada_b=v_ada_b, norm1_g=v_norm1_g, w_in=v_w_in, ssd_conv_w=v_ssd_conv_w,
             ssd_conv_b=v_ssd_conv_b, ssd_dt_bias=v_ssd_dt_bias, ssd_a_log=v_ssd_a_log, ssd_d=v_ssd_d,
             ssd_norm_g=v_ssd_norm_g, pool_w=v_pool_w, pool_scale=v_pool_scale, w_out=v_w_out, norm2_g=v_norm2_g,
             ffn_up=v_ffn_up, ffn_conv_w=v_ffn_conv_w, ffn_conv_b=v_ffn_conv_b, ffn_down=v_ffn_down,
             final_g=v_final_g)
    ix, iy, ic = _axes()
    dev = 4 * ix + 2 * iy + ic

    n_scw, n_fcw = ssd_conv_w.size, ffn_conv_w.size
    small_all = _allgather([_pack([c, ssd_conv_w, ffn_conv_w], 8, F32)], name="gather_small")[0].reshape(N_DEV, -1)
    c_all = small_all[:, :D_MODEL]
    scw = small_all[:, D_MODEL:D_MODEL + n_scw].reshape(N_DEV, DEPTH, SSD_CONV_K, -1)
    scw = scw.transpose(1, 2, 0, 3).reshape(DEPTH, SSD_CONV_K, SSD_CONV_CH)
    fcw = small_all[:, D_MODEL + n_scw:D_MODEL + n_scw + n_fcw].reshape(N_DEV, DEPTH, FFN_CONV_K, -1)
    fcw = fcw.transpose(1, 2, 0, 3).reshape(DEPTH, FFN_CONV_K, 2 * FFN_DIM)

    mod_cols = _ada_mod(c_all, ada_w, name="ada_mod")
    mod_all = _allgather([mod_cols.reshape(DEPTH * N_DEV, _ADA_SHARD)], name="gather_mod")[0]
    mod_all = mod_all.reshape(N_DEV, DEPTH, N_DEV, _ADA_SHARD)
    mod_mine = lax.dynamic_index_in_dim(mod_all, dev, axis=2, keepdims=False)
    mod = _add_rows(mod_mine.transpose(1, 0, 2).reshape(DEPTH, 6 * D_MODEL), ada_b, name="ada_bias")

    def gather(l, after):
        shares = _big_shares(w, l)
        if after is not None:
            shares, _ = lax.optimization_barrier((shares, after))
        return _unshard_big(_allgather_async(shares, name=f"gather_weights_l{l}", collective_id=1 + l))

    core = ic.astype(jnp.int32).reshape(1)
    from_chips = {}

    def exchange(l, group, g):
        cid = 3 + 4 * l + 2 * (group == "mix")
        blocks = _shard_big(g)
        from_sibling = _pair_exchange(blocks, name=f"grads_pair_exchange_l{l}_{group}", collective_id=cid)
        parts = [_pair_sum(core, b, s, name=f"grads_pair_sum_l{l}_{name}")
                 for name, b, s in zip(g, blocks, from_sibling)]
        got = _chip_exchange(parts, name=f"grads_chip_exchange_l{l}_{group}", collective_id=cid + 1)
        from_chips.update({(l, name): t for name, t in zip(g, got)})

    small = dict(norm1_g=norm1_g, norm2_g=norm2_g, ssd_conv_w=scw, ssd_conv_b=ssd_conv_b, ssd_dt_bias=ssd_dt_bias,
                 ssd_a_log=ssd_a_log, ssd_d=ssd_d, ssd_norm_g=ssd_norm_g, pool_w=pool_w, pool_scale=pool_scale,
                 ffn_conv_w=fcw, ffn_conv_b=ffn_conv_b)

    inv_freq = ROPE_THETA ** (-jnp.arange(0, ROT_DIM, 2, dtype=F32) / ROT_DIM)
    lane = jnp.arange(128) % HEAD_LANES
    inv_freq_lane = jnp.where(lane < ROT_DIM, inv_freq[lane % (ROT_DIM // 2)], 0.0)[None, :]
    pos_col = positions.reshape(SEQ, 1).astype(F32)
    loss_row, dx, dmod, g_small, g_final = _example_step(
        x[0], loss_target[0], pos_col, inv_freq_lane, mod, gather, small, final_g, exchange)

    grads = {}
    for name, _ in _BIG:
        per_layer = [_sum_blocks(from_chips[l, name], name=f"grads_chip_sum_l{l}_{name}") for l in range(DEPTH)]
        grads[name] = jnp.stack([g.T if name in _COLUMN_SHARDED else g for g in per_layer], axis=0)

    small_parts = [loss_row, dmod] + [jnp.stack([g_small[l][name] for l in range(DEPTH)], axis=0)
                                      for name, _ in _SMALL_GRADS] + [g_final]
    n_small = sum(p.size for p in small_parts)
    small_rows = -(-n_small // _PACK_LANES)
    gathered = _allgather([_pack(small_parts, small_rows, F32)], name="gather_small_grads")[0]
    total = _sum_blocks(gathered, name="sum_small_grads").reshape(-1)
    loss = total[0]
    off = 128
    grads["ada_b"] = total[off:off + DEPTH * 6 * D_MODEL].reshape(DEPTH, 6 * D_MODEL)
    dmod_all = gathered.reshape(N_DEV, -1)[:, off:off + DEPTH * 6 * D_MODEL].reshape(N_DEV, DEPTH, 6 * D_MODEL)
    off += DEPTH * 6 * D_MODEL
    for name, n in _SMALL_GRADS:
        grads[name] = total[off:off + DEPTH * n].reshape(DEPTH, n)
        off += DEPTH * n
    grads["final_g"] = total[off:off + D_MODEL]
    dmod_cols = lax.dynamic_slice_in_dim(dmod_all, dev * _ADA_SHARD, _ADA_SHARD, axis=2).transpose(1, 0, 2)
    grads["ada_w"] = _ada_wgrad(c_all, dmod_cols, name="ada_wgrad")
    for name in ("ssd_dt_bias", "ssd_a_log", "ssd_d"):
        grads[name] = grads[name][:, :SSD_HEADS]
    grads["pool_w"] = grads["pool_w"].reshape(pool_w.shape)
    grads["ssd_conv_w"] = lax.dynamic_slice_in_dim(
        grads["ssd_conv_w"].reshape(DEPTH, SSD_CONV_K, SSD_CONV_CH), dev * ssd_conv_w.shape[2], ssd_conv_w.shape[2], axis=2)
    grads["ffn_conv_w"] = lax.dynamic_slice_in_dim(
        grads["ffn_conv_w"].reshape(DEPTH, FFN_CONV_K, 2 * FFN_DIM), dev * ffn_conv_w.shape[2], ffn_conv_w.shape[2], axis=2)

    delta, new_m, new_v = {}, {}, {}
    for name, tr in (("ada_w", 512), ("w_in", 512), ("w_out", 256), ("ffn_up", 512), ("ffn_down", 352)):
        shp = w[name].shape
        two_d = lambda a: a.reshape(shp[0] * shp[1], shp[2])
        d_, m_, v_ = _adamw(two_d(w[name]), two_d(grads[name]), two_d(m[name]), two_d(v[name]), tr=tr,
                               name=f"adamw_{name}")
        delta[name], new_m[name], new_v[name] = (t.reshape(shp) for t in (d_, m_, v_))
    n_packed = sum(w[name].size for name in _SMALL_PARAMS)
    rows = -(-n_packed // _PACK_LANES)
    packed = [_pack([t[name] for name in _SMALL_PARAMS], rows, F32) for t in (w, grads, m, v)]
    outs = [o.reshape(-1) for o in _adamw(*packed, tr=rows, name="adamw_small")]
    off = 0
    for name in _SMALL_PARAMS:
        n = w[name].size
        delta[name], new_m[name], new_v[name] = (o[off:off + n].reshape(w[name].shape) for o in outs)
        off += n

    grad_x = dx[None]
    return (loss, grad_x, *[grads[n].reshape(w[n].shape) for n in _WEIGHT_ORDER],
            *[delta[n] for n in _WEIGHT_ORDER], *[new_m[n] for n in _WEIGHT_ORDER],
            *[new_v[n] for n in _WEIGHT_ORDER])
```

```python
import functools
import math

import jax
import jax.numpy as jnp
from jax import lax
from jax.experimental import pallas as pl
from jax.experimental.pallas import tpu as pltpu
from jax.experimental.pallas import tpu_sc as plsc

F32 = jnp.float32
BF16 = jnp.bfloat16

N_DEV = 8
D_MODEL = 1024
SEQ = 4096
DEPTH = 2
SSD_INNER = 512
SSD_HEADS = 8
SSD_HPG = 4
SSD_STATE = 128
SSD_CHUNK = 256
SSD_CONV_K = 4
SSD_CONV_CH = 1024
POOL_W = 256
POOL_WINDOWS = (2, 4, 8, 16)
ATT_W = 256
ATT_PATTERNS = ((128, 1), (512, 4), (2048, 16))
ATT_BLOCK = 128
ROT_DIM = 16
ROPE_THETA = 500000.0
IN_W = 2568
FFN_DIM = 2816
FFN_CONV_K = 3
NORM_EPS = 1e-6
HEAD_LANES = 64

ADAM_LR = 0.001
ADAM_B1 = 0.9
ADAM_B2 = 0.999
ADAM_EPS = 1e-08
ADAM_WD = 0.01
ADAM_STEP = 10

PROJ_W = 2816
PROJ_SSD_W = 1792
PROJ_Z_BLK = 2
PROJ_DT_BLK = 12
PROJ_POOL_BLK = 7
PROJ_Q_BLK, PROJ_K_BLK, PROJ_V_BLK = 16, 18, 20
MIX_POOL_BLK = 2
MIX_ATT_BLK = 6
VMEM_LIMIT = 56 * 1024 * 1024
CONV_HALO = 8
POOL_HALO = 16
ATT_KPAD = ATT_BLOCK * 16
MESH = pl.DeviceIdType.MESH
_HBM = pl.BlockSpec(memory_space=pl.ANY)


def _cparams(*sem):
    return pltpu.CompilerParams(dimension_semantics=sem, vmem_limit_bytes=VMEM_LIMIT)


def _silu(x):
    return x * jax.nn.sigmoid(x)


def _pick_lane(v, h):
    lane = lax.broadcasted_iota(jnp.int32, v.shape, 1)
    return jnp.sum(jnp.where(lane == h, v, 0.0), axis=1, keepdims=True)


def _pick_row(v, h):
    row = lax.broadcasted_iota(jnp.int32, v.shape, 0)
    return jnp.sum(jnp.where(row == h, v, 0.0), axis=0, keepdims=True)


def _head_of_lane(width):
    return lax.broadcasted_iota(jnp.int32, (1, width), 1) // HEAD_LANES


@functools.partial(jax.custom_vjp, nondiff_argnums=(1, 2))
def _shift_rows(x_ext, s, halo):
    y = x_ext if s == 0 else pltpu.roll(x_ext, s, 0)
    return y[halo:]


def _shift_rows_fwd(x_ext, s, halo):
    return _shift_rows(x_ext, s, halo), None


def _shift_rows_bwd(s, halo, _, g):
    ge = jnp.concatenate([jnp.zeros((halo, g.shape[1]), g.dtype), g], axis=0)
    return (ge if s == 0 else pltpu.roll(ge, ge.shape[0] - s, 0),)


_shift_rows.defvjp(_shift_rows_fwd, _shift_rows_bwd)


@functools.partial(jax.custom_vjp, nondiff_argnums=(1,))
def _roll_rows(x, s):
    return pltpu.roll(x, s, 0)


def _roll_rows_fwd(x, s):
    return _roll_rows(x, s), None


def _roll_rows_bwd(s, _, g):
    return (pltpu.roll(g, g.shape[0] - s, 0),)


_roll_rows.defvjp(_roll_rows_fwd, _roll_rows_bwd)


def _mm(a, w, *, name, nt=False, tm=512, tn=None, out_dtype=F32):
    t, k = a.shape
    n = w.shape[0] if nt else w.shape[1]
    tn = tn or n

    def body(a_ref, w_ref, o_ref):
        av = a_ref[...].astype(BF16)
        if nt:
            acc = lax.dot_general(av, w_ref[...], (((1,), (1,)), ((), ())), preferred_element_type=F32)
        else:
            acc = jnp.dot(av, w_ref[...], preferred_element_type=F32)
        o_ref[...] = acc.astype(out_dtype)

    w_spec = pl.BlockSpec((tn, k), lambda i, j: (j, 0)) if nt else pl.BlockSpec((k, tn), lambda i, j: (0, j))
    return pl.pallas_call(
        body, grid=(t // tm, n // tn),
        in_specs=[pl.BlockSpec((tm, k), lambda i, j: (i, 0)), w_spec],
        out_specs=pl.BlockSpec((tm, tn), lambda i, j: (i, j)),
        out_shape=jax.ShapeDtypeStruct((t, n), out_dtype), name=name,
        compiler_params=_cparams("parallel", "parallel"))(a, w)


def _wgrad(a, b, *, name, tk=None, tn=None, tt=512, out_dtype=BF16):
    t, k = a.shape
    n = b.shape[1]
    tk = tk or k
    tn = tn or n
    steps = t // tt

    def body(a_ref, b_ref, o_ref, acc_ref):
        s = pl.program_id(2)

        @pl.when(s == 0)
        def _():
            acc_ref[...] = jnp.zeros_like(acc_ref)

        acc_ref[...] += lax.dot_general(a_ref[...].astype(BF16), b_ref[...].astype(BF16),
                                        (((0,), (0,)), ((), ())), preferred_element_type=F32)

        @pl.when(s == steps - 1)
        def _():
            o_ref[...] = acc_ref[...].astype(out_dtype)

    return pl.pallas_call(
        body, grid=(k // tk, n // tn, steps),
        in_specs=[pl.BlockSpec((tt, tk), lambda i, j, s: (s, i)), pl.BlockSpec((tt, tn), lambda i, j, s: (s, j))],
        out_specs=pl.BlockSpec((tk, tn), lambda i, j, s: (i, j)),
        out_shape=jax.ShapeDtypeStruct((k, n), out_dtype),
        scratch_shapes=[pltpu.VMEM((tk, tn), F32)], name=name,
        compiler_params=_cparams("parallel", "parallel", "arbitrary"))(a, b)


def _norm_mod(x, g, sc, sh, *, name, tm=512):
    s, d = x.shape

    def body(x_ref, g_ref, sc_ref, sh_ref, o_ref):
        xv = x_ref[...]
        r = lax.rsqrt(jnp.mean(xv * xv, axis=-1, keepdims=True) + NORM_EPS)
        o_ref[...] = ((xv * r * g_ref[...]) * (1.0 + sc_ref[...]) + sh_ref[...]).astype(BF16)

    row = pl.BlockSpec((1, d), lambda i: (0, 0))
    return pl.pallas_call(
        body, grid=(s // tm,), in_specs=[pl.BlockSpec((tm, d), lambda i: (i, 0)), row, row, row],
        out_specs=pl.BlockSpec((tm, d), lambda i: (i, 0)),
        out_shape=jax.ShapeDtypeStruct((s, d), BF16), name=name, compiler_params=_cparams("parallel"))(x, g, sc, sh)


def _norm_mod_bwd(x, dh, dres, g, sc, *, name, tm=512):
    s, d = x.shape
    steps = s // tm

    def body(x_ref, dh_ref, dres_ref, g_ref, sc_ref, dx_ref, dg_ref, dsc_ref, dsh_ref, da_acc, dsh_acc):
        i = pl.program_id(0)

        @pl.when(i == 0)
        def _():
            da_acc[...] = jnp.zeros_like(da_acc)
            dsh_acc[...] = jnp.zeros_like(dsh_acc)

        xv = x_ref[...]
        dhv = dh_ref[...].astype(F32)
        r = lax.rsqrt(jnp.mean(xv * xv, axis=-1, keepdims=True) + NORM_EPS)
        xhat = xv * r
        gain = g_ref[...] * (1.0 + sc_ref[...])
        dxhat = dhv * gain
        dx_ref[...] = dres_ref[...] + r * (dxhat - xhat * jnp.mean(dxhat * xhat, axis=-1, keepdims=True))
        da_acc[...] += jnp.sum(dhv * xhat, axis=0, keepdims=True)
        dsh_acc[...] += jnp.sum(dhv, axis=0, keepdims=True)

        @pl.when(i == steps - 1)
        def _():
            dg_ref[...] = da_acc[...] * (1.0 + sc_ref[...])
            dsc_ref[...] = da_acc[...] * g_ref[...]
            dsh_ref[...] = dsh_acc[...]

    row = pl.BlockSpec((1, d), lambda i: (0, 0))
    tile = pl.BlockSpec((tm, d), lambda i: (i, 0))
    row_shape = jax.ShapeDtypeStruct((1, d), F32)
    return pl.pallas_call(
        body, grid=(steps,), in_specs=[tile, tile, tile, row, row],
        out_specs=[tile, row, row, row],
        out_shape=[jax.ShapeDtypeStruct((s, d), F32), row_shape, row_shape, row_shape],
        scratch_shapes=[pltpu.VMEM((1, d), F32), pltpu.VMEM((1, d), F32)], name=name,
        compiler_params=_cparams("arbitrary"))(x, dh, dres, g, sc)


def _residual(x, gate, f, *, name, tm=512):
    s, d = x.shape

    def body(x_ref, g_ref, f_ref, o_ref):
        o_ref[...] = x_ref[...] + g_ref[...] * f_ref[...]

    tile = pl.BlockSpec((tm, d), lambda i: (i, 0))
    return pl.pallas_call(
        body, grid=(s // tm,), in_specs=[tile, pl.BlockSpec((1, d), lambda i: (0, 0)), tile], out_specs=tile,
        out_shape=jax.ShapeDtypeStruct((s, d), F32), name=name, compiler_params=_cparams("parallel"))(x, gate, f)


def _residual_bwd(dx, gate, f, *, name, tm=512):
    s, d = dx.shape
    steps = s // tm

    def body(dx_ref, g_ref, f_ref, df_ref, dg_ref):
        i = pl.program_id(0)

        @pl.when(i == 0)
        def _():
            dg_ref[...] = jnp.zeros_like(dg_ref)

        dxv = dx_ref[...]
        df_ref[...] = (g_ref[...] * dxv).astype(BF16)
        dg_ref[...] += jnp.sum(dxv * f_ref[...], axis=0, keepdims=True)

    tile = pl.BlockSpec((tm, d), lambda i: (i, 0))
    row = pl.BlockSpec((1, d), lambda i: (0, 0))
    return pl.pallas_call(
        body, grid=(steps,), in_specs=[tile, row, tile], out_specs=[tile, row],
        out_shape=[jax.ShapeDtypeStruct((s, d), BF16), jax.ShapeDtypeStruct((1, d), F32)], name=name,
        compiler_params=_cparams("arbitrary"))(dx, gate, f)


def _final_loss(x, g, target, *, name, tm=512):
    s, d = x.shape
    steps = s // tm

    def body(x_ref, g_ref, t_ref, loss_ref, dx_ref, dg_ref, sq_acc):
        i = pl.program_id(0)

        @pl.when(i == 0)
        def _():
            sq_acc[...] = jnp.zeros_like(sq_acc)
            dg_ref[...] = jnp.zeros_like(dg_ref)

        xv = x_ref[...]
        r = lax.rsqrt(jnp.mean(xv * xv, axis=-1, keepdims=True) + NORM_EPS)
        xhat = xv * r
        err = xhat * g_ref[...] - t_ref[...]
        sq_acc[...] += jnp.sum(err * err, axis=0, keepdims=True)
        dy = err * (1.0 / d)
        dg_ref[...] += jnp.sum(dy * xhat, axis=0, keepdims=True)
        dxhat = dy * g_ref[...]
        dx_ref[...] = r * (dxhat - xhat * jnp.mean(dxhat * xhat, axis=-1, keepdims=True))

        @pl.when(i == steps - 1)
        def _():
            total = jnp.sum(sq_acc[...], axis=1, keepdims=True) * (0.5 / d)
            loss_ref[...] = jnp.broadcast_to(total, loss_ref.shape)

    tile = pl.BlockSpec((tm, d), lambda i: (i, 0))
    row = pl.BlockSpec((1, d), lambda i: (0, 0))
    return pl.pallas_call(
        body, grid=(steps,), in_specs=[tile, row, tile],
        out_specs=[pl.BlockSpec((1, 128), lambda i: (0, 0)), tile, row],
        out_shape=[jax.ShapeDtypeStruct((1, 128), F32), jax.ShapeDtypeStruct((s, d), F32),
                   jax.ShapeDtypeStruct((1, d), F32)],
        scratch_shapes=[pltpu.VMEM((1, d), F32)], name=name, compiler_params=_cparams("arbitrary"))(x, g, target)


def _ssd_chunk(z, xbc_ext, dt_raw, conv_w, conv_b, dt_bias, a_log, d_skip, norm_g, h_in):
    q = z.shape[0]
    gw = SSD_HPG * HEAD_LANES
    xc = conv_b
    for k in range(SSD_CONV_K):
        xc = xc + _pick_row(conv_w, k) * _shift_rows(xbc_ext, SSD_CONV_K - 1 - k, CONV_HALO)
    xc = _silu(xc)
    dt = jax.nn.softplus(dt_raw + dt_bias)
    da = dt * (-jnp.exp(a_log))
    ri = lax.broadcasted_iota(jnp.int32, (q, q), 0)
    ci = lax.broadcasted_iota(jnp.int32, (q, q), 1)
    causal = ri >= ci
    tril = causal.astype(F32)
    a_cum = jnp.dot(tril, da, preferred_element_type=F32, precision=lax.Precision.HIGHEST)
    a_cum_t = lax.dot_general(da, tril, (((0,), (1,)), ((), ())), preferred_element_type=F32,
                              precision=lax.Precision.HIGHEST)
    a_last = _pick_row(a_cum, q - 1)
    head = _head_of_lane(gw)
    ys, hs = [], []
    for g in range(2):
        xs = xc[:, gw * g:gw * (g + 1)]
        bm = xc[:, SSD_INNER + SSD_STATE * g:SSD_INNER + SSD_STATE * (g + 1)]
        cm = xc[:, SSD_INNER + 2 * SSD_STATE + SSD_STATE * g:SSD_INNER + 2 * SSD_STATE + SSD_STATE * (g + 1)]
        cb = lax.dot_general(cm.astype(BF16), bm.astype(BF16), (((1,), (1,)), ((), ())), preferred_element_type=F32)
        cols = [_pick_lane(a_cum, SSD_HPG * g + j) for j in range(SSD_HPG)]
        lasts = [_pick_lane(a_last, SSD_HPG * g + j) for j in range(SSD_HPG)]
        dt_exp = sum(jnp.where(head == j, _pick_lane(dt, SSD_HPG * g + j), 0.0) for j in range(SSD_HPG))
        d_exp = sum(jnp.where(head == j, _pick_lane(d_skip, SSD_HPG * g + j), 0.0) for j in range(SSD_HPG))
        e_cum = sum(jnp.where(head == j, jnp.exp(cols[j]), 0.0) for j in range(SSD_HPG))
        c_dec = sum(jnp.where(head == j, jnp.exp(lasts[j]), 0.0) for j in range(SSD_HPG))
        xsdt = (xs * dt_exp).astype(BF16)
        y_diag = jnp.zeros((q, gw), F32)
        st_new = jnp.zeros((SSD_STATE, gw), F32)
        for j in range(SSD_HPG):
            row = _pick_row(a_cum_t, SSD_HPG * g + j)
            lmat = jnp.exp(jnp.where(causal, cols[j] - row, -jnp.inf))
            r = jnp.dot((cb * lmat).astype(BF16), xsdt, preferred_element_type=F32)
            y_diag = y_diag + jnp.where(head == j, r, 0.0)
            bd = (bm * jnp.exp(lasts[j] - cols[j])).astype(BF16)
            st = lax.dot_general(bd, xsdt, (((0,), (0,)), ((), ())), preferred_element_type=F32)
            st_new = st_new + jnp.where(head == j, st, 0.0)
        y_off = jnp.dot(cm.astype(BF16), h_in[g].astype(BF16), preferred_element_type=F32) * e_cum
        hs.append(h_in[g] * c_dec + st_new)
        y = y_diag + y_off + d_exp * xs
        yz = y * _silu(z[:, gw * g:gw * (g + 1)])
        yz = yz * lax.rsqrt(jnp.mean(yz * yz, axis=-1, keepdims=True) + NORM_EPS)
        ys.append(yz * norm_g[:, gw * g:gw * (g + 1)])
    return jnp.concatenate(ys, axis=1), tuple(hs)


_SSD_NCHUNK = SEQ // SSD_CHUNK
_HALO_PER_CHUNK = SSD_CHUNK // CONV_HALO


def _ssd_param_specs(const):
    return [pl.BlockSpec((8, SSD_CONV_CH), const), pl.BlockSpec((1, SSD_CONV_CH), const),
            pl.BlockSpec((1, 128), const), pl.BlockSpec((1, 128), const), pl.BlockSpec((1, 128), const),
            pl.BlockSpec((1, SSD_INNER), const)]


def _ssd_fwd(proj, conv_w, conv_b, dt_bias, a_log, d_skip, norm_g, *, name):
    q = SSD_CHUNK

    def body(z_ref, xbc_ref, halo_ref, dt_ref, cw_ref, cb_ref, db_ref, al_ref, d_ref, ng_ref, y_ref, hs_ref, h_acc):
        i = pl.program_id(0)

        @pl.when(i == 0)
        def _():
            h_acc[...] = jnp.zeros_like(h_acc)

        halo = jnp.where(i == 0, 0.0, halo_ref[...])
        xbc_ext = jnp.concatenate([halo, xbc_ref[...]], axis=0)
        h_in = (h_acc[0], h_acc[1])
        hs_ref[0, 0] = h_in[0]
        hs_ref[0, 1] = h_in[1]
        y, h_out = _ssd_chunk(z_ref[...], xbc_ext, dt_ref[...], cw_ref[...], cb_ref[...], db_ref[...], al_ref[...],
                              d_ref[...], ng_ref[...], h_in)
        y_ref[...] = y.astype(BF16)
        h_acc[0] = h_out[0]
        h_acc[1] = h_out[1]

    const = lambda i: (0, 0)
    return pl.pallas_call(
        body, grid=(_SSD_NCHUNK,),
        in_specs=[pl.BlockSpec((q, SSD_INNER), lambda i: (i, PROJ_Z_BLK)),
                  pl.BlockSpec((q, SSD_CONV_CH), lambda i: (i, 0)),
                  pl.BlockSpec((CONV_HALO, SSD_CONV_CH), lambda i: (jnp.maximum(i * _HALO_PER_CHUNK - 1, 0), 0)),
                  pl.BlockSpec((q, 128), lambda i: (i, PROJ_DT_BLK))] + _ssd_param_specs(const),
        out_specs=[pl.BlockSpec((q, SSD_INNER), lambda i: (i, 0)),
                   pl.BlockSpec((1, 2, SSD_STATE, 256), lambda i: (i, 0, 0, 0))],
        out_shape=[jax.ShapeDtypeStruct((SEQ, D_MODEL), BF16),
                   jax.ShapeDtypeStruct((_SSD_NCHUNK, 2, SSD_STATE, 256), F32)],
        scratch_shapes=[pltpu.VMEM((2, SSD_STATE, 256), F32)], name=name,
        compiler_params=_cparams("arbitrary"))(proj, proj, proj, proj, conv_w, conv_b, dt_bias, a_log, d_skip, norm_g)


def _ssd_bwd(proj, hstates, dmix, conv_w, conv_b, dt_bias, a_log, d_skip, norm_g, *, name):
    q = SSD_CHUNK
    last = _SSD_NCHUNK - 1

    def body(z_ref, xbc_ref, halo_ref, dt_ref, hs_ref, dy_ref, cw_ref, cb_ref, db_ref, al_ref, d_ref, ng_ref,
             dp_ref, dcw_ref, dcb_ref, ddb_ref, dal_ref, dd_ref, dng_ref, dh_acc, dhalo_acc):
        i = pl.program_id(0)

        @pl.when(i == 0)
        def _():
            dh_acc[...] = jnp.zeros_like(dh_acc)
            dhalo_acc[...] = jnp.zeros_like(dhalo_acc)
            for r in (dcw_ref, dcb_ref, ddb_ref, dal_ref, dd_ref, dng_ref):
                r[...] = jnp.zeros_like(r)

        halo = jnp.where(i == last, 0.0, halo_ref[...])
        xbc_ext = jnp.concatenate([halo, xbc_ref[...]], axis=0)
        _, vjp = jax.vjp(_ssd_chunk, z_ref[...], xbc_ext, dt_ref[...], cw_ref[...], cb_ref[...], db_ref[...],
                         al_ref[...], d_ref[...], ng_ref[...], (hs_ref[0, 0], hs_ref[0, 1]))
        gz, gx, gdt, gcw, gcb, gdb, gal, gd, gng, gh = vjp((dy_ref[...], (dh_acc[0], dh_acc[1])))
        dxbc = jnp.concatenate([gx[CONV_HALO:q], gx[q:] + dhalo_acc[...]], axis=0)
        dp_ref[...] = jnp.concatenate([dxbc, gz, gdt, jnp.zeros_like(gdt)], axis=1).astype(BF16)
        dhalo_acc[...] = gx[:CONV_HALO]
        dh_acc[0] = gh[0]
        dh_acc[1] = gh[1]
        dcw_ref[...] += gcw
        dcb_ref[...] += gcb
        ddb_ref[...] += gdb
        dal_ref[...] += gal
        dd_ref[...] += gd
        dng_ref[...] += gng

    const = lambda i: (0, 0)
    rev = lambda i: last - i
    row = lambda n: jax.ShapeDtypeStruct((1, n), F32)
    return pl.pallas_call(
        body, grid=(_SSD_NCHUNK,),
        in_specs=[pl.BlockSpec((q, SSD_INNER), lambda i: (rev(i), PROJ_Z_BLK)),
                  pl.BlockSpec((q, SSD_CONV_CH), lambda i: (rev(i), 0)),
                  pl.BlockSpec((CONV_HALO, SSD_CONV_CH), lambda i: (jnp.maximum(rev(i) * _HALO_PER_CHUNK - 1, 0), 0)),
                  pl.BlockSpec((q, 128), lambda i: (rev(i), PROJ_DT_BLK)),
                  pl.BlockSpec((1, 2, SSD_STATE, 256), lambda i: (rev(i), 0, 0, 0)),
                  pl.BlockSpec((q, SSD_INNER), lambda i: (rev(i), 0))] + _ssd_param_specs(const),
        out_specs=[pl.BlockSpec((q, PROJ_SSD_W), lambda i: (rev(i), 0))] + _ssd_param_specs(const),
        out_shape=[jax.ShapeDtypeStruct((SEQ, PROJ_W), BF16), jax.ShapeDtypeStruct((8, SSD_CONV_CH), F32),
                   row(SSD_CONV_CH), row(128), row(128), row(128), row(SSD_INNER)],
        scratch_shapes=[pltpu.VMEM((2, SSD_STATE, 256), F32), pltpu.VMEM((CONV_HALO, SSD_CONV_CH), F32)], name=name,
        compiler_params=_cparams("arbitrary"))(proj, proj, proj, proj, hstates, dmix, conv_w, conv_b, dt_bias, a_log,
                                                d_skip, norm_g)


def _rope_tables(pos_col, inv_freq_lane, *, name):
    s = pos_col.shape[0]

    def body(p_ref, f_ref, c_ref, s1_ref, s2_ref):
        ang = p_ref[...] * f_ref[...]
        within = lax.broadcasted_iota(jnp.int32, ang.shape, 1) % HEAD_LANES
        half = ROT_DIM // 2
        c_ref[...] = jnp.where(within < ROT_DIM, jnp.cos(ang), 1.0)
        sn = jnp.sin(ang)
        s1_ref[...] = jnp.where(within < half, -sn, 0.0)
        s2_ref[...] = jnp.where((within >= half) & (within < ROT_DIM), sn, 0.0)

    shp = jax.ShapeDtypeStruct((s, 128), F32)
    return pl.pallas_call(body, out_shape=[shp, shp, shp], name=name,
                          compiler_params=pltpu.CompilerParams(vmem_limit_bytes=VMEM_LIMIT))(pos_col, inv_freq_lane)


def _rope(t, c, s1, s2):
    half = ROT_DIM // 2
    return t * c + pltpu.roll(t, 128 - half, 1) * s1 + pltpu.roll(t, half, 1) * s2


def _rope_t(g, c, s1, s2):
    half = ROT_DIM // 2
    return g * c + pltpu.roll(g * s1, half, 1) + pltpu.roll(g * s2, 128 - half, 1)


def _att_valid(b):
    qi = lax.broadcasted_iota(jnp.int32, (ATT_BLOCK, 2 * ATT_BLOCK), 0)
    kj = lax.broadcasted_iota(jnp.int32, (ATT_BLOCK, 2 * ATT_BLOCK), 1)
    rel = qi + ATT_BLOCK - kj
    return (rel >= 0) & (rel <= ATT_BLOCK) & (b * ATT_BLOCK + kj - ATT_BLOCK >= 0)


def _att_slices(i, d):
    if d == 1:
        qstart = pl.multiple_of(i * ATT_BLOCK, ATT_BLOCK)
        return i, pl.ds(qstart, ATT_BLOCK), pl.ds(pl.multiple_of(qstart - ATT_BLOCK + ATT_KPAD, ATT_BLOCK), 2 * ATT_BLOCK)
    r = i % d
    b = i // d
    qstart = r + d * ATT_BLOCK * b
    return b, pl.ds(qstart, ATT_BLOCK, stride=d), pl.ds(qstart - ATT_BLOCK * d + ATT_KPAD, 2 * ATT_BLOCK, stride=d)


_ATT_NBLK = SEQ // ATT_BLOCK
_ATT_SCALE = HEAD_LANES ** -0.5
_ATT_UNROLL = 4


def _att_fwd(proj, cos, sin1, sin2, mix, *, name):
    s = SEQ

    def body(q_ref, k_ref, v_ref, c_ref, s1_ref, s2_ref, _, o_ref, lse_ref, mix_ref, qs, ks, vs, acc, m_s, l_s):
        c, s1, s2 = c_ref[...], s1_ref[...], s2_ref[...]
        qs[...] = _rope(q_ref[...], c, s1, s2) * _ATT_SCALE
        zeros = jnp.zeros((ATT_KPAD, 128), F32)
        ks[pl.ds(0, ATT_KPAD), :] = zeros
        vs[pl.ds(0, ATT_KPAD), :] = zeros
        ks[pl.ds(ATT_KPAD, s), :] = _rope(k_ref[...], c, s1, s2)
        vs[pl.ds(ATT_KPAD, s), :] = v_ref[...]
        head0 = _head_of_lane(128) == 0

        for bi, (_, d) in enumerate(ATT_PATTERNS):
            def blk(i, carry, d=d, first=(bi == 0)):
                b, sq, sk = _att_slices(i, d)
                qb = qs[sq, :]
                kw = ks[sk, :].astype(BF16)
                vw = vs[sk, :].astype(BF16)
                valid = _att_valid(b)
                ms, ls, os_ = [], [], []
                for hh in range(2):
                    qh = jnp.where(head0 if hh == 0 else ~head0, qb, 0.0).astype(BF16)
                    sc = lax.dot_general(qh, kw, (((1,), (1,)), ((), ())), preferred_element_type=F32)
                    sc = jnp.where(valid, sc, -jnp.inf)
                    mb = jnp.max(sc, axis=1, keepdims=True)
                    p = jnp.exp(sc - mb)
                    ms.append(mb)
                    ls.append(jnp.sum(p, axis=1, keepdims=True))
                    os_.append(jnp.dot(p.astype(BF16), vw, preferred_element_type=F32))
                m_b = jnp.where(head0, ms[0], ms[1])
                l_b = jnp.where(head0, ls[0], ls[1])
                o_b = jnp.where(head0, os_[0], os_[1])
                if first:
                    m_s[sq, :] = m_b
                    l_s[sq, :] = l_b
                    acc[sq, :] = o_b
                else:
                    m_old = m_s[sq, :]
                    m_new = jnp.maximum(m_old, m_b)
                    a_old = jnp.exp(m_old - m_new)
                    a_b = jnp.exp(m_b - m_new)
                    m_s[sq, :] = m_new
                    l_s[sq, :] = l_s[sq, :] * a_old + l_b * a_b
                    acc[sq, :] = acc[sq, :] * a_old + o_b * a_b
                return carry

            lax.fori_loop(0, _ATT_NBLK, blk, 0, unroll=_ATT_UNROLL)

        out = acc[...] / l_s[...]
        o_ref[...] = out
        mix_ref[...] = out.astype(BF16)
        lse_ref[...] = m_s[...] + jnp.log(l_s[...])

    col = lambda base: pl.BlockSpec((s, 128), lambda p: (0, base + p))
    tab = pl.BlockSpec((s, 128), lambda p: (0, 0))
    big = pltpu.VMEM((ATT_KPAD + s, 128), F32)
    tok = pltpu.VMEM((s, 128), F32)
    return pl.pallas_call(
        body, grid=(2,), in_specs=[col(PROJ_Q_BLK), col(PROJ_K_BLK), col(PROJ_V_BLK), tab, tab, tab, _HBM],
        out_specs=[pl.BlockSpec((s, 128), lambda p: (0, p)), pl.BlockSpec((s, 128), lambda p: (0, p)),
                   col(MIX_ATT_BLK)],
        out_shape=[jax.ShapeDtypeStruct((s, ATT_W), F32), jax.ShapeDtypeStruct((s, ATT_W), F32),
                   jax.ShapeDtypeStruct(mix.shape, mix.dtype)],
        input_output_aliases={6: 2}, scratch_shapes=[tok, big, big, tok, tok, tok], name=name,
        compiler_params=_cparams("arbitrary"))(proj, proj, proj, cos, sin1, sin2, mix)


def _att_bwd(proj, cos, sin1, sin2, out, lse, dmix, dproj, *, name):
    s = SEQ

    def body(proj_ref, c_hbm, s1_hbm, s2_hbm, out_hbm, lse_hbm, dmix_hbm, _, dproj_hbm,
             c_ref, s1_ref, s2_ref, o_ref, lse_ref, do_ref, qs, ks, vs, dqs, dks, dvs, staged, sems):
        def start(copies):
            for cp in copies:
                cp.start()
            return copies

        def load(pair):
            lanes = pl.ds(128 * pair, 128)
            rows = pl.ds(ATT_KPAD, s)
            return start([
                pltpu.make_async_copy(proj_ref.at[:, pl.ds(128 * (PROJ_Q_BLK + pair), 128)], qs, sems.at[0]),
                pltpu.make_async_copy(proj_ref.at[:, pl.ds(128 * (PROJ_K_BLK + pair), 128)], ks.at[rows, :], sems.at[1]),
                pltpu.make_async_copy(proj_ref.at[:, pl.ds(128 * (PROJ_V_BLK + pair), 128)], vs.at[rows, :], sems.at[2]),
                pltpu.make_async_copy(out_hbm.at[:, lanes], o_ref, sems.at[3]),
                pltpu.make_async_copy(lse_hbm.at[:, lanes], lse_ref, sems.at[4]),
                pltpu.make_async_copy(dmix_hbm.at[:, pl.ds(128 * (MIX_ATT_BLK + pair), 128)], do_ref, sems.at[5])])

        tables = start([pltpu.make_async_copy(c_hbm, c_ref, sems.at[6]),
                        pltpu.make_async_copy(s1_hbm, s1_ref, sems.at[7]),
                        pltpu.make_async_copy(s2_hbm, s2_ref, sems.at[8])])
        loads = load(0)
        for cp in tables:
            cp.wait()
        head0 = _head_of_lane(128) == 0
        zeros = jnp.zeros((ATT_KPAD, 128), F32)
        for pair in range(2):
            for cp in loads:
                cp.wait()
            c, s1, s2 = c_ref[...], s1_ref[...], s2_ref[...]
            qs[...] = _rope(qs[...], c, s1, s2) * _ATT_SCALE
            ks[pl.ds(0, ATT_KPAD), :] = zeros
            vs[pl.ds(0, ATT_KPAD), :] = zeros
            ks[pl.ds(ATT_KPAD, s), :] = _rope(ks[pl.ds(ATT_KPAD, s), :], c, s1, s2)
            dqs[...] = jnp.zeros_like(dqs)
            dks[...] = jnp.zeros_like(dks)
            dvs[...] = jnp.zeros_like(dvs)

            for _, d in ATT_PATTERNS:
                def blk(i, carry, d=d):
                    b, sq, sk = _att_slices(i, d)
                    qb = qs[sq, :]
                    kw = ks[sk, :].astype(BF16)
                    vw = vs[sk, :].astype(BF16)
                    dob = do_ref[sq, :]
                    lse_b = lse_ref[sq, :]
                    dd = dob * o_ref[sq, :]
                    valid = _att_valid(b)
                    dq_b = jnp.zeros((ATT_BLOCK, 128), F32)
                    dk_w = jnp.zeros((2 * ATT_BLOCK, 128), F32)
                    dv_w = jnp.zeros((2 * ATT_BLOCK, 128), F32)
                    for hh in range(2):
                        hm = head0 if hh == 0 else ~head0
                        qh = jnp.where(hm, qb, 0.0).astype(BF16)
                        doh = jnp.where(hm, dob, 0.0).astype(BF16)
                        lse_h = _pick_lane(lse_b, hh * HEAD_LANES)
                        d_h = jnp.sum(jnp.where(hm, dd, 0.0), axis=1, keepdims=True)
                        sc = lax.dot_general(qh, kw, (((1,), (1,)), ((), ())), preferred_element_type=F32)
                        p = jnp.where(valid, jnp.exp(sc - lse_h), 0.0)
                        dp = lax.dot_general(doh, vw, (((1,), (1,)), ((), ())), preferred_element_type=F32)
                        ds = (p * (dp - d_h)).astype(BF16)
                        dq_b = dq_b + jnp.where(hm, jnp.dot(ds, kw, preferred_element_type=F32), 0.0)
                        dk_w = dk_w + lax.dot_general(ds, qh, (((0,), (0,)), ((), ())), preferred_element_type=F32)
                        dv_w = dv_w + lax.dot_general(p.astype(BF16), doh, (((0,), (0,)), ((), ())),
                                                      preferred_element_type=F32)
                    dqs[sq, :] += dq_b
                    dks[sk, :] += dk_w
                    dvs[sk, :] += dv_w
                    return carry

                lax.fori_loop(0, _ATT_NBLK, blk, 0, unroll=_ATT_UNROLL)

            staged[0] = _rope_t(dqs[...] * _ATT_SCALE, c, s1, s2).astype(BF16)
            staged[1] = _rope_t(dks[pl.ds(ATT_KPAD, s), :], c, s1, s2).astype(BF16)
            staged[2] = dvs[pl.ds(ATT_KPAD, s), :].astype(BF16)
            stores = start([
                pltpu.make_async_copy(staged.at[j], dproj_hbm.at[:, pl.ds(128 * (col + pair), 128)], sems.at[9 + j])
                for j, col in enumerate((PROJ_Q_BLK, PROJ_K_BLK, PROJ_V_BLK))])
            if pair == 0:
                loads = load(1)
            for cp in stores:
                cp.wait()

    big = pltpu.VMEM((ATT_KPAD + s, 128), F32)
    tok = pltpu.VMEM((s, 128), F32)
    return pl.pallas_call(
        body, in_specs=[_HBM] * 8, out_specs=_HBM, out_shape=jax.ShapeDtypeStruct(dproj.shape, dproj.dtype),
        input_output_aliases={7: 0},
        scratch_shapes=[tok] * 6 + [tok, big, big, tok, big, big, pltpu.VMEM((3, s, 128), BF16),
                                    pltpu.SemaphoreType.DMA((12,))], name=name,
        compiler_params=pltpu.CompilerParams(vmem_limit_bytes=VMEM_LIMIT))(
            proj, cos, sin1, sin2, out, lse, dmix, dproj)


_POOL_TM = 512
_POOL_NT = SEQ // _POOL_TM
_POOL_HALO_PER_TILE = _POOL_TM // POOL_HALO


def _pool_tile(u_ext, w_bd, scale, t0):
    s2 = u_ext + _roll_rows(u_ext, 1)
    s4 = s2 + _roll_rows(s2, 2)
    s8 = s4 + _roll_rows(s4, 4)
    s16 = s8 + _roll_rows(s8, 8)
    grp = _head_of_lane(POOL_W)
    sel = jnp.where(grp == 0, s2, jnp.where(grp == 1, s4, jnp.where(grp == 2, s8, s16)))[POOL_HALO:]
    t = sel.shape[0]
    pos = t0 + lax.broadcasted_iota(jnp.int32, (t, POOL_W), 0) + 1
    win = jnp.where(grp == 0, 2, jnp.where(grp == 1, 4, jnp.where(grp == 2, 8, 16)))
    cnt = jnp.minimum(pos, win).astype(F32)
    diff = sel / cnt - u_ext[POOL_HALO:]
    return jnp.dot(diff.astype(BF16), w_bd.astype(BF16), preferred_element_type=F32) * scale


def _pool_fwd(proj, w_bd, scale, mix, *, name):
    tm = _POOL_TM

    def body(u_ref, halo_ref, w_ref, sc_ref, _, y_ref):
        i = pl.program_id(0)
        halo = jnp.where(i == 0, 0.0, halo_ref[...])
        u_ext = jnp.concatenate([halo, u_ref[...]], axis=0)
        y_ref[...] = _pool_tile(u_ext, w_ref[...], sc_ref[...], i * tm).astype(BF16)

    return pl.pallas_call(
        body, grid=(_POOL_NT,),
        in_specs=[pl.BlockSpec((tm, POOL_W), lambda i: (i, PROJ_POOL_BLK)),
                  pl.BlockSpec((POOL_HALO, POOL_W),
                               lambda i: (jnp.maximum(i * _POOL_HALO_PER_TILE - 1, 0), PROJ_POOL_BLK)),
                  pl.BlockSpec((POOL_W, POOL_W), lambda i: (0, 0)), pl.BlockSpec((1, POOL_W), lambda i: (0, 0)), _HBM],
        out_specs=pl.BlockSpec((tm, POOL_W), lambda i: (i, MIX_POOL_BLK)),
        out_shape=jax.ShapeDtypeStruct(mix.shape, mix.dtype), input_output_aliases={4: 0}, name=name,
        compiler_params=_cparams("parallel"))(proj, proj, w_bd, scale, mix)


def _pool_bwd(proj, dmix, w_bd, scale, dproj, *, name):
    tm = _POOL_TM
    last = _POOL_NT - 1

    def body(u_ref, halo_ref, dy_ref, w_ref, sc_ref, _, du_ref, dw_ref, dsc_ref, dhalo_acc):
        i = pl.program_id(0)

        @pl.when(i == 0)
        def _():
            dhalo_acc[...] = jnp.zeros_like(dhalo_acc)
            dw_ref[...] = jnp.zeros_like(dw_ref)
            dsc_ref[...] = jnp.zeros_like(dsc_ref)

        tile = last - i
        halo = jnp.where(tile == 0, 0.0, halo_ref[...])
        u_ext = jnp.concatenate([halo, u_ref[...]], axis=0)
        _, vjp = jax.vjp(functools.partial(_pool_tile, t0=tile * tm), u_ext, w_ref[...], sc_ref[...])
        gu, gw, gs = vjp(dy_ref[...])
        du_ref[...] = jnp.concatenate([gu[POOL_HALO:tm], gu[tm:] + dhalo_acc[...]], axis=0).astype(BF16)
        dhalo_acc[...] = gu[:POOL_HALO]
        dw_ref[...] += gw
        dsc_ref[...] += gs

    rev = lambda i: last - i
    return pl.pallas_call(
        body, grid=(_POOL_NT,),
        in_specs=[pl.BlockSpec((tm, POOL_W), lambda i: (rev(i), PROJ_POOL_BLK)),
                  pl.BlockSpec((POOL_HALO, POOL_W),
                               lambda i: (jnp.maximum(rev(i) * _POOL_HALO_PER_TILE - 1, 0), PROJ_POOL_BLK)),
                  pl.BlockSpec((tm, POOL_W), lambda i: (rev(i), MIX_POOL_BLK)),
                  pl.BlockSpec((POOL_W, POOL_W), lambda i: (0, 0)), pl.BlockSpec((1, POOL_W), lambda i: (0, 0)), _HBM],
        out_specs=[pl.BlockSpec((tm, POOL_W), lambda i: (rev(i), PROJ_POOL_BLK)),
                   pl.BlockSpec((POOL_W, POOL_W), lambda i: (0, 0)), pl.BlockSpec((1, POOL_W), lambda i: (0, 0))],
        out_shape=[jax.ShapeDtypeStruct(dproj.shape, dproj.dtype), jax.ShapeDtypeStruct((POOL_W, POOL_W), F32),
                   jax.ShapeDtypeStruct((1, POOL_W), F32)],
        input_output_aliases={5: 0}, scratch_shapes=[pltpu.VMEM((POOL_HALO, POOL_W), F32)], name=name,
        compiler_params=_cparams("arbitrary"))(proj, proj, dmix, w_bd, scale, dproj)


_FFN_TM = 256
_FFN_NT = SEQ // _FFN_TM
_FFN_HALO_PER_TILE = _FFN_TM // CONV_HALO


def _ffn_act_tile(hid_ext, conv_w, conv_b):
    hc = conv_b
    for k in range(FFN_CONV_K):
        hc = hc + _pick_row(conv_w, k) * _shift_rows(hid_ext, FFN_CONV_K - 1 - k, CONV_HALO)
    return _silu(hc[:, :FFN_DIM]) * hc[:, FFN_DIM:]


def _ffn_act(hid, conv_w, conv_b, *, name):
    tm = _FFN_TM
    w = 2 * FFN_DIM

    def body(h_ref, halo_ref, cw_ref, cb_ref, a_ref):
        i = pl.program_id(0)
        halo = jnp.where(i == 0, 0.0, halo_ref[...])
        hid_ext = jnp.concatenate([halo, h_ref[...]], axis=0)
        a_ref[...] = _ffn_act_tile(hid_ext, cw_ref[...], cb_ref[...]).astype(BF16)

    return pl.pallas_call(
        body, grid=(_FFN_NT,),
        in_specs=[pl.BlockSpec((tm, w), lambda i: (i, 0)),
                  pl.BlockSpec((CONV_HALO, w), lambda i: (jnp.maximum(i * _FFN_HALO_PER_TILE - 1, 0), 0)),
                  pl.BlockSpec((8, w), lambda i: (0, 0)), pl.BlockSpec((1, w), lambda i: (0, 0))],
        out_specs=pl.BlockSpec((tm, FFN_DIM), lambda i: (i, 0)),
        out_shape=jax.ShapeDtypeStruct((SEQ, FFN_DIM), BF16), name=name,
        compiler_params=_cparams("parallel"))(hid, hid, conv_w, conv_b)


def _ffn_act_bwd(hid, dact, conv_w, conv_b, *, name):
    tm = _FFN_TM
    w = 2 * FFN_DIM
    last = _FFN_NT - 1

    def body(h_ref, halo_ref, da_ref, cw_ref, cb_ref, dh_ref, dcw_ref, dcb_ref, dhalo_acc):
        i = pl.program_id(0)

        @pl.when(i == 0)
        def _():
            dhalo_acc[...] = jnp.zeros_like(dhalo_acc)
            dcw_ref[...] = jnp.zeros_like(dcw_ref)
            dcb_ref[...] = jnp.zeros_like(dcb_ref)

        halo = jnp.where(i == last, 0.0, halo_ref[...])
        hid_ext = jnp.concatenate([halo, h_ref[...]], axis=0)
        _, vjp = jax.vjp(_ffn_act_tile, hid_ext, cw_ref[...], cb_ref[...])
        gh, gw, gb = vjp(da_ref[...])
        carry = dhalo_acc[...]
        dhalo_acc[...] = gh[:CONV_HALO]
        dh_ref[...] = jnp.concatenate([gh[CONV_HALO:tm], gh[tm:] + carry], axis=0).astype(BF16)
        dcw_ref[...] += gw
        dcb_ref[...] += gb

    rev = lambda i: last - i
    return pl.pallas_call(
        body, grid=(_FFN_NT,),
        in_specs=[pl.BlockSpec((tm, w), lambda i: (rev(i), 0)),
                  pl.BlockSpec((CONV_HALO, w), lambda i: (jnp.maximum(rev(i) * _FFN_HALO_PER_TILE - 1, 0), 0)),
                  pl.BlockSpec((tm, FFN_DIM), lambda i: (rev(i), 0)),
                  pl.BlockSpec((8, w), lambda i: (0, 0)), pl.BlockSpec((1, w), lambda i: (0, 0))],
        out_specs=[pl.BlockSpec((tm, w), lambda i: (rev(i), 0)),
                   pl.BlockSpec((8, w), lambda i: (0, 0)), pl.BlockSpec((1, w), lambda i: (0, 0))],
        out_shape=[jax.ShapeDtypeStruct((SEQ, w), BF16), jax.ShapeDtypeStruct((8, w), F32),
                   jax.ShapeDtypeStruct((1, w), F32)],
        scratch_shapes=[pltpu.VMEM((CONV_HALO, w), F32)], name=name,
        compiler_params=_cparams("arbitrary"))(hid, hid, dact, conv_w, conv_b)


def _axes():
    return lax.axis_index("x"), lax.axis_index("y"), lax.axis_index("c")


def _handshake(peers):
    barrier = pltpu.get_barrier_semaphore()
    for peer in peers:
        pl.semaphore_signal(barrier, inc=1, device_id=peer, device_id_type=MESH)
    pl.semaphore_wait(barrier, len(peers))


def _allgather_body(x_refs, out_refs, send_sems, recv_sems, local_sems, own_barrier):
    n = len(x_refs)
    x, y, c = _axes()
    me, sibling = (x, y, c), (x, y, 1 - c)
    chips = [(1 - x, y), (x, 1 - y), (1 - x, 1 - y)]
    if own_barrier:
        _handshake([sibling] + [(*chip, c) for chip in chips])

    def slot(a, px, py, pc):
        return out_refs[a].at[4 * px + 2 * py + pc]

    def copy(a, k, block, to, src=None):
        return pltpu.make_async_remote_copy(
            src_ref=slot(a, *block) if src is None else src, dst_ref=slot(a, *block),
            send_sem=send_sems.at[a, k], recv_sem=recv_sems.at[a, k], device_id=to, device_id_type=MESH)

    mines, firsts = [], []
    for a in range(n):
        mines.append(pltpu.make_async_copy(x_refs[a], slot(a, *me), local_sems.at[a]))
        mines[-1].start()
        first = [copy(a, 0, me, sibling, src=x_refs[a])]
        first += [copy(a, 1 + j, me, (*chip, c), src=x_refs[a]) for j, chip in enumerate(chips)]
        for cp in first:
            cp.start()
        firsts += first
    passed = []
    for j, chip in enumerate(chips):
        for a in range(n):
            copy(a, 1 + j, (*chip, c), me).wait_recv()
            passed.append(copy(a, 4 + j, (*chip, c), sibling))
            passed[-1].start()
    for a in range(n):
        copy(a, 0, sibling, me).wait_recv()
    for j, chip in enumerate(chips):
        for a in range(n):
            copy(a, 4 + j, (*chip, 1 - c), me).wait_recv()
    for cp in firsts + passed:
        cp.wait_send()
    for cp in mines:
        cp.wait()


def _allgather_sems(n):
    return [pltpu.SemaphoreType.DMA((n, 7)), pltpu.SemaphoreType.DMA((n, 7)), pltpu.SemaphoreType.DMA((n,))]


def _allgather(xs, *, name):
    n = len(xs)

    def body(*refs):
        _allgather_body(refs[:n], refs[n:2 * n], *refs[2 * n:], own_barrier=False)

    return pl.pallas_call(
        body, out_shape=[jax.ShapeDtypeStruct((N_DEV,) + xb.shape, xb.dtype) for xb in xs],
        in_specs=[_HBM] * n, out_specs=[_HBM] * n, scratch_shapes=_allgather_sems(n), name=name)(*xs)


def _allgather_async(xs, *, name, collective_id):
    n = len(xs)
    x_refs = [jax.new_ref(xb, memory_space=pltpu.MemorySpace.HBM) for xb in xs]
    out_refs = [jax.empty_ref(jax.ShapeDtypeStruct((N_DEV,) + xb.shape, xb.dtype), memory_space=pltpu.MemorySpace.HBM)
                for xb in xs]

    @pl.kernel(mesh=plsc.ScalarSubcoreMesh(axis_name="sequencer", num_cores=1), name=name,
               scratch_types=tuple(_allgather_sems(n)),
               compiler_params=pltpu.CompilerParams(collective_id=collective_id))
    def launch(send_sems, recv_sems, local_sems):
        _allgather_body(x_refs, out_refs, send_sems, recv_sems, local_sems, own_barrier=True)

    launch()
    return [r[...] for r in out_refs]


def _pair_exchange(blocks, *, name, collective_id):
    n = len(blocks)
    hbm = pltpu.MemorySpace.HBM
    in_refs = [jax.new_ref(b, memory_space=hbm) for b in blocks]
    out_refs = [jax.empty_ref(jax.ShapeDtypeStruct((4,) + b.shape[1:], b.dtype), memory_space=hbm) for b in blocks]

    @pl.kernel(mesh=plsc.ScalarSubcoreMesh(axis_name="sequencer", num_cores=1), name=name,
               scratch_types=(pltpu.SemaphoreType.DMA((n, 4)), pltpu.SemaphoreType.DMA((n, 4))),
               compiler_params=pltpu.CompilerParams(collective_id=collective_id))
    def launch(send_sems, recv_sems):
        x, y, c = _axes()
        _handshake([(x, y, 1 - c)])
        copies = [pltpu.make_async_remote_copy(
            src_ref=in_refs[a].at[2 * s + (1 - c)], dst_ref=out_refs[a].at[s], send_sem=send_sems.at[a, s],
            recv_sem=recv_sems.at[a, s], device_id=(x, y, 1 - c), device_id_type=MESH)
            for a in range(n) for s in range(4)]
        for cp in copies:
            cp.start()
        for cp in copies:
            cp.wait_recv()
        for cp in copies:
            cp.wait_send()

    launch()
    return [r[...] for r in out_refs]


def _chip_exchange(parts, *, name, collective_id):
    n = len(parts)
    hbm = pltpu.MemorySpace.HBM
    in_refs = [jax.new_ref(p, memory_space=hbm) for p in parts]
    out_refs = [jax.empty_ref(jax.ShapeDtypeStruct(p.shape, p.dtype), memory_space=hbm) for p in parts]

    @pl.kernel(mesh=plsc.ScalarSubcoreMesh(axis_name="sequencer", num_cores=1), name=name,
               scratch_types=(pltpu.SemaphoreType.DMA((n, 3)), pltpu.SemaphoreType.DMA((n, 3)),
                              pltpu.SemaphoreType.DMA((n,))),
               compiler_params=pltpu.CompilerParams(collective_id=collective_id))
    def launch(send_sems, recv_sems, local_sems):
        x, y, c = _axes()
        my_chip = 2 * x + y
        chips = [(1 - x, y), (x, 1 - y), (1 - x, 1 - y)]
        _handshake([(*chip, c) for chip in chips])
        locals_ = [pltpu.make_async_copy(in_refs[a].at[my_chip], out_refs[a].at[my_chip], local_sems.at[a])
                   for a in range(n)]
        for cp in locals_:
            cp.start()
        copies = [pltpu.make_async_remote_copy(
            src_ref=in_refs[a].at[2 * px + py], dst_ref=out_refs[a].at[my_chip], send_sem=send_sems.at[a, k],
            recv_sem=recv_sems.at[a, k], device_id=(px, py, c), device_id_type=MESH)
            for a in range(n) for k, (px, py) in enumerate(chips)]
        for cp in copies:
            cp.start()
        for cp in copies:
            cp.wait_recv()
        for cp in copies:
            cp.wait_send()
        for cp in locals_:
            cp.wait()

    launch()
    return [r[...] for r in out_refs]


def _pair_sum(core, blocks, from_sibling, *, name):
    _, r, cdim = blocks.shape

    def body(core_ref, a_ref, b_ref, o_ref):
        o_ref[...] = (a_ref[...].astype(F32) + b_ref[...].astype(F32)).astype(o_ref.dtype)

    return pl.pallas_call(
        body,
        grid_spec=pltpu.PrefetchScalarGridSpec(
            num_scalar_prefetch=1, grid=(4,),
            in_specs=[pl.BlockSpec((1, r, cdim), lambda s, core_ref: (2 * s + core_ref[0], 0, 0)),
                      pl.BlockSpec((1, r, cdim), lambda s, core_ref: (s, 0, 0))],
            out_specs=pl.BlockSpec((1, r, cdim), lambda s, core_ref: (s, 0, 0))),
        out_shape=jax.ShapeDtypeStruct(from_sibling.shape, from_sibling.dtype), name=name,
        compiler_params=_cparams("parallel"))(core, blocks, from_sibling)


def _sum_blocks(a, *, name, tr=None):
    n, r, cdim = a.shape
    tr = tr or r

    def body(a_ref, o_ref):
        acc = a_ref[0].astype(F32)
        for k in range(1, n):
            acc = acc + a_ref[k].astype(F32)
        o_ref[...] = acc

    return pl.pallas_call(body, grid=(r // tr,), in_specs=[pl.BlockSpec((n, tr, cdim), lambda i: (0, i, 0))],
                          out_specs=pl.BlockSpec((tr, cdim), lambda i: (i, 0)),
                          out_shape=jax.ShapeDtypeStruct((r, cdim), F32), name=name,
                          compiler_params=_cparams("parallel"))(a)


_ADA_SHARD = 6 * D_MODEL // N_DEV


def _ada_mod(c_all, ada_w, *, name):
    def body(c_ref, w_ref, o_ref):
        o_ref[0] = jnp.dot(_silu(c_ref[...]).astype(BF16), w_ref[0].astype(BF16), preferred_element_type=F32)

    return pl.pallas_call(
        body, grid=(DEPTH,),
        in_specs=[pl.BlockSpec((N_DEV, D_MODEL), lambda l: (0, 0)),
                  pl.BlockSpec((1, D_MODEL, _ADA_SHARD), lambda l: (l, 0, 0))],
        out_specs=pl.BlockSpec((1, N_DEV, _ADA_SHARD), lambda l: (l, 0, 0)),
        out_shape=jax.ShapeDtypeStruct((DEPTH, N_DEV, _ADA_SHARD), F32), name=name,
        compiler_params=_cparams("parallel"))(c_all, ada_w)


def _ada_wgrad(c_all, dmod_cols, *, name):
    def body(c_ref, d_ref, o_ref):
        o_ref[0] = lax.dot_general(_silu(c_ref[...]), d_ref[0], (((0,), (0,)), ((), ())),
                                   preferred_element_type=F32, precision=lax.Precision.HIGHEST)

    return pl.pallas_call(
        body, grid=(DEPTH,),
        in_specs=[pl.BlockSpec((N_DEV, D_MODEL), lambda l: (0, 0)),
                  pl.BlockSpec((1, N_DEV, _ADA_SHARD), lambda l: (l, 0, 0))],
        out_specs=pl.BlockSpec((1, D_MODEL, _ADA_SHARD), lambda l: (l, 0, 0)),
        out_shape=jax.ShapeDtypeStruct((DEPTH, D_MODEL, _ADA_SHARD), F32), name=name,
        compiler_params=_cparams("parallel"))(c_all, dmod_cols)


def _add_rows(a, b, *, name):
    def body(a_ref, b_ref, o_ref):
        o_ref[...] = a_ref[...] + b_ref[...]

    return pl.pallas_call(body, out_shape=jax.ShapeDtypeStruct(a.shape, a.dtype), name=name)(a, b)


def _adamw(w, g, m, v, *, name, tr):
    r, cdim = w.shape
    c1 = 1.0 - ADAM_B1 ** ADAM_STEP
    c2 = 1.0 - ADAM_B2 ** ADAM_STEP

    def body(w_ref, g_ref, m_ref, v_ref, d_ref, mo_ref, vo_ref):
        gv = g_ref[...]
        mn = ADAM_B1 * m_ref[...] + (1.0 - ADAM_B1) * gv
        vn = ADAM_B2 * v_ref[...] + (1.0 - ADAM_B2) * (gv * gv)
        mo_ref[...] = mn
        vo_ref[...] = vn
        d_ref[...] = -ADAM_LR * ((mn / c1) / (jnp.sqrt(vn / c2) + ADAM_EPS) + ADAM_WD * w_ref[...])

    spec = pl.BlockSpec((tr, cdim), lambda i: (i, 0))
    shp = jax.ShapeDtypeStruct((r, cdim), F32)
    return pl.pallas_call(body, grid=(r // tr,), in_specs=[spec] * 4, out_specs=[spec] * 3, out_shape=[shp] * 3,
                          name=name, compiler_params=_cparams("parallel"))(w, g, m, v)


def _pad_rows(a, rows):
    return jnp.concatenate([a, jnp.zeros((rows - a.shape[0],) + a.shape[1:], a.dtype)], axis=0)


def _pad_lanes(a, lanes):
    return jnp.concatenate([a, jnp.zeros(a.shape[:-1] + (lanes - a.shape[-1],), a.dtype)], axis=-1)


def _permute_w_in(wt):
    return jnp.concatenate([wt[512:1536], wt[:512], wt[1536:1544],
                            jnp.zeros((PROJ_W - IN_W, wt.shape[1]), wt.dtype), wt[1544:]], axis=0)


def _unpermute_w_in(wp):
    return jnp.concatenate([wp[1024:1536], wp[:1024], wp[1536:1544], wp[PROJ_SSD_W:]], axis=0)


def _block_diag(w):
    rows = []
    for g in range(4):
        rows.append(jnp.concatenate([w[g] if k == g else jnp.zeros_like(w[g]) for k in range(4)], axis=1))
    return jnp.concatenate(rows, axis=0)


def _diag_blocks(wbd):
    return jnp.stack([wbd[64 * g:64 * (g + 1), 64 * g:64 * (g + 1)] for g in range(4)], axis=0)


def _layer_params(l, big, small):
    return dict(
        w_in=big["w_in"], w_out=big["w_out"], up=big["ffn_up"], down=big["ffn_down"],
        norm1_g=small["norm1_g"][l][None], norm2_g=small["norm2_g"][l][None],
        conv_w=_pad_rows(small["ssd_conv_w"][l], 8), conv_b=small["ssd_conv_b"][l][None],
        dt_bias=_pad_lanes(small["ssd_dt_bias"][l][None], 128), a_log=_pad_lanes(small["ssd_a_log"][l][None], 128),
        d_skip=_pad_lanes(small["ssd_d"][l][None], 128), ssd_norm_g=small["ssd_norm_g"][l][None],
        pool_bd=_block_diag(small["pool_w"][l]), pool_scale=small["pool_scale"][l][None],
        fcw=_pad_rows(small["ffn_conv_w"][l], 8), fcb=small["ffn_conv_b"][l][None])


def _mod_rows(mod_l):
    return [mod_l[None, D_MODEL * i:D_MODEL * (i + 1)] for i in range(6)]


def _layer_fwd(x, mod_l, p, tabs, l):
    sh1, sc1, g1, sh2, sc2, g2 = _mod_rows(mod_l)
    h1 = _norm_mod(x, p["norm1_g"], sc1, sh1, name=f"l{l}_norm1")
    proj = _mm(h1, p["w_in"], nt=True, name=f"l{l}_proj")
    mix, hst = _ssd_fwd(proj, p["conv_w"], p["conv_b"], p["dt_bias"], p["a_log"], p["d_skip"], p["ssd_norm_g"],
                        name=f"l{l}_ssd")
    mix = _pool_fwd(proj, p["pool_bd"], p["pool_scale"], mix, name=f"l{l}_pool")
    y_att, lse, mix = _att_fwd(proj, *tabs, mix, name=f"l{l}_att")
    mo = _mm(mix, p["w_out"], name=f"l{l}_out")
    x1 = _residual(x, g1, mo, name=f"l{l}_res1")
    h2 = _norm_mod(x1, p["norm2_g"], sc2, sh2, name=f"l{l}_norm2")
    hid = _mm(h2, p["up"], nt=True, tm=256, name=f"l{l}_up")
    act = _ffn_act(hid, p["fcw"], p["fcb"], name=f"l{l}_act")
    f = _mm(act, p["down"], name=f"l{l}_down")
    x2 = _residual(x1, g2, f, name=f"l{l}_res2")
    return x2, dict(x=x, h1=h1, proj=proj, hst=hst, y_att=y_att, lse=lse, mix=mix, mo=mo, x1=x1, h2=h2, hid=hid,
                    act=act, f=f)


def _layer_bwd(dx2, sv, mod_l, p, tabs, l, exchange):
    sh1, sc1, g1, sh2, sc2, g2 = _mod_rows(mod_l)
    df, dg2 = _residual_bwd(dx2, g2, sv["f"], name=f"l{l}_res2_b")
    dact = _mm(df, p["down"], nt=True, tn=1408, name=f"l{l}_down_bx")
    d_down = _wgrad(sv["act"], df, name=f"l{l}_down_bw")
    dhid, dfcw, dfcb = _ffn_act_bwd(sv["hid"], dact, p["fcw"], p["fcb"], name=f"l{l}_act_b")
    dh2 = _mm(dhid, p["up"], tm=256, name=f"l{l}_up_bx")
    d_up = _wgrad(dhid, sv["h2"], tk=1408, name=f"l{l}_up_bw")
    exchange(l, "ffn", dict(ffn_up=d_up, ffn_down=d_down))
    dx1, dn2, dsc2, dsh2 = _norm_mod_bwd(sv["x1"], dh2, dx2, p["norm2_g"], sc2, name=f"l{l}_norm2_b")
    dmo, dg1 = _residual_bwd(dx1, g1, sv["mo"], name=f"l{l}_res1_b")
    dmix = _mm(dmo, p["w_out"], nt=True, name=f"l{l}_out_bx")
    d_wout = _wgrad(sv["mix"], dmo, name=f"l{l}_out_bw")
    dproj, dcw, dcb, ddb, dal, dd, dng = _ssd_bwd(
        sv["proj"], sv["hst"], dmix, p["conv_w"], p["conv_b"], p["dt_bias"], p["a_log"], p["d_skip"],
        p["ssd_norm_g"], name=f"l{l}_ssd_b")
    dproj, dwbd, dpsc = _pool_bwd(sv["proj"], dmix, p["pool_bd"], p["pool_scale"], dproj, name=f"l{l}_pool_b")
    dproj = _att_bwd(sv["proj"], *tabs, sv["y_att"], sv["lse"], dmix, dproj, name=f"l{l}_att_b")
    dh1 = _mm(dproj, p["w_in"], name=f"l{l}_proj_bx")
    d_win = _wgrad(dproj, sv["h1"], name=f"l{l}_proj_bw")
    exchange(l, "mix", dict(w_in=d_win, w_out=d_wout))
    dx0, dn1, dsc1, dsh1 = _norm_mod_bwd(sv["x"], dh1, dx1, p["norm1_g"], sc1, name=f"l{l}_norm1_b")
    dmod = jnp.concatenate([dsh1, dsc1, dg1, dsh2, dsc2, dg2], axis=1)[0]
    small = dict(norm1_g=dn1[0], ssd_conv_w=dcw[:SSD_CONV_K], ssd_conv_b=dcb[0], ssd_dt_bias=ddb[0], ssd_a_log=dal[0],
                 ssd_d=dd[0], ssd_norm_g=dng[0], pool_w=_diag_blocks(dwbd), pool_scale=dpsc[0], norm2_g=dn2[0],
                 ffn_conv_w=dfcw[:FFN_CONV_K], ffn_conv_b=dfcb[0])
    return dx0, dmod, small


def _example_step(x, target, pos_col, inv_freq_lane, mod, big, small, final_g, exchange):
    tabs = _rope_tables(pos_col, inv_freq_lane, name="rope_tables")
    params, saved = [], []
    for l in range(DEPTH):
        params.append(_layer_params(l, big[l], small))
        x, sv = _layer_fwd(x, mod[l], params[l], tabs, l)
        saved.append(sv)
    loss_row, dx, dfg = _final_loss(x, final_g[None], target, name="final_loss")
    dmods, smalls = [None] * DEPTH, [None] * DEPTH
    for l in reversed(range(DEPTH)):
        dx, dmods[l], smalls[l] = _layer_bwd(dx, saved[l], mod[l], params[l], tabs, l, exchange)
    return loss_row, dx, jnp.stack(dmods, axis=0), smalls, dfg[0]


_BIG = (("w_in", (D_MODEL, IN_W // N_DEV)), ("w_out", (D_MODEL // N_DEV, D_MODEL)),
        ("ffn_up", (D_MODEL, 2 * FFN_DIM // N_DEV)), ("ffn_down", (FFN_DIM // N_DEV, D_MODEL)))
_PACK_LANES = 1024

_SMALL_GRADS = (("norm1_g", D_MODEL), ("ssd_conv_w", SSD_CONV_K * SSD_CONV_CH), ("ssd_conv_b", SSD_CONV_CH),
                ("ssd_dt_bias", 128), ("ssd_a_log", 128), ("ssd_d", 128), ("ssd_norm_g", SSD_INNER),
                ("pool_w", 4 * 64 * 64), ("pool_scale", POOL_W), ("norm2_g", D_MODEL),
                ("ffn_conv_w", FFN_CONV_K * 2 * FFN_DIM), ("ffn_conv_b", 2 * FFN_DIM))
_SMALL_PARAMS = ("ada_b", "norm1_g", "ssd_conv_w", "ssd_conv_b", "ssd_dt_bias", "ssd_a_log", "ssd_d", "ssd_norm_g",
                 "pool_w", "pool_scale", "norm2_g", "ffn_conv_w", "ffn_conv_b", "final_g")
_WEIGHT_ORDER = ("ada_w", "ada_b", "norm1_g", "w_in", "ssd_conv_w", "ssd_conv_b", "ssd_dt_bias", "ssd_a_log", "ssd_d",
                 "ssd_norm_g", "pool_w", "pool_scale", "w_out", "norm2_g", "ffn_up", "ffn_conv_w", "ffn_conv_b",
                 "ffn_down", "final_g")


def _pack(parts, rows, dtype):
    flat = jnp.concatenate([p.reshape(-1).astype(dtype) for p in parts])
    return jnp.concatenate([flat, jnp.zeros((rows * _PACK_LANES - flat.shape[0],), dtype)]).reshape(rows, _PACK_LANES)


_COLUMN_SHARDED = ("w_in", "ffn_up")
_GROUPS = (("mix", ("w_in", "w_out")), ("ffn", ("ffn_up", "ffn_down")))


def _big_shares(w, l, names):
    return [(w[name][l].T if name in _COLUMN_SHARDED else w[name][l]).astype(BF16) for name in names]


def _unshard_big(names, gathered):
    out = {}
    for name, g in zip(names, gathered):
        full = g.reshape(N_DEV * g.shape[1], g.shape[2])
        out[name] = _permute_w_in(full) if name == "w_in" else full
    return out


def _shard_big(grads):
    out = []
    for name, g in grads.items():
        g = _unpermute_w_in(g) if name == "w_in" else g
        out.append(g.reshape(N_DEV, g.shape[0] // N_DEV, g.shape[1]))
    return out


def kernel(x, c, positions, ada_w, ada_b, norm1_g, w_in, ssd_conv_w, ssd_conv_b, ssd_dt_bias, ssd_a_log, ssd_d, ssd_norm_g, pool_w, pool_scale, w_out, norm2_g, ffn_up, ffn_conv_w, ffn_conv_b, ffn_down, final_g, loss_target, m_ada_w, m_ada_b, m_norm1_g, m_w_in, m_ssd_conv_w, m_ssd_conv_b, m_ssd_dt_bias, m_ssd_a_log, m_ssd_d, m_ssd_norm_g, m_pool_w, m_pool_scale, m_w_out, m_norm2_g, m_ffn_up, m_ffn_conv_w, m_ffn_conv_b, m_ffn_down, m_final_g, v_ada_w, v_ada_b, v_norm1_g, v_w_in, v_ssd_conv_w, v_ssd_conv_b, v_ssd_dt_bias, v_ssd_a_log, v_ssd_d, v_ssd_norm_g, v_pool_w, v_pool_scale, v_w_out, v_norm2_g, v_ffn_up, v_ffn_conv_w, v_ffn_conv_b, v_ffn_down, v_final_g):
    w = dict(ada_w=ada_w, ada_b=ada_b, norm1_g=norm1_g, w_in=w_in, ssd_conv_w=ssd_conv_w, ssd_conv_b=ssd_conv_b,
             ssd_dt_bias=ssd_dt_bias, ssd_a_log=ssd_a_log, ssd_d=ssd_d, ssd_norm_g=ssd_norm_g, pool_w=pool_w,
             pool_scale=pool_scale, w_out=w_out, norm2_g=norm2_g, ffn_up=ffn_up, ffn_conv_w=ffn_conv_w,
             ffn_conv_b=ffn_conv_b, ffn_down=ffn_down, final_g=final_g)
    m = dict(ada_w=m_ada_w, ada_b=m_ada_b, norm1_g=m_norm1_g, w_in=m_w_in, ssd_conv_w=m_ssd_conv_w,
             ssd_conv_b=m_ssd_conv_b, ssd_dt_bias=m_ssd_dt_bias, ssd_a_log=m_ssd_a_log, ssd_d=m_ssd_d,
             ssd_norm_g=m_ssd_norm_g, pool_w=m_pool_w, pool_scale=m_pool_scale, w_out=m_w_out, norm2_g=m_norm2_g,
             ffn_up=m_ffn_up, ffn_conv_w=m_ffn_conv_w, ffn_conv_b=m_ffn_conv_b, ffn_down=m_ffn_down,
             final_g=m_final_g)
    v = dict(ada_w=v_ada_w, ada_b=v_ada_b, norm1_g=v_norm1_g, w_in=v_w_in, ssd_conv_w=v_ssd_conv_w,
             ssd_conv_b=v_ssd_conv_b, ssd_dt_bias=v_ssd_dt_bias, ssd_a_log=v_ssd_a_log, ssd_d=v_ssd_d,
             ssd_norm_g=v_ssd_norm_g, pool_w=v_pool_w, pool_scale=v_pool_scale, w_out=v_w_out, norm2_g=v_norm2_g,
             ffn_up=v_ffn_up, ffn_conv_w=v_ffn_conv_w, ffn_conv_b=v_ffn_conv_b, ffn_down=v_ffn_down,
             final_g=v_final_g)
    ix, iy, ic = _axes()
    dev = 4 * ix + 2 * iy + ic

    n_scw, n_fcw = ssd_conv_w.size, ffn_conv_w.size
    small_all = _allgather([_pack([c, ssd_conv_w, ffn_conv_w], 8, F32)], name="gather_small")[0].reshape(N_DEV, -1)
    c_all = small_all[:, :D_MODEL]
    scw = small_all[:, D_MODEL:D_MODEL + n_scw].reshape(N_DEV, DEPTH, SSD_CONV_K, -1)
    scw = scw.transpose(1, 2, 0, 3).reshape(DEPTH, SSD_CONV_K, SSD_CONV_CH)
    fcw = small_all[:, D_MODEL + n_scw:D_MODEL + n_scw + n_fcw].reshape(N_DEV, DEPTH, FFN_CONV_K, -1)
    fcw = fcw.transpose(1, 2, 0, 3).reshape(DEPTH, FFN_CONV_K, 2 * FFN_DIM)

    mod_cols = _ada_mod(c_all, ada_w, name="ada_mod")
    mod_all = _allgather([mod_cols.reshape(DEPTH * N_DEV, _ADA_SHARD)], name="gather_mod")[0]
    mod_all = mod_all.reshape(N_DEV, DEPTH, N_DEV, _ADA_SHARD)
    mod_mine = lax.dynamic_index_in_dim(mod_all, dev, axis=2, keepdims=False)
    mod = _add_rows(mod_mine.transpose(1, 0, 2).reshape(DEPTH, 6 * D_MODEL), ada_b, name="ada_bias")

    big = [{}, {}]
    landed = small_all
    for l in range(DEPTH):
        for k, (group, names) in enumerate(_GROUPS):
            shares, _ = lax.optimization_barrier((_big_shares(w, l, names), landed))
            got = _allgather_async(shares, name=f"gather_weights_l{l}_{group}", collective_id=1 + 2 * l + k)
            big[l].update(_unshard_big(names, got))
            landed = got[0]

    core = ic.astype(jnp.int32).reshape(1)
    from_chips = {}

    def exchange(l, group, g):
        cid = 5 + 4 * l + 2 * (group == "mix")
        blocks = _shard_big(g)
        from_sibling = _pair_exchange(blocks, name=f"grads_pair_exchange_l{l}_{group}", collective_id=cid)
        parts = [_pair_sum(core, b, s, name=f"grads_pair_sum_l{l}_{name}")
                 for name, b, s in zip(g, blocks, from_sibling)]
        got = _chip_exchange(parts, name=f"grads_chip_exchange_l{l}_{group}", collective_id=cid + 1)
        from_chips.update({(l, name): t for name, t in zip(g, got)})

    small = dict(norm1_g=norm1_g, norm2_g=norm2_g, ssd_conv_w=scw, ssd_conv_b=ssd_conv_b, ssd_dt_bias=ssd_dt_bias,
                 ssd_a_log=ssd_a_log, ssd_d=ssd_d, ssd_norm_g=ssd_norm_g, pool_w=pool_w, pool_scale=pool_scale,
                 ffn_conv_w=fcw, ffn_conv_b=ffn_conv_b)

    inv_freq = ROPE_THETA ** (-jnp.arange(0, ROT_DIM, 2, dtype=F32) / ROT_DIM)
    lane = jnp.arange(128) % HEAD_LANES
    inv_freq_lane = jnp.where(lane < ROT_DIM, inv_freq[lane % (ROT_DIM // 2)], 0.0)[None, :]
    pos_col = positions.reshape(SEQ, 1).astype(F32)
    loss_row, dx, dmod, g_small, g_final = _example_step(
        x[0], loss_target[0], pos_col, inv_freq_lane, mod, big, small, final_g, exchange)

    grads = {}
    for name, _ in _BIG:
        per_layer = [_sum_blocks(from_chips[l, name], name=f"grads_chip_sum_l{l}_{name}") for l in range(DEPTH)]
        grads[name] = jnp.stack([g.T if name in _COLUMN_SHARDED else g for g in per_layer], axis=0)

    small_parts = [loss_row, dmod] + [jnp.stack([g_small[l][name] for l in range(DEPTH)], axis=0)
                                      for name, _ in _SMALL_GRADS] + [g_final]
    n_small = sum(p.size for p in small_parts)
    small_rows = -(-n_small // _PACK_LANES)
    gathered = _allgather([_pack(small_parts, small_rows, F32)], name="gather_small_grads")[0]
    total = _sum_blocks(gathered, name="sum_small_grads").reshape(-1)
    loss = total[0]
    off = 128
    grads["ada_b"] = total[off:off + DEPTH * 6 * D_MODEL].reshape(DEPTH, 6 * D_MODEL)
    dmod_all = gathered.reshape(N_DEV, -1)[:, off:off + DEPTH * 6 * D_MODEL].reshape(N_DEV, DEPTH, 6 * D_MODEL)
    off += DEPTH * 6 * D_MODEL
    for name, n in _SMALL_GRADS:
        grads[name] = total[off:off + DEPTH * n].reshape(DEPTH, n)
        off += DEPTH * n
    grads["final_g"] = total[off:off + D_MODEL]
    dmod_cols = lax.dynamic_slice_in_dim(dmod_all, dev * _ADA_SHARD, _ADA_SHARD, axis=2).transpose(1, 0, 2)
    grads["ada_w"] = _ada_wgrad(c_all, dmod_cols, name="ada_wgrad")
    for name in ("ssd_dt_bias", "ssd_a_log", "ssd_d"):
        grads[name] = grads[name][:, :SSD_HEADS]
    grads["pool_w"] = grads["pool_w"].reshape(pool_w.shape)
    grads["ssd_conv_w"] = lax.dynamic_slice_in_dim(
        grads["ssd_conv_w"].reshape(DEPTH, SSD_CONV_K, SSD_CONV_CH), dev * ssd_conv_w.shape[2], ssd_conv_w.shape[2], axis=2)
    grads["ffn_conv_w"] = lax.dynamic_slice_in_dim(
        grads["ffn_conv_w"].reshape(DEPTH, FFN_CONV_K, 2 * FFN_DIM), dev * ffn_conv_w.shape[2], ffn_conv_w.shape[2], axis=2)

    delta, new_m, new_v = {}, {}, {}
    for name, tr in (("ada_w", 512), ("w_in", 512), ("w_out", 256), ("ffn_up", 512), ("ffn_down", 352)):
        shp = w[name].shape
        two_d = lambda a: a.reshape(shp[0] * shp[1], shp[2])
        d_, m_, v_ = _adamw(two_d(w[name]), two_d(grads[name]), two_d(m[name]), two_d(v[name]), tr=tr,
                               name=f"adamw_{name}")
        delta[name], new_m[name], new_v[name] = (t.reshape(shp) for t in (d_, m_, v_))
    n_packed = sum(w[name].size for name in _SMALL_PARAMS)
    rows = -(-n_packed // _PACK_LANES)
    packed = [_pack([t[name] for name in _SMALL_PARAMS], rows, F32) for t in (w, grads, m, v)]
    outs = [o.reshape(-1) for o in _adamw(*packed, tr=rows, name="adamw_small")]
    off = 0
    for name in _SMALL_PARAMS:
        n = w[name].size
        delta[name], new_m[name], new_v[name] = (o[off:off + n].reshape(w[name].shape) for o in outs)
        off += n

    grad_x = dx[None]
    return (loss, grad_x, *[grads[n].reshape(w[n].shape) for n in _WEIGHT_ORDER],
            *[delta[n] for n in _WEIGHT_ORDER], *[new_m[n] for n in _WEIGHT_ORDER],
            *[new_v[n] for n in _WEIGHT_ORDER])
```

```python
import functools
import math

import jax
import jax.numpy as jnp
from jax import lax
from jax.experimental import pallas as pl
from jax.experimental.pallas import tpu as pltpu
from jax.experimental.pallas import tpu_sc as plsc

F32 = jnp.float32
BF16 = jnp.bfloat16

N_DEV = 8
D_MODEL = 1024
SEQ = 4096
DEPTH = 2
SSD_INNER = 512
SSD_HEADS = 8
SSD_HPG = 4
SSD_STATE = 128
SSD_CHUNK = 256
SSD_CONV_K = 4
SSD_CONV_CH = 1024
POOL_W = 256
POOL_WINDOWS = (2, 4, 8, 16)
ATT_W = 256
ATT_PATTERNS = ((128, 1), (512, 4), (2048, 16))
ATT_BLOCK = 128
ROT_DIM = 16
ROPE_THETA = 500000.0
IN_W = 2568
FFN_DIM = 2816
FFN_CONV_K = 3
NORM_EPS = 1e-6
HEAD_LANES = 64

ADAM_LR = 0.001
ADAM_B1 = 0.9
ADAM_B2 = 0.999
ADAM_EPS = 1e-08
ADAM_WD = 0.01
ADAM_STEP = 10

PROJ_W = 2816
PROJ_SSD_W = 1792
PROJ_Z_BLK = 2
PROJ_DT_BLK = 12
PROJ_POOL_BLK = 7
PROJ_Q_BLK, PROJ_K_BLK, PROJ_V_BLK = 16, 18, 20
MIX_POOL_BLK = 2
MIX_ATT_BLK = 6
VMEM_LIMIT = 56 * 1024 * 1024
CONV_HALO = 8
POOL_HALO = 16
ATT_KPAD = ATT_BLOCK * 16
MESH = pl.DeviceIdType.MESH
_HBM = pl.BlockSpec(memory_space=pl.ANY)


def _cparams(*sem):
    return pltpu.CompilerParams(dimension_semantics=sem, vmem_limit_bytes=VMEM_LIMIT)


def _silu(x):
    return x * jax.nn.sigmoid(x)


def _pick_lane(v, h):
    lane = lax.broadcasted_iota(jnp.int32, v.shape, 1)
    return jnp.sum(jnp.where(lane == h, v, 0.0), axis=1, keepdims=True)


def _pick_row(v, h):
    row = lax.broadcasted_iota(jnp.int32, v.shape, 0)
    return jnp.sum(jnp.where(row == h, v, 0.0), axis=0, keepdims=True)


def _head_of_lane(width):
    return lax.broadcasted_iota(jnp.int32, (1, width), 1) // HEAD_LANES


@functools.partial(jax.custom_vjp, nondiff_argnums=(1, 2))
def _shift_rows(x_ext, s, halo):
    y = x_ext if s == 0 else pltpu.roll(x_ext, s, 0)
    return y[halo:]


def _shift_rows_fwd(x_ext, s, halo):
    return _shift_rows(x_ext, s, halo), None


def _shift_rows_bwd(s, halo, _, g):
    ge = jnp.concatenate([jnp.zeros((halo, g.shape[1]), g.dtype), g], axis=0)
    return (ge if s == 0 else pltpu.roll(ge, ge.shape[0] - s, 0),)


_shift_rows.defvjp(_shift_rows_fwd, _shift_rows_bwd)


@functools.partial(jax.custom_vjp, nondiff_argnums=(1,))
def _roll_rows(x, s):
    return pltpu.roll(x, s, 0)


def _roll_rows_fwd(x, s):
    return _roll_rows(x, s), None


def _roll_rows_bwd(s, _, g):
    return (pltpu.roll(g, g.shape[0] - s, 0),)


_roll_rows.defvjp(_roll_rows_fwd, _roll_rows_bwd)


def _mm(a, w, *, name, nt=False, tm=512, tn=None, out_dtype=F32):
    t, k = a.shape
    n = w.shape[0] if nt else w.shape[1]
    tn = tn or n

    def body(a_ref, w_ref, o_ref):
        av = a_ref[...].astype(BF16)
        if nt:
            acc = lax.dot_general(av, w_ref[...], (((1,), (1,)), ((), ())), preferred_element_type=F32)
        else:
            acc = jnp.dot(av, w_ref[...], preferred_element_type=F32)
        o_ref[...] = acc.astype(out_dtype)

    w_spec = pl.BlockSpec((tn, k), lambda i, j: (j, 0)) if nt else pl.BlockSpec((k, tn), lambda i, j: (0, j))
    return pl.pallas_call(
        body, grid=(t // tm, n // tn),
        in_specs=[pl.BlockSpec((tm, k), lambda i, j: (i, 0)), w_spec],
        out_specs=pl.BlockSpec((tm, tn), lambda i, j: (i, j)),
        out_shape=jax.ShapeDtypeStruct((t, n), out_dtype), name=name,
        compiler_params=_cparams("parallel", "parallel"))(a, w)


def _wgrad(a, b, *, name, tk=None, tn=None, tt=512, out_dtype=BF16):
    t, k = a.shape
    n = b.shape[1]
    tk = tk or k
    tn = tn or n
    steps = t // tt

    def body(a_ref, b_ref, o_ref, acc_ref):
        s = pl.program_id(2)

        @pl.when(s == 0)
        def _():
            acc_ref[...] = jnp.zeros_like(acc_ref)

        acc_ref[...] += lax.dot_general(a_ref[...].astype(BF16), b_ref[...].astype(BF16),
                                        (((0,), (0,)), ((), ())), preferred_element_type=F32)

        @pl.when(s == steps - 1)
        def _():
            o_ref[...] = acc_ref[...].astype(out_dtype)

    return pl.pallas_call(
        body, grid=(k // tk, n // tn, steps),
        in_specs=[pl.BlockSpec((tt, tk), lambda i, j, s: (s, i)), pl.BlockSpec((tt, tn), lambda i, j, s: (s, j))],
        out_specs=pl.BlockSpec((tk, tn), lambda i, j, s: (i, j)),
        out_shape=jax.ShapeDtypeStruct((k, n), out_dtype),
        scratch_shapes=[pltpu.VMEM((tk, tn), F32)], name=name,
        compiler_params=_cparams("parallel", "parallel", "arbitrary"))(a, b)


def _norm_mod(x, g, sc, sh, *, name, tm=512):
    s, d = x.shape

    def body(x_ref, g_ref, sc_ref, sh_ref, o_ref):
        xv = x_ref[...]
        r = lax.rsqrt(jnp.mean(xv * xv, axis=-1, keepdims=True) + NORM_EPS)
        o_ref[...] = ((xv * r * g_ref[...]) * (1.0 + sc_ref[...]) + sh_ref[...]).astype(BF16)

    row = pl.BlockSpec((1, d), lambda i: (0, 0))
    return pl.pallas_call(
        body, grid=(s // tm,), in_specs=[pl.BlockSpec((tm, d), lambda i: (i, 0)), row, row, row],
        out_specs=pl.BlockSpec((tm, d), lambda i: (i, 0)),
        out_shape=jax.ShapeDtypeStruct((s, d), BF16), name=name, compiler_params=_cparams("parallel"))(x, g, sc, sh)


def _norm_mod_bwd(x, dh, dres, g, sc, *, name, tm=512):
    s, d = x.shape
    steps = s // tm

    def body(x_ref, dh_ref, dres_ref, g_ref, sc_ref, dx_ref, dg_ref, dsc_ref, dsh_ref, da_acc, dsh_acc):
        i = pl.program_id(0)

        @pl.when(i == 0)
        def _():
            da_acc[...] = jnp.zeros_like(da_acc)
            dsh_acc[...] = jnp.zeros_like(dsh_acc)

        xv = x_ref[...]
        dhv = dh_ref[...].astype(F32)
        r = lax.rsqrt(jnp.mean(xv * xv, axis=-1, keepdims=True) + NORM_EPS)
        xhat = xv * r
        gain = g_ref[...] * (1.0 + sc_ref[...])
        dxhat = dhv * gain
        dx_ref[...] = dres_ref[...] + r * (dxhat - xhat * jnp.mean(dxhat * xhat, axis=-1, keepdims=True))
        da_acc[...] += jnp.sum(dhv * xhat, axis=0, keepdims=True)
        dsh_acc[...] += jnp.sum(dhv, axis=0, keepdims=True)

        @pl.when(i == steps - 1)
        def _():
            dg_ref[...] = da_acc[...] * (1.0 + sc_ref[...])
            dsc_ref[...] = da_acc[...] * g_ref[...]
            dsh_ref[...] = dsh_acc[...]

    row = pl.BlockSpec((1, d), lambda i: (0, 0))
    tile = pl.BlockSpec((tm, d), lambda i: (i, 0))
    row_shape = jax.ShapeDtypeStruct((1, d), F32)
    return pl.pallas_call(
        body, grid=(steps,), in_specs=[tile, tile, tile, row, row],
        out_specs=[tile, row, row, row],
        out_shape=[jax.ShapeDtypeStruct((s, d), F32), row_shape, row_shape, row_shape],
        scratch_shapes=[pltpu.VMEM((1, d), F32), pltpu.VMEM((1, d), F32)], name=name,
        compiler_params=_cparams("arbitrary"))(x, dh, dres, g, sc)


def _residual(x, gate, f, *, name, tm=512):
    s, d = x.shape

    def body(x_ref, g_ref, f_ref, o_ref):
        o_ref[...] = x_ref[...] + g_ref[...] * f_ref[...]

    tile = pl.BlockSpec((tm, d), lambda i: (i, 0))
    return pl.pallas_call(
        body, grid=(s // tm,), in_specs=[tile, pl.BlockSpec((1, d), lambda i: (0, 0)), tile], out_specs=tile,
        out_shape=jax.ShapeDtypeStruct((s, d), F32), name=name, compiler_params=_cparams("parallel"))(x, gate, f)


def _residual_bwd(dx, gate, f, *, name, tm=512):
    s, d = dx.shape
    steps = s // tm

    def body(dx_ref, g_ref, f_ref, df_ref, dg_ref):
        i = pl.program_id(0)

        @pl.when(i == 0)
        def _():
            dg_ref[...] = jnp.zeros_like(dg_ref)

        dxv = dx_ref[...]
        df_ref[...] = (g_ref[...] * dxv).astype(BF16)
        dg_ref[...] += jnp.sum(dxv * f_ref[...], axis=0, keepdims=True)

    tile = pl.BlockSpec((tm, d), lambda i: (i, 0))
    row = pl.BlockSpec((1, d), lambda i: (0, 0))
    return pl.pallas_call(
        body, grid=(steps,), in_specs=[tile, row, tile], out_specs=[tile, row],
        out_shape=[jax.ShapeDtypeStruct((s, d), BF16), jax.ShapeDtypeStruct((1, d), F32)], name=name,
        compiler_params=_cparams("arbitrary"))(dx, gate, f)


def _final_loss(x, g, target, *, name, tm=512):
    s, d = x.shape
    steps = s // tm

    def body(x_ref, g_ref, t_ref, loss_ref, dx_ref, dg_ref, sq_acc):
        i = pl.program_id(0)

        @pl.when(i == 0)
        def _():
            sq_acc[...] = jnp.zeros_like(sq_acc)
            dg_ref[...] = jnp.zeros_like(dg_ref)

        xv = x_ref[...]
        r = lax.rsqrt(jnp.mean(xv * xv, axis=-1, keepdims=True) + NORM_EPS)
        xhat = xv * r
        err = xhat * g_ref[...] - t_ref[...]
        sq_acc[...] += jnp.sum(err * err, axis=0, keepdims=True)
        dy = err * (1.0 / d)
        dg_ref[...] += jnp.sum(dy * xhat, axis=0, keepdims=True)
        dxhat = dy * g_ref[...]
        dx_ref[...] = r * (dxhat - xhat * jnp.mean(dxhat * xhat, axis=-1, keepdims=True))

        @pl.when(i == steps - 1)
        def _():
            total = jnp.sum(sq_acc[...], axis=1, keepdims=True) * (0.5 / d)
            loss_ref[...] = jnp.broadcast_to(total, loss_ref.shape)

    tile = pl.BlockSpec((tm, d), lambda i: (i, 0))
    row = pl.BlockSpec((1, d), lambda i: (0, 0))
    return pl.pallas_call(
        body, grid=(steps,), in_specs=[tile, row, tile],
        out_specs=[pl.BlockSpec((1, 128), lambda i: (0, 0)), tile, row],
        out_shape=[jax.ShapeDtypeStruct((1, 128), F32), jax.ShapeDtypeStruct((s, d), F32),
                   jax.ShapeDtypeStruct((1, d), F32)],
        scratch_shapes=[pltpu.VMEM((1, d), F32)], name=name, compiler_params=_cparams("arbitrary"))(x, g, target)


def _ssd_chunk(z, xbc_ext, dt_raw, conv_w, conv_b, dt_bias, a_log, d_skip, norm_g, h_in):
    q = z.shape[0]
    gw = SSD_HPG * HEAD_LANES
    xc = conv_b
    for k in range(SSD_CONV_K):
        xc = xc + _pick_row(conv_w, k) * _shift_rows(xbc_ext, SSD_CONV_K - 1 - k, CONV_HALO)
    xc = _silu(xc)
    dt = jax.nn.softplus(dt_raw + dt_bias)
    da = dt * (-jnp.exp(a_log))
    ri = lax.broadcasted_iota(jnp.int32, (q, q), 0)
    ci = lax.broadcasted_iota(jnp.int32, (q, q), 1)
    causal = ri >= ci
    tril = causal.astype(F32)
    a_cum = jnp.dot(tril, da, preferred_element_type=F32, precision=lax.Precision.HIGHEST)
    a_cum_t = lax.dot_general(da, tril, (((0,), (1,)), ((), ())), preferred_element_type=F32,
                              precision=lax.Precision.HIGHEST)
    a_last = _pick_row(a_cum, q - 1)
    head = _head_of_lane(gw)
    ys, hs = [], []
    for g in range(2):
        xs = xc[:, gw * g:gw * (g + 1)]
        bm = xc[:, SSD_INNER + SSD_STATE * g:SSD_INNER + SSD_STATE * (g + 1)]
        cm = xc[:, SSD_INNER + 2 * SSD_STATE + SSD_STATE * g:SSD_INNER + 2 * SSD_STATE + SSD_STATE * (g + 1)]
        cb = lax.dot_general(cm.astype(BF16), bm.astype(BF16), (((1,), (1,)), ((), ())), preferred_element_type=F32)
        cols = [_pick_lane(a_cum, SSD_HPG * g + j) for j in range(SSD_HPG)]
        lasts = [_pick_lane(a_last, SSD_HPG * g + j) for j in range(SSD_HPG)]
        dt_exp = sum(jnp.where(head == j, _pick_lane(dt, SSD_HPG * g + j), 0.0) for j in range(SSD_HPG))
        d_exp = sum(jnp.where(head == j, _pick_lane(d_skip, SSD_HPG * g + j), 0.0) for j in range(SSD_HPG))
        e_cum = sum(jnp.where(head == j, jnp.exp(cols[j]), 0.0) for j in range(SSD_HPG))
        c_dec = sum(jnp.where(head == j, jnp.exp(lasts[j]), 0.0) for j in range(SSD_HPG))
        xsdt = (xs * dt_exp).astype(BF16)
        y_diag = jnp.zeros((q, gw), F32)
        st_new = jnp.zeros((SSD_STATE, gw), F32)
        for j in range(SSD_HPG):
            row = _pick_row(a_cum_t, SSD_HPG * g + j)
            lmat = jnp.exp(jnp.where(causal, cols[j] - row, -jnp.inf))
            r = jnp.dot((cb * lmat).astype(BF16), xsdt, preferred_element_type=F32)
            y_diag = y_diag + jnp.where(head == j, r, 0.0)
            bd = (bm * jnp.exp(lasts[j] - cols[j])).astype(BF16)
            st = lax.dot_general(bd, xsdt, (((0,), (0,)), ((), ())), preferred_element_type=F32)
            st_new = st_new + jnp.where(head == j, st, 0.0)
        y_off = jnp.dot(cm.astype(BF16), h_in[g].astype(BF16), preferred_element_type=F32) * e_cum
        hs.append(h_in[g] * c_dec + st_new)
        y = y_diag + y_off + d_exp * xs
        yz = y * _silu(z[:, gw * g:gw * (g + 1)])
        yz = yz * lax.rsqrt(jnp.mean(yz * yz, axis=-1, keepdims=True) + NORM_EPS)
        ys.append(yz * norm_g[:, gw * g:gw * (g + 1)])
    return jnp.concatenate(ys, axis=1), tuple(hs)


_SSD_NCHUNK = SEQ // SSD_CHUNK
_HALO_PER_CHUNK = SSD_CHUNK // CONV_HALO


def _ssd_param_specs(const):
    return [pl.BlockSpec((8, SSD_CONV_CH), const), pl.BlockSpec((1, SSD_CONV_CH), const),
            pl.BlockSpec((1, 128), const), pl.BlockSpec((1, 128), const), pl.BlockSpec((1, 128), const),
            pl.BlockSpec((1, SSD_INNER), const)]


def _ssd_fwd(proj, conv_w, conv_b, dt_bias, a_log, d_skip, norm_g, *, name):
    q = SSD_CHUNK

    def body(z_ref, xbc_ref, halo_ref, dt_ref, cw_ref, cb_ref, db_ref, al_ref, d_ref, ng_ref, y_ref, hs_ref, h_acc):
        i = pl.program_id(0)

        @pl.when(i == 0)
        def _():
            h_acc[...] = jnp.zeros_like(h_acc)

        halo = jnp.where(i == 0, 0.0, halo_ref[...])
        xbc_ext = jnp.concatenate([halo, xbc_ref[...]], axis=0)
        h_in = (h_acc[0], h_acc[1])
        hs_ref[0, 0] = h_in[0]
        hs_ref[0, 1] = h_in[1]
        y, h_out = _ssd_chunk(z_ref[...], xbc_ext, dt_ref[...], cw_ref[...], cb_ref[...], db_ref[...], al_ref[...],
                              d_ref[...], ng_ref[...], h_in)
        y_ref[...] = y.astype(BF16)
        h_acc[0] = h_out[0]
        h_acc[1] = h_out[1]

    const = lambda i: (0, 0)
    return pl.pallas_call(
        body, grid=(_SSD_NCHUNK,),
        in_specs=[pl.BlockSpec((q, SSD_INNER), lambda i: (i, PROJ_Z_BLK)),
                  pl.BlockSpec((q, SSD_CONV_CH), lambda i: (i, 0)),
                  pl.BlockSpec((CONV_HALO, SSD_CONV_CH), lambda i: (jnp.maximum(i * _HALO_PER_CHUNK - 1, 0), 0)),
                  pl.BlockSpec((q, 128), lambda i: (i, PROJ_DT_BLK))] + _ssd_param_specs(const),
        out_specs=[pl.BlockSpec((q, SSD_INNER), lambda i: (i, 0)),
                   pl.BlockSpec((1, 2, SSD_STATE, 256), lambda i: (i, 0, 0, 0))],
        out_shape=[jax.ShapeDtypeStruct((SEQ, D_MODEL), BF16),
                   jax.ShapeDtypeStruct((_SSD_NCHUNK, 2, SSD_STATE, 256), F32)],
        scratch_shapes=[pltpu.VMEM((2, SSD_STATE, 256), F32)], name=name,
        compiler_params=_cparams("arbitrary"))(proj, proj, proj, proj, conv_w, conv_b, dt_bias, a_log, d_skip, norm_g)


def _ssd_bwd(proj, hstates, dmix, conv_w, conv_b, dt_bias, a_log, d_skip, norm_g, *, name):
    q = SSD_CHUNK
    last = _SSD_NCHUNK - 1

    def body(z_ref, xbc_ref, halo_ref, dt_ref, hs_ref, dy_ref, cw_ref, cb_ref, db_ref, al_ref, d_ref, ng_ref,
             dp_ref, dcw_ref, dcb_ref, ddb_ref, dal_ref, dd_ref, dng_ref, dh_acc, dhalo_acc):
        i = pl.program_id(0)

        @pl.when(i == 0)
        def _():
            dh_acc[...] = jnp.zeros_like(dh_acc)
            dhalo_acc[...] = jnp.zeros_like(dhalo_acc)
            for r in (dcw_ref, dcb_ref, ddb_ref, dal_ref, dd_ref, dng_ref):
                r[...] = jnp.zeros_like(r)

        halo = jnp.where(i == last, 0.0, halo_ref[...])
        xbc_ext = jnp.concatenate([halo, xbc_ref[...]], axis=0)
        _, vjp = jax.vjp(_ssd_chunk, z_ref[...], xbc_ext, dt_ref[...], cw_ref[...], cb_ref[...], db_ref[...],
                         al_ref[...], d_ref[...], ng_ref[...], (hs_ref[0, 0], hs_ref[0, 1]))
        gz, gx, gdt, gcw, gcb, gdb, gal, gd, gng, gh = vjp((dy_ref[...], (dh_acc[0], dh_acc[1])))
        dxbc = jnp.concatenate([gx[CONV_HALO:q], gx[q:] + dhalo_acc[...]], axis=0)
        dp_ref[...] = jnp.concatenate([dxbc, gz, gdt, jnp.zeros_like(gdt)], axis=1).astype(BF16)
        dhalo_acc[...] = gx[:CONV_HALO]
        dh_acc[0] = gh[0]
        dh_acc[1] = gh[1]
        dcw_ref[...] += gcw
        dcb_ref[...] += gcb
        ddb_ref[...] += gdb
        dal_ref[...] += gal
        dd_ref[...] += gd
        dng_ref[...] += gng

    const = lambda i: (0, 0)
    rev = lambda i: last - i
    row = lambda n: jax.ShapeDtypeStruct((1, n), F32)
    return pl.pallas_call(
        body, grid=(_SSD_NCHUNK,),
        in_specs=[pl.BlockSpec((q, SSD_INNER), lambda i: (rev(i), PROJ_Z_BLK)),
                  pl.BlockSpec((q, SSD_CONV_CH), lambda i: (rev(i), 0)),
                  pl.BlockSpec((CONV_HALO, SSD_CONV_CH), lambda i: (jnp.maximum(rev(i) * _HALO_PER_CHUNK - 1, 0), 0)),
                  pl.BlockSpec((q, 128), lambda i: (rev(i), PROJ_DT_BLK)),
                  pl.BlockSpec((1, 2, SSD_STATE, 256), lambda i: (rev(i), 0, 0, 0)),
                  pl.BlockSpec((q, SSD_INNER), lambda i: (rev(i), 0))] + _ssd_param_specs(const),
        out_specs=[pl.BlockSpec((q, PROJ_SSD_W), lambda i: (rev(i), 0))] + _ssd_param_specs(const),
        out_shape=[jax.ShapeDtypeStruct((SEQ, PROJ_W), BF16), jax.ShapeDtypeStruct((8, SSD_CONV_CH), F32),
                   row(SSD_CONV_CH), row(128), row(128), row(128), row(SSD_INNER)],
        scratch_shapes=[pltpu.VMEM((2, SSD_STATE, 256), F32), pltpu.VMEM((CONV_HALO, SSD_CONV_CH), F32)], name=name,
        compiler_params=_cparams("arbitrary"))(proj, proj, proj, proj, hstates, dmix, conv_w, conv_b, dt_bias, a_log,
                                                d_skip, norm_g)


def _rope_tables(pos_col, inv_freq_lane, *, name):
    s = pos_col.shape[0]

    def body(p_ref, f_ref, c_ref, s1_ref, s2_ref):
        ang = p_ref[...] * f_ref[...]
        within = lax.broadcasted_iota(jnp.int32, ang.shape, 1) % HEAD_LANES
        half = ROT_DIM // 2
        c_ref[...] = jnp.where(within < ROT_DIM, jnp.cos(ang), 1.0)
        sn = jnp.sin(ang)
        s1_ref[...] = jnp.where(within < half, -sn, 0.0)
        s2_ref[...] = jnp.where((within >= half) & (within < ROT_DIM), sn, 0.0)

    shp = jax.ShapeDtypeStruct((s, 128), F32)
    return pl.pallas_call(body, out_shape=[shp, shp, shp], name=name,
                          compiler_params=pltpu.CompilerParams(vmem_limit_bytes=VMEM_LIMIT))(pos_col, inv_freq_lane)


def _rope(t, c, s1, s2):
    half = ROT_DIM // 2
    return t * c + pltpu.roll(t, 128 - half, 1) * s1 + pltpu.roll(t, half, 1) * s2


def _rope_t(g, c, s1, s2):
    half = ROT_DIM // 2
    return g * c + pltpu.roll(g * s1, half, 1) + pltpu.roll(g * s2, 128 - half, 1)


def _att_valid(b):
    qi = lax.broadcasted_iota(jnp.int32, (ATT_BLOCK, 2 * ATT_BLOCK), 0)
    kj = lax.broadcasted_iota(jnp.int32, (ATT_BLOCK, 2 * ATT_BLOCK), 1)
    rel = qi + ATT_BLOCK - kj
    return (rel >= 0) & (rel <= ATT_BLOCK) & (b * ATT_BLOCK + kj - ATT_BLOCK >= 0)


def _att_slices(i, d):
    if d == 1:
        qstart = pl.multiple_of(i * ATT_BLOCK, ATT_BLOCK)
        return i, pl.ds(qstart, ATT_BLOCK), pl.ds(pl.multiple_of(qstart - ATT_BLOCK + ATT_KPAD, ATT_BLOCK), 2 * ATT_BLOCK)
    r = i % d
    b = i // d
    qstart = r + d * ATT_BLOCK * b
    return b, pl.ds(qstart, ATT_BLOCK, stride=d), pl.ds(qstart - ATT_BLOCK * d + ATT_KPAD, 2 * ATT_BLOCK, stride=d)


_ATT_NBLK = SEQ // ATT_BLOCK
_ATT_SCALE = HEAD_LANES ** -0.5
_ATT_UNROLL = 4


def _att_fwd(proj, cos, sin1, sin2, mix, *, name):
    s = SEQ

    def body(q_ref, k_ref, v_ref, c_ref, s1_ref, s2_ref, _, o_ref, lse_ref, mix_ref, qs, ks, vs, acc, m_s, l_s):
        c, s1, s2 = c_ref[...], s1_ref[...], s2_ref[...]
        qs[...] = _rope(q_ref[...], c, s1, s2) * _ATT_SCALE
        zeros = jnp.zeros((ATT_KPAD, 128), F32)
        ks[pl.ds(0, ATT_KPAD), :] = zeros
        vs[pl.ds(0, ATT_KPAD), :] = zeros
        ks[pl.ds(ATT_KPAD, s), :] = _rope(k_ref[...], c, s1, s2)
        vs[pl.ds(ATT_KPAD, s), :] = v_ref[...]
        head0 = _head_of_lane(128) == 0

        for bi, (_, d) in enumerate(ATT_PATTERNS):
            def blk(i, carry, d=d, first=(bi == 0)):
                b, sq, sk = _att_slices(i, d)
                qb = qs[sq, :]
                kw = ks[sk, :].astype(BF16)
                vw = vs[sk, :].astype(BF16)
                valid = _att_valid(b)
                ms, ls, os_ = [], [], []
                for hh in range(2):
                    qh = jnp.where(head0 if hh == 0 else ~head0, qb, 0.0).astype(BF16)
                    sc = lax.dot_general(qh, kw, (((1,), (1,)), ((), ())), preferred_element_type=F32)
                    sc = jnp.where(valid, sc, -jnp.inf)
                    mb = jnp.max(sc, axis=1, keepdims=True)
                    p = jnp.exp(sc - mb)
                    ms.append(mb)
                    ls.append(jnp.sum(p, axis=1, keepdims=True))
                    os_.append(jnp.dot(p.astype(BF16), vw, preferred_element_type=F32))
                m_b = jnp.where(head0, ms[0], ms[1])
                l_b = jnp.where(head0, ls[0], ls[1])
                o_b = jnp.where(head0, os_[0], os_[1])
                if first:
                    m_s[sq, :] = m_b
                    l_s[sq, :] = l_b
                    acc[sq, :] = o_b
                else:
                    m_old = m_s[sq, :]
                    m_new = jnp.maximum(m_old, m_b)
                    a_old = jnp.exp(m_old - m_new)
                    a_b = jnp.exp(m_b - m_new)
                    m_s[sq, :] = m_new
                    l_s[sq, :] = l_s[sq, :] * a_old + l_b * a_b
                    acc[sq, :] = acc[sq, :] * a_old + o_b * a_b
                return carry

            lax.fori_loop(0, _ATT_NBLK, blk, 0, unroll=_ATT_UNROLL)

        out = acc[...] / l_s[...]
        o_ref[...] = out
        mix_ref[...] = out.astype(BF16)
        lse_ref[...] = m_s[...] + jnp.log(l_s[...])

    col = lambda base: pl.BlockSpec((s, 128), lambda p: (0, base + p))
    tab = pl.BlockSpec((s, 128), lambda p: (0, 0))
    big = pltpu.VMEM((ATT_KPAD + s, 128), F32)
    tok = pltpu.VMEM((s, 128), F32)
    return pl.pallas_call(
        body, grid=(2,), in_specs=[col(PROJ_Q_BLK), col(PROJ_K_BLK), col(PROJ_V_BLK), tab, tab, tab, _HBM],
        out_specs=[pl.BlockSpec((s, 128), lambda p: (0, p)), pl.BlockSpec((s, 128), lambda p: (0, p)),
                   col(MIX_ATT_BLK)],
        out_shape=[jax.ShapeDtypeStruct((s, ATT_W), F32), jax.ShapeDtypeStruct((s, ATT_W), F32),
                   jax.ShapeDtypeStruct(mix.shape, mix.dtype)],
        input_output_aliases={6: 2}, scratch_shapes=[tok, big, big, tok, tok, tok], name=name,
        compiler_params=_cparams("arbitrary"))(proj, proj, proj, cos, sin1, sin2, mix)


def _att_bwd(proj, cos, sin1, sin2, out, lse, dmix, dproj, *, name):
    s = SEQ

    def body(proj_ref, c_hbm, s1_hbm, s2_hbm, out_hbm, lse_hbm, dmix_hbm, _, dproj_hbm,
             c_ref, s1_ref, s2_ref, o_ref, lse_ref, do_ref, qs, ks, vs, dqs, dks, dvs, staged, sems):
        def start(copies):
            for cp in copies:
                cp.start()
            return copies

        def load(pair):
            lanes = pl.ds(128 * pair, 128)
            rows = pl.ds(ATT_KPAD, s)
            return start([
                pltpu.make_async_copy(proj_ref.at[:, pl.ds(128 * (PROJ_Q_BLK + pair), 128)], qs, sems.at[0]),
                pltpu.make_async_copy(proj_ref.at[:, pl.ds(128 * (PROJ_K_BLK + pair), 128)], ks.at[rows, :], sems.at[1]),
                pltpu.make_async_copy(proj_ref.at[:, pl.ds(128 * (PROJ_V_BLK + pair), 128)], vs.at[rows, :], sems.at[2]),
                pltpu.make_async_copy(out_hbm.at[:, lanes], o_ref, sems.at[3]),
                pltpu.make_async_copy(lse_hbm.at[:, lanes], lse_ref, sems.at[4]),
                pltpu.make_async_copy(dmix_hbm.at[:, pl.ds(128 * (MIX_ATT_BLK + pair), 128)], do_ref, sems.at[5])])

        tables = start([pltpu.make_async_copy(c_hbm, c_ref, sems.at[6]),
                        pltpu.make_async_copy(s1_hbm, s1_ref, sems.at[7]),
                        pltpu.make_async_copy(s2_hbm, s2_ref, sems.at[8])])
        loads = load(0)
        for cp in tables:
            cp.wait()
        head0 = _head_of_lane(128) == 0
        zeros = jnp.zeros((ATT_KPAD, 128), F32)
        for pair in range(2):
            for cp in loads:
                cp.wait()
            c, s1, s2 = c_ref[...], s1_ref[...], s2_ref[...]
            qs[...] = _rope(qs[...], c, s1, s2) * _ATT_SCALE
            ks[pl.ds(0, ATT_KPAD), :] = zeros
            vs[pl.ds(0, ATT_KPAD), :] = zeros
            ks[pl.ds(ATT_KPAD, s), :] = _rope(ks[pl.ds(ATT_KPAD, s), :], c, s1, s2)
            dqs[...] = jnp.zeros_like(dqs)
            dks[...] = jnp.zeros_like(dks)
            dvs[...] = jnp.zeros_like(dvs)

            for _, d in ATT_PATTERNS:
                def blk(i, carry, d=d):
                    b, sq, sk = _att_slices(i, d)
                    qb = qs[sq, :]
                    kw = ks[sk, :].astype(BF16)
                    vw = vs[sk, :].astype(BF16)
                    dob = do_ref[sq, :]
                    lse_b = lse_ref[sq, :]
                    dd = dob * o_ref[sq, :]
                    valid = _att_valid(b)
                    dq_b = jnp.zeros((ATT_BLOCK, 128), F32)
                    dk_w = jnp.zeros((2 * ATT_BLOCK, 128), F32)
                    dv_w = jnp.zeros((2 * ATT_BLOCK, 128), F32)
                    for hh in range(2):
                        hm = head0 if hh == 0 else ~head0
                        qh = jnp.where(hm, qb, 0.0).astype(BF16)
                        doh = jnp.where(hm, dob, 0.0).astype(BF16)
                        lse_h = _pick_lane(lse_b, hh * HEAD_LANES)
                        d_h = jnp.sum(jnp.where(hm, dd, 0.0), axis=1, keepdims=True)
                        sc = lax.dot_general(qh, kw, (((1,), (1,)), ((), ())), preferred_element_type=F32)
                        p = jnp.where(valid, jnp.exp(sc - lse_h), 0.0)
                        dp = lax.dot_general(doh, vw, (((1,), (1,)), ((), ())), preferred_element_type=F32)
                        ds = (p * (dp - d_h)).astype(BF16)
                        dq_b = dq_b + jnp.where(hm, jnp.dot(ds, kw, preferred_element_type=F32), 0.0)
                        dk_w = dk_w + lax.dot_general(ds, qh, (((0,), (0,)), ((), ())), preferred_element_type=F32)
                        dv_w = dv_w + lax.dot_general(p.astype(BF16), doh, (((0,), (0,)), ((), ())),
                                                      preferred_element_type=F32)
                    dqs[sq, :] += dq_b
                    dks[sk, :] += dk_w
                    dvs[sk, :] += dv_w
                    return carry

                lax.fori_loop(0, _ATT_NBLK, blk, 0, unroll=_ATT_UNROLL)

            staged[0] = _rope_t(dqs[...] * _ATT_SCALE, c, s1, s2).astype(BF16)
            staged[1] = _rope_t(dks[pl.ds(ATT_KPAD, s), :], c, s1, s2).astype(BF16)
            staged[2] = dvs[pl.ds(ATT_KPAD, s), :].astype(BF16)
            stores = start([
                pltpu.make_async_copy(staged.at[j], dproj_hbm.at[:, pl.ds(128 * (col + pair), 128)], sems.at[9 + j])
                for j, col in enumerate((PROJ_Q_BLK, PROJ_K_BLK, PROJ_V_BLK))])
            if pair == 0:
                loads = load(1)
            for cp in stores:
                cp.wait()

    big = pltpu.VMEM((ATT_KPAD + s, 128), F32)
    tok = pltpu.VMEM((s, 128), F32)
    return pl.pallas_call(
        body, in_specs=[_HBM] * 8, out_specs=_HBM, out_shape=jax.ShapeDtypeStruct(dproj.shape, dproj.dtype),
        input_output_aliases={7: 0},
        scratch_shapes=[tok] * 6 + [tok, big, big, tok, big, big, pltpu.VMEM((3, s, 128), BF16),
                                    pltpu.SemaphoreType.DMA((12,))], name=name,
        compiler_params=pltpu.CompilerParams(vmem_limit_bytes=VMEM_LIMIT))(
            proj, cos, sin1, sin2, out, lse, dmix, dproj)


_POOL_TM = 512
_POOL_NT = SEQ // _POOL_TM
_POOL_HALO_PER_TILE = _POOL_TM // POOL_HALO


def _pool_tile(u_ext, w_bd, scale, t0):
    s2 = u_ext + _roll_rows(u_ext, 1)
    s4 = s2 + _roll_rows(s2, 2)
    s8 = s4 + _roll_rows(s4, 4)
    s16 = s8 + _roll_rows(s8, 8)
    grp = _head_of_lane(POOL_W)
    sel = jnp.where(grp == 0, s2, jnp.where(grp == 1, s4, jnp.where(grp == 2, s8, s16)))[POOL_HALO:]
    t = sel.shape[0]
    pos = t0 + lax.broadcasted_iota(jnp.int32, (t, POOL_W), 0) + 1
    win = jnp.where(grp == 0, 2, jnp.where(grp == 1, 4, jnp.where(grp == 2, 8, 16)))
    cnt = jnp.minimum(pos, win).astype(F32)
    diff = sel / cnt - u_ext[POOL_HALO:]
    return jnp.dot(diff.astype(BF16), w_bd.astype(BF16), preferred_element_type=F32) * scale


def _pool_fwd(proj, w_bd, scale, mix, *, name):
    tm = _POOL_TM

    def body(u_ref, halo_ref, w_ref, sc_ref, _, y_ref):
        i = pl.program_id(0)
        halo = jnp.where(i == 0, 0.0, halo_ref[...])
        u_ext = jnp.concatenate([halo, u_ref[...]], axis=0)
        y_ref[...] = _pool_tile(u_ext, w_ref[...], sc_ref[...], i * tm).astype(BF16)

    return pl.pallas_call(
        body, grid=(_POOL_NT,),
        in_specs=[pl.BlockSpec((tm, POOL_W), lambda i: (i, PROJ_POOL_BLK)),
                  pl.BlockSpec((POOL_HALO, POOL_W),
                               lambda i: (jnp.maximum(i * _POOL_HALO_PER_TILE - 1, 0), PROJ_POOL_BLK)),
                  pl.BlockSpec((POOL_W, POOL_W), lambda i: (0, 0)), pl.BlockSpec((1, POOL_W), lambda i: (0, 0)), _HBM],
        out_specs=pl.BlockSpec((tm, POOL_W), lambda i: (i, MIX_POOL_BLK)),
        out_shape=jax.ShapeDtypeStruct(mix.shape, mix.dtype), input_output_aliases={4: 0}, name=name,
        compiler_params=_cparams("parallel"))(proj, proj, w_bd, scale, mix)


def _pool_bwd(proj, dmix, w_bd, scale, dproj, *, name):
    tm = _POOL_TM
    last = _POOL_NT - 1

    def body(u_ref, halo_ref, dy_ref, w_ref, sc_ref, _, du_ref, dw_ref, dsc_ref, dhalo_acc):
        i = pl.program_id(0)

        @pl.when(i == 0)
        def _():
            dhalo_acc[...] = jnp.zeros_like(dhalo_acc)
            dw_ref[...] = jnp.zeros_like(dw_ref)
            dsc_ref[...] = jnp.zeros_like(dsc_ref)

        tile = last - i
        halo = jnp.where(tile == 0, 0.0, halo_ref[...])
        u_ext = jnp.concatenate([halo, u_ref[...]], axis=0)
        _, vjp = jax.vjp(functools.partial(_pool_tile, t0=tile * tm), u_ext, w_ref[...], sc_ref[...])
        gu, gw, gs = vjp(dy_ref[...])
        du_ref[...] = jnp.concatenate([gu[POOL_HALO:tm], gu[tm:] + dhalo_acc[...]], axis=0).astype(BF16)
        dhalo_acc[...] = gu[:POOL_HALO]
        dw_ref[...] += gw
        dsc_ref[...] += gs

    rev = lambda i: last - i
    return pl.pallas_call(
        body, grid=(_POOL_NT,),
        in_specs=[pl.BlockSpec((tm, POOL_W), lambda i: (rev(i), PROJ_POOL_BLK)),
                  pl.BlockSpec((POOL_HALO, POOL_W),
                               lambda i: (jnp.maximum(rev(i) * _POOL_HALO_PER_TILE - 1, 0), PROJ_POOL_BLK)),
                  pl.BlockSpec((tm, POOL_W), lambda i: (rev(i), MIX_POOL_BLK)),
                  pl.BlockSpec((POOL_W, POOL_W), lambda i: (0, 0)), pl.BlockSpec((1, POOL_W), lambda i: (0, 0)), _HBM],
        out_specs=[pl.BlockSpec((tm, POOL_W), lambda i: (rev(i), PROJ_POOL_BLK)),
                   pl.BlockSpec((POOL_W, POOL_W), lambda i: (0, 0)), pl.BlockSpec((1, POOL_W), lambda i: (0, 0))],
        out_shape=[jax.ShapeDtypeStruct(dproj.shape, dproj.dtype), jax.ShapeDtypeStruct((POOL_W, POOL_W), F32),
                   jax.ShapeDtypeStruct((1, POOL_W), F32)],
        input_output_aliases={5: 0}, scratch_shapes=[pltpu.VMEM((POOL_HALO, POOL_W), F32)], name=name,
        compiler_params=_cparams("arbitrary"))(proj, proj, dmix, w_bd, scale, dproj)


_FFN_TM = 256
_FFN_NT = SEQ // _FFN_TM
_FFN_HALO_PER_TILE = _FFN_TM // CONV_HALO


def _ffn_act_tile(hid_ext, conv_w, conv_b):
    hc = conv_b
    for k in range(FFN_CONV_K):
        hc = hc + _pick_row(conv_w, k) * _shift_rows(hid_ext, FFN_CONV_K - 1 - k, CONV_HALO)
    return _silu(hc[:, :FFN_DIM]) * hc[:, FFN_DIM:]


def _ffn_act(hid, conv_w, conv_b, *, name):
    tm = _FFN_TM
    w = 2 * FFN_DIM

    def body(h_ref, halo_ref, cw_ref, cb_ref, a_ref):
        i = pl.program_id(0)
        halo = jnp.where(i == 0, 0.0, halo_ref[...])
        hid_ext = jnp.concatenate([halo, h_ref[...]], axis=0)
        a_ref[...] = _ffn_act_tile(hid_ext, cw_ref[...], cb_ref[...]).astype(BF16)

    return pl.pallas_call(
        body, grid=(_FFN_NT,),
        in_specs=[pl.BlockSpec((tm, w), lambda i: (i, 0)),
                  pl.BlockSpec((CONV_HALO, w), lambda i: (jnp.maximum(i * _FFN_HALO_PER_TILE - 1, 0), 0)),
                  pl.BlockSpec((8, w), lambda i: (0, 0)), pl.BlockSpec((1, w), lambda i: (0, 0))],
        out_specs=pl.BlockSpec((tm, FFN_DIM), lambda i: (i, 0)),
        out_shape=jax.ShapeDtypeStruct((SEQ, FFN_DIM), BF16), name=name,
        compiler_params=_cparams("parallel"))(hid, hid, conv_w, conv_b)


def _ffn_act_bwd(hid, dact, conv_w, conv_b, *, name):
    tm = _FFN_TM
    w = 2 * FFN_DIM
    last = _FFN_NT - 1

    def body(h_ref, halo_ref, da_ref, cw_ref, cb_ref, dh_ref, dcw_ref, dcb_ref, dhalo_acc):
        i = pl.program_id(0)

        @pl.when(i == 0)
        def _():
            dhalo_acc[...] = jnp.zeros_like(dhalo_acc)
            dcw_ref[...] = jnp.zeros_like(dcw_ref)
            dcb_ref[...] = jnp.zeros_like(dcb_ref)

        halo = jnp.where(i == last, 0.0, halo_ref[...])
        hid_ext = jnp.concatenate([halo, h_ref[...]], axis=0)
        _, vjp = jax.vjp(_ffn_act_tile, hid_ext, cw_ref[...], cb_ref[...])
        gh, gw, gb = vjp(da_ref[...])
        carry = dhalo_acc[...]
        dhalo_acc[...] = gh[:CONV_HALO]
        dh_ref[...] = jnp.concatenate([gh[CONV_HALO:tm], gh[tm:] + carry], axis=0).astype(BF16)
        dcw_ref[...] += gw
        dcb_ref[...] += gb

    rev = lambda i: last - i
    return pl.pallas_call(
        body, grid=(_FFN_NT,),
        in_specs=[pl.BlockSpec((tm, w), lambda i: (rev(i), 0)),
                  pl.BlockSpec((CONV_HALO, w), lambda i: (jnp.maximum(rev(i) * _FFN_HALO_PER_TILE - 1, 0), 0)),
                  pl.BlockSpec((tm, FFN_DIM), lambda i: (rev(i), 0)),
                  pl.BlockSpec((8, w), lambda i: (0, 0)), pl.BlockSpec((1, w), lambda i: (0, 0))],
        out_specs=[pl.BlockSpec((tm, w), lambda i: (rev(i), 0)),
                   pl.BlockSpec((8, w), lambda i: (0, 0)), pl.BlockSpec((1, w), lambda i: (0, 0))],
        out_shape=[jax.ShapeDtypeStruct((SEQ, w), BF16), jax.ShapeDtypeStruct((8, w), F32),
                   jax.ShapeDtypeStruct((1, w), F32)],
        scratch_shapes=[pltpu.VMEM((CONV_HALO, w), F32)], name=name,
        compiler_params=_cparams("arbitrary"))(hid, hid, dact, conv_w, conv_b)


def _axes():
    return lax.axis_index("x"), lax.axis_index("y"), lax.axis_index("c")


def _handshake(peers):
    barrier = pltpu.get_barrier_semaphore()
    for peer in peers:
        pl.semaphore_signal(barrier, inc=1, device_id=peer, device_id_type=MESH)
    pl.semaphore_wait(barrier, len(peers))


def _allgather_body(x_refs, out_refs, send_sems, recv_sems, local_sems, own_barrier):
    n = len(x_refs)
    x, y, c = _axes()
    me, sibling = (x, y, c), (x, y, 1 - c)
    chips = [(1 - x, y), (x, 1 - y), (1 - x, 1 - y)]
    if own_barrier:
        _handshake([sibling] + [(*chip, c) for chip in chips])

    def slot(a, px, py, pc):
        return out_refs[a].at[4 * px + 2 * py + pc]

    def copy(a, k, block, to, src=None):
        return pltpu.make_async_remote_copy(
            src_ref=slot(a, *block) if src is None else src, dst_ref=slot(a, *block),
            send_sem=send_sems.at[a, k], recv_sem=recv_sems.at[a, k], device_id=to, device_id_type=MESH)

    mines, firsts = [], []
    for a in range(n):
        mines.append(pltpu.make_async_copy(x_refs[a], slot(a, *me), local_sems.at[a]))
        mines[-1].start()
        first = [copy(a, 0, me, sibling, src=x_refs[a])]
        first += [copy(a, 1 + j, me, (*chip, c), src=x_refs[a]) for j, chip in enumerate(chips)]
        for cp in first:
            cp.start()
        firsts += first
    passed = []
    for j, chip in enumerate(chips):
        for a in range(n):
            copy(a, 1 + j, (*chip, c), me).wait_recv()
            passed.append(copy(a, 4 + j, (*chip, c), sibling))
            passed[-1].start()
    for a in range(n):
        copy(a, 0, sibling, me).wait_recv()
    for j, chip in enumerate(chips):
        for a in range(n):
            copy(a, 4 + j, (*chip, 1 - c), me).wait_recv()
    for cp in firsts + passed:
        cp.wait_send()
    for cp in mines:
        cp.wait()


def _allgather_sems(n):
    return [pltpu.SemaphoreType.DMA((n, 7)), pltpu.SemaphoreType.DMA((n, 7)), pltpu.SemaphoreType.DMA((n,))]


def _allgather(xs, *, name):
    n = len(xs)

    def body(*refs):
        _allgather_body(refs[:n], refs[n:2 * n], *refs[2 * n:], own_barrier=False)

    return pl.pallas_call(
        body, out_shape=[jax.ShapeDtypeStruct((N_DEV,) + xb.shape, xb.dtype) for xb in xs],
        in_specs=[_HBM] * n, out_specs=[_HBM] * n, scratch_shapes=_allgather_sems(n), name=name)(*xs)


def _allgather_async(xs, *, name, collective_id):
    n = len(xs)
    x_refs = [jax.new_ref(xb, memory_space=pltpu.MemorySpace.HBM) for xb in xs]
    out_refs = [jax.empty_ref(jax.ShapeDtypeStruct((N_DEV,) + xb.shape, xb.dtype), memory_space=pltpu.MemorySpace.HBM)
                for xb in xs]

    @pl.kernel(mesh=plsc.ScalarSubcoreMesh(axis_name="sequencer", num_cores=1), name=name,
               scratch_types=tuple(_allgather_sems(n)),
               compiler_params=pltpu.CompilerParams(collective_id=collective_id))
    def launch(send_sems, recv_sems, local_sems):
        _allgather_body(x_refs, out_refs, send_sems, recv_sems, local_sems, own_barrier=True)

    launch()
    return [r[...] for r in out_refs]


def _pair_exchange(blocks, *, name, collective_id):
    n = len(blocks)
    hbm = pltpu.MemorySpace.HBM
    in_refs = [jax.new_ref(b, memory_space=hbm) for b in blocks]
    out_refs = [jax.empty_ref(jax.ShapeDtypeStruct((4,) + b.shape[1:], b.dtype), memory_space=hbm) for b in blocks]

    @pl.kernel(mesh=plsc.ScalarSubcoreMesh(axis_name="sequencer", num_cores=1), name=name,
               scratch_types=(pltpu.SemaphoreType.DMA((n, 4)), pltpu.SemaphoreType.DMA((n, 4))),
               compiler_params=pltpu.CompilerParams(collective_id=collective_id))
    def launch(send_sems, recv_sems):
        x, y, c = _axes()
        _handshake([(x, y, 1 - c)])
        copies = [pltpu.make_async_remote_copy(
            src_ref=in_refs[a].at[2 * s + (1 - c)], dst_ref=out_refs[a].at[s], send_sem=send_sems.at[a, s],
            recv_sem=recv_sems.at[a, s], device_id=(x, y, 1 - c), device_id_type=MESH)
            for a in range(n) for s in range(4)]
        for cp in copies:
            cp.start()
        for cp in copies:
            cp.wait_recv()
        for cp in copies:
            cp.wait_send()

    launch()
    return [r[...] for r in out_refs]


def _chip_exchange(parts, *, name, collective_id):
    n = len(parts)
    hbm = pltpu.MemorySpace.HBM
    in_refs = [jax.new_ref(p, memory_space=hbm) for p in parts]
    out_refs = [jax.empty_ref(jax.ShapeDtypeStruct(p.shape, p.dtype), memory_space=hbm) for p in parts]

    @pl.kernel(mesh=plsc.ScalarSubcoreMesh(axis_name="sequencer", num_cores=1), name=name,
               scratch_types=(pltpu.SemaphoreType.DMA((n, 3)), pltpu.SemaphoreType.DMA((n, 3)),
                              pltpu.SemaphoreType.DMA((n,))),
               compiler_params=pltpu.CompilerParams(collective_id=collective_id))
    def launch(send_sems, recv_sems, local_sems):
        x, y, c = _axes()
        my_chip = 2 * x + y
        chips = [(1 - x, y), (x, 1 - y), (1 - x, 1 - y)]
        _handshake([(*chip, c) for chip in chips])
        locals_ = [pltpu.make_async_copy(in_refs[a].at[my_chip], out_refs[a].at[my_chip], local_sems.at[a])
                   for a in range(n)]
        for cp in locals_:
            cp.start()
        copies = [pltpu.make_async_remote_copy(
            src_ref=in_refs[a].at[2 * px + py], dst_ref=out_refs[a].at[my_chip], send_sem=send_sems.at[a, k],
            recv_sem=recv_sems.at[a, k], device_id=(px, py, c), device_id_type=MESH)
            for a in range(n) for k, (px, py) in enumerate(chips)]
        for cp in copies:
            cp.start()
        for cp in copies:
            cp.wait_recv()
        for cp in copies:
            cp.wait_send()
        for cp in locals_:
            cp.wait()

    launch()
    return [r[...] for r in out_refs]


def _pair_sum(core, blocks, from_sibling, *, name):
    _, r, cdim = blocks.shape

    def body(core_ref, a_ref, b_ref, o_ref):
        o_ref[...] = (a_ref[...].astype(F32) + b_ref[...].astype(F32)).astype(o_ref.dtype)

    return pl.pallas_call(
        body,
        grid_spec=pltpu.PrefetchScalarGridSpec(
            num_scalar_prefetch=1, grid=(4,),
            in_specs=[pl.BlockSpec((1, r, cdim), lambda s, core_ref: (2 * s + core_ref[0], 0, 0)),
                      pl.BlockSpec((1, r, cdim), lambda s, core_ref: (s, 0, 0))],
            out_specs=pl.BlockSpec((1, r, cdim), lambda s, core_ref: (s, 0, 0))),
        out_shape=jax.ShapeDtypeStruct(from_sibling.shape, from_sibling.dtype), name=name,
        compiler_params=_cparams("parallel"))(core, blocks, from_sibling)


def _sum_blocks(a, *, name, tr=None):
    n, r, cdim = a.shape
    tr = tr or r

    def body(a_ref, o_ref):
        acc = a_ref[0].astype(F32)
        for k in range(1, n):
            acc = acc + a_ref[k].astype(F32)
        o_ref[...] = acc

    return pl.pallas_call(body, grid=(r // tr,), in_specs=[pl.BlockSpec((n, tr, cdim), lambda i: (0, i, 0))],
                          out_specs=pl.BlockSpec((tr, cdim), lambda i: (i, 0)),
                          out_shape=jax.ShapeDtypeStruct((r, cdim), F32), name=name,
                          compiler_params=_cparams("parallel"))(a)


_ADA_SHARD = 6 * D_MODEL // N_DEV


def _ada_mod(c_all, ada_w, *, name):
    def body(c_ref, w_ref, o_ref):
        o_ref[0] = jnp.dot(_silu(c_ref[...]).astype(BF16), w_ref[0].astype(BF16), preferred_element_type=F32)

    return pl.pallas_call(
        body, grid=(DEPTH,),
        in_specs=[pl.BlockSpec((N_DEV, D_MODEL), lambda l: (0, 0)),
                  pl.BlockSpec((1, D_MODEL, _ADA_SHARD), lambda l: (l, 0, 0))],
        out_specs=pl.BlockSpec((1, N_DEV, _ADA_SHARD), lambda l: (l, 0, 0)),
        out_shape=jax.ShapeDtypeStruct((DEPTH, N_DEV, _ADA_SHARD), F32), name=name,
        compiler_params=_cparams("parallel"))(c_all, ada_w)


def _ada_wgrad(c_all, dmod_cols, *, name):
    def body(c_ref, d_ref, o_ref):
        o_ref[0] = lax.dot_general(_silu(c_ref[...]), d_ref[0], (((0,), (0,)), ((), ())),
                                   preferred_element_type=F32, precision=lax.Precision.HIGHEST)

    return pl.pallas_call(
        body, grid=(DEPTH,),
        in_specs=[pl.BlockSpec((N_DEV, D_MODEL), lambda l: (0, 0)),
                  pl.BlockSpec((1, N_DEV, _ADA_SHARD), lambda l: (l, 0, 0))],
        out_specs=pl.BlockSpec((1, D_MODEL, _ADA_SHARD), lambda l: (l, 0, 0)),
        out_shape=jax.ShapeDtypeStruct((DEPTH, D_MODEL, _ADA_SHARD), F32), name=name,
        compiler_params=_cparams("parallel"))(c_all, dmod_cols)


def _add_rows(a, b, *, name):
    def body(a_ref, b_ref, o_ref):
        o_ref[...] = a_ref[...] + b_ref[...]

    return pl.pallas_call(body, out_shape=jax.ShapeDtypeStruct(a.shape, a.dtype), name=name)(a, b)


def _adamw(w, g, m, v, *, name, tr):
    r, cdim = w.shape
    c1 = 1.0 - ADAM_B1 ** ADAM_STEP
    c2 = 1.0 - ADAM_B2 ** ADAM_STEP

    def body(w_ref, g_ref, m_ref, v_ref, d_ref, mo_ref, vo_ref):
        gv = g_ref[...]
        mn = ADAM_B1 * m_ref[...] + (1.0 - ADAM_B1) * gv
        vn = ADAM_B2 * v_ref[...] + (1.0 - ADAM_B2) * (gv * gv)
        mo_ref[...] = mn
        vo_ref[...] = vn
        d_ref[...] = -ADAM_LR * ((mn / c1) / (jnp.sqrt(vn / c2) + ADAM_EPS) + ADAM_WD * w_ref[...])

    spec = pl.BlockSpec((tr, cdim), lambda i: (i, 0))
    shp = jax.ShapeDtypeStruct((r, cdim), F32)
    return pl.pallas_call(body, grid=(r // tr,), in_specs=[spec] * 4, out_specs=[spec] * 3, out_shape=[shp] * 3,
                          name=name, compiler_params=_cparams("parallel"))(w, g, m, v)


def _pad_rows(a, rows):
    return jnp.concatenate([a, jnp.zeros((rows - a.shape[0],) + a.shape[1:], a.dtype)], axis=0)


def _pad_lanes(a, lanes):
    return jnp.concatenate([a, jnp.zeros(a.shape[:-1] + (lanes - a.shape[-1],), a.dtype)], axis=-1)


def _permute_w_in(wt):
    return jnp.concatenate([wt[512:1536], wt[:512], wt[1536:1544],
                            jnp.zeros((PROJ_W - IN_W, wt.shape[1]), wt.dtype), wt[1544:]], axis=0)


def _unpermute_w_in(wp):
    return jnp.concatenate([wp[1024:1536], wp[:1024], wp[1536:1544], wp[PROJ_SSD_W:]], axis=0)


def _block_diag(w):
    rows = []
    for g in range(4):
        rows.append(jnp.concatenate([w[g] if k == g else jnp.zeros_like(w[g]) for k in range(4)], axis=1))
    return jnp.concatenate(rows, axis=0)


def _diag_blocks(wbd):
    return jnp.stack([wbd[64 * g:64 * (g + 1), 64 * g:64 * (g + 1)] for g in range(4)], axis=0)


def _layer_params(l, small):
    return dict(
        norm1_g=small["norm1_g"][l][None], norm2_g=small["norm2_g"][l][None],
        conv_w=_pad_rows(small["ssd_conv_w"][l], 8), conv_b=small["ssd_conv_b"][l][None],
        dt_bias=_pad_lanes(small["ssd_dt_bias"][l][None], 128), a_log=_pad_lanes(small["ssd_a_log"][l][None], 128),
        d_skip=_pad_lanes(small["ssd_d"][l][None], 128), ssd_norm_g=small["ssd_norm_g"][l][None],
        pool_bd=_block_diag(small["pool_w"][l]), pool_scale=small["pool_scale"][l][None],
        fcw=_pad_rows(small["ffn_conv_w"][l], 8), fcb=small["ffn_conv_b"][l][None])


def _mod_rows(mod_l):
    return [mod_l[None, D_MODEL * i:D_MODEL * (i + 1)] for i in range(6)]


def _layer_fwd(x, mod_l, p, tabs, l, gather):
    sh1, sc1, g1, sh2, sc2, g2 = _mod_rows(mod_l)
    h1 = _norm_mod(x, p["norm1_g"], sc1, sh1, name=f"l{l}_norm1")
    mix_w = gather(l, "mix", None)
    p.update(w_in=mix_w["w_in"], w_out=mix_w["w_out"])
    proj = _mm(h1, p["w_in"], nt=True, name=f"l{l}_proj")
    ffn_w = gather(l, "ffn", proj)
    p.update(up=ffn_w["ffn_up"], down=ffn_w["ffn_down"])
    mix, hst = _ssd_fwd(proj, p["conv_w"], p["conv_b"], p["dt_bias"], p["a_log"], p["d_skip"], p["ssd_norm_g"],
                        name=f"l{l}_ssd")
    mix = _pool_fwd(proj, p["pool_bd"], p["pool_scale"], mix, name=f"l{l}_pool")
    y_att, lse, mix = _att_fwd(proj, *tabs, mix, name=f"l{l}_att")
    mo = _mm(mix, p["w_out"], name=f"l{l}_out")
    x1 = _residual(x, g1, mo, name=f"l{l}_res1")
    h2 = _norm_mod(x1, p["norm2_g"], sc2, sh2, name=f"l{l}_norm2")
    hid = _mm(h2, p["up"], nt=True, tm=256, name=f"l{l}_up")
    gather(l + 1, "mix", hid)
    act = _ffn_act(hid, p["fcw"], p["fcb"], name=f"l{l}_act")
    f = _mm(act, p["down"], name=f"l{l}_down")
    x2 = _residual(x1, g2, f, name=f"l{l}_res2")
    return x2, dict(x=x, h1=h1, proj=proj, hst=hst, y_att=y_att, lse=lse, mix=mix, mo=mo, x1=x1, h2=h2, hid=hid,
                    act=act, f=f)


def _layer_bwd(dx2, sv, mod_l, p, tabs, l, exchange):
    sh1, sc1, g1, sh2, sc2, g2 = _mod_rows(mod_l)
    df, dg2 = _residual_bwd(dx2, g2, sv["f"], name=f"l{l}_res2_b")
    dact = _mm(df, p["down"], nt=True, tn=1408, name=f"l{l}_down_bx")
    d_down = _wgrad(sv["act"], df, name=f"l{l}_down_bw")
    dhid, dfcw, dfcb = _ffn_act_bwd(sv["hid"], dact, p["fcw"], p["fcb"], name=f"l{l}_act_b")
    dh2 = _mm(dhid, p["up"], tm=256, name=f"l{l}_up_bx")
    d_up = _wgrad(dhid, sv["h2"], tk=1408, name=f"l{l}_up_bw")
    exchange(l, "ffn", dict(ffn_up=d_up, ffn_down=d_down))
    dx1, dn2, dsc2, dsh2 = _norm_mod_bwd(sv["x1"], dh2, dx2, p["norm2_g"], sc2, name=f"l{l}_norm2_b")
    dmo, dg1 = _residual_bwd(dx1, g1, sv["mo"], name=f"l{l}_res1_b")
    dmix = _mm(dmo, p["w_out"], nt=True, name=f"l{l}_out_bx")
    d_wout = _wgrad(sv["mix"], dmo, name=f"l{l}_out_bw")
    dproj, dcw, dcb, ddb, dal, dd, dng = _ssd_bwd(
        sv["proj"], sv["hst"], dmix, p["conv_w"], p["conv_b"], p["dt_bias"], p["a_log"], p["d_skip"],
        p["ssd_norm_g"], name=f"l{l}_ssd_b")
    dproj, dwbd, dpsc = _pool_bwd(sv["proj"], dmix, p["pool_bd"], p["pool_scale"], dproj, name=f"l{l}_pool_b")
    dproj = _att_bwd(sv["proj"], *tabs, sv["y_att"], sv["lse"], dmix, dproj, name=f"l{l}_att_b")
    dh1 = _mm(dproj, p["w_in"], name=f"l{l}_proj_bx")
    d_win = _wgrad(dproj, sv["h1"], name=f"l{l}_proj_bw")
    exchange(l, "mix", dict(w_in=d_win, w_out=d_wout))
    dx0, dn1, dsc1, dsh1 = _norm_mod_bwd(sv["x"], dh1, dx1, p["norm1_g"], sc1, name=f"l{l}_norm1_b")
    dmod = jnp.concatenate([dsh1, dsc1, dg1, dsh2, dsc2, dg2], axis=1)[0]
    small = dict(norm1_g=dn1[0], ssd_conv_w=dcw[:SSD_CONV_K], ssd_conv_b=dcb[0], ssd_dt_bias=ddb[0], ssd_a_log=dal[0],
                 ssd_d=dd[0], ssd_norm_g=dng[0], pool_w=_diag_blocks(dwbd), pool_scale=dpsc[0], norm2_g=dn2[0],
                 ffn_conv_w=dfcw[:FFN_CONV_K], ffn_conv_b=dfcb[0])
    return dx0, dmod, small


def _example_step(x, target, pos_col, inv_freq_lane, mod, gather, small, final_g, exchange):
    tabs = _rope_tables(pos_col, inv_freq_lane, name="rope_tables")
    params, saved = [], []
    for l in range(DEPTH):
        params.append(_layer_params(l, small))
        x, sv = _layer_fwd(x, mod[l], params[l], tabs, l, gather)
        saved.append(sv)
    loss_row, dx, dfg = _final_loss(x, final_g[None], target, name="final_loss")
    dmods, smalls = [None] * DEPTH, [None] * DEPTH
    for l in reversed(range(DEPTH)):
        dx, dmods[l], smalls[l] = _layer_bwd(dx, saved[l], mod[l], params[l], tabs, l, exchange)
    return loss_row, dx, jnp.stack(dmods, axis=0), smalls, dfg[0]


_BIG = (("w_in", (D_MODEL, IN_W // N_DEV)), ("w_out", (D_MODEL // N_DEV, D_MODEL)),
        ("ffn_up", (D_MODEL, 2 * FFN_DIM // N_DEV)), ("ffn_down", (FFN_DIM // N_DEV, D_MODEL)))
_PACK_LANES = 1024

_SMALL_GRADS = (("norm1_g", D_MODEL), ("ssd_conv_w", SSD_CONV_K * SSD_CONV_CH), ("ssd_conv_b", SSD_CONV_CH),
                ("ssd_dt_bias", 128), ("ssd_a_log", 128), ("ssd_d", 128), ("ssd_norm_g", SSD_INNER),
                ("pool_w", 4 * 64 * 64), ("pool_scale", POOL_W), ("norm2_g", D_MODEL),
                ("ffn_conv_w", FFN_CONV_K * 2 * FFN_DIM), ("ffn_conv_b", 2 * FFN_DIM))
_SMALL_PARAMS = ("ada_b", "norm1_g", "ssd_conv_w", "ssd_conv_b", "ssd_dt_bias", "ssd_a_log", "ssd_d", "ssd_norm_g",
                 "pool_w", "pool_scale", "norm2_g", "ffn_conv_w", "ffn_conv_b", "final_g")
_WEIGHT_ORDER = ("ada_w", "ada_b", "norm1_g", "w_in", "ssd_conv_w", "ssd_conv_b", "ssd_dt_bias", "ssd_a_log", "ssd_d",
                 "ssd_norm_g", "pool_w", "pool_scale", "w_out", "norm2_g", "ffn_up", "ffn_conv_w", "ffn_conv_b",
                 "ffn_down", "final_g")


def _pack(parts, rows, dtype):
    flat = jnp.concatenate([p.reshape(-1).astype(dtype) for p in parts])
    return jnp.concatenate([flat, jnp.zeros((rows * _PACK_LANES - flat.shape[0],), dtype)]).reshape(rows, _PACK_LANES)


_COLUMN_SHARDED = ("w_in", "ffn_up")
_GROUPS = (("mix", ("w_in", "w_out")), ("ffn", ("ffn_up", "ffn_down")))


def _big_shares(w, l, names):
    return [(w[name][l].T if name in _COLUMN_SHARDED else w[name][l]).astype(BF16) for name in names]


def _unshard_big(names, gathered):
    out = {}
    for name, g in zip(names, gathered):
        full = g.reshape(N_DEV * g.shape[1], g.shape[2])
        out[name] = _permute_w_in(full) if name == "w_in" else full
    return out


def _shard_big(grads):
    out = []
    for name, g in grads.items():
        g = _unpermute_w_in(g) if name == "w_in" else g
        out.append(g.reshape(N_DEV, g.shape[0] // N_DEV, g.shape[1]))
    return out


def kernel(x, c, positions, ada_w, ada_b, norm1_g, w_in, ssd_conv_w, ssd_conv_b, ssd_dt_bias, ssd_a_log, ssd_d, ssd_norm_g, pool_w, pool_scale, w_out, norm2_g, ffn_up, ffn_conv_w, ffn_conv_b, ffn_down, final_g, loss_target, m_ada_w, m_ada_b, m_norm1_g, m_w_in, m_ssd_conv_w, m_ssd_conv_b, m_ssd_dt_bias, m_ssd_a_log, m_ssd_d, m_ssd_norm_g, m_pool_w, m_pool_scale, m_w_out, m_norm2_g, m_ffn_up, m_ffn_conv_w, m_ffn_conv_b, m_ffn_down, m_final_g, v_ada_w, v_ada_b, v_norm1_g, v_w_in, v_ssd_conv_w, v_ssd_conv_b, v_ssd_dt_bias, v_ssd_a_log, v_ssd_d, v_ssd_norm_g, v_pool_w, v_pool_scale, v_w_out, v_norm2_g, v_ffn_up, v_ffn_conv_w, v_ffn_conv_b, v_ffn_down, v_final_g):
    w = dict(ada_w=ada_w, ada_b=ada_b, norm1_g=norm1_g, w_in=w_in, ssd_conv_w=ssd_conv_w, ssd_conv_b=ssd_conv_b,
             ssd_dt_bias=ssd_dt_bias, ssd_a_log=ssd_a_log, ssd_d=ssd_d, ssd_norm_g=ssd_norm_g, pool_w=pool_w,
             pool_scale=pool_scale, w_out=w_out, norm2_g=norm2_g, ffn_up=ffn_up, ffn_conv_w=ffn_conv_w,
             ffn_conv_b=ffn_conv_b, ffn_down=ffn_down, final_g=final_g)
    m = dict(ada_w=m_ada_w, ada_b=m_ada_b, norm1_g=m_norm1_g, w_in=m_w_in, ssd_conv_w=m_ssd_conv_w,
             ssd_conv_b=m_ssd_conv_b, ssd_dt_bias=m_ssd_dt_bias, ssd_a_log=m_ssd_a_log, ssd_d=m_ssd_d,
             ssd_norm_g=m_ssd_norm_g, pool_w=m_pool_w, pool_scale=m_pool_scale, w_out=m_w_out, norm2_g=m_norm2_g,
             ffn_up=m_ffn_up, ffn_conv_w=m_ffn_conv_w, ffn_conv_b=m_ffn_conv_b, ffn_down=m_ffn_down,
             final_g=m_final_g)
    v = dict(ada_w=v_ada_w, ada_b=v_ada_b, norm1_g=v_norm1_g, w_in=v_w_in, ssd_conv_w=v_ssd_conv_w,
             ssd_conv_b=v_ssd_conv_b, ssd_dt_bias=v_ssd_dt_bias, ssd_a_log=v_ssd_a_log, ssd_d=v_ssd_d,
             ssd_norm_g=v_ssd_norm_g, pool_w=v_pool_w, pool_scale=v_pool_scale, w_out=v_w_out, norm2_g=v_norm2_g,
             ffn_up=v_ffn_up, ffn_conv_w=v_ffn_conv_w, ffn_conv_b=v_ffn_conv_b, ffn_down=v_ffn_down,
             final_g=v_final_g)
    ix, iy, ic = _axes()
    dev = 4 * ix + 2 * iy + ic

    n_scw, n_fcw = ssd_conv_w.size, ffn_conv_w.size
    small_all = _allgather([_pack([c, ssd_conv_w, ffn_conv_w], 8, F32)], name="gather_small")[0].reshape(N_DEV, -1)
    c_all = small_all[:, :D_MODEL]
    scw = small_all[:, D_MODEL:D_MODEL + n_scw].reshape(N_DEV, DEPTH, SSD_CONV_K, -1)
    scw = scw.transpose(1, 2, 0, 3).reshape(DEPTH, SSD_CONV_K, SSD_CONV_CH)
    fcw = small_all[:, D_MODEL + n_scw:D_MODEL + n_scw + n_fcw].reshape(N_DEV, DEPTH, FFN_CONV_K, -1)
    fcw = fcw.transpose(1, 2, 0, 3).reshape(DEPTH, FFN_CONV_K, 2 * FFN_DIM)

    mod_cols = _ada_mod(c_all, ada_w, name="ada_mod")
    mod_all = _allgather([mod_cols.reshape(DEPTH * N_DEV, _ADA_SHARD)], name="gather_mod")[0]
    mod_all = mod_all.reshape(N_DEV, DEPTH, N_DEV, _ADA_SHARD)
    mod_mine = lax.dynamic_index_in_dim(mod_all, dev, axis=2, keepdims=False)
    mod = _add_rows(mod_mine.transpose(1, 0, 2).reshape(DEPTH, 6 * D_MODEL), ada_b, name="ada_bias")

    fetched = {}

    def gather(l, group, after):
        if l < DEPTH and (l, group) not in fetched:
            names = dict(_GROUPS)[group]
            shares, _ = lax.optimization_barrier((_big_shares(w, l, names), small_all if after is None else after))
            got = _allgather_async(shares, name=f"gather_weights_l{l}_{group}",
                                   collective_id=1 + 2 * l + (group == "ffn"))
            fetched[l, group] = _unshard_big(names, got)
        return fetched.get((l, group))

    core = ic.astype(jnp.int32).reshape(1)
    from_chips = {}

    def exchange(l, group, g):
        cid = 5 + 4 * l + 2 * (group == "mix")
        blocks = _shard_big(g)
        from_sibling = _pair_exchange(blocks, name=f"grads_pair_exchange_l{l}_{group}", collective_id=cid)
        parts = [_pair_sum(core, b, s, name=f"grads_pair_sum_l{l}_{name}")
                 for name, b, s in zip(g, blocks, from_sibling)]
        got = _chip_exchange(parts, name=f"grads_chip_exchange_l{l}_{group}", collective_id=cid + 1)
        from_chips.update({(l, name): t for name, t in zip(g, got)})

    small = dict(norm1_g=norm1_g, norm2_g=norm2_g, ssd_conv_w=scw, ssd_conv_b=ssd_conv_b, ssd_dt_bias=ssd_dt_bias,
                 ssd_a_log=ssd_a_log, ssd_d=ssd_d, ssd_norm_g=ssd_norm_g, pool_w=pool_w, pool_scale=pool_scale,
                 ffn_conv_w=fcw, ffn_conv_b=ffn_conv_b)

    inv_freq = ROPE_THETA ** (-jnp.arange(0, ROT_DIM, 2, dtype=F32) / ROT_DIM)
    lane = jnp.arange(128) % HEAD_LANES
    inv_freq_lane = jnp.where(lane < ROT_DIM, inv_freq[lane % (ROT_DIM // 2)], 0.0)[None, :]
    pos_col = positions.reshape(SEQ, 1).astype(F32)
    loss_row, dx, dmod, g_small, g_final = _example_step(
        x[0], loss_target[0], pos_col, inv_freq_lane, mod, gather, small, final_g, exchange)

    grads = {}
    for name, _ in _BIG:
        per_layer = [_sum_blocks(from_chips[l, name], name=f"grads_chip_sum_l{l}_{name}") for l in range(DEPTH)]
        grads[name] = jnp.stack([g.T if name in _COLUMN_SHARDED else g for g in per_layer], axis=0)

    small_parts = [loss_row, dmod] + [jnp.stack([g_small[l][name] for l in range(DEPTH)], axis=0)
                                      for name, _ in _SMALL_GRADS] + [g_final]
    n_small = sum(p.size for p in small_parts)
    small_rows = -(-n_small // _PACK_LANES)
    gathered = _allgather([_pack(small_parts, small_rows, F32)], name="gather_small_grads")[0]
    total = _sum_blocks(gathered, name="sum_small_grads").reshape(-1)
    loss = total[0]
    off = 128
    grads["ada_b"] = total[off:off + DEPTH * 6 * D_MODEL].reshape(DEPTH, 6 * D_MODEL)
    dmod_all = gathered.reshape(N_DEV, -1)[:, off:off + DEPTH * 6 * D_MODEL].reshape(N_DEV, DEPTH, 6 * D_MODEL)
    off += DEPTH * 6 * D_MODEL
    for name, n in _SMALL_GRADS:
        grads[name] = total[off:off + DEPTH * n].reshape(DEPTH, n)
        off += DEPTH * n
    grads["final_g"] = total[off:off + D_MODEL]
    dmod_cols = lax.dynamic_slice_in_dim(dmod_all, dev * _ADA_SHARD, _ADA_SHARD, axis=2).transpose(1, 0, 2)
    grads["ada_w"] = _ada_wgrad(c_all, dmod_cols, name="ada_wgrad")
    for name in ("ssd_dt_bias", "ssd_a_log", "ssd_d"):
        grads[name] = grads[name][:, :SSD_HEADS]
    grads["pool_w"] = grads["pool_w"].reshape(pool_w.shape)
    grads["ssd_conv_w"] = lax.dynamic_slice_in_dim(
        grads["ssd_conv_w"].reshape(DEPTH, SSD_CONV_K, SSD_CONV_CH), dev * ssd_conv_w.shape[2], ssd_conv_w.shape[2], axis=2)
    grads["ffn_conv_w"] = lax.dynamic_slice_in_dim(
        grads["ffn_conv_w"].reshape(DEPTH, FFN_CONV_K, 2 * FFN_DIM), dev * ffn_conv_w.shape[2], ffn_conv_w.shape[2], axis=2)

    delta, new_m, new_v = {}, {}, {}
    for name, tr in (("ada_w", 512), ("w_in", 512), ("w_out", 256), ("ffn_up", 512), ("ffn_down", 352)):
        shp = w[name].shape
        two_d = lambda a: a.reshape(shp[0] * shp[1], shp[2])
        d_, m_, v_ = _adamw(two_d(w[name]), two_d(grads[name]), two_d(m[name]), two_d(v[name]), tr=tr,
                               name=f"adamw_{name}")
        delta[name], new_m[name], new_v[name] = (t.reshape(shp) for t in (d_, m_, v_))
    n_packed = sum(w[name].size for name in _SMALL_PARAMS)
    rows = -(-n_packed // _PACK_LANES)
    packed = [_pack([t[name] for name in _SMALL_PARAMS], rows, F32) for t in (w, grads, m, v)]
    outs = [o.reshape(-1) for o in _adamw(*packed, tr=rows, name="adamw_small")]
    off = 0
    for name in _SMALL_PARAMS:
        n = w[name].size
        delta[name], new_m[name], new_v[name] = (o[off:off + n].reshape(w[name].shape) for o in outs)
        off += n

    grad_x = dx[None]
    return (loss, grad_x, *[grads[n].reshape(w[n].shape) for n in _WEIGHT_ORDER],
            *[delta[n] for n in _WEIGHT_ORDER], *[new_m[n] for n in _WEIGHT_ORDER],
            *[new_v[n] for n in _WEIGHT_ORDER])
```

```python
import functools
import math

import jax
import jax.numpy as jnp
from jax import lax
from jax.experimental import pallas as pl
from jax.experimental.pallas import tpu as pltpu
from jax.experimental.pallas import tpu_sc as plsc

F32 = jnp.float32
BF16 = jnp.bfloat16

N_DEV = 8
D_MODEL = 1024
SEQ = 4096
DEPTH = 2
SSD_INNER = 512
SSD_HEADS = 8
SSD_HPG = 4
SSD_STATE = 128
SSD_CHUNK = 256
SSD_CONV_K = 4
SSD_CONV_CH = 1024
POOL_W = 256
POOL_WINDOWS = (2, 4, 8, 16)
ATT_W = 256
ATT_PATTERNS = ((128, 1), (512, 4), (2048, 16))
ATT_BLOCK = 128
ROT_DIM = 16
ROPE_THETA = 500000.0
IN_W = 2568
FFN_DIM = 2816
FFN_CONV_K = 3
NORM_EPS = 1e-6
HEAD_LANES = 64

ADAM_LR = 0.001
ADAM_B1 = 0.9
ADAM_B2 = 0.999
ADAM_EPS = 1e-08
ADAM_WD = 0.01
ADAM_STEP = 10

PROJ_W = 2816
PROJ_SSD_W = 1792
PROJ_Z_BLK = 2
PROJ_DT_BLK = 12
PROJ_POOL_BLK = 7
PROJ_Q_BLK, PROJ_K_BLK, PROJ_V_BLK = 16, 18, 20
MIX_POOL_BLK = 2
MIX_ATT_BLK = 6
VMEM_LIMIT = 56 * 1024 * 1024
CONV_HALO = 8
POOL_HALO = 16
ATT_KPAD = ATT_BLOCK * 16
MESH = pl.DeviceIdType.MESH
_HBM = pl.BlockSpec(memory_space=pl.ANY)


def _cparams(*sem):
    return pltpu.CompilerParams(dimension_semantics=sem, vmem_limit_bytes=VMEM_LIMIT)


def _silu(x):
    return x * jax.nn.sigmoid(x)


def _pick_lane(v, h):
    lane = lax.broadcasted_iota(jnp.int32, v.shape, 1)
    return jnp.sum(jnp.where(lane == h, v, 0.0), axis=1, keepdims=True)


def _pick_row(v, h):
    row = lax.broadcasted_iota(jnp.int32, v.shape, 0)
    return jnp.sum(jnp.where(row == h, v, 0.0), axis=0, keepdims=True)


def _head_of_lane(width):
    return lax.broadcasted_iota(jnp.int32, (1, width), 1) // HEAD_LANES


@functools.partial(jax.custom_vjp, nondiff_argnums=(1, 2))
def _shift_rows(x_ext, s, halo):
    y = x_ext if s == 0 else pltpu.roll(x_ext, s, 0)
    return y[halo:]


def _shift_rows_fwd(x_ext, s, halo):
    return _shift_rows(x_ext, s, halo), None


def _shift_rows_bwd(s, halo, _, g):
    ge = jnp.concatenate([jnp.zeros((halo, g.shape[1]), g.dtype), g], axis=0)
    return (ge if s == 0 else pltpu.roll(ge, ge.shape[0] - s, 0),)


_shift_rows.defvjp(_shift_rows_fwd, _shift_rows_bwd)


@functools.partial(jax.custom_vjp, nondiff_argnums=(1,))
def _roll_rows(x, s):
    return pltpu.roll(x, s, 0)


def _roll_rows_fwd(x, s):
    return _roll_rows(x, s), None


def _roll_rows_bwd(s, _, g):
    return (pltpu.roll(g, g.shape[0] - s, 0),)


_roll_rows.defvjp(_roll_rows_fwd, _roll_rows_bwd)


def _rms_modulate(xv, g, sc, sh):
    r = lax.rsqrt(jnp.mean(xv * xv, axis=-1, keepdims=True) + NORM_EPS)
    return (xv * r * g) * (1.0 + sc) + sh


def _mm(a, w, *, name, nt=False, tm=512, tn=None, out_dtype=F32, norm=None, residual=None):
    t, k = a.shape
    n = w.shape[0] if nt else w.shape[1]
    tn = tn or n
    assert tn == n or (norm is None and residual is None)
    extra_in = list(norm or ()) + list(residual or ())

    def body(*refs):
        a_ref, w_ref = refs[:2]
        ins = refs[2:2 + len(extra_in)]
        outs = refs[2 + len(extra_in):]
        if norm is None:
            av = a_ref[...].astype(BF16)
        else:
            av = _rms_modulate(a_ref[...], ins[0][...], ins[1][...], ins[2][...]).astype(BF16)
            outs[1][...] = av
        if nt:
            acc = lax.dot_general(av, w_ref[...], (((1,), (1,)), ((), ())), preferred_element_type=F32)
        else:
            acc = jnp.dot(av, w_ref[...], preferred_element_type=F32)
        outs[0][...] = acc.astype(out_dtype)
        if residual is not None:
            x_ref, gate_ref = ins[-2:]
            outs[-1][...] = x_ref[...] + gate_ref[...] * acc

    row = lambda width: pl.BlockSpec((1, width), lambda i, j: (0, 0))
    tile = lambda width: pl.BlockSpec((tm, width), lambda i, j: (i, 0))
    w_spec = pl.BlockSpec((tn, k), lambda i, j: (j, 0)) if nt else pl.BlockSpec((k, tn), lambda i, j: (0, j))
    in_specs = [tile(k), w_spec] + ([row(k)] * 3 if norm else []) + ([tile(n), row(n)] if residual else [])
    out_specs = [pl.BlockSpec((tm, tn), lambda i, j: (i, j))] + ([tile(k)] if norm else []) + \
        ([tile(n)] if residual else [])
    out_shape = [jax.ShapeDtypeStruct((t, n), out_dtype)] + \
        ([jax.ShapeDtypeStruct((t, k), BF16)] if norm else []) + \
        ([jax.ShapeDtypeStruct((t, n), F32)] if residual else [])
    outs = pl.pallas_call(
        body, grid=(t // tm, n // tn), in_specs=in_specs, out_specs=out_specs, out_shape=out_shape, name=name,
        compiler_params=_cparams("parallel", "parallel"))(a, w, *extra_in)
    return outs[0] if len(outs) == 1 else outs


def _wgrad(a, b, *, name, tk=None, tn=None, tt=512, out_dtype=BF16):
    t, k = a.shape
    n = b.shape[1]
    tk = tk or k
    tn = tn or n
    steps = t // tt

    def body(a_ref, b_ref, o_ref, acc_ref):
        s = pl.program_id(2)

        @pl.when(s == 0)
        def _():
            acc_ref[...] = jnp.zeros_like(acc_ref)

        acc_ref[...] += lax.dot_general(a_ref[...].astype(BF16), b_ref[...].astype(BF16),
                                        (((0,), (0,)), ((), ())), preferred_element_type=F32)

        @pl.when(s == steps - 1)
        def _():
            o_ref[...] = acc_ref[...].astype(out_dtype)

    return pl.pallas_call(
        body, grid=(k // tk, n // tn, steps),
        in_specs=[pl.BlockSpec((tt, tk), lambda i, j, s: (s, i)), pl.BlockSpec((tt, tn), lambda i, j, s: (s, j))],
        out_specs=pl.BlockSpec((tk, tn), lambda i, j, s: (i, j)),
        out_shape=jax.ShapeDtypeStruct((k, n), out_dtype),
        scratch_shapes=[pltpu.VMEM((tk, tn), F32)], name=name,
        compiler_params=_cparams("parallel", "parallel", "arbitrary"))(a, b)


def _norm_mod_bwd(x, dh, dres, g, sc, *, name, tm=512):
    s, d = x.shape
    steps = s // tm

    def body(x_ref, dh_ref, dres_ref, g_ref, sc_ref, dx_ref, dg_ref, dsc_ref, dsh_ref, da_acc, dsh_acc):
        i = pl.program_id(0)

        @pl.when(i == 0)
        def _():
            da_acc[...] = jnp.zeros_like(da_acc)
            dsh_acc[...] = jnp.zeros_like(dsh_acc)

        xv = x_ref[...]
        dhv = dh_ref[...].astype(F32)
        r = lax.rsqrt(jnp.mean(xv * xv, axis=-1, keepdims=True) + NORM_EPS)
        xhat = xv * r
        gain = g_ref[...] * (1.0 + sc_ref[...])
        dxhat = dhv * gain
        dx_ref[...] = dres_ref[...] + r * (dxhat - xhat * jnp.mean(dxhat * xhat, axis=-1, keepdims=True))
        da_acc[...] += jnp.sum(dhv * xhat, axis=0, keepdims=True)
        dsh_acc[...] += jnp.sum(dhv, axis=0, keepdims=True)

        @pl.when(i == steps - 1)
        def _():
            dg_ref[...] = da_acc[...] * (1.0 + sc_ref[...])
            dsc_ref[...] = da_acc[...] * g_ref[...]
            dsh_ref[...] = dsh_acc[...]

    row = pl.BlockSpec((1, d), lambda i: (0, 0))
    tile = pl.BlockSpec((tm, d), lambda i: (i, 0))
    row_shape = jax.ShapeDtypeStruct((1, d), F32)
    return pl.pallas_call(
        body, grid=(steps,), in_specs=[tile, tile, tile, row, row],
        out_specs=[tile, row, row, row],
        out_shape=[jax.ShapeDtypeStruct((s, d), F32), row_shape, row_shape, row_shape],
        scratch_shapes=[pltpu.VMEM((1, d), F32), pltpu.VMEM((1, d), F32)], name=name,
        compiler_params=_cparams("arbitrary"))(x, dh, dres, g, sc)


def _residual_bwd(dx, gate, f, *, name, tm=512):
    s, d = dx.shape
    steps = s // tm

    def body(dx_ref, g_ref, f_ref, df_ref, dg_ref):
        i = pl.program_id(0)

        @pl.when(i == 0)
        def _():
            dg_ref[...] = jnp.zeros_like(dg_ref)

        dxv = dx_ref[...]
        df_ref[...] = (g_ref[...] * dxv).astype(BF16)
        dg_ref[...] += jnp.sum(dxv * f_ref[...], axis=0, keepdims=True)

    tile = pl.BlockSpec((tm, d), lambda i: (i, 0))
    row = pl.BlockSpec((1, d), lambda i: (0, 0))
    return pl.pallas_call(
        body, grid=(steps,), in_specs=[tile, row, tile], out_specs=[tile, row],
        out_shape=[jax.ShapeDtypeStruct((s, d), BF16), jax.ShapeDtypeStruct((1, d), F32)], name=name,
        compiler_params=_cparams("arbitrary"))(dx, gate, f)


def _final_loss(x, g, target, *, name, tm=512):
    s, d = x.shape
    steps = s // tm

    def body(x_ref, g_ref, t_ref, loss_ref, dx_ref, dg_ref, sq_acc):
        i = pl.program_id(0)

        @pl.when(i == 0)
        def _():
            sq_acc[...] = jnp.zeros_like(sq_acc)
            dg_ref[...] = jnp.zeros_like(dg_ref)

        xv = x_ref[...]
        r = lax.rsqrt(jnp.mean(xv * xv, axis=-1, keepdims=True) + NORM_EPS)
        xhat = xv * r
        err = xhat * g_ref[...] - t_ref[...]
        sq_acc[...] += jnp.sum(err * err, axis=0, keepdims=True)
        dy = err * (1.0 / d)
        dg_ref[...] += jnp.sum(dy * xhat, axis=0, keepdims=True)
        dxhat = dy * g_ref[...]
        dx_ref[...] = r * (dxhat - xhat * jnp.mean(dxhat * xhat, axis=-1, keepdims=True))

        @pl.when(i == steps - 1)
        def _():
            total = jnp.sum(sq_acc[...], axis=1, keepdims=True) * (0.5 / d)
            loss_ref[...] = jnp.broadcast_to(total, loss_ref.shape)

    tile = pl.BlockSpec((tm, d), lambda i: (i, 0))
    row = pl.BlockSpec((1, d), lambda i: (0, 0))
    return pl.pallas_call(
        body, grid=(steps,), in_specs=[tile, row, tile],
        out_specs=[pl.BlockSpec((1, 128), lambda i: (0, 0)), tile, row],
        out_shape=[jax.ShapeDtypeStruct((1, 128), F32), jax.ShapeDtypeStruct((s, d), F32),
                   jax.ShapeDtypeStruct((1, d), F32)],
        scratch_shapes=[pltpu.VMEM((1, d), F32)], name=name, compiler_params=_cparams("arbitrary"))(x, g, target)


def _ssd_chunk(z, xbc_ext, dt_raw, conv_w, conv_b, dt_bias, a_log, d_skip, norm_g, h_in):
    q = z.shape[0]
    gw = SSD_HPG * HEAD_LANES
    xc = conv_b
    for k in range(SSD_CONV_K):
        xc = xc + _pick_row(conv_w, k) * _shift_rows(xbc_ext, SSD_CONV_K - 1 - k, CONV_HALO)
    xc = _silu(xc)
    dt = jax.nn.softplus(dt_raw + dt_bias)
    da = dt * (-jnp.exp(a_log))
    ri = lax.broadcasted_iota(jnp.int32, (q, q), 0)
    ci = lax.broadcasted_iota(jnp.int32, (q, q), 1)
    causal = ri >= ci
    tril = causal.astype(F32)
    a_cum = jnp.dot(tril, da, preferred_element_type=F32, precision=lax.Precision.HIGHEST)
    a_cum_t = lax.dot_general(da, tril, (((0,), (1,)), ((), ())), preferred_element_type=F32,
                              precision=lax.Precision.HIGHEST)
    a_last = _pick_row(a_cum, q - 1)
    head = _head_of_lane(gw)
    ys, hs = [], []
    for g in range(2):
        xs = xc[:, gw * g:gw * (g + 1)]
        bm = xc[:, SSD_INNER + SSD_STATE * g:SSD_INNER + SSD_STATE * (g + 1)]
        cm = xc[:, SSD_INNER + 2 * SSD_STATE + SSD_STATE * g:SSD_INNER + 2 * SSD_STATE + SSD_STATE * (g + 1)]
        cb = lax.dot_general(cm.astype(BF16), bm.astype(BF16), (((1,), (1,)), ((), ())), preferred_element_type=F32)
        cols = [_pick_lane(a_cum, SSD_HPG * g + j) for j in range(SSD_HPG)]
        lasts = [_pick_lane(a_last, SSD_HPG * g + j) for j in range(SSD_HPG)]
        dt_exp = sum(jnp.where(head == j, _pick_lane(dt, SSD_HPG * g + j), 0.0) for j in range(SSD_HPG))
        d_exp = sum(jnp.where(head == j, _pick_lane(d_skip, SSD_HPG * g + j), 0.0) for j in range(SSD_HPG))
        e_cum = sum(jnp.where(head == j, jnp.exp(cols[j]), 0.0) for j in range(SSD_HPG))
        c_dec = sum(jnp.where(head == j, jnp.exp(lasts[j]), 0.0) for j in range(SSD_HPG))
        xsdt = (xs * dt_exp).astype(BF16)
        y_diag = jnp.zeros((q, gw), F32)
        st_new = jnp.zeros((SSD_STATE, gw), F32)
        for j in range(SSD_HPG):
            row = _pick_row(a_cum_t, SSD_HPG * g + j)
            lmat = jnp.exp(jnp.where(causal, cols[j] - row, -jnp.inf))
            r = jnp.dot((cb * lmat).astype(BF16), xsdt, preferred_element_type=F32)
            y_diag = y_diag + jnp.where(head == j, r, 0.0)
            bd = (bm * jnp.exp(lasts[j] - cols[j])).astype(BF16)
            st = lax.dot_general(bd, xsdt, (((0,), (0,)), ((), ())), preferred_element_type=F32)
            st_new = st_new + jnp.where(head == j, st, 0.0)
        y_off = jnp.dot(cm.astype(BF16), h_in[g].astype(BF16), preferred_element_type=F32) * e_cum
        hs.append(h_in[g] * c_dec + st_new)
        y = y_diag + y_off + d_exp * xs
        yz = y * _silu(z[:, gw * g:gw * (g + 1)])
        yz = yz * lax.rsqrt(jnp.mean(yz * yz, axis=-1, keepdims=True) + NORM_EPS)
        ys.append(yz * norm_g[:, gw * g:gw * (g + 1)])
    return jnp.concatenate(ys, axis=1), tuple(hs)


_SSD_NCHUNK = SEQ // SSD_CHUNK
_HALO_PER_CHUNK = SSD_CHUNK // CONV_HALO


def _ssd_param_specs(const):
    return [pl.BlockSpec((8, SSD_CONV_CH), const), pl.BlockSpec((1, SSD_CONV_CH), const),
            pl.BlockSpec((1, 128), const), pl.BlockSpec((1, 128), const), pl.BlockSpec((1, 128), const),
            pl.BlockSpec((1, SSD_INNER), const)]


def _ssd_fwd(proj, conv_w, conv_b, dt_bias, a_log, d_skip, norm_g, *, name):
    q = SSD_CHUNK

    def body(z_ref, xbc_ref, halo_ref, dt_ref, cw_ref, cb_ref, db_ref, al_ref, d_ref, ng_ref, y_ref, hs_ref, h_acc):
        i = pl.program_id(0)

        @pl.when(i == 0)
        def _():
            h_acc[...] = jnp.zeros_like(h_acc)

        halo = jnp.where(i == 0, 0.0, halo_ref[...])
        xbc_ext = jnp.concatenate([halo, xbc_ref[...]], axis=0)
        h_in = (h_acc[0], h_acc[1])
        hs_ref[0, 0] = h_in[0]
        hs_ref[0, 1] = h_in[1]
        y, h_out = _ssd_chunk(z_ref[...], xbc_ext, dt_ref[...], cw_ref[...], cb_ref[...], db_ref[...], al_ref[...],
                              d_ref[...], ng_ref[...], h_in)
        y_ref[...] = y.astype(BF16)
        h_acc[0] = h_out[0]
        h_acc[1] = h_out[1]

    const = lambda i: (0, 0)
    return pl.pallas_call(
        body, grid=(_SSD_NCHUNK,),
        in_specs=[pl.BlockSpec((q, SSD_INNER), lambda i: (i, PROJ_Z_BLK)),
                  pl.BlockSpec((q, SSD_CONV_CH), lambda i: (i, 0)),
                  pl.BlockSpec((CONV_HALO, SSD_CONV_CH), lambda i: (jnp.maximum(i * _HALO_PER_CHUNK - 1, 0), 0)),
                  pl.BlockSpec((q, 128), lambda i: (i, PROJ_DT_BLK))] + _ssd_param_specs(const),
        out_specs=[pl.BlockSpec((q, SSD_INNER), lambda i: (i, 0)),
                   pl.BlockSpec((1, 2, SSD_STATE, 256), lambda i: (i, 0, 0, 0))],
        out_shape=[jax.ShapeDtypeStruct((SEQ, D_MODEL), BF16),
                   jax.ShapeDtypeStruct((_SSD_NCHUNK, 2, SSD_STATE, 256), F32)],
        scratch_shapes=[pltpu.VMEM((2, SSD_STATE, 256), F32)], name=name,
        compiler_params=_cparams("arbitrary"))(proj, proj, proj, proj, conv_w, conv_b, dt_bias, a_log, d_skip, norm_g)


def _ssd_bwd(proj, hstates, dmix, conv_w, conv_b, dt_bias, a_log, d_skip, norm_g, *, name):
    q = SSD_CHUNK
    last = _SSD_NCHUNK - 1

    def body(z_ref, xbc_ref, halo_ref, dt_ref, hs_ref, dy_ref, cw_ref, cb_ref, db_ref, al_ref, d_ref, ng_ref,
             dp_ref, dcw_ref, dcb_ref, ddb_ref, dal_ref, dd_ref, dng_ref, dh_acc, dhalo_acc):
        i = pl.program_id(0)

        @pl.when(i == 0)
        def _():
            dh_acc[...] = jnp.zeros_like(dh_acc)
            dhalo_acc[...] = jnp.zeros_like(dhalo_acc)
            for r in (dcw_ref, dcb_ref, ddb_ref, dal_ref, dd_ref, dng_ref):
                r[...] = jnp.zeros_like(r)

        halo = jnp.where(i == last, 0.0, halo_ref[...])
        xbc_ext = jnp.concatenate([halo, xbc_ref[...]], axis=0)
        _, vjp = jax.vjp(_ssd_chunk, z_ref[...], xbc_ext, dt_ref[...], cw_ref[...], cb_ref[...], db_ref[...],
                         al_ref[...], d_ref[...], ng_ref[...], (hs_ref[0, 0], hs_ref[0, 1]))
        gz, gx, gdt, gcw, gcb, gdb, gal, gd, gng, gh = vjp((dy_ref[...], (dh_acc[0], dh_acc[1])))
        dxbc = jnp.concatenate([gx[CONV_HALO:q], gx[q:] + dhalo_acc[...]], axis=0)
        dp_ref[...] = jnp.concatenate([dxbc, gz, gdt, jnp.zeros_like(gdt)], axis=1).astype(BF16)
        dhalo_acc[...] = gx[:CONV_HALO]
        dh_acc[0] = gh[0]
        dh_acc[1] = gh[1]
        dcw_ref[...] += gcw
        dcb_ref[...] += gcb
        ddb_ref[...] += gdb
        dal_ref[...] += gal
        dd_ref[...] += gd
        dng_ref[...] += gng

    const = lambda i: (0, 0)
    rev = lambda i: last - i
    row = lambda n: jax.ShapeDtypeStruct((1, n), F32)
    return pl.pallas_call(
        body, grid=(_SSD_NCHUNK,),
        in_specs=[pl.BlockSpec((q, SSD_INNER), lambda i: (rev(i), PROJ_Z_BLK)),
                  pl.BlockSpec((q, SSD_CONV_CH), lambda i: (rev(i), 0)),
                  pl.BlockSpec((CONV_HALO, SSD_CONV_CH), lambda i: (jnp.maximum(rev(i) * _HALO_PER_CHUNK - 1, 0), 0)),
                  pl.BlockSpec((q, 128), lambda i: (rev(i), PROJ_DT_BLK)),
                  pl.BlockSpec((1, 2, SSD_STATE, 256), lambda i: (rev(i), 0, 0, 0)),
                  pl.BlockSpec((q, SSD_INNER), lambda i: (rev(i), 0))] + _ssd_param_specs(const),
        out_specs=[pl.BlockSpec((q, PROJ_SSD_W), lambda i: (rev(i), 0))] + _ssd_param_specs(const),
        out_shape=[jax.ShapeDtypeStruct((SEQ, PROJ_W), BF16), jax.ShapeDtypeStruct((8, SSD_CONV_CH), F32),
                   row(SSD_CONV_CH), row(128), row(128), row(128), row(SSD_INNER)],
        scratch_shapes=[pltpu.VMEM((2, SSD_STATE, 256), F32), pltpu.VMEM((CONV_HALO, SSD_CONV_CH), F32)], name=name,
        compiler_params=_cparams("arbitrary"))(proj, proj, proj, proj, hstates, dmix, conv_w, conv_b, dt_bias, a_log,
                                                d_skip, norm_g)


def _rope_tables(pos_col, inv_freq_lane, *, name):
    s = pos_col.shape[0]

    def body(p_ref, f_ref, c_ref, s1_ref, s2_ref):
        ang = p_ref[...] * f_ref[...]
        within = lax.broadcasted_iota(jnp.int32, ang.shape, 1) % HEAD_LANES
        half = ROT_DIM // 2
        c_ref[...] = jnp.where(within < ROT_DIM, jnp.cos(ang), 1.0)
        sn = jnp.sin(ang)
        s1_ref[...] = jnp.where(within < half, -sn, 0.0)
        s2_ref[...] = jnp.where((within >= half) & (within < ROT_DIM), sn, 0.0)

    shp = jax.ShapeDtypeStruct((s, 128), F32)
    return pl.pallas_call(body, out_shape=[shp, shp, shp], name=name,
                          compiler_params=pltpu.CompilerParams(vmem_limit_bytes=VMEM_LIMIT))(pos_col, inv_freq_lane)


def _rope(t, c, s1, s2):
    half = ROT_DIM // 2
    return t * c + pltpu.roll(t, 128 - half, 1) * s1 + pltpu.roll(t, half, 1) * s2


def _rope_t(g, c, s1, s2):
    half = ROT_DIM // 2
    return g * c + pltpu.roll(g * s1, half, 1) + pltpu.roll(g * s2, 128 - half, 1)


def _att_valid(b):
    qi = lax.broadcasted_iota(jnp.int32, (ATT_BLOCK, 2 * ATT_BLOCK), 0)
    kj = lax.broadcasted_iota(jnp.int32, (ATT_BLOCK, 2 * ATT_BLOCK), 1)
    rel = qi + ATT_BLOCK - kj
    return (rel >= 0) & (rel <= ATT_BLOCK) & (b * ATT_BLOCK + kj - ATT_BLOCK >= 0)


def _att_slices(i, d):
    if d == 1:
        qstart = pl.multiple_of(i * ATT_BLOCK, ATT_BLOCK)
        return i, pl.ds(qstart, ATT_BLOCK), pl.ds(pl.multiple_of(qstart - ATT_BLOCK + ATT_KPAD, ATT_BLOCK), 2 * ATT_BLOCK)
    r = i % d
    b = i // d
    qstart = r + d * ATT_BLOCK * b
    return b, pl.ds(qstart, ATT_BLOCK, stride=d), pl.ds(qstart - ATT_BLOCK * d + ATT_KPAD, 2 * ATT_BLOCK, stride=d)


_ATT_NBLK = SEQ // ATT_BLOCK
_ATT_SCALE = HEAD_LANES ** -0.5
_ATT_UNROLL = 4


def _att_fwd(proj, cos, sin1, sin2, mix, *, name):
    s = SEQ

    def body(q_ref, k_ref, v_ref, c_ref, s1_ref, s2_ref, _, o_ref, lse_ref, mix_ref, qs, ks, vs, acc, m_s, l_s):
        c, s1, s2 = c_ref[...], s1_ref[...], s2_ref[...]
        qs[...] = _rope(q_ref[...], c, s1, s2) * _ATT_SCALE
        zeros = jnp.zeros((ATT_KPAD, 128), F32)
        ks[pl.ds(0, ATT_KPAD), :] = zeros
        vs[pl.ds(0, ATT_KPAD), :] = zeros
        ks[pl.ds(ATT_KPAD, s), :] = _rope(k_ref[...], c, s1, s2)
        vs[pl.ds(ATT_KPAD, s), :] = v_ref[...]
        head0 = _head_of_lane(128) == 0

        for bi, (_, d) in enumerate(ATT_PATTERNS):
            def blk(i, carry, d=d, first=(bi == 0)):
                b, sq, sk = _att_slices(i, d)
                qb = qs[sq, :]
                kw = ks[sk, :].astype(BF16)
                vw = vs[sk, :].astype(BF16)
                valid = _att_valid(b)
                ms, ls, os_ = [], [], []
                for hh in range(2):
                    qh = jnp.where(head0 if hh == 0 else ~head0, qb, 0.0).astype(BF16)
                    sc = lax.dot_general(qh, kw, (((1,), (1,)), ((), ())), preferred_element_type=F32)
                    sc = jnp.where(valid, sc, -jnp.inf)
                    mb = jnp.max(sc, axis=1, keepdims=True)
                    p = jnp.exp(sc - mb)
                    ms.append(mb)
                    ls.append(jnp.sum(p, axis=1, keepdims=True))
                    os_.append(jnp.dot(p.astype(BF16), vw, preferred_element_type=F32))
                m_b = jnp.where(head0, ms[0], ms[1])
                l_b = jnp.where(head0, ls[0], ls[1])
                o_b = jnp.where(head0, os_[0], os_[1])
                if first:
                    m_s[sq, :] = m_b
                    l_s[sq, :] = l_b
                    acc[sq, :] = o_b
                else:
                    m_old = m_s[sq, :]
                    m_new = jnp.maximum(m_old, m_b)
                    a_old = jnp.exp(m_old - m_new)
                    a_b = jnp.exp(m_b - m_new)
                    m_s[sq, :] = m_new
                    l_s[sq, :] = l_s[sq, :] * a_old + l_b * a_b
                    acc[sq, :] = acc[sq, :] * a_old + o_b * a_b
                return carry

            lax.fori_loop(0, _ATT_NBLK, blk, 0, unroll=_ATT_UNROLL)

        out = acc[...] / l_s[...]
        o_ref[...] = out
        mix_ref[...] = out.astype(BF16)
        lse_ref[...] = m_s[...] + jnp.log(l_s[...])

    col = lambda base: pl.BlockSpec((s, 128), lambda p: (0, base + p))
    tab = pl.BlockSpec((s, 128), lambda p: (0, 0))
    big = pltpu.VMEM((ATT_KPAD + s, 128), F32)
    tok = pltpu.VMEM((s, 128), F32)
    return pl.pallas_call(
        body, grid=(2,), in_specs=[col(PROJ_Q_BLK), col(PROJ_K_BLK), col(PROJ_V_BLK), tab, tab, tab, _HBM],
        out_specs=[pl.BlockSpec((s, 128), lambda p: (0, p)), pl.BlockSpec((s, 128), lambda p: (0, p)),
                   col(MIX_ATT_BLK)],
        out_shape=[jax.ShapeDtypeStruct((s, ATT_W), F32), jax.ShapeDtypeStruct((s, ATT_W), F32),
                   jax.ShapeDtypeStruct(mix.shape, mix.dtype)],
        input_output_aliases={6: 2}, scratch_shapes=[tok, big, big, tok, tok, tok], name=name,
        compiler_params=_cparams("arbitrary"))(proj, proj, proj, cos, sin1, sin2, mix)


def _att_bwd(proj, cos, sin1, sin2, out, lse, dmix, dproj, *, name):
    s = SEQ

    def body(proj_ref, c_hbm, s1_hbm, s2_hbm, out_hbm, lse_hbm, dmix_hbm, _, dproj_hbm,
             c_ref, s1_ref, s2_ref, o_ref, lse_ref, do_ref, qs, ks, vs, dqs, dks, dvs, staged, sems):
        def start(copies):
            for cp in copies:
                cp.start()
            return copies

        def load(pair):
            lanes = pl.ds(128 * pair, 128)
            rows = pl.ds(ATT_KPAD, s)
            return start([
                pltpu.make_async_copy(proj_ref.at[:, pl.ds(128 * (PROJ_Q_BLK + pair), 128)], qs, sems.at[0]),
                pltpu.make_async_copy(proj_ref.at[:, pl.ds(128 * (PROJ_K_BLK + pair), 128)], ks.at[rows, :], sems.at[1]),
                pltpu.make_async_copy(proj_ref.at[:, pl.ds(128 * (PROJ_V_BLK + pair), 128)], vs.at[rows, :], sems.at[2]),
                pltpu.make_async_copy(out_hbm.at[:, lanes], o_ref, sems.at[3]),
                pltpu.make_async_copy(lse_hbm.at[:, lanes], lse_ref, sems.at[4]),
                pltpu.make_async_copy(dmix_hbm.at[:, pl.ds(128 * (MIX_ATT_BLK + pair), 128)], do_ref, sems.at[5])])

        tables = start([pltpu.make_async_copy(c_hbm, c_ref, sems.at[6]),
                        pltpu.make_async_copy(s1_hbm, s1_ref, sems.at[7]),
                        pltpu.make_async_copy(s2_hbm, s2_ref, sems.at[8])])
        loads = load(0)
        for cp in tables:
            cp.wait()
        head0 = _head_of_lane(128) == 0
        zeros = jnp.zeros((ATT_KPAD, 128), F32)
        for pair in range(2):
            for cp in loads:
                cp.wait()
            c, s1, s2 = c_ref[...], s1_ref[...], s2_ref[...]
            qs[...] = _rope(qs[...], c, s1, s2) * _ATT_SCALE
            ks[pl.ds(0, ATT_KPAD), :] = zeros
            vs[pl.ds(0, ATT_KPAD), :] = zeros
            ks[pl.ds(ATT_KPAD, s), :] = _rope(ks[pl.ds(ATT_KPAD, s), :], c, s1, s2)
            dqs[...] = jnp.zeros_like(dqs)
            dks[...] = jnp.zeros_like(dks)
            dvs[...] = jnp.zeros_like(dvs)

            for _, d in ATT_PATTERNS:
                def blk(i, carry, d=d):
                    b, sq, sk = _att_slices(i, d)
                    qb = qs[sq, :]
                    kw = ks[sk, :].astype(BF16)
                    vw = vs[sk, :].astype(BF16)
                    dob = do_ref[sq, :]
                    lse_b = lse_ref[sq, :]
                    dd = dob * o_ref[sq, :]
                    valid = _att_valid(b)
                    dq_b = jnp.zeros((ATT_BLOCK, 128), F32)
                    dk_w = jnp.zeros((2 * ATT_BLOCK, 128), F32)
                    dv_w = jnp.zeros((2 * ATT_BLOCK, 128), F32)
                    for hh in range(2):
                        hm = head0 if hh == 0 else ~head0
                        qh = jnp.where(hm, qb, 0.0).astype(BF16)
                        doh = jnp.where(hm, dob, 0.0).astype(BF16)
                        lse_h = _pick_lane(lse_b, hh * HEAD_LANES)
                        d_h = jnp.sum(jnp.where(hm, dd, 0.0), axis=1, keepdims=True)
                        sc = lax.dot_general(qh, kw, (((1,), (1,)), ((), ())), preferred_element_type=F32)
                        p = jnp.where(valid, jnp.exp(sc - lse_h), 0.0)
                        dp = lax.dot_general(doh, vw, (((1,), (1,)), ((), ())), preferred_element_type=F32)
                        ds = (p * (dp - d_h)).astype(BF16)
                        dq_b = dq_b + jnp.where(hm, jnp.dot(ds, kw, preferred_element_type=F32), 0.0)
                        dk_w = dk_w + lax.dot_general(ds, qh, (((0,), (0,)), ((), ())), preferred_element_type=F32)
                        dv_w = dv_w + lax.dot_general(p.astype(BF16), doh, (((0,), (0,)), ((), ())),
                                                      preferred_element_type=F32)
                    dqs[sq, :] += dq_b
                    dks[sk, :] += dk_w
                    dvs[sk, :] += dv_w
                    return carry

                lax.fori_loop(0, _ATT_NBLK, blk, 0, unroll=_ATT_UNROLL)

            staged[0] = _rope_t(dqs[...] * _ATT_SCALE, c, s1, s2).astype(BF16)
            staged[1] = _rope_t(dks[pl.ds(ATT_KPAD, s), :], c, s1, s2).astype(BF16)
            staged[2] = dvs[pl.ds(ATT_KPAD, s), :].astype(BF16)
            stores = start([
                pltpu.make_async_copy(staged.at[j], dproj_hbm.at[:, pl.ds(128 * (col + pair), 128)], sems.at[9 + j])
                for j, col in enumerate((PROJ_Q_BLK, PROJ_K_BLK, PROJ_V_BLK))])
            if pair == 0:
                loads = load(1)
            for cp in stores:
                cp.wait()

    big = pltpu.VMEM((ATT_KPAD + s, 128), F32)
    tok = pltpu.VMEM((s, 128), F32)
    return pl.pallas_call(
        body, in_specs=[_HBM] * 8, out_specs=_HBM, out_shape=jax.ShapeDtypeStruct(dproj.shape, dproj.dtype),
        input_output_aliases={7: 0},
        scratch_shapes=[tok] * 6 + [tok, big, big, tok, big, big, pltpu.VMEM((3, s, 128), BF16),
                                    pltpu.SemaphoreType.DMA((12,))], name=name,
        compiler_params=pltpu.CompilerParams(vmem_limit_bytes=VMEM_LIMIT))(
            proj, cos, sin1, sin2, out, lse, dmix, dproj)


_POOL_TM = 512
_POOL_NT = SEQ // _POOL_TM
_POOL_HALO_PER_TILE = _POOL_TM // POOL_HALO


def _pool_tile(u_ext, w_bd, scale, t0):
    s2 = u_ext + _roll_rows(u_ext, 1)
    s4 = s2 + _roll_rows(s2, 2)
    s8 = s4 + _roll_rows(s4, 4)
    s16 = s8 + _roll_rows(s8, 8)
    grp = _head_of_lane(POOL_W)
    sel = jnp.where(grp == 0, s2, jnp.where(grp == 1, s4, jnp.where(grp == 2, s8, s16)))[POOL_HALO:]
    t = sel.shape[0]
    pos = t0 + lax.broadcasted_iota(jnp.int32, (t, POOL_W), 0) + 1
    win = jnp.where(grp == 0, 2, jnp.where(grp == 1, 4, jnp.where(grp == 2, 8, 16)))
    cnt = jnp.minimum(pos, win).astype(F32)
    diff = sel / cnt - u_ext[POOL_HALO:]
    return jnp.dot(diff.astype(BF16), w_bd.astype(BF16), preferred_element_type=F32) * scale


def _pool_fwd(proj, w_bd, scale, mix, *, name):
    tm = _POOL_TM

    def body(u_ref, halo_ref, w_ref, sc_ref, _, y_ref):
        i = pl.program_id(0)
        halo = jnp.where(i == 0, 0.0, halo_ref[...])
        u_ext = jnp.concatenate([halo, u_ref[...]], axis=0)
        y_ref[...] = _pool_tile(u_ext, w_ref[...], sc_ref[...], i * tm).astype(BF16)

    return pl.pallas_call(
        body, grid=(_POOL_NT,),
        in_specs=[pl.BlockSpec((tm, POOL_W), lambda i: (i, PROJ_POOL_BLK)),
                  pl.BlockSpec((POOL_HALO, POOL_W),
                               lambda i: (jnp.maximum(i * _POOL_HALO_PER_TILE - 1, 0), PROJ_POOL_BLK)),
                  pl.BlockSpec((POOL_W, POOL_W), lambda i: (0, 0)), pl.BlockSpec((1, POOL_W), lambda i: (0, 0)), _HBM],
        out_specs=pl.BlockSpec((tm, POOL_W), lambda i: (i, MIX_POOL_BLK)),
        out_shape=jax.ShapeDtypeStruct(mix.shape, mix.dtype), input_output_aliases={4: 0}, name=name,
        compiler_params=_cparams("parallel"))(proj, proj, w_bd, scale, mix)


def _pool_bwd(proj, dmix, w_bd, scale, dproj, *, name):
    tm = _POOL_TM
    last = _POOL_NT - 1

    def body(u_ref, halo_ref, dy_ref, w_ref, sc_ref, _, du_ref, dw_ref, dsc_ref, dhalo_acc):
        i = pl.program_id(0)

        @pl.when(i == 0)
        def _():
            dhalo_acc[...] = jnp.zeros_like(dhalo_acc)
            dw_ref[...] = jnp.zeros_like(dw_ref)
            dsc_ref[...] = jnp.zeros_like(dsc_ref)

        tile = last - i
        halo = jnp.where(tile == 0, 0.0, halo_ref[...])
        u_ext = jnp.concatenate([halo, u_ref[...]], axis=0)
        _, vjp = jax.vjp(functools.partial(_pool_tile, t0=tile * tm), u_ext, w_ref[...], sc_ref[...])
        gu, gw, gs = vjp(dy_ref[...])
        du_ref[...] = jnp.concatenate([gu[POOL_HALO:tm], gu[tm:] + dhalo_acc[...]], axis=0).astype(BF16)
        dhalo_acc[...] = gu[:POOL_HALO]
        dw_ref[...] += gw
        dsc_ref[...] += gs

    rev = lambda i: last - i
    return pl.pallas_call(
        body, grid=(_POOL_NT,),
        in_specs=[pl.BlockSpec((tm, POOL_W), lambda i: (rev(i), PROJ_POOL_BLK)),
                  pl.BlockSpec((POOL_HALO, POOL_W),
                               lambda i: (jnp.maximum(rev(i) * _POOL_HALO_PER_TILE - 1, 0), PROJ_POOL_BLK)),
                  pl.BlockSpec((tm, POOL_W), lambda i: (rev(i), MIX_POOL_BLK)),
                  pl.BlockSpec((POOL_W, POOL_W), lambda i: (0, 0)), pl.BlockSpec((1, POOL_W), lambda i: (0, 0)), _HBM],
        out_specs=[pl.BlockSpec((tm, POOL_W), lambda i: (rev(i), PROJ_POOL_BLK)),
                   pl.BlockSpec((POOL_W, POOL_W), lambda i: (0, 0)), pl.BlockSpec((1, POOL_W), lambda i: (0, 0))],
        out_shape=[jax.ShapeDtypeStruct(dproj.shape, dproj.dtype), jax.ShapeDtypeStruct((POOL_W, POOL_W), F32),
                   jax.ShapeDtypeStruct((1, POOL_W), F32)],
        input_output_aliases={5: 0}, scratch_shapes=[pltpu.VMEM((POOL_HALO, POOL_W), F32)], name=name,
        compiler_params=_cparams("arbitrary"))(proj, proj, dmix, w_bd, scale, dproj)


_FFN_TM = 256
_FFN_NT = SEQ // _FFN_TM
_FFN_HALO_PER_TILE = _FFN_TM // CONV_HALO


def _ffn_act_tile(hid_ext, conv_w, conv_b):
    hc = conv_b
    for k in range(FFN_CONV_K):
        hc = hc + _pick_row(conv_w, k) * _shift_rows(hid_ext, FFN_CONV_K - 1 - k, CONV_HALO)
    return _silu(hc[:, :FFN_DIM]) * hc[:, FFN_DIM:]


def _resident(shape):
    return pl.BlockSpec(shape, lambda i: (0,) * len(shape), pipeline_mode=pl.Buffered(1))


def _ffn_fwd(x1, norm_g, sc, sh, gate, up_t, down, conv_w, conv_b, *, name):
    tm = _FFN_TM
    w = 2 * FFN_DIM
    d = D_MODEL

    def body(x_ref, ng_ref, sc_ref, sh_ref, g_ref, up_ref, dn_ref, cw_ref, cb_ref,
             h_ref, hid_ref, act_ref, f_ref, x2_ref, halo_acc):
        i = pl.program_id(0)
        h2 = _rms_modulate(x_ref[...], ng_ref[...], sc_ref[...], sh_ref[...]).astype(BF16)
        h_ref[...] = h2
        hid = lax.dot_general(h2, up_ref[...], (((1,), (1,)), ((), ())), preferred_element_type=F32)
        hid_ref[...] = hid
        halo = jnp.where(i == 0, 0.0, halo_acc[...])
        act = _ffn_act_tile(jnp.concatenate([halo, hid], axis=0), cw_ref[...], cb_ref[...]).astype(BF16)
        halo_acc[...] = hid[tm - CONV_HALO:]
        act_ref[...] = act
        f = jnp.dot(act, dn_ref[...], preferred_element_type=F32)
        f_ref[...] = f
        x2_ref[...] = x_ref[...] + g_ref[...] * f

    tile = lambda n: pl.BlockSpec((tm, n), lambda i: (i, 0))
    return pl.pallas_call(
        body, grid=(_FFN_NT,),
        in_specs=[tile(d)] + [_resident((1, d))] * 4 + [_resident((w, d)), _resident((FFN_DIM, d)),
                                                        _resident((8, w)), _resident((1, w))],
        out_specs=[tile(d), tile(w), tile(FFN_DIM), tile(d), tile(d)],
        out_shape=[jax.ShapeDtypeStruct((SEQ, d), BF16), jax.ShapeDtypeStruct((SEQ, w), F32),
                   jax.ShapeDtypeStruct((SEQ, FFN_DIM), BF16), jax.ShapeDtypeStruct((SEQ, d), F32),
                   jax.ShapeDtypeStruct((SEQ, d), F32)],
        scratch_shapes=[pltpu.VMEM((CONV_HALO, w), F32)], name=name,
        compiler_params=_cparams("arbitrary"))(x1, norm_g, sc, sh, gate, up_t, down, conv_w, conv_b)


def _ffn_bwd(df, hid, up_t, down, conv_w, conv_b, *, name):
    tm = _FFN_TM
    w = 2 * FFN_DIM
    d = D_MODEL
    last = _FFN_NT - 1

    def body(df_ref, h_ref, halo_ref, up_ref, dn_ref, cw_ref, cb_ref, dh_ref, dh2_ref, dcw_ref, dcb_ref, dhalo_acc):
        i = pl.program_id(0)

        @pl.when(i == 0)
        def _():
            dhalo_acc[...] = jnp.zeros_like(dhalo_acc)
            dcw_ref[...] = jnp.zeros_like(dcw_ref)
            dcb_ref[...] = jnp.zeros_like(dcb_ref)

        dact = lax.dot_general(df_ref[...], dn_ref[...], (((1,), (1,)), ((), ())), preferred_element_type=F32)
        halo = jnp.where(i == last, 0.0, halo_ref[...])
        hid_ext = jnp.concatenate([halo, h_ref[...]], axis=0)
        _, vjp = jax.vjp(_ffn_act_tile, hid_ext, cw_ref[...], cb_ref[...])
        gh, gw, gb = vjp(dact)
        dhid = jnp.concatenate([gh[CONV_HALO:tm], gh[tm:] + dhalo_acc[...]], axis=0).astype(BF16)
        dhalo_acc[...] = gh[:CONV_HALO]
        dh_ref[...] = dhid
        dh2_ref[...] = jnp.dot(dhid, up_ref[...], preferred_element_type=F32)
        dcw_ref[...] += gw
        dcb_ref[...] += gb

    rev = lambda i: last - i
    tile = lambda n: pl.BlockSpec((tm, n), lambda i: (rev(i), 0))
    acc = lambda shape: pl.BlockSpec(shape, lambda i: (0, 0))
    return pl.pallas_call(
        body, grid=(_FFN_NT,),
        in_specs=[tile(d), tile(w),
                  pl.BlockSpec((CONV_HALO, w), lambda i: (jnp.maximum(rev(i) * _FFN_HALO_PER_TILE - 1, 0), 0)),
                  _resident((w, d)), _resident((FFN_DIM, d)), _resident((8, w)), _resident((1, w))],
        out_specs=[tile(w), tile(d), acc((8, w)), acc((1, w))],
        out_shape=[jax.ShapeDtypeStruct((SEQ, w), BF16), jax.ShapeDtypeStruct((SEQ, d), F32),
                   jax.ShapeDtypeStruct((8, w), F32), jax.ShapeDtypeStruct((1, w), F32)],
        scratch_shapes=[pltpu.VMEM((CONV_HALO, w), F32)], name=name,
        compiler_params=_cparams("arbitrary"))(df, hid, hid, up_t, down, conv_w, conv_b)


def _axes():
    return lax.axis_index("x"), lax.axis_index("y"), lax.axis_index("c")


def _handshake(peers):
    barrier = pltpu.get_barrier_semaphore()
    for peer in peers:
        pl.semaphore_signal(barrier, inc=1, device_id=peer, device_id_type=MESH)
    pl.semaphore_wait(barrier, len(peers))


def _allgather_body(x_refs, out_refs, send_sems, recv_sems, local_sems, own_barrier):
    n = len(x_refs)
    x, y, c = _axes()
    me, sibling = (x, y, c), (x, y, 1 - c)
    chips = [(1 - x, y), (x, 1 - y), (1 - x, 1 - y)]
    if own_barrier:
        _handshake([sibling] + [(*chip, c) for chip in chips])

    def slot(a, px, py, pc):
        return out_refs[a].at[4 * px + 2 * py + pc]

    def copy(a, k, block, to, src=None):
        return pltpu.make_async_remote_copy(
            src_ref=slot(a, *block) if src is None else src, dst_ref=slot(a, *block),
            send_sem=send_sems.at[a, k], recv_sem=recv_sems.at[a, k], device_id=to, device_id_type=MESH)

    mines, firsts = [], []
    for a in range(n):
        mines.append(pltpu.make_async_copy(x_refs[a], slot(a, *me), local_sems.at[a]))
        mines[-1].start()
        first = [copy(a, 0, me, sibling, src=x_refs[a])]
        first += [copy(a, 1 + j, me, (*chip, c), src=x_refs[a]) for j, chip in enumerate(chips)]
        for cp in first:
            cp.start()
        firsts += first
    passed = []
    for j, chip in enumerate(chips):
        for a in range(n):
            copy(a, 1 + j, (*chip, c), me).wait_recv()
            passed.append(copy(a, 4 + j, (*chip, c), sibling))
            passed[-1].start()
    for a in range(n):
        copy(a, 0, sibling, me).wait_recv()
    for j, chip in enumerate(chips):
        for a in range(n):
            copy(a, 4 + j, (*chip, 1 - c), me).wait_recv()
    for cp in firsts + passed:
        cp.wait_send()
    for cp in mines:
        cp.wait()


def _allgather_sems(n):
    return [pltpu.SemaphoreType.DMA((n, 7)), pltpu.SemaphoreType.DMA((n, 7)), pltpu.SemaphoreType.DMA((n,))]


def _allgather(xs, *, name):
    n = len(xs)

    def body(*refs):
        _allgather_body(refs[:n], refs[n:2 * n], *refs[2 * n:], own_barrier=False)

    return pl.pallas_call(
        body, out_shape=[jax.ShapeDtypeStruct((N_DEV,) + xb.shape, xb.dtype) for xb in xs],
        in_specs=[_HBM] * n, out_specs=[_HBM] * n, scratch_shapes=_allgather_sems(n), name=name)(*xs)


def _allgather_async(xs, *, name, collective_id):
    n = len(xs)
    x_refs = [jax.new_ref(xb, memory_space=pltpu.MemorySpace.HBM) for xb in xs]
    out_refs = [jax.empty_ref(jax.ShapeDtypeStruct((N_DEV,) + xb.shape, xb.dtype), memory_space=pltpu.MemorySpace.HBM)
                for xb in xs]

    @pl.kernel(mesh=plsc.ScalarSubcoreMesh(axis_name="sequencer", num_cores=1), name=name,
               scratch_types=tuple(_allgather_sems(n)),
               compiler_params=pltpu.CompilerParams(collective_id=collective_id))
    def launch(send_sems, recv_sems, local_sems):
        _allgather_body(x_refs, out_refs, send_sems, recv_sems, local_sems, own_barrier=True)

    launch()
    return [r[...] for r in out_refs]


def _pair_exchange(blocks, *, name, collective_id):
    n = len(blocks)
    hbm = pltpu.MemorySpace.HBM
    in_refs = [jax.new_ref(b, memory_space=hbm) for b in blocks]
    out_refs = [jax.empty_ref(jax.ShapeDtypeStruct((4,) + b.shape[1:], b.dtype), memory_space=hbm) for b in blocks]

    @pl.kernel(mesh=plsc.ScalarSubcoreMesh(axis_name="sequencer", num_cores=1), name=name,
               scratch_types=(pltpu.SemaphoreType.DMA((n, 4)), pltpu.SemaphoreType.DMA((n, 4))),
               compiler_params=pltpu.CompilerParams(collective_id=collective_id))
    def launch(send_sems, recv_sems):
        x, y, c = _axes()
        _handshake([(x, y, 1 - c)])
        copies = [pltpu.make_async_remote_copy(
            src_ref=in_refs[a].at[2 * s + (1 - c)], dst_ref=out_refs[a].at[s], send_sem=send_sems.at[a, s],
            recv_sem=recv_sems.at[a, s], device_id=(x, y, 1 - c), device_id_type=MESH)
            for a in range(n) for s in range(4)]
        for cp in copies:
            cp.start()
        for cp in copies:
            cp.wait_recv()
        for cp in copies:
            cp.wait_send()

    launch()
    return [r[...] for r in out_refs]


def _chip_exchange(parts, *, name, collective_id):
    n = len(parts)
    hbm = pltpu.MemorySpace.HBM
    in_refs = [jax.new_ref(p, memory_space=hbm) for p in parts]
    out_refs = [jax.empty_ref(jax.ShapeDtypeStruct(p.shape, p.dtype), memory_space=hbm) for p in parts]

    @pl.kernel(mesh=plsc.ScalarSubcoreMesh(axis_name="sequencer", num_cores=1), name=name,
               scratch_types=(pltpu.SemaphoreType.DMA((n, 3)), pltpu.SemaphoreType.DMA((n, 3)),
                              pltpu.SemaphoreType.DMA((n,))),
               compiler_params=pltpu.CompilerParams(collective_id=collective_id))
    def launch(send_sems, recv_sems, local_sems):
        x, y, c = _axes()
        my_chip = 2 * x + y
        chips = [(1 - x, y), (x, 1 - y), (1 - x, 1 - y)]
        _handshake([(*chip, c) for chip in chips])
        locals_ = [pltpu.make_async_copy(in_refs[a].at[my_chip], out_refs[a].at[my_chip], local_sems.at[a])
                   for a in range(n)]
        for cp in locals_:
            cp.start()
        copies = [pltpu.make_async_remote_copy(
            src_ref=in_refs[a].at[2 * px + py], dst_ref=out_refs[a].at[my_chip], send_sem=send_sems.at[a, k],
            recv_sem=recv_sems.at[a, k], device_id=(px, py, c), device_id_type=MESH)
            for a in range(n) for k, (px, py) in enumerate(chips)]
        for cp in copies:
            cp.start()
        for cp in copies:
            cp.wait_recv()
        for cp in copies:
            cp.wait_send()
        for cp in locals_:
            cp.wait()

    launch()
    return [r[...] for r in out_refs]


def _pair_sum(core, blocks, from_sibling, *, name):
    _, r, cdim = blocks.shape

    def body(core_ref, a_ref, b_ref, o_ref):
        o_ref[...] = (a_ref[...].astype(F32) + b_ref[...].astype(F32)).astype(o_ref.dtype)

    return pl.pallas_call(
        body,
        grid_spec=pltpu.PrefetchScalarGridSpec(
            num_scalar_prefetch=1, grid=(4,),
            in_specs=[pl.BlockSpec((1, r, cdim), lambda s, core_ref: (2 * s + core_ref[0], 0, 0)),
                      pl.BlockSpec((1, r, cdim), lambda s, core_ref: (s, 0, 0))],
            out_specs=pl.BlockSpec((1, r, cdim), lambda s, core_ref: (s, 0, 0))),
        out_shape=jax.ShapeDtypeStruct(from_sibling.shape, from_sibling.dtype), name=name,
        compiler_params=_cparams("parallel"))(core, blocks, from_sibling)


def _sum_blocks(a, *, name, tr=None):
    n, r, cdim = a.shape
    tr = tr or r

    def body(a_ref, o_ref):
        acc = a_ref[0].astype(F32)
        for k in range(1, n):
            acc = acc + a_ref[k].astype(F32)
        o_ref[...] = acc

    return pl.pallas_call(body, grid=(r // tr,), in_specs=[pl.BlockSpec((n, tr, cdim), lambda i: (0, i, 0))],
                          out_specs=pl.BlockSpec((tr, cdim), lambda i: (i, 0)),
                          out_shape=jax.ShapeDtypeStruct((r, cdim), F32), name=name,
                          compiler_params=_cparams("parallel"))(a)


_ADA_SHARD = 6 * D_MODEL // N_DEV


def _ada_mod(c_all, ada_w, *, name):
    def body(c_ref, w_ref, o_ref):
        o_ref[0] = jnp.dot(_silu(c_ref[...]).astype(BF16), w_ref[0].astype(BF16), preferred_element_type=F32)

    return pl.pallas_call(
        body, grid=(DEPTH,),
        in_specs=[pl.BlockSpec((N_DEV, D_MODEL), lambda l: (0, 0)),
                  pl.BlockSpec((1, D_MODEL, _ADA_SHARD), lambda l: (l, 0, 0))],
        out_specs=pl.BlockSpec((1, N_DEV, _ADA_SHARD), lambda l: (l, 0, 0)),
        out_shape=jax.ShapeDtypeStruct((DEPTH, N_DEV, _ADA_SHARD), F32), name=name,
        compiler_params=_cparams("parallel"))(c_all, ada_w)


def _ada_wgrad(c_all, dmod_cols, *, name):
    def body(c_ref, d_ref, o_ref):
        o_ref[0] = lax.dot_general(_silu(c_ref[...]), d_ref[0], (((0,), (0,)), ((), ())),
                                   preferred_element_type=F32, precision=lax.Precision.HIGHEST)

    return pl.pallas_call(
        body, grid=(DEPTH,),
        in_specs=[pl.BlockSpec((N_DEV, D_MODEL), lambda l: (0, 0)),
                  pl.BlockSpec((1, N_DEV, _ADA_SHARD), lambda l: (l, 0, 0))],
        out_specs=pl.BlockSpec((1, D_MODEL, _ADA_SHARD), lambda l: (l, 0, 0)),
        out_shape=jax.ShapeDtypeStruct((DEPTH, D_MODEL, _ADA_SHARD), F32), name=name,
        compiler_params=_cparams("parallel"))(c_all, dmod_cols)


def _add_rows(a, b, *, name):
    def body(a_ref, b_ref, o_ref):
        o_ref[...] = a_ref[...] + b_ref[...]

    return pl.pallas_call(body, out_shape=jax.ShapeDtypeStruct(a.shape, a.dtype), name=name)(a, b)


def _adamw(w, g, m, v, *, name, tr):
    r, cdim = w.shape
    c1 = 1.0 - ADAM_B1 ** ADAM_STEP
    c2 = 1.0 - ADAM_B2 ** ADAM_STEP

    def body(w_ref, g_ref, m_ref, v_ref, d_ref, mo_ref, vo_ref):
        gv = g_ref[...]
        mn = ADAM_B1 * m_ref[...] + (1.0 - ADAM_B1) * gv
        vn = ADAM_B2 * v_ref[...] + (1.0 - ADAM_B2) * (gv * gv)
        mo_ref[...] = mn
        vo_ref[...] = vn
        d_ref[...] = -ADAM_LR * ((mn / c1) / (jnp.sqrt(vn / c2) + ADAM_EPS) + ADAM_WD * w_ref[...])

    spec = pl.BlockSpec((tr, cdim), lambda i: (i, 0))
    shp = jax.ShapeDtypeStruct((r, cdim), F32)
    return pl.pallas_call(body, grid=(r // tr,), in_specs=[spec] * 4, out_specs=[spec] * 3, out_shape=[shp] * 3,
                          name=name, compiler_params=_cparams("parallel"))(w, g, m, v)


def _pad_rows(a, rows):
    return jnp.concatenate([a, jnp.zeros((rows - a.shape[0],) + a.shape[1:], a.dtype)], axis=0)


def _pad_lanes(a, lanes):
    return jnp.concatenate([a, jnp.zeros(a.shape[:-1] + (lanes - a.shape[-1],), a.dtype)], axis=-1)


def _permute_w_in(wt):
    return jnp.concatenate([wt[512:1536], wt[:512], wt[1536:1544],
                            jnp.zeros((PROJ_W - IN_W, wt.shape[1]), wt.dtype), wt[1544:]], axis=0)


def _unpermute_w_in(wp):
    return jnp.concatenate([wp[1024:1536], wp[:1024], wp[1536:1544], wp[PROJ_SSD_W:]], axis=0)


def _block_diag(w):
    rows = []
    for g in range(4):
        rows.append(jnp.concatenate([w[g] if k == g else jnp.zeros_like(w[g]) for k in range(4)], axis=1))
    return jnp.concatenate(rows, axis=0)


def _diag_blocks(wbd):
    return jnp.stack([wbd[64 * g:64 * (g + 1), 64 * g:64 * (g + 1)] for g in range(4)], axis=0)


def _layer_params(l, small):
    return dict(
        norm1_g=small["norm1_g"][l][None], norm2_g=small["norm2_g"][l][None],
        conv_w=_pad_rows(small["ssd_conv_w"][l], 8), conv_b=small["ssd_conv_b"][l][None],
        dt_bias=_pad_lanes(small["ssd_dt_bias"][l][None], 128), a_log=_pad_lanes(small["ssd_a_log"][l][None], 128),
        d_skip=_pad_lanes(small["ssd_d"][l][None], 128), ssd_norm_g=small["ssd_norm_g"][l][None],
        pool_bd=_block_diag(small["pool_w"][l]), pool_scale=small["pool_scale"][l][None],
        fcw=_pad_rows(small["ffn_conv_w"][l], 8), fcb=small["ffn_conv_b"][l][None])


def _mod_rows(mod_l):
    return [mod_l[None, D_MODEL * i:D_MODEL * (i + 1)] for i in range(6)]


def _layer_fwd(x, mod_l, p, tabs, l, gather):
    sh1, sc1, g1, sh2, sc2, g2 = _mod_rows(mod_l)
    mix_w = gather(l, "mix", None)
    p.update(w_in=mix_w["w_in"], w_out=mix_w["w_out"])
    proj, h1 = _mm(x, p["w_in"], nt=True, norm=(p["norm1_g"], sc1, sh1), name=f"l{l}_proj")
    ffn_w = gather(l, "ffn", proj)
    p.update(up=ffn_w["ffn_up"], down=ffn_w["ffn_down"])
    mix, hst = _ssd_fwd(proj, p["conv_w"], p["conv_b"], p["dt_bias"], p["a_log"], p["d_skip"], p["ssd_norm_g"],
                        name=f"l{l}_ssd")
    mix = _pool_fwd(proj, p["pool_bd"], p["pool_scale"], mix, name=f"l{l}_pool")
    y_att, lse, mix = _att_fwd(proj, *tabs, mix, name=f"l{l}_att")
    mo, x1 = _mm(mix, p["w_out"], residual=(x, g1), name=f"l{l}_out")
    gather(l + 1, "mix", x1)
    h2, hid, act, f, x2 = _ffn_fwd(x1, p["norm2_g"], sc2, sh2, g2, p["up"], p["down"], p["fcw"], p["fcb"],
                                   name=f"l{l}_ffn")
    return x2, dict(x=x, h1=h1, proj=proj, hst=hst, y_att=y_att, lse=lse, mix=mix, mo=mo, x1=x1, h2=h2, hid=hid,
                    act=act, f=f)


def _layer_bwd(dx2, sv, mod_l, p, tabs, l, exchange):
    sh1, sc1, g1, sh2, sc2, g2 = _mod_rows(mod_l)
    df, dg2 = _residual_bwd(dx2, g2, sv["f"], name=f"l{l}_res2_b")
    d_down = _wgrad(sv["act"], df, name=f"l{l}_down_bw")
    dhid, dh2, dfcw, dfcb = _ffn_bwd(df, sv["hid"], p["up"], p["down"], p["fcw"], p["fcb"], name=f"l{l}_ffn_b")
    d_up = _wgrad(dhid, sv["h2"], tk=1408, name=f"l{l}_up_bw")
    exchange(l, "ffn", dict(ffn_up=d_up, ffn_down=d_down))
    dx1, dn2, dsc2, dsh2 = _norm_mod_bwd(sv["x1"], dh2, dx2, p["norm2_g"], sc2, name=f"l{l}_norm2_b")
    dmo, dg1 = _residual_bwd(dx1, g1, sv["mo"], name=f"l{l}_res1_b")
    dmix = _mm(dmo, p["w_out"], nt=True, name=f"l{l}_out_bx")
    d_wout = _wgrad(sv["mix"], dmo, name=f"l{l}_out_bw")
    dproj, dcw, dcb, ddb, dal, dd, dng = _ssd_bwd(
        sv["proj"], sv["hst"], dmix, p["conv_w"], p["conv_b"], p["dt_bias"], p["a_log"], p["d_skip"],
        p["ssd_norm_g"], name=f"l{l}_ssd_b")
    dproj, dwbd, dpsc = _pool_bwd(sv["proj"], dmix, p["pool_bd"], p["pool_scale"], dproj, name=f"l{l}_pool_b")
    dproj = _att_bwd(sv["proj"], *tabs, sv["y_att"], sv["lse"], dmix, dproj, name=f"l{l}_att_b")
    dh1 = _mm(dproj, p["w_in"], name=f"l{l}_proj_bx")
    d_win = _wgrad(dproj, sv["h1"], name=f"l{l}_proj_bw")
    exchange(l, "mix", dict(w_in=d_win, w_out=d_wout))
    dx0, dn1, dsc1, dsh1 = _norm_mod_bwd(sv["x"], dh1, dx1, p["norm1_g"], sc1, name=f"l{l}_norm1_b")
    dmod = jnp.concatenate([dsh1, dsc1, dg1, dsh2, dsc2, dg2], axis=1)[0]
    small = dict(norm1_g=dn1[0], ssd_conv_w=dcw[:SSD_CONV_K], ssd_conv_b=dcb[0], ssd_dt_bias=ddb[0], ssd_a_log=dal[0],
                 ssd_d=dd[0], ssd_norm_g=dng[0], pool_w=_diag_blocks(dwbd), pool_scale=dpsc[0], norm2_g=dn2[0],
                 ffn_conv_w=dfcw[:FFN_CONV_K], ffn_conv_b=dfcb[0])
    return dx0, dmod, small


def _example_step(x, target, pos_col, inv_freq_lane, mod, gather, small, final_g, exchange):
    tabs = _rope_tables(pos_col, inv_freq_lane, name="rope_tables")
    params, saved = [], []
    for l in range(DEPTH):
        params.append(_layer_params(l, small))
        x, sv = _layer_fwd(x, mod[l], params[l], tabs, l, gather)
        saved.append(sv)
    loss_row, dx, dfg = _final_loss(x, final_g[None], target, name="final_loss")
    dmods, smalls = [None] * DEPTH, [None] * DEPTH
    for l in reversed(range(DEPTH)):
        dx, dmods[l], smalls[l] = _layer_bwd(dx, saved[l], mod[l], params[l], tabs, l, exchange)
    return loss_row, dx, jnp.stack(dmods, axis=0), smalls, dfg[0]


_BIG = (("w_in", (D_MODEL, IN_W // N_DEV)), ("w_out", (D_MODEL // N_DEV, D_MODEL)),
        ("ffn_up", (D_MODEL, 2 * FFN_DIM // N_DEV)), ("ffn_down", (FFN_DIM // N_DEV, D_MODEL)))
_PACK_LANES = 1024

_SMALL_GRADS = (("norm1_g", D_MODEL), ("ssd_conv_w", SSD_CONV_K * SSD_CONV_CH), ("ssd_conv_b", SSD_CONV_CH),
                ("ssd_dt_bias", 128), ("ssd_a_log", 128), ("ssd_d", 128), ("ssd_norm_g", SSD_INNER),
                ("pool_w", 4 * 64 * 64), ("pool_scale", POOL_W), ("norm2_g", D_MODEL),
                ("ffn_conv_w", FFN_CONV_K * 2 * FFN_DIM), ("ffn_conv_b", 2 * FFN_DIM))
_SMALL_PARAMS = ("ada_b", "norm1_g", "ssd_conv_w", "ssd_conv_b", "ssd_dt_bias", "ssd_a_log", "ssd_d", "ssd_norm_g",
                 "pool_w", "pool_scale", "norm2_g", "ffn_conv_w", "ffn_conv_b", "final_g")
_WEIGHT_ORDER = ("ada_w", "ada_b", "norm1_g", "w_in", "ssd_conv_w", "ssd_conv_b", "ssd_dt_bias", "ssd_a_log", "ssd_d",
                 "ssd_norm_g", "pool_w", "pool_scale", "w_out", "norm2_g", "ffn_up", "ffn_conv_w", "ffn_conv_b",
                 "ffn_down", "final_g")


def _pack(parts, rows, dtype):
    flat = jnp.concatenate([p.reshape(-1).astype(dtype) for p in parts])
    return jnp.concatenate([flat, jnp.zeros((rows * _PACK_LANES - flat.shape[0],), dtype)]).reshape(rows, _PACK_LANES)


_COLUMN_SHARDED = ("w_in", "ffn_up")
_GROUPS = (("mix", ("w_in", "w_out")), ("ffn", ("ffn_up", "ffn_down")))


def _big_shares(w, l, names):
    return [(w[name][l].T if name in _COLUMN_SHARDED else w[name][l]).astype(BF16) for name in names]


def _unshard_big(names, gathered):
    out = {}
    for name, g in zip(names, gathered):
        full = g.reshape(N_DEV * g.shape[1], g.shape[2])
        out[name] = _permute_w_in(full) if name == "w_in" else full
    return out


def _shard_big(grads):
    out = []
    for name, g in grads.items():
        g = _unpermute_w_in(g) if name == "w_in" else g
        out.append(g.reshape(N_DEV, g.shape[0] // N_DEV, g.shape[1]))
    return out


def kernel(x, c, positions, ada_w, ada_b, norm1_g, w_in, ssd_conv_w, ssd_conv_b, ssd_dt_bias, ssd_a_log, ssd_d, ssd_norm_g, pool_w, pool_scale, w_out, norm2_g, ffn_up, ffn_conv_w, ffn_conv_b, ffn_down, final_g, loss_target, m_ada_w, m_ada_b, m_norm1_g, m_w_in, m_ssd_conv_w, m_ssd_conv_b, m_ssd_dt_bias, m_ssd_a_log, m_ssd_d, m_ssd_norm_g, m_pool_w, m_pool_scale, m_w_out, m_norm2_g, m_ffn_up, m_ffn_conv_w, m_ffn_conv_b, m_ffn_down, m_final_g, v_ada_w, v_ada_b, v_norm1_g, v_w_in, v_ssd_conv_w, v_ssd_conv_b, v_ssd_dt_bias, v_ssd_a_log, v_ssd_d, v_ssd_norm_g, v_pool_w, v_pool_scale, v_w_out, v_norm2_g, v_ffn_up, v_ffn_conv_w, v_ffn_conv_b, v_ffn_down, v_final_g):
    w = dict(ada_w=ada_w, ada_b=ada_b, norm1_g=norm1_g, w_in=w_in, ssd_conv_w=ssd_conv_w, ssd_conv_b=ssd_conv_b,
             ssd_dt_bias=ssd_dt_bias, ssd_a_log=ssd_a_log, ssd_d=ssd_d, ssd_norm_g=ssd_norm_g, pool_w=pool_w,
             pool_scale=pool_scale, w_out=w_out, norm2_g=norm2_g, ffn_up=ffn_up, ffn_conv_w=ffn_conv_w,
             ffn_conv_b=ffn_conv_b, ffn_down=ffn_down, final_g=final_g)
    m = dict(ada_w=m_ada_w, ada_b=m_ada_b, norm1_g=m_norm1_g, w_in=m_w_in, ssd_conv_w=m_ssd_conv_w,
             ssd_conv_b=m_ssd_conv_b, ssd_dt_bias=m_ssd_dt_bias, ssd_a_log=m_ssd_a_log, ssd_d=m_ssd_d,
             ssd_norm_g=m_ssd_norm_g, pool_w=m_pool_w, pool_scale=m_pool_scale, w_out=m_w_out, norm2_g=m_norm2_g,
             ffn_up=m_ffn_up, ffn_conv_w=m_ffn_conv_w, ffn_conv_b=m_ffn_conv_b, ffn_down=m_ffn_down,
             final_g=m_final_g)
    v = dict(ada_w=v_ada_w, ada_b=v_ada_b, norm1_g=v_norm1_g, w_in=v_w_in, ssd_conv_w=v_ssd_conv_w,
             ssd_conv_b=v_ssd_conv_b, ssd_dt_bias=v_ssd_dt_bias, ssd_a_log=v_ssd_a_log, ssd_d=v_ssd_d,
             ssd_norm_g=v_ssd_norm_g, pool_w=v_pool_w, pool_scale=v_pool_scale, w_out=v_w_out, norm2_g=v_norm2_g,
             ffn_up=v_ffn_up, ffn_conv_w=v_ffn_conv_w, ffn_conv_b=v_ffn_conv_b, ffn_down=v_ffn_down,
             final_g=v_final_g)
    ix, iy, ic = _axes()
    dev = 4 * ix + 2 * iy + ic

    n_scw, n_fcw = ssd_conv_w.size, ffn_conv_w.size
    small_all = _allgather([_pack([c, ssd_conv_w, ffn_conv_w], 8, F32)], name="gather_small")[0].reshape(N_DEV, -1)
    c_all = small_all[:, :D_MODEL]
    scw = small_all[:, D_MODEL:D_MODEL + n_scw].reshape(N_DEV, DEPTH, SSD_CONV_K, -1)
    scw = scw.transpose(1, 2, 0, 3).reshape(DEPTH, SSD_CONV_K, SSD_CONV_CH)
    fcw = small_all[:, D_MODEL + n_scw:D_MODEL + n_scw + n_fcw].reshape(N_DEV, DEPTH, FFN_CONV_K, -1)
    fcw = fcw.transpose(1, 2, 0, 3).reshape(DEPTH, FFN_CONV_K, 2 * FFN_DIM)

    mod_cols = _ada_mod(c_all, ada_w, name="ada_mod")
    mod_all = _allgather([mod_cols.reshape(DEPTH * N_DEV, _ADA_SHARD)], name="gather_mod")[0]
    mod_all = mod_all.reshape(N_DEV, DEPTH, N_DEV, _ADA_SHARD)
    mod_mine = lax.dynamic_index_in_dim(mod_all, dev, axis=2, keepdims=False)
    mod = _add_rows(mod_mine.transpose(1, 0, 2).reshape(DEPTH, 6 * D_MODEL), ada_b, name="ada_bias")

    fetched = {}

    def gather(l, group, after):
        if l < DEPTH and (l, group) not in fetched:
            names = dict(_GROUPS)[group]
            shares, _ = lax.optimization_barrier((_big_shares(w, l, names), small_all if after is None else after))
            got = _allgather_async(shares, name=f"gather_weights_l{l}_{group}",
                                   collective_id=1 + 2 * l + (group == "ffn"))
            fetched[l, group] = _unshard_big(names, got)
        return fetched.get((l, group))

    core = ic.astype(jnp.int32).reshape(1)
    from_chips = {}

    def exchange(l, group, g):
        cid = 5 + 4 * l + 2 * (group == "mix")
        blocks = _shard_big(g)
        from_sibling = _pair_exchange(blocks, name=f"grads_pair_exchange_l{l}_{group}", collective_id=cid)
        parts = [_pair_sum(core, b, s, name=f"grads_pair_sum_l{l}_{name}")
                 for name, b, s in zip(g, blocks, from_sibling)]
        got = _chip_exchange(parts, name=f"grads_chip_exchange_l{l}_{group}", collective_id=cid + 1)
        from_chips.update({(l, name): t for name, t in zip(g, got)})

    small = dict(norm1_g=norm1_g, norm2_g=norm2_g, ssd_conv_w=scw, ssd_conv_b=ssd_conv_b, ssd_dt_bias=ssd_dt_bias,
                 ssd_a_log=ssd_a_log, ssd_d=ssd_d, ssd_norm_g=ssd_norm_g, pool_w=pool_w, pool_scale=pool_scale,
                 ffn_conv_w=fcw, ffn_conv_b=ffn_conv_b)

    inv_freq = ROPE_THETA ** (-jnp.arange(0, ROT_DIM, 2, dtype=F32) / ROT_DIM)
    lane = jnp.arange(128) % HEAD_LANES
    inv_freq_lane = jnp.where(lane < ROT_DIM, inv_freq[lane % (ROT_DIM // 2)], 0.0)[None, :]
    pos_col = positions.reshape(SEQ, 1).astype(F32)
    loss_row, dx, dmod, g_small, g_final = _example_step(
        x[0], loss_target[0], pos_col, inv_freq_lane, mod, gather, small, final_g, exchange)

    grads = {}
    for name, _ in _BIG:
        per_layer = [_sum_blocks(from_chips[l, name], name=f"grads_chip_sum_l{l}_{name}") for l in range(DEPTH)]
        grads[name] = jnp.stack([g.T if name in _COLUMN_SHARDED else g for g in per_layer], axis=0)

    small_parts = [loss_row, dmod] + [jnp.stack([g_small[l][name] for l in range(DEPTH)], axis=0)
                                      for name, _ in _SMALL_GRADS] + [g_final]
    n_small = sum(p.size for p in small_parts)
    small_rows = -(-n_small // _PACK_LANES)
    gathered = _allgather([_pack(small_parts, small_rows, F32)], name="gather_small_grads")[0]
    total = _sum_blocks(gathered, name="sum_small_grads").reshape(-1)
    loss = total[0]
    off = 128
    grads["ada_b"] = total[off:off + DEPTH * 6 * D_MODEL].reshape(DEPTH, 6 * D_MODEL)
    dmod_all = gathered.reshape(N_DEV, -1)[:, off:off + DEPTH * 6 * D_MODEL].reshape(N_DEV, DEPTH, 6 * D_MODEL)
    off += DEPTH * 6 * D_MODEL
    for name, n in _SMALL_GRADS:
        grads[name] = total[off:off + DEPTH * n].reshape(DEPTH, n)
        off += DEPTH * n
    grads["final_g"] = total[off:off + D_MODEL]
    dmod_cols = lax.dynamic_slice_in_dim(dmod_all, dev * _ADA_SHARD, _ADA_SHARD, axis=2).transpose(1, 0, 2)
    grads["ada_w"] = _ada_wgrad(c_all, dmod_cols, name="ada_wgrad")
    for name in ("ssd_dt_bias", "ssd_a_log", "ssd_d"):
        grads[name] = grads[name][:, :SSD_HEADS]
    grads["pool_w"] = grads["pool_w"].reshape(pool_w.shape)
    grads["ssd_conv_w"] = lax.dynamic_slice_in_dim(
        grads["ssd_conv_w"].reshape(DEPTH, SSD_CONV_K, SSD_CONV_CH), dev * ssd_conv_w.shape[2], ssd_conv_w.shape[2], axis=2)
    grads["ffn_conv_w"] = lax.dynamic_slice_in_dim(
        grads["ffn_conv_w"].reshape(DEPTH, FFN_CONV_K, 2 * FFN_DIM), dev * ffn_conv_w.shape[2], ffn_conv_w.shape[2], axis=2)

    delta, new_m, new_v = {}, {}, {}
    for name, tr in (("ada_w", 512), ("w_in", 512), ("w_out", 256), ("ffn_up", 512), ("ffn_down", 352)):
        shp = w[name].shape
        two_d = lambda a: a.reshape(shp[0] * shp[1], shp[2])
        d_, m_, v_ = _adamw(two_d(w[name]), two_d(grads[name]), two_d(m[name]), two_d(v[name]), tr=tr,
                               name=f"adamw_{name}")
        delta[name], new_m[name], new_v[name] = (t.reshape(shp) for t in (d_, m_, v_))
    n_packed = sum(w[name].size for name in _SMALL_PARAMS)
    rows = -(-n_packed // _PACK_LANES)
    packed = [_pack([t[name] for name in _SMALL_PARAMS], rows, F32) for t in (w, grads, m, v)]
    outs = [o.reshape(-1) for o in _adamw(*packed, tr=rows, name="adamw_small")]
    off = 0
    for name in _SMALL_PARAMS:
        n = w[name].size
        delta[name], new_m[name], new_v[name] = (o[off:off + n].reshape(w[name].shape) for o in outs)
        off += n

    grad_x = dx[None]
    return (loss, grad_x, *[grads[n].reshape(w[n].shape) for n in _WEIGHT_ORDER],
            *[delta[n] for n in _WEIGHT_ORDER], *[new_m[n] for n in _WEIGHT_ORDER],
            *[new_v[n] for n in _WEIGHT_ORDER])
```

```python
import functools
import math

import jax
import jax.numpy as jnp
from jax import lax
from jax.experimental import pallas as pl
from jax.experimental.pallas import tpu as pltpu
from jax.experimental.pallas import tpu_sc as plsc

F32 = jnp.float32
BF16 = jnp.bfloat16

N_DEV = 8
D_MODEL = 1024
SEQ = 4096
DEPTH = 2
SSD_INNER = 512
SSD_HEADS = 8
SSD_HPG = 4
SSD_STATE = 128
SSD_CHUNK = 256
SSD_CONV_K = 4
SSD_CONV_CH = 1024
POOL_W = 256
POOL_WINDOWS = (2, 4, 8, 16)
ATT_W = 256
ATT_PATTERNS = ((128, 1), (512, 4), (2048, 16))
ATT_BLOCK = 128
ROT_DIM = 16
ROPE_THETA = 500000.0
IN_W = 2568
FFN_DIM = 2816
FFN_CONV_K = 3
NORM_EPS = 1e-6
HEAD_LANES = 64

ADAM_LR = 0.001
ADAM_B1 = 0.9
ADAM_B2 = 0.999
ADAM_EPS = 1e-08
ADAM_WD = 0.01
ADAM_STEP = 10

PROJ_W = 2816
PROJ_SSD_W = 1792
PROJ_Z_BLK = 2
PROJ_DT_BLK = 12
PROJ_POOL_BLK = 7
PROJ_Q_BLK, PROJ_K_BLK, PROJ_V_BLK = 16, 18, 20
MIX_POOL_BLK = 2
MIX_ATT_BLK = 6
VMEM_LIMIT = 56 * 1024 * 1024
CONV_HALO = 8
POOL_HALO = 16
ATT_KPAD = ATT_BLOCK * 16
MESH = pl.DeviceIdType.MESH
_HBM = pl.BlockSpec(memory_space=pl.ANY)


def _cparams(*sem):
    return pltpu.CompilerParams(dimension_semantics=sem, vmem_limit_bytes=VMEM_LIMIT)


def _silu(x):
    return x * jax.nn.sigmoid(x)


def _pick_lane(v, h):
    lane = lax.broadcasted_iota(jnp.int32, v.shape, 1)
    return jnp.sum(jnp.where(lane == h, v, 0.0), axis=1, keepdims=True)


def _pick_row(v, h):
    row = lax.broadcasted_iota(jnp.int32, v.shape, 0)
    return jnp.sum(jnp.where(row == h, v, 0.0), axis=0, keepdims=True)


def _head_of_lane(width):
    return lax.broadcasted_iota(jnp.int32, (1, width), 1) // HEAD_LANES


@functools.partial(jax.custom_vjp, nondiff_argnums=(1, 2))
def _shift_rows(x_ext, s, halo):
    y = x_ext if s == 0 else pltpu.roll(x_ext, s, 0)
    return y[halo:]


def _shift_rows_fwd(x_ext, s, halo):
    return _shift_rows(x_ext, s, halo), None


def _shift_rows_bwd(s, halo, _, g):
    ge = jnp.concatenate([jnp.zeros((halo, g.shape[1]), g.dtype), g], axis=0)
    return (ge if s == 0 else pltpu.roll(ge, ge.shape[0] - s, 0),)


_shift_rows.defvjp(_shift_rows_fwd, _shift_rows_bwd)


@functools.partial(jax.custom_vjp, nondiff_argnums=(1,))
def _roll_rows(x, s):
    return pltpu.roll(x, s, 0)


def _roll_rows_fwd(x, s):
    return _roll_rows(x, s), None


def _roll_rows_bwd(s, _, g):
    return (pltpu.roll(g, g.shape[0] - s, 0),)


_roll_rows.defvjp(_roll_rows_fwd, _roll_rows_bwd)


def _rms_modulate(xv, g, sc, sh):
    r = lax.rsqrt(jnp.mean(xv * xv, axis=-1, keepdims=True) + NORM_EPS)
    return (xv * r * g) * (1.0 + sc) + sh


def _mm(a, w, *, name, nt=False, tm=512, tn=None, out_dtype=F32, norm=None, residual=None):
    t, k = a.shape
    n = w.shape[0] if nt else w.shape[1]
    tn = tn or n
    assert tn == n or (norm is None and residual is None)
    extra_in = list(norm or ()) + list(residual or ())

    def body(*refs):
        a_ref, w_ref = refs[:2]
        ins = refs[2:2 + len(extra_in)]
        outs = refs[2 + len(extra_in):]
        if norm is None:
            av = a_ref[...].astype(BF16)
        else:
            av = _rms_modulate(a_ref[...], ins[0][...], ins[1][...], ins[2][...]).astype(BF16)
            outs[1][...] = av
        if nt:
            acc = lax.dot_general(av, w_ref[...], (((1,), (1,)), ((), ())), preferred_element_type=F32)
        else:
            acc = jnp.dot(av, w_ref[...], preferred_element_type=F32)
        outs[0][...] = acc.astype(out_dtype)
        if residual is not None:
            x_ref, gate_ref = ins[-2:]
            outs[-1][...] = x_ref[...] + gate_ref[...] * acc

    row = lambda width: pl.BlockSpec((1, width), lambda i, j: (0, 0))
    tile = lambda width: pl.BlockSpec((tm, width), lambda i, j: (i, 0))
    w_spec = pl.BlockSpec((tn, k), lambda i, j: (j, 0)) if nt else pl.BlockSpec((k, tn), lambda i, j: (0, j))
    in_specs = [tile(k), w_spec] + ([row(k)] * 3 if norm else []) + ([tile(n), row(n)] if residual else [])
    out_specs = [pl.BlockSpec((tm, tn), lambda i, j: (i, j))] + ([tile(k)] if norm else []) + \
        ([tile(n)] if residual else [])
    out_shape = [jax.ShapeDtypeStruct((t, n), out_dtype)] + \
        ([jax.ShapeDtypeStruct((t, k), BF16)] if norm else []) + \
        ([jax.ShapeDtypeStruct((t, n), F32)] if residual else [])
    outs = pl.pallas_call(
        body, grid=(t // tm, n // tn), in_specs=in_specs, out_specs=out_specs, out_shape=out_shape, name=name,
        compiler_params=_cparams("parallel", "parallel"))(a, w, *extra_in)
    return outs[0] if len(outs) == 1 else outs


def _wgrad(a, b, *, name, tk=None, tn=None, tt=512, out_dtype=BF16):
    t, k = a.shape
    n = b.shape[1]
    tk = tk or k
    tn = tn or n
    steps = t // tt

    def body(a_ref, b_ref, o_ref, acc_ref):
        s = pl.program_id(2)

        @pl.when(s == 0)
        def _():
            acc_ref[...] = jnp.zeros_like(acc_ref)

        acc_ref[...] += lax.dot_general(a_ref[...].astype(BF16), b_ref[...].astype(BF16),
                                        (((0,), (0,)), ((), ())), preferred_element_type=F32)

        @pl.when(s == steps - 1)
        def _():
            o_ref[...] = acc_ref[...].astype(out_dtype)

    return pl.pallas_call(
        body, grid=(k // tk, n // tn, steps),
        in_specs=[pl.BlockSpec((tt, tk), lambda i, j, s: (s, i)), pl.BlockSpec((tt, tn), lambda i, j, s: (s, j))],
        out_specs=pl.BlockSpec((tk, tn), lambda i, j, s: (i, j)),
        out_shape=jax.ShapeDtypeStruct((k, n), out_dtype),
        scratch_shapes=[pltpu.VMEM((tk, tn), F32)], name=name,
        compiler_params=_cparams("parallel", "parallel", "arbitrary"))(a, b)


def _norm_mod_bwd(x, dh, dres, g, sc, *, name, tm=512):
    s, d = x.shape
    steps = s // tm

    def body(x_ref, dh_ref, dres_ref, g_ref, sc_ref, dx_ref, dg_ref, dsc_ref, dsh_ref, da_acc, dsh_acc):
        i = pl.program_id(0)

        @pl.when(i == 0)
        def _():
            da_acc[...] = jnp.zeros_like(da_acc)
            dsh_acc[...] = jnp.zeros_like(dsh_acc)

        xv = x_ref[...]
        dhv = dh_ref[...].astype(F32)
        r = lax.rsqrt(jnp.mean(xv * xv, axis=-1, keepdims=True) + NORM_EPS)
        xhat = xv * r
        gain = g_ref[...] * (1.0 + sc_ref[...])
        dxhat = dhv * gain
        dx_ref[...] = dres_ref[...] + r * (dxhat - xhat * jnp.mean(dxhat * xhat, axis=-1, keepdims=True))
        da_acc[...] += jnp.sum(dhv * xhat, axis=0, keepdims=True)
        dsh_acc[...] += jnp.sum(dhv, axis=0, keepdims=True)

        @pl.when(i == steps - 1)
        def _():
            dg_ref[...] = da_acc[...] * (1.0 + sc_ref[...])
            dsc_ref[...] = da_acc[...] * g_ref[...]
            dsh_ref[...] = dsh_acc[...]

    row = pl.BlockSpec((1, d), lambda i: (0, 0))
    tile = pl.BlockSpec((tm, d), lambda i: (i, 0))
    row_shape = jax.ShapeDtypeStruct((1, d), F32)
    return pl.pallas_call(
        body, grid=(steps,), in_specs=[tile, tile, tile, row, row],
        out_specs=[tile, row, row, row],
        out_shape=[jax.ShapeDtypeStruct((s, d), F32), row_shape, row_shape, row_shape],
        scratch_shapes=[pltpu.VMEM((1, d), F32), pltpu.VMEM((1, d), F32)], name=name,
        compiler_params=_cparams("arbitrary"))(x, dh, dres, g, sc)


def _residual_bwd(dx, gate, f, *, name, tm=512):
    s, d = dx.shape
    steps = s // tm

    def body(dx_ref, g_ref, f_ref, df_ref, dg_ref):
        i = pl.program_id(0)

        @pl.when(i == 0)
        def _():
            dg_ref[...] = jnp.zeros_like(dg_ref)

        dxv = dx_ref[...]
        df_ref[...] = (g_ref[...] * dxv).astype(BF16)
        dg_ref[...] += jnp.sum(dxv * f_ref[...], axis=0, keepdims=True)

    tile = pl.BlockSpec((tm, d), lambda i: (i, 0))
    row = pl.BlockSpec((1, d), lambda i: (0, 0))
    return pl.pallas_call(
        body, grid=(steps,), in_specs=[tile, row, tile], out_specs=[tile, row],
        out_shape=[jax.ShapeDtypeStruct((s, d), BF16), jax.ShapeDtypeStruct((1, d), F32)], name=name,
        compiler_params=_cparams("arbitrary"))(dx, gate, f)


def _final_loss(x, g, target, *, name, tm=512):
    s, d = x.shape
    steps = s // tm

    def body(x_ref, g_ref, t_ref, loss_ref, dx_ref, dg_ref, sq_acc):
        i = pl.program_id(0)

        @pl.when(i == 0)
        def _():
            sq_acc[...] = jnp.zeros_like(sq_acc)
            dg_ref[...] = jnp.zeros_like(dg_ref)

        xv = x_ref[...]
        r = lax.rsqrt(jnp.mean(xv * xv, axis=-1, keepdims=True) + NORM_EPS)
        xhat = xv * r
        err = xhat * g_ref[...] - t_ref[...]
        sq_acc[...] += jnp.sum(err * err, axis=0, keepdims=True)
        dy = err * (1.0 / d)
        dg_ref[...] += jnp.sum(dy * xhat, axis=0, keepdims=True)
        dxhat = dy * g_ref[...]
        dx_ref[...] = r * (dxhat - xhat * jnp.mean(dxhat * xhat, axis=-1, keepdims=True))

        @pl.when(i == steps - 1)
        def _():
            total = jnp.sum(sq_acc[...], axis=1, keepdims=True) * (0.5 / d)
            loss_ref[...] = jnp.broadcast_to(total, loss_ref.shape)

    tile = pl.BlockSpec((tm, d), lambda i: (i, 0))
    row = pl.BlockSpec((1, d), lambda i: (0, 0))
    return pl.pallas_call(
        body, grid=(steps,), in_specs=[tile, row, tile],
        out_specs=[pl.BlockSpec((1, 128), lambda i: (0, 0)), tile, row],
        out_shape=[jax.ShapeDtypeStruct((1, 128), F32), jax.ShapeDtypeStruct((s, d), F32),
                   jax.ShapeDtypeStruct((1, d), F32)],
        scratch_shapes=[pltpu.VMEM((1, d), F32)], name=name, compiler_params=_cparams("arbitrary"))(x, g, target)


def _ssd_chunk(z, xbc_ext, dt_raw, conv_w, conv_b, dt_bias, a_log, d_skip, norm_g, h_in):
    q = z.shape[0]
    gw = SSD_HPG * HEAD_LANES
    xc = conv_b
    for k in range(SSD_CONV_K):
        xc = xc + _pick_row(conv_w, k) * _shift_rows(xbc_ext, SSD_CONV_K - 1 - k, CONV_HALO)
    xc = _silu(xc)
    dt = jax.nn.softplus(dt_raw + dt_bias)
    da = dt * (-jnp.exp(a_log))
    ri = lax.broadcasted_iota(jnp.int32, (q, q), 0)
    ci = lax.broadcasted_iota(jnp.int32, (q, q), 1)
    causal = ri >= ci
    tril = causal.astype(F32)
    a_cum = jnp.dot(tril, da, preferred_element_type=F32, precision=lax.Precision.HIGHEST)
    a_cum_t = lax.dot_general(da, tril, (((0,), (1,)), ((), ())), preferred_element_type=F32,
                              precision=lax.Precision.HIGHEST)
    a_last = _pick_row(a_cum, q - 1)
    head = _head_of_lane(gw)
    ys, hs = [], []
    for g in range(2):
        xs = xc[:, gw * g:gw * (g + 1)]
        bm = xc[:, SSD_INNER + SSD_STATE * g:SSD_INNER + SSD_STATE * (g + 1)]
        cm = xc[:, SSD_INNER + 2 * SSD_STATE + SSD_STATE * g:SSD_INNER + 2 * SSD_STATE + SSD_STATE * (g + 1)]
        cb = lax.dot_general(cm.astype(BF16), bm.astype(BF16), (((1,), (1,)), ((), ())), preferred_element_type=F32)
        cols = [_pick_lane(a_cum, SSD_HPG * g + j) for j in range(SSD_HPG)]
        lasts = [_pick_lane(a_last, SSD_HPG * g + j) for j in range(SSD_HPG)]
        dt_exp = sum(jnp.where(head == j, _pick_lane(dt, SSD_HPG * g + j), 0.0) for j in range(SSD_HPG))
        d_exp = sum(jnp.where(head == j, _pick_lane(d_skip, SSD_HPG * g + j), 0.0) for j in range(SSD_HPG))
        e_cum = sum(jnp.where(head == j, jnp.exp(cols[j]), 0.0) for j in range(SSD_HPG))
        c_dec = sum(jnp.where(head == j, jnp.exp(lasts[j]), 0.0) for j in range(SSD_HPG))
        xsdt = (xs * dt_exp).astype(BF16)
        y_diag = jnp.zeros((q, gw), F32)
        st_new = jnp.zeros((SSD_STATE, gw), F32)
        for j in range(SSD_HPG):
            row = _pick_row(a_cum_t, SSD_HPG * g + j)
            lmat = jnp.exp(jnp.where(causal, cols[j] - row, -jnp.inf))
            r = jnp.dot((cb * lmat).astype(BF16), xsdt, preferred_element_type=F32)
            y_diag = y_diag + jnp.where(head == j, r, 0.0)
            bd = (bm * jnp.exp(lasts[j] - cols[j])).astype(BF16)
            st = lax.dot_general(bd, xsdt, (((0,), (0,)), ((), ())), preferred_element_type=F32)
            st_new = st_new + jnp.where(head == j, st, 0.0)
        y_off = jnp.dot(cm.astype(BF16), h_in[g].astype(BF16), preferred_element_type=F32) * e_cum
        hs.append(h_in[g] * c_dec + st_new)
        y = y_diag + y_off + d_exp * xs
        yz = y * _silu(z[:, gw * g:gw * (g + 1)])
        yz = yz * lax.rsqrt(jnp.mean(yz * yz, axis=-1, keepdims=True) + NORM_EPS)
        ys.append(yz * norm_g[:, gw * g:gw * (g + 1)])
    return jnp.concatenate(ys, axis=1), tuple(hs)


_SSD_NCHUNK = SEQ // SSD_CHUNK
_HALO_PER_CHUNK = SSD_CHUNK // CONV_HALO


def _ssd_param_specs(const):
    return [pl.BlockSpec((8, SSD_CONV_CH), const), pl.BlockSpec((1, SSD_CONV_CH), const),
            pl.BlockSpec((1, 128), const), pl.BlockSpec((1, 128), const), pl.BlockSpec((1, 128), const),
            pl.BlockSpec((1, SSD_INNER), const)]


def _ssd_fwd(proj, conv_w, conv_b, dt_bias, a_log, d_skip, norm_g, *, name):
    q = SSD_CHUNK

    def body(z_ref, xbc_ref, halo_ref, dt_ref, cw_ref, cb_ref, db_ref, al_ref, d_ref, ng_ref, y_ref, hs_ref, h_acc):
        i = pl.program_id(0)

        @pl.when(i == 0)
        def _():
            h_acc[...] = jnp.zeros_like(h_acc)

        halo = jnp.where(i == 0, 0.0, halo_ref[...])
        xbc_ext = jnp.concatenate([halo, xbc_ref[...]], axis=0)
        h_in = (h_acc[0], h_acc[1])
        hs_ref[0, 0] = h_in[0]
        hs_ref[0, 1] = h_in[1]
        y, h_out = _ssd_chunk(z_ref[...], xbc_ext, dt_ref[...], cw_ref[...], cb_ref[...], db_ref[...], al_ref[...],
                              d_ref[...], ng_ref[...], h_in)
        y_ref[...] = y.astype(BF16)
        h_acc[0] = h_out[0]
        h_acc[1] = h_out[1]

    const = lambda i: (0, 0)
    return pl.pallas_call(
        body, grid=(_SSD_NCHUNK,),
        in_specs=[pl.BlockSpec((q, SSD_INNER), lambda i: (i, PROJ_Z_BLK)),
                  pl.BlockSpec((q, SSD_CONV_CH), lambda i: (i, 0)),
                  pl.BlockSpec((CONV_HALO, SSD_CONV_CH), lambda i: (jnp.maximum(i * _HALO_PER_CHUNK - 1, 0), 0)),
                  pl.BlockSpec((q, 128), lambda i: (i, PROJ_DT_BLK))] + _ssd_param_specs(const),
        out_specs=[pl.BlockSpec((q, SSD_INNER), lambda i: (i, 0)),
                   pl.BlockSpec((1, 2, SSD_STATE, 256), lambda i: (i, 0, 0, 0))],
        out_shape=[jax.ShapeDtypeStruct((SEQ, D_MODEL), BF16),
                   jax.ShapeDtypeStruct((_SSD_NCHUNK, 2, SSD_STATE, 256), F32)],
        scratch_shapes=[pltpu.VMEM((2, SSD_STATE, 256), F32)], name=name,
        compiler_params=_cparams("arbitrary"))(proj, proj, proj, proj, conv_w, conv_b, dt_bias, a_log, d_skip, norm_g)


def _ssd_bwd(proj, hstates, dmix, conv_w, conv_b, dt_bias, a_log, d_skip, norm_g, *, name):
    q = SSD_CHUNK
    last = _SSD_NCHUNK - 1

    def body(z_ref, xbc_ref, halo_ref, dt_ref, hs_ref, dy_ref, cw_ref, cb_ref, db_ref, al_ref, d_ref, ng_ref,
             dp_ref, dcw_ref, dcb_ref, ddb_ref, dal_ref, dd_ref, dng_ref, dh_acc, dhalo_acc):
        i = pl.program_id(0)

        @pl.when(i == 0)
        def _():
            dh_acc[...] = jnp.zeros_like(dh_acc)
            dhalo_acc[...] = jnp.zeros_like(dhalo_acc)
            for r in (dcw_ref, dcb_ref, ddb_ref, dal_ref, dd_ref, dng_ref):
                r[...] = jnp.zeros_like(r)

        halo = jnp.where(i == last, 0.0, halo_ref[...])
        xbc_ext = jnp.concatenate([halo, xbc_ref[...]], axis=0)
        _, vjp = jax.vjp(_ssd_chunk, z_ref[...], xbc_ext, dt_ref[...], cw_ref[...], cb_ref[...], db_ref[...],
                         al_ref[...], d_ref[...], ng_ref[...], (hs_ref[0, 0], hs_ref[0, 1]))
        gz, gx, gdt, gcw, gcb, gdb, gal, gd, gng, gh = vjp((dy_ref[...], (dh_acc[0], dh_acc[1])))
        dxbc = jnp.concatenate([gx[CONV_HALO:q], gx[q:] + dhalo_acc[...]], axis=0)
        dp_ref[...] = jnp.concatenate([dxbc, gz, gdt, jnp.zeros_like(gdt)], axis=1).astype(BF16)
        dhalo_acc[...] = gx[:CONV_HALO]
        dh_acc[0] = gh[0]
        dh_acc[1] = gh[1]
        dcw_ref[...] += gcw
        dcb_ref[...] += gcb
        ddb_ref[...] += gdb
        dal_ref[...] += gal
        dd_ref[...] += gd
        dng_ref[...] += gng

    const = lambda i: (0, 0)
    rev = lambda i: last - i
    row = lambda n: jax.ShapeDtypeStruct((1, n), F32)
    return pl.pallas_call(
        body, grid=(_SSD_NCHUNK,),
        in_specs=[pl.BlockSpec((q, SSD_INNER), lambda i: (rev(i), PROJ_Z_BLK)),
                  pl.BlockSpec((q, SSD_CONV_CH), lambda i: (rev(i), 0)),
                  pl.BlockSpec((CONV_HALO, SSD_CONV_CH), lambda i: (jnp.maximum(rev(i) * _HALO_PER_CHUNK - 1, 0), 0)),
                  pl.BlockSpec((q, 128), lambda i: (rev(i), PROJ_DT_BLK)),
                  pl.BlockSpec((1, 2, SSD_STATE, 256), lambda i: (rev(i), 0, 0, 0)),
                  pl.BlockSpec((q, SSD_INNER), lambda i: (rev(i), 0))] + _ssd_param_specs(const),
        out_specs=[pl.BlockSpec((q, PROJ_SSD_W), lambda i: (rev(i), 0))] + _ssd_param_specs(const),
        out_shape=[jax.ShapeDtypeStruct((SEQ, PROJ_W), BF16), jax.ShapeDtypeStruct((8, SSD_CONV_CH), F32),
                   row(SSD_CONV_CH), row(128), row(128), row(128), row(SSD_INNER)],
        scratch_shapes=[pltpu.VMEM((2, SSD_STATE, 256), F32), pltpu.VMEM((CONV_HALO, SSD_CONV_CH), F32)], name=name,
        compiler_params=_cparams("arbitrary"))(proj, proj, proj, proj, hstates, dmix, conv_w, conv_b, dt_bias, a_log,
                                                d_skip, norm_g)


def _rope_tables(pos_col, inv_freq_lane, *, name):
    s = pos_col.shape[0]

    def body(p_ref, f_ref, c_ref, s1_ref, s2_ref):
        ang = p_ref[...] * f_ref[...]
        within = lax.broadcasted_iota(jnp.int32, ang.shape, 1) % HEAD_LANES
        half = ROT_DIM // 2
        c_ref[...] = jnp.where(within < ROT_DIM, jnp.cos(ang), 1.0)
        sn = jnp.sin(ang)
        s1_ref[...] = jnp.where(within < half, -sn, 0.0)
        s2_ref[...] = jnp.where((within >= half) & (within < ROT_DIM), sn, 0.0)

    shp = jax.ShapeDtypeStruct((s, 128), F32)
    return pl.pallas_call(body, out_shape=[shp, shp, shp], name=name,
                          compiler_params=pltpu.CompilerParams(vmem_limit_bytes=VMEM_LIMIT))(pos_col, inv_freq_lane)


def _rope(t, c, s1, s2):
    half = ROT_DIM // 2
    return t * c + pltpu.roll(t, 128 - half, 1) * s1 + pltpu.roll(t, half, 1) * s2


def _rope_t(g, c, s1, s2):
    half = ROT_DIM // 2
    return g * c + pltpu.roll(g * s1, half, 1) + pltpu.roll(g * s2, 128 - half, 1)


def _att_valid(b):
    qi = lax.broadcasted_iota(jnp.int32, (ATT_BLOCK, 2 * ATT_BLOCK), 0)
    kj = lax.broadcasted_iota(jnp.int32, (ATT_BLOCK, 2 * ATT_BLOCK), 1)
    rel = qi + ATT_BLOCK - kj
    return (rel >= 0) & (rel <= ATT_BLOCK) & (b * ATT_BLOCK + kj - ATT_BLOCK >= 0)


def _att_slices(i, d):
    if d == 1:
        qstart = pl.multiple_of(i * ATT_BLOCK, ATT_BLOCK)
        return i, pl.ds(qstart, ATT_BLOCK), pl.ds(pl.multiple_of(qstart - ATT_BLOCK + ATT_KPAD, ATT_BLOCK), 2 * ATT_BLOCK)
    r = i % d
    b = i // d
    qstart = r + d * ATT_BLOCK * b
    return b, pl.ds(qstart, ATT_BLOCK, stride=d), pl.ds(qstart - ATT_BLOCK * d + ATT_KPAD, 2 * ATT_BLOCK, stride=d)


_ATT_NBLK = SEQ // ATT_BLOCK
_ATT_SCALE = HEAD_LANES ** -0.5
_ATT_UNROLL = 4


def _att_fwd(proj, cos, sin1, sin2, mix, *, name):
    s = SEQ

    def body(q_ref, k_ref, v_ref, c_ref, s1_ref, s2_ref, _, o_ref, lse_ref, mix_ref, qs, ks, vs, acc, m_s, l_s):
        c, s1, s2 = c_ref[...], s1_ref[...], s2_ref[...]
        qs[...] = _rope(q_ref[...], c, s1, s2) * _ATT_SCALE
        zeros = jnp.zeros((ATT_KPAD, 128), F32)
        ks[pl.ds(0, ATT_KPAD), :] = zeros
        vs[pl.ds(0, ATT_KPAD), :] = zeros
        ks[pl.ds(ATT_KPAD, s), :] = _rope(k_ref[...], c, s1, s2)
        vs[pl.ds(ATT_KPAD, s), :] = v_ref[...]
        head0 = _head_of_lane(128) == 0

        for bi, (_, d) in enumerate(ATT_PATTERNS):
            def blk(i, carry, d=d, first=(bi == 0)):
                b, sq, sk = _att_slices(i, d)
                qb = qs[sq, :]
                kw = ks[sk, :].astype(BF16)
                vw = vs[sk, :].astype(BF16)
                valid = _att_valid(b)
                ms, ls, os_ = [], [], []
                for hh in range(2):
                    qh = jnp.where(head0 if hh == 0 else ~head0, qb, 0.0).astype(BF16)
                    sc = lax.dot_general(qh, kw, (((1,), (1,)), ((), ())), preferred_element_type=F32)
                    sc = jnp.where(valid, sc, -jnp.inf)
                    mb = jnp.max(sc, axis=1, keepdims=True)
                    p = jnp.exp(sc - mb)
                    ms.append(mb)
                    ls.append(jnp.sum(p, axis=1, keepdims=True))
                    os_.append(jnp.dot(p.astype(BF16), vw, preferred_element_type=F32))
                m_b = jnp.where(head0, ms[0], ms[1])
                l_b = jnp.where(head0, ls[0], ls[1])
                o_b = jnp.where(head0, os_[0], os_[1])
                if first:
                    m_s[sq, :] = m_b
                    l_s[sq, :] = l_b
                    acc[sq, :] = o_b
                else:
                    m_old = m_s[sq, :]
                    m_new = jnp.maximum(m_old, m_b)
                    a_old = jnp.exp(m_old - m_new)
                    a_b = jnp.exp(m_b - m_new)
                    m_s[sq, :] = m_new
                    l_s[sq, :] = l_s[sq, :] * a_old + l_b * a_b
                    acc[sq, :] = acc[sq, :] * a_old + o_b * a_b
                return carry

            lax.fori_loop(0, _ATT_NBLK, blk, 0, unroll=_ATT_UNROLL)

        out = acc[...] / l_s[...]
        o_ref[...] = out
        mix_ref[...] = out.astype(BF16)
        lse_ref[...] = m_s[...] + jnp.log(l_s[...])

    col = lambda base: pl.BlockSpec((s, 128), lambda p: (0, base + p))
    tab = pl.BlockSpec((s, 128), lambda p: (0, 0))
    big = pltpu.VMEM((ATT_KPAD + s, 128), F32)
    tok = pltpu.VMEM((s, 128), F32)
    return pl.pallas_call(
        body, grid=(2,), in_specs=[col(PROJ_Q_BLK), col(PROJ_K_BLK), col(PROJ_V_BLK), tab, tab, tab, _HBM],
        out_specs=[pl.BlockSpec((s, 128), lambda p: (0, p)), pl.BlockSpec((s, 128), lambda p: (0, p)),
                   col(MIX_ATT_BLK)],
        out_shape=[jax.ShapeDtypeStruct((s, ATT_W), F32), jax.ShapeDtypeStruct((s, ATT_W), F32),
                   jax.ShapeDtypeStruct(mix.shape, mix.dtype)],
        input_output_aliases={6: 2}, scratch_shapes=[tok, big, big, tok, tok, tok], name=name,
        compiler_params=_cparams("arbitrary"))(proj, proj, proj, cos, sin1, sin2, mix)


def _att_bwd(proj, cos, sin1, sin2, out, lse, dmix, dproj, *, name):
    s = SEQ

    def body(proj_ref, c_hbm, s1_hbm, s2_hbm, out_hbm, lse_hbm, dmix_hbm, _, dproj_hbm,
             c_ref, s1_ref, s2_ref, o_ref, lse_ref, do_ref, qs, ks, vs, dqs, dks, dvs, staged, sems):
        def start(copies):
            for cp in copies:
                cp.start()
            return copies

        def load(pair):
            lanes = pl.ds(128 * pair, 128)
            rows = pl.ds(ATT_KPAD, s)
            return start([
                pltpu.make_async_copy(proj_ref.at[:, pl.ds(128 * (PROJ_Q_BLK + pair), 128)], qs, sems.at[0]),
                pltpu.make_async_copy(proj_ref.at[:, pl.ds(128 * (PROJ_K_BLK + pair), 128)], ks.at[rows, :], sems.at[1]),
                pltpu.make_async_copy(proj_ref.at[:, pl.ds(128 * (PROJ_V_BLK + pair), 128)], vs.at[rows, :], sems.at[2]),
                pltpu.make_async_copy(out_hbm.at[:, lanes], o_ref, sems.at[3]),
                pltpu.make_async_copy(lse_hbm.at[:, lanes], lse_ref, sems.at[4]),
                pltpu.make_async_copy(dmix_hbm.at[:, pl.ds(128 * (MIX_ATT_BLK + pair), 128)], do_ref, sems.at[5])])

        tables = start([pltpu.make_async_copy(c_hbm, c_ref, sems.at[6]),
                        pltpu.make_async_copy(s1_hbm, s1_ref, sems.at[7]),
                        pltpu.make_async_copy(s2_hbm, s2_ref, sems.at[8])])
        loads = load(0)
        for cp in tables:
            cp.wait()
        head0 = _head_of_lane(128) == 0
        zeros = jnp.zeros((ATT_KPAD, 128), F32)
        for pair in range(2):
            for cp in loads:
                cp.wait()
            c, s1, s2 = c_ref[...], s1_ref[...], s2_ref[...]
            qs[...] = _rope(qs[...], c, s1, s2) * _ATT_SCALE
            ks[pl.ds(0, ATT_KPAD), :] = zeros
            vs[pl.ds(0, ATT_KPAD), :] = zeros
            ks[pl.ds(ATT_KPAD, s), :] = _rope(ks[pl.ds(ATT_KPAD, s), :], c, s1, s2)
            dqs[...] = jnp.zeros_like(dqs)
            dks[...] = jnp.zeros_like(dks)
            dvs[...] = jnp.zeros_like(dvs)

            for _, d in ATT_PATTERNS:
                def blk(i, carry, d=d):
                    b, sq, sk = _att_slices(i, d)
                    qb = qs[sq, :]
                    kw = ks[sk, :].astype(BF16)
                    vw = vs[sk, :].astype(BF16)
                    dob = do_ref[sq, :]
                    lse_b = lse_ref[sq, :]
                    dd = dob * o_ref[sq, :]
                    valid = _att_valid(b)
                    dq_b = jnp.zeros((ATT_BLOCK, 128), F32)
                    dk_w = jnp.zeros((2 * ATT_BLOCK, 128), F32)
                    dv_w = jnp.zeros((2 * ATT_BLOCK, 128), F32)
                    for hh in range(2):
                        hm = head0 if hh == 0 else ~head0
                        qh = jnp.where(hm, qb, 0.0).astype(BF16)
                        doh = jnp.where(hm, dob, 0.0).astype(BF16)
                        lse_h = _pick_lane(lse_b, hh * HEAD_LANES)
                        d_h = jnp.sum(jnp.where(hm, dd, 0.0), axis=1, keepdims=True)
                        sc = lax.dot_general(qh, kw, (((1,), (1,)), ((), ())), preferred_element_type=F32)
                        p = jnp.where(valid, jnp.exp(sc - lse_h), 0.0)
                        dp = lax.dot_general(doh, vw, (((1,), (1,)), ((), ())), preferred_element_type=F32)
                        ds = (p * (dp - d_h)).astype(BF16)
                        dq_b = dq_b + jnp.where(hm, jnp.dot(ds, kw, preferred_element_type=F32), 0.0)
                        dk_w = dk_w + lax.dot_general(ds, qh, (((0,), (0,)), ((), ())), preferred_element_type=F32)
                        dv_w = dv_w + lax.dot_general(p.astype(BF16), doh, (((0,), (0,)), ((), ())),
                                                      preferred_element_type=F32)
                    dqs[sq, :] += dq_b
                    dks[sk, :] += dk_w
                    dvs[sk, :] += dv_w
                    return carry

                lax.fori_loop(0, _ATT_NBLK, blk, 0, unroll=_ATT_UNROLL)

            staged[0] = _rope_t(dqs[...] * _ATT_SCALE, c, s1, s2).astype(BF16)
            staged[1] = _rope_t(dks[pl.ds(ATT_KPAD, s), :], c, s1, s2).astype(BF16)
            staged[2] = dvs[pl.ds(ATT_KPAD, s), :].astype(BF16)
            stores = start([
                pltpu.make_async_copy(staged.at[j], dproj_hbm.at[:, pl.ds(128 * (col + pair), 128)], sems.at[9 + j])
                for j, col in enumerate((PROJ_Q_BLK, PROJ_K_BLK, PROJ_V_BLK))])
            if pair == 0:
                loads = load(1)
            for cp in stores:
                cp.wait()

    big = pltpu.VMEM((ATT_KPAD + s, 128), F32)
    tok = pltpu.VMEM((s, 128), F32)
    return pl.pallas_call(
        body, in_specs=[_HBM] * 8, out_specs=_HBM, out_shape=jax.ShapeDtypeStruct(dproj.shape, dproj.dtype),
        input_output_aliases={7: 0},
        scratch_shapes=[tok] * 6 + [tok, big, big, tok, big, big, pltpu.VMEM((3, s, 128), BF16),
                                    pltpu.SemaphoreType.DMA((12,))], name=name,
        compiler_params=pltpu.CompilerParams(vmem_limit_bytes=VMEM_LIMIT))(
            proj, cos, sin1, sin2, out, lse, dmix, dproj)


_POOL_TM = 512
_POOL_NT = SEQ // _POOL_TM
_POOL_HALO_PER_TILE = _POOL_TM // POOL_HALO


def _pool_tile(u_ext, w_bd, scale, t0):
    s2 = u_ext + _roll_rows(u_ext, 1)
    s4 = s2 + _roll_rows(s2, 2)
    s8 = s4 + _roll_rows(s4, 4)
    s16 = s8 + _roll_rows(s8, 8)
    grp = _head_of_lane(POOL_W)
    sel = jnp.where(grp == 0, s2, jnp.where(grp == 1, s4, jnp.where(grp == 2, s8, s16)))[POOL_HALO:]
    t = sel.shape[0]
    pos = t0 + lax.broadcasted_iota(jnp.int32, (t, POOL_W), 0) + 1
    win = jnp.where(grp == 0, 2, jnp.where(grp == 1, 4, jnp.where(grp == 2, 8, 16)))
    cnt = jnp.minimum(pos, win).astype(F32)
    diff = sel / cnt - u_ext[POOL_HALO:]
    return jnp.dot(diff.astype(BF16), w_bd.astype(BF16), preferred_element_type=F32) * scale


def _pool_fwd(proj, w_bd, scale, mix, *, name):
    tm = _POOL_TM

    def body(u_ref, halo_ref, w_ref, sc_ref, _, y_ref):
        i = pl.program_id(0)
        halo = jnp.where(i == 0, 0.0, halo_ref[...])
        u_ext = jnp.concatenate([halo, u_ref[...]], axis=0)
        y_ref[...] = _pool_tile(u_ext, w_ref[...], sc_ref[...], i * tm).astype(BF16)

    return pl.pallas_call(
        body, grid=(_POOL_NT,),
        in_specs=[pl.BlockSpec((tm, POOL_W), lambda i: (i, PROJ_POOL_BLK)),
                  pl.BlockSpec((POOL_HALO, POOL_W),
                               lambda i: (jnp.maximum(i * _POOL_HALO_PER_TILE - 1, 0), PROJ_POOL_BLK)),
                  pl.BlockSpec((POOL_W, POOL_W), lambda i: (0, 0)), pl.BlockSpec((1, POOL_W), lambda i: (0, 0)), _HBM],
        out_specs=pl.BlockSpec((tm, POOL_W), lambda i: (i, MIX_POOL_BLK)),
        out_shape=jax.ShapeDtypeStruct(mix.shape, mix.dtype), input_output_aliases={4: 0}, name=name,
        compiler_params=_cparams("parallel"))(proj, proj, w_bd, scale, mix)


def _pool_bwd(proj, dmix, w_bd, scale, dproj, *, name):
    tm = _POOL_TM
    last = _POOL_NT - 1

    def body(u_ref, halo_ref, dy_ref, w_ref, sc_ref, _, du_ref, dw_ref, dsc_ref, dhalo_acc):
        i = pl.program_id(0)

        @pl.when(i == 0)
        def _():
            dhalo_acc[...] = jnp.zeros_like(dhalo_acc)
            dw_ref[...] = jnp.zeros_like(dw_ref)
            dsc_ref[...] = jnp.zeros_like(dsc_ref)

        tile = last - i
        halo = jnp.where(tile == 0, 0.0, halo_ref[...])
        u_ext = jnp.concatenate([halo, u_ref[...]], axis=0)
        _, vjp = jax.vjp(functools.partial(_pool_tile, t0=tile * tm), u_ext, w_ref[...], sc_ref[...])
        gu, gw, gs = vjp(dy_ref[...])
        du_ref[...] = jnp.concatenate([gu[POOL_HALO:tm], gu[tm:] + dhalo_acc[...]], axis=0).astype(BF16)
        dhalo_acc[...] = gu[:POOL_HALO]
        dw_ref[...] += gw
        dsc_ref[...] += gs

    rev = lambda i: last - i
    return pl.pallas_call(
        body, grid=(_POOL_NT,),
        in_specs=[pl.BlockSpec((tm, POOL_W), lambda i: (rev(i), PROJ_POOL_BLK)),
                  pl.BlockSpec((POOL_HALO, POOL_W),
                               lambda i: (jnp.maximum(rev(i) * _POOL_HALO_PER_TILE - 1, 0), PROJ_POOL_BLK)),
                  pl.BlockSpec((tm, POOL_W), lambda i: (rev(i), MIX_POOL_BLK)),
                  pl.BlockSpec((POOL_W, POOL_W), lambda i: (0, 0)), pl.BlockSpec((1, POOL_W), lambda i: (0, 0)), _HBM],
        out_specs=[pl.BlockSpec((tm, POOL_W), lambda i: (rev(i), PROJ_POOL_BLK)),
                   pl.BlockSpec((POOL_W, POOL_W), lambda i: (0, 0)), pl.BlockSpec((1, POOL_W), lambda i: (0, 0))],
        out_shape=[jax.ShapeDtypeStruct(dproj.shape, dproj.dtype), jax.ShapeDtypeStruct((POOL_W, POOL_W), F32),
                   jax.ShapeDtypeStruct((1, POOL_W), F32)],
        input_output_aliases={5: 0}, scratch_shapes=[pltpu.VMEM((POOL_HALO, POOL_W), F32)], name=name,
        compiler_params=_cparams("arbitrary"))(proj, proj, dmix, w_bd, scale, dproj)


_FFN_TM = 256
_FFN_NT = SEQ // _FFN_TM
_FFN_HALO_PER_TILE = _FFN_TM // CONV_HALO


def _ffn_act_tile(hid_ext, conv_w, conv_b):
    hc = conv_b
    for k in range(FFN_CONV_K):
        hc = hc + _pick_row(conv_w, k) * _shift_rows(hid_ext, FFN_CONV_K - 1 - k, CONV_HALO)
    return _silu(hc[:, :FFN_DIM]) * hc[:, FFN_DIM:]


def _resident(shape):
    return pl.BlockSpec(shape, lambda i: (0,) * len(shape), pipeline_mode=pl.Buffered(1))


def _ffn_fwd(x1, norm_g, sc, sh, gate, up_t, down, conv_w, conv_b, *, name):
    tm = _FFN_TM
    w = 2 * FFN_DIM
    d = D_MODEL

    def body(x_ref, ng_ref, sc_ref, sh_ref, g_ref, up_ref, dn_ref, cw_ref, cb_ref,
             h_ref, hid_ref, act_ref, f_ref, x2_ref, halo_acc):
        i = pl.program_id(0)
        h2 = _rms_modulate(x_ref[...], ng_ref[...], sc_ref[...], sh_ref[...]).astype(BF16)
        h_ref[...] = h2
        hid = lax.dot_general(h2, up_ref[...], (((1,), (1,)), ((), ())), preferred_element_type=F32)
        hid_ref[...] = hid
        halo = jnp.where(i == 0, 0.0, halo_acc[...])
        act = _ffn_act_tile(jnp.concatenate([halo, hid], axis=0), cw_ref[...], cb_ref[...]).astype(BF16)
        halo_acc[...] = hid[tm - CONV_HALO:]
        act_ref[...] = act
        f = jnp.dot(act, dn_ref[...], preferred_element_type=F32)
        f_ref[...] = f
        x2_ref[...] = x_ref[...] + g_ref[...] * f

    tile = lambda n: pl.BlockSpec((tm, n), lambda i: (i, 0))
    return pl.pallas_call(
        body, grid=(_FFN_NT,),
        in_specs=[tile(d)] + [_resident((1, d))] * 4 + [_resident((w, d)), _resident((FFN_DIM, d)),
                                                        _resident((8, w)), _resident((1, w))],
        out_specs=[tile(d), tile(w), tile(FFN_DIM), tile(d), tile(d)],
        out_shape=[jax.ShapeDtypeStruct((SEQ, d), BF16), jax.ShapeDtypeStruct((SEQ, w), F32),
                   jax.ShapeDtypeStruct((SEQ, FFN_DIM), BF16), jax.ShapeDtypeStruct((SEQ, d), F32),
                   jax.ShapeDtypeStruct((SEQ, d), F32)],
        scratch_shapes=[pltpu.VMEM((CONV_HALO, w), F32)], name=name,
        compiler_params=_cparams("arbitrary"))(x1, norm_g, sc, sh, gate, up_t, down, conv_w, conv_b)


def _ffn_bwd(df, hid, up_t, down, conv_w, conv_b, *, name):
    tm = _FFN_TM
    w = 2 * FFN_DIM
    d = D_MODEL
    last = _FFN_NT - 1

    def body(df_ref, h_ref, halo_ref, up_ref, dn_ref, cw_ref, cb_ref, dh_ref, dh2_ref, dcw_ref, dcb_ref, dhalo_acc):
        i = pl.program_id(0)

        @pl.when(i == 0)
        def _():
            dhalo_acc[...] = jnp.zeros_like(dhalo_acc)
            dcw_ref[...] = jnp.zeros_like(dcw_ref)
            dcb_ref[...] = jnp.zeros_like(dcb_ref)

        dact = lax.dot_general(df_ref[...], dn_ref[...], (((1,), (1,)), ((), ())), preferred_element_type=F32)
        halo = jnp.where(i == last, 0.0, halo_ref[...])
        hid_ext = jnp.concatenate([halo, h_ref[...]], axis=0)
        _, vjp = jax.vjp(_ffn_act_tile, hid_ext, cw_ref[...], cb_ref[...])
        gh, gw, gb = vjp(dact)
        dhid = jnp.concatenate([gh[CONV_HALO:tm], gh[tm:] + dhalo_acc[...]], axis=0).astype(BF16)
        dhalo_acc[...] = gh[:CONV_HALO]
        dh_ref[...] = dhid
        dh2_ref[...] = jnp.dot(dhid, up_ref[...], preferred_element_type=F32)
        dcw_ref[...] += gw
        dcb_ref[...] += gb

    rev = lambda i: last - i
    tile = lambda n: pl.BlockSpec((tm, n), lambda i: (rev(i), 0))
    acc = lambda shape: pl.BlockSpec(shape, lambda i: (0, 0))
    return pl.pallas_call(
        body, grid=(_FFN_NT,),
        in_specs=[tile(d), tile(w),
                  pl.BlockSpec((CONV_HALO, w), lambda i: (jnp.maximum(rev(i) * _FFN_HALO_PER_TILE - 1, 0), 0)),
                  _resident((w, d)), _resident((FFN_DIM, d)), _resident((8, w)), _resident((1, w))],
        out_specs=[tile(w), tile(d), acc((8, w)), acc((1, w))],
        out_shape=[jax.ShapeDtypeStruct((SEQ, w), BF16), jax.ShapeDtypeStruct((SEQ, d), F32),
                   jax.ShapeDtypeStruct((8, w), F32), jax.ShapeDtypeStruct((1, w), F32)],
        scratch_shapes=[pltpu.VMEM((CONV_HALO, w), F32)], name=name,
        compiler_params=_cparams("arbitrary"))(df, hid, hid, up_t, down, conv_w, conv_b)


def _axes():
    return lax.axis_index("x"), lax.axis_index("y"), lax.axis_index("c")


def _handshake(peers):
    barrier = pltpu.get_barrier_semaphore()
    for peer in peers:
        pl.semaphore_signal(barrier, inc=1, device_id=peer, device_id_type=MESH)
    pl.semaphore_wait(barrier, len(peers))


def _allgather_body(x_refs, out_refs, send_sems, recv_sems, local_sems, own_barrier):
    n = len(x_refs)
    x, y, c = _axes()
    me, sibling = (x, y, c), (x, y, 1 - c)
    chips = [(1 - x, y), (x, 1 - y), (1 - x, 1 - y)]
    if own_barrier:
        _handshake([sibling] + [(*chip, c) for chip in chips])

    def slot(a, px, py, pc):
        return out_refs[a].at[4 * px + 2 * py + pc]

    def copy(a, k, block, to, src=None):
        return pltpu.make_async_remote_copy(
            src_ref=slot(a, *block) if src is None else src, dst_ref=slot(a, *block),
            send_sem=send_sems.at[a, k], recv_sem=recv_sems.at[a, k], device_id=to, device_id_type=MESH)

    mines, firsts = [], []
    for a in range(n):
        mines.append(pltpu.make_async_copy(x_refs[a], slot(a, *me), local_sems.at[a]))
        mines[-1].start()
        first = [copy(a, 0, me, sibling, src=x_refs[a])]
        first += [copy(a, 1 + j, me, (*chip, c), src=x_refs[a]) for j, chip in enumerate(chips)]
        for cp in first:
            cp.start()
        firsts += first
    passed = []
    for j, chip in enumerate(chips):
        for a in range(n):
            copy(a, 1 + j, (*chip, c), me).wait_recv()
            passed.append(copy(a, 4 + j, (*chip, c), sibling))
            passed[-1].start()
    for a in range(n):
        copy(a, 0, sibling, me).wait_recv()
    for j, chip in enumerate(chips):
        for a in range(n):
            copy(a, 4 + j, (*chip, 1 - c), me).wait_recv()
    for cp in firsts + passed:
        cp.wait_send()
    for cp in mines:
        cp.wait()


def _allgather_sems(n):
    return [pltpu.SemaphoreType.DMA((n, 7)), pltpu.SemaphoreType.DMA((n, 7)), pltpu.SemaphoreType.DMA((n,))]


def _allgather(xs, *, name):
    n = len(xs)

    def body(*refs):
        _allgather_body(refs[:n], refs[n:2 * n], *refs[2 * n:], own_barrier=False)

    return pl.pallas_call(
        body, out_shape=[jax.ShapeDtypeStruct((N_DEV,) + xb.shape, xb.dtype) for xb in xs],
        in_specs=[_HBM] * n, out_specs=[_HBM] * n, scratch_shapes=_allgather_sems(n), name=name)(*xs)


def _allgather_async(xs, *, name, collective_id):
    n = len(xs)
    x_refs = [jax.new_ref(xb, memory_space=pltpu.MemorySpace.HBM) for xb in xs]
    out_refs = [jax.empty_ref(jax.ShapeDtypeStruct((N_DEV,) + xb.shape, xb.dtype), memory_space=pltpu.MemorySpace.HBM)
                for xb in xs]

    @pl.kernel(mesh=plsc.ScalarSubcoreMesh(axis_name="sequencer", num_cores=1), name=name,
               scratch_types=tuple(_allgather_sems(n)),
               compiler_params=pltpu.CompilerParams(collective_id=collective_id))
    def launch(send_sems, recv_sems, local_sems):
        _allgather_body(x_refs, out_refs, send_sems, recv_sems, local_sems, own_barrier=True)

    launch()
    return [r[...] for r in out_refs]


def _pair_exchange(blocks, *, name, collective_id):
    n = len(blocks)
    hbm = pltpu.MemorySpace.HBM
    in_refs = [jax.new_ref(b, memory_space=hbm) for b in blocks]
    out_refs = [jax.empty_ref(jax.ShapeDtypeStruct((4,) + b.shape[1:], b.dtype), memory_space=hbm) for b in blocks]

    @pl.kernel(mesh=plsc.ScalarSubcoreMesh(axis_name="sequencer", num_cores=1), name=name,
               scratch_types=(pltpu.SemaphoreType.DMA((n, 4)), pltpu.SemaphoreType.DMA((n, 4))),
               compiler_params=pltpu.CompilerParams(collective_id=collective_id))
    def launch(send_sems, recv_sems):
        x, y, c = _axes()
        _handshake([(x, y, 1 - c)])
        copies = [pltpu.make_async_remote_copy(
            src_ref=in_refs[a].at[2 * s + (1 - c)], dst_ref=out_refs[a].at[s], send_sem=send_sems.at[a, s],
            recv_sem=recv_sems.at[a, s], device_id=(x, y, 1 - c), device_id_type=MESH)
            for a in range(n) for s in range(4)]
        for cp in copies:
            cp.start()
        for cp in copies:
            cp.wait_recv()
        for cp in copies:
            cp.wait_send()

    launch()
    return [r[...] for r in out_refs]


def _chip_exchange(parts, *, name, collective_id):
    n = len(parts)
    hbm = pltpu.MemorySpace.HBM
    in_refs = [jax.new_ref(p, memory_space=hbm) for p in parts]
    out_refs = [jax.empty_ref(jax.ShapeDtypeStruct(p.shape, p.dtype), memory_space=hbm) for p in parts]

    @pl.kernel(mesh=plsc.ScalarSubcoreMesh(axis_name="sequencer", num_cores=1), name=name,
               scratch_types=(pltpu.SemaphoreType.DMA((n, 3)), pltpu.SemaphoreType.DMA((n, 3)),
                              pltpu.SemaphoreType.DMA((n,))),
               compiler_params=pltpu.CompilerParams(collective_id=collective_id))
    def launch(send_sems, recv_sems, local_sems):
        x, y, c = _axes()
        my_chip = 2 * x + y
        chips = [(1 - x, y), (x, 1 - y), (1 - x, 1 - y)]
        _handshake([(*chip, c) for chip in chips])
        locals_ = [pltpu.make_async_copy(in_refs[a].at[my_chip], out_refs[a].at[my_chip], local_sems.at[a])
                   for a in range(n)]
        for cp in locals_:
            cp.start()
        copies = [pltpu.make_async_remote_copy(
            src_ref=in_refs[a].at[2 * px + py], dst_ref=out_refs[a].at[my_chip], send_sem=send_sems.at[a, k],
            recv_sem=recv_sems.at[a, k], device_id=(px, py, c), device_id_type=MESH)
            for a in range(n) for k, (px, py) in enumerate(chips)]
        for cp in copies:
            cp.start()
        for cp in copies:
            cp.wait_recv()
        for cp in copies:
            cp.wait_send()
        for cp in locals_:
            cp.wait()

    launch()
    return [r[...] for r in out_refs]


def _pair_sum(core, blocks, from_sibling, *, name):
    _, r, cdim = blocks.shape

    def body(core_ref, a_ref, b_ref, o_ref):
        o_ref[...] = (a_ref[...].astype(F32) + b_ref[...].astype(F32)).astype(o_ref.dtype)

    return pl.pallas_call(
        body,
        grid_spec=pltpu.PrefetchScalarGridSpec(
            num_scalar_prefetch=1, grid=(4,),
            in_specs=[pl.BlockSpec((1, r, cdim), lambda s, core_ref: (2 * s + core_ref[0], 0, 0)),
                      pl.BlockSpec((1, r, cdim), lambda s, core_ref: (s, 0, 0))],
            out_specs=pl.BlockSpec((1, r, cdim), lambda s, core_ref: (s, 0, 0))),
        out_shape=jax.ShapeDtypeStruct(from_sibling.shape, from_sibling.dtype), name=name,
        compiler_params=_cparams("parallel"))(core, blocks, from_sibling)


def _sum_blocks(a, *, name, tr=None):
    n, r, cdim = a.shape
    tr = tr or r

    def body(a_ref, o_ref):
        acc = a_ref[0].astype(F32)
        for k in range(1, n):
            acc = acc + a_ref[k].astype(F32)
        o_ref[...] = acc

    return pl.pallas_call(body, grid=(r // tr,), in_specs=[pl.BlockSpec((n, tr, cdim), lambda i: (0, i, 0))],
                          out_specs=pl.BlockSpec((tr, cdim), lambda i: (i, 0)),
                          out_shape=jax.ShapeDtypeStruct((r, cdim), F32), name=name,
                          compiler_params=_cparams("parallel"))(a)


_ADA_SHARD = 6 * D_MODEL // N_DEV


def _ada_mod(c_all, ada_w, *, name):
    def body(c_ref, w_ref, o_ref):
        o_ref[0] = jnp.dot(_silu(c_ref[...]).astype(BF16), w_ref[0].astype(BF16), preferred_element_type=F32)

    return pl.pallas_call(
        body, grid=(DEPTH,),
        in_specs=[pl.BlockSpec((N_DEV, D_MODEL), lambda l: (0, 0)),
                  pl.BlockSpec((1, D_MODEL, _ADA_SHARD), lambda l: (l, 0, 0))],
        out_specs=pl.BlockSpec((1, N_DEV, _ADA_SHARD), lambda l: (l, 0, 0)),
        out_shape=jax.ShapeDtypeStruct((DEPTH, N_DEV, _ADA_SHARD), F32), name=name,
        compiler_params=_cparams("parallel"))(c_all, ada_w)


def _ada_wgrad(c_all, dmod_cols, *, name):
    def body(c_ref, d_ref, o_ref):
        o_ref[0] = lax.dot_general(_silu(c_ref[...]), d_ref[0], (((0,), (0,)), ((), ())),
                                   preferred_element_type=F32, precision=lax.Precision.HIGHEST)

    return pl.pallas_call(
        body, grid=(DEPTH,),
        in_specs=[pl.BlockSpec((N_DEV, D_MODEL), lambda l: (0, 0)),
                  pl.BlockSpec((1, N_DEV, _ADA_SHARD), lambda l: (l, 0, 0))],
        out_specs=pl.BlockSpec((1, D_MODEL, _ADA_SHARD), lambda l: (l, 0, 0)),
        out_shape=jax.ShapeDtypeStruct((DEPTH, D_MODEL, _ADA_SHARD), F32), name=name,
        compiler_params=_cparams("parallel"))(c_all, dmod_cols)


def _add_rows(a, b, *, name):
    def body(a_ref, b_ref, o_ref):
        o_ref[...] = a_ref[...] + b_ref[...]

    return pl.pallas_call(body, out_shape=jax.ShapeDtypeStruct(a.shape, a.dtype), name=name)(a, b)


def _adamw(w, g, m, v, *, name, tr):
    r, cdim = w.shape
    c1 = 1.0 - ADAM_B1 ** ADAM_STEP
    c2 = 1.0 - ADAM_B2 ** ADAM_STEP

    def body(w_ref, g_ref, m_ref, v_ref, d_ref, mo_ref, vo_ref):
        gv = g_ref[...]
        mn = ADAM_B1 * m_ref[...] + (1.0 - ADAM_B1) * gv
        vn = ADAM_B2 * v_ref[...] + (1.0 - ADAM_B2) * (gv * gv)
        mo_ref[...] = mn
        vo_ref[...] = vn
        d_ref[...] = -ADAM_LR * ((mn / c1) / (jnp.sqrt(vn / c2) + ADAM_EPS) + ADAM_WD * w_ref[...])

    spec = pl.BlockSpec((tr, cdim), lambda i: (i, 0))
    shp = jax.ShapeDtypeStruct((r, cdim), F32)
    return pl.pallas_call(body, grid=(r // tr,), in_specs=[spec] * 4, out_specs=[spec] * 3, out_shape=[shp] * 3,
                          name=name, compiler_params=_cparams("parallel"))(w, g, m, v)


def _pad_rows(a, rows):
    return jnp.concatenate([a, jnp.zeros((rows - a.shape[0],) + a.shape[1:], a.dtype)], axis=0)


def _pad_lanes(a, lanes):
    return jnp.concatenate([a, jnp.zeros(a.shape[:-1] + (lanes - a.shape[-1],), a.dtype)], axis=-1)


def _permute_w_in(wt):
    return jnp.concatenate([wt[512:1536], wt[:512], wt[1536:1544],
                            jnp.zeros((PROJ_W - IN_W, wt.shape[1]), wt.dtype), wt[1544:]], axis=0)


def _unpermute_w_in(wp):
    return jnp.concatenate([wp[1024:1536], wp[:1024], wp[1536:1544], wp[PROJ_SSD_W:]], axis=0)


def _block_diag(w):
    rows = []
    for g in range(4):
        rows.append(jnp.concatenate([w[g] if k == g else jnp.zeros_like(w[g]) for k in range(4)], axis=1))
    return jnp.concatenate(rows, axis=0)


def _diag_blocks(wbd):
    return jnp.stack([wbd[64 * g:64 * (g + 1), 64 * g:64 * (g + 1)] for g in range(4)], axis=0)


def _layer_params(l, small):
    return dict(
        norm1_g=small["norm1_g"][l][None], norm2_g=small["norm2_g"][l][None],
        conv_w=_pad_rows(small["ssd_conv_w"][l], 8), conv_b=small["ssd_conv_b"][l][None],
        dt_bias=_pad_lanes(small["ssd_dt_bias"][l][None], 128), a_log=_pad_lanes(small["ssd_a_log"][l][None], 128),
        d_skip=_pad_lanes(small["ssd_d"][l][None], 128), ssd_norm_g=small["ssd_norm_g"][l][None],
        pool_bd=_block_diag(small["pool_w"][l]), pool_scale=small["pool_scale"][l][None],
        fcw=_pad_rows(small["ffn_conv_w"][l], 8), fcb=small["ffn_conv_b"][l][None])


def _mod_rows(mod_l):
    return [mod_l[None, D_MODEL * i:D_MODEL * (i + 1)] for i in range(6)]


def _layer_fwd(x, mod_l, p, tabs, l, gather):
    sh1, sc1, g1, sh2, sc2, g2 = _mod_rows(mod_l)
    mix_w = gather(l, "mix", None)
    p.update(w_in=mix_w["w_in"], w_out=mix_w["w_out"])
    proj, h1 = _mm(x, p["w_in"], nt=True, norm=(p["norm1_g"], sc1, sh1), name=f"l{l}_proj")
    ffn_w = gather(l, "ffn", proj)
    p.update(up=ffn_w["ffn_up"], down=ffn_w["ffn_down"])
    mix, hst = _ssd_fwd(proj, p["conv_w"], p["conv_b"], p["dt_bias"], p["a_log"], p["d_skip"], p["ssd_norm_g"],
                        name=f"l{l}_ssd")
    mix = _pool_fwd(proj, p["pool_bd"], p["pool_scale"], mix, name=f"l{l}_pool")
    y_att, lse, mix = _att_fwd(proj, *tabs, mix, name=f"l{l}_att")
    mo, x1 = _mm(mix, p["w_out"], residual=(x, g1), name=f"l{l}_out")
    gather(l + 1, "mix", (x1, p["up"]))
    h2, hid, act, f, x2 = _ffn_fwd(x1, p["norm2_g"], sc2, sh2, g2, p["up"], p["down"], p["fcw"], p["fcb"],
                                   name=f"l{l}_ffn")
    return x2, dict(x=x, h1=h1, proj=proj, hst=hst, y_att=y_att, lse=lse, mix=mix, mo=mo, x1=x1, h2=h2, hid=hid,
                    act=act, f=f)


def _layer_bwd(dx2, sv, mod_l, p, tabs, l, exchange):
    sh1, sc1, g1, sh2, sc2, g2 = _mod_rows(mod_l)
    df, dg2 = _residual_bwd(dx2, g2, sv["f"], name=f"l{l}_res2_b")
    d_down = _wgrad(sv["act"], df, name=f"l{l}_down_bw")
    dhid, dh2, dfcw, dfcb = _ffn_bwd(df, sv["hid"], p["up"], p["down"], p["fcw"], p["fcb"], name=f"l{l}_ffn_b")
    d_up = _wgrad(dhid, sv["h2"], tk=1408, name=f"l{l}_up_bw")
    exchange(l, "ffn", dict(ffn_up=d_up, ffn_down=d_down))
    dx1, dn2, dsc2, dsh2 = _norm_mod_bwd(sv["x1"], dh2, dx2, p["norm2_g"], sc2, name=f"l{l}_norm2_b")
    dmo, dg1 = _residual_bwd(dx1, g1, sv["mo"], name=f"l{l}_res1_b")
    dmix = _mm(dmo, p["w_out"], nt=True, name=f"l{l}_out_bx")
    d_wout = _wgrad(sv["mix"], dmo, name=f"l{l}_out_bw")
    dproj, dcw, dcb, ddb, dal, dd, dng = _ssd_bwd(
        sv["proj"], sv["hst"], dmix, p["conv_w"], p["conv_b"], p["dt_bias"], p["a_log"], p["d_skip"],
        p["ssd_norm_g"], name=f"l{l}_ssd_b")
    dproj, dwbd, dpsc = _pool_bwd(sv["proj"], dmix, p["pool_bd"], p["pool_scale"], dproj, name=f"l{l}_pool_b")
    dproj = _att_bwd(sv["proj"], *tabs, sv["y_att"], sv["lse"], dmix, dproj, name=f"l{l}_att_b")
    dh1 = _mm(dproj, p["w_in"], name=f"l{l}_proj_bx")
    d_win = _wgrad(dproj, sv["h1"], name=f"l{l}_proj_bw")
    exchange(l, "mix", dict(w_in=d_win, w_out=d_wout))
    dx0, dn1, dsc1, dsh1 = _norm_mod_bwd(sv["x"], dh1, dx1, p["norm1_g"], sc1, name=f"l{l}_norm1_b")
    dmod = jnp.concatenate([dsh1, dsc1, dg1, dsh2, dsc2, dg2], axis=1)[0]
    small = dict(norm1_g=dn1[0], ssd_conv_w=dcw[:SSD_CONV_K], ssd_conv_b=dcb[0], ssd_dt_bias=ddb[0], ssd_a_log=dal[0],
                 ssd_d=dd[0], ssd_norm_g=dng[0], pool_w=_diag_blocks(dwbd), pool_scale=dpsc[0], norm2_g=dn2[0],
                 ffn_conv_w=dfcw[:FFN_CONV_K], ffn_conv_b=dfcb[0])
    return dx0, dmod, small


def _example_step(x, target, pos_col, inv_freq_lane, mod, gather, small, final_g, exchange):
    tabs = _rope_tables(pos_col, inv_freq_lane, name="rope_tables")
    params, saved = [], []
    for l in range(DEPTH):
        params.append(_layer_params(l, small))
        x, sv = _layer_fwd(x, mod[l], params[l], tabs, l, gather)
        saved.append(sv)
    loss_row, dx, dfg = _final_loss(x, final_g[None], target, name="final_loss")
    dmods, smalls = [None] * DEPTH, [None] * DEPTH
    for l in reversed(range(DEPTH)):
        dx, dmods[l], smalls[l] = _layer_bwd(dx, saved[l], mod[l], params[l], tabs, l, exchange)
    return loss_row, dx, jnp.stack(dmods, axis=0), smalls, dfg[0]


_BIG = (("w_in", (D_MODEL, IN_W // N_DEV)), ("w_out", (D_MODEL // N_DEV, D_MODEL)),
        ("ffn_up", (D_MODEL, 2 * FFN_DIM // N_DEV)), ("ffn_down", (FFN_DIM // N_DEV, D_MODEL)))
_PACK_LANES = 1024

_SMALL_GRADS = (("norm1_g", D_MODEL), ("ssd_conv_w", SSD_CONV_K * SSD_CONV_CH), ("ssd_conv_b", SSD_CONV_CH),
                ("ssd_dt_bias", 128), ("ssd_a_log", 128), ("ssd_d", 128), ("ssd_norm_g", SSD_INNER),
                ("pool_w", 4 * 64 * 64), ("pool_scale", POOL_W), ("norm2_g", D_MODEL),
                ("ffn_conv_w", FFN_CONV_K * 2 * FFN_DIM), ("ffn_conv_b", 2 * FFN_DIM))
_SMALL_PARAMS = ("ada_b", "norm1_g", "ssd_conv_w", "ssd_conv_b", "ssd_dt_bias", "ssd_a_log", "ssd_d", "ssd_norm_g",
                 "pool_w", "pool_scale", "norm2_g", "ffn_conv_w", "ffn_conv_b", "final_g")
_WEIGHT_ORDER = ("ada_w", "ada_b", "norm1_g", "w_in", "ssd_conv_w", "ssd_conv_b", "ssd_dt_bias", "ssd_a_log", "ssd_d",
                 "ssd_norm_g", "pool_w", "pool_scale", "w_out", "norm2_g", "ffn_up", "ffn_conv_w", "ffn_conv_b",
                 "ffn_down", "final_g")


def _pack(parts, rows, dtype):
    flat = jnp.concatenate([p.reshape(-1).astype(dtype) for p in parts])
    return jnp.concatenate([flat, jnp.zeros((rows * _PACK_LANES - flat.shape[0],), dtype)]).reshape(rows, _PACK_LANES)


_COLUMN_SHARDED = ("w_in", "ffn_up")
_GROUPS = (("mix", ("w_in", "w_out")), ("ffn", ("ffn_up", "ffn_down")))


def _big_shares(w, l, names):
    return [(w[name][l].T if name in _COLUMN_SHARDED else w[name][l]).astype(BF16) for name in names]


def _unshard_big(names, gathered):
    out = {}
    for name, g in zip(names, gathered):
        full = g.reshape(N_DEV * g.shape[1], g.shape[2])
        out[name] = _permute_w_in(full) if name == "w_in" else full
    return out


def _shard_big(grads):
    out = []
    for name, g in grads.items():
        g = _unpermute_w_in(g) if name == "w_in" else g
        out.append(g.reshape(N_DEV, g.shape[0] // N_DEV, g.shape[1]))
    return out


def kernel(x, c, positions, ada_w, ada_b, norm1_g, w_in, ssd_conv_w, ssd_conv_b, ssd_dt_bias, ssd_a_log, ssd_d, ssd_norm_g, pool_w, pool_scale, w_out, norm2_g, ffn_up, ffn_conv_w, ffn_conv_b, ffn_down, final_g, loss_target, m_ada_w, m_ada_b, m_norm1_g, m_w_in, m_ssd_conv_w, m_ssd_conv_b, m_ssd_dt_bias, m_ssd_a_log, m_ssd_d, m_ssd_norm_g, m_pool_w, m_pool_scale, m_w_out, m_norm2_g, m_ffn_up, m_ffn_conv_w, m_ffn_conv_b, m_ffn_down, m_final_g, v_ada_w, v_ada_b, v_norm1_g, v_w_in, v_ssd_conv_w, v_ssd_conv_b, v_ssd_dt_bias, v_ssd_a_log, v_ssd_d, v_ssd_norm_g, v_pool_w, v_pool_scale, v_w_out, v_norm2_g, v_ffn_up, v_ffn_conv_w, v_ffn_conv_b, v_ffn_down, v_final_g):
    w = dict(ada_w=ada_w, ada_b=ada_b, norm1_g=norm1_g, w_in=w_in, ssd_conv_w=ssd_conv_w, ssd_conv_b=ssd_conv_b,
             ssd_dt_bias=ssd_dt_bias, ssd_a_log=ssd_a_log, ssd_d=ssd_d, ssd_norm_g=ssd_norm_g, pool_w=pool_w,
             pool_scale=pool_scale, w_out=w_out, norm2_g=norm2_g, ffn_up=ffn_up, ffn_conv_w=ffn_conv_w,
             ffn_conv_b=ffn_conv_b, ffn_down=ffn_down, final_g=final_g)
    m = dict(ada_w=m_ada_w, ada_b=m_ada_b, norm1_g=m_norm1_g, w_in=m_w_in, ssd_conv_w=m_ssd_conv_w,
             ssd_conv_b=m_ssd_conv_b, ssd_dt_bias=m_ssd_dt_bias, ssd_a_log=m_ssd_a_log, ssd_d=m_ssd_d,
             ssd_norm_g=m_ssd_norm_g, pool_w=m_pool_w, pool_scale=m_pool_scale, w_out=m_w_out, norm2_g=m_norm2_g,
             ffn_up=m_ffn_up, ffn_conv_w=m_ffn_conv_w, ffn_conv_b=m_ffn_conv_b, ffn_down=m_ffn_down,
             final_g=m_final_g)
    v = dict(ada_w=v_ada_w, ada_b=v_ada_b, norm1_g=v_norm1_g, w_in=v_w_in, ssd_conv_w=v_ssd_conv_w,
             ssd_conv_b=v_ssd_conv_b, ssd_dt_bias=v_ssd_dt_bias, ssd_a_log=v_ssd_a_log, ssd_d=v_ssd_d,
             ssd_norm_g=v_ssd_norm_g, pool_w=v_pool_w, pool_scale=v_pool_scale, w_out=v_w_out, norm2_g=v_norm2_g,
             ffn_up=v_ffn_up, ffn_conv_w=v_ffn_conv_w, ffn_conv_b=v_ffn_conv_b, ffn_down=v_ffn_down,
             final_g=v_final_g)
    ix, iy, ic = _axes()
    dev = 4 * ix + 2 * iy + ic

    n_scw, n_fcw = ssd_conv_w.size, ffn_conv_w.size
    small_all = _allgather([_pack([c, ssd_conv_w, ffn_conv_w], 8, F32)], name="gather_small")[0].reshape(N_DEV, -1)
    c_all = small_all[:, :D_MODEL]
    scw = small_all[:, D_MODEL:D_MODEL + n_scw].reshape(N_DEV, DEPTH, SSD_CONV_K, -1)
    scw = scw.transpose(1, 2, 0, 3).reshape(DEPTH, SSD_CONV_K, SSD_CONV_CH)
    fcw = small_all[:, D_MODEL + n_scw:D_MODEL + n_scw + n_fcw].reshape(N_DEV, DEPTH, FFN_CONV_K, -1)
    fcw = fcw.transpose(1, 2, 0, 3).reshape(DEPTH, FFN_CONV_K, 2 * FFN_DIM)

    mod_cols = _ada_mod(c_all, ada_w, name="ada_mod")
    mod_all = _allgather([mod_cols.reshape(DEPTH * N_DEV, _ADA_SHARD)], name="gather_mod")[0]
    mod_all = mod_all.reshape(N_DEV, DEPTH, N_DEV, _ADA_SHARD)
    mod_mine = lax.dynamic_index_in_dim(mod_all, dev, axis=2, keepdims=False)
    mod = _add_rows(mod_mine.transpose(1, 0, 2).reshape(DEPTH, 6 * D_MODEL), ada_b, name="ada_bias")

    fetched = {}

    def gather(l, group, after):
        if l < DEPTH and (l, group) not in fetched:
            names = dict(_GROUPS)[group]
            shares, _ = lax.optimization_barrier((_big_shares(w, l, names), small_all if after is None else after))
            got = _allgather_async(shares, name=f"gather_weights_l{l}_{group}",
                                   collective_id=1 + 2 * l + (group == "ffn"))
            fetched[l, group] = _unshard_big(names, got)
        return fetched.get((l, group))

    core = ic.astype(jnp.int32).reshape(1)
    from_chips = {}

    def exchange(l, group, g):
        cid = 5 + 4 * l + 2 * (group == "mix")
        blocks = _shard_big(g)
        from_sibling = _pair_exchange(blocks, name=f"grads_pair_exchange_l{l}_{group}", collective_id=cid)
        parts = [_pair_sum(core, b, s, name=f"grads_pair_sum_l{l}_{name}")
                 for name, b, s in zip(g, blocks, from_sibling)]
        got = _chip_exchange(parts, name=f"grads_chip_exchange_l{l}_{group}", collective_id=cid + 1)
        from_chips.update({(l, name): t for name, t in zip(g, got)})

    small = dict(norm1_g=norm1_g, norm2_g=norm2_g, ssd_conv_w=scw, ssd_conv_b=ssd_conv_b, ssd_dt_bias=ssd_dt_bias,
                 ssd_a_log=ssd_a_log, ssd_d=ssd_d, ssd_norm_g=ssd_norm_g, pool_w=pool_w, pool_scale=pool_scale,
                 ffn_conv_w=fcw, ffn_conv_b=ffn_conv_b)

    inv_freq = ROPE_THETA ** (-jnp.arange(0, ROT_DIM, 2, dtype=F32) / ROT_DIM)
    lane = jnp.arange(128) % HEAD_LANES
    inv_freq_lane = jnp.where(lane < ROT_DIM, inv_freq[lane % (ROT_DIM // 2)], 0.0)[None, :]
    pos_col = positions.reshape(SEQ, 1).astype(F32)
    loss_row, dx, dmod, g_small, g_final = _example_step(
        x[0], loss_target[0], pos_col, inv_freq_lane, mod, gather, small, final_g, exchange)

    grads = {}
    for name, _ in _BIG:
        per_layer = [_sum_blocks(from_chips[l, name], name=f"grads_chip_sum_l{l}_{name}") for l in range(DEPTH)]
        grads[name] = jnp.stack([g.T if name in _COLUMN_SHARDED else g for g in per_layer], axis=0)

    small_parts = [loss_row, dmod] + [jnp.stack([g_small[l][name] for l in range(DEPTH)], axis=0)
                                      for name, _ in _SMALL_GRADS] + [g_final]
    n_small = sum(p.size for p in small_parts)
    small_rows = -(-n_small // _PACK_LANES)
    gathered = _allgather([_pack(small_parts, small_rows, F32)], name="gather_small_grads")[0]
    total = _sum_blocks(gathered, name="sum_small_grads").reshape(-1)
    loss = total[0]
    off = 128
    grads["ada_b"] = total[off:off + DEPTH * 6 * D_MODEL].reshape(DEPTH, 6 * D_MODEL)
    dmod_all = gathered.reshape(N_DEV, -1)[:, off:off + DEPTH * 6 * D_MODEL].reshape(N_DEV, DEPTH, 6 * D_MODEL)
    off += DEPTH * 6 * D_MODEL
    for name, n in _SMALL_GRADS:
        grads[name] = total[off:off + DEPTH * n].reshape(DEPTH, n)
        off += DEPTH * n
    grads["final_g"] = total[off:off + D_MODEL]
    dmod_cols = lax.dynamic_slice_in_dim(dmod_all, dev * _ADA_SHARD, _ADA_SHARD, axis=2).transpose(1, 0, 2)
    grads["ada_w"] = _ada_wgrad(c_all, dmod_cols, name="ada_wgrad")
    for name in ("ssd_dt_bias", "ssd_a_log", "ssd_d"):
        grads[name] = grads[name][:, :SSD_HEADS]
    grads["pool_w"] = grads["pool_w"].reshape(pool_w.shape)
    grads["ssd_conv_w"] = lax.dynamic_slice_in_dim(
        grads["ssd_conv_w"].reshape(DEPTH, SSD_CONV_K, SSD_CONV_CH), dev * ssd_conv_w.shape[2], ssd_conv_w.shape[2], axis=2)
    grads["ffn_conv_w"] = lax.dynamic_slice_in_dim(
        grads["ffn_conv_w"].reshape(DEPTH, FFN_CONV_K, 2 * FFN_DIM), dev * ffn_conv_w.shape[2], ffn_conv_w.shape[2], axis=2)

    delta, new_m, new_v = {}, {}, {}
    for name, tr in (("ada_w", 512), ("w_in", 512), ("w_out", 256), ("ffn_up", 512), ("ffn_down", 352)):
        shp = w[name].shape
        two_d = lambda a: a.reshape(shp[0] * shp[1], shp[2])
        d_, m_, v_ = _adamw(two_d(w[name]), two_d(grads[name]), two_d(m[name]), two_d(v[name]), tr=tr,
                               name=f"adamw_{name}")
        delta[name], new_m[name], new_v[name] = (t.reshape(shp) for t in (d_, m_, v_))
    n_packed = sum(w[name].size for name in _SMALL_PARAMS)
    rows = -(-n_packed // _PACK_LANES)
    packed = [_pack([t[name] for name in _SMALL_PARAMS], rows, F32) for t in (w, grads, m, v)]
    outs = [o.reshape(-1) for o in _adamw(*packed, tr=rows, name="adamw_small")]
    off = 0
    for name in _SMALL_PARAMS:
        n = w[name].size
        delta[name], new_m[name], new_v[name] = (o[off:off + n].reshape(w[name].shape) for o in outs)
        off += n

    grad_x = dx[None]
    return (loss, grad_x, *[grads[n].reshape(w[n].shape) for n in _WEIGHT_ORDER],
            *[delta[n] for n in _WEIGHT_ORDER], *[new_m[n] for n in _WEIGHT_ORDER],
            *[new_v[n] for n in _WEIGHT_ORDER])
```

```python
import functools
import math

import jax
import jax.numpy as jnp
from jax import lax
from jax.experimental import pallas as pl
from jax.experimental.pallas import tpu as pltpu
from jax.experimental.pallas import tpu_sc as plsc

F32 = jnp.float32
BF16 = jnp.bfloat16

N_DEV = 8
D_MODEL = 1024
SEQ = 4096
DEPTH = 2
SSD_INNER = 512
SSD_HEADS = 8
SSD_HPG = 4
SSD_STATE = 128
SSD_CHUNK = 256
SSD_CONV_K = 4
SSD_CONV_CH = 1024
POOL_W = 256
POOL_WINDOWS = (2, 4, 8, 16)
ATT_W = 256
ATT_PATTERNS = ((128, 1), (512, 4), (2048, 16))
ATT_BLOCK = 128
ROT_DIM = 16
ROPE_THETA = 500000.0
IN_W = 2568
FFN_DIM = 2816
FFN_CONV_K = 3
NORM_EPS = 1e-6
HEAD_LANES = 64

ADAM_LR = 0.001
ADAM_B1 = 0.9
ADAM_B2 = 0.999
ADAM_EPS = 1e-08
ADAM_WD = 0.01
ADAM_STEP = 10

PROJ_W = 2816
PROJ_SSD_W = 1792
PROJ_Z_BLK = 2
PROJ_DT_BLK = 12
PROJ_POOL_BLK = 7
PROJ_Q_BLK, PROJ_K_BLK, PROJ_V_BLK = 16, 18, 20
MIX_POOL_BLK = 2
MIX_ATT_BLK = 6
VMEM_LIMIT = 56 * 1024 * 1024
CONV_HALO = 8
POOL_HALO = 16
ATT_KPAD = ATT_BLOCK * 16
MESH = pl.DeviceIdType.MESH
_HBM = pl.BlockSpec(memory_space=pl.ANY)


def _cparams(*sem):
    return pltpu.CompilerParams(dimension_semantics=sem, vmem_limit_bytes=VMEM_LIMIT)


def _silu(x):
    return x * jax.nn.sigmoid(x)


def _pick_lane(v, h):
    lane = lax.broadcasted_iota(jnp.int32, v.shape, 1)
    return jnp.sum(jnp.where(lane == h, v, 0.0), axis=1, keepdims=True)


def _pick_row(v, h):
    row = lax.broadcasted_iota(jnp.int32, v.shape, 0)
    return jnp.sum(jnp.where(row == h, v, 0.0), axis=0, keepdims=True)


def _head_of_lane(width):
    return lax.broadcasted_iota(jnp.int32, (1, width), 1) // HEAD_LANES


@functools.partial(jax.custom_vjp, nondiff_argnums=(1, 2))
def _shift_rows(x_ext, s, halo):
    y = x_ext if s == 0 else pltpu.roll(x_ext, s, 0)
    return y[halo:]


def _shift_rows_fwd(x_ext, s, halo):
    return _shift_rows(x_ext, s, halo), None


def _shift_rows_bwd(s, halo, _, g):
    ge = jnp.concatenate([jnp.zeros((halo, g.shape[1]), g.dtype), g], axis=0)
    return (ge if s == 0 else pltpu.roll(ge, ge.shape[0] - s, 0),)


_shift_rows.defvjp(_shift_rows_fwd, _shift_rows_bwd)


@functools.partial(jax.custom_vjp, nondiff_argnums=(1,))
def _roll_rows(x, s):
    return pltpu.roll(x, s, 0)


def _roll_rows_fwd(x, s):
    return _roll_rows(x, s), None


def _roll_rows_bwd(s, _, g):
    return (pltpu.roll(g, g.shape[0] - s, 0),)


_roll_rows.defvjp(_roll_rows_fwd, _roll_rows_bwd)


def _rms_modulate(xv, g, sc, sh):
    r = lax.rsqrt(jnp.mean(xv * xv, axis=-1, keepdims=True) + NORM_EPS)
    return (xv * r * g) * (1.0 + sc) + sh


def _mm(a, w, *, name, nt=False, tm=512, tn=None, out_dtype=F32, norm=None, residual=None):
    t, k = a.shape
    n = w.shape[0] if nt else w.shape[1]
    tn = tn or n
    assert tn == n or (norm is None and residual is None)
    extra_in = list(norm or ()) + list(residual or ())

    def body(*refs):
        a_ref, w_ref = refs[:2]
        ins = refs[2:2 + len(extra_in)]
        outs = refs[2 + len(extra_in):]
        if norm is None:
            av = a_ref[...].astype(BF16)
        else:
            av = _rms_modulate(a_ref[...], ins[0][...], ins[1][...], ins[2][...]).astype(BF16)
            outs[1][...] = av
        if nt:
            acc = lax.dot_general(av, w_ref[...], (((1,), (1,)), ((), ())), preferred_element_type=F32)
        else:
            acc = jnp.dot(av, w_ref[...], preferred_element_type=F32)
        outs[0][...] = acc.astype(out_dtype)
        if residual is not None:
            x_ref, gate_ref = ins[-2:]
            outs[-1][...] = x_ref[...] + gate_ref[...] * acc

    row = lambda width: pl.BlockSpec((1, width), lambda i, j: (0, 0))
    tile = lambda width: pl.BlockSpec((tm, width), lambda i, j: (i, 0))
    w_spec = pl.BlockSpec((tn, k), lambda i, j: (j, 0)) if nt else pl.BlockSpec((k, tn), lambda i, j: (0, j))
    in_specs = [tile(k), w_spec] + ([row(k)] * 3 if norm else []) + ([tile(n), row(n)] if residual else [])
    out_specs = [pl.BlockSpec((tm, tn), lambda i, j: (i, j))] + ([tile(k)] if norm else []) + \
        ([tile(n)] if residual else [])
    out_shape = [jax.ShapeDtypeStruct((t, n), out_dtype)] + \
        ([jax.ShapeDtypeStruct((t, k), BF16)] if norm else []) + \
        ([jax.ShapeDtypeStruct((t, n), F32)] if residual else [])
    outs = pl.pallas_call(
        body, grid=(t // tm, n // tn), in_specs=in_specs, out_specs=out_specs, out_shape=out_shape, name=name,
        compiler_params=_cparams("parallel", "parallel"))(a, w, *extra_in)
    return outs[0] if len(outs) == 1 else outs


def _wgrad(a, b, *, name, tk=None, tn=None, tt=512, out_dtype=BF16):
    t, k = a.shape
    n = b.shape[1]
    tk = tk or k
    tn = tn or n
    steps = t // tt

    def body(a_ref, b_ref, o_ref, acc_ref):
        s = pl.program_id(2)

        @pl.when(s == 0)
        def _():
            acc_ref[...] = jnp.zeros_like(acc_ref)

        acc_ref[...] += lax.dot_general(a_ref[...].astype(BF16), b_ref[...].astype(BF16),
                                        (((0,), (0,)), ((), ())), preferred_element_type=F32)

        @pl.when(s == steps - 1)
        def _():
            o_ref[...] = acc_ref[...].astype(out_dtype)

    return pl.pallas_call(
        body, grid=(k // tk, n // tn, steps),
        in_specs=[pl.BlockSpec((tt, tk), lambda i, j, s: (s, i)), pl.BlockSpec((tt, tn), lambda i, j, s: (s, j))],
        out_specs=pl.BlockSpec((tk, tn), lambda i, j, s: (i, j)),
        out_shape=jax.ShapeDtypeStruct((k, n), out_dtype),
        scratch_shapes=[pltpu.VMEM((tk, tn), F32)], name=name,
        compiler_params=_cparams("parallel", "parallel", "arbitrary"))(a, b)


def _norm_mod_bwd(x, dh, dres, g, sc, *, name, tm=512):
    s, d = x.shape
    steps = s // tm

    def body(x_ref, dh_ref, dres_ref, g_ref, sc_ref, dx_ref, dg_ref, dsc_ref, dsh_ref, da_acc, dsh_acc):
        i = pl.program_id(0)

        @pl.when(i == 0)
        def _():
            da_acc[...] = jnp.zeros_like(da_acc)
            dsh_acc[...] = jnp.zeros_like(dsh_acc)

        xv = x_ref[...]
        dhv = dh_ref[...].astype(F32)
        r = lax.rsqrt(jnp.mean(xv * xv, axis=-1, keepdims=True) + NORM_EPS)
        xhat = xv * r
        gain = g_ref[...] * (1.0 + sc_ref[...])
        dxhat = dhv * gain
        dx_ref[...] = dres_ref[...] + r * (dxhat - xhat * jnp.mean(dxhat * xhat, axis=-1, keepdims=True))
        da_acc[...] += jnp.sum(dhv * xhat, axis=0, keepdims=True)
        dsh_acc[...] += jnp.sum(dhv, axis=0, keepdims=True)

        @pl.when(i == steps - 1)
        def _():
            dg_ref[...] = da_acc[...] * (1.0 + sc_ref[...])
            dsc_ref[...] = da_acc[...] * g_ref[...]
            dsh_ref[...] = dsh_acc[...]

    row = pl.BlockSpec((1, d), lambda i: (0, 0))
    tile = pl.BlockSpec((tm, d), lambda i: (i, 0))
    row_shape = jax.ShapeDtypeStruct((1, d), F32)
    return pl.pallas_call(
        body, grid=(steps,), in_specs=[tile, tile, tile, row, row],
        out_specs=[tile, row, row, row],
        out_shape=[jax.ShapeDtypeStruct((s, d), F32), row_shape, row_shape, row_shape],
        scratch_shapes=[pltpu.VMEM((1, d), F32), pltpu.VMEM((1, d), F32)], name=name,
        compiler_params=_cparams("arbitrary"))(x, dh, dres, g, sc)


def _residual_bwd(dx, gate, f, *, name, tm=512):
    s, d = dx.shape
    steps = s // tm

    def body(dx_ref, g_ref, f_ref, df_ref, dg_ref):
        i = pl.program_id(0)

        @pl.when(i == 0)
        def _():
            dg_ref[...] = jnp.zeros_like(dg_ref)

        dxv = dx_ref[...]
        df_ref[...] = (g_ref[...] * dxv).astype(BF16)
        dg_ref[...] += jnp.sum(dxv * f_ref[...], axis=0, keepdims=True)

    tile = pl.BlockSpec((tm, d), lambda i: (i, 0))
    row = pl.BlockSpec((1, d), lambda i: (0, 0))
    return pl.pallas_call(
        body, grid=(steps,), in_specs=[tile, row, tile], out_specs=[tile, row],
        out_shape=[jax.ShapeDtypeStruct((s, d), BF16), jax.ShapeDtypeStruct((1, d), F32)], name=name,
        compiler_params=_cparams("arbitrary"))(dx, gate, f)


def _final_loss(x, g, target, *, name, tm=512):
    s, d = x.shape
    steps = s // tm

    def body(x_ref, g_ref, t_ref, loss_ref, dx_ref, dg_ref, sq_acc):
        i = pl.program_id(0)

        @pl.when(i == 0)
        def _():
            sq_acc[...] = jnp.zeros_like(sq_acc)
            dg_ref[...] = jnp.zeros_like(dg_ref)

        xv = x_ref[...]
        r = lax.rsqrt(jnp.mean(xv * xv, axis=-1, keepdims=True) + NORM_EPS)
        xhat = xv * r
        err = xhat * g_ref[...] - t_ref[...]
        sq_acc[...] += jnp.sum(err * err, axis=0, keepdims=True)
        dy = err * (1.0 / d)
        dg_ref[...] += jnp.sum(dy * xhat, axis=0, keepdims=True)
        dxhat = dy * g_ref[...]
        dx_ref[...] = r * (dxhat - xhat * jnp.mean(dxhat * xhat, axis=-1, keepdims=True))

        @pl.when(i == steps - 1)
        def _():
            total = jnp.sum(sq_acc[...], axis=1, keepdims=True) * (0.5 / d)
            loss_ref[...] = jnp.broadcast_to(total, loss_ref.shape)

    tile = pl.BlockSpec((tm, d), lambda i: (i, 0))
    row = pl.BlockSpec((1, d), lambda i: (0, 0))
    return pl.pallas_call(
        body, grid=(steps,), in_specs=[tile, row, tile],
        out_specs=[pl.BlockSpec((1, 128), lambda i: (0, 0)), tile, row],
        out_shape=[jax.ShapeDtypeStruct((1, 128), F32), jax.ShapeDtypeStruct((s, d), F32),
                   jax.ShapeDtypeStruct((1, d), F32)],
        scratch_shapes=[pltpu.VMEM((1, d), F32)], name=name, compiler_params=_cparams("arbitrary"))(x, g, target)


def _ssd_chunk(z, xbc_ext, dt_raw, conv_w, conv_b, dt_bias, a_log, d_skip, norm_g, h_in):
    q = z.shape[0]
    gw = SSD_HPG * HEAD_LANES
    xc = conv_b
    for k in range(SSD_CONV_K):
        xc = xc + _pick_row(conv_w, k) * _shift_rows(xbc_ext, SSD_CONV_K - 1 - k, CONV_HALO)
    xc = _silu(xc)
    dt = jax.nn.softplus(dt_raw + dt_bias)
    da = dt * (-jnp.exp(a_log))
    ri = lax.broadcasted_iota(jnp.int32, (q, q), 0)
    ci = lax.broadcasted_iota(jnp.int32, (q, q), 1)
    causal = ri >= ci
    tril = causal.astype(F32)
    a_cum = jnp.dot(tril, da, preferred_element_type=F32, precision=lax.Precision.HIGHEST)
    a_cum_t = lax.dot_general(da, tril, (((0,), (1,)), ((), ())), preferred_element_type=F32,
                              precision=lax.Precision.HIGHEST)
    a_last = _pick_row(a_cum, q - 1)
    head = _head_of_lane(gw)
    ys, hs = [], []
    for g in range(2):
        xs = xc[:, gw * g:gw * (g + 1)]
        bm = xc[:, SSD_INNER + SSD_STATE * g:SSD_INNER + SSD_STATE * (g + 1)]
        cm = xc[:, SSD_INNER + 2 * SSD_STATE + SSD_STATE * g:SSD_INNER + 2 * SSD_STATE + SSD_STATE * (g + 1)]
        cb = lax.dot_general(cm.astype(BF16), bm.astype(BF16), (((1,), (1,)), ((), ())), preferred_element_type=F32)
        cols = [_pick_lane(a_cum, SSD_HPG * g + j) for j in range(SSD_HPG)]
        lasts = [_pick_lane(a_last, SSD_HPG * g + j) for j in range(SSD_HPG)]
        dt_exp = sum(jnp.where(head == j, _pick_lane(dt, SSD_HPG * g + j), 0.0) for j in range(SSD_HPG))
        d_exp = sum(jnp.where(head == j, _pick_lane(d_skip, SSD_HPG * g + j), 0.0) for j in range(SSD_HPG))
        e_cum = sum(jnp.where(head == j, jnp.exp(cols[j]), 0.0) for j in range(SSD_HPG))
        c_dec = sum(jnp.where(head == j, jnp.exp(lasts[j]), 0.0) for j in range(SSD_HPG))
        xsdt = (xs * dt_exp).astype(BF16)
        y_diag = jnp.zeros((q, gw), F32)
        st_new = jnp.zeros((SSD_STATE, gw), F32)
        for j in range(SSD_HPG):
            row = _pick_row(a_cum_t, SSD_HPG * g + j)
            lmat = jnp.exp(jnp.where(causal, cols[j] - row, -jnp.inf))
            r = jnp.dot((cb * lmat).astype(BF16), xsdt, preferred_element_type=F32)
            y_diag = y_diag + jnp.where(head == j, r, 0.0)
            bd = (bm * jnp.exp(lasts[j] - cols[j])).astype(BF16)
            st = lax.dot_general(bd, xsdt, (((0,), (0,)), ((), ())), preferred_element_type=F32)
            st_new = st_new + jnp.where(head == j, st, 0.0)
        y_off = jnp.dot(cm.astype(BF16), h_in[g].astype(BF16), preferred_element_type=F32) * e_cum
        hs.append(h_in[g] * c_dec + st_new)
        y = y_diag + y_off + d_exp * xs
        yz = y * _silu(z[:, gw * g:gw * (g + 1)])
        yz = yz * lax.rsqrt(jnp.mean(yz * yz, axis=-1, keepdims=True) + NORM_EPS)
        ys.append(yz * norm_g[:, gw * g:gw * (g + 1)])
    return jnp.concatenate(ys, axis=1), tuple(hs)


_SSD_NCHUNK = SEQ // SSD_CHUNK
_HALO_PER_CHUNK = SSD_CHUNK // CONV_HALO


def _ssd_param_specs(const):
    return [pl.BlockSpec((8, SSD_CONV_CH), const), pl.BlockSpec((1, SSD_CONV_CH), const),
            pl.BlockSpec((1, 128), const), pl.BlockSpec((1, 128), const), pl.BlockSpec((1, 128), const),
            pl.BlockSpec((1, SSD_INNER), const)]


def _ssd_fwd(proj, conv_w, conv_b, dt_bias, a_log, d_skip, norm_g, *, name):
    q = SSD_CHUNK

    def body(z_ref, xbc_ref, halo_ref, dt_ref, cw_ref, cb_ref, db_ref, al_ref, d_ref, ng_ref, y_ref, hs_ref, h_acc):
        i = pl.program_id(0)

        @pl.when(i == 0)
        def _():
            h_acc[...] = jnp.zeros_like(h_acc)

        halo = jnp.where(i == 0, 0.0, halo_ref[...])
        xbc_ext = jnp.concatenate([halo, xbc_ref[...]], axis=0)
        h_in = (h_acc[0], h_acc[1])
        hs_ref[0, 0] = h_in[0]
        hs_ref[0, 1] = h_in[1]
        y, h_out = _ssd_chunk(z_ref[...], xbc_ext, dt_ref[...], cw_ref[...], cb_ref[...], db_ref[...], al_ref[...],
                              d_ref[...], ng_ref[...], h_in)
        y_ref[...] = y.astype(BF16)
        h_acc[0] = h_out[0]
        h_acc[1] = h_out[1]

    const = lambda i: (0, 0)
    return pl.pallas_call(
        body, grid=(_SSD_NCHUNK,),
        in_specs=[pl.BlockSpec((q, SSD_INNER), lambda i: (i, PROJ_Z_BLK)),
                  pl.BlockSpec((q, SSD_CONV_CH), lambda i: (i, 0)),
                  pl.BlockSpec((CONV_HALO, SSD_CONV_CH), lambda i: (jnp.maximum(i * _HALO_PER_CHUNK - 1, 0), 0)),
                  pl.BlockSpec((q, 128), lambda i: (i, PROJ_DT_BLK))] + _ssd_param_specs(const),
        out_specs=[pl.BlockSpec((q, SSD_INNER), lambda i: (i, 0)),
                   pl.BlockSpec((1, 2, SSD_STATE, 256), lambda i: (i, 0, 0, 0))],
        out_shape=[jax.ShapeDtypeStruct((SEQ, D_MODEL), BF16),
                   jax.ShapeDtypeStruct((_SSD_NCHUNK, 2, SSD_STATE, 256), F32)],
        scratch_shapes=[pltpu.VMEM((2, SSD_STATE, 256), F32)], name=name,
        compiler_params=_cparams("arbitrary"))(proj, proj, proj, proj, conv_w, conv_b, dt_bias, a_log, d_skip, norm_g)


def _ssd_bwd(proj, hstates, dmix, conv_w, conv_b, dt_bias, a_log, d_skip, norm_g, *, name):
    q = SSD_CHUNK
    last = _SSD_NCHUNK - 1

    def body(z_ref, xbc_ref, halo_ref, dt_ref, hs_ref, dy_ref, cw_ref, cb_ref, db_ref, al_ref, d_ref, ng_ref,
             dp_ref, dcw_ref, dcb_ref, ddb_ref, dal_ref, dd_ref, dng_ref, dh_acc, dhalo_acc):
        i = pl.program_id(0)

        @pl.when(i == 0)
        def _():
            dh_acc[...] = jnp.zeros_like(dh_acc)
            dhalo_acc[...] = jnp.zeros_like(dhalo_acc)
            for r in (dcw_ref, dcb_ref, ddb_ref, dal_ref, dd_ref, dng_ref):
                r[...] = jnp.zeros_like(r)

        halo = jnp.where(i == last, 0.0, halo_ref[...])
        xbc_ext = jnp.concatenate([halo, xbc_ref[...]], axis=0)
        _, vjp = jax.vjp(_ssd_chunk, z_ref[...], xbc_ext, dt_ref[...], cw_ref[...], cb_ref[...], db_ref[...],
                         al_ref[...], d_ref[...], ng_ref[...], (hs_ref[0, 0], hs_ref[0, 1]))
        gz, gx, gdt, gcw, gcb, gdb, gal, gd, gng, gh = vjp((dy_ref[...], (dh_acc[0], dh_acc[1])))
        dxbc = jnp.concatenate([gx[CONV_HALO:q], gx[q:] + dhalo_acc[...]], axis=0)
        dp_ref[...] = jnp.concatenate([dxbc, gz, gdt, jnp.zeros_like(gdt)], axis=1).astype(BF16)
        dhalo_acc[...] = gx[:CONV_HALO]
        dh_acc[0] = gh[0]
        dh_acc[1] = gh[1]
        dcw_ref[...] += gcw
        dcb_ref[...] += gcb
        ddb_ref[...] += gdb
        dal_ref[...] += gal
        dd_ref[...] += gd
        dng_ref[...] += gng

    const = lambda i: (0, 0)
    rev = lambda i: last - i
    row = lambda n: jax.ShapeDtypeStruct((1, n), F32)
    return pl.pallas_call(
        body, grid=(_SSD_NCHUNK,),
        in_specs=[pl.BlockSpec((q, SSD_INNER), lambda i: (rev(i), PROJ_Z_BLK)),
                  pl.BlockSpec((q, SSD_CONV_CH), lambda i: (rev(i), 0)),
                  pl.BlockSpec((CONV_HALO, SSD_CONV_CH), lambda i: (jnp.maximum(rev(i) * _HALO_PER_CHUNK - 1, 0), 0)),
                  pl.BlockSpec((q, 128), lambda i: (rev(i), PROJ_DT_BLK)),
                  pl.BlockSpec((1, 2, SSD_STATE, 256), lambda i: (rev(i), 0, 0, 0)),
                  pl.BlockSpec((q, SSD_INNER), lambda i: (rev(i), 0))] + _ssd_param_specs(const),
        out_specs=[pl.BlockSpec((q, PROJ_SSD_W), lambda i: (rev(i), 0))] + _ssd_param_specs(const),
        out_shape=[jax.ShapeDtypeStruct((SEQ, PROJ_W), BF16), jax.ShapeDtypeStruct((8, SSD_CONV_CH), F32),
                   row(SSD_CONV_CH), row(128), row(128), row(128), row(SSD_INNER)],
        scratch_shapes=[pltpu.VMEM((2, SSD_STATE, 256), F32), pltpu.VMEM((CONV_HALO, SSD_CONV_CH), F32)], name=name,
        compiler_params=_cparams("arbitrary"))(proj, proj, proj, proj, hstates, dmix, conv_w, conv_b, dt_bias, a_log,
                                                d_skip, norm_g)


def _rope_tables(pos_col, inv_freq_lane, *, name):
    s = pos_col.shape[0]

    def body(p_ref, f_ref, c_ref, s1_ref, s2_ref):
        ang = p_ref[...] * f_ref[...]
        within = lax.broadcasted_iota(jnp.int32, ang.shape, 1) % HEAD_LANES
        half = ROT_DIM // 2
        c_ref[...] = jnp.where(within < ROT_DIM, jnp.cos(ang), 1.0)
        sn = jnp.sin(ang)
        s1_ref[...] = jnp.where(within < half, -sn, 0.0)
        s2_ref[...] = jnp.where((within >= half) & (within < ROT_DIM), sn, 0.0)

    shp = jax.ShapeDtypeStruct((s, 128), F32)
    return pl.pallas_call(body, out_shape=[shp, shp, shp], name=name,
                          compiler_params=pltpu.CompilerParams(vmem_limit_bytes=VMEM_LIMIT))(pos_col, inv_freq_lane)


def _rope(t, c, s1, s2):
    half = ROT_DIM // 2
    return t * c + pltpu.roll(t, 128 - half, 1) * s1 + pltpu.roll(t, half, 1) * s2


def _rope_t(g, c, s1, s2):
    half = ROT_DIM // 2
    return g * c + pltpu.roll(g * s1, half, 1) + pltpu.roll(g * s2, 128 - half, 1)


def _att_valid(b):
    qi = lax.broadcasted_iota(jnp.int32, (ATT_BLOCK, 2 * ATT_BLOCK), 0)
    kj = lax.broadcasted_iota(jnp.int32, (ATT_BLOCK, 2 * ATT_BLOCK), 1)
    rel = qi + ATT_BLOCK - kj
    return (rel >= 0) & (rel <= ATT_BLOCK) & (b * ATT_BLOCK + kj - ATT_BLOCK >= 0)


def _att_slices(i, d):
    if d == 1:
        qstart = pl.multiple_of(i * ATT_BLOCK, ATT_BLOCK)
        return i, pl.ds(qstart, ATT_BLOCK), pl.ds(pl.multiple_of(qstart - ATT_BLOCK + ATT_KPAD, ATT_BLOCK), 2 * ATT_BLOCK)
    r = i % d
    b = i // d
    qstart = r + d * ATT_BLOCK * b
    return b, pl.ds(qstart, ATT_BLOCK, stride=d), pl.ds(qstart - ATT_BLOCK * d + ATT_KPAD, 2 * ATT_BLOCK, stride=d)


_ATT_NBLK = SEQ // ATT_BLOCK
_ATT_SCALE = HEAD_LANES ** -0.5
_ATT_UNROLL = 4


def _att_fwd(proj, cos, sin1, sin2, mix, *, name):
    s = SEQ

    def body(q_ref, k_ref, v_ref, c_ref, s1_ref, s2_ref, _, o_ref, lse_ref, mix_ref, qs, ks, vs, acc, m_s, l_s):
        c, s1, s2 = c_ref[...], s1_ref[...], s2_ref[...]
        qs[...] = _rope(q_ref[...], c, s1, s2) * _ATT_SCALE
        zeros = jnp.zeros((ATT_KPAD, 128), F32)
        ks[pl.ds(0, ATT_KPAD), :] = zeros
        vs[pl.ds(0, ATT_KPAD), :] = zeros
        ks[pl.ds(ATT_KPAD, s), :] = _rope(k_ref[...], c, s1, s2)
        vs[pl.ds(ATT_KPAD, s), :] = v_ref[...]
        head0 = _head_of_lane(128) == 0

        for bi, (_, d) in enumerate(ATT_PATTERNS):
            def blk(i, carry, d=d, first=(bi == 0)):
                b, sq, sk = _att_slices(i, d)
                qb = qs[sq, :]
                kw = ks[sk, :].astype(BF16)
                vw = vs[sk, :].astype(BF16)
                valid = _att_valid(b)
                ms, ls, os_ = [], [], []
                for hh in range(2):
                    qh = jnp.where(head0 if hh == 0 else ~head0, qb, 0.0).astype(BF16)
                    sc = lax.dot_general(qh, kw, (((1,), (1,)), ((), ())), preferred_element_type=F32)
                    sc = jnp.where(valid, sc, -jnp.inf)
                    mb = jnp.max(sc, axis=1, keepdims=True)
                    p = jnp.exp(sc - mb)
                    ms.append(mb)
                    ls.append(jnp.sum(p, axis=1, keepdims=True))
                    os_.append(jnp.dot(p.astype(BF16), vw, preferred_element_type=F32))
                m_b = jnp.where(head0, ms[0], ms[1])
                l_b = jnp.where(head0, ls[0], ls[1])
                o_b = jnp.where(head0, os_[0], os_[1])
                if first:
                    m_s[sq, :] = m_b
                    l_s[sq, :] = l_b
                    acc[sq, :] = o_b
                else:
                    m_old = m_s[sq, :]
                    m_new = jnp.maximum(m_old, m_b)
                    a_old = jnp.exp(m_old - m_new)
                    a_b = jnp.exp(m_b - m_new)
                    m_s[sq, :] = m_new
                    l_s[sq, :] = l_s[sq, :] * a_old + l_b * a_b
                    acc[sq, :] = acc[sq, :] * a_old + o_b * a_b
                return carry

            lax.fori_loop(0, _ATT_NBLK, blk, 0, unroll=_ATT_UNROLL)

        out = acc[...] / l_s[...]
        o_ref[...] = out
        mix_ref[...] = out.astype(BF16)
        lse_ref[...] = m_s[...] + jnp.log(l_s[...])

    col = lambda base: pl.BlockSpec((s, 128), lambda p: (0, base + p))
    tab = pl.BlockSpec((s, 128), lambda p: (0, 0))
    big = pltpu.VMEM((ATT_KPAD + s, 128), F32)
    tok = pltpu.VMEM((s, 128), F32)
    return pl.pallas_call(
        body, grid=(2,), in_specs=[col(PROJ_Q_BLK), col(PROJ_K_BLK), col(PROJ_V_BLK), tab, tab, tab, _HBM],
        out_specs=[pl.BlockSpec((s, 128), lambda p: (0, p)), pl.BlockSpec((s, 128), lambda p: (0, p)),
                   col(MIX_ATT_BLK)],
        out_shape=[jax.ShapeDtypeStruct((s, ATT_W), F32), jax.ShapeDtypeStruct((s, ATT_W), F32),
                   jax.ShapeDtypeStruct(mix.shape, mix.dtype)],
        input_output_aliases={6: 2}, scratch_shapes=[tok, big, big, tok, tok, tok], name=name,
        compiler_params=_cparams("arbitrary"))(proj, proj, proj, cos, sin1, sin2, mix)


def _att_bwd(proj, cos, sin1, sin2, out, lse, dmix, dproj, *, name):
    s = SEQ

    def body(proj_ref, c_hbm, s1_hbm, s2_hbm, out_hbm, lse_hbm, dmix_hbm, _, dproj_hbm,
             c_ref, s1_ref, s2_ref, o_ref, lse_ref, do_ref, qs, ks, vs, dqs, dks, dvs, staged, sems):
        def start(copies):
            for cp in copies:
                cp.start()
            return copies

        def load(pair):
            lanes = pl.ds(128 * pair, 128)
            rows = pl.ds(ATT_KPAD, s)
            return start([
                pltpu.make_async_copy(proj_ref.at[:, pl.ds(128 * (PROJ_Q_BLK + pair), 128)], qs, sems.at[0]),
                pltpu.make_async_copy(proj_ref.at[:, pl.ds(128 * (PROJ_K_BLK + pair), 128)], ks.at[rows, :], sems.at[1]),
                pltpu.make_async_copy(proj_ref.at[:, pl.ds(128 * (PROJ_V_BLK + pair), 128)], vs.at[rows, :], sems.at[2]),
                pltpu.make_async_copy(out_hbm.at[:, lanes], o_ref, sems.at[3]),
                pltpu.make_async_copy(lse_hbm.at[:, lanes], lse_ref, sems.at[4]),
                pltpu.make_async_copy(dmix_hbm.at[:, pl.ds(128 * (MIX_ATT_BLK + pair), 128)], do_ref, sems.at[5])])

        tables = start([pltpu.make_async_copy(c_hbm, c_ref, sems.at[6]),
                        pltpu.make_async_copy(s1_hbm, s1_ref, sems.at[7]),
                        pltpu.make_async_copy(s2_hbm, s2_ref, sems.at[8])])
        loads = load(0)
        for cp in tables:
            cp.wait()
        head0 = _head_of_lane(128) == 0
        zeros = jnp.zeros((ATT_KPAD, 128), F32)
        for pair in range(2):
            for cp in loads:
                cp.wait()
            c, s1, s2 = c_ref[...], s1_ref[...], s2_ref[...]
            qs[...] = _rope(qs[...], c, s1, s2) * _ATT_SCALE
            ks[pl.ds(0, ATT_KPAD), :] = zeros
            vs[pl.ds(0, ATT_KPAD), :] = zeros
            ks[pl.ds(ATT_KPAD, s), :] = _rope(ks[pl.ds(ATT_KPAD, s), :], c, s1, s2)
            dqs[...] = jnp.zeros_like(dqs)
            dks[...] = jnp.zeros_like(dks)
            dvs[...] = jnp.zeros_like(dvs)

            for _, d in ATT_PATTERNS:
                def blk(i, carry, d=d):
                    b, sq, sk = _att_slices(i, d)
                    qb = qs[sq, :]
                    kw = ks[sk, :].astype(BF16)
                    vw = vs[sk, :].astype(BF16)
                    dob = do_ref[sq, :]
                    lse_b = lse_ref[sq, :]
                    dd = dob * o_ref[sq, :]
                    valid = _att_valid(b)
                    dq_b = jnp.zeros((ATT_BLOCK, 128), F32)
                    dk_w = jnp.zeros((2 * ATT_BLOCK, 128), F32)
                    dv_w = jnp.zeros((2 * ATT_BLOCK, 128), F32)
                    for hh in range(2):
                        hm = head0 if hh == 0 else ~head0
                        qh = jnp.where(hm, qb, 0.0).astype(BF16)
                        doh = jnp.where(hm, dob, 0.0).astype(BF16)
                        lse_h = _pick_lane(lse_b, hh * HEAD_LANES)
                        d_h = jnp.sum(jnp.where(hm, dd, 0.0), axis=1, keepdims=True)
                        sc = lax.dot_general(qh, kw, (((1,), (1,)), ((), ())), preferred_element_type=F32)
                        p = jnp.where(valid, jnp.exp(sc - lse_h), 0.0)
                        dp = lax.dot_general(doh, vw, (((1,), (1,)), ((), ())), preferred_element_type=F32)
                        ds = (p * (dp - d_h)).astype(BF16)
                        dq_b = dq_b + jnp.where(hm, jnp.dot(ds, kw, preferred_element_type=F32), 0.0)
                        dk_w = dk_w + lax.dot_general(ds, qh, (((0,), (0,)), ((), ())), preferred_element_type=F32)
                        dv_w = dv_w + lax.dot_general(p.astype(BF16), doh, (((0,), (0,)), ((), ())),
                                                      preferred_element_type=F32)
                    dqs[sq, :] += dq_b
                    dks[sk, :] += dk_w
                    dvs[sk, :] += dv_w
                    return carry

                lax.fori_loop(0, _ATT_NBLK, blk, 0, unroll=_ATT_UNROLL)

            staged[0] = _rope_t(dqs[...] * _ATT_SCALE, c, s1, s2).astype(BF16)
            staged[1] = _rope_t(dks[pl.ds(ATT_KPAD, s), :], c, s1, s2).astype(BF16)
            staged[2] = dvs[pl.ds(ATT_KPAD, s), :].astype(BF16)
            stores = start([
                pltpu.make_async_copy(staged.at[j], dproj_hbm.at[:, pl.ds(128 * (col + pair), 128)], sems.at[9 + j])
                for j, col in enumerate((PROJ_Q_BLK, PROJ_K_BLK, PROJ_V_BLK))])
            if pair == 0:
                loads = load(1)
            for cp in stores:
                cp.wait()

    big = pltpu.VMEM((ATT_KPAD + s, 128), F32)
    tok = pltpu.VMEM((s, 128), F32)
    return pl.pallas_call(
        body, in_specs=[_HBM] * 8, out_specs=_HBM, out_shape=jax.ShapeDtypeStruct(dproj.shape, dproj.dtype),
        input_output_aliases={7: 0},
        scratch_shapes=[tok] * 6 + [tok, big, big, tok, big, big, pltpu.VMEM((3, s, 128), BF16),
                                    pltpu.SemaphoreType.DMA((12,))], name=name,
        compiler_params=pltpu.CompilerParams(vmem_limit_bytes=VMEM_LIMIT))(
            proj, cos, sin1, sin2, out, lse, dmix, dproj)


_POOL_TM = 512
_POOL_NT = SEQ // _POOL_TM
_POOL_HALO_PER_TILE = _POOL_TM // POOL_HALO


def _pool_tile(u_ext, w_bd, scale, t0):
    s2 = u_ext + _roll_rows(u_ext, 1)
    s4 = s2 + _roll_rows(s2, 2)
    s8 = s4 + _roll_rows(s4, 4)
    s16 = s8 + _roll_rows(s8, 8)
    grp = _head_of_lane(POOL_W)
    sel = jnp.where(grp == 0, s2, jnp.where(grp == 1, s4, jnp.where(grp == 2, s8, s16)))[POOL_HALO:]
    t = sel.shape[0]
    pos = t0 + lax.broadcasted_iota(jnp.int32, (t, POOL_W), 0) + 1
    win = jnp.where(grp == 0, 2, jnp.where(grp == 1, 4, jnp.where(grp == 2, 8, 16)))
    cnt = jnp.minimum(pos, win).astype(F32)
    diff = sel / cnt - u_ext[POOL_HALO:]
    return jnp.dot(diff.astype(BF16), w_bd.astype(BF16), preferred_element_type=F32) * scale


def _pool_fwd(proj, w_bd, scale, mix, *, name):
    tm = _POOL_TM

    def body(u_ref, halo_ref, w_ref, sc_ref, _, y_ref):
        i = pl.program_id(0)
        halo = jnp.where(i == 0, 0.0, halo_ref[...])
        u_ext = jnp.concatenate([halo, u_ref[...]], axis=0)
        y_ref[...] = _pool_tile(u_ext, w_ref[...], sc_ref[...], i * tm).astype(BF16)

    return pl.pallas_call(
        body, grid=(_POOL_NT,),
        in_specs=[pl.BlockSpec((tm, POOL_W), lambda i: (i, PROJ_POOL_BLK)),
                  pl.BlockSpec((POOL_HALO, POOL_W),
                               lambda i: (jnp.maximum(i * _POOL_HALO_PER_TILE - 1, 0), PROJ_POOL_BLK)),
                  pl.BlockSpec((POOL_W, POOL_W), lambda i: (0, 0)), pl.BlockSpec((1, POOL_W), lambda i: (0, 0)), _HBM],
        out_specs=pl.BlockSpec((tm, POOL_W), lambda i: (i, MIX_POOL_BLK)),
        out_shape=jax.ShapeDtypeStruct(mix.shape, mix.dtype), input_output_aliases={4: 0}, name=name,
        compiler_params=_cparams("parallel"))(proj, proj, w_bd, scale, mix)


def _pool_bwd(proj, dmix, w_bd, scale, dproj, *, name):
    tm = _POOL_TM
    last = _POOL_NT - 1

    def body(u_ref, halo_ref, dy_ref, w_ref, sc_ref, _, du_ref, dw_ref, dsc_ref, dhalo_acc):
        i = pl.program_id(0)

        @pl.when(i == 0)
        def _():
            dhalo_acc[...] = jnp.zeros_like(dhalo_acc)
            dw_ref[...] = jnp.zeros_like(dw_ref)
            dsc_ref[...] = jnp.zeros_like(dsc_ref)

        tile = last - i
        halo = jnp.where(tile == 0, 0.0, halo_ref[...])
        u_ext = jnp.concatenate([halo, u_ref[...]], axis=0)
        _, vjp = jax.vjp(functools.partial(_pool_tile, t0=tile * tm), u_ext, w_ref[...], sc_ref[...])
        gu, gw, gs = vjp(dy_ref[...])
        du_ref[...] = jnp.concatenate([gu[POOL_HALO:tm], gu[tm:] + dhalo_acc[...]], axis=0).astype(BF16)
        dhalo_acc[...] = gu[:POOL_HALO]
        dw_ref[...] += gw
        dsc_ref[...] += gs

    rev = lambda i: last - i
    return pl.pallas_call(
        body, grid=(_POOL_NT,),
        in_specs=[pl.BlockSpec((tm, POOL_W), lambda i: (rev(i), PROJ_POOL_BLK)),
                  pl.BlockSpec((POOL_HALO, POOL_W),
                               lambda i: (jnp.maximum(rev(i) * _POOL_HALO_PER_TILE - 1, 0), PROJ_POOL_BLK)),
                  pl.BlockSpec((tm, POOL_W), lambda i: (rev(i), MIX_POOL_BLK)),
                  pl.BlockSpec((POOL_W, POOL_W), lambda i: (0, 0)), pl.BlockSpec((1, POOL_W), lambda i: (0, 0)), _HBM],
        out_specs=[pl.BlockSpec((tm, POOL_W), lambda i: (rev(i), PROJ_POOL_BLK)),
                   pl.BlockSpec((POOL_W, POOL_W), lambda i: (0, 0)), pl.BlockSpec((1, POOL_W), lambda i: (0, 0))],
        out_shape=[jax.ShapeDtypeStruct(dproj.shape, dproj.dtype), jax.ShapeDtypeStruct((POOL_W, POOL_W), F32),
                   jax.ShapeDtypeStruct((1, POOL_W), F32)],
        input_output_aliases={5: 0}, scratch_shapes=[pltpu.VMEM((POOL_HALO, POOL_W), F32)], name=name,
        compiler_params=_cparams("arbitrary"))(proj, proj, dmix, w_bd, scale, dproj)


_FFN_TM = 256
_FFN_NT = SEQ // _FFN_TM
_FFN_HALO_PER_TILE = _FFN_TM // CONV_HALO


def _ffn_act_tile(hid_ext, conv_w, conv_b):
    hc = conv_b
    for k in range(FFN_CONV_K):
        hc = hc + _pick_row(conv_w, k) * _shift_rows(hid_ext, FFN_CONV_K - 1 - k, CONV_HALO)
    return _silu(hc[:, :FFN_DIM]) * hc[:, FFN_DIM:]


def _resident(shape):
    return pl.BlockSpec(shape, lambda i: (0,) * len(shape), pipeline_mode=pl.Buffered(1))


def _ffn_fwd(x1, norm_g, sc, sh, gate, up_t, down, conv_w, conv_b, *, name):
    tm = _FFN_TM
    w = 2 * FFN_DIM
    d = D_MODEL

    def body(x_ref, ng_ref, sc_ref, sh_ref, g_ref, up_ref, dn_ref, cw_ref, cb_ref,
             h_ref, hid_ref, act_ref, f_ref, x2_ref, halo_acc):
        i = pl.program_id(0)
        h2 = _rms_modulate(x_ref[...], ng_ref[...], sc_ref[...], sh_ref[...]).astype(BF16)
        h_ref[...] = h2
        hid = lax.dot_general(h2, up_ref[...], (((1,), (1,)), ((), ())), preferred_element_type=F32)
        hid_ref[...] = hid
        halo = jnp.where(i == 0, 0.0, halo_acc[...])
        act = _ffn_act_tile(jnp.concatenate([halo, hid], axis=0), cw_ref[...], cb_ref[...]).astype(BF16)
        halo_acc[...] = hid[tm - CONV_HALO:]
        act_ref[...] = act
        f = jnp.dot(act, dn_ref[...], preferred_element_type=F32)
        f_ref[...] = f
        x2_ref[...] = x_ref[...] + g_ref[...] * f

    tile = lambda n: pl.BlockSpec((tm, n), lambda i: (i, 0))
    return pl.pallas_call(
        body, grid=(_FFN_NT,),
        in_specs=[tile(d)] + [_resident((1, d))] * 4 + [_resident((w, d)), _resident((FFN_DIM, d)),
                                                        _resident((8, w)), _resident((1, w))],
        out_specs=[tile(d), tile(w), tile(FFN_DIM), tile(d), tile(d)],
        out_shape=[jax.ShapeDtypeStruct((SEQ, d), BF16), jax.ShapeDtypeStruct((SEQ, w), F32),
                   jax.ShapeDtypeStruct((SEQ, FFN_DIM), BF16), jax.ShapeDtypeStruct((SEQ, d), F32),
                   jax.ShapeDtypeStruct((SEQ, d), F32)],
        scratch_shapes=[pltpu.VMEM((CONV_HALO, w), F32)], name=name,
        compiler_params=_cparams("arbitrary"))(x1, norm_g, sc, sh, gate, up_t, down, conv_w, conv_b)


def _ffn_bwd(df, hid, up_t, down, conv_w, conv_b, *, name):
    tm = _FFN_TM
    w = 2 * FFN_DIM
    d = D_MODEL
    last = _FFN_NT - 1

    def body(df_ref, h_ref, halo_ref, up_ref, dn_ref, cw_ref, cb_ref, dh_ref, dh2_ref, dcw_ref, dcb_ref, dhalo_acc):
        i = pl.program_id(0)

        @pl.when(i == 0)
        def _():
            dhalo_acc[...] = jnp.zeros_like(dhalo_acc)
            dcw_ref[...] = jnp.zeros_like(dcw_ref)
            dcb_ref[...] = jnp.zeros_like(dcb_ref)

        dact = lax.dot_general(df_ref[...], dn_ref[...], (((1,), (1,)), ((), ())), preferred_element_type=F32)
        halo = jnp.where(i == last, 0.0, halo_ref[...])
        hid_ext = jnp.concatenate([halo, h_ref[...]], axis=0)
        _, vjp = jax.vjp(_ffn_act_tile, hid_ext, cw_ref[...], cb_ref[...])
        gh, gw, gb = vjp(dact)
        dhid = jnp.concatenate([gh[CONV_HALO:tm], gh[tm:] + dhalo_acc[...]], axis=0).astype(BF16)
        dhalo_acc[...] = gh[:CONV_HALO]
        dh_ref[...] = dhid
        dh2_ref[...] = jnp.dot(dhid, up_ref[...], preferred_element_type=F32)
        dcw_ref[...] += gw
        dcb_ref[...] += gb

    rev = lambda i: last - i
    tile = lambda n: pl.BlockSpec((tm, n), lambda i: (rev(i), 0))
    acc = lambda shape: pl.BlockSpec(shape, lambda i: (0, 0))
    return pl.pallas_call(
        body, grid=(_FFN_NT,),
        in_specs=[tile(d), tile(w),
                  pl.BlockSpec((CONV_HALO, w), lambda i: (jnp.maximum(rev(i) * _FFN_HALO_PER_TILE - 1, 0), 0)),
                  _resident((w, d)), _resident((FFN_DIM, d)), _resident((8, w)), _resident((1, w))],
        out_specs=[tile(w), tile(d), acc((8, w)), acc((1, w))],
        out_shape=[jax.ShapeDtypeStruct((SEQ, w), BF16), jax.ShapeDtypeStruct((SEQ, d), F32),
                   jax.ShapeDtypeStruct((8, w), F32), jax.ShapeDtypeStruct((1, w), F32)],
        scratch_shapes=[pltpu.VMEM((CONV_HALO, w), F32)], name=name,
        compiler_params=_cparams("arbitrary"))(df, hid, hid, up_t, down, conv_w, conv_b)


def _axes():
    return lax.axis_index("x"), lax.axis_index("y"), lax.axis_index("c")


def _handshake(peers):
    barrier = pltpu.get_barrier_semaphore()
    for peer in peers:
        pl.semaphore_signal(barrier, inc=1, device_id=peer, device_id_type=MESH)
    pl.semaphore_wait(barrier, len(peers))


def _allgather_body(x_refs, out_refs, send_sems, recv_sems, local_sems, own_barrier):
    n = len(x_refs)
    x, y, c = _axes()
    me, sibling = (x, y, c), (x, y, 1 - c)
    chips = [(1 - x, y), (x, 1 - y), (1 - x, 1 - y)]
    if own_barrier:
        _handshake([sibling] + [(*chip, c) for chip in chips])

    def slot(a, px, py, pc):
        return out_refs[a].at[4 * px + 2 * py + pc]

    def copy(a, k, block, to, src=None):
        return pltpu.make_async_remote_copy(
            src_ref=slot(a, *block) if src is None else src, dst_ref=slot(a, *block),
            send_sem=send_sems.at[a, k], recv_sem=recv_sems.at[a, k], device_id=to, device_id_type=MESH)

    mines, firsts = [], []
    for a in range(n):
        mines.append(pltpu.make_async_copy(x_refs[a], slot(a, *me), local_sems.at[a]))
        mines[-1].start()
        first = [copy(a, 0, me, sibling, src=x_refs[a])]
        first += [copy(a, 1 + j, me, (*chip, c), src=x_refs[a]) for j, chip in enumerate(chips)]
        for cp in first:
            cp.start()
        firsts += first
    passed = []
    for j, chip in enumerate(chips):
        for a in range(n):
            copy(a, 1 + j, (*chip, c), me).wait_recv()
            passed.append(copy(a, 4 + j, (*chip, c), sibling))
            passed[-1].start()
    for a in range(n):
        copy(a, 0, sibling, me).wait_recv()
    for j, chip in enumerate(chips):
        for a in range(n):
            copy(a, 4 + j, (*chip, 1 - c), me).wait_recv()
    for cp in firsts + passed:
        cp.wait_send()
    for cp in mines:
        cp.wait()


def _allgather_sems(n):
    return [pltpu.SemaphoreType.DMA((n, 7)), pltpu.SemaphoreType.DMA((n, 7)), pltpu.SemaphoreType.DMA((n,))]


def _allgather(xs, *, name):
    n = len(xs)

    def body(*refs):
        _allgather_body(refs[:n], refs[n:2 * n], *refs[2 * n:], own_barrier=False)

    return pl.pallas_call(
        body, out_shape=[jax.ShapeDtypeStruct((N_DEV,) + xb.shape, xb.dtype) for xb in xs],
        in_specs=[_HBM] * n, out_specs=[_HBM] * n, scratch_shapes=_allgather_sems(n), name=name)(*xs)


def _allgather_async(xs, *, name, collective_id):
    n = len(xs)
    x_refs = [jax.new_ref(xb, memory_space=pltpu.MemorySpace.HBM) for xb in xs]
    out_refs = [jax.empty_ref(jax.ShapeDtypeStruct((N_DEV,) + xb.shape, xb.dtype), memory_space=pltpu.MemorySpace.HBM)
                for xb in xs]

    @pl.kernel(mesh=plsc.ScalarSubcoreMesh(axis_name="sequencer", num_cores=1), name=name,
               scratch_types=tuple(_allgather_sems(n)),
               compiler_params=pltpu.CompilerParams(collective_id=collective_id))
    def launch(send_sems, recv_sems, local_sems):
        _allgather_body(x_refs, out_refs, send_sems, recv_sems, local_sems, own_barrier=True)

    launch()
    return [r[...] for r in out_refs]


def _pair_exchange(blocks, *, name, collective_id):
    n = len(blocks)
    hbm = pltpu.MemorySpace.HBM
    in_refs = [jax.new_ref(b, memory_space=hbm) for b in blocks]
    out_refs = [jax.empty_ref(jax.ShapeDtypeStruct((4,) + b.shape[1:], b.dtype), memory_space=hbm) for b in blocks]

    @pl.kernel(mesh=plsc.ScalarSubcoreMesh(axis_name="sequencer", num_cores=1), name=name,
               scratch_types=(pltpu.SemaphoreType.DMA((n, 4)), pltpu.SemaphoreType.DMA((n, 4))),
               compiler_params=pltpu.CompilerParams(collective_id=collective_id))
    def launch(send_sems, recv_sems):
        x, y, c = _axes()
        _handshake([(x, y, 1 - c)])
        copies = [pltpu.make_async_remote_copy(
            src_ref=in_refs[a].at[2 * s + (1 - c)], dst_ref=out_refs[a].at[s], send_sem=send_sems.at[a, s],
            recv_sem=recv_sems.at[a, s], device_id=(x, y, 1 - c), device_id_type=MESH)
            for a in range(n) for s in range(4)]
        for cp in copies:
            cp.start()
        for cp in copies:
            cp.wait_recv()
        for cp in copies:
            cp.wait_send()

    launch()
    return [r[...] for r in out_refs]


def _chip_exchange(parts, *, name, collective_id):
    n = len(parts)
    hbm = pltpu.MemorySpace.HBM
    in_refs = [jax.new_ref(p, memory_space=hbm) for p in parts]
    out_refs = [jax.empty_ref(jax.ShapeDtypeStruct(p.shape, p.dtype), memory_space=hbm) for p in parts]

    @pl.kernel(mesh=plsc.ScalarSubcoreMesh(axis_name="sequencer", num_cores=1), name=name,
               scratch_types=(pltpu.SemaphoreType.DMA((n, 3)), pltpu.SemaphoreType.DMA((n, 3)),
                              pltpu.SemaphoreType.DMA((n,))),
               compiler_params=pltpu.CompilerParams(collective_id=collective_id))
    def launch(send_sems, recv_sems, local_sems):
        x, y, c = _axes()
        my_chip = 2 * x + y
        chips = [(1 - x, y), (x, 1 - y), (1 - x, 1 - y)]
        _handshake([(*chip, c) for chip in chips])
        locals_ = [pltpu.make_async_copy(in_refs[a].at[my_chip], out_refs[a].at[my_chip], local_sems.at[a])
                   for a in range(n)]
        for cp in locals_:
            cp.start()
        copies = [pltpu.make_async_remote_copy(
            src_ref=in_refs[a].at[2 * px + py], dst_ref=out_refs[a].at[my_chip], send_sem=send_sems.at[a, k],
            recv_sem=recv_sems.at[a, k], device_id=(px, py, c), device_id_type=MESH)
            for a in range(n) for k, (px, py) in enumerate(chips)]
        for cp in copies:
            cp.start()
        for cp in copies:
            cp.wait_recv()
        for cp in copies:
            cp.wait_send()
        for cp in locals_:
            cp.wait()

    launch()
    return [r[...] for r in out_refs]


def _pair_sum(core, blocks, from_sibling, *, name):
    _, r, cdim = blocks.shape

    def body(core_ref, a_ref, b_ref, o_ref):
        o_ref[...] = (a_ref[...].astype(F32) + b_ref[...].astype(F32)).astype(o_ref.dtype)

    return pl.pallas_call(
        body,
        grid_spec=pltpu.PrefetchScalarGridSpec(
            num_scalar_prefetch=1, grid=(4,),
            in_specs=[pl.BlockSpec((1, r, cdim), lambda s, core_ref: (2 * s + core_ref[0], 0, 0)),
                      pl.BlockSpec((1, r, cdim), lambda s, core_ref: (s, 0, 0))],
            out_specs=pl.BlockSpec((1, r, cdim), lambda s, core_ref: (s, 0, 0))),
        out_shape=jax.ShapeDtypeStruct(from_sibling.shape, from_sibling.dtype), name=name,
        compiler_params=_cparams("parallel"))(core, blocks, from_sibling)


def _sum_blocks(a, *, name, tr=None):
    n, r, cdim = a.shape
    tr = tr or r

    def body(a_ref, o_ref):
        acc = a_ref[0].astype(F32)
        for k in range(1, n):
            acc = acc + a_ref[k].astype(F32)
        o_ref[...] = acc

    return pl.pallas_call(body, grid=(r // tr,), in_specs=[pl.BlockSpec((n, tr, cdim), lambda i: (0, i, 0))],
                          out_specs=pl.BlockSpec((tr, cdim), lambda i: (i, 0)),
                          out_shape=jax.ShapeDtypeStruct((r, cdim), F32), name=name,
                          compiler_params=_cparams("parallel"))(a)


def _sum_gathered(gathered, *, name):
    n = len(gathered)

    def body(*refs):
        for a_ref, o_ref in zip(refs[:n], refs[n:]):
            acc = a_ref[0]
            for k in range(1, N_DEV):
                acc = acc + a_ref[k]
            o_ref[...] = acc

    return pl.pallas_call(body, out_shape=[jax.ShapeDtypeStruct(g.shape[1:], F32) for g in gathered], name=name,
                          compiler_params=pltpu.CompilerParams(vmem_limit_bytes=VMEM_LIMIT))(*gathered)


_ADA_SHARD = 6 * D_MODEL // N_DEV


def _ada_mod(c_all, ada_w, *, name):
    def body(c_ref, w_ref, o_ref):
        o_ref[0] = jnp.dot(_silu(c_ref[...]).astype(BF16), w_ref[0].astype(BF16), preferred_element_type=F32)

    return pl.pallas_call(
        body, grid=(DEPTH,),
        in_specs=[pl.BlockSpec((N_DEV, D_MODEL), lambda l: (0, 0)),
                  pl.BlockSpec((1, D_MODEL, _ADA_SHARD), lambda l: (l, 0, 0))],
        out_specs=pl.BlockSpec((1, N_DEV, _ADA_SHARD), lambda l: (l, 0, 0)),
        out_shape=jax.ShapeDtypeStruct((DEPTH, N_DEV, _ADA_SHARD), F32), name=name,
        compiler_params=_cparams("parallel"))(c_all, ada_w)


def _ada_wgrad(c_all, dmod_cols, *, name):
    def body(c_ref, d_ref, o_ref):
        o_ref[0] = lax.dot_general(_silu(c_ref[...]), d_ref[0], (((0,), (0,)), ((), ())),
                                   preferred_element_type=F32, precision=lax.Precision.HIGHEST)

    return pl.pallas_call(
        body, grid=(DEPTH,),
        in_specs=[pl.BlockSpec((N_DEV, D_MODEL), lambda l: (0, 0)),
                  pl.BlockSpec((1, N_DEV, _ADA_SHARD), lambda l: (l, 0, 0))],
        out_specs=pl.BlockSpec((1, D_MODEL, _ADA_SHARD), lambda l: (l, 0, 0)),
        out_shape=jax.ShapeDtypeStruct((DEPTH, D_MODEL, _ADA_SHARD), F32), name=name,
        compiler_params=_cparams("parallel"))(c_all, dmod_cols)


def _add_rows(a, b, *, name):
    def body(a_ref, b_ref, o_ref):
        o_ref[...] = a_ref[...] + b_ref[...]

    return pl.pallas_call(body, out_shape=jax.ShapeDtypeStruct(a.shape, a.dtype), name=name)(a, b)


def _adamw_update(w_ref, g_ref, m_ref, v_ref, d_ref, mo_ref, vo_ref):
    gv = g_ref[...]
    mn = ADAM_B1 * m_ref[...] + (1.0 - ADAM_B1) * gv
    vn = ADAM_B2 * v_ref[...] + (1.0 - ADAM_B2) * (gv * gv)
    mo_ref[...] = mn
    vo_ref[...] = vn
    m_hat = mn / (1.0 - ADAM_B1 ** ADAM_STEP)
    v_hat = vn / (1.0 - ADAM_B2 ** ADAM_STEP)
    d_ref[...] = -ADAM_LR * (m_hat / (jnp.sqrt(v_hat) + ADAM_EPS) + ADAM_WD * w_ref[...])


def _adamw_small(ws, gs, ms, vs, *, name):
    n = len(ws)

    def body(*refs):
        ins, outs = refs[:4 * n], refs[4 * n:]
        for i in range(n):
            _adamw_update(ins[i], ins[n + i], ins[2 * n + i], ins[3 * n + i], outs[i], outs[n + i], outs[2 * n + i])

    shapes = [jax.ShapeDtypeStruct(a.shape, F32) for a in ws]
    outs = pl.pallas_call(body, out_shape=shapes * 3, name=name,
                          compiler_params=pltpu.CompilerParams(vmem_limit_bytes=VMEM_LIMIT))(*ws, *gs, *ms, *vs)
    return outs[:n], outs[n:2 * n], outs[2 * n:]


def _adamw(w, g, m, v, *, name, tr):
    r, cdim = w.shape
    body = functools.partial(_adamw_update)

    spec = pl.BlockSpec((tr, cdim), lambda i: (i, 0))
    shp = jax.ShapeDtypeStruct((r, cdim), F32)
    return pl.pallas_call(body, grid=(r // tr,), in_specs=[spec] * 4, out_specs=[spec] * 3, out_shape=[shp] * 3,
                          name=name, compiler_params=_cparams("parallel"))(w, g, m, v)


def _pad_rows(a, rows):
    return jnp.concatenate([a, jnp.zeros((rows - a.shape[0],) + a.shape[1:], a.dtype)], axis=0)


def _pad_lanes(a, lanes):
    return jnp.concatenate([a, jnp.zeros(a.shape[:-1] + (lanes - a.shape[-1],), a.dtype)], axis=-1)


def _permute_w_in(wt):
    return jnp.concatenate([wt[512:1536], wt[:512], wt[1536:1544],
                            jnp.zeros((PROJ_W - IN_W, wt.shape[1]), wt.dtype), wt[1544:]], axis=0)


def _unpermute_w_in(wp):
    return jnp.concatenate([wp[1024:1536], wp[:1024], wp[1536:1544], wp[PROJ_SSD_W:]], axis=0)


def _block_diag(w):
    rows = []
    for g in range(4):
        rows.append(jnp.concatenate([w[g] if k == g else jnp.zeros_like(w[g]) for k in range(4)], axis=1))
    return jnp.concatenate(rows, axis=0)


def _diag_blocks(wbd):
    return jnp.stack([wbd[64 * g:64 * (g + 1), 64 * g:64 * (g + 1)] for g in range(4)], axis=0)


def _layer_params(l, small):
    return dict(
        norm1_g=small["norm1_g"][l][None], norm2_g=small["norm2_g"][l][None],
        conv_w=_pad_rows(small["ssd_conv_w"][l], 8), conv_b=small["ssd_conv_b"][l][None],
        dt_bias=_pad_lanes(small["ssd_dt_bias"][l][None], 128), a_log=_pad_lanes(small["ssd_a_log"][l][None], 128),
        d_skip=_pad_lanes(small["ssd_d"][l][None], 128), ssd_norm_g=small["ssd_norm_g"][l][None],
        pool_bd=_block_diag(small["pool_w"][l]), pool_scale=small["pool_scale"][l][None],
        fcw=_pad_rows(small["ffn_conv_w"][l], 8), fcb=small["ffn_conv_b"][l][None])


def _mod_rows(mod_l):
    return [mod_l[None, D_MODEL * i:D_MODEL * (i + 1)] for i in range(6)]


def _layer_fwd(x, mod_l, p, tabs, l, gather):
    sh1, sc1, g1, sh2, sc2, g2 = _mod_rows(mod_l)
    mix_w = gather(l, "mix", None)
    p.update(w_in=mix_w["w_in"], w_out=mix_w["w_out"])
    proj, h1 = _mm(x, p["w_in"], nt=True, norm=(p["norm1_g"], sc1, sh1), name=f"l{l}_proj")
    ffn_w = gather(l, "ffn", proj)
    p.update(up=ffn_w["ffn_up"], down=ffn_w["ffn_down"])
    mix, hst = _ssd_fwd(proj, p["conv_w"], p["conv_b"], p["dt_bias"], p["a_log"], p["d_skip"], p["ssd_norm_g"],
                        name=f"l{l}_ssd")
    mix = _pool_fwd(proj, p["pool_bd"], p["pool_scale"], mix, name=f"l{l}_pool")
    y_att, lse, mix = _att_fwd(proj, *tabs, mix, name=f"l{l}_att")
    mo, x1 = _mm(mix, p["w_out"], residual=(x, g1), name=f"l{l}_out")
    gather(l + 1, "mix", (x1, p["up"]))
    h2, hid, act, f, x2 = _ffn_fwd(x1, p["norm2_g"], sc2, sh2, g2, p["up"], p["down"], p["fcw"], p["fcb"],
                                   name=f"l{l}_ffn")
    return x2, dict(x=x, h1=h1, proj=proj, hst=hst, y_att=y_att, lse=lse, mix=mix, mo=mo, x1=x1, h2=h2, hid=hid,
                    act=act, f=f)


def _layer_bwd(dx2, sv, mod_l, p, tabs, l, exchange):
    sh1, sc1, g1, sh2, sc2, g2 = _mod_rows(mod_l)
    df, dg2 = _residual_bwd(dx2, g2, sv["f"], name=f"l{l}_res2_b")
    d_down = _wgrad(sv["act"], df, name=f"l{l}_down_bw")
    dhid, dh2, dfcw, dfcb = _ffn_bwd(df, sv["hid"], p["up"], p["down"], p["fcw"], p["fcb"], name=f"l{l}_ffn_b")
    d_up = _wgrad(dhid, sv["h2"], tk=1408, name=f"l{l}_up_bw")
    exchange(l, "ffn", dict(ffn_up=d_up, ffn_down=d_down))
    dx1, dn2, dsc2, dsh2 = _norm_mod_bwd(sv["x1"], dh2, dx2, p["norm2_g"], sc2, name=f"l{l}_norm2_b")
    dmo, dg1 = _residual_bwd(dx1, g1, sv["mo"], name=f"l{l}_res1_b")
    dmix = _mm(dmo, p["w_out"], nt=True, name=f"l{l}_out_bx")
    d_wout = _wgrad(sv["mix"], dmo, name=f"l{l}_out_bw")
    dproj, dcw, dcb, ddb, dal, dd, dng = _ssd_bwd(
        sv["proj"], sv["hst"], dmix, p["conv_w"], p["conv_b"], p["dt_bias"], p["a_log"], p["d_skip"],
        p["ssd_norm_g"], name=f"l{l}_ssd_b")
    dproj, dwbd, dpsc = _pool_bwd(sv["proj"], dmix, p["pool_bd"], p["pool_scale"], dproj, name=f"l{l}_pool_b")
    dproj = _att_bwd(sv["proj"], *tabs, sv["y_att"], sv["lse"], dmix, dproj, name=f"l{l}_att_b")
    dh1 = _mm(dproj, p["w_in"], name=f"l{l}_proj_bx")
    d_win = _wgrad(dproj, sv["h1"], name=f"l{l}_proj_bw")
    exchange(l, "mix", dict(w_in=d_win, w_out=d_wout))
    dx0, dn1, dsc1, dsh1 = _norm_mod_bwd(sv["x"], dh1, dx1, p["norm1_g"], sc1, name=f"l{l}_norm1_b")
    dmod = jnp.concatenate([dsh1, dsc1, dg1, dsh2, dsc2, dg2], axis=1)[0]
    small = dict(norm1_g=dn1[0], ssd_conv_w=dcw[:SSD_CONV_K], ssd_conv_b=dcb[0], ssd_dt_bias=ddb[0], ssd_a_log=dal[0],
                 ssd_d=dd[0], ssd_norm_g=dng[0], pool_w=_diag_blocks(dwbd), pool_scale=dpsc[0], norm2_g=dn2[0],
                 ffn_conv_w=dfcw[:FFN_CONV_K], ffn_conv_b=dfcb[0])
    return dx0, dmod, small


def _example_step(x, target, pos_col, inv_freq_lane, mod, gather, small, final_g, exchange):
    tabs = _rope_tables(pos_col, inv_freq_lane, name="rope_tables")
    params, saved = [], []
    for l in range(DEPTH):
        params.append(_layer_params(l, small))
        x, sv = _layer_fwd(x, mod[l], params[l], tabs, l, gather)
        saved.append(sv)
    loss_row, dx, dfg = _final_loss(x, final_g[None], target, name="final_loss")
    dmods, smalls = [None] * DEPTH, [None] * DEPTH
    for l in reversed(range(DEPTH)):
        dx, dmods[l], smalls[l] = _layer_bwd(dx, saved[l], mod[l], params[l], tabs, l, exchange)
    return loss_row, dx, jnp.stack(dmods, axis=0), smalls, dfg[0]


_BIG = ("w_in", "w_out", "ffn_up", "ffn_down")
_SMALL_GRADS = ("norm1_g", "ssd_conv_w", "ssd_conv_b", "ssd_dt_bias", "ssd_a_log", "ssd_d", "ssd_norm_g", "pool_w",
                "pool_scale", "norm2_g", "ffn_conv_w", "ffn_conv_b")
_SMALL_PARAMS = ("ada_b", "norm1_g", "ssd_conv_w", "ssd_conv_b", "ssd_dt_bias", "ssd_a_log", "ssd_d", "ssd_norm_g",
                 "pool_w", "pool_scale", "norm2_g", "ffn_conv_w", "ffn_conv_b", "final_g")
_WEIGHT_ORDER = ("ada_w", "ada_b", "norm1_g", "w_in", "ssd_conv_w", "ssd_conv_b", "ssd_dt_bias", "ssd_a_log", "ssd_d",
                 "ssd_norm_g", "pool_w", "pool_scale", "w_out", "norm2_g", "ffn_up", "ffn_conv_w", "ffn_conv_b",
                 "ffn_down", "final_g")


_COLUMN_SHARDED = ("w_in", "ffn_up")
_GROUPS = (("mix", ("w_in", "w_out")), ("ffn", ("ffn_up", "ffn_down")))


def _big_shares(w, l, names):
    return [(w[name][l].T if name in _COLUMN_SHARDED else w[name][l]).astype(BF16) for name in names]


def _unshard_big(names, gathered):
    out = {}
    for name, g in zip(names, gathered):
        full = g.reshape(N_DEV * g.shape[1], g.shape[2])
        out[name] = _permute_w_in(full) if name == "w_in" else full
    return out


def _shard_big(grads):
    out = []
    for name, g in grads.items():
        g = _unpermute_w_in(g) if name == "w_in" else g
        out.append(g.reshape(N_DEV, g.shape[0] // N_DEV, g.shape[1]))
    return out


def kernel(x, c, positions, ada_w, ada_b, norm1_g, w_in, ssd_conv_w, ssd_conv_b, ssd_dt_bias, ssd_a_log, ssd_d, ssd_norm_g, pool_w, pool_scale, w_out, norm2_g, ffn_up, ffn_conv_w, ffn_conv_b, ffn_down, final_g, loss_target, m_ada_w, m_ada_b, m_norm1_g, m_w_in, m_ssd_conv_w, m_ssd_conv_b, m_ssd_dt_bias, m_ssd_a_log, m_ssd_d, m_ssd_norm_g, m_pool_w, m_pool_scale, m_w_out, m_norm2_g, m_ffn_up, m_ffn_conv_w, m_ffn_conv_b, m_ffn_down, m_final_g, v_ada_w, v_ada_b, v_norm1_g, v_w_in, v_ssd_conv_w, v_ssd_conv_b, v_ssd_dt_bias, v_ssd_a_log, v_ssd_d, v_ssd_norm_g, v_pool_w, v_pool_scale, v_w_out, v_norm2_g, v_ffn_up, v_ffn_conv_w, v_ffn_conv_b, v_ffn_down, v_final_g):
    w = dict(ada_w=ada_w, ada_b=ada_b, norm1_g=norm1_g, w_in=w_in, ssd_conv_w=ssd_conv_w, ssd_conv_b=ssd_conv_b,
             ssd_dt_bias=ssd_dt_bias, ssd_a_log=ssd_a_log, ssd_d=ssd_d, ssd_norm_g=ssd_norm_g, pool_w=pool_w,
             pool_scale=pool_scale, w_out=w_out, norm2_g=norm2_g, ffn_up=ffn_up, ffn_conv_w=ffn_conv_w,
             ffn_conv_b=ffn_conv_b, ffn_down=ffn_down, final_g=final_g)
    m = dict(ada_w=m_ada_w, ada_b=m_ada_b, norm1_g=m_norm1_g, w_in=m_w_in, ssd_conv_w=m_ssd_conv_w,
             ssd_conv_b=m_ssd_conv_b, ssd_dt_bias=m_ssd_dt_bias, ssd_a_log=m_ssd_a_log, ssd_d=m_ssd_d,
             ssd_norm_g=m_ssd_norm_g, pool_w=m_pool_w, pool_scale=m_pool_scale, w_out=m_w_out, norm2_g=m_norm2_g,
             ffn_up=m_ffn_up, ffn_conv_w=m_ffn_conv_w, ffn_conv_b=m_ffn_conv_b, ffn_down=m_ffn_down,
             final_g=m_final_g)
    v = dict(ada_w=v_ada_w, ada_b=v_ada_b, norm1_g=v_norm1_g, w_in=v_w_in, ssd_conv_w=v_ssd_conv_w,
             ssd_conv_b=v_ssd_conv_b, ssd_dt_bias=v_ssd_dt_bias, ssd_a_log=v_ssd_a_log, ssd_d=v_ssd_d,
             ssd_norm_g=v_ssd_norm_g, pool_w=v_pool_w, pool_scale=v_pool_scale, w_out=v_w_out, norm2_g=v_norm2_g,
             ffn_up=v_ffn_up, ffn_conv_w=v_ffn_conv_w, ffn_conv_b=v_ffn_conv_b, ffn_down=v_ffn_down,
             final_g=v_final_g)
    ix, iy, ic = _axes()
    dev = 4 * ix + 2 * iy + ic

    c_all, scw, fcw = _allgather([c, ssd_conv_w.reshape(DEPTH * SSD_CONV_K, -1),
                                  ffn_conv_w.reshape(DEPTH * FFN_CONV_K, -1)], name="gather_small")
    small_all = c_all
    c_all = c_all.reshape(N_DEV, D_MODEL)
    scw = scw.reshape(N_DEV, DEPTH, SSD_CONV_K, -1).transpose(1, 2, 0, 3).reshape(DEPTH, SSD_CONV_K, SSD_CONV_CH)
    fcw = fcw.reshape(N_DEV, DEPTH, FFN_CONV_K, -1).transpose(1, 2, 0, 3).reshape(DEPTH, FFN_CONV_K, 2 * FFN_DIM)

    mod_cols = _ada_mod(c_all, ada_w, name="ada_mod")
    mod_all = _allgather([mod_cols.reshape(DEPTH * N_DEV, _ADA_SHARD)], name="gather_mod")[0]
    mod_all = mod_all.reshape(N_DEV, DEPTH, N_DEV, _ADA_SHARD)
    mod_mine = lax.dynamic_index_in_dim(mod_all, dev, axis=2, keepdims=False)
    mod = _add_rows(mod_mine.transpose(1, 0, 2).reshape(DEPTH, 6 * D_MODEL), ada_b, name="ada_bias")

    fetched = {}

    def gather(l, group, after):
        if l < DEPTH and (l, group) not in fetched:
            names = dict(_GROUPS)[group]
            shares, _ = lax.optimization_barrier((_big_shares(w, l, names), small_all if after is None else after))
            got = _allgather_async(shares, name=f"gather_weights_l{l}_{group}",
                                   collective_id=1 + 2 * l + (group == "ffn"))
            fetched[l, group] = _unshard_big(names, got)
        return fetched.get((l, group))

    core = ic.astype(jnp.int32).reshape(1)
    from_chips = {}

    def exchange(l, group, g):
        cid = 5 + 4 * l + 2 * (group == "mix")
        blocks = _shard_big(g)
        from_sibling = _pair_exchange(blocks, name=f"grads_pair_exchange_l{l}_{group}", collective_id=cid)
        parts = [_pair_sum(core, b, s, name=f"grads_pair_sum_l{l}_{name}")
                 for name, b, s in zip(g, blocks, from_sibling)]
        got = _chip_exchange(parts, name=f"grads_chip_exchange_l{l}_{group}", collective_id=cid + 1)
        from_chips.update({(l, name): t for name, t in zip(g, got)})

    small = dict(norm1_g=norm1_g, norm2_g=norm2_g, ssd_conv_w=scw, ssd_conv_b=ssd_conv_b, ssd_dt_bias=ssd_dt_bias,
                 ssd_a_log=ssd_a_log, ssd_d=ssd_d, ssd_norm_g=ssd_norm_g, pool_w=pool_w, pool_scale=pool_scale,
                 ffn_conv_w=fcw, ffn_conv_b=ffn_conv_b)

    inv_freq = ROPE_THETA ** (-jnp.arange(0, ROT_DIM, 2, dtype=F32) / ROT_DIM)
    lane = jnp.arange(128) % HEAD_LANES
    inv_freq_lane = jnp.where(lane < ROT_DIM, inv_freq[lane % (ROT_DIM // 2)], 0.0)[None, :]
    pos_col = positions.reshape(SEQ, 1).astype(F32)
    loss_row, dx, dmod, g_small, g_final = _example_step(
        x[0], loss_target[0], pos_col, inv_freq_lane, mod, gather, small, final_g, exchange)

    grads = {}
    for name in _BIG:
        per_layer = [_sum_blocks(from_chips[l, name], name=f"grads_chip_sum_l{l}_{name}") for l in range(DEPTH)]
        grads[name] = jnp.stack([g.T if name in _COLUMN_SHARDED else g for g in per_layer], axis=0)

    small_names = list(_SMALL_GRADS)
    stacked = [jnp.stack([g_small[l][name] for l in range(DEPTH)], axis=0) for name in small_names]
    small_parts = [loss_row, dmod] + [s.reshape(-1, s.shape[-1]) for s in stacked] + [g_final[None]]
    gathered = _allgather(small_parts, name="gather_small_grads")
    total = _sum_gathered(gathered, name="sum_small_grads")
    loss = total[0][0, 0]
    grads["ada_b"] = total[1]
    grads.update(zip(small_names, total[2:-1]))
    grads["final_g"] = total[-1][0]
    dmod_cols = lax.dynamic_slice_in_dim(gathered[1], dev * _ADA_SHARD, _ADA_SHARD, axis=2).transpose(1, 0, 2)
    grads["ada_w"] = _ada_wgrad(c_all, dmod_cols, name="ada_wgrad")
    for name in ("ssd_dt_bias", "ssd_a_log", "ssd_d"):
        grads[name] = grads[name][:, :SSD_HEADS]
    grads["pool_w"] = grads["pool_w"].reshape(pool_w.shape)
    grads["ssd_conv_w"] = lax.dynamic_slice_in_dim(
        grads["ssd_conv_w"].reshape(DEPTH, SSD_CONV_K, SSD_CONV_CH), dev * ssd_conv_w.shape[2], ssd_conv_w.shape[2], axis=2)
    grads["ffn_conv_w"] = lax.dynamic_slice_in_dim(
        grads["ffn_conv_w"].reshape(DEPTH, FFN_CONV_K, 2 * FFN_DIM), dev * ffn_conv_w.shape[2], ffn_conv_w.shape[2], axis=2)

    delta, new_m, new_v = {}, {}, {}
    for name, tr in (("ada_w", 512), ("w_in", 512), ("w_out", 256), ("ffn_up", 512), ("ffn_down", 352)):
        shp = w[name].shape
        two_d = lambda a: a.reshape(shp[0] * shp[1], shp[2])
        d_, m_, v_ = _adamw(two_d(w[name]), two_d(grads[name]), two_d(m[name]), two_d(v[name]), tr=tr,
                               name=f"adamw_{name}")
        delta[name], new_m[name], new_v[name] = (t.reshape(shp) for t in (d_, m_, v_))
    two_d = lambda a: a.reshape(-1, a.shape[-1])
    outs = _adamw_small(*[[two_d(t[name]) for name in _SMALL_PARAMS] for t in (w, grads, m, v)], name="adamw_small")
    for name, d_, m_, v_ in zip(_SMALL_PARAMS, *outs):
        delta[name], new_m[name], new_v[name] = (t.reshape(w[name].shape) for t in (d_, m_, v_))

    grad_x = dx[None]
    return (loss, grad_x, *[grads[n].reshape(w[n].shape) for n in _WEIGHT_ORDER],
            *[delta[n] for n in _WEIGHT_ORDER], *[new_m[n] for n in _WEIGHT_ORDER],
            *[new_v[n] for n in _WEIGHT_ORDER])
```

```python
import functools
import math

import jax
import jax.numpy as jnp
from jax import lax
from jax.experimental import pallas as pl
from jax.experimental.pallas import tpu as pltpu
from jax.experimental.pallas import tpu_sc as plsc

F32 = jnp.float32
BF16 = jnp.bfloat16

N_DEV = 8
D_MODEL = 1024
SEQ = 4096
DEPTH = 2
SSD_INNER = 512
SSD_HEADS = 8
SSD_HPG = 4
SSD_STATE = 128
SSD_CHUNK = 256
SSD_CONV_K = 4
SSD_CONV_CH = 1024
POOL_W = 256
POOL_WINDOWS = (2, 4, 8, 16)
ATT_W = 256
ATT_PATTERNS = ((128, 1), (512, 4), (2048, 16))
ATT_BLOCK = 128
ROT_DIM = 16
ROPE_THETA = 500000.0
IN_W = 2568
FFN_DIM = 2816
FFN_CONV_K = 3
NORM_EPS = 1e-6
HEAD_LANES = 64

ADAM_LR = 0.001
ADAM_B1 = 0.9
ADAM_B2 = 0.999
ADAM_EPS = 1e-08
ADAM_WD = 0.01
ADAM_STEP = 10

PROJ_W = 2816
PROJ_SSD_W = 1792
PROJ_Z_BLK = 2
PROJ_DT_BLK = 12
PROJ_POOL_BLK = 7
PROJ_Q_BLK, PROJ_K_BLK, PROJ_V_BLK = 16, 18, 20
MIX_POOL_BLK = 2
MIX_ATT_BLK = 6
VMEM_LIMIT = 56 * 1024 * 1024
CONV_HALO = 8
POOL_HALO = 16
ATT_KPAD = ATT_BLOCK * 16
MESH = pl.DeviceIdType.MESH
_HBM = pl.BlockSpec(memory_space=pl.ANY)


def _cparams(*sem):
    return pltpu.CompilerParams(dimension_semantics=sem, vmem_limit_bytes=VMEM_LIMIT)


def _resident(shape):
    return pl.BlockSpec(shape, lambda i: (0,) * len(shape), pipeline_mode=pl.Buffered(1))


def _silu(x):
    return x * jax.nn.sigmoid(x)


def _pick_lane(v, h):
    lane = lax.broadcasted_iota(jnp.int32, v.shape, 1)
    return jnp.sum(jnp.where(lane == h, v, 0.0), axis=1, keepdims=True)


def _pick_row(v, h):
    row = lax.broadcasted_iota(jnp.int32, v.shape, 0)
    return jnp.sum(jnp.where(row == h, v, 0.0), axis=0, keepdims=True)


def _head_of_lane(width):
    return lax.broadcasted_iota(jnp.int32, (1, width), 1) // HEAD_LANES


@functools.partial(jax.custom_vjp, nondiff_argnums=(1, 2))
def _shift_rows(x_ext, s, halo):
    y = x_ext if s == 0 else pltpu.roll(x_ext, s, 0)
    return y[halo:]


def _shift_rows_fwd(x_ext, s, halo):
    return _shift_rows(x_ext, s, halo), None


def _shift_rows_bwd(s, halo, _, g):
    ge = jnp.concatenate([jnp.zeros((halo, g.shape[1]), g.dtype), g], axis=0)
    return (ge if s == 0 else pltpu.roll(ge, ge.shape[0] - s, 0),)


_shift_rows.defvjp(_shift_rows_fwd, _shift_rows_bwd)


@functools.partial(jax.custom_vjp, nondiff_argnums=(1,))
def _roll_rows(x, s):
    return pltpu.roll(x, s, 0)


def _roll_rows_fwd(x, s):
    return _roll_rows(x, s), None


def _roll_rows_bwd(s, _, g):
    return (pltpu.roll(g, g.shape[0] - s, 0),)


_roll_rows.defvjp(_roll_rows_fwd, _roll_rows_bwd)


def _rms_modulate(xv, g, sc, sh):
    r = lax.rsqrt(jnp.mean(xv * xv, axis=-1, keepdims=True) + NORM_EPS)
    return (xv * r * g) * (1.0 + sc) + sh


def _mm(a, w, *, name, nt=False, tm=512, tn=None, out_dtype=F32, norm=None, residual=None):
    t, k = a.shape
    n = w.shape[0] if nt else w.shape[1]
    tn = tn or n
    assert tn == n or (norm is None and residual is None)
    extra_in = list(norm or ()) + list(residual or ())

    def body(*refs):
        a_ref, w_ref = refs[:2]
        ins = refs[2:2 + len(extra_in)]
        outs = refs[2 + len(extra_in):]
        if norm is None:
            av = a_ref[...].astype(BF16)
        else:
            av = _rms_modulate(a_ref[...], ins[0][...], ins[1][...], ins[2][...]).astype(BF16)
            outs[1][...] = av
        if nt:
            acc = lax.dot_general(av, w_ref[...], (((1,), (1,)), ((), ())), preferred_element_type=F32)
        else:
            acc = jnp.dot(av, w_ref[...], preferred_element_type=F32)
        outs[0][...] = acc.astype(out_dtype)
        if residual is not None:
            x_ref, gate_ref = ins[-2:]
            outs[-1][...] = x_ref[...] + gate_ref[...] * acc

    row = lambda width: pl.BlockSpec((1, width), lambda i, j: (0, 0))
    tile = lambda width: pl.BlockSpec((tm, width), lambda i, j: (i, 0))
    w_spec = pl.BlockSpec((tn, k), lambda i, j: (j, 0)) if nt else pl.BlockSpec((k, tn), lambda i, j: (0, j))
    in_specs = [tile(k), w_spec] + ([row(k)] * 3 if norm else []) + ([tile(n), row(n)] if residual else [])
    out_specs = [pl.BlockSpec((tm, tn), lambda i, j: (i, j))] + ([tile(k)] if norm else []) + \
        ([tile(n)] if residual else [])
    out_shape = [jax.ShapeDtypeStruct((t, n), out_dtype)] + \
        ([jax.ShapeDtypeStruct((t, k), BF16)] if norm else []) + \
        ([jax.ShapeDtypeStruct((t, n), F32)] if residual else [])
    outs = pl.pallas_call(
        body, grid=(t // tm, n // tn), in_specs=in_specs, out_specs=out_specs, out_shape=out_shape, name=name,
        compiler_params=_cparams("parallel", "parallel"))(a, w, *extra_in)
    return outs[0] if len(outs) == 1 else outs


def _wgrad(a, b, *, name, tk=None, tn=None, tt=512, out_dtype=BF16):
    t, k = a.shape
    n = b.shape[1]
    tk = tk or k
    tn = tn or n
    steps = t // tt

    def body(a_ref, b_ref, o_ref, acc_ref):
        s = pl.program_id(2)

        @pl.when(s == 0)
        def _():
            acc_ref[...] = jnp.zeros_like(acc_ref)

        acc_ref[...] += lax.dot_general(a_ref[...].astype(BF16), b_ref[...].astype(BF16),
                                        (((0,), (0,)), ((), ())), preferred_element_type=F32)

        @pl.when(s == steps - 1)
        def _():
            o_ref[...] = acc_ref[...].astype(out_dtype)

    return pl.pallas_call(
        body, grid=(k // tk, n // tn, steps),
        in_specs=[pl.BlockSpec((tt, tk), lambda i, j, s: (s, i)), pl.BlockSpec((tt, tn), lambda i, j, s: (s, j))],
        out_specs=pl.BlockSpec((tk, tn), lambda i, j, s: (i, j)),
        out_shape=jax.ShapeDtypeStruct((k, n), out_dtype),
        scratch_shapes=[pltpu.VMEM((tk, tn), F32)], name=name,
        compiler_params=_cparams("parallel", "parallel", "arbitrary"))(a, b)


def _norm_mod_bwd(x, dh, dres, g, sc, *, name, w=None, tm=512):
    s, d = x.shape
    steps = s // tm

    def body(x_ref, dh_ref, dres_ref, g_ref, sc_ref, *rest):
        w_ref = rest[0] if w is not None else None
        dx_ref, dg_ref, dsc_ref, dsh_ref, da_acc, dsh_acc = rest[-6:]
        i = pl.program_id(0)

        @pl.when(i == 0)
        def _():
            da_acc[...] = jnp.zeros_like(da_acc)
            dsh_acc[...] = jnp.zeros_like(dsh_acc)

        xv = x_ref[...]
        if w is None:
            dhv = dh_ref[...].astype(F32)
        else:
            dhv = jnp.dot(dh_ref[...], w_ref[...], preferred_element_type=F32)
        r = lax.rsqrt(jnp.mean(xv * xv, axis=-1, keepdims=True) + NORM_EPS)
        xhat = xv * r
        gain = g_ref[...] * (1.0 + sc_ref[...])
        dxhat = dhv * gain
        dx_ref[...] = dres_ref[...] + r * (dxhat - xhat * jnp.mean(dxhat * xhat, axis=-1, keepdims=True))
        da_acc[...] += jnp.sum(dhv * xhat, axis=0, keepdims=True)
        dsh_acc[...] += jnp.sum(dhv, axis=0, keepdims=True)

        @pl.when(i == steps - 1)
        def _():
            dg_ref[...] = da_acc[...] * (1.0 + sc_ref[...])
            dsc_ref[...] = da_acc[...] * g_ref[...]
            dsh_ref[...] = dsh_acc[...]

    row = pl.BlockSpec((1, d), lambda i: (0, 0))
    tile = pl.BlockSpec((tm, d), lambda i: (i, 0))
    row_shape = jax.ShapeDtypeStruct((1, d), F32)
    dh_spec = tile if w is None else pl.BlockSpec((tm, dh.shape[1]), lambda i: (i, 0))
    return pl.pallas_call(
        body, grid=(steps,), in_specs=[tile, dh_spec, tile, row, row] + ([] if w is None else [_resident(w.shape)]),
        out_specs=[tile, row, row, row],
        out_shape=[jax.ShapeDtypeStruct((s, d), F32), row_shape, row_shape, row_shape],
        scratch_shapes=[pltpu.VMEM((1, d), F32), pltpu.VMEM((1, d), F32)], name=name,
        compiler_params=_cparams("arbitrary"))(x, dh, dres, g, sc, *([] if w is None else [w]))


def _out_bwd(dx, gate, mo, w_out, *, name, tm=512):
    s, d = dx.shape
    steps = s // tm

    def body(dx_ref, g_ref, mo_ref, w_ref, dmix_ref, dmo_ref, dg_ref):
        i = pl.program_id(0)

        @pl.when(i == 0)
        def _():
            dg_ref[...] = jnp.zeros_like(dg_ref)

        dxv = dx_ref[...]
        dmo = (g_ref[...] * dxv).astype(BF16)
        dmo_ref[...] = dmo
        dg_ref[...] += jnp.sum(dxv * mo_ref[...], axis=0, keepdims=True)
        dmix_ref[...] = lax.dot_general(dmo, w_ref[...], (((1,), (1,)), ((), ())), preferred_element_type=F32)

    tile = pl.BlockSpec((tm, d), lambda i: (i, 0))
    row = pl.BlockSpec((1, d), lambda i: (0, 0))
    mix_tile = pl.BlockSpec((tm, w_out.shape[0]), lambda i: (i, 0))
    return pl.pallas_call(
        body, grid=(steps,), in_specs=[tile, row, tile, _resident(w_out.shape)], out_specs=[mix_tile, tile, row],
        out_shape=[jax.ShapeDtypeStruct((s, w_out.shape[0]), F32), jax.ShapeDtypeStruct((s, d), BF16),
                   jax.ShapeDtypeStruct((1, d), F32)], name=name,
        compiler_params=_cparams("arbitrary"))(dx, gate, mo, w_out)


def _final_loss(x, g, target, *, name, tm=512):
    s, d = x.shape
    steps = s // tm

    def body(x_ref, g_ref, t_ref, loss_ref, dx_ref, dg_ref, sq_acc):
        i = pl.program_id(0)

        @pl.when(i == 0)
        def _():
            sq_acc[...] = jnp.zeros_like(sq_acc)
            dg_ref[...] = jnp.zeros_like(dg_ref)

        xv = x_ref[...]
        r = lax.rsqrt(jnp.mean(xv * xv, axis=-1, keepdims=True) + NORM_EPS)
        xhat = xv * r
        err = xhat * g_ref[...] - t_ref[...]
        sq_acc[...] += jnp.sum(err * err, axis=0, keepdims=True)
        dy = err * (1.0 / d)
        dg_ref[...] += jnp.sum(dy * xhat, axis=0, keepdims=True)
        dxhat = dy * g_ref[...]
        dx_ref[...] = r * (dxhat - xhat * jnp.mean(dxhat * xhat, axis=-1, keepdims=True))

        @pl.when(i == steps - 1)
        def _():
            total = jnp.sum(sq_acc[...], axis=1, keepdims=True) * (0.5 / d)
            loss_ref[...] = jnp.broadcast_to(total, loss_ref.shape)

    tile = pl.BlockSpec((tm, d), lambda i: (i, 0))
    row = pl.BlockSpec((1, d), lambda i: (0, 0))
    return pl.pallas_call(
        body, grid=(steps,), in_specs=[tile, row, tile],
        out_specs=[pl.BlockSpec((1, 128), lambda i: (0, 0)), tile, row],
        out_shape=[jax.ShapeDtypeStruct((1, 128), F32), jax.ShapeDtypeStruct((s, d), F32),
                   jax.ShapeDtypeStruct((1, d), F32)],
        scratch_shapes=[pltpu.VMEM((1, d), F32)], name=name, compiler_params=_cparams("arbitrary"))(x, g, target)


def _ssd_chunk(z, xbc_ext, dt_raw, conv_w, conv_b, dt_bias, a_log, d_skip, norm_g, h_in):
    q = z.shape[0]
    gw = SSD_HPG * HEAD_LANES
    xc = conv_b
    for k in range(SSD_CONV_K):
        xc = xc + _pick_row(conv_w, k) * _shift_rows(xbc_ext, SSD_CONV_K - 1 - k, CONV_HALO)
    xc = _silu(xc)
    dt = jax.nn.softplus(dt_raw + dt_bias)
    da = dt * (-jnp.exp(a_log))
    ri = lax.broadcasted_iota(jnp.int32, (q, q), 0)
    ci = lax.broadcasted_iota(jnp.int32, (q, q), 1)
    causal = ri >= ci
    tril = causal.astype(F32)
    a_cum = jnp.dot(tril, da, preferred_element_type=F32, precision=lax.Precision.HIGHEST)
    a_cum_t = lax.dot_general(da, tril, (((0,), (1,)), ((), ())), preferred_element_type=F32,
                              precision=lax.Precision.HIGHEST)
    a_last = _pick_row(a_cum, q - 1)
    head = _head_of_lane(gw)
    ys, hs = [], []
    for g in range(2):
        xs = xc[:, gw * g:gw * (g + 1)]
        bm = xc[:, SSD_INNER + SSD_STATE * g:SSD_INNER + SSD_STATE * (g + 1)]
        cm = xc[:, SSD_INNER + 2 * SSD_STATE + SSD_STATE * g:SSD_INNER + 2 * SSD_STATE + SSD_STATE * (g + 1)]
        cb = lax.dot_general(cm.astype(BF16), bm.astype(BF16), (((1,), (1,)), ((), ())), preferred_element_type=F32)
        cols = [_pick_lane(a_cum, SSD_HPG * g + j) for j in range(SSD_HPG)]
        lasts = [_pick_lane(a_last, SSD_HPG * g + j) for j in range(SSD_HPG)]
        dt_exp = sum(jnp.where(head == j, _pick_lane(dt, SSD_HPG * g + j), 0.0) for j in range(SSD_HPG))
        d_exp = sum(jnp.where(head == j, _pick_lane(d_skip, SSD_HPG * g + j), 0.0) for j in range(SSD_HPG))
        e_cum = sum(jnp.where(head == j, jnp.exp(cols[j]), 0.0) for j in range(SSD_HPG))
        c_dec = sum(jnp.where(head == j, jnp.exp(lasts[j]), 0.0) for j in range(SSD_HPG))
        xsdt = (xs * dt_exp).astype(BF16)
        y_diag = jnp.zeros((q, gw), F32)
        st_new = jnp.zeros((SSD_STATE, gw), F32)
        for j in range(SSD_HPG):
            row = _pick_row(a_cum_t, SSD_HPG * g + j)
            lmat = jnp.exp(jnp.where(causal, cols[j] - row, -jnp.inf))
            r = jnp.dot((cb * lmat).astype(BF16), xsdt, preferred_element_type=F32)
            y_diag = y_diag + jnp.where(head == j, r, 0.0)
            bd = (bm * jnp.exp(lasts[j] - cols[j])).astype(BF16)
            st = lax.dot_general(bd, xsdt, (((0,), (0,)), ((), ())), preferred_element_type=F32)
            st_new = st_new + jnp.where(head == j, st, 0.0)
        y_off = jnp.dot(cm.astype(BF16), h_in[g].astype(BF16), preferred_element_type=F32) * e_cum
        hs.append(h_in[g] * c_dec + st_new)
        y = y_diag + y_off + d_exp * xs
        yz = y * _silu(z[:, gw * g:gw * (g + 1)])
        yz = yz * lax.rsqrt(jnp.mean(yz * yz, axis=-1, keepdims=True) + NORM_EPS)
        ys.append(yz * norm_g[:, gw * g:gw * (g + 1)])
    return jnp.concatenate(ys, axis=1), tuple(hs)


_SSD_NCHUNK = SEQ // SSD_CHUNK
_HALO_PER_CHUNK = SSD_CHUNK // CONV_HALO


def _ssd_param_specs(const):
    return [pl.BlockSpec((8, SSD_CONV_CH), const), pl.BlockSpec((1, SSD_CONV_CH), const),
            pl.BlockSpec((1, 128), const), pl.BlockSpec((1, 128), const), pl.BlockSpec((1, 128), const),
            pl.BlockSpec((1, SSD_INNER), const)]


def _ssd_fwd(proj, conv_w, conv_b, dt_bias, a_log, d_skip, norm_g, *, name):
    q = SSD_CHUNK

    def body(z_ref, xbc_ref, halo_ref, dt_ref, cw_ref, cb_ref, db_ref, al_ref, d_ref, ng_ref, y_ref, hs_ref, h_acc):
        i = pl.program_id(0)

        @pl.when(i == 0)
        def _():
            h_acc[...] = jnp.zeros_like(h_acc)

        halo = jnp.where(i == 0, 0.0, halo_ref[...])
        xbc_ext = jnp.concatenate([halo, xbc_ref[...]], axis=0)
        h_in = (h_acc[0], h_acc[1])
        hs_ref[0, 0] = h_in[0]
        hs_ref[0, 1] = h_in[1]
        y, h_out = _ssd_chunk(z_ref[...], xbc_ext, dt_ref[...], cw_ref[...], cb_ref[...], db_ref[...], al_ref[...],
                              d_ref[...], ng_ref[...], h_in)
        y_ref[...] = y.astype(BF16)
        h_acc[0] = h_out[0]
        h_acc[1] = h_out[1]

    const = lambda i: (0, 0)
    return pl.pallas_call(
        body, grid=(_SSD_NCHUNK,),
        in_specs=[pl.BlockSpec((q, SSD_INNER), lambda i: (i, PROJ_Z_BLK)),
                  pl.BlockSpec((q, SSD_CONV_CH), lambda i: (i, 0)),
                  pl.BlockSpec((CONV_HALO, SSD_CONV_CH), lambda i: (jnp.maximum(i * _HALO_PER_CHUNK - 1, 0), 0)),
                  pl.BlockSpec((q, 128), lambda i: (i, PROJ_DT_BLK))] + _ssd_param_specs(const),
        out_specs=[pl.BlockSpec((q, SSD_INNER), lambda i: (i, 0)),
                   pl.BlockSpec((1, 2, SSD_STATE, 256), lambda i: (i, 0, 0, 0))],
        out_shape=[jax.ShapeDtypeStruct((SEQ, D_MODEL), BF16),
                   jax.ShapeDtypeStruct((_SSD_NCHUNK, 2, SSD_STATE, 256), F32)],
        scratch_shapes=[pltpu.VMEM((2, SSD_STATE, 256), F32)], name=name,
        compiler_params=_cparams("arbitrary"))(proj, proj, proj, proj, conv_w, conv_b, dt_bias, a_log, d_skip, norm_g)


def _ssd_bwd(proj, hstates, dmix, conv_w, conv_b, dt_bias, a_log, d_skip, norm_g, *, name):
    q = SSD_CHUNK
    last = _SSD_NCHUNK - 1

    def body(z_ref, xbc_ref, halo_ref, dt_ref, hs_ref, dy_ref, cw_ref, cb_ref, db_ref, al_ref, d_ref, ng_ref,
             dp_ref, dcw_ref, dcb_ref, ddb_ref, dal_ref, dd_ref, dng_ref, dh_acc, dhalo_acc):
        i = pl.program_id(0)

        @pl.when(i == 0)
        def _():
            dh_acc[...] = jnp.zeros_like(dh_acc)
            dhalo_acc[...] = jnp.zeros_like(dhalo_acc)
            for r in (dcw_ref, dcb_ref, ddb_ref, dal_ref, dd_ref, dng_ref):
                r[...] = jnp.zeros_like(r)

        halo = jnp.where(i == last, 0.0, halo_ref[...])
        xbc_ext = jnp.concatenate([halo, xbc_ref[...]], axis=0)
        _, vjp = jax.vjp(_ssd_chunk, z_ref[...], xbc_ext, dt_ref[...], cw_ref[...], cb_ref[...], db_ref[...],
                         al_ref[...], d_ref[...], ng_ref[...], (hs_ref[0, 0], hs_ref[0, 1]))
        gz, gx, gdt, gcw, gcb, gdb, gal, gd, gng, gh = vjp((dy_ref[...], (dh_acc[0], dh_acc[1])))
        dxbc = jnp.concatenate([gx[CONV_HALO:q], gx[q:] + dhalo_acc[...]], axis=0)
        dp_ref[...] = jnp.concatenate([dxbc, gz, gdt, jnp.zeros_like(gdt)], axis=1).astype(BF16)
        dhalo_acc[...] = gx[:CONV_HALO]
        dh_acc[0] = gh[0]
        dh_acc[1] = gh[1]
        dcw_ref[...] += gcw
        dcb_ref[...] += gcb
        ddb_ref[...] += gdb
        dal_ref[...] += gal
        dd_ref[...] += gd
        dng_ref[...] += gng

    const = lambda i: (0, 0)
    rev = lambda i: last - i
    row = lambda n: jax.ShapeDtypeStruct((1, n), F32)
    return pl.pallas_call(
        body, grid=(_SSD_NCHUNK,),
        in_specs=[pl.BlockSpec((q, SSD_INNER), lambda i: (rev(i), PROJ_Z_BLK)),
                  pl.BlockSpec((q, SSD_CONV_CH), lambda i: (rev(i), 0)),
                  pl.BlockSpec((CONV_HALO, SSD_CONV_CH), lambda i: (jnp.maximum(rev(i) * _HALO_PER_CHUNK - 1, 0), 0)),
                  pl.BlockSpec((q, 128), lambda i: (rev(i), PROJ_DT_BLK)),
                  pl.BlockSpec((1, 2, SSD_STATE, 256), lambda i: (rev(i), 0, 0, 0)),
                  pl.BlockSpec((q, SSD_INNER), lambda i: (rev(i), 0))] + _ssd_param_specs(const),
        out_specs=[pl.BlockSpec((q, PROJ_SSD_W), lambda i: (rev(i), 0))] + _ssd_param_specs(const),
        out_shape=[jax.ShapeDtypeStruct((SEQ, PROJ_W), BF16), jax.ShapeDtypeStruct((8, SSD_CONV_CH), F32),
                   row(SSD_CONV_CH), row(128), row(128), row(128), row(SSD_INNER)],
        scratch_shapes=[pltpu.VMEM((2, SSD_STATE, 256), F32), pltpu.VMEM((CONV_HALO, SSD_CONV_CH), F32)], name=name,
        compiler_params=_cparams("arbitrary"))(proj, proj, proj, proj, hstates, dmix, conv_w, conv_b, dt_bias, a_log,
                                                d_skip, norm_g)


def _rope_tables(pos_col, inv_freq_lane, *, name):
    s = pos_col.shape[0]

    def body(p_ref, f_ref, c_ref, s1_ref, s2_ref):
        ang = p_ref[...] * f_ref[...]
        within = lax.broadcasted_iota(jnp.int32, ang.shape, 1) % HEAD_LANES
        half = ROT_DIM // 2
        c_ref[...] = jnp.where(within < ROT_DIM, jnp.cos(ang), 1.0)
        sn = jnp.sin(ang)
        s1_ref[...] = jnp.where(within < half, -sn, 0.0)
        s2_ref[...] = jnp.where((within >= half) & (within < ROT_DIM), sn, 0.0)

    shp = jax.ShapeDtypeStruct((s, 128), F32)
    return pl.pallas_call(body, out_shape=[shp, shp, shp], name=name,
                          compiler_params=pltpu.CompilerParams(vmem_limit_bytes=VMEM_LIMIT))(pos_col, inv_freq_lane)


def _rope(t, c, s1, s2):
    half = ROT_DIM // 2
    return t * c + pltpu.roll(t, 128 - half, 1) * s1 + pltpu.roll(t, half, 1) * s2


def _rope_t(g, c, s1, s2):
    half = ROT_DIM // 2
    return g * c + pltpu.roll(g * s1, half, 1) + pltpu.roll(g * s2, 128 - half, 1)


def _att_valid(b):
    qi = lax.broadcasted_iota(jnp.int32, (ATT_BLOCK, 2 * ATT_BLOCK), 0)
    kj = lax.broadcasted_iota(jnp.int32, (ATT_BLOCK, 2 * ATT_BLOCK), 1)
    rel = qi + ATT_BLOCK - kj
    return (rel >= 0) & (rel <= ATT_BLOCK) & (b * ATT_BLOCK + kj - ATT_BLOCK >= 0)


def _att_slices(i, d):
    if d == 1:
        qstart = pl.multiple_of(i * ATT_BLOCK, ATT_BLOCK)
        return i, pl.ds(qstart, ATT_BLOCK), pl.ds(pl.multiple_of(qstart - ATT_BLOCK + ATT_KPAD, ATT_BLOCK), 2 * ATT_BLOCK)
    r = i % d
    b = i // d
    qstart = r + d * ATT_BLOCK * b
    return b, pl.ds(qstart, ATT_BLOCK, stride=d), pl.ds(qstart - ATT_BLOCK * d + ATT_KPAD, 2 * ATT_BLOCK, stride=d)


_ATT_NBLK = SEQ // ATT_BLOCK
_ATT_SCALE = HEAD_LANES ** -0.5
_ATT_UNROLL = 4


def _att_fwd(proj, cos, sin1, sin2, mix, *, name):
    s = SEQ

    def body(q_ref, k_ref, v_ref, c_ref, s1_ref, s2_ref, _, o_ref, lse_ref, mix_ref, qs, ks, vs, acc, m_s, l_s):
        c, s1, s2 = c_ref[...], s1_ref[...], s2_ref[...]
        qs[...] = _rope(q_ref[...], c, s1, s2) * _ATT_SCALE
        zeros = jnp.zeros((ATT_KPAD, 128), F32)
        ks[pl.ds(0, ATT_KPAD), :] = zeros
        vs[pl.ds(0, ATT_KPAD), :] = zeros
        ks[pl.ds(ATT_KPAD, s), :] = _rope(k_ref[...], c, s1, s2)
        vs[pl.ds(ATT_KPAD, s), :] = v_ref[...]
        head0 = _head_of_lane(128) == 0

        for bi, (_, d) in enumerate(ATT_PATTERNS):
            def blk(i, carry, d=d, first=(bi == 0)):
                b, sq, sk = _att_slices(i, d)
                qb = qs[sq, :]
                kw = ks[sk, :].astype(BF16)
                vw = vs[sk, :].astype(BF16)
                valid = _att_valid(b)
                ms, ls, os_ = [], [], []
                for hh in range(2):
                    qh = jnp.where(head0 if hh == 0 else ~head0, qb, 0.0).astype(BF16)
                    sc = lax.dot_general(qh, kw, (((1,), (1,)), ((), ())), preferred_element_type=F32)
                    sc = jnp.where(valid, sc, -jnp.inf)
                    mb = jnp.max(sc, axis=1, keepdims=True)
                    p = jnp.exp(sc - mb)
                    ms.append(mb)
                    ls.append(jnp.sum(p, axis=1, keepdims=True))
                    os_.append(jnp.dot(p.astype(BF16), vw, preferred_element_type=F32))
                m_b = jnp.where(head0, ms[0], ms[1])
                l_b = jnp.where(head0, ls[0], ls[1])
                o_b = jnp.where(head0, os_[0], os_[1])
                if first:
                    m_s[sq, :] = m_b
                    l_s[sq, :] = l_b
                    acc[sq, :] = o_b
                else:
                    m_old = m_s[sq, :]
                    m_new = jnp.maximum(m_old, m_b)
                    a_old = jnp.exp(m_old - m_new)
                    a_b = jnp.exp(m_b - m_new)
                    m_s[sq, :] = m_new
                    l_s[sq, :] = l_s[sq, :] * a_old + l_b * a_b
                    acc[sq, :] = acc[sq, :] * a_old + o_b * a_b
                return carry

            lax.fori_loop(0, _ATT_NBLK, blk, 0, unroll=_ATT_UNROLL)

        out = acc[...] / l_s[...]
        o_ref[...] = out
        mix_ref[...] = out.astype(BF16)
        lse_ref[...] = m_s[...] + jnp.log(l_s[...])

    col = lambda base: pl.BlockSpec((s, 128), lambda p: (0, base + p))
    tab = pl.BlockSpec((s, 128), lambda p: (0, 0))
    big = pltpu.VMEM((ATT_KPAD + s, 128), F32)
    tok = pltpu.VMEM((s, 128), F32)
    return pl.pallas_call(
        body, grid=(2,), in_specs=[col(PROJ_Q_BLK), col(PROJ_K_BLK), col(PROJ_V_BLK), tab, tab, tab, _HBM],
        out_specs=[pl.BlockSpec((s, 128), lambda p: (0, p)), pl.BlockSpec((s, 128), lambda p: (0, p)),
                   col(MIX_ATT_BLK)],
        out_shape=[jax.ShapeDtypeStruct((s, ATT_W), F32), jax.ShapeDtypeStruct((s, ATT_W), F32),
                   jax.ShapeDtypeStruct(mix.shape, mix.dtype)],
        input_output_aliases={6: 2}, scratch_shapes=[tok, big, big, tok, tok, tok], name=name,
        compiler_params=_cparams("arbitrary"))(proj, proj, proj, cos, sin1, sin2, mix)


def _att_bwd(proj, cos, sin1, sin2, out, lse, dmix, dproj, *, name):
    s = SEQ

    def body(proj_ref, c_hbm, s1_hbm, s2_hbm, out_hbm, lse_hbm, dmix_hbm, _, dproj_hbm,
             c_ref, s1_ref, s2_ref, o_ref, lse_ref, do_ref, qs, ks, vs, dqs, dks, dvs, staged, sems):
        def start(copies):
            for cp in copies:
                cp.start()
            return copies

        def load(pair):
            lanes = pl.ds(128 * pair, 128)
            rows = pl.ds(ATT_KPAD, s)
            return start([
                pltpu.make_async_copy(proj_ref.at[:, pl.ds(128 * (PROJ_Q_BLK + pair), 128)], qs, sems.at[0]),
                pltpu.make_async_copy(proj_ref.at[:, pl.ds(128 * (PROJ_K_BLK + pair), 128)], ks.at[rows, :], sems.at[1]),
                pltpu.make_async_copy(proj_ref.at[:, pl.ds(128 * (PROJ_V_BLK + pair), 128)], vs.at[rows, :], sems.at[2]),
                pltpu.make_async_copy(out_hbm.at[:, lanes], o_ref, sems.at[3]),
                pltpu.make_async_copy(lse_hbm.at[:, lanes], lse_ref, sems.at[4]),
                pltpu.make_async_copy(dmix_hbm.at[:, pl.ds(128 * (MIX_ATT_BLK + pair), 128)], do_ref, sems.at[5])])

        tables = start([pltpu.make_async_copy(c_hbm, c_ref, sems.at[6]),
                        pltpu.make_async_copy(s1_hbm, s1_ref, sems.at[7]),
                        pltpu.make_async_copy(s2_hbm, s2_ref, sems.at[8])])
        loads = load(0)
        for cp in tables:
            cp.wait()
        head0 = _head_of_lane(128) == 0
        zeros = jnp.zeros((ATT_KPAD, 128), F32)
        for pair in range(2):
            for cp in loads:
                cp.wait()
            c, s1, s2 = c_ref[...], s1_ref[...], s2_ref[...]
            qs[...] = _rope(qs[...], c, s1, s2) * _ATT_SCALE
            ks[pl.ds(0, ATT_KPAD), :] = zeros
            vs[pl.ds(0, ATT_KPAD), :] = zeros
            ks[pl.ds(ATT_KPAD, s), :] = _rope(ks[pl.ds(ATT_KPAD, s), :], c, s1, s2)
            dqs[...] = jnp.zeros_like(dqs)
            dks[...] = jnp.zeros_like(dks)
            dvs[...] = jnp.zeros_like(dvs)

            for _, d in ATT_PATTERNS:
                def blk(i, carry, d=d):
                    b, sq, sk = _att_slices(i, d)
                    qb = qs[sq, :]
                    kw = ks[sk, :].astype(BF16)
                    vw = vs[sk, :].astype(BF16)
                    dob = do_ref[sq, :]
                    lse_b = lse_ref[sq, :]
                    dd = dob * o_ref[sq, :]
                    valid = _att_valid(b)
                    dq_b = jnp.zeros((ATT_BLOCK, 128), F32)
                    dk_w = jnp.zeros((2 * ATT_BLOCK, 128), F32)
                    dv_w = jnp.zeros((2 * ATT_BLOCK, 128), F32)
                    for hh in range(2):
                        hm = head0 if hh == 0 else ~head0
                        qh = jnp.where(hm, qb, 0.0).astype(BF16)
                        doh = jnp.where(hm, dob, 0.0).astype(BF16)
                        lse_h = _pick_lane(lse_b, hh * HEAD_LANES)
                        d_h = jnp.sum(jnp.where(hm, dd, 0.0), axis=1, keepdims=True)
                        sc = lax.dot_general(qh, kw, (((1,), (1,)), ((), ())), preferred_element_type=F32)
                        p = jnp.where(valid, jnp.exp(sc - lse_h), 0.0)
                        dp = lax.dot_general(doh, vw, (((1,), (1,)), ((), ())), preferred_element_type=F32)
                        ds = (p * (dp - d_h)).astype(BF16)
                        dq_b = dq_b + jnp.where(hm, jnp.dot(ds, kw, preferred_element_type=F32), 0.0)
                        dk_w = dk_w + lax.dot_general(ds, qh, (((0,), (0,)), ((), ())), preferred_element_type=F32)
                        dv_w = dv_w + lax.dot_general(p.astype(BF16), doh, (((0,), (0,)), ((), ())),
                                                      preferred_element_type=F32)
                    dqs[sq, :] += dq_b
                    dks[sk, :] += dk_w
                    dvs[sk, :] += dv_w
                    return carry

                lax.fori_loop(0, _ATT_NBLK, blk, 0, unroll=_ATT_UNROLL)

            staged[0] = _rope_t(dqs[...] * _ATT_SCALE, c, s1, s2).astype(BF16)
            staged[1] = _rope_t(dks[pl.ds(ATT_KPAD, s), :], c, s1, s2).astype(BF16)
            staged[2] = dvs[pl.ds(ATT_KPAD, s), :].astype(BF16)
            stores = start([
                pltpu.make_async_copy(staged.at[j], dproj_hbm.at[:, pl.ds(128 * (col + pair), 128)], sems.at[9 + j])
                for j, col in enumerate((PROJ_Q_BLK, PROJ_K_BLK, PROJ_V_BLK))])
            if pair == 0:
                loads = load(1)
            for cp in stores:
                cp.wait()

    big = pltpu.VMEM((ATT_KPAD + s, 128), F32)
    tok = pltpu.VMEM((s, 128), F32)
    return pl.pallas_call(
        body, in_specs=[_HBM] * 8, out_specs=_HBM, out_shape=jax.ShapeDtypeStruct(dproj.shape, dproj.dtype),
        input_output_aliases={7: 0},
        scratch_shapes=[tok] * 6 + [tok, big, big, tok, big, big, pltpu.VMEM((3, s, 128), BF16),
                                    pltpu.SemaphoreType.DMA((12,))], name=name,
        compiler_params=pltpu.CompilerParams(vmem_limit_bytes=VMEM_LIMIT))(
            proj, cos, sin1, sin2, out, lse, dmix, dproj)


_POOL_TM = 512
_POOL_NT = SEQ // _POOL_TM
_POOL_HALO_PER_TILE = _POOL_TM // POOL_HALO


def _pool_tile(u_ext, w_bd, scale, t0):
    s2 = u_ext + _roll_rows(u_ext, 1)
    s4 = s2 + _roll_rows(s2, 2)
    s8 = s4 + _roll_rows(s4, 4)
    s16 = s8 + _roll_rows(s8, 8)
    grp = _head_of_lane(POOL_W)
    sel = jnp.where(grp == 0, s2, jnp.where(grp == 1, s4, jnp.where(grp == 2, s8, s16)))[POOL_HALO:]
    t = sel.shape[0]
    pos = t0 + lax.broadcasted_iota(jnp.int32, (t, POOL_W), 0) + 1
    win = jnp.where(grp == 0, 2, jnp.where(grp == 1, 4, jnp.where(grp == 2, 8, 16)))
    cnt = jnp.minimum(pos, win).astype(F32)
    diff = sel / cnt - u_ext[POOL_HALO:]
    return jnp.dot(diff.astype(BF16), w_bd.astype(BF16), preferred_element_type=F32) * scale


def _pool_fwd(proj, w_bd, scale, mix, *, name):
    tm = _POOL_TM

    def body(u_ref, halo_ref, w_ref, sc_ref, _, y_ref):
        i = pl.program_id(0)
        halo = jnp.where(i == 0, 0.0, halo_ref[...])
        u_ext = jnp.concatenate([halo, u_ref[...]], axis=0)
        y_ref[...] = _pool_tile(u_ext, w_ref[...], sc_ref[...], i * tm).astype(BF16)

    return pl.pallas_call(
        body, grid=(_POOL_NT,),
        in_specs=[pl.BlockSpec((tm, POOL_W), lambda i: (i, PROJ_POOL_BLK)),
                  pl.BlockSpec((POOL_HALO, POOL_W),
                               lambda i: (jnp.maximum(i * _POOL_HALO_PER_TILE - 1, 0), PROJ_POOL_BLK)),
                  pl.BlockSpec((POOL_W, POOL_W), lambda i: (0, 0)), pl.BlockSpec((1, POOL_W), lambda i: (0, 0)), _HBM],
        out_specs=pl.BlockSpec((tm, POOL_W), lambda i: (i, MIX_POOL_BLK)),
        out_shape=jax.ShapeDtypeStruct(mix.shape, mix.dtype), input_output_aliases={4: 0}, name=name,
        compiler_params=_cparams("parallel"))(proj, proj, w_bd, scale, mix)


def _pool_bwd(proj, dmix, w_bd, scale, dproj, *, name):
    tm = _POOL_TM
    last = _POOL_NT - 1

    def body(u_ref, halo_ref, dy_ref, w_ref, sc_ref, _, du_ref, dw_ref, dsc_ref, dhalo_acc):
        i = pl.program_id(0)

        @pl.when(i == 0)
        def _():
            dhalo_acc[...] = jnp.zeros_like(dhalo_acc)
            dw_ref[...] = jnp.zeros_like(dw_ref)
            dsc_ref[...] = jnp.zeros_like(dsc_ref)

        tile = last - i
        halo = jnp.where(tile == 0, 0.0, halo_ref[...])
        u_ext = jnp.concatenate([halo, u_ref[...]], axis=0)
        _, vjp = jax.vjp(functools.partial(_pool_tile, t0=tile * tm), u_ext, w_ref[...], sc_ref[...])
        gu, gw, gs = vjp(dy_ref[...])
        du_ref[...] = jnp.concatenate([gu[POOL_HALO:tm], gu[tm:] + dhalo_acc[...]], axis=0).astype(BF16)
        dhalo_acc[...] = gu[:POOL_HALO]
        dw_ref[...] += gw
        dsc_ref[...] += gs

    rev = lambda i: last - i
    return pl.pallas_call(
        body, grid=(_POOL_NT,),
        in_specs=[pl.BlockSpec((tm, POOL_W), lambda i: (rev(i), PROJ_POOL_BLK)),
                  pl.BlockSpec((POOL_HALO, POOL_W),
                               lambda i: (jnp.maximum(rev(i) * _POOL_HALO_PER_TILE - 1, 0), PROJ_POOL_BLK)),
                  pl.BlockSpec((tm, POOL_W), lambda i: (rev(i), MIX_POOL_BLK)),
                  pl.BlockSpec((POOL_W, POOL_W), lambda i: (0, 0)), pl.BlockSpec((1, POOL_W), lambda i: (0, 0)), _HBM],
        out_specs=[pl.BlockSpec((tm, POOL_W), lambda i: (rev(i), PROJ_POOL_BLK)),
                   pl.BlockSpec((POOL_W, POOL_W), lambda i: (0, 0)), pl.BlockSpec((1, POOL_W), lambda i: (0, 0))],
        out_shape=[jax.ShapeDtypeStruct(dproj.shape, dproj.dtype), jax.ShapeDtypeStruct((POOL_W, POOL_W), F32),
                   jax.ShapeDtypeStruct((1, POOL_W), F32)],
        input_output_aliases={5: 0}, scratch_shapes=[pltpu.VMEM((POOL_HALO, POOL_W), F32)], name=name,
        compiler_params=_cparams("arbitrary"))(proj, proj, dmix, w_bd, scale, dproj)


_FFN_TM = 256
_FFN_NT = SEQ // _FFN_TM
_FFN_HALO_PER_TILE = _FFN_TM // CONV_HALO


def _ffn_act_tile(hid_ext, conv_w, conv_b):
    hc = conv_b
    for k in range(FFN_CONV_K):
        hc = hc + _pick_row(conv_w, k) * _shift_rows(hid_ext, FFN_CONV_K - 1 - k, CONV_HALO)
    return _silu(hc[:, :FFN_DIM]) * hc[:, FFN_DIM:]


def _ffn_fwd(x1, norm_g, sc, sh, gate, up_t, down, conv_w, conv_b, *, name):
    tm = _FFN_TM
    w = 2 * FFN_DIM
    d = D_MODEL

    def body(x_ref, ng_ref, sc_ref, sh_ref, g_ref, up_ref, dn_ref, cw_ref, cb_ref,
             h_ref, hid_ref, act_ref, f_ref, x2_ref, halo_acc):
        i = pl.program_id(0)
        h2 = _rms_modulate(x_ref[...], ng_ref[...], sc_ref[...], sh_ref[...]).astype(BF16)
        h_ref[...] = h2
        hid = lax.dot_general(h2, up_ref[...], (((1,), (1,)), ((), ())), preferred_element_type=F32)
        hid_ref[...] = hid
        halo = jnp.where(i == 0, 0.0, halo_acc[...])
        act = _ffn_act_tile(jnp.concatenate([halo, hid], axis=0), cw_ref[...], cb_ref[...]).astype(BF16)
        halo_acc[...] = hid[tm - CONV_HALO:]
        act_ref[...] = act
        f = jnp.dot(act, dn_ref[...], preferred_element_type=F32)
        f_ref[...] = f
        x2_ref[...] = x_ref[...] + g_ref[...] * f

    tile = lambda n: pl.BlockSpec((tm, n), lambda i: (i, 0))
    return pl.pallas_call(
        body, grid=(_FFN_NT,),
        in_specs=[tile(d)] + [_resident((1, d))] * 4 + [_resident((w, d)), _resident((FFN_DIM, d)),
                                                        _resident((8, w)), _resident((1, w))],
        out_specs=[tile(d), tile(w), tile(FFN_DIM), tile(d), tile(d)],
        out_shape=[jax.ShapeDtypeStruct((SEQ, d), BF16), jax.ShapeDtypeStruct((SEQ, w), F32),
                   jax.ShapeDtypeStruct((SEQ, FFN_DIM), BF16), jax.ShapeDtypeStruct((SEQ, d), F32),
                   jax.ShapeDtypeStruct((SEQ, d), F32)],
        scratch_shapes=[pltpu.VMEM((CONV_HALO, w), F32)], name=name,
        compiler_params=_cparams("arbitrary"))(x1, norm_g, sc, sh, gate, up_t, down, conv_w, conv_b)


def _ffn_bwd(dx2, gate, f, hid, up_t, down, conv_w, conv_b, *, name):
    tm = _FFN_TM
    w = 2 * FFN_DIM
    d = D_MODEL
    last = _FFN_NT - 1

    def body(dx_ref, g_ref, f_ref, h_ref, halo_ref, up_ref, dn_ref, cw_ref, cb_ref,
             df_ref, dg_ref, dh_ref, dh2_ref, dcw_ref, dcb_ref, dhalo_acc):
        i = pl.program_id(0)

        @pl.when(i == 0)
        def _():
            dhalo_acc[...] = jnp.zeros_like(dhalo_acc)
            dcw_ref[...] = jnp.zeros_like(dcw_ref)
            dcb_ref[...] = jnp.zeros_like(dcb_ref)
            dg_ref[...] = jnp.zeros_like(dg_ref)

        dxv = dx_ref[...]
        df = (g_ref[...] * dxv).astype(BF16)
        df_ref[...] = df
        dg_ref[...] += jnp.sum(dxv * f_ref[...], axis=0, keepdims=True)
        dact = lax.dot_general(df, dn_ref[...], (((1,), (1,)), ((), ())), preferred_element_type=F32)
        halo = jnp.where(i == last, 0.0, halo_ref[...])
        hid_ext = jnp.concatenate([halo, h_ref[...]], axis=0)
        _, vjp = jax.vjp(_ffn_act_tile, hid_ext, cw_ref[...], cb_ref[...])
        gh, gw, gb = vjp(dact)
        dhid = jnp.concatenate([gh[CONV_HALO:tm], gh[tm:] + dhalo_acc[...]], axis=0).astype(BF16)
        dhalo_acc[...] = gh[:CONV_HALO]
        dh_ref[...] = dhid
        dh2_ref[...] = jnp.dot(dhid, up_ref[...], preferred_element_type=F32)
        dcw_ref[...] += gw
        dcb_ref[...] += gb

    rev = lambda i: last - i
    tile = lambda n: pl.BlockSpec((tm, n), lambda i: (rev(i), 0))
    acc = lambda shape: pl.BlockSpec(shape, lambda i: (0, 0))
    return pl.pallas_call(
        body, grid=(_FFN_NT,),
        in_specs=[tile(d), _resident((1, d)), tile(d), tile(w),
                  pl.BlockSpec((CONV_HALO, w), lambda i: (jnp.maximum(rev(i) * _FFN_HALO_PER_TILE - 1, 0), 0)),
                  _resident((w, d)), _resident((FFN_DIM, d)), _resident((8, w)), _resident((1, w))],
        out_specs=[tile(d), acc((1, d)), tile(w), tile(d), acc((8, w)), acc((1, w))],
        out_shape=[jax.ShapeDtypeStruct((SEQ, d), BF16), jax.ShapeDtypeStruct((1, d), F32),
                   jax.ShapeDtypeStruct((SEQ, w), BF16), jax.ShapeDtypeStruct((SEQ, d), F32),
                   jax.ShapeDtypeStruct((8, w), F32), jax.ShapeDtypeStruct((1, w), F32)],
        scratch_shapes=[pltpu.VMEM((CONV_HALO, w), F32)], name=name,
        compiler_params=_cparams("arbitrary"))(dx2, gate, f, hid, hid, up_t, down, conv_w, conv_b)


def _axes():
    return lax.axis_index("x"), lax.axis_index("y"), lax.axis_index("c")


def _handshake(peers):
    barrier = pltpu.get_barrier_semaphore()
    for peer in peers:
        pl.semaphore_signal(barrier, inc=1, device_id=peer, device_id_type=MESH)
    pl.semaphore_wait(barrier, len(peers))


def _allgather_body(x_refs, out_refs, send_sems, recv_sems, local_sems, own_barrier):
    n = len(x_refs)
    x, y, c = _axes()
    me, sibling = (x, y, c), (x, y, 1 - c)
    chips = [(1 - x, y), (x, 1 - y), (1 - x, 1 - y)]
    if own_barrier:
        _handshake([sibling] + [(*chip, c) for chip in chips])

    def slot(a, px, py, pc):
        return out_refs[a].at[4 * px + 2 * py + pc]

    def copy(a, k, block, to, src=None):
        return pltpu.make_async_remote_copy(
            src_ref=slot(a, *block) if src is None else src, dst_ref=slot(a, *block),
            send_sem=send_sems.at[a, k], recv_sem=recv_sems.at[a, k], device_id=to, device_id_type=MESH)

    mines, firsts = [], []
    for a in range(n):
        mines.append(pltpu.make_async_copy(x_refs[a], slot(a, *me), local_sems.at[a]))
        mines[-1].start()
        first = [copy(a, 0, me, sibling, src=x_refs[a])]
        first += [copy(a, 1 + j, me, (*chip, c), src=x_refs[a]) for j, chip in enumerate(chips)]
        for cp in first:
            cp.start()
        firsts += first
    passed = []
    for j, chip in enumerate(chips):
        for a in range(n):
            copy(a, 1 + j, (*chip, c), me).wait_recv()
            passed.append(copy(a, 4 + j, (*chip, c), sibling))
            passed[-1].start()
    for a in range(n):
        copy(a, 0, sibling, me).wait_recv()
    for j, chip in enumerate(chips):
        for a in range(n):
            copy(a, 4 + j, (*chip, 1 - c), me).wait_recv()
    for cp in firsts + passed:
        cp.wait_send()
    for cp in mines:
        cp.wait()


def _allgather_sems(n):
    return [pltpu.SemaphoreType.DMA((n, 7)), pltpu.SemaphoreType.DMA((n, 7)), pltpu.SemaphoreType.DMA((n,))]


def _allgather(xs, *, name):
    n = len(xs)

    def body(*refs):
        _allgather_body(refs[:n], refs[n:2 * n], *refs[2 * n:], own_barrier=False)

    return pl.pallas_call(
        body, out_shape=[jax.ShapeDtypeStruct((N_DEV,) + xb.shape, xb.dtype) for xb in xs],
        in_specs=[_HBM] * n, out_specs=[_HBM] * n, scratch_shapes=_allgather_sems(n), name=name)(*xs)


def _allgather_async(xs, *, name, collective_id):
    n = len(xs)
    x_refs = [jax.new_ref(xb, memory_space=pltpu.MemorySpace.HBM) for xb in xs]
    out_refs = [jax.empty_ref(jax.ShapeDtypeStruct((N_DEV,) + xb.shape, xb.dtype), memory_space=pltpu.MemorySpace.HBM)
                for xb in xs]

    @pl.kernel(mesh=plsc.ScalarSubcoreMesh(axis_name="sequencer", num_cores=1), name=name,
               scratch_types=tuple(_allgather_sems(n)),
               compiler_params=pltpu.CompilerParams(collective_id=collective_id))
    def launch(send_sems, recv_sems, local_sems):
        _allgather_body(x_refs, out_refs, send_sems, recv_sems, local_sems, own_barrier=True)

    launch()
    return [r[...] for r in out_refs]


def _pair_exchange(blocks, *, name, collective_id):
    n = len(blocks)
    hbm = pltpu.MemorySpace.HBM
    in_refs = [jax.new_ref(b, memory_space=hbm) for b in blocks]
    out_refs = [jax.empty_ref(jax.ShapeDtypeStruct((4,) + b.shape[1:], b.dtype), memory_space=hbm) for b in blocks]

    @pl.kernel(mesh=plsc.ScalarSubcoreMesh(axis_name="sequencer", num_cores=1), name=name,
               scratch_types=(pltpu.SemaphoreType.DMA((n, 4)), pltpu.SemaphoreType.DMA((n, 4))),
               compiler_params=pltpu.CompilerParams(collective_id=collective_id))
    def launch(send_sems, recv_sems):
        x, y, c = _axes()
        _handshake([(x, y, 1 - c)])
        copies = [pltpu.make_async_remote_copy(
            src_ref=in_refs[a].at[2 * s + (1 - c)], dst_ref=out_refs[a].at[s], send_sem=send_sems.at[a, s],
            recv_sem=recv_sems.at[a, s], device_id=(x, y, 1 - c), device_id_type=MESH)
            for a in range(n) for s in range(4)]
        for cp in copies:
            cp.start()
        for cp in copies:
            cp.wait_recv()
        for cp in copies:
            cp.wait_send()

    launch()
    return [r[...] for r in out_refs]


def _chip_exchange(parts, *, name, collective_id):
    n = len(parts)
    hbm = pltpu.MemorySpace.HBM
    in_refs = [jax.new_ref(p, memory_space=hbm) for p in parts]
    out_refs = [jax.empty_ref(jax.ShapeDtypeStruct(p.shape, p.dtype), memory_space=hbm) for p in parts]

    @pl.kernel(mesh=plsc.ScalarSubcoreMesh(axis_name="sequencer", num_cores=1), name=name,
               scratch_types=(pltpu.SemaphoreType.DMA((n, 3)), pltpu.SemaphoreType.DMA((n, 3)),
                              pltpu.SemaphoreType.DMA((n,))),
               compiler_params=pltpu.CompilerParams(collective_id=collective_id))
    def launch(send_sems, recv_sems, local_sems):
        x, y, c = _axes()
        my_chip = 2 * x + y
        chips = [(1 - x, y), (x, 1 - y), (1 - x, 1 - y)]
        _handshake([(*chip, c) for chip in chips])
        locals_ = [pltpu.make_async_copy(in_refs[a].at[my_chip], out_refs[a].at[my_chip], local_sems.at[a])
                   for a in range(n)]
        for cp in locals_:
            cp.start()
        copies = [pltpu.make_async_remote_copy(
            src_ref=in_refs[a].at[2 * px + py], dst_ref=out_refs[a].at[my_chip], send_sem=send_sems.at[a, k],
            recv_sem=recv_sems.at[a, k], device_id=(px, py, c), device_id_type=MESH)
            for a in range(n) for k, (px, py) in enumerate(chips)]
        for cp in copies:
            cp.start()
        for cp in copies:
            cp.wait_recv()
        for cp in copies:
            cp.wait_send()
        for cp in locals_:
            cp.wait()

    launch()
    return [r[...] for r in out_refs]


def _pair_sum(core, blocks, from_sibling, *, name):
    n = len(blocks)

    def body(core_ref, *refs):
        for a_ref, b_ref, o_ref in zip(refs[:n], refs[n:2 * n], refs[2 * n:]):
            o_ref[...] = (a_ref[...].astype(F32) + b_ref[...].astype(F32)).astype(o_ref.dtype)

    mine = lambda b: pl.BlockSpec((1,) + b.shape[1:], lambda s, core_ref: (2 * s + core_ref[0], 0, 0))
    slot = lambda b: pl.BlockSpec((1,) + b.shape[1:], lambda s, core_ref: (s, 0, 0))
    return pl.pallas_call(
        body,
        grid_spec=pltpu.PrefetchScalarGridSpec(
            num_scalar_prefetch=1, grid=(4,),
            in_specs=[mine(b) for b in blocks] + [slot(b) for b in blocks], out_specs=[slot(b) for b in blocks]),
        out_shape=[jax.ShapeDtypeStruct(s.shape, s.dtype) for s in from_sibling], name=name,
        compiler_params=_cparams("parallel"))(core, *blocks, *from_sibling)


def _sum_blocks(a, *, name, tr=None):
    n, r, cdim = a.shape
    tr = tr or r

    def body(a_ref, o_ref):
        acc = a_ref[0].astype(F32)
        for k in range(1, n):
            acc = acc + a_ref[k].astype(F32)
        o_ref[...] = acc

    return pl.pallas_call(body, grid=(r // tr,), in_specs=[pl.BlockSpec((n, tr, cdim), lambda i: (0, i, 0))],
                          out_specs=pl.BlockSpec((tr, cdim), lambda i: (i, 0)),
                          out_shape=jax.ShapeDtypeStruct((r, cdim), F32), name=name,
                          compiler_params=_cparams("parallel"))(a)


def _sum_gathered(gathered, *, name):
    n = len(gathered)

    def body(*refs):
        for a_ref, o_ref in zip(refs[:n], refs[n:]):
            acc = a_ref[0]
            for k in range(1, N_DEV):
                acc = acc + a_ref[k]
            o_ref[...] = acc

    return pl.pallas_call(body, out_shape=[jax.ShapeDtypeStruct(g.shape[1:], F32) for g in gathered], name=name,
                          compiler_params=pltpu.CompilerParams(vmem_limit_bytes=VMEM_LIMIT))(*gathered)


_ADA_SHARD = 6 * D_MODEL // N_DEV


def _ada_mod(c_all, ada_w, *, name):
    def body(c_ref, w_ref, o_ref):
        o_ref[0] = jnp.dot(_silu(c_ref[...]).astype(BF16), w_ref[0].astype(BF16), preferred_element_type=F32)

    return pl.pallas_call(
        body, grid=(DEPTH,),
        in_specs=[pl.BlockSpec((N_DEV, D_MODEL), lambda l: (0, 0)),
                  pl.BlockSpec((1, D_MODEL, _ADA_SHARD), lambda l: (l, 0, 0))],
        out_specs=pl.BlockSpec((1, N_DEV, _ADA_SHARD), lambda l: (l, 0, 0)),
        out_shape=jax.ShapeDtypeStruct((DEPTH, N_DEV, _ADA_SHARD), F32), name=name,
        compiler_params=_cparams("parallel"))(c_all, ada_w)


def _ada_wgrad(c_all, dmod_cols, *, name):
    def body(c_ref, d_ref, o_ref):
        o_ref[0] = lax.dot_general(_silu(c_ref[...]), d_ref[0], (((0,), (0,)), ((), ())),
                                   preferred_element_type=F32, precision=lax.Precision.HIGHEST)

    return pl.pallas_call(
        body, grid=(DEPTH,),
        in_specs=[pl.BlockSpec((N_DEV, D_MODEL), lambda l: (0, 0)),
                  pl.BlockSpec((1, N_DEV, _ADA_SHARD), lambda l: (l, 0, 0))],
        out_specs=pl.BlockSpec((1, D_MODEL, _ADA_SHARD), lambda l: (l, 0, 0)),
        out_shape=jax.ShapeDtypeStruct((DEPTH, D_MODEL, _ADA_SHARD), F32), name=name,
        compiler_params=_cparams("parallel"))(c_all, dmod_cols)


def _add_rows(a, b, *, name):
    def body(a_ref, b_ref, o_ref):
        o_ref[...] = a_ref[...] + b_ref[...]

    return pl.pallas_call(body, out_shape=jax.ShapeDtypeStruct(a.shape, a.dtype), name=name)(a, b)


def _adamw_update(w_ref, g_ref, m_ref, v_ref, d_ref, mo_ref, vo_ref):
    gv = g_ref[...]
    mn = ADAM_B1 * m_ref[...] + (1.0 - ADAM_B1) * gv
    vn = ADAM_B2 * v_ref[...] + (1.0 - ADAM_B2) * (gv * gv)
    mo_ref[...] = mn
    vo_ref[...] = vn
    m_hat = mn / (1.0 - ADAM_B1 ** ADAM_STEP)
    v_hat = vn / (1.0 - ADAM_B2 ** ADAM_STEP)
    d_ref[...] = -ADAM_LR * (m_hat / (jnp.sqrt(v_hat) + ADAM_EPS) + ADAM_WD * w_ref[...])


def _adamw_small(ws, gs, ms, vs, *, name):
    n = len(ws)

    def body(*refs):
        ins, outs = refs[:4 * n], refs[4 * n:]
        for i in range(n):
            _adamw_update(ins[i], ins[n + i], ins[2 * n + i], ins[3 * n + i], outs[i], outs[n + i], outs[2 * n + i])

    shapes = [jax.ShapeDtypeStruct(a.shape, F32) for a in ws]
    outs = pl.pallas_call(body, out_shape=shapes * 3, name=name,
                          compiler_params=pltpu.CompilerParams(vmem_limit_bytes=VMEM_LIMIT))(*ws, *gs, *ms, *vs)
    return outs[:n], outs[n:2 * n], outs[2 * n:]


def _adamw(w, g, m, v, *, name, tr):
    r, cdim = w.shape
    body = functools.partial(_adamw_update)

    spec = pl.BlockSpec((tr, cdim), lambda i: (i, 0))
    shp = jax.ShapeDtypeStruct((r, cdim), F32)
    return pl.pallas_call(body, grid=(r // tr,), in_specs=[spec] * 4, out_specs=[spec] * 3, out_shape=[shp] * 3,
                          name=name, compiler_params=_cparams("parallel"))(w, g, m, v)


def _pad_rows(a, rows):
    return jnp.concatenate([a, jnp.zeros((rows - a.shape[0],) + a.shape[1:], a.dtype)], axis=0)


def _pad_lanes(a, lanes):
    return jnp.concatenate([a, jnp.zeros(a.shape[:-1] + (lanes - a.shape[-1],), a.dtype)], axis=-1)


def _permute_w_in(wt):
    return jnp.concatenate([wt[512:1536], wt[:512], wt[1536:1544],
                            jnp.zeros((PROJ_W - IN_W, wt.shape[1]), wt.dtype), wt[1544:]], axis=0)


def _unpermute_w_in(wp):
    return jnp.concatenate([wp[1024:1536], wp[:1024], wp[1536:1544], wp[PROJ_SSD_W:]], axis=0)


def _block_diag(w):
    rows = []
    for g in range(4):
        rows.append(jnp.concatenate([w[g] if k == g else jnp.zeros_like(w[g]) for k in range(4)], axis=1))
    return jnp.concatenate(rows, axis=0)


def _diag_blocks(wbd):
    return jnp.stack([wbd[64 * g:64 * (g + 1), 64 * g:64 * (g + 1)] for g in range(4)], axis=0)


def _layer_params(l, small):
    return dict(
        norm1_g=small["norm1_g"][l][None], norm2_g=small["norm2_g"][l][None],
        conv_w=_pad_rows(small["ssd_conv_w"][l], 8), conv_b=small["ssd_conv_b"][l][None],
        dt_bias=_pad_lanes(small["ssd_dt_bias"][l][None], 128), a_log=_pad_lanes(small["ssd_a_log"][l][None], 128),
        d_skip=_pad_lanes(small["ssd_d"][l][None], 128), ssd_norm_g=small["ssd_norm_g"][l][None],
        pool_bd=_block_diag(small["pool_w"][l]), pool_scale=small["pool_scale"][l][None],
        fcw=_pad_rows(small["ffn_conv_w"][l], 8), fcb=small["ffn_conv_b"][l][None])


def _mod_rows(mod_l):
    return [mod_l[None, D_MODEL * i:D_MODEL * (i + 1)] for i in range(6)]


def _layer_fwd(x, mod_l, p, tabs, l, gather):
    sh1, sc1, g1, sh2, sc2, g2 = _mod_rows(mod_l)
    mix_w = gather(l, "mix", None)
    p.update(w_in=mix_w["w_in"], w_out=mix_w["w_out"])
    proj, h1 = _mm(x, p["w_in"], nt=True, norm=(p["norm1_g"], sc1, sh1), name=f"l{l}_proj")
    ffn_w = gather(l, "ffn", proj)
    p.update(up=ffn_w["ffn_up"], down=ffn_w["ffn_down"])
    mix, hst = _ssd_fwd(proj, p["conv_w"], p["conv_b"], p["dt_bias"], p["a_log"], p["d_skip"], p["ssd_norm_g"],
                        name=f"l{l}_ssd")
    mix = _pool_fwd(proj, p["pool_bd"], p["pool_scale"], mix, name=f"l{l}_pool")
    y_att, lse, mix = _att_fwd(proj, *tabs, mix, name=f"l{l}_att")
    mo, x1 = _mm(mix, p["w_out"], residual=(x, g1), name=f"l{l}_out")
    gather(l + 1, "mix", (x1, p["up"]))
    h2, hid, act, f, x2 = _ffn_fwd(x1, p["norm2_g"], sc2, sh2, g2, p["up"], p["down"], p["fcw"], p["fcb"],
                                   name=f"l{l}_ffn")
    return x2, dict(x=x, h1=h1, proj=proj, hst=hst, y_att=y_att, lse=lse, mix=mix, mo=mo, x1=x1, h2=h2, hid=hid,
                    act=act, f=f)


def _layer_bwd(dx2, sv, mod_l, p, tabs, l, exchange):
    sh1, sc1, g1, sh2, sc2, g2 = _mod_rows(mod_l)
    df, dg2, dhid, dh2, dfcw, dfcb = _ffn_bwd(dx2, g2, sv["f"], sv["hid"], p["up"], p["down"], p["fcw"], p["fcb"],
                                              name=f"l{l}_ffn_b")
    d_down = _wgrad(sv["act"], df, name=f"l{l}_down_bw")
    d_up = _wgrad(dhid, sv["h2"], tk=1408, name=f"l{l}_up_bw")
    exchange(l, "ffn", dict(ffn_up=d_up, ffn_down=d_down))
    dx1, dn2, dsc2, dsh2 = _norm_mod_bwd(sv["x1"], dh2, dx2, p["norm2_g"], sc2, name=f"l{l}_norm2_b")
    dmix, dmo, dg1 = _out_bwd(dx1, g1, sv["mo"], p["w_out"], name=f"l{l}_out_b")
    d_wout = _wgrad(sv["mix"], dmo, name=f"l{l}_out_bw")
    dproj, dcw, dcb, ddb, dal, dd, dng = _ssd_bwd(
        sv["proj"], sv["hst"], dmix, p["conv_w"], p["conv_b"], p["dt_bias"], p["a_log"], p["d_skip"],
        p["ssd_norm_g"], name=f"l{l}_ssd_b")
    dproj, dwbd, dpsc = _pool_bwd(sv["proj"], dmix, p["pool_bd"], p["pool_scale"], dproj, name=f"l{l}_pool_b")
    dproj = _att_bwd(sv["proj"], *tabs, sv["y_att"], sv["lse"], dmix, dproj, name=f"l{l}_att_b")
    d_win = _wgrad(dproj, sv["h1"], name=f"l{l}_proj_bw")
    exchange(l, "mix", dict(w_in=d_win, w_out=d_wout))
    dx0, dn1, dsc1, dsh1 = _norm_mod_bwd(sv["x"], dproj, dx1, p["norm1_g"], sc1, w=p["w_in"], name=f"l{l}_proj_b")
    dmod = jnp.concatenate([dsh1, dsc1, dg1, dsh2, dsc2, dg2], axis=1)[0]
    small = dict(norm1_g=dn1[0], ssd_conv_w=dcw[:SSD_CONV_K], ssd_conv_b=dcb[0], ssd_dt_bias=ddb[0], ssd_a_log=dal[0],
                 ssd_d=dd[0], ssd_norm_g=dng[0], pool_w=_diag_blocks(dwbd), pool_scale=dpsc[0], norm2_g=dn2[0],
                 ffn_conv_w=dfcw[:FFN_CONV_K], ffn_conv_b=dfcb[0])
    return dx0, dmod, small


def _example_step(x, target, pos_col, inv_freq_lane, mod, gather, small, final_g, exchange):
    tabs = _rope_tables(pos_col, inv_freq_lane, name="rope_tables")
    params, saved = [], []
    for l in range(DEPTH):
        params.append(_layer_params(l, small))
        x, sv = _layer_fwd(x, mod[l], params[l], tabs, l, gather)
        saved.append(sv)
    loss_row, dx, dfg = _final_loss(x, final_g[None], target, name="final_loss")
    dmods, smalls = [None] * DEPTH, [None] * DEPTH
    for l in reversed(range(DEPTH)):
        dx, dmods[l], smalls[l] = _layer_bwd(dx, saved[l], mod[l], params[l], tabs, l, exchange)
    return loss_row, dx, jnp.stack(dmods, axis=0), smalls, dfg[0]


_BIG = ("w_in", "w_out", "ffn_up", "ffn_down")
_SMALL_GRADS = ("norm1_g", "ssd_conv_w", "ssd_conv_b", "ssd_dt_bias", "ssd_a_log", "ssd_d", "ssd_norm_g", "pool_w",
                "pool_scale", "norm2_g", "ffn_conv_w", "ffn_conv_b")
_SMALL_PARAMS = ("ada_b", "norm1_g", "ssd_conv_w", "ssd_conv_b", "ssd_dt_bias", "ssd_a_log", "ssd_d", "ssd_norm_g",
                 "pool_w", "pool_scale", "norm2_g", "ffn_conv_w", "ffn_conv_b", "final_g")
_WEIGHT_ORDER = ("ada_w", "ada_b", "norm1_g", "w_in", "ssd_conv_w", "ssd_conv_b", "ssd_dt_bias", "ssd_a_log", "ssd_d",
                 "ssd_norm_g", "pool_w", "pool_scale", "w_out", "norm2_g", "ffn_up", "ffn_conv_w", "ffn_conv_b",
                 "ffn_down", "final_g")


_COLUMN_SHARDED = ("w_in", "ffn_up")
_GROUPS = (("mix", ("w_in", "w_out")), ("ffn", ("ffn_up", "ffn_down")))


def _big_shares(w, l, names):
    return [(w[name][l].T if name in _COLUMN_SHARDED else w[name][l]).astype(BF16) for name in names]


def _unshard_big(names, gathered):
    out = {}
    for name, g in zip(names, gathered):
        full = g.reshape(N_DEV * g.shape[1], g.shape[2])
        out[name] = _permute_w_in(full) if name == "w_in" else full
    return out


def _shard_big(grads):
    out = []
    for name, g in grads.items():
        g = _unpermute_w_in(g) if name == "w_in" else g
        out.append(g.reshape(N_DEV, g.shape[0] // N_DEV, g.shape[1]))
    return out


def kernel(x, c, positions, ada_w, ada_b, norm1_g, w_in, ssd_conv_w, ssd_conv_b, ssd_dt_bias, ssd_a_log, ssd_d, ssd_norm_g, pool_w, pool_scale, w_out, norm2_g, ffn_up, ffn_conv_w, ffn_conv_b, ffn_down, final_g, loss_target, m_ada_w, m_ada_b, m_norm1_g, m_w_in, m_ssd_conv_w, m_ssd_conv_b, m_ssd_dt_bias, m_ssd_a_log, m_ssd_d, m_ssd_norm_g, m_pool_w, m_pool_scale, m_w_out, m_norm2_g, m_ffn_up, m_ffn_conv_w, m_ffn_conv_b, m_ffn_down, m_final_g, v_ada_w, v_ada_b, v_norm1_g, v_w_in, v_ssd_conv_w, v_ssd_conv_b, v_ssd_dt_bias, v_ssd_a_log, v_ssd_d, v_ssd_norm_g, v_pool_w, v_pool_scale, v_w_out, v_norm2_g, v_ffn_up, v_ffn_conv_w, v_ffn_conv_b, v_ffn_down, v_final_g):
    w = dict(ada_w=ada_w, ada_b=ada_b, norm1_g=norm1_g, w_in=w_in, ssd_conv_w=ssd_conv_w, ssd_conv_b=ssd_conv_b,
             ssd_dt_bias=ssd_dt_bias, ssd_a_log=ssd_a_log, ssd_d=ssd_d, ssd_norm_g=ssd_norm_g, pool_w=pool_w,
             pool_scale=pool_scale, w_out=w_out, norm2_g=norm2_g, ffn_up=ffn_up, ffn_conv_w=ffn_conv_w,
             ffn_conv_b=ffn_conv_b, ffn_down=ffn_down, final_g=final_g)
    m = dict(ada_w=m_ada_w, ada_b=m_ada_b, norm1_g=m_norm1_g, w_in=m_w_in, ssd_conv_w=m_ssd_conv_w,
             ssd_conv_b=m_ssd_conv_b, ssd_dt_bias=m_ssd_dt_bias, ssd_a_log=m_ssd_a_log, ssd_d=m_ssd_d,
             ssd_norm_g=m_ssd_norm_g, pool_w=m_pool_w, pool_scale=m_pool_scale, w_out=m_w_out, norm2_g=m_norm2_g,
             ffn_up=m_ffn_up, ffn_conv_w=m_ffn_conv_w, ffn_conv_b=m_ffn_conv_b, ffn_down=m_ffn_down,
             final_g=m_final_g)
    v = dict(ada_w=v_ada_w, ada_b=v_ada_b, norm1_g=v_norm1_g, w_in=v_w_in, ssd_conv_w=v_ssd_conv_w,
             ssd_conv_b=v_ssd_conv_b, ssd_dt_bias=v_ssd_dt_bias, ssd_a_log=v_ssd_a_log, ssd_d=v_ssd_d,
             ssd_norm_g=v_ssd_norm_g, pool_w=v_pool_w, pool_scale=v_pool_scale, w_out=v_w_out, norm2_g=v_norm2_g,
             ffn_up=v_ffn_up, ffn_conv_w=v_ffn_conv_w, ffn_conv_b=v_ffn_conv_b, ffn_down=v_ffn_down,
             final_g=v_final_g)
    ix, iy, ic = _axes()
    dev = 4 * ix + 2 * iy + ic

    c_all, scw, fcw = _allgather([c, ssd_conv_w.reshape(DEPTH * SSD_CONV_K, -1),
                                  ffn_conv_w.reshape(DEPTH * FFN_CONV_K, -1)], name="gather_small")
    small_all = c_all
    c_all = c_all.reshape(N_DEV, D_MODEL)
    scw = scw.reshape(N_DEV, DEPTH, SSD_CONV_K, -1).transpose(1, 2, 0, 3).reshape(DEPTH, SSD_CONV_K, SSD_CONV_CH)
    fcw = fcw.reshape(N_DEV, DEPTH, FFN_CONV_K, -1).transpose(1, 2, 0, 3).reshape(DEPTH, FFN_CONV_K, 2 * FFN_DIM)

    mod_cols = _ada_mod(c_all, ada_w, name="ada_mod")
    mod_all = _allgather([mod_cols.reshape(DEPTH * N_DEV, _ADA_SHARD)], name="gather_mod")[0]
    mod_all = mod_all.reshape(N_DEV, DEPTH, N_DEV, _ADA_SHARD)
    mod_mine = lax.dynamic_index_in_dim(mod_all, dev, axis=2, keepdims=False)
    mod = _add_rows(mod_mine.transpose(1, 0, 2).reshape(DEPTH, 6 * D_MODEL), ada_b, name="ada_bias")

    fetched = {}

    def gather(l, group, after):
        if l < DEPTH and (l, group) not in fetched:
            names = dict(_GROUPS)[group]
            shares, _ = lax.optimization_barrier((_big_shares(w, l, names), small_all if after is None else after))
            got = _allgather_async(shares, name=f"gather_weights_l{l}_{group}",
                                   collective_id=1 + 2 * l + (group == "ffn"))
            fetched[l, group] = _unshard_big(names, got)
        return fetched.get((l, group))

    core = ic.astype(jnp.int32).reshape(1)
    from_chips = {}

    def exchange(l, group, g):
        cid = 5 + 4 * l + 2 * (group == "mix")
        blocks = _shard_big(g)
        from_sibling = _pair_exchange(blocks, name=f"grads_pair_exchange_l{l}_{group}", collective_id=cid)
        parts = _pair_sum(core, blocks, from_sibling, name=f"grads_pair_sum_l{l}_{group}")
        got = _chip_exchange(parts, name=f"grads_chip_exchange_l{l}_{group}", collective_id=cid + 1)
        from_chips.update({(l, name): t for name, t in zip(g, got)})

    small = dict(norm1_g=norm1_g, norm2_g=norm2_g, ssd_conv_w=scw, ssd_conv_b=ssd_conv_b, ssd_dt_bias=ssd_dt_bias,
                 ssd_a_log=ssd_a_log, ssd_d=ssd_d, ssd_norm_g=ssd_norm_g, pool_w=pool_w, pool_scale=pool_scale,
                 ffn_conv_w=fcw, ffn_conv_b=ffn_conv_b)

    inv_freq = ROPE_THETA ** (-jnp.arange(0, ROT_DIM, 2, dtype=F32) / ROT_DIM)
    lane = jnp.arange(128) % HEAD_LANES
    inv_freq_lane = jnp.where(lane < ROT_DIM, inv_freq[lane % (ROT_DIM // 2)], 0.0)[None, :]
    pos_col = positions.reshape(SEQ, 1).astype(F32)
    loss_row, dx, dmod, g_small, g_final = _example_step(
        x[0], loss_target[0], pos_col, inv_freq_lane, mod, gather, small, final_g, exchange)

    grads = {}
    for name in _BIG:
        per_layer = [_sum_blocks(from_chips[l, name], name=f"grads_chip_sum_l{l}_{name}") for l in range(DEPTH)]
        grads[name] = jnp.stack([g.T if name in _COLUMN_SHARDED else g for g in per_layer], axis=0)

    small_names = list(_SMALL_GRADS)
    stacked = [jnp.stack([g_small[l][name] for l in range(DEPTH)], axis=0) for name in small_names]
    small_parts = [loss_row, dmod] + [s.reshape(-1, s.shape[-1]) for s in stacked] + [g_final[None]]
    gathered = _allgather(small_parts, name="gather_small_grads")
    total = _sum_gathered(gathered, name="sum_small_grads")
    loss = total[0][0, 0]
    grads["ada_b"] = total[1]
    grads.update(zip(small_names, total[2:-1]))
    grads["final_g"] = total[-1][0]
    dmod_cols = lax.dynamic_slice_in_dim(gathered[1], dev * _ADA_SHARD, _ADA_SHARD, axis=2).transpose(1, 0, 2)
    grads["ada_w"] = _ada_wgrad(c_all, dmod_cols, name="ada_wgrad")
    for name in ("ssd_dt_bias", "ssd_a_log", "ssd_d"):
        grads[name] = grads[name][:, :SSD_HEADS]
    grads["pool_w"] = grads["pool_w"].reshape(pool_w.shape)
    grads["ssd_conv_w"] = lax.dynamic_slice_in_dim(
        grads["ssd_conv_w"].reshape(DEPTH, SSD_CONV_K, SSD_CONV_CH), dev * ssd_conv_w.shape[2], ssd_conv_w.shape[2], axis=2)
    grads["ffn_conv_w"] = lax.dynamic_slice_in_dim(
        grads["ffn_conv_w"].reshape(DEPTH, FFN_CONV_K, 2 * FFN_DIM), dev * ffn_conv_w.shape[2], ffn_conv_w.shape[2], axis=2)

    delta, new_m, new_v = {}, {}, {}
    for name, tr in (("ada_w", 512), ("w_in", 512), ("w_out", 256), ("ffn_up", 512), ("ffn_down", 352)):
        shp = w[name].shape
        two_d = lambda a: a.reshape(shp[0] * shp[1], shp[2])
        d_, m_, v_ = _adamw(two_d(w[name]), two_d(grads[name]), two_d(m[name]), two_d(v[name]), tr=tr,
                               name=f"adamw_{name}")
        delta[name], new_m[name], new_v[name] = (t.reshape(shp) for t in (d_, m_, v_))
    two_d = lambda a: a.reshape(-1, a.shape[-1])
    outs = _adamw_small(*[[two_d(t[name]) for name in _SMALL_PARAMS] for t in (w, grads, m, v)], name="adamw_small")
    for name, d_, m_, v_ in zip(_SMALL_PARAMS, *outs):
        delta[name], new_m[name], new_v[name] = (t.reshape(w[name].shape) for t in (d_, m_, v_))

    grad_x = dx[None]
    return (loss, grad_x, *[grads[n].reshape(w[n].shape) for n in _WEIGHT_ORDER],
            *[delta[n] for n in _WEIGHT_ORDER], *[new_m[n] for n in _WEIGHT_ORDER],
            *[new_v[n] for n in _WEIGHT_ORDER])
```

```python
import functools
import math

import jax
import jax.numpy as jnp
from jax import lax
from jax.experimental import pallas as pl
from jax.experimental.pallas import tpu as pltpu
from jax.experimental.pallas import tpu_sc as plsc

F32 = jnp.float32
BF16 = jnp.bfloat16

N_DEV = 8
D_MODEL = 1024
SEQ = 4096
DEPTH = 2
SSD_INNER = 512
SSD_HEADS = 8
SSD_HPG = 4
SSD_STATE = 128
SSD_CHUNK = 256
SSD_CONV_K = 4
SSD_CONV_CH = 1024
POOL_W = 256
POOL_WINDOWS = (2, 4, 8, 16)
ATT_W = 256
ATT_PATTERNS = ((128, 1), (512, 4), (2048, 16))
ATT_BLOCK = 128
ROT_DIM = 16
ROPE_THETA = 500000.0
IN_W = 2568
FFN_DIM = 2816
FFN_CONV_K = 3
NORM_EPS = 1e-6
HEAD_LANES = 64

ADAM_LR = 0.001
ADAM_B1 = 0.9
ADAM_B2 = 0.999
ADAM_EPS = 1e-08
ADAM_WD = 0.01
ADAM_STEP = 10

PROJ_W = 2816
PROJ_SSD_W = 1792
PROJ_Z_BLK = 2
PROJ_DT_BLK = 12
PROJ_POOL_BLK = 7
PROJ_Q_BLK, PROJ_K_BLK, PROJ_V_BLK = 16, 18, 20
MIX_POOL_BLK = 2
MIX_ATT_BLK = 6
VMEM_LIMIT = 56 * 1024 * 1024
CONV_HALO = 8
POOL_HALO = 16
ATT_KPAD = ATT_BLOCK * 16
MESH = pl.DeviceIdType.MESH
_HBM = pl.BlockSpec(memory_space=pl.ANY)


def _cparams(*sem):
    return pltpu.CompilerParams(dimension_semantics=sem, vmem_limit_bytes=VMEM_LIMIT)


def _resident(shape):
    return pl.BlockSpec(shape, lambda i: (0,) * len(shape), pipeline_mode=pl.Buffered(1))


def _silu(x):
    return x * jax.nn.sigmoid(x)


def _pick_lane(v, h):
    lane = lax.broadcasted_iota(jnp.int32, v.shape, 1)
    return jnp.sum(jnp.where(lane == h, v, 0.0), axis=1, keepdims=True)


def _pick_row(v, h):
    row = lax.broadcasted_iota(jnp.int32, v.shape, 0)
    return jnp.sum(jnp.where(row == h, v, 0.0), axis=0, keepdims=True)


def _head_of_lane(width):
    return lax.broadcasted_iota(jnp.int32, (1, width), 1) // HEAD_LANES


@functools.partial(jax.custom_vjp, nondiff_argnums=(1, 2))
def _shift_rows(x_ext, s, halo):
    y = x_ext if s == 0 else pltpu.roll(x_ext, s, 0)
    return y[halo:]


def _shift_rows_fwd(x_ext, s, halo):
    return _shift_rows(x_ext, s, halo), None


def _shift_rows_bwd(s, halo, _, g):
    ge = jnp.concatenate([jnp.zeros((halo, g.shape[1]), g.dtype), g], axis=0)
    return (ge if s == 0 else pltpu.roll(ge, ge.shape[0] - s, 0),)


_shift_rows.defvjp(_shift_rows_fwd, _shift_rows_bwd)


@functools.partial(jax.custom_vjp, nondiff_argnums=(1,))
def _roll_rows(x, s):
    return pltpu.roll(x, s, 0)


def _roll_rows_fwd(x, s):
    return _roll_rows(x, s), None


def _roll_rows_bwd(s, _, g):
    return (pltpu.roll(g, g.shape[0] - s, 0),)


_roll_rows.defvjp(_roll_rows_fwd, _roll_rows_bwd)


def _rms_modulate(xv, g, sc, sh):
    r = lax.rsqrt(jnp.mean(xv * xv, axis=-1, keepdims=True) + NORM_EPS)
    return (xv * r * g) * (1.0 + sc) + sh


def _mm(a, w, *, name, nt=False, tm=512, tn=None, out_dtype=F32, norm=None, residual=None):
    t, k = a.shape
    n = w.shape[0] if nt else w.shape[1]
    tn = tn or n
    assert tn == n or (norm is None and residual is None)
    extra_in = list(norm or ()) + list(residual or ())

    def body(*refs):
        a_ref, w_ref = refs[:2]
        ins = refs[2:2 + len(extra_in)]
        outs = refs[2 + len(extra_in):]
        if norm is None:
            av = a_ref[...].astype(BF16)
        else:
            av = _rms_modulate(a_ref[...], ins[0][...], ins[1][...], ins[2][...]).astype(BF16)
            outs[1][...] = av
        if nt:
            acc = lax.dot_general(av, w_ref[...], (((1,), (1,)), ((), ())), preferred_element_type=F32)
        else:
            acc = jnp.dot(av, w_ref[...], preferred_element_type=F32)
        outs[0][...] = acc.astype(out_dtype)
        if residual is not None:
            x_ref, gate_ref = ins[-2:]
            outs[-1][...] = x_ref[...] + gate_ref[...] * acc

    row = lambda width: pl.BlockSpec((1, width), lambda i, j: (0, 0))
    tile = lambda width: pl.BlockSpec((tm, width), lambda i, j: (i, 0))
    w_spec = pl.BlockSpec((tn, k), lambda i, j: (j, 0)) if nt else pl.BlockSpec((k, tn), lambda i, j: (0, j))
    in_specs = [tile(k), w_spec] + ([row(k)] * 3 if norm else []) + ([tile(n), row(n)] if residual else [])
    out_specs = [pl.BlockSpec((tm, tn), lambda i, j: (i, j))] + ([tile(k)] if norm else []) + \
        ([tile(n)] if residual else [])
    out_shape = [jax.ShapeDtypeStruct((t, n), out_dtype)] + \
        ([jax.ShapeDtypeStruct((t, k), BF16)] if norm else []) + \
        ([jax.ShapeDtypeStruct((t, n), F32)] if residual else [])
    outs = pl.pallas_call(
        body, grid=(t // tm, n // tn), in_specs=in_specs, out_specs=out_specs, out_shape=out_shape, name=name,
        compiler_params=_cparams("parallel", "parallel"))(a, w, *extra_in)
    return outs[0] if len(outs) == 1 else outs


def _wgrad(a, b, *, name, tk=None, tn=None, tt=512, out_dtype=BF16):
    t, k = a.shape
    n = b.shape[1]
    tk = tk or k
    tn = tn or n
    steps = t // tt

    def body(a_ref, b_ref, o_ref, acc_ref):
        s = pl.program_id(2)

        @pl.when(s == 0)
        def _():
            acc_ref[...] = jnp.zeros_like(acc_ref)

        acc_ref[...] += lax.dot_general(a_ref[...].astype(BF16), b_ref[...].astype(BF16),
                                        (((0,), (0,)), ((), ())), preferred_element_type=F32)

        @pl.when(s == steps - 1)
        def _():
            o_ref[...] = acc_ref[...].astype(out_dtype)

    return pl.pallas_call(
        body, grid=(k // tk, n // tn, steps),
        in_specs=[pl.BlockSpec((tt, tk), lambda i, j, s: (s, i)), pl.BlockSpec((tt, tn), lambda i, j, s: (s, j))],
        out_specs=pl.BlockSpec((tk, tn), lambda i, j, s: (i, j)),
        out_shape=jax.ShapeDtypeStruct((k, n), out_dtype),
        scratch_shapes=[pltpu.VMEM((tk, tn), F32)], name=name,
        compiler_params=_cparams("parallel", "parallel", "arbitrary"))(a, b)


def _norm_mod_bwd(x, dh, dres, g, sc, *, name, w=None, tm=512):
    s, d = x.shape
    steps = s // tm

    def body(x_ref, dh_ref, dres_ref, g_ref, sc_ref, *rest):
        w_ref = rest[0] if w is not None else None
        dx_ref, dg_ref, dsc_ref, dsh_ref, da_acc, dsh_acc = rest[-6:]
        i = pl.program_id(0)

        @pl.when(i == 0)
        def _():
            da_acc[...] = jnp.zeros_like(da_acc)
            dsh_acc[...] = jnp.zeros_like(dsh_acc)

        xv = x_ref[...]
        if w is None:
            dhv = dh_ref[...].astype(F32)
        else:
            dhv = jnp.dot(dh_ref[...], w_ref[...], preferred_element_type=F32)
        r = lax.rsqrt(jnp.mean(xv * xv, axis=-1, keepdims=True) + NORM_EPS)
        xhat = xv * r
        gain = g_ref[...] * (1.0 + sc_ref[...])
        dxhat = dhv * gain
        dx_ref[...] = dres_ref[...] + r * (dxhat - xhat * jnp.mean(dxhat * xhat, axis=-1, keepdims=True))
        da_acc[...] += jnp.sum(dhv * xhat, axis=0, keepdims=True)
        dsh_acc[...] += jnp.sum(dhv, axis=0, keepdims=True)

        @pl.when(i == steps - 1)
        def _():
            dg_ref[...] = da_acc[...] * (1.0 + sc_ref[...])
            dsc_ref[...] = da_acc[...] * g_ref[...]
            dsh_ref[...] = dsh_acc[...]

    row = pl.BlockSpec((1, d), lambda i: (0, 0))
    tile = pl.BlockSpec((tm, d), lambda i: (i, 0))
    row_shape = jax.ShapeDtypeStruct((1, d), F32)
    dh_spec = tile if w is None else pl.BlockSpec((tm, dh.shape[1]), lambda i: (i, 0))
    return pl.pallas_call(
        body, grid=(steps,), in_specs=[tile, dh_spec, tile, row, row] + ([] if w is None else [_resident(w.shape)]),
        out_specs=[tile, row, row, row],
        out_shape=[jax.ShapeDtypeStruct((s, d), F32), row_shape, row_shape, row_shape],
        scratch_shapes=[pltpu.VMEM((1, d), F32), pltpu.VMEM((1, d), F32)], name=name,
        compiler_params=_cparams("arbitrary"))(x, dh, dres, g, sc, *([] if w is None else [w]))


def _mid_bwd(x1, dh2, dx2, norm_g, sc, gate, mo, w_out, *, name, tm=512):
    s, d = x1.shape
    steps = s // tm

    def body(x_ref, dh_ref, dres_ref, ng_ref, sc_ref, g_ref, mo_ref, w_ref,
             dx_ref, dng_ref, dsc_ref, dsh_ref, dmix_ref, dmo_ref, dg_ref, da_acc, dsh_acc):
        i = pl.program_id(0)

        @pl.when(i == 0)
        def _():
            da_acc[...] = jnp.zeros_like(da_acc)
            dsh_acc[...] = jnp.zeros_like(dsh_acc)
            dg_ref[...] = jnp.zeros_like(dg_ref)

        xv = x_ref[...]
        dhv = dh_ref[...]
        r = lax.rsqrt(jnp.mean(xv * xv, axis=-1, keepdims=True) + NORM_EPS)
        xhat = xv * r
        dxhat = dhv * (ng_ref[...] * (1.0 + sc_ref[...]))
        dxv = dres_ref[...] + r * (dxhat - xhat * jnp.mean(dxhat * xhat, axis=-1, keepdims=True))
        dx_ref[...] = dxv
        da_acc[...] += jnp.sum(dhv * xhat, axis=0, keepdims=True)
        dsh_acc[...] += jnp.sum(dhv, axis=0, keepdims=True)
        dmo = (g_ref[...] * dxv).astype(BF16)
        dmo_ref[...] = dmo
        dg_ref[...] += jnp.sum(dxv * mo_ref[...], axis=0, keepdims=True)
        dmix_ref[...] = lax.dot_general(dmo, w_ref[...], (((1,), (1,)), ((), ())), preferred_element_type=F32)

        @pl.when(i == steps - 1)
        def _():
            dng_ref[...] = da_acc[...] * (1.0 + sc_ref[...])
            dsc_ref[...] = da_acc[...] * ng_ref[...]
            dsh_ref[...] = dsh_acc[...]

    tile = pl.BlockSpec((tm, d), lambda i: (i, 0))
    row = pl.BlockSpec((1, d), lambda i: (0, 0))
    mix_tile = pl.BlockSpec((tm, w_out.shape[0]), lambda i: (i, 0))
    row_shape = jax.ShapeDtypeStruct((1, d), F32)
    return pl.pallas_call(
        body, grid=(steps,), in_specs=[tile, tile, tile, row, row, row, tile, _resident(w_out.shape)],
        out_specs=[tile, row, row, row, mix_tile, tile, row],
        out_shape=[jax.ShapeDtypeStruct((s, d), F32), row_shape, row_shape, row_shape,
                   jax.ShapeDtypeStruct((s, w_out.shape[0]), F32), jax.ShapeDtypeStruct((s, d), BF16), row_shape],
        scratch_shapes=[pltpu.VMEM((1, d), F32), pltpu.VMEM((1, d), F32)], name=name,
        compiler_params=_cparams("arbitrary"))(x1, dh2, dx2, norm_g, sc, gate, mo, w_out)


def _final_loss(x, g, target, *, name, tm=512):
    s, d = x.shape
    steps = s // tm

    def body(x_ref, g_ref, t_ref, loss_ref, dx_ref, dg_ref, sq_acc):
        i = pl.program_id(0)

        @pl.when(i == 0)
        def _():
            sq_acc[...] = jnp.zeros_like(sq_acc)
            dg_ref[...] = jnp.zeros_like(dg_ref)

        xv = x_ref[...]
        r = lax.rsqrt(jnp.mean(xv * xv, axis=-1, keepdims=True) + NORM_EPS)
        xhat = xv * r
        err = xhat * g_ref[...] - t_ref[...]
        sq_acc[...] += jnp.sum(err * err, axis=0, keepdims=True)
        dy = err * (1.0 / d)
        dg_ref[...] += jnp.sum(dy * xhat, axis=0, keepdims=True)
        dxhat = dy * g_ref[...]
        dx_ref[...] = r * (dxhat - xhat * jnp.mean(dxhat * xhat, axis=-1, keepdims=True))

        @pl.when(i == steps - 1)
        def _():
            total = jnp.sum(sq_acc[...], axis=1, keepdims=True) * (0.5 / d)
            loss_ref[...] = jnp.broadcast_to(total, loss_ref.shape)

    tile = pl.BlockSpec((tm, d), lambda i: (i, 0))
    row = pl.BlockSpec((1, d), lambda i: (0, 0))
    return pl.pallas_call(
        body, grid=(steps,), in_specs=[tile, row, tile],
        out_specs=[pl.BlockSpec((1, 128), lambda i: (0, 0)), tile, row],
        out_shape=[jax.ShapeDtypeStruct((1, 128), F32), jax.ShapeDtypeStruct((s, d), F32),
                   jax.ShapeDtypeStruct((1, d), F32)],
        scratch_shapes=[pltpu.VMEM((1, d), F32)], name=name, compiler_params=_cparams("arbitrary"))(x, g, target)


def _ssd_chunk(z, xbc_ext, dt_raw, conv_w, conv_b, dt_bias, a_log, d_skip, norm_g, h_in):
    q = z.shape[0]
    gw = SSD_HPG * HEAD_LANES
    xc = conv_b
    for k in range(SSD_CONV_K):
        xc = xc + _pick_row(conv_w, k) * _shift_rows(xbc_ext, SSD_CONV_K - 1 - k, CONV_HALO)
    xc = _silu(xc)
    dt = jax.nn.softplus(dt_raw + dt_bias)
    da = dt * (-jnp.exp(a_log))
    ri = lax.broadcasted_iota(jnp.int32, (q, q), 0)
    ci = lax.broadcasted_iota(jnp.int32, (q, q), 1)
    causal = ri >= ci
    tril = causal.astype(F32)
    a_cum = jnp.dot(tril, da, preferred_element_type=F32, precision=lax.Precision.HIGHEST)
    a_cum_t = lax.dot_general(da, tril, (((0,), (1,)), ((), ())), preferred_element_type=F32,
                              precision=lax.Precision.HIGHEST)
    a_last = _pick_row(a_cum, q - 1)
    head = _head_of_lane(gw)
    ys, hs = [], []
    for g in range(2):
        xs = xc[:, gw * g:gw * (g + 1)]
        bm = xc[:, SSD_INNER + SSD_STATE * g:SSD_INNER + SSD_STATE * (g + 1)]
        cm = xc[:, SSD_INNER + 2 * SSD_STATE + SSD_STATE * g:SSD_INNER + 2 * SSD_STATE + SSD_STATE * (g + 1)]
        cb = lax.dot_general(cm.astype(BF16), bm.astype(BF16), (((1,), (1,)), ((), ())), preferred_element_type=F32)
        cols = [_pick_lane(a_cum, SSD_HPG * g + j) for j in range(SSD_HPG)]
        lasts = [_pick_lane(a_last, SSD_HPG * g + j) for j in range(SSD_HPG)]
        dt_exp = sum(jnp.where(head == j, _pick_lane(dt, SSD_HPG * g + j), 0.0) for j in range(SSD_HPG))
        d_exp = sum(jnp.where(head == j, _pick_lane(d_skip, SSD_HPG * g + j), 0.0) for j in range(SSD_HPG))
        e_cum = sum(jnp.where(head == j, jnp.exp(cols[j]), 0.0) for j in range(SSD_HPG))
        c_dec = sum(jnp.where(head == j, jnp.exp(lasts[j]), 0.0) for j in range(SSD_HPG))
        xsdt = (xs * dt_exp).astype(BF16)
        y_diag = jnp.zeros((q, gw), F32)
        st_new = jnp.zeros((SSD_STATE, gw), F32)
        for j in range(SSD_HPG):
            row = _pick_row(a_cum_t, SSD_HPG * g + j)
            lmat = jnp.exp(jnp.where(causal, cols[j] - row, -jnp.inf))
            r = jnp.dot((cb * lmat).astype(BF16), xsdt, preferred_element_type=F32)
            y_diag = y_diag + jnp.where(head == j, r, 0.0)
            bd = (bm * jnp.exp(lasts[j] - cols[j])).astype(BF16)
            st = lax.dot_general(bd, xsdt, (((0,), (0,)), ((), ())), preferred_element_type=F32)
            st_new = st_new + jnp.where(head == j, st, 0.0)
        y_off = jnp.dot(cm.astype(BF16), h_in[g].astype(BF16), preferred_element_type=F32) * e_cum
        hs.append(h_in[g] * c_dec + st_new)
        y = y_diag + y_off + d_exp * xs
        yz = y * _silu(z[:, gw * g:gw * (g + 1)])
        yz = yz * lax.rsqrt(jnp.mean(yz * yz, axis=-1, keepdims=True) + NORM_EPS)
        ys.append(yz * norm_g[:, gw * g:gw * (g + 1)])
    return jnp.concatenate(ys, axis=1), tuple(hs)


_SSD_NCHUNK = SEQ // SSD_CHUNK
_HALO_PER_CHUNK = SSD_CHUNK // CONV_HALO


def _ssd_param_specs(const):
    return [pl.BlockSpec((8, SSD_CONV_CH), const), pl.BlockSpec((1, SSD_CONV_CH), const),
            pl.BlockSpec((1, 128), const), pl.BlockSpec((1, 128), const), pl.BlockSpec((1, 128), const),
            pl.BlockSpec((1, SSD_INNER), const)]


def _ssd_fwd(proj, conv_w, conv_b, dt_bias, a_log, d_skip, norm_g, *, name):
    q = SSD_CHUNK

    def body(z_ref, xbc_ref, halo_ref, dt_ref, cw_ref, cb_ref, db_ref, al_ref, d_ref, ng_ref, y_ref, hs_ref, h_acc):
        i = pl.program_id(0)

        @pl.when(i == 0)
        def _():
            h_acc[...] = jnp.zeros_like(h_acc)

        halo = jnp.where(i == 0, 0.0, halo_ref[...])
        xbc_ext = jnp.concatenate([halo, xbc_ref[...]], axis=0)
        h_in = (h_acc[0], h_acc[1])
        hs_ref[0, 0] = h_in[0]
        hs_ref[0, 1] = h_in[1]
        y, h_out = _ssd_chunk(z_ref[...], xbc_ext, dt_ref[...], cw_ref[...], cb_ref[...], db_ref[...], al_ref[...],
                              d_ref[...], ng_ref[...], h_in)
        y_ref[...] = y.astype(BF16)
        h_acc[0] = h_out[0]
        h_acc[1] = h_out[1]

    const = lambda i: (0, 0)
    return pl.pallas_call(
        body, grid=(_SSD_NCHUNK,),
        in_specs=[pl.BlockSpec((q, SSD_INNER), lambda i: (i, PROJ_Z_BLK)),
                  pl.BlockSpec((q, SSD_CONV_CH), lambda i: (i, 0)),
                  pl.BlockSpec((CONV_HALO, SSD_CONV_CH), lambda i: (jnp.maximum(i * _HALO_PER_CHUNK - 1, 0), 0)),
                  pl.BlockSpec((q, 128), lambda i: (i, PROJ_DT_BLK))] + _ssd_param_specs(const),
        out_specs=[pl.BlockSpec((q, SSD_INNER), lambda i: (i, 0)),
                   pl.BlockSpec((1, 2, SSD_STATE, 256), lambda i: (i, 0, 0, 0))],
        out_shape=[jax.ShapeDtypeStruct((SEQ, D_MODEL), BF16),
                   jax.ShapeDtypeStruct((_SSD_NCHUNK, 2, SSD_STATE, 256), F32)],
        scratch_shapes=[pltpu.VMEM((2, SSD_STATE, 256), F32)], name=name,
        compiler_params=_cparams("arbitrary"))(proj, proj, proj, proj, conv_w, conv_b, dt_bias, a_log, d_skip, norm_g)


def _ssd_bwd(proj, hstates, dmix, conv_w, conv_b, dt_bias, a_log, d_skip, norm_g, *, name):
    q = SSD_CHUNK
    last = _SSD_NCHUNK - 1

    def body(z_ref, xbc_ref, halo_ref, dt_ref, hs_ref, dy_ref, cw_ref, cb_ref, db_ref, al_ref, d_ref, ng_ref,
             dp_ref, dcw_ref, dcb_ref, ddb_ref, dal_ref, dd_ref, dng_ref, dh_acc, dhalo_acc):
        i = pl.program_id(0)

        @pl.when(i == 0)
        def _():
            dh_acc[...] = jnp.zeros_like(dh_acc)
            dhalo_acc[...] = jnp.zeros_like(dhalo_acc)
            for r in (dcw_ref, dcb_ref, ddb_ref, dal_ref, dd_ref, dng_ref):
                r[...] = jnp.zeros_like(r)

        halo = jnp.where(i == last, 0.0, halo_ref[...])
        xbc_ext = jnp.concatenate([halo, xbc_ref[...]], axis=0)
        _, vjp = jax.vjp(_ssd_chunk, z_ref[...], xbc_ext, dt_ref[...], cw_ref[...], cb_ref[...], db_ref[...],
                         al_ref[...], d_ref[...], ng_ref[...], (hs_ref[0, 0], hs_ref[0, 1]))
        gz, gx, gdt, gcw, gcb, gdb, gal, gd, gng, gh = vjp((dy_ref[...], (dh_acc[0], dh_acc[1])))
        dxbc = jnp.concatenate([gx[CONV_HALO:q], gx[q:] + dhalo_acc[...]], axis=0)
        dp_ref[...] = jnp.concatenate([dxbc, gz, gdt, jnp.zeros_like(gdt)], axis=1).astype(BF16)
        dhalo_acc[...] = gx[:CONV_HALO]
        dh_acc[0] = gh[0]
        dh_acc[1] = gh[1]
        dcw_ref[...] += gcw
        dcb_ref[...] += gcb
        ddb_ref[...] += gdb
        dal_ref[...] += gal
        dd_ref[...] += gd
        dng_ref[...] += gng

    const = lambda i: (0, 0)
    rev = lambda i: last - i
    row = lambda n: jax.ShapeDtypeStruct((1, n), F32)
    return pl.pallas_call(
        body, grid=(_SSD_NCHUNK,),
        in_specs=[pl.BlockSpec((q, SSD_INNER), lambda i: (rev(i), PROJ_Z_BLK)),
                  pl.BlockSpec((q, SSD_CONV_CH), lambda i: (rev(i), 0)),
                  pl.BlockSpec((CONV_HALO, SSD_CONV_CH), lambda i: (jnp.maximum(rev(i) * _HALO_PER_CHUNK - 1, 0), 0)),
                  pl.BlockSpec((q, 128), lambda i: (rev(i), PROJ_DT_BLK)),
                  pl.BlockSpec((1, 2, SSD_STATE, 256), lambda i: (rev(i), 0, 0, 0)),
                  pl.BlockSpec((q, SSD_INNER), lambda i: (rev(i), 0))] + _ssd_param_specs(const),
        out_specs=[pl.BlockSpec((q, PROJ_SSD_W), lambda i: (rev(i), 0))] + _ssd_param_specs(const),
        out_shape=[jax.ShapeDtypeStruct((SEQ, PROJ_W), BF16), jax.ShapeDtypeStruct((8, SSD_CONV_CH), F32),
                   row(SSD_CONV_CH), row(128), row(128), row(128), row(SSD_INNER)],
        scratch_shapes=[pltpu.VMEM((2, SSD_STATE, 256), F32), pltpu.VMEM((CONV_HALO, SSD_CONV_CH), F32)], name=name,
        compiler_params=_cparams("arbitrary"))(proj, proj, proj, proj, hstates, dmix, conv_w, conv_b, dt_bias, a_log,
                                                d_skip, norm_g)


def _rope_tables(pos_col, inv_freq_lane, *, name):
    s = pos_col.shape[0]

    def body(p_ref, f_ref, c_ref, s1_ref, s2_ref):
        ang = p_ref[...] * f_ref[...]
        within = lax.broadcasted_iota(jnp.int32, ang.shape, 1) % HEAD_LANES
        half = ROT_DIM // 2
        c_ref[...] = jnp.where(within < ROT_DIM, jnp.cos(ang), 1.0)
        sn = jnp.sin(ang)
        s1_ref[...] = jnp.where(within < half, -sn, 0.0)
        s2_ref[...] = jnp.where((within >= half) & (within < ROT_DIM), sn, 0.0)

    shp = jax.ShapeDtypeStruct((s, 128), F32)
    return pl.pallas_call(body, out_shape=[shp, shp, shp], name=name,
                          compiler_params=pltpu.CompilerParams(vmem_limit_bytes=VMEM_LIMIT))(pos_col, inv_freq_lane)


def _rope(t, c, s1, s2):
    half = ROT_DIM // 2
    return t * c + pltpu.roll(t, 128 - half, 1) * s1 + pltpu.roll(t, half, 1) * s2


def _rope_t(g, c, s1, s2):
    half = ROT_DIM // 2
    return g * c + pltpu.roll(g * s1, half, 1) + pltpu.roll(g * s2, 128 - half, 1)


def _att_valid(b):
    qi = lax.broadcasted_iota(jnp.int32, (ATT_BLOCK, 2 * ATT_BLOCK), 0)
    kj = lax.broadcasted_iota(jnp.int32, (ATT_BLOCK, 2 * ATT_BLOCK), 1)
    rel = qi + ATT_BLOCK - kj
    return (rel >= 0) & (rel <= ATT_BLOCK) & (b * ATT_BLOCK + kj - ATT_BLOCK >= 0)


def _att_slices(i, d):
    if d == 1:
        qstart = pl.multiple_of(i * ATT_BLOCK, ATT_BLOCK)
        return i, pl.ds(qstart, ATT_BLOCK), pl.ds(pl.multiple_of(qstart - ATT_BLOCK + ATT_KPAD, ATT_BLOCK), 2 * ATT_BLOCK)
    r = i % d
    b = i // d
    qstart = r + d * ATT_BLOCK * b
    return b, pl.ds(qstart, ATT_BLOCK, stride=d), pl.ds(qstart - ATT_BLOCK * d + ATT_KPAD, 2 * ATT_BLOCK, stride=d)


_ATT_NBLK = SEQ // ATT_BLOCK
_ATT_SCALE = HEAD_LANES ** -0.5
_ATT_UNROLL_FWD = 8
_ATT_UNROLL = 4


def _att_fwd(proj, cos, sin1, sin2, mix, *, name):
    s = SEQ

    def body(q_ref, k_ref, v_ref, c_ref, s1_ref, s2_ref, _, o_ref, lse_ref, mix_ref, qs, ks, vs, acc, m_s, l_s):
        c, s1, s2 = c_ref[...], s1_ref[...], s2_ref[...]
        qs[...] = _rope(q_ref[...], c, s1, s2) * _ATT_SCALE
        zeros = jnp.zeros((ATT_KPAD, 128), F32)
        ks[pl.ds(0, ATT_KPAD), :] = zeros
        vs[pl.ds(0, ATT_KPAD), :] = zeros
        ks[pl.ds(ATT_KPAD, s), :] = _rope(k_ref[...], c, s1, s2)
        vs[pl.ds(ATT_KPAD, s), :] = v_ref[...]
        head0 = _head_of_lane(128) == 0

        for bi, (_, d) in enumerate(ATT_PATTERNS):
            def blk(i, carry, d=d, first=(bi == 0)):
                b, sq, sk = _att_slices(i, d)
                qb = qs[sq, :]
                kw = ks[sk, :].astype(BF16)
                vw = vs[sk, :].astype(BF16)
                valid = _att_valid(b)
                ms, ls, os_ = [], [], []
                for hh in range(2):
                    qh = jnp.where(head0 if hh == 0 else ~head0, qb, 0.0).astype(BF16)
                    sc = lax.dot_general(qh, kw, (((1,), (1,)), ((), ())), preferred_element_type=F32)
                    sc = jnp.where(valid, sc, -jnp.inf)
                    mb = jnp.max(sc, axis=1, keepdims=True)
                    p = jnp.exp(sc - mb)
                    ms.append(mb)
                    ls.append(jnp.sum(p, axis=1, keepdims=True))
                    os_.append(jnp.dot(p.astype(BF16), vw, preferred_element_type=F32))
                m_b = jnp.where(head0, ms[0], ms[1])
                l_b = jnp.where(head0, ls[0], ls[1])
                o_b = jnp.where(head0, os_[0], os_[1])
                if first:
                    m_s[sq, :] = m_b
                    l_s[sq, :] = l_b
                    acc[sq, :] = o_b
                else:
                    m_old = m_s[sq, :]
                    m_new = jnp.maximum(m_old, m_b)
                    a_old = jnp.exp(m_old - m_new)
                    a_b = jnp.exp(m_b - m_new)
                    m_s[sq, :] = m_new
                    l_s[sq, :] = l_s[sq, :] * a_old + l_b * a_b
                    acc[sq, :] = acc[sq, :] * a_old + o_b * a_b
                return carry

            lax.fori_loop(0, _ATT_NBLK, blk, 0, unroll=_ATT_UNROLL_FWD)

        out = acc[...] / l_s[...]
        o_ref[...] = out
        mix_ref[...] = out.astype(BF16)
        lse_ref[...] = m_s[...] + jnp.log(l_s[...])

    col = lambda base: pl.BlockSpec((s, 128), lambda p: (0, base + p))
    tab = pl.BlockSpec((s, 128), lambda p: (0, 0))
    big = pltpu.VMEM((ATT_KPAD + s, 128), F32)
    tok = pltpu.VMEM((s, 128), F32)
    return pl.pallas_call(
        body, grid=(2,), in_specs=[col(PROJ_Q_BLK), col(PROJ_K_BLK), col(PROJ_V_BLK), tab, tab, tab, _HBM],
        out_specs=[pl.BlockSpec((s, 128), lambda p: (0, p)), pl.BlockSpec((s, 128), lambda p: (0, p)),
                   col(MIX_ATT_BLK)],
        out_shape=[jax.ShapeDtypeStruct((s, ATT_W), F32), jax.ShapeDtypeStruct((s, ATT_W), F32),
                   jax.ShapeDtypeStruct(mix.shape, mix.dtype)],
        input_output_aliases={6: 2}, scratch_shapes=[tok, big, big, tok, tok, tok], name=name,
        compiler_params=_cparams("arbitrary"))(proj, proj, proj, cos, sin1, sin2, mix)


def _att_bwd(proj, cos, sin1, sin2, out, lse, dmix, dproj, *, name):
    s = SEQ

    def body(proj_ref, c_hbm, s1_hbm, s2_hbm, out_hbm, lse_hbm, dmix_hbm, _, dproj_hbm,
             c_ref, s1_ref, s2_ref, o_ref, lse_ref, do_ref, qs, ks, vs, dqs, dks, dvs, staged, sems):
        def start(copies):
            for cp in copies:
                cp.start()
            return copies

        def load(pair):
            lanes = pl.ds(128 * pair, 128)
            rows = pl.ds(ATT_KPAD, s)
            return start([
                pltpu.make_async_copy(proj_ref.at[:, pl.ds(128 * (PROJ_Q_BLK + pair), 128)], qs, sems.at[0]),
                pltpu.make_async_copy(proj_ref.at[:, pl.ds(128 * (PROJ_K_BLK + pair), 128)], ks.at[rows, :], sems.at[1]),
                pltpu.make_async_copy(proj_ref.at[:, pl.ds(128 * (PROJ_V_BLK + pair), 128)], vs.at[rows, :], sems.at[2]),
                pltpu.make_async_copy(out_hbm.at[:, lanes], o_ref, sems.at[3]),
                pltpu.make_async_copy(lse_hbm.at[:, lanes], lse_ref, sems.at[4]),
                pltpu.make_async_copy(dmix_hbm.at[:, pl.ds(128 * (MIX_ATT_BLK + pair), 128)], do_ref, sems.at[5])])

        tables = start([pltpu.make_async_copy(c_hbm, c_ref, sems.at[6]),
                        pltpu.make_async_copy(s1_hbm, s1_ref, sems.at[7]),
                        pltpu.make_async_copy(s2_hbm, s2_ref, sems.at[8])])
        loads = load(0)
        for cp in tables:
            cp.wait()
        head0 = _head_of_lane(128) == 0
        zeros = jnp.zeros((ATT_KPAD, 128), F32)
        for pair in range(2):
            for cp in loads:
                cp.wait()
            c, s1, s2 = c_ref[...], s1_ref[...], s2_ref[...]
            qs[...] = _rope(qs[...], c, s1, s2) * _ATT_SCALE
            ks[pl.ds(0, ATT_KPAD), :] = zeros
            vs[pl.ds(0, ATT_KPAD), :] = zeros
            ks[pl.ds(ATT_KPAD, s), :] = _rope(ks[pl.ds(ATT_KPAD, s), :], c, s1, s2)
            dqs[...] = jnp.zeros_like(dqs)
            dks[...] = jnp.zeros_like(dks)
            dvs[...] = jnp.zeros_like(dvs)

            for _, d in ATT_PATTERNS:
                def blk(i, carry, d=d):
                    b, sq, sk = _att_slices(i, d)
                    qb = qs[sq, :]
                    kw = ks[sk, :].astype(BF16)
                    vw = vs[sk, :].astype(BF16)
                    dob = do_ref[sq, :]
                    lse_b = lse_ref[sq, :]
                    dd = dob * o_ref[sq, :]
                    valid = _att_valid(b)
                    dq_b = jnp.zeros((ATT_BLOCK, 128), F32)
                    dk_w = jnp.zeros((2 * ATT_BLOCK, 128), F32)
                    dv_w = jnp.zeros((2 * ATT_BLOCK, 128), F32)
                    for hh in range(2):
                        hm = head0 if hh == 0 else ~head0
                        qh = jnp.where(hm, qb, 0.0).astype(BF16)
                        doh = jnp.where(hm, dob, 0.0).astype(BF16)
                        lse_h = _pick_lane(lse_b, hh * HEAD_LANES)
                        d_h = jnp.sum(jnp.where(hm, dd, 0.0), axis=1, keepdims=True)
                        sc = lax.dot_general(qh, kw, (((1,), (1,)), ((), ())), preferred_element_type=F32)
                        p = jnp.where(valid, jnp.exp(sc - lse_h), 0.0)
                        dp = lax.dot_general(doh, vw, (((1,), (1,)), ((), ())), preferred_element_type=F32)
                        ds = (p * (dp - d_h)).astype(BF16)
                        dq_b = dq_b + jnp.where(hm, jnp.dot(ds, kw, preferred_element_type=F32), 0.0)
                        dk_w = dk_w + lax.dot_general(ds, qh, (((0,), (0,)), ((), ())), preferred_element_type=F32)
                        dv_w = dv_w + lax.dot_general(p.astype(BF16), doh, (((0,), (0,)), ((), ())),
                                                      preferred_element_type=F32)
                    dqs[sq, :] += dq_b
                    dks[sk, :] += dk_w
                    dvs[sk, :] += dv_w
                    return carry

                lax.fori_loop(0, _ATT_NBLK, blk, 0, unroll=_ATT_UNROLL)

            staged[0] = _rope_t(dqs[...] * _ATT_SCALE, c, s1, s2).astype(BF16)
            staged[1] = _rope_t(dks[pl.ds(ATT_KPAD, s), :], c, s1, s2).astype(BF16)
            staged[2] = dvs[pl.ds(ATT_KPAD, s), :].astype(BF16)
            stores = start([
                pltpu.make_async_copy(staged.at[j], dproj_hbm.at[:, pl.ds(128 * (col + pair), 128)], sems.at[9 + j])
                for j, col in enumerate((PROJ_Q_BLK, PROJ_K_BLK, PROJ_V_BLK))])
            if pair == 0:
                loads = load(1)
            for cp in stores:
                cp.wait()

    big = pltpu.VMEM((ATT_KPAD + s, 128), F32)
    tok = pltpu.VMEM((s, 128), F32)
    return pl.pallas_call(
        body, in_specs=[_HBM] * 8, out_specs=_HBM, out_shape=jax.ShapeDtypeStruct(dproj.shape, dproj.dtype),
        input_output_aliases={7: 0},
        scratch_shapes=[tok] * 6 + [tok, big, big, tok, big, big, pltpu.VMEM((3, s, 128), BF16),
                                    pltpu.SemaphoreType.DMA((12,))], name=name,
        compiler_params=pltpu.CompilerParams(vmem_limit_bytes=VMEM_LIMIT))(
            proj, cos, sin1, sin2, out, lse, dmix, dproj)


_POOL_TM = 512
_POOL_NT = SEQ // _POOL_TM
_POOL_HALO_PER_TILE = _POOL_TM // POOL_HALO


def _pool_tile(u_ext, w_bd, scale, t0):
    s2 = u_ext + _roll_rows(u_ext, 1)
    s4 = s2 + _roll_rows(s2, 2)
    s8 = s4 + _roll_rows(s4, 4)
    s16 = s8 + _roll_rows(s8, 8)
    grp = _head_of_lane(POOL_W)
    sel = jnp.where(grp == 0, s2, jnp.where(grp == 1, s4, jnp.where(grp == 2, s8, s16)))[POOL_HALO:]
    t = sel.shape[0]
    pos = t0 + lax.broadcasted_iota(jnp.int32, (t, POOL_W), 0) + 1
    win = jnp.where(grp == 0, 2, jnp.where(grp == 1, 4, jnp.where(grp == 2, 8, 16)))
    cnt = jnp.minimum(pos, win).astype(F32)
    diff = sel / cnt - u_ext[POOL_HALO:]
    return jnp.dot(diff.astype(BF16), w_bd.astype(BF16), preferred_element_type=F32) * scale


def _pool_fwd(proj, w_bd, scale, mix, *, name):
    tm = _POOL_TM

    def body(u_ref, halo_ref, w_ref, sc_ref, _, y_ref):
        i = pl.program_id(0)
        halo = jnp.where(i == 0, 0.0, halo_ref[...])
        u_ext = jnp.concatenate([halo, u_ref[...]], axis=0)
        y_ref[...] = _pool_tile(u_ext, w_ref[...], sc_ref[...], i * tm).astype(BF16)

    return pl.pallas_call(
        body, grid=(_POOL_NT,),
        in_specs=[pl.BlockSpec((tm, POOL_W), lambda i: (i, PROJ_POOL_BLK)),
                  pl.BlockSpec((POOL_HALO, POOL_W),
                               lambda i: (jnp.maximum(i * _POOL_HALO_PER_TILE - 1, 0), PROJ_POOL_BLK)),
                  pl.BlockSpec((POOL_W, POOL_W), lambda i: (0, 0)), pl.BlockSpec((1, POOL_W), lambda i: (0, 0)), _HBM],
        out_specs=pl.BlockSpec((tm, POOL_W), lambda i: (i, MIX_POOL_BLK)),
        out_shape=jax.ShapeDtypeStruct(mix.shape, mix.dtype), input_output_aliases={4: 0}, name=name,
        compiler_params=_cparams("parallel"))(proj, proj, w_bd, scale, mix)


def _pool_bwd(proj, dmix, w_bd, scale, dproj, *, name):
    tm = _POOL_TM
    last = _POOL_NT - 1

    def body(u_ref, halo_ref, dy_ref, w_ref, sc_ref, _, du_ref, dw_ref, dsc_ref, dhalo_acc):
        i = pl.program_id(0)

        @pl.when(i == 0)
        def _():
            dhalo_acc[...] = jnp.zeros_like(dhalo_acc)
            dw_ref[...] = jnp.zeros_like(dw_ref)
            dsc_ref[...] = jnp.zeros_like(dsc_ref)

        tile = last - i
        halo = jnp.where(tile == 0, 0.0, halo_ref[...])
        u_ext = jnp.concatenate([halo, u_ref[...]], axis=0)
        _, vjp = jax.vjp(functools.partial(_pool_tile, t0=tile * tm), u_ext, w_ref[...], sc_ref[...])
        gu, gw, gs = vjp(dy_ref[...])
        du_ref[...] = jnp.concatenate([gu[POOL_HALO:tm], gu[tm:] + dhalo_acc[...]], axis=0).astype(BF16)
        dhalo_acc[...] = gu[:POOL_HALO]
        dw_ref[...] += gw
        dsc_ref[...] += gs

    rev = lambda i: last - i
    return pl.pallas_call(
        body, grid=(_POOL_NT,),
        in_specs=[pl.BlockSpec((tm, POOL_W), lambda i: (rev(i), PROJ_POOL_BLK)),
                  pl.BlockSpec((POOL_HALO, POOL_W),
                               lambda i: (jnp.maximum(rev(i) * _POOL_HALO_PER_TILE - 1, 0), PROJ_POOL_BLK)),
                  pl.BlockSpec((tm, POOL_W), lambda i: (rev(i), MIX_POOL_BLK)),
                  pl.BlockSpec((POOL_W, POOL_W), lambda i: (0, 0)), pl.BlockSpec((1, POOL_W), lambda i: (0, 0)), _HBM],
        out_specs=[pl.BlockSpec((tm, POOL_W), lambda i: (rev(i), PROJ_POOL_BLK)),
                   pl.BlockSpec((POOL_W, POOL_W), lambda i: (0, 0)), pl.BlockSpec((1, POOL_W), lambda i: (0, 0))],
        out_shape=[jax.ShapeDtypeStruct(dproj.shape, dproj.dtype), jax.ShapeDtypeStruct((POOL_W, POOL_W), F32),
                   jax.ShapeDtypeStruct((1, POOL_W), F32)],
        input_output_aliases={5: 0}, scratch_shapes=[pltpu.VMEM((POOL_HALO, POOL_W), F32)], name=name,
        compiler_params=_cparams("arbitrary"))(proj, proj, dmix, w_bd, scale, dproj)


_FFN_TM = 256
_FFN_NT = SEQ // _FFN_TM
_FFN_HALO_PER_TILE = _FFN_TM // CONV_HALO


def _ffn_act_tile(hid_ext, conv_w, conv_b):
    hc = conv_b
    for k in range(FFN_CONV_K):
        hc = hc + _pick_row(conv_w, k) * _shift_rows(hid_ext, FFN_CONV_K - 1 - k, CONV_HALO)
    return _silu(hc[:, :FFN_DIM]) * hc[:, FFN_DIM:]


def _ffn_fwd(x1, norm_g, sc, sh, gate, up_t, down, conv_w, conv_b, *, name):
    tm = _FFN_TM
    w = 2 * FFN_DIM
    d = D_MODEL

    def body(x_ref, ng_ref, sc_ref, sh_ref, g_ref, up_ref, dn_ref, cw_ref, cb_ref,
             h_ref, hid_ref, act_ref, f_ref, x2_ref, halo_acc):
        i = pl.program_id(0)
        h2 = _rms_modulate(x_ref[...], ng_ref[...], sc_ref[...], sh_ref[...]).astype(BF16)
        h_ref[...] = h2
        hid = lax.dot_general(h2, up_ref[...], (((1,), (1,)), ((), ())), preferred_element_type=F32)
        hid_ref[...] = hid
        halo = jnp.where(i == 0, 0.0, halo_acc[...])
        act = _ffn_act_tile(jnp.concatenate([halo, hid], axis=0), cw_ref[...], cb_ref[...]).astype(BF16)
        halo_acc[...] = hid[tm - CONV_HALO:]
        act_ref[...] = act
        f = jnp.dot(act, dn_ref[...], preferred_element_type=F32)
        f_ref[...] = f
        x2_ref[...] = x_ref[...] + g_ref[...] * f

    tile = lambda n: pl.BlockSpec((tm, n), lambda i: (i, 0))
    return pl.pallas_call(
        body, grid=(_FFN_NT,),
        in_specs=[tile(d)] + [_resident((1, d))] * 4 + [_resident((w, d)), _resident((FFN_DIM, d)),
                                                        _resident((8, w)), _resident((1, w))],
        out_specs=[tile(d), tile(w), tile(FFN_DIM), tile(d), tile(d)],
        out_shape=[jax.ShapeDtypeStruct((SEQ, d), BF16), jax.ShapeDtypeStruct((SEQ, w), F32),
                   jax.ShapeDtypeStruct((SEQ, FFN_DIM), BF16), jax.ShapeDtypeStruct((SEQ, d), F32),
                   jax.ShapeDtypeStruct((SEQ, d), F32)],
        scratch_shapes=[pltpu.VMEM((CONV_HALO, w), F32)], name=name,
        compiler_params=_cparams("arbitrary"))(x1, norm_g, sc, sh, gate, up_t, down, conv_w, conv_b)


def _ffn_bwd(dx2, gate, f, hid, up_t, down, conv_w, conv_b, *, name):
    tm = _FFN_TM
    w = 2 * FFN_DIM
    d = D_MODEL
    last = _FFN_NT - 1

    def body(dx_ref, g_ref, f_ref, h_ref, halo_ref, up_ref, dn_ref, cw_ref, cb_ref,
             df_ref, dg_ref, dh_ref, dh2_ref, dcw_ref, dcb_ref, dhalo_acc):
        i = pl.program_id(0)

        @pl.when(i == 0)
        def _():
            dhalo_acc[...] = jnp.zeros_like(dhalo_acc)
            dcw_ref[...] = jnp.zeros_like(dcw_ref)
            dcb_ref[...] = jnp.zeros_like(dcb_ref)
            dg_ref[...] = jnp.zeros_like(dg_ref)

        dxv = dx_ref[...]
        df = (g_ref[...] * dxv).astype(BF16)
        df_ref[...] = df
        dg_ref[...] += jnp.sum(dxv * f_ref[...], axis=0, keepdims=True)
        dact = lax.dot_general(df, dn_ref[...], (((1,), (1,)), ((), ())), preferred_element_type=F32)
        halo = jnp.where(i == last, 0.0, halo_ref[...])
        hid_ext = jnp.concatenate([halo, h_ref[...]], axis=0)
        _, vjp = jax.vjp(_ffn_act_tile, hid_ext, cw_ref[...], cb_ref[...])
        gh, gw, gb = vjp(dact)
        dhid = jnp.concatenate([gh[CONV_HALO:tm], gh[tm:] + dhalo_acc[...]], axis=0).astype(BF16)
        dhalo_acc[...] = gh[:CONV_HALO]
        dh_ref[...] = dhid
        dh2_ref[...] = jnp.dot(dhid, up_ref[...], preferred_element_type=F32)
        dcw_ref[...] += gw
        dcb_ref[...] += gb

    rev = lambda i: last - i
    tile = lambda n: pl.BlockSpec((tm, n), lambda i: (rev(i), 0))
    acc = lambda shape: pl.BlockSpec(shape, lambda i: (0, 0))
    return pl.pallas_call(
        body, grid=(_FFN_NT,),
        in_specs=[tile(d), _resident((1, d)), tile(d), tile(w),
                  pl.BlockSpec((CONV_HALO, w), lambda i: (jnp.maximum(rev(i) * _FFN_HALO_PER_TILE - 1, 0), 0)),
                  _resident((w, d)), _resident((FFN_DIM, d)), _resident((8, w)), _resident((1, w))],
        out_specs=[tile(d), acc((1, d)), tile(w), tile(d), acc((8, w)), acc((1, w))],
        out_shape=[jax.ShapeDtypeStruct((SEQ, d), BF16), jax.ShapeDtypeStruct((1, d), F32),
                   jax.ShapeDtypeStruct((SEQ, w), BF16), jax.ShapeDtypeStruct((SEQ, d), F32),
                   jax.ShapeDtypeStruct((8, w), F32), jax.ShapeDtypeStruct((1, w), F32)],
        scratch_shapes=[pltpu.VMEM((CONV_HALO, w), F32)], name=name,
        compiler_params=_cparams("arbitrary"))(dx2, gate, f, hid, hid, up_t, down, conv_w, conv_b)


def _axes():
    return lax.axis_index("x"), lax.axis_index("y"), lax.axis_index("c")


def _handshake(peers):
    barrier = pltpu.get_barrier_semaphore()
    for peer in peers:
        pl.semaphore_signal(barrier, inc=1, device_id=peer, device_id_type=MESH)
    pl.semaphore_wait(barrier, len(peers))


def _allgather_body(x_refs, out_refs, send_sems, recv_sems, local_sems, own_barrier):
    n = len(x_refs)
    x, y, c = _axes()
    me, sibling = (x, y, c), (x, y, 1 - c)
    chips = [(1 - x, y), (x, 1 - y), (1 - x, 1 - y)]
    if own_barrier:
        _handshake([sibling] + [(*chip, c) for chip in chips])

    def slot(a, px, py, pc):
        return out_refs[a].at[4 * px + 2 * py + pc]

    def copy(a, k, block, to, src=None):
        return pltpu.make_async_remote_copy(
            src_ref=slot(a, *block) if src is None else src, dst_ref=slot(a, *block),
            send_sem=send_sems.at[a, k], recv_sem=recv_sems.at[a, k], device_id=to, device_id_type=MESH)

    mines, firsts = [], []
    for a in range(n):
        mines.append(pltpu.make_async_copy(x_refs[a], slot(a, *me), local_sems.at[a]))
        mines[-1].start()
        first = [copy(a, 0, me, sibling, src=x_refs[a])]
        first += [copy(a, 1 + j, me, (*chip, c), src=x_refs[a]) for j, chip in enumerate(chips)]
        for cp in first:
            cp.start()
        firsts += first
    passed = []
    for j, chip in enumerate(chips):
        for a in range(n):
            copy(a, 1 + j, (*chip, c), me).wait_recv()
            passed.append(copy(a, 4 + j, (*chip, c), sibling))
            passed[-1].start()
    for a in range(n):
        copy(a, 0, sibling, me).wait_recv()
    for j, chip in enumerate(chips):
        for a in range(n):
            copy(a, 4 + j, (*chip, 1 - c), me).wait_recv()
    for cp in firsts + passed:
        cp.wait_send()
    for cp in mines:
        cp.wait()


def _allgather_sems(n):
    return [pltpu.SemaphoreType.DMA((n, 7)), pltpu.SemaphoreType.DMA((n, 7)), pltpu.SemaphoreType.DMA((n,))]


def _allgather(xs, *, name):
    n = len(xs)

    def body(*refs):
        _allgather_body(refs[:n], refs[n:2 * n], *refs[2 * n:], own_barrier=False)

    return pl.pallas_call(
        body, out_shape=[jax.ShapeDtypeStruct((N_DEV,) + xb.shape, xb.dtype) for xb in xs],
        in_specs=[_HBM] * n, out_specs=[_HBM] * n, scratch_shapes=_allgather_sems(n), name=name)(*xs)


def _allgather_async(xs, *, name, collective_id):
    n = len(xs)
    x_refs = [jax.new_ref(xb, memory_space=pltpu.MemorySpace.HBM) for xb in xs]
    out_refs = [jax.empty_ref(jax.ShapeDtypeStruct((N_DEV,) + xb.shape, xb.dtype), memory_space=pltpu.MemorySpace.HBM)
                for xb in xs]

    @pl.kernel(mesh=plsc.ScalarSubcoreMesh(axis_name="sequencer", num_cores=1), name=name,
               scratch_types=tuple(_allgather_sems(n)),
               compiler_params=pltpu.CompilerParams(collective_id=collective_id))
    def launch(send_sems, recv_sems, local_sems):
        _allgather_body(x_refs, out_refs, send_sems, recv_sems, local_sems, own_barrier=True)

    launch()
    return [r[...] for r in out_refs]


def _pair_exchange(blocks, *, name, collective_id):
    n = len(blocks)
    hbm = pltpu.MemorySpace.HBM
    in_refs = [jax.new_ref(b, memory_space=hbm) for b in blocks]
    out_refs = [jax.empty_ref(jax.ShapeDtypeStruct((4,) + b.shape[1:], b.dtype), memory_space=hbm) for b in blocks]

    @pl.kernel(mesh=plsc.ScalarSubcoreMesh(axis_name="sequencer", num_cores=1), name=name,
               scratch_types=(pltpu.SemaphoreType.DMA((n, 4)), pltpu.SemaphoreType.DMA((n, 4))),
               compiler_params=pltpu.CompilerParams(collective_id=collective_id))
    def launch(send_sems, recv_sems):
        x, y, c = _axes()
        _handshake([(x, y, 1 - c)])
        copies = [pltpu.make_async_remote_copy(
            src_ref=in_refs[a].at[2 * s + (1 - c)], dst_ref=out_refs[a].at[s], send_sem=send_sems.at[a, s],
            recv_sem=recv_sems.at[a, s], device_id=(x, y, 1 - c), device_id_type=MESH)
            for a in range(n) for s in range(4)]
        for cp in copies:
            cp.start()
        for cp in copies:
            cp.wait_recv()
        for cp in copies:
            cp.wait_send()

    launch()
    return [r[...] for r in out_refs]


def _chip_exchange(parts, *, name, collective_id):
    n = len(parts)
    hbm = pltpu.MemorySpace.HBM
    in_refs = [jax.new_ref(p, memory_space=hbm) for p in parts]
    out_refs = [jax.empty_ref(jax.ShapeDtypeStruct(p.shape, p.dtype), memory_space=hbm) for p in parts]

    @pl.kernel(mesh=plsc.ScalarSubcoreMesh(axis_name="sequencer", num_cores=1), name=name,
               scratch_types=(pltpu.SemaphoreType.DMA((n, 3)), pltpu.SemaphoreType.DMA((n, 3)),
                              pltpu.SemaphoreType.DMA((n,))),
               compiler_params=pltpu.CompilerParams(collective_id=collective_id))
    def launch(send_sems, recv_sems, local_sems):
        x, y, c = _axes()
        my_chip = 2 * x + y
        chips = [(1 - x, y), (x, 1 - y), (1 - x, 1 - y)]
        _handshake([(*chip, c) for chip in chips])
        locals_ = [pltpu.make_async_copy(in_refs[a].at[my_chip], out_refs[a].at[my_chip], local_sems.at[a])
                   for a in range(n)]
        for cp in locals_:
            cp.start()
        copies = [pltpu.make_async_remote_copy(
            src_ref=in_refs[a].at[2 * px + py], dst_ref=out_refs[a].at[my_chip], send_sem=send_sems.at[a, k],
            recv_sem=recv_sems.at[a, k], device_id=(px, py, c), device_id_type=MESH)
            for a in range(n) for k, (px, py) in enumerate(chips)]
        for cp in copies:
            cp.start()
        for cp in copies:
            cp.wait_recv()
        for cp in copies:
            cp.wait_send()
        for cp in locals_:
            cp.wait()

    launch()
    return [r[...] for r in out_refs]


def _pair_sum(core, blocks, from_sibling, *, name):
    n = len(blocks)

    def body(core_ref, *refs):
        for a_ref, b_ref, o_ref in zip(refs[:n], refs[n:2 * n], refs[2 * n:]):
            o_ref[...] = (a_ref[...].astype(F32) + b_ref[...].astype(F32)).astype(o_ref.dtype)

    mine = lambda b: pl.BlockSpec((1,) + b.shape[1:], lambda s, core_ref: (2 * s + core_ref[0], 0, 0))
    slot = lambda b: pl.BlockSpec((1,) + b.shape[1:], lambda s, core_ref: (s, 0, 0))
    return pl.pallas_call(
        body,
        grid_spec=pltpu.PrefetchScalarGridSpec(
            num_scalar_prefetch=1, grid=(4,),
            in_specs=[mine(b) for b in blocks] + [slot(b) for b in blocks], out_specs=[slot(b) for b in blocks]),
        out_shape=[jax.ShapeDtypeStruct(s.shape, s.dtype) for s in from_sibling], name=name,
        compiler_params=_cparams("parallel"))(core, *blocks, *from_sibling)


def _sum_blocks(a, *, name, tr=None):
    n, r, cdim = a.shape
    tr = tr or r

    def body(a_ref, o_ref):
        acc = a_ref[0].astype(F32)
        for k in range(1, n):
            acc = acc + a_ref[k].astype(F32)
        o_ref[...] = acc

    return pl.pallas_call(body, grid=(r // tr,), in_specs=[pl.BlockSpec((n, tr, cdim), lambda i: (0, i, 0))],
                          out_specs=pl.BlockSpec((tr, cdim), lambda i: (i, 0)),
                          out_shape=jax.ShapeDtypeStruct((r, cdim), F32), name=name,
                          compiler_params=_cparams("parallel"))(a)


def _sum_gathered(gathered, *, name):
    n = len(gathered)

    def body(*refs):
        for a_ref, o_ref in zip(refs[:n], refs[n:]):
            acc = a_ref[0]
            for k in range(1, N_DEV):
                acc = acc + a_ref[k]
            o_ref[...] = acc

    return pl.pallas_call(body, out_shape=[jax.ShapeDtypeStruct(g.shape[1:], F32) for g in gathered], name=name,
                          compiler_params=pltpu.CompilerParams(vmem_limit_bytes=VMEM_LIMIT))(*gathered)


_ADA_SHARD = 6 * D_MODEL // N_DEV


def _ada_mod(c_all, ada_w, *, name):
    def body(c_ref, w_ref, o_ref):
        o_ref[0] = jnp.dot(_silu(c_ref[...]).astype(BF16), w_ref[0].astype(BF16), preferred_element_type=F32)

    return pl.pallas_call(
        body, grid=(DEPTH,),
        in_specs=[pl.BlockSpec((N_DEV, D_MODEL), lambda l: (0, 0)),
                  pl.BlockSpec((1, D_MODEL, _ADA_SHARD), lambda l: (l, 0, 0))],
        out_specs=pl.BlockSpec((1, N_DEV, _ADA_SHARD), lambda l: (l, 0, 0)),
        out_shape=jax.ShapeDtypeStruct((DEPTH, N_DEV, _ADA_SHARD), F32), name=name,
        compiler_params=_cparams("parallel"))(c_all, ada_w)


def _ada_wgrad(c_all, dmod_cols, *, name):
    def body(c_ref, d_ref, o_ref):
        o_ref[0] = lax.dot_general(_silu(c_ref[...]), d_ref[0], (((0,), (0,)), ((), ())),
                                   preferred_element_type=F32, precision=lax.Precision.HIGHEST)

    return pl.pallas_call(
        body, grid=(DEPTH,),
        in_specs=[pl.BlockSpec((N_DEV, D_MODEL), lambda l: (0, 0)),
                  pl.BlockSpec((1, N_DEV, _ADA_SHARD), lambda l: (l, 0, 0))],
        out_specs=pl.BlockSpec((1, D_MODEL, _ADA_SHARD), lambda l: (l, 0, 0)),
        out_shape=jax.ShapeDtypeStruct((DEPTH, D_MODEL, _ADA_SHARD), F32), name=name,
        compiler_params=_cparams("parallel"))(c_all, dmod_cols)


def _add_rows(a, b, *, name):
    def body(a_ref, b_ref, o_ref):
        o_ref[...] = a_ref[...] + b_ref[...]

    return pl.pallas_call(body, out_shape=jax.ShapeDtypeStruct(a.shape, a.dtype), name=name)(a, b)


def _adamw_update(w_ref, g_ref, m_ref, v_ref, d_ref, mo_ref, vo_ref):
    gv = g_ref[...]
    mn = ADAM_B1 * m_ref[...] + (1.0 - ADAM_B1) * gv
    vn = ADAM_B2 * v_ref[...] + (1.0 - ADAM_B2) * (gv * gv)
    mo_ref[...] = mn
    vo_ref[...] = vn
    m_hat = mn / (1.0 - ADAM_B1 ** ADAM_STEP)
    v_hat = vn / (1.0 - ADAM_B2 ** ADAM_STEP)
    d_ref[...] = -ADAM_LR * (m_hat / (jnp.sqrt(v_hat) + ADAM_EPS) + ADAM_WD * w_ref[...])


def _adamw_small(ws, gs, ms, vs, *, name):
    n = len(ws)

    def body(*refs):
        ins, outs = refs[:4 * n], refs[4 * n:]
        for i in range(n):
            _adamw_update(ins[i], ins[n + i], ins[2 * n + i], ins[3 * n + i], outs[i], outs[n + i], outs[2 * n + i])

    shapes = [jax.ShapeDtypeStruct(a.shape, F32) for a in ws]
    outs = pl.pallas_call(body, out_shape=shapes * 3, name=name,
                          compiler_params=pltpu.CompilerParams(vmem_limit_bytes=VMEM_LIMIT))(*ws, *gs, *ms, *vs)
    return outs[:n], outs[n:2 * n], outs[2 * n:]


def _adamw(w, g, m, v, *, name, tr):
    r, cdim = w.shape
    body = functools.partial(_adamw_update)

    spec = pl.BlockSpec((tr, cdim), lambda i: (i, 0))
    shp = jax.ShapeDtypeStruct((r, cdim), F32)
    return pl.pallas_call(body, grid=(r // tr,), in_specs=[spec] * 4, out_specs=[spec] * 3, out_shape=[shp] * 3,
                          name=name, compiler_params=_cparams("parallel"))(w, g, m, v)


def _pad_rows(a, rows):
    return jnp.concatenate([a, jnp.zeros((rows - a.shape[0],) + a.shape[1:], a.dtype)], axis=0)


def _pad_lanes(a, lanes):
    return jnp.concatenate([a, jnp.zeros(a.shape[:-1] + (lanes - a.shape[-1],), a.dtype)], axis=-1)


def _permute_w_in(wt):
    return jnp.concatenate([wt[512:1536], wt[:512], wt[1536:1544],
                            jnp.zeros((PROJ_W - IN_W, wt.shape[1]), wt.dtype), wt[1544:]], axis=0)


def _unpermute_w_in(wp):
    return jnp.concatenate([wp[1024:1536], wp[:1024], wp[1536:1544], wp[PROJ_SSD_W:]], axis=0)


def _block_diag(w):
    rows = []
    for g in range(4):
        rows.append(jnp.concatenate([w[g] if k == g else jnp.zeros_like(w[g]) for k in range(4)], axis=1))
    return jnp.concatenate(rows, axis=0)


def _diag_blocks(wbd):
    return jnp.stack([wbd[64 * g:64 * (g + 1), 64 * g:64 * (g + 1)] for g in range(4)], axis=0)


def _layer_params(l, small):
    return dict(
        norm1_g=small["norm1_g"][l][None], norm2_g=small["norm2_g"][l][None],
        conv_w=_pad_rows(small["ssd_conv_w"][l], 8), conv_b=small["ssd_conv_b"][l][None],
        dt_bias=_pad_lanes(small["ssd_dt_bias"][l][None], 128), a_log=_pad_lanes(small["ssd_a_log"][l][None], 128),
        d_skip=_pad_lanes(small["ssd_d"][l][None], 128), ssd_norm_g=small["ssd_norm_g"][l][None],
        pool_bd=_block_diag(small["pool_w"][l]), pool_scale=small["pool_scale"][l][None],
        fcw=_pad_rows(small["ffn_conv_w"][l], 8), fcb=small["ffn_conv_b"][l][None])


def _mod_rows(mod_l):
    return [mod_l[None, D_MODEL * i:D_MODEL * (i + 1)] for i in range(6)]


def _layer_fwd(x, mod_l, p, tabs, l, gather):
    sh1, sc1, g1, sh2, sc2, g2 = _mod_rows(mod_l)
    mix_w = gather(l, "mix", None)
    p.update(w_in=mix_w["w_in"], w_out=mix_w["w_out"])
    proj, h1 = _mm(x, p["w_in"], nt=True, norm=(p["norm1_g"], sc1, sh1), name=f"l{l}_proj")
    ffn_w = gather(l, "ffn", proj)
    p.update(up=ffn_w["ffn_up"], down=ffn_w["ffn_down"])
    mix, hst = _ssd_fwd(proj, p["conv_w"], p["conv_b"], p["dt_bias"], p["a_log"], p["d_skip"], p["ssd_norm_g"],
                        name=f"l{l}_ssd")
    mix = _pool_fwd(proj, p["pool_bd"], p["pool_scale"], mix, name=f"l{l}_pool")
    y_att, lse, mix = _att_fwd(proj, *tabs, mix, name=f"l{l}_att")
    mo, x1 = _mm(mix, p["w_out"], residual=(x, g1), name=f"l{l}_out")
    gather(l + 1, "mix", (x1, p["up"]))
    h2, hid, act, f, x2 = _ffn_fwd(x1, p["norm2_g"], sc2, sh2, g2, p["up"], p["down"], p["fcw"], p["fcb"],
                                   name=f"l{l}_ffn")
    return x2, dict(x=x, h1=h1, proj=proj, hst=hst, y_att=y_att, lse=lse, mix=mix, mo=mo, x1=x1, h2=h2, hid=hid,
                    act=act, f=f)


def _layer_bwd(dx2, sv, mod_l, p, tabs, l, exchange):
    sh1, sc1, g1, sh2, sc2, g2 = _mod_rows(mod_l)
    df, dg2, dhid, dh2, dfcw, dfcb = _ffn_bwd(dx2, g2, sv["f"], sv["hid"], p["up"], p["down"], p["fcw"], p["fcb"],
                                              name=f"l{l}_ffn_b")
    d_down = _wgrad(sv["act"], df, name=f"l{l}_down_bw")
    d_up = _wgrad(dhid, sv["h2"], tk=1408, name=f"l{l}_up_bw")
    exchange(l, "ffn", dict(ffn_up=d_up, ffn_down=d_down))
    dx1, dn2, dsc2, dsh2, dmix, dmo, dg1 = _mid_bwd(sv["x1"], dh2, dx2, p["norm2_g"], sc2, g1, sv["mo"], p["w_out"],
                                                    name=f"l{l}_mid_b")
    d_wout = _wgrad(sv["mix"], dmo, name=f"l{l}_out_bw")
    dproj, dcw, dcb, ddb, dal, dd, dng = _ssd_bwd(
        sv["proj"], sv["hst"], dmix, p["conv_w"], p["conv_b"], p["dt_bias"], p["a_log"], p["d_skip"],
        p["ssd_norm_g"], name=f"l{l}_ssd_b")
    dproj, dwbd, dpsc = _pool_bwd(sv["proj"], dmix, p["pool_bd"], p["pool_scale"], dproj, name=f"l{l}_pool_b")
    dproj = _att_bwd(sv["proj"], *tabs, sv["y_att"], sv["lse"], dmix, dproj, name=f"l{l}_att_b")
    d_win = _wgrad(dproj, sv["h1"], name=f"l{l}_proj_bw")
    exchange(l, "mix", dict(w_in=d_win, w_out=d_wout))
    dx0, dn1, dsc1, dsh1 = _norm_mod_bwd(sv["x"], dproj, dx1, p["norm1_g"], sc1, w=p["w_in"], name=f"l{l}_proj_b")
    dmod = jnp.concatenate([dsh1, dsc1, dg1, dsh2, dsc2, dg2], axis=1)[0]
    small = dict(norm1_g=dn1[0], ssd_conv_w=dcw[:SSD_CONV_K], ssd_conv_b=dcb[0], ssd_dt_bias=ddb[0], ssd_a_log=dal[0],
                 ssd_d=dd[0], ssd_norm_g=dng[0], pool_w=_diag_blocks(dwbd), pool_scale=dpsc[0], norm2_g=dn2[0],
                 ffn_conv_w=dfcw[:FFN_CONV_K], ffn_conv_b=dfcb[0])
    return dx0, dmod, small


def _example_step(x, target, pos_col, inv_freq_lane, mod, gather, small, final_g, exchange):
    tabs = _rope_tables(pos_col, inv_freq_lane, name="rope_tables")
    params, saved = [], []
    for l in range(DEPTH):
        params.append(_layer_params(l, small))
        x, sv = _layer_fwd(x, mod[l], params[l], tabs, l, gather)
        saved.append(sv)
    loss_row, dx, dfg = _final_loss(x, final_g[None], target, name="final_loss")
    dmods, smalls = [None] * DEPTH, [None] * DEPTH
    for l in reversed(range(DEPTH)):
        dx, dmods[l], smalls[l] = _layer_bwd(dx, saved[l], mod[l], params[l], tabs, l, exchange)
    return loss_row, dx, jnp.stack(dmods, axis=0), smalls, dfg[0]


_BIG = ("w_in", "w_out", "ffn_up", "ffn_down")
_SMALL_GRADS = ("norm1_g", "ssd_conv_w", "ssd_conv_b", "ssd_dt_bias", "ssd_a_log", "ssd_d", "ssd_norm_g", "pool_w",
                "pool_scale", "norm2_g", "ffn_conv_w", "ffn_conv_b")
_SMALL_PARAMS = ("ada_b", "norm1_g", "ssd_conv_w", "ssd_conv_b", "ssd_dt_bias", "ssd_a_log", "ssd_d", "ssd_norm_g",
                 "pool_w", "pool_scale", "norm2_g", "ffn_conv_w", "ffn_conv_b", "final_g")
_WEIGHT_ORDER = ("ada_w", "ada_b", "norm1_g", "w_in", "ssd_conv_w", "ssd_conv_b", "ssd_dt_bias", "ssd_a_log", "ssd_d",
                 "ssd_norm_g", "pool_w", "pool_scale", "w_out", "norm2_g", "ffn_up", "ffn_conv_w", "ffn_conv_b",
                 "ffn_down", "final_g")


_COLUMN_SHARDED = ("w_in", "ffn_up")
_GROUPS = (("mix", ("w_in", "w_out")), ("ffn", ("ffn_up", "ffn_down")))


def _big_shares(w, l, names):
    return [(w[name][l].T if name in _COLUMN_SHARDED else w[name][l]).astype(BF16) for name in names]


def _unshard_big(names, gathered):
    out = {}
    for name, g in zip(names, gathered):
        full = g.reshape(N_DEV * g.shape[1], g.shape[2])
        out[name] = _permute_w_in(full) if name == "w_in" else full
    return out


def _shard_big(grads):
    out = []
    for name, g in grads.items():
        g = _unpermute_w_in(g) if name == "w_in" else g
        out.append(g.reshape(N_DEV, g.shape[0] // N_DEV, g.shape[1]))
    return out


def kernel(x, c, positions, ada_w, ada_b, norm1_g, w_in, ssd_conv_w, ssd_conv_b, ssd_dt_bias, ssd_a_log, ssd_d, ssd_norm_g, pool_w, pool_scale, w_out, norm2_g, ffn_up, ffn_conv_w, ffn_conv_b, ffn_down, final_g, loss_target, m_ada_w, m_ada_b, m_norm1_g, m_w_in, m_ssd_conv_w, m_ssd_conv_b, m_ssd_dt_bias, m_ssd_a_log, m_ssd_d, m_ssd_norm_g, m_pool_w, m_pool_scale, m_w_out, m_norm2_g, m_ffn_up, m_ffn_conv_w, m_ffn_conv_b, m_ffn_down, m_final_g, v_ada_w, v_ada_b, v_norm1_g, v_w_in, v_ssd_conv_w, v_ssd_conv_b, v_ssd_dt_bias, v_ssd_a_log, v_ssd_d, v_ssd_norm_g, v_pool_w, v_pool_scale, v_w_out, v_norm2_g, v_ffn_up, v_ffn_conv_w, v_ffn_conv_b, v_ffn_down, v_final_g):
    w = dict(ada_w=ada_w, ada_b=ada_b, norm1_g=norm1_g, w_in=w_in, ssd_conv_w=ssd_conv_w, ssd_conv_b=ssd_conv_b,
             ssd_dt_bias=ssd_dt_bias, ssd_a_log=ssd_a_log, ssd_d=ssd_d, ssd_norm_g=ssd_norm_g, pool_w=pool_w,
             pool_scale=pool_scale, w_out=w_out, norm2_g=norm2_g, ffn_up=ffn_up, ffn_conv_w=ffn_conv_w,
             ffn_conv_b=ffn_conv_b, ffn_down=ffn_down, final_g=final_g)
    m = dict(ada_w=m_ada_w, ada_b=m_ada_b, norm1_g=m_norm1_g, w_in=m_w_in, ssd_conv_w=m_ssd_conv_w,
             ssd_conv_b=m_ssd_conv_b, ssd_dt_bias=m_ssd_dt_bias, ssd_a_log=m_ssd_a_log, ssd_d=m_ssd_d,
             ssd_norm_g=m_ssd_norm_g, pool_w=m_pool_w, pool_scale=m_pool_scale, w_out=m_w_out, norm2_g=m_norm2_g,
             ffn_up=m_ffn_up, ffn_conv_w=m_ffn_conv_w, ffn_conv_b=m_ffn_conv_b, ffn_down=m_ffn_down,
             final_g=m_final_g)
    v = dict(ada_w=v_ada_w, ada_b=v_ada_b, norm1_g=v_norm1_g, w_in=v_w_in, ssd_conv_w=v_ssd_conv_w,
             ssd_conv_b=v_ssd_conv_b, ssd_dt_bias=v_ssd_dt_bias, ssd_a_log=v_ssd_a_log, ssd_d=v_ssd_d,
             ssd_norm_g=v_ssd_norm_g, pool_w=v_pool_w, pool_scale=v_pool_scale, w_out=v_w_out, norm2_g=v_norm2_g,
             ffn_up=v_ffn_up, ffn_conv_w=v_ffn_conv_w, ffn_conv_b=v_ffn_conv_b, ffn_down=v_ffn_down,
             final_g=v_final_g)
    ix, iy, ic = _axes()
    dev = 4 * ix + 2 * iy + ic

    c_all, scw, fcw = _allgather([c, ssd_conv_w.reshape(DEPTH * SSD_CONV_K, -1),
                                  ffn_conv_w.reshape(DEPTH * FFN_CONV_K, -1)], name="gather_small")
    small_all = c_all
    c_all = c_all.reshape(N_DEV, D_MODEL)
    scw = scw.reshape(N_DEV, DEPTH, SSD_CONV_K, -1).transpose(1, 2, 0, 3).reshape(DEPTH, SSD_CONV_K, SSD_CONV_CH)
    fcw = fcw.reshape(N_DEV, DEPTH, FFN_CONV_K, -1).transpose(1, 2, 0, 3).reshape(DEPTH, FFN_CONV_K, 2 * FFN_DIM)

    mod_cols = _ada_mod(c_all, ada_w, name="ada_mod")
    mod_all = _allgather([mod_cols.reshape(DEPTH * N_DEV, _ADA_SHARD)], name="gather_mod")[0]
    mod_all = mod_all.reshape(N_DEV, DEPTH, N_DEV, _ADA_SHARD)
    mod_mine = lax.dynamic_index_in_dim(mod_all, dev, axis=2, keepdims=False)
    mod = _add_rows(mod_mine.transpose(1, 0, 2).reshape(DEPTH, 6 * D_MODEL), ada_b, name="ada_bias")

    fetched = {}

    def gather(l, group, after):
        if l < DEPTH and (l, group) not in fetched:
            names = dict(_GROUPS)[group]
            shares, _ = lax.optimization_barrier((_big_shares(w, l, names), small_all if after is None else after))
            got = _allgather_async(shares, name=f"gather_weights_l{l}_{group}",
                                   collective_id=1 + 2 * l + (group == "ffn"))
            fetched[l, group] = _unshard_big(names, got)
        return fetched.get((l, group))

    core = ic.astype(jnp.int32).reshape(1)
    from_chips = {}

    def exchange(l, group, g):
        cid = 5 + 4 * l + 2 * (group == "mix")
        blocks = _shard_big(g)
        from_sibling = _pair_exchange(blocks, name=f"grads_pair_exchange_l{l}_{group}", collective_id=cid)
        parts = _pair_sum(core, blocks, from_sibling, name=f"grads_pair_sum_l{l}_{group}")
        got = _chip_exchange(parts, name=f"grads_chip_exchange_l{l}_{group}", collective_id=cid + 1)
        from_chips.update({(l, name): t for name, t in zip(g, got)})

    small = dict(norm1_g=norm1_g, norm2_g=norm2_g, ssd_conv_w=scw, ssd_conv_b=ssd_conv_b, ssd_dt_bias=ssd_dt_bias,
                 ssd_a_log=ssd_a_log, ssd_d=ssd_d, ssd_norm_g=ssd_norm_g, pool_w=pool_w, pool_scale=pool_scale,
                 ffn_conv_w=fcw, ffn_conv_b=ffn_conv_b)

    inv_freq = ROPE_THETA ** (-jnp.arange(0, ROT_DIM, 2, dtype=F32) / ROT_DIM)
    lane = jnp.arange(128) % HEAD_LANES
    inv_freq_lane = jnp.where(lane < ROT_DIM, inv_freq[lane % (ROT_DIM // 2)], 0.0)[None, :]
    pos_col = positions.reshape(SEQ, 1).astype(F32)
    loss_row, dx, dmod, g_small, g_final = _example_step(
        x[0], loss_target[0], pos_col, inv_freq_lane, mod, gather, small, final_g, exchange)

    grads = {}
    for name in _BIG:
        per_layer = [_sum_blocks(from_chips[l, name], name=f"grads_chip_sum_l{l}_{name}") for l in range(DEPTH)]
        grads[name] = jnp.stack([g.T if name in _COLUMN_SHARDED else g for g in per_layer], axis=0)

    small_names = list(_SMALL_GRADS)
    stacked = [jnp.stack([g_small[l][name] for l in range(DEPTH)], axis=0) for name in small_names]
    small_parts = [loss_row, dmod] + [s.reshape(-1, s.shape[-1]) for s in stacked] + [g_final[None]]
    gathered = _allgather(small_parts, name="gather_small_grads")
    total = _sum_gathered(gathered, name="sum_small_grads")
    loss = total[0][0, 0]
    grads["ada_b"] = total[1]
    grads.update(zip(small_names, total[2:-1]))
    grads["final_g"] = total[-1][0]
    dmod_cols = lax.dynamic_slice_in_dim(gathered[1], dev * _ADA_SHARD, _ADA_SHARD, axis=2).transpose(1, 0, 2)
    grads["ada_w"] = _ada_wgrad(c_all, dmod_cols, name="ada_wgrad")
    for name in ("ssd_dt_bias", "ssd_a_log", "ssd_d"):
        grads[name] = grads[name][:, :SSD_HEADS]
    grads["pool_w"] = grads["pool_w"].reshape(pool_w.shape)
    grads["ssd_conv_w"] = lax.dynamic_slice_in_dim(
        grads["ssd_conv_w"].reshape(DEPTH, SSD_CONV_K, SSD_CONV_CH), dev * ssd_conv_w.shape[2], ssd_conv_w.shape[2], axis=2)
    grads["ffn_conv_w"] = lax.dynamic_slice_in_dim(
        grads["ffn_conv_w"].reshape(DEPTH, FFN_CONV_K, 2 * FFN_DIM), dev * ffn_conv_w.shape[2], ffn_conv_w.shape[2], axis=2)

    delta, new_m, new_v = {}, {}, {}
    for name, tr in (("ada_w", 512), ("w_in", 512), ("w_out", 256), ("ffn_up", 512), ("ffn_down", 352)):
        shp = w[name].shape
        two_d = lambda a: a.reshape(shp[0] * shp[1], shp[2])
        d_, m_, v_ = _adamw(two_d(w[name]), two_d(grads[name]), two_d(m[name]), two_d(v[name]), tr=tr,
                               name=f"adamw_{name}")
        delta[name], new_m[name], new_v[name] = (t.reshape(shp) for t in (d_, m_, v_))
    two_d = lambda a: a.reshape(-1, a.shape[-1])
    outs = _adamw_small(*[[two_d(t[name]) for name in _SMALL_PARAMS] for t in (w, grads, m, v)], name="adamw_small")
    for name, d_, m_, v_ in zip(_SMALL_PARAMS, *outs):
        delta[name], new_m[name], new_v[name] = (t.reshape(w[name].shape) for t in (d_, m_, v_))

    grad_x = dx[None]
    return (loss, grad_x, *[grads[n].reshape(w[n].shape) for n in _WEIGHT_ORDER],
            *[delta[n] for n in _WEIGHT_ORDER], *[new_m[n] for n in _WEIGHT_ORDER],
            *[new_v[n] for n in _WEIGHT_ORDER])
```

```python
import functools
import math

import jax
import jax.numpy as jnp
from jax import lax
from jax.experimental import pallas as pl
from jax.experimental.pallas import tpu as pltpu
from jax.experimental.pallas import tpu_sc as plsc

F32 = jnp.float32
BF16 = jnp.bfloat16

N_DEV = 8
D_MODEL = 1024
SEQ = 4096
DEPTH = 2
SSD_INNER = 512
SSD_HEADS = 8
SSD_HPG = 4
SSD_STATE = 128
SSD_CHUNK = 256
SSD_CONV_K = 4
SSD_CONV_CH = 1024
POOL_W = 256
POOL_WINDOWS = (2, 4, 8, 16)
ATT_W = 256
ATT_PATTERNS = ((128, 1), (512, 4), (2048, 16))
ATT_BLOCK = 128
ROT_DIM = 16
ROPE_THETA = 500000.0
IN_W = 2568
FFN_DIM = 2816
FFN_CONV_K = 3
NORM_EPS = 1e-6
HEAD_LANES = 64

ADAM_LR = 0.001
ADAM_B1 = 0.9
ADAM_B2 = 0.999
ADAM_EPS = 1e-08
ADAM_WD = 0.01
ADAM_STEP = 10

PROJ_W = 2816
PROJ_SSD_W = 1792
PROJ_Z_BLK = 2
PROJ_DT_BLK = 12
PROJ_POOL_BLK = 7
PROJ_Q_BLK, PROJ_K_BLK, PROJ_V_BLK = 16, 18, 20
MIX_POOL_BLK = 2
MIX_ATT_BLK = 6
VMEM_LIMIT = 56 * 1024 * 1024
CONV_HALO = 8
POOL_HALO = 16
ATT_KPAD = ATT_BLOCK * 16
MESH = pl.DeviceIdType.MESH
_HBM = pl.BlockSpec(memory_space=pl.ANY)


def _cparams(*sem):
    return pltpu.CompilerParams(dimension_semantics=sem, vmem_limit_bytes=VMEM_LIMIT)


def _resident(shape):
    return pl.BlockSpec(shape, lambda i: (0,) * len(shape), pipeline_mode=pl.Buffered(1))


def _silu(x):
    return x * jax.nn.sigmoid(x)


def _pick_lane(v, h):
    lane = lax.broadcasted_iota(jnp.int32, v.shape, 1)
    return jnp.sum(jnp.where(lane == h, v, 0.0), axis=1, keepdims=True)


def _pick_row(v, h):
    row = lax.broadcasted_iota(jnp.int32, v.shape, 0)
    return jnp.sum(jnp.where(row == h, v, 0.0), axis=0, keepdims=True)


def _head_of_lane(width):
    return lax.broadcasted_iota(jnp.int32, (1, width), 1) // HEAD_LANES


@functools.partial(jax.custom_vjp, nondiff_argnums=(1, 2))
def _shift_rows(x_ext, s, halo):
    y = x_ext if s == 0 else pltpu.roll(x_ext, s, 0)
    return y[halo:]


def _shift_rows_fwd(x_ext, s, halo):
    return _shift_rows(x_ext, s, halo), None


def _shift_rows_bwd(s, halo, _, g):
    ge = jnp.concatenate([jnp.zeros((halo, g.shape[1]), g.dtype), g], axis=0)
    return (ge if s == 0 else pltpu.roll(ge, ge.shape[0] - s, 0),)


_shift_rows.defvjp(_shift_rows_fwd, _shift_rows_bwd)


@functools.partial(jax.custom_vjp, nondiff_argnums=(1,))
def _roll_rows(x, s):
    return pltpu.roll(x, s, 0)


def _roll_rows_fwd(x, s):
    return _roll_rows(x, s), None


def _roll_rows_bwd(s, _, g):
    return (pltpu.roll(g, g.shape[0] - s, 0),)


_roll_rows.defvjp(_roll_rows_fwd, _roll_rows_bwd)


def _rms_modulate(xv, g, sc, sh):
    r = lax.rsqrt(jnp.mean(xv * xv, axis=-1, keepdims=True) + NORM_EPS)
    return (xv * r * g) * (1.0 + sc) + sh


def _mm(a, w, *, name, nt=False, tm=512, tn=None, out_dtype=F32, norm=None, residual=None):
    t, k = a.shape
    n = w.shape[0] if nt else w.shape[1]
    tn = tn or n
    assert tn == n or (norm is None and residual is None)
    extra_in = list(norm or ()) + list(residual or ())

    def body(*refs):
        a_ref, w_ref = refs[:2]
        ins = refs[2:2 + len(extra_in)]
        outs = refs[2 + len(extra_in):]
        if norm is None:
            av = a_ref[...].astype(BF16)
        else:
            av = _rms_modulate(a_ref[...], ins[0][...], ins[1][...], ins[2][...]).astype(BF16)
            outs[1][...] = av
        if nt:
            acc = lax.dot_general(av, w_ref[...], (((1,), (1,)), ((), ())), preferred_element_type=F32)
        else:
            acc = jnp.dot(av, w_ref[...], preferred_element_type=F32)
        outs[0][...] = acc.astype(out_dtype)
        if residual is not None:
            x_ref, gate_ref = ins[-2:]
            outs[-1][...] = x_ref[...] + gate_ref[...] * acc

    row = lambda width: pl.BlockSpec((1, width), lambda i, j: (0, 0))
    tile = lambda width: pl.BlockSpec((tm, width), lambda i, j: (i, 0))
    w_spec = pl.BlockSpec((tn, k), lambda i, j: (j, 0)) if nt else pl.BlockSpec((k, tn), lambda i, j: (0, j))
    in_specs = [tile(k), w_spec] + ([row(k)] * 3 if norm else []) + ([tile(n), row(n)] if residual else [])
    out_specs = [pl.BlockSpec((tm, tn), lambda i, j: (i, j))] + ([tile(k)] if norm else []) + \
        ([tile(n)] if residual else [])
    out_shape = [jax.ShapeDtypeStruct((t, n), out_dtype)] + \
        ([jax.ShapeDtypeStruct((t, k), BF16)] if norm else []) + \
        ([jax.ShapeDtypeStruct((t, n), F32)] if residual else [])
    outs = pl.pallas_call(
        body, grid=(t // tm, n // tn), in_specs=in_specs, out_specs=out_specs, out_shape=out_shape, name=name,
        compiler_params=_cparams("parallel", "parallel"))(a, w, *extra_in)
    return outs[0] if len(outs) == 1 else outs


def _wgrad(a, b, *, name, tk=None, tn=None, tt=2048, out_dtype=BF16):
    t, k = a.shape
    n = b.shape[1]
    tk = tk or k
    tn = tn or n
    steps = t // tt

    def body(a_ref, b_ref, o_ref, acc_ref):
        s = pl.program_id(2)

        @pl.when(s == 0)
        def _():
            acc_ref[...] = jnp.zeros_like(acc_ref)

        acc_ref[...] += lax.dot_general(a_ref[...].astype(BF16), b_ref[...].astype(BF16),
                                        (((0,), (0,)), ((), ())), preferred_element_type=F32)

        @pl.when(s == steps - 1)
        def _():
            o_ref[...] = acc_ref[...].astype(out_dtype)

    return pl.pallas_call(
        body, grid=(k // tk, n // tn, steps),
        in_specs=[pl.BlockSpec((tt, tk), lambda i, j, s: (s, i)), pl.BlockSpec((tt, tn), lambda i, j, s: (s, j))],
        out_specs=pl.BlockSpec((tk, tn), lambda i, j, s: (i, j)),
        out_shape=jax.ShapeDtypeStruct((k, n), out_dtype),
        scratch_shapes=[pltpu.VMEM((tk, tn), F32)], name=name,
        compiler_params=_cparams("parallel", "parallel", "arbitrary"))(a, b)


def _norm_mod_bwd(x, dh, dres, g, sc, *, name, w=None, tm=512):
    s, d = x.shape
    steps = s // tm

    def body(x_ref, dh_ref, dres_ref, g_ref, sc_ref, *rest):
        w_ref = rest[0] if w is not None else None
        dx_ref, dg_ref, dsc_ref, dsh_ref, da_acc, dsh_acc = rest[-6:]
        i = pl.program_id(0)

        @pl.when(i == 0)
        def _():
            da_acc[...] = jnp.zeros_like(da_acc)
            dsh_acc[...] = jnp.zeros_like(dsh_acc)

        xv = x_ref[...]
        if w is None:
            dhv = dh_ref[...].astype(F32)
        else:
            dhv = jnp.dot(dh_ref[...], w_ref[...], preferred_element_type=F32)
        r = lax.rsqrt(jnp.mean(xv * xv, axis=-1, keepdims=True) + NORM_EPS)
        xhat = xv * r
        gain = g_ref[...] * (1.0 + sc_ref[...])
        dxhat = dhv * gain
        dx_ref[...] = dres_ref[...] + r * (dxhat - xhat * jnp.mean(dxhat * xhat, axis=-1, keepdims=True))
        da_acc[...] += jnp.sum(dhv * xhat, axis=0, keepdims=True)
        dsh_acc[...] += jnp.sum(dhv, axis=0, keepdims=True)

        @pl.when(i == steps - 1)
        def _():
            dg_ref[...] = da_acc[...] * (1.0 + sc_ref[...])
            dsc_ref[...] = da_acc[...] * g_ref[...]
            dsh_ref[...] = dsh_acc[...]

    row = pl.BlockSpec((1, d), lambda i: (0, 0))
    tile = pl.BlockSpec((tm, d), lambda i: (i, 0))
    row_shape = jax.ShapeDtypeStruct((1, d), F32)
    dh_spec = tile if w is None else pl.BlockSpec((tm, dh.shape[1]), lambda i: (i, 0))
    return pl.pallas_call(
        body, grid=(steps,), in_specs=[tile, dh_spec, tile, row, row] + ([] if w is None else [_resident(w.shape)]),
        out_specs=[tile, row, row, row],
        out_shape=[jax.ShapeDtypeStruct((s, d), F32), row_shape, row_shape, row_shape],
        scratch_shapes=[pltpu.VMEM((1, d), F32), pltpu.VMEM((1, d), F32)], name=name,
        compiler_params=_cparams("arbitrary"))(x, dh, dres, g, sc, *([] if w is None else [w]))


def _mid_bwd(x1, dh2, dx2, norm_g, sc, gate, mo, w_out, *, name, tm=512):
    s, d = x1.shape
    steps = s // tm

    def body(x_ref, dh_ref, dres_ref, ng_ref, sc_ref, g_ref, mo_ref, w_ref,
             dx_ref, dng_ref, dsc_ref, dsh_ref, dmix_ref, dmo_ref, dg_ref, da_acc, dsh_acc):
        i = pl.program_id(0)

        @pl.when(i == 0)
        def _():
            da_acc[...] = jnp.zeros_like(da_acc)
            dsh_acc[...] = jnp.zeros_like(dsh_acc)
            dg_ref[...] = jnp.zeros_like(dg_ref)

        xv = x_ref[...]
        dhv = dh_ref[...]
        r = lax.rsqrt(jnp.mean(xv * xv, axis=-1, keepdims=True) + NORM_EPS)
        xhat = xv * r
        dxhat = dhv * (ng_ref[...] * (1.0 + sc_ref[...]))
        dxv = dres_ref[...] + r * (dxhat - xhat * jnp.mean(dxhat * xhat, axis=-1, keepdims=True))
        dx_ref[...] = dxv
        da_acc[...] += jnp.sum(dhv * xhat, axis=0, keepdims=True)
        dsh_acc[...] += jnp.sum(dhv, axis=0, keepdims=True)
        dmo = (g_ref[...] * dxv).astype(BF16)
        dmo_ref[...] = dmo
        dg_ref[...] += jnp.sum(dxv * mo_ref[...], axis=0, keepdims=True)
        dmix_ref[...] = lax.dot_general(dmo, w_ref[...], (((1,), (1,)), ((), ())), preferred_element_type=F32)

        @pl.when(i == steps - 1)
        def _():
            dng_ref[...] = da_acc[...] * (1.0 + sc_ref[...])
            dsc_ref[...] = da_acc[...] * ng_ref[...]
            dsh_ref[...] = dsh_acc[...]

    tile = pl.BlockSpec((tm, d), lambda i: (i, 0))
    row = pl.BlockSpec((1, d), lambda i: (0, 0))
    mix_tile = pl.BlockSpec((tm, w_out.shape[0]), lambda i: (i, 0))
    row_shape = jax.ShapeDtypeStruct((1, d), F32)
    return pl.pallas_call(
        body, grid=(steps,), in_specs=[tile, tile, tile, row, row, row, tile, _resident(w_out.shape)],
        out_specs=[tile, row, row, row, mix_tile, tile, row],
        out_shape=[jax.ShapeDtypeStruct((s, d), F32), row_shape, row_shape, row_shape,
                   jax.ShapeDtypeStruct((s, w_out.shape[0]), F32), jax.ShapeDtypeStruct((s, d), BF16), row_shape],
        scratch_shapes=[pltpu.VMEM((1, d), F32), pltpu.VMEM((1, d), F32)], name=name,
        compiler_params=_cparams("arbitrary"))(x1, dh2, dx2, norm_g, sc, gate, mo, w_out)


def _final_loss(x, g, target, *, name, tm=512):
    s, d = x.shape
    steps = s // tm

    def body(x_ref, g_ref, t_ref, loss_ref, dx_ref, dg_ref, sq_acc):
        i = pl.program_id(0)

        @pl.when(i == 0)
        def _():
            sq_acc[...] = jnp.zeros_like(sq_acc)
            dg_ref[...] = jnp.zeros_like(dg_ref)

        xv = x_ref[...]
        r = lax.rsqrt(jnp.mean(xv * xv, axis=-1, keepdims=True) + NORM_EPS)
        xhat = xv * r
        err = xhat * g_ref[...] - t_ref[...]
        sq_acc[...] += jnp.sum(err * err, axis=0, keepdims=True)
        dy = err * (1.0 / d)
        dg_ref[...] += jnp.sum(dy * xhat, axis=0, keepdims=True)
        dxhat = dy * g_ref[...]
        dx_ref[...] = r * (dxhat - xhat * jnp.mean(dxhat * xhat, axis=-1, keepdims=True))

        @pl.when(i == steps - 1)
        def _():
            total = jnp.sum(sq_acc[...], axis=1, keepdims=True) * (0.5 / d)
            loss_ref[...] = jnp.broadcast_to(total, loss_ref.shape)

    tile = pl.BlockSpec((tm, d), lambda i: (i, 0))
    row = pl.BlockSpec((1, d), lambda i: (0, 0))
    return pl.pallas_call(
        body, grid=(steps,), in_specs=[tile, row, tile],
        out_specs=[pl.BlockSpec((1, 128), lambda i: (0, 0)), tile, row],
        out_shape=[jax.ShapeDtypeStruct((1, 128), F32), jax.ShapeDtypeStruct((s, d), F32),
                   jax.ShapeDtypeStruct((1, d), F32)],
        scratch_shapes=[pltpu.VMEM((1, d), F32)], name=name, compiler_params=_cparams("arbitrary"))(x, g, target)


def _ssd_chunk(z, xbc_ext, dt_raw, conv_w, conv_b, dt_bias, a_log, d_skip, norm_g, h_in):
    q = z.shape[0]
    gw = SSD_HPG * HEAD_LANES
    xc = conv_b
    for k in range(SSD_CONV_K):
        xc = xc + _pick_row(conv_w, k) * _shift_rows(xbc_ext, SSD_CONV_K - 1 - k, CONV_HALO)
    xc = _silu(xc)
    dt = jax.nn.softplus(dt_raw + dt_bias)
    da = dt * (-jnp.exp(a_log))
    ri = lax.broadcasted_iota(jnp.int32, (q, q), 0)
    ci = lax.broadcasted_iota(jnp.int32, (q, q), 1)
    causal = ri >= ci
    tril = causal.astype(F32)
    a_cum = jnp.dot(tril, da, preferred_element_type=F32, precision=lax.Precision.HIGHEST)
    a_cum_t = lax.dot_general(da, tril, (((0,), (1,)), ((), ())), preferred_element_type=F32,
                              precision=lax.Precision.HIGHEST)
    a_last = _pick_row(a_cum, q - 1)
    head = _head_of_lane(gw)
    ys, hs = [], []
    for g in range(2):
        xs = xc[:, gw * g:gw * (g + 1)]
        bm = xc[:, SSD_INNER + SSD_STATE * g:SSD_INNER + SSD_STATE * (g + 1)]
        cm = xc[:, SSD_INNER + 2 * SSD_STATE + SSD_STATE * g:SSD_INNER + 2 * SSD_STATE + SSD_STATE * (g + 1)]
        cb = lax.dot_general(cm.astype(BF16), bm.astype(BF16), (((1,), (1,)), ((), ())), preferred_element_type=F32)
        cols = [_pick_lane(a_cum, SSD_HPG * g + j) for j in range(SSD_HPG)]
        lasts = [_pick_lane(a_last, SSD_HPG * g + j) for j in range(SSD_HPG)]
        dt_exp = sum(jnp.where(head == j, _pick_lane(dt, SSD_HPG * g + j), 0.0) for j in range(SSD_HPG))
        d_exp = sum(jnp.where(head == j, _pick_lane(d_skip, SSD_HPG * g + j), 0.0) for j in range(SSD_HPG))
        e_cum = sum(jnp.where(head == j, jnp.exp(cols[j]), 0.0) for j in range(SSD_HPG))
        c_dec = sum(jnp.where(head == j, jnp.exp(lasts[j]), 0.0) for j in range(SSD_HPG))
        xsdt = (xs * dt_exp).astype(BF16)
        y_diag = jnp.zeros((q, gw), F32)
        st_new = jnp.zeros((SSD_STATE, gw), F32)
        for j in range(SSD_HPG):
            row = _pick_row(a_cum_t, SSD_HPG * g + j)
            lmat = jnp.exp(jnp.where(causal, cols[j] - row, -jnp.inf))
            r = jnp.dot((cb * lmat).astype(BF16), xsdt, preferred_element_type=F32)
            y_diag = y_diag + jnp.where(head == j, r, 0.0)
            bd = (bm * jnp.exp(lasts[j] - cols[j])).astype(BF16)
            st = lax.dot_general(bd, xsdt, (((0,), (0,)), ((), ())), preferred_element_type=F32)
            st_new = st_new + jnp.where(head == j, st, 0.0)
        y_off = jnp.dot(cm.astype(BF16), h_in[g].astype(BF16), preferred_element_type=F32) * e_cum
        hs.append(h_in[g] * c_dec + st_new)
        y = y_diag + y_off + d_exp * xs
        yz = y * _silu(z[:, gw * g:gw * (g + 1)])
        yz = yz * lax.rsqrt(jnp.mean(yz * yz, axis=-1, keepdims=True) + NORM_EPS)
        ys.append(yz * norm_g[:, gw * g:gw * (g + 1)])
    return jnp.concatenate(ys, axis=1), tuple(hs)


_SSD_NCHUNK = SEQ // SSD_CHUNK
_HALO_PER_CHUNK = SSD_CHUNK // CONV_HALO


def _ssd_param_specs(const):
    return [pl.BlockSpec((8, SSD_CONV_CH), const), pl.BlockSpec((1, SSD_CONV_CH), const),
            pl.BlockSpec((1, 128), const), pl.BlockSpec((1, 128), const), pl.BlockSpec((1, 128), const),
            pl.BlockSpec((1, SSD_INNER), const)]


def _ssd_fwd(proj, conv_w, conv_b, dt_bias, a_log, d_skip, norm_g, *, name):
    q = SSD_CHUNK

    def body(z_ref, xbc_ref, halo_ref, dt_ref, cw_ref, cb_ref, db_ref, al_ref, d_ref, ng_ref, y_ref, hs_ref, h_acc):
        i = pl.program_id(0)

        @pl.when(i == 0)
        def _():
            h_acc[...] = jnp.zeros_like(h_acc)

        halo = jnp.where(i == 0, 0.0, halo_ref[...])
        xbc_ext = jnp.concatenate([halo, xbc_ref[...]], axis=0)
        h_in = (h_acc[0], h_acc[1])
        hs_ref[0, 0] = h_in[0]
        hs_ref[0, 1] = h_in[1]
        y, h_out = _ssd_chunk(z_ref[...], xbc_ext, dt_ref[...], cw_ref[...], cb_ref[...], db_ref[...], al_ref[...],
                              d_ref[...], ng_ref[...], h_in)
        y_ref[...] = y.astype(BF16)
        h_acc[0] = h_out[0]
        h_acc[1] = h_out[1]

    const = lambda i: (0, 0)
    return pl.pallas_call(
        body, grid=(_SSD_NCHUNK,),
        in_specs=[pl.BlockSpec((q, SSD_INNER), lambda i: (i, PROJ_Z_BLK)),
                  pl.BlockSpec((q, SSD_CONV_CH), lambda i: (i, 0)),
                  pl.BlockSpec((CONV_HALO, SSD_CONV_CH), lambda i: (jnp.maximum(i * _HALO_PER_CHUNK - 1, 0), 0)),
                  pl.BlockSpec((q, 128), lambda i: (i, PROJ_DT_BLK))] + _ssd_param_specs(const),
        out_specs=[pl.BlockSpec((q, SSD_INNER), lambda i: (i, 0)),
                   pl.BlockSpec((1, 2, SSD_STATE, 256), lambda i: (i, 0, 0, 0))],
        out_shape=[jax.ShapeDtypeStruct((SEQ, D_MODEL), BF16),
                   jax.ShapeDtypeStruct((_SSD_NCHUNK, 2, SSD_STATE, 256), F32)],
        scratch_shapes=[pltpu.VMEM((2, SSD_STATE, 256), F32)], name=name,
        compiler_params=_cparams("arbitrary"))(proj, proj, proj, proj, conv_w, conv_b, dt_bias, a_log, d_skip, norm_g)


def _ssd_bwd(proj, hstates, dmix, conv_w, conv_b, dt_bias, a_log, d_skip, norm_g, *, name):
    q = SSD_CHUNK
    last = _SSD_NCHUNK - 1

    def body(z_ref, xbc_ref, halo_ref, dt_ref, hs_ref, dy_ref, cw_ref, cb_ref, db_ref, al_ref, d_ref, ng_ref,
             dp_ref, dcw_ref, dcb_ref, ddb_ref, dal_ref, dd_ref, dng_ref, dh_acc, dhalo_acc):
        i = pl.program_id(0)

        @pl.when(i == 0)
        def _():
            dh_acc[...] = jnp.zeros_like(dh_acc)
            dhalo_acc[...] = jnp.zeros_like(dhalo_acc)
            for r in (dcw_ref, dcb_ref, ddb_ref, dal_ref, dd_ref, dng_ref):
                r[...] = jnp.zeros_like(r)

        halo = jnp.where(i == last, 0.0, halo_ref[...])
        xbc_ext = jnp.concatenate([halo, xbc_ref[...]], axis=0)
        _, vjp = jax.vjp(_ssd_chunk, z_ref[...], xbc_ext, dt_ref[...], cw_ref[...], cb_ref[...], db_ref[...],
                         al_ref[...], d_ref[...], ng_ref[...], (hs_ref[0, 0], hs_ref[0, 1]))
        gz, gx, gdt, gcw, gcb, gdb, gal, gd, gng, gh = vjp((dy_ref[...], (dh_acc[0], dh_acc[1])))
        dxbc = jnp.concatenate([gx[CONV_HALO:q], gx[q:] + dhalo_acc[...]], axis=0)
        dp_ref[...] = jnp.concatenate([dxbc, gz, gdt, jnp.zeros_like(gdt)], axis=1).astype(BF16)
        dhalo_acc[...] = gx[:CONV_HALO]
        dh_acc[0] = gh[0]
        dh_acc[1] = gh[1]
        dcw_ref[...] += gcw
        dcb_ref[...] += gcb
        ddb_ref[...] += gdb
        dal_ref[...] += gal
        dd_ref[...] += gd
        dng_ref[...] += gng

    const = lambda i: (0, 0)
    rev = lambda i: last - i
    row = lambda n: jax.ShapeDtypeStruct((1, n), F32)
    return pl.pallas_call(
        body, grid=(_SSD_NCHUNK,),
        in_specs=[pl.BlockSpec((q, SSD_INNER), lambda i: (rev(i), PROJ_Z_BLK)),
                  pl.BlockSpec((q, SSD_CONV_CH), lambda i: (rev(i), 0)),
                  pl.BlockSpec((CONV_HALO, SSD_CONV_CH), lambda i: (jnp.maximum(rev(i) * _HALO_PER_CHUNK - 1, 0), 0)),
                  pl.BlockSpec((q, 128), lambda i: (rev(i), PROJ_DT_BLK)),
                  pl.BlockSpec((1, 2, SSD_STATE, 256), lambda i: (rev(i), 0, 0, 0)),
                  pl.BlockSpec((q, SSD_INNER), lambda i: (rev(i), 0))] + _ssd_param_specs(const),
        out_specs=[pl.BlockSpec((q, PROJ_SSD_W), lambda i: (rev(i), 0))] + _ssd_param_specs(const),
        out_shape=[jax.ShapeDtypeStruct((SEQ, PROJ_W), BF16), jax.ShapeDtypeStruct((8, SSD_CONV_CH), F32),
                   row(SSD_CONV_CH), row(128), row(128), row(128), row(SSD_INNER)],
        scratch_shapes=[pltpu.VMEM((2, SSD_STATE, 256), F32), pltpu.VMEM((CONV_HALO, SSD_CONV_CH), F32)], name=name,
        compiler_params=_cparams("arbitrary"))(proj, proj, proj, proj, hstates, dmix, conv_w, conv_b, dt_bias, a_log,
                                                d_skip, norm_g)


def _rope_tables(pos_col, inv_freq_lane, *, name):
    s = pos_col.shape[0]

    def body(p_ref, f_ref, c_ref, s1_ref, s2_ref):
        ang = p_ref[...] * f_ref[...]
        within = lax.broadcasted_iota(jnp.int32, ang.shape, 1) % HEAD_LANES
        half = ROT_DIM // 2
        c_ref[...] = jnp.where(within < ROT_DIM, jnp.cos(ang), 1.0)
        sn = jnp.sin(ang)
        s1_ref[...] = jnp.where(within < half, -sn, 0.0)
        s2_ref[...] = jnp.where((within >= half) & (within < ROT_DIM), sn, 0.0)

    shp = jax.ShapeDtypeStruct((s, 128), F32)
    return pl.pallas_call(body, out_shape=[shp, shp, shp], name=name,
                          compiler_params=pltpu.CompilerParams(vmem_limit_bytes=VMEM_LIMIT))(pos_col, inv_freq_lane)


def _rope(t, c, s1, s2):
    half = ROT_DIM // 2
    return t * c + pltpu.roll(t, 128 - half, 1) * s1 + pltpu.roll(t, half, 1) * s2


def _rope_t(g, c, s1, s2):
    half = ROT_DIM // 2
    return g * c + pltpu.roll(g * s1, half, 1) + pltpu.roll(g * s2, 128 - half, 1)


def _att_valid(b):
    qi = lax.broadcasted_iota(jnp.int32, (ATT_BLOCK, 2 * ATT_BLOCK), 0)
    kj = lax.broadcasted_iota(jnp.int32, (ATT_BLOCK, 2 * ATT_BLOCK), 1)
    rel = qi + ATT_BLOCK - kj
    return (rel >= 0) & (rel <= ATT_BLOCK) & (b * ATT_BLOCK + kj - ATT_BLOCK >= 0)


def _att_slices(i, d):
    if d == 1:
        qstart = pl.multiple_of(i * ATT_BLOCK, ATT_BLOCK)
        return i, pl.ds(qstart, ATT_BLOCK), pl.ds(pl.multiple_of(qstart - ATT_BLOCK + ATT_KPAD, ATT_BLOCK), 2 * ATT_BLOCK)
    r = i % d
    b = i // d
    qstart = r + d * ATT_BLOCK * b
    return b, pl.ds(qstart, ATT_BLOCK, stride=d), pl.ds(qstart - ATT_BLOCK * d + ATT_KPAD, 2 * ATT_BLOCK, stride=d)


_ATT_NBLK = SEQ // ATT_BLOCK
_ATT_SCALE = HEAD_LANES ** -0.5
_ATT_UNROLL_FWD = 8
_ATT_UNROLL = 4


def _att_fwd(proj, cos, sin1, sin2, mix, *, name):
    s = SEQ

    def body(q_ref, k_ref, v_ref, c_ref, s1_ref, s2_ref, _, o_ref, lse_ref, mix_ref, qs, ks, vs, acc, m_s, l_s):
        c, s1, s2 = c_ref[...], s1_ref[...], s2_ref[...]
        qs[...] = _rope(q_ref[...], c, s1, s2) * _ATT_SCALE
        zeros = jnp.zeros((ATT_KPAD, 128), F32)
        ks[pl.ds(0, ATT_KPAD), :] = zeros
        vs[pl.ds(0, ATT_KPAD), :] = zeros
        ks[pl.ds(ATT_KPAD, s), :] = _rope(k_ref[...], c, s1, s2)
        vs[pl.ds(ATT_KPAD, s), :] = v_ref[...]
        head0 = _head_of_lane(128) == 0

        for bi, (_, d) in enumerate(ATT_PATTERNS):
            def blk(i, carry, d=d, first=(bi == 0)):
                b, sq, sk = _att_slices(i, d)
                qb = qs[sq, :]
                kw = ks[sk, :].astype(BF16)
                vw = vs[sk, :].astype(BF16)
                valid = _att_valid(b)
                ms, ls, os_ = [], [], []
                for hh in range(2):
                    qh = jnp.where(head0 if hh == 0 else ~head0, qb, 0.0).astype(BF16)
                    sc = lax.dot_general(qh, kw, (((1,), (1,)), ((), ())), preferred_element_type=F32)
                    sc = jnp.where(valid, sc, -jnp.inf)
                    mb = jnp.max(sc, axis=1, keepdims=True)
                    p = jnp.exp(sc - mb)
                    ms.append(mb)
                    ls.append(jnp.sum(p, axis=1, keepdims=True))
                    os_.append(jnp.dot(p.astype(BF16), vw, preferred_element_type=F32))
                m_b = jnp.where(head0, ms[0], ms[1])
                l_b = jnp.where(head0, ls[0], ls[1])
                o_b = jnp.where(head0, os_[0], os_[1])
                if first:
                    m_s[sq, :] = m_b
                    l_s[sq, :] = l_b
                    acc[sq, :] = o_b
                else:
                    m_old = m_s[sq, :]
                    m_new = jnp.maximum(m_old, m_b)
                    a_old = jnp.exp(m_old - m_new)
                    a_b = jnp.exp(m_b - m_new)
                    m_s[sq, :] = m_new
                    l_s[sq, :] = l_s[sq, :] * a_old + l_b * a_b
                    acc[sq, :] = acc[sq, :] * a_old + o_b * a_b
                return carry

            lax.fori_loop(0, _ATT_NBLK, blk, 0, unroll=_ATT_UNROLL_FWD)

        out = acc[...] / l_s[...]
        o_ref[...] = out
        mix_ref[...] = out.astype(BF16)
        lse_ref[...] = m_s[...] + jnp.log(l_s[...])

    col = lambda base: pl.BlockSpec((s, 128), lambda p: (0, base + p))
    tab = pl.BlockSpec((s, 128), lambda p: (0, 0))
    big = pltpu.VMEM((ATT_KPAD + s, 128), F32)
    tok = pltpu.VMEM((s, 128), F32)
    return pl.pallas_call(
        body, grid=(2,), in_specs=[col(PROJ_Q_BLK), col(PROJ_K_BLK), col(PROJ_V_BLK), tab, tab, tab, _HBM],
        out_specs=[pl.BlockSpec((s, 128), lambda p: (0, p)), pl.BlockSpec((s, 128), lambda p: (0, p)),
                   col(MIX_ATT_BLK)],
        out_shape=[jax.ShapeDtypeStruct((s, ATT_W), F32), jax.ShapeDtypeStruct((s, ATT_W), F32),
                   jax.ShapeDtypeStruct(mix.shape, mix.dtype)],
        input_output_aliases={6: 2}, scratch_shapes=[tok, big, big, tok, tok, tok], name=name,
        compiler_params=_cparams("arbitrary"))(proj, proj, proj, cos, sin1, sin2, mix)


def _att_bwd(proj, cos, sin1, sin2, out, lse, dmix, dproj, *, name):
    s = SEQ

    def body(proj_ref, c_hbm, s1_hbm, s2_hbm, out_hbm, lse_hbm, dmix_hbm, _, dproj_hbm,
             c_ref, s1_ref, s2_ref, o_ref, lse_ref, do_ref, qs, ks, vs, dqs, dks, dvs, staged, sems):
        def start(copies):
            for cp in copies:
                cp.start()
            return copies

        def load(pair):
            lanes = pl.ds(128 * pair, 128)
            rows = pl.ds(ATT_KPAD, s)
            return start([
                pltpu.make_async_copy(proj_ref.at[:, pl.ds(128 * (PROJ_Q_BLK + pair), 128)], qs, sems.at[0]),
                pltpu.make_async_copy(proj_ref.at[:, pl.ds(128 * (PROJ_K_BLK + pair), 128)], ks.at[rows, :], sems.at[1]),
                pltpu.make_async_copy(proj_ref.at[:, pl.ds(128 * (PROJ_V_BLK + pair), 128)], vs.at[rows, :], sems.at[2]),
                pltpu.make_async_copy(out_hbm.at[:, lanes], o_ref, sems.at[3]),
                pltpu.make_async_copy(lse_hbm.at[:, lanes], lse_ref, sems.at[4]),
                pltpu.make_async_copy(dmix_hbm.at[:, pl.ds(128 * (MIX_ATT_BLK + pair), 128)], do_ref, sems.at[5])])

        tables = start([pltpu.make_async_copy(c_hbm, c_ref, sems.at[6]),
                        pltpu.make_async_copy(s1_hbm, s1_ref, sems.at[7]),
                        pltpu.make_async_copy(s2_hbm, s2_ref, sems.at[8])])
        loads = load(0)
        for cp in tables:
            cp.wait()
        head0 = _head_of_lane(128) == 0
        zeros = jnp.zeros((ATT_KPAD, 128), F32)
        for pair in range(2):
            for cp in loads:
                cp.wait()
            c, s1, s2 = c_ref[...], s1_ref[...], s2_ref[...]
            qs[...] = _rope(qs[...], c, s1, s2) * _ATT_SCALE
            ks[pl.ds(0, ATT_KPAD), :] = zeros
            vs[pl.ds(0, ATT_KPAD), :] = zeros
            ks[pl.ds(ATT_KPAD, s), :] = _rope(ks[pl.ds(ATT_KPAD, s), :], c, s1, s2)
            dqs[...] = jnp.zeros_like(dqs)
            dks[...] = jnp.zeros_like(dks)
            dvs[...] = jnp.zeros_like(dvs)

            for _, d in ATT_PATTERNS:
                def blk(i, carry, d=d):
                    b, sq, sk = _att_slices(i, d)
                    qb = qs[sq, :]
                    kw = ks[sk, :].astype(BF16)
                    vw = vs[sk, :].astype(BF16)
                    dob = do_ref[sq, :]
                    lse_b = lse_ref[sq, :]
                    dd = dob * o_ref[sq, :]
                    valid = _att_valid(b)
                    dq_b = jnp.zeros((ATT_BLOCK, 128), F32)
                    dk_w = jnp.zeros((2 * ATT_BLOCK, 128), F32)
                    dv_w = jnp.zeros((2 * ATT_BLOCK, 128), F32)
                    for hh in range(2):
                        hm = head0 if hh == 0 else ~head0
                        qh = jnp.where(hm, qb, 0.0).astype(BF16)
                        doh = jnp.where(hm, dob, 0.0).astype(BF16)
                        lse_h = _pick_lane(lse_b, hh * HEAD_LANES)
                        d_h = jnp.sum(jnp.where(hm, dd, 0.0), axis=1, keepdims=True)
                        sc = lax.dot_general(qh, kw, (((1,), (1,)), ((), ())), preferred_element_type=F32)
                        p = jnp.where(valid, jnp.exp(sc - lse_h), 0.0)
                        dp = lax.dot_general(doh, vw, (((1,), (1,)), ((), ())), preferred_element_type=F32)
                        ds = (p * (dp - d_h)).astype(BF16)
                        dq_b = dq_b + jnp.where(hm, jnp.dot(ds, kw, preferred_element_type=F32), 0.0)
                        dk_w = dk_w + lax.dot_general(ds, qh, (((0,), (0,)), ((), ())), preferred_element_type=F32)
                        dv_w = dv_w + lax.dot_general(p.astype(BF16), doh, (((0,), (0,)), ((), ())),
                                                      preferred_element_type=F32)
                    dqs[sq, :] += dq_b
                    dks[sk, :] += dk_w
                    dvs[sk, :] += dv_w
                    return carry

                lax.fori_loop(0, _ATT_NBLK, blk, 0, unroll=_ATT_UNROLL)

            staged[0] = _rope_t(dqs[...] * _ATT_SCALE, c, s1, s2).astype(BF16)
            staged[1] = _rope_t(dks[pl.ds(ATT_KPAD, s), :], c, s1, s2).astype(BF16)
            staged[2] = dvs[pl.ds(ATT_KPAD, s), :].astype(BF16)
            stores = start([
                pltpu.make_async_copy(staged.at[j], dproj_hbm.at[:, pl.ds(128 * (col + pair), 128)], sems.at[9 + j])
                for j, col in enumerate((PROJ_Q_BLK, PROJ_K_BLK, PROJ_V_BLK))])
            if pair == 0:
                loads = load(1)
            for cp in stores:
                cp.wait()

    big = pltpu.VMEM((ATT_KPAD + s, 128), F32)
    tok = pltpu.VMEM((s, 128), F32)
    return pl.pallas_call(
        body, in_specs=[_HBM] * 8, out_specs=_HBM, out_shape=jax.ShapeDtypeStruct(dproj.shape, dproj.dtype),
        input_output_aliases={7: 0},
        scratch_shapes=[tok] * 6 + [tok, big, big, tok, big, big, pltpu.VMEM((3, s, 128), BF16),
                                    pltpu.SemaphoreType.DMA((12,))], name=name,
        compiler_params=pltpu.CompilerParams(vmem_limit_bytes=VMEM_LIMIT))(
            proj, cos, sin1, sin2, out, lse, dmix, dproj)


_POOL_TM = 512
_POOL_NT = SEQ // _POOL_TM
_POOL_HALO_PER_TILE = _POOL_TM // POOL_HALO


def _pool_tile(u_ext, w_bd, scale, t0):
    s2 = u_ext + _roll_rows(u_ext, 1)
    s4 = s2 + _roll_rows(s2, 2)
    s8 = s4 + _roll_rows(s4, 4)
    s16 = s8 + _roll_rows(s8, 8)
    grp = _head_of_lane(POOL_W)
    sel = jnp.where(grp == 0, s2, jnp.where(grp == 1, s4, jnp.where(grp == 2, s8, s16)))[POOL_HALO:]
    t = sel.shape[0]
    pos = t0 + lax.broadcasted_iota(jnp.int32, (t, POOL_W), 0) + 1
    win = jnp.where(grp == 0, 2, jnp.where(grp == 1, 4, jnp.where(grp == 2, 8, 16)))
    cnt = jnp.minimum(pos, win).astype(F32)
    diff = sel / cnt - u_ext[POOL_HALO:]
    return jnp.dot(diff.astype(BF16), w_bd.astype(BF16), preferred_element_type=F32) * scale


def _pool_fwd(proj, w_bd, scale, mix, *, name):
    tm = _POOL_TM

    def body(u_ref, halo_ref, w_ref, sc_ref, _, y_ref):
        i = pl.program_id(0)
        halo = jnp.where(i == 0, 0.0, halo_ref[...])
        u_ext = jnp.concatenate([halo, u_ref[...]], axis=0)
        y_ref[...] = _pool_tile(u_ext, w_ref[...], sc_ref[...], i * tm).astype(BF16)

    return pl.pallas_call(
        body, grid=(_POOL_NT,),
        in_specs=[pl.BlockSpec((tm, POOL_W), lambda i: (i, PROJ_POOL_BLK)),
                  pl.BlockSpec((POOL_HALO, POOL_W),
                               lambda i: (jnp.maximum(i * _POOL_HALO_PER_TILE - 1, 0), PROJ_POOL_BLK)),
                  pl.BlockSpec((POOL_W, POOL_W), lambda i: (0, 0)), pl.BlockSpec((1, POOL_W), lambda i: (0, 0)), _HBM],
        out_specs=pl.BlockSpec((tm, POOL_W), lambda i: (i, MIX_POOL_BLK)),
        out_shape=jax.ShapeDtypeStruct(mix.shape, mix.dtype), input_output_aliases={4: 0}, name=name,
        compiler_params=_cparams("parallel"))(proj, proj, w_bd, scale, mix)


def _pool_bwd(proj, dmix, w_bd, scale, dproj, *, name):
    tm = _POOL_TM
    last = _POOL_NT - 1

    def body(u_ref, halo_ref, dy_ref, w_ref, sc_ref, _, du_ref, dw_ref, dsc_ref, dhalo_acc):
        i = pl.program_id(0)

        @pl.when(i == 0)
        def _():
            dhalo_acc[...] = jnp.zeros_like(dhalo_acc)
            dw_ref[...] = jnp.zeros_like(dw_ref)
            dsc_ref[...] = jnp.zeros_like(dsc_ref)

        tile = last - i
        halo = jnp.where(tile == 0, 0.0, halo_ref[...])
        u_ext = jnp.concatenate([halo, u_ref[...]], axis=0)
        _, vjp = jax.vjp(functools.partial(_pool_tile, t0=tile * tm), u_ext, w_ref[...], sc_ref[...])
        gu, gw, gs = vjp(dy_ref[...])
        du_ref[...] = jnp.concatenate([gu[POOL_HALO:tm], gu[tm:] + dhalo_acc[...]], axis=0).astype(BF16)
        dhalo_acc[...] = gu[:POOL_HALO]
        dw_ref[...] += gw
        dsc_ref[...] += gs

    rev = lambda i: last - i
    return pl.pallas_call(
        body, grid=(_POOL_NT,),
        in_specs=[pl.BlockSpec((tm, POOL_W), lambda i: (rev(i), PROJ_POOL_BLK)),
                  pl.BlockSpec((POOL_HALO, POOL_W),
                               lambda i: (jnp.maximum(rev(i) * _POOL_HALO_PER_TILE - 1, 0), PROJ_POOL_BLK)),
                  pl.BlockSpec((tm, POOL_W), lambda i: (rev(i), MIX_POOL_BLK)),
                  pl.BlockSpec((POOL_W, POOL_W), lambda i: (0, 0)), pl.BlockSpec((1, POOL_W), lambda i: (0, 0)), _HBM],
        out_specs=[pl.BlockSpec((tm, POOL_W), lambda i: (rev(i), PROJ_POOL_BLK)),
                   pl.BlockSpec((POOL_W, POOL_W), lambda i: (0, 0)), pl.BlockSpec((1, POOL_W), lambda i: (0, 0))],
        out_shape=[jax.ShapeDtypeStruct(dproj.shape, dproj.dtype), jax.ShapeDtypeStruct((POOL_W, POOL_W), F32),
                   jax.ShapeDtypeStruct((1, POOL_W), F32)],
        input_output_aliases={5: 0}, scratch_shapes=[pltpu.VMEM((POOL_HALO, POOL_W), F32)], name=name,
        compiler_params=_cparams("arbitrary"))(proj, proj, dmix, w_bd, scale, dproj)


_FFN_TM = 256
_FFN_NT = SEQ // _FFN_TM
_FFN_HALO_PER_TILE = _FFN_TM // CONV_HALO


def _ffn_act_tile(hid_ext, conv_w, conv_b):
    hc = conv_b
    for k in range(FFN_CONV_K):
        hc = hc + _pick_row(conv_w, k) * _shift_rows(hid_ext, FFN_CONV_K - 1 - k, CONV_HALO)
    return _silu(hc[:, :FFN_DIM]) * hc[:, FFN_DIM:]


def _ffn_fwd(x1, norm_g, sc, sh, gate, up_t, down, conv_w, conv_b, *, name):
    tm = _FFN_TM
    w = 2 * FFN_DIM
    d = D_MODEL

    def body(x_ref, ng_ref, sc_ref, sh_ref, g_ref, up_ref, dn_ref, cw_ref, cb_ref,
             h_ref, hid_ref, act_ref, f_ref, x2_ref, halo_acc):
        i = pl.program_id(0)
        h2 = _rms_modulate(x_ref[...], ng_ref[...], sc_ref[...], sh_ref[...]).astype(BF16)
        h_ref[...] = h2
        hid = lax.dot_general(h2, up_ref[...], (((1,), (1,)), ((), ())), preferred_element_type=F32)
        hid_ref[...] = hid
        halo = jnp.where(i == 0, 0.0, halo_acc[...])
        act = _ffn_act_tile(jnp.concatenate([halo, hid], axis=0), cw_ref[...], cb_ref[...]).astype(BF16)
        halo_acc[...] = hid[tm - CONV_HALO:]
        act_ref[...] = act
        f = jnp.dot(act, dn_ref[...], preferred_element_type=F32)
        f_ref[...] = f
        x2_ref[...] = x_ref[...] + g_ref[...] * f

    tile = lambda n: pl.BlockSpec((tm, n), lambda i: (i, 0))
    return pl.pallas_call(
        body, grid=(_FFN_NT,),
        in_specs=[tile(d)] + [_resident((1, d))] * 4 + [_resident((w, d)), _resident((FFN_DIM, d)),
                                                        _resident((8, w)), _resident((1, w))],
        out_specs=[tile(d), tile(w), tile(FFN_DIM), tile(d), tile(d)],
        out_shape=[jax.ShapeDtypeStruct((SEQ, d), BF16), jax.ShapeDtypeStruct((SEQ, w), F32),
                   jax.ShapeDtypeStruct((SEQ, FFN_DIM), BF16), jax.ShapeDtypeStruct((SEQ, d), F32),
                   jax.ShapeDtypeStruct((SEQ, d), F32)],
        scratch_shapes=[pltpu.VMEM((CONV_HALO, w), F32)], name=name,
        compiler_params=_cparams("arbitrary"))(x1, norm_g, sc, sh, gate, up_t, down, conv_w, conv_b)


def _ffn_bwd(dx2, gate, f, hid, up_t, down, conv_w, conv_b, *, name):
    tm = _FFN_TM
    w = 2 * FFN_DIM
    d = D_MODEL
    last = _FFN_NT - 1

    def body(dx_ref, g_ref, f_ref, h_ref, halo_ref, up_ref, dn_ref, cw_ref, cb_ref,
             df_ref, dg_ref, dh_ref, dh2_ref, dcw_ref, dcb_ref, dhalo_acc):
        i = pl.program_id(0)

        @pl.when(i == 0)
        def _():
            dhalo_acc[...] = jnp.zeros_like(dhalo_acc)
            dcw_ref[...] = jnp.zeros_like(dcw_ref)
            dcb_ref[...] = jnp.zeros_like(dcb_ref)
            dg_ref[...] = jnp.zeros_like(dg_ref)

        dxv = dx_ref[...]
        df = (g_ref[...] * dxv).astype(BF16)
        df_ref[...] = df
        dg_ref[...] += jnp.sum(dxv * f_ref[...], axis=0, keepdims=True)
        dact = lax.dot_general(df, dn_ref[...], (((1,), (1,)), ((), ())), preferred_element_type=F32)
        halo = jnp.where(i == last, 0.0, halo_ref[...])
        hid_ext = jnp.concatenate([halo, h_ref[...]], axis=0)
        _, vjp = jax.vjp(_ffn_act_tile, hid_ext, cw_ref[...], cb_ref[...])
        gh, gw, gb = vjp(dact)
        dhid = jnp.concatenate([gh[CONV_HALO:tm], gh[tm:] + dhalo_acc[...]], axis=0).astype(BF16)
        dhalo_acc[...] = gh[:CONV_HALO]
        dh_ref[...] = dhid
        dh2_ref[...] = jnp.dot(dhid, up_ref[...], preferred_element_type=F32)
        dcw_ref[...] += gw
        dcb_ref[...] += gb

    rev = lambda i: last - i
    tile = lambda n: pl.BlockSpec((tm, n), lambda i: (rev(i), 0))
    acc = lambda shape: pl.BlockSpec(shape, lambda i: (0, 0))
    return pl.pallas_call(
        body, grid=(_FFN_NT,),
        in_specs=[tile(d), _resident((1, d)), tile(d), tile(w),
                  pl.BlockSpec((CONV_HALO, w), lambda i: (jnp.maximum(rev(i) * _FFN_HALO_PER_TILE - 1, 0), 0)),
                  _resident((w, d)), _resident((FFN_DIM, d)), _resident((8, w)), _resident((1, w))],
        out_specs=[tile(d), acc((1, d)), tile(w), tile(d), acc((8, w)), acc((1, w))],
        out_shape=[jax.ShapeDtypeStruct((SEQ, d), BF16), jax.ShapeDtypeStruct((1, d), F32),
                   jax.ShapeDtypeStruct((SEQ, w), BF16), jax.ShapeDtypeStruct((SEQ, d), F32),
                   jax.ShapeDtypeStruct((8, w), F32), jax.ShapeDtypeStruct((1, w), F32)],
        scratch_shapes=[pltpu.VMEM((CONV_HALO, w), F32)], name=name,
        compiler_params=_cparams("arbitrary"))(dx2, gate, f, hid, hid, up_t, down, conv_w, conv_b)


def _axes():
    return lax.axis_index("x"), lax.axis_index("y"), lax.axis_index("c")


def _handshake(peers):
    barrier = pltpu.get_barrier_semaphore()
    for peer in peers:
        pl.semaphore_signal(barrier, inc=1, device_id=peer, device_id_type=MESH)
    pl.semaphore_wait(barrier, len(peers))


def _allgather_body(x_refs, out_refs, send_sems, recv_sems, local_sems, own_barrier):
    n = len(x_refs)
    x, y, c = _axes()
    me, sibling = (x, y, c), (x, y, 1 - c)
    chips = [(1 - x, y), (x, 1 - y), (1 - x, 1 - y)]
    if own_barrier:
        _handshake([sibling] + [(*chip, c) for chip in chips])

    def slot(a, px, py, pc):
        return out_refs[a].at[4 * px + 2 * py + pc]

    def copy(a, k, block, to, src=None):
        return pltpu.make_async_remote_copy(
            src_ref=slot(a, *block) if src is None else src, dst_ref=slot(a, *block),
            send_sem=send_sems.at[a, k], recv_sem=recv_sems.at[a, k], device_id=to, device_id_type=MESH)

    mines, firsts = [], []
    for a in range(n):
        mines.append(pltpu.make_async_copy(x_refs[a], slot(a, *me), local_sems.at[a]))
        mines[-1].start()
        first = [copy(a, 0, me, sibling, src=x_refs[a])]
        first += [copy(a, 1 + j, me, (*chip, c), src=x_refs[a]) for j, chip in enumerate(chips)]
        for cp in first:
            cp.start()
        firsts += first
    passed = []
    for j, chip in enumerate(chips):
        for a in range(n):
            copy(a, 1 + j, (*chip, c), me).wait_recv()
            passed.append(copy(a, 4 + j, (*chip, c), sibling))
            passed[-1].start()
    for a in range(n):
        copy(a, 0, sibling, me).wait_recv()
    for j, chip in enumerate(chips):
        for a in range(n):
            copy(a, 4 + j, (*chip, 1 - c), me).wait_recv()
    for cp in firsts + passed:
        cp.wait_send()
    for cp in mines:
        cp.wait()


def _allgather_sems(n):
    return [pltpu.SemaphoreType.DMA((n, 7)), pltpu.SemaphoreType.DMA((n, 7)), pltpu.SemaphoreType.DMA((n,))]


def _allgather(xs, *, name):
    n = len(xs)

    def body(*refs):
        _allgather_body(refs[:n], refs[n:2 * n], *refs[2 * n:], own_barrier=False)

    return pl.pallas_call(
        body, out_shape=[jax.ShapeDtypeStruct((N_DEV,) + xb.shape, xb.dtype) for xb in xs],
        in_specs=[_HBM] * n, out_specs=[_HBM] * n, scratch_shapes=_allgather_sems(n), name=name)(*xs)


def _allgather_async(xs, *, name, collective_id):
    n = len(xs)
    x_refs = [jax.new_ref(xb, memory_space=pltpu.MemorySpace.HBM) for xb in xs]
    out_refs = [jax.empty_ref(jax.ShapeDtypeStruct((N_DEV,) + xb.shape, xb.dtype), memory_space=pltpu.MemorySpace.HBM)
                for xb in xs]

    @pl.kernel(mesh=plsc.ScalarSubcoreMesh(axis_name="sequencer", num_cores=1), name=name,
               scratch_types=tuple(_allgather_sems(n)),
               compiler_params=pltpu.CompilerParams(collective_id=collective_id))
    def launch(send_sems, recv_sems, local_sems):
        _allgather_body(x_refs, out_refs, send_sems, recv_sems, local_sems, own_barrier=True)

    launch()
    return [r[...] for r in out_refs]


def _pair_exchange(blocks, *, name, collective_id):
    n = len(blocks)
    hbm = pltpu.MemorySpace.HBM
    in_refs = [jax.new_ref(b, memory_space=hbm) for b in blocks]
    out_refs = [jax.empty_ref(jax.ShapeDtypeStruct((4,) + b.shape[1:], b.dtype), memory_space=hbm) for b in blocks]

    @pl.kernel(mesh=plsc.ScalarSubcoreMesh(axis_name="sequencer", num_cores=1), name=name,
               scratch_types=(pltpu.SemaphoreType.DMA((n, 4)), pltpu.SemaphoreType.DMA((n, 4))),
               compiler_params=pltpu.CompilerParams(collective_id=collective_id))
    def launch(send_sems, recv_sems):
        x, y, c = _axes()
        _handshake([(x, y, 1 - c)])
        copies = [pltpu.make_async_remote_copy(
            src_ref=in_refs[a].at[2 * s + (1 - c)], dst_ref=out_refs[a].at[s], send_sem=send_sems.at[a, s],
            recv_sem=recv_sems.at[a, s], device_id=(x, y, 1 - c), device_id_type=MESH)
            for a in range(n) for s in range(4)]
        for cp in copies:
            cp.start()
        for cp in copies:
            cp.wait_recv()
        for cp in copies:
            cp.wait_send()

    launch()
    return [r[...] for r in out_refs]


def _chip_exchange(parts, *, name, collective_id):
    n = len(parts)
    hbm = pltpu.MemorySpace.HBM
    in_refs = [jax.new_ref(p, memory_space=hbm) for p in parts]
    out_refs = [jax.empty_ref(jax.ShapeDtypeStruct(p.shape, p.dtype), memory_space=hbm) for p in parts]

    @pl.kernel(mesh=plsc.ScalarSubcoreMesh(axis_name="sequencer", num_cores=1), name=name,
               scratch_types=(pltpu.SemaphoreType.DMA((n, 3)), pltpu.SemaphoreType.DMA((n, 3)),
                              pltpu.SemaphoreType.DMA((n,))),
               compiler_params=pltpu.CompilerParams(collective_id=collective_id))
    def launch(send_sems, recv_sems, local_sems):
        x, y, c = _axes()
        my_chip = 2 * x + y
        chips = [(1 - x, y), (x, 1 - y), (1 - x, 1 - y)]
        _handshake([(*chip, c) for chip in chips])
        locals_ = [pltpu.make_async_copy(in_refs[a].at[my_chip], out_refs[a].at[my_chip], local_sems.at[a])
                   for a in range(n)]
        for cp in locals_:
            cp.start()
        copies = [pltpu.make_async_remote_copy(
            src_ref=in_refs[a].at[2 * px + py], dst_ref=out_refs[a].at[my_chip], send_sem=send_sems.at[a, k],
            recv_sem=recv_sems.at[a, k], device_id=(px, py, c), device_id_type=MESH)
            for a in range(n) for k, (px, py) in enumerate(chips)]
        for cp in copies:
            cp.start()
        for cp in copies:
            cp.wait_recv()
        for cp in copies:
            cp.wait_send()
        for cp in locals_:
            cp.wait()

    launch()
    return [r[...] for r in out_refs]


def _pair_sum(core, blocks, from_sibling, *, name):
    n = len(blocks)

    def body(core_ref, *refs):
        for a_ref, b_ref, o_ref in zip(refs[:n], refs[n:2 * n], refs[2 * n:]):
            o_ref[...] = (a_ref[...].astype(F32) + b_ref[...].astype(F32)).astype(o_ref.dtype)

    mine = lambda b: pl.BlockSpec((1,) + b.shape[1:], lambda s, core_ref: (2 * s + core_ref[0], 0, 0))
    slot = lambda b: pl.BlockSpec((1,) + b.shape[1:], lambda s, core_ref: (s, 0, 0))
    return pl.pallas_call(
        body,
        grid_spec=pltpu.PrefetchScalarGridSpec(
            num_scalar_prefetch=1, grid=(4,),
            in_specs=[mine(b) for b in blocks] + [slot(b) for b in blocks], out_specs=[slot(b) for b in blocks]),
        out_shape=[jax.ShapeDtypeStruct(s.shape, s.dtype) for s in from_sibling], name=name,
        compiler_params=_cparams("parallel"))(core, *blocks, *from_sibling)


def _sum_blocks(a, *, name, tr=None):
    n, r, cdim = a.shape
    tr = tr or r

    def body(a_ref, o_ref):
        acc = a_ref[0].astype(F32)
        for k in range(1, n):
            acc = acc + a_ref[k].astype(F32)
        o_ref[...] = acc

    return pl.pallas_call(body, grid=(r // tr,), in_specs=[pl.BlockSpec((n, tr, cdim), lambda i: (0, i, 0))],
                          out_specs=pl.BlockSpec((tr, cdim), lambda i: (i, 0)),
                          out_shape=jax.ShapeDtypeStruct((r, cdim), F32), name=name,
                          compiler_params=_cparams("parallel"))(a)


def _sum_gathered(gathered, *, name):
    n = len(gathered)

    def body(*refs):
        for a_ref, o_ref in zip(refs[:n], refs[n:]):
            acc = a_ref[0]
            for k in range(1, N_DEV):
                acc = acc + a_ref[k]
            o_ref[...] = acc

    return pl.pallas_call(body, out_shape=[jax.ShapeDtypeStruct(g.shape[1:], F32) for g in gathered], name=name,
                          compiler_params=pltpu.CompilerParams(vmem_limit_bytes=VMEM_LIMIT))(*gathered)


_ADA_SHARD = 6 * D_MODEL // N_DEV


def _ada_mod(c_all, ada_w, *, name):
    def body(c_ref, w_ref, o_ref):
        o_ref[0] = jnp.dot(_silu(c_ref[...]).astype(BF16), w_ref[0].astype(BF16), preferred_element_type=F32)

    return pl.pallas_call(
        body, grid=(DEPTH,),
        in_specs=[pl.BlockSpec((N_DEV, D_MODEL), lambda l: (0, 0)),
                  pl.BlockSpec((1, D_MODEL, _ADA_SHARD), lambda l: (l, 0, 0))],
        out_specs=pl.BlockSpec((1, N_DEV, _ADA_SHARD), lambda l: (l, 0, 0)),
        out_shape=jax.ShapeDtypeStruct((DEPTH, N_DEV, _ADA_SHARD), F32), name=name,
        compiler_params=_cparams("parallel"))(c_all, ada_w)


def _ada_wgrad(c_all, dmod_cols, *, name):
    def body(c_ref, d_ref, o_ref):
        o_ref[0] = lax.dot_general(_silu(c_ref[...]), d_ref[0], (((0,), (0,)), ((), ())),
                                   preferred_element_type=F32, precision=lax.Precision.HIGHEST)

    return pl.pallas_call(
        body, grid=(DEPTH,),
        in_specs=[pl.BlockSpec((N_DEV, D_MODEL), lambda l: (0, 0)),
                  pl.BlockSpec((1, N_DEV, _ADA_SHARD), lambda l: (l, 0, 0))],
        out_specs=pl.BlockSpec((1, D_MODEL, _ADA_SHARD), lambda l: (l, 0, 0)),
        out_shape=jax.ShapeDtypeStruct((DEPTH, D_MODEL, _ADA_SHARD), F32), name=name,
        compiler_params=_cparams("parallel"))(c_all, dmod_cols)


def _add_rows(a, b, *, name):
    def body(a_ref, b_ref, o_ref):
        o_ref[...] = a_ref[...] + b_ref[...]

    return pl.pallas_call(body, out_shape=jax.ShapeDtypeStruct(a.shape, a.dtype), name=name)(a, b)


def _adamw_update(w_ref, g_ref, m_ref, v_ref, d_ref, mo_ref, vo_ref):
    gv = g_ref[...]
    mn = ADAM_B1 * m_ref[...] + (1.0 - ADAM_B1) * gv
    vn = ADAM_B2 * v_ref[...] + (1.0 - ADAM_B2) * (gv * gv)
    mo_ref[...] = mn
    vo_ref[...] = vn
    m_hat = mn / (1.0 - ADAM_B1 ** ADAM_STEP)
    v_hat = vn / (1.0 - ADAM_B2 ** ADAM_STEP)
    d_ref[...] = -ADAM_LR * (m_hat / (jnp.sqrt(v_hat) + ADAM_EPS) + ADAM_WD * w_ref[...])


def _adamw_small(ws, gs, ms, vs, *, name):
    n = len(ws)

    def body(*refs):
        ins, outs = refs[:4 * n], refs[4 * n:]
        for i in range(n):
            _adamw_update(ins[i], ins[n + i], ins[2 * n + i], ins[3 * n + i], outs[i], outs[n + i], outs[2 * n + i])

    shapes = [jax.ShapeDtypeStruct(a.shape, F32) for a in ws]
    outs = pl.pallas_call(body, out_shape=shapes * 3, name=name,
                          compiler_params=pltpu.CompilerParams(vmem_limit_bytes=VMEM_LIMIT))(*ws, *gs, *ms, *vs)
    return outs[:n], outs[n:2 * n], outs[2 * n:]


def _adamw(w, g, m, v, *, name, tr):
    r, cdim = w.shape
    body = functools.partial(_adamw_update)

    spec = pl.BlockSpec((tr, cdim), lambda i: (i, 0))
    shp = jax.ShapeDtypeStruct((r, cdim), F32)
    return pl.pallas_call(body, grid=(r // tr,), in_specs=[spec] * 4, out_specs=[spec] * 3, out_shape=[shp] * 3,
                          name=name, compiler_params=_cparams("parallel"))(w, g, m, v)


def _pad_rows(a, rows):
    return jnp.concatenate([a, jnp.zeros((rows - a.shape[0],) + a.shape[1:], a.dtype)], axis=0)


def _pad_lanes(a, lanes):
    return jnp.concatenate([a, jnp.zeros(a.shape[:-1] + (lanes - a.shape[-1],), a.dtype)], axis=-1)


def _permute_w_in(wt):
    return jnp.concatenate([wt[512:1536], wt[:512], wt[1536:1544],
                            jnp.zeros((PROJ_W - IN_W, wt.shape[1]), wt.dtype), wt[1544:]], axis=0)


def _unpermute_w_in(wp):
    return jnp.concatenate([wp[1024:1536], wp[:1024], wp[1536:1544], wp[PROJ_SSD_W:]], axis=0)


def _block_diag(w):
    rows = []
    for g in range(4):
        rows.append(jnp.concatenate([w[g] if k == g else jnp.zeros_like(w[g]) for k in range(4)], axis=1))
    return jnp.concatenate(rows, axis=0)


def _diag_blocks(wbd):
    return jnp.stack([wbd[64 * g:64 * (g + 1), 64 * g:64 * (g + 1)] for g in range(4)], axis=0)


def _layer_params(l, small):
    return dict(
        norm1_g=small["norm1_g"][l][None], norm2_g=small["norm2_g"][l][None],
        conv_w=_pad_rows(small["ssd_conv_w"][l], 8), conv_b=small["ssd_conv_b"][l][None],
        dt_bias=_pad_lanes(small["ssd_dt_bias"][l][None], 128), a_log=_pad_lanes(small["ssd_a_log"][l][None], 128),
        d_skip=_pad_lanes(small["ssd_d"][l][None], 128), ssd_norm_g=small["ssd_norm_g"][l][None],
        pool_bd=_block_diag(small["pool_w"][l]), pool_scale=small["pool_scale"][l][None],
        fcw=_pad_rows(small["ffn_conv_w"][l], 8), fcb=small["ffn_conv_b"][l][None])


def _mod_rows(mod_l):
    return [mod_l[None, D_MODEL * i:D_MODEL * (i + 1)] for i in range(6)]


def _layer_fwd(x, mod_l, p, tabs, l, gather):
    sh1, sc1, g1, sh2, sc2, g2 = _mod_rows(mod_l)
    mix_w = gather(l, "mix", None)
    p.update(w_in=mix_w["w_in"], w_out=mix_w["w_out"])
    proj, h1 = _mm(x, p["w_in"], nt=True, norm=(p["norm1_g"], sc1, sh1), name=f"l{l}_proj")
    ffn_w = gather(l, "ffn", proj)
    p.update(up=ffn_w["ffn_up"], down=ffn_w["ffn_down"])
    mix, hst = _ssd_fwd(proj, p["conv_w"], p["conv_b"], p["dt_bias"], p["a_log"], p["d_skip"], p["ssd_norm_g"],
                        name=f"l{l}_ssd")
    mix = _pool_fwd(proj, p["pool_bd"], p["pool_scale"], mix, name=f"l{l}_pool")
    y_att, lse, mix = _att_fwd(proj, *tabs, mix, name=f"l{l}_att")
    mo, x1 = _mm(mix, p["w_out"], residual=(x, g1), name=f"l{l}_out")
    gather(l + 1, "mix", (x1, p["up"]))
    h2, hid, act, f, x2 = _ffn_fwd(x1, p["norm2_g"], sc2, sh2, g2, p["up"], p["down"], p["fcw"], p["fcb"],
                                   name=f"l{l}_ffn")
    return x2, dict(x=x, h1=h1, proj=proj, hst=hst, y_att=y_att, lse=lse, mix=mix, mo=mo, x1=x1, h2=h2, hid=hid,
                    act=act, f=f)


def _layer_bwd(dx2, sv, mod_l, p, tabs, l, exchange):
    sh1, sc1, g1, sh2, sc2, g2 = _mod_rows(mod_l)
    df, dg2, dhid, dh2, dfcw, dfcb = _ffn_bwd(dx2, g2, sv["f"], sv["hid"], p["up"], p["down"], p["fcw"], p["fcb"],
                                              name=f"l{l}_ffn_b")
    d_down = _wgrad(sv["act"], df, tk=1408, name=f"l{l}_down_bw")
    d_up = _wgrad(dhid, sv["h2"], tk=1408, name=f"l{l}_up_bw")
    exchange(l, "ffn", dict(ffn_up=d_up, ffn_down=d_down))
    dx1, dn2, dsc2, dsh2, dmix, dmo, dg1 = _mid_bwd(sv["x1"], dh2, dx2, p["norm2_g"], sc2, g1, sv["mo"], p["w_out"],
                                                    name=f"l{l}_mid_b")
    d_wout = _wgrad(sv["mix"], dmo, name=f"l{l}_out_bw")
    dproj, dcw, dcb, ddb, dal, dd, dng = _ssd_bwd(
        sv["proj"], sv["hst"], dmix, p["conv_w"], p["conv_b"], p["dt_bias"], p["a_log"], p["d_skip"],
        p["ssd_norm_g"], name=f"l{l}_ssd_b")
    dproj, dwbd, dpsc = _pool_bwd(sv["proj"], dmix, p["pool_bd"], p["pool_scale"], dproj, name=f"l{l}_pool_b")
    dproj = _att_bwd(sv["proj"], *tabs, sv["y_att"], sv["lse"], dmix, dproj, name=f"l{l}_att_b")
    d_win = _wgrad(dproj, sv["h1"], tk=1408, name=f"l{l}_proj_bw")
    exchange(l, "mix", dict(w_in=d_win, w_out=d_wout))
    dx0, dn1, dsc1, dsh1 = _norm_mod_bwd(sv["x"], dproj, dx1, p["norm1_g"], sc1, w=p["w_in"], name=f"l{l}_proj_b")
    dmod = jnp.concatenate([dsh1, dsc1, dg1, dsh2, dsc2, dg2], axis=1)[0]
    small = dict(norm1_g=dn1[0], ssd_conv_w=dcw[:SSD_CONV_K], ssd_conv_b=dcb[0], ssd_dt_bias=ddb[0], ssd_a_log=dal[0],
                 ssd_d=dd[0], ssd_norm_g=dng[0], pool_w=_diag_blocks(dwbd), pool_scale=dpsc[0], norm2_g=dn2[0],
                 ffn_conv_w=dfcw[:FFN_CONV_K], ffn_conv_b=dfcb[0])
    return dx0, dmod, small


def _example_step(x, target, pos_col, inv_freq_lane, mod, gather, small, final_g, exchange):
    tabs = _rope_tables(pos_col, inv_freq_lane, name="rope_tables")
    params, saved = [], []
    for l in range(DEPTH):
        params.append(_layer_params(l, small))
        x, sv = _layer_fwd(x, mod[l], params[l], tabs, l, gather)
        saved.append(sv)
    loss_row, dx, dfg = _final_loss(x, final_g[None], target, name="final_loss")
    dmods, smalls = [None] * DEPTH, [None] * DEPTH
    for l in reversed(range(DEPTH)):
        dx, dmods[l], smalls[l] = _layer_bwd(dx, saved[l], mod[l], params[l], tabs, l, exchange)
    return loss_row, dx, jnp.stack(dmods, axis=0), smalls, dfg[0]


_BIG = ("w_in", "w_out", "ffn_up", "ffn_down")
_SMALL_GRADS = ("norm1_g", "ssd_conv_w", "ssd_conv_b", "ssd_dt_bias", "ssd_a_log", "ssd_d", "ssd_norm_g", "pool_w",
                "pool_scale", "norm2_g", "ffn_conv_w", "ffn_conv_b")
_SMALL_PARAMS = ("ada_b", "norm1_g", "ssd_conv_w", "ssd_conv_b", "ssd_dt_bias", "ssd_a_log", "ssd_d", "ssd_norm_g",
                 "pool_w", "pool_scale", "norm2_g", "ffn_conv_w", "ffn_conv_b", "final_g")
_WEIGHT_ORDER = ("ada_w", "ada_b", "norm1_g", "w_in", "ssd_conv_w", "ssd_conv_b", "ssd_dt_bias", "ssd_a_log", "ssd_d",
                 "ssd_norm_g", "pool_w", "pool_scale", "w_out", "norm2_g", "ffn_up", "ffn_conv_w", "ffn_conv_b",
                 "ffn_down", "final_g")


_COLUMN_SHARDED = ("w_in", "ffn_up")
_GROUPS = (("mix", ("w_in", "w_out")), ("ffn", ("ffn_up", "ffn_down")))


def _big_shares(w, l, names):
    return [(w[name][l].T if name in _COLUMN_SHARDED else w[name][l]).astype(BF16) for name in names]


def _unshard_big(names, gathered):
    out = {}
    for name, g in zip(names, gathered):
        full = g.reshape(N_DEV * g.shape[1], g.shape[2])
        out[name] = _permute_w_in(full) if name == "w_in" else full
    return out


def _shard_big(grads):
    out = []
    for name, g in grads.items():
        g = _unpermute_w_in(g) if name == "w_in" else g
        out.append(g.reshape(N_DEV, g.shape[0] // N_DEV, g.shape[1]))
    return out


def kernel(x, c, positions, ada_w, ada_b, norm1_g, w_in, ssd_conv_w, ssd_conv_b, ssd_dt_bias, ssd_a_log, ssd_d, ssd_norm_g, pool_w, pool_scale, w_out, norm2_g, ffn_up, ffn_conv_w, ffn_conv_b, ffn_down, final_g, loss_target, m_ada_w, m_ada_b, m_norm1_g, m_w_in, m_ssd_conv_w, m_ssd_conv_b, m_ssd_dt_bias, m_ssd_a_log, m_ssd_d, m_ssd_norm_g, m_pool_w, m_pool_scale, m_w_out, m_norm2_g, m_ffn_up, m_ffn_conv_w, m_ffn_conv_b, m_ffn_down, m_final_g, v_ada_w, v_ada_b, v_norm1_g, v_w_in, v_ssd_conv_w, v_ssd_conv_b, v_ssd_dt_bias, v_ssd_a_log, v_ssd_d, v_ssd_norm_g, v_pool_w, v_pool_scale, v_w_out, v_norm2_g, v_ffn_up, v_ffn_conv_w, v_ffn_conv_b, v_ffn_down, v_final_g):
    w = dict(ada_w=ada_w, ada_b=ada_b, norm1_g=norm1_g, w_in=w_in, ssd_conv_w=ssd_conv_w, ssd_conv_b=ssd_conv_b,
             ssd_dt_bias=ssd_dt_bias, ssd_a_log=ssd_a_log, ssd_d=ssd_d, ssd_norm_g=ssd_norm_g, pool_w=pool_w,
             pool_scale=pool_scale, w_out=w_out, norm2_g=norm2_g, ffn_up=ffn_up, ffn_conv_w=ffn_conv_w,
             ffn_conv_b=ffn_conv_b, ffn_down=ffn_down, final_g=final_g)
    m = dict(ada_w=m_ada_w, ada_b=m_ada_b, norm1_g=m_norm1_g, w_in=m_w_in, ssd_conv_w=m_ssd_conv_w,
             ssd_conv_b=m_ssd_conv_b, ssd_dt_bias=m_ssd_dt_bias, ssd_a_log=m_ssd_a_log, ssd_d=m_ssd_d,
             ssd_norm_g=m_ssd_norm_g, pool_w=m_pool_w, pool_scale=m_pool_scale, w_out=m_w_out, norm2_g=m_norm2_g,
             ffn_up=m_ffn_up, ffn_conv_w=m_ffn_conv_w, ffn_conv_b=m_ffn_conv_b, ffn_down=m_ffn_down,
             final_g=m_final_g)
    v = dict(ada_w=v_ada_w, ada_b=v_ada_b, norm1_g=v_norm1_g, w_in=v_w_in, ssd_conv_w=v_ssd_conv_w,
             ssd_conv_b=v_ssd_conv_b, ssd_dt_bias=v_ssd_dt_bias, ssd_a_log=v_ssd_a_log, ssd_d=v_ssd_d,
             ssd_norm_g=v_ssd_norm_g, pool_w=v_pool_w, pool_scale=v_pool_scale, w_out=v_w_out, norm2_g=v_norm2_g,
             ffn_up=v_ffn_up, ffn_conv_w=v_ffn_conv_w, ffn_conv_b=v_ffn_conv_b, ffn_down=v_ffn_down,
             final_g=v_final_g)
    ix, iy, ic = _axes()
    dev = 4 * ix + 2 * iy + ic

    c_all, scw, fcw = _allgather([c, ssd_conv_w.reshape(DEPTH * SSD_CONV_K, -1),
                                  ffn_conv_w.reshape(DEPTH * FFN_CONV_K, -1)], name="gather_small")
    small_all = c_all
    c_all = c_all.reshape(N_DEV, D_MODEL)
    scw = scw.reshape(N_DEV, DEPTH, SSD_CONV_K, -1).transpose(1, 2, 0, 3).reshape(DEPTH, SSD_CONV_K, SSD_CONV_CH)
    fcw = fcw.reshape(N_DEV, DEPTH, FFN_CONV_K, -1).transpose(1, 2, 0, 3).reshape(DEPTH, FFN_CONV_K, 2 * FFN_DIM)

    mod_cols = _ada_mod(c_all, ada_w, name="ada_mod")
    mod_all = _allgather([mod_cols.reshape(DEPTH * N_DEV, _ADA_SHARD)], name="gather_mod")[0]
    mod_all = mod_all.reshape(N_DEV, DEPTH, N_DEV, _ADA_SHARD)
    mod_mine = lax.dynamic_index_in_dim(mod_all, dev, axis=2, keepdims=False)
    mod = _add_rows(mod_mine.transpose(1, 0, 2).reshape(DEPTH, 6 * D_MODEL), ada_b, name="ada_bias")

    fetched = {}

    def gather(l, group, after):
        if l < DEPTH and (l, group) not in fetched:
            names = dict(_GROUPS)[group]
            shares, _ = lax.optimization_barrier((_big_shares(w, l, names), small_all if after is None else after))
            got = _allgather_async(shares, name=f"gather_weights_l{l}_{group}",
                                   collective_id=1 + 2 * l + (group == "ffn"))
            fetched[l, group] = _unshard_big(names, got)
        return fetched.get((l, group))

    core = ic.astype(jnp.int32).reshape(1)
    from_chips = {}

    def exchange(l, group, g):
        cid = 5 + 4 * l + 2 * (group == "mix")
        blocks = _shard_big(g)
        from_sibling = _pair_exchange(blocks, name=f"grads_pair_exchange_l{l}_{group}", collective_id=cid)
        parts = _pair_sum(core, blocks, from_sibling, name=f"grads_pair_sum_l{l}_{group}")
        got = _chip_exchange(parts, name=f"grads_chip_exchange_l{l}_{group}", collective_id=cid + 1)
        from_chips.update({(l, name): t for name, t in zip(g, got)})

    small = dict(norm1_g=norm1_g, norm2_g=norm2_g, ssd_conv_w=scw, ssd_conv_b=ssd_conv_b, ssd_dt_bias=ssd_dt_bias,
                 ssd_a_log=ssd_a_log, ssd_d=ssd_d, ssd_norm_g=ssd_norm_g, pool_w=pool_w, pool_scale=pool_scale,
                 ffn_conv_w=fcw, ffn_conv_b=ffn_conv_b)

    inv_freq = ROPE_THETA ** (-jnp.arange(0, ROT_DIM, 2, dtype=F32) / ROT_DIM)
    lane = jnp.arange(128) % HEAD_LANES
    inv_freq_lane = jnp.where(lane < ROT_DIM, inv_freq[lane % (ROT_DIM // 2)], 0.0)[None, :]
    pos_col = positions.reshape(SEQ, 1).astype(F32)
    loss_row, dx, dmod, g_small, g_final = _example_step(
        x[0], loss_target[0], pos_col, inv_freq_lane, mod, gather, small, final_g, exchange)

    grads = {}
    for name in _BIG:
        per_layer = [_sum_blocks(from_chips[l, name], name=f"grads_chip_sum_l{l}_{name}") for l in range(DEPTH)]
        grads[name] = jnp.stack([g.T if name in _COLUMN_SHARDED else g for g in per_layer], axis=0)

    small_names = list(_SMALL_GRADS)
    stacked = [jnp.stack([g_small[l][name] for l in range(DEPTH)], axis=0) for name in small_names]
    small_parts = [loss_row, dmod] + [s.reshape(-1, s.shape[-1]) for s in stacked] + [g_final[None]]
    gathered = _allgather(small_parts, name="gather_small_grads")
    total = _sum_gathered(gathered, name="sum_small_grads")
    loss = total[0][0, 0]
    grads["ada_b"] = total[1]
    grads.update(zip(small_names, total[2:-1]))
    grads["final_g"] = total[-1][0]
    dmod_cols = lax.dynamic_slice_in_dim(gathered[1], dev * _ADA_SHARD, _ADA_SHARD, axis=2).transpose(1, 0, 2)
    grads["ada_w"] = _ada_wgrad(c_all, dmod_cols, name="ada_wgrad")
    for name in ("ssd_dt_bias", "ssd_a_log", "ssd_d"):
        grads[name] = grads[name][:, :SSD_HEADS]
    grads["pool_w"] = grads["pool_w"].reshape(pool_w.shape)
    grads["ssd_conv_w"] = lax.dynamic_slice_in_dim(
        grads["ssd_conv_w"].reshape(DEPTH, SSD_CONV_K, SSD_CONV_CH), dev * ssd_conv_w.shape[2], ssd_conv_w.shape[2], axis=2)
    grads["ffn_conv_w"] = lax.dynamic_slice_in_dim(
        grads["ffn_conv_w"].reshape(DEPTH, FFN_CONV_K, 2 * FFN_DIM), dev * ffn_conv_w.shape[2], ffn_conv_w.shape[2], axis=2)

    delta, new_m, new_v = {}, {}, {}
    for name, tr in (("ada_w", 512), ("w_in", 512), ("w_out", 256), ("ffn_up", 512), ("ffn_down", 352)):
        shp = w[name].shape
        two_d = lambda a: a.reshape(shp[0] * shp[1], shp[2])
        d_, m_, v_ = _adamw(two_d(w[name]), two_d(grads[name]), two_d(m[name]), two_d(v[name]), tr=tr,
                               name=f"adamw_{name}")
        delta[name], new_m[name], new_v[name] = (t.reshape(shp) for t in (d_, m_, v_))
    two_d = lambda a: a.reshape(-1, a.shape[-1])
    outs = _adamw_small(*[[two_d(t[name]) for name in _SMALL_PARAMS] for t in (w, grads, m, v)], name="adamw_small")
    for name, d_, m_, v_ in zip(_SMALL_PARAMS, *outs):
        delta[name], new_m[name], new_v[name] = (t.reshape(w[name].shape) for t in (d_, m_, v_))

    grad_x = dx[None]
    return (loss, grad_x, *[grads[n].reshape(w[n].shape) for n in _WEIGHT_ORDER],
            *[delta[n] for n in _WEIGHT_ORDER], *[new_m[n] for n in _WEIGHT_ORDER],
            *[new_v[n] for n in _WEIGHT_ORDER])
```

```python
import functools
import math

import jax
import jax.numpy as jnp
from jax import lax
from jax.experimental import pallas as pl
from jax.experimental.pallas import tpu as pltpu
from jax.experimental.pallas import tpu_sc as plsc

F32 = jnp.float32
BF16 = jnp.bfloat16

N_DEV = 8
D_MODEL = 1024
SEQ = 4096
DEPTH = 2
SSD_INNER = 512
SSD_HEADS = 8
SSD_HPG = 4
SSD_STATE = 128
SSD_CHUNK = 256
SSD_CONV_K = 4
SSD_CONV_CH = 1024
POOL_W = 256
POOL_WINDOWS = (2, 4, 8, 16)
ATT_W = 256
ATT_PATTERNS = ((128, 1), (512, 4), (2048, 16))
ATT_BLOCK = 128
ROT_DIM = 16
ROPE_THETA = 500000.0
IN_W = 2568
FFN_DIM = 2816
FFN_CONV_K = 3
NORM_EPS = 1e-6
HEAD_LANES = 64

ADAM_LR = 0.001
ADAM_B1 = 0.9
ADAM_B2 = 0.999
ADAM_EPS = 1e-08
ADAM_WD = 0.01
ADAM_STEP = 10

PROJ_W = 2816
PROJ_SSD_W = 1792
PROJ_Z_BLK = 2
PROJ_DT_BLK = 12
PROJ_POOL_BLK = 7
PROJ_Q_BLK, PROJ_K_BLK, PROJ_V_BLK = 16, 18, 20
MIX_POOL_BLK = 2
MIX_ATT_BLK = 6
VMEM_LIMIT = 56 * 1024 * 1024
CONV_HALO = 8
POOL_HALO = 16
ATT_KPAD = ATT_BLOCK * 16
MESH = pl.DeviceIdType.MESH
_HBM = pl.BlockSpec(memory_space=pl.ANY)


def _cparams(*sem):
    return pltpu.CompilerParams(dimension_semantics=sem, vmem_limit_bytes=VMEM_LIMIT)


def _resident(shape):
    return pl.BlockSpec(shape, lambda i: (0,) * len(shape), pipeline_mode=pl.Buffered(1))


def _silu(x):
    return x * jax.nn.sigmoid(x)


def _pick_lane(v, h):
    lane = lax.broadcasted_iota(jnp.int32, v.shape, 1)
    return jnp.sum(jnp.where(lane == h, v, 0.0), axis=1, keepdims=True)


def _pick_row(v, h):
    row = lax.broadcasted_iota(jnp.int32, v.shape, 0)
    return jnp.sum(jnp.where(row == h, v, 0.0), axis=0, keepdims=True)


def _head_of_lane(width):
    return lax.broadcasted_iota(jnp.int32, (1, width), 1) // HEAD_LANES


@functools.partial(jax.custom_vjp, nondiff_argnums=(1, 2))
def _shift_rows(x_ext, s, halo):
    y = x_ext if s == 0 else pltpu.roll(x_ext, s, 0)
    return y[halo:]


def _shift_rows_fwd(x_ext, s, halo):
    return _shift_rows(x_ext, s, halo), None


def _shift_rows_bwd(s, halo, _, g):
    ge = jnp.concatenate([jnp.zeros((halo, g.shape[1]), g.dtype), g], axis=0)
    return (ge if s == 0 else pltpu.roll(ge, ge.shape[0] - s, 0),)


_shift_rows.defvjp(_shift_rows_fwd, _shift_rows_bwd)


@functools.partial(jax.custom_vjp, nondiff_argnums=(1,))
def _roll_rows(x, s):
    return pltpu.roll(x, s, 0)


def _roll_rows_fwd(x, s):
    return _roll_rows(x, s), None


def _roll_rows_bwd(s, _, g):
    return (pltpu.roll(g, g.shape[0] - s, 0),)


_roll_rows.defvjp(_roll_rows_fwd, _roll_rows_bwd)


def _rms_modulate(xv, g, sc, sh):
    r = lax.rsqrt(jnp.mean(xv * xv, axis=-1, keepdims=True) + NORM_EPS)
    return (xv * r * g) * (1.0 + sc) + sh


def _mm(a, w, *, name, nt=False, tm=512, tn=None, out_dtype=F32, norm=None, residual=None):
    t, k = a.shape
    n = w.shape[0] if nt else w.shape[1]
    tn = tn or n
    assert tn == n or (norm is None and residual is None)
    extra_in = list(norm or ()) + list(residual or ())

    def body(*refs):
        a_ref, w_ref = refs[:2]
        ins = refs[2:2 + len(extra_in)]
        outs = refs[2 + len(extra_in):]
        if norm is None:
            av = a_ref[...].astype(BF16)
        else:
            av = _rms_modulate(a_ref[...], ins[0][...], ins[1][...], ins[2][...]).astype(BF16)
            outs[1][...] = av
        if nt:
            acc = lax.dot_general(av, w_ref[...], (((1,), (1,)), ((), ())), preferred_element_type=F32)
        else:
            acc = jnp.dot(av, w_ref[...], preferred_element_type=F32)
        outs[0][...] = acc.astype(out_dtype)
        if residual is not None:
            x_ref, gate_ref = ins[-2:]
            outs[-1][...] = x_ref[...] + gate_ref[...] * acc

    row = lambda width: pl.BlockSpec((1, width), lambda i, j: (0, 0))
    tile = lambda width: pl.BlockSpec((tm, width), lambda i, j: (i, 0))
    w_spec = pl.BlockSpec((tn, k), lambda i, j: (j, 0)) if nt else pl.BlockSpec((k, tn), lambda i, j: (0, j))
    in_specs = [tile(k), w_spec] + ([row(k)] * 3 if norm else []) + ([tile(n), row(n)] if residual else [])
    out_specs = [pl.BlockSpec((tm, tn), lambda i, j: (i, j))] + ([tile(k)] if norm else []) + \
        ([tile(n)] if residual else [])
    out_shape = [jax.ShapeDtypeStruct((t, n), out_dtype)] + \
        ([jax.ShapeDtypeStruct((t, k), BF16)] if norm else []) + \
        ([jax.ShapeDtypeStruct((t, n), F32)] if residual else [])
    outs = pl.pallas_call(
        body, grid=(t // tm, n // tn), in_specs=in_specs, out_specs=out_specs, out_shape=out_shape, name=name,
        compiler_params=_cparams("parallel", "parallel"))(a, w, *extra_in)
    return outs[0] if len(outs) == 1 else outs


def _wgrad(a, b, *, name, tk=None, tn=None, tt=2048, out_dtype=BF16):
    t, k = a.shape
    n = b.shape[1]
    tk = tk or k
    tn = tn or n
    steps = t // tt

    def body(a_ref, b_ref, o_ref, acc_ref):
        s = pl.program_id(2)

        @pl.when(s == 0)
        def _():
            acc_ref[...] = jnp.zeros_like(acc_ref)

        acc_ref[...] += lax.dot_general(a_ref[...].astype(BF16), b_ref[...].astype(BF16),
                                        (((0,), (0,)), ((), ())), preferred_element_type=F32)

        @pl.when(s == steps - 1)
        def _():
            o_ref[...] = acc_ref[...].astype(out_dtype)

    return pl.pallas_call(
        body, grid=(k // tk, n // tn, steps),
        in_specs=[pl.BlockSpec((tt, tk), lambda i, j, s: (s, i)), pl.BlockSpec((tt, tn), lambda i, j, s: (s, j))],
        out_specs=pl.BlockSpec((tk, tn), lambda i, j, s: (i, j)),
        out_shape=jax.ShapeDtypeStruct((k, n), out_dtype),
        scratch_shapes=[pltpu.VMEM((tk, tn), F32)], name=name,
        compiler_params=_cparams("parallel", "parallel", "arbitrary"))(a, b)


def _norm_mod_bwd(x, dh, dres, g, sc, *, name, w=None, tm=512):
    s, d = x.shape
    steps = s // tm

    def body(x_ref, dh_ref, dres_ref, g_ref, sc_ref, *rest):
        w_ref = rest[0] if w is not None else None
        dx_ref, dg_ref, dsc_ref, dsh_ref, da_acc, dsh_acc = rest[-6:]
        i = pl.program_id(0)

        @pl.when(i == 0)
        def _():
            da_acc[...] = jnp.zeros_like(da_acc)
            dsh_acc[...] = jnp.zeros_like(dsh_acc)

        xv = x_ref[...]
        if w is None:
            dhv = dh_ref[...].astype(F32)
        else:
            dhv = jnp.dot(dh_ref[...], w_ref[...], preferred_element_type=F32)
        r = lax.rsqrt(jnp.mean(xv * xv, axis=-1, keepdims=True) + NORM_EPS)
        xhat = xv * r
        gain = g_ref[...] * (1.0 + sc_ref[...])
        dxhat = dhv * gain
        dx_ref[...] = dres_ref[...] + r * (dxhat - xhat * jnp.mean(dxhat * xhat, axis=-1, keepdims=True))
        da_acc[...] += jnp.sum(dhv * xhat, axis=0, keepdims=True)
        dsh_acc[...] += jnp.sum(dhv, axis=0, keepdims=True)

        @pl.when(i == steps - 1)
        def _():
            dg_ref[...] = da_acc[...] * (1.0 + sc_ref[...])
            dsc_ref[...] = da_acc[...] * g_ref[...]
            dsh_ref[...] = dsh_acc[...]

    row = pl.BlockSpec((1, d), lambda i: (0, 0))
    tile = pl.BlockSpec((tm, d), lambda i: (i, 0))
    row_shape = jax.ShapeDtypeStruct((1, d), F32)
    dh_spec = tile if w is None else pl.BlockSpec((tm, dh.shape[1]), lambda i: (i, 0))
    return pl.pallas_call(
        body, grid=(steps,), in_specs=[tile, dh_spec, tile, row, row] + ([] if w is None else [_resident(w.shape)]),
        out_specs=[tile, row, row, row],
        out_shape=[jax.ShapeDtypeStruct((s, d), F32), row_shape, row_shape, row_shape],
        scratch_shapes=[pltpu.VMEM((1, d), F32), pltpu.VMEM((1, d), F32)], name=name,
        compiler_params=_cparams("arbitrary"))(x, dh, dres, g, sc, *([] if w is None else [w]))


def _mid_bwd(x1, dh2, dx2, norm_g, sc, gate, mo, w_out, *, name, tm=512):
    s, d = x1.shape
    steps = s // tm

    def body(x_ref, dh_ref, dres_ref, ng_ref, sc_ref, g_ref, mo_ref, w_ref,
             dx_ref, dng_ref, dsc_ref, dsh_ref, dmix_ref, dmo_ref, dg_ref, da_acc, dsh_acc):
        i = pl.program_id(0)

        @pl.when(i == 0)
        def _():
            da_acc[...] = jnp.zeros_like(da_acc)
            dsh_acc[...] = jnp.zeros_like(dsh_acc)
            dg_ref[...] = jnp.zeros_like(dg_ref)

        xv = x_ref[...]
        dhv = dh_ref[...]
        r = lax.rsqrt(jnp.mean(xv * xv, axis=-1, keepdims=True) + NORM_EPS)
        xhat = xv * r
        dxhat = dhv * (ng_ref[...] * (1.0 + sc_ref[...]))
        dxv = dres_ref[...] + r * (dxhat - xhat * jnp.mean(dxhat * xhat, axis=-1, keepdims=True))
        dx_ref[...] = dxv
        da_acc[...] += jnp.sum(dhv * xhat, axis=0, keepdims=True)
        dsh_acc[...] += jnp.sum(dhv, axis=0, keepdims=True)
        dmo = (g_ref[...] * dxv).astype(BF16)
        dmo_ref[...] = dmo
        dg_ref[...] += jnp.sum(dxv * mo_ref[...], axis=0, keepdims=True)
        dmix_ref[...] = lax.dot_general(dmo, w_ref[...], (((1,), (1,)), ((), ())), preferred_element_type=F32)

        @pl.when(i == steps - 1)
        def _():
            dng_ref[...] = da_acc[...] * (1.0 + sc_ref[...])
            dsc_ref[...] = da_acc[...] * ng_ref[...]
            dsh_ref[...] = dsh_acc[...]

    tile = pl.BlockSpec((tm, d), lambda i: (i, 0))
    row = pl.BlockSpec((1, d), lambda i: (0, 0))
    mix_tile = pl.BlockSpec((tm, w_out.shape[0]), lambda i: (i, 0))
    row_shape = jax.ShapeDtypeStruct((1, d), F32)
    return pl.pallas_call(
        body, grid=(steps,), in_specs=[tile, tile, tile, row, row, row, tile, _resident(w_out.shape)],
        out_specs=[tile, row, row, row, mix_tile, tile, row],
        out_shape=[jax.ShapeDtypeStruct((s, d), F32), row_shape, row_shape, row_shape,
                   jax.ShapeDtypeStruct((s, w_out.shape[0]), F32), jax.ShapeDtypeStruct((s, d), BF16), row_shape],
        scratch_shapes=[pltpu.VMEM((1, d), F32), pltpu.VMEM((1, d), F32)], name=name,
        compiler_params=_cparams("arbitrary"))(x1, dh2, dx2, norm_g, sc, gate, mo, w_out)


def _final_loss(x, g, target, *, name, tm=512):
    s, d = x.shape
    steps = s // tm

    def body(x_ref, g_ref, t_ref, loss_ref, dx_ref, dg_ref, sq_acc):
        i = pl.program_id(0)

        @pl.when(i == 0)
        def _():
            sq_acc[...] = jnp.zeros_like(sq_acc)
            dg_ref[...] = jnp.zeros_like(dg_ref)

        xv = x_ref[...]
        r = lax.rsqrt(jnp.mean(xv * xv, axis=-1, keepdims=True) + NORM_EPS)
        xhat = xv * r
        err = xhat * g_ref[...] - t_ref[...]
        sq_acc[...] += jnp.sum(err * err, axis=0, keepdims=True)
        dy = err * (1.0 / d)
        dg_ref[...] += jnp.sum(dy * xhat, axis=0, keepdims=True)
        dxhat = dy * g_ref[...]
        dx_ref[...] = r * (dxhat - xhat * jnp.mean(dxhat * xhat, axis=-1, keepdims=True))

        @pl.when(i == steps - 1)
        def _():
            total = jnp.sum(sq_acc[...], axis=1, keepdims=True) * (0.5 / d)
            loss_ref[...] = jnp.broadcast_to(total, loss_ref.shape)

    tile = pl.BlockSpec((tm, d), lambda i: (i, 0))
    row = pl.BlockSpec((1, d), lambda i: (0, 0))
    return pl.pallas_call(
        body, grid=(steps,), in_specs=[tile, row, tile],
        out_specs=[pl.BlockSpec((1, 128), lambda i: (0, 0)), tile, row],
        out_shape=[jax.ShapeDtypeStruct((1, 128), F32), jax.ShapeDtypeStruct((s, d), F32),
                   jax.ShapeDtypeStruct((1, d), F32)],
        scratch_shapes=[pltpu.VMEM((1, d), F32)], name=name, compiler_params=_cparams("arbitrary"))(x, g, target)


def _ssd_chunk(z, xbc_ext, dt_raw, conv_w, conv_b, dt_bias, a_log, d_skip, norm_g, h_in):
    q = z.shape[0]
    gw = SSD_HPG * HEAD_LANES
    xc = conv_b
    for k in range(SSD_CONV_K):
        xc = xc + _pick_row(conv_w, k) * _shift_rows(xbc_ext, SSD_CONV_K - 1 - k, CONV_HALO)
    xc = _silu(xc)
    dt = jax.nn.softplus(dt_raw + dt_bias)
    da = dt * (-jnp.exp(a_log))
    ri = lax.broadcasted_iota(jnp.int32, (q, q), 0)
    ci = lax.broadcasted_iota(jnp.int32, (q, q), 1)
    causal = ri >= ci
    tril = causal.astype(F32)
    a_cum = jnp.dot(tril, da, preferred_element_type=F32, precision=lax.Precision.HIGHEST)
    a_cum_t = lax.dot_general(da, tril, (((0,), (1,)), ((), ())), preferred_element_type=F32,
                              precision=lax.Precision.HIGHEST)
    a_last = _pick_row(a_cum, q - 1)
    head = _head_of_lane(gw)
    ys, hs = [], []
    for g in range(2):
        xs = xc[:, gw * g:gw * (g + 1)]
        bm = xc[:, SSD_INNER + SSD_STATE * g:SSD_INNER + SSD_STATE * (g + 1)]
        cm = xc[:, SSD_INNER + 2 * SSD_STATE + SSD_STATE * g:SSD_INNER + 2 * SSD_STATE + SSD_STATE * (g + 1)]
        cb = lax.dot_general(cm.astype(BF16), bm.astype(BF16), (((1,), (1,)), ((), ())), preferred_element_type=F32)
        cols = [_pick_lane(a_cum, SSD_HPG * g + j) for j in range(SSD_HPG)]
        lasts = [_pick_lane(a_last, SSD_HPG * g + j) for j in range(SSD_HPG)]
        dt_exp = sum(jnp.where(head == j, _pick_lane(dt, SSD_HPG * g + j), 0.0) for j in range(SSD_HPG))
        d_exp = sum(jnp.where(head == j, _pick_lane(d_skip, SSD_HPG * g + j), 0.0) for j in range(SSD_HPG))
        e_cum = sum(jnp.where(head == j, jnp.exp(cols[j]), 0.0) for j in range(SSD_HPG))
        c_dec = sum(jnp.where(head == j, jnp.exp(lasts[j]), 0.0) for j in range(SSD_HPG))
        xsdt = (xs * dt_exp).astype(BF16)
        y_diag = jnp.zeros((q, gw), F32)
        st_new = jnp.zeros((SSD_STATE, gw), F32)
        for j in range(SSD_HPG):
            row = _pick_row(a_cum_t, SSD_HPG * g + j)
            lmat = jnp.exp(jnp.where(causal, cols[j] - row, -jnp.inf))
            r = jnp.dot((cb * lmat).astype(BF16), xsdt, preferred_element_type=F32)
            y_diag = y_diag + jnp.where(head == j, r, 0.0)
            bd = (bm * jnp.exp(lasts[j] - cols[j])).astype(BF16)
            st = lax.dot_general(bd, xsdt, (((0,), (0,)), ((), ())), preferred_element_type=F32)
            st_new = st_new + jnp.where(head == j, st, 0.0)
        y_off = jnp.dot(cm.astype(BF16), h_in[g].astype(BF16), preferred_element_type=F32) * e_cum
        hs.append(h_in[g] * c_dec + st_new)
        y = y_diag + y_off + d_exp * xs
        yz = y * _silu(z[:, gw * g:gw * (g + 1)])
        yz = yz * lax.rsqrt(jnp.mean(yz * yz, axis=-1, keepdims=True) + NORM_EPS)
        ys.append(yz * norm_g[:, gw * g:gw * (g + 1)])
    return jnp.concatenate(ys, axis=1), tuple(hs)


_SSD_NCHUNK = SEQ // SSD_CHUNK
_HALO_PER_CHUNK = SSD_CHUNK // CONV_HALO


def _ssd_param_specs(const):
    return [pl.BlockSpec((8, SSD_CONV_CH), const), pl.BlockSpec((1, SSD_CONV_CH), const),
            pl.BlockSpec((1, 128), const), pl.BlockSpec((1, 128), const), pl.BlockSpec((1, 128), const),
            pl.BlockSpec((1, SSD_INNER), const)]


def _ssd_fwd(proj, conv_w, conv_b, dt_bias, a_log, d_skip, norm_g, *, name):
    q = SSD_CHUNK

    def body(z_ref, xbc_ref, halo_ref, dt_ref, cw_ref, cb_ref, db_ref, al_ref, d_ref, ng_ref, y_ref, hs_ref, h_acc):
        i = pl.program_id(0)

        @pl.when(i == 0)
        def _():
            h_acc[...] = jnp.zeros_like(h_acc)

        halo = jnp.where(i == 0, 0.0, halo_ref[...])
        xbc_ext = jnp.concatenate([halo, xbc_ref[...]], axis=0)
        h_in = (h_acc[0], h_acc[1])
        hs_ref[0, 0] = h_in[0]
        hs_ref[0, 1] = h_in[1]
        y, h_out = _ssd_chunk(z_ref[...], xbc_ext, dt_ref[...], cw_ref[...], cb_ref[...], db_ref[...], al_ref[...],
                              d_ref[...], ng_ref[...], h_in)
        y_ref[...] = y.astype(BF16)
        h_acc[0] = h_out[0]
        h_acc[1] = h_out[1]

    const = lambda i: (0, 0)
    return pl.pallas_call(
        body, grid=(_SSD_NCHUNK,),
        in_specs=[pl.BlockSpec((q, SSD_INNER), lambda i: (i, PROJ_Z_BLK)),
                  pl.BlockSpec((q, SSD_CONV_CH), lambda i: (i, 0)),
                  pl.BlockSpec((CONV_HALO, SSD_CONV_CH), lambda i: (jnp.maximum(i * _HALO_PER_CHUNK - 1, 0), 0)),
                  pl.BlockSpec((q, 128), lambda i: (i, PROJ_DT_BLK))] + _ssd_param_specs(const),
        out_specs=[pl.BlockSpec((q, SSD_INNER), lambda i: (i, 0)),
                   pl.BlockSpec((1, 2, SSD_STATE, 256), lambda i: (i, 0, 0, 0))],
        out_shape=[jax.ShapeDtypeStruct((SEQ, D_MODEL), BF16),
                   jax.ShapeDtypeStruct((_SSD_NCHUNK, 2, SSD_STATE, 256), F32)],
        scratch_shapes=[pltpu.VMEM((2, SSD_STATE, 256), F32)], name=name,
        compiler_params=_cparams("arbitrary"))(proj, proj, proj, proj, conv_w, conv_b, dt_bias, a_log, d_skip, norm_g)


def _ssd_bwd(proj, hstates, dmix, conv_w, conv_b, dt_bias, a_log, d_skip, norm_g, *, name):
    q = SSD_CHUNK
    last = _SSD_NCHUNK - 1

    def body(z_ref, xbc_ref, halo_ref, dt_ref, hs_ref, dy_ref, cw_ref, cb_ref, db_ref, al_ref, d_ref, ng_ref,
             dp_ref, dcw_ref, dcb_ref, ddb_ref, dal_ref, dd_ref, dng_ref, dh_acc, dhalo_acc):
        i = pl.program_id(0)

        @pl.when(i == 0)
        def _():
            dh_acc[...] = jnp.zeros_like(dh_acc)
            dhalo_acc[...] = jnp.zeros_like(dhalo_acc)
            for r in (dcw_ref, dcb_ref, ddb_ref, dal_ref, dd_ref, dng_ref):
                r[...] = jnp.zeros_like(r)

        halo = jnp.where(i == last, 0.0, halo_ref[...])
        xbc_ext = jnp.concatenate([halo, xbc_ref[...]], axis=0)
        _, vjp = jax.vjp(_ssd_chunk, z_ref[...], xbc_ext, dt_ref[...], cw_ref[...], cb_ref[...], db_ref[...],
                         al_ref[...], d_ref[...], ng_ref[...], (hs_ref[0, 0], hs_ref[0, 1]))
        gz, gx, gdt, gcw, gcb, gdb, gal, gd, gng, gh = vjp((dy_ref[...], (dh_acc[0], dh_acc[1])))
        dxbc = jnp.concatenate([gx[CONV_HALO:q], gx[q:] + dhalo_acc[...]], axis=0)
        dp_ref[...] = jnp.concatenate([dxbc, gz, gdt, jnp.zeros_like(gdt)], axis=1).astype(BF16)
        dhalo_acc[...] = gx[:CONV_HALO]
        dh_acc[0] = gh[0]
        dh_acc[1] = gh[1]
        dcw_ref[...] += gcw
        dcb_ref[...] += gcb
        ddb_ref[...] += gdb
        dal_ref[...] += gal
        dd_ref[...] += gd
        dng_ref[...] += gng

    const = lambda i: (0, 0)
    rev = lambda i: last - i
    row = lambda n: jax.ShapeDtypeStruct((1, n), F32)
    return pl.pallas_call(
        body, grid=(_SSD_NCHUNK,),
        in_specs=[pl.BlockSpec((q, SSD_INNER), lambda i: (rev(i), PROJ_Z_BLK)),
                  pl.BlockSpec((q, SSD_CONV_CH), lambda i: (rev(i), 0)),
                  pl.BlockSpec((CONV_HALO, SSD_CONV_CH), lambda i: (jnp.maximum(rev(i) * _HALO_PER_CHUNK - 1, 0), 0)),
                  pl.BlockSpec((q, 128), lambda i: (rev(i), PROJ_DT_BLK)),
                  pl.BlockSpec((1, 2, SSD_STATE, 256), lambda i: (rev(i), 0, 0, 0)),
                  pl.BlockSpec((q, SSD_INNER), lambda i: (rev(i), 0))] + _ssd_param_specs(const),
        out_specs=[pl.BlockSpec((q, PROJ_SSD_W), lambda i: (rev(i), 0))] + _ssd_param_specs(const),
        out_shape=[jax.ShapeDtypeStruct((SEQ, PROJ_W), BF16), jax.ShapeDtypeStruct((8, SSD_CONV_CH), F32),
                   row(SSD_CONV_CH), row(128), row(128), row(128), row(SSD_INNER)],
        scratch_shapes=[pltpu.VMEM((2, SSD_STATE, 256), F32), pltpu.VMEM((CONV_HALO, SSD_CONV_CH), F32)], name=name,
        compiler_params=_cparams("arbitrary"))(proj, proj, proj, proj, hstates, dmix, conv_w, conv_b, dt_bias, a_log,
                                                d_skip, norm_g)


def _rope_tables(pos_col, inv_freq_lane, *, name):
    s = pos_col.shape[0]

    def body(p_ref, f_ref, c_ref, s1_ref, s2_ref):
        ang = p_ref[...] * f_ref[...]
        within = lax.broadcasted_iota(jnp.int32, ang.shape, 1) % HEAD_LANES
        half = ROT_DIM // 2
        c_ref[...] = jnp.where(within < ROT_DIM, jnp.cos(ang), 1.0)
        sn = jnp.sin(ang)
        s1_ref[...] = jnp.where(within < half, -sn, 0.0)
        s2_ref[...] = jnp.where((within >= half) & (within < ROT_DIM), sn, 0.0)

    shp = jax.ShapeDtypeStruct((s, 128), F32)
    return pl.pallas_call(body, out_shape=[shp, shp, shp], name=name,
                          compiler_params=pltpu.CompilerParams(vmem_limit_bytes=VMEM_LIMIT))(pos_col, inv_freq_lane)


def _rope(t, c, s1, s2):
    half = ROT_DIM // 2
    return t * c + pltpu.roll(t, 128 - half, 1) * s1 + pltpu.roll(t, half, 1) * s2


def _rope_t(g, c, s1, s2):
    half = ROT_DIM // 2
    return g * c + pltpu.roll(g * s1, half, 1) + pltpu.roll(g * s2, 128 - half, 1)


def _att_valid(b):
    qi = lax.broadcasted_iota(jnp.int32, (ATT_BLOCK, 2 * ATT_BLOCK), 0)
    kj = lax.broadcasted_iota(jnp.int32, (ATT_BLOCK, 2 * ATT_BLOCK), 1)
    rel = qi + ATT_BLOCK - kj
    return (rel >= 0) & (rel <= ATT_BLOCK) & (b * ATT_BLOCK + kj - ATT_BLOCK >= 0)


def _att_slices(i, d):
    if d == 1:
        qstart = pl.multiple_of(i * ATT_BLOCK, ATT_BLOCK)
        return i, pl.ds(qstart, ATT_BLOCK), pl.ds(pl.multiple_of(qstart - ATT_BLOCK + ATT_KPAD, ATT_BLOCK), 2 * ATT_BLOCK)
    r = i % d
    b = i // d
    qstart = r + d * ATT_BLOCK * b
    return b, pl.ds(qstart, ATT_BLOCK, stride=d), pl.ds(qstart - ATT_BLOCK * d + ATT_KPAD, 2 * ATT_BLOCK, stride=d)


_ATT_NBLK = SEQ // ATT_BLOCK
_ATT_SCALE = HEAD_LANES ** -0.5
_ATT_UNROLL_FWD = 8
_ATT_UNROLL = 4


def _att_fwd(proj, cos, sin1, sin2, mix, *, name):
    s = SEQ

    def body(q_ref, k_ref, v_ref, c_ref, s1_ref, s2_ref, _, o_ref, lse_ref, mix_ref, qs, ks, vs, acc, m_s, l_s):
        c, s1, s2 = c_ref[...], s1_ref[...], s2_ref[...]
        qs[...] = _rope(q_ref[...], c, s1, s2) * _ATT_SCALE
        zeros = jnp.zeros((ATT_KPAD, 128), F32)
        ks[pl.ds(0, ATT_KPAD), :] = zeros
        vs[pl.ds(0, ATT_KPAD), :] = zeros
        ks[pl.ds(ATT_KPAD, s), :] = _rope(k_ref[...], c, s1, s2)
        vs[pl.ds(ATT_KPAD, s), :] = v_ref[...]
        head0 = _head_of_lane(128) == 0

        for bi, (_, d) in enumerate(ATT_PATTERNS):
            def blk(i, carry, d=d, first=(bi == 0)):
                b, sq, sk = _att_slices(i, d)
                qb = qs[sq, :]
                kw = ks[sk, :].astype(BF16)
                vw = vs[sk, :].astype(BF16)
                valid = _att_valid(b)
                ms, ls, os_ = [], [], []
                for hh in range(2):
                    qh = jnp.where(head0 if hh == 0 else ~head0, qb, 0.0).astype(BF16)
                    sc = lax.dot_general(qh, kw, (((1,), (1,)), ((), ())), preferred_element_type=F32)
                    sc = jnp.where(valid, sc, -jnp.inf)
                    mb = jnp.max(sc, axis=1, keepdims=True)
                    p = jnp.exp(sc - mb)
                    ms.append(mb)
                    ls.append(jnp.sum(p, axis=1, keepdims=True))
                    os_.append(jnp.dot(p.astype(BF16), vw, preferred_element_type=F32))
                m_b = jnp.where(head0, ms[0], ms[1])
                l_b = jnp.where(head0, ls[0], ls[1])
                o_b = jnp.where(head0, os_[0], os_[1])
                if first:
                    m_s[sq, :] = m_b
                    l_s[sq, :] = l_b
                    acc[sq, :] = o_b
                else:
                    m_old = m_s[sq, :]
                    m_new = jnp.maximum(m_old, m_b)
                    a_old = jnp.exp(m_old - m_new)
                    a_b = jnp.exp(m_b - m_new)
                    m_s[sq, :] = m_new
                    l_s[sq, :] = l_s[sq, :] * a_old + l_b * a_b
                    acc[sq, :] = acc[sq, :] * a_old + o_b * a_b
                return carry

            lax.fori_loop(0, _ATT_NBLK, blk, 0, unroll=_ATT_UNROLL_FWD)

        out = acc[...] / l_s[...]
        o_ref[...] = out
        mix_ref[...] = out.astype(BF16)
        lse_ref[...] = m_s[...] + jnp.log(l_s[...])

    col = lambda base: pl.BlockSpec((s, 128), lambda p: (0, base + p))
    tab = pl.BlockSpec((s, 128), lambda p: (0, 0))
    big = pltpu.VMEM((ATT_KPAD + s, 128), F32)
    tok = pltpu.VMEM((s, 128), F32)
    return pl.pallas_call(
        body, grid=(2,), in_specs=[col(PROJ_Q_BLK), col(PROJ_K_BLK), col(PROJ_V_BLK), tab, tab, tab, _HBM],
        out_specs=[pl.BlockSpec((s, 128), lambda p: (0, p)), pl.BlockSpec((s, 128), lambda p: (0, p)),
                   col(MIX_ATT_BLK)],
        out_shape=[jax.ShapeDtypeStruct((s, ATT_W), F32), jax.ShapeDtypeStruct((s, ATT_W), F32),
                   jax.ShapeDtypeStruct(mix.shape, mix.dtype)],
        input_output_aliases={6: 2}, scratch_shapes=[tok, big, big, tok, tok, tok], name=name,
        compiler_params=_cparams("arbitrary"))(proj, proj, proj, cos, sin1, sin2, mix)


def _att_bwd(proj, cos, sin1, sin2, out, lse, dmix, dproj, *, name):
    s = SEQ

    def body(proj_ref, c_hbm, s1_hbm, s2_hbm, out_hbm, lse_hbm, dmix_hbm, _, dproj_hbm,
             c_ref, s1_ref, s2_ref, o_ref, lse_ref, do_ref, qs, ks, vs, dqs, dks, dvs, staged, sems):
        def start(copies):
            for cp in copies:
                cp.start()
            return copies

        def load(pair):
            lanes = pl.ds(128 * pair, 128)
            rows = pl.ds(ATT_KPAD, s)
            return start([
                pltpu.make_async_copy(proj_ref.at[:, pl.ds(128 * (PROJ_Q_BLK + pair), 128)], qs, sems.at[0]),
                pltpu.make_async_copy(proj_ref.at[:, pl.ds(128 * (PROJ_K_BLK + pair), 128)], ks.at[rows, :], sems.at[1]),
                pltpu.make_async_copy(proj_ref.at[:, pl.ds(128 * (PROJ_V_BLK + pair), 128)], vs.at[rows, :], sems.at[2]),
                pltpu.make_async_copy(out_hbm.at[:, lanes], o_ref, sems.at[3]),
                pltpu.make_async_copy(lse_hbm.at[:, lanes], lse_ref, sems.at[4]),
                pltpu.make_async_copy(dmix_hbm.at[:, pl.ds(128 * (MIX_ATT_BLK + pair), 128)], do_ref, sems.at[5])])

        tables = start([pltpu.make_async_copy(c_hbm, c_ref, sems.at[6]),
                        pltpu.make_async_copy(s1_hbm, s1_ref, sems.at[7]),
                        pltpu.make_async_copy(s2_hbm, s2_ref, sems.at[8])])
        loads = load(0)
        for cp in tables:
            cp.wait()
        head0 = _head_of_lane(128) == 0
        zeros = jnp.zeros((ATT_KPAD, 128), F32)
        for pair in range(2):
            for cp in loads:
                cp.wait()
            c, s1, s2 = c_ref[...], s1_ref[...], s2_ref[...]
            qs[...] = _rope(qs[...], c, s1, s2) * _ATT_SCALE
            ks[pl.ds(0, ATT_KPAD), :] = zeros
            vs[pl.ds(0, ATT_KPAD), :] = zeros
            ks[pl.ds(ATT_KPAD, s), :] = _rope(ks[pl.ds(ATT_KPAD, s), :], c, s1, s2)
            dqs[...] = jnp.zeros_like(dqs)
            dks[...] = jnp.zeros_like(dks)
            dvs[...] = jnp.zeros_like(dvs)

            for _, d in ATT_PATTERNS:
                def blk(i, carry, d=d):
                    b, sq, sk = _att_slices(i, d)
                    qb = qs[sq, :]
                    kw = ks[sk, :].astype(BF16)
                    vw = vs[sk, :].astype(BF16)
                    dob = do_ref[sq, :]
                    lse_b = lse_ref[sq, :]
                    dd = dob * o_ref[sq, :]
                    valid = _att_valid(b)
                    dq_b = jnp.zeros((ATT_BLOCK, 128), F32)
                    dk_w = jnp.zeros((2 * ATT_BLOCK, 128), F32)
                    dv_w = jnp.zeros((2 * ATT_BLOCK, 128), F32)
                    for hh in range(2):
                        hm = head0 if hh == 0 else ~head0
                        qh = jnp.where(hm, qb, 0.0).astype(BF16)
                        doh = jnp.where(hm, dob, 0.0).astype(BF16)
                        lse_h = _pick_lane(lse_b, hh * HEAD_LANES)
                        d_h = jnp.sum(jnp.where(hm, dd, 0.0), axis=1, keepdims=True)
                        sc = lax.dot_general(qh, kw, (((1,), (1,)), ((), ())), preferred_element_type=F32)
                        p = jnp.where(valid, jnp.exp(sc - lse_h), 0.0)
                        dp = lax.dot_general(doh, vw, (((1,), (1,)), ((), ())), preferred_element_type=F32)
                        ds = (p * (dp - d_h)).astype(BF16)
                        dq_b = dq_b + jnp.where(hm, jnp.dot(ds, kw, preferred_element_type=F32), 0.0)
                        dk_w = dk_w + lax.dot_general(ds, qh, (((0,), (0,)), ((), ())), preferred_element_type=F32)
                        dv_w = dv_w + lax.dot_general(p.astype(BF16), doh, (((0,), (0,)), ((), ())),
                                                      preferred_element_type=F32)
                    dqs[sq, :] += dq_b
                    dks[sk, :] += dk_w
                    dvs[sk, :] += dv_w
                    return carry

                lax.fori_loop(0, _ATT_NBLK, blk, 0, unroll=_ATT_UNROLL)

            staged[0] = _rope_t(dqs[...] * _ATT_SCALE, c, s1, s2).astype(BF16)
            staged[1] = _rope_t(dks[pl.ds(ATT_KPAD, s), :], c, s1, s2).astype(BF16)
            staged[2] = dvs[pl.ds(ATT_KPAD, s), :].astype(BF16)
            stores = start([
                pltpu.make_async_copy(staged.at[j], dproj_hbm.at[:, pl.ds(128 * (col + pair), 128)], sems.at[9 + j])
                for j, col in enumerate((PROJ_Q_BLK, PROJ_K_BLK, PROJ_V_BLK))])
            if pair == 0:
                loads = load(1)
            for cp in stores:
                cp.wait()

    big = pltpu.VMEM((ATT_KPAD + s, 128), F32)
    tok = pltpu.VMEM((s, 128), F32)
    return pl.pallas_call(
        body, in_specs=[_HBM] * 8, out_specs=_HBM, out_shape=jax.ShapeDtypeStruct(dproj.shape, dproj.dtype),
        input_output_aliases={7: 0},
        scratch_shapes=[tok] * 6 + [tok, big, big, tok, big, big, pltpu.VMEM((3, s, 128), BF16),
                                    pltpu.SemaphoreType.DMA((12,))], name=name,
        compiler_params=pltpu.CompilerParams(vmem_limit_bytes=VMEM_LIMIT))(
            proj, cos, sin1, sin2, out, lse, dmix, dproj)


_POOL_TM = 512
_POOL_NT = SEQ // _POOL_TM
_POOL_HALO_PER_TILE = _POOL_TM // POOL_HALO


def _pool_tile(u_ext, w_bd, scale, t0):
    s2 = u_ext + _roll_rows(u_ext, 1)
    s4 = s2 + _roll_rows(s2, 2)
    s8 = s4 + _roll_rows(s4, 4)
    s16 = s8 + _roll_rows(s8, 8)
    grp = _head_of_lane(POOL_W)
    sel = jnp.where(grp == 0, s2, jnp.where(grp == 1, s4, jnp.where(grp == 2, s8, s16)))[POOL_HALO:]
    t = sel.shape[0]
    pos = t0 + lax.broadcasted_iota(jnp.int32, (t, POOL_W), 0) + 1
    win = jnp.where(grp == 0, 2, jnp.where(grp == 1, 4, jnp.where(grp == 2, 8, 16)))
    cnt = jnp.minimum(pos, win).astype(F32)
    diff = sel / cnt - u_ext[POOL_HALO:]
    return jnp.dot(diff.astype(BF16), w_bd.astype(BF16), preferred_element_type=F32) * scale


def _pool_fwd(proj, w_bd, scale, mix, *, name):
    tm = _POOL_TM

    def body(u_ref, halo_ref, w_ref, sc_ref, _, y_ref):
        i = pl.program_id(0)
        halo = jnp.where(i == 0, 0.0, halo_ref[...])
        u_ext = jnp.concatenate([halo, u_ref[...]], axis=0)
        y_ref[...] = _pool_tile(u_ext, w_ref[...], sc_ref[...], i * tm).astype(BF16)

    return pl.pallas_call(
        body, grid=(_POOL_NT,),
        in_specs=[pl.BlockSpec((tm, POOL_W), lambda i: (i, PROJ_POOL_BLK)),
                  pl.BlockSpec((POOL_HALO, POOL_W),
                               lambda i: (jnp.maximum(i * _POOL_HALO_PER_TILE - 1, 0), PROJ_POOL_BLK)),
                  pl.BlockSpec((POOL_W, POOL_W), lambda i: (0, 0)), pl.BlockSpec((1, POOL_W), lambda i: (0, 0)), _HBM],
        out_specs=pl.BlockSpec((tm, POOL_W), lambda i: (i, MIX_POOL_BLK)),
        out_shape=jax.ShapeDtypeStruct(mix.shape, mix.dtype), input_output_aliases={4: 0}, name=name,
        compiler_params=_cparams("parallel"))(proj, proj, w_bd, scale, mix)


def _pool_bwd(proj, dmix, w_bd, scale, dproj, *, name):
    tm = _POOL_TM
    last = _POOL_NT - 1

    def body(u_ref, halo_ref, dy_ref, w_ref, sc_ref, _, du_ref, dw_ref, dsc_ref, dhalo_acc):
        i = pl.program_id(0)

        @pl.when(i == 0)
        def _():
            dhalo_acc[...] = jnp.zeros_like(dhalo_acc)
            dw_ref[...] = jnp.zeros_like(dw_ref)
            dsc_ref[...] = jnp.zeros_like(dsc_ref)

        tile = last - i
        halo = jnp.where(tile == 0, 0.0, halo_ref[...])
        u_ext = jnp.concatenate([halo, u_ref[...]], axis=0)
        _, vjp = jax.vjp(functools.partial(_pool_tile, t0=tile * tm), u_ext, w_ref[...], sc_ref[...])
        gu, gw, gs = vjp(dy_ref[...])
        du_ref[...] = jnp.concatenate([gu[POOL_HALO:tm], gu[tm:] + dhalo_acc[...]], axis=0).astype(BF16)
        dhalo_acc[...] = gu[:POOL_HALO]
        dw_ref[...] += gw
        dsc_ref[...] += gs

    rev = lambda i: last - i
    return pl.pallas_call(
        body, grid=(_POOL_NT,),
        in_specs=[pl.BlockSpec((tm, POOL_W), lambda i: (rev(i), PROJ_POOL_BLK)),
                  pl.BlockSpec((POOL_HALO, POOL_W),
                               lambda i: (jnp.maximum(rev(i) * _POOL_HALO_PER_TILE - 1, 0), PROJ_POOL_BLK)),
                  pl.BlockSpec((tm, POOL_W), lambda i: (rev(i), MIX_POOL_BLK)),
                  pl.BlockSpec((POOL_W, POOL_W), lambda i: (0, 0)), pl.BlockSpec((1, POOL_W), lambda i: (0, 0)), _HBM],
        out_specs=[pl.BlockSpec((tm, POOL_W), lambda i: (rev(i), PROJ_POOL_BLK)),
                   pl.BlockSpec((POOL_W, POOL_W), lambda i: (0, 0)), pl.BlockSpec((1, POOL_W), lambda i: (0, 0))],
        out_shape=[jax.ShapeDtypeStruct(dproj.shape, dproj.dtype), jax.ShapeDtypeStruct((POOL_W, POOL_W), F32),
                   jax.ShapeDtypeStruct((1, POOL_W), F32)],
        input_output_aliases={5: 0}, scratch_shapes=[pltpu.VMEM((POOL_HALO, POOL_W), F32)], name=name,
        compiler_params=_cparams("arbitrary"))(proj, proj, dmix, w_bd, scale, dproj)


_FFN_TM = 256
_FFN_NT = SEQ // _FFN_TM
_FFN_HALO_PER_TILE = _FFN_TM // CONV_HALO


def _ffn_act_tile(hid_ext, conv_w, conv_b):
    hc = conv_b
    for k in range(FFN_CONV_K):
        hc = hc + _pick_row(conv_w, k) * _shift_rows(hid_ext, FFN_CONV_K - 1 - k, CONV_HALO)
    return _silu(hc[:, :FFN_DIM]) * hc[:, FFN_DIM:]


def _ffn_fwd(x1, norm_g, sc, sh, gate, up_t, down, conv_w, conv_b, *, name):
    tm = _FFN_TM
    w = 2 * FFN_DIM
    d = D_MODEL

    def body(x_ref, ng_ref, sc_ref, sh_ref, g_ref, up_ref, dn_ref, cw_ref, cb_ref,
             h_ref, hid_ref, act_ref, f_ref, x2_ref, halo_acc):
        i = pl.program_id(0)
        h2 = _rms_modulate(x_ref[...], ng_ref[...], sc_ref[...], sh_ref[...]).astype(BF16)
        h_ref[...] = h2
        hid = lax.dot_general(h2, up_ref[...], (((1,), (1,)), ((), ())), preferred_element_type=F32)
        hid_ref[...] = hid
        halo = jnp.where(i == 0, 0.0, halo_acc[...])
        act = _ffn_act_tile(jnp.concatenate([halo, hid], axis=0), cw_ref[...], cb_ref[...]).astype(BF16)
        halo_acc[...] = hid[tm - CONV_HALO:]
        act_ref[...] = act
        f = jnp.dot(act, dn_ref[...], preferred_element_type=F32)
        f_ref[...] = f
        x2_ref[...] = x_ref[...] + g_ref[...] * f

    tile = lambda n: pl.BlockSpec((tm, n), lambda i: (i, 0))
    return pl.pallas_call(
        body, grid=(_FFN_NT,),
        in_specs=[tile(d)] + [_resident((1, d))] * 4 + [_resident((w, d)), _resident((FFN_DIM, d)),
                                                        _resident((8, w)), _resident((1, w))],
        out_specs=[tile(d), tile(w), tile(FFN_DIM), tile(d), tile(d)],
        out_shape=[jax.ShapeDtypeStruct((SEQ, d), BF16), jax.ShapeDtypeStruct((SEQ, w), F32),
                   jax.ShapeDtypeStruct((SEQ, FFN_DIM), BF16), jax.ShapeDtypeStruct((SEQ, d), F32),
                   jax.ShapeDtypeStruct((SEQ, d), F32)],
        scratch_shapes=[pltpu.VMEM((CONV_HALO, w), F32)], name=name,
        compiler_params=_cparams("arbitrary"))(x1, norm_g, sc, sh, gate, up_t, down, conv_w, conv_b)


def _ffn_bwd(dx2, gate, f, hid, up_t, down, conv_w, conv_b, *, name):
    tm = _FFN_TM
    w = 2 * FFN_DIM
    d = D_MODEL
    last = _FFN_NT - 1

    def body(dx_ref, g_ref, f_ref, h_ref, halo_ref, up_ref, dn_ref, cw_ref, cb_ref,
             df_ref, dg_ref, dh_ref, dh2_ref, dcw_ref, dcb_ref, dhalo_acc):
        i = pl.program_id(0)

        @pl.when(i == 0)
        def _():
            dhalo_acc[...] = jnp.zeros_like(dhalo_acc)
            dcw_ref[...] = jnp.zeros_like(dcw_ref)
            dcb_ref[...] = jnp.zeros_like(dcb_ref)
            dg_ref[...] = jnp.zeros_like(dg_ref)

        dxv = dx_ref[...]
        df = (g_ref[...] * dxv).astype(BF16)
        df_ref[...] = df
        dg_ref[...] += jnp.sum(dxv * f_ref[...], axis=0, keepdims=True)
        dact = lax.dot_general(df, dn_ref[...], (((1,), (1,)), ((), ())), preferred_element_type=F32)
        halo = jnp.where(i == last, 0.0, halo_ref[...])
        hid_ext = jnp.concatenate([halo, h_ref[...]], axis=0)
        _, vjp = jax.vjp(_ffn_act_tile, hid_ext, cw_ref[...], cb_ref[...])
        gh, gw, gb = vjp(dact)
        dhid = jnp.concatenate([gh[CONV_HALO:tm], gh[tm:] + dhalo_acc[...]], axis=0).astype(BF16)
        dhalo_acc[...] = gh[:CONV_HALO]
        dh_ref[...] = dhid
        dh2_ref[...] = jnp.dot(dhid, up_ref[...], preferred_element_type=F32)
        dcw_ref[...] += gw
        dcb_ref[...] += gb

    rev = lambda i: last - i
    tile = lambda n: pl.BlockSpec((tm, n), lambda i: (rev(i), 0))
    acc = lambda shape: pl.BlockSpec(shape, lambda i: (0, 0))
    return pl.pallas_call(
        body, grid=(_FFN_NT,),
        in_specs=[tile(d), _resident((1, d)), tile(d), tile(w),
                  pl.BlockSpec((CONV_HALO, w), lambda i: (jnp.maximum(rev(i) * _FFN_HALO_PER_TILE - 1, 0), 0)),
                  _resident((w, d)), _resident((FFN_DIM, d)), _resident((8, w)), _resident((1, w))],
        out_specs=[tile(d), acc((1, d)), tile(w), tile(d), acc((8, w)), acc((1, w))],
        out_shape=[jax.ShapeDtypeStruct((SEQ, d), BF16), jax.ShapeDtypeStruct((1, d), F32),
                   jax.ShapeDtypeStruct((SEQ, w), BF16), jax.ShapeDtypeStruct((SEQ, d), F32),
                   jax.ShapeDtypeStruct((8, w), F32), jax.ShapeDtypeStruct((1, w), F32)],
        scratch_shapes=[pltpu.VMEM((CONV_HALO, w), F32)], name=name,
        compiler_params=_cparams("arbitrary"))(dx2, gate, f, hid, hid, up_t, down, conv_w, conv_b)


def _axes():
    return lax.axis_index("x"), lax.axis_index("y"), lax.axis_index("c")


def _handshake(peers):
    barrier = pltpu.get_barrier_semaphore()
    for peer in peers:
        pl.semaphore_signal(barrier, inc=1, device_id=peer, device_id_type=MESH)
    pl.semaphore_wait(barrier, len(peers))


def _allgather_body(x_refs, out_refs, send_sems, recv_sems, local_sems):
    n = len(x_refs)
    x, y, c = _axes()
    me, sibling = (x, y, c), (x, y, 1 - c)
    chips = [(1 - x, y), (x, 1 - y), (1 - x, 1 - y)]
    _handshake([sibling] + [(*chip, c) for chip in chips])

    def slot(a, px, py, pc):
        return out_refs[a].at[4 * px + 2 * py + pc]

    def copy(a, k, block, to, src=None):
        return pltpu.make_async_remote_copy(
            src_ref=slot(a, *block) if src is None else src, dst_ref=slot(a, *block),
            send_sem=send_sems.at[a, k], recv_sem=recv_sems.at[a, k], device_id=to, device_id_type=MESH)

    mines, firsts = [], []
    for a in range(n):
        mines.append(pltpu.make_async_copy(x_refs[a], slot(a, *me), local_sems.at[a]))
        mines[-1].start()
        first = [copy(a, 0, me, sibling, src=x_refs[a])]
        first += [copy(a, 1 + j, me, (*chip, c), src=x_refs[a]) for j, chip in enumerate(chips)]
        for cp in first:
            cp.start()
        firsts += first
    passed = []
    for j, chip in enumerate(chips):
        for a in range(n):
            copy(a, 1 + j, (*chip, c), me).wait_recv()
            passed.append(copy(a, 4 + j, (*chip, c), sibling))
            passed[-1].start()
    for a in range(n):
        copy(a, 0, sibling, me).wait_recv()
    for j, chip in enumerate(chips):
        for a in range(n):
            copy(a, 4 + j, (*chip, 1 - c), me).wait_recv()
    for cp in firsts + passed:
        cp.wait_send()
    for cp in mines:
        cp.wait()


def _allgather_sems(n):
    return [pltpu.SemaphoreType.DMA((n, 7)), pltpu.SemaphoreType.DMA((n, 7)), pltpu.SemaphoreType.DMA((n,))]


def _allgather_small(xs, *, name):
    n = len(xs)

    def body(*refs):
        x_refs, out_refs = refs[:n], refs[n:2 * n]
        send_sems, recv_sems, local_sems = refs[2 * n:]
        x, y, c = _axes()
        mine = 4 * x + 2 * y + c
        flip = lambda v, bit: 1 - v if bit else v
        peers = [(flip(x, k & 4), flip(y, k & 2), flip(c, k & 1)) for k in range(1, N_DEV)]
        copies, local = [], []
        for a in range(n):
            local.append(pltpu.make_async_copy(x_refs[a], out_refs[a].at[mine], local_sems.at[a]))
            copies += [pltpu.make_async_remote_copy(
                src_ref=x_refs[a], dst_ref=out_refs[a].at[mine], send_sem=send_sems.at[a, k],
                recv_sem=recv_sems.at[a, k], device_id=peer, device_id_type=MESH) for k, peer in enumerate(peers)]
        for cp in local + copies:
            cp.start()
        for cp in copies:
            cp.wait_recv()
        for cp in copies:
            cp.wait_send()
        for cp in local:
            cp.wait()

    return pl.pallas_call(
        body, out_shape=[jax.ShapeDtypeStruct((N_DEV,) + xb.shape, xb.dtype) for xb in xs],
        in_specs=[_HBM] * n, out_specs=[_HBM] * n, scratch_shapes=_allgather_sems(n), name=name)(*xs)


def _allgather_async(xs, *, name, collective_id):
    n = len(xs)
    x_refs = [jax.new_ref(xb, memory_space=pltpu.MemorySpace.HBM) for xb in xs]
    out_refs = [jax.empty_ref(jax.ShapeDtypeStruct((N_DEV,) + xb.shape, xb.dtype), memory_space=pltpu.MemorySpace.HBM)
                for xb in xs]

    @pl.kernel(mesh=plsc.ScalarSubcoreMesh(axis_name="sequencer", num_cores=1), name=name,
               scratch_types=tuple(_allgather_sems(n)),
               compiler_params=pltpu.CompilerParams(collective_id=collective_id))
    def launch(send_sems, recv_sems, local_sems):
        _allgather_body(x_refs, out_refs, send_sems, recv_sems, local_sems)

    launch()
    return [r[...] for r in out_refs]


def _pair_exchange(blocks, *, name, collective_id):
    n = len(blocks)
    hbm = pltpu.MemorySpace.HBM
    in_refs = [jax.new_ref(b, memory_space=hbm) for b in blocks]
    out_refs = [jax.empty_ref(jax.ShapeDtypeStruct((4,) + b.shape[1:], b.dtype), memory_space=hbm) for b in blocks]

    @pl.kernel(mesh=plsc.ScalarSubcoreMesh(axis_name="sequencer", num_cores=1), name=name,
               scratch_types=(pltpu.SemaphoreType.DMA((n, 4)), pltpu.SemaphoreType.DMA((n, 4))),
               compiler_params=pltpu.CompilerParams(collective_id=collective_id))
    def launch(send_sems, recv_sems):
        x, y, c = _axes()
        _handshake([(x, y, 1 - c)])
        copies = [pltpu.make_async_remote_copy(
            src_ref=in_refs[a].at[2 * s + (1 - c)], dst_ref=out_refs[a].at[s], send_sem=send_sems.at[a, s],
            recv_sem=recv_sems.at[a, s], device_id=(x, y, 1 - c), device_id_type=MESH)
            for a in range(n) for s in range(4)]
        for cp in copies:
            cp.start()
        for cp in copies:
            cp.wait_recv()
        for cp in copies:
            cp.wait_send()

    launch()
    return [r[...] for r in out_refs]


def _chip_exchange(parts, *, name, collective_id):
    n = len(parts)
    hbm = pltpu.MemorySpace.HBM
    in_refs = [jax.new_ref(p, memory_space=hbm) for p in parts]
    out_refs = [jax.empty_ref(jax.ShapeDtypeStruct(p.shape, p.dtype), memory_space=hbm) for p in parts]

    @pl.kernel(mesh=plsc.ScalarSubcoreMesh(axis_name="sequencer", num_cores=1), name=name,
               scratch_types=(pltpu.SemaphoreType.DMA((n, 3)), pltpu.SemaphoreType.DMA((n, 3)),
                              pltpu.SemaphoreType.DMA((n,))),
               compiler_params=pltpu.CompilerParams(collective_id=collective_id))
    def launch(send_sems, recv_sems, local_sems):
        x, y, c = _axes()
        my_chip = 2 * x + y
        chips = [(1 - x, y), (x, 1 - y), (1 - x, 1 - y)]
        _handshake([(*chip, c) for chip in chips])
        locals_ = [pltpu.make_async_copy(in_refs[a].at[my_chip], out_refs[a].at[my_chip], local_sems.at[a])
                   for a in range(n)]
        for cp in locals_:
            cp.start()
        copies = [pltpu.make_async_remote_copy(
            src_ref=in_refs[a].at[2 * px + py], dst_ref=out_refs[a].at[my_chip], send_sem=send_sems.at[a, k],
            recv_sem=recv_sems.at[a, k], device_id=(px, py, c), device_id_type=MESH)
            for a in range(n) for k, (px, py) in enumerate(chips)]
        for cp in copies:
            cp.start()
        for cp in copies:
            cp.wait_recv()
        for cp in copies:
            cp.wait_send()
        for cp in locals_:
            cp.wait()

    launch()
    return [r[...] for r in out_refs]


def _pair_sum(core, blocks, from_sibling, *, name):
    n = len(blocks)

    def body(core_ref, *refs):
        for a_ref, b_ref, o_ref in zip(refs[:n], refs[n:2 * n], refs[2 * n:]):
            o_ref[...] = (a_ref[...].astype(F32) + b_ref[...].astype(F32)).astype(o_ref.dtype)

    mine = lambda b: pl.BlockSpec((1,) + b.shape[1:], lambda s, core_ref: (2 * s + core_ref[0], 0, 0))
    slot = lambda b: pl.BlockSpec((1,) + b.shape[1:], lambda s, core_ref: (s, 0, 0))
    return pl.pallas_call(
        body,
        grid_spec=pltpu.PrefetchScalarGridSpec(
            num_scalar_prefetch=1, grid=(4,),
            in_specs=[mine(b) for b in blocks] + [slot(b) for b in blocks], out_specs=[slot(b) for b in blocks]),
        out_shape=[jax.ShapeDtypeStruct(s.shape, s.dtype) for s in from_sibling], name=name,
        compiler_params=_cparams("parallel"))(core, *blocks, *from_sibling)


def _sum_gathered(gathered, *, name):
    n = len(gathered)

    def body(*refs):
        for a_ref, o_ref in zip(refs[:n], refs[n:]):
            acc = a_ref[0].astype(F32)
            for k in range(1, a_ref.shape[0]):
                acc = acc + a_ref[k].astype(F32)
            o_ref[...] = acc

    return pl.pallas_call(body, out_shape=[jax.ShapeDtypeStruct(g.shape[1:], F32) for g in gathered], name=name,
                          compiler_params=pltpu.CompilerParams(vmem_limit_bytes=VMEM_LIMIT))(*gathered)


_ADA_SHARD = 6 * D_MODEL // N_DEV


def _ada_mod(c_all, ada_w, *, name):
    def body(c_ref, w_ref, o_ref):
        o_ref[0] = jnp.dot(_silu(c_ref[...]).astype(BF16), w_ref[0].astype(BF16), preferred_element_type=F32)

    return pl.pallas_call(
        body, grid=(DEPTH,),
        in_specs=[pl.BlockSpec((N_DEV, D_MODEL), lambda l: (0, 0)),
                  pl.BlockSpec((1, D_MODEL, _ADA_SHARD), lambda l: (l, 0, 0))],
        out_specs=pl.BlockSpec((1, N_DEV, _ADA_SHARD), lambda l: (l, 0, 0)),
        out_shape=jax.ShapeDtypeStruct((DEPTH, N_DEV, _ADA_SHARD), F32), name=name,
        compiler_params=_cparams("parallel"))(c_all, ada_w)


def _ada_wgrad(c_all, dmod_cols, *, name):
    def body(c_ref, d_ref, o_ref):
        o_ref[0] = lax.dot_general(_silu(c_ref[...]), d_ref[0], (((0,), (0,)), ((), ())),
                                   preferred_element_type=F32, precision=lax.Precision.HIGHEST)

    return pl.pallas_call(
        body, grid=(DEPTH,),
        in_specs=[pl.BlockSpec((N_DEV, D_MODEL), lambda l: (0, 0)),
                  pl.BlockSpec((1, N_DEV, _ADA_SHARD), lambda l: (l, 0, 0))],
        out_specs=pl.BlockSpec((1, D_MODEL, _ADA_SHARD), lambda l: (l, 0, 0)),
        out_shape=jax.ShapeDtypeStruct((DEPTH, D_MODEL, _ADA_SHARD), F32), name=name,
        compiler_params=_cparams("parallel"))(c_all, dmod_cols)


def _add_rows(a, b, *, name):
    def body(a_ref, b_ref, o_ref):
        o_ref[...] = a_ref[...] + b_ref[...]

    return pl.pallas_call(body, out_shape=jax.ShapeDtypeStruct(a.shape, a.dtype), name=name)(a, b)


def _adamw_update(w_ref, g_ref, m_ref, v_ref, d_ref, mo_ref, vo_ref):
    gv = g_ref[...]
    mn = ADAM_B1 * m_ref[...] + (1.0 - ADAM_B1) * gv
    vn = ADAM_B2 * v_ref[...] + (1.0 - ADAM_B2) * (gv * gv)
    mo_ref[...] = mn
    vo_ref[...] = vn
    m_hat = mn / (1.0 - ADAM_B1 ** ADAM_STEP)
    v_hat = vn / (1.0 - ADAM_B2 ** ADAM_STEP)
    d_ref[...] = -ADAM_LR * (m_hat / (jnp.sqrt(v_hat) + ADAM_EPS) + ADAM_WD * w_ref[...])


def _adamw_small(ws, gs, ms, vs, *, name):
    n = len(ws)

    def body(*refs):
        ins, outs = refs[:4 * n], refs[4 * n:]
        for i in range(n):
            _adamw_update(ins[i], ins[n + i], ins[2 * n + i], ins[3 * n + i], outs[i], outs[n + i], outs[2 * n + i])

    shapes = [jax.ShapeDtypeStruct(a.shape, F32) for a in ws]
    outs = pl.pallas_call(body, out_shape=shapes * 3, name=name,
                          compiler_params=pltpu.CompilerParams(vmem_limit_bytes=VMEM_LIMIT))(*ws, *gs, *ms, *vs)
    return outs[:n], outs[n:2 * n], outs[2 * n:]


def _adamw(w, g, m, v, *, name, tr):
    r, cdim = w.shape
    body = functools.partial(_adamw_update)

    spec = pl.BlockSpec((tr, cdim), lambda i: (i, 0))
    shp = jax.ShapeDtypeStruct((r, cdim), F32)
    return pl.pallas_call(body, grid=(r // tr,), in_specs=[spec] * 4, out_specs=[spec] * 3, out_shape=[shp] * 3,
                          name=name, compiler_params=_cparams("parallel"))(w, g, m, v)


def _pad_rows(a, rows):
    return jnp.concatenate([a, jnp.zeros((rows - a.shape[0],) + a.shape[1:], a.dtype)], axis=0)


def _pad_lanes(a, lanes):
    return jnp.concatenate([a, jnp.zeros(a.shape[:-1] + (lanes - a.shape[-1],), a.dtype)], axis=-1)


def _permute_w_in(wt):
    return jnp.concatenate([wt[512:1536], wt[:512], wt[1536:1544],
                            jnp.zeros((PROJ_W - IN_W, wt.shape[1]), wt.dtype), wt[1544:]], axis=0)


def _unpermute_w_in(wp):
    return jnp.concatenate([wp[1024:1536], wp[:1024], wp[1536:1544], wp[PROJ_SSD_W:]], axis=0)


def _block_diag(w):
    rows = []
    for g in range(4):
        rows.append(jnp.concatenate([w[g] if k == g else jnp.zeros_like(w[g]) for k in range(4)], axis=1))
    return jnp.concatenate(rows, axis=0)


def _diag_blocks(wbd):
    return jnp.stack([wbd[64 * g:64 * (g + 1), 64 * g:64 * (g + 1)] for g in range(4)], axis=0)


def _layer_params(l, small):
    return dict(
        norm1_g=small["norm1_g"][l][None], norm2_g=small["norm2_g"][l][None],
        conv_w=_pad_rows(small["ssd_conv_w"][l], 8), conv_b=small["ssd_conv_b"][l][None],
        dt_bias=_pad_lanes(small["ssd_dt_bias"][l][None], 128), a_log=_pad_lanes(small["ssd_a_log"][l][None], 128),
        d_skip=_pad_lanes(small["ssd_d"][l][None], 128), ssd_norm_g=small["ssd_norm_g"][l][None],
        pool_bd=_block_diag(small["pool_w"][l]), pool_scale=small["pool_scale"][l][None],
        fcw=_pad_rows(small["ffn_conv_w"][l], 8), fcb=small["ffn_conv_b"][l][None])


def _mod_rows(mod_l):
    return [mod_l[None, D_MODEL * i:D_MODEL * (i + 1)] for i in range(6)]


def _layer_fwd(x, mod_l, p, tabs, l, gather):
    sh1, sc1, g1, sh2, sc2, g2 = _mod_rows(mod_l)
    mix_w = gather(l, "mix", None)
    p.update(w_in=mix_w["w_in"], w_out=mix_w["w_out"])
    proj, h1 = _mm(x, p["w_in"], nt=True, norm=(p["norm1_g"], sc1, sh1), name=f"l{l}_proj")
    ffn_w = gather(l, "ffn", proj)
    p.update(up=ffn_w["ffn_up"], down=ffn_w["ffn_down"])
    mix, hst = _ssd_fwd(proj, p["conv_w"], p["conv_b"], p["dt_bias"], p["a_log"], p["d_skip"], p["ssd_norm_g"],
                        name=f"l{l}_ssd")
    mix = _pool_fwd(proj, p["pool_bd"], p["pool_scale"], mix, name=f"l{l}_pool")
    y_att, lse, mix = _att_fwd(proj, *tabs, mix, name=f"l{l}_att")
    mo, x1 = _mm(mix, p["w_out"], residual=(x, g1), name=f"l{l}_out")
    gather(l + 1, "mix", (x1, p["up"]))
    h2, hid, act, f, x2 = _ffn_fwd(x1, p["norm2_g"], sc2, sh2, g2, p["up"], p["down"], p["fcw"], p["fcb"],
                                   name=f"l{l}_ffn")
    return x2, dict(x=x, h1=h1, proj=proj, hst=hst, y_att=y_att, lse=lse, mix=mix, mo=mo, x1=x1, h2=h2, hid=hid,
                    act=act, f=f)


def _layer_bwd(dx2, sv, mod_l, p, tabs, l, exchange):
    sh1, sc1, g1, sh2, sc2, g2 = _mod_rows(mod_l)
    df, dg2, dhid, dh2, dfcw, dfcb = _ffn_bwd(dx2, g2, sv["f"], sv["hid"], p["up"], p["down"], p["fcw"], p["fcb"],
                                              name=f"l{l}_ffn_b")
    d_down = _wgrad(sv["act"], df, tk=1408, name=f"l{l}_down_bw")
    d_up = _wgrad(dhid, sv["h2"], tk=1408, name=f"l{l}_up_bw")
    exchange(l, "ffn", dict(ffn_up=d_up, ffn_down=d_down))
    dx1, dn2, dsc2, dsh2, dmix, dmo, dg1 = _mid_bwd(sv["x1"], dh2, dx2, p["norm2_g"], sc2, g1, sv["mo"], p["w_out"],
                                                    name=f"l{l}_mid_b")
    d_wout = _wgrad(sv["mix"], dmo, name=f"l{l}_out_bw")
    dproj, dcw, dcb, ddb, dal, dd, dng = _ssd_bwd(
        sv["proj"], sv["hst"], dmix, p["conv_w"], p["conv_b"], p["dt_bias"], p["a_log"], p["d_skip"],
        p["ssd_norm_g"], name=f"l{l}_ssd_b")
    dproj, dwbd, dpsc = _pool_bwd(sv["proj"], dmix, p["pool_bd"], p["pool_scale"], dproj, name=f"l{l}_pool_b")
    dproj = _att_bwd(sv["proj"], *tabs, sv["y_att"], sv["lse"], dmix, dproj, name=f"l{l}_att_b")
    d_win = _wgrad(dproj, sv["h1"], tk=1408, name=f"l{l}_proj_bw")
    exchange(l, "mix", dict(w_in=d_win, w_out=d_wout))
    dx0, dn1, dsc1, dsh1 = _norm_mod_bwd(sv["x"], dproj, dx1, p["norm1_g"], sc1, w=p["w_in"], name=f"l{l}_proj_b")
    dmod = jnp.concatenate([dsh1, dsc1, dg1, dsh2, dsc2, dg2], axis=1)[0]
    small = dict(norm1_g=dn1[0], ssd_conv_w=dcw[:SSD_CONV_K], ssd_conv_b=dcb[0], ssd_dt_bias=ddb[0], ssd_a_log=dal[0],
                 ssd_d=dd[0], ssd_norm_g=dng[0], pool_w=_diag_blocks(dwbd), pool_scale=dpsc[0], norm2_g=dn2[0],
                 ffn_conv_w=dfcw[:FFN_CONV_K], ffn_conv_b=dfcb[0])
    return dx0, dmod, small


def _example_step(x, target, pos_col, inv_freq_lane, mod, gather, small, final_g, exchange):
    tabs = _rope_tables(pos_col, inv_freq_lane, name="rope_tables")
    params, saved = [], []
    for l in range(DEPTH):
        params.append(_layer_params(l, small))
        x, sv = _layer_fwd(x, mod[l], params[l], tabs, l, gather)
        saved.append(sv)
    loss_row, dx, dfg = _final_loss(x, final_g[None], target, name="final_loss")
    dmods, smalls = [None] * DEPTH, [None] * DEPTH
    for l in reversed(range(DEPTH)):
        dx, dmods[l], smalls[l] = _layer_bwd(dx, saved[l], mod[l], params[l], tabs, l, exchange)
    return loss_row, dx, jnp.stack(dmods, axis=0), smalls, dfg[0]


_BIG = ("w_in", "w_out", "ffn_up", "ffn_down")
_SMALL_GRADS = ("norm1_g", "ssd_conv_w", "ssd_conv_b", "ssd_dt_bias", "ssd_a_log", "ssd_d", "ssd_norm_g", "pool_w",
                "pool_scale", "norm2_g", "ffn_conv_w", "ffn_conv_b")
_SMALL_PARAMS = ("ada_b", "norm1_g", "ssd_conv_w", "ssd_conv_b", "ssd_dt_bias", "ssd_a_log", "ssd_d", "ssd_norm_g",
                 "pool_w", "pool_scale", "norm2_g", "ffn_conv_w", "ffn_conv_b", "final_g")
_WEIGHT_ORDER = ("ada_w", "ada_b", "norm1_g", "w_in", "ssd_conv_w", "ssd_conv_b", "ssd_dt_bias", "ssd_a_log", "ssd_d",
                 "ssd_norm_g", "pool_w", "pool_scale", "w_out", "norm2_g", "ffn_up", "ffn_conv_w", "ffn_conv_b",
                 "ffn_down", "final_g")


_COLUMN_SHARDED = ("w_in", "ffn_up")
_GROUPS = (("mix", ("w_in", "w_out")), ("ffn", ("ffn_up", "ffn_down")))


def _big_shares(w, l, names):
    return [(w[name][l].T if name in _COLUMN_SHARDED else w[name][l]).astype(BF16) for name in names]


def _unshard_big(names, gathered):
    out = {}
    for name, g in zip(names, gathered):
        full = g.reshape(N_DEV * g.shape[1], g.shape[2])
        out[name] = _permute_w_in(full) if name == "w_in" else full
    return out


def _shard_big(grads):
    out = []
    for name, g in grads.items():
        g = _unpermute_w_in(g) if name == "w_in" else g
        out.append(g.reshape(N_DEV, g.shape[0] // N_DEV, g.shape[1]))
    return out


def kernel(x, c, positions, ada_w, ada_b, norm1_g, w_in, ssd_conv_w, ssd_conv_b, ssd_dt_bias, ssd_a_log, ssd_d, ssd_norm_g, pool_w, pool_scale, w_out, norm2_g, ffn_up, ffn_conv_w, ffn_conv_b, ffn_down, final_g, loss_target, m_ada_w, m_ada_b, m_norm1_g, m_w_in, m_ssd_conv_w, m_ssd_conv_b, m_ssd_dt_bias, m_ssd_a_log, m_ssd_d, m_ssd_norm_g, m_pool_w, m_pool_scale, m_w_out, m_norm2_g, m_ffn_up, m_ffn_conv_w, m_ffn_conv_b, m_ffn_down, m_final_g, v_ada_w, v_ada_b, v_norm1_g, v_w_in, v_ssd_conv_w, v_ssd_conv_b, v_ssd_dt_bias, v_ssd_a_log, v_ssd_d, v_ssd_norm_g, v_pool_w, v_pool_scale, v_w_out, v_norm2_g, v_ffn_up, v_ffn_conv_w, v_ffn_conv_b, v_ffn_down, v_final_g):
    w = dict(ada_w=ada_w, ada_b=ada_b, norm1_g=norm1_g, w_in=w_in, ssd_conv_w=ssd_conv_w, ssd_conv_b=ssd_conv_b,
             ssd_dt_bias=ssd_dt_bias, ssd_a_log=ssd_a_log, ssd_d=ssd_d, ssd_norm_g=ssd_norm_g, pool_w=pool_w,
             pool_scale=pool_scale, w_out=w_out, norm2_g=norm2_g, ffn_up=ffn_up, ffn_conv_w=ffn_conv_w,
             ffn_conv_b=ffn_conv_b, ffn_down=ffn_down, final_g=final_g)
    m = dict(ada_w=m_ada_w, ada_b=m_ada_b, norm1_g=m_norm1_g, w_in=m_w_in, ssd_conv_w=m_ssd_conv_w,
             ssd_conv_b=m_ssd_conv_b, ssd_dt_bias=m_ssd_dt_bias, ssd_a_log=m_ssd_a_log, ssd_d=m_ssd_d,
             ssd_norm_g=m_ssd_norm_g, pool_w=m_pool_w, pool_scale=m_pool_scale, w_out=m_w_out, norm2_g=m_norm2_g,
             ffn_up=m_ffn_up, ffn_conv_w=m_ffn_conv_w, ffn_conv_b=m_ffn_conv_b, ffn_down=m_ffn_down,
             final_g=m_final_g)
    v = dict(ada_w=v_ada_w, ada_b=v_ada_b, norm1_g=v_norm1_g, w_in=v_w_in, ssd_conv_w=v_ssd_conv_w,
             ssd_conv_b=v_ssd_conv_b, ssd_dt_bias=v_ssd_dt_bias, ssd_a_log=v_ssd_a_log, ssd_d=v_ssd_d,
             ssd_norm_g=v_ssd_norm_g, pool_w=v_pool_w, pool_scale=v_pool_scale, w_out=v_w_out, norm2_g=v_norm2_g,
             ffn_up=v_ffn_up, ffn_conv_w=v_ffn_conv_w, ffn_conv_b=v_ffn_conv_b, ffn_down=v_ffn_down,
             final_g=v_final_g)
    ix, iy, ic = _axes()
    dev = 4 * ix + 2 * iy + ic

    c_all, scw, fcw = _allgather_small([c, ssd_conv_w.reshape(DEPTH * SSD_CONV_K, -1),
                                  ffn_conv_w.reshape(DEPTH * FFN_CONV_K, -1)], name="gather_small")
    small_all = c_all
    c_all = c_all.reshape(N_DEV, D_MODEL)
    scw = scw.reshape(N_DEV, DEPTH, SSD_CONV_K, -1).transpose(1, 2, 0, 3).reshape(DEPTH, SSD_CONV_K, SSD_CONV_CH)
    fcw = fcw.reshape(N_DEV, DEPTH, FFN_CONV_K, -1).transpose(1, 2, 0, 3).reshape(DEPTH, FFN_CONV_K, 2 * FFN_DIM)

    mod_cols = _ada_mod(c_all, ada_w, name="ada_mod")
    mod_all = _allgather_small([mod_cols.reshape(DEPTH * N_DEV, _ADA_SHARD)], name="gather_mod")[0]
    mod_all = mod_all.reshape(N_DEV, DEPTH, N_DEV, _ADA_SHARD)
    mod_mine = lax.dynamic_index_in_dim(mod_all, dev, axis=2, keepdims=False)
    mod = _add_rows(mod_mine.transpose(1, 0, 2).reshape(DEPTH, 6 * D_MODEL), ada_b, name="ada_bias")

    fetched = {}

    def gather(l, group, after):
        if l < DEPTH and (l, group) not in fetched:
            names = dict(_GROUPS)[group]
            shares, _ = lax.optimization_barrier((_big_shares(w, l, names), small_all if after is None else after))
            got = _allgather_async(shares, name=f"gather_weights_l{l}_{group}",
                                   collective_id=1 + 2 * l + (group == "ffn"))
            fetched[l, group] = _unshard_big(names, got)
        return fetched.get((l, group))

    core = ic.astype(jnp.int32).reshape(1)
    from_chips = {}

    def exchange(l, group, g):
        cid = 5 + 4 * l + 2 * (group == "mix")
        blocks = _shard_big(g)
        from_sibling = _pair_exchange(blocks, name=f"grads_pair_exchange_l{l}_{group}", collective_id=cid)
        parts = _pair_sum(core, blocks, from_sibling, name=f"grads_pair_sum_l{l}_{group}")
        got = _chip_exchange(parts, name=f"grads_chip_exchange_l{l}_{group}", collective_id=cid + 1)
        from_chips.update({(l, name): t for name, t in zip(g, got)})

    small = dict(norm1_g=norm1_g, norm2_g=norm2_g, ssd_conv_w=scw, ssd_conv_b=ssd_conv_b, ssd_dt_bias=ssd_dt_bias,
                 ssd_a_log=ssd_a_log, ssd_d=ssd_d, ssd_norm_g=ssd_norm_g, pool_w=pool_w, pool_scale=pool_scale,
                 ffn_conv_w=fcw, ffn_conv_b=ffn_conv_b)

    inv_freq = ROPE_THETA ** (-jnp.arange(0, ROT_DIM, 2, dtype=F32) / ROT_DIM)
    lane = jnp.arange(128) % HEAD_LANES
    inv_freq_lane = jnp.where(lane < ROT_DIM, inv_freq[lane % (ROT_DIM // 2)], 0.0)[None, :]
    pos_col = positions.reshape(SEQ, 1).astype(F32)
    loss_row, dx, dmod, g_small, g_final = _example_step(
        x[0], loss_target[0], pos_col, inv_freq_lane, mod, gather, small, final_g, exchange)

    grads = {}
    chip_sums = [_sum_gathered([from_chips[l, name] for name in _BIG], name=f"grads_chip_sum_l{l}")
                 for l in range(DEPTH)]
    for i, name in enumerate(_BIG):
        per_layer = [chip_sums[l][i] for l in range(DEPTH)]
        grads[name] = jnp.stack([g.T if name in _COLUMN_SHARDED else g for g in per_layer], axis=0)

    small_names = list(_SMALL_GRADS)
    stacked = [jnp.stack([g_small[l][name] for l in range(DEPTH)], axis=0) for name in small_names]
    small_parts = [loss_row, dmod] + [s.reshape(-1, s.shape[-1]) for s in stacked] + [g_final[None]]
    gathered = _allgather_small(small_parts, name="gather_small_grads")
    total = _sum_gathered(gathered, name="sum_small_grads")
    loss = total[0][0, 0]
    grads["ada_b"] = total[1]
    grads.update(zip(small_names, total[2:-1]))
    grads["final_g"] = total[-1][0]
    dmod_cols = lax.dynamic_slice_in_dim(gathered[1], dev * _ADA_SHARD, _ADA_SHARD, axis=2).transpose(1, 0, 2)
    grads["ada_w"] = _ada_wgrad(c_all, dmod_cols, name="ada_wgrad")
    for name in ("ssd_dt_bias", "ssd_a_log", "ssd_d"):
        grads[name] = grads[name][:, :SSD_HEADS]
    grads["pool_w"] = grads["pool_w"].reshape(pool_w.shape)
    grads["ssd_conv_w"] = lax.dynamic_slice_in_dim(
        grads["ssd_conv_w"].reshape(DEPTH, SSD_CONV_K, SSD_CONV_CH), dev * ssd_conv_w.shape[2], ssd_conv_w.shape[2], axis=2)
    grads["ffn_conv_w"] = lax.dynamic_slice_in_dim(
        grads["ffn_conv_w"].reshape(DEPTH, FFN_CONV_K, 2 * FFN_DIM), dev * ffn_conv_w.shape[2], ffn_conv_w.shape[2], axis=2)

    delta, new_m, new_v = {}, {}, {}
    for name, tr in (("ada_w", 512), ("w_in", 512), ("w_out", 256), ("ffn_up", 512), ("ffn_down", 352)):
        shp = w[name].shape
        two_d = lambda a: a.reshape(shp[0] * shp[1], shp[2])
        d_, m_, v_ = _adamw(two_d(w[name]), two_d(grads[name]), two_d(m[name]), two_d(v[name]), tr=tr,
                               name=f"adamw_{name}")
        delta[name], new_m[name], new_v[name] = (t.reshape(shp) for t in (d_, m_, v_))
    two_d = lambda a: a.reshape(-1, a.shape[-1])
    outs = _adamw_small(*[[two_d(t[name]) for name in _SMALL_PARAMS] for t in (w, grads, m, v)], name="adamw_small")
    for name, d_, m_, v_ in zip(_SMALL_PARAMS, *outs):
        delta[name], new_m[name], new_v[name] = (t.reshape(w[name].shape) for t in (d_, m_, v_))

    grad_x = dx[None]
    return (loss, grad_x, *[grads[n].reshape(w[n].shape) for n in _WEIGHT_ORDER],
            *[delta[n] for n in _WEIGHT_ORDER], *[new_m[n] for n in _WEIGHT_ORDER],
            *[new_v[n] for n in _WEIGHT_ORDER])
```

```python
import functools
import math

import jax
import jax.numpy as jnp
from jax import lax
from jax.experimental import pallas as pl
from jax.experimental.pallas import tpu as pltpu
from jax.experimental.pallas import tpu_sc as plsc

F32 = jnp.float32
BF16 = jnp.bfloat16

N_DEV = 8
D_MODEL = 1024
SEQ = 4096
DEPTH = 2
SSD_INNER = 512
SSD_HEADS = 8
SSD_HPG = 4
SSD_STATE = 128
SSD_CHUNK = 256
SSD_CONV_K = 4
SSD_CONV_CH = 1024
POOL_W = 256
POOL_WINDOWS = (2, 4, 8, 16)
ATT_W = 256
ATT_PATTERNS = ((128, 1), (512, 4), (2048, 16))
ATT_BLOCK = 128
ROT_DIM = 16
ROPE_THETA = 500000.0
IN_W = 2568
FFN_DIM = 2816
FFN_CONV_K = 3
NORM_EPS = 1e-6
HEAD_LANES = 64

ADAM_LR = 0.001
ADAM_B1 = 0.9
ADAM_B2 = 0.999
ADAM_EPS = 1e-08
ADAM_WD = 0.01
ADAM_STEP = 10

PROJ_W = 2816
PROJ_SSD_W = 1792
PROJ_Z_BLK = 2
PROJ_DT_BLK = 12
PROJ_POOL_BLK = 7
PROJ_Q_BLK, PROJ_K_BLK, PROJ_V_BLK = 16, 18, 20
MIX_POOL_BLK = 2
MIX_ATT_BLK = 6
VMEM_LIMIT = 56 * 1024 * 1024
CONV_HALO = 8
POOL_HALO = 16
ATT_KPAD = ATT_BLOCK * 16
MESH = pl.DeviceIdType.MESH
_HBM = pl.BlockSpec(memory_space=pl.ANY)


def _cparams(*sem):
    return pltpu.CompilerParams(dimension_semantics=sem, vmem_limit_bytes=VMEM_LIMIT)


def _resident(shape):
    return pl.BlockSpec(shape, lambda i: (0,) * len(shape), pipeline_mode=pl.Buffered(1))


def _silu(x):
    return x * jax.nn.sigmoid(x)


def _pick_lane(v, h):
    lane = lax.broadcasted_iota(jnp.int32, v.shape, 1)
    return jnp.sum(jnp.where(lane == h, v, 0.0), axis=1, keepdims=True)


def _pick_row(v, h):
    row = lax.broadcasted_iota(jnp.int32, v.shape, 0)
    return jnp.sum(jnp.where(row == h, v, 0.0), axis=0, keepdims=True)


def _head_of_lane(width):
    return lax.broadcasted_iota(jnp.int32, (1, width), 1) // HEAD_LANES


@functools.partial(jax.custom_vjp, nondiff_argnums=(1, 2))
def _shift_rows(x_ext, s, halo):
    y = x_ext if s == 0 else pltpu.roll(x_ext, s, 0)
    return y[halo:]


def _shift_rows_fwd(x_ext, s, halo):
    return _shift_rows(x_ext, s, halo), None


def _shift_rows_bwd(s, halo, _, g):
    ge = jnp.concatenate([jnp.zeros((halo, g.shape[1]), g.dtype), g], axis=0)
    return (ge if s == 0 else pltpu.roll(ge, ge.shape[0] - s, 0),)


_shift_rows.defvjp(_shift_rows_fwd, _shift_rows_bwd)


@functools.partial(jax.custom_vjp, nondiff_argnums=(1,))
def _roll_rows(x, s):
    return pltpu.roll(x, s, 0)


def _roll_rows_fwd(x, s):
    return _roll_rows(x, s), None


def _roll_rows_bwd(s, _, g):
    return (pltpu.roll(g, g.shape[0] - s, 0),)


_roll_rows.defvjp(_roll_rows_fwd, _roll_rows_bwd)


def _rms_modulate(xv, g, sc, sh):
    r = lax.rsqrt(jnp.mean(xv * xv, axis=-1, keepdims=True) + NORM_EPS)
    return (xv * r * g) * (1.0 + sc) + sh


def _mm(a, w, *, name, nt=False, tm=512, tn=None, out_dtype=F32, norm=None, residual=None):
    t, k = a.shape
    n = w.shape[0] if nt else w.shape[1]
    tn = tn or n
    assert tn == n or (norm is None and residual is None)
    extra_in = list(norm or ()) + list(residual or ())

    def body(*refs):
        a_ref, w_ref = refs[:2]
        ins = refs[2:2 + len(extra_in)]
        outs = refs[2 + len(extra_in):]
        if norm is None:
            av = a_ref[...].astype(BF16)
        else:
            av = _rms_modulate(a_ref[...], ins[0][...], ins[1][...], ins[2][...]).astype(BF16)
            outs[1][...] = av
        if nt:
            acc = lax.dot_general(av, w_ref[...], (((1,), (1,)), ((), ())), preferred_element_type=F32)
        else:
            acc = jnp.dot(av, w_ref[...], preferred_element_type=F32)
        outs[0][...] = acc.astype(out_dtype)
        if residual is not None:
            x_ref, gate_ref = ins[-2:]
            outs[-1][...] = x_ref[...] + gate_ref[...] * acc

    row = lambda width: pl.BlockSpec((1, width), lambda i, j: (0, 0))
    tile = lambda width: pl.BlockSpec((tm, width), lambda i, j: (i, 0))
    w_spec = pl.BlockSpec((tn, k), lambda i, j: (j, 0)) if nt else pl.BlockSpec((k, tn), lambda i, j: (0, j))
    in_specs = [tile(k), w_spec] + ([row(k)] * 3 if norm else []) + ([tile(n), row(n)] if residual else [])
    out_specs = [pl.BlockSpec((tm, tn), lambda i, j: (i, j))] + ([tile(k)] if norm else []) + \
        ([tile(n)] if residual else [])
    out_shape = [jax.ShapeDtypeStruct((t, n), out_dtype)] + \
        ([jax.ShapeDtypeStruct((t, k), BF16)] if norm else []) + \
        ([jax.ShapeDtypeStruct((t, n), F32)] if residual else [])
    outs = pl.pallas_call(
        body, grid=(t // tm, n // tn), in_specs=in_specs, out_specs=out_specs, out_shape=out_shape, name=name,
        compiler_params=_cparams("parallel", "parallel"))(a, w, *extra_in)
    return outs[0] if len(outs) == 1 else outs


def _wgrad(a, b, *, name, tk=None, tn=None, tt=2048, out_dtype=BF16):
    t, k = a.shape
    n = b.shape[1]
    tk = tk or k
    tn = tn or n
    steps = t // tt

    def body(a_ref, b_ref, o_ref, acc_ref):
        s = pl.program_id(2)

        @pl.when(s == 0)
        def _():
            acc_ref[...] = jnp.zeros_like(acc_ref)

        acc_ref[...] += lax.dot_general(a_ref[...].astype(BF16), b_ref[...].astype(BF16),
                                        (((0,), (0,)), ((), ())), preferred_element_type=F32)

        @pl.when(s == steps - 1)
        def _():
            o_ref[...] = acc_ref[...].astype(out_dtype)

    return pl.pallas_call(
        body, grid=(k // tk, n // tn, steps),
        in_specs=[pl.BlockSpec((tt, tk), lambda i, j, s: (s, i)), pl.BlockSpec((tt, tn), lambda i, j, s: (s, j))],
        out_specs=pl.BlockSpec((tk, tn), lambda i, j, s: (i, j)),
        out_shape=jax.ShapeDtypeStruct((k, n), out_dtype),
        scratch_shapes=[pltpu.VMEM((tk, tn), F32)], name=name,
        compiler_params=_cparams("parallel", "parallel", "arbitrary"))(a, b)


def _norm_mod_bwd(x, dh, dres, g, sc, *, name, w=None, tm=512):
    s, d = x.shape
    steps = s // tm

    def body(x_ref, dh_ref, dres_ref, g_ref, sc_ref, *rest):
        w_ref = rest[0] if w is not None else None
        dx_ref, dg_ref, dsc_ref, dsh_ref, da_acc, dsh_acc = rest[-6:]
        i = pl.program_id(0)

        @pl.when(i == 0)
        def _():
            da_acc[...] = jnp.zeros_like(da_acc)
            dsh_acc[...] = jnp.zeros_like(dsh_acc)

        xv = x_ref[...]
        if w is None:
            dhv = dh_ref[...].astype(F32)
        else:
            dhv = jnp.dot(dh_ref[...], w_ref[...], preferred_element_type=F32)
        r = lax.rsqrt(jnp.mean(xv * xv, axis=-1, keepdims=True) + NORM_EPS)
        xhat = xv * r
        gain = g_ref[...] * (1.0 + sc_ref[...])
        dxhat = dhv * gain
        dx_ref[...] = dres_ref[...] + r * (dxhat - xhat * jnp.mean(dxhat * xhat, axis=-1, keepdims=True))
        da_acc[...] += jnp.sum(dhv * xhat, axis=0, keepdims=True)
        dsh_acc[...] += jnp.sum(dhv, axis=0, keepdims=True)

        @pl.when(i == steps - 1)
        def _():
            dg_ref[...] = da_acc[...] * (1.0 + sc_ref[...])
            dsc_ref[...] = da_acc[...] * g_ref[...]
            dsh_ref[...] = dsh_acc[...]

    row = pl.BlockSpec((1, d), lambda i: (0, 0))
    tile = pl.BlockSpec((tm, d), lambda i: (i, 0))
    row_shape = jax.ShapeDtypeStruct((1, d), F32)
    dh_spec = tile if w is None else pl.BlockSpec((tm, dh.shape[1]), lambda i: (i, 0))
    return pl.pallas_call(
        body, grid=(steps,), in_specs=[tile, dh_spec, tile, row, row] + ([] if w is None else [_resident(w.shape)]),
        out_specs=[tile, row, row, row],
        out_shape=[jax.ShapeDtypeStruct((s, d), F32), row_shape, row_shape, row_shape],
        scratch_shapes=[pltpu.VMEM((1, d), F32), pltpu.VMEM((1, d), F32)], name=name,
        compiler_params=_cparams("arbitrary"))(x, dh, dres, g, sc, *([] if w is None else [w]))


def _mid_bwd(x1, dh2, dx2, norm_g, sc, gate, mo, w_out, *, name, tm=512):
    s, d = x1.shape
    steps = s // tm

    def body(x_ref, dh_ref, dres_ref, ng_ref, sc_ref, g_ref, mo_ref, w_ref,
             dx_ref, dng_ref, dsc_ref, dsh_ref, dmix_ref, dmo_ref, dg_ref, da_acc, dsh_acc):
        i = pl.program_id(0)

        @pl.when(i == 0)
        def _():
            da_acc[...] = jnp.zeros_like(da_acc)
            dsh_acc[...] = jnp.zeros_like(dsh_acc)
            dg_ref[...] = jnp.zeros_like(dg_ref)

        xv = x_ref[...]
        dhv = dh_ref[...]
        r = lax.rsqrt(jnp.mean(xv * xv, axis=-1, keepdims=True) + NORM_EPS)
        xhat = xv * r
        dxhat = dhv * (ng_ref[...] * (1.0 + sc_ref[...]))
        dxv = dres_ref[...] + r * (dxhat - xhat * jnp.mean(dxhat * xhat, axis=-1, keepdims=True))
        dx_ref[...] = dxv
        da_acc[...] += jnp.sum(dhv * xhat, axis=0, keepdims=True)
        dsh_acc[...] += jnp.sum(dhv, axis=0, keepdims=True)
        dmo = (g_ref[...] * dxv).astype(BF16)
        dmo_ref[...] = dmo
        dg_ref[...] += jnp.sum(dxv * mo_ref[...], axis=0, keepdims=True)
        dmix_ref[...] = lax.dot_general(dmo, w_ref[...], (((1,), (1,)), ((), ())), preferred_element_type=F32)

        @pl.when(i == steps - 1)
        def _():
            dng_ref[...] = da_acc[...] * (1.0 + sc_ref[...])
            dsc_ref[...] = da_acc[...] * ng_ref[...]
            dsh_ref[...] = dsh_acc[...]

    tile = pl.BlockSpec((tm, d), lambda i: (i, 0))
    row = pl.BlockSpec((1, d), lambda i: (0, 0))
    mix_tile = pl.BlockSpec((tm, w_out.shape[0]), lambda i: (i, 0))
    row_shape = jax.ShapeDtypeStruct((1, d), F32)
    return pl.pallas_call(
        body, grid=(steps,), in_specs=[tile, tile, tile, row, row, row, tile, _resident(w_out.shape)],
        out_specs=[tile, row, row, row, mix_tile, tile, row],
        out_shape=[jax.ShapeDtypeStruct((s, d), F32), row_shape, row_shape, row_shape,
                   jax.ShapeDtypeStruct((s, w_out.shape[0]), F32), jax.ShapeDtypeStruct((s, d), BF16), row_shape],
        scratch_shapes=[pltpu.VMEM((1, d), F32), pltpu.VMEM((1, d), F32)], name=name,
        compiler_params=_cparams("arbitrary"))(x1, dh2, dx2, norm_g, sc, gate, mo, w_out)


def _final_loss(x, g, target, *, name, tm=512):
    s, d = x.shape
    steps = s // tm

    def body(x_ref, g_ref, t_ref, loss_ref, dx_ref, dg_ref, sq_acc):
        i = pl.program_id(0)

        @pl.when(i == 0)
        def _():
            sq_acc[...] = jnp.zeros_like(sq_acc)
            dg_ref[...] = jnp.zeros_like(dg_ref)

        xv = x_ref[...]
        r = lax.rsqrt(jnp.mean(xv * xv, axis=-1, keepdims=True) + NORM_EPS)
        xhat = xv * r
        err = xhat * g_ref[...] - t_ref[...]
        sq_acc[...] += jnp.sum(err * err, axis=0, keepdims=True)
        dy = err * (1.0 / d)
        dg_ref[...] += jnp.sum(dy * xhat, axis=0, keepdims=True)
        dxhat = dy * g_ref[...]
        dx_ref[...] = r * (dxhat - xhat * jnp.mean(dxhat * xhat, axis=-1, keepdims=True))

        @pl.when(i == steps - 1)
        def _():
            total = jnp.sum(sq_acc[...], axis=1, keepdims=True) * (0.5 / d)
            loss_ref[...] = jnp.broadcast_to(total, loss_ref.shape)

    tile = pl.BlockSpec((tm, d), lambda i: (i, 0))
    row = pl.BlockSpec((1, d), lambda i: (0, 0))
    return pl.pallas_call(
        body, grid=(steps,), in_specs=[tile, row, tile],
        out_specs=[pl.BlockSpec((1, 128), lambda i: (0, 0)), tile, row],
        out_shape=[jax.ShapeDtypeStruct((1, 128), F32), jax.ShapeDtypeStruct((s, d), F32),
                   jax.ShapeDtypeStruct((1, d), F32)],
        scratch_shapes=[pltpu.VMEM((1, d), F32)], name=name, compiler_params=_cparams("arbitrary"))(x, g, target)


def _ssd_chunk(z, xbc_ext, dt_raw, conv_w, conv_b, dt_bias, a_log, d_skip, norm_g, h_in):
    q = z.shape[0]
    gw = SSD_HPG * HEAD_LANES
    xc = conv_b
    for k in range(SSD_CONV_K):
        xc = xc + _pick_row(conv_w, k) * _shift_rows(xbc_ext, SSD_CONV_K - 1 - k, CONV_HALO)
    xc = _silu(xc)
    dt = jax.nn.softplus(dt_raw + dt_bias)
    da = dt * (-jnp.exp(a_log))
    ri = lax.broadcasted_iota(jnp.int32, (q, q), 0)
    ci = lax.broadcasted_iota(jnp.int32, (q, q), 1)
    causal = ri >= ci
    tril = causal.astype(F32)
    a_cum = jnp.dot(tril, da, preferred_element_type=F32, precision=lax.Precision.HIGHEST)
    a_cum_t = lax.dot_general(da, tril, (((0,), (1,)), ((), ())), preferred_element_type=F32,
                              precision=lax.Precision.HIGHEST)
    a_last = _pick_row(a_cum, q - 1)
    head = _head_of_lane(gw)
    ys, hs = [], []
    for g in range(2):
        xs = xc[:, gw * g:gw * (g + 1)]
        bm = xc[:, SSD_INNER + SSD_STATE * g:SSD_INNER + SSD_STATE * (g + 1)]
        cm = xc[:, SSD_INNER + 2 * SSD_STATE + SSD_STATE * g:SSD_INNER + 2 * SSD_STATE + SSD_STATE * (g + 1)]
        cb = lax.dot_general(cm.astype(BF16), bm.astype(BF16), (((1,), (1,)), ((), ())), preferred_element_type=F32)
        cols = [_pick_lane(a_cum, SSD_HPG * g + j) for j in range(SSD_HPG)]
        lasts = [_pick_lane(a_last, SSD_HPG * g + j) for j in range(SSD_HPG)]
        dt_exp = sum(jnp.where(head == j, _pick_lane(dt, SSD_HPG * g + j), 0.0) for j in range(SSD_HPG))
        d_exp = sum(jnp.where(head == j, _pick_lane(d_skip, SSD_HPG * g + j), 0.0) for j in range(SSD_HPG))
        e_cum = sum(jnp.where(head == j, jnp.exp(cols[j]), 0.0) for j in range(SSD_HPG))
        c_dec = sum(jnp.where(head == j, jnp.exp(lasts[j]), 0.0) for j in range(SSD_HPG))
        xsdt = (xs * dt_exp).astype(BF16)
        y_diag = jnp.zeros((q, gw), F32)
        st_new = jnp.zeros((SSD_STATE, gw), F32)
        for j in range(SSD_HPG):
            row = _pick_row(a_cum_t, SSD_HPG * g + j)
            lmat = jnp.exp(jnp.where(causal, cols[j] - row, -jnp.inf))
            r = jnp.dot((cb * lmat).astype(BF16), xsdt, preferred_element_type=F32)
            y_diag = y_diag + jnp.where(head == j, r, 0.0)
            bd = (bm * jnp.exp(lasts[j] - cols[j])).astype(BF16)
            st = lax.dot_general(bd, xsdt, (((0,), (0,)), ((), ())), preferred_element_type=F32)
            st_new = st_new + jnp.where(head == j, st, 0.0)
        y_off = jnp.dot(cm.astype(BF16), h_in[g].astype(BF16), preferred_element_type=F32) * e_cum
        hs.append(h_in[g] * c_dec + st_new)
        y = y_diag + y_off + d_exp * xs
        yz = y * _silu(z[:, gw * g:gw * (g + 1)])
        yz = yz * lax.rsqrt(jnp.mean(yz * yz, axis=-1, keepdims=True) + NORM_EPS)
        ys.append(yz * norm_g[:, gw * g:gw * (g + 1)])
    return jnp.concatenate(ys, axis=1), tuple(hs)


_SSD_NCHUNK = SEQ // SSD_CHUNK
_HALO_PER_CHUNK = SSD_CHUNK // CONV_HALO


def _ssd_param_specs(const):
    return [pl.BlockSpec((8, SSD_CONV_CH), const), pl.BlockSpec((1, SSD_CONV_CH), const),
            pl.BlockSpec((1, 128), const), pl.BlockSpec((1, 128), const), pl.BlockSpec((1, 128), const),
            pl.BlockSpec((1, SSD_INNER), const)]


def _ssd_fwd(proj, conv_w, conv_b, dt_bias, a_log, d_skip, norm_g, *, name):
    q = SSD_CHUNK

    def body(z_ref, xbc_ref, halo_ref, dt_ref, cw_ref, cb_ref, db_ref, al_ref, d_ref, ng_ref, y_ref, hs_ref, h_acc):
        i = pl.program_id(0)

        @pl.when(i == 0)
        def _():
            h_acc[...] = jnp.zeros_like(h_acc)

        halo = jnp.where(i == 0, 0.0, halo_ref[...])
        xbc_ext = jnp.concatenate([halo, xbc_ref[...]], axis=0)
        h_in = (h_acc[0], h_acc[1])
        hs_ref[0, 0] = h_in[0]
        hs_ref[0, 1] = h_in[1]
        y, h_out = _ssd_chunk(z_ref[...], xbc_ext, dt_ref[...], cw_ref[...], cb_ref[...], db_ref[...], al_ref[...],
                              d_ref[...], ng_ref[...], h_in)
        y_ref[...] = y.astype(BF16)
        h_acc[0] = h_out[0]
        h_acc[1] = h_out[1]

    const = lambda i: (0, 0)
    return pl.pallas_call(
        body, grid=(_SSD_NCHUNK,),
        in_specs=[pl.BlockSpec((q, SSD_INNER), lambda i: (i, PROJ_Z_BLK)),
                  pl.BlockSpec((q, SSD_CONV_CH), lambda i: (i, 0)),
                  pl.BlockSpec((CONV_HALO, SSD_CONV_CH), lambda i: (jnp.maximum(i * _HALO_PER_CHUNK - 1, 0), 0)),
                  pl.BlockSpec((q, 128), lambda i: (i, PROJ_DT_BLK))] + _ssd_param_specs(const),
        out_specs=[pl.BlockSpec((q, SSD_INNER), lambda i: (i, 0)),
                   pl.BlockSpec((1, 2, SSD_STATE, 256), lambda i: (i, 0, 0, 0))],
        out_shape=[jax.ShapeDtypeStruct((SEQ, D_MODEL), BF16),
                   jax.ShapeDtypeStruct((_SSD_NCHUNK, 2, SSD_STATE, 256), F32)],
        scratch_shapes=[pltpu.VMEM((2, SSD_STATE, 256), F32)], name=name,
        compiler_params=_cparams("arbitrary"))(proj, proj, proj, proj, conv_w, conv_b, dt_bias, a_log, d_skip, norm_g)


def _ssd_bwd(proj, hstates, dmix, conv_w, conv_b, dt_bias, a_log, d_skip, norm_g, *, name):
    q = SSD_CHUNK
    last = _SSD_NCHUNK - 1

    def body(z_ref, xbc_ref, halo_ref, dt_ref, hs_ref, dy_ref, cw_ref, cb_ref, db_ref, al_ref, d_ref, ng_ref,
             dp_ref, dcw_ref, dcb_ref, ddb_ref, dal_ref, dd_ref, dng_ref, dh_acc, dhalo_acc):
        i = pl.program_id(0)

        @pl.when(i == 0)
        def _():
            dh_acc[...] = jnp.zeros_like(dh_acc)
            dhalo_acc[...] = jnp.zeros_like(dhalo_acc)
            for r in (dcw_ref, dcb_ref, ddb_ref, dal_ref, dd_ref, dng_ref):
                r[...] = jnp.zeros_like(r)

        halo = jnp.where(i == last, 0.0, halo_ref[...])
        xbc_ext = jnp.concatenate([halo, xbc_ref[...]], axis=0)
        _, vjp = jax.vjp(_ssd_chunk, z_ref[...], xbc_ext, dt_ref[...], cw_ref[...], cb_ref[...], db_ref[...],
                         al_ref[...], d_ref[...], ng_ref[...], (hs_ref[0, 0], hs_ref[0, 1]))
        gz, gx, gdt, gcw, gcb, gdb, gal, gd, gng, gh = vjp((dy_ref[...], (dh_acc[0], dh_acc[1])))
        dxbc = jnp.concatenate([gx[CONV_HALO:q], gx[q:] + dhalo_acc[...]], axis=0)
        dp_ref[...] = jnp.concatenate([dxbc, gz, gdt, jnp.zeros_like(gdt)], axis=1).astype(BF16)
        dhalo_acc[...] = gx[:CONV_HALO]
        dh_acc[0] = gh[0]
        dh_acc[1] = gh[1]
        dcw_ref[...] += gcw
        dcb_ref[...] += gcb
        ddb_ref[...] += gdb
        dal_ref[...] += gal
        dd_ref[...] += gd
        dng_ref[...] += gng

    const = lambda i: (0, 0)
    rev = lambda i: last - i
    row = lambda n: jax.ShapeDtypeStruct((1, n), F32)
    return pl.pallas_call(
        body, grid=(_SSD_NCHUNK,),
        in_specs=[pl.BlockSpec((q, SSD_INNER), lambda i: (rev(i), PROJ_Z_BLK)),
                  pl.BlockSpec((q, SSD_CONV_CH), lambda i: (rev(i), 0)),
                  pl.BlockSpec((CONV_HALO, SSD_CONV_CH), lambda i: (jnp.maximum(rev(i) * _HALO_PER_CHUNK - 1, 0), 0)),
                  pl.BlockSpec((q, 128), lambda i: (rev(i), PROJ_DT_BLK)),
                  pl.BlockSpec((1, 2, SSD_STATE, 256), lambda i: (rev(i), 0, 0, 0)),
                  pl.BlockSpec((q, SSD_INNER), lambda i: (rev(i), 0))] + _ssd_param_specs(const),
        out_specs=[pl.BlockSpec((q, PROJ_SSD_W), lambda i: (rev(i), 0))] + _ssd_param_specs(const),
        out_shape=[jax.ShapeDtypeStruct((SEQ, PROJ_W), BF16), jax.ShapeDtypeStruct((8, SSD_CONV_CH), F32),
                   row(SSD_CONV_CH), row(128), row(128), row(128), row(SSD_INNER)],
        scratch_shapes=[pltpu.VMEM((2, SSD_STATE, 256), F32), pltpu.VMEM((CONV_HALO, SSD_CONV_CH), F32)], name=name,
        compiler_params=_cparams("arbitrary"))(proj, proj, proj, proj, hstates, dmix, conv_w, conv_b, dt_bias, a_log,
                                                d_skip, norm_g)


def _rope_tables(pos_col, inv_freq_lane, *, name):
    s = pos_col.shape[0]

    def body(p_ref, f_ref, c_ref, s1_ref, s2_ref):
        ang = p_ref[...] * f_ref[...]
        within = lax.broadcasted_iota(jnp.int32, ang.shape, 1) % HEAD_LANES
        half = ROT_DIM // 2
        c_ref[...] = jnp.where(within < ROT_DIM, jnp.cos(ang), 1.0)
        sn = jnp.sin(ang)
        s1_ref[...] = jnp.where(within < half, -sn, 0.0)
        s2_ref[...] = jnp.where((within >= half) & (within < ROT_DIM), sn, 0.0)

    shp = jax.ShapeDtypeStruct((s, 128), F32)
    return pl.pallas_call(body, out_shape=[shp, shp, shp], name=name,
                          compiler_params=pltpu.CompilerParams(vmem_limit_bytes=VMEM_LIMIT))(pos_col, inv_freq_lane)


def _rope(t, c, s1, s2):
    half = ROT_DIM // 2
    return t * c + pltpu.roll(t, 128 - half, 1) * s1 + pltpu.roll(t, half, 1) * s2


def _rope_t(g, c, s1, s2):
    half = ROT_DIM // 2
    return g * c + pltpu.roll(g * s1, half, 1) + pltpu.roll(g * s2, 128 - half, 1)


def _att_valid(b):
    qi = lax.broadcasted_iota(jnp.int32, (ATT_BLOCK, 2 * ATT_BLOCK), 0)
    kj = lax.broadcasted_iota(jnp.int32, (ATT_BLOCK, 2 * ATT_BLOCK), 1)
    rel = qi + ATT_BLOCK - kj
    return (rel >= 0) & (rel <= ATT_BLOCK) & (b * ATT_BLOCK + kj - ATT_BLOCK >= 0)


def _att_slices(i, d):
    if d == 1:
        qstart = pl.multiple_of(i * ATT_BLOCK, ATT_BLOCK)
        return i, pl.ds(qstart, ATT_BLOCK), pl.ds(pl.multiple_of(qstart - ATT_BLOCK + ATT_KPAD, ATT_BLOCK), 2 * ATT_BLOCK)
    r = i % d
    b = i // d
    qstart = r + d * ATT_BLOCK * b
    return b, pl.ds(qstart, ATT_BLOCK, stride=d), pl.ds(qstart - ATT_BLOCK * d + ATT_KPAD, 2 * ATT_BLOCK, stride=d)


_ATT_NBLK = SEQ // ATT_BLOCK
_ATT_SCALE = HEAD_LANES ** -0.5
_ATT_UNROLL_FWD = 8
_ATT_UNROLL = 4


def _att_fwd(proj, cos, sin1, sin2, mix, *, name):
    s = SEQ

    def body(q_ref, k_ref, v_ref, c_ref, s1_ref, s2_ref, _, o_ref, lse_ref, mix_ref, qs, ks, vs, acc, m_s, l_s):
        c, s1, s2 = c_ref[...], s1_ref[...], s2_ref[...]
        qs[...] = _rope(q_ref[...], c, s1, s2) * _ATT_SCALE
        zeros = jnp.zeros((ATT_KPAD, 128), F32)
        ks[pl.ds(0, ATT_KPAD), :] = zeros
        vs[pl.ds(0, ATT_KPAD), :] = zeros
        ks[pl.ds(ATT_KPAD, s), :] = _rope(k_ref[...], c, s1, s2)
        vs[pl.ds(ATT_KPAD, s), :] = v_ref[...]
        head0 = _head_of_lane(128) == 0

        for bi, (_, d) in enumerate(ATT_PATTERNS):
            def blk(i, carry, d=d, first=(bi == 0)):
                b, sq, sk = _att_slices(i, d)
                qb = qs[sq, :]
                kw = ks[sk, :].astype(BF16)
                vw = vs[sk, :].astype(BF16)
                valid = _att_valid(b)
                ms, ls, os_ = [], [], []
                for hh in range(2):
                    qh = jnp.where(head0 if hh == 0 else ~head0, qb, 0.0).astype(BF16)
                    sc = lax.dot_general(qh, kw, (((1,), (1,)), ((), ())), preferred_element_type=F32)
                    sc = jnp.where(valid, sc, -jnp.inf)
                    mb = jnp.max(sc, axis=1, keepdims=True)
                    p = jnp.exp(sc - mb)
                    ms.append(mb)
                    ls.append(jnp.sum(p, axis=1, keepdims=True))
                    os_.append(jnp.dot(p.astype(BF16), vw, preferred_element_type=F32))
                m_b = jnp.where(head0, ms[0], ms[1])
                l_b = jnp.where(head0, ls[0], ls[1])
                o_b = jnp.where(head0, os_[0], os_[1])
                if first:
                    m_s[sq, :] = m_b
                    l_s[sq, :] = l_b
                    acc[sq, :] = o_b
                else:
                    m_old = m_s[sq, :]
                    m_new = jnp.maximum(m_old, m_b)
                    a_old = jnp.exp(m_old - m_new)
                    a_b = jnp.exp(m_b - m_new)
                    m_s[sq, :] = m_new
                    l_s[sq, :] = l_s[sq, :] * a_old + l_b * a_b
                    acc[sq, :] = acc[sq, :] * a_old + o_b * a_b
                return carry

            lax.fori_loop(0, _ATT_NBLK, blk, 0, unroll=_ATT_UNROLL_FWD)

        out = acc[...] / l_s[...]
        o_ref[...] = out
        mix_ref[...] = out.astype(BF16)
        lse_ref[...] = m_s[...] + jnp.log(l_s[...])

    col = lambda base: pl.BlockSpec((s, 128), lambda p: (0, base + p))
    tab = pl.BlockSpec((s, 128), lambda p: (0, 0))
    big = pltpu.VMEM((ATT_KPAD + s, 128), F32)
    tok = pltpu.VMEM((s, 128), F32)
    return pl.pallas_call(
        body, grid=(2,), in_specs=[col(PROJ_Q_BLK), col(PROJ_K_BLK), col(PROJ_V_BLK), tab, tab, tab, _HBM],
        out_specs=[pl.BlockSpec((s, 128), lambda p: (0, p)), pl.BlockSpec((s, 128), lambda p: (0, p)),
                   col(MIX_ATT_BLK)],
        out_shape=[jax.ShapeDtypeStruct((s, ATT_W), F32), jax.ShapeDtypeStruct((s, ATT_W), F32),
                   jax.ShapeDtypeStruct(mix.shape, mix.dtype)],
        input_output_aliases={6: 2}, scratch_shapes=[tok, big, big, tok, tok, tok], name=name,
        compiler_params=_cparams("arbitrary"))(proj, proj, proj, cos, sin1, sin2, mix)


def _att_bwd(proj, cos, sin1, sin2, out, lse, dmix, dproj, *, name):
    s = SEQ

    def body(proj_ref, c_hbm, s1_hbm, s2_hbm, out_hbm, lse_hbm, dmix_hbm, _, dproj_hbm,
             c_ref, s1_ref, s2_ref, o_ref, lse_ref, do_ref, qs, ks, vs, dqs, dks, dvs, staged, sems):
        def start(copies):
            for cp in copies:
                cp.start()
            return copies

        def load(pair):
            lanes = pl.ds(128 * pair, 128)
            rows = pl.ds(ATT_KPAD, s)
            return start([
                pltpu.make_async_copy(proj_ref.at[:, pl.ds(128 * (PROJ_Q_BLK + pair), 128)], qs, sems.at[0]),
                pltpu.make_async_copy(proj_ref.at[:, pl.ds(128 * (PROJ_K_BLK + pair), 128)], ks.at[rows, :], sems.at[1]),
                pltpu.make_async_copy(proj_ref.at[:, pl.ds(128 * (PROJ_V_BLK + pair), 128)], vs.at[rows, :], sems.at[2]),
                pltpu.make_async_copy(out_hbm.at[:, lanes], o_ref, sems.at[3]),
                pltpu.make_async_copy(lse_hbm.at[:, lanes], lse_ref, sems.at[4]),
                pltpu.make_async_copy(dmix_hbm.at[:, pl.ds(128 * (MIX_ATT_BLK + pair), 128)], do_ref, sems.at[5])])

        tables = start([pltpu.make_async_copy(c_hbm, c_ref, sems.at[6]),
                        pltpu.make_async_copy(s1_hbm, s1_ref, sems.at[7]),
                        pltpu.make_async_copy(s2_hbm, s2_ref, sems.at[8])])
        loads = load(0)
        for cp in tables:
            cp.wait()
        head0 = _head_of_lane(128) == 0
        zeros = jnp.zeros((ATT_KPAD, 128), F32)
        for pair in range(2):
            for cp in loads:
                cp.wait()
            c, s1, s2 = c_ref[...], s1_ref[...], s2_ref[...]
            qs[...] = _rope(qs[...], c, s1, s2) * _ATT_SCALE
            ks[pl.ds(0, ATT_KPAD), :] = zeros
            vs[pl.ds(0, ATT_KPAD), :] = zeros
            ks[pl.ds(ATT_KPAD, s), :] = _rope(ks[pl.ds(ATT_KPAD, s), :], c, s1, s2)
            dqs[...] = jnp.zeros_like(dqs)
            dks[...] = jnp.zeros_like(dks)
            dvs[...] = jnp.zeros_like(dvs)

            for _, d in ATT_PATTERNS:
                def blk(i, carry, d=d):
                    b, sq, sk = _att_slices(i, d)
                    qb = qs[sq, :]
                    kw = ks[sk, :].astype(BF16)
                    vw = vs[sk, :].astype(BF16)
                    dob = do_ref[sq, :]
                    lse_b = lse_ref[sq, :]
                    dd = dob * o_ref[sq, :]
                    valid = _att_valid(b)
                    dq_b = jnp.zeros((ATT_BLOCK, 128), F32)
                    dk_w = jnp.zeros((2 * ATT_BLOCK, 128), F32)
                    dv_w = jnp.zeros((2 * ATT_BLOCK, 128), F32)
                    for hh in range(2):
                        hm = head0 if hh == 0 else ~head0
                        qh = jnp.where(hm, qb, 0.0).astype(BF16)
                        doh = jnp.where(hm, dob, 0.0).astype(BF16)
                        lse_h = _pick_lane(lse_b, hh * HEAD_LANES)
                        d_h = jnp.sum(jnp.where(hm, dd, 0.0), axis=1, keepdims=True)
                        sc = lax.dot_general(qh, kw, (((1,), (1,)), ((), ())), preferred_element_type=F32)
                        p = jnp.where(valid, jnp.exp(sc - lse_h), 0.0)
                        dp = lax.dot_general(doh, vw, (((1,), (1,)), ((), ())), preferred_element_type=F32)
                        ds = (p * (dp - d_h)).astype(BF16)
                        dq_b = dq_b + jnp.where(hm, jnp.dot(ds, kw, preferred_element_type=F32), 0.0)
                        dk_w = dk_w + lax.dot_general(ds, qh, (((0,), (0,)), ((), ())), preferred_element_type=F32)
                        dv_w = dv_w + lax.dot_general(p.astype(BF16), doh, (((0,), (0,)), ((), ())),
                                                      preferred_element_type=F32)
                    dqs[sq, :] += dq_b
                    dks[sk, :] += dk_w
                    dvs[sk, :] += dv_w
                    return carry

                lax.fori_loop(0, _ATT_NBLK, blk, 0, unroll=_ATT_UNROLL)

            staged[0] = _rope_t(dqs[...] * _ATT_SCALE, c, s1, s2).astype(BF16)
            staged[1] = _rope_t(dks[pl.ds(ATT_KPAD, s), :], c, s1, s2).astype(BF16)
            staged[2] = dvs[pl.ds(ATT_KPAD, s), :].astype(BF16)
            stores = start([
                pltpu.make_async_copy(staged.at[j], dproj_hbm.at[:, pl.ds(128 * (col + pair), 128)], sems.at[9 + j])
                for j, col in enumerate((PROJ_Q_BLK, PROJ_K_BLK, PROJ_V_BLK))])
            if pair == 0:
                loads = load(1)
            for cp in stores:
                cp.wait()

    big = pltpu.VMEM((ATT_KPAD + s, 128), F32)
    tok = pltpu.VMEM((s, 128), F32)
    return pl.pallas_call(
        body, in_specs=[_HBM] * 8, out_specs=_HBM, out_shape=jax.ShapeDtypeStruct(dproj.shape, dproj.dtype),
        input_output_aliases={7: 0},
        scratch_shapes=[tok] * 6 + [tok, big, big, tok, big, big, pltpu.VMEM((3, s, 128), BF16),
                                    pltpu.SemaphoreType.DMA((12,))], name=name,
        compiler_params=pltpu.CompilerParams(vmem_limit_bytes=VMEM_LIMIT))(
            proj, cos, sin1, sin2, out, lse, dmix, dproj)


_POOL_TM = 512
_POOL_NT = SEQ // _POOL_TM
_POOL_HALO_PER_TILE = _POOL_TM // POOL_HALO


def _pool_tile(u_ext, w_bd, scale, t0):
    s2 = u_ext + _roll_rows(u_ext, 1)
    s4 = s2 + _roll_rows(s2, 2)
    s8 = s4 + _roll_rows(s4, 4)
    s16 = s8 + _roll_rows(s8, 8)
    grp = _head_of_lane(POOL_W)
    sel = jnp.where(grp == 0, s2, jnp.where(grp == 1, s4, jnp.where(grp == 2, s8, s16)))[POOL_HALO:]
    t = sel.shape[0]
    pos = t0 + lax.broadcasted_iota(jnp.int32, (t, POOL_W), 0) + 1
    win = jnp.where(grp == 0, 2, jnp.where(grp == 1, 4, jnp.where(grp == 2, 8, 16)))
    cnt = jnp.minimum(pos, win).astype(F32)
    diff = sel / cnt - u_ext[POOL_HALO:]
    return jnp.dot(diff.astype(BF16), w_bd.astype(BF16), preferred_element_type=F32) * scale


def _pool_fwd(proj, w_bd, scale, mix, *, name):
    tm = _POOL_TM

    def body(u_ref, halo_ref, w_ref, sc_ref, _, y_ref):
        i = pl.program_id(0)
        halo = jnp.where(i == 0, 0.0, halo_ref[...])
        u_ext = jnp.concatenate([halo, u_ref[...]], axis=0)
        y_ref[...] = _pool_tile(u_ext, w_ref[...], sc_ref[...], i * tm).astype(BF16)

    return pl.pallas_call(
        body, grid=(_POOL_NT,),
        in_specs=[pl.BlockSpec((tm, POOL_W), lambda i: (i, PROJ_POOL_BLK)),
                  pl.BlockSpec((POOL_HALO, POOL_W),
                               lambda i: (jnp.maximum(i * _POOL_HALO_PER_TILE - 1, 0), PROJ_POOL_BLK)),
                  pl.BlockSpec((POOL_W, POOL_W), lambda i: (0, 0)), pl.BlockSpec((1, POOL_W), lambda i: (0, 0)), _HBM],
        out_specs=pl.BlockSpec((tm, POOL_W), lambda i: (i, MIX_POOL_BLK)),
        out_shape=jax.ShapeDtypeStruct(mix.shape, mix.dtype), input_output_aliases={4: 0}, name=name,
        compiler_params=_cparams("parallel"))(proj, proj, w_bd, scale, mix)


def _pool_bwd(proj, dmix, w_bd, scale, dproj, *, name):
    tm = _POOL_TM
    last = _POOL_NT - 1

    def body(u_ref, halo_ref, dy_ref, w_ref, sc_ref, _, du_ref, dw_ref, dsc_ref, dhalo_acc):
        i = pl.program_id(0)

        @pl.when(i == 0)
        def _():
            dhalo_acc[...] = jnp.zeros_like(dhalo_acc)
            dw_ref[...] = jnp.zeros_like(dw_ref)
            dsc_ref[...] = jnp.zeros_like(dsc_ref)

        tile = last - i
        halo = jnp.where(tile == 0, 0.0, halo_ref[...])
        u_ext = jnp.concatenate([halo, u_ref[...]], axis=0)
        _, vjp = jax.vjp(functools.partial(_pool_tile, t0=tile * tm), u_ext, w_ref[...], sc_ref[...])
        gu, gw, gs = vjp(dy_ref[...])
        du_ref[...] = jnp.concatenate([gu[POOL_HALO:tm], gu[tm:] + dhalo_acc[...]], axis=0).astype(BF16)
        dhalo_acc[...] = gu[:POOL_HALO]
        dw_ref[...] += gw
        dsc_ref[...] += gs

    rev = lambda i: last - i
    return pl.pallas_call(
        body, grid=(_POOL_NT,),
        in_specs=[pl.BlockSpec((tm, POOL_W), lambda i: (rev(i), PROJ_POOL_BLK)),
                  pl.BlockSpec((POOL_HALO, POOL_W),
                               lambda i: (jnp.maximum(rev(i) * _POOL_HALO_PER_TILE - 1, 0), PROJ_POOL_BLK)),
                  pl.BlockSpec((tm, POOL_W), lambda i: (rev(i), MIX_POOL_BLK)),
                  pl.BlockSpec((POOL_W, POOL_W), lambda i: (0, 0)), pl.BlockSpec((1, POOL_W), lambda i: (0, 0)), _HBM],
        out_specs=[pl.BlockSpec((tm, POOL_W), lambda i: (rev(i), PROJ_POOL_BLK)),
                   pl.BlockSpec((POOL_W, POOL_W), lambda i: (0, 0)), pl.BlockSpec((1, POOL_W), lambda i: (0, 0))],
        out_shape=[jax.ShapeDtypeStruct(dproj.shape, dproj.dtype), jax.ShapeDtypeStruct((POOL_W, POOL_W), F32),
                   jax.ShapeDtypeStruct((1, POOL_W), F32)],
        input_output_aliases={5: 0}, scratch_shapes=[pltpu.VMEM((POOL_HALO, POOL_W), F32)], name=name,
        compiler_params=_cparams("arbitrary"))(proj, proj, dmix, w_bd, scale, dproj)


_FFN_TM = 256
_FFN_NT = SEQ // _FFN_TM
_FFN_HALO_PER_TILE = _FFN_TM // CONV_HALO


def _ffn_act_tile(hid_ext, conv_w, conv_b):
    hc = conv_b
    for k in range(FFN_CONV_K):
        hc = hc + _pick_row(conv_w, k) * _shift_rows(hid_ext, FFN_CONV_K - 1 - k, CONV_HALO)
    return _silu(hc[:, :FFN_DIM]) * hc[:, FFN_DIM:]


def _ffn_fwd(x1, norm_g, sc, sh, gate, up_t, down, conv_w, conv_b, *, name):
    tm = _FFN_TM
    w = 2 * FFN_DIM
    d = D_MODEL

    def body(x_ref, ng_ref, sc_ref, sh_ref, g_ref, up_ref, dn_ref, cw_ref, cb_ref,
             h_ref, hid_ref, act_ref, f_ref, x2_ref, halo_acc):
        i = pl.program_id(0)
        h2 = _rms_modulate(x_ref[...], ng_ref[...], sc_ref[...], sh_ref[...]).astype(BF16)
        h_ref[...] = h2
        hid = lax.dot_general(h2, up_ref[...], (((1,), (1,)), ((), ())), preferred_element_type=F32)
        hid_ref[...] = hid
        halo = jnp.where(i == 0, 0.0, halo_acc[...])
        act = _ffn_act_tile(jnp.concatenate([halo, hid], axis=0), cw_ref[...], cb_ref[...]).astype(BF16)
        halo_acc[...] = hid[tm - CONV_HALO:]
        act_ref[...] = act
        f = jnp.dot(act, dn_ref[...], preferred_element_type=F32)
        f_ref[...] = f
        x2_ref[...] = x_ref[...] + g_ref[...] * f

    tile = lambda n: pl.BlockSpec((tm, n), lambda i: (i, 0))
    return pl.pallas_call(
        body, grid=(_FFN_NT,),
        in_specs=[tile(d)] + [_resident((1, d))] * 4 + [_resident((w, d)), _resident((FFN_DIM, d)),
                                                        _resident((8, w)), _resident((1, w))],
        out_specs=[tile(d), tile(w), tile(FFN_DIM), tile(d), tile(d)],
        out_shape=[jax.ShapeDtypeStruct((SEQ, d), BF16), jax.ShapeDtypeStruct((SEQ, w), F32),
                   jax.ShapeDtypeStruct((SEQ, FFN_DIM), BF16), jax.ShapeDtypeStruct((SEQ, d), F32),
                   jax.ShapeDtypeStruct((SEQ, d), F32)],
        scratch_shapes=[pltpu.VMEM((CONV_HALO, w), F32)], name=name,
        compiler_params=_cparams("arbitrary"))(x1, norm_g, sc, sh, gate, up_t, down, conv_w, conv_b)


def _ffn_bwd(dx2, gate, f, hid, up_t, down, conv_w, conv_b, *, name):
    tm = _FFN_TM
    w = 2 * FFN_DIM
    d = D_MODEL
    last = _FFN_NT - 1

    def body(dx_ref, g_ref, f_ref, h_ref, halo_ref, up_ref, dn_ref, cw_ref, cb_ref,
             df_ref, dg_ref, dh_ref, dh2_ref, dcw_ref, dcb_ref, dhalo_acc):
        i = pl.program_id(0)

        @pl.when(i == 0)
        def _():
            dhalo_acc[...] = jnp.zeros_like(dhalo_acc)
            dcw_ref[...] = jnp.zeros_like(dcw_ref)
            dcb_ref[...] = jnp.zeros_like(dcb_ref)
            dg_ref[...] = jnp.zeros_like(dg_ref)

        dxv = dx_ref[...]
        df = (g_ref[...] * dxv).astype(BF16)
        df_ref[...] = df
        dg_ref[...] += jnp.sum(dxv * f_ref[...], axis=0, keepdims=True)
        dact = lax.dot_general(df, dn_ref[...], (((1,), (1,)), ((), ())), preferred_element_type=F32)
        halo = jnp.where(i == last, 0.0, halo_ref[...])
        hid_ext = jnp.concatenate([halo, h_ref[...]], axis=0)
        _, vjp = jax.vjp(_ffn_act_tile, hid_ext, cw_ref[...], cb_ref[...])
        gh, gw, gb = vjp(dact)
        dhid = jnp.concatenate([gh[CONV_HALO:tm], gh[tm:] + dhalo_acc[...]], axis=0).astype(BF16)
        dhalo_acc[...] = gh[:CONV_HALO]
        dh_ref[...] = dhid
        dh2_ref[...] = jnp.dot(dhid, up_ref[...], preferred_element_type=F32)
        dcw_ref[...] += gw
        dcb_ref[...] += gb

    rev = lambda i: last - i
    tile = lambda n: pl.BlockSpec((tm, n), lambda i: (rev(i), 0))
    acc = lambda shape: pl.BlockSpec(shape, lambda i: (0, 0))
    return pl.pallas_call(
        body, grid=(_FFN_NT,),
        in_specs=[tile(d), _resident((1, d)), tile(d), tile(w),
                  pl.BlockSpec((CONV_HALO, w), lambda i: (jnp.maximum(rev(i) * _FFN_HALO_PER_TILE - 1, 0), 0)),
                  _resident((w, d)), _resident((FFN_DIM, d)), _resident((8, w)), _resident((1, w))],
        out_specs=[tile(d), acc((1, d)), tile(w), tile(d), acc((8, w)), acc((1, w))],
        out_shape=[jax.ShapeDtypeStruct((SEQ, d), BF16), jax.ShapeDtypeStruct((1, d), F32),
                   jax.ShapeDtypeStruct((SEQ, w), BF16), jax.ShapeDtypeStruct((SEQ, d), F32),
                   jax.ShapeDtypeStruct((8, w), F32), jax.ShapeDtypeStruct((1, w), F32)],
        scratch_shapes=[pltpu.VMEM((CONV_HALO, w), F32)], name=name,
        compiler_params=_cparams("arbitrary"))(dx2, gate, f, hid, hid, up_t, down, conv_w, conv_b)


def _axes():
    return lax.axis_index("x"), lax.axis_index("y"), lax.axis_index("c")


def _handshake(peers):
    barrier = pltpu.get_barrier_semaphore()
    for peer in peers:
        pl.semaphore_signal(barrier, inc=1, device_id=peer, device_id_type=MESH)
    pl.semaphore_wait(barrier, len(peers))


def _allgather_body(x_refs, out_refs, send_sems, recv_sems, local_sems, own_barrier):
    n = len(x_refs)
    x, y, c = _axes()
    me, sibling = (x, y, c), (x, y, 1 - c)
    chips = [(1 - x, y), (x, 1 - y), (1 - x, 1 - y)]
    if own_barrier:
        _handshake([sibling] + [(*chip, c) for chip in chips])

    def slot(a, px, py, pc):
        return out_refs[a].at[4 * px + 2 * py + pc]

    def copy(a, k, block, to, src=None):
        return pltpu.make_async_remote_copy(
            src_ref=slot(a, *block) if src is None else src, dst_ref=slot(a, *block),
            send_sem=send_sems.at[a, k], recv_sem=recv_sems.at[a, k], device_id=to, device_id_type=MESH)

    mines, firsts = [], []
    for a in range(n):
        mines.append(pltpu.make_async_copy(x_refs[a], slot(a, *me), local_sems.at[a]))
        mines[-1].start()
        first = [copy(a, 0, me, sibling, src=x_refs[a])]
        first += [copy(a, 1 + j, me, (*chip, c), src=x_refs[a]) for j, chip in enumerate(chips)]
        for cp in first:
            cp.start()
        firsts += first
    passed = []
    for j, chip in enumerate(chips):
        for a in range(n):
            copy(a, 1 + j, (*chip, c), me).wait_recv()
            passed.append(copy(a, 4 + j, (*chip, c), sibling))
            passed[-1].start()
    for a in range(n):
        copy(a, 0, sibling, me).wait_recv()
    for j, chip in enumerate(chips):
        for a in range(n):
            copy(a, 4 + j, (*chip, 1 - c), me).wait_recv()
    for cp in firsts + passed:
        cp.wait_send()
    for cp in mines:
        cp.wait()


def _allgather_sems(n):
    return [pltpu.SemaphoreType.DMA((n, 7)), pltpu.SemaphoreType.DMA((n, 7)), pltpu.SemaphoreType.DMA((n,))]


def _allgather(xs, *, name):
    n = len(xs)

    def body(*refs):
        _allgather_body(refs[:n], refs[n:2 * n], *refs[2 * n:], own_barrier=False)

    return pl.pallas_call(
        body, out_shape=[jax.ShapeDtypeStruct((N_DEV,) + xb.shape, xb.dtype) for xb in xs],
        in_specs=[_HBM] * n, out_specs=[_HBM] * n, scratch_shapes=_allgather_sems(n), name=name)(*xs)


def _allgather_async(xs, *, name, collective_id):
    n = len(xs)
    x_refs = [jax.new_ref(xb, memory_space=pltpu.MemorySpace.HBM) for xb in xs]
    out_refs = [jax.empty_ref(jax.ShapeDtypeStruct((N_DEV,) + xb.shape, xb.dtype), memory_space=pltpu.MemorySpace.HBM)
                for xb in xs]

    @pl.kernel(mesh=plsc.ScalarSubcoreMesh(axis_name="sequencer", num_cores=1), name=name,
               scratch_types=tuple(_allgather_sems(n)),
               compiler_params=pltpu.CompilerParams(collective_id=collective_id))
    def launch(send_sems, recv_sems, local_sems):
        _allgather_body(x_refs, out_refs, send_sems, recv_sems, local_sems, own_barrier=True)

    launch()
    return [r[...] for r in out_refs]


def _pair_exchange(blocks, *, name, collective_id):
    n = len(blocks)
    hbm = pltpu.MemorySpace.HBM
    in_refs = [jax.new_ref(b, memory_space=hbm) for b in blocks]
    out_refs = [jax.empty_ref(jax.ShapeDtypeStruct((4,) + b.shape[1:], b.dtype), memory_space=hbm) for b in blocks]

    @pl.kernel(mesh=plsc.ScalarSubcoreMesh(axis_name="sequencer", num_cores=1), name=name,
               scratch_types=(pltpu.SemaphoreType.DMA((n, 4)), pltpu.SemaphoreType.DMA((n, 4))),
               compiler_params=pltpu.CompilerParams(collective_id=collective_id))
    def launch(send_sems, recv_sems):
        x, y, c = _axes()
        _handshake([(x, y, 1 - c)])
        copies = [pltpu.make_async_remote_copy(
            src_ref=in_refs[a].at[2 * s + (1 - c)], dst_ref=out_refs[a].at[s], send_sem=send_sems.at[a, s],
            recv_sem=recv_sems.at[a, s], device_id=(x, y, 1 - c), device_id_type=MESH)
            for a in range(n) for s in range(4)]
        for cp in copies:
            cp.start()
        for cp in copies:
            cp.wait_recv()
        for cp in copies:
            cp.wait_send()

    launch()
    return [r[...] for r in out_refs]


def _chip_exchange(parts, *, name, collective_id):
    n = len(parts)
    hbm = pltpu.MemorySpace.HBM
    in_refs = [jax.new_ref(p, memory_space=hbm) for p in parts]
    out_refs = [jax.empty_ref(jax.ShapeDtypeStruct(p.shape, p.dtype), memory_space=hbm) for p in parts]

    @pl.kernel(mesh=plsc.ScalarSubcoreMesh(axis_name="sequencer", num_cores=1), name=name,
               scratch_types=(pltpu.SemaphoreType.DMA((n, 3)), pltpu.SemaphoreType.DMA((n, 3)),
                              pltpu.SemaphoreType.DMA((n,))),
               compiler_params=pltpu.CompilerParams(collective_id=collective_id))
    def launch(send_sems, recv_sems, local_sems):
        x, y, c = _axes()
        my_chip = 2 * x + y
        chips = [(1 - x, y), (x, 1 - y), (1 - x, 1 - y)]
        _handshake([(*chip, c) for chip in chips])
        locals_ = [pltpu.make_async_copy(in_refs[a].at[my_chip], out_refs[a].at[my_chip], local_sems.at[a])
                   for a in range(n)]
        for cp in locals_:
            cp.start()
        copies = [pltpu.make_async_remote_copy(
            src_ref=in_refs[a].at[2 * px + py], dst_ref=out_refs[a].at[my_chip], send_sem=send_sems.at[a, k],
            recv_sem=recv_sems.at[a, k], device_id=(px, py, c), device_id_type=MESH)
            for a in range(n) for k, (px, py) in enumerate(chips)]
        for cp in copies:
            cp.start()
        for cp in copies:
            cp.wait_recv()
        for cp in copies:
            cp.wait_send()
        for cp in locals_:
            cp.wait()

    launch()
    return [r[...] for r in out_refs]


def _pair_sum(core, blocks, from_sibling, *, name):
    n = len(blocks)

    def body(core_ref, *refs):
        for a_ref, b_ref, o_ref in zip(refs[:n], refs[n:2 * n], refs[2 * n:]):
            o_ref[...] = (a_ref[...].astype(F32) + b_ref[...].astype(F32)).astype(o_ref.dtype)

    mine = lambda b: pl.BlockSpec((1,) + b.shape[1:], lambda s, core_ref: (2 * s + core_ref[0], 0, 0))
    slot = lambda b: pl.BlockSpec((1,) + b.shape[1:], lambda s, core_ref: (s, 0, 0))
    return pl.pallas_call(
        body,
        grid_spec=pltpu.PrefetchScalarGridSpec(
            num_scalar_prefetch=1, grid=(4,),
            in_specs=[mine(b) for b in blocks] + [slot(b) for b in blocks], out_specs=[slot(b) for b in blocks]),
        out_shape=[jax.ShapeDtypeStruct(s.shape, s.dtype) for s in from_sibling], name=name,
        compiler_params=_cparams("parallel"))(core, *blocks, *from_sibling)


def _sum_gathered(gathered, *, name):
    n = len(gathered)

    def body(*refs):
        for a_ref, o_ref in zip(refs[:n], refs[n:]):
            acc = a_ref[0].astype(F32)
            for k in range(1, a_ref.shape[0]):
                acc = acc + a_ref[k].astype(F32)
            o_ref[...] = acc

    return pl.pallas_call(body, out_shape=[jax.ShapeDtypeStruct(g.shape[1:], F32) for g in gathered], name=name,
                          compiler_params=pltpu.CompilerParams(vmem_limit_bytes=VMEM_LIMIT))(*gathered)


_ADA_SHARD = 6 * D_MODEL // N_DEV


def _ada_mod(c_all, ada_w, *, name):
    def body(c_ref, w_ref, o_ref):
        o_ref[0] = jnp.dot(_silu(c_ref[...]).astype(BF16), w_ref[0].astype(BF16), preferred_element_type=F32)

    return pl.pallas_call(
        body, grid=(DEPTH,),
        in_specs=[pl.BlockSpec((N_DEV, D_MODEL), lambda l: (0, 0)),
                  pl.BlockSpec((1, D_MODEL, _ADA_SHARD), lambda l: (l, 0, 0))],
        out_specs=pl.BlockSpec((1, N_DEV, _ADA_SHARD), lambda l: (l, 0, 0)),
        out_shape=jax.ShapeDtypeStruct((DEPTH, N_DEV, _ADA_SHARD), F32), name=name,
        compiler_params=_cparams("parallel"))(c_all, ada_w)


def _ada_wgrad(c_all, dmod_cols, *, name):
    def body(c_ref, d_ref, o_ref):
        o_ref[0] = lax.dot_general(_silu(c_ref[...]), d_ref[0], (((0,), (0,)), ((), ())),
                                   preferred_element_type=F32, precision=lax.Precision.HIGHEST)

    return pl.pallas_call(
        body, grid=(DEPTH,),
        in_specs=[pl.BlockSpec((N_DEV, D_MODEL), lambda l: (0, 0)),
                  pl.BlockSpec((1, N_DEV, _ADA_SHARD), lambda l: (l, 0, 0))],
        out_specs=pl.BlockSpec((1, D_MODEL, _ADA_SHARD), lambda l: (l, 0, 0)),
        out_shape=jax.ShapeDtypeStruct((DEPTH, D_MODEL, _ADA_SHARD), F32), name=name,
        compiler_params=_cparams("parallel"))(c_all, dmod_cols)


def _add_rows(a, b, *, name):
    def body(a_ref, b_ref, o_ref):
        o_ref[...] = a_ref[...] + b_ref[...]

    return pl.pallas_call(body, out_shape=jax.ShapeDtypeStruct(a.shape, a.dtype), name=name)(a, b)


def _adamw_update(w_ref, g_ref, m_ref, v_ref, d_ref, mo_ref, vo_ref):
    gv = g_ref[...]
    mn = ADAM_B1 * m_ref[...] + (1.0 - ADAM_B1) * gv
    vn = ADAM_B2 * v_ref[...] + (1.0 - ADAM_B2) * (gv * gv)
    mo_ref[...] = mn
    vo_ref[...] = vn
    m_hat = mn / (1.0 - ADAM_B1 ** ADAM_STEP)
    v_hat = vn / (1.0 - ADAM_B2 ** ADAM_STEP)
    d_ref[...] = -ADAM_LR * (m_hat / (jnp.sqrt(v_hat) + ADAM_EPS) + ADAM_WD * w_ref[...])


def _adamw_small(ws, gs, ms, vs, *, name):
    n = len(ws)

    def body(*refs):
        ins, outs = refs[:4 * n], refs[4 * n:]
        for i in range(n):
            _adamw_update(ins[i], ins[n + i], ins[2 * n + i], ins[3 * n + i], outs[i], outs[n + i], outs[2 * n + i])

    shapes = [jax.ShapeDtypeStruct(a.shape, F32) for a in ws]
    outs = pl.pallas_call(body, out_shape=shapes * 3, name=name,
                          compiler_params=pltpu.CompilerParams(vmem_limit_bytes=VMEM_LIMIT))(*ws, *gs, *ms, *vs)
    return outs[:n], outs[n:2 * n], outs[2 * n:]


def _adamw(w, g, m, v, *, name, tr):
    layers, r, cdim = w.shape
    body = functools.partial(_adamw_update)

    spec = pl.BlockSpec((1, tr, cdim), lambda l, i: (l, i, 0))
    shp = jax.ShapeDtypeStruct(w.shape, F32)
    return pl.pallas_call(body, grid=(layers, r // tr), in_specs=[spec] * 4, out_specs=[spec] * 3,
                          out_shape=[shp] * 3, name=name, compiler_params=_cparams("parallel", "parallel"))(w, g, m, v)


def _pad_rows(a, rows):
    return jnp.concatenate([a, jnp.zeros((rows - a.shape[0],) + a.shape[1:], a.dtype)], axis=0)


def _pad_lanes(a, lanes):
    return jnp.concatenate([a, jnp.zeros(a.shape[:-1] + (lanes - a.shape[-1],), a.dtype)], axis=-1)


def _permute_w_in(wt):
    return jnp.concatenate([wt[512:1536], wt[:512], wt[1536:1544],
                            jnp.zeros((PROJ_W - IN_W, wt.shape[1]), wt.dtype), wt[1544:]], axis=0)


def _unpermute_w_in(wp):
    return jnp.concatenate([wp[1024:1536], wp[:1024], wp[1536:1544], wp[PROJ_SSD_W:]], axis=0)


def _block_diag(w):
    rows = []
    for g in range(4):
        rows.append(jnp.concatenate([w[g] if k == g else jnp.zeros_like(w[g]) for k in range(4)], axis=1))
    return jnp.concatenate(rows, axis=0)


def _diag_blocks(wbd):
    return jnp.stack([wbd[64 * g:64 * (g + 1), 64 * g:64 * (g + 1)] for g in range(4)], axis=0)


def _layer_params(l, small):
    return dict(
        norm1_g=small["norm1_g"][l][None], norm2_g=small["norm2_g"][l][None],
        conv_w=_pad_rows(small["ssd_conv_w"][l], 8), conv_b=small["ssd_conv_b"][l][None],
        dt_bias=_pad_lanes(small["ssd_dt_bias"][l][None], 128), a_log=_pad_lanes(small["ssd_a_log"][l][None], 128),
        d_skip=_pad_lanes(small["ssd_d"][l][None], 128), ssd_norm_g=small["ssd_norm_g"][l][None],
        pool_bd=_block_diag(small["pool_w"][l]), pool_scale=small["pool_scale"][l][None],
        fcw=_pad_rows(small["ffn_conv_w"][l], 8), fcb=small["ffn_conv_b"][l][None])


def _mod_rows(mod_l):
    return [mod_l[None, D_MODEL * i:D_MODEL * (i + 1)] for i in range(6)]


def _layer_fwd(x, mod_l, p, tabs, l, gather):
    sh1, sc1, g1, sh2, sc2, g2 = _mod_rows(mod_l)
    mix_w = gather(l, "mix", None)
    p.update(w_in=mix_w["w_in"], w_out=mix_w["w_out"])
    proj, h1 = _mm(x, p["w_in"], nt=True, norm=(p["norm1_g"], sc1, sh1), name=f"l{l}_proj")
    ffn_w = gather(l, "ffn", proj)
    p.update(up=ffn_w["ffn_up"], down=ffn_w["ffn_down"])
    mix, hst = _ssd_fwd(proj, p["conv_w"], p["conv_b"], p["dt_bias"], p["a_log"], p["d_skip"], p["ssd_norm_g"],
                        name=f"l{l}_ssd")
    mix = _pool_fwd(proj, p["pool_bd"], p["pool_scale"], mix, name=f"l{l}_pool")
    y_att, lse, mix = _att_fwd(proj, *tabs, mix, name=f"l{l}_att")
    mo, x1 = _mm(mix, p["w_out"], residual=(x, g1), name=f"l{l}_out")
    gather(l + 1, "mix", (x1, p["up"]))
    h2, hid, act, f, x2 = _ffn_fwd(x1, p["norm2_g"], sc2, sh2, g2, p["up"], p["down"], p["fcw"], p["fcb"],
                                   name=f"l{l}_ffn")
    return x2, dict(x=x, h1=h1, proj=proj, hst=hst, y_att=y_att, lse=lse, mix=mix, mo=mo, x1=x1, h2=h2, hid=hid,
                    act=act, f=f)


def _layer_bwd(dx2, sv, mod_l, p, tabs, l, exchange):
    sh1, sc1, g1, sh2, sc2, g2 = _mod_rows(mod_l)
    df, dg2, dhid, dh2, dfcw, dfcb = _ffn_bwd(dx2, g2, sv["f"], sv["hid"], p["up"], p["down"], p["fcw"], p["fcb"],
                                              name=f"l{l}_ffn_b")
    d_down = _wgrad(sv["act"], df, tk=1408, name=f"l{l}_down_bw")
    d_up = _wgrad(dhid, sv["h2"], tk=1408, name=f"l{l}_up_bw")
    exchange(l, "ffn", dict(ffn_up=d_up, ffn_down=d_down))
    dx1, dn2, dsc2, dsh2, dmix, dmo, dg1 = _mid_bwd(sv["x1"], dh2, dx2, p["norm2_g"], sc2, g1, sv["mo"], p["w_out"],
                                                    name=f"l{l}_mid_b")
    d_wout = _wgrad(sv["mix"], dmo, name=f"l{l}_out_bw")
    dproj, dcw, dcb, ddb, dal, dd, dng = _ssd_bwd(
        sv["proj"], sv["hst"], dmix, p["conv_w"], p["conv_b"], p["dt_bias"], p["a_log"], p["d_skip"],
        p["ssd_norm_g"], name=f"l{l}_ssd_b")
    dproj, dwbd, dpsc = _pool_bwd(sv["proj"], dmix, p["pool_bd"], p["pool_scale"], dproj, name=f"l{l}_pool_b")
    dproj = _att_bwd(sv["proj"], *tabs, sv["y_att"], sv["lse"], dmix, dproj, name=f"l{l}_att_b")
    d_win = _wgrad(dproj, sv["h1"], tk=1408, name=f"l{l}_proj_bw")
    exchange(l, "mix", dict(w_in=d_win, w_out=d_wout))
    dx0, dn1, dsc1, dsh1 = _norm_mod_bwd(sv["x"], dproj, dx1, p["norm1_g"], sc1, w=p["w_in"], name=f"l{l}_proj_b")
    dmod = jnp.concatenate([dsh1, dsc1, dg1, dsh2, dsc2, dg2], axis=1)[0]
    small = dict(norm1_g=dn1[0], ssd_conv_w=dcw[:SSD_CONV_K], ssd_conv_b=dcb[0], ssd_dt_bias=ddb[0], ssd_a_log=dal[0],
                 ssd_d=dd[0], ssd_norm_g=dng[0], pool_w=_diag_blocks(dwbd), pool_scale=dpsc[0], norm2_g=dn2[0],
                 ffn_conv_w=dfcw[:FFN_CONV_K], ffn_conv_b=dfcb[0])
    return dx0, dmod, small


def _example_step(x, target, pos_col, inv_freq_lane, mod, gather, small, final_g, exchange):
    tabs = _rope_tables(pos_col, inv_freq_lane, name="rope_tables")
    params, saved = [], []
    for l in range(DEPTH):
        params.append(_layer_params(l, small))
        x, sv = _layer_fwd(x, mod[l], params[l], tabs, l, gather)
        saved.append(sv)
    loss_row, dx, dfg = _final_loss(x, final_g[None], target, name="final_loss")
    dmods, smalls = [None] * DEPTH, [None] * DEPTH
    for l in reversed(range(DEPTH)):
        dx, dmods[l], smalls[l] = _layer_bwd(dx, saved[l], mod[l], params[l], tabs, l, exchange)
    return loss_row, dx, jnp.stack(dmods, axis=0), smalls, dfg[0]


_BIG = ("w_in", "w_out", "ffn_up", "ffn_down")
_SMALL_GRADS = ("norm1_g", "ssd_conv_w", "ssd_conv_b", "ssd_dt_bias", "ssd_a_log", "ssd_d", "ssd_norm_g", "pool_w",
                "pool_scale", "norm2_g", "ffn_conv_w", "ffn_conv_b")
_SMALL_PARAMS = ("ada_b", "norm1_g", "ssd_conv_w", "ssd_conv_b", "ssd_dt_bias", "ssd_a_log", "ssd_d", "ssd_norm_g",
                 "pool_w", "pool_scale", "norm2_g", "ffn_conv_w", "ffn_conv_b", "final_g")
_WEIGHT_ORDER = ("ada_w", "ada_b", "norm1_g", "w_in", "ssd_conv_w", "ssd_conv_b", "ssd_dt_bias", "ssd_a_log", "ssd_d",
                 "ssd_norm_g", "pool_w", "pool_scale", "w_out", "norm2_g", "ffn_up", "ffn_conv_w", "ffn_conv_b",
                 "ffn_down", "final_g")


_COLUMN_SHARDED = ("w_in", "ffn_up")
_GROUPS = (("mix", ("w_in", "w_out")), ("ffn", ("ffn_up", "ffn_down")))


def _big_shares(w, l, names):
    return [(w[name][l].T if name in _COLUMN_SHARDED else w[name][l]).astype(BF16) for name in names]


def _unshard_big(names, gathered):
    out = {}
    for name, g in zip(names, gathered):
        full = g.reshape(N_DEV * g.shape[1], g.shape[2])
        out[name] = _permute_w_in(full) if name == "w_in" else full
    return out


def _shard_big(grads):
    out = []
    for name, g in grads.items():
        g = _unpermute_w_in(g) if name == "w_in" else g
        out.append(g.reshape(N_DEV, g.shape[0] // N_DEV, g.shape[1]))
    return out


def kernel(x, c, positions, ada_w, ada_b, norm1_g, w_in, ssd_conv_w, ssd_conv_b, ssd_dt_bias, ssd_a_log, ssd_d, ssd_norm_g, pool_w, pool_scale, w_out, norm2_g, ffn_up, ffn_conv_w, ffn_conv_b, ffn_down, final_g, loss_target, m_ada_w, m_ada_b, m_norm1_g, m_w_in, m_ssd_conv_w, m_ssd_conv_b, m_ssd_dt_bias, m_ssd_a_log, m_ssd_d, m_ssd_norm_g, m_pool_w, m_pool_scale, m_w_out, m_norm2_g, m_ffn_up, m_ffn_conv_w, m_ffn_conv_b, m_ffn_down, m_final_g, v_ada_w, v_ada_b, v_norm1_g, v_w_in, v_ssd_conv_w, v_ssd_conv_b, v_ssd_dt_bias, v_ssd_a_log, v_ssd_d, v_ssd_norm_g, v_pool_w, v_pool_scale, v_w_out, v_norm2_g, v_ffn_up, v_ffn_conv_w, v_ffn_conv_b, v_ffn_down, v_final_g):
    w = dict(ada_w=ada_w, ada_b=ada_b, norm1_g=norm1_g, w_in=w_in, ssd_conv_w=ssd_conv_w, ssd_conv_b=ssd_conv_b,
             ssd_dt_bias=ssd_dt_bias, ssd_a_log=ssd_a_log, ssd_d=ssd_d, ssd_norm_g=ssd_norm_g, pool_w=pool_w,
             pool_scale=pool_scale, w_out=w_out, norm2_g=norm2_g, ffn_up=ffn_up, ffn_conv_w=ffn_conv_w,
             ffn_conv_b=ffn_conv_b, ffn_down=ffn_down, final_g=final_g)
    m = dict(ada_w=m_ada_w, ada_b=m_ada_b, norm1_g=m_norm1_g, w_in=m_w_in, ssd_conv_w=m_ssd_conv_w,
             ssd_conv_b=m_ssd_conv_b, ssd_dt_bias=m_ssd_dt_bias, ssd_a_log=m_ssd_a_log, ssd_d=m_ssd_d,
             ssd_norm_g=m_ssd_norm_g, pool_w=m_pool_w, pool_scale=m_pool_scale, w_out=m_w_out, norm2_g=m_norm2_g,
             ffn_up=m_ffn_up, ffn_conv_w=m_ffn_conv_w, ffn_conv_b=m_ffn_conv_b, ffn_down=m_ffn_down,
             final_g=m_final_g)
    v = dict(ada_w=v_ada_w, ada_b=v_ada_b, norm1_g=v_norm1_g, w_in=v_w_in, ssd_conv_w=v_ssd_conv_w,
             ssd_conv_b=v_ssd_conv_b, ssd_dt_bias=v_ssd_dt_bias, ssd_a_log=v_ssd_a_log, ssd_d=v_ssd_d,
             ssd_norm_g=v_ssd_norm_g, pool_w=v_pool_w, pool_scale=v_pool_scale, w_out=v_w_out, norm2_g=v_norm2_g,
             ffn_up=v_ffn_up, ffn_conv_w=v_ffn_conv_w, ffn_conv_b=v_ffn_conv_b, ffn_down=v_ffn_down,
             final_g=v_final_g)
    ix, iy, ic = _axes()
    dev = 4 * ix + 2 * iy + ic

    c_all, scw, fcw = _allgather([c, ssd_conv_w.reshape(DEPTH * SSD_CONV_K, -1),
                                  ffn_conv_w.reshape(DEPTH * FFN_CONV_K, -1)], name="gather_small")
    small_all = c_all
    c_all = c_all.reshape(N_DEV, D_MODEL)
    scw = scw.reshape(N_DEV, DEPTH, SSD_CONV_K, -1).transpose(1, 2, 0, 3).reshape(DEPTH, SSD_CONV_K, SSD_CONV_CH)
    fcw = fcw.reshape(N_DEV, DEPTH, FFN_CONV_K, -1).transpose(1, 2, 0, 3).reshape(DEPTH, FFN_CONV_K, 2 * FFN_DIM)

    mod_cols = _ada_mod(c_all, ada_w, name="ada_mod")
    mod_all = _allgather([mod_cols.reshape(DEPTH * N_DEV, _ADA_SHARD)], name="gather_mod")[0]
    mod_all = mod_all.reshape(N_DEV, DEPTH, N_DEV, _ADA_SHARD)
    mod_mine = lax.dynamic_index_in_dim(mod_all, dev, axis=2, keepdims=False)
    mod = _add_rows(mod_mine.transpose(1, 0, 2).reshape(DEPTH, 6 * D_MODEL), ada_b, name="ada_bias")

    fetched = {}

    def gather(l, group, after):
        if l < DEPTH and (l, group) not in fetched:
            names = dict(_GROUPS)[group]
            shares, _ = lax.optimization_barrier((_big_shares(w, l, names), small_all if after is None else after))
            got = _allgather_async(shares, name=f"gather_weights_l{l}_{group}",
                                   collective_id=1 + 2 * l + (group == "ffn"))
            fetched[l, group] = _unshard_big(names, got)
        return fetched.get((l, group))

    core = ic.astype(jnp.int32).reshape(1)
    from_chips = {}

    def exchange(l, group, g):
        cid = 5 + 4 * l + 2 * (group == "mix")
        blocks = _shard_big(g)
        from_sibling = _pair_exchange(blocks, name=f"grads_pair_exchange_l{l}_{group}", collective_id=cid)
        parts = _pair_sum(core, blocks, from_sibling, name=f"grads_pair_sum_l{l}_{group}")
        got = _chip_exchange(parts, name=f"grads_chip_exchange_l{l}_{group}", collective_id=cid + 1)
        from_chips.update({(l, name): t for name, t in zip(g, got)})

    small = dict(norm1_g=norm1_g, norm2_g=norm2_g, ssd_conv_w=scw, ssd_conv_b=ssd_conv_b, ssd_dt_bias=ssd_dt_bias,
                 ssd_a_log=ssd_a_log, ssd_d=ssd_d, ssd_norm_g=ssd_norm_g, pool_w=pool_w, pool_scale=pool_scale,
                 ffn_conv_w=fcw, ffn_conv_b=ffn_conv_b)

    inv_freq = ROPE_THETA ** (-jnp.arange(0, ROT_DIM, 2, dtype=F32) / ROT_DIM)
    lane = jnp.arange(128) % HEAD_LANES
    inv_freq_lane = jnp.where(lane < ROT_DIM, inv_freq[lane % (ROT_DIM // 2)], 0.0)[None, :]
    pos_col = positions.reshape(SEQ, 1).astype(F32)
    loss_row, dx, dmod, g_small, g_final = _example_step(
        x[0], loss_target[0], pos_col, inv_freq_lane, mod, gather, small, final_g, exchange)

    grads = {}
    for name in _BIG:
        per_layer = [_sum_gathered([from_chips[l, name]], name=f"grads_chip_sum_l{l}_{name}")[0]
                     for l in range(DEPTH)]
        grads[name] = jnp.stack([g.T if name in _COLUMN_SHARDED else g for g in per_layer], axis=0)

    small_names = list(_SMALL_GRADS)
    stacked = [jnp.stack([g_small[l][name] for l in range(DEPTH)], axis=0) for name in small_names]
    small_parts = [loss_row, dmod] + [s.reshape(-1, s.shape[-1]) for s in stacked] + [g_final[None]]
    gathered = _allgather(small_parts, name="gather_small_grads")
    total = _sum_gathered(gathered, name="sum_small_grads")
    loss = total[0][0, 0]
    grads["ada_b"] = total[1]
    grads.update(zip(small_names, total[2:-1]))
    grads["final_g"] = total[-1][0]
    dmod_cols = lax.dynamic_slice_in_dim(gathered[1], dev * _ADA_SHARD, _ADA_SHARD, axis=2).transpose(1, 0, 2)
    grads["ada_w"] = _ada_wgrad(c_all, dmod_cols, name="ada_wgrad")
    for name in ("ssd_dt_bias", "ssd_a_log", "ssd_d"):
        grads[name] = grads[name][:, :SSD_HEADS]
    grads["pool_w"] = grads["pool_w"].reshape(pool_w.shape)
    grads["ssd_conv_w"] = lax.dynamic_slice_in_dim(
        grads["ssd_conv_w"].reshape(DEPTH, SSD_CONV_K, SSD_CONV_CH), dev * ssd_conv_w.shape[2], ssd_conv_w.shape[2], axis=2)
    grads["ffn_conv_w"] = lax.dynamic_slice_in_dim(
        grads["ffn_conv_w"].reshape(DEPTH, FFN_CONV_K, 2 * FFN_DIM), dev * ffn_conv_w.shape[2], ffn_conv_w.shape[2], axis=2)

    delta, new_m, new_v = {}, {}, {}
    for name, tr in (("ada_w", 512), ("w_in", 512), ("w_out", 128), ("ffn_up", 512), ("ffn_down", 352)):
        delta[name], new_m[name], new_v[name] = _adamw(w[name], grads[name], m[name], v[name], tr=tr,
                                                       name=f"adamw_{name}")
    two_d = lambda a: a.reshape(-1, a.shape[-1])
    outs = _adamw_small(*[[two_d(t[name]) for name in _SMALL_PARAMS] for t in (w, grads, m, v)], name="adamw_small")
    for name, d_, m_, v_ in zip(_SMALL_PARAMS, *outs):
        delta[name], new_m[name], new_v[name] = (t.reshape(w[name].shape) for t in (d_, m_, v_))

    grad_x = dx[None]
    return (loss, grad_x, *[grads[n].reshape(w[n].shape) for n in _WEIGHT_ORDER],
            *[delta[n] for n in _WEIGHT_ORDER], *[new_m[n] for n in _WEIGHT_ORDER],
            *[new_v[n] for n in _WEIGHT_ORDER])
```

```python
import functools
import math

import jax
import jax.numpy as jnp
from jax import lax
from jax.experimental import pallas as pl
from jax.experimental.pallas import tpu as pltpu
from jax.experimental.pallas import tpu_sc as plsc

F32 = jnp.float32
BF16 = jnp.bfloat16

N_DEV = 8
D_MODEL = 1024
SEQ = 4096
DEPTH = 2
SSD_INNER = 512
SSD_HEADS = 8
SSD_HPG = 4
SSD_STATE = 128
SSD_CHUNK = 256
SSD_CONV_K = 4
SSD_CONV_CH = 1024
POOL_W = 256
POOL_WINDOWS = (2, 4, 8, 16)
ATT_W = 256
ATT_PATTERNS = ((128, 1), (512, 4), (2048, 16))
ATT_BLOCK = 128
ROT_DIM = 16
ROPE_THETA = 500000.0
IN_W = 2568
FFN_DIM = 2816
FFN_CONV_K = 3
NORM_EPS = 1e-6
HEAD_LANES = 64

ADAM_LR = 0.001
ADAM_B1 = 0.9
ADAM_B2 = 0.999
ADAM_EPS = 1e-08
ADAM_WD = 0.01
ADAM_STEP = 10

PROJ_W = 2816
PROJ_SSD_W = 1792
PROJ_Z_BLK = 2
PROJ_DT_BLK = 12
PROJ_POOL_BLK = 7
PROJ_Q_BLK, PROJ_K_BLK, PROJ_V_BLK = 16, 18, 20
MIX_POOL_BLK = 2
MIX_ATT_BLK = 6
VMEM_LIMIT = 56 * 1024 * 1024
ROW_TILE = 256
CONV_HALO = 8
POOL_HALO = 16
ATT_KPAD = ATT_BLOCK * 16
MESH = pl.DeviceIdType.MESH
_HBM = pl.BlockSpec(memory_space=pl.ANY)


def _cparams(*sem):
    return pltpu.CompilerParams(dimension_semantics=sem, vmem_limit_bytes=VMEM_LIMIT)


def _resident(shape):
    return pl.BlockSpec(shape, lambda i: (0,) * len(shape), pipeline_mode=pl.Buffered(1))


def _silu(x):
    return x * jax.nn.sigmoid(x)


def _pick_lane(v, h):
    lane = lax.broadcasted_iota(jnp.int32, v.shape, 1)
    return jnp.sum(jnp.where(lane == h, v, 0.0), axis=1, keepdims=True)


def _pick_row(v, h):
    row = lax.broadcasted_iota(jnp.int32, v.shape, 0)
    return jnp.sum(jnp.where(row == h, v, 0.0), axis=0, keepdims=True)


def _head_of_lane(width):
    return lax.broadcasted_iota(jnp.int32, (1, width), 1) // HEAD_LANES


@functools.partial(jax.custom_vjp, nondiff_argnums=(1, 2))
def _shift_rows(x_ext, s, halo):
    y = x_ext if s == 0 else pltpu.roll(x_ext, s, 0)
    return y[halo:]


def _shift_rows_fwd(x_ext, s, halo):
    return _shift_rows(x_ext, s, halo), None


def _shift_rows_bwd(s, halo, _, g):
    ge = jnp.concatenate([jnp.zeros((halo, g.shape[1]), g.dtype), g], axis=0)
    return (ge if s == 0 else pltpu.roll(ge, ge.shape[0] - s, 0),)


_shift_rows.defvjp(_shift_rows_fwd, _shift_rows_bwd)


@functools.partial(jax.custom_vjp, nondiff_argnums=(1,))
def _roll_rows(x, s):
    return pltpu.roll(x, s, 0)


def _roll_rows_fwd(x, s):
    return _roll_rows(x, s), None


def _roll_rows_bwd(s, _, g):
    return (pltpu.roll(g, g.shape[0] - s, 0),)


_roll_rows.defvjp(_roll_rows_fwd, _roll_rows_bwd)


def _rms_modulate(xv, g, sc, sh):
    r = lax.rsqrt(jnp.mean(xv * xv, axis=-1, keepdims=True) + NORM_EPS)
    return (xv * r * g) * (1.0 + sc) + sh


def _mm(a, w, *, name, nt=False, tm=ROW_TILE, tn=None, out_dtype=F32, norm=None, residual=None):
    t, k = a.shape
    n = w.shape[0] if nt else w.shape[1]
    tn = tn or n
    assert tn == n or (norm is None and residual is None)
    extra_in = list(norm or ()) + list(residual or ())

    def body(*refs):
        a_ref, w_ref = refs[:2]
        ins = refs[2:2 + len(extra_in)]
        outs = refs[2 + len(extra_in):]
        if norm is None:
            av = a_ref[...].astype(BF16)
        else:
            av = _rms_modulate(a_ref[...], ins[0][...], ins[1][...], ins[2][...]).astype(BF16)
            outs[1][...] = av
        if nt:
            acc = lax.dot_general(av, w_ref[...], (((1,), (1,)), ((), ())), preferred_element_type=F32)
        else:
            acc = jnp.dot(av, w_ref[...], preferred_element_type=F32)
        outs[0][...] = acc.astype(out_dtype)
        if residual is not None:
            x_ref, gate_ref = ins[-2:]
            outs[-1][...] = x_ref[...] + gate_ref[...] * acc

    row = lambda width: pl.BlockSpec((1, width), lambda i, j: (0, 0))
    tile = lambda width: pl.BlockSpec((tm, width), lambda i, j: (i, 0))
    w_spec = pl.BlockSpec((tn, k), lambda i, j: (j, 0)) if nt else pl.BlockSpec((k, tn), lambda i, j: (0, j))
    in_specs = [tile(k), w_spec] + ([row(k)] * 3 if norm else []) + ([tile(n), row(n)] if residual else [])
    out_specs = [pl.BlockSpec((tm, tn), lambda i, j: (i, j))] + ([tile(k)] if norm else []) + \
        ([tile(n)] if residual else [])
    out_shape = [jax.ShapeDtypeStruct((t, n), out_dtype)] + \
        ([jax.ShapeDtypeStruct((t, k), BF16)] if norm else []) + \
        ([jax.ShapeDtypeStruct((t, n), F32)] if residual else [])
    outs = pl.pallas_call(
        body, grid=(t // tm, n // tn), in_specs=in_specs, out_specs=out_specs, out_shape=out_shape, name=name,
        compiler_params=_cparams("parallel", "parallel"))(a, w, *extra_in)
    return outs[0] if len(outs) == 1 else outs


def _wgrad(a, b, *, name, tk=None, tn=None, tt=2048, out_dtype=BF16):
    t, k = a.shape
    n = b.shape[1]
    tk = tk or k
    tn = tn or n
    steps = t // tt

    def body(a_ref, b_ref, o_ref, acc_ref):
        s = pl.program_id(2)

        @pl.when(s == 0)
        def _():
            acc_ref[...] = jnp.zeros_like(acc_ref)

        acc_ref[...] += lax.dot_general(a_ref[...].astype(BF16), b_ref[...].astype(BF16),
                                        (((0,), (0,)), ((), ())), preferred_element_type=F32)

        @pl.when(s == steps - 1)
        def _():
            o_ref[...] = acc_ref[...].astype(out_dtype)

    return pl.pallas_call(
        body, grid=(k // tk, n // tn, steps),
        in_specs=[pl.BlockSpec((tt, tk), lambda i, j, s: (s, i)), pl.BlockSpec((tt, tn), lambda i, j, s: (s, j))],
        out_specs=pl.BlockSpec((tk, tn), lambda i, j, s: (i, j)),
        out_shape=jax.ShapeDtypeStruct((k, n), out_dtype),
        scratch_shapes=[pltpu.VMEM((tk, tn), F32)], name=name,
        compiler_params=_cparams("parallel", "parallel", "arbitrary"))(a, b)


def _norm_mod_bwd(x, dh, dres, g, sc, *, name, w=None, tm=ROW_TILE):
    s, d = x.shape
    steps = s // tm

    def body(x_ref, dh_ref, dres_ref, g_ref, sc_ref, *rest):
        w_ref = rest[0] if w is not None else None
        dx_ref, dg_ref, dsc_ref, dsh_ref, da_acc, dsh_acc = rest[-6:]
        i = pl.program_id(0)

        @pl.when(i == 0)
        def _():
            da_acc[...] = jnp.zeros_like(da_acc)
            dsh_acc[...] = jnp.zeros_like(dsh_acc)

        xv = x_ref[...]
        if w is None:
            dhv = dh_ref[...].astype(F32)
        else:
            dhv = jnp.dot(dh_ref[...], w_ref[...], preferred_element_type=F32)
        r = lax.rsqrt(jnp.mean(xv * xv, axis=-1, keepdims=True) + NORM_EPS)
        xhat = xv * r
        gain = g_ref[...] * (1.0 + sc_ref[...])
        dxhat = dhv * gain
        dx_ref[...] = dres_ref[...] + r * (dxhat - xhat * jnp.mean(dxhat * xhat, axis=-1, keepdims=True))
        da_acc[...] += jnp.sum(dhv * xhat, axis=0, keepdims=True)
        dsh_acc[...] += jnp.sum(dhv, axis=0, keepdims=True)

        @pl.when(i == steps - 1)
        def _():
            dg_ref[...] = da_acc[...] * (1.0 + sc_ref[...])
            dsc_ref[...] = da_acc[...] * g_ref[...]
            dsh_ref[...] = dsh_acc[...]

    row = pl.BlockSpec((1, d), lambda i: (0, 0))
    tile = pl.BlockSpec((tm, d), lambda i: (i, 0))
    row_shape = jax.ShapeDtypeStruct((1, d), F32)
    dh_spec = tile if w is None else pl.BlockSpec((tm, dh.shape[1]), lambda i: (i, 0))
    return pl.pallas_call(
        body, grid=(steps,), in_specs=[tile, dh_spec, tile, row, row] + ([] if w is None else [_resident(w.shape)]),
        out_specs=[tile, row, row, row],
        out_shape=[jax.ShapeDtypeStruct((s, d), F32), row_shape, row_shape, row_shape],
        scratch_shapes=[pltpu.VMEM((1, d), F32), pltpu.VMEM((1, d), F32)], name=name,
        compiler_params=_cparams("arbitrary"))(x, dh, dres, g, sc, *([] if w is None else [w]))


def _mid_bwd(x1, dh2, dx2, norm_g, sc, gate, mo, w_out, *, name, tm=ROW_TILE):
    s, d = x1.shape
    steps = s // tm

    def body(x_ref, dh_ref, dres_ref, ng_ref, sc_ref, g_ref, mo_ref, w_ref,
             dx_ref, dng_ref, dsc_ref, dsh_ref, dmix_ref, dmo_ref, dg_ref, da_acc, dsh_acc):
        i = pl.program_id(0)

        @pl.when(i == 0)
        def _():
            da_acc[...] = jnp.zeros_like(da_acc)
            dsh_acc[...] = jnp.zeros_like(dsh_acc)
            dg_ref[...] = jnp.zeros_like(dg_ref)

        xv = x_ref[...]
        dhv = dh_ref[...]
        r = lax.rsqrt(jnp.mean(xv * xv, axis=-1, keepdims=True) + NORM_EPS)
        xhat = xv * r
        dxhat = dhv * (ng_ref[...] * (1.0 + sc_ref[...]))
        dxv = dres_ref[...] + r * (dxhat - xhat * jnp.mean(dxhat * xhat, axis=-1, keepdims=True))
        dx_ref[...] = dxv
        da_acc[...] += jnp.sum(dhv * xhat, axis=0, keepdims=True)
        dsh_acc[...] += jnp.sum(dhv, axis=0, keepdims=True)
        dmo = (g_ref[...] * dxv).astype(BF16)
        dmo_ref[...] = dmo
        dg_ref[...] += jnp.sum(dxv * mo_ref[...], axis=0, keepdims=True)
        dmix_ref[...] = lax.dot_general(dmo, w_ref[...], (((1,), (1,)), ((), ())), preferred_element_type=F32)

        @pl.when(i == steps - 1)
        def _():
            dng_ref[...] = da_acc[...] * (1.0 + sc_ref[...])
            dsc_ref[...] = da_acc[...] * ng_ref[...]
            dsh_ref[...] = dsh_acc[...]

    tile = pl.BlockSpec((tm, d), lambda i: (i, 0))
    row = pl.BlockSpec((1, d), lambda i: (0, 0))
    mix_tile = pl.BlockSpec((tm, w_out.shape[0]), lambda i: (i, 0))
    row_shape = jax.ShapeDtypeStruct((1, d), F32)
    return pl.pallas_call(
        body, grid=(steps,), in_specs=[tile, tile, tile, row, row, row, tile, _resident(w_out.shape)],
        out_specs=[tile, row, row, row, mix_tile, tile, row],
        out_shape=[jax.ShapeDtypeStruct((s, d), F32), row_shape, row_shape, row_shape,
                   jax.ShapeDtypeStruct((s, w_out.shape[0]), F32), jax.ShapeDtypeStruct((s, d), BF16), row_shape],
        scratch_shapes=[pltpu.VMEM((1, d), F32), pltpu.VMEM((1, d), F32)], name=name,
        compiler_params=_cparams("arbitrary"))(x1, dh2, dx2, norm_g, sc, gate, mo, w_out)


def _final_loss(x, g, target, *, name, tm=ROW_TILE):
    s, d = x.shape
    steps = s // tm

    def body(x_ref, g_ref, t_ref, loss_ref, dx_ref, dg_ref, sq_acc):
        i = pl.program_id(0)

        @pl.when(i == 0)
        def _():
            sq_acc[...] = jnp.zeros_like(sq_acc)
            dg_ref[...] = jnp.zeros_like(dg_ref)

        xv = x_ref[...]
        r = lax.rsqrt(jnp.mean(xv * xv, axis=-1, keepdims=True) + NORM_EPS)
        xhat = xv * r
        err = xhat * g_ref[...] - t_ref[...]
        sq_acc[...] += jnp.sum(err * err, axis=0, keepdims=True)
        dy = err * (1.0 / d)
        dg_ref[...] += jnp.sum(dy * xhat, axis=0, keepdims=True)
        dxhat = dy * g_ref[...]
        dx_ref[...] = r * (dxhat - xhat * jnp.mean(dxhat * xhat, axis=-1, keepdims=True))

        @pl.when(i == steps - 1)
        def _():
            total = jnp.sum(sq_acc[...], axis=1, keepdims=True) * (0.5 / d)
            loss_ref[...] = jnp.broadcast_to(total, loss_ref.shape)

    tile = pl.BlockSpec((tm, d), lambda i: (i, 0))
    row = pl.BlockSpec((1, d), lambda i: (0, 0))
    return pl.pallas_call(
        body, grid=(steps,), in_specs=[tile, row, tile],
        out_specs=[pl.BlockSpec((1, 128), lambda i: (0, 0)), tile, row],
        out_shape=[jax.ShapeDtypeStruct((1, 128), F32), jax.ShapeDtypeStruct((s, d), F32),
                   jax.ShapeDtypeStruct((1, d), F32)],
        scratch_shapes=[pltpu.VMEM((1, d), F32)], name=name, compiler_params=_cparams("arbitrary"))(x, g, target)


def _ssd_chunk(z, xbc_ext, dt_raw, conv_w, conv_b, dt_bias, a_log, d_skip, norm_g, h_in):
    q = z.shape[0]
    gw = SSD_HPG * HEAD_LANES
    xc = conv_b
    for k in range(SSD_CONV_K):
        xc = xc + _pick_row(conv_w, k) * _shift_rows(xbc_ext, SSD_CONV_K - 1 - k, CONV_HALO)
    xc = _silu(xc)
    dt = jax.nn.softplus(dt_raw + dt_bias)
    da = dt * (-jnp.exp(a_log))
    ri = lax.broadcasted_iota(jnp.int32, (q, q), 0)
    ci = lax.broadcasted_iota(jnp.int32, (q, q), 1)
    causal = ri >= ci
    tril = causal.astype(F32)
    a_cum = jnp.dot(tril, da, preferred_element_type=F32, precision=lax.Precision.HIGHEST)
    a_cum_t = lax.dot_general(da, tril, (((0,), (1,)), ((), ())), preferred_element_type=F32,
                              precision=lax.Precision.HIGHEST)
    a_last = _pick_row(a_cum, q - 1)
    head = _head_of_lane(gw)
    ys, hs = [], []
    for g in range(2):
        xs = xc[:, gw * g:gw * (g + 1)]
        bm = xc[:, SSD_INNER + SSD_STATE * g:SSD_INNER + SSD_STATE * (g + 1)]
        cm = xc[:, SSD_INNER + 2 * SSD_STATE + SSD_STATE * g:SSD_INNER + 2 * SSD_STATE + SSD_STATE * (g + 1)]
        cb = lax.dot_general(cm.astype(BF16), bm.astype(BF16), (((1,), (1,)), ((), ())), preferred_element_type=F32)
        cols = [_pick_lane(a_cum, SSD_HPG * g + j) for j in range(SSD_HPG)]
        lasts = [_pick_lane(a_last, SSD_HPG * g + j) for j in range(SSD_HPG)]
        dt_exp = sum(jnp.where(head == j, _pick_lane(dt, SSD_HPG * g + j), 0.0) for j in range(SSD_HPG))
        d_exp = sum(jnp.where(head == j, _pick_lane(d_skip, SSD_HPG * g + j), 0.0) for j in range(SSD_HPG))
        e_cum = sum(jnp.where(head == j, jnp.exp(cols[j]), 0.0) for j in range(SSD_HPG))
        c_dec = sum(jnp.where(head == j, jnp.exp(lasts[j]), 0.0) for j in range(SSD_HPG))
        xsdt = (xs * dt_exp).astype(BF16)
        y_diag = jnp.zeros((q, gw), F32)
        st_new = jnp.zeros((SSD_STATE, gw), F32)
        for j in range(SSD_HPG):
            row = _pick_row(a_cum_t, SSD_HPG * g + j)
            lmat = jnp.exp(jnp.where(causal, cols[j] - row, -jnp.inf))
            r = jnp.dot((cb * lmat).astype(BF16), xsdt, preferred_element_type=F32)
            y_diag = y_diag + jnp.where(head == j, r, 0.0)
            bd = (bm * jnp.exp(lasts[j] - cols[j])).astype(BF16)
            st = lax.dot_general(bd, xsdt, (((0,), (0,)), ((), ())), preferred_element_type=F32)
            st_new = st_new + jnp.where(head == j, st, 0.0)
        y_off = jnp.dot(cm.astype(BF16), h_in[g].astype(BF16), preferred_element_type=F32) * e_cum
        hs.append(h_in[g] * c_dec + st_new)
        y = y_diag + y_off + d_exp * xs
        yz = y * _silu(z[:, gw * g:gw * (g + 1)])
        yz = yz * lax.rsqrt(jnp.mean(yz * yz, axis=-1, keepdims=True) + NORM_EPS)
        ys.append(yz * norm_g[:, gw * g:gw * (g + 1)])
    return jnp.concatenate(ys, axis=1), tuple(hs)


_SSD_NCHUNK = SEQ // SSD_CHUNK
_HALO_PER_CHUNK = SSD_CHUNK // CONV_HALO


def _ssd_param_specs(const):
    return [pl.BlockSpec((8, SSD_CONV_CH), const), pl.BlockSpec((1, SSD_CONV_CH), const),
            pl.BlockSpec((1, 128), const), pl.BlockSpec((1, 128), const), pl.BlockSpec((1, 128), const),
            pl.BlockSpec((1, SSD_INNER), const)]


def _ssd_fwd(proj, conv_w, conv_b, dt_bias, a_log, d_skip, norm_g, *, name):
    q = SSD_CHUNK

    def body(z_ref, xbc_ref, halo_ref, dt_ref, cw_ref, cb_ref, db_ref, al_ref, d_ref, ng_ref, y_ref, hs_ref, h_acc):
        i = pl.program_id(0)

        @pl.when(i == 0)
        def _():
            h_acc[...] = jnp.zeros_like(h_acc)

        halo = jnp.where(i == 0, 0.0, halo_ref[...])
        xbc_ext = jnp.concatenate([halo, xbc_ref[...]], axis=0)
        h_in = (h_acc[0], h_acc[1])
        hs_ref[0, 0] = h_in[0]
        hs_ref[0, 1] = h_in[1]
        y, h_out = _ssd_chunk(z_ref[...], xbc_ext, dt_ref[...], cw_ref[...], cb_ref[...], db_ref[...], al_ref[...],
                              d_ref[...], ng_ref[...], h_in)
        y_ref[...] = y.astype(BF16)
        h_acc[0] = h_out[0]
        h_acc[1] = h_out[1]

    const = lambda i: (0, 0)
    return pl.pallas_call(
        body, grid=(_SSD_NCHUNK,),
        in_specs=[pl.BlockSpec((q, SSD_INNER), lambda i: (i, PROJ_Z_BLK)),
                  pl.BlockSpec((q, SSD_CONV_CH), lambda i: (i, 0)),
                  pl.BlockSpec((CONV_HALO, SSD_CONV_CH), lambda i: (jnp.maximum(i * _HALO_PER_CHUNK - 1, 0), 0)),
                  pl.BlockSpec((q, 128), lambda i: (i, PROJ_DT_BLK))] + _ssd_param_specs(const),
        out_specs=[pl.BlockSpec((q, SSD_INNER), lambda i: (i, 0)),
                   pl.BlockSpec((1, 2, SSD_STATE, 256), lambda i: (i, 0, 0, 0))],
        out_shape=[jax.ShapeDtypeStruct((SEQ, D_MODEL), BF16),
                   jax.ShapeDtypeStruct((_SSD_NCHUNK, 2, SSD_STATE, 256), F32)],
        scratch_shapes=[pltpu.VMEM((2, SSD_STATE, 256), F32)], name=name,
        compiler_params=_cparams("arbitrary"))(proj, proj, proj, proj, conv_w, conv_b, dt_bias, a_log, d_skip, norm_g)


def _ssd_bwd(proj, hstates, dmix, conv_w, conv_b, dt_bias, a_log, d_skip, norm_g, *, name):
    q = SSD_CHUNK
    last = _SSD_NCHUNK - 1

    def body(z_ref, xbc_ref, halo_ref, dt_ref, hs_ref, dy_ref, cw_ref, cb_ref, db_ref, al_ref, d_ref, ng_ref,
             dp_ref, dcw_ref, dcb_ref, ddb_ref, dal_ref, dd_ref, dng_ref, dh_acc, dhalo_acc):
        i = pl.program_id(0)

        @pl.when(i == 0)
        def _():
            dh_acc[...] = jnp.zeros_like(dh_acc)
            dhalo_acc[...] = jnp.zeros_like(dhalo_acc)
            for r in (dcw_ref, dcb_ref, ddb_ref, dal_ref, dd_ref, dng_ref):
                r[...] = jnp.zeros_like(r)

        halo = jnp.where(i == last, 0.0, halo_ref[...])
        xbc_ext = jnp.concatenate([halo, xbc_ref[...]], axis=0)
        _, vjp = jax.vjp(_ssd_chunk, z_ref[...], xbc_ext, dt_ref[...], cw_ref[...], cb_ref[...], db_ref[...],
                         al_ref[...], d_ref[...], ng_ref[...], (hs_ref[0, 0], hs_ref[0, 1]))
        gz, gx, gdt, gcw, gcb, gdb, gal, gd, gng, gh = vjp((dy_ref[...], (dh_acc[0], dh_acc[1])))
        dxbc = jnp.concatenate([gx[CONV_HALO:q], gx[q:] + dhalo_acc[...]], axis=0)
        dp_ref[...] = jnp.concatenate([dxbc, gz, gdt, jnp.zeros_like(gdt)], axis=1).astype(BF16)
        dhalo_acc[...] = gx[:CONV_HALO]
        dh_acc[0] = gh[0]
        dh_acc[1] = gh[1]
        dcw_ref[...] += gcw
        dcb_ref[...] += gcb
        ddb_ref[...] += gdb
        dal_ref[...] += gal
        dd_ref[...] += gd
        dng_ref[...] += gng

    const = lambda i: (0, 0)
    rev = lambda i: last - i
    row = lambda n: jax.ShapeDtypeStruct((1, n), F32)
    return pl.pallas_call(
        body, grid=(_SSD_NCHUNK,),
        in_specs=[pl.BlockSpec((q, SSD_INNER), lambda i: (rev(i), PROJ_Z_BLK)),
                  pl.BlockSpec((q, SSD_CONV_CH), lambda i: (rev(i), 0)),
                  pl.BlockSpec((CONV_HALO, SSD_CONV_CH), lambda i: (jnp.maximum(rev(i) * _HALO_PER_CHUNK - 1, 0), 0)),
                  pl.BlockSpec((q, 128), lambda i: (rev(i), PROJ_DT_BLK)),
                  pl.BlockSpec((1, 2, SSD_STATE, 256), lambda i: (rev(i), 0, 0, 0)),
                  pl.BlockSpec((q, SSD_INNER), lambda i: (rev(i), 0))] + _ssd_param_specs(const),
        out_specs=[pl.BlockSpec((q, PROJ_SSD_W), lambda i: (rev(i), 0))] + _ssd_param_specs(const),
        out_shape=[jax.ShapeDtypeStruct((SEQ, PROJ_W), BF16), jax.ShapeDtypeStruct((8, SSD_CONV_CH), F32),
                   row(SSD_CONV_CH), row(128), row(128), row(128), row(SSD_INNER)],
        scratch_shapes=[pltpu.VMEM((2, SSD_STATE, 256), F32), pltpu.VMEM((CONV_HALO, SSD_CONV_CH), F32)], name=name,
        compiler_params=_cparams("arbitrary"))(proj, proj, proj, proj, hstates, dmix, conv_w, conv_b, dt_bias, a_log,
                                                d_skip, norm_g)


def _rope_tables(pos_col, inv_freq_lane, *, name):
    s = pos_col.shape[0]

    def body(p_ref, f_ref, c_ref, s1_ref, s2_ref):
        ang = p_ref[...] * f_ref[...]
        within = lax.broadcasted_iota(jnp.int32, ang.shape, 1) % HEAD_LANES
        half = ROT_DIM // 2
        c_ref[...] = jnp.where(within < ROT_DIM, jnp.cos(ang), 1.0)
        sn = jnp.sin(ang)
        s1_ref[...] = jnp.where(within < half, -sn, 0.0)
        s2_ref[...] = jnp.where((within >= half) & (within < ROT_DIM), sn, 0.0)

    shp = jax.ShapeDtypeStruct((s, 128), F32)
    return pl.pallas_call(body, out_shape=[shp, shp, shp], name=name,
                          compiler_params=pltpu.CompilerParams(vmem_limit_bytes=VMEM_LIMIT))(pos_col, inv_freq_lane)


def _rope(t, c, s1, s2):
    half = ROT_DIM // 2
    return t * c + pltpu.roll(t, 128 - half, 1) * s1 + pltpu.roll(t, half, 1) * s2


def _rope_t(g, c, s1, s2):
    half = ROT_DIM // 2
    return g * c + pltpu.roll(g * s1, half, 1) + pltpu.roll(g * s2, 128 - half, 1)


def _att_valid(b):
    qi = lax.broadcasted_iota(jnp.int32, (ATT_BLOCK, 2 * ATT_BLOCK), 0)
    kj = lax.broadcasted_iota(jnp.int32, (ATT_BLOCK, 2 * ATT_BLOCK), 1)
    rel = qi + ATT_BLOCK - kj
    return (rel >= 0) & (rel <= ATT_BLOCK) & (b * ATT_BLOCK + kj - ATT_BLOCK >= 0)


def _att_slices(i, d):
    if d == 1:
        qstart = pl.multiple_of(i * ATT_BLOCK, ATT_BLOCK)
        return i, pl.ds(qstart, ATT_BLOCK), pl.ds(pl.multiple_of(qstart - ATT_BLOCK + ATT_KPAD, ATT_BLOCK), 2 * ATT_BLOCK)
    r = i % d
    b = i // d
    qstart = r + d * ATT_BLOCK * b
    return b, pl.ds(qstart, ATT_BLOCK, stride=d), pl.ds(qstart - ATT_BLOCK * d + ATT_KPAD, 2 * ATT_BLOCK, stride=d)


_ATT_NBLK = SEQ // ATT_BLOCK
_ATT_SCALE = HEAD_LANES ** -0.5
_ATT_UNROLL_FWD = 8
_ATT_UNROLL = 4


def _att_fwd(proj, cos, sin1, sin2, mix, *, name):
    s = SEQ

    def body(q_ref, k_ref, v_ref, c_ref, s1_ref, s2_ref, _, o_ref, lse_ref, mix_ref, qs, ks, vs, acc, m_s, l_s):
        c, s1, s2 = c_ref[...], s1_ref[...], s2_ref[...]
        qs[...] = _rope(q_ref[...], c, s1, s2) * _ATT_SCALE
        zeros = jnp.zeros((ATT_KPAD, 128), F32)
        ks[pl.ds(0, ATT_KPAD), :] = zeros
        vs[pl.ds(0, ATT_KPAD), :] = zeros
        ks[pl.ds(ATT_KPAD, s), :] = _rope(k_ref[...], c, s1, s2)
        vs[pl.ds(ATT_KPAD, s), :] = v_ref[...]
        head0 = _head_of_lane(128) == 0

        for bi, (_, d) in enumerate(ATT_PATTERNS):
            def blk(i, carry, d=d, first=(bi == 0)):
                b, sq, sk = _att_slices(i, d)
                qb = qs[sq, :]
                kw = ks[sk, :].astype(BF16)
                vw = vs[sk, :].astype(BF16)
                valid = _att_valid(b)
                ms, ls, os_ = [], [], []
                for hh in range(2):
                    qh = jnp.where(head0 if hh == 0 else ~head0, qb, 0.0).astype(BF16)
                    sc = lax.dot_general(qh, kw, (((1,), (1,)), ((), ())), preferred_element_type=F32)
                    sc = jnp.where(valid, sc, -jnp.inf)
                    mb = jnp.max(sc, axis=1, keepdims=True)
                    p = jnp.exp(sc - mb)
                    ms.append(mb)
                    ls.append(jnp.sum(p, axis=1, keepdims=True))
                    os_.append(jnp.dot(p.astype(BF16), vw, preferred_element_type=F32))
                m_b = jnp.where(head0, ms[0], ms[1])
                l_b = jnp.where(head0, ls[0], ls[1])
                o_b = jnp.where(head0, os_[0], os_[1])
                if first:
                    m_s[sq, :] = m_b
                    l_s[sq, :] = l_b
                    acc[sq, :] = o_b
                else:
                    m_old = m_s[sq, :]
                    m_new = jnp.maximum(m_old, m_b)
                    a_old = jnp.exp(m_old - m_new)
                    a_b = jnp.exp(m_b - m_new)
                    m_s[sq, :] = m_new
                    l_s[sq, :] = l_s[sq, :] * a_old + l_b * a_b
                    acc[sq, :] = acc[sq, :] * a_old + o_b * a_b
                return carry

            lax.fori_loop(0, _ATT_NBLK, blk, 0, unroll=_ATT_UNROLL_FWD)

        out = acc[...] / l_s[...]
        o_ref[...] = out
        mix_ref[...] = out.astype(BF16)
        lse_ref[...] = m_s[...] + jnp.log(l_s[...])

    col = lambda base: pl.BlockSpec((s, 128), lambda p: (0, base + p))
    tab = pl.BlockSpec((s, 128), lambda p: (0, 0))
    big = pltpu.VMEM((ATT_KPAD + s, 128), F32)
    tok = pltpu.VMEM((s, 128), F32)
    return pl.pallas_call(
        body, grid=(2,), in_specs=[col(PROJ_Q_BLK), col(PROJ_K_BLK), col(PROJ_V_BLK), tab, tab, tab, _HBM],
        out_specs=[pl.BlockSpec((s, 128), lambda p: (0, p)), pl.BlockSpec((s, 128), lambda p: (0, p)),
                   col(MIX_ATT_BLK)],
        out_shape=[jax.ShapeDtypeStruct((s, ATT_W), F32), jax.ShapeDtypeStruct((s, ATT_W), F32),
                   jax.ShapeDtypeStruct(mix.shape, mix.dtype)],
        input_output_aliases={6: 2}, scratch_shapes=[tok, big, big, tok, tok, tok], name=name,
        compiler_params=_cparams("arbitrary"))(proj, proj, proj, cos, sin1, sin2, mix)


def _att_bwd(proj, cos, sin1, sin2, out, lse, dmix, dproj, *, name):
    s = SEQ

    def body(proj_ref, c_hbm, s1_hbm, s2_hbm, out_hbm, lse_hbm, dmix_hbm, _, dproj_hbm,
             c_ref, s1_ref, s2_ref, o_ref, lse_ref, do_ref, qs, ks, vs, dqs, dks, dvs, staged, sems):
        def start(copies):
            for cp in copies:
                cp.start()
            return copies

        def load(pair):
            lanes = pl.ds(128 * pair, 128)
            rows = pl.ds(ATT_KPAD, s)
            return start([
                pltpu.make_async_copy(proj_ref.at[:, pl.ds(128 * (PROJ_Q_BLK + pair), 128)], qs, sems.at[0]),
                pltpu.make_async_copy(proj_ref.at[:, pl.ds(128 * (PROJ_K_BLK + pair), 128)], ks.at[rows, :], sems.at[1]),
                pltpu.make_async_copy(proj_ref.at[:, pl.ds(128 * (PROJ_V_BLK + pair), 128)], vs.at[rows, :], sems.at[2]),
                pltpu.make_async_copy(out_hbm.at[:, lanes], o_ref, sems.at[3]),
                pltpu.make_async_copy(lse_hbm.at[:, lanes], lse_ref, sems.at[4]),
                pltpu.make_async_copy(dmix_hbm.at[:, pl.ds(128 * (MIX_ATT_BLK + pair), 128)], do_ref, sems.at[5])])

        tables = start([pltpu.make_async_copy(c_hbm, c_ref, sems.at[6]),
                        pltpu.make_async_copy(s1_hbm, s1_ref, sems.at[7]),
                        pltpu.make_async_copy(s2_hbm, s2_ref, sems.at[8])])
        loads = load(0)
        for cp in tables:
            cp.wait()
        head0 = _head_of_lane(128) == 0
        zeros = jnp.zeros((ATT_KPAD, 128), F32)
        for pair in range(2):
            for cp in loads:
                cp.wait()
            c, s1, s2 = c_ref[...], s1_ref[...], s2_ref[...]
            qs[...] = _rope(qs[...], c, s1, s2) * _ATT_SCALE
            ks[pl.ds(0, ATT_KPAD), :] = zeros
            vs[pl.ds(0, ATT_KPAD), :] = zeros
            ks[pl.ds(ATT_KPAD, s), :] = _rope(ks[pl.ds(ATT_KPAD, s), :], c, s1, s2)
            dqs[...] = jnp.zeros_like(dqs)
            dks[...] = jnp.zeros_like(dks)
            dvs[...] = jnp.zeros_like(dvs)

            for _, d in ATT_PATTERNS:
                def blk(i, carry, d=d):
                    b, sq, sk = _att_slices(i, d)
                    qb = qs[sq, :]
                    kw = ks[sk, :].astype(BF16)
                    vw = vs[sk, :].astype(BF16)
                    dob = do_ref[sq, :]
                    lse_b = lse_ref[sq, :]
                    dd = dob * o_ref[sq, :]
                    valid = _att_valid(b)
                    dq_b = jnp.zeros((ATT_BLOCK, 128), F32)
                    dk_w = jnp.zeros((2 * ATT_BLOCK, 128), F32)
                    dv_w = jnp.zeros((2 * ATT_BLOCK, 128), F32)
                    for hh in range(2):
                        hm = head0 if hh == 0 else ~head0
                        qh = jnp.where(hm, qb, 0.0).astype(BF16)
                        doh = jnp.where(hm, dob, 0.0).astype(BF16)
                        lse_h = _pick_lane(lse_b, hh * HEAD_LANES)
                        d_h = jnp.sum(jnp.where(hm, dd, 0.0), axis=1, keepdims=True)
                        sc = lax.dot_general(qh, kw, (((1,), (1,)), ((), ())), preferred_element_type=F32)
                        p = jnp.where(valid, jnp.exp(sc - lse_h), 0.0)
                        dp = lax.dot_general(doh, vw, (((1,), (1,)), ((), ())), preferred_element_type=F32)
                        ds = (p * (dp - d_h)).astype(BF16)
                        dq_b = dq_b + jnp.where(hm, jnp.dot(ds, kw, preferred_element_type=F32), 0.0)
                        dk_w = dk_w + lax.dot_general(ds, qh, (((0,), (0,)), ((), ())), preferred_element_type=F32)
                        dv_w = dv_w + lax.dot_general(p.astype(BF16), doh, (((0,), (0,)), ((), ())),
                                                      preferred_element_type=F32)
                    dqs[sq, :] += dq_b
                    dks[sk, :] += dk_w
                    dvs[sk, :] += dv_w
                    return carry

                lax.fori_loop(0, _ATT_NBLK, blk, 0, unroll=_ATT_UNROLL)

            staged[0] = _rope_t(dqs[...] * _ATT_SCALE, c, s1, s2).astype(BF16)
            staged[1] = _rope_t(dks[pl.ds(ATT_KPAD, s), :], c, s1, s2).astype(BF16)
            staged[2] = dvs[pl.ds(ATT_KPAD, s), :].astype(BF16)
            stores = start([
                pltpu.make_async_copy(staged.at[j], dproj_hbm.at[:, pl.ds(128 * (col + pair), 128)], sems.at[9 + j])
                for j, col in enumerate((PROJ_Q_BLK, PROJ_K_BLK, PROJ_V_BLK))])
            if pair == 0:
                loads = load(1)
            for cp in stores:
                cp.wait()

    big = pltpu.VMEM((ATT_KPAD + s, 128), F32)
    tok = pltpu.VMEM((s, 128), F32)
    return pl.pallas_call(
        body, in_specs=[_HBM] * 8, out_specs=_HBM, out_shape=jax.ShapeDtypeStruct(dproj.shape, dproj.dtype),
        input_output_aliases={7: 0},
        scratch_shapes=[tok] * 6 + [tok, big, big, tok, big, big, pltpu.VMEM((3, s, 128), BF16),
                                    pltpu.SemaphoreType.DMA((12,))], name=name,
        compiler_params=pltpu.CompilerParams(vmem_limit_bytes=VMEM_LIMIT))(
            proj, cos, sin1, sin2, out, lse, dmix, dproj)


_POOL_TM = 512
_POOL_NT = SEQ // _POOL_TM
_POOL_HALO_PER_TILE = _POOL_TM // POOL_HALO


def _pool_tile(u_ext, w_bd, scale, t0):
    s2 = u_ext + _roll_rows(u_ext, 1)
    s4 = s2 + _roll_rows(s2, 2)
    s8 = s4 + _roll_rows(s4, 4)
    s16 = s8 + _roll_rows(s8, 8)
    grp = _head_of_lane(POOL_W)
    sel = jnp.where(grp == 0, s2, jnp.where(grp == 1, s4, jnp.where(grp == 2, s8, s16)))[POOL_HALO:]
    t = sel.shape[0]
    pos = t0 + lax.broadcasted_iota(jnp.int32, (t, POOL_W), 0) + 1
    win = jnp.where(grp == 0, 2, jnp.where(grp == 1, 4, jnp.where(grp == 2, 8, 16)))
    cnt = jnp.minimum(pos, win).astype(F32)
    diff = sel / cnt - u_ext[POOL_HALO:]
    return jnp.dot(diff.astype(BF16), w_bd.astype(BF16), preferred_element_type=F32) * scale


def _pool_fwd(proj, w_bd, scale, mix, *, name):
    tm = _POOL_TM

    def body(u_ref, halo_ref, w_ref, sc_ref, _, y_ref):
        i = pl.program_id(0)
        halo = jnp.where(i == 0, 0.0, halo_ref[...])
        u_ext = jnp.concatenate([halo, u_ref[...]], axis=0)
        y_ref[...] = _pool_tile(u_ext, w_ref[...], sc_ref[...], i * tm).astype(BF16)

    return pl.pallas_call(
        body, grid=(_POOL_NT,),
        in_specs=[pl.BlockSpec((tm, POOL_W), lambda i: (i, PROJ_POOL_BLK)),
                  pl.BlockSpec((POOL_HALO, POOL_W),
                               lambda i: (jnp.maximum(i * _POOL_HALO_PER_TILE - 1, 0), PROJ_POOL_BLK)),
                  pl.BlockSpec((POOL_W, POOL_W), lambda i: (0, 0)), pl.BlockSpec((1, POOL_W), lambda i: (0, 0)), _HBM],
        out_specs=pl.BlockSpec((tm, POOL_W), lambda i: (i, MIX_POOL_BLK)),
        out_shape=jax.ShapeDtypeStruct(mix.shape, mix.dtype), input_output_aliases={4: 0}, name=name,
        compiler_params=_cparams("parallel"))(proj, proj, w_bd, scale, mix)


def _pool_bwd(proj, dmix, w_bd, scale, dproj, *, name):
    tm = _POOL_TM
    last = _POOL_NT - 1

    def body(u_ref, halo_ref, dy_ref, w_ref, sc_ref, _, du_ref, dw_ref, dsc_ref, dhalo_acc):
        i = pl.program_id(0)

        @pl.when(i == 0)
        def _():
            dhalo_acc[...] = jnp.zeros_like(dhalo_acc)
            dw_ref[...] = jnp.zeros_like(dw_ref)
            dsc_ref[...] = jnp.zeros_like(dsc_ref)

        tile = last - i
        halo = jnp.where(tile == 0, 0.0, halo_ref[...])
        u_ext = jnp.concatenate([halo, u_ref[...]], axis=0)
        _, vjp = jax.vjp(functools.partial(_pool_tile, t0=tile * tm), u_ext, w_ref[...], sc_ref[...])
        gu, gw, gs = vjp(dy_ref[...])
        du_ref[...] = jnp.concatenate([gu[POOL_HALO:tm], gu[tm:] + dhalo_acc[...]], axis=0).astype(BF16)
        dhalo_acc[...] = gu[:POOL_HALO]
        dw_ref[...] += gw
        dsc_ref[...] += gs

    rev = lambda i: last - i
    return pl.pallas_call(
        body, grid=(_POOL_NT,),
        in_specs=[pl.BlockSpec((tm, POOL_W), lambda i: (rev(i), PROJ_POOL_BLK)),
                  pl.BlockSpec((POOL_HALO, POOL_W),
                               lambda i: (jnp.maximum(rev(i) * _POOL_HALO_PER_TILE - 1, 0), PROJ_POOL_BLK)),
                  pl.BlockSpec((tm, POOL_W), lambda i: (rev(i), MIX_POOL_BLK)),
                  pl.BlockSpec((POOL_W, POOL_W), lambda i: (0, 0)), pl.BlockSpec((1, POOL_W), lambda i: (0, 0)), _HBM],
        out_specs=[pl.BlockSpec((tm, POOL_W), lambda i: (rev(i), PROJ_POOL_BLK)),
                   pl.BlockSpec((POOL_W, POOL_W), lambda i: (0, 0)), pl.BlockSpec((1, POOL_W), lambda i: (0, 0))],
        out_shape=[jax.ShapeDtypeStruct(dproj.shape, dproj.dtype), jax.ShapeDtypeStruct((POOL_W, POOL_W), F32),
                   jax.ShapeDtypeStruct((1, POOL_W), F32)],
        input_output_aliases={5: 0}, scratch_shapes=[pltpu.VMEM((POOL_HALO, POOL_W), F32)], name=name,
        compiler_params=_cparams("arbitrary"))(proj, proj, dmix, w_bd, scale, dproj)


_FFN_TM = 256
_FFN_NT = SEQ // _FFN_TM
_FFN_HALO_PER_TILE = _FFN_TM // CONV_HALO


def _ffn_act_tile(hid_ext, conv_w, conv_b):
    hc = conv_b
    for k in range(FFN_CONV_K):
        hc = hc + _pick_row(conv_w, k) * _shift_rows(hid_ext, FFN_CONV_K - 1 - k, CONV_HALO)
    return _silu(hc[:, :FFN_DIM]) * hc[:, FFN_DIM:]


def _ffn_fwd(x1, norm_g, sc, sh, gate, up_t, down, conv_w, conv_b, *, name):
    tm = _FFN_TM
    w = 2 * FFN_DIM
    d = D_MODEL

    def body(x_ref, ng_ref, sc_ref, sh_ref, g_ref, up_ref, dn_ref, cw_ref, cb_ref,
             h_ref, hid_ref, act_ref, f_ref, x2_ref, halo_acc):
        i = pl.program_id(0)
        h2 = _rms_modulate(x_ref[...], ng_ref[...], sc_ref[...], sh_ref[...]).astype(BF16)
        h_ref[...] = h2
        hid = lax.dot_general(h2, up_ref[...], (((1,), (1,)), ((), ())), preferred_element_type=F32)
        hid_ref[...] = hid
        halo = jnp.where(i == 0, 0.0, halo_acc[...])
        act = _ffn_act_tile(jnp.concatenate([halo, hid], axis=0), cw_ref[...], cb_ref[...]).astype(BF16)
        halo_acc[...] = hid[tm - CONV_HALO:]
        act_ref[...] = act
        f = jnp.dot(act, dn_ref[...], preferred_element_type=F32)
        f_ref[...] = f
        x2_ref[...] = x_ref[...] + g_ref[...] * f

    tile = lambda n: pl.BlockSpec((tm, n), lambda i: (i, 0))
    return pl.pallas_call(
        body, grid=(_FFN_NT,),
        in_specs=[tile(d)] + [_resident((1, d))] * 4 + [_resident((w, d)), _resident((FFN_DIM, d)),
                                                        _resident((8, w)), _resident((1, w))],
        out_specs=[tile(d), tile(w), tile(FFN_DIM), tile(d), tile(d)],
        out_shape=[jax.ShapeDtypeStruct((SEQ, d), BF16), jax.ShapeDtypeStruct((SEQ, w), F32),
                   jax.ShapeDtypeStruct((SEQ, FFN_DIM), BF16), jax.ShapeDtypeStruct((SEQ, d), F32),
                   jax.ShapeDtypeStruct((SEQ, d), F32)],
        scratch_shapes=[pltpu.VMEM((CONV_HALO, w), F32)], name=name,
        compiler_params=_cparams("arbitrary"))(x1, norm_g, sc, sh, gate, up_t, down, conv_w, conv_b)


def _ffn_bwd(dx2, gate, f, hid, up_t, down, conv_w, conv_b, *, name):
    tm = _FFN_TM
    w = 2 * FFN_DIM
    d = D_MODEL
    last = _FFN_NT - 1

    def body(dx_ref, g_ref, f_ref, h_ref, halo_ref, up_ref, dn_ref, cw_ref, cb_ref,
             df_ref, dg_ref, dh_ref, dh2_ref, dcw_ref, dcb_ref, dhalo_acc):
        i = pl.program_id(0)

        @pl.when(i == 0)
        def _():
            dhalo_acc[...] = jnp.zeros_like(dhalo_acc)
            dcw_ref[...] = jnp.zeros_like(dcw_ref)
            dcb_ref[...] = jnp.zeros_like(dcb_ref)
            dg_ref[...] = jnp.zeros_like(dg_ref)

        dxv = dx_ref[...]
        df = (g_ref[...] * dxv).astype(BF16)
        df_ref[...] = df
        dg_ref[...] += jnp.sum(dxv * f_ref[...], axis=0, keepdims=True)
        dact = lax.dot_general(df, dn_ref[...], (((1,), (1,)), ((), ())), preferred_element_type=F32)
        halo = jnp.where(i == last, 0.0, halo_ref[...])
        hid_ext = jnp.concatenate([halo, h_ref[...]], axis=0)
        _, vjp = jax.vjp(_ffn_act_tile, hid_ext, cw_ref[...], cb_ref[...])
        gh, gw, gb = vjp(dact)
        dhid = jnp.concatenate([gh[CONV_HALO:tm], gh[tm:] + dhalo_acc[...]], axis=0).astype(BF16)
        dhalo_acc[...] = gh[:CONV_HALO]
        dh_ref[...] = dhid
        dh2_ref[...] = jnp.dot(dhid, up_ref[...], preferred_element_type=F32)
        dcw_ref[...] += gw
        dcb_ref[...] += gb

    rev = lambda i: last - i
    tile = lambda n: pl.BlockSpec((tm, n), lambda i: (rev(i), 0))
    acc = lambda shape: pl.BlockSpec(shape, lambda i: (0, 0))
    return pl.pallas_call(
        body, grid=(_FFN_NT,),
        in_specs=[tile(d), _resident((1, d)), tile(d), tile(w),
                  pl.BlockSpec((CONV_HALO, w), lambda i: (jnp.maximum(rev(i) * _FFN_HALO_PER_TILE - 1, 0), 0)),
                  _resident((w, d)), _resident((FFN_DIM, d)), _resident((8, w)), _resident((1, w))],
        out_specs=[tile(d), acc((1, d)), tile(w), tile(d), acc((8, w)), acc((1, w))],
        out_shape=[jax.ShapeDtypeStruct((SEQ, d), BF16), jax.ShapeDtypeStruct((1, d), F32),
                   jax.ShapeDtypeStruct((SEQ, w), BF16), jax.ShapeDtypeStruct((SEQ, d), F32),
                   jax.ShapeDtypeStruct((8, w), F32), jax.ShapeDtypeStruct((1, w), F32)],
        scratch_shapes=[pltpu.VMEM((CONV_HALO, w), F32)], name=name,
        compiler_params=_cparams("arbitrary"))(dx2, gate, f, hid, hid, up_t, down, conv_w, conv_b)


def _axes():
    return lax.axis_index("x"), lax.axis_index("y"), lax.axis_index("c")


def _handshake(peers):
    barrier = pltpu.get_barrier_semaphore()
    for peer in peers:
        pl.semaphore_signal(barrier, inc=1, device_id=peer, device_id_type=MESH)
    pl.semaphore_wait(barrier, len(peers))


def _allgather_body(x_refs, out_refs, send_sems, recv_sems, local_sems, own_barrier):
    n = len(x_refs)
    x, y, c = _axes()
    me, sibling = (x, y, c), (x, y, 1 - c)
    chips = [(1 - x, y), (x, 1 - y), (1 - x, 1 - y)]
    if own_barrier:
        _handshake([sibling] + [(*chip, c) for chip in chips])

    def slot(a, px, py, pc):
        return out_refs[a].at[4 * px + 2 * py + pc]

    def copy(a, k, block, to, src=None):
        return pltpu.make_async_remote_copy(
            src_ref=slot(a, *block) if src is None else src, dst_ref=slot(a, *block),
            send_sem=send_sems.at[a, k], recv_sem=recv_sems.at[a, k], device_id=to, device_id_type=MESH)

    mines, firsts = [], []
    for a in range(n):
        mines.append(pltpu.make_async_copy(x_refs[a], slot(a, *me), local_sems.at[a]))
        mines[-1].start()
        first = [copy(a, 0, me, sibling, src=x_refs[a])]
        first += [copy(a, 1 + j, me, (*chip, c), src=x_refs[a]) for j, chip in enumerate(chips)]
        for cp in first:
            cp.start()
        firsts += first
    passed = []
    for j, chip in enumerate(chips):
        for a in range(n):
            copy(a, 1 + j, (*chip, c), me).wait_recv()
            passed.append(copy(a, 4 + j, (*chip, c), sibling))
            passed[-1].start()
    for a in range(n):
        copy(a, 0, sibling, me).wait_recv()
    for j, chip in enumerate(chips):
        for a in range(n):
            copy(a, 4 + j, (*chip, 1 - c), me).wait_recv()
    for cp in firsts + passed:
        cp.wait_send()
    for cp in mines:
        cp.wait()


def _allgather_sems(n):
    return [pltpu.SemaphoreType.DMA((n, 7)), pltpu.SemaphoreType.DMA((n, 7)), pltpu.SemaphoreType.DMA((n,))]


def _allgather(xs, *, name):
    n = len(xs)

    def body(*refs):
        _allgather_body(refs[:n], refs[n:2 * n], *refs[2 * n:], own_barrier=False)

    return pl.pallas_call(
        body, out_shape=[jax.ShapeDtypeStruct((N_DEV,) + xb.shape, xb.dtype) for xb in xs],
        in_specs=[_HBM] * n, out_specs=[_HBM] * n, scratch_shapes=_allgather_sems(n), name=name)(*xs)


def _allgather_async(xs, *, name, collective_id):
    n = len(xs)
    x_refs = [jax.new_ref(xb, memory_space=pltpu.MemorySpace.HBM) for xb in xs]
    out_refs = [jax.empty_ref(jax.ShapeDtypeStruct((N_DEV,) + xb.shape, xb.dtype), memory_space=pltpu.MemorySpace.HBM)
                for xb in xs]

    @pl.kernel(mesh=plsc.ScalarSubcoreMesh(axis_name="sequencer", num_cores=1), name=name,
               scratch_types=tuple(_allgather_sems(n)),
               compiler_params=pltpu.CompilerParams(collective_id=collective_id))
    def launch(send_sems, recv_sems, local_sems):
        _allgather_body(x_refs, out_refs, send_sems, recv_sems, local_sems, own_barrier=True)

    launch()
    return [r[...] for r in out_refs]


def _pair_exchange(blocks, *, name, collective_id):
    n = len(blocks)
    hbm = pltpu.MemorySpace.HBM
    in_refs = [jax.new_ref(b, memory_space=hbm) for b in blocks]
    out_refs = [jax.empty_ref(jax.ShapeDtypeStruct((4,) + b.shape[1:], b.dtype), memory_space=hbm) for b in blocks]

    @pl.kernel(mesh=plsc.ScalarSubcoreMesh(axis_name="sequencer", num_cores=1), name=name,
               scratch_types=(pltpu.SemaphoreType.DMA((n, 4)), pltpu.SemaphoreType.DMA((n, 4))),
               compiler_params=pltpu.CompilerParams(collective_id=collective_id))
    def launch(send_sems, recv_sems):
        x, y, c = _axes()
        _handshake([(x, y, 1 - c)])
        copies = [pltpu.make_async_remote_copy(
            src_ref=in_refs[a].at[2 * s + (1 - c)], dst_ref=out_refs[a].at[s], send_sem=send_sems.at[a, s],
            recv_sem=recv_sems.at[a, s], device_id=(x, y, 1 - c), device_id_type=MESH)
            for a in range(n) for s in range(4)]
        for cp in copies:
            cp.start()
        for cp in copies:
            cp.wait_recv()
        for cp in copies:
            cp.wait_send()

    launch()
    return [r[...] for r in out_refs]


def _chip_exchange(parts, *, name, collective_id):
    n = len(parts)
    hbm = pltpu.MemorySpace.HBM
    in_refs = [jax.new_ref(p, memory_space=hbm) for p in parts]
    out_refs = [jax.empty_ref(jax.ShapeDtypeStruct(p.shape, p.dtype), memory_space=hbm) for p in parts]

    @pl.kernel(mesh=plsc.ScalarSubcoreMesh(axis_name="sequencer", num_cores=1), name=name,
               scratch_types=(pltpu.SemaphoreType.DMA((n, 3)), pltpu.SemaphoreType.DMA((n, 3)),
                              pltpu.SemaphoreType.DMA((n,))),
               compiler_params=pltpu.CompilerParams(collective_id=collective_id))
    def launch(send_sems, recv_sems, local_sems):
        x, y, c = _axes()
        my_chip = 2 * x + y
        chips = [(1 - x, y), (x, 1 - y), (1 - x, 1 - y)]
        _handshake([(*chip, c) for chip in chips])
        locals_ = [pltpu.make_async_copy(in_refs[a].at[my_chip], out_refs[a].at[my_chip], local_sems.at[a])
                   for a in range(n)]
        for cp in locals_:
            cp.start()
        copies = [pltpu.make_async_remote_copy(
            src_ref=in_refs[a].at[2 * px + py], dst_ref=out_refs[a].at[my_chip], send_sem=send_sems.at[a, k],
            recv_sem=recv_sems.at[a, k], device_id=(px, py, c), device_id_type=MESH)
            for a in range(n) for k, (px, py) in enumerate(chips)]
        for cp in copies:
            cp.start()
        for cp in copies:
            cp.wait_recv()
        for cp in copies:
            cp.wait_send()
        for cp in locals_:
            cp.wait()

    launch()
    return [r[...] for r in out_refs]


def _pair_sum(core, blocks, from_sibling, *, name):
    n = len(blocks)

    def body(core_ref, *refs):
        for a_ref, b_ref, o_ref in zip(refs[:n], refs[n:2 * n], refs[2 * n:]):
            o_ref[...] = (a_ref[...].astype(F32) + b_ref[...].astype(F32)).astype(o_ref.dtype)

    mine = lambda b: pl.BlockSpec((1,) + b.shape[1:], lambda s, core_ref: (2 * s + core_ref[0], 0, 0))
    slot = lambda b: pl.BlockSpec((1,) + b.shape[1:], lambda s, core_ref: (s, 0, 0))
    return pl.pallas_call(
        body,
        grid_spec=pltpu.PrefetchScalarGridSpec(
            num_scalar_prefetch=1, grid=(4,),
            in_specs=[mine(b) for b in blocks] + [slot(b) for b in blocks], out_specs=[slot(b) for b in blocks]),
        out_shape=[jax.ShapeDtypeStruct(s.shape, s.dtype) for s in from_sibling], name=name,
        compiler_params=_cparams("parallel"))(core, *blocks, *from_sibling)


def _sum_blocks(a, *, name, tr=None):
    n, r, cdim = a.shape
    tr = tr or r

    def body(a_ref, o_ref):
        acc = a_ref[0].astype(F32)
        for k in range(1, n):
            acc = acc + a_ref[k].astype(F32)
        o_ref[...] = acc

    return pl.pallas_call(body, grid=(r // tr,), in_specs=[pl.BlockSpec((n, tr, cdim), lambda i: (0, i, 0))],
                          out_specs=pl.BlockSpec((tr, cdim), lambda i: (i, 0)),
                          out_shape=jax.ShapeDtypeStruct((r, cdim), F32), name=name,
                          compiler_params=_cparams("parallel"))(a)


def _sum_gathered(gathered, *, name):
    n = len(gathered)

    def body(*refs):
        for a_ref, o_ref in zip(refs[:n], refs[n:]):
            acc = a_ref[0]
            for k in range(1, N_DEV):
                acc = acc + a_ref[k]
            o_ref[...] = acc

    return pl.pallas_call(body, out_shape=[jax.ShapeDtypeStruct(g.shape[1:], F32) for g in gathered], name=name,
                          compiler_params=pltpu.CompilerParams(vmem_limit_bytes=VMEM_LIMIT))(*gathered)


_ADA_SHARD = 6 * D_MODEL // N_DEV


def _ada_mod(c_all, ada_w, *, name):
    def body(c_ref, w_ref, o_ref):
        o_ref[0] = jnp.dot(_silu(c_ref[...]).astype(BF16), w_ref[0].astype(BF16), preferred_element_type=F32)

    return pl.pallas_call(
        body, grid=(DEPTH,),
        in_specs=[pl.BlockSpec((N_DEV, D_MODEL), lambda l: (0, 0)),
                  pl.BlockSpec((1, D_MODEL, _ADA_SHARD), lambda l: (l, 0, 0))],
        out_specs=pl.BlockSpec((1, N_DEV, _ADA_SHARD), lambda l: (l, 0, 0)),
        out_shape=jax.ShapeDtypeStruct((DEPTH, N_DEV, _ADA_SHARD), F32), name=name,
        compiler_params=_cparams("parallel"))(c_all, ada_w)


def _ada_wgrad(c_all, dmod_cols, *, name):
    def body(c_ref, d_ref, o_ref):
        o_ref[0] = lax.dot_general(_silu(c_ref[...]), d_ref[0], (((0,), (0,)), ((), ())),
                                   preferred_element_type=F32, precision=lax.Precision.HIGHEST)

    return pl.pallas_call(
        body, grid=(DEPTH,),
        in_specs=[pl.BlockSpec((N_DEV, D_MODEL), lambda l: (0, 0)),
                  pl.BlockSpec((1, N_DEV, _ADA_SHARD), lambda l: (l, 0, 0))],
        out_specs=pl.BlockSpec((1, D_MODEL, _ADA_SHARD), lambda l: (l, 0, 0)),
        out_shape=jax.ShapeDtypeStruct((DEPTH, D_MODEL, _ADA_SHARD), F32), name=name,
        compiler_params=_cparams("parallel"))(c_all, dmod_cols)


def _add_rows(a, b, *, name):
    def body(a_ref, b_ref, o_ref):
        o_ref[...] = a_ref[...] + b_ref[...]

    return pl.pallas_call(body, out_shape=jax.ShapeDtypeStruct(a.shape, a.dtype), name=name)(a, b)


def _adamw_update(w_ref, g_ref, m_ref, v_ref, d_ref, mo_ref, vo_ref):
    gv = g_ref[...]
    mn = ADAM_B1 * m_ref[...] + (1.0 - ADAM_B1) * gv
    vn = ADAM_B2 * v_ref[...] + (1.0 - ADAM_B2) * (gv * gv)
    mo_ref[...] = mn
    vo_ref[...] = vn
    m_hat = mn / (1.0 - ADAM_B1 ** ADAM_STEP)
    v_hat = vn / (1.0 - ADAM_B2 ** ADAM_STEP)
    d_ref[...] = -ADAM_LR * (m_hat / (jnp.sqrt(v_hat) + ADAM_EPS) + ADAM_WD * w_ref[...])


def _adamw_small(ws, gs, ms, vs, *, name):
    n = len(ws)

    def body(*refs):
        ins, outs = refs[:4 * n], refs[4 * n:]
        for i in range(n):
            _adamw_update(ins[i], ins[n + i], ins[2 * n + i], ins[3 * n + i], outs[i], outs[n + i], outs[2 * n + i])

    shapes = [jax.ShapeDtypeStruct(a.shape, F32) for a in ws]
    outs = pl.pallas_call(body, out_shape=shapes * 3, name=name,
                          compiler_params=pltpu.CompilerParams(vmem_limit_bytes=VMEM_LIMIT))(*ws, *gs, *ms, *vs)
    return outs[:n], outs[n:2 * n], outs[2 * n:]


def _adamw(w, g, m, v, *, name, tr):
    r, cdim = w.shape
    body = functools.partial(_adamw_update)

    spec = pl.BlockSpec((tr, cdim), lambda i: (i, 0))
    shp = jax.ShapeDtypeStruct((r, cdim), F32)
    return pl.pallas_call(body, grid=(r // tr,), in_specs=[spec] * 4, out_specs=[spec] * 3, out_shape=[shp] * 3,
                          name=name, compiler_params=_cparams("parallel"))(w, g, m, v)


def _pad_rows(a, rows):
    return jnp.concatenate([a, jnp.zeros((rows - a.shape[0],) + a.shape[1:], a.dtype)], axis=0)


def _pad_lanes(a, lanes):
    return jnp.concatenate([a, jnp.zeros(a.shape[:-1] + (lanes - a.shape[-1],), a.dtype)], axis=-1)


def _permute_w_in(wt):
    return jnp.concatenate([wt[512:1536], wt[:512], wt[1536:1544],
                            jnp.zeros((PROJ_W - IN_W, wt.shape[1]), wt.dtype), wt[1544:]], axis=0)


def _unpermute_w_in(wp):
    return jnp.concatenate([wp[1024:1536], wp[:1024], wp[1536:1544], wp[PROJ_SSD_W:]], axis=0)


def _block_diag(w):
    rows = []
    for g in range(4):
        rows.append(jnp.concatenate([w[g] if k == g else jnp.zeros_like(w[g]) for k in range(4)], axis=1))
    return jnp.concatenate(rows, axis=0)


def _diag_blocks(wbd):
    return jnp.stack([wbd[64 * g:64 * (g + 1), 64 * g:64 * (g + 1)] for g in range(4)], axis=0)


def _layer_params(l, small):
    return dict(
        norm1_g=small["norm1_g"][l][None], norm2_g=small["norm2_g"][l][None],
        conv_w=_pad_rows(small["ssd_conv_w"][l], 8), conv_b=small["ssd_conv_b"][l][None],
        dt_bias=_pad_lanes(small["ssd_dt_bias"][l][None], 128), a_log=_pad_lanes(small["ssd_a_log"][l][None], 128),
        d_skip=_pad_lanes(small["ssd_d"][l][None], 128), ssd_norm_g=small["ssd_norm_g"][l][None],
        pool_bd=_block_diag(small["pool_w"][l]), pool_scale=small["pool_scale"][l][None],
        fcw=_pad_rows(small["ffn_conv_w"][l], 8), fcb=small["ffn_conv_b"][l][None])


def _mod_rows(mod_l):
    return [mod_l[None, D_MODEL * i:D_MODEL * (i + 1)] for i in range(6)]


def _layer_fwd(x, mod_l, p, tabs, l, gather):
    sh1, sc1, g1, sh2, sc2, g2 = _mod_rows(mod_l)
    mix_w = gather(l, "mix", None)
    p.update(w_in=mix_w["w_in"], w_out=mix_w["w_out"])
    proj, h1 = _mm(x, p["w_in"], nt=True, norm=(p["norm1_g"], sc1, sh1), name=f"l{l}_proj")
    ffn_w = gather(l, "ffn", proj)
    p.update(up=ffn_w["ffn_up"], down=ffn_w["ffn_down"])
    mix, hst = _ssd_fwd(proj, p["conv_w"], p["conv_b"], p["dt_bias"], p["a_log"], p["d_skip"], p["ssd_norm_g"],
                        name=f"l{l}_ssd")
    mix = _pool_fwd(proj, p["pool_bd"], p["pool_scale"], mix, name=f"l{l}_pool")
    y_att, lse, mix = _att_fwd(proj, *tabs, mix, name=f"l{l}_att")
    mo, x1 = _mm(mix, p["w_out"], residual=(x, g1), name=f"l{l}_out")
    gather(l + 1, "mix", (x1, p["up"]))
    h2, hid, act, f, x2 = _ffn_fwd(x1, p["norm2_g"], sc2, sh2, g2, p["up"], p["down"], p["fcw"], p["fcb"],
                                   name=f"l{l}_ffn")
    return x2, dict(x=x, h1=h1, proj=proj, hst=hst, y_att=y_att, lse=lse, mix=mix, mo=mo, x1=x1, h2=h2, hid=hid,
                    act=act, f=f)


def _layer_bwd(dx2, sv, mod_l, p, tabs, l, exchange):
    sh1, sc1, g1, sh2, sc2, g2 = _mod_rows(mod_l)
    df, dg2, dhid, dh2, dfcw, dfcb = _ffn_bwd(dx2, g2, sv["f"], sv["hid"], p["up"], p["down"], p["fcw"], p["fcb"],
                                              name=f"l{l}_ffn_b")
    d_down = _wgrad(sv["act"], df, tk=1408, name=f"l{l}_down_bw")
    d_up = _wgrad(dhid, sv["h2"], tk=1408, name=f"l{l}_up_bw")
    exchange(l, "ffn", dict(ffn_up=d_up, ffn_down=d_down))
    dx1, dn2, dsc2, dsh2, dmix, dmo, dg1 = _mid_bwd(sv["x1"], dh2, dx2, p["norm2_g"], sc2, g1, sv["mo"], p["w_out"],
                                                    name=f"l{l}_mid_b")
    d_wout = _wgrad(sv["mix"], dmo, name=f"l{l}_out_bw")
    dproj, dcw, dcb, ddb, dal, dd, dng = _ssd_bwd(
        sv["proj"], sv["hst"], dmix, p["conv_w"], p["conv_b"], p["dt_bias"], p["a_log"], p["d_skip"],
        p["ssd_norm_g"], name=f"l{l}_ssd_b")
    dproj, dwbd, dpsc = _pool_bwd(sv["proj"], dmix, p["pool_bd"], p["pool_scale"], dproj, name=f"l{l}_pool_b")
    dproj = _att_bwd(sv["proj"], *tabs, sv["y_att"], sv["lse"], dmix, dproj, name=f"l{l}_att_b")
    d_win = _wgrad(dproj, sv["h1"], tk=1408, name=f"l{l}_proj_bw")
    exchange(l, "mix", dict(w_in=d_win, w_out=d_wout))
    dx0, dn1, dsc1, dsh1 = _norm_mod_bwd(sv["x"], dproj, dx1, p["norm1_g"], sc1, w=p["w_in"], name=f"l{l}_proj_b")
    dmod = jnp.concatenate([dsh1, dsc1, dg1, dsh2, dsc2, dg2], axis=1)[0]
    small = dict(norm1_g=dn1[0], ssd_conv_w=dcw[:SSD_CONV_K], ssd_conv_b=dcb[0], ssd_dt_bias=ddb[0], ssd_a_log=dal[0],
                 ssd_d=dd[0], ssd_norm_g=dng[0], pool_w=_diag_blocks(dwbd), pool_scale=dpsc[0], norm2_g=dn2[0],
                 ffn_conv_w=dfcw[:FFN_CONV_K], ffn_conv_b=dfcb[0])
    return dx0, dmod, small


def _example_step(x, target, pos_col, inv_freq_lane, mod, gather, small, final_g, exchange):
    tabs = _rope_tables(pos_col, inv_freq_lane, name="rope_tables")
    params, saved = [], []
    for l in range(DEPTH):
        params.append(_layer_params(l, small))
        x, sv = _layer_fwd(x, mod[l], params[l], tabs, l, gather)
        saved.append(sv)
    loss_row, dx, dfg = _final_loss(x, final_g[None], target, name="final_loss")
    dmods, smalls = [None] * DEPTH, [None] * DEPTH
    for l in reversed(range(DEPTH)):
        dx, dmods[l], smalls[l] = _layer_bwd(dx, saved[l], mod[l], params[l], tabs, l, exchange)
    return loss_row, dx, jnp.stack(dmods, axis=0), smalls, dfg[0]


_BIG = ("w_in", "w_out", "ffn_up", "ffn_down")
_SMALL_GRADS = ("norm1_g", "ssd_conv_w", "ssd_conv_b", "ssd_dt_bias", "ssd_a_log", "ssd_d", "ssd_norm_g", "pool_w",
                "pool_scale", "norm2_g", "ffn_conv_w", "ffn_conv_b")
_SMALL_PARAMS = ("ada_b", "norm1_g", "ssd_conv_w", "ssd_conv_b", "ssd_dt_bias", "ssd_a_log", "ssd_d", "ssd_norm_g",
                 "pool_w", "pool_scale", "norm2_g", "ffn_conv_w", "ffn_conv_b", "final_g")
_WEIGHT_ORDER = ("ada_w", "ada_b", "norm1_g", "w_in", "ssd_conv_w", "ssd_conv_b", "ssd_dt_bias", "ssd_a_log", "ssd_d",
                 "ssd_norm_g", "pool_w", "pool_scale", "w_out", "norm2_g", "ffn_up", "ffn_conv_w", "ffn_conv_b",
                 "ffn_down", "final_g")


_COLUMN_SHARDED = ("w_in", "ffn_up")
_GROUPS = (("mix", ("w_in", "w_out")), ("ffn", ("ffn_up", "ffn_down")))


def _big_shares(w, l, names):
    return [(w[name][l].T if name in _COLUMN_SHARDED else w[name][l]).astype(BF16) for name in names]


def _unshard_big(names, gathered):
    out = {}
    for name, g in zip(names, gathered):
        full = g.reshape(N_DEV * g.shape[1], g.shape[2])
        out[name] = _permute_w_in(full) if name == "w_in" else full
    return out


def _shard_big(grads):
    out = []
    for name, g in grads.items():
        g = _unpermute_w_in(g) if name == "w_in" else g
        out.append(g.reshape(N_DEV, g.shape[0] // N_DEV, g.shape[1]))
    return out


def kernel(x, c, positions, ada_w, ada_b, norm1_g, w_in, ssd_conv_w, ssd_conv_b, ssd_dt_bias, ssd_a_log, ssd_d, ssd_norm_g, pool_w, pool_scale, w_out, norm2_g, ffn_up, ffn_conv_w, ffn_conv_b, ffn_down, final_g, loss_target, m_ada_w, m_ada_b, m_norm1_g, m_w_in, m_ssd_conv_w, m_ssd_conv_b, m_ssd_dt_bias, m_ssd_a_log, m_ssd_d, m_ssd_norm_g, m_pool_w, m_pool_scale, m_w_out, m_norm2_g, m_ffn_up, m_ffn_conv_w, m_ffn_conv_b, m_ffn_down, m_final_g, v_ada_w, v_ada_b, v_norm1_g, v_w_in, v_ssd_conv_w, v_ssd_conv_b, v_ssd_dt_bias, v_ssd_a_log, v_ssd_d, v_ssd_norm_g, v_pool_w, v_pool_scale, v_w_out, v_norm2_g, v_ffn_up, v_ffn_conv_w, v_ffn_conv_b, v_ffn_down, v_final_g):
    w = dict(ada_w=ada_w, ada_b=ada_b, norm1_g=norm1_g, w_in=w_in, ssd_conv_w=ssd_conv_w, ssd_conv_b=ssd_conv_b,
             ssd_dt_bias=ssd_dt_bias, ssd_a_log=ssd_a_log, ssd_d=ssd_d, ssd_norm_g=ssd_norm_g, pool_w=pool_w,
             pool_scale=pool_scale, w_out=w_out, norm2_g=norm2_g, ffn_up=ffn_up, ffn_conv_w=ffn_conv_w,
             ffn_conv_b=ffn_conv_b, ffn_down=ffn_down, final_g=final_g)
    m = dict(ada_w=m_ada_w, ada_b=m_ada_b, norm1_g=m_norm1_g, w_in=m_w_in, ssd_conv_w=m_ssd_conv_w,
             ssd_conv_b=m_ssd_conv_b, ssd_dt_bias=m_ssd_dt_bias, ssd_a_log=m_ssd_a_log, ssd_d=m_ssd_d,
             ssd_norm_g=m_ssd_norm_g, pool_w=m_pool_w, pool_scale=m_pool_scale, w_out=m_w_out, norm2_g=m_norm2_g,
             ffn_up=m_ffn_up, ffn_conv_w=m_ffn_conv_w, ffn_conv_b=m_ffn_conv_b, ffn_down=m_ffn_down,
             final_g=m_final_g)
    v = dict(ada_w=v_ada_w, ada_b=v_ada_b, norm1_g=v_norm1_g, w_in=v_w_in, ssd_conv_w=v_ssd_conv_w,
             ssd_conv_b=v_ssd_conv_b, ssd_dt_bias=v_ssd_dt_bias, ssd_a_log=v_ssd_a_log, ssd_d=v_ssd_d,
             ssd_norm_g=v_ssd_norm_g, pool_w=v_pool_w, pool_scale=v_pool_scale, w_out=v_w_out, norm2_g=v_norm2_g,
             ffn_up=v_ffn_up, ffn_conv_w=v_ffn_conv_w, ffn_conv_b=v_ffn_conv_b, ffn_down=v_ffn_down,
             final_g=v_final_g)
    ix, iy, ic = _axes()
    dev = 4 * ix + 2 * iy + ic

    c_all, scw, fcw = _allgather([c, ssd_conv_w.reshape(DEPTH * SSD_CONV_K, -1),
                                  ffn_conv_w.reshape(DEPTH * FFN_CONV_K, -1)], name="gather_small")
    small_all = c_all
    c_all = c_all.reshape(N_DEV, D_MODEL)
    scw = scw.reshape(N_DEV, DEPTH, SSD_CONV_K, -1).transpose(1, 2, 0, 3).reshape(DEPTH, SSD_CONV_K, SSD_CONV_CH)
    fcw = fcw.reshape(N_DEV, DEPTH, FFN_CONV_K, -1).transpose(1, 2, 0, 3).reshape(DEPTH, FFN_CONV_K, 2 * FFN_DIM)

    mod_cols = _ada_mod(c_all, ada_w, name="ada_mod")
    mod_all = _allgather([mod_cols.reshape(DEPTH * N_DEV, _ADA_SHARD)], name="gather_mod")[0]
    mod_all = mod_all.reshape(N_DEV, DEPTH, N_DEV, _ADA_SHARD)
    mod_mine = lax.dynamic_index_in_dim(mod_all, dev, axis=2, keepdims=False)
    mod = _add_rows(mod_mine.transpose(1, 0, 2).reshape(DEPTH, 6 * D_MODEL), ada_b, name="ada_bias")

    fetched = {}

    def gather(l, group, after):
        if l < DEPTH and (l, group) not in fetched:
            names = dict(_GROUPS)[group]
            shares, _ = lax.optimization_barrier((_big_shares(w, l, names), small_all if after is None else after))
            got = _allgather_async(shares, name=f"gather_weights_l{l}_{group}",
                                   collective_id=1 + 2 * l + (group == "ffn"))
            fetched[l, group] = _unshard_big(names, got)
        return fetched.get((l, group))

    core = ic.astype(jnp.int32).reshape(1)
    from_chips = {}

    def exchange(l, group, g):
        cid = 5 + 4 * l + 2 * (group == "mix")
        blocks = _shard_big(g)
        from_sibling = _pair_exchange(blocks, name=f"grads_pair_exchange_l{l}_{group}", collective_id=cid)
        parts = _pair_sum(core, blocks, from_sibling, name=f"grads_pair_sum_l{l}_{group}")
        got = _chip_exchange(parts, name=f"grads_chip_exchange_l{l}_{group}", collective_id=cid + 1)
        from_chips.update({(l, name): t for name, t in zip(g, got)})

    small = dict(norm1_g=norm1_g, norm2_g=norm2_g, ssd_conv_w=scw, ssd_conv_b=ssd_conv_b, ssd_dt_bias=ssd_dt_bias,
                 ssd_a_log=ssd_a_log, ssd_d=ssd_d, ssd_norm_g=ssd_norm_g, pool_w=pool_w, pool_scale=pool_scale,
                 ffn_conv_w=fcw, ffn_conv_b=ffn_conv_b)

    inv_freq = ROPE_THETA ** (-jnp.arange(0, ROT_DIM, 2, dtype=F32) / ROT_DIM)
    lane = jnp.arange(128) % HEAD_LANES
    inv_freq_lane = jnp.where(lane < ROT_DIM, inv_freq[lane % (ROT_DIM // 2)], 0.0)[None, :]
    pos_col = positions.reshape(SEQ, 1).astype(F32)
    loss_row, dx, dmod, g_small, g_final = _example_step(
        x[0], loss_target[0], pos_col, inv_freq_lane, mod, gather, small, final_g, exchange)

    grads = {}
    for name in _BIG:
        per_layer = [_sum_blocks(from_chips[l, name], name=f"grads_chip_sum_l{l}_{name}") for l in range(DEPTH)]
        grads[name] = jnp.stack([g.T if name in _COLUMN_SHARDED else g for g in per_layer], axis=0)

    small_names = list(_SMALL_GRADS)
    stacked = [jnp.stack([g_small[l][name] for l in range(DEPTH)], axis=0) for name in small_names]
    small_parts = [loss_row, dmod] + [s.reshape(-1, s.shape[-1]) for s in stacked] + [g_final[None]]
    gathered = _allgather(small_parts, name="gather_small_grads")
    total = _sum_gathered(gathered, name="sum_small_grads")
    loss = total[0][0, 0]
    grads["ada_b"] = total[1]
    grads.update(zip(small_names, total[2:-1]))
    grads["final_g"] = total[-1][0]
    dmod_cols = lax.dynamic_slice_in_dim(gathered[1], dev * _ADA_SHARD, _ADA_SHARD, axis=2).transpose(1, 0, 2)
    grads["ada_w"] = _ada_wgrad(c_all, dmod_cols, name="ada_wgrad")
    for name in ("ssd_dt_bias", "ssd_a_log", "ssd_d"):
        grads[name] = grads[name][:, :SSD_HEADS]
    grads["pool_w"] = grads["pool_w"].reshape(pool_w.shape)
    grads["ssd_conv_w"] = lax.dynamic_slice_in_dim(
        grads["ssd_conv_w"].reshape(DEPTH, SSD_CONV_K, SSD_CONV_CH), dev * ssd_conv_w.shape[2], ssd_conv_w.shape[2], axis=2)
    grads["ffn_conv_w"] = lax.dynamic_slice_in_dim(
        grads["ffn_conv_w"].reshape(DEPTH, FFN_CONV_K, 2 * FFN_DIM), dev * ffn_conv_w.shape[2], ffn_conv_w.shape[2], axis=2)

    delta, new_m, new_v = {}, {}, {}
    for name, tr in (("ada_w", 512), ("w_in", 512), ("w_out", 256), ("ffn_up", 512), ("ffn_down", 352)):
        shp = w[name].shape
        two_d = lambda a: a.reshape(shp[0] * shp[1], shp[2])
        d_, m_, v_ = _adamw(two_d(w[name]), two_d(grads[name]), two_d(m[name]), two_d(v[name]), tr=tr,
                               name=f"adamw_{name}")
        delta[name], new_m[name], new_v[name] = (t.reshape(shp) for t in (d_, m_, v_))
    two_d = lambda a: a.reshape(-1, a.shape[-1])
    outs = _adamw_small(*[[two_d(t[name]) for name in _SMALL_PARAMS] for t in (w, grads, m, v)], name="adamw_small")
    for name, d_, m_, v_ in zip(_SMALL_PARAMS, *outs):
        delta[name], new_m[name], new_v[name] = (t.reshape(w[name].shape) for t in (d_, m_, v_))

    grad_x = dx[None]
    return (loss, grad_x, *[grads[n].reshape(w[n].shape) for n in _WEIGHT_ORDER],
            *[delta[n] for n in _WEIGHT_ORDER], *[new_m[n] for n in _WEIGHT_ORDER],
            *[new_v[n] for n in _WEIGHT_ORDER])
```

```python
import functools
import math

import jax
import jax.numpy as jnp
from jax import lax
from jax.experimental import pallas as pl
from jax.experimental.pallas import tpu as pltpu
from jax.experimental.pallas import tpu_sc as plsc

F32 = jnp.float32
BF16 = jnp.bfloat16

N_DEV = 8
D_MODEL = 1024
SEQ = 4096
DEPTH = 2
SSD_INNER = 512
SSD_HEADS = 8
SSD_HPG = 4
SSD_STATE = 128
SSD_CHUNK = 256
SSD_CONV_K = 4
SSD_CONV_CH = 1024
POOL_W = 256
POOL_WINDOWS = (2, 4, 8, 16)
ATT_W = 256
ATT_PATTERNS = ((128, 1), (512, 4), (2048, 16))
ATT_BLOCK = 128
ROT_DIM = 16
ROPE_THETA = 500000.0
IN_W = 2568
FFN_DIM = 2816
FFN_CONV_K = 3
NORM_EPS = 1e-6
HEAD_LANES = 64

ADAM_LR = 0.001
ADAM_B1 = 0.9
ADAM_B2 = 0.999
ADAM_EPS = 1e-08
ADAM_WD = 0.01
ADAM_STEP = 10

PROJ_W = 2816
PROJ_SSD_W = 1792
PROJ_Z_BLK = 2
PROJ_DT_BLK = 12
PROJ_POOL_BLK = 7
PROJ_Q_BLK, PROJ_K_BLK, PROJ_V_BLK = 16, 18, 20
MIX_POOL_BLK = 2
MIX_ATT_BLK = 6
VMEM_LIMIT = 56 * 1024 * 1024
ROW_TILE = 1024
CONV_HALO = 8
POOL_HALO = 16
ATT_KPAD = ATT_BLOCK * 16
MESH = pl.DeviceIdType.MESH
_HBM = pl.BlockSpec(memory_space=pl.ANY)


def _cparams(*sem):
    return pltpu.CompilerParams(dimension_semantics=sem, vmem_limit_bytes=VMEM_LIMIT)


def _resident(shape):
    return pl.BlockSpec(shape, lambda i: (0,) * len(shape), pipeline_mode=pl.Buffered(1))


def _silu(x):
    return x * jax.nn.sigmoid(x)


def _pick_lane(v, h):
    lane = lax.broadcasted_iota(jnp.int32, v.shape, 1)
    return jnp.sum(jnp.where(lane == h, v, 0.0), axis=1, keepdims=True)


def _pick_row(v, h):
    row = lax.broadcasted_iota(jnp.int32, v.shape, 0)
    return jnp.sum(jnp.where(row == h, v, 0.0), axis=0, keepdims=True)


def _head_of_lane(width):
    return lax.broadcasted_iota(jnp.int32, (1, width), 1) // HEAD_LANES


@functools.partial(jax.custom_vjp, nondiff_argnums=(1, 2))
def _shift_rows(x_ext, s, halo):
    y = x_ext if s == 0 else pltpu.roll(x_ext, s, 0)
    return y[halo:]


def _shift_rows_fwd(x_ext, s, halo):
    return _shift_rows(x_ext, s, halo), None


def _shift_rows_bwd(s, halo, _, g):
    ge = jnp.concatenate([jnp.zeros((halo, g.shape[1]), g.dtype), g], axis=0)
    return (ge if s == 0 else pltpu.roll(ge, ge.shape[0] - s, 0),)


_shift_rows.defvjp(_shift_rows_fwd, _shift_rows_bwd)


@functools.partial(jax.custom_vjp, nondiff_argnums=(1,))
def _roll_rows(x, s):
    return pltpu.roll(x, s, 0)


def _roll_rows_fwd(x, s):
    return _roll_rows(x, s), None


def _roll_rows_bwd(s, _, g):
    return (pltpu.roll(g, g.shape[0] - s, 0),)


_roll_rows.defvjp(_roll_rows_fwd, _roll_rows_bwd)


def _rms_modulate(xv, g, sc, sh):
    r = lax.rsqrt(jnp.mean(xv * xv, axis=-1, keepdims=True) + NORM_EPS)
    return (xv * r * g) * (1.0 + sc) + sh


def _mm(a, w, *, name, nt=False, tm=ROW_TILE, tn=None, out_dtype=F32, norm=None, residual=None):
    t, k = a.shape
    n = w.shape[0] if nt else w.shape[1]
    tn = tn or n
    assert tn == n or (norm is None and residual is None)
    extra_in = list(norm or ()) + list(residual or ())

    def body(*refs):
        a_ref, w_ref = refs[:2]
        ins = refs[2:2 + len(extra_in)]
        outs = refs[2 + len(extra_in):]
        if norm is None:
            av = a_ref[...].astype(BF16)
        else:
            av = _rms_modulate(a_ref[...], ins[0][...], ins[1][...], ins[2][...]).astype(BF16)
            outs[1][...] = av
        if nt:
            acc = lax.dot_general(av, w_ref[...], (((1,), (1,)), ((), ())), preferred_element_type=F32)
        else:
            acc = jnp.dot(av, w_ref[...], preferred_element_type=F32)
        outs[0][...] = acc.astype(out_dtype)
        if residual is not None:
            x_ref, gate_ref = ins[-2:]
            outs[-1][...] = x_ref[...] + gate_ref[...] * acc

    row = lambda width: pl.BlockSpec((1, width), lambda i, j: (0, 0))
    tile = lambda width: pl.BlockSpec((tm, width), lambda i, j: (i, 0))
    w_spec = pl.BlockSpec((tn, k), lambda i, j: (j, 0)) if nt else pl.BlockSpec((k, tn), lambda i, j: (0, j))
    in_specs = [tile(k), w_spec] + ([row(k)] * 3 if norm else []) + ([tile(n), row(n)] if residual else [])
    out_specs = [pl.BlockSpec((tm, tn), lambda i, j: (i, j))] + ([tile(k)] if norm else []) + \
        ([tile(n)] if residual else [])
    out_shape = [jax.ShapeDtypeStruct((t, n), out_dtype)] + \
        ([jax.ShapeDtypeStruct((t, k), BF16)] if norm else []) + \
        ([jax.ShapeDtypeStruct((t, n), F32)] if residual else [])
    outs = pl.pallas_call(
        body, grid=(t // tm, n // tn), in_specs=in_specs, out_specs=out_specs, out_shape=out_shape, name=name,
        compiler_params=_cparams("parallel", "parallel"))(a, w, *extra_in)
    return outs[0] if len(outs) == 1 else outs


def _wgrad(a, b, *, name, tk=None, tn=None, tt=2048, out_dtype=BF16):
    t, k = a.shape
    n = b.shape[1]
    tk = tk or k
    tn = tn or n
    steps = t // tt

    def body(a_ref, b_ref, o_ref, acc_ref):
        s = pl.program_id(2)

        @pl.when(s == 0)
        def _():
            acc_ref[...] = jnp.zeros_like(acc_ref)

        acc_ref[...] += lax.dot_general(a_ref[...].astype(BF16), b_ref[...].astype(BF16),
                                        (((0,), (0,)), ((), ())), preferred_element_type=F32)

        @pl.when(s == steps - 1)
        def _():
            o_ref[...] = acc_ref[...].astype(out_dtype)

    return pl.pallas_call(
        body, grid=(k // tk, n // tn, steps),
        in_specs=[pl.BlockSpec((tt, tk), lambda i, j, s: (s, i)), pl.BlockSpec((tt, tn), lambda i, j, s: (s, j))],
        out_specs=pl.BlockSpec((tk, tn), lambda i, j, s: (i, j)),
        out_shape=jax.ShapeDtypeStruct((k, n), out_dtype),
        scratch_shapes=[pltpu.VMEM((tk, tn), F32)], name=name,
        compiler_params=_cparams("parallel", "parallel", "arbitrary"))(a, b)


def _norm_mod_bwd(x, dh, dres, g, sc, *, name, w=None, tm=ROW_TILE):
    s, d = x.shape
    steps = s // tm

    def body(x_ref, dh_ref, dres_ref, g_ref, sc_ref, *rest):
        w_ref = rest[0] if w is not None else None
        dx_ref, dg_ref, dsc_ref, dsh_ref, da_acc, dsh_acc = rest[-6:]
        i = pl.program_id(0)

        @pl.when(i == 0)
        def _():
            da_acc[...] = jnp.zeros_like(da_acc)
            dsh_acc[...] = jnp.zeros_like(dsh_acc)

        xv = x_ref[...]
        if w is None:
            dhv = dh_ref[...].astype(F32)
        else:
            dhv = jnp.dot(dh_ref[...], w_ref[...], preferred_element_type=F32)
        r = lax.rsqrt(jnp.mean(xv * xv, axis=-1, keepdims=True) + NORM_EPS)
        xhat = xv * r
        gain = g_ref[...] * (1.0 + sc_ref[...])
        dxhat = dhv * gain
        dx_ref[...] = dres_ref[...] + r * (dxhat - xhat * jnp.mean(dxhat * xhat, axis=-1, keepdims=True))
        da_acc[...] += jnp.sum(dhv * xhat, axis=0, keepdims=True)
        dsh_acc[...] += jnp.sum(dhv, axis=0, keepdims=True)

        @pl.when(i == steps - 1)
        def _():
            dg_ref[...] = da_acc[...] * (1.0 + sc_ref[...])
            dsc_ref[...] = da_acc[...] * g_ref[...]
            dsh_ref[...] = dsh_acc[...]

    row = pl.BlockSpec((1, d), lambda i: (0, 0))
    tile = pl.BlockSpec((tm, d), lambda i: (i, 0))
    row_shape = jax.ShapeDtypeStruct((1, d), F32)
    dh_spec = tile if w is None else pl.BlockSpec((tm, dh.shape[1]), lambda i: (i, 0))
    return pl.pallas_call(
        body, grid=(steps,), in_specs=[tile, dh_spec, tile, row, row] + ([] if w is None else [_resident(w.shape)]),
        out_specs=[tile, row, row, row],
        out_shape=[jax.ShapeDtypeStruct((s, d), F32), row_shape, row_shape, row_shape],
        scratch_shapes=[pltpu.VMEM((1, d), F32), pltpu.VMEM((1, d), F32)], name=name,
        compiler_params=_cparams("arbitrary"))(x, dh, dres, g, sc, *([] if w is None else [w]))


def _mid_bwd(x1, dh2, dx2, norm_g, sc, gate, mo, w_out, *, name, tm=512):
    s, d = x1.shape
    steps = s // tm

    def body(x_ref, dh_ref, dres_ref, ng_ref, sc_ref, g_ref, mo_ref, w_ref,
             dx_ref, dng_ref, dsc_ref, dsh_ref, dmix_ref, dmo_ref, dg_ref, da_acc, dsh_acc):
        i = pl.program_id(0)

        @pl.when(i == 0)
        def _():
            da_acc[...] = jnp.zeros_like(da_acc)
            dsh_acc[...] = jnp.zeros_like(dsh_acc)
            dg_ref[...] = jnp.zeros_like(dg_ref)

        xv = x_ref[...]
        dhv = dh_ref[...]
        r = lax.rsqrt(jnp.mean(xv * xv, axis=-1, keepdims=True) + NORM_EPS)
        xhat = xv * r
        dxhat = dhv * (ng_ref[...] * (1.0 + sc_ref[...]))
        dxv = dres_ref[...] + r * (dxhat - xhat * jnp.mean(dxhat * xhat, axis=-1, keepdims=True))
        dx_ref[...] = dxv
        da_acc[...] += jnp.sum(dhv * xhat, axis=0, keepdims=True)
        dsh_acc[...] += jnp.sum(dhv, axis=0, keepdims=True)
        dmo = (g_ref[...] * dxv).astype(BF16)
        dmo_ref[...] = dmo
        dg_ref[...] += jnp.sum(dxv * mo_ref[...], axis=0, keepdims=True)
        dmix_ref[...] = lax.dot_general(dmo, w_ref[...], (((1,), (1,)), ((), ())), preferred_element_type=F32)

        @pl.when(i == steps - 1)
        def _():
            dng_ref[...] = da_acc[...] * (1.0 + sc_ref[...])
            dsc_ref[...] = da_acc[...] * ng_ref[...]
            dsh_ref[...] = dsh_acc[...]

    tile = pl.BlockSpec((tm, d), lambda i: (i, 0))
    row = pl.BlockSpec((1, d), lambda i: (0, 0))
    mix_tile = pl.BlockSpec((tm, w_out.shape[0]), lambda i: (i, 0))
    row_shape = jax.ShapeDtypeStruct((1, d), F32)
    return pl.pallas_call(
        body, grid=(steps,), in_specs=[tile, tile, tile, row, row, row, tile, _resident(w_out.shape)],
        out_specs=[tile, row, row, row, mix_tile, tile, row],
        out_shape=[jax.ShapeDtypeStruct((s, d), F32), row_shape, row_shape, row_shape,
                   jax.ShapeDtypeStruct((s, w_out.shape[0]), F32), jax.ShapeDtypeStruct((s, d), BF16), row_shape],
        scratch_shapes=[pltpu.VMEM((1, d), F32), pltpu.VMEM((1, d), F32)], name=name,
        compiler_params=_cparams("arbitrary"))(x1, dh2, dx2, norm_g, sc, gate, mo, w_out)


def _final_loss(x, g, target, *, name, tm=ROW_TILE):
    s, d = x.shape
    steps = s // tm

    def body(x_ref, g_ref, t_ref, loss_ref, dx_ref, dg_ref, sq_acc):
        i = pl.program_id(0)

        @pl.when(i == 0)
        def _():
            sq_acc[...] = jnp.zeros_like(sq_acc)
            dg_ref[...] = jnp.zeros_like(dg_ref)

        xv = x_ref[...]
        r = lax.rsqrt(jnp.mean(xv * xv, axis=-1, keepdims=True) + NORM_EPS)
        xhat = xv * r
        err = xhat * g_ref[...] - t_ref[...]
        sq_acc[...] += jnp.sum(err * err, axis=0, keepdims=True)
        dy = err * (1.0 / d)
        dg_ref[...] += jnp.sum(dy * xhat, axis=0, keepdims=True)
        dxhat = dy * g_ref[...]
        dx_ref[...] = r * (dxhat - xhat * jnp.mean(dxhat * xhat, axis=-1, keepdims=True))

        @pl.when(i == steps - 1)
        def _():
            total = jnp.sum(sq_acc[...], axis=1, keepdims=True) * (0.5 / d)
            loss_ref[...] = jnp.broadcast_to(total, loss_ref.shape)

    tile = pl.BlockSpec((tm, d), lambda i: (i, 0))
    row = pl.BlockSpec((1, d), lambda i: (0, 0))
    return pl.pallas_call(
        body, grid=(steps,), in_specs=[tile, row, tile],
        out_specs=[pl.BlockSpec((1, 128), lambda i: (0, 0)), tile, row],
        out_shape=[jax.ShapeDtypeStruct((1, 128), F32), jax.ShapeDtypeStruct((s, d), F32),
                   jax.ShapeDtypeStruct((1, d), F32)],
        scratch_shapes=[pltpu.VMEM((1, d), F32)], name=name, compiler_params=_cparams("arbitrary"))(x, g, target)


def _ssd_chunk(z, xbc_ext, dt_raw, conv_w, conv_b, dt_bias, a_log, d_skip, norm_g, h_in):
    q = z.shape[0]
    gw = SSD_HPG * HEAD_LANES
    xc = conv_b
    for k in range(SSD_CONV_K):
        xc = xc + _pick_row(conv_w, k) * _shift_rows(xbc_ext, SSD_CONV_K - 1 - k, CONV_HALO)
    xc = _silu(xc)
    dt = jax.nn.softplus(dt_raw + dt_bias)
    da = dt * (-jnp.exp(a_log))
    ri = lax.broadcasted_iota(jnp.int32, (q, q), 0)
    ci = lax.broadcasted_iota(jnp.int32, (q, q), 1)
    causal = ri >= ci
    tril = causal.astype(F32)
    a_cum = jnp.dot(tril, da, preferred_element_type=F32, precision=lax.Precision.HIGHEST)
    a_cum_t = lax.dot_general(da, tril, (((0,), (1,)), ((), ())), preferred_element_type=F32,
                              precision=lax.Precision.HIGHEST)
    a_last = _pick_row(a_cum, q - 1)
    head = _head_of_lane(gw)
    ys, hs = [], []
    for g in range(2):
        xs = xc[:, gw * g:gw * (g + 1)]
        bm = xc[:, SSD_INNER + SSD_STATE * g:SSD_INNER + SSD_STATE * (g + 1)]
        cm = xc[:, SSD_INNER + 2 * SSD_STATE + SSD_STATE * g:SSD_INNER + 2 * SSD_STATE + SSD_STATE * (g + 1)]
        cb = lax.dot_general(cm.astype(BF16), bm.astype(BF16), (((1,), (1,)), ((), ())), preferred_element_type=F32)
        cols = [_pick_lane(a_cum, SSD_HPG * g + j) for j in range(SSD_HPG)]
        lasts = [_pick_lane(a_last, SSD_HPG * g + j) for j in range(SSD_HPG)]
        dt_exp = sum(jnp.where(head == j, _pick_lane(dt, SSD_HPG * g + j), 0.0) for j in range(SSD_HPG))
        d_exp = sum(jnp.where(head == j, _pick_lane(d_skip, SSD_HPG * g + j), 0.0) for j in range(SSD_HPG))
        e_cum = sum(jnp.where(head == j, jnp.exp(cols[j]), 0.0) for j in range(SSD_HPG))
        c_dec = sum(jnp.where(head == j, jnp.exp(lasts[j]), 0.0) for j in range(SSD_HPG))
        xsdt = (xs * dt_exp).astype(BF16)
        y_diag = jnp.zeros((q, gw), F32)
        st_new = jnp.zeros((SSD_STATE, gw), F32)
        for j in range(SSD_HPG):
            row = _pick_row(a_cum_t, SSD_HPG * g + j)
            lmat = jnp.exp(jnp.where(causal, cols[j] - row, -jnp.inf))
            r = jnp.dot((cb * lmat).astype(BF16), xsdt, preferred_element_type=F32)
            y_diag = y_diag + jnp.where(head == j, r, 0.0)
            bd = (bm * jnp.exp(lasts[j] - cols[j])).astype(BF16)
            st = lax.dot_general(bd, xsdt, (((0,), (0,)), ((), ())), preferred_element_type=F32)
            st_new = st_new + jnp.where(head == j, st, 0.0)
        y_off = jnp.dot(cm.astype(BF16), h_in[g].astype(BF16), preferred_element_type=F32) * e_cum
        hs.append(h_in[g] * c_dec + st_new)
        y = y_diag + y_off + d_exp * xs
        yz = y * _silu(z[:, gw * g:gw * (g + 1)])
        yz = yz * lax.rsqrt(jnp.mean(yz * yz, axis=-1, keepdims=True) + NORM_EPS)
        ys.append(yz * norm_g[:, gw * g:gw * (g + 1)])
    return jnp.concatenate(ys, axis=1), tuple(hs)


_SSD_NCHUNK = SEQ // SSD_CHUNK
_HALO_PER_CHUNK = SSD_CHUNK // CONV_HALO


def _ssd_param_specs(const):
    return [pl.BlockSpec((8, SSD_CONV_CH), const), pl.BlockSpec((1, SSD_CONV_CH), const),
            pl.BlockSpec((1, 128), const), pl.BlockSpec((1, 128), const), pl.BlockSpec((1, 128), const),
            pl.BlockSpec((1, SSD_INNER), const)]


def _ssd_fwd(proj, conv_w, conv_b, dt_bias, a_log, d_skip, norm_g, *, name):
    q = SSD_CHUNK

    def body(z_ref, xbc_ref, halo_ref, dt_ref, cw_ref, cb_ref, db_ref, al_ref, d_ref, ng_ref, y_ref, hs_ref, h_acc):
        i = pl.program_id(0)

        @pl.when(i == 0)
        def _():
            h_acc[...] = jnp.zeros_like(h_acc)

        halo = jnp.where(i == 0, 0.0, halo_ref[...])
        xbc_ext = jnp.concatenate([halo, xbc_ref[...]], axis=0)
        h_in = (h_acc[0], h_acc[1])
        hs_ref[0, 0] = h_in[0]
        hs_ref[0, 1] = h_in[1]
        y, h_out = _ssd_chunk(z_ref[...], xbc_ext, dt_ref[...], cw_ref[...], cb_ref[...], db_ref[...], al_ref[...],
                              d_ref[...], ng_ref[...], h_in)
        y_ref[...] = y.astype(BF16)
        h_acc[0] = h_out[0]
        h_acc[1] = h_out[1]

    const = lambda i: (0, 0)
    return pl.pallas_call(
        body, grid=(_SSD_NCHUNK,),
        in_specs=[pl.BlockSpec((q, SSD_INNER), lambda i: (i, PROJ_Z_BLK)),
                  pl.BlockSpec((q, SSD_CONV_CH), lambda i: (i, 0)),
                  pl.BlockSpec((CONV_HALO, SSD_CONV_CH), lambda i: (jnp.maximum(i * _HALO_PER_CHUNK - 1, 0), 0)),
                  pl.BlockSpec((q, 128), lambda i: (i, PROJ_DT_BLK))] + _ssd_param_specs(const),
        out_specs=[pl.BlockSpec((q, SSD_INNER), lambda i: (i, 0)),
                   pl.BlockSpec((1, 2, SSD_STATE, 256), lambda i: (i, 0, 0, 0))],
        out_shape=[jax.ShapeDtypeStruct((SEQ, D_MODEL), BF16),
                   jax.ShapeDtypeStruct((_SSD_NCHUNK, 2, SSD_STATE, 256), F32)],
        scratch_shapes=[pltpu.VMEM((2, SSD_STATE, 256), F32)], name=name,
        compiler_params=_cparams("arbitrary"))(proj, proj, proj, proj, conv_w, conv_b, dt_bias, a_log, d_skip, norm_g)


def _ssd_bwd(proj, hstates, dmix, conv_w, conv_b, dt_bias, a_log, d_skip, norm_g, *, name):
    q = SSD_CHUNK
    last = _SSD_NCHUNK - 1

    def body(z_ref, xbc_ref, halo_ref, dt_ref, hs_ref, dy_ref, cw_ref, cb_ref, db_ref, al_ref, d_ref, ng_ref,
             dp_ref, dcw_ref, dcb_ref, ddb_ref, dal_ref, dd_ref, dng_ref, dh_acc, dhalo_acc):
        i = pl.program_id(0)

        @pl.when(i == 0)
        def _():
            dh_acc[...] = jnp.zeros_like(dh_acc)
            dhalo_acc[...] = jnp.zeros_like(dhalo_acc)
            for r in (dcw_ref, dcb_ref, ddb_ref, dal_ref, dd_ref, dng_ref):
                r[...] = jnp.zeros_like(r)

        halo = jnp.where(i == last, 0.0, halo_ref[...])
        xbc_ext = jnp.concatenate([halo, xbc_ref[...]], axis=0)
        _, vjp = jax.vjp(_ssd_chunk, z_ref[...], xbc_ext, dt_ref[...], cw_ref[...], cb_ref[...], db_ref[...],
                         al_ref[...], d_ref[...], ng_ref[...], (hs_ref[0, 0], hs_ref[0, 1]))
        gz, gx, gdt, gcw, gcb, gdb, gal, gd, gng, gh = vjp((dy_ref[...], (dh_acc[0], dh_acc[1])))
        dxbc = jnp.concatenate([gx[CONV_HALO:q], gx[q:] + dhalo_acc[...]], axis=0)
        dp_ref[...] = jnp.concatenate([dxbc, gz, gdt, jnp.zeros_like(gdt)], axis=1).astype(BF16)
        dhalo_acc[...] = gx[:CONV_HALO]
        dh_acc[0] = gh[0]
        dh_acc[1] = gh[1]
        dcw_ref[...] += gcw
        dcb_ref[...] += gcb
        ddb_ref[...] += gdb
        dal_ref[...] += gal
        dd_ref[...] += gd
        dng_ref[...] += gng

    const = lambda i: (0, 0)
    rev = lambda i: last - i
    row = lambda n: jax.ShapeDtypeStruct((1, n), F32)
    return pl.pallas_call(
        body, grid=(_SSD_NCHUNK,),
        in_specs=[pl.BlockSpec((q, SSD_INNER), lambda i: (rev(i), PROJ_Z_BLK)),
                  pl.BlockSpec((q, SSD_CONV_CH), lambda i: (rev(i), 0)),
                  pl.BlockSpec((CONV_HALO, SSD_CONV_CH), lambda i: (jnp.maximum(rev(i) * _HALO_PER_CHUNK - 1, 0), 0)),
                  pl.BlockSpec((q, 128), lambda i: (rev(i), PROJ_DT_BLK)),
                  pl.BlockSpec((1, 2, SSD_STATE, 256), lambda i: (rev(i), 0, 0, 0)),
                  pl.BlockSpec((q, SSD_INNER), lambda i: (rev(i), 0))] + _ssd_param_specs(const),
        out_specs=[pl.BlockSpec((q, PROJ_SSD_W), lambda i: (rev(i), 0))] + _ssd_param_specs(const),
        out_shape=[jax.ShapeDtypeStruct((SEQ, PROJ_W), BF16), jax.ShapeDtypeStruct((8, SSD_CONV_CH), F32),
                   row(SSD_CONV_CH), row(128), row(128), row(128), row(SSD_INNER)],
        scratch_shapes=[pltpu.VMEM((2, SSD_STATE, 256), F32), pltpu.VMEM((CONV_HALO, SSD_CONV_CH), F32)], name=name,
        compiler_params=_cparams("arbitrary"))(proj, proj, proj, proj, hstates, dmix, conv_w, conv_b, dt_bias, a_log,
                                                d_skip, norm_g)


def _rope_tables(pos_col, inv_freq_lane, *, name):
    s = pos_col.shape[0]

    def body(p_ref, f_ref, c_ref, s1_ref, s2_ref):
        ang = p_ref[...] * f_ref[...]
        within = lax.broadcasted_iota(jnp.int32, ang.shape, 1) % HEAD_LANES
        half = ROT_DIM // 2
        c_ref[...] = jnp.where(within < ROT_DIM, jnp.cos(ang), 1.0)
        sn = jnp.sin(ang)
        s1_ref[...] = jnp.where(within < half, -sn, 0.0)
        s2_ref[...] = jnp.where((within >= half) & (within < ROT_DIM), sn, 0.0)

    shp = jax.ShapeDtypeStruct((s, 128), F32)
    return pl.pallas_call(body, out_shape=[shp, shp, shp], name=name,
                          compiler_params=pltpu.CompilerParams(vmem_limit_bytes=VMEM_LIMIT))(pos_col, inv_freq_lane)


def _rope(t, c, s1, s2):
    half = ROT_DIM // 2
    return t * c + pltpu.roll(t, 128 - half, 1) * s1 + pltpu.roll(t, half, 1) * s2


def _rope_t(g, c, s1, s2):
    half = ROT_DIM // 2
    return g * c + pltpu.roll(g * s1, half, 1) + pltpu.roll(g * s2, 128 - half, 1)


def _att_valid(b):
    qi = lax.broadcasted_iota(jnp.int32, (ATT_BLOCK, 2 * ATT_BLOCK), 0)
    kj = lax.broadcasted_iota(jnp.int32, (ATT_BLOCK, 2 * ATT_BLOCK), 1)
    rel = qi + ATT_BLOCK - kj
    return (rel >= 0) & (rel <= ATT_BLOCK) & (b * ATT_BLOCK + kj - ATT_BLOCK >= 0)


def _att_slices(i, d):
    if d == 1:
        qstart = pl.multiple_of(i * ATT_BLOCK, ATT_BLOCK)
        return i, pl.ds(qstart, ATT_BLOCK), pl.ds(pl.multiple_of(qstart - ATT_BLOCK + ATT_KPAD, ATT_BLOCK), 2 * ATT_BLOCK)
    r = i % d
    b = i // d
    qstart = r + d * ATT_BLOCK * b
    return b, pl.ds(qstart, ATT_BLOCK, stride=d), pl.ds(qstart - ATT_BLOCK * d + ATT_KPAD, 2 * ATT_BLOCK, stride=d)


_ATT_NBLK = SEQ // ATT_BLOCK
_ATT_SCALE = HEAD_LANES ** -0.5
_ATT_UNROLL_FWD = 8
_ATT_UNROLL = 4


def _att_fwd(proj, cos, sin1, sin2, mix, *, name):
    s = SEQ

    def body(q_ref, k_ref, v_ref, c_ref, s1_ref, s2_ref, _, o_ref, lse_ref, mix_ref, qs, ks, vs, acc, m_s, l_s):
        c, s1, s2 = c_ref[...], s1_ref[...], s2_ref[...]
        qs[...] = _rope(q_ref[...], c, s1, s2) * _ATT_SCALE
        zeros = jnp.zeros((ATT_KPAD, 128), F32)
        ks[pl.ds(0, ATT_KPAD), :] = zeros
        vs[pl.ds(0, ATT_KPAD), :] = zeros
        ks[pl.ds(ATT_KPAD, s), :] = _rope(k_ref[...], c, s1, s2)
        vs[pl.ds(ATT_KPAD, s), :] = v_ref[...]
        head0 = _head_of_lane(128) == 0

        for bi, (_, d) in enumerate(ATT_PATTERNS):
            def blk(i, carry, d=d, first=(bi == 0)):
                b, sq, sk = _att_slices(i, d)
                qb = qs[sq, :]
                kw = ks[sk, :].astype(BF16)
                vw = vs[sk, :].astype(BF16)
                valid = _att_valid(b)
                ms, ls, os_ = [], [], []
                for hh in range(2):
                    qh = jnp.where(head0 if hh == 0 else ~head0, qb, 0.0).astype(BF16)
                    sc = lax.dot_general(qh, kw, (((1,), (1,)), ((), ())), preferred_element_type=F32)
                    sc = jnp.where(valid, sc, -jnp.inf)
                    mb = jnp.max(sc, axis=1, keepdims=True)
                    p = jnp.exp(sc - mb)
                    ms.append(mb)
                    ls.append(jnp.sum(p, axis=1, keepdims=True))
                    os_.append(jnp.dot(p.astype(BF16), vw, preferred_element_type=F32))
                m_b = jnp.where(head0, ms[0], ms[1])
                l_b = jnp.where(head0, ls[0], ls[1])
                o_b = jnp.where(head0, os_[0], os_[1])
                if first:
                    m_s[sq, :] = m_b
                    l_s[sq, :] = l_b
                    acc[sq, :] = o_b
                else:
                    m_old = m_s[sq, :]
                    m_new = jnp.maximum(m_old, m_b)
                    a_old = jnp.exp(m_old - m_new)
                    a_b = jnp.exp(m_b - m_new)
                    m_s[sq, :] = m_new
                    l_s[sq, :] = l_s[sq, :] * a_old + l_b * a_b
                    acc[sq, :] = acc[sq, :] * a_old + o_b * a_b
                return carry

            lax.fori_loop(0, _ATT_NBLK, blk, 0, unroll=_ATT_UNROLL_FWD)

        out = acc[...] / l_s[...]
        o_ref[...] = out
        mix_ref[...] = out.astype(BF16)
        lse_ref[...] = m_s[...] + jnp.log(l_s[...])

    col = lambda base: pl.BlockSpec((s, 128), lambda p: (0, base + p))
    tab = pl.BlockSpec((s, 128), lambda p: (0, 0))
    big = pltpu.VMEM((ATT_KPAD + s, 128), F32)
    tok = pltpu.VMEM((s, 128), F32)
    return pl.pallas_call(
        body, grid=(2,), in_specs=[col(PROJ_Q_BLK), col(PROJ_K_BLK), col(PROJ_V_BLK), tab, tab, tab, _HBM],
        out_specs=[pl.BlockSpec((s, 128), lambda p: (0, p)), pl.BlockSpec((s, 128), lambda p: (0, p)),
                   col(MIX_ATT_BLK)],
        out_shape=[jax.ShapeDtypeStruct((s, ATT_W), F32), jax.ShapeDtypeStruct((s, ATT_W), F32),
                   jax.ShapeDtypeStruct(mix.shape, mix.dtype)],
        input_output_aliases={6: 2}, scratch_shapes=[tok, big, big, tok, tok, tok], name=name,
        compiler_params=_cparams("arbitrary"))(proj, proj, proj, cos, sin1, sin2, mix)


def _att_bwd(proj, cos, sin1, sin2, out, lse, dmix, dproj, *, name):
    s = SEQ

    def body(proj_ref, c_hbm, s1_hbm, s2_hbm, out_hbm, lse_hbm, dmix_hbm, _, dproj_hbm,
             c_ref, s1_ref, s2_ref, o_ref, lse_ref, do_ref, qs, ks, vs, dqs, dks, dvs, staged, sems):
        def start(copies):
            for cp in copies:
                cp.start()
            return copies

        def load(pair):
            lanes = pl.ds(128 * pair, 128)
            rows = pl.ds(ATT_KPAD, s)
            return start([
                pltpu.make_async_copy(proj_ref.at[:, pl.ds(128 * (PROJ_Q_BLK + pair), 128)], qs, sems.at[0]),
                pltpu.make_async_copy(proj_ref.at[:, pl.ds(128 * (PROJ_K_BLK + pair), 128)], ks.at[rows, :], sems.at[1]),
                pltpu.make_async_copy(proj_ref.at[:, pl.ds(128 * (PROJ_V_BLK + pair), 128)], vs.at[rows, :], sems.at[2]),
                pltpu.make_async_copy(out_hbm.at[:, lanes], o_ref, sems.at[3]),
                pltpu.make_async_copy(lse_hbm.at[:, lanes], lse_ref, sems.at[4]),
                pltpu.make_async_copy(dmix_hbm.at[:, pl.ds(128 * (MIX_ATT_BLK + pair), 128)], do_ref, sems.at[5])])

        tables = start([pltpu.make_async_copy(c_hbm, c_ref, sems.at[6]),
                        pltpu.make_async_copy(s1_hbm, s1_ref, sems.at[7]),
                        pltpu.make_async_copy(s2_hbm, s2_ref, sems.at[8])])
        loads = load(0)
        for cp in tables:
            cp.wait()
        head0 = _head_of_lane(128) == 0
        zeros = jnp.zeros((ATT_KPAD, 128), F32)
        for pair in range(2):
            for cp in loads:
                cp.wait()
            c, s1, s2 = c_ref[...], s1_ref[...], s2_ref[...]
            qs[...] = _rope(qs[...], c, s1, s2) * _ATT_SCALE
            ks[pl.ds(0, ATT_KPAD), :] = zeros
            vs[pl.ds(0, ATT_KPAD), :] = zeros
            ks[pl.ds(ATT_KPAD, s), :] = _rope(ks[pl.ds(ATT_KPAD, s), :], c, s1, s2)
            dqs[...] = jnp.zeros_like(dqs)
            dks[...] = jnp.zeros_like(dks)
            dvs[...] = jnp.zeros_like(dvs)

            for _, d in ATT_PATTERNS:
                def blk(i, carry, d=d):
                    b, sq, sk = _att_slices(i, d)
                    qb = qs[sq, :]
                    kw = ks[sk, :].astype(BF16)
                    vw = vs[sk, :].astype(BF16)
                    dob = do_ref[sq, :]
                    lse_b = lse_ref[sq, :]
                    dd = dob * o_ref[sq, :]
                    valid = _att_valid(b)
                    dq_b = jnp.zeros((ATT_BLOCK, 128), F32)
                    dk_w = jnp.zeros((2 * ATT_BLOCK, 128), F32)
                    dv_w = jnp.zeros((2 * ATT_BLOCK, 128), F32)
                    for hh in range(2):
                        hm = head0 if hh == 0 else ~head0
                        qh = jnp.where(hm, qb, 0.0).astype(BF16)
                        doh = jnp.where(hm, dob, 0.0).astype(BF16)
                        lse_h = _pick_lane(lse_b, hh * HEAD_LANES)
                        d_h = jnp.sum(jnp.where(hm, dd, 0.0), axis=1, keepdims=True)
                        sc = lax.dot_general(qh, kw, (((1,), (1,)), ((), ())), preferred_element_type=F32)
                        p = jnp.where(valid, jnp.exp(sc - lse_h), 0.0)
                        dp = lax.dot_general(doh, vw, (((1,), (1,)), ((), ())), preferred_element_type=F32)
                        ds = (p * (dp - d_h)).astype(BF16)
                        dq_b = dq_b + jnp.where(hm, jnp.dot(ds, kw, preferred_element_type=F32), 0.0)
                        dk_w = dk_w + lax.dot_general(ds, qh, (((0,), (0,)), ((), ())), preferred_element_type=F32)
                        dv_w = dv_w + lax.dot_general(p.astype(BF16), doh, (((0,), (0,)), ((), ())),
                                                      preferred_element_type=F32)
                    dqs[sq, :] += dq_b
                    dks[sk, :] += dk_w
                    dvs[sk, :] += dv_w
                    return carry

                lax.fori_loop(0, _ATT_NBLK, blk, 0, unroll=_ATT_UNROLL)

            staged[0] = _rope_t(dqs[...] * _ATT_SCALE, c, s1, s2).astype(BF16)
            staged[1] = _rope_t(dks[pl.ds(ATT_KPAD, s), :], c, s1, s2).astype(BF16)
            staged[2] = dvs[pl.ds(ATT_KPAD, s), :].astype(BF16)
            stores = start([
                pltpu.make_async_copy(staged.at[j], dproj_hbm.at[:, pl.ds(128 * (col + pair), 128)], sems.at[9 + j])
                for j, col in enumerate((PROJ_Q_BLK, PROJ_K_BLK, PROJ_V_BLK))])
            if pair == 0:
                loads = load(1)
            for cp in stores:
                cp.wait()

    big = pltpu.VMEM((ATT_KPAD + s, 128), F32)
    tok = pltpu.VMEM((s, 128), F32)
    return pl.pallas_call(
        body, in_specs=[_HBM] * 8, out_specs=_HBM, out_shape=jax.ShapeDtypeStruct(dproj.shape, dproj.dtype),
        input_output_aliases={7: 0},
        scratch_shapes=[tok] * 6 + [tok, big, big, tok, big, big, pltpu.VMEM((3, s, 128), BF16),
                                    pltpu.SemaphoreType.DMA((12,))], name=name,
        compiler_params=pltpu.CompilerParams(vmem_limit_bytes=VMEM_LIMIT))(
            proj, cos, sin1, sin2, out, lse, dmix, dproj)


_POOL_TM = 512
_POOL_NT = SEQ // _POOL_TM
_POOL_HALO_PER_TILE = _POOL_TM // POOL_HALO


def _pool_tile(u_ext, w_bd, scale, t0):
    s2 = u_ext + _roll_rows(u_ext, 1)
    s4 = s2 + _roll_rows(s2, 2)
    s8 = s4 + _roll_rows(s4, 4)
    s16 = s8 + _roll_rows(s8, 8)
    grp = _head_of_lane(POOL_W)
    sel = jnp.where(grp == 0, s2, jnp.where(grp == 1, s4, jnp.where(grp == 2, s8, s16)))[POOL_HALO:]
    t = sel.shape[0]
    pos = t0 + lax.broadcasted_iota(jnp.int32, (t, POOL_W), 0) + 1
    win = jnp.where(grp == 0, 2, jnp.where(grp == 1, 4, jnp.where(grp == 2, 8, 16)))
    cnt = jnp.minimum(pos, win).astype(F32)
    diff = sel / cnt - u_ext[POOL_HALO:]
    return jnp.dot(diff.astype(BF16), w_bd.astype(BF16), preferred_element_type=F32) * scale


def _pool_fwd(proj, w_bd, scale, mix, *, name):
    tm = _POOL_TM

    def body(u_ref, halo_ref, w_ref, sc_ref, _, y_ref):
        i = pl.program_id(0)
        halo = jnp.where(i == 0, 0.0, halo_ref[...])
        u_ext = jnp.concatenate([halo, u_ref[...]], axis=0)
        y_ref[...] = _pool_tile(u_ext, w_ref[...], sc_ref[...], i * tm).astype(BF16)

    return pl.pallas_call(
        body, grid=(_POOL_NT,),
        in_specs=[pl.BlockSpec((tm, POOL_W), lambda i: (i, PROJ_POOL_BLK)),
                  pl.BlockSpec((POOL_HALO, POOL_W),
                               lambda i: (jnp.maximum(i * _POOL_HALO_PER_TILE - 1, 0), PROJ_POOL_BLK)),
                  pl.BlockSpec((POOL_W, POOL_W), lambda i: (0, 0)), pl.BlockSpec((1, POOL_W), lambda i: (0, 0)), _HBM],
        out_specs=pl.BlockSpec((tm, POOL_W), lambda i: (i, MIX_POOL_BLK)),
        out_shape=jax.ShapeDtypeStruct(mix.shape, mix.dtype), input_output_aliases={4: 0}, name=name,
        compiler_params=_cparams("parallel"))(proj, proj, w_bd, scale, mix)


def _pool_bwd(proj, dmix, w_bd, scale, dproj, *, name):
    tm = _POOL_TM
    last = _POOL_NT - 1

    def body(u_ref, halo_ref, dy_ref, w_ref, sc_ref, _, du_ref, dw_ref, dsc_ref, dhalo_acc):
        i = pl.program_id(0)

        @pl.when(i == 0)
        def _():
            dhalo_acc[...] = jnp.zeros_like(dhalo_acc)
            dw_ref[...] = jnp.zeros_like(dw_ref)
            dsc_ref[...] = jnp.zeros_like(dsc_ref)

        tile = last - i
        halo = jnp.where(tile == 0, 0.0, halo_ref[...])
        u_ext = jnp.concatenate([halo, u_ref[...]], axis=0)
        _, vjp = jax.vjp(functools.partial(_pool_tile, t0=tile * tm), u_ext, w_ref[...], sc_ref[...])
        gu, gw, gs = vjp(dy_ref[...])
        du_ref[...] = jnp.concatenate([gu[POOL_HALO:tm], gu[tm:] + dhalo_acc[...]], axis=0).astype(BF16)
        dhalo_acc[...] = gu[:POOL_HALO]
        dw_ref[...] += gw
        dsc_ref[...] += gs

    rev = lambda i: last - i
    return pl.pallas_call(
        body, grid=(_POOL_NT,),
        in_specs=[pl.BlockSpec((tm, POOL_W), lambda i: (rev(i), PROJ_POOL_BLK)),
                  pl.BlockSpec((POOL_HALO, POOL_W),
                               lambda i: (jnp.maximum(rev(i) * _POOL_HALO_PER_TILE - 1, 0), PROJ_POOL_BLK)),
                  pl.BlockSpec((tm, POOL_W), lambda i: (rev(i), MIX_POOL_BLK)),
                  pl.BlockSpec((POOL_W, POOL_W), lambda i: (0, 0)), pl.BlockSpec((1, POOL_W), lambda i: (0, 0)), _HBM],
        out_specs=[pl.BlockSpec((tm, POOL_W), lambda i: (rev(i), PROJ_POOL_BLK)),
                   pl.BlockSpec((POOL_W, POOL_W), lambda i: (0, 0)), pl.BlockSpec((1, POOL_W), lambda i: (0, 0))],
        out_shape=[jax.ShapeDtypeStruct(dproj.shape, dproj.dtype), jax.ShapeDtypeStruct((POOL_W, POOL_W), F32),
                   jax.ShapeDtypeStruct((1, POOL_W), F32)],
        input_output_aliases={5: 0}, scratch_shapes=[pltpu.VMEM((POOL_HALO, POOL_W), F32)], name=name,
        compiler_params=_cparams("arbitrary"))(proj, proj, dmix, w_bd, scale, dproj)


_FFN_TM = 256
_FFN_NT = SEQ // _FFN_TM
_FFN_HALO_PER_TILE = _FFN_TM // CONV_HALO


def _ffn_act_tile(hid_ext, conv_w, conv_b):
    hc = conv_b
    for k in range(FFN_CONV_K):
        hc = hc + _pick_row(conv_w, k) * _shift_rows(hid_ext, FFN_CONV_K - 1 - k, CONV_HALO)
    return _silu(hc[:, :FFN_DIM]) * hc[:, FFN_DIM:]


def _ffn_fwd(x1, norm_g, sc, sh, gate, up_t, down, conv_w, conv_b, *, name):
    tm = _FFN_TM
    w = 2 * FFN_DIM
    d = D_MODEL

    def body(x_ref, ng_ref, sc_ref, sh_ref, g_ref, up_ref, dn_ref, cw_ref, cb_ref,
             h_ref, hid_ref, act_ref, f_ref, x2_ref, halo_acc):
        i = pl.program_id(0)
        h2 = _rms_modulate(x_ref[...], ng_ref[...], sc_ref[...], sh_ref[...]).astype(BF16)
        h_ref[...] = h2
        hid = lax.dot_general(h2, up_ref[...], (((1,), (1,)), ((), ())), preferred_element_type=F32)
        hid_ref[...] = hid
        halo = jnp.where(i == 0, 0.0, halo_acc[...])
        act = _ffn_act_tile(jnp.concatenate([halo, hid], axis=0), cw_ref[...], cb_ref[...]).astype(BF16)
        halo_acc[...] = hid[tm - CONV_HALO:]
        act_ref[...] = act
        f = jnp.dot(act, dn_ref[...], preferred_element_type=F32)
        f_ref[...] = f
        x2_ref[...] = x_ref[...] + g_ref[...] * f

    tile = lambda n: pl.BlockSpec((tm, n), lambda i: (i, 0))
    return pl.pallas_call(
        body, grid=(_FFN_NT,),
        in_specs=[tile(d)] + [_resident((1, d))] * 4 + [_resident((w, d)), _resident((FFN_DIM, d)),
                                                        _resident((8, w)), _resident((1, w))],
        out_specs=[tile(d), tile(w), tile(FFN_DIM), tile(d), tile(d)],
        out_shape=[jax.ShapeDtypeStruct((SEQ, d), BF16), jax.ShapeDtypeStruct((SEQ, w), F32),
                   jax.ShapeDtypeStruct((SEQ, FFN_DIM), BF16), jax.ShapeDtypeStruct((SEQ, d), F32),
                   jax.ShapeDtypeStruct((SEQ, d), F32)],
        scratch_shapes=[pltpu.VMEM((CONV_HALO, w), F32)], name=name,
        compiler_params=_cparams("arbitrary"))(x1, norm_g, sc, sh, gate, up_t, down, conv_w, conv_b)


def _ffn_bwd(dx2, gate, f, hid, up_t, down, conv_w, conv_b, *, name):
    tm = _FFN_TM
    w = 2 * FFN_DIM
    d = D_MODEL
    last = _FFN_NT - 1

    def body(dx_ref, g_ref, f_ref, h_ref, halo_ref, up_ref, dn_ref, cw_ref, cb_ref,
             df_ref, dg_ref, dh_ref, dh2_ref, dcw_ref, dcb_ref, dhalo_acc):
        i = pl.program_id(0)

        @pl.when(i == 0)
        def _():
            dhalo_acc[...] = jnp.zeros_like(dhalo_acc)
            dcw_ref[...] = jnp.zeros_like(dcw_ref)
            dcb_ref[...] = jnp.zeros_like(dcb_ref)
            dg_ref[...] = jnp.zeros_like(dg_ref)

        dxv = dx_ref[...]
        df = (g_ref[...] * dxv).astype(BF16)
        df_ref[...] = df
        dg_ref[...] += jnp.sum(dxv * f_ref[...], axis=0, keepdims=True)
        dact = lax.dot_general(df, dn_ref[...], (((1,), (1,)), ((), ())), preferred_element_type=F32)
        halo = jnp.where(i == last, 0.0, halo_ref[...])
        hid_ext = jnp.concatenate([halo, h_ref[...]], axis=0)
        _, vjp = jax.vjp(_ffn_act_tile, hid_ext, cw_ref[...], cb_ref[...])
        gh, gw, gb = vjp(dact)
        dhid = jnp.concatenate([gh[CONV_HALO:tm], gh[tm:] + dhalo_acc[...]], axis=0).astype(BF16)
        dhalo_acc[...] = gh[:CONV_HALO]
        dh_ref[...] = dhid
        dh2_ref[...] = jnp.dot(dhid, up_ref[...], preferred_element_type=F32)
        dcw_ref[...] += gw
        dcb_ref[...] += gb

    rev = lambda i: last - i
    tile = lambda n: pl.BlockSpec((tm, n), lambda i: (rev(i), 0))
    acc = lambda shape: pl.BlockSpec(shape, lambda i: (0, 0))
    return pl.pallas_call(
        body, grid=(_FFN_NT,),
        in_specs=[tile(d), _resident((1, d)), tile(d), tile(w),
                  pl.BlockSpec((CONV_HALO, w), lambda i: (jnp.maximum(rev(i) * _FFN_HALO_PER_TILE - 1, 0), 0)),
                  _resident((w, d)), _resident((FFN_DIM, d)), _resident((8, w)), _resident((1, w))],
        out_specs=[tile(d), acc((1, d)), tile(w), tile(d), acc((8, w)), acc((1, w))],
        out_shape=[jax.ShapeDtypeStruct((SEQ, d), BF16), jax.ShapeDtypeStruct((1, d), F32),
                   jax.ShapeDtypeStruct((SEQ, w), BF16), jax.ShapeDtypeStruct((SEQ, d), F32),
                   jax.ShapeDtypeStruct((8, w), F32), jax.ShapeDtypeStruct((1, w), F32)],
        scratch_shapes=[pltpu.VMEM((CONV_HALO, w), F32)], name=name,
        compiler_params=_cparams("arbitrary"))(dx2, gate, f, hid, hid, up_t, down, conv_w, conv_b)


def _axes():
    return lax.axis_index("x"), lax.axis_index("y"), lax.axis_index("c")


def _handshake(peers):
    barrier = pltpu.get_barrier_semaphore()
    for peer in peers:
        pl.semaphore_signal(barrier, inc=1, device_id=peer, device_id_type=MESH)
    pl.semaphore_wait(barrier, len(peers))


def _allgather_body(x_refs, out_refs, send_sems, recv_sems, local_sems, own_barrier):
    n = len(x_refs)
    x, y, c = _axes()
    me, sibling = (x, y, c), (x, y, 1 - c)
    chips = [(1 - x, y), (x, 1 - y), (1 - x, 1 - y)]
    if own_barrier:
        _handshake([sibling] + [(*chip, c) for chip in chips])

    def slot(a, px, py, pc):
        return out_refs[a].at[4 * px + 2 * py + pc]

    def copy(a, k, block, to, src=None):
        return pltpu.make_async_remote_copy(
            src_ref=slot(a, *block) if src is None else src, dst_ref=slot(a, *block),
            send_sem=send_sems.at[a, k], recv_sem=recv_sems.at[a, k], device_id=to, device_id_type=MESH)

    mines, firsts = [], []
    for a in range(n):
        mines.append(pltpu.make_async_copy(x_refs[a], slot(a, *me), local_sems.at[a]))
        mines[-1].start()
        first = [copy(a, 0, me, sibling, src=x_refs[a])]
        first += [copy(a, 1 + j, me, (*chip, c), src=x_refs[a]) for j, chip in enumerate(chips)]
        for cp in first:
            cp.start()
        firsts += first
    passed = []
    for j, chip in enumerate(chips):
        for a in range(n):
            copy(a, 1 + j, (*chip, c), me).wait_recv()
            passed.append(copy(a, 4 + j, (*chip, c), sibling))
            passed[-1].start()
    for a in range(n):
        copy(a, 0, sibling, me).wait_recv()
    for j, chip in enumerate(chips):
        for a in range(n):
            copy(a, 4 + j, (*chip, 1 - c), me).wait_recv()
    for cp in firsts + passed:
        cp.wait_send()
    for cp in mines:
        cp.wait()


def _allgather_sems(n):
    return [pltpu.SemaphoreType.DMA((n, 7)), pltpu.SemaphoreType.DMA((n, 7)), pltpu.SemaphoreType.DMA((n,))]


def _allgather(xs, *, name):
    n = len(xs)

    def body(*refs):
        _allgather_body(refs[:n], refs[n:2 * n], *refs[2 * n:], own_barrier=False)

    return pl.pallas_call(
        body, out_shape=[jax.ShapeDtypeStruct((N_DEV,) + xb.shape, xb.dtype) for xb in xs],
        in_specs=[_HBM] * n, out_specs=[_HBM] * n, scratch_shapes=_allgather_sems(n), name=name)(*xs)


def _allgather_async(xs, *, name, collective_id):
    n = len(xs)
    x_refs = [jax.new_ref(xb, memory_space=pltpu.MemorySpace.HBM) for xb in xs]
    out_refs = [jax.empty_ref(jax.ShapeDtypeStruct((N_DEV,) + xb.shape, xb.dtype), memory_space=pltpu.MemorySpace.HBM)
                for xb in xs]

    @pl.kernel(mesh=plsc.ScalarSubcoreMesh(axis_name="sequencer", num_cores=1), name=name,
               scratch_types=tuple(_allgather_sems(n)),
               compiler_params=pltpu.CompilerParams(collective_id=collective_id))
    def launch(send_sems, recv_sems, local_sems):
        _allgather_body(x_refs, out_refs, send_sems, recv_sems, local_sems, own_barrier=True)

    launch()
    return [r[...] for r in out_refs]


def _pair_exchange(blocks, *, name, collective_id):
    n = len(blocks)
    hbm = pltpu.MemorySpace.HBM
    in_refs = [jax.new_ref(b, memory_space=hbm) for b in blocks]
    out_refs = [jax.empty_ref(jax.ShapeDtypeStruct((4,) + b.shape[1:], b.dtype), memory_space=hbm) for b in blocks]

    @pl.kernel(mesh=plsc.ScalarSubcoreMesh(axis_name="sequencer", num_cores=1), name=name,
               scratch_types=(pltpu.SemaphoreType.DMA((n, 4)), pltpu.SemaphoreType.DMA((n, 4))),
               compiler_params=pltpu.CompilerParams(collective_id=collective_id))
    def launch(send_sems, recv_sems):
        x, y, c = _axes()
        _handshake([(x, y, 1 - c)])
        copies = [pltpu.make_async_remote_copy(
            src_ref=in_refs[a].at[2 * s + (1 - c)], dst_ref=out_refs[a].at[s], send_sem=send_sems.at[a, s],
            recv_sem=recv_sems.at[a, s], device_id=(x, y, 1 - c), device_id_type=MESH)
            for a in range(n) for s in range(4)]
        for cp in copies:
            cp.start()
        for cp in copies:
            cp.wait_recv()
        for cp in copies:
            cp.wait_send()

    launch()
    return [r[...] for r in out_refs]


def _chip_exchange(parts, *, name, collective_id):
    n = len(parts)
    hbm = pltpu.MemorySpace.HBM
    in_refs = [jax.new_ref(p, memory_space=hbm) for p in parts]
    out_refs = [jax.empty_ref(jax.ShapeDtypeStruct(p.shape, p.dtype), memory_space=hbm) for p in parts]

    @pl.kernel(mesh=plsc.ScalarSubcoreMesh(axis_name="sequencer", num_cores=1), name=name,
               scratch_types=(pltpu.SemaphoreType.DMA((n, 3)), pltpu.SemaphoreType.DMA((n, 3)),
                              pltpu.SemaphoreType.DMA((n,))),
               compiler_params=pltpu.CompilerParams(collective_id=collective_id))
    def launch(send_sems, recv_sems, local_sems):
        x, y, c = _axes()
        my_chip = 2 * x + y
        chips = [(1 - x, y), (x, 1 - y), (1 - x, 1 - y)]
        _handshake([(*chip, c) for chip in chips])
        locals_ = [pltpu.make_async_copy(in_refs[a].at[my_chip], out_refs[a].at[my_chip], local_sems.at[a])
                   for a in range(n)]
        for cp in locals_:
            cp.start()
        copies = [pltpu.make_async_remote_copy(
            src_ref=in_refs[a].at[2 * px + py], dst_ref=out_refs[a].at[my_chip], send_sem=send_sems.at[a, k],
            recv_sem=recv_sems.at[a, k], device_id=(px, py, c), device_id_type=MESH)
            for a in range(n) for k, (px, py) in enumerate(chips)]
        for cp in copies:
            cp.start()
        for cp in copies:
            cp.wait_recv()
        for cp in copies:
            cp.wait_send()
        for cp in locals_:
            cp.wait()

    launch()
    return [r[...] for r in out_refs]


def _pair_sum(core, blocks, from_sibling, *, name):
    n = len(blocks)

    def body(core_ref, *refs):
        for a_ref, b_ref, o_ref in zip(refs[:n], refs[n:2 * n], refs[2 * n:]):
            o_ref[...] = (a_ref[...].astype(F32) + b_ref[...].astype(F32)).astype(o_ref.dtype)

    mine = lambda b: pl.BlockSpec((1,) + b.shape[1:], lambda s, core_ref: (2 * s + core_ref[0], 0, 0))
    slot = lambda b: pl.BlockSpec((1,) + b.shape[1:], lambda s, core_ref: (s, 0, 0))
    return pl.pallas_call(
        body,
        grid_spec=pltpu.PrefetchScalarGridSpec(
            num_scalar_prefetch=1, grid=(4,),
            in_specs=[mine(b) for b in blocks] + [slot(b) for b in blocks], out_specs=[slot(b) for b in blocks]),
        out_shape=[jax.ShapeDtypeStruct(s.shape, s.dtype) for s in from_sibling], name=name,
        compiler_params=_cparams("parallel"))(core, *blocks, *from_sibling)


def _sum_blocks(a, *, name, tr=None):
    n, r, cdim = a.shape
    tr = tr or r

    def body(a_ref, o_ref):
        acc = a_ref[0].astype(F32)
        for k in range(1, n):
            acc = acc + a_ref[k].astype(F32)
        o_ref[...] = acc

    return pl.pallas_call(body, grid=(r // tr,), in_specs=[pl.BlockSpec((n, tr, cdim), lambda i: (0, i, 0))],
                          out_specs=pl.BlockSpec((tr, cdim), lambda i: (i, 0)),
                          out_shape=jax.ShapeDtypeStruct((r, cdim), F32), name=name,
                          compiler_params=_cparams("parallel"))(a)


def _sum_gathered(gathered, *, name):
    n = len(gathered)

    def body(*refs):
        for a_ref, o_ref in zip(refs[:n], refs[n:]):
            acc = a_ref[0]
            for k in range(1, N_DEV):
                acc = acc + a_ref[k]
            o_ref[...] = acc

    return pl.pallas_call(body, out_shape=[jax.ShapeDtypeStruct(g.shape[1:], F32) for g in gathered], name=name,
                          compiler_params=pltpu.CompilerParams(vmem_limit_bytes=VMEM_LIMIT))(*gathered)


_ADA_SHARD = 6 * D_MODEL // N_DEV


def _ada_mod(c_all, ada_w, *, name):
    def body(c_ref, w_ref, o_ref):
        o_ref[0] = jnp.dot(_silu(c_ref[...]).astype(BF16), w_ref[0].astype(BF16), preferred_element_type=F32)

    return pl.pallas_call(
        body, grid=(DEPTH,),
        in_specs=[pl.BlockSpec((N_DEV, D_MODEL), lambda l: (0, 0)),
                  pl.BlockSpec((1, D_MODEL, _ADA_SHARD), lambda l: (l, 0, 0))],
        out_specs=pl.BlockSpec((1, N_DEV, _ADA_SHARD), lambda l: (l, 0, 0)),
        out_shape=jax.ShapeDtypeStruct((DEPTH, N_DEV, _ADA_SHARD), F32), name=name,
        compiler_params=_cparams("parallel"))(c_all, ada_w)


def _ada_wgrad(c_all, dmod_cols, *, name):
    def body(c_ref, d_ref, o_ref):
        o_ref[0] = lax.dot_general(_silu(c_ref[...]), d_ref[0], (((0,), (0,)), ((), ())),
                                   preferred_element_type=F32, precision=lax.Precision.HIGHEST)

    return pl.pallas_call(
        body, grid=(DEPTH,),
        in_specs=[pl.BlockSpec((N_DEV, D_MODEL), lambda l: (0, 0)),
                  pl.BlockSpec((1, N_DEV, _ADA_SHARD), lambda l: (l, 0, 0))],
        out_specs=pl.BlockSpec((1, D_MODEL, _ADA_SHARD), lambda l: (l, 0, 0)),
        out_shape=jax.ShapeDtypeStruct((DEPTH, D_MODEL, _ADA_SHARD), F32), name=name,
        compiler_params=_cparams("parallel"))(c_all, dmod_cols)


def _add_rows(a, b, *, name):
    def body(a_ref, b_ref, o_ref):
        o_ref[...] = a_ref[...] + b_ref[...]

    return pl.pallas_call(body, out_shape=jax.ShapeDtypeStruct(a.shape, a.dtype), name=name)(a, b)


def _adamw_update(w_ref, g_ref, m_ref, v_ref, d_ref, mo_ref, vo_ref):
    gv = g_ref[...]
    mn = ADAM_B1 * m_ref[...] + (1.0 - ADAM_B1) * gv
    vn = ADAM_B2 * v_ref[...] + (1.0 - ADAM_B2) * (gv * gv)
    mo_ref[...] = mn
    vo_ref[...] = vn
    m_hat = mn / (1.0 - ADAM_B1 ** ADAM_STEP)
    v_hat = vn / (1.0 - ADAM_B2 ** ADAM_STEP)
    d_ref[...] = -ADAM_LR * (m_hat / (jnp.sqrt(v_hat) + ADAM_EPS) + ADAM_WD * w_ref[...])


def _adamw_small(ws, gs, ms, vs, *, name):
    n = len(ws)

    def body(*refs):
        ins, outs = refs[:4 * n], refs[4 * n:]
        for i in range(n):
            _adamw_update(ins[i], ins[n + i], ins[2 * n + i], ins[3 * n + i], outs[i], outs[n + i], outs[2 * n + i])

    shapes = [jax.ShapeDtypeStruct(a.shape, F32) for a in ws]
    outs = pl.pallas_call(body, out_shape=shapes * 3, name=name,
                          compiler_params=pltpu.CompilerParams(vmem_limit_bytes=VMEM_LIMIT))(*ws, *gs, *ms, *vs)
    return outs[:n], outs[n:2 * n], outs[2 * n:]


def _adamw(w, g, m, v, *, name, tr):
    r, cdim = w.shape
    body = functools.partial(_adamw_update)

    spec = pl.BlockSpec((tr, cdim), lambda i: (i, 0))
    shp = jax.ShapeDtypeStruct((r, cdim), F32)
    return pl.pallas_call(body, grid=(r // tr,), in_specs=[spec] * 4, out_specs=[spec] * 3, out_shape=[shp] * 3,
                          name=name, compiler_params=_cparams("parallel"))(w, g, m, v)


def _pad_rows(a, rows):
    return jnp.concatenate([a, jnp.zeros((rows - a.shape[0],) + a.shape[1:], a.dtype)], axis=0)


def _pad_lanes(a, lanes):
    return jnp.concatenate([a, jnp.zeros(a.shape[:-1] + (lanes - a.shape[-1],), a.dtype)], axis=-1)


def _permute_w_in(wt):
    return jnp.concatenate([wt[512:1536], wt[:512], wt[1536:1544],
                            jnp.zeros((PROJ_W - IN_W, wt.shape[1]), wt.dtype), wt[1544:]], axis=0)


def _unpermute_w_in(wp):
    return jnp.concatenate([wp[1024:1536], wp[:1024], wp[1536:1544], wp[PROJ_SSD_W:]], axis=0)


def _block_diag(w):
    rows = []
    for g in range(4):
        rows.append(jnp.concatenate([w[g] if k == g else jnp.zeros_like(w[g]) for k in range(4)], axis=1))
    return jnp.concatenate(rows, axis=0)


def _diag_blocks(wbd):
    return jnp.stack([wbd[64 * g:64 * (g + 1), 64 * g:64 * (g + 1)] for g in range(4)], axis=0)


def _layer_params(l, small):
    return dict(
        norm1_g=small["norm1_g"][l][None], norm2_g=small["norm2_g"][l][None],
        conv_w=_pad_rows(small["ssd_conv_w"][l], 8), conv_b=small["ssd_conv_b"][l][None],
        dt_bias=_pad_lanes(small["ssd_dt_bias"][l][None], 128), a_log=_pad_lanes(small["ssd_a_log"][l][None], 128),
        d_skip=_pad_lanes(small["ssd_d"][l][None], 128), ssd_norm_g=small["ssd_norm_g"][l][None],
        pool_bd=_block_diag(small["pool_w"][l]), pool_scale=small["pool_scale"][l][None],
        fcw=_pad_rows(small["ffn_conv_w"][l], 8), fcb=small["ffn_conv_b"][l][None])


def _mod_rows(mod_l):
    return [mod_l[None, D_MODEL * i:D_MODEL * (i + 1)] for i in range(6)]


def _layer_fwd(x, mod_l, p, tabs, l, gather):
    sh1, sc1, g1, sh2, sc2, g2 = _mod_rows(mod_l)
    mix_w = gather(l, "mix", None)
    p.update(w_in=mix_w["w_in"], w_out=mix_w["w_out"])
    proj, h1 = _mm(x, p["w_in"], nt=True, norm=(p["norm1_g"], sc1, sh1), name=f"l{l}_proj")
    ffn_w = gather(l, "ffn", proj)
    p.update(up=ffn_w["ffn_up"], down=ffn_w["ffn_down"])
    mix, hst = _ssd_fwd(proj, p["conv_w"], p["conv_b"], p["dt_bias"], p["a_log"], p["d_skip"], p["ssd_norm_g"],
                        name=f"l{l}_ssd")
    mix = _pool_fwd(proj, p["pool_bd"], p["pool_scale"], mix, name=f"l{l}_pool")
    y_att, lse, mix = _att_fwd(proj, *tabs, mix, name=f"l{l}_att")
    mo, x1 = _mm(mix, p["w_out"], residual=(x, g1), name=f"l{l}_out")
    gather(l + 1, "mix", (x1, p["up"]))
    h2, hid, act, f, x2 = _ffn_fwd(x1, p["norm2_g"], sc2, sh2, g2, p["up"], p["down"], p["fcw"], p["fcb"],
                                   name=f"l{l}_ffn")
    return x2, dict(x=x, h1=h1, proj=proj, hst=hst, y_att=y_att, lse=lse, mix=mix, mo=mo, x1=x1, h2=h2, hid=hid,
                    act=act, f=f)


def _layer_bwd(dx2, sv, mod_l, p, tabs, l, exchange):
    sh1, sc1, g1, sh2, sc2, g2 = _mod_rows(mod_l)
    df, dg2, dhid, dh2, dfcw, dfcb = _ffn_bwd(dx2, g2, sv["f"], sv["hid"], p["up"], p["down"], p["fcw"], p["fcb"],
                                              name=f"l{l}_ffn_b")
    d_down = _wgrad(sv["act"], df, tk=1408, name=f"l{l}_down_bw")
    d_up = _wgrad(dhid, sv["h2"], tk=1408, name=f"l{l}_up_bw")
    exchange(l, "ffn", dict(ffn_up=d_up, ffn_down=d_down))
    dx1, dn2, dsc2, dsh2, dmix, dmo, dg1 = _mid_bwd(sv["x1"], dh2, dx2, p["norm2_g"], sc2, g1, sv["mo"], p["w_out"],
                                                    name=f"l{l}_mid_b")
    d_wout = _wgrad(sv["mix"], dmo, name=f"l{l}_out_bw")
    dproj, dcw, dcb, ddb, dal, dd, dng = _ssd_bwd(
        sv["proj"], sv["hst"], dmix, p["conv_w"], p["conv_b"], p["dt_bias"], p["a_log"], p["d_skip"],
        p["ssd_norm_g"], name=f"l{l}_ssd_b")
    dproj, dwbd, dpsc = _pool_bwd(sv["proj"], dmix, p["pool_bd"], p["pool_scale"], dproj, name=f"l{l}_pool_b")
    dproj = _att_bwd(sv["proj"], *tabs, sv["y_att"], sv["lse"], dmix, dproj, name=f"l{l}_att_b")
    d_win = _wgrad(dproj, sv["h1"], tk=1408, name=f"l{l}_proj_bw")
    exchange(l, "mix", dict(w_in=d_win, w_out=d_wout))
    dx0, dn1, dsc1, dsh1 = _norm_mod_bwd(sv["x"], dproj, dx1, p["norm1_g"], sc1, w=p["w_in"], name=f"l{l}_proj_b")
    dmod = jnp.concatenate([dsh1, dsc1, dg1, dsh2, dsc2, dg2], axis=1)[0]
    small = dict(norm1_g=dn1[0], ssd_conv_w=dcw[:SSD_CONV_K], ssd_conv_b=dcb[0], ssd_dt_bias=ddb[0], ssd_a_log=dal[0],
                 ssd_d=dd[0], ssd_norm_g=dng[0], pool_w=_diag_blocks(dwbd), pool_scale=dpsc[0], norm2_g=dn2[0],
                 ffn_conv_w=dfcw[:FFN_CONV_K], ffn_conv_b=dfcb[0])
    return dx0, dmod, small


def _example_step(x, target, pos_col, inv_freq_lane, mod, gather, small, final_g, exchange):
    tabs = _rope_tables(pos_col, inv_freq_lane, name="rope_tables")
    params, saved = [], []
    for l in range(DEPTH):
        params.append(_layer_params(l, small))
        x, sv = _layer_fwd(x, mod[l], params[l], tabs, l, gather)
        saved.append(sv)
    loss_row, dx, dfg = _final_loss(x, final_g[None], target, name="final_loss")
    dmods, smalls = [None] * DEPTH, [None] * DEPTH
    for l in reversed(range(DEPTH)):
        dx, dmods[l], smalls[l] = _layer_bwd(dx, saved[l], mod[l], params[l], tabs, l, exchange)
    return loss_row, dx, jnp.stack(dmods, axis=0), smalls, dfg[0]


_BIG = ("w_in", "w_out", "ffn_up", "ffn_down")
_SMALL_GRADS = ("norm1_g", "ssd_conv_w", "ssd_conv_b", "ssd_dt_bias", "ssd_a_log", "ssd_d", "ssd_norm_g", "pool_w",
                "pool_scale", "norm2_g", "ffn_conv_w", "ffn_conv_b")
_SMALL_PARAMS = ("ada_b", "norm1_g", "ssd_conv_w", "ssd_conv_b", "ssd_dt_bias", "ssd_a_log", "ssd_d", "ssd_norm_g",
                 "pool_w", "pool_scale", "norm2_g", "ffn_conv_w", "ffn_conv_b", "final_g")
_WEIGHT_ORDER = ("ada_w", "ada_b", "norm1_g", "w_in", "ssd_conv_w", "ssd_conv_b", "ssd_dt_bias", "ssd_a_log", "ssd_d",
                 "ssd_norm_g", "pool_w", "pool_scale", "w_out", "norm2_g", "ffn_up", "ffn_conv_w", "ffn_conv_b",
                 "ffn_down", "final_g")


_COLUMN_SHARDED = ("w_in", "ffn_up")
_GROUPS = (("mix", ("w_in", "w_out")), ("ffn", ("ffn_up", "ffn_down")))


def _big_shares(w, l, names):
    return [(w[name][l].T if name in _COLUMN_SHARDED else w[name][l]).astype(BF16) for name in names]


def _unshard_big(names, gathered):
    out = {}
    for name, g in zip(names, gathered):
        full = g.reshape(N_DEV * g.shape[1], g.shape[2])
        out[name] = _permute_w_in(full) if name == "w_in" else full
    return out


def _shard_big(grads):
    out = []
    for name, g in grads.items():
        g = _unpermute_w_in(g) if name == "w_in" else g
        out.append(g.reshape(N_DEV, g.shape[0] // N_DEV, g.shape[1]))
    return out


def kernel(x, c, positions, ada_w, ada_b, norm1_g, w_in, ssd_conv_w, ssd_conv_b, ssd_dt_bias, ssd_a_log, ssd_d, ssd_norm_g, pool_w, pool_scale, w_out, norm2_g, ffn_up, ffn_conv_w, ffn_conv_b, ffn_down, final_g, loss_target, m_ada_w, m_ada_b, m_norm1_g, m_w_in, m_ssd_conv_w, m_ssd_conv_b, m_ssd_dt_bias, m_ssd_a_log, m_ssd_d, m_ssd_norm_g, m_pool_w, m_pool_scale, m_w_out, m_norm2_g, m_ffn_up, m_ffn_conv_w, m_ffn_conv_b, m_ffn_down, m_final_g, v_ada_w, v_ada_b, v_norm1_g, v_w_in, v_ssd_conv_w, v_ssd_conv_b, v_ssd_dt_bias, v_ssd_a_log, v_ssd_d, v_ssd_norm_g, v_pool_w, v_pool_scale, v_w_out, v_norm2_g, v_ffn_up, v_ffn_conv_w, v_ffn_conv_b, v_ffn_down, v_final_g):
    w = dict(ada_w=ada_w, ada_b=ada_b, norm1_g=norm1_g, w_in=w_in, ssd_conv_w=ssd_conv_w, ssd_conv_b=ssd_conv_b,
             ssd_dt_bias=ssd_dt_bias, ssd_a_log=ssd_a_log, ssd_d=ssd_d, ssd_norm_g=ssd_norm_g, pool_w=pool_w,
             pool_scale=pool_scale, w_out=w_out, norm2_g=norm2_g, ffn_up=ffn_up, ffn_conv_w=ffn_conv_w,
             ffn_conv_b=ffn_conv_b, ffn_down=ffn_down, final_g=final_g)
    m = dict(ada_w=m_ada_w, ada_b=m_ada_b, norm1_g=m_norm1_g, w_in=m_w_in, ssd_conv_w=m_ssd_conv_w,
             ssd_conv_b=m_ssd_conv_b, ssd_dt_bias=m_ssd_dt_bias, ssd_a_log=m_ssd_a_log, ssd_d=m_ssd_d,
             ssd_norm_g=m_ssd_norm_g, pool_w=m_pool_w, pool_scale=m_pool_scale, w_out=m_w_out, norm2_g=m_norm2_g,
             ffn_up=m_ffn_up, ffn_conv_w=m_ffn_conv_w, ffn_conv_b=m_ffn_conv_b, ffn_down=m_ffn_down,
             final_g=m_final_g)
    v = dict(ada_w=v_ada_w, ada_b=v_ada_b, norm1_g=v_norm1_g, w_in=v_w_in, ssd_conv_w=v_ssd_conv_w,
             ssd_conv_b=v_ssd_conv_b, ssd_dt_bias=v_ssd_dt_bias, ssd_a_log=v_ssd_a_log, ssd_d=v_ssd_d,
             ssd_norm_g=v_ssd_norm_g, pool_w=v_pool_w, pool_scale=v_pool_scale, w_out=v_w_out, norm2_g=v_norm2_g,
             ffn_up=v_ffn_up, ffn_conv_w=v_ffn_conv_w, ffn_conv_b=v_ffn_conv_b, ffn_down=v_ffn_down,
             final_g=v_final_g)
    ix, iy, ic = _axes()
    dev = 4 * ix + 2 * iy + ic

    c_all, scw, fcw = _allgather([c, ssd_conv_w.reshape(DEPTH * SSD_CONV_K, -1),
                                  ffn_conv_w.reshape(DEPTH * FFN_CONV_K, -1)], name="gather_small")
    small_all = c_all
    c_all = c_all.reshape(N_DEV, D_MODEL)
    scw = scw.reshape(N_DEV, DEPTH, SSD_CONV_K, -1).transpose(1, 2, 0, 3).reshape(DEPTH, SSD_CONV_K, SSD_CONV_CH)
    fcw = fcw.reshape(N_DEV, DEPTH, FFN_CONV_K, -1).transpose(1, 2, 0, 3).reshape(DEPTH, FFN_CONV_K, 2 * FFN_DIM)

    mod_cols = _ada_mod(c_all, ada_w, name="ada_mod")
    mod_all = _allgather([mod_cols.reshape(DEPTH * N_DEV, _ADA_SHARD)], name="gather_mod")[0]
    mod_all = mod_all.reshape(N_DEV, DEPTH, N_DEV, _ADA_SHARD)
    mod_mine = lax.dynamic_index_in_dim(mod_all, dev, axis=2, keepdims=False)
    mod = _add_rows(mod_mine.transpose(1, 0, 2).reshape(DEPTH, 6 * D_MODEL), ada_b, name="ada_bias")

    fetched = {}

    def gather(l, group, after):
        if l < DEPTH and (l, group) not in fetched:
            names = dict(_GROUPS)[group]
            shares, _ = lax.optimization_barrier((_big_shares(w, l, names), small_all if after is None else after))
            got = _allgather_async(shares, name=f"gather_weights_l{l}_{group}",
                                   collective_id=1 + 2 * l + (group == "ffn"))
            fetched[l, group] = _unshard_big(names, got)
        return fetched.get((l, group))

    core = ic.astype(jnp.int32).reshape(1)
    from_chips = {}

    def exchange(l, group, g):
        cid = 5 + 4 * l + 2 * (group == "mix")
        blocks = _shard_big(g)
        from_sibling = _pair_exchange(blocks, name=f"grads_pair_exchange_l{l}_{group}", collective_id=cid)
        parts = _pair_sum(core, blocks, from_sibling, name=f"grads_pair_sum_l{l}_{group}")
        got = _chip_exchange(parts, name=f"grads_chip_exchange_l{l}_{group}", collective_id=cid + 1)
        from_chips.update({(l, name): t for name, t in zip(g, got)})

    small = dict(norm1_g=norm1_g, norm2_g=norm2_g, ssd_conv_w=scw, ssd_conv_b=ssd_conv_b, ssd_dt_bias=ssd_dt_bias,
                 ssd_a_log=ssd_a_log, ssd_d=ssd_d, ssd_norm_g=ssd_norm_g, pool_w=pool_w, pool_scale=pool_scale,
                 ffn_conv_w=fcw, ffn_conv_b=ffn_conv_b)

    inv_freq = ROPE_THETA ** (-jnp.arange(0, ROT_DIM, 2, dtype=F32) / ROT_DIM)
    lane = jnp.arange(128) % HEAD_LANES
    inv_freq_lane = jnp.where(lane < ROT_DIM, inv_freq[lane % (ROT_DIM // 2)], 0.0)[None, :]
    pos_col = positions.reshape(SEQ, 1).astype(F32)
    loss_row, dx, dmod, g_small, g_final = _example_step(
        x[0], loss_target[0], pos_col, inv_freq_lane, mod, gather, small, final_g, exchange)

    grads = {}
    for name in _BIG:
        per_layer = [_sum_blocks(from_chips[l, name], name=f"grads_chip_sum_l{l}_{name}") for l in range(DEPTH)]
        grads[name] = jnp.stack([g.T if name in _COLUMN_SHARDED else g for g in per_layer], axis=0)

    small_names = list(_SMALL_GRADS)
    stacked = [jnp.stack([g_small[l][name] for l in range(DEPTH)], axis=0) for name in small_names]
    small_parts = [loss_row, dmod] + [s.reshape(-1, s.shape[-1]) for s in stacked] + [g_final[None]]
    gathered = _allgather(small_parts, name="gather_small_grads")
    total = _sum_gathered(gathered, name="sum_small_grads")
    loss = total[0][0, 0]
    grads["ada_b"] = total[1]
    grads.update(zip(small_names, total[2:-1]))
    grads["final_g"] = total[-1][0]
    dmod_cols = lax.dynamic_slice_in_dim(gathered[1], dev * _ADA_SHARD, _ADA_SHARD, axis=2).transpose(1, 0, 2)
    grads["ada_w"] = _ada_wgrad(c_all, dmod_cols, name="ada_wgrad")
    for name in ("ssd_dt_bias", "ssd_a_log", "ssd_d"):
        grads[name] = grads[name][:, :SSD_HEADS]
    grads["pool_w"] = grads["pool_w"].reshape(pool_w.shape)
    grads["ssd_conv_w"] = lax.dynamic_slice_in_dim(
        grads["ssd_conv_w"].reshape(DEPTH, SSD_CONV_K, SSD_CONV_CH), dev * ssd_conv_w.shape[2], ssd_conv_w.shape[2], axis=2)
    grads["ffn_conv_w"] = lax.dynamic_slice_in_dim(
        grads["ffn_conv_w"].reshape(DEPTH, FFN_CONV_K, 2 * FFN_DIM), dev * ffn_conv_w.shape[2], ffn_conv_w.shape[2], axis=2)

    delta, new_m, new_v = {}, {}, {}
    for name, tr in (("ada_w", 512), ("w_in", 512), ("w_out", 256), ("ffn_up", 512), ("ffn_down", 352)):
        shp = w[name].shape
        two_d = lambda a: a.reshape(shp[0] * shp[1], shp[2])
        d_, m_, v_ = _adamw(two_d(w[name]), two_d(grads[name]), two_d(m[name]), two_d(v[name]), tr=tr,
                               name=f"adamw_{name}")
        delta[name], new_m[name], new_v[name] = (t.reshape(shp) for t in (d_, m_, v_))
    two_d = lambda a: a.reshape(-1, a.shape[-1])
    outs = _adamw_small(*[[two_d(t[name]) for name in _SMALL_PARAMS] for t in (w, grads, m, v)], name="adamw_small")
    for name, d_, m_, v_ in zip(_SMALL_PARAMS, *outs):
        delta[name], new_m[name], new_v[name] = (t.reshape(w[name].shape) for t in (d_, m_, v_))

    grad_x = dx[None]
    return (loss, grad_x, *[grads[n].reshape(w[n].shape) for n in _WEIGHT_ORDER],
            *[delta[n] for n in _WEIGHT_ORDER], *[new_m[n] for n in _WEIGHT_ORDER],
            *[new_v[n] for n in _WEIGHT_ORDER])
```

```python
import functools
import math

import jax
import jax.numpy as jnp
from jax import lax
from jax.experimental import pallas as pl
from jax.experimental.pallas import tpu as pltpu
from jax.experimental.pallas import tpu_sc as plsc

F32 = jnp.float32
BF16 = jnp.bfloat16

N_DEV = 8
D_MODEL = 1024
SEQ = 4096
DEPTH = 2
SSD_INNER = 512
SSD_HEADS = 8
SSD_HPG = 4
SSD_STATE = 128
SSD_CHUNK = 256
SSD_CONV_K = 4
SSD_CONV_CH = 1024
POOL_W = 256
POOL_WINDOWS = (2, 4, 8, 16)
ATT_W = 256
ATT_PATTERNS = ((128, 1), (512, 4), (2048, 16))
ATT_BLOCK = 128
ROT_DIM = 16
ROPE_THETA = 500000.0
IN_W = 2568
FFN_DIM = 2816
FFN_CONV_K = 3
NORM_EPS = 1e-6
HEAD_LANES = 64

ADAM_LR = 0.001
ADAM_B1 = 0.9
ADAM_B2 = 0.999
ADAM_EPS = 1e-08
ADAM_WD = 0.01
ADAM_STEP = 10

PROJ_W = 2816
PROJ_SSD_W = 1792
PROJ_Z_BLK = 2
PROJ_DT_BLK = 12
PROJ_POOL_BLK = 7
PROJ_Q_BLK, PROJ_K_BLK, PROJ_V_BLK = 16, 18, 20
MIX_POOL_BLK = 2
MIX_ATT_BLK = 6
VMEM_LIMIT = 56 * 1024 * 1024
ROW_TILE = 512
CONV_HALO = 8
POOL_HALO = 16
ATT_KPAD = ATT_BLOCK * 16
MESH = pl.DeviceIdType.MESH
_HBM = pl.BlockSpec(memory_space=pl.ANY)


def _cparams(*sem):
    return pltpu.CompilerParams(dimension_semantics=sem, vmem_limit_bytes=VMEM_LIMIT)


def _resident(shape):
    return pl.BlockSpec(shape, lambda i: (0,) * len(shape), pipeline_mode=pl.Buffered(1))


def _silu(x):
    return x * jax.nn.sigmoid(x)


def _pick_lane(v, h):
    lane = lax.broadcasted_iota(jnp.int32, v.shape, 1)
    return jnp.sum(jnp.where(lane == h, v, 0.0), axis=1, keepdims=True)


def _pick_row(v, h):
    row = lax.broadcasted_iota(jnp.int32, v.shape, 0)
    return jnp.sum(jnp.where(row == h, v, 0.0), axis=0, keepdims=True)


def _head_of_lane(width):
    return lax.broadcasted_iota(jnp.int32, (1, width), 1) // HEAD_LANES


@functools.partial(jax.custom_vjp, nondiff_argnums=(1, 2))
def _shift_rows(x_ext, s, halo):
    y = x_ext if s == 0 else pltpu.roll(x_ext, s, 0)
    return y[halo:]


def _shift_rows_fwd(x_ext, s, halo):
    return _shift_rows(x_ext, s, halo), None


def _shift_rows_bwd(s, halo, _, g):
    ge = jnp.concatenate([jnp.zeros((halo, g.shape[1]), g.dtype), g], axis=0)
    return (ge if s == 0 else pltpu.roll(ge, ge.shape[0] - s, 0),)


_shift_rows.defvjp(_shift_rows_fwd, _shift_rows_bwd)


@functools.partial(jax.custom_vjp, nondiff_argnums=(1,))
def _roll_rows(x, s):
    return pltpu.roll(x, s, 0)


def _roll_rows_fwd(x, s):
    return _roll_rows(x, s), None


def _roll_rows_bwd(s, _, g):
    return (pltpu.roll(g, g.shape[0] - s, 0),)


_roll_rows.defvjp(_roll_rows_fwd, _roll_rows_bwd)


def _rms_modulate(xv, g, sc, sh):
    r = lax.rsqrt(jnp.mean(xv * xv, axis=-1, keepdims=True) + NORM_EPS)
    return (xv * r * g) * (1.0 + sc) + sh


def _mm(a, w, *, name, nt=False, tm=ROW_TILE, tn=None, out_dtype=F32, norm=None, residual=None):
    t, k = a.shape
    n = w.shape[0] if nt else w.shape[1]
    tn = tn or n
    assert tn == n or (norm is None and residual is None)
    extra_in = list(norm or ()) + list(residual or ())

    def body(*refs):
        a_ref, w_ref = refs[:2]
        ins = refs[2:2 + len(extra_in)]
        outs = refs[2 + len(extra_in):]
        if norm is None:
            av = a_ref[...].astype(BF16)
        else:
            av = _rms_modulate(a_ref[...], ins[0][...], ins[1][...], ins[2][...]).astype(BF16)
            outs[1][...] = av
        if nt:
            acc = lax.dot_general(av, w_ref[...], (((1,), (1,)), ((), ())), preferred_element_type=F32)
        else:
            acc = jnp.dot(av, w_ref[...], preferred_element_type=F32)
        outs[0][...] = acc.astype(out_dtype)
        if residual is not None:
            x_ref, gate_ref = ins[-2:]
            outs[-1][...] = x_ref[...] + gate_ref[...] * acc

    row = lambda width: pl.BlockSpec((1, width), lambda i, j: (0, 0))
    tile = lambda width: pl.BlockSpec((tm, width), lambda i, j: (i, 0))
    w_spec = pl.BlockSpec((tn, k), lambda i, j: (j, 0)) if nt else pl.BlockSpec((k, tn), lambda i, j: (0, j))
    in_specs = [tile(k), w_spec] + ([row(k)] * 3 if norm else []) + ([tile(n), row(n)] if residual else [])
    out_specs = [pl.BlockSpec((tm, tn), lambda i, j: (i, j))] + ([tile(k)] if norm else []) + \
        ([tile(n)] if residual else [])
    out_shape = [jax.ShapeDtypeStruct((t, n), out_dtype)] + \
        ([jax.ShapeDtypeStruct((t, k), BF16)] if norm else []) + \
        ([jax.ShapeDtypeStruct((t, n), F32)] if residual else [])
    outs = pl.pallas_call(
        body, grid=(t // tm, n // tn), in_specs=in_specs, out_specs=out_specs, out_shape=out_shape, name=name,
        compiler_params=_cparams("parallel", "parallel"))(a, w, *extra_in)
    return outs[0] if len(outs) == 1 else outs


def _wgrad(a, b, *, name, tk=None, tn=None, tt=2048, out_dtype=BF16):
    t, k = a.shape
    n = b.shape[1]
    tk = tk or k
    tn = tn or n
    steps = t // tt

    def body(a_ref, b_ref, o_ref, acc_ref):
        s = pl.program_id(2)

        @pl.when(s == 0)
        def _():
            acc_ref[...] = jnp.zeros_like(acc_ref)

        acc_ref[...] += lax.dot_general(a_ref[...].astype(BF16), b_ref[...].astype(BF16),
                                        (((0,), (0,)), ((), ())), preferred_element_type=F32)

        @pl.when(s == steps - 1)
        def _():
            o_ref[...] = acc_ref[...].astype(out_dtype)

    return pl.pallas_call(
        body, grid=(k // tk, n // tn, steps),
        in_specs=[pl.BlockSpec((tt, tk), lambda i, j, s: (s, i)), pl.BlockSpec((tt, tn), lambda i, j, s: (s, j))],
        out_specs=pl.BlockSpec((tk, tn), lambda i, j, s: (i, j)),
        out_shape=jax.ShapeDtypeStruct((k, n), out_dtype),
        scratch_shapes=[pltpu.VMEM((tk, tn), F32)], name=name,
        compiler_params=_cparams("parallel", "parallel", "arbitrary"))(a, b)


def _norm_mod_bwd(x, dh, dres, g, sc, *, name, w=None, tm=ROW_TILE):
    s, d = x.shape
    steps = s // tm

    def body(x_ref, dh_ref, dres_ref, g_ref, sc_ref, *rest):
        w_ref = rest[0] if w is not None else None
        dx_ref, dg_ref, dsc_ref, dsh_ref, da_acc, dsh_acc = rest[-6:]
        i = pl.program_id(0)

        @pl.when(i == 0)
        def _():
            da_acc[...] = jnp.zeros_like(da_acc)
            dsh_acc[...] = jnp.zeros_like(dsh_acc)

        xv = x_ref[...]
        if w is None:
            dhv = dh_ref[...].astype(F32)
        else:
            dhv = jnp.dot(dh_ref[...], w_ref[...], preferred_element_type=F32)
        r = lax.rsqrt(jnp.mean(xv * xv, axis=-1, keepdims=True) + NORM_EPS)
        xhat = xv * r
        gain = g_ref[...] * (1.0 + sc_ref[...])
        dxhat = dhv * gain
        dx_ref[...] = dres_ref[...] + r * (dxhat - xhat * jnp.mean(dxhat * xhat, axis=-1, keepdims=True))
        da_acc[...] += jnp.sum(dhv * xhat, axis=0, keepdims=True)
        dsh_acc[...] += jnp.sum(dhv, axis=0, keepdims=True)

        @pl.when(i == steps - 1)
        def _():
            dg_ref[...] = da_acc[...] * (1.0 + sc_ref[...])
            dsc_ref[...] = da_acc[...] * g_ref[...]
            dsh_ref[...] = dsh_acc[...]

    row = pl.BlockSpec((1, d), lambda i: (0, 0))
    tile = pl.BlockSpec((tm, d), lambda i: (i, 0))
    row_shape = jax.ShapeDtypeStruct((1, d), F32)
    dh_spec = tile if w is None else pl.BlockSpec((tm, dh.shape[1]), lambda i: (i, 0))
    return pl.pallas_call(
        body, grid=(steps,), in_specs=[tile, dh_spec, tile, row, row] + ([] if w is None else [_resident(w.shape)]),
        out_specs=[tile, row, row, row],
        out_shape=[jax.ShapeDtypeStruct((s, d), F32), row_shape, row_shape, row_shape],
        scratch_shapes=[pltpu.VMEM((1, d), F32), pltpu.VMEM((1, d), F32)], name=name,
        compiler_params=_cparams("arbitrary"))(x, dh, dres, g, sc, *([] if w is None else [w]))


def _mid_bwd(x1, dh2, dx2, norm_g, sc, gate, mo, w_out, *, name, tm=ROW_TILE):
    s, d = x1.shape
    steps = s // tm

    def body(x_ref, dh_ref, dres_ref, ng_ref, sc_ref, g_ref, mo_ref, w_ref,
             dx_ref, dng_ref, dsc_ref, dsh_ref, dmix_ref, dmo_ref, dg_ref, da_acc, dsh_acc):
        i = pl.program_id(0)

        @pl.when(i == 0)
        def _():
            da_acc[...] = jnp.zeros_like(da_acc)
            dsh_acc[...] = jnp.zeros_like(dsh_acc)
            dg_ref[...] = jnp.zeros_like(dg_ref)

        xv = x_ref[...]
        dhv = dh_ref[...]
        r = lax.rsqrt(jnp.mean(xv * xv, axis=-1, keepdims=True) + NORM_EPS)
        xhat = xv * r
        dxhat = dhv * (ng_ref[...] * (1.0 + sc_ref[...]))
        dxv = dres_ref[...] + r * (dxhat - xhat * jnp.mean(dxhat * xhat, axis=-1, keepdims=True))
        dx_ref[...] = dxv
        da_acc[...] += jnp.sum(dhv * xhat, axis=0, keepdims=True)
        dsh_acc[...] += jnp.sum(dhv, axis=0, keepdims=True)
        dmo = (g_ref[...] * dxv).astype(BF16)
        dmo_ref[...] = dmo
        dg_ref[...] += jnp.sum(dxv * mo_ref[...], axis=0, keepdims=True)
        dmix_ref[...] = lax.dot_general(dmo, w_ref[...], (((1,), (1,)), ((), ())), preferred_element_type=F32)

        @pl.when(i == steps - 1)
        def _():
            dng_ref[...] = da_acc[...] * (1.0 + sc_ref[...])
            dsc_ref[...] = da_acc[...] * ng_ref[...]
            dsh_ref[...] = dsh_acc[...]

    tile = pl.BlockSpec((tm, d), lambda i: (i, 0))
    row = pl.BlockSpec((1, d), lambda i: (0, 0))
    mix_tile = pl.BlockSpec((tm, w_out.shape[0]), lambda i: (i, 0))
    row_shape = jax.ShapeDtypeStruct((1, d), F32)
    return pl.pallas_call(
        body, grid=(steps,), in_specs=[tile, tile, tile, row, row, row, tile, _resident(w_out.shape)],
        out_specs=[tile, row, row, row, mix_tile, tile, row],
        out_shape=[jax.ShapeDtypeStruct((s, d), F32), row_shape, row_shape, row_shape,
                   jax.ShapeDtypeStruct((s, w_out.shape[0]), F32), jax.ShapeDtypeStruct((s, d), BF16), row_shape],
        scratch_shapes=[pltpu.VMEM((1, d), F32), pltpu.VMEM((1, d), F32)], name=name,
        compiler_params=_cparams("arbitrary"))(x1, dh2, dx2, norm_g, sc, gate, mo, w_out)


def _final_loss(x, g, target, *, name, tm=ROW_TILE):
    s, d = x.shape
    steps = s // tm

    def body(x_ref, g_ref, t_ref, loss_ref, dx_ref, dg_ref, sq_acc):
        i = pl.program_id(0)

        @pl.when(i == 0)
        def _():
            sq_acc[...] = jnp.zeros_like(sq_acc)
            dg_ref[...] = jnp.zeros_like(dg_ref)

        xv = x_ref[...]
        r = lax.rsqrt(jnp.mean(xv * xv, axis=-1, keepdims=True) + NORM_EPS)
        xhat = xv * r
        err = xhat * g_ref[...] - t_ref[...]
        sq_acc[...] += jnp.sum(err * err, axis=0, keepdims=True)
        dy = err * (1.0 / d)
        dg_ref[...] += jnp.sum(dy * xhat, axis=0, keepdims=True)
        dxhat = dy * g_ref[...]
        dx_ref[...] = r * (dxhat - xhat * jnp.mean(dxhat * xhat, axis=-1, keepdims=True))

        @pl.when(i == steps - 1)
        def _():
            total = jnp.sum(sq_acc[...], axis=1, keepdims=True) * (0.5 / d)
            loss_ref[...] = jnp.broadcast_to(total, loss_ref.shape)

    tile = pl.BlockSpec((tm, d), lambda i: (i, 0))
    row = pl.BlockSpec((1, d), lambda i: (0, 0))
    return pl.pallas_call(
        body, grid=(steps,), in_specs=[tile, row, tile],
        out_specs=[pl.BlockSpec((1, 128), lambda i: (0, 0)), tile, row],
        out_shape=[jax.ShapeDtypeStruct((1, 128), F32), jax.ShapeDtypeStruct((s, d), F32),
                   jax.ShapeDtypeStruct((1, d), F32)],
        scratch_shapes=[pltpu.VMEM((1, d), F32)], name=name, compiler_params=_cparams("arbitrary"))(x, g, target)


def _ssd_chunk(z, xbc_ext, dt_raw, conv_w, conv_b, dt_bias, a_log, d_skip, norm_g, h_in):
    q = z.shape[0]
    gw = SSD_HPG * HEAD_LANES
    xc = conv_b
    for k in range(SSD_CONV_K):
        xc = xc + _pick_row(conv_w, k) * _shift_rows(xbc_ext, SSD_CONV_K - 1 - k, CONV_HALO)
    xc = _silu(xc)
    dt = jax.nn.softplus(dt_raw + dt_bias)
    da = dt * (-jnp.exp(a_log))
    ri = lax.broadcasted_iota(jnp.int32, (q, q), 0)
    ci = lax.broadcasted_iota(jnp.int32, (q, q), 1)
    causal = ri >= ci
    tril = causal.astype(F32)
    a_cum = jnp.dot(tril, da, preferred_element_type=F32, precision=lax.Precision.HIGHEST)
    a_cum_t = lax.dot_general(da, tril, (((0,), (1,)), ((), ())), preferred_element_type=F32,
                              precision=lax.Precision.HIGHEST)
    a_last = _pick_row(a_cum, q - 1)
    head = _head_of_lane(gw)
    ys, hs = [], []
    for g in range(2):
        xs = xc[:, gw * g:gw * (g + 1)]
        bm = xc[:, SSD_INNER + SSD_STATE * g:SSD_INNER + SSD_STATE * (g + 1)]
        cm = xc[:, SSD_INNER + 2 * SSD_STATE + SSD_STATE * g:SSD_INNER + 2 * SSD_STATE + SSD_STATE * (g + 1)]
        cb = lax.dot_general(cm.astype(BF16), bm.astype(BF16), (((1,), (1,)), ((), ())), preferred_element_type=F32)
        cols = [_pick_lane(a_cum, SSD_HPG * g + j) for j in range(SSD_HPG)]
        lasts = [_pick_lane(a_last, SSD_HPG * g + j) for j in range(SSD_HPG)]
        dt_exp = sum(jnp.where(head == j, _pick_lane(dt, SSD_HPG * g + j), 0.0) for j in range(SSD_HPG))
        d_exp = sum(jnp.where(head == j, _pick_lane(d_skip, SSD_HPG * g + j), 0.0) for j in range(SSD_HPG))
        e_cum = sum(jnp.where(head == j, jnp.exp(cols[j]), 0.0) for j in range(SSD_HPG))
        c_dec = sum(jnp.where(head == j, jnp.exp(lasts[j]), 0.0) for j in range(SSD_HPG))
        xsdt = (xs * dt_exp).astype(BF16)
        y_diag = jnp.zeros((q, gw), F32)
        st_new = jnp.zeros((SSD_STATE, gw), F32)
        for j in range(SSD_HPG):
            row = _pick_row(a_cum_t, SSD_HPG * g + j)
            lmat = jnp.exp(jnp.where(causal, cols[j] - row, -jnp.inf))
            r = jnp.dot((cb * lmat).astype(BF16), xsdt, preferred_element_type=F32)
            y_diag = y_diag + jnp.where(head == j, r, 0.0)
            bd = (bm * jnp.exp(lasts[j] - cols[j])).astype(BF16)
            st = lax.dot_general(bd, xsdt, (((0,), (0,)), ((), ())), preferred_element_type=F32)
            st_new = st_new + jnp.where(head == j, st, 0.0)
        y_off = jnp.dot(cm.astype(BF16), h_in[g].astype(BF16), preferred_element_type=F32) * e_cum
        hs.append(h_in[g] * c_dec + st_new)
        y = y_diag + y_off + d_exp * xs
        yz = y * _silu(z[:, gw * g:gw * (g + 1)])
        yz = yz * lax.rsqrt(jnp.mean(yz * yz, axis=-1, keepdims=True) + NORM_EPS)
        ys.append(yz * norm_g[:, gw * g:gw * (g + 1)])
    return jnp.concatenate(ys, axis=1), tuple(hs)


_SSD_NCHUNK = SEQ // SSD_CHUNK
_HALO_PER_CHUNK = SSD_CHUNK // CONV_HALO


def _ssd_param_specs(const):
    return [pl.BlockSpec((8, SSD_CONV_CH), const), pl.BlockSpec((1, SSD_CONV_CH), const),
            pl.BlockSpec((1, 128), const), pl.BlockSpec((1, 128), const), pl.BlockSpec((1, 128), const),
            pl.BlockSpec((1, SSD_INNER), const)]


def _ssd_fwd(proj, conv_w, conv_b, dt_bias, a_log, d_skip, norm_g, *, name):
    q = SSD_CHUNK

    def body(z_ref, xbc_ref, halo_ref, dt_ref, cw_ref, cb_ref, db_ref, al_ref, d_ref, ng_ref, y_ref, hs_ref, h_acc):
        i = pl.program_id(0)

        @pl.when(i == 0)
        def _():
            h_acc[...] = jnp.zeros_like(h_acc)

        halo = jnp.where(i == 0, 0.0, halo_ref[...])
        xbc_ext = jnp.concatenate([halo, xbc_ref[...]], axis=0)
        h_in = (h_acc[0], h_acc[1])
        hs_ref[0, 0] = h_in[0]
        hs_ref[0, 1] = h_in[1]
        y, h_out = _ssd_chunk(z_ref[...], xbc_ext, dt_ref[...], cw_ref[...], cb_ref[...], db_ref[...], al_ref[...],
                              d_ref[...], ng_ref[...], h_in)
        y_ref[...] = y.astype(BF16)
        h_acc[0] = h_out[0]
        h_acc[1] = h_out[1]

    const = lambda i: (0, 0)
    return pl.pallas_call(
        body, grid=(_SSD_NCHUNK,),
        in_specs=[pl.BlockSpec((q, SSD_INNER), lambda i: (i, PROJ_Z_BLK)),
                  pl.BlockSpec((q, SSD_CONV_CH), lambda i: (i, 0)),
                  pl.BlockSpec((CONV_HALO, SSD_CONV_CH), lambda i: (jnp.maximum(i * _HALO_PER_CHUNK - 1, 0), 0)),
                  pl.BlockSpec((q, 128), lambda i: (i, PROJ_DT_BLK))] + _ssd_param_specs(const),
        out_specs=[pl.BlockSpec((q, SSD_INNER), lambda i: (i, 0)),
                   pl.BlockSpec((1, 2, SSD_STATE, 256), lambda i: (i, 0, 0, 0))],
        out_shape=[jax.ShapeDtypeStruct((SEQ, D_MODEL), BF16),
                   jax.ShapeDtypeStruct((_SSD_NCHUNK, 2, SSD_STATE, 256), F32)],
        scratch_shapes=[pltpu.VMEM((2, SSD_STATE, 256), F32)], name=name,
        compiler_params=_cparams("arbitrary"))(proj, proj, proj, proj, conv_w, conv_b, dt_bias, a_log, d_skip, norm_g)


def _ssd_bwd(proj, hstates, dmix, conv_w, conv_b, dt_bias, a_log, d_skip, norm_g, *, name):
    q = SSD_CHUNK
    last = _SSD_NCHUNK - 1

    def body(z_ref, xbc_ref, halo_ref, dt_ref, hs_ref, dy_ref, cw_ref, cb_ref, db_ref, al_ref, d_ref, ng_ref,
             dp_ref, dcw_ref, dcb_ref, ddb_ref, dal_ref, dd_ref, dng_ref, dh_acc, dhalo_acc):
        i = pl.program_id(0)

        @pl.when(i == 0)
        def _():
            dh_acc[...] = jnp.zeros_like(dh_acc)
            dhalo_acc[...] = jnp.zeros_like(dhalo_acc)
            for r in (dcw_ref, dcb_ref, ddb_ref, dal_ref, dd_ref, dng_ref):
                r[...] = jnp.zeros_like(r)

        halo = jnp.where(i == last, 0.0, halo_ref[...])
        xbc_ext = jnp.concatenate([halo, xbc_ref[...]], axis=0)
        _, vjp = jax.vjp(_ssd_chunk, z_ref[...], xbc_ext, dt_ref[...], cw_ref[...], cb_ref[...], db_ref[...],
                         al_ref[...], d_ref[...], ng_ref[...], (hs_ref[0, 0], hs_ref[0, 1]))
        gz, gx, gdt, gcw, gcb, gdb, gal, gd, gng, gh = vjp((dy_ref[...], (dh_acc[0], dh_acc[1])))
        dxbc = jnp.concatenate([gx[CONV_HALO:q], gx[q:] + dhalo_acc[...]], axis=0)
        dp_ref[...] = jnp.concatenate([dxbc, gz, gdt, jnp.zeros_like(gdt)], axis=1).astype(BF16)
        dhalo_acc[...] = gx[:CONV_HALO]
        dh_acc[0] = gh[0]
        dh_acc[1] = gh[1]
        dcw_ref[...] += gcw
        dcb_ref[...] += gcb
        ddb_ref[...] += gdb
        dal_ref[...] += gal
        dd_ref[...] += gd
        dng_ref[...] += gng

    const = lambda i: (0, 0)
    rev = lambda i: last - i
    row = lambda n: jax.ShapeDtypeStruct((1, n), F32)
    return pl.pallas_call(
        body, grid=(_SSD_NCHUNK,),
        in_specs=[pl.BlockSpec((q, SSD_INNER), lambda i: (rev(i), PROJ_Z_BLK)),
                  pl.BlockSpec((q, SSD_CONV_CH), lambda i: (rev(i), 0)),
                  pl.BlockSpec((CONV_HALO, SSD_CONV_CH), lambda i: (jnp.maximum(rev(i) * _HALO_PER_CHUNK - 1, 0), 0)),
                  pl.BlockSpec((q, 128), lambda i: (rev(i), PROJ_DT_BLK)),
                  pl.BlockSpec((1, 2, SSD_STATE, 256), lambda i: (rev(i), 0, 0, 0)),
                  pl.BlockSpec((q, SSD_INNER), lambda i: (rev(i), 0))] + _ssd_param_specs(const),
        out_specs=[pl.BlockSpec((q, PROJ_SSD_W), lambda i: (rev(i), 0))] + _ssd_param_specs(const),
        out_shape=[jax.ShapeDtypeStruct((SEQ, PROJ_W), BF16), jax.ShapeDtypeStruct((8, SSD_CONV_CH), F32),
                   row(SSD_CONV_CH), row(128), row(128), row(128), row(SSD_INNER)],
        scratch_shapes=[pltpu.VMEM((2, SSD_STATE, 256), F32), pltpu.VMEM((CONV_HALO, SSD_CONV_CH), F32)], name=name,
        compiler_params=_cparams("arbitrary"))(proj, proj, proj, proj, hstates, dmix, conv_w, conv_b, dt_bias, a_log,
                                                d_skip, norm_g)


def _rope_tables(pos_col, inv_freq_lane, *, name):
    s = pos_col.shape[0]

    def body(p_ref, f_ref, c_ref, s1_ref, s2_ref):
        ang = p_ref[...] * f_ref[...]
        within = lax.broadcasted_iota(jnp.int32, ang.shape, 1) % HEAD_LANES
        half = ROT_DIM // 2
        c_ref[...] = jnp.where(within < ROT_DIM, jnp.cos(ang), 1.0)
        sn = jnp.sin(ang)
        s1_ref[...] = jnp.where(within < half, -sn, 0.0)
        s2_ref[...] = jnp.where((within >= half) & (within < ROT_DIM), sn, 0.0)

    shp = jax.ShapeDtypeStruct((s, 128), F32)
    return pl.pallas_call(body, out_shape=[shp, shp, shp], name=name,
                          compiler_params=pltpu.CompilerParams(vmem_limit_bytes=VMEM_LIMIT))(pos_col, inv_freq_lane)


def _rope(t, c, s1, s2):
    half = ROT_DIM // 2
    return t * c + pltpu.roll(t, 128 - half, 1) * s1 + pltpu.roll(t, half, 1) * s2


def _rope_t(g, c, s1, s2):
    half = ROT_DIM // 2
    return g * c + pltpu.roll(g * s1, half, 1) + pltpu.roll(g * s2, 128 - half, 1)


def _att_valid(b):
    qi = lax.broadcasted_iota(jnp.int32, (ATT_BLOCK, 2 * ATT_BLOCK), 0)
    kj = lax.broadcasted_iota(jnp.int32, (ATT_BLOCK, 2 * ATT_BLOCK), 1)
    rel = qi + ATT_BLOCK - kj
    return (rel >= 0) & (rel <= ATT_BLOCK) & (b * ATT_BLOCK + kj - ATT_BLOCK >= 0)


def _att_slices(i, d):
    if d == 1:
        qstart = pl.multiple_of(i * ATT_BLOCK, ATT_BLOCK)
        return i, pl.ds(qstart, ATT_BLOCK), pl.ds(pl.multiple_of(qstart - ATT_BLOCK + ATT_KPAD, ATT_BLOCK), 2 * ATT_BLOCK)
    r = i % d
    b = i // d
    qstart = r + d * ATT_BLOCK * b
    return b, pl.ds(qstart, ATT_BLOCK, stride=d), pl.ds(qstart - ATT_BLOCK * d + ATT_KPAD, 2 * ATT_BLOCK, stride=d)


_ATT_NBLK = SEQ // ATT_BLOCK
_ATT_SCALE = HEAD_LANES ** -0.5
_ATT_UNROLL_FWD = 8
_ATT_UNROLL = 4


def _att_fwd(proj, cos, sin1, sin2, mix, *, name):
    s = SEQ

    def body(q_ref, k_ref, v_ref, c_ref, s1_ref, s2_ref, _, o_ref, lse_ref, mix_ref, qs, ks, vs, acc, m_s, l_s):
        c, s1, s2 = c_ref[...], s1_ref[...], s2_ref[...]
        qs[...] = _rope(q_ref[...], c, s1, s2) * _ATT_SCALE
        zeros = jnp.zeros((ATT_KPAD, 128), F32)
        ks[pl.ds(0, ATT_KPAD), :] = zeros
        vs[pl.ds(0, ATT_KPAD), :] = zeros
        ks[pl.ds(ATT_KPAD, s), :] = _rope(k_ref[...], c, s1, s2)
        vs[pl.ds(ATT_KPAD, s), :] = v_ref[...]
        head0 = _head_of_lane(128) == 0

        for bi, (_, d) in enumerate(ATT_PATTERNS):
            def blk(i, carry, d=d, first=(bi == 0)):
                b, sq, sk = _att_slices(i, d)
                qb = qs[sq, :]
                kw = ks[sk, :].astype(BF16)
                vw = vs[sk, :].astype(BF16)
                valid = _att_valid(b)
                ms, ls, os_ = [], [], []
                for hh in range(2):
                    qh = jnp.where(head0 if hh == 0 else ~head0, qb, 0.0).astype(BF16)
                    sc = lax.dot_general(qh, kw, (((1,), (1,)), ((), ())), preferred_element_type=F32)
                    sc = jnp.where(valid, sc, -jnp.inf)
                    mb = jnp.max(sc, axis=1, keepdims=True)
                    p = jnp.exp(sc - mb)
                    ms.append(mb)
                    ls.append(jnp.sum(p, axis=1, keepdims=True))
                    os_.append(jnp.dot(p.astype(BF16), vw, preferred_element_type=F32))
                m_b = jnp.where(head0, ms[0], ms[1])
                l_b = jnp.where(head0, ls[0], ls[1])
                o_b = jnp.where(head0, os_[0], os_[1])
                if first:
                    m_s[sq, :] = m_b
                    l_s[sq, :] = l_b
                    acc[sq, :] = o_b
                else:
                    m_old = m_s[sq, :]
                    m_new = jnp.maximum(m_old, m_b)
                    a_old = jnp.exp(m_old - m_new)
                    a_b = jnp.exp(m_b - m_new)
                    m_s[sq, :] = m_new
                    l_s[sq, :] = l_s[sq, :] * a_old + l_b * a_b
                    acc[sq, :] = acc[sq, :] * a_old + o_b * a_b
                return carry

            lax.fori_loop(0, _ATT_NBLK, blk, 0, unroll=_ATT_UNROLL_FWD)

        out = acc[...] / l_s[...]
        o_ref[...] = out
        mix_ref[...] = out.astype(BF16)
        lse_ref[...] = m_s[...] + jnp.log(l_s[...])

    col = lambda base: pl.BlockSpec((s, 128), lambda p: (0, base + p))
    tab = pl.BlockSpec((s, 128), lambda p: (0, 0))
    big = pltpu.VMEM((ATT_KPAD + s, 128), F32)
    tok = pltpu.VMEM((s, 128), F32)
    return pl.pallas_call(
        body, grid=(2,), in_specs=[col(PROJ_Q_BLK), col(PROJ_K_BLK), col(PROJ_V_BLK), tab, tab, tab, _HBM],
        out_specs=[pl.BlockSpec((s, 128), lambda p: (0, p)), pl.BlockSpec((s, 128), lambda p: (0, p)),
                   col(MIX_ATT_BLK)],
        out_shape=[jax.ShapeDtypeStruct((s, ATT_W), F32), jax.ShapeDtypeStruct((s, ATT_W), F32),
                   jax.ShapeDtypeStruct(mix.shape, mix.dtype)],
        input_output_aliases={6: 2}, scratch_shapes=[tok, big, big, tok, tok, tok], name=name,
        compiler_params=_cparams("arbitrary"))(proj, proj, proj, cos, sin1, sin2, mix)


def _att_bwd(proj, cos, sin1, sin2, out, lse, dmix, dproj, *, name):
    s = SEQ

    def body(proj_ref, c_hbm, s1_hbm, s2_hbm, out_hbm, lse_hbm, dmix_hbm, _, dproj_hbm,
             c_ref, s1_ref, s2_ref, o_ref, lse_ref, do_ref, qs, ks, vs, dqs, dks, dvs, staged, sems):
        def start(copies):
            for cp in copies:
                cp.start()
            return copies

        def load(pair):
            lanes = pl.ds(128 * pair, 128)
            rows = pl.ds(ATT_KPAD, s)
            return start([
                pltpu.make_async_copy(proj_ref.at[:, pl.ds(128 * (PROJ_Q_BLK + pair), 128)], qs, sems.at[0]),
                pltpu.make_async_copy(proj_ref.at[:, pl.ds(128 * (PROJ_K_BLK + pair), 128)], ks.at[rows, :], sems.at[1]),
                pltpu.make_async_copy(proj_ref.at[:, pl.ds(128 * (PROJ_V_BLK + pair), 128)], vs.at[rows, :], sems.at[2]),
                pltpu.make_async_copy(out_hbm.at[:, lanes], o_ref, sems.at[3]),
                pltpu.make_async_copy(lse_hbm.at[:, lanes], lse_ref, sems.at[4]),
                pltpu.make_async_copy(dmix_hbm.at[:, pl.ds(128 * (MIX_ATT_BLK + pair), 128)], do_ref, sems.at[5])])

        tables = start([pltpu.make_async_copy(c_hbm, c_ref, sems.at[6]),
                        pltpu.make_async_copy(s1_hbm, s1_ref, sems.at[7]),
                        pltpu.make_async_copy(s2_hbm, s2_ref, sems.at[8])])
        loads = load(0)
        for cp in tables:
            cp.wait()
        head0 = _head_of_lane(128) == 0
        zeros = jnp.zeros((ATT_KPAD, 128), F32)
        for pair in range(2):
            for cp in loads:
                cp.wait()
            c, s1, s2 = c_ref[...], s1_ref[...], s2_ref[...]
            qs[...] = _rope(qs[...], c, s1, s2) * _ATT_SCALE
            ks[pl.ds(0, ATT_KPAD), :] = zeros
            vs[pl.ds(0, ATT_KPAD), :] = zeros
            ks[pl.ds(ATT_KPAD, s), :] = _rope(ks[pl.ds(ATT_KPAD, s), :], c, s1, s2)
            dqs[...] = jnp.zeros_like(dqs)
            dks[...] = jnp.zeros_like(dks)
            dvs[...] = jnp.zeros_like(dvs)

            for _, d in ATT_PATTERNS:
                def blk(i, carry, d=d):
                    b, sq, sk = _att_slices(i, d)
                    qb = qs[sq, :]
                    kw = ks[sk, :].astype(BF16)
                    vw = vs[sk, :].astype(BF16)
                    dob = do_ref[sq, :]
                    lse_b = lse_ref[sq, :]
                    dd = dob * o_ref[sq, :]
                    valid = _att_valid(b)
                    dq_b = jnp.zeros((ATT_BLOCK, 128), F32)
                    dk_w = jnp.zeros((2 * ATT_BLOCK, 128), F32)
                    dv_w = jnp.zeros((2 * ATT_BLOCK, 128), F32)
                    for hh in range(2):
                        hm = head0 if hh == 0 else ~head0
                        qh = jnp.where(hm, qb, 0.0).astype(BF16)
                        doh = jnp.where(hm, dob, 0.0).astype(BF16)
                        lse_h = _pick_lane(lse_b, hh * HEAD_LANES)
                        d_h = jnp.sum(jnp.where(hm, dd, 0.0), axis=1, keepdims=True)
                        sc = lax.dot_general(qh, kw, (((1,), (1,)), ((), ())), preferred_element_type=F32)
                        p = jnp.where(valid, jnp.exp(sc - lse_h), 0.0)
                        dp = lax.dot_general(doh, vw, (((1,), (1,)), ((), ())), preferred_element_type=F32)
                        ds = (p * (dp - d_h)).astype(BF16)
                        dq_b = dq_b + jnp.where(hm, jnp.dot(ds, kw, preferred_element_type=F32), 0.0)
                        dk_w = dk_w + lax.dot_general(ds, qh, (((0,), (0,)), ((), ())), preferred_element_type=F32)
                        dv_w = dv_w + lax.dot_general(p.astype(BF16), doh, (((0,), (0,)), ((), ())),
                                                      preferred_element_type=F32)
                    dqs[sq, :] += dq_b
                    dks[sk, :] += dk_w
                    dvs[sk, :] += dv_w
                    return carry

                lax.fori_loop(0, _ATT_NBLK, blk, 0, unroll=_ATT_UNROLL)

            staged[0] = _rope_t(dqs[...] * _ATT_SCALE, c, s1, s2).astype(BF16)
            staged[1] = _rope_t(dks[pl.ds(ATT_KPAD, s), :], c, s1, s2).astype(BF16)
            staged[2] = dvs[pl.ds(ATT_KPAD, s), :].astype(BF16)
            stores = start([
                pltpu.make_async_copy(staged.at[j], dproj_hbm.at[:, pl.ds(128 * (col + pair), 128)], sems.at[9 + j])
                for j, col in enumerate((PROJ_Q_BLK, PROJ_K_BLK, PROJ_V_BLK))])
            if pair == 0:
                loads = load(1)
            for cp in stores:
                cp.wait()

    big = pltpu.VMEM((ATT_KPAD + s, 128), F32)
    tok = pltpu.VMEM((s, 128), F32)
    return pl.pallas_call(
        body, in_specs=[_HBM] * 8, out_specs=_HBM, out_shape=jax.ShapeDtypeStruct(dproj.shape, dproj.dtype),
        input_output_aliases={7: 0},
        scratch_shapes=[tok] * 6 + [tok, big, big, tok, big, big, pltpu.VMEM((3, s, 128), BF16),
                                    pltpu.SemaphoreType.DMA((12,))], name=name,
        compiler_params=pltpu.CompilerParams(vmem_limit_bytes=VMEM_LIMIT))(
            proj, cos, sin1, sin2, out, lse, dmix, dproj)


_POOL_TM = 512
_POOL_NT = SEQ // _POOL_TM
_POOL_HALO_PER_TILE = _POOL_TM // POOL_HALO


def _pool_tile(u_ext, w_bd, scale, t0):
    s2 = u_ext + _roll_rows(u_ext, 1)
    s4 = s2 + _roll_rows(s2, 2)
    s8 = s4 + _roll_rows(s4, 4)
    s16 = s8 + _roll_rows(s8, 8)
    grp = _head_of_lane(POOL_W)
    sel = jnp.where(grp == 0, s2, jnp.where(grp == 1, s4, jnp.where(grp == 2, s8, s16)))[POOL_HALO:]
    t = sel.shape[0]
    pos = t0 + lax.broadcasted_iota(jnp.int32, (t, POOL_W), 0) + 1
    win = jnp.where(grp == 0, 2, jnp.where(grp == 1, 4, jnp.where(grp == 2, 8, 16)))
    cnt = jnp.minimum(pos, win).astype(F32)
    diff = sel / cnt - u_ext[POOL_HALO:]
    return jnp.dot(diff.astype(BF16), w_bd.astype(BF16), preferred_element_type=F32) * scale


def _pool_fwd(proj, w_bd, scale, mix, *, name):
    tm = _POOL_TM

    def body(u_ref, halo_ref, w_ref, sc_ref, _, y_ref):
        i = pl.program_id(0)
        halo = jnp.where(i == 0, 0.0, halo_ref[...])
        u_ext = jnp.concatenate([halo, u_ref[...]], axis=0)
        y_ref[...] = _pool_tile(u_ext, w_ref[...], sc_ref[...], i * tm).astype(BF16)

    return pl.pallas_call(
        body, grid=(_POOL_NT,),
        in_specs=[pl.BlockSpec((tm, POOL_W), lambda i: (i, PROJ_POOL_BLK)),
                  pl.BlockSpec((POOL_HALO, POOL_W),
                               lambda i: (jnp.maximum(i * _POOL_HALO_PER_TILE - 1, 0), PROJ_POOL_BLK)),
                  pl.BlockSpec((POOL_W, POOL_W), lambda i: (0, 0)), pl.BlockSpec((1, POOL_W), lambda i: (0, 0)), _HBM],
        out_specs=pl.BlockSpec((tm, POOL_W), lambda i: (i, MIX_POOL_BLK)),
        out_shape=jax.ShapeDtypeStruct(mix.shape, mix.dtype), input_output_aliases={4: 0}, name=name,
        compiler_params=_cparams("parallel"))(proj, proj, w_bd, scale, mix)


def _pool_bwd(proj, dmix, w_bd, scale, dproj, *, name):
    tm = _POOL_TM
    last = _POOL_NT - 1

    def body(u_ref, halo_ref, dy_ref, w_ref, sc_ref, _, du_ref, dw_ref, dsc_ref, dhalo_acc):
        i = pl.program_id(0)

        @pl.when(i == 0)
        def _():
            dhalo_acc[...] = jnp.zeros_like(dhalo_acc)
            dw_ref[...] = jnp.zeros_like(dw_ref)
            dsc_ref[...] = jnp.zeros_like(dsc_ref)

        tile = last - i
        halo = jnp.where(tile == 0, 0.0, halo_ref[...])
        u_ext = jnp.concatenate([halo, u_ref[...]], axis=0)
        _, vjp = jax.vjp(functools.partial(_pool_tile, t0=tile * tm), u_ext, w_ref[...], sc_ref[...])
        gu, gw, gs = vjp(dy_ref[...])
        du_ref[...] = jnp.concatenate([gu[POOL_HALO:tm], gu[tm:] + dhalo_acc[...]], axis=0).astype(BF16)
        dhalo_acc[...] = gu[:POOL_HALO]
        dw_ref[...] += gw
        dsc_ref[...] += gs

    rev = lambda i: last - i
    return pl.pallas_call(
        body, grid=(_POOL_NT,),
        in_specs=[pl.BlockSpec((tm, POOL_W), lambda i: (rev(i), PROJ_POOL_BLK)),
                  pl.BlockSpec((POOL_HALO, POOL_W),
                               lambda i: (jnp.maximum(rev(i) * _POOL_HALO_PER_TILE - 1, 0), PROJ_POOL_BLK)),
                  pl.BlockSpec((tm, POOL_W), lambda i: (rev(i), MIX_POOL_BLK)),
                  pl.BlockSpec((POOL_W, POOL_W), lambda i: (0, 0)), pl.BlockSpec((1, POOL_W), lambda i: (0, 0)), _HBM],
        out_specs=[pl.BlockSpec((tm, POOL_W), lambda i: (rev(i), PROJ_POOL_BLK)),
                   pl.BlockSpec((POOL_W, POOL_W), lambda i: (0, 0)), pl.BlockSpec((1, POOL_W), lambda i: (0, 0))],
        out_shape=[jax.ShapeDtypeStruct(dproj.shape, dproj.dtype), jax.ShapeDtypeStruct((POOL_W, POOL_W), F32),
                   jax.ShapeDtypeStruct((1, POOL_W), F32)],
        input_output_aliases={5: 0}, scratch_shapes=[pltpu.VMEM((POOL_HALO, POOL_W), F32)], name=name,
        compiler_params=_cparams("arbitrary"))(proj, proj, dmix, w_bd, scale, dproj)


_FFN_TM = 256
_FFN_NT = SEQ // _FFN_TM
_FFN_HALO_PER_TILE = _FFN_TM // CONV_HALO


def _ffn_act_tile(hid_ext, conv_w, conv_b):
    hc = conv_b
    for k in range(FFN_CONV_K):
        hc = hc + _pick_row(conv_w, k) * _shift_rows(hid_ext, FFN_CONV_K - 1 - k, CONV_HALO)
    return _silu(hc[:, :FFN_DIM]) * hc[:, FFN_DIM:]


def _ffn_fwd(x1, norm_g, sc, sh, gate, up_t, down, conv_w, conv_b, *, name):
    tm = _FFN_TM
    w = 2 * FFN_DIM
    d = D_MODEL

    def body(x_ref, ng_ref, sc_ref, sh_ref, g_ref, up_ref, dn_ref, cw_ref, cb_ref,
             h_ref, hid_ref, act_ref, f_ref, x2_ref, halo_acc):
        i = pl.program_id(0)
        h2 = _rms_modulate(x_ref[...], ng_ref[...], sc_ref[...], sh_ref[...]).astype(BF16)
        h_ref[...] = h2
        hid = lax.dot_general(h2, up_ref[...], (((1,), (1,)), ((), ())), preferred_element_type=F32)
        hid_ref[...] = hid
        halo = jnp.where(i == 0, 0.0, halo_acc[...])
        act = _ffn_act_tile(jnp.concatenate([halo, hid], axis=0), cw_ref[...], cb_ref[...]).astype(BF16)
        halo_acc[...] = hid[tm - CONV_HALO:]
        act_ref[...] = act
        f = jnp.dot(act, dn_ref[...], preferred_element_type=F32)
        f_ref[...] = f
        x2_ref[...] = x_ref[...] + g_ref[...] * f

    tile = lambda n: pl.BlockSpec((tm, n), lambda i: (i, 0))
    return pl.pallas_call(
        body, grid=(_FFN_NT,),
        in_specs=[tile(d)] + [_resident((1, d))] * 4 + [_resident((w, d)), _resident((FFN_DIM, d)),
                                                        _resident((8, w)), _resident((1, w))],
        out_specs=[tile(d), tile(w), tile(FFN_DIM), tile(d), tile(d)],
        out_shape=[jax.ShapeDtypeStruct((SEQ, d), BF16), jax.ShapeDtypeStruct((SEQ, w), F32),
                   jax.ShapeDtypeStruct((SEQ, FFN_DIM), BF16), jax.ShapeDtypeStruct((SEQ, d), F32),
                   jax.ShapeDtypeStruct((SEQ, d), F32)],
        scratch_shapes=[pltpu.VMEM((CONV_HALO, w), F32)], name=name,
        compiler_params=_cparams("arbitrary"))(x1, norm_g, sc, sh, gate, up_t, down, conv_w, conv_b)


def _ffn_bwd(dx2, gate, f, hid, up_t, down, conv_w, conv_b, *, name):
    tm = _FFN_TM
    w = 2 * FFN_DIM
    d = D_MODEL
    last = _FFN_NT - 1

    def body(dx_ref, g_ref, f_ref, h_ref, halo_ref, up_ref, dn_ref, cw_ref, cb_ref,
             df_ref, dg_ref, dh_ref, dh2_ref, dcw_ref, dcb_ref, dhalo_acc):
        i = pl.program_id(0)

        @pl.when(i == 0)
        def _():
            dhalo_acc[...] = jnp.zeros_like(dhalo_acc)
            dcw_ref[...] = jnp.zeros_like(dcw_ref)
            dcb_ref[...] = jnp.zeros_like(dcb_ref)
            dg_ref[...] = jnp.zeros_like(dg_ref)

        dxv = dx_ref[...]
        df = (g_ref[...] * dxv).astype(BF16)
        df_ref[...] = df
        dg_ref[...] += jnp.sum(dxv * f_ref[...], axis=0, keepdims=True)
        dact = lax.dot_general(df, dn_ref[...], (((1,), (1,)), ((), ())), preferred_element_type=F32)
        halo = jnp.where(i == last, 0.0, halo_ref[...])
        hid_ext = jnp.concatenate([halo, h_ref[...]], axis=0)
        _, vjp = jax.vjp(_ffn_act_tile, hid_ext, cw_ref[...], cb_ref[...])
        gh, gw, gb = vjp(dact)
        dhid = jnp.concatenate([gh[CONV_HALO:tm], gh[tm:] + dhalo_acc[...]], axis=0).astype(BF16)
        dhalo_acc[...] = gh[:CONV_HALO]
        dh_ref[...] = dhid
        dh2_ref[...] = jnp.dot(dhid, up_ref[...], preferred_element_type=F32)
        dcw_ref[...] += gw
        dcb_ref[...] += gb

    rev = lambda i: last - i
    tile = lambda n: pl.BlockSpec((tm, n), lambda i: (rev(i), 0))
    acc = lambda shape: pl.BlockSpec(shape, lambda i: (0, 0))
    return pl.pallas_call(
        body, grid=(_FFN_NT,),
        in_specs=[tile(d), _resident((1, d)), tile(d), tile(w),
                  pl.BlockSpec((CONV_HALO, w), lambda i: (jnp.maximum(rev(i) * _FFN_HALO_PER_TILE - 1, 0), 0)),
                  _resident((w, d)), _resident((FFN_DIM, d)), _resident((8, w)), _resident((1, w))],
        out_specs=[tile(d), acc((1, d)), tile(w), tile(d), acc((8, w)), acc((1, w))],
        out_shape=[jax.ShapeDtypeStruct((SEQ, d), BF16), jax.ShapeDtypeStruct((1, d), F32),
                   jax.ShapeDtypeStruct((SEQ, w), BF16), jax.ShapeDtypeStruct((SEQ, d), F32),
                   jax.ShapeDtypeStruct((8, w), F32), jax.ShapeDtypeStruct((1, w), F32)],
        scratch_shapes=[pltpu.VMEM((CONV_HALO, w), F32)], name=name,
        compiler_params=_cparams("arbitrary"))(dx2, gate, f, hid, hid, up_t, down, conv_w, conv_b)


def _axes():
    return lax.axis_index("x"), lax.axis_index("y"), lax.axis_index("c")


def _handshake(peers):
    barrier = pltpu.get_barrier_semaphore()
    for peer in peers:
        pl.semaphore_signal(barrier, inc=1, device_id=peer, device_id_type=MESH)
    pl.semaphore_wait(barrier, len(peers))


def _allgather_body(x_refs, out_refs, send_sems, recv_sems, local_sems, own_barrier):
    n = len(x_refs)
    x, y, c = _axes()
    me, sibling = (x, y, c), (x, y, 1 - c)
    chips = [(1 - x, y), (x, 1 - y), (1 - x, 1 - y)]
    if own_barrier:
        _handshake([sibling] + [(*chip, c) for chip in chips])

    def slot(a, px, py, pc):
        return out_refs[a].at[4 * px + 2 * py + pc]

    def copy(a, k, block, to, src=None):
        return pltpu.make_async_remote_copy(
            src_ref=slot(a, *block) if src is None else src, dst_ref=slot(a, *block),
            send_sem=send_sems.at[a, k], recv_sem=recv_sems.at[a, k], device_id=to, device_id_type=MESH)

    mines, firsts = [], []
    for a in range(n):
        mines.append(pltpu.make_async_copy(x_refs[a], slot(a, *me), local_sems.at[a]))
        mines[-1].start()
        first = [copy(a, 0, me, sibling, src=x_refs[a])]
        first += [copy(a, 1 + j, me, (*chip, c), src=x_refs[a]) for j, chip in enumerate(chips)]
        for cp in first:
            cp.start()
        firsts += first
    passed = []
    for j, chip in enumerate(chips):
        for a in range(n):
            copy(a, 1 + j, (*chip, c), me).wait_recv()
            passed.append(copy(a, 4 + j, (*chip, c), sibling))
            passed[-1].start()
    for a in range(n):
        copy(a, 0, sibling, me).wait_recv()
    for j, chip in enumerate(chips):
        for a in range(n):
            copy(a, 4 + j, (*chip, 1 - c), me).wait_recv()
    for cp in firsts + passed:
        cp.wait_send()
    for cp in mines:
        cp.wait()


def _allgather_sems(n):
    return [pltpu.SemaphoreType.DMA((n, 7)), pltpu.SemaphoreType.DMA((n, 7)), pltpu.SemaphoreType.DMA((n,))]


def _allgather(xs, *, name):
    n = len(xs)

    def body(*refs):
        _allgather_body(refs[:n], refs[n:2 * n], *refs[2 * n:], own_barrier=False)

    return pl.pallas_call(
        body, out_shape=[jax.ShapeDtypeStruct((N_DEV,) + xb.shape, xb.dtype) for xb in xs],
        in_specs=[_HBM] * n, out_specs=[_HBM] * n, scratch_shapes=_allgather_sems(n), name=name)(*xs)


def _allgather_async(xs, *, name, collective_id):
    n = len(xs)
    x_refs = [jax.new_ref(xb, memory_space=pltpu.MemorySpace.HBM) for xb in xs]
    out_refs = [jax.empty_ref(jax.ShapeDtypeStruct((N_DEV,) + xb.shape, xb.dtype), memory_space=pltpu.MemorySpace.HBM)
                for xb in xs]

    @pl.kernel(mesh=plsc.ScalarSubcoreMesh(axis_name="sequencer", num_cores=1), name=name,
               scratch_types=tuple(_allgather_sems(n)),
               compiler_params=pltpu.CompilerParams(collective_id=collective_id))
    def launch(send_sems, recv_sems, local_sems):
        _allgather_body(x_refs, out_refs, send_sems, recv_sems, local_sems, own_barrier=True)

    launch()
    return [r[...] for r in out_refs]


def _pair_exchange(blocks, *, name, collective_id):
    n = len(blocks)
    hbm = pltpu.MemorySpace.HBM
    in_refs = [jax.new_ref(b, memory_space=hbm) for b in blocks]
    out_refs = [jax.empty_ref(jax.ShapeDtypeStruct((4,) + b.shape[1:], b.dtype), memory_space=hbm) for b in blocks]

    @pl.kernel(mesh=plsc.ScalarSubcoreMesh(axis_name="sequencer", num_cores=1), name=name,
               scratch_types=(pltpu.SemaphoreType.DMA((n, 4)), pltpu.SemaphoreType.DMA((n, 4))),
               compiler_params=pltpu.CompilerParams(collective_id=collective_id))
    def launch(send_sems, recv_sems):
        x, y, c = _axes()
        _handshake([(x, y, 1 - c)])
        copies = [pltpu.make_async_remote_copy(
            src_ref=in_refs[a].at[2 * s + (1 - c)], dst_ref=out_refs[a].at[s], send_sem=send_sems.at[a, s],
            recv_sem=recv_sems.at[a, s], device_id=(x, y, 1 - c), device_id_type=MESH)
            for a in range(n) for s in range(4)]
        for cp in copies:
            cp.start()
        for cp in copies:
            cp.wait_recv()
        for cp in copies:
            cp.wait_send()

    launch()
    return [r[...] for r in out_refs]


def _chip_exchange(parts, *, name, collective_id):
    n = len(parts)
    hbm = pltpu.MemorySpace.HBM
    in_refs = [jax.new_ref(p, memory_space=hbm) for p in parts]
    out_refs = [jax.empty_ref(jax.ShapeDtypeStruct(p.shape, p.dtype), memory_space=hbm) for p in parts]

    @pl.kernel(mesh=plsc.ScalarSubcoreMesh(axis_name="sequencer", num_cores=1), name=name,
               scratch_types=(pltpu.SemaphoreType.DMA((n, 3)), pltpu.SemaphoreType.DMA((n, 3)),
                              pltpu.SemaphoreType.DMA((n,))),
               compiler_params=pltpu.CompilerParams(collective_id=collective_id))
    def launch(send_sems, recv_sems, local_sems):
        x, y, c = _axes()
        my_chip = 2 * x + y
        chips = [(1 - x, y), (x, 1 - y), (1 - x, 1 - y)]
        _handshake([(*chip, c) for chip in chips])
        locals_ = [pltpu.make_async_copy(in_refs[a].at[my_chip], out_refs[a].at[my_chip], local_sems.at[a])
                   for a in range(n)]
        for cp in locals_:
            cp.start()
        copies = [pltpu.make_async_remote_copy(
            src_ref=in_refs[a].at[2 * px + py], dst_ref=out_refs[a].at[my_chip], send_sem=send_sems.at[a, k],
            recv_sem=recv_sems.at[a, k], device_id=(px, py, c), device_id_type=MESH)
            for a in range(n) for k, (px, py) in enumerate(chips)]
        for cp in copies:
            cp.start()
        for cp in copies:
            cp.wait_recv()
        for cp in copies:
            cp.wait_send()
        for cp in locals_:
            cp.wait()

    launch()
    return [r[...] for r in out_refs]


def _pair_sum(core, blocks, from_sibling, *, name):
    n = len(blocks)

    def body(core_ref, *refs):
        for a_ref, b_ref, o_ref in zip(refs[:n], refs[n:2 * n], refs[2 * n:]):
            o_ref[...] = (a_ref[...].astype(F32) + b_ref[...].astype(F32)).astype(o_ref.dtype)

    mine = lambda b: pl.BlockSpec((1,) + b.shape[1:], lambda s, core_ref: (2 * s + core_ref[0], 0, 0))
    slot = lambda b: pl.BlockSpec((1,) + b.shape[1:], lambda s, core_ref: (s, 0, 0))
    return pl.pallas_call(
        body,
        grid_spec=pltpu.PrefetchScalarGridSpec(
            num_scalar_prefetch=1, grid=(4,),
            in_specs=[mine(b) for b in blocks] + [slot(b) for b in blocks], out_specs=[slot(b) for b in blocks]),
        out_shape=[jax.ShapeDtypeStruct(s.shape, s.dtype) for s in from_sibling], name=name,
        compiler_params=_cparams("parallel"))(core, *blocks, *from_sibling)


def _sum_blocks(a, *, name, tr=None):
    n, r, cdim = a.shape
    tr = tr or r

    def body(a_ref, o_ref):
        acc = a_ref[0].astype(F32)
        for k in range(1, n):
            acc = acc + a_ref[k].astype(F32)
        o_ref[...] = acc

    return pl.pallas_call(body, grid=(r // tr,), in_specs=[pl.BlockSpec((n, tr, cdim), lambda i: (0, i, 0))],
                          out_specs=pl.BlockSpec((tr, cdim), lambda i: (i, 0)),
                          out_shape=jax.ShapeDtypeStruct((r, cdim), F32), name=name,
                          compiler_params=_cparams("parallel"))(a)


def _sum_gathered(gathered, *, name):
    n = len(gathered)

    def body(*refs):
        for a_ref, o_ref in zip(refs[:n], refs[n:]):
            acc = a_ref[0]
            for k in range(1, N_DEV):
                acc = acc + a_ref[k]
            o_ref[...] = acc

    return pl.pallas_call(body, out_shape=[jax.ShapeDtypeStruct(g.shape[1:], F32) for g in gathered], name=name,
                          compiler_params=pltpu.CompilerParams(vmem_limit_bytes=VMEM_LIMIT))(*gathered)


_ADA_SHARD = 6 * D_MODEL // N_DEV


def _ada_mod(c_all, ada_w, *, name):
    def body(c_ref, w_ref, o_ref):
        o_ref[0] = jnp.dot(_silu(c_ref[...]).astype(BF16), w_ref[0].astype(BF16), preferred_element_type=F32)

    return pl.pallas_call(
        body, grid=(DEPTH,),
        in_specs=[pl.BlockSpec((N_DEV, D_MODEL), lambda l: (0, 0)),
                  pl.BlockSpec((1, D_MODEL, _ADA_SHARD), lambda l: (l, 0, 0))],
        out_specs=pl.BlockSpec((1, N_DEV, _ADA_SHARD), lambda l: (l, 0, 0)),
        out_shape=jax.ShapeDtypeStruct((DEPTH, N_DEV, _ADA_SHARD), F32), name=name,
        compiler_params=_cparams("parallel"))(c_all, ada_w)


def _ada_wgrad(c_all, dmod_cols, *, name):
    def body(c_ref, d_ref, o_ref):
        o_ref[0] = lax.dot_general(_silu(c_ref[...]), d_ref[0], (((0,), (0,)), ((), ())),
                                   preferred_element_type=F32, precision=lax.Precision.HIGHEST)

    return pl.pallas_call(
        body, grid=(DEPTH,),
        in_specs=[pl.BlockSpec((N_DEV, D_MODEL), lambda l: (0, 0)),
                  pl.BlockSpec((1, N_DEV, _ADA_SHARD), lambda l: (l, 0, 0))],
        out_specs=pl.BlockSpec((1, D_MODEL, _ADA_SHARD), lambda l: (l, 0, 0)),
        out_shape=jax.ShapeDtypeStruct((DEPTH, D_MODEL, _ADA_SHARD), F32), name=name,
        compiler_params=_cparams("parallel"))(c_all, dmod_cols)


def _add_rows(a, b, *, name):
    def body(a_ref, b_ref, o_ref):
        o_ref[...] = a_ref[...] + b_ref[...]

    return pl.pallas_call(body, out_shape=jax.ShapeDtypeStruct(a.shape, a.dtype), name=name)(a, b)


def _adamw_update(w_ref, g_ref, m_ref, v_ref, d_ref, mo_ref, vo_ref):
    gv = g_ref[...]
    mn = ADAM_B1 * m_ref[...] + (1.0 - ADAM_B1) * gv
    vn = ADAM_B2 * v_ref[...] + (1.0 - ADAM_B2) * (gv * gv)
    mo_ref[...] = mn
    vo_ref[...] = vn
    m_hat = mn / (1.0 - ADAM_B1 ** ADAM_STEP)
    v_hat = vn / (1.0 - ADAM_B2 ** ADAM_STEP)
    d_ref[...] = -ADAM_LR * (m_hat / (jnp.sqrt(v_hat) + ADAM_EPS) + ADAM_WD * w_ref[...])


def _adamw_small(ws, gs, ms, vs, *, name):
    n = len(ws)

    def body(*refs):
        ins, outs = refs[:4 * n], refs[4 * n:]
        for i in range(n):
            _adamw_update(ins[i], ins[n + i], ins[2 * n + i], ins[3 * n + i], outs[i], outs[n + i], outs[2 * n + i])

    shapes = [jax.ShapeDtypeStruct(a.shape, F32) for a in ws]
    outs = pl.pallas_call(body, out_shape=shapes * 3, name=name,
                          compiler_params=pltpu.CompilerParams(vmem_limit_bytes=VMEM_LIMIT))(*ws, *gs, *ms, *vs)
    return outs[:n], outs[n:2 * n], outs[2 * n:]


def _adamw(w, g, m, v, *, name, tr):
    r, cdim = w.shape
    body = functools.partial(_adamw_update)

    spec = pl.BlockSpec((tr, cdim), lambda i: (i, 0))
    shp = jax.ShapeDtypeStruct((r, cdim), F32)
    return pl.pallas_call(body, grid=(r // tr,), in_specs=[spec] * 4, out_specs=[spec] * 3, out_shape=[shp] * 3,
                          name=name, compiler_params=_cparams("parallel"))(w, g, m, v)


def _pad_rows(a, rows):
    return jnp.concatenate([a, jnp.zeros((rows - a.shape[0],) + a.shape[1:], a.dtype)], axis=0)


def _pad_lanes(a, lanes):
    return jnp.concatenate([a, jnp.zeros(a.shape[:-1] + (lanes - a.shape[-1],), a.dtype)], axis=-1)


def _permute_w_in(wt):
    return jnp.concatenate([wt[512:1536], wt[:512], wt[1536:1544],
                            jnp.zeros((PROJ_W - IN_W, wt.shape[1]), wt.dtype), wt[1544:]], axis=0)


def _unpermute_w_in(wp):
    return jnp.concatenate([wp[1024:1536], wp[:1024], wp[1536:1544], wp[PROJ_SSD_W:]], axis=0)


def _block_diag(w):
    rows = []
    for g in range(4):
        rows.append(jnp.concatenate([w[g] if k == g else jnp.zeros_like(w[g]) for k in range(4)], axis=1))
    return jnp.concatenate(rows, axis=0)


def _diag_blocks(wbd):
    return jnp.stack([wbd[64 * g:64 * (g + 1), 64 * g:64 * (g + 1)] for g in range(4)], axis=0)


def _layer_params(l, small):
    return dict(
        norm1_g=small["norm1_g"][l][None], norm2_g=small["norm2_g"][l][None],
        conv_w=_pad_rows(small["ssd_conv_w"][l], 8), conv_b=small["ssd_conv_b"][l][None],
        dt_bias=_pad_lanes(small["ssd_dt_bias"][l][None], 128), a_log=_pad_lanes(small["ssd_a_log"][l][None], 128),
        d_skip=_pad_lanes(small["ssd_d"][l][None], 128), ssd_norm_g=small["ssd_norm_g"][l][None],
        pool_bd=_block_diag(small["pool_w"][l]), pool_scale=small["pool_scale"][l][None],
        fcw=_pad_rows(small["ffn_conv_w"][l], 8), fcb=small["ffn_conv_b"][l][None])


def _mod_rows(mod_l):
    return [mod_l[None, D_MODEL * i:D_MODEL * (i + 1)] for i in range(6)]


def _layer_fwd(x, mod_l, p, tabs, l, gather):
    sh1, sc1, g1, sh2, sc2, g2 = _mod_rows(mod_l)
    mix_w = gather(l, "mix", None)
    p.update(w_in=mix_w["w_in"], w_out=mix_w["w_out"])
    proj, h1 = _mm(x, p["w_in"], nt=True, norm=(p["norm1_g"], sc1, sh1), name=f"l{l}_proj")
    ffn_w = gather(l, "ffn", proj)
    p.update(up=ffn_w["ffn_up"], down=ffn_w["ffn_down"])
    mix, hst = _ssd_fwd(proj, p["conv_w"], p["conv_b"], p["dt_bias"], p["a_log"], p["d_skip"], p["ssd_norm_g"],
                        name=f"l{l}_ssd")
    mix = _pool_fwd(proj, p["pool_bd"], p["pool_scale"], mix, name=f"l{l}_pool")
    y_att, lse, mix = _att_fwd(proj, *tabs, mix, name=f"l{l}_att")
    mo, x1 = _mm(mix, p["w_out"], residual=(x, g1), name=f"l{l}_out")
    gather(l + 1, "mix", (x1, p["up"]))
    h2, hid, act, f, x2 = _ffn_fwd(x1, p["norm2_g"], sc2, sh2, g2, p["up"], p["down"], p["fcw"], p["fcb"],
                                   name=f"l{l}_ffn")
    return x2, dict(x=x, h1=h1, proj=proj, hst=hst, y_att=y_att, lse=lse, mix=mix, mo=mo, x1=x1, h2=h2, hid=hid,
                    act=act, f=f)


def _layer_bwd(dx2, sv, mod_l, p, tabs, l, exchange):
    sh1, sc1, g1, sh2, sc2, g2 = _mod_rows(mod_l)
    df, dg2, dhid, dh2, dfcw, dfcb = _ffn_bwd(dx2, g2, sv["f"], sv["hid"], p["up"], p["down"], p["fcw"], p["fcb"],
                                              name=f"l{l}_ffn_b")
    d_down = _wgrad(sv["act"], df, tk=1408, name=f"l{l}_down_bw")
    d_up = _wgrad(dhid, sv["h2"], tk=1408, name=f"l{l}_up_bw")
    exchange(l, "ffn", dict(ffn_up=d_up, ffn_down=d_down))
    dx1, dn2, dsc2, dsh2, dmix, dmo, dg1 = _mid_bwd(sv["x1"], dh2, dx2, p["norm2_g"], sc2, g1, sv["mo"], p["w_out"],
                                                    name=f"l{l}_mid_b")
    d_wout = _wgrad(sv["mix"], dmo, name=f"l{l}_out_bw")
    dproj, dcw, dcb, ddb, dal, dd, dng = _ssd_bwd(
        sv["proj"], sv["hst"], dmix, p["conv_w"], p["conv_b"], p["dt_bias"], p["a_log"], p["d_skip"],
        p["ssd_norm_g"], name=f"l{l}_ssd_b")
    dproj, dwbd, dpsc = _pool_bwd(sv["proj"], dmix, p["pool_bd"], p["pool_scale"], dproj, name=f"l{l}_pool_b")
    dproj = _att_bwd(sv["proj"], *tabs, sv["y_att"], sv["lse"], dmix, dproj, name=f"l{l}_att_b")
    d_win = _wgrad(dproj, sv["h1"], tk=1408, name=f"l{l}_proj_bw")
    exchange(l, "mix", dict(w_in=d_win, w_out=d_wout))
    dx0, dn1, dsc1, dsh1 = _norm_mod_bwd(sv["x"], dproj, dx1, p["norm1_g"], sc1, w=p["w_in"], name=f"l{l}_proj_b")
    dmod = jnp.concatenate([dsh1, dsc1, dg1, dsh2, dsc2, dg2], axis=1)[0]
    small = dict(norm1_g=dn1[0], ssd_conv_w=dcw[:SSD_CONV_K], ssd_conv_b=dcb[0], ssd_dt_bias=ddb[0], ssd_a_log=dal[0],
                 ssd_d=dd[0], ssd_norm_g=dng[0], pool_w=_diag_blocks(dwbd), pool_scale=dpsc[0], norm2_g=dn2[0],
                 ffn_conv_w=dfcw[:FFN_CONV_K], ffn_conv_b=dfcb[0])
    return dx0, dmod, small


def _example_step(x, target, pos_col, inv_freq_lane, mod, gather, small, final_g, exchange):
    tabs = _rope_tables(pos_col, inv_freq_lane, name="rope_tables")
    params, saved = [], []
    for l in range(DEPTH):
        params.append(_layer_params(l, small))
        x, sv = _layer_fwd(x, mod[l], params[l], tabs, l, gather)
        saved.append(sv)
    loss_row, dx, dfg = _final_loss(x, final_g[None], target, name="final_loss")
    dmods, smalls = [None] * DEPTH, [None] * DEPTH
    for l in reversed(range(DEPTH)):
        dx, dmods[l], smalls[l] = _layer_bwd(dx, saved[l], mod[l], params[l], tabs, l, exchange)
    return loss_row, dx, jnp.stack(dmods, axis=0), smalls, dfg[0]


_BIG = ("w_in", "w_out", "ffn_up", "ffn_down")
_SMALL_GRADS = ("norm1_g", "ssd_conv_w", "ssd_conv_b", "ssd_dt_bias", "ssd_a_log", "ssd_d", "ssd_norm_g", "pool_w",
                "pool_scale", "norm2_g", "ffn_conv_w", "ffn_conv_b")
_SMALL_PARAMS = ("ada_b", "norm1_g", "ssd_conv_w", "ssd_conv_b", "ssd_dt_bias", "ssd_a_log", "ssd_d", "ssd_norm_g",
                 "pool_w", "pool_scale", "norm2_g", "ffn_conv_w", "ffn_conv_b", "final_g")
_WEIGHT_ORDER = ("ada_w", "ada_b", "norm1_g", "w_in", "ssd_conv_w", "ssd_conv_b", "ssd_dt_bias", "ssd_a_log", "ssd_d",
                 "ssd_norm_g", "pool_w", "pool_scale", "w_out", "norm2_g", "ffn_up", "ffn_conv_w", "ffn_conv_b",
                 "ffn_down", "final_g")


_COLUMN_SHARDED = ("w_in", "ffn_up")
_GROUPS = (("mix", ("w_in", "w_out")), ("ffn", ("ffn_up", "ffn_down")))


def _big_shares(w, l, names):
    return [(w[name][l].T if name in _COLUMN_SHARDED else w[name][l]).astype(BF16) for name in names]


def _unshard_big(names, gathered):
    out = {}
    for name, g in zip(names, gathered):
        full = g.reshape(N_DEV * g.shape[1], g.shape[2])
        out[name] = _permute_w_in(full) if name == "w_in" else full
    return out


def _shard_big(grads):
    out = []
    for name, g in grads.items():
        g = _unpermute_w_in(g) if name == "w_in" else g
        out.append(g.reshape(N_DEV, g.shape[0] // N_DEV, g.shape[1]))
    return out


def kernel(x, c, positions, ada_w, ada_b, norm1_g, w_in, ssd_conv_w, ssd_conv_b, ssd_dt_bias, ssd_a_log, ssd_d, ssd_norm_g, pool_w, pool_scale, w_out, norm2_g, ffn_up, ffn_conv_w, ffn_conv_b, ffn_down, final_g, loss_target, m_ada_w, m_ada_b, m_norm1_g, m_w_in, m_ssd_conv_w, m_ssd_conv_b, m_ssd_dt_bias, m_ssd_a_log, m_ssd_d, m_ssd_norm_g, m_pool_w, m_pool_scale, m_w_out, m_norm2_g, m_ffn_up, m_ffn_conv_w, m_ffn_conv_b, m_ffn_down, m_final_g, v_ada_w, v_ada_b, v_norm1_g, v_w_in, v_ssd_conv_w, v_ssd_conv_b, v_ssd_dt_bias, v_ssd_a_log, v_ssd_d, v_ssd_norm_g, v_pool_w, v_pool_scale, v_w_out, v_norm2_g, v_ffn_up, v_ffn_conv_w, v_ffn_conv_b, v_ffn_down, v_final_g):
    w = dict(ada_w=ada_w, ada_b=ada_b, norm1_g=norm1_g, w_in=w_in, ssd_conv_w=ssd_conv_w, ssd_conv_b=ssd_conv_b,
             ssd_dt_bias=ssd_dt_bias, ssd_a_log=ssd_a_log, ssd_d=ssd_d, ssd_norm_g=ssd_norm_g, pool_w=pool_w,
             pool_scale=pool_scale, w_out=w_out, norm2_g=norm2_g, ffn_up=ffn_up, ffn_conv_w=ffn_conv_w,
             ffn_conv_b=ffn_conv_b, ffn_down=ffn_down, final_g=final_g)
    m = dict(ada_w=m_ada_w, ada_b=m_ada_b, norm1_g=m_norm1_g, w_in=m_w_in, ssd_conv_w=m_ssd_conv_w,
             ssd_conv_b=m_ssd_conv_b, ssd_dt_bias=m_ssd_dt_bias, ssd_a_log=m_ssd_a_log, ssd_d=m_ssd_d,
             ssd_norm_g=m_ssd_norm_g, pool_w=m_pool_w, pool_scale=m_pool_scale, w_out=m_w_out, norm2_g=m_norm2_g,
             ffn_up=m_ffn_up, ffn_conv_w=m_ffn_conv_w, ffn_conv_b=m_ffn_conv_b, ffn_down=m_ffn_down,
             final_g=m_final_g)
    v = dict(ada_w=v_ada_w, ada_b=v_ada_b, norm1_g=v_norm1_g, w_in=v_w_in, ssd_conv_w=v_ssd_conv_w,
             ssd_conv_b=v_ssd_conv_b, ssd_dt_bias=v_ssd_dt_bias, ssd_a_log=v_ssd_a_log, ssd_d=v_ssd_d,
             ssd_norm_g=v_ssd_norm_g, pool_w=v_pool_w, pool_scale=v_pool_scale, w_out=v_w_out, norm2_g=v_norm2_g,
             ffn_up=v_ffn_up, ffn_conv_w=v_ffn_conv_w, ffn_conv_b=v_ffn_conv_b, ffn_down=v_ffn_down,
             final_g=v_final_g)
    ix, iy, ic = _axes()
    dev = 4 * ix + 2 * iy + ic

    c_all, scw, fcw = _allgather([c, ssd_conv_w.reshape(DEPTH * SSD_CONV_K, -1),
                                  ffn_conv_w.reshape(DEPTH * FFN_CONV_K, -1)], name="gather_small")
    small_all = c_all
    c_all = c_all.reshape(N_DEV, D_MODEL)
    scw = scw.reshape(N_DEV, DEPTH, SSD_CONV_K, -1).transpose(1, 2, 0, 3).reshape(DEPTH, SSD_CONV_K, SSD_CONV_CH)
    fcw = fcw.reshape(N_DEV, DEPTH, FFN_CONV_K, -1).transpose(1, 2, 0, 3).reshape(DEPTH, FFN_CONV_K, 2 * FFN_DIM)

    mod_cols = _ada_mod(c_all, ada_w, name="ada_mod")
    mod_all = _allgather([mod_cols.reshape(DEPTH * N_DEV, _ADA_SHARD)], name="gather_mod")[0]
    mod_all = mod_all.reshape(N_DEV, DEPTH, N_DEV, _ADA_SHARD)
    mod_mine = lax.dynamic_index_in_dim(mod_all, dev, axis=2, keepdims=False)
    mod = _add_rows(mod_mine.transpose(1, 0, 2).reshape(DEPTH, 6 * D_MODEL), ada_b, name="ada_bias")

    fetched = {}

    def gather(l, group, after):
        if l < DEPTH and (l, group) not in fetched:
            names = dict(_GROUPS)[group]
            shares, _ = lax.optimization_barrier((_big_shares(w, l, names), small_all if after is None else after))
            got = _allgather_async(shares, name=f"gather_weights_l{l}_{group}",
                                   collective_id=1 + 2 * l + (group == "ffn"))
            fetched[l, group] = _unshard_big(names, got)
        return fetched.get((l, group))

    core = ic.astype(jnp.int32).reshape(1)
    from_chips = {}

    def exchange(l, group, g):
        cid = 5 + 4 * l + 2 * (group == "mix")
        blocks = _shard_big(g)
        from_sibling = _pair_exchange(blocks, name=f"grads_pair_exchange_l{l}_{group}", collective_id=cid)
        parts = _pair_sum(core, blocks, from_sibling, name=f"grads_pair_sum_l{l}_{group}")
        got = _chip_exchange(parts, name=f"grads_chip_exchange_l{l}_{group}", collective_id=cid + 1)
        from_chips.update({(l, name): t for name, t in zip(g, got)})

    small = dict(norm1_g=norm1_g, norm2_g=norm2_g, ssd_conv_w=scw, ssd_conv_b=ssd_conv_b, ssd_dt_bias=ssd_dt_bias,
                 ssd_a_log=ssd_a_log, ssd_d=ssd_d, ssd_norm_g=ssd_norm_g, pool_w=pool_w, pool_scale=pool_scale,
                 ffn_conv_w=fcw, ffn_conv_b=ffn_conv_b)

    inv_freq = ROPE_THETA ** (-jnp.arange(0, ROT_DIM, 2, dtype=F32) / ROT_DIM)
    lane = jnp.arange(128) % HEAD_LANES
    inv_freq_lane = jnp.where(lane < ROT_DIM, inv_freq[lane % (ROT_DIM // 2)], 0.0)[None, :]
    pos_col = positions.reshape(SEQ, 1).astype(F32)
    loss_row, dx, dmod, g_small, g_final = _example_step(
        x[0], loss_target[0], pos_col, inv_freq_lane, mod, gather, small, final_g, exchange)

    grads = {}
    for name in _BIG:
        per_layer = [_sum_blocks(from_chips[l, name], name=f"grads_chip_sum_l{l}_{name}") for l in range(DEPTH)]
        grads[name] = jnp.stack([g.T if name in _COLUMN_SHARDED else g for g in per_layer], axis=0)

    small_names = list(_SMALL_GRADS)
    stacked = [jnp.stack([g_small[l][name] for l in range(DEPTH)], axis=0) for name in small_names]
    small_parts = [loss_row, dmod] + [s.reshape(-1, s.shape[-1]) for s in stacked] + [g_final[None]]
    gathered = _allgather_async(small_parts, name="gather_small_grads", collective_id=13)
    total = _sum_gathered(gathered, name="sum_small_grads")
    loss = total[0][0, 0]
    grads["ada_b"] = total[1]
    grads.update(zip(small_names, total[2:-1]))
    grads["final_g"] = total[-1][0]
    dmod_cols = lax.dynamic_slice_in_dim(gathered[1], dev * _ADA_SHARD, _ADA_SHARD, axis=2).transpose(1, 0, 2)
    grads["ada_w"] = _ada_wgrad(c_all, dmod_cols, name="ada_wgrad")
    for name in ("ssd_dt_bias", "ssd_a_log", "ssd_d"):
        grads[name] = grads[name][:, :SSD_HEADS]
    grads["pool_w"] = grads["pool_w"].reshape(pool_w.shape)
    grads["ssd_conv_w"] = lax.dynamic_slice_in_dim(
        grads["ssd_conv_w"].reshape(DEPTH, SSD_CONV_K, SSD_CONV_CH), dev * ssd_conv_w.shape[2], ssd_conv_w.shape[2], axis=2)
    grads["ffn_conv_w"] = lax.dynamic_slice_in_dim(
        grads["ffn_conv_w"].reshape(DEPTH, FFN_CONV_K, 2 * FFN_DIM), dev * ffn_conv_w.shape[2], ffn_conv_w.shape[2], axis=2)

    delta, new_m, new_v = {}, {}, {}
    for name, tr in (("ada_w", 512), ("w_in", 512), ("w_out", 256), ("ffn_up", 512), ("ffn_down", 352)):
        shp = w[name].shape
        two_d = lambda a: a.reshape(shp[0] * shp[1], shp[2])
        d_, m_, v_ = _adamw(two_d(w[name]), two_d(grads[name]), two_d(m[name]), two_d(v[name]), tr=tr,
                               name=f"adamw_{name}")
        delta[name], new_m[name], new_v[name] = (t.reshape(shp) for t in (d_, m_, v_))
    two_d = lambda a: a.reshape(-1, a.shape[-1])
    outs = _adamw_small(*[[two_d(t[name]) for name in _SMALL_PARAMS] for t in (w, grads, m, v)], name="adamw_small")
    for name, d_, m_, v_ in zip(_SMALL_PARAMS, *outs):
        delta[name], new_m[name], new_v[name] = (t.reshape(w[name].shape) for t in (d_, m_, v_))

    grad_x = dx[None]
    return (loss, grad_x, *[grads[n].reshape(w[n].shape) for n in _WEIGHT_ORDER],
            *[delta[n] for n in _WEIGHT_ORDER], *[new_m[n] for n in _WEIGHT_ORDER],
            *[new_v[n] for n in _WEIGHT_ORDER])
```

```python
import functools
import math

import jax
import jax.numpy as jnp
from jax import lax
from jax.experimental import pallas as pl
from jax.experimental.pallas import tpu as pltpu
from jax.experimental.pallas import tpu_sc as plsc

F32 = jnp.float32
BF16 = jnp.bfloat16

N_DEV = 8
D_MODEL = 1024
SEQ = 4096
DEPTH = 2
SSD_INNER = 512
SSD_HEADS = 8
SSD_HPG = 4
SSD_STATE = 128
SSD_CHUNK = 256
SSD_CONV_K = 4
SSD_CONV_CH = 1024
POOL_W = 256
POOL_WINDOWS = (2, 4, 8, 16)
ATT_W = 256
ATT_PATTERNS = ((128, 1), (512, 4), (2048, 16))
ATT_BLOCK = 128
ROT_DIM = 16
ROPE_THETA = 500000.0
IN_W = 2568
FFN_DIM = 2816
FFN_CONV_K = 3
NORM_EPS = 1e-6
HEAD_LANES = 64

ADAM_LR = 0.001
ADAM_B1 = 0.9
ADAM_B2 = 0.999
ADAM_EPS = 1e-08
ADAM_WD = 0.01
ADAM_STEP = 10

PROJ_W = 2816
PROJ_SSD_W = 1792
PROJ_Z_BLK = 2
PROJ_DT_BLK = 12
PROJ_POOL_BLK = 7
PROJ_Q_BLK, PROJ_K_BLK, PROJ_V_BLK = 16, 18, 20
MIX_POOL_BLK = 2
MIX_ATT_BLK = 6
VMEM_LIMIT = 56 * 1024 * 1024
ROW_TILE = 512
CONV_HALO = 8
POOL_HALO = 16
ATT_KPAD = ATT_BLOCK * 16
MESH = pl.DeviceIdType.MESH
_HBM = pl.BlockSpec(memory_space=pl.ANY)


def _cparams(*sem):
    return pltpu.CompilerParams(dimension_semantics=sem, vmem_limit_bytes=VMEM_LIMIT)


def _resident(shape):
    return pl.BlockSpec(shape, lambda i: (0,) * len(shape), pipeline_mode=pl.Buffered(1))


def _silu(x):
    return x * jax.nn.sigmoid(x)


def _pick_lane(v, h):
    lane = lax.broadcasted_iota(jnp.int32, v.shape, 1)
    return jnp.sum(jnp.where(lane == h, v, 0.0), axis=1, keepdims=True)


def _pick_row(v, h):
    row = lax.broadcasted_iota(jnp.int32, v.shape, 0)
    return jnp.sum(jnp.where(row == h, v, 0.0), axis=0, keepdims=True)


def _head_of_lane(width):
    return lax.broadcasted_iota(jnp.int32, (1, width), 1) // HEAD_LANES


@functools.partial(jax.custom_vjp, nondiff_argnums=(1, 2))
def _shift_rows(x_ext, s, halo):
    y = x_ext if s == 0 else pltpu.roll(x_ext, s, 0)
    return y[halo:]


def _shift_rows_fwd(x_ext, s, halo):
    return _shift_rows(x_ext, s, halo), None


def _shift_rows_bwd(s, halo, _, g):
    ge = jnp.concatenate([jnp.zeros((halo, g.shape[1]), g.dtype), g], axis=0)
    return (ge if s == 0 else pltpu.roll(ge, ge.shape[0] - s, 0),)


_shift_rows.defvjp(_shift_rows_fwd, _shift_rows_bwd)


@functools.partial(jax.custom_vjp, nondiff_argnums=(1,))
def _roll_rows(x, s):
    return pltpu.roll(x, s, 0)


def _roll_rows_fwd(x, s):
    return _roll_rows(x, s), None


def _roll_rows_bwd(s, _, g):
    return (pltpu.roll(g, g.shape[0] - s, 0),)


_roll_rows.defvjp(_roll_rows_fwd, _roll_rows_bwd)


def _rms_modulate(xv, g, sc, sh):
    r = lax.rsqrt(jnp.mean(xv * xv, axis=-1, keepdims=True) + NORM_EPS)
    return (xv * r * g) * (1.0 + sc) + sh


def _mm(a, w, *, name, nt=False, tm=ROW_TILE, tn=None, out_dtype=F32, norm=None, residual=None):
    t, k = a.shape
    n = w.shape[0] if nt else w.shape[1]
    tn = tn or n
    assert tn == n or (norm is None and residual is None)
    extra_in = list(norm or ()) + list(residual or ())

    def body(*refs):
        a_ref, w_ref = refs[:2]
        ins = refs[2:2 + len(extra_in)]
        outs = refs[2 + len(extra_in):]
        if norm is None:
            av = a_ref[...].astype(BF16)
        else:
            av = _rms_modulate(a_ref[...], ins[0][...], ins[1][...], ins[2][...]).astype(BF16)
            outs[1][...] = av
        if nt:
            acc = lax.dot_general(av, w_ref[...], (((1,), (1,)), ((), ())), preferred_element_type=F32)
        else:
            acc = jnp.dot(av, w_ref[...], preferred_element_type=F32)
        outs[0][...] = acc.astype(out_dtype)
        if residual is not None:
            x_ref, gate_ref = ins[-2:]
            outs[-1][...] = x_ref[...] + gate_ref[...] * acc

    row = lambda width: pl.BlockSpec((1, width), lambda i, j: (0, 0))
    tile = lambda width: pl.BlockSpec((tm, width), lambda i, j: (i, 0))
    w_spec = pl.BlockSpec((tn, k), lambda i, j: (j, 0)) if nt else pl.BlockSpec((k, tn), lambda i, j: (0, j))
    in_specs = [tile(k), w_spec] + ([row(k)] * 3 if norm else []) + ([tile(n), row(n)] if residual else [])
    out_specs = [pl.BlockSpec((tm, tn), lambda i, j: (i, j))] + ([tile(k)] if norm else []) + \
        ([tile(n)] if residual else [])
    out_shape = [jax.ShapeDtypeStruct((t, n), out_dtype)] + \
        ([jax.ShapeDtypeStruct((t, k), BF16)] if norm else []) + \
        ([jax.ShapeDtypeStruct((t, n), F32)] if residual else [])
    outs = pl.pallas_call(
        body, grid=(t // tm, n // tn), in_specs=in_specs, out_specs=out_specs, out_shape=out_shape, name=name,
        compiler_params=_cparams("parallel", "parallel"))(a, w, *extra_in)
    return outs[0] if len(outs) == 1 else outs


def _wgrad(a, b, *, name, tk=None, tn=None, tt=2048, out_dtype=BF16):
    t, k = a.shape
    n = b.shape[1]
    tk = tk or k
    tn = tn or n
    steps = t // tt

    def body(a_ref, b_ref, o_ref, acc_ref):
        s = pl.program_id(2)

        @pl.when(s == 0)
        def _():
            acc_ref[...] = jnp.zeros_like(acc_ref)

        acc_ref[...] += lax.dot_general(a_ref[...].astype(BF16), b_ref[...].astype(BF16),
                                        (((0,), (0,)), ((), ())), preferred_element_type=F32)

        @pl.when(s == steps - 1)
        def _():
            o_ref[...] = acc_ref[...].astype(out_dtype)

    return pl.pallas_call(
        body, grid=(k // tk, n // tn, steps),
        in_specs=[pl.BlockSpec((tt, tk), lambda i, j, s: (s, i)), pl.BlockSpec((tt, tn), lambda i, j, s: (s, j))],
        out_specs=pl.BlockSpec((tk, tn), lambda i, j, s: (i, j)),
        out_shape=jax.ShapeDtypeStruct((k, n), out_dtype),
        scratch_shapes=[pltpu.VMEM((tk, tn), F32)], name=name,
        compiler_params=_cparams("parallel", "parallel", "arbitrary"))(a, b)


def _norm_mod_bwd(x, dh, dres, g, sc, *, name, w=None, tm=ROW_TILE):
    s, d = x.shape
    steps = s // tm

    def body(x_ref, dh_ref, dres_ref, g_ref, sc_ref, *rest):
        w_ref = rest[0] if w is not None else None
        dx_ref, dg_ref, dsc_ref, dsh_ref, da_acc, dsh_acc = rest[-6:]
        i = pl.program_id(0)

        @pl.when(i == 0)
        def _():
            da_acc[...] = jnp.zeros_like(da_acc)
            dsh_acc[...] = jnp.zeros_like(dsh_acc)

        xv = x_ref[...]
        if w is None:
            dhv = dh_ref[...].astype(F32)
        else:
            dhv = jnp.dot(dh_ref[...], w_ref[...], preferred_element_type=F32)
        r = lax.rsqrt(jnp.mean(xv * xv, axis=-1, keepdims=True) + NORM_EPS)
        xhat = xv * r
        gain = g_ref[...] * (1.0 + sc_ref[...])
        dxhat = dhv * gain
        dx_ref[...] = dres_ref[...] + r * (dxhat - xhat * jnp.mean(dxhat * xhat, axis=-1, keepdims=True))
        da_acc[...] += jnp.sum(dhv * xhat, axis=0, keepdims=True)
        dsh_acc[...] += jnp.sum(dhv, axis=0, keepdims=True)

        @pl.when(i == steps - 1)
        def _():
            dg_ref[...] = da_acc[...] * (1.0 + sc_ref[...])
            dsc_ref[...] = da_acc[...] * g_ref[...]
            dsh_ref[...] = dsh_acc[...]

    row = pl.BlockSpec((1, d), lambda i: (0, 0))
    tile = pl.BlockSpec((tm, d), lambda i: (i, 0))
    row_shape = jax.ShapeDtypeStruct((1, d), F32)
    dh_spec = tile if w is None else pl.BlockSpec((tm, dh.shape[1]), lambda i: (i, 0))
    return pl.pallas_call(
        body, grid=(steps,), in_specs=[tile, dh_spec, tile, row, row] + ([] if w is None else [_resident(w.shape)]),
        out_specs=[tile, row, row, row],
        out_shape=[jax.ShapeDtypeStruct((s, d), F32), row_shape, row_shape, row_shape],
        scratch_shapes=[pltpu.VMEM((1, d), F32), pltpu.VMEM((1, d), F32)], name=name,
        compiler_params=_cparams("arbitrary"))(x, dh, dres, g, sc, *([] if w is None else [w]))


def _mid_bwd(x1, dh2, dx2, norm_g, sc, gate, mo, w_out, *, name, tm=ROW_TILE):
    s, d = x1.shape
    steps = s // tm

    def body(x_ref, dh_ref, dres_ref, ng_ref, sc_ref, g_ref, mo_ref, w_ref,
             dx_ref, dng_ref, dsc_ref, dsh_ref, dmix_ref, dmo_ref, dg_ref, da_acc, dsh_acc):
        i = pl.program_id(0)

        @pl.when(i == 0)
        def _():
            da_acc[...] = jnp.zeros_like(da_acc)
            dsh_acc[...] = jnp.zeros_like(dsh_acc)
            dg_ref[...] = jnp.zeros_like(dg_ref)

        xv = x_ref[...]
        dhv = dh_ref[...]
        r = lax.rsqrt(jnp.mean(xv * xv, axis=-1, keepdims=True) + NORM_EPS)
        xhat = xv * r
        dxhat = dhv * (ng_ref[...] * (1.0 + sc_ref[...]))
        dxv = dres_ref[...] + r * (dxhat - xhat * jnp.mean(dxhat * xhat, axis=-1, keepdims=True))
        dx_ref[...] = dxv
        da_acc[...] += jnp.sum(dhv * xhat, axis=0, keepdims=True)
        dsh_acc[...] += jnp.sum(dhv, axis=0, keepdims=True)
        dmo = (g_ref[...] * dxv).astype(BF16)
        dmo_ref[...] = dmo
        dg_ref[...] += jnp.sum(dxv * mo_ref[...], axis=0, keepdims=True)
        dmix_ref[...] = lax.dot_general(dmo, w_ref[...], (((1,), (1,)), ((), ())), preferred_element_type=F32)

        @pl.when(i == steps - 1)
        def _():
            dng_ref[...] = da_acc[...] * (1.0 + sc_ref[...])
            dsc_ref[...] = da_acc[...] * ng_ref[...]
            dsh_ref[...] = dsh_acc[...]

    tile = pl.BlockSpec((tm, d), lambda i: (i, 0))
    row = pl.BlockSpec((1, d), lambda i: (0, 0))
    mix_tile = pl.BlockSpec((tm, w_out.shape[0]), lambda i: (i, 0))
    row_shape = jax.ShapeDtypeStruct((1, d), F32)
    return pl.pallas_call(
        body, grid=(steps,), in_specs=[tile, tile, tile, row, row, row, tile, _resident(w_out.shape)],
        out_specs=[tile, row, row, row, mix_tile, tile, row],
        out_shape=[jax.ShapeDtypeStruct((s, d), F32), row_shape, row_shape, row_shape,
                   jax.ShapeDtypeStruct((s, w_out.shape[0]), F32), jax.ShapeDtypeStruct((s, d), BF16), row_shape],
        scratch_shapes=[pltpu.VMEM((1, d), F32), pltpu.VMEM((1, d), F32)], name=name,
        compiler_params=_cparams("arbitrary"))(x1, dh2, dx2, norm_g, sc, gate, mo, w_out)


def _final_loss(x, g, target, *, name, tm=ROW_TILE):
    s, d = x.shape
    steps = s // tm

    def body(x_ref, g_ref, t_ref, loss_ref, dx_ref, dg_ref, sq_acc):
        i = pl.program_id(0)

        @pl.when(i == 0)
        def _():
            sq_acc[...] = jnp.zeros_like(sq_acc)
            dg_ref[...] = jnp.zeros_like(dg_ref)

        xv = x_ref[...]
        r = lax.rsqrt(jnp.mean(xv * xv, axis=-1, keepdims=True) + NORM_EPS)
        xhat = xv * r
        err = xhat * g_ref[...] - t_ref[...]
        sq_acc[...] += jnp.sum(err * err, axis=0, keepdims=True)
        dy = err * (1.0 / d)
        dg_ref[...] += jnp.sum(dy * xhat, axis=0, keepdims=True)
        dxhat = dy * g_ref[...]
        dx_ref[...] = r * (dxhat - xhat * jnp.mean(dxhat * xhat, axis=-1, keepdims=True))

        @pl.when(i == steps - 1)
        def _():
            total = jnp.sum(sq_acc[...], axis=1, keepdims=True) * (0.5 / d)
            loss_ref[...] = jnp.broadcast_to(total, loss_ref.shape)

    tile = pl.BlockSpec((tm, d), lambda i: (i, 0))
    row = pl.BlockSpec((1, d), lambda i: (0, 0))
    return pl.pallas_call(
        body, grid=(steps,), in_specs=[tile, row, tile],
        out_specs=[pl.BlockSpec((1, 128), lambda i: (0, 0)), tile, row],
        out_shape=[jax.ShapeDtypeStruct((1, 128), F32), jax.ShapeDtypeStruct((s, d), F32),
                   jax.ShapeDtypeStruct((1, d), F32)],
        scratch_shapes=[pltpu.VMEM((1, d), F32)], name=name, compiler_params=_cparams("arbitrary"))(x, g, target)


def _ssd_chunk(z, xbc_ext, dt_raw, conv_w, conv_b, dt_bias, a_log, d_skip, norm_g, h_in):
    q = z.shape[0]
    gw = SSD_HPG * HEAD_LANES
    xc = conv_b
    for k in range(SSD_CONV_K):
        xc = xc + _pick_row(conv_w, k) * _shift_rows(xbc_ext, SSD_CONV_K - 1 - k, CONV_HALO)
    xc = _silu(xc)
    dt = jax.nn.softplus(dt_raw + dt_bias)
    da = dt * (-jnp.exp(a_log))
    ri = lax.broadcasted_iota(jnp.int32, (q, q), 0)
    ci = lax.broadcasted_iota(jnp.int32, (q, q), 1)
    causal = ri >= ci
    tril = causal.astype(F32)
    a_cum = jnp.dot(tril, da, preferred_element_type=F32, precision=lax.Precision.HIGHEST)
    a_cum_t = lax.dot_general(da, tril, (((0,), (1,)), ((), ())), preferred_element_type=F32,
                              precision=lax.Precision.HIGHEST)
    a_last = _pick_row(a_cum, q - 1)
    head = _head_of_lane(gw)
    ys, hs = [], []
    for g in range(2):
        xs = xc[:, gw * g:gw * (g + 1)]
        bm = xc[:, SSD_INNER + SSD_STATE * g:SSD_INNER + SSD_STATE * (g + 1)]
        cm = xc[:, SSD_INNER + 2 * SSD_STATE + SSD_STATE * g:SSD_INNER + 2 * SSD_STATE + SSD_STATE * (g + 1)]
        cb = lax.dot_general(cm.astype(BF16), bm.astype(BF16), (((1,), (1,)), ((), ())), preferred_element_type=F32)
        cols = [_pick_lane(a_cum, SSD_HPG * g + j) for j in range(SSD_HPG)]
        lasts = [_pick_lane(a_last, SSD_HPG * g + j) for j in range(SSD_HPG)]
        dt_exp = sum(jnp.where(head == j, _pick_lane(dt, SSD_HPG * g + j), 0.0) for j in range(SSD_HPG))
        d_exp = sum(jnp.where(head == j, _pick_lane(d_skip, SSD_HPG * g + j), 0.0) for j in range(SSD_HPG))
        e_cum = sum(jnp.where(head == j, jnp.exp(cols[j]), 0.0) for j in range(SSD_HPG))
        c_dec = sum(jnp.where(head == j, jnp.exp(lasts[j]), 0.0) for j in range(SSD_HPG))
        xsdt = (xs * dt_exp).astype(BF16)
        y_diag = jnp.zeros((q, gw), F32)
        st_new = jnp.zeros((SSD_STATE, gw), F32)
        for j in range(SSD_HPG):
            row = _pick_row(a_cum_t, SSD_HPG * g + j)
            lmat = jnp.exp(jnp.where(causal, cols[j] - row, -jnp.inf))
            r = jnp.dot((cb * lmat).astype(BF16), xsdt, preferred_element_type=F32)
            y_diag = y_diag + jnp.where(head == j, r, 0.0)
            bd = (bm * jnp.exp(lasts[j] - cols[j])).astype(BF16)
            st = lax.dot_general(bd, xsdt, (((0,), (0,)), ((), ())), preferred_element_type=F32)
            st_new = st_new + jnp.where(head == j, st, 0.0)
        y_off = jnp.dot(cm.astype(BF16), h_in[g].astype(BF16), preferred_element_type=F32) * e_cum
        hs.append(h_in[g] * c_dec + st_new)
        y = y_diag + y_off + d_exp * xs
        yz = y * _silu(z[:, gw * g:gw * (g + 1)])
        yz = yz * lax.rsqrt(jnp.mean(yz * yz, axis=-1, keepdims=True) + NORM_EPS)
        ys.append(yz * norm_g[:, gw * g:gw * (g + 1)])
    return jnp.concatenate(ys, axis=1), tuple(hs)


_SSD_NCHUNK = SEQ // SSD_CHUNK
_HALO_PER_CHUNK = SSD_CHUNK // CONV_HALO


def _ssd_param_specs(const):
    return [pl.BlockSpec((8, SSD_CONV_CH), const), pl.BlockSpec((1, SSD_CONV_CH), const),
            pl.BlockSpec((1, 128), const), pl.BlockSpec((1, 128), const), pl.BlockSpec((1, 128), const),
            pl.BlockSpec((1, SSD_INNER), const)]


def _ssd_fwd(proj, conv_w, conv_b, dt_bias, a_log, d_skip, norm_g, *, name):
    q = SSD_CHUNK

    def body(z_ref, xbc_ref, halo_ref, dt_ref, cw_ref, cb_ref, db_ref, al_ref, d_ref, ng_ref, y_ref, hs_ref, h_acc):
        i = pl.program_id(0)

        @pl.when(i == 0)
        def _():
            h_acc[...] = jnp.zeros_like(h_acc)

        halo = jnp.where(i == 0, 0.0, halo_ref[...])
        xbc_ext = jnp.concatenate([halo, xbc_ref[...]], axis=0)
        h_in = (h_acc[0], h_acc[1])
        hs_ref[0, 0] = h_in[0]
        hs_ref[0, 1] = h_in[1]
        y, h_out = _ssd_chunk(z_ref[...], xbc_ext, dt_ref[...], cw_ref[...], cb_ref[...], db_ref[...], al_ref[...],
                              d_ref[...], ng_ref[...], h_in)
        y_ref[...] = y.astype(BF16)
        h_acc[0] = h_out[0]
        h_acc[1] = h_out[1]

    const = lambda i: (0, 0)
    return pl.pallas_call(
        body, grid=(_SSD_NCHUNK,),
        in_specs=[pl.BlockSpec((q, SSD_INNER), lambda i: (i, PROJ_Z_BLK)),
                  pl.BlockSpec((q, SSD_CONV_CH), lambda i: (i, 0)),
                  pl.BlockSpec((CONV_HALO, SSD_CONV_CH), lambda i: (jnp.maximum(i * _HALO_PER_CHUNK - 1, 0), 0)),
                  pl.BlockSpec((q, 128), lambda i: (i, PROJ_DT_BLK))] + _ssd_param_specs(const),
        out_specs=[pl.BlockSpec((q, SSD_INNER), lambda i: (i, 0)),
                   pl.BlockSpec((1, 2, SSD_STATE, 256), lambda i: (i, 0, 0, 0))],
        out_shape=[jax.ShapeDtypeStruct((SEQ, D_MODEL), BF16),
                   jax.ShapeDtypeStruct((_SSD_NCHUNK, 2, SSD_STATE, 256), F32)],
        scratch_shapes=[pltpu.VMEM((2, SSD_STATE, 256), F32)], name=name,
        compiler_params=_cparams("arbitrary"))(proj, proj, proj, proj, conv_w, conv_b, dt_bias, a_log, d_skip, norm_g)


def _ssd_bwd(proj, hstates, dmix, conv_w, conv_b, dt_bias, a_log, d_skip, norm_g, *, name):
    q = SSD_CHUNK
    last = _SSD_NCHUNK - 1

    def body(z_ref, xbc_ref, halo_ref, dt_ref, hs_ref, dy_ref, cw_ref, cb_ref, db_ref, al_ref, d_ref, ng_ref,
             dp_ref, dcw_ref, dcb_ref, ddb_ref, dal_ref, dd_ref, dng_ref, dh_acc, dhalo_acc):
        i = pl.program_id(0)

        @pl.when(i == 0)
        def _():
            dh_acc[...] = jnp.zeros_like(dh_acc)
            dhalo_acc[...] = jnp.zeros_like(dhalo_acc)
            for r in (dcw_ref, dcb_ref, ddb_ref, dal_ref, dd_ref, dng_ref):
                r[...] = jnp.zeros_like(r)

        halo = jnp.where(i == last, 0.0, halo_ref[...])
        xbc_ext = jnp.concatenate([halo, xbc_ref[...]], axis=0)
        _, vjp = jax.vjp(_ssd_chunk, z_ref[...], xbc_ext, dt_ref[...], cw_ref[...], cb_ref[...], db_ref[...],
                         al_ref[...], d_ref[...], ng_ref[...], (hs_ref[0, 0], hs_ref[0, 1]))
        gz, gx, gdt, gcw, gcb, gdb, gal, gd, gng, gh = vjp((dy_ref[...], (dh_acc[0], dh_acc[1])))
        dxbc = jnp.concatenate([gx[CONV_HALO:q], gx[q:] + dhalo_acc[...]], axis=0)
        dp_ref[...] = jnp.concatenate([dxbc, gz, gdt, jnp.zeros_like(gdt)], axis=1).astype(BF16)
        dhalo_acc[...] = gx[:CONV_HALO]
        dh_acc[0] = gh[0]
        dh_acc[1] = gh[1]
        dcw_ref[...] += gcw
        dcb_ref[...] += gcb
        ddb_ref[...] += gdb
        dal_ref[...] += gal
        dd_ref[...] += gd
        dng_ref[...] += gng

    const = lambda i: (0, 0)
    rev = lambda i: last - i
    row = lambda n: jax.ShapeDtypeStruct((1, n), F32)
    return pl.pallas_call(
        body, grid=(_SSD_NCHUNK,),
        in_specs=[pl.BlockSpec((q, SSD_INNER), lambda i: (rev(i), PROJ_Z_BLK)),
                  pl.BlockSpec((q, SSD_CONV_CH), lambda i: (rev(i), 0)),
                  pl.BlockSpec((CONV_HALO, SSD_CONV_CH), lambda i: (jnp.maximum(rev(i) * _HALO_PER_CHUNK - 1, 0), 0)),
                  pl.BlockSpec((q, 128), lambda i: (rev(i), PROJ_DT_BLK)),
                  pl.BlockSpec((1, 2, SSD_STATE, 256), lambda i: (rev(i), 0, 0, 0)),
                  pl.BlockSpec((q, SSD_INNER), lambda i: (rev(i), 0))] + _ssd_param_specs(const),
        out_specs=[pl.BlockSpec((q, PROJ_SSD_W), lambda i: (rev(i), 0))] + _ssd_param_specs(const),
        out_shape=[jax.ShapeDtypeStruct((SEQ, PROJ_W), BF16), jax.ShapeDtypeStruct((8, SSD_CONV_CH), F32),
                   row(SSD_CONV_CH), row(128), row(128), row(128), row(SSD_INNER)],
        scratch_shapes=[pltpu.VMEM((2, SSD_STATE, 256), F32), pltpu.VMEM((CONV_HALO, SSD_CONV_CH), F32)], name=name,
        compiler_params=_cparams("arbitrary"))(proj, proj, proj, proj, hstates, dmix, conv_w, conv_b, dt_bias, a_log,
                                                d_skip, norm_g)


def _rope_tables(pos_col, inv_freq_lane, *, name):
    s = pos_col.shape[0]

    def body(p_ref, f_ref, c_ref, s1_ref, s2_ref):
        ang = p_ref[...] * f_ref[...]
        within = lax.broadcasted_iota(jnp.int32, ang.shape, 1) % HEAD_LANES
        half = ROT_DIM // 2
        c_ref[...] = jnp.where(within < ROT_DIM, jnp.cos(ang), 1.0)
        sn = jnp.sin(ang)
        s1_ref[...] = jnp.where(within < half, -sn, 0.0)
        s2_ref[...] = jnp.where((within >= half) & (within < ROT_DIM), sn, 0.0)

    shp = jax.ShapeDtypeStruct((s, 128), F32)
    return pl.pallas_call(body, out_shape=[shp, shp, shp], name=name,
                          compiler_params=pltpu.CompilerParams(vmem_limit_bytes=VMEM_LIMIT))(pos_col, inv_freq_lane)


def _rope(t, c, s1, s2):
    half = ROT_DIM // 2
    return t * c + pltpu.roll(t, 128 - half, 1) * s1 + pltpu.roll(t, half, 1) * s2


def _rope_t(g, c, s1, s2):
    half = ROT_DIM // 2
    return g * c + pltpu.roll(g * s1, half, 1) + pltpu.roll(g * s2, 128 - half, 1)


def _att_valid(b):
    qi = lax.broadcasted_iota(jnp.int32, (ATT_BLOCK, 2 * ATT_BLOCK), 0)
    kj = lax.broadcasted_iota(jnp.int32, (ATT_BLOCK, 2 * ATT_BLOCK), 1)
    rel = qi + ATT_BLOCK - kj
    return (rel >= 0) & (rel <= ATT_BLOCK) & (b * ATT_BLOCK + kj - ATT_BLOCK >= 0)


def _att_slices(i, d):
    if d == 1:
        qstart = pl.multiple_of(i * ATT_BLOCK, ATT_BLOCK)
        return i, pl.ds(qstart, ATT_BLOCK), pl.ds(pl.multiple_of(qstart - ATT_BLOCK + ATT_KPAD, ATT_BLOCK), 2 * ATT_BLOCK)
    r = i % d
    b = i // d
    qstart = r + d * ATT_BLOCK * b
    return b, pl.ds(qstart, ATT_BLOCK, stride=d), pl.ds(qstart - ATT_BLOCK * d + ATT_KPAD, 2 * ATT_BLOCK, stride=d)


_ATT_NBLK = SEQ // ATT_BLOCK
_ATT_SCALE = HEAD_LANES ** -0.5
_ATT_UNROLL_FWD = 8
_ATT_UNROLL = 4


def _att_fwd(proj, cos, sin1, sin2, mix, *, name):
    s = SEQ

    def body(q_ref, k_ref, v_ref, c_ref, s1_ref, s2_ref, _, o_ref, lse_ref, mix_ref, qs, ks, vs, acc, m_s, l_s):
        c, s1, s2 = c_ref[...], s1_ref[...], s2_ref[...]
        qs[...] = _rope(q_ref[...], c, s1, s2) * _ATT_SCALE
        zeros = jnp.zeros((ATT_KPAD, 128), F32)
        ks[pl.ds(0, ATT_KPAD), :] = zeros
        vs[pl.ds(0, ATT_KPAD), :] = zeros
        ks[pl.ds(ATT_KPAD, s), :] = _rope(k_ref[...], c, s1, s2)
        vs[pl.ds(ATT_KPAD, s), :] = v_ref[...]
        head0 = _head_of_lane(128) == 0

        for bi, (_, d) in enumerate(ATT_PATTERNS):
            def blk(i, carry, d=d, first=(bi == 0)):
                b, sq, sk = _att_slices(i, d)
                qb = qs[sq, :]
                kw = ks[sk, :].astype(BF16)
                vw = vs[sk, :].astype(BF16)
                valid = _att_valid(b)
                ms, ls, os_ = [], [], []
                for hh in range(2):
                    qh = jnp.where(head0 if hh == 0 else ~head0, qb, 0.0).astype(BF16)
                    sc = lax.dot_general(qh, kw, (((1,), (1,)), ((), ())), preferred_element_type=F32)
                    sc = jnp.where(valid, sc, -jnp.inf)
                    mb = jnp.max(sc, axis=1, keepdims=True)
                    p = jnp.exp(sc - mb)
                    ms.append(mb)
                    ls.append(jnp.sum(p, axis=1, keepdims=True))
                    os_.append(jnp.dot(p.astype(BF16), vw, preferred_element_type=F32))
                m_b = jnp.where(head0, ms[0], ms[1])
                l_b = jnp.where(head0, ls[0], ls[1])
                o_b = jnp.where(head0, os_[0], os_[1])
                if first:
                    m_s[sq, :] = m_b
                    l_s[sq, :] = l_b
                    acc[sq, :] = o_b
                else:
                    m_old = m_s[sq, :]
                    m_new = jnp.maximum(m_old, m_b)
                    a_old = jnp.exp(m_old - m_new)
                    a_b = jnp.exp(m_b - m_new)
                    m_s[sq, :] = m_new
                    l_s[sq, :] = l_s[sq, :] * a_old + l_b * a_b
                    acc[sq, :] = acc[sq, :] * a_old + o_b * a_b
                return carry

            lax.fori_loop(0, _ATT_NBLK, blk, 0, unroll=_ATT_UNROLL_FWD)

        out = acc[...] / l_s[...]
        o_ref[...] = out
        mix_ref[...] = out.astype(BF16)
        lse_ref[...] = m_s[...] + jnp.log(l_s[...])

    col = lambda base: pl.BlockSpec((s, 128), lambda p: (0, base + p))
    tab = pl.BlockSpec((s, 128), lambda p: (0, 0))
    big = pltpu.VMEM((ATT_KPAD + s, 128), F32)
    tok = pltpu.VMEM((s, 128), F32)
    return pl.pallas_call(
        body, grid=(2,), in_specs=[col(PROJ_Q_BLK), col(PROJ_K_BLK), col(PROJ_V_BLK), tab, tab, tab, _HBM],
        out_specs=[pl.BlockSpec((s, 128), lambda p: (0, p)), pl.BlockSpec((s, 128), lambda p: (0, p)),
                   col(MIX_ATT_BLK)],
        out_shape=[jax.ShapeDtypeStruct((s, ATT_W), F32), jax.ShapeDtypeStruct((s, ATT_W), F32),
                   jax.ShapeDtypeStruct(mix.shape, mix.dtype)],
        input_output_aliases={6: 2}, scratch_shapes=[tok, big, big, tok, tok, tok], name=name,
        compiler_params=_cparams("arbitrary"))(proj, proj, proj, cos, sin1, sin2, mix)


def _att_bwd(proj, cos, sin1, sin2, out, lse, dmix, dproj, *, name):
    s = SEQ

    def body(proj_ref, c_hbm, s1_hbm, s2_hbm, out_hbm, lse_hbm, dmix_hbm, _, dproj_hbm,
             c_ref, s1_ref, s2_ref, o_ref, lse_ref, do_ref, qs, ks, vs, dqs, dks, dvs, staged, sems):
        def start(copies):
            for cp in copies:
                cp.start()
            return copies

        def load(pair):
            lanes = pl.ds(128 * pair, 128)
            rows = pl.ds(ATT_KPAD, s)
            return start([
                pltpu.make_async_copy(proj_ref.at[:, pl.ds(128 * (PROJ_Q_BLK + pair), 128)], qs, sems.at[0]),
                pltpu.make_async_copy(proj_ref.at[:, pl.ds(128 * (PROJ_K_BLK + pair), 128)], ks.at[rows, :], sems.at[1]),
                pltpu.make_async_copy(proj_ref.at[:, pl.ds(128 * (PROJ_V_BLK + pair), 128)], vs.at[rows, :], sems.at[2]),
                pltpu.make_async_copy(out_hbm.at[:, lanes], o_ref, sems.at[3]),
                pltpu.make_async_copy(lse_hbm.at[:, lanes], lse_ref, sems.at[4]),
                pltpu.make_async_copy(dmix_hbm.at[:, pl.ds(128 * (MIX_ATT_BLK + pair), 128)], do_ref, sems.at[5])])

        tables = start([pltpu.make_async_copy(c_hbm, c_ref, sems.at[6]),
                        pltpu.make_async_copy(s1_hbm, s1_ref, sems.at[7]),
                        pltpu.make_async_copy(s2_hbm, s2_ref, sems.at[8])])
        loads = load(0)
        for cp in tables:
            cp.wait()
        head0 = _head_of_lane(128) == 0
        zeros = jnp.zeros((ATT_KPAD, 128), F32)
        for pair in range(2):
            for cp in loads:
                cp.wait()
            c, s1, s2 = c_ref[...], s1_ref[...], s2_ref[...]
            qs[...] = _rope(qs[...], c, s1, s2) * _ATT_SCALE
            ks[pl.ds(0, ATT_KPAD), :] = zeros
            vs[pl.ds(0, ATT_KPAD), :] = zeros
            ks[pl.ds(ATT_KPAD, s), :] = _rope(ks[pl.ds(ATT_KPAD, s), :], c, s1, s2)
            dqs[...] = jnp.zeros_like(dqs)
            dks[...] = jnp.zeros_like(dks)
            dvs[...] = jnp.zeros_like(dvs)

            for _, d in ATT_PATTERNS:
                def blk(i, carry, d=d):
                    b, sq, sk = _att_slices(i, d)
                    qb = qs[sq, :]
                    kw = ks[sk, :].astype(BF16)
                    vw = vs[sk, :].astype(BF16)
                    dob = do_ref[sq, :]
                    lse_b = lse_ref[sq, :]
                    dd = dob * o_ref[sq, :]
                    valid = _att_valid(b)
                    dq_b = jnp.zeros((ATT_BLOCK, 128), F32)
                    dk_w = jnp.zeros((2 * ATT_BLOCK, 128), F32)
                    dv_w = jnp.zeros((2 * ATT_BLOCK, 128), F32)
                    for hh in range(2):
                        hm = head0 if hh == 0 else ~head0
                        qh = jnp.where(hm, qb, 0.0).astype(BF16)
                        doh = jnp.where(hm, dob, 0.0).astype(BF16)
                        lse_h = _pick_lane(lse_b, hh * HEAD_LANES)
                        d_h = jnp.sum(jnp.where(hm, dd, 0.0), axis=1, keepdims=True)
                        sc = lax.dot_general(qh, kw, (((1,), (1,)), ((), ())), preferred_element_type=F32)
                        p = jnp.where(valid, jnp.exp(sc - lse_h), 0.0)
                        dp = lax.dot_general(doh, vw, (((1,), (1,)), ((), ())), preferred_element_type=F32)
                        ds = (p * (dp - d_h)).astype(BF16)
                        dq_b = dq_b + jnp.where(hm, jnp.dot(ds, kw, preferred_element_type=F32), 0.0)
                        dk_w = dk_w + lax.dot_general(ds, qh, (((0,), (0,)), ((), ())), preferred_element_type=F32)
                        dv_w = dv_w + lax.dot_general(p.astype(BF16), doh, (((0,), (0,)), ((), ())),
                                                      preferred_element_type=F32)
                    dqs[sq, :] += dq_b
                    dks[sk, :] += dk_w
                    dvs[sk, :] += dv_w
                    return carry

                lax.fori_loop(0, _ATT_NBLK, blk, 0, unroll=_ATT_UNROLL)

            staged[0] = _rope_t(dqs[...] * _ATT_SCALE, c, s1, s2).astype(BF16)
            staged[1] = _rope_t(dks[pl.ds(ATT_KPAD, s), :], c, s1, s2).astype(BF16)
            staged[2] = dvs[pl.ds(ATT_KPAD, s), :].astype(BF16)
            stores = start([
                pltpu.make_async_copy(staged.at[j], dproj_hbm.at[:, pl.ds(128 * (col + pair), 128)], sems.at[9 + j])
                for j, col in enumerate((PROJ_Q_BLK, PROJ_K_BLK, PROJ_V_BLK))])
            if pair == 0:
                loads = load(1)
            for cp in stores:
                cp.wait()

    big = pltpu.VMEM((ATT_KPAD + s, 128), F32)
    tok = pltpu.VMEM((s, 128), F32)
    return pl.pallas_call(
        body, in_specs=[_HBM] * 8, out_specs=_HBM, out_shape=jax.ShapeDtypeStruct(dproj.shape, dproj.dtype),
        input_output_aliases={7: 0},
        scratch_shapes=[tok] * 6 + [tok, big, big, tok, big, big, pltpu.VMEM((3, s, 128), BF16),
                                    pltpu.SemaphoreType.DMA((12,))], name=name,
        compiler_params=pltpu.CompilerParams(vmem_limit_bytes=VMEM_LIMIT))(
            proj, cos, sin1, sin2, out, lse, dmix, dproj)


_POOL_TM = 512
_POOL_NT = SEQ // _POOL_TM
_POOL_HALO_PER_TILE = _POOL_TM // POOL_HALO


def _pool_tile(u_ext, w_bd, scale, t0):
    s2 = u_ext + _roll_rows(u_ext, 1)
    s4 = s2 + _roll_rows(s2, 2)
    s8 = s4 + _roll_rows(s4, 4)
    s16 = s8 + _roll_rows(s8, 8)
    grp = _head_of_lane(POOL_W)
    sel = jnp.where(grp == 0, s2, jnp.where(grp == 1, s4, jnp.where(grp == 2, s8, s16)))[POOL_HALO:]
    t = sel.shape[0]
    pos = t0 + lax.broadcasted_iota(jnp.int32, (t, POOL_W), 0) + 1
    win = jnp.where(grp == 0, 2, jnp.where(grp == 1, 4, jnp.where(grp == 2, 8, 16)))
    cnt = jnp.minimum(pos, win).astype(F32)
    diff = sel / cnt - u_ext[POOL_HALO:]
    return jnp.dot(diff.astype(BF16), w_bd.astype(BF16), preferred_element_type=F32) * scale


def _pool_fwd(proj, w_bd, scale, mix, *, name):
    tm = _POOL_TM

    def body(u_ref, halo_ref, w_ref, sc_ref, _, y_ref):
        i = pl.program_id(0)
        halo = jnp.where(i == 0, 0.0, halo_ref[...])
        u_ext = jnp.concatenate([halo, u_ref[...]], axis=0)
        y_ref[...] = _pool_tile(u_ext, w_ref[...], sc_ref[...], i * tm).astype(BF16)

    return pl.pallas_call(
        body, grid=(_POOL_NT,),
        in_specs=[pl.BlockSpec((tm, POOL_W), lambda i: (i, PROJ_POOL_BLK)),
                  pl.BlockSpec((POOL_HALO, POOL_W),
                               lambda i: (jnp.maximum(i * _POOL_HALO_PER_TILE - 1, 0), PROJ_POOL_BLK)),
                  pl.BlockSpec((POOL_W, POOL_W), lambda i: (0, 0)), pl.BlockSpec((1, POOL_W), lambda i: (0, 0)), _HBM],
        out_specs=pl.BlockSpec((tm, POOL_W), lambda i: (i, MIX_POOL_BLK)),
        out_shape=jax.ShapeDtypeStruct(mix.shape, mix.dtype), input_output_aliases={4: 0}, name=name,
        compiler_params=_cparams("parallel"))(proj, proj, w_bd, scale, mix)


def _pool_bwd(proj, dmix, w_bd, scale, dproj, *, name):
    tm = _POOL_TM
    last = _POOL_NT - 1

    def body(u_ref, halo_ref, dy_ref, w_ref, sc_ref, _, du_ref, dw_ref, dsc_ref, dhalo_acc):
        i = pl.program_id(0)

        @pl.when(i == 0)
        def _():
            dhalo_acc[...] = jnp.zeros_like(dhalo_acc)
            dw_ref[...] = jnp.zeros_like(dw_ref)
            dsc_ref[...] = jnp.zeros_like(dsc_ref)

        tile = last - i
        halo = jnp.where(tile == 0, 0.0, halo_ref[...])
        u_ext = jnp.concatenate([halo, u_ref[...]], axis=0)
        _, vjp = jax.vjp(functools.partial(_pool_tile, t0=tile * tm), u_ext, w_ref[...], sc_ref[...])
        gu, gw, gs = vjp(dy_ref[...])
        du_ref[...] = jnp.concatenate([gu[POOL_HALO:tm], gu[tm:] + dhalo_acc[...]], axis=0).astype(BF16)
        dhalo_acc[...] = gu[:POOL_HALO]
        dw_ref[...] += gw
        dsc_ref[...] += gs

    rev = lambda i: last - i
    return pl.pallas_call(
        body, grid=(_POOL_NT,),
        in_specs=[pl.BlockSpec((tm, POOL_W), lambda i: (rev(i), PROJ_POOL_BLK)),
                  pl.BlockSpec((POOL_HALO, POOL_W),
                               lambda i: (jnp.maximum(rev(i) * _POOL_HALO_PER_TILE - 1, 0), PROJ_POOL_BLK)),
                  pl.BlockSpec((tm, POOL_W), lambda i: (rev(i), MIX_POOL_BLK)),
                  pl.BlockSpec((POOL_W, POOL_W), lambda i: (0, 0)), pl.BlockSpec((1, POOL_W), lambda i: (0, 0)), _HBM],
        out_specs=[pl.BlockSpec((tm, POOL_W), lambda i: (rev(i), PROJ_POOL_BLK)),
                   pl.BlockSpec((POOL_W, POOL_W), lambda i: (0, 0)), pl.BlockSpec((1, POOL_W), lambda i: (0, 0))],
        out_shape=[jax.ShapeDtypeStruct(dproj.shape, dproj.dtype), jax.ShapeDtypeStruct((POOL_W, POOL_W), F32),
                   jax.ShapeDtypeStruct((1, POOL_W), F32)],
        input_output_aliases={5: 0}, scratch_shapes=[pltpu.VMEM((POOL_HALO, POOL_W), F32)], name=name,
        compiler_params=_cparams("arbitrary"))(proj, proj, dmix, w_bd, scale, dproj)


_FFN_TM = 256
_FFN_NT = SEQ // _FFN_TM
_FFN_HALO_PER_TILE = _FFN_TM // CONV_HALO


def _ffn_act_tile(hid_ext, conv_w, conv_b):
    hc = conv_b
    for k in range(FFN_CONV_K):
        hc = hc + _pick_row(conv_w, k) * _shift_rows(hid_ext, FFN_CONV_K - 1 - k, CONV_HALO)
    return _silu(hc[:, :FFN_DIM]) * hc[:, FFN_DIM:]


def _ffn_fwd(x1, norm_g, sc, sh, gate, up_t, down, conv_w, conv_b, *, name):
    tm = _FFN_TM
    w = 2 * FFN_DIM
    d = D_MODEL

    def body(x_ref, ng_ref, sc_ref, sh_ref, g_ref, up_ref, dn_ref, cw_ref, cb_ref,
             h_ref, hid_ref, act_ref, f_ref, x2_ref, halo_acc):
        i = pl.program_id(0)
        h2 = _rms_modulate(x_ref[...], ng_ref[...], sc_ref[...], sh_ref[...]).astype(BF16)
        h_ref[...] = h2
        hid = lax.dot_general(h2, up_ref[...], (((1,), (1,)), ((), ())), preferred_element_type=F32)
        hid_ref[...] = hid
        halo = jnp.where(i == 0, 0.0, halo_acc[...])
        act = _ffn_act_tile(jnp.concatenate([halo, hid], axis=0), cw_ref[...], cb_ref[...]).astype(BF16)
        halo_acc[...] = hid[tm - CONV_HALO:]
        act_ref[...] = act
        f = jnp.dot(act, dn_ref[...], preferred_element_type=F32)
        f_ref[...] = f
        x2_ref[...] = x_ref[...] + g_ref[...] * f

    tile = lambda n: pl.BlockSpec((tm, n), lambda i: (i, 0))
    return pl.pallas_call(
        body, grid=(_FFN_NT,),
        in_specs=[tile(d)] + [_resident((1, d))] * 4 + [_resident((w, d)), _resident((FFN_DIM, d)),
                                                        _resident((8, w)), _resident((1, w))],
        out_specs=[tile(d), tile(w), tile(FFN_DIM), tile(d), tile(d)],
        out_shape=[jax.ShapeDtypeStruct((SEQ, d), BF16), jax.ShapeDtypeStruct((SEQ, w), F32),
                   jax.ShapeDtypeStruct((SEQ, FFN_DIM), BF16), jax.ShapeDtypeStruct((SEQ, d), F32),
                   jax.ShapeDtypeStruct((SEQ, d), F32)],
        scratch_shapes=[pltpu.VMEM((CONV_HALO, w), F32)], name=name,
        compiler_params=_cparams("arbitrary"))(x1, norm_g, sc, sh, gate, up_t, down, conv_w, conv_b)


def _ffn_bwd(dx2, gate, f, hid, up_t, down, conv_w, conv_b, *, name):
    tm = _FFN_TM
    w = 2 * FFN_DIM
    d = D_MODEL
    last = _FFN_NT - 1

    def body(dx_ref, g_ref, f_ref, h_ref, halo_ref, up_ref, dn_ref, cw_ref, cb_ref,
             df_ref, dg_ref, dh_ref, dh2_ref, dcw_ref, dcb_ref, dhalo_acc):
        i = pl.program_id(0)

        @pl.when(i == 0)
        def _():
            dhalo_acc[...] = jnp.zeros_like(dhalo_acc)
            dcw_ref[...] = jnp.zeros_like(dcw_ref)
            dcb_ref[...] = jnp.zeros_like(dcb_ref)
            dg_ref[...] = jnp.zeros_like(dg_ref)

        dxv = dx_ref[...]
        df = (g_ref[...] * dxv).astype(BF16)
        df_ref[...] = df
        dg_ref[...] += jnp.sum(dxv * f_ref[...], axis=0, keepdims=True)
        dact = lax.dot_general(df, dn_ref[...], (((1,), (1,)), ((), ())), preferred_element_type=F32)
        halo = jnp.where(i == last, 0.0, halo_ref[...])
        hid_ext = jnp.concatenate([halo, h_ref[...]], axis=0)
        _, vjp = jax.vjp(_ffn_act_tile, hid_ext, cw_ref[...], cb_ref[...])
        gh, gw, gb = vjp(dact)
        dhid = jnp.concatenate([gh[CONV_HALO:tm], gh[tm:] + dhalo_acc[...]], axis=0).astype(BF16)
        dhalo_acc[...] = gh[:CONV_HALO]
        dh_ref[...] = dhid
        dh2_ref[...] = jnp.dot(dhid, up_ref[...], preferred_element_type=F32)
        dcw_ref[...] += gw
        dcb_ref[...] += gb

    rev = lambda i: last - i
    tile = lambda n: pl.BlockSpec((tm, n), lambda i: (rev(i), 0))
    acc = lambda shape: pl.BlockSpec(shape, lambda i: (0, 0))
    return pl.pallas_call(
        body, grid=(_FFN_NT,),
        in_specs=[tile(d), _resident((1, d)), tile(d), tile(w),
                  pl.BlockSpec((CONV_HALO, w), lambda i: (jnp.maximum(rev(i) * _FFN_HALO_PER_TILE - 1, 0), 0)),
                  _resident((w, d)), _resident((FFN_DIM, d)), _resident((8, w)), _resident((1, w))],
        out_specs=[tile(d), acc((1, d)), tile(w), tile(d), acc((8, w)), acc((1, w))],
        out_shape=[jax.ShapeDtypeStruct((SEQ, d), BF16), jax.ShapeDtypeStruct((1, d), F32),
                   jax.ShapeDtypeStruct((SEQ, w), BF16), jax.ShapeDtypeStruct((SEQ, d), F32),
                   jax.ShapeDtypeStruct((8, w), F32), jax.ShapeDtypeStruct((1, w), F32)],
        scratch_shapes=[pltpu.VMEM((CONV_HALO, w), F32)], name=name,
        compiler_params=_cparams("arbitrary"))(dx2, gate, f, hid, hid, up_t, down, conv_w, conv_b)


def _axes():
    return lax.axis_index("x"), lax.axis_index("y"), lax.axis_index("c")


def _handshake(peers):
    barrier = pltpu.get_barrier_semaphore()
    for peer in peers:
        pl.semaphore_signal(barrier, inc=1, device_id=peer, device_id_type=MESH)
    pl.semaphore_wait(barrier, len(peers))


def _allgather_body(x_refs, out_refs, send_sems, recv_sems, local_sems, own_barrier):
    n = len(x_refs)
    x, y, c = _axes()
    me, sibling = (x, y, c), (x, y, 1 - c)
    chips = [(1 - x, y), (x, 1 - y), (1 - x, 1 - y)]
    if own_barrier:
        _handshake([sibling] + [(*chip, c) for chip in chips])

    def slot(a, px, py, pc):
        return out_refs[a].at[4 * px + 2 * py + pc]

    def copy(a, k, block, to, src=None):
        return pltpu.make_async_remote_copy(
            src_ref=slot(a, *block) if src is None else src, dst_ref=slot(a, *block),
            send_sem=send_sems.at[a, k], recv_sem=recv_sems.at[a, k], device_id=to, device_id_type=MESH)

    mines, firsts = [], []
    for a in range(n):
        mines.append(pltpu.make_async_copy(x_refs[a], slot(a, *me), local_sems.at[a]))
        mines[-1].start()
        first = [copy(a, 0, me, sibling, src=x_refs[a])]
        first += [copy(a, 1 + j, me, (*chip, c), src=x_refs[a]) for j, chip in enumerate(chips)]
        for cp in first:
            cp.start()
        firsts += first
    passed = []
    for j, chip in enumerate(chips):
        for a in range(n):
            copy(a, 1 + j, (*chip, c), me).wait_recv()
            passed.append(copy(a, 4 + j, (*chip, c), sibling))
            passed[-1].start()
    for a in range(n):
        copy(a, 0, sibling, me).wait_recv()
    for j, chip in enumerate(chips):
        for a in range(n):
            copy(a, 4 + j, (*chip, 1 - c), me).wait_recv()
    for cp in firsts + passed:
        cp.wait_send()
    for cp in mines:
        cp.wait()


def _allgather_sems(n):
    return [pltpu.SemaphoreType.DMA((n, 7)), pltpu.SemaphoreType.DMA((n, 7)), pltpu.SemaphoreType.DMA((n,))]


def _allgather(xs, *, name):
    n = len(xs)

    def body(*refs):
        _allgather_body(refs[:n], refs[n:2 * n], *refs[2 * n:], own_barrier=False)

    return pl.pallas_call(
        body, out_shape=[jax.ShapeDtypeStruct((N_DEV,) + xb.shape, xb.dtype) for xb in xs],
        in_specs=[_HBM] * n, out_specs=[_HBM] * n, scratch_shapes=_allgather_sems(n), name=name)(*xs)


def _allgather_async(xs, *, name, collective_id):
    n = len(xs)
    x_refs = [jax.new_ref(xb, memory_space=pltpu.MemorySpace.HBM) for xb in xs]
    out_refs = [jax.empty_ref(jax.ShapeDtypeStruct((N_DEV,) + xb.shape, xb.dtype), memory_space=pltpu.MemorySpace.HBM)
                for xb in xs]

    @pl.kernel(mesh=plsc.ScalarSubcoreMesh(axis_name="sequencer", num_cores=1), name=name,
               scratch_types=tuple(_allgather_sems(n)),
               compiler_params=pltpu.CompilerParams(collective_id=collective_id))
    def launch(send_sems, recv_sems, local_sems):
        _allgather_body(x_refs, out_refs, send_sems, recv_sems, local_sems, own_barrier=True)

    launch()
    return [r[...] for r in out_refs]


def _pair_exchange(blocks, *, name, collective_id):
    n = len(blocks)
    hbm = pltpu.MemorySpace.HBM
    in_refs = [jax.new_ref(b, memory_space=hbm) for b in blocks]
    out_refs = [jax.empty_ref(jax.ShapeDtypeStruct((4,) + b.shape[1:], b.dtype), memory_space=hbm) for b in blocks]

    @pl.kernel(mesh=plsc.ScalarSubcoreMesh(axis_name="sequencer", num_cores=1), name=name,
               scratch_types=(pltpu.SemaphoreType.DMA((n, 4)), pltpu.SemaphoreType.DMA((n, 4))),
               compiler_params=pltpu.CompilerParams(collective_id=collective_id))
    def launch(send_sems, recv_sems):
        x, y, c = _axes()
        _handshake([(x, y, 1 - c)])
        copies = [pltpu.make_async_remote_copy(
            src_ref=in_refs[a].at[2 * s + (1 - c)], dst_ref=out_refs[a].at[s], send_sem=send_sems.at[a, s],
            recv_sem=recv_sems.at[a, s], device_id=(x, y, 1 - c), device_id_type=MESH)
            for a in range(n) for s in range(4)]
        for cp in copies:
            cp.start()
        for cp in copies:
            cp.wait_recv()
        for cp in copies:
            cp.wait_send()

    launch()
    return [r[...] for r in out_refs]


def _chip_exchange(parts, *, name, collective_id):
    n = len(parts)
    hbm = pltpu.MemorySpace.HBM
    in_refs = [jax.new_ref(p, memory_space=hbm) for p in parts]
    out_refs = [jax.empty_ref(jax.ShapeDtypeStruct(p.shape, p.dtype), memory_space=hbm) for p in parts]

    @pl.kernel(mesh=plsc.ScalarSubcoreMesh(axis_name="sequencer", num_cores=1), name=name,
               scratch_types=(pltpu.SemaphoreType.DMA((n, 3)), pltpu.SemaphoreType.DMA((n, 3)),
                              pltpu.SemaphoreType.DMA((n,))),
               compiler_params=pltpu.CompilerParams(collective_id=collective_id))
    def launch(send_sems, recv_sems, local_sems):
        x, y, c = _axes()
        my_chip = 2 * x + y
        chips = [(1 - x, y), (x, 1 - y), (1 - x, 1 - y)]
        _handshake([(*chip, c) for chip in chips])
        locals_ = [pltpu.make_async_copy(in_refs[a].at[my_chip], out_refs[a].at[my_chip], local_sems.at[a])
                   for a in range(n)]
        for cp in locals_:
            cp.start()
        copies = [pltpu.make_async_remote_copy(
            src_ref=in_refs[a].at[2 * px + py], dst_ref=out_refs[a].at[my_chip], send_sem=send_sems.at[a, k],
            recv_sem=recv_sems.at[a, k], device_id=(px, py, c), device_id_type=MESH)
            for a in range(n) for k, (px, py) in enumerate(chips)]
        for cp in copies:
            cp.start()
        for cp in copies:
            cp.wait_recv()
        for cp in copies:
            cp.wait_send()
        for cp in locals_:
            cp.wait()

    launch()
    return [r[...] for r in out_refs]


def _pair_sum(core, blocks, from_sibling, *, name):
    n = len(blocks)

    def body(core_ref, *refs):
        for a_ref, b_ref, o_ref in zip(refs[:n], refs[n:2 * n], refs[2 * n:]):
            o_ref[...] = (a_ref[...].astype(F32) + b_ref[...].astype(F32)).astype(o_ref.dtype)

    mine = lambda b: pl.BlockSpec((1,) + b.shape[1:], lambda s, core_ref: (2 * s + core_ref[0], 0, 0))
    slot = lambda b: pl.BlockSpec((1,) + b.shape[1:], lambda s, core_ref: (s, 0, 0))
    return pl.pallas_call(
        body,
        grid_spec=pltpu.PrefetchScalarGridSpec(
            num_scalar_prefetch=1, grid=(4,),
            in_specs=[mine(b) for b in blocks] + [slot(b) for b in blocks], out_specs=[slot(b) for b in blocks]),
        out_shape=[jax.ShapeDtypeStruct(s.shape, s.dtype) for s in from_sibling], name=name,
        compiler_params=_cparams("parallel"))(core, *blocks, *from_sibling)


def _sum_blocks(a, *, name, tr=None):
    n, r, cdim = a.shape
    tr = tr or r

    def body(a_ref, o_ref):
        acc = a_ref[0].astype(F32)
        for k in range(1, n):
            acc = acc + a_ref[k].astype(F32)
        o_ref[...] = acc

    return pl.pallas_call(body, grid=(r // tr,), in_specs=[pl.BlockSpec((n, tr, cdim), lambda i: (0, i, 0))],
                          out_specs=pl.BlockSpec((tr, cdim), lambda i: (i, 0)),
                          out_shape=jax.ShapeDtypeStruct((r, cdim), F32), name=name,
                          compiler_params=_cparams("parallel"))(a)


def _sum_gathered(gathered, *, name):
    n = len(gathered)

    def body(*refs):
        for a_ref, o_ref in zip(refs[:n], refs[n:]):
            acc = a_ref[0]
            for k in range(1, N_DEV):
                acc = acc + a_ref[k]
            o_ref[...] = acc

    return pl.pallas_call(body, out_shape=[jax.ShapeDtypeStruct(g.shape[1:], F32) for g in gathered], name=name,
                          compiler_params=pltpu.CompilerParams(vmem_limit_bytes=VMEM_LIMIT))(*gathered)


_ADA_SHARD = 6 * D_MODEL // N_DEV


def _ada_mod(c_all, ada_w, *, name):
    def body(c_ref, w_ref, o_ref):
        o_ref[0] = jnp.dot(_silu(c_ref[...]).astype(BF16), w_ref[0].astype(BF16), preferred_element_type=F32)

    return pl.pallas_call(
        body, grid=(DEPTH,),
        in_specs=[pl.BlockSpec((N_DEV, D_MODEL), lambda l: (0, 0)),
                  pl.BlockSpec((1, D_MODEL, _ADA_SHARD), lambda l: (l, 0, 0))],
        out_specs=pl.BlockSpec((1, N_DEV, _ADA_SHARD), lambda l: (l, 0, 0)),
        out_shape=jax.ShapeDtypeStruct((DEPTH, N_DEV, _ADA_SHARD), F32), name=name,
        compiler_params=_cparams("parallel"))(c_all, ada_w)


def _ada_wgrad(c_all, dmod_cols, *, name):
    def body(c_ref, d_ref, o_ref):
        o_ref[0] = lax.dot_general(_silu(c_ref[...]), d_ref[0], (((0,), (0,)), ((), ())),
                                   preferred_element_type=F32, precision=lax.Precision.HIGHEST)

    return pl.pallas_call(
        body, grid=(DEPTH,),
        in_specs=[pl.BlockSpec((N_DEV, D_MODEL), lambda l: (0, 0)),
                  pl.BlockSpec((1, N_DEV, _ADA_SHARD), lambda l: (l, 0, 0))],
        out_specs=pl.BlockSpec((1, D_MODEL, _ADA_SHARD), lambda l: (l, 0, 0)),
        out_shape=jax.ShapeDtypeStruct((DEPTH, D_MODEL, _ADA_SHARD), F32), name=name,
        compiler_params=_cparams("parallel"))(c_all, dmod_cols)


def _add_rows(a, b, *, name):
    def body(a_ref, b_ref, o_ref):
        o_ref[...] = a_ref[...] + b_ref[...]

    return pl.pallas_call(body, out_shape=jax.ShapeDtypeStruct(a.shape, a.dtype), name=name)(a, b)


def _adamw_update(w_ref, g_ref, m_ref, v_ref, d_ref, mo_ref, vo_ref):
    gv = g_ref[...]
    mn = ADAM_B1 * m_ref[...] + (1.0 - ADAM_B1) * gv
    vn = ADAM_B2 * v_ref[...] + (1.0 - ADAM_B2) * (gv * gv)
    mo_ref[...] = mn
    vo_ref[...] = vn
    m_hat = mn / (1.0 - ADAM_B1 ** ADAM_STEP)
    v_hat = vn / (1.0 - ADAM_B2 ** ADAM_STEP)
    d_ref[...] = -ADAM_LR * (m_hat / (jnp.sqrt(v_hat) + ADAM_EPS) + ADAM_WD * w_ref[...])


def _adamw_small(ws, gs, ms, vs, *, name):
    n = len(ws)

    def body(*refs):
        ins, outs = refs[:4 * n], refs[4 * n:]
        for i in range(n):
            _adamw_update(ins[i], ins[n + i], ins[2 * n + i], ins[3 * n + i], outs[i], outs[n + i], outs[2 * n + i])

    shapes = [jax.ShapeDtypeStruct(a.shape, F32) for a in ws]
    outs = pl.pallas_call(body, out_shape=shapes * 3, name=name,
                          compiler_params=pltpu.CompilerParams(vmem_limit_bytes=VMEM_LIMIT))(*ws, *gs, *ms, *vs)
    return outs[:n], outs[n:2 * n], outs[2 * n:]


def _adamw(w, g, m, v, *, name, tr):
    r, cdim = w.shape
    body = functools.partial(_adamw_update)

    spec = pl.BlockSpec((tr, cdim), lambda i: (i, 0))
    shp = jax.ShapeDtypeStruct((r, cdim), F32)
    return pl.pallas_call(body, grid=(r // tr,), in_specs=[spec] * 4, out_specs=[spec] * 3, out_shape=[shp] * 3,
                          name=name, compiler_params=_cparams("parallel"))(w, g, m, v)


def _pad_rows(a, rows):
    return jnp.concatenate([a, jnp.zeros((rows - a.shape[0],) + a.shape[1:], a.dtype)], axis=0)


def _pad_lanes(a, lanes):
    return jnp.concatenate([a, jnp.zeros(a.shape[:-1] + (lanes - a.shape[-1],), a.dtype)], axis=-1)


def _permute_w_in(wt):
    return jnp.concatenate([wt[512:1536], wt[:512], wt[1536:1544],
                            jnp.zeros((PROJ_W - IN_W, wt.shape[1]), wt.dtype), wt[1544:]], axis=0)


def _unpermute_w_in(wp):
    return jnp.concatenate([wp[1024:1536], wp[:1024], wp[1536:1544], wp[PROJ_SSD_W:]], axis=0)


def _block_diag(w):
    rows = []
    for g in range(4):
        rows.append(jnp.concatenate([w[g] if k == g else jnp.zeros_like(w[g]) for k in range(4)], axis=1))
    return jnp.concatenate(rows, axis=0)


def _diag_blocks(wbd):
    return jnp.stack([wbd[64 * g:64 * (g + 1), 64 * g:64 * (g + 1)] for g in range(4)], axis=0)


def _layer_params(l, small):
    return dict(
        norm1_g=small["norm1_g"][l][None], norm2_g=small["norm2_g"][l][None],
        conv_w=_pad_rows(small["ssd_conv_w"][l], 8), conv_b=small["ssd_conv_b"][l][None],
        dt_bias=_pad_lanes(small["ssd_dt_bias"][l][None], 128), a_log=_pad_lanes(small["ssd_a_log"][l][None], 128),
        d_skip=_pad_lanes(small["ssd_d"][l][None], 128), ssd_norm_g=small["ssd_norm_g"][l][None],
        pool_bd=_block_diag(small["pool_w"][l]), pool_scale=small["pool_scale"][l][None],
        fcw=_pad_rows(small["ffn_conv_w"][l], 8), fcb=small["ffn_conv_b"][l][None])


def _mod_rows(mod_l):
    return [mod_l[None, D_MODEL * i:D_MODEL * (i + 1)] for i in range(6)]


def _layer_fwd(x, mod_l, p, tabs, l, gather):
    sh1, sc1, g1, sh2, sc2, g2 = _mod_rows(mod_l)
    mix_w = gather(l, "mix", None)
    p.update(w_in=mix_w["w_in"], w_out=mix_w["w_out"])
    proj, h1 = _mm(x, p["w_in"], nt=True, norm=(p["norm1_g"], sc1, sh1), name=f"l{l}_proj")
    ffn_w = gather(l, "ffn", proj)
    p.update(up=ffn_w["ffn_up"], down=ffn_w["ffn_down"])
    mix, hst = _ssd_fwd(proj, p["conv_w"], p["conv_b"], p["dt_bias"], p["a_log"], p["d_skip"], p["ssd_norm_g"],
                        name=f"l{l}_ssd")
    mix = _pool_fwd(proj, p["pool_bd"], p["pool_scale"], mix, name=f"l{l}_pool")
    y_att, lse, mix = _att_fwd(proj, *tabs, mix, name=f"l{l}_att")
    mo, x1 = _mm(mix, p["w_out"], residual=(x, g1), name=f"l{l}_out")
    gather(l + 1, "mix", (x1, p["up"]))
    h2, hid, act, f, x2 = _ffn_fwd(x1, p["norm2_g"], sc2, sh2, g2, p["up"], p["down"], p["fcw"], p["fcb"],
                                   name=f"l{l}_ffn")
    return x2, dict(x=x, h1=h1, proj=proj, hst=hst, y_att=y_att, lse=lse, mix=mix, mo=mo, x1=x1, h2=h2, hid=hid,
                    act=act, f=f)


def _layer_bwd(dx2, sv, mod_l, p, tabs, l, exchange):
    sh1, sc1, g1, sh2, sc2, g2 = _mod_rows(mod_l)
    df, dg2, dhid, dh2, dfcw, dfcb = _ffn_bwd(dx2, g2, sv["f"], sv["hid"], p["up"], p["down"], p["fcw"], p["fcb"],
                                              name=f"l{l}_ffn_b")
    d_down = _wgrad(sv["act"], df, tk=1408, name=f"l{l}_down_bw")
    d_up = _wgrad(dhid, sv["h2"], tk=1408, name=f"l{l}_up_bw")
    exchange(l, "ffn", dict(ffn_up=d_up, ffn_down=d_down))
    dx1, dn2, dsc2, dsh2, dmix, dmo, dg1 = _mid_bwd(sv["x1"], dh2, dx2, p["norm2_g"], sc2, g1, sv["mo"], p["w_out"],
                                                    name=f"l{l}_mid_b")
    d_wout = _wgrad(sv["mix"], dmo, name=f"l{l}_out_bw")
    dproj, dcw, dcb, ddb, dal, dd, dng = _ssd_bwd(
        sv["proj"], sv["hst"], dmix, p["conv_w"], p["conv_b"], p["dt_bias"], p["a_log"], p["d_skip"],
        p["ssd_norm_g"], name=f"l{l}_ssd_b")
    dproj, dwbd, dpsc = _pool_bwd(sv["proj"], dmix, p["pool_bd"], p["pool_scale"], dproj, name=f"l{l}_pool_b")
    dproj = _att_bwd(sv["proj"], *tabs, sv["y_att"], sv["lse"], dmix, dproj, name=f"l{l}_att_b")
    d_win = _wgrad(dproj, sv["h1"], tk=1408, name=f"l{l}_proj_bw")
    exchange(l, "mix", dict(w_in=d_win, w_out=d_wout))
    dx0, dn1, dsc1, dsh1 = _norm_mod_bwd(sv["x"], dproj, dx1, p["norm1_g"], sc1, w=p["w_in"], name=f"l{l}_proj_b")
    dx0, _ = lax.optimization_barrier((dx0, (d_win, d_wout, d_up, d_down)))
    dmod = jnp.concatenate([dsh1, dsc1, dg1, dsh2, dsc2, dg2], axis=1)[0]
    small = dict(norm1_g=dn1[0], ssd_conv_w=dcw[:SSD_CONV_K], ssd_conv_b=dcb[0], ssd_dt_bias=ddb[0], ssd_a_log=dal[0],
                 ssd_d=dd[0], ssd_norm_g=dng[0], pool_w=_diag_blocks(dwbd), pool_scale=dpsc[0], norm2_g=dn2[0],
                 ffn_conv_w=dfcw[:FFN_CONV_K], ffn_conv_b=dfcb[0])
    return dx0, dmod, small


def _example_step(x, target, pos_col, inv_freq_lane, mod, gather, small, final_g, exchange):
    tabs = _rope_tables(pos_col, inv_freq_lane, name="rope_tables")
    params, saved = [], []
    for l in range(DEPTH):
        params.append(_layer_params(l, small))
        x, sv = _layer_fwd(x, mod[l], params[l], tabs, l, gather)
        saved.append(sv)
    loss_row, dx, dfg = _final_loss(x, final_g[None], target, name="final_loss")
    dmods, smalls = [None] * DEPTH, [None] * DEPTH
    for l in reversed(range(DEPTH)):
        dx, dmods[l], smalls[l] = _layer_bwd(dx, saved[l], mod[l], params[l], tabs, l, exchange)
    return loss_row, dx, jnp.stack(dmods, axis=0), smalls, dfg[0]


_BIG = ("w_in", "w_out", "ffn_up", "ffn_down")
_SMALL_GRADS = ("norm1_g", "ssd_conv_w", "ssd_conv_b", "ssd_dt_bias", "ssd_a_log", "ssd_d", "ssd_norm_g", "pool_w",
                "pool_scale", "norm2_g", "ffn_conv_w", "ffn_conv_b")
_SMALL_PARAMS = ("ada_b", "norm1_g", "ssd_conv_w", "ssd_conv_b", "ssd_dt_bias", "ssd_a_log", "ssd_d", "ssd_norm_g",
                 "pool_w", "pool_scale", "norm2_g", "ffn_conv_w", "ffn_conv_b", "final_g")
_WEIGHT_ORDER = ("ada_w", "ada_b", "norm1_g", "w_in", "ssd_conv_w", "ssd_conv_b", "ssd_dt_bias", "ssd_a_log", "ssd_d",
                 "ssd_norm_g", "pool_w", "pool_scale", "w_out", "norm2_g", "ffn_up", "ffn_conv_w", "ffn_conv_b",
                 "ffn_down", "final_g")


_COLUMN_SHARDED = ("w_in", "ffn_up")
_GROUPS = (("mix", ("w_in", "w_out")), ("ffn", ("ffn_up", "ffn_down")))


def _big_shares(w, l, names):
    return [(w[name][l].T if name in _COLUMN_SHARDED else w[name][l]).astype(BF16) for name in names]


def _unshard_big(names, gathered):
    out = {}
    for name, g in zip(names, gathered):
        full = g.reshape(N_DEV * g.shape[1], g.shape[2])
        out[name] = _permute_w_in(full) if name == "w_in" else full
    return out


def _shard_big(grads):
    out = []
    for name, g in grads.items():
        g = _unpermute_w_in(g) if name == "w_in" else g
        out.append(g.reshape(N_DEV, g.shape[0] // N_DEV, g.shape[1]))
    return out


def kernel(x, c, positions, ada_w, ada_b, norm1_g, w_in, ssd_conv_w, ssd_conv_b, ssd_dt_bias, ssd_a_log, ssd_d, ssd_norm_g, pool_w, pool_scale, w_out, norm2_g, ffn_up, ffn_conv_w, ffn_conv_b, ffn_down, final_g, loss_target, m_ada_w, m_ada_b, m_norm1_g, m_w_in, m_ssd_conv_w, m_ssd_conv_b, m_ssd_dt_bias, m_ssd_a_log, m_ssd_d, m_ssd_norm_g, m_pool_w, m_pool_scale, m_w_out, m_norm2_g, m_ffn_up, m_ffn_conv_w, m_ffn_conv_b, m_ffn_down, m_final_g, v_ada_w, v_ada_b, v_norm1_g, v_w_in, v_ssd_conv_w, v_ssd_conv_b, v_ssd_dt_bias, v_ssd_a_log, v_ssd_d, v_ssd_norm_g, v_pool_w, v_pool_scale, v_w_out, v_norm2_g, v_ffn_up, v_ffn_conv_w, v_ffn_conv_b, v_ffn_down, v_final_g):
    w = dict(ada_w=ada_w, ada_b=ada_b, norm1_g=norm1_g, w_in=w_in, ssd_conv_w=ssd_conv_w, ssd_conv_b=ssd_conv_b,
             ssd_dt_bias=ssd_dt_bias, ssd_a_log=ssd_a_log, ssd_d=ssd_d, ssd_norm_g=ssd_norm_g, pool_w=pool_w,
             pool_scale=pool_scale, w_out=w_out, norm2_g=norm2_g, ffn_up=ffn_up, ffn_conv_w=ffn_conv_w,
             ffn_conv_b=ffn_conv_b, ffn_down=ffn_down, final_g=final_g)
    m = dict(ada_w=m_ada_w, ada_b=m_ada_b, norm1_g=m_norm1_g, w_in=m_w_in, ssd_conv_w=m_ssd_conv_w,
             ssd_conv_b=m_ssd_conv_b, ssd_dt_bias=m_ssd_dt_bias, ssd_a_log=m_ssd_a_log, ssd_d=m_ssd_d,
             ssd_norm_g=m_ssd_norm_g, pool_w=m_pool_w, pool_scale=m_pool_scale, w_out=m_w_out, norm2_g=m_norm2_g,
             ffn_up=m_ffn_up, ffn_conv_w=m_ffn_conv_w, ffn_conv_b=m_ffn_conv_b, ffn_down=m_ffn_down,
             final_g=m_final_g)
    v = dict(ada_w=v_ada_w, ada_b=v_ada_b, norm1_g=v_norm1_g, w_in=v_w_in, ssd_conv_w=v_ssd_conv_w,
             ssd_conv_b=v_ssd_conv_b, ssd_dt_bias=v_ssd_dt_bias, ssd_a_log=v_ssd_a_log, ssd_d=v_ssd_d,
             ssd_norm_g=v_ssd_norm_g, pool_w=v_pool_w, pool_scale=v_pool_scale, w_out=v_w_out, norm2_g=v_norm2_g,
             ffn_up=v_ffn_up, ffn_conv_w=v_ffn_conv_w, ffn_conv_b=v_ffn_conv_b, ffn_down=v_ffn_down,
             final_g=v_final_g)
    ix, iy, ic = _axes()
    dev = 4 * ix + 2 * iy + ic

    c_all, scw, fcw = _allgather([c, ssd_conv_w.reshape(DEPTH * SSD_CONV_K, -1),
                                  ffn_conv_w.reshape(DEPTH * FFN_CONV_K, -1)], name="gather_small")
    small_all = c_all
    c_all = c_all.reshape(N_DEV, D_MODEL)
    scw = scw.reshape(N_DEV, DEPTH, SSD_CONV_K, -1).transpose(1, 2, 0, 3).reshape(DEPTH, SSD_CONV_K, SSD_CONV_CH)
    fcw = fcw.reshape(N_DEV, DEPTH, FFN_CONV_K, -1).transpose(1, 2, 0, 3).reshape(DEPTH, FFN_CONV_K, 2 * FFN_DIM)

    mod_cols = _ada_mod(c_all, ada_w, name="ada_mod")
    mod_all = _allgather([mod_cols.reshape(DEPTH * N_DEV, _ADA_SHARD)], name="gather_mod")[0]
    mod_all = mod_all.reshape(N_DEV, DEPTH, N_DEV, _ADA_SHARD)
    mod_mine = lax.dynamic_index_in_dim(mod_all, dev, axis=2, keepdims=False)
    mod = _add_rows(mod_mine.transpose(1, 0, 2).reshape(DEPTH, 6 * D_MODEL), ada_b, name="ada_bias")

    fetched = {}

    def gather(l, group, after):
        if l < DEPTH and (l, group) not in fetched:
            names = dict(_GROUPS)[group]
            shares, _ = lax.optimization_barrier((_big_shares(w, l, names), small_all if after is None else after))
            got = _allgather_async(shares, name=f"gather_weights_l{l}_{group}",
                                   collective_id=1 + 2 * l + (group == "ffn"))
            fetched[l, group] = _unshard_big(names, got)
        return fetched.get((l, group))

    core = ic.astype(jnp.int32).reshape(1)
    from_chips = {}

    def exchange(l, group, g):
        cid = 5 + 4 * l + 2 * (group == "mix")
        blocks = _shard_big(g)
        from_sibling = _pair_exchange(blocks, name=f"grads_pair_exchange_l{l}_{group}", collective_id=cid)
        parts = _pair_sum(core, blocks, from_sibling, name=f"grads_pair_sum_l{l}_{group}")
        got = _chip_exchange(parts, name=f"grads_chip_exchange_l{l}_{group}", collective_id=cid + 1)
        from_chips.update({(l, name): t for name, t in zip(g, got)})

    small = dict(norm1_g=norm1_g, norm2_g=norm2_g, ssd_conv_w=scw, ssd_conv_b=ssd_conv_b, ssd_dt_bias=ssd_dt_bias,
                 ssd_a_log=ssd_a_log, ssd_d=ssd_d, ssd_norm_g=ssd_norm_g, pool_w=pool_w, pool_scale=pool_scale,
                 ffn_conv_w=fcw, ffn_conv_b=ffn_conv_b)

    inv_freq = ROPE_THETA ** (-jnp.arange(0, ROT_DIM, 2, dtype=F32) / ROT_DIM)
    lane = jnp.arange(128) % HEAD_LANES
    inv_freq_lane = jnp.where(lane < ROT_DIM, inv_freq[lane % (ROT_DIM // 2)], 0.0)[None, :]
    pos_col = positions.reshape(SEQ, 1).astype(F32)
    loss_row, dx, dmod, g_small, g_final = _example_step(
        x[0], loss_target[0], pos_col, inv_freq_lane, mod, gather, small, final_g, exchange)

    grads = {}
    for name in _BIG:
        per_layer = [_sum_blocks(from_chips[l, name], name=f"grads_chip_sum_l{l}_{name}") for l in range(DEPTH)]
        grads[name] = jnp.stack([g.T if name in _COLUMN_SHARDED else g for g in per_layer], axis=0)

    small_names = list(_SMALL_GRADS)
    stacked = [jnp.stack([g_small[l][name] for l in range(DEPTH)], axis=0) for name in small_names]
    small_parts = [loss_row, dmod] + [s.reshape(-1, s.shape[-1]) for s in stacked] + [g_final[None]]
    gathered = _allgather_async(small_parts, name="gather_small_grads", collective_id=13)
    total = _sum_gathered(gathered, name="sum_small_grads")
    loss = total[0][0, 0]
    grads["ada_b"] = total[1]
    grads.update(zip(small_names, total[2:-1]))
    grads["final_g"] = total[-1][0]
    dmod_cols = lax.dynamic_slice_in_dim(gathered[1], dev * _ADA_SHARD, _ADA_SHARD, axis=2).transpose(1, 0, 2)
    grads["ada_w"] = _ada_wgrad(c_all, dmod_cols, name="ada_wgrad")
    for name in ("ssd_dt_bias", "ssd_a_log", "ssd_d"):
        grads[name] = grads[name][:, :SSD_HEADS]
    grads["pool_w"] = grads["pool_w"].reshape(pool_w.shape)
    grads["ssd_conv_w"] = lax.dynamic_slice_in_dim(
        grads["ssd_conv_w"].reshape(DEPTH, SSD_CONV_K, SSD_CONV_CH), dev * ssd_conv_w.shape[2], ssd_conv_w.shape[2], axis=2)
    grads["ffn_conv_w"] = lax.dynamic_slice_in_dim(
        grads["ffn_conv_w"].reshape(DEPTH, FFN_CONV_K, 2 * FFN_DIM), dev * ffn_conv_w.shape[2], ffn_conv_w.shape[2], axis=2)

    delta, new_m, new_v = {}, {}, {}
    for name, tr in (("ada_w", 512), ("w_in", 512), ("w_out", 256), ("ffn_up", 512), ("ffn_down", 352)):
        shp = w[name].shape
        two_d = lambda a: a.reshape(shp[0] * shp[1], shp[2])
        d_, m_, v_ = _adamw(two_d(w[name]), two_d(grads[name]), two_d(m[name]), two_d(v[name]), tr=tr,
                               name=f"adamw_{name}")
        delta[name], new_m[name], new_v[name] = (t.reshape(shp) for t in (d_, m_, v_))
    two_d = lambda a: a.reshape(-1, a.shape[-1])
    outs = _adamw_small(*[[two_d(t[name]) for name in _SMALL_PARAMS] for t in (w, grads, m, v)], name="adamw_small")
    for name, d_, m_, v_ in zip(_SMALL_PARAMS, *outs):
        delta[name], new_m[name], new_v[name] = (t.reshape(w[name].shape) for t in (d_, m_, v_))

    grad_x = dx[None]
    return (loss, grad_x, *[grads[n].reshape(w[n].shape) for n in _WEIGHT_ORDER],
            *[delta[n] for n in _WEIGHT_ORDER], *[new_m[n] for n in _WEIGHT_ORDER],
            *[new_v[n] for n in _WEIGHT_ORDER])
```

```python
import functools
import math

import jax
import jax.numpy as jnp
from jax import lax
from jax.experimental import pallas as pl
from jax.experimental.pallas import tpu as pltpu
from jax.experimental.pallas import tpu_sc as plsc

F32 = jnp.float32
BF16 = jnp.bfloat16

N_DEV = 8
D_MODEL = 1024
SEQ = 4096
DEPTH = 2
SSD_INNER = 512
SSD_HEADS = 8
SSD_HPG = 4
SSD_STATE = 128
SSD_CHUNK = 256
SSD_CONV_K = 4
SSD_CONV_CH = 1024
POOL_W = 256
POOL_WINDOWS = (2, 4, 8, 16)
ATT_W = 256
ATT_PATTERNS = ((128, 1), (512, 4), (2048, 16))
ATT_BLOCK = 128
ROT_DIM = 16
ROPE_THETA = 500000.0
IN_W = 2568
FFN_DIM = 2816
FFN_CONV_K = 3
NORM_EPS = 1e-6
HEAD_LANES = 64

ADAM_LR = 0.001
ADAM_B1 = 0.9
ADAM_B2 = 0.999
ADAM_EPS = 1e-08
ADAM_WD = 0.01
ADAM_STEP = 10

PROJ_W = 2816
PROJ_SSD_W = 1792
PROJ_Z_BLK = 2
PROJ_DT_BLK = 12
PROJ_POOL_BLK = 7
PROJ_Q_BLK, PROJ_K_BLK, PROJ_V_BLK = 16, 18, 20
MIX_POOL_BLK = 2
MIX_ATT_BLK = 6
VMEM_LIMIT = 56 * 1024 * 1024
ROW_TILE = 512
CONV_HALO = 8
POOL_HALO = 16
ATT_KPAD = ATT_BLOCK * 16
MESH = pl.DeviceIdType.MESH
_HBM = pl.BlockSpec(memory_space=pl.ANY)


def _cparams(*sem):
    return pltpu.CompilerParams(dimension_semantics=sem, vmem_limit_bytes=VMEM_LIMIT)


def _resident(shape):
    return pl.BlockSpec(shape, lambda i: (0,) * len(shape), pipeline_mode=pl.Buffered(1))


def _silu(x):
    return x * jax.nn.sigmoid(x)


def _pick_lane(v, h):
    lane = lax.broadcasted_iota(jnp.int32, v.shape, 1)
    return jnp.sum(jnp.where(lane == h, v, 0.0), axis=1, keepdims=True)


def _pick_row(v, h):
    row = lax.broadcasted_iota(jnp.int32, v.shape, 0)
    return jnp.sum(jnp.where(row == h, v, 0.0), axis=0, keepdims=True)


def _head_of_lane(width):
    return lax.broadcasted_iota(jnp.int32, (1, width), 1) // HEAD_LANES


@functools.partial(jax.custom_vjp, nondiff_argnums=(1, 2))
def _shift_rows(x_ext, s, halo):
    y = x_ext if s == 0 else pltpu.roll(x_ext, s, 0)
    return y[halo:]


def _shift_rows_fwd(x_ext, s, halo):
    return _shift_rows(x_ext, s, halo), None


def _shift_rows_bwd(s, halo, _, g):
    ge = jnp.concatenate([jnp.zeros((halo, g.shape[1]), g.dtype), g], axis=0)
    return (ge if s == 0 else pltpu.roll(ge, ge.shape[0] - s, 0),)


_shift_rows.defvjp(_shift_rows_fwd, _shift_rows_bwd)


@functools.partial(jax.custom_vjp, nondiff_argnums=(1,))
def _roll_rows(x, s):
    return pltpu.roll(x, s, 0)


def _roll_rows_fwd(x, s):
    return _roll_rows(x, s), None


def _roll_rows_bwd(s, _, g):
    return (pltpu.roll(g, g.shape[0] - s, 0),)


_roll_rows.defvjp(_roll_rows_fwd, _roll_rows_bwd)


def _rms_modulate(xv, g, sc, sh):
    r = lax.rsqrt(jnp.mean(xv * xv, axis=-1, keepdims=True) + NORM_EPS)
    return (xv * r * g) * (1.0 + sc) + sh


def _mm(a, w, *, name, nt=False, tm=ROW_TILE, tn=None, out_dtype=F32, norm=None, residual=None):
    t, k = a.shape
    n = w.shape[0] if nt else w.shape[1]
    tn = tn or n
    assert tn == n or (norm is None and residual is None)
    extra_in = list(norm or ()) + list(residual or ())

    def body(*refs):
        a_ref, w_ref = refs[:2]
        ins = refs[2:2 + len(extra_in)]
        outs = refs[2 + len(extra_in):]
        if norm is None:
            av = a_ref[...].astype(BF16)
        else:
            av = _rms_modulate(a_ref[...], ins[0][...], ins[1][...], ins[2][...]).astype(BF16)
            outs[1][...] = av
        if nt:
            acc = lax.dot_general(av, w_ref[...], (((1,), (1,)), ((), ())), preferred_element_type=F32)
        else:
            acc = jnp.dot(av, w_ref[...], preferred_element_type=F32)
        outs[0][...] = acc.astype(out_dtype)
        if residual is not None:
            x_ref, gate_ref = ins[-2:]
            outs[-1][...] = x_ref[...] + gate_ref[...] * acc

    row = lambda width: pl.BlockSpec((1, width), lambda i, j: (0, 0))
    tile = lambda width: pl.BlockSpec((tm, width), lambda i, j: (i, 0))
    w_spec = pl.BlockSpec((tn, k), lambda i, j: (j, 0)) if nt else pl.BlockSpec((k, tn), lambda i, j: (0, j))
    in_specs = [tile(k), w_spec] + ([row(k)] * 3 if norm else []) + ([tile(n), row(n)] if residual else [])
    out_specs = [pl.BlockSpec((tm, tn), lambda i, j: (i, j))] + ([tile(k)] if norm else []) + \
        ([tile(n)] if residual else [])
    out_shape = [jax.ShapeDtypeStruct((t, n), out_dtype)] + \
        ([jax.ShapeDtypeStruct((t, k), BF16)] if norm else []) + \
        ([jax.ShapeDtypeStruct((t, n), F32)] if residual else [])
    outs = pl.pallas_call(
        body, grid=(t // tm, n // tn), in_specs=in_specs, out_specs=out_specs, out_shape=out_shape, name=name,
        compiler_params=_cparams("parallel", "parallel"))(a, w, *extra_in)
    return outs[0] if len(outs) == 1 else outs


def _wgrad(a, b, *, name, tk=None, tn=None, tt=2048, out_dtype=BF16):
    t, k = a.shape
    n = b.shape[1]
    tk = tk or k
    tn = tn or n
    steps = t // tt

    def body(a_ref, b_ref, o_ref, acc_ref):
        s = pl.program_id(2)

        @pl.when(s == 0)
        def _():
            acc_ref[...] = jnp.zeros_like(acc_ref)

        acc_ref[...] += lax.dot_general(a_ref[...].astype(BF16), b_ref[...].astype(BF16),
                                        (((0,), (0,)), ((), ())), preferred_element_type=F32)

        @pl.when(s == steps - 1)
        def _():
            o_ref[...] = acc_ref[...].astype(out_dtype)

    return pl.pallas_call(
        body, grid=(k // tk, n // tn, steps),
        in_specs=[pl.BlockSpec((tt, tk), lambda i, j, s: (s, i)), pl.BlockSpec((tt, tn), lambda i, j, s: (s, j))],
        out_specs=pl.BlockSpec((tk, tn), lambda i, j, s: (i, j)),
        out_shape=jax.ShapeDtypeStruct((k, n), out_dtype),
        scratch_shapes=[pltpu.VMEM((tk, tn), F32)], name=name,
        compiler_params=_cparams("parallel", "parallel", "arbitrary"))(a, b)


def _norm_mod_bwd(x, dh, dres, g, sc, *, name, w=None, tm=ROW_TILE):
    s, d = x.shape
    steps = s // tm

    def body(x_ref, dh_ref, dres_ref, g_ref, sc_ref, *rest):
        w_ref = rest[0] if w is not None else None
        dx_ref, dg_ref, dsc_ref, dsh_ref, da_acc, dsh_acc = rest[-6:]
        i = pl.program_id(0)

        @pl.when(i == 0)
        def _():
            da_acc[...] = jnp.zeros_like(da_acc)
            dsh_acc[...] = jnp.zeros_like(dsh_acc)

        xv = x_ref[...]
        if w is None:
            dhv = dh_ref[...].astype(F32)
        else:
            dhv = jnp.dot(dh_ref[...], w_ref[...], preferred_element_type=F32)
        r = lax.rsqrt(jnp.mean(xv * xv, axis=-1, keepdims=True) + NORM_EPS)
        xhat = xv * r
        gain = g_ref[...] * (1.0 + sc_ref[...])
        dxhat = dhv * gain
        dx_ref[...] = dres_ref[...] + r * (dxhat - xhat * jnp.mean(dxhat * xhat, axis=-1, keepdims=True))
        da_acc[...] += jnp.sum(dhv * xhat, axis=0, keepdims=True)
        dsh_acc[...] += jnp.sum(dhv, axis=0, keepdims=True)

        @pl.when(i == steps - 1)
        def _():
            dg_ref[...] = da_acc[...] * (1.0 + sc_ref[...])
            dsc_ref[...] = da_acc[...] * g_ref[...]
            dsh_ref[...] = dsh_acc[...]

    row = pl.BlockSpec((1, d), lambda i: (0, 0))
    tile = pl.BlockSpec((tm, d), lambda i: (i, 0))
    row_shape = jax.ShapeDtypeStruct((1, d), F32)
    dh_spec = tile if w is None else pl.BlockSpec((tm, dh.shape[1]), lambda i: (i, 0))
    return pl.pallas_call(
        body, grid=(steps,), in_specs=[tile, dh_spec, tile, row, row] + ([] if w is None else [_resident(w.shape)]),
        out_specs=[tile, row, row, row],
        out_shape=[jax.ShapeDtypeStruct((s, d), F32), row_shape, row_shape, row_shape],
        scratch_shapes=[pltpu.VMEM((1, d), F32), pltpu.VMEM((1, d), F32)], name=name,
        compiler_params=_cparams("arbitrary"))(x, dh, dres, g, sc, *([] if w is None else [w]))


def _mid_bwd(x1, dh2, dx2, norm_g, sc, gate, mo, w_out, *, name, tm=ROW_TILE):
    s, d = x1.shape
    steps = s // tm

    def body(x_ref, dh_ref, dres_ref, ng_ref, sc_ref, g_ref, mo_ref, w_ref,
             dx_ref, dng_ref, dsc_ref, dsh_ref, dmix_ref, dmo_ref, dg_ref, da_acc, dsh_acc):
        i = pl.program_id(0)

        @pl.when(i == 0)
        def _():
            da_acc[...] = jnp.zeros_like(da_acc)
            dsh_acc[...] = jnp.zeros_like(dsh_acc)
            dg_ref[...] = jnp.zeros_like(dg_ref)

        xv = x_ref[...]
        dhv = dh_ref[...]
        r = lax.rsqrt(jnp.mean(xv * xv, axis=-1, keepdims=True) + NORM_EPS)
        xhat = xv * r
        dxhat = dhv * (ng_ref[...] * (1.0 + sc_ref[...]))
        dxv = dres_ref[...] + r * (dxhat - xhat * jnp.mean(dxhat * xhat, axis=-1, keepdims=True))
        dx_ref[...] = dxv
        da_acc[...] += jnp.sum(dhv * xhat, axis=0, keepdims=True)
        dsh_acc[...] += jnp.sum(dhv, axis=0, keepdims=True)
        dmo = (g_ref[...] * dxv).astype(BF16)
        dmo_ref[...] = dmo
        dg_ref[...] += jnp.sum(dxv * mo_ref[...], axis=0, keepdims=True)
        dmix_ref[...] = lax.dot_general(dmo, w_ref[...], (((1,), (1,)), ((), ())), preferred_element_type=F32)

        @pl.when(i == steps - 1)
        def _():
            dng_ref[...] = da_acc[...] * (1.0 + sc_ref[...])
            dsc_ref[...] = da_acc[...] * ng_ref[...]
            dsh_ref[...] = dsh_acc[...]

    tile = pl.BlockSpec((tm, d), lambda i: (i, 0))
    row = pl.BlockSpec((1, d), lambda i: (0, 0))
    mix_tile = pl.BlockSpec((tm, w_out.shape[0]), lambda i: (i, 0))
    row_shape = jax.ShapeDtypeStruct((1, d), F32)
    return pl.pallas_call(
        body, grid=(steps,), in_specs=[tile, tile, tile, row, row, row, tile, _resident(w_out.shape)],
        out_specs=[tile, row, row, row, mix_tile, tile, row],
        out_shape=[jax.ShapeDtypeStruct((s, d), F32), row_shape, row_shape, row_shape,
                   jax.ShapeDtypeStruct((s, w_out.shape[0]), F32), jax.ShapeDtypeStruct((s, d), BF16), row_shape],
        scratch_shapes=[pltpu.VMEM((1, d), F32), pltpu.VMEM((1, d), F32)], name=name,
        compiler_params=_cparams("arbitrary"))(x1, dh2, dx2, norm_g, sc, gate, mo, w_out)


def _final_loss(x, g, target, *, name, tm=ROW_TILE):
    s, d = x.shape
    steps = s // tm

    def body(x_ref, g_ref, t_ref, loss_ref, dx_ref, dg_ref, sq_acc):
        i = pl.program_id(0)

        @pl.when(i == 0)
        def _():
            sq_acc[...] = jnp.zeros_like(sq_acc)
            dg_ref[...] = jnp.zeros_like(dg_ref)

        xv = x_ref[...]
        r = lax.rsqrt(jnp.mean(xv * xv, axis=-1, keepdims=True) + NORM_EPS)
        xhat = xv * r
        err = xhat * g_ref[...] - t_ref[...]
        sq_acc[...] += jnp.sum(err * err, axis=0, keepdims=True)
        dy = err * (1.0 / d)
        dg_ref[...] += jnp.sum(dy * xhat, axis=0, keepdims=True)
        dxhat = dy * g_ref[...]
        dx_ref[...] = r * (dxhat - xhat * jnp.mean(dxhat * xhat, axis=-1, keepdims=True))

        @pl.when(i == steps - 1)
        def _():
            total = jnp.sum(sq_acc[...], axis=1, keepdims=True) * (0.5 / d)
            loss_ref[...] = jnp.broadcast_to(total, loss_ref.shape)

    tile = pl.BlockSpec((tm, d), lambda i: (i, 0))
    row = pl.BlockSpec((1, d), lambda i: (0, 0))
    return pl.pallas_call(
        body, grid=(steps,), in_specs=[tile, row, tile],
        out_specs=[pl.BlockSpec((1, 128), lambda i: (0, 0)), tile, row],
        out_shape=[jax.ShapeDtypeStruct((1, 128), F32), jax.ShapeDtypeStruct((s, d), F32),
                   jax.ShapeDtypeStruct((1, d), F32)],
        scratch_shapes=[pltpu.VMEM((1, d), F32)], name=name, compiler_params=_cparams("arbitrary"))(x, g, target)


def _ssd_chunk(z, xbc_ext, dt_raw, conv_w, conv_b, dt_bias, a_log, d_skip, norm_g, h_in):
    q = z.shape[0]
    gw = SSD_HPG * HEAD_LANES
    xc = conv_b
    for k in range(SSD_CONV_K):
        xc = xc + _pick_row(conv_w, k) * _shift_rows(xbc_ext, SSD_CONV_K - 1 - k, CONV_HALO)
    xc = _silu(xc)
    dt = jax.nn.softplus(dt_raw + dt_bias)
    da = dt * (-jnp.exp(a_log))
    ri = lax.broadcasted_iota(jnp.int32, (q, q), 0)
    ci = lax.broadcasted_iota(jnp.int32, (q, q), 1)
    causal = ri >= ci
    tril = causal.astype(F32)
    a_cum = jnp.dot(tril, da, preferred_element_type=F32, precision=lax.Precision.HIGHEST)
    a_cum_t = lax.dot_general(da, tril, (((0,), (1,)), ((), ())), preferred_element_type=F32,
                              precision=lax.Precision.HIGHEST)
    a_last = _pick_row(a_cum, q - 1)
    head = _head_of_lane(gw)
    ys, hs = [], []
    for g in range(2):
        xs = xc[:, gw * g:gw * (g + 1)]
        bm = xc[:, SSD_INNER + SSD_STATE * g:SSD_INNER + SSD_STATE * (g + 1)]
        cm = xc[:, SSD_INNER + 2 * SSD_STATE + SSD_STATE * g:SSD_INNER + 2 * SSD_STATE + SSD_STATE * (g + 1)]
        cb = lax.dot_general(cm.astype(BF16), bm.astype(BF16), (((1,), (1,)), ((), ())), preferred_element_type=F32)
        cols = [_pick_lane(a_cum, SSD_HPG * g + j) for j in range(SSD_HPG)]
        lasts = [_pick_lane(a_last, SSD_HPG * g + j) for j in range(SSD_HPG)]
        dt_exp = sum(jnp.where(head == j, _pick_lane(dt, SSD_HPG * g + j), 0.0) for j in range(SSD_HPG))
        d_exp = sum(jnp.where(head == j, _pick_lane(d_skip, SSD_HPG * g + j), 0.0) for j in range(SSD_HPG))
        e_cum = sum(jnp.where(head == j, jnp.exp(cols[j]), 0.0) for j in range(SSD_HPG))
        c_dec = sum(jnp.where(head == j, jnp.exp(lasts[j]), 0.0) for j in range(SSD_HPG))
        xsdt = (xs * dt_exp).astype(BF16)
        y_diag = jnp.zeros((q, gw), F32)
        st_new = jnp.zeros((SSD_STATE, gw), F32)
        for j in range(SSD_HPG):
            row = _pick_row(a_cum_t, SSD_HPG * g + j)
            lmat = jnp.exp(jnp.where(causal, cols[j] - row, -jnp.inf))
            r = jnp.dot((cb * lmat).astype(BF16), xsdt, preferred_element_type=F32)
            y_diag = y_diag + jnp.where(head == j, r, 0.0)
            bd = (bm * jnp.exp(lasts[j] - cols[j])).astype(BF16)
            st = lax.dot_general(bd, xsdt, (((0,), (0,)), ((), ())), preferred_element_type=F32)
            st_new = st_new + jnp.where(head == j, st, 0.0)
        y_off = jnp.dot(cm.astype(BF16), h_in[g].astype(BF16), preferred_element_type=F32) * e_cum
        hs.append(h_in[g] * c_dec + st_new)
        y = y_diag + y_off + d_exp * xs
        yz = y * _silu(z[:, gw * g:gw * (g + 1)])
        yz = yz * lax.rsqrt(jnp.mean(yz * yz, axis=-1, keepdims=True) + NORM_EPS)
        ys.append(yz * norm_g[:, gw * g:gw * (g + 1)])
    return jnp.concatenate(ys, axis=1), tuple(hs)


_SSD_NCHUNK = SEQ // SSD_CHUNK
_HALO_PER_CHUNK = SSD_CHUNK // CONV_HALO


def _ssd_param_specs(const):
    return [pl.BlockSpec((8, SSD_CONV_CH), const), pl.BlockSpec((1, SSD_CONV_CH), const),
            pl.BlockSpec((1, 128), const), pl.BlockSpec((1, 128), const), pl.BlockSpec((1, 128), const),
            pl.BlockSpec((1, SSD_INNER), const)]


def _ssd_fwd(proj, conv_w, conv_b, dt_bias, a_log, d_skip, norm_g, *, name):
    q = SSD_CHUNK

    def body(z_ref, xbc_ref, halo_ref, dt_ref, cw_ref, cb_ref, db_ref, al_ref, d_ref, ng_ref, y_ref, hs_ref, h_acc):
        i = pl.program_id(0)

        @pl.when(i == 0)
        def _():
            h_acc[...] = jnp.zeros_like(h_acc)

        halo = jnp.where(i == 0, 0.0, halo_ref[...])
        xbc_ext = jnp.concatenate([halo, xbc_ref[...]], axis=0)
        h_in = (h_acc[0], h_acc[1])
        hs_ref[0, 0] = h_in[0]
        hs_ref[0, 1] = h_in[1]
        y, h_out = _ssd_chunk(z_ref[...], xbc_ext, dt_ref[...], cw_ref[...], cb_ref[...], db_ref[...], al_ref[...],
                              d_ref[...], ng_ref[...], h_in)
        y_ref[...] = y.astype(BF16)
        h_acc[0] = h_out[0]
        h_acc[1] = h_out[1]

    const = lambda i: (0, 0)
    return pl.pallas_call(
        body, grid=(_SSD_NCHUNK,),
        in_specs=[pl.BlockSpec((q, SSD_INNER), lambda i: (i, PROJ_Z_BLK)),
                  pl.BlockSpec((q, SSD_CONV_CH), lambda i: (i, 0)),
                  pl.BlockSpec((CONV_HALO, SSD_CONV_CH), lambda i: (jnp.maximum(i * _HALO_PER_CHUNK - 1, 0), 0)),
                  pl.BlockSpec((q, 128), lambda i: (i, PROJ_DT_BLK))] + _ssd_param_specs(const),
        out_specs=[pl.BlockSpec((q, SSD_INNER), lambda i: (i, 0)),
                   pl.BlockSpec((1, 2, SSD_STATE, 256), lambda i: (i, 0, 0, 0))],
        out_shape=[jax.ShapeDtypeStruct((SEQ, D_MODEL), BF16),
                   jax.ShapeDtypeStruct((_SSD_NCHUNK, 2, SSD_STATE, 256), F32)],
        scratch_shapes=[pltpu.VMEM((2, SSD_STATE, 256), F32)], name=name,
        compiler_params=_cparams("arbitrary"))(proj, proj, proj, proj, conv_w, conv_b, dt_bias, a_log, d_skip, norm_g)


def _ssd_bwd(proj, hstates, dmix, conv_w, conv_b, dt_bias, a_log, d_skip, norm_g, *, name):
    q = SSD_CHUNK
    last = _SSD_NCHUNK - 1

    def body(z_ref, xbc_ref, halo_ref, dt_ref, hs_ref, dy_ref, cw_ref, cb_ref, db_ref, al_ref, d_ref, ng_ref,
             dp_ref, dcw_ref, dcb_ref, ddb_ref, dal_ref, dd_ref, dng_ref, dh_acc, dhalo_acc):
        i = pl.program_id(0)

        @pl.when(i == 0)
        def _():
            dh_acc[...] = jnp.zeros_like(dh_acc)
            dhalo_acc[...] = jnp.zeros_like(dhalo_acc)
            for r in (dcw_ref, dcb_ref, ddb_ref, dal_ref, dd_ref, dng_ref):
                r[...] = jnp.zeros_like(r)

        halo = jnp.where(i == last, 0.0, halo_ref[...])
        xbc_ext = jnp.concatenate([halo, xbc_ref[...]], axis=0)
        _, vjp = jax.vjp(_ssd_chunk, z_ref[...], xbc_ext, dt_ref[...], cw_ref[...], cb_ref[...], db_ref[...],
                         al_ref[...], d_ref[...], ng_ref[...], (hs_ref[0, 0], hs_ref[0, 1]))
        gz, gx, gdt, gcw, gcb, gdb, gal, gd, gng, gh = vjp((dy_ref[...], (dh_acc[0], dh_acc[1])))
        dxbc = jnp.concatenate([gx[CONV_HALO:q], gx[q:] + dhalo_acc[...]], axis=0)
        dp_ref[...] = jnp.concatenate([dxbc, gz, gdt, jnp.zeros_like(gdt)], axis=1).astype(BF16)
        dhalo_acc[...] = gx[:CONV_HALO]
        dh_acc[0] = gh[0]
        dh_acc[1] = gh[1]
        dcw_ref[...] += gcw
        dcb_ref[...] += gcb
        ddb_ref[...] += gdb
        dal_ref[...] += gal
        dd_ref[...] += gd
        dng_ref[...] += gng

    const = lambda i: (0, 0)
    rev = lambda i: last - i
    row = lambda n: jax.ShapeDtypeStruct((1, n), F32)
    return pl.pallas_call(
        body, grid=(_SSD_NCHUNK,),
        in_specs=[pl.BlockSpec((q, SSD_INNER), lambda i: (rev(i), PROJ_Z_BLK)),
                  pl.BlockSpec((q, SSD_CONV_CH), lambda i: (rev(i), 0)),
                  pl.BlockSpec((CONV_HALO, SSD_CONV_CH), lambda i: (jnp.maximum(rev(i) * _HALO_PER_CHUNK - 1, 0), 0)),
                  pl.BlockSpec((q, 128), lambda i: (rev(i), PROJ_DT_BLK)),
                  pl.BlockSpec((1, 2, SSD_STATE, 256), lambda i: (rev(i), 0, 0, 0)),
                  pl.BlockSpec((q, SSD_INNER), lambda i: (rev(i), 0))] + _ssd_param_specs(const),
        out_specs=[pl.BlockSpec((q, PROJ_SSD_W), lambda i: (rev(i), 0))] + _ssd_param_specs(const),
        out_shape=[jax.ShapeDtypeStruct((SEQ, PROJ_W), BF16), jax.ShapeDtypeStruct((8, SSD_CONV_CH), F32),
                   row(SSD_CONV_CH), row(128), row(128), row(128), row(SSD_INNER)],
        scratch_shapes=[pltpu.VMEM((2, SSD_STATE, 256), F32), pltpu.VMEM((CONV_HALO, SSD_CONV_CH), F32)], name=name,
        compiler_params=_cparams("arbitrary"))(proj, proj, proj, proj, hstates, dmix, conv_w, conv_b, dt_bias, a_log,
                                                d_skip, norm_g)


def _rope_tables(pos_col, inv_freq_lane, *, name):
    s = pos_col.shape[0]

    def body(p_ref, f_ref, c_ref, s1_ref, s2_ref):
        ang = p_ref[...] * f_ref[...]
        within = lax.broadcasted_iota(jnp.int32, ang.shape, 1) % HEAD_LANES
        half = ROT_DIM // 2
        c_ref[...] = jnp.where(within < ROT_DIM, jnp.cos(ang), 1.0)
        sn = jnp.sin(ang)
        s1_ref[...] = jnp.where(within < half, -sn, 0.0)
        s2_ref[...] = jnp.where((within >= half) & (within < ROT_DIM), sn, 0.0)

    shp = jax.ShapeDtypeStruct((s, 128), F32)
    return pl.pallas_call(body, out_shape=[shp, shp, shp], name=name,
                          compiler_params=pltpu.CompilerParams(vmem_limit_bytes=VMEM_LIMIT))(pos_col, inv_freq_lane)


def _rope(t, c, s1, s2):
    half = ROT_DIM // 2
    return t * c + pltpu.roll(t, 128 - half, 1) * s1 + pltpu.roll(t, half, 1) * s2


def _rope_t(g, c, s1, s2):
    half = ROT_DIM // 2
    return g * c + pltpu.roll(g * s1, half, 1) + pltpu.roll(g * s2, 128 - half, 1)


def _att_valid(b):
    qi = lax.broadcasted_iota(jnp.int32, (ATT_BLOCK, 2 * ATT_BLOCK), 0)
    kj = lax.broadcasted_iota(jnp.int32, (ATT_BLOCK, 2 * ATT_BLOCK), 1)
    rel = qi + ATT_BLOCK - kj
    return (rel >= 0) & (rel <= ATT_BLOCK) & (b * ATT_BLOCK + kj - ATT_BLOCK >= 0)


def _att_slices(i, d):
    if d == 1:
        qstart = pl.multiple_of(i * ATT_BLOCK, ATT_BLOCK)
        return i, pl.ds(qstart, ATT_BLOCK), pl.ds(pl.multiple_of(qstart - ATT_BLOCK + ATT_KPAD, ATT_BLOCK), 2 * ATT_BLOCK)
    r = i % d
    b = i // d
    qstart = r + d * ATT_BLOCK * b
    return b, pl.ds(qstart, ATT_BLOCK, stride=d), pl.ds(qstart - ATT_BLOCK * d + ATT_KPAD, 2 * ATT_BLOCK, stride=d)


_ATT_NBLK = SEQ // ATT_BLOCK
_ATT_SCALE = HEAD_LANES ** -0.5
_ATT_UNROLL_FWD = 8
_ATT_UNROLL = 4


def _att_fwd(proj, cos, sin1, sin2, mix, *, name):
    s = SEQ

    def body(q_ref, k_ref, v_ref, c_ref, s1_ref, s2_ref, _, o_ref, lse_ref, mix_ref, qs, ks, vs, acc, m_s, l_s):
        c, s1, s2 = c_ref[...], s1_ref[...], s2_ref[...]
        qs[...] = _rope(q_ref[...], c, s1, s2) * _ATT_SCALE
        zeros = jnp.zeros((ATT_KPAD, 128), F32)
        ks[pl.ds(0, ATT_KPAD), :] = zeros
        vs[pl.ds(0, ATT_KPAD), :] = zeros
        ks[pl.ds(ATT_KPAD, s), :] = _rope(k_ref[...], c, s1, s2)
        vs[pl.ds(ATT_KPAD, s), :] = v_ref[...]
        head0 = _head_of_lane(128) == 0

        for bi, (_, d) in enumerate(ATT_PATTERNS):
            def blk(i, carry, d=d, first=(bi == 0)):
                b, sq, sk = _att_slices(i, d)
                qb = qs[sq, :]
                kw = ks[sk, :].astype(BF16)
                vw = vs[sk, :].astype(BF16)
                valid = _att_valid(b)
                ms, ls, os_ = [], [], []
                for hh in range(2):
                    qh = jnp.where(head0 if hh == 0 else ~head0, qb, 0.0).astype(BF16)
                    sc = lax.dot_general(qh, kw, (((1,), (1,)), ((), ())), preferred_element_type=F32)
                    sc = jnp.where(valid, sc, -jnp.inf)
                    mb = jnp.max(sc, axis=1, keepdims=True)
                    p = jnp.exp(sc - mb)
                    ms.append(mb)
                    ls.append(jnp.sum(p, axis=1, keepdims=True))
                    os_.append(jnp.dot(p.astype(BF16), vw, preferred_element_type=F32))
                m_b = jnp.where(head0, ms[0], ms[1])
                l_b = jnp.where(head0, ls[0], ls[1])
                o_b = jnp.where(head0, os_[0], os_[1])
                if first:
                    m_s[sq, :] = m_b
                    l_s[sq, :] = l_b
                    acc[sq, :] = o_b
                else:
                    m_old = m_s[sq, :]
                    m_new = jnp.maximum(m_old, m_b)
                    a_old = jnp.exp(m_old - m_new)
                    a_b = jnp.exp(m_b - m_new)
                    m_s[sq, :] = m_new
                    l_s[sq, :] = l_s[sq, :] * a_old + l_b * a_b
                    acc[sq, :] = acc[sq, :] * a_old + o_b * a_b
                return carry

            lax.fori_loop(0, _ATT_NBLK, blk, 0, unroll=_ATT_UNROLL_FWD)

        out = acc[...] / l_s[...]
        o_ref[...] = out
        mix_ref[...] = out.astype(BF16)
        lse_ref[...] = m_s[...] + jnp.log(l_s[...])

    col = lambda base: pl.BlockSpec((s, 128), lambda p: (0, base + p))
    tab = pl.BlockSpec((s, 128), lambda p: (0, 0))
    big = pltpu.VMEM((ATT_KPAD + s, 128), F32)
    tok = pltpu.VMEM((s, 128), F32)
    return pl.pallas_call(
        body, grid=(2,), in_specs=[col(PROJ_Q_BLK), col(PROJ_K_BLK), col(PROJ_V_BLK), tab, tab, tab, _HBM],
        out_specs=[pl.BlockSpec((s, 128), lambda p: (0, p)), pl.BlockSpec((s, 128), lambda p: (0, p)),
                   col(MIX_ATT_BLK)],
        out_shape=[jax.ShapeDtypeStruct((s, ATT_W), F32), jax.ShapeDtypeStruct((s, ATT_W), F32),
                   jax.ShapeDtypeStruct(mix.shape, mix.dtype)],
        input_output_aliases={6: 2}, scratch_shapes=[tok, big, big, tok, tok, tok], name=name,
        compiler_params=_cparams("arbitrary"))(proj, proj, proj, cos, sin1, sin2, mix)


def _att_bwd(proj, cos, sin1, sin2, out, lse, dmix, dproj, *, name):
    s = SEQ

    def body(proj_ref, c_hbm, s1_hbm, s2_hbm, out_hbm, lse_hbm, dmix_hbm, _, dproj_hbm,
             c_ref, s1_ref, s2_ref, o_ref, lse_ref, do_ref, qs, ks, vs, dqs, dks, dvs, staged, sems):
        def start(copies):
            for cp in copies:
                cp.start()
            return copies

        def load(pair):
            lanes = pl.ds(128 * pair, 128)
            rows = pl.ds(ATT_KPAD, s)
            return start([
                pltpu.make_async_copy(proj_ref.at[:, pl.ds(128 * (PROJ_Q_BLK + pair), 128)], qs, sems.at[0]),
                pltpu.make_async_copy(proj_ref.at[:, pl.ds(128 * (PROJ_K_BLK + pair), 128)], ks.at[rows, :], sems.at[1]),
                pltpu.make_async_copy(proj_ref.at[:, pl.ds(128 * (PROJ_V_BLK + pair), 128)], vs.at[rows, :], sems.at[2]),
                pltpu.make_async_copy(out_hbm.at[:, lanes], o_ref, sems.at[3]),
                pltpu.make_async_copy(lse_hbm.at[:, lanes], lse_ref, sems.at[4]),
                pltpu.make_async_copy(dmix_hbm.at[:, pl.ds(128 * (MIX_ATT_BLK + pair), 128)], do_ref, sems.at[5])])

        tables = start([pltpu.make_async_copy(c_hbm, c_ref, sems.at[6]),
                        pltpu.make_async_copy(s1_hbm, s1_ref, sems.at[7]),
                        pltpu.make_async_copy(s2_hbm, s2_ref, sems.at[8])])
        loads = load(0)
        for cp in tables:
            cp.wait()
        head0 = _head_of_lane(128) == 0
        zeros = jnp.zeros((ATT_KPAD, 128), F32)
        for pair in range(2):
            for cp in loads:
                cp.wait()
            c, s1, s2 = c_ref[...], s1_ref[...], s2_ref[...]
            qs[...] = _rope(qs[...], c, s1, s2) * _ATT_SCALE
            ks[pl.ds(0, ATT_KPAD), :] = zeros
            vs[pl.ds(0, ATT_KPAD), :] = zeros
            ks[pl.ds(ATT_KPAD, s), :] = _rope(ks[pl.ds(ATT_KPAD, s), :], c, s1, s2)
            dqs[...] = jnp.zeros_like(dqs)
            dks[...] = jnp.zeros_like(dks)
            dvs[...] = jnp.zeros_like(dvs)

            for _, d in ATT_PATTERNS:
                def blk(i, carry, d=d):
                    b, sq, sk = _att_slices(i, d)
                    qb = qs[sq, :]
                    kw = ks[sk, :].astype(BF16)
                    vw = vs[sk, :].astype(BF16)
                    dob = do_ref[sq, :]
                    lse_b = lse_ref[sq, :]
                    dd = dob * o_ref[sq, :]
                    valid = _att_valid(b)
                    dq_b = jnp.zeros((ATT_BLOCK, 128), F32)
                    dk_w = jnp.zeros((2 * ATT_BLOCK, 128), F32)
                    dv_w = jnp.zeros((2 * ATT_BLOCK, 128), F32)
                    for hh in range(2):
                        hm = head0 if hh == 0 else ~head0
                        qh = jnp.where(hm, qb, 0.0).astype(BF16)
                        doh = jnp.where(hm, dob, 0.0).astype(BF16)
                        lse_h = _pick_lane(lse_b, hh * HEAD_LANES)
                        d_h = jnp.sum(jnp.where(hm, dd, 0.0), axis=1, keepdims=True)
                        sc = lax.dot_general(qh, kw, (((1,), (1,)), ((), ())), preferred_element_type=F32)
                        p = jnp.where(valid, jnp.exp(sc - lse_h), 0.0)
                        dp = lax.dot_general(doh, vw, (((1,), (1,)), ((), ())), preferred_element_type=F32)
                        ds = (p * (dp - d_h)).astype(BF16)
                        dq_b = dq_b + jnp.where(hm, jnp.dot(ds, kw, preferred_element_type=F32), 0.0)
                        dk_w = dk_w + lax.dot_general(ds, qh, (((0,), (0,)), ((), ())), preferred_element_type=F32)
                        dv_w = dv_w + lax.dot_general(p.astype(BF16), doh, (((0,), (0,)), ((), ())),
                                                      preferred_element_type=F32)
                    dqs[sq, :] += dq_b
                    dks[sk, :] += dk_w
                    dvs[sk, :] += dv_w
                    return carry

                lax.fori_loop(0, _ATT_NBLK, blk, 0, unroll=_ATT_UNROLL)

            staged[0] = _rope_t(dqs[...] * _ATT_SCALE, c, s1, s2).astype(BF16)
            staged[1] = _rope_t(dks[pl.ds(ATT_KPAD, s), :], c, s1, s2).astype(BF16)
            staged[2] = dvs[pl.ds(ATT_KPAD, s), :].astype(BF16)
            stores = start([
                pltpu.make_async_copy(staged.at[j], dproj_hbm.at[:, pl.ds(128 * (col + pair), 128)], sems.at[9 + j])
                for j, col in enumerate((PROJ_Q_BLK, PROJ_K_BLK, PROJ_V_BLK))])
            if pair == 0:
                loads = load(1)
            for cp in stores:
                cp.wait()

    big = pltpu.VMEM((ATT_KPAD + s, 128), F32)
    tok = pltpu.VMEM((s, 128), F32)
    return pl.pallas_call(
        body, in_specs=[_HBM] * 8, out_specs=_HBM, out_shape=jax.ShapeDtypeStruct(dproj.shape, dproj.dtype),
        input_output_aliases={7: 0},
        scratch_shapes=[tok] * 6 + [tok, big, big, tok, big, big, pltpu.VMEM((3, s, 128), BF16),
                                    pltpu.SemaphoreType.DMA((12,))], name=name,
        compiler_params=pltpu.CompilerParams(vmem_limit_bytes=VMEM_LIMIT))(
            proj, cos, sin1, sin2, out, lse, dmix, dproj)


_POOL_TM = 512
_POOL_NT = SEQ // _POOL_TM
_POOL_HALO_PER_TILE = _POOL_TM // POOL_HALO


def _pool_tile(u_ext, w_bd, scale, t0):
    s2 = u_ext + _roll_rows(u_ext, 1)
    s4 = s2 + _roll_rows(s2, 2)
    s8 = s4 + _roll_rows(s4, 4)
    s16 = s8 + _roll_rows(s8, 8)
    grp = _head_of_lane(POOL_W)
    sel = jnp.where(grp == 0, s2, jnp.where(grp == 1, s4, jnp.where(grp == 2, s8, s16)))[POOL_HALO:]
    t = sel.shape[0]
    pos = t0 + lax.broadcasted_iota(jnp.int32, (t, POOL_W), 0) + 1
    win = jnp.where(grp == 0, 2, jnp.where(grp == 1, 4, jnp.where(grp == 2, 8, 16)))
    cnt = jnp.minimum(pos, win).astype(F32)
    diff = sel / cnt - u_ext[POOL_HALO:]
    return jnp.dot(diff.astype(BF16), w_bd.astype(BF16), preferred_element_type=F32) * scale


def _pool_fwd(proj, w_bd, scale, mix, *, name):
    tm = _POOL_TM

    def body(u_ref, halo_ref, w_ref, sc_ref, _, y_ref):
        i = pl.program_id(0)
        halo = jnp.where(i == 0, 0.0, halo_ref[...])
        u_ext = jnp.concatenate([halo, u_ref[...]], axis=0)
        y_ref[...] = _pool_tile(u_ext, w_ref[...], sc_ref[...], i * tm).astype(BF16)

    return pl.pallas_call(
        body, grid=(_POOL_NT,),
        in_specs=[pl.BlockSpec((tm, POOL_W), lambda i: (i, PROJ_POOL_BLK)),
                  pl.BlockSpec((POOL_HALO, POOL_W),
                               lambda i: (jnp.maximum(i * _POOL_HALO_PER_TILE - 1, 0), PROJ_POOL_BLK)),
                  pl.BlockSpec((POOL_W, POOL_W), lambda i: (0, 0)), pl.BlockSpec((1, POOL_W), lambda i: (0, 0)), _HBM],
        out_specs=pl.BlockSpec((tm, POOL_W), lambda i: (i, MIX_POOL_BLK)),
        out_shape=jax.ShapeDtypeStruct(mix.shape, mix.dtype), input_output_aliases={4: 0}, name=name,
        compiler_params=_cparams("parallel"))(proj, proj, w_bd, scale, mix)


def _pool_bwd(proj, dmix, w_bd, scale, dproj, *, name):
    tm = _POOL_TM
    last = _POOL_NT - 1

    def body(u_ref, halo_ref, dy_ref, w_ref, sc_ref, _, du_ref, dw_ref, dsc_ref, dhalo_acc):
        i = pl.program_id(0)

        @pl.when(i == 0)
        def _():
            dhalo_acc[...] = jnp.zeros_like(dhalo_acc)
            dw_ref[...] = jnp.zeros_like(dw_ref)
            dsc_ref[...] = jnp.zeros_like(dsc_ref)

        tile = last - i
        halo = jnp.where(tile == 0, 0.0, halo_ref[...])
        u_ext = jnp.concatenate([halo, u_ref[...]], axis=0)
        _, vjp = jax.vjp(functools.partial(_pool_tile, t0=tile * tm), u_ext, w_ref[...], sc_ref[...])
        gu, gw, gs = vjp(dy_ref[...])
        du_ref[...] = jnp.concatenate([gu[POOL_HALO:tm], gu[tm:] + dhalo_acc[...]], axis=0).astype(BF16)
        dhalo_acc[...] = gu[:POOL_HALO]
        dw_ref[...] += gw
        dsc_ref[...] += gs

    rev = lambda i: last - i
    return pl.pallas_call(
        body, grid=(_POOL_NT,),
        in_specs=[pl.BlockSpec((tm, POOL_W), lambda i: (rev(i), PROJ_POOL_BLK)),
                  pl.BlockSpec((POOL_HALO, POOL_W),
                               lambda i: (jnp.maximum(rev(i) * _POOL_HALO_PER_TILE - 1, 0), PROJ_POOL_BLK)),
                  pl.BlockSpec((tm, POOL_W), lambda i: (rev(i), MIX_POOL_BLK)),
                  pl.BlockSpec((POOL_W, POOL_W), lambda i: (0, 0)), pl.BlockSpec((1, POOL_W), lambda i: (0, 0)), _HBM],
        out_specs=[pl.BlockSpec((tm, POOL_W), lambda i: (rev(i), PROJ_POOL_BLK)),
                   pl.BlockSpec((POOL_W, POOL_W), lambda i: (0, 0)), pl.BlockSpec((1, POOL_W), lambda i: (0, 0))],
        out_shape=[jax.ShapeDtypeStruct(dproj.shape, dproj.dtype), jax.ShapeDtypeStruct((POOL_W, POOL_W), F32),
                   jax.ShapeDtypeStruct((1, POOL_W), F32)],
        input_output_aliases={5: 0}, scratch_shapes=[pltpu.VMEM((POOL_HALO, POOL_W), F32)], name=name,
        compiler_params=_cparams("arbitrary"))(proj, proj, dmix, w_bd, scale, dproj)


_FFN_TM = 256
_FFN_NT = SEQ // _FFN_TM
_FFN_HALO_PER_TILE = _FFN_TM // CONV_HALO


def _ffn_act_tile(hid_ext, conv_w, conv_b):
    hc = conv_b
    for k in range(FFN_CONV_K):
        hc = hc + _pick_row(conv_w, k) * _shift_rows(hid_ext, FFN_CONV_K - 1 - k, CONV_HALO)
    return _silu(hc[:, :FFN_DIM]) * hc[:, FFN_DIM:]


def _ffn_fwd(x1, norm_g, sc, sh, gate, up_t, down, conv_w, conv_b, *, name):
    tm = _FFN_TM
    w = 2 * FFN_DIM
    d = D_MODEL

    def body(x_ref, ng_ref, sc_ref, sh_ref, g_ref, up_ref, dn_ref, cw_ref, cb_ref,
             h_ref, hid_ref, act_ref, f_ref, x2_ref, halo_acc):
        i = pl.program_id(0)
        h2 = _rms_modulate(x_ref[...], ng_ref[...], sc_ref[...], sh_ref[...]).astype(BF16)
        h_ref[...] = h2
        hid = lax.dot_general(h2, up_ref[...], (((1,), (1,)), ((), ())), preferred_element_type=F32)
        hid_ref[...] = hid
        halo = jnp.where(i == 0, 0.0, halo_acc[...])
        act = _ffn_act_tile(jnp.concatenate([halo, hid], axis=0), cw_ref[...], cb_ref[...]).astype(BF16)
        halo_acc[...] = hid[tm - CONV_HALO:]
        act_ref[...] = act
        f = jnp.dot(act, dn_ref[...], preferred_element_type=F32)
        f_ref[...] = f
        x2_ref[...] = x_ref[...] + g_ref[...] * f

    tile = lambda n: pl.BlockSpec((tm, n), lambda i: (i, 0))
    return pl.pallas_call(
        body, grid=(_FFN_NT,),
        in_specs=[tile(d)] + [_resident((1, d))] * 4 + [_resident((w, d)), _resident((FFN_DIM, d)),
                                                        _resident((8, w)), _resident((1, w))],
        out_specs=[tile(d), tile(w), tile(FFN_DIM), tile(d), tile(d)],
        out_shape=[jax.ShapeDtypeStruct((SEQ, d), BF16), jax.ShapeDtypeStruct((SEQ, w), F32),
                   jax.ShapeDtypeStruct((SEQ, FFN_DIM), BF16), jax.ShapeDtypeStruct((SEQ, d), F32),
                   jax.ShapeDtypeStruct((SEQ, d), F32)],
        scratch_shapes=[pltpu.VMEM((CONV_HALO, w), F32)], name=name,
        compiler_params=_cparams("arbitrary"))(x1, norm_g, sc, sh, gate, up_t, down, conv_w, conv_b)


def _ffn_bwd(dx2, gate, f, hid, up_t, down, conv_w, conv_b, *, name):
    tm = _FFN_TM
    w = 2 * FFN_DIM
    d = D_MODEL
    last = _FFN_NT - 1

    def body(dx_ref, g_ref, f_ref, h_ref, halo_ref, up_ref, dn_ref, cw_ref, cb_ref,
             df_ref, dg_ref, dh_ref, dh2_ref, dcw_ref, dcb_ref, dhalo_acc):
        i = pl.program_id(0)

        @pl.when(i == 0)
        def _():
            dhalo_acc[...] = jnp.zeros_like(dhalo_acc)
            dcw_ref[...] = jnp.zeros_like(dcw_ref)
            dcb_ref[...] = jnp.zeros_like(dcb_ref)
            dg_ref[...] = jnp.zeros_like(dg_ref)

        dxv = dx_ref[...]
        df = (g_ref[...] * dxv).astype(BF16)
        df_ref[...] = df
        dg_ref[...] += jnp.sum(dxv * f_ref[...], axis=0, keepdims=True)
        dact = lax.dot_general(df, dn_ref[...], (((1,), (1,)), ((), ())), preferred_element_type=F32)
        halo = jnp.where(i == last, 0.0, halo_ref[...])
        hid_ext = jnp.concatenate([halo, h_ref[...]], axis=0)
        _, vjp = jax.vjp(_ffn_act_tile, hid_ext, cw_ref[...], cb_ref[...])
        gh, gw, gb = vjp(dact)
        dhid = jnp.concatenate([gh[CONV_HALO:tm], gh[tm:] + dhalo_acc[...]], axis=0).astype(BF16)
        dhalo_acc[...] = gh[:CONV_HALO]
        dh_ref[...] = dhid
        dh2_ref[...] = jnp.dot(dhid, up_ref[...], preferred_element_type=F32)
        dcw_ref[...] += gw
        dcb_ref[...] += gb

    rev = lambda i: last - i
    tile = lambda n: pl.BlockSpec((tm, n), lambda i: (rev(i), 0))
    acc = lambda shape: pl.BlockSpec(shape, lambda i: (0, 0))
    return pl.pallas_call(
        body, grid=(_FFN_NT,),
        in_specs=[tile(d), _resident((1, d)), tile(d), tile(w),
                  pl.BlockSpec((CONV_HALO, w), lambda i: (jnp.maximum(rev(i) * _FFN_HALO_PER_TILE - 1, 0), 0)),
                  _resident((w, d)), _resident((FFN_DIM, d)), _resident((8, w)), _resident((1, w))],
        out_specs=[tile(d), acc((1, d)), tile(w), tile(d), acc((8, w)), acc((1, w))],
        out_shape=[jax.ShapeDtypeStruct((SEQ, d), BF16), jax.ShapeDtypeStruct((1, d), F32),
                   jax.ShapeDtypeStruct((SEQ, w), BF16), jax.ShapeDtypeStruct((SEQ, d), F32),
                   jax.ShapeDtypeStruct((8, w), F32), jax.ShapeDtypeStruct((1, w), F32)],
        scratch_shapes=[pltpu.VMEM((CONV_HALO, w), F32)], name=name,
        compiler_params=_cparams("arbitrary"))(dx2, gate, f, hid, hid, up_t, down, conv_w, conv_b)


def _axes():
    return lax.axis_index("x"), lax.axis_index("y"), lax.axis_index("c")


def _handshake(peers):
    barrier = pltpu.get_barrier_semaphore()
    for peer in peers:
        pl.semaphore_signal(barrier, inc=1, device_id=peer, device_id_type=MESH)
    pl.semaphore_wait(barrier, len(peers))


def _allgather_body(x_refs, out_refs, send_sems, recv_sems, local_sems, own_barrier):
    n = len(x_refs)
    x, y, c = _axes()
    me, sibling = (x, y, c), (x, y, 1 - c)
    chips = [(1 - x, y), (x, 1 - y), (1 - x, 1 - y)]
    if own_barrier:
        _handshake([sibling] + [(*chip, c) for chip in chips])

    def slot(a, px, py, pc):
        return out_refs[a].at[4 * px + 2 * py + pc]

    def copy(a, k, block, to, src=None):
        return pltpu.make_async_remote_copy(
            src_ref=slot(a, *block) if src is None else src, dst_ref=slot(a, *block),
            send_sem=send_sems.at[a, k], recv_sem=recv_sems.at[a, k], device_id=to, device_id_type=MESH)

    mines, firsts = [], []
    for a in range(n):
        mines.append(pltpu.make_async_copy(x_refs[a], slot(a, *me), local_sems.at[a]))
        mines[-1].start()
        first = [copy(a, 0, me, sibling, src=x_refs[a])]
        first += [copy(a, 1 + j, me, (*chip, c), src=x_refs[a]) for j, chip in enumerate(chips)]
        for cp in first:
            cp.start()
        firsts += first
    passed = []
    for j, chip in enumerate(chips):
        for a in range(n):
            copy(a, 1 + j, (*chip, c), me).wait_recv()
            passed.append(copy(a, 4 + j, (*chip, c), sibling))
            passed[-1].start()
    for a in range(n):
        copy(a, 0, sibling, me).wait_recv()
    for j, chip in enumerate(chips):
        for a in range(n):
            copy(a, 4 + j, (*chip, 1 - c), me).wait_recv()
    for cp in firsts + passed:
        cp.wait_send()
    for cp in mines:
        cp.wait()


def _allgather_sems(n):
    return [pltpu.SemaphoreType.DMA((n, 7)), pltpu.SemaphoreType.DMA((n, 7)), pltpu.SemaphoreType.DMA((n,))]


def _allgather(xs, *, name):
    n = len(xs)

    def body(*refs):
        _allgather_body(refs[:n], refs[n:2 * n], *refs[2 * n:], own_barrier=False)

    return pl.pallas_call(
        body, out_shape=[jax.ShapeDtypeStruct((N_DEV,) + xb.shape, xb.dtype) for xb in xs],
        in_specs=[_HBM] * n, out_specs=[_HBM] * n, scratch_shapes=_allgather_sems(n), name=name)(*xs)


def _allgather_async(xs, *, name, collective_id):
    n = len(xs)
    x_refs = [jax.new_ref(xb, memory_space=pltpu.MemorySpace.HBM) for xb in xs]
    out_refs = [jax.empty_ref(jax.ShapeDtypeStruct((N_DEV,) + xb.shape, xb.dtype), memory_space=pltpu.MemorySpace.HBM)
                for xb in xs]

    @pl.kernel(mesh=plsc.ScalarSubcoreMesh(axis_name="sequencer", num_cores=1), name=name,
               scratch_types=tuple(_allgather_sems(n)),
               compiler_params=pltpu.CompilerParams(collective_id=collective_id))
    def launch(send_sems, recv_sems, local_sems):
        _allgather_body(x_refs, out_refs, send_sems, recv_sems, local_sems, own_barrier=True)

    launch()
    return [r[...] for r in out_refs]


def _pair_exchange(blocks, *, name, collective_id):
    n = len(blocks)
    hbm = pltpu.MemorySpace.HBM
    in_refs = [jax.new_ref(b, memory_space=hbm) for b in blocks]
    out_refs = [jax.empty_ref(jax.ShapeDtypeStruct((4,) + b.shape[1:], b.dtype), memory_space=hbm) for b in blocks]

    @pl.kernel(mesh=plsc.ScalarSubcoreMesh(axis_name="sequencer", num_cores=1), name=name,
               scratch_types=(pltpu.SemaphoreType.DMA((n, 4)), pltpu.SemaphoreType.DMA((n, 4))),
               compiler_params=pltpu.CompilerParams(collective_id=collective_id))
    def launch(send_sems, recv_sems):
        x, y, c = _axes()
        _handshake([(x, y, 1 - c)])
        copies = [pltpu.make_async_remote_copy(
            src_ref=in_refs[a].at[2 * s + (1 - c)], dst_ref=out_refs[a].at[s], send_sem=send_sems.at[a, s],
            recv_sem=recv_sems.at[a, s], device_id=(x, y, 1 - c), device_id_type=MESH)
            for a in range(n) for s in range(4)]
        for cp in copies:
            cp.start()
        for cp in copies:
            cp.wait_recv()
        for cp in copies:
            cp.wait_send()

    launch()
    return [r[...] for r in out_refs]


def _chip_exchange(parts, *, name, collective_id):
    n = len(parts)
    hbm = pltpu.MemorySpace.HBM
    in_refs = [jax.new_ref(p, memory_space=hbm) for p in parts]
    out_refs = [jax.empty_ref(jax.ShapeDtypeStruct(p.shape, p.dtype), memory_space=hbm) for p in parts]

    @pl.kernel(mesh=plsc.ScalarSubcoreMesh(axis_name="sequencer", num_cores=1), name=name,
               scratch_types=(pltpu.SemaphoreType.DMA((n, 3)), pltpu.SemaphoreType.DMA((n, 3)),
                              pltpu.SemaphoreType.DMA((n,))),
               compiler_params=pltpu.CompilerParams(collective_id=collective_id))
    def launch(send_sems, recv_sems, local_sems):
        x, y, c = _axes()
        my_chip = 2 * x + y
        chips = [(1 - x, y), (x, 1 - y), (1 - x, 1 - y)]
        _handshake([(*chip, c) for chip in chips])
        locals_ = [pltpu.make_async_copy(in_refs[a].at[my_chip], out_refs[a].at[my_chip], local_sems.at[a])
                   for a in range(n)]
        for cp in locals_:
            cp.start()
        copies = [pltpu.make_async_remote_copy(
            src_ref=in_refs[a].at[2 * px + py], dst_ref=out_refs[a].at[my_chip], send_sem=send_sems.at[a, k],
            recv_sem=recv_sems.at[a, k], device_id=(px, py, c), device_id_type=MESH)
            for a in range(n) for k, (px, py) in enumerate(chips)]
        for cp in copies:
            cp.start()
        for cp in copies:
            cp.wait_recv()
        for cp in copies:
            cp.wait_send()
        for cp in locals_:
            cp.wait()

    launch()
    return [r[...] for r in out_refs]


def _pair_sum(core, blocks, from_sibling, *, name):
    n = len(blocks)

    def body(core_ref, *refs):
        for a_ref, b_ref, o_ref in zip(refs[:n], refs[n:2 * n], refs[2 * n:]):
            o_ref[...] = (a_ref[...].astype(F32) + b_ref[...].astype(F32)).astype(o_ref.dtype)

    mine = lambda b: pl.BlockSpec((1,) + b.shape[1:], lambda s, core_ref: (2 * s + core_ref[0], 0, 0))
    slot = lambda b: pl.BlockSpec((1,) + b.shape[1:], lambda s, core_ref: (s, 0, 0))
    return pl.pallas_call(
        body,
        grid_spec=pltpu.PrefetchScalarGridSpec(
            num_scalar_prefetch=1, grid=(4,),
            in_specs=[mine(b) for b in blocks] + [slot(b) for b in blocks], out_specs=[slot(b) for b in blocks]),
        out_shape=[jax.ShapeDtypeStruct(s.shape, s.dtype) for s in from_sibling], name=name,
        compiler_params=_cparams("parallel"))(core, *blocks, *from_sibling)


def _sum_blocks(a, *, name, tr=None):
    n, r, cdim = a.shape
    tr = tr or r

    def body(a_ref, o_ref):
        acc = a_ref[0].astype(F32)
        for k in range(1, n):
            acc = acc + a_ref[k].astype(F32)
        o_ref[...] = acc

    return pl.pallas_call(body, grid=(r // tr,), in_specs=[pl.BlockSpec((n, tr, cdim), lambda i: (0, i, 0))],
                          out_specs=pl.BlockSpec((tr, cdim), lambda i: (i, 0)),
                          out_shape=jax.ShapeDtypeStruct((r, cdim), F32), name=name,
                          compiler_params=_cparams("parallel"))(a)


def _sum_gathered(gathered, *, name):
    n = len(gathered)

    def body(*refs):
        for a_ref, o_ref in zip(refs[:n], refs[n:]):
            acc = a_ref[0]
            for k in range(1, N_DEV):
                acc = acc + a_ref[k]
            o_ref[...] = acc

    return pl.pallas_call(body, out_shape=[jax.ShapeDtypeStruct(g.shape[1:], F32) for g in gathered], name=name,
                          compiler_params=pltpu.CompilerParams(vmem_limit_bytes=VMEM_LIMIT))(*gathered)


_ADA_SHARD = 6 * D_MODEL // N_DEV


def _ada_mod(c_all, ada_w, *, name):
    def body(c_ref, w_ref, o_ref):
        o_ref[0] = jnp.dot(_silu(c_ref[...]).astype(BF16), w_ref[0].astype(BF16), preferred_element_type=F32)

    return pl.pallas_call(
        body, grid=(DEPTH,),
        in_specs=[pl.BlockSpec((N_DEV, D_MODEL), lambda l: (0, 0)),
                  pl.BlockSpec((1, D_MODEL, _ADA_SHARD), lambda l: (l, 0, 0))],
        out_specs=pl.BlockSpec((1, N_DEV, _ADA_SHARD), lambda l: (l, 0, 0)),
        out_shape=jax.ShapeDtypeStruct((DEPTH, N_DEV, _ADA_SHARD), F32), name=name,
        compiler_params=_cparams("parallel"))(c_all, ada_w)


def _ada_wgrad(c_all, dmod_cols, *, name):
    def body(c_ref, d_ref, o_ref):
        o_ref[0] = lax.dot_general(_silu(c_ref[...]), d_ref[0], (((0,), (0,)), ((), ())),
                                   preferred_element_type=F32, precision=lax.Precision.HIGHEST)

    return pl.pallas_call(
        body, grid=(DEPTH,),
        in_specs=[pl.BlockSpec((N_DEV, D_MODEL), lambda l: (0, 0)),
                  pl.BlockSpec((1, N_DEV, _ADA_SHARD), lambda l: (l, 0, 0))],
        out_specs=pl.BlockSpec((1, D_MODEL, _ADA_SHARD), lambda l: (l, 0, 0)),
        out_shape=jax.ShapeDtypeStruct((DEPTH, D_MODEL, _ADA_SHARD), F32), name=name,
        compiler_params=_cparams("parallel"))(c_all, dmod_cols)


def _add_rows(a, b, *, name):
    def body(a_ref, b_ref, o_ref):
        o_ref[...] = a_ref[...] + b_ref[...]

    return pl.pallas_call(body, out_shape=jax.ShapeDtypeStruct(a.shape, a.dtype), name=name)(a, b)


def _adamw_update(w_ref, g_ref, m_ref, v_ref, d_ref, mo_ref, vo_ref):
    gv = g_ref[...]
    mn = ADAM_B1 * m_ref[...] + (1.0 - ADAM_B1) * gv
    vn = ADAM_B2 * v_ref[...] + (1.0 - ADAM_B2) * (gv * gv)
    mo_ref[...] = mn
    vo_ref[...] = vn
    m_hat = mn / (1.0 - ADAM_B1 ** ADAM_STEP)
    v_hat = vn / (1.0 - ADAM_B2 ** ADAM_STEP)
    d_ref[...] = -ADAM_LR * (m_hat / (jnp.sqrt(v_hat) + ADAM_EPS) + ADAM_WD * w_ref[...])


def _adamw_small(ws, gs, ms, vs, *, name):
    n = len(ws)

    def body(*refs):
        ins, outs = refs[:4 * n], refs[4 * n:]
        for i in range(n):
            _adamw_update(ins[i], ins[n + i], ins[2 * n + i], ins[3 * n + i], outs[i], outs[n + i], outs[2 * n + i])

    shapes = [jax.ShapeDtypeStruct(a.shape, F32) for a in ws]
    outs = pl.pallas_call(body, out_shape=shapes * 3, name=name,
                          compiler_params=pltpu.CompilerParams(vmem_limit_bytes=VMEM_LIMIT))(*ws, *gs, *ms, *vs)
    return outs[:n], outs[n:2 * n], outs[2 * n:]


def _adamw(w, g, m, v, *, name, tr):
    r, cdim = w.shape
    body = functools.partial(_adamw_update)

    spec = pl.BlockSpec((tr, cdim), lambda i: (i, 0))
    shp = jax.ShapeDtypeStruct((r, cdim), F32)
    return pl.pallas_call(body, grid=(r // tr,), in_specs=[spec] * 4, out_specs=[spec] * 3, out_shape=[shp] * 3,
                          name=name, compiler_params=_cparams("parallel"))(w, g, m, v)


def _pad_rows(a, rows):
    return jnp.concatenate([a, jnp.zeros((rows - a.shape[0],) + a.shape[1:], a.dtype)], axis=0)


def _pad_lanes(a, lanes):
    return jnp.concatenate([a, jnp.zeros(a.shape[:-1] + (lanes - a.shape[-1],), a.dtype)], axis=-1)


def _permute_w_in(wt):
    return jnp.concatenate([wt[512:1536], wt[:512], wt[1536:1544],
                            jnp.zeros((PROJ_W - IN_W, wt.shape[1]), wt.dtype), wt[1544:]], axis=0)


def _unpermute_w_in(wp):
    return jnp.concatenate([wp[1024:1536], wp[:1024], wp[1536:1544], wp[PROJ_SSD_W:]], axis=0)


def _block_diag(w):
    rows = []
    for g in range(4):
        rows.append(jnp.concatenate([w[g] if k == g else jnp.zeros_like(w[g]) for k in range(4)], axis=1))
    return jnp.concatenate(rows, axis=0)


def _diag_blocks(wbd):
    return jnp.stack([wbd[64 * g:64 * (g + 1), 64 * g:64 * (g + 1)] for g in range(4)], axis=0)


def _layer_params(l, small):
    return dict(
        norm1_g=small["norm1_g"][l][None], norm2_g=small["norm2_g"][l][None],
        conv_w=_pad_rows(small["ssd_conv_w"][l], 8), conv_b=small["ssd_conv_b"][l][None],
        dt_bias=_pad_lanes(small["ssd_dt_bias"][l][None], 128), a_log=_pad_lanes(small["ssd_a_log"][l][None], 128),
        d_skip=_pad_lanes(small["ssd_d"][l][None], 128), ssd_norm_g=small["ssd_norm_g"][l][None],
        pool_bd=_block_diag(small["pool_w"][l]), pool_scale=small["pool_scale"][l][None],
        fcw=_pad_rows(small["ffn_conv_w"][l], 8), fcb=small["ffn_conv_b"][l][None])


def _mod_rows(mod_l):
    return [mod_l[None, D_MODEL * i:D_MODEL * (i + 1)] for i in range(6)]


def _layer_fwd(x, mod_l, p, tabs, l, gather):
    sh1, sc1, g1, sh2, sc2, g2 = _mod_rows(mod_l)
    mix_w = gather(l, "mix", None)
    p.update(w_in=mix_w["w_in"], w_out=mix_w["w_out"])
    proj, h1 = _mm(x, p["w_in"], nt=True, norm=(p["norm1_g"], sc1, sh1), name=f"l{l}_proj")
    ffn_w = gather(l, "ffn", proj)
    p.update(up=ffn_w["ffn_up"], down=ffn_w["ffn_down"])
    mix, hst = _ssd_fwd(proj, p["conv_w"], p["conv_b"], p["dt_bias"], p["a_log"], p["d_skip"], p["ssd_norm_g"],
                        name=f"l{l}_ssd")
    mix = _pool_fwd(proj, p["pool_bd"], p["pool_scale"], mix, name=f"l{l}_pool")
    y_att, lse, mix = _att_fwd(proj, *tabs, mix, name=f"l{l}_att")
    mo, x1 = _mm(mix, p["w_out"], residual=(x, g1), name=f"l{l}_out")
    gather(l + 1, "mix", (x1, p["up"]))
    h2, hid, act, f, x2 = _ffn_fwd(x1, p["norm2_g"], sc2, sh2, g2, p["up"], p["down"], p["fcw"], p["fcb"],
                                   name=f"l{l}_ffn")
    return x2, dict(x=x, h1=h1, proj=proj, hst=hst, y_att=y_att, lse=lse, mix=mix, mo=mo, x1=x1, h2=h2, hid=hid,
                    act=act, f=f)


def _layer_bwd(dx2, sv, mod_l, p, tabs, l, exchange):
    sh1, sc1, g1, sh2, sc2, g2 = _mod_rows(mod_l)
    df, dg2, dhid, dh2, dfcw, dfcb = _ffn_bwd(dx2, g2, sv["f"], sv["hid"], p["up"], p["down"], p["fcw"], p["fcb"],
                                              name=f"l{l}_ffn_b")
    d_down = _wgrad(sv["act"], df, tk=1408, name=f"l{l}_down_bw")
    d_up = _wgrad(dhid, sv["h2"], tk=1408, name=f"l{l}_up_bw")
    exchange(l, "ffn", dict(ffn_up=d_up, ffn_down=d_down))
    dx1, dn2, dsc2, dsh2, dmix, dmo, dg1 = _mid_bwd(sv["x1"], dh2, dx2, p["norm2_g"], sc2, g1, sv["mo"], p["w_out"],
                                                    name=f"l{l}_mid_b")
    d_wout = _wgrad(sv["mix"], dmo, name=f"l{l}_out_bw")
    dproj, dcw, dcb, ddb, dal, dd, dng = _ssd_bwd(
        sv["proj"], sv["hst"], dmix, p["conv_w"], p["conv_b"], p["dt_bias"], p["a_log"], p["d_skip"],
        p["ssd_norm_g"], name=f"l{l}_ssd_b")
    dproj, dwbd, dpsc = _pool_bwd(sv["proj"], dmix, p["pool_bd"], p["pool_scale"], dproj, name=f"l{l}_pool_b")
    dproj = _att_bwd(sv["proj"], *tabs, sv["y_att"], sv["lse"], dmix, dproj, name=f"l{l}_att_b")
    d_win = _wgrad(dproj, sv["h1"], tk=1408, name=f"l{l}_proj_bw")
    exchange(l, "mix", dict(w_in=d_win, w_out=d_wout))
    dx0, dn1, dsc1, dsh1 = _norm_mod_bwd(sv["x"], dproj, dx1, p["norm1_g"], sc1, w=p["w_in"], name=f"l{l}_proj_b")
    dx0, _ = lax.optimization_barrier((dx0, (d_win, d_wout, d_up, d_down)))
    dmod = jnp.concatenate([dsh1, dsc1, dg1, dsh2, dsc2, dg2], axis=1)[0]
    small = dict(norm1_g=dn1[0], ssd_conv_w=dcw[:SSD_CONV_K], ssd_conv_b=dcb[0], ssd_dt_bias=ddb[0], ssd_a_log=dal[0],
                 ssd_d=dd[0], ssd_norm_g=dng[0], pool_w=_diag_blocks(dwbd), pool_scale=dpsc[0], norm2_g=dn2[0],
                 ffn_conv_w=dfcw[:FFN_CONV_K], ffn_conv_b=dfcb[0])
    return dx0, dmod, small


def _example_step(x, target, pos_col, inv_freq_lane, mod, gather, small, final_g, exchange):
    tabs = _rope_tables(pos_col, inv_freq_lane, name="rope_tables")
    params, saved = [], []
    for l in range(DEPTH):
        params.append(_layer_params(l, small))
        x, sv = _layer_fwd(x, mod[l], params[l], tabs, l, gather)
        saved.append(sv)
    loss_row, dx, dfg = _final_loss(x, final_g[None], target, name="final_loss")
    dmods, smalls = [None] * DEPTH, [None] * DEPTH
    for l in reversed(range(DEPTH)):
        dx, dmods[l], smalls[l] = _layer_bwd(dx, saved[l], mod[l], params[l], tabs, l, exchange)
    return loss_row, dx, jnp.stack(dmods, axis=0), smalls, dfg[0]


_BIG = ("w_in", "w_out", "ffn_up", "ffn_down")
_SMALL_GRADS = ("norm1_g", "ssd_conv_w", "ssd_conv_b", "ssd_dt_bias", "ssd_a_log", "ssd_d", "ssd_norm_g", "pool_w",
                "pool_scale", "norm2_g", "ffn_conv_w", "ffn_conv_b")
_SMALL_PARAMS = ("ada_b", "norm1_g", "ssd_conv_w", "ssd_conv_b", "ssd_dt_bias", "ssd_a_log", "ssd_d", "ssd_norm_g",
                 "pool_w", "pool_scale", "norm2_g", "ffn_conv_w", "ffn_conv_b", "final_g")
_WEIGHT_ORDER = ("ada_w", "ada_b", "norm1_g", "w_in", "ssd_conv_w", "ssd_conv_b", "ssd_dt_bias", "ssd_a_log", "ssd_d",
                 "ssd_norm_g", "pool_w", "pool_scale", "w_out", "norm2_g", "ffn_up", "ffn_conv_w", "ffn_conv_b",
                 "ffn_down", "final_g")


_COLUMN_SHARDED = ("w_in", "ffn_up")
_GROUPS = (("mix", ("w_in", "w_out")), ("ffn", ("ffn_up", "ffn_down")))


def _big_shares(w, l, names):
    return [(w[name][l].T if name in _COLUMN_SHARDED else w[name][l]).astype(BF16) for name in names]


def _unshard_big(names, gathered):
    out = {}
    for name, g in zip(names, gathered):
        full = g.reshape(N_DEV * g.shape[1], g.shape[2])
        out[name] = _permute_w_in(full) if name == "w_in" else full
    return out


def _shard_big(grads):
    out = []
    for name, g in grads.items():
        g = _unpermute_w_in(g) if name == "w_in" else g
        out.append(g.reshape(N_DEV, g.shape[0] // N_DEV, g.shape[1]))
    return out


def kernel(x, c, positions, ada_w, ada_b, norm1_g, w_in, ssd_conv_w, ssd_conv_b, ssd_dt_bias, ssd_a_log, ssd_d, ssd_norm_g, pool_w, pool_scale, w_out, norm2_g, ffn_up, ffn_conv_w, ffn_conv_b, ffn_down, final_g, loss_target, m_ada_w, m_ada_b, m_norm1_g, m_w_in, m_ssd_conv_w, m_ssd_conv_b, m_ssd_dt_bias, m_ssd_a_log, m_ssd_d, m_ssd_norm_g, m_pool_w, m_pool_scale, m_w_out, m_norm2_g, m_ffn_up, m_ffn_conv_w, m_ffn_conv_b, m_ffn_down, m_final_g, v_ada_w, v_ada_b, v_norm1_g, v_w_in, v_ssd_conv_w, v_ssd_conv_b, v_ssd_dt_bias, v_ssd_a_log, v_ssd_d, v_ssd_norm_g, v_pool_w, v_pool_scale, v_w_out, v_norm2_g, v_ffn_up, v_ffn_conv_w, v_ffn_conv_b, v_ffn_down, v_final_g):
    w = dict(ada_w=ada_w, ada_b=ada_b, norm1_g=norm1_g, w_in=w_in, ssd_conv_w=ssd_conv_w, ssd_conv_b=ssd_conv_b,
             ssd_dt_bias=ssd_dt_bias, ssd_a_log=ssd_a_log, ssd_d=ssd_d, ssd_norm_g=ssd_norm_g, pool_w=pool_w,
             pool_scale=pool_scale, w_out=w_out, norm2_g=norm2_g, ffn_up=ffn_up, ffn_conv_w=ffn_conv_w,
             ffn_conv_b=ffn_conv_b, ffn_down=ffn_down, final_g=final_g)
    m = dict(ada_w=m_ada_w, ada_b=m_ada_b, norm1_g=m_norm1_g, w_in=m_w_in, ssd_conv_w=m_ssd_conv_w,
             ssd_conv_b=m_ssd_conv_b, ssd_dt_bias=m_ssd_dt_bias, ssd_a_log=m_ssd_a_log, ssd_d=m_ssd_d,
             ssd_norm_g=m_ssd_norm_g, pool_w=m_pool_w, pool_scale=m_pool_scale, w_out=m_w_out, norm2_g=m_norm2_g,
             ffn_up=m_ffn_up, ffn_conv_w=m_ffn_conv_w, ffn_conv_b=m_ffn_conv_b, ffn_down=m_ffn_down,
             final_g=m_final_g)
    v = dict(ada_w=v_ada_w, ada_b=v_ada_b, norm1_g=v_norm1_g, w_in=v_w_in, ssd_conv_w=v_ssd_conv_w,
             ssd_conv_b=v_ssd_conv_b, ssd_dt_bias=v_ssd_dt_bias, ssd_a_log=v_ssd_a_log, ssd_d=v_ssd_d,
             ssd_norm_g=v_ssd_norm_g, pool_w=v_pool_w, pool_scale=v_pool_scale, w_out=v_w_out, norm2_g=v_norm2_g,
             ffn_up=v_ffn_up, ffn_conv_w=v_ffn_conv_w, ffn_conv_b=v_ffn_conv_b, ffn_down=v_ffn_down,
             final_g=v_final_g)
    ix, iy, ic = _axes()
    dev = 4 * ix + 2 * iy + ic

    c_all, scw, fcw = _allgather([c, ssd_conv_w.reshape(DEPTH * SSD_CONV_K, -1),
                                  ffn_conv_w.reshape(DEPTH * FFN_CONV_K, -1)], name="gather_small")
    small_all = c_all
    c_all = c_all.reshape(N_DEV, D_MODEL)
    scw = scw.reshape(N_DEV, DEPTH, SSD_CONV_K, -1).transpose(1, 2, 0, 3).reshape(DEPTH, SSD_CONV_K, SSD_CONV_CH)
    fcw = fcw.reshape(N_DEV, DEPTH, FFN_CONV_K, -1).transpose(1, 2, 0, 3).reshape(DEPTH, FFN_CONV_K, 2 * FFN_DIM)

    mod_cols = _ada_mod(c_all, ada_w, name="ada_mod")
    mod_all = _allgather([mod_cols.reshape(DEPTH * N_DEV, _ADA_SHARD)], name="gather_mod")[0]
    mod_all = mod_all.reshape(N_DEV, DEPTH, N_DEV, _ADA_SHARD)
    mod_mine = lax.dynamic_index_in_dim(mod_all, dev, axis=2, keepdims=False)
    mod = _add_rows(mod_mine.transpose(1, 0, 2).reshape(DEPTH, 6 * D_MODEL), ada_b, name="ada_bias")

    fetched = {}

    def gather(l, group, after):
        if l < DEPTH and (l, group) not in fetched:
            names = dict(_GROUPS)[group]
            shares, _ = lax.optimization_barrier((_big_shares(w, l, names), small_all if after is None else after))
            got = _allgather_async(shares, name=f"gather_weights_l{l}_{group}",
                                   collective_id=1 + 2 * l + (group == "ffn"))
            fetched[l, group] = _unshard_big(names, got)
        return fetched.get((l, group))

    core = ic.astype(jnp.int32).reshape(1)
    from_chips = {}

    def exchange(l, group, g):
        cid = 5 + 4 * l + 2 * (group == "mix")
        blocks = _shard_big(g)
        if from_chips:
            blocks, _ = lax.optimization_barrier((blocks, list(from_chips.values())))
        from_sibling = _pair_exchange(blocks, name=f"grads_pair_exchange_l{l}_{group}", collective_id=cid)
        parts = _pair_sum(core, blocks, from_sibling, name=f"grads_pair_sum_l{l}_{group}")
        got = _chip_exchange(parts, name=f"grads_chip_exchange_l{l}_{group}", collective_id=cid + 1)
        from_chips.update({(l, name): t for name, t in zip(g, got)})

    small = dict(norm1_g=norm1_g, norm2_g=norm2_g, ssd_conv_w=scw, ssd_conv_b=ssd_conv_b, ssd_dt_bias=ssd_dt_bias,
                 ssd_a_log=ssd_a_log, ssd_d=ssd_d, ssd_norm_g=ssd_norm_g, pool_w=pool_w, pool_scale=pool_scale,
                 ffn_conv_w=fcw, ffn_conv_b=ffn_conv_b)

    inv_freq = ROPE_THETA ** (-jnp.arange(0, ROT_DIM, 2, dtype=F32) / ROT_DIM)
    lane = jnp.arange(128) % HEAD_LANES
    inv_freq_lane = jnp.where(lane < ROT_DIM, inv_freq[lane % (ROT_DIM // 2)], 0.0)[None, :]
    pos_col = positions.reshape(SEQ, 1).astype(F32)
    loss_row, dx, dmod, g_small, g_final = _example_step(
        x[0], loss_target[0], pos_col, inv_freq_lane, mod, gather, small, final_g, exchange)

    grads = {}
    for name in _BIG:
        per_layer = [_sum_blocks(from_chips[l, name], name=f"grads_chip_sum_l{l}_{name}") for l in range(DEPTH)]
        grads[name] = jnp.stack([g.T if name in _COLUMN_SHARDED else g for g in per_layer], axis=0)

    small_names = list(_SMALL_GRADS)
    stacked = [jnp.stack([g_small[l][name] for l in range(DEPTH)], axis=0) for name in small_names]
    small_parts = [loss_row, dmod] + [s.reshape(-1, s.shape[-1]) for s in stacked] + [g_final[None]]
    gathered = _allgather_async(small_parts, name="gather_small_grads", collective_id=13)
    total = _sum_gathered(gathered, name="sum_small_grads")
    loss = total[0][0, 0]
    grads["ada_b"] = total[1]
    grads.update(zip(small_names, total[2:-1]))
    grads["final_g"] = total[-1][0]
    dmod_cols = lax.dynamic_slice_in_dim(gathered[1], dev * _ADA_SHARD, _ADA_SHARD, axis=2).transpose(1, 0, 2)
    grads["ada_w"] = _ada_wgrad(c_all, dmod_cols, name="ada_wgrad")
    for name in ("ssd_dt_bias", "ssd_a_log", "ssd_d"):
        grads[name] = grads[name][:, :SSD_HEADS]
    grads["pool_w"] = grads["pool_w"].reshape(pool_w.shape)
    grads["ssd_conv_w"] = lax.dynamic_slice_in_dim(
        grads["ssd_conv_w"].reshape(DEPTH, SSD_CONV_K, SSD_CONV_CH), dev * ssd_conv_w.shape[2], ssd_conv_w.shape[2], axis=2)
    grads["ffn_conv_w"] = lax.dynamic_slice_in_dim(
        grads["ffn_conv_w"].reshape(DEPTH, FFN_CONV_K, 2 * FFN_DIM), dev * ffn_conv_w.shape[2], ffn_conv_w.shape[2], axis=2)

    delta, new_m, new_v = {}, {}, {}
    for name, tr in (("ada_w", 512), ("w_in", 512), ("w_out", 256), ("ffn_up", 512), ("ffn_down", 352)):
        shp = w[name].shape
        two_d = lambda a: a.reshape(shp[0] * shp[1], shp[2])
        d_, m_, v_ = _adamw(two_d(w[name]), two_d(grads[name]), two_d(m[name]), two_d(v[name]), tr=tr,
                               name=f"adamw_{name}")
        delta[name], new_m[name], new_v[name] = (t.reshape(shp) for t in (d_, m_, v_))
    two_d = lambda a: a.reshape(-1, a.shape[-1])
    outs = _adamw_small(*[[two_d(t[name]) for name in _SMALL_PARAMS] for t in (w, grads, m, v)], name="adamw_small")
    for name, d_, m_, v_ in zip(_SMALL_PARAMS, *outs):
        delta[name], new_m[name], new_v[name] = (t.reshape(w[name].shape) for t in (d_, m_, v_))

    grad_x = dx[None]
    return (loss, grad_x, *[grads[n].reshape(w[n].shape) for n in _WEIGHT_ORDER],
            *[delta[n] for n in _WEIGHT_ORDER], *[new_m[n] for n in _WEIGHT_ORDER],
            *[new_v[n] for n in _WEIGHT_ORDER])
```

```python
import functools
import math

import jax
import jax.numpy as jnp
from jax import lax
from jax.experimental import pallas as pl
from jax.experimental.pallas import tpu as pltpu
from jax.experimental.pallas import tpu_sc as plsc

F32 = jnp.float32
BF16 = jnp.bfloat16

N_DEV = 8
D_MODEL = 1024
SEQ = 4096
DEPTH = 2
SSD_INNER = 512
SSD_HEADS = 8
SSD_HPG = 4
SSD_STATE = 128
SSD_CHUNK = 256
SSD_CONV_K = 4
SSD_CONV_CH = 1024
POOL_W = 256
POOL_WINDOWS = (2, 4, 8, 16)
ATT_W = 256
ATT_PATTERNS = ((128, 1), (512, 4), (2048, 16))
ATT_BLOCK = 128
ROT_DIM = 16
ROPE_THETA = 500000.0
IN_W = 2568
FFN_DIM = 2816
FFN_CONV_K = 3
NORM_EPS = 1e-6
HEAD_LANES = 64

ADAM_LR = 0.001
ADAM_B1 = 0.9
ADAM_B2 = 0.999
ADAM_EPS = 1e-08
ADAM_WD = 0.01
ADAM_STEP = 10

PROJ_W = 2816
PROJ_SSD_W = 1792
PROJ_Z_BLK = 2
PROJ_DT_BLK = 12
PROJ_POOL_BLK = 7
PROJ_Q_BLK, PROJ_K_BLK, PROJ_V_BLK = 16, 18, 20
MIX_POOL_BLK = 2
MIX_ATT_BLK = 6
VMEM_LIMIT = 56 * 1024 * 1024
ROW_TILE = 512
CONV_HALO = 8
POOL_HALO = 16
ATT_KPAD = ATT_BLOCK * 16
MESH = pl.DeviceIdType.MESH
_HBM = pl.BlockSpec(memory_space=pl.ANY)


def _cparams(*sem):
    return pltpu.CompilerParams(dimension_semantics=sem, vmem_limit_bytes=VMEM_LIMIT)


def _resident(shape):
    return pl.BlockSpec(shape, lambda i: (0,) * len(shape), pipeline_mode=pl.Buffered(1))


def _silu(x):
    return x * jax.nn.sigmoid(x)


def _pick_lane(v, h):
    lane = lax.broadcasted_iota(jnp.int32, v.shape, 1)
    return jnp.sum(jnp.where(lane == h, v, 0.0), axis=1, keepdims=True)


def _pick_row(v, h):
    row = lax.broadcasted_iota(jnp.int32, v.shape, 0)
    return jnp.sum(jnp.where(row == h, v, 0.0), axis=0, keepdims=True)


def _head_of_lane(width):
    return lax.broadcasted_iota(jnp.int32, (1, width), 1) // HEAD_LANES


@functools.partial(jax.custom_vjp, nondiff_argnums=(1, 2))
def _shift_rows(x_ext, s, halo):
    y = x_ext if s == 0 else pltpu.roll(x_ext, s, 0)
    return y[halo:]


def _shift_rows_fwd(x_ext, s, halo):
    return _shift_rows(x_ext, s, halo), None


def _shift_rows_bwd(s, halo, _, g):
    ge = jnp.concatenate([jnp.zeros((halo, g.shape[1]), g.dtype), g], axis=0)
    return (ge if s == 0 else pltpu.roll(ge, ge.shape[0] - s, 0),)


_shift_rows.defvjp(_shift_rows_fwd, _shift_rows_bwd)


@functools.partial(jax.custom_vjp, nondiff_argnums=(1,))
def _roll_rows(x, s):
    return pltpu.roll(x, s, 0)


def _roll_rows_fwd(x, s):
    return _roll_rows(x, s), None


def _roll_rows_bwd(s, _, g):
    return (pltpu.roll(g, g.shape[0] - s, 0),)


_roll_rows.defvjp(_roll_rows_fwd, _roll_rows_bwd)


def _rms_modulate(xv, g, sc, sh):
    r = lax.rsqrt(jnp.mean(xv * xv, axis=-1, keepdims=True) + NORM_EPS)
    return (xv * r * g) * (1.0 + sc) + sh


def _mm(a, w, *, name, nt=False, tm=ROW_TILE, tn=None, out_dtype=F32, norm=None, residual=None):
    t, k = a.shape
    n = w.shape[0] if nt else w.shape[1]
    tn = tn or n
    assert tn == n or (norm is None and residual is None)
    extra_in = list(norm or ()) + list(residual or ())

    def body(*refs):
        a_ref, w_ref = refs[:2]
        ins = refs[2:2 + len(extra_in)]
        outs = refs[2 + len(extra_in):]
        if norm is None:
            av = a_ref[...].astype(BF16)
        else:
            av = _rms_modulate(a_ref[...], ins[0][...], ins[1][...], ins[2][...]).astype(BF16)
            outs[1][...] = av
        if nt:
            acc = lax.dot_general(av, w_ref[...], (((1,), (1,)), ((), ())), preferred_element_type=F32)
        else:
            acc = jnp.dot(av, w_ref[...], preferred_element_type=F32)
        outs[0][...] = acc.astype(out_dtype)
        if residual is not None:
            x_ref, gate_ref = ins[-2:]
            outs[-1][...] = x_ref[...] + gate_ref[...] * acc

    row = lambda width: pl.BlockSpec((1, width), lambda i, j: (0, 0))
    tile = lambda width: pl.BlockSpec((tm, width), lambda i, j: (i, 0))
    w_spec = pl.BlockSpec((tn, k), lambda i, j: (j, 0)) if nt else pl.BlockSpec((k, tn), lambda i, j: (0, j))
    in_specs = [tile(k), w_spec] + ([row(k)] * 3 if norm else []) + ([tile(n), row(n)] if residual else [])
    out_specs = [pl.BlockSpec((tm, tn), lambda i, j: (i, j))] + ([tile(k)] if norm else []) + \
        ([tile(n)] if residual else [])
    out_shape = [jax.ShapeDtypeStruct((t, n), out_dtype)] + \
        ([jax.ShapeDtypeStruct((t, k), BF16)] if norm else []) + \
        ([jax.ShapeDtypeStruct((t, n), F32)] if residual else [])
    outs = pl.pallas_call(
        body, grid=(t // tm, n // tn), in_specs=in_specs, out_specs=out_specs, out_shape=out_shape, name=name,
        compiler_params=_cparams("parallel", "parallel"))(a, w, *extra_in)
    return outs[0] if len(outs) == 1 else outs


def _wgrad(a, b, *, name, tk=None, tn=None, tt=2048, out_dtype=BF16):
    t, k = a.shape
    n = b.shape[1]
    tk = tk or k
    tn = tn or n
    steps = t // tt

    def body(a_ref, b_ref, o_ref, acc_ref):
        s = pl.program_id(2)

        @pl.when(s == 0)
        def _():
            acc_ref[...] = jnp.zeros_like(acc_ref)

        acc_ref[...] += lax.dot_general(a_ref[...].astype(BF16), b_ref[...].astype(BF16),
                                        (((0,), (0,)), ((), ())), preferred_element_type=F32)

        @pl.when(s == steps - 1)
        def _():
            o_ref[...] = acc_ref[...].astype(out_dtype)

    return pl.pallas_call(
        body, grid=(k // tk, n // tn, steps),
        in_specs=[pl.BlockSpec((tt, tk), lambda i, j, s: (s, i)), pl.BlockSpec((tt, tn), lambda i, j, s: (s, j))],
        out_specs=pl.BlockSpec((tk, tn), lambda i, j, s: (i, j)),
        out_shape=jax.ShapeDtypeStruct((k, n), out_dtype),
        scratch_shapes=[pltpu.VMEM((tk, tn), F32)], name=name,
        compiler_params=_cparams("parallel", "parallel", "arbitrary"))(a, b)


def _norm_mod_bwd(x, dh, dres, g, sc, *, name, w=None, tm=ROW_TILE):
    s, d = x.shape
    steps = s // tm

    def body(x_ref, dh_ref, dres_ref, g_ref, sc_ref, *rest):
        w_ref = rest[0] if w is not None else None
        dx_ref, dg_ref, dsc_ref, dsh_ref, da_acc, dsh_acc = rest[-6:]
        i = pl.program_id(0)

        @pl.when(i == 0)
        def _():
            da_acc[...] = jnp.zeros_like(da_acc)
            dsh_acc[...] = jnp.zeros_like(dsh_acc)

        xv = x_ref[...]
        if w is None:
            dhv = dh_ref[...].astype(F32)
        else:
            dhv = jnp.dot(dh_ref[...], w_ref[...], preferred_element_type=F32)
        r = lax.rsqrt(jnp.mean(xv * xv, axis=-1, keepdims=True) + NORM_EPS)
        xhat = xv * r
        gain = g_ref[...] * (1.0 + sc_ref[...])
        dxhat = dhv * gain
        dx_ref[...] = dres_ref[...] + r * (dxhat - xhat * jnp.mean(dxhat * xhat, axis=-1, keepdims=True))
        da_acc[...] += jnp.sum(dhv * xhat, axis=0, keepdims=True)
        dsh_acc[...] += jnp.sum(dhv, axis=0, keepdims=True)

        @pl.when(i == steps - 1)
        def _():
            dg_ref[...] = da_acc[...] * (1.0 + sc_ref[...])
            dsc_ref[...] = da_acc[...] * g_ref[...]
            dsh_ref[...] = dsh_acc[...]

    row = pl.BlockSpec((1, d), lambda i: (0, 0))
    tile = pl.BlockSpec((tm, d), lambda i: (i, 0))
    row_shape = jax.ShapeDtypeStruct((1, d), F32)
    dh_spec = tile if w is None else pl.BlockSpec((tm, dh.shape[1]), lambda i: (i, 0))
    return pl.pallas_call(
        body, grid=(steps,), in_specs=[tile, dh_spec, tile, row, row] + ([] if w is None else [_resident(w.shape)]),
        out_specs=[tile, row, row, row],
        out_shape=[jax.ShapeDtypeStruct((s, d), F32), row_shape, row_shape, row_shape],
        scratch_shapes=[pltpu.VMEM((1, d), F32), pltpu.VMEM((1, d), F32)], name=name,
        compiler_params=_cparams("arbitrary"))(x, dh, dres, g, sc, *([] if w is None else [w]))


def _mid_bwd(x1, dh2, dx2, norm_g, sc, gate, mo, w_out, *, name, tm=ROW_TILE):
    s, d = x1.shape
    steps = s // tm

    def body(x_ref, dh_ref, dres_ref, ng_ref, sc_ref, g_ref, mo_ref, w_ref,
             dx_ref, dng_ref, dsc_ref, dsh_ref, dmix_ref, dmo_ref, dg_ref, da_acc, dsh_acc):
        i = pl.program_id(0)

        @pl.when(i == 0)
        def _():
            da_acc[...] = jnp.zeros_like(da_acc)
            dsh_acc[...] = jnp.zeros_like(dsh_acc)
            dg_ref[...] = jnp.zeros_like(dg_ref)

        xv = x_ref[...]
        dhv = dh_ref[...]
        r = lax.rsqrt(jnp.mean(xv * xv, axis=-1, keepdims=True) + NORM_EPS)
        xhat = xv * r
        dxhat = dhv * (ng_ref[...] * (1.0 + sc_ref[...]))
        dxv = dres_ref[...] + r * (dxhat - xhat * jnp.mean(dxhat * xhat, axis=-1, keepdims=True))
        dx_ref[...] = dxv
        da_acc[...] += jnp.sum(dhv * xhat, axis=0, keepdims=True)
        dsh_acc[...] += jnp.sum(dhv, axis=0, keepdims=True)
        dmo = (g_ref[...] * dxv).astype(BF16)
        dmo_ref[...] = dmo
        dg_ref[...] += jnp.sum(dxv * mo_ref[...], axis=0, keepdims=True)
        dmix_ref[...] = lax.dot_general(dmo, w_ref[...], (((1,), (1,)), ((), ())), preferred_element_type=F32)

        @pl.when(i == steps - 1)
        def _():
            dng_ref[...] = da_acc[...] * (1.0 + sc_ref[...])
            dsc_ref[...] = da_acc[...] * ng_ref[...]
            dsh_ref[...] = dsh_acc[...]

    tile = pl.BlockSpec((tm, d), lambda i: (i, 0))
    row = pl.BlockSpec((1, d), lambda i: (0, 0))
    mix_tile = pl.BlockSpec((tm, w_out.shape[0]), lambda i: (i, 0))
    row_shape = jax.ShapeDtypeStruct((1, d), F32)
    return pl.pallas_call(
        body, grid=(steps,), in_specs=[tile, tile, tile, row, row, row, tile, _resident(w_out.shape)],
        out_specs=[tile, row, row, row, mix_tile, tile, row],
        out_shape=[jax.ShapeDtypeStruct((s, d), F32), row_shape, row_shape, row_shape,
                   jax.ShapeDtypeStruct((s, w_out.shape[0]), F32), jax.ShapeDtypeStruct((s, d), BF16), row_shape],
        scratch_shapes=[pltpu.VMEM((1, d), F32), pltpu.VMEM((1, d), F32)], name=name,
        compiler_params=_cparams("arbitrary"))(x1, dh2, dx2, norm_g, sc, gate, mo, w_out)


def _final_loss(x, g, target, *, name, tm=ROW_TILE):
    s, d = x.shape
    steps = s // tm

    def body(x_ref, g_ref, t_ref, loss_ref, dx_ref, dg_ref, sq_acc):
        i = pl.program_id(0)

        @pl.when(i == 0)
        def _():
            sq_acc[...] = jnp.zeros_like(sq_acc)
            dg_ref[...] = jnp.zeros_like(dg_ref)

        xv = x_ref[...]
        r = lax.rsqrt(jnp.mean(xv * xv, axis=-1, keepdims=True) + NORM_EPS)
        xhat = xv * r
        err = xhat * g_ref[...] - t_ref[...]
        sq_acc[...] += jnp.sum(err * err, axis=0, keepdims=True)
        dy = err * (1.0 / d)
        dg_ref[...] += jnp.sum(dy * xhat, axis=0, keepdims=True)
        dxhat = dy * g_ref[...]
        dx_ref[...] = r * (dxhat - xhat * jnp.mean(dxhat * xhat, axis=-1, keepdims=True))

        @pl.when(i == steps - 1)
        def _():
            total = jnp.sum(sq_acc[...], axis=1, keepdims=True) * (0.5 / d)
            loss_ref[...] = jnp.broadcast_to(total, loss_ref.shape)

    tile = pl.BlockSpec((tm, d), lambda i: (i, 0))
    row = pl.BlockSpec((1, d), lambda i: (0, 0))
    return pl.pallas_call(
        body, grid=(steps,), in_specs=[tile, row, tile],
        out_specs=[pl.BlockSpec((1, 128), lambda i: (0, 0)), tile, row],
        out_shape=[jax.ShapeDtypeStruct((1, 128), F32), jax.ShapeDtypeStruct((s, d), F32),
                   jax.ShapeDtypeStruct((1, d), F32)],
        scratch_shapes=[pltpu.VMEM((1, d), F32)], name=name, compiler_params=_cparams("arbitrary"))(x, g, target)


def _ssd_chunk(z, xbc_ext, dt_raw, conv_w, conv_b, dt_bias, a_log, d_skip, norm_g, h_in):
    q = z.shape[0]
    gw = SSD_HPG * HEAD_LANES
    xc = conv_b
    for k in range(SSD_CONV_K):
        xc = xc + _pick_row(conv_w, k) * _shift_rows(xbc_ext, SSD_CONV_K - 1 - k, CONV_HALO)
    xc = _silu(xc)
    dt = jax.nn.softplus(dt_raw + dt_bias)
    da = dt * (-jnp.exp(a_log))
    ri = lax.broadcasted_iota(jnp.int32, (q, q), 0)
    ci = lax.broadcasted_iota(jnp.int32, (q, q), 1)
    causal = ri >= ci
    tril = causal.astype(F32)
    a_cum = jnp.dot(tril, da, preferred_element_type=F32, precision=lax.Precision.HIGHEST)
    a_cum_t = lax.dot_general(da, tril, (((0,), (1,)), ((), ())), preferred_element_type=F32,
                              precision=lax.Precision.HIGHEST)
    a_last = _pick_row(a_cum, q - 1)
    head = _head_of_lane(gw)
    ys, hs = [], []
    for g in range(2):
        xs = xc[:, gw * g:gw * (g + 1)]
        bm = xc[:, SSD_INNER + SSD_STATE * g:SSD_INNER + SSD_STATE * (g + 1)]
        cm = xc[:, SSD_INNER + 2 * SSD_STATE + SSD_STATE * g:SSD_INNER + 2 * SSD_STATE + SSD_STATE * (g + 1)]
        cb = lax.dot_general(cm.astype(BF16), bm.astype(BF16), (((1,), (1,)), ((), ())), preferred_element_type=F32)
        cols = [_pick_lane(a_cum, SSD_HPG * g + j) for j in range(SSD_HPG)]
        lasts = [_pick_lane(a_last, SSD_HPG * g + j) for j in range(SSD_HPG)]
        dt_exp = sum(jnp.where(head == j, _pick_lane(dt, SSD_HPG * g + j), 0.0) for j in range(SSD_HPG))
        d_exp = sum(jnp.where(head == j, _pick_lane(d_skip, SSD_HPG * g + j), 0.0) for j in range(SSD_HPG))
        e_cum = sum(jnp.where(head == j, jnp.exp(cols[j]), 0.0) for j in range(SSD_HPG))
        c_dec = sum(jnp.where(head == j, jnp.exp(lasts[j]), 0.0) for j in range(SSD_HPG))
        xsdt = (xs * dt_exp).astype(BF16)
        y_diag = jnp.zeros((q, gw), F32)
        st_new = jnp.zeros((SSD_STATE, gw), F32)
        for j in range(SSD_HPG):
            row = _pick_row(a_cum_t, SSD_HPG * g + j)
            lmat = jnp.exp(jnp.where(causal, cols[j] - row, -jnp.inf))
            r = jnp.dot((cb * lmat).astype(BF16), xsdt, preferred_element_type=F32)
            y_diag = y_diag + jnp.where(head == j, r, 0.0)
            bd = (bm * jnp.exp(lasts[j] - cols[j])).astype(BF16)
            st = lax.dot_general(bd, xsdt, (((0,), (0,)), ((), ())), preferred_element_type=F32)
            st_new = st_new + jnp.where(head == j, st, 0.0)
        y_off = jnp.dot(cm.astype(BF16), h_in[g].astype(BF16), preferred_element_type=F32) * e_cum
        hs.append(h_in[g] * c_dec + st_new)
        y = y_diag + y_off + d_exp * xs
        yz = y * _silu(z[:, gw * g:gw * (g + 1)])
        yz = yz * lax.rsqrt(jnp.mean(yz * yz, axis=-1, keepdims=True) + NORM_EPS)
        ys.append(yz * norm_g[:, gw * g:gw * (g + 1)])
    return jnp.concatenate(ys, axis=1), tuple(hs)


_SSD_NCHUNK = SEQ // SSD_CHUNK
_HALO_PER_CHUNK = SSD_CHUNK // CONV_HALO


def _ssd_param_specs(const):
    return [pl.BlockSpec((8, SSD_CONV_CH), const), pl.BlockSpec((1, SSD_CONV_CH), const),
            pl.BlockSpec((1, 128), const), pl.BlockSpec((1, 128), const), pl.BlockSpec((1, 128), const),
            pl.BlockSpec((1, SSD_INNER), const)]


def _ssd_fwd(proj, conv_w, conv_b, dt_bias, a_log, d_skip, norm_g, *, name):
    q = SSD_CHUNK

    def body(z_ref, xbc_ref, halo_ref, dt_ref, cw_ref, cb_ref, db_ref, al_ref, d_ref, ng_ref, y_ref, hs_ref, h_acc):
        i = pl.program_id(0)

        @pl.when(i == 0)
        def _():
            h_acc[...] = jnp.zeros_like(h_acc)

        halo = jnp.where(i == 0, 0.0, halo_ref[...])
        xbc_ext = jnp.concatenate([halo, xbc_ref[...]], axis=0)
        h_in = (h_acc[0], h_acc[1])
        hs_ref[0, 0] = h_in[0]
        hs_ref[0, 1] = h_in[1]
        y, h_out = _ssd_chunk(z_ref[...], xbc_ext, dt_ref[...], cw_ref[...], cb_ref[...], db_ref[...], al_ref[...],
                              d_ref[...], ng_ref[...], h_in)
        y_ref[...] = y.astype(BF16)
        h_acc[0] = h_out[0]
        h_acc[1] = h_out[1]

    const = lambda i: (0, 0)
    return pl.pallas_call(
        body, grid=(_SSD_NCHUNK,),
        in_specs=[pl.BlockSpec((q, SSD_INNER), lambda i: (i, PROJ_Z_BLK)),
                  pl.BlockSpec((q, SSD_CONV_CH), lambda i: (i, 0)),
                  pl.BlockSpec((CONV_HALO, SSD_CONV_CH), lambda i: (jnp.maximum(i * _HALO_PER_CHUNK - 1, 0), 0)),
                  pl.BlockSpec((q, 128), lambda i: (i, PROJ_DT_BLK))] + _ssd_param_specs(const),
        out_specs=[pl.BlockSpec((q, SSD_INNER), lambda i: (i, 0)),
                   pl.BlockSpec((1, 2, SSD_STATE, 256), lambda i: (i, 0, 0, 0))],
        out_shape=[jax.ShapeDtypeStruct((SEQ, D_MODEL), BF16),
                   jax.ShapeDtypeStruct((_SSD_NCHUNK, 2, SSD_STATE, 256), F32)],
        scratch_shapes=[pltpu.VMEM((2, SSD_STATE, 256), F32)], name=name,
        compiler_params=_cparams("arbitrary"))(proj, proj, proj, proj, conv_w, conv_b, dt_bias, a_log, d_skip, norm_g)


def _ssd_bwd(proj, hstates, dmix, conv_w, conv_b, dt_bias, a_log, d_skip, norm_g, *, name):
    q = SSD_CHUNK
    last = _SSD_NCHUNK - 1

    def body(z_ref, xbc_ref, halo_ref, dt_ref, hs_ref, dy_ref, cw_ref, cb_ref, db_ref, al_ref, d_ref, ng_ref,
             dp_ref, dcw_ref, dcb_ref, ddb_ref, dal_ref, dd_ref, dng_ref, dh_acc, dhalo_acc):
        i = pl.program_id(0)

        @pl.when(i == 0)
        def _():
            dh_acc[...] = jnp.zeros_like(dh_acc)
            dhalo_acc[...] = jnp.zeros_like(dhalo_acc)
            for r in (dcw_ref, dcb_ref, ddb_ref, dal_ref, dd_ref, dng_ref):
                r[...] = jnp.zeros_like(r)

        halo = jnp.where(i == last, 0.0, halo_ref[...])
        xbc_ext = jnp.concatenate([halo, xbc_ref[...]], axis=0)
        _, vjp = jax.vjp(_ssd_chunk, z_ref[...], xbc_ext, dt_ref[...], cw_ref[...], cb_ref[...], db_ref[...],
                         al_ref[...], d_ref[...], ng_ref[...], (hs_ref[0, 0], hs_ref[0, 1]))
        gz, gx, gdt, gcw, gcb, gdb, gal, gd, gng, gh = vjp((dy_ref[...], (dh_acc[0], dh_acc[1])))
        dxbc = jnp.concatenate([gx[CONV_HALO:q], gx[q:] + dhalo_acc[...]], axis=0)
        dp_ref[...] = jnp.concatenate([dxbc, gz, gdt, jnp.zeros_like(gdt)], axis=1).astype(BF16)
        dhalo_acc[...] = gx[:CONV_HALO]
        dh_acc[0] = gh[0]
        dh_acc[1] = gh[1]
        dcw_ref[...] += gcw
        dcb_ref[...] += gcb
        ddb_ref[...] += gdb
        dal_ref[...] += gal
        dd_ref[...] += gd
        dng_ref[...] += gng

    const = lambda i: (0, 0)
    rev = lambda i: last - i
    row = lambda n: jax.ShapeDtypeStruct((1, n), F32)
    return pl.pallas_call(
        body, grid=(_SSD_NCHUNK,),
        in_specs=[pl.BlockSpec((q, SSD_INNER), lambda i: (rev(i), PROJ_Z_BLK)),
                  pl.BlockSpec((q, SSD_CONV_CH), lambda i: (rev(i), 0)),
                  pl.BlockSpec((CONV_HALO, SSD_CONV_CH), lambda i: (jnp.maximum(rev(i) * _HALO_PER_CHUNK - 1, 0), 0)),
                  pl.BlockSpec((q, 128), lambda i: (rev(i), PROJ_DT_BLK)),
                  pl.BlockSpec((1, 2, SSD_STATE, 256), lambda i: (rev(i), 0, 0, 0)),
                  pl.BlockSpec((q, SSD_INNER), lambda i: (rev(i), 0))] + _ssd_param_specs(const),
        out_specs=[pl.BlockSpec((q, PROJ_SSD_W), lambda i: (rev(i), 0))] + _ssd_param_specs(const),
        out_shape=[jax.ShapeDtypeStruct((SEQ, PROJ_W), BF16), jax.ShapeDtypeStruct((8, SSD_CONV_CH), F32),
                   row(SSD_CONV_CH), row(128), row(128), row(128), row(SSD_INNER)],
        scratch_shapes=[pltpu.VMEM((2, SSD_STATE, 256), F32), pltpu.VMEM((CONV_HALO, SSD_CONV_CH), F32)], name=name,
        compiler_params=_cparams("arbitrary"))(proj, proj, proj, proj, hstates, dmix, conv_w, conv_b, dt_bias, a_log,
                                                d_skip, norm_g)


def _rope_tables(pos_col, inv_freq_lane, *, name):
    s = pos_col.shape[0]

    def body(p_ref, f_ref, c_ref, s1_ref, s2_ref):
        ang = p_ref[...] * f_ref[...]
        within = lax.broadcasted_iota(jnp.int32, ang.shape, 1) % HEAD_LANES
        half = ROT_DIM // 2
        c_ref[...] = jnp.where(within < ROT_DIM, jnp.cos(ang), 1.0)
        sn = jnp.sin(ang)
        s1_ref[...] = jnp.where(within < half, -sn, 0.0)
        s2_ref[...] = jnp.where((within >= half) & (within < ROT_DIM), sn, 0.0)

    shp = jax.ShapeDtypeStruct((s, 128), F32)
    return pl.pallas_call(body, out_shape=[shp, shp, shp], name=name,
                          compiler_params=pltpu.CompilerParams(vmem_limit_bytes=VMEM_LIMIT))(pos_col, inv_freq_lane)


def _rope(t, c, s1, s2):
    half = ROT_DIM // 2
    return t * c + pltpu.roll(t, 128 - half, 1) * s1 + pltpu.roll(t, half, 1) * s2


def _rope_t(g, c, s1, s2):
    half = ROT_DIM // 2
    return g * c + pltpu.roll(g * s1, half, 1) + pltpu.roll(g * s2, 128 - half, 1)


def _att_valid(b):
    qi = lax.broadcasted_iota(jnp.int32, (ATT_BLOCK, 2 * ATT_BLOCK), 0)
    kj = lax.broadcasted_iota(jnp.int32, (ATT_BLOCK, 2 * ATT_BLOCK), 1)
    rel = qi + ATT_BLOCK - kj
    return (rel >= 0) & (rel <= ATT_BLOCK) & (b * ATT_BLOCK + kj - ATT_BLOCK >= 0)


def _att_slices(i, d):
    if d == 1:
        qstart = pl.multiple_of(i * ATT_BLOCK, ATT_BLOCK)
        return i, pl.ds(qstart, ATT_BLOCK), pl.ds(pl.multiple_of(qstart - ATT_BLOCK + ATT_KPAD, ATT_BLOCK), 2 * ATT_BLOCK)
    r = i % d
    b = i // d
    qstart = r + d * ATT_BLOCK * b
    return b, pl.ds(qstart, ATT_BLOCK, stride=d), pl.ds(qstart - ATT_BLOCK * d + ATT_KPAD, 2 * ATT_BLOCK, stride=d)


_ATT_NBLK = SEQ // ATT_BLOCK
_ATT_SCALE = HEAD_LANES ** -0.5
_ATT_UNROLL_FWD = 8
_ATT_UNROLL = 4


def _att_fwd(proj, cos, sin1, sin2, mix, *, name):
    s = SEQ

    def body(q_ref, k_ref, v_ref, c_ref, s1_ref, s2_ref, _, o_ref, lse_ref, mix_ref, qs, ks, vs, acc, m_s, l_s):
        c, s1, s2 = c_ref[...], s1_ref[...], s2_ref[...]
        qs[...] = _rope(q_ref[...], c, s1, s2) * _ATT_SCALE
        zeros = jnp.zeros((ATT_KPAD, 128), F32)
        ks[pl.ds(0, ATT_KPAD), :] = zeros
        vs[pl.ds(0, ATT_KPAD), :] = zeros
        ks[pl.ds(ATT_KPAD, s), :] = _rope(k_ref[...], c, s1, s2)
        vs[pl.ds(ATT_KPAD, s), :] = v_ref[...]
        head0 = _head_of_lane(128) == 0

        for bi, (_, d) in enumerate(ATT_PATTERNS):
            def blk(i, carry, d=d, first=(bi == 0)):
                b, sq, sk = _att_slices(i, d)
                qb = qs[sq, :]
                kw = ks[sk, :].astype(BF16)
                vw = vs[sk, :].astype(BF16)
                valid = _att_valid(b)
                ms, ls, os_ = [], [], []
                for hh in range(2):
                    qh = jnp.where(head0 if hh == 0 else ~head0, qb, 0.0).astype(BF16)
                    sc = lax.dot_general(qh, kw, (((1,), (1,)), ((), ())), preferred_element_type=F32)
                    sc = jnp.where(valid, sc, -jnp.inf)
                    mb = jnp.max(sc, axis=1, keepdims=True)
                    p = jnp.exp(sc - mb)
                    ms.append(mb)
                    ls.append(jnp.sum(p, axis=1, keepdims=True))
                    os_.append(jnp.dot(p.astype(BF16), vw, preferred_element_type=F32))
                m_b = jnp.where(head0, ms[0], ms[1])
                l_b = jnp.where(head0, ls[0], ls[1])
                o_b = jnp.where(head0, os_[0], os_[1])
                if first:
                    m_s[sq, :] = m_b
                    l_s[sq, :] = l_b
                    acc[sq, :] = o_b
                else:
                    m_old = m_s[sq, :]
                    m_new = jnp.maximum(m_old, m_b)
                    a_old = jnp.exp(m_old - m_new)
                    a_b = jnp.exp(m_b - m_new)
                    m_s[sq, :] = m_new
                    l_s[sq, :] = l_s[sq, :] * a_old + l_b * a_b
                    acc[sq, :] = acc[sq, :] * a_old + o_b * a_b
                return carry

            lax.fori_loop(0, _ATT_NBLK, blk, 0, unroll=_ATT_UNROLL_FWD)

        out = acc[...] / l_s[...]
        o_ref[...] = out
        mix_ref[...] = out.astype(BF16)
        lse_ref[...] = m_s[...] + jnp.log(l_s[...])

    col = lambda base: pl.BlockSpec((s, 128), lambda p: (0, base + p))
    tab = pl.BlockSpec((s, 128), lambda p: (0, 0))
    big = pltpu.VMEM((ATT_KPAD + s, 128), F32)
    tok = pltpu.VMEM((s, 128), F32)
    return pl.pallas_call(
        body, grid=(2,), in_specs=[col(PROJ_Q_BLK), col(PROJ_K_BLK), col(PROJ_V_BLK), tab, tab, tab, _HBM],
        out_specs=[pl.BlockSpec((s, 128), lambda p: (0, p)), pl.BlockSpec((s, 128), lambda p: (0, p)),
                   col(MIX_ATT_BLK)],
        out_shape=[jax.ShapeDtypeStruct((s, ATT_W), F32), jax.ShapeDtypeStruct((s, ATT_W), F32),
                   jax.ShapeDtypeStruct(mix.shape, mix.dtype)],
        input_output_aliases={6: 2}, scratch_shapes=[tok, big, big, tok, tok, tok], name=name,
        compiler_params=_cparams("arbitrary"))(proj, proj, proj, cos, sin1, sin2, mix)


def _att_bwd(proj, cos, sin1, sin2, out, lse, dmix, dproj, *, name):
    s = SEQ

    def body(proj_ref, c_hbm, s1_hbm, s2_hbm, out_hbm, lse_hbm, dmix_hbm, _, dproj_hbm,
             c_ref, s1_ref, s2_ref, o_ref, lse_ref, do_ref, qs, ks, vs, dqs, dks, dvs, staged, sems):
        def start(copies):
            for cp in copies:
                cp.start()
            return copies

        def load(pair):
            lanes = pl.ds(128 * pair, 128)
            rows = pl.ds(ATT_KPAD, s)
            return start([
                pltpu.make_async_copy(proj_ref.at[:, pl.ds(128 * (PROJ_Q_BLK + pair), 128)], qs, sems.at[0]),
                pltpu.make_async_copy(proj_ref.at[:, pl.ds(128 * (PROJ_K_BLK + pair), 128)], ks.at[rows, :], sems.at[1]),
                pltpu.make_async_copy(proj_ref.at[:, pl.ds(128 * (PROJ_V_BLK + pair), 128)], vs.at[rows, :], sems.at[2]),
                pltpu.make_async_copy(out_hbm.at[:, lanes], o_ref, sems.at[3]),
                pltpu.make_async_copy(lse_hbm.at[:, lanes], lse_ref, sems.at[4]),
                pltpu.make_async_copy(dmix_hbm.at[:, pl.ds(128 * (MIX_ATT_BLK + pair), 128)], do_ref, sems.at[5])])

        tables = start([pltpu.make_async_copy(c_hbm, c_ref, sems.at[6]),
                        pltpu.make_async_copy(s1_hbm, s1_ref, sems.at[7]),
                        pltpu.make_async_copy(s2_hbm, s2_ref, sems.at[8])])
        loads = load(0)
        for cp in tables:
            cp.wait()
        head0 = _head_of_lane(128) == 0
        zeros = jnp.zeros((ATT_KPAD, 128), F32)
        for pair in range(2):
            for cp in loads:
                cp.wait()
            c, s1, s2 = c_ref[...], s1_ref[...], s2_ref[...]
            qs[...] = _rope(qs[...], c, s1, s2) * _ATT_SCALE
            ks[pl.ds(0, ATT_KPAD), :] = zeros
            vs[pl.ds(0, ATT_KPAD), :] = zeros
            ks[pl.ds(ATT_KPAD, s), :] = _rope(ks[pl.ds(ATT_KPAD, s), :], c, s1, s2)
            dqs[...] = jnp.zeros_like(dqs)
            dks[...] = jnp.zeros_like(dks)
            dvs[...] = jnp.zeros_like(dvs)

            for _, d in ATT_PATTERNS:
                def blk(i, carry, d=d):
                    b, sq, sk = _att_slices(i, d)
                    qb = qs[sq, :]
                    kw = ks[sk, :].astype(BF16)
                    vw = vs[sk, :].astype(BF16)
                    dob = do_ref[sq, :]
                    lse_b = lse_ref[sq, :]
                    dd = dob * o_ref[sq, :]
                    valid = _att_valid(b)
                    dq_b = jnp.zeros((ATT_BLOCK, 128), F32)
                    dk_w = jnp.zeros((2 * ATT_BLOCK, 128), F32)
                    dv_w = jnp.zeros((2 * ATT_BLOCK, 128), F32)
                    for hh in range(2):
                        hm = head0 if hh == 0 else ~head0
                        qh = jnp.where(hm, qb, 0.0).astype(BF16)
                        doh = jnp.where(hm, dob, 0.0).astype(BF16)
                        lse_h = _pick_lane(lse_b, hh * HEAD_LANES)
                        d_h = jnp.sum(jnp.where(hm, dd, 0.0), axis=1, keepdims=True)
                        sc = lax.dot_general(qh, kw, (((1,), (1,)), ((), ())), preferred_element_type=F32)
                        p = jnp.where(valid, jnp.exp(sc - lse_h), 0.0)
                        dp = lax.dot_general(doh, vw, (((1,), (1,)), ((), ())), preferred_element_type=F32)
                        ds = (p * (dp - d_h)).astype(BF16)
                        dq_b = dq_b + jnp.where(hm, jnp.dot(ds, kw, preferred_element_type=F32), 0.0)
                        dk_w = dk_w + lax.dot_general(ds, qh, (((0,), (0,)), ((), ())), preferred_element_type=F32)
                        dv_w = dv_w + lax.dot_general(p.astype(BF16), doh, (((0,), (0,)), ((), ())),
                                                      preferred_element_type=F32)
                    dqs[sq, :] += dq_b
                    dks[sk, :] += dk_w
                    dvs[sk, :] += dv_w
                    return carry

                lax.fori_loop(0, _ATT_NBLK, blk, 0, unroll=_ATT_UNROLL)

            staged[0] = _rope_t(dqs[...] * _ATT_SCALE, c, s1, s2).astype(BF16)
            staged[1] = _rope_t(dks[pl.ds(ATT_KPAD, s), :], c, s1, s2).astype(BF16)
            staged[2] = dvs[pl.ds(ATT_KPAD, s), :].astype(BF16)
            stores = start([
                pltpu.make_async_copy(staged.at[j], dproj_hbm.at[:, pl.ds(128 * (col + pair), 128)], sems.at[9 + j])
                for j, col in enumerate((PROJ_Q_BLK, PROJ_K_BLK, PROJ_V_BLK))])
            if pair == 0:
                loads = load(1)
            for cp in stores:
                cp.wait()

    big = pltpu.VMEM((ATT_KPAD + s, 128), F32)
    tok = pltpu.VMEM((s, 128), F32)
    return pl.pallas_call(
        body, in_specs=[_HBM] * 8, out_specs=_HBM, out_shape=jax.ShapeDtypeStruct(dproj.shape, dproj.dtype),
        input_output_aliases={7: 0},
        scratch_shapes=[tok] * 6 + [tok, big, big, tok, big, big, pltpu.VMEM((3, s, 128), BF16),
                                    pltpu.SemaphoreType.DMA((12,))], name=name,
        compiler_params=pltpu.CompilerParams(vmem_limit_bytes=VMEM_LIMIT))(
            proj, cos, sin1, sin2, out, lse, dmix, dproj)


_POOL_TM = 512
_POOL_NT = SEQ // _POOL_TM
_POOL_HALO_PER_TILE = _POOL_TM // POOL_HALO


def _pool_tile(u_ext, w_bd, scale, t0):
    s2 = u_ext + _roll_rows(u_ext, 1)
    s4 = s2 + _roll_rows(s2, 2)
    s8 = s4 + _roll_rows(s4, 4)
    s16 = s8 + _roll_rows(s8, 8)
    grp = _head_of_lane(POOL_W)
    sel = jnp.where(grp == 0, s2, jnp.where(grp == 1, s4, jnp.where(grp == 2, s8, s16)))[POOL_HALO:]
    t = sel.shape[0]
    pos = t0 + lax.broadcasted_iota(jnp.int32, (t, POOL_W), 0) + 1
    win = jnp.where(grp == 0, 2, jnp.where(grp == 1, 4, jnp.where(grp == 2, 8, 16)))
    cnt = jnp.minimum(pos, win).astype(F32)
    diff = sel / cnt - u_ext[POOL_HALO:]
    return jnp.dot(diff.astype(BF16), w_bd.astype(BF16), preferred_element_type=F32) * scale


def _pool_fwd(proj, w_bd, scale, mix, *, name):
    tm = _POOL_TM

    def body(u_ref, halo_ref, w_ref, sc_ref, _, y_ref):
        i = pl.program_id(0)
        halo = jnp.where(i == 0, 0.0, halo_ref[...])
        u_ext = jnp.concatenate([halo, u_ref[...]], axis=0)
        y_ref[...] = _pool_tile(u_ext, w_ref[...], sc_ref[...], i * tm).astype(BF16)

    return pl.pallas_call(
        body, grid=(_POOL_NT,),
        in_specs=[pl.BlockSpec((tm, POOL_W), lambda i: (i, PROJ_POOL_BLK)),
                  pl.BlockSpec((POOL_HALO, POOL_W),
                               lambda i: (jnp.maximum(i * _POOL_HALO_PER_TILE - 1, 0), PROJ_POOL_BLK)),
                  pl.BlockSpec((POOL_W, POOL_W), lambda i: (0, 0)), pl.BlockSpec((1, POOL_W), lambda i: (0, 0)), _HBM],
        out_specs=pl.BlockSpec((tm, POOL_W), lambda i: (i, MIX_POOL_BLK)),
        out_shape=jax.ShapeDtypeStruct(mix.shape, mix.dtype), input_output_aliases={4: 0}, name=name,
        compiler_params=_cparams("parallel"))(proj, proj, w_bd, scale, mix)


def _pool_bwd(proj, dmix, w_bd, scale, dproj, *, name):
    tm = _POOL_TM
    last = _POOL_NT - 1

    def body(u_ref, halo_ref, dy_ref, w_ref, sc_ref, _, du_ref, dw_ref, dsc_ref, dhalo_acc):
        i = pl.program_id(0)

        @pl.when(i == 0)
        def _():
            dhalo_acc[...] = jnp.zeros_like(dhalo_acc)
            dw_ref[...] = jnp.zeros_like(dw_ref)
            dsc_ref[...] = jnp.zeros_like(dsc_ref)

        tile = last - i
        halo = jnp.where(tile == 0, 0.0, halo_ref[...])
        u_ext = jnp.concatenate([halo, u_ref[...]], axis=0)
        _, vjp = jax.vjp(functools.partial(_pool_tile, t0=tile * tm), u_ext, w_ref[...], sc_ref[...])
        gu, gw, gs = vjp(dy_ref[...])
        du_ref[...] = jnp.concatenate([gu[POOL_HALO:tm], gu[tm:] + dhalo_acc[...]], axis=0).astype(BF16)
        dhalo_acc[...] = gu[:POOL_HALO]
        dw_ref[...] += gw
        dsc_ref[...] += gs

    rev = lambda i: last - i
    return pl.pallas_call(
        body, grid=(_POOL_NT,),
        in_specs=[pl.BlockSpec((tm, POOL_W), lambda i: (rev(i), PROJ_POOL_BLK)),
                  pl.BlockSpec((POOL_HALO, POOL_W),
                               lambda i: (jnp.maximum(rev(i) * _POOL_HALO_PER_TILE - 1, 0), PROJ_POOL_BLK)),
                  pl.BlockSpec((tm, POOL_W), lambda i: (rev(i), MIX_POOL_BLK)),
                  pl.BlockSpec((POOL_W, POOL_W), lambda i: (0, 0)), pl.BlockSpec((1, POOL_W), lambda i: (0, 0)), _HBM],
        out_specs=[pl.BlockSpec((tm, POOL_W), lambda i: (rev(i), PROJ_POOL_BLK)),
                   pl.BlockSpec((POOL_W, POOL_W), lambda i: (0, 0)), pl.BlockSpec((1, POOL_W), lambda i: (0, 0))],
        out_shape=[jax.ShapeDtypeStruct(dproj.shape, dproj.dtype), jax.ShapeDtypeStruct((POOL_W, POOL_W), F32),
                   jax.ShapeDtypeStruct((1, POOL_W), F32)],
        input_output_aliases={5: 0}, scratch_shapes=[pltpu.VMEM((POOL_HALO, POOL_W), F32)], name=name,
        compiler_params=_cparams("arbitrary"))(proj, proj, dmix, w_bd, scale, dproj)


_FFN_TM = 256
_FFN_NT = SEQ // _FFN_TM
_FFN_HALO_PER_TILE = _FFN_TM // CONV_HALO


def _ffn_act_tile(hid_ext, conv_w, conv_b):
    hc = conv_b
    for k in range(FFN_CONV_K):
        hc = hc + _pick_row(conv_w, k) * _shift_rows(hid_ext, FFN_CONV_K - 1 - k, CONV_HALO)
    return _silu(hc[:, :FFN_DIM]) * hc[:, FFN_DIM:]


def _ffn_fwd(x1, norm_g, sc, sh, gate, up_t, down, conv_w, conv_b, *, name):
    tm = _FFN_TM
    w = 2 * FFN_DIM
    d = D_MODEL

    def body(x_ref, ng_ref, sc_ref, sh_ref, g_ref, up_ref, dn_ref, cw_ref, cb_ref,
             h_ref, hid_ref, act_ref, f_ref, x2_ref, halo_acc):
        i = pl.program_id(0)
        h2 = _rms_modulate(x_ref[...], ng_ref[...], sc_ref[...], sh_ref[...]).astype(BF16)
        h_ref[...] = h2
        hid = lax.dot_general(h2, up_ref[...], (((1,), (1,)), ((), ())), preferred_element_type=F32)
        hid_ref[...] = hid
        halo = jnp.where(i == 0, 0.0, halo_acc[...])
        act = _ffn_act_tile(jnp.concatenate([halo, hid], axis=0), cw_ref[...], cb_ref[...]).astype(BF16)
        halo_acc[...] = hid[tm - CONV_HALO:]
        act_ref[...] = act
        f = jnp.dot(act, dn_ref[...], preferred_element_type=F32)
        f_ref[...] = f
        x2_ref[...] = x_ref[...] + g_ref[...] * f

    tile = lambda n: pl.BlockSpec((tm, n), lambda i: (i, 0))
    return pl.pallas_call(
        body, grid=(_FFN_NT,),
        in_specs=[tile(d)] + [_resident((1, d))] * 4 + [_resident((w, d)), _resident((FFN_DIM, d)),
                                                        _resident((8, w)), _resident((1, w))],
        out_specs=[tile(d), tile(w), tile(FFN_DIM), tile(d), tile(d)],
        out_shape=[jax.ShapeDtypeStruct((SEQ, d), BF16), jax.ShapeDtypeStruct((SEQ, w), F32),
                   jax.ShapeDtypeStruct((SEQ, FFN_DIM), BF16), jax.ShapeDtypeStruct((SEQ, d), F32),
                   jax.ShapeDtypeStruct((SEQ, d), F32)],
        scratch_shapes=[pltpu.VMEM((CONV_HALO, w), F32)], name=name,
        compiler_params=_cparams("arbitrary"))(x1, norm_g, sc, sh, gate, up_t, down, conv_w, conv_b)


def _ffn_bwd(dx2, gate, f, hid, up_t, down, conv_w, conv_b, *, name):
    tm = _FFN_TM
    w = 2 * FFN_DIM
    d = D_MODEL
    last = _FFN_NT - 1

    def body(dx_ref, g_ref, f_ref, h_ref, halo_ref, up_ref, dn_ref, cw_ref, cb_ref,
             df_ref, dg_ref, dh_ref, dh2_ref, dcw_ref, dcb_ref, dhalo_acc):
        i = pl.program_id(0)

        @pl.when(i == 0)
        def _():
            dhalo_acc[...] = jnp.zeros_like(dhalo_acc)
            dcw_ref[...] = jnp.zeros_like(dcw_ref)
            dcb_ref[...] = jnp.zeros_like(dcb_ref)
            dg_ref[...] = jnp.zeros_like(dg_ref)

        dxv = dx_ref[...]
        df = (g_ref[...] * dxv).astype(BF16)
        df_ref[...] = df
        dg_ref[...] += jnp.sum(dxv * f_ref[...], axis=0, keepdims=True)
        dact = lax.dot_general(df, dn_ref[...], (((1,), (1,)), ((), ())), preferred_element_type=F32)
        halo = jnp.where(i == last, 0.0, halo_ref[...])
        hid_ext = jnp.concatenate([halo, h_ref[...]], axis=0)
        _, vjp = jax.vjp(_ffn_act_tile, hid_ext, cw_ref[...], cb_ref[...])
        gh, gw, gb = vjp(dact)
        dhid = jnp.concatenate([gh[CONV_HALO:tm], gh[tm:] + dhalo_acc[...]], axis=0).astype(BF16)
        dhalo_acc[...] = gh[:CONV_HALO]
        dh_ref[...] = dhid
        dh2_ref[...] = jnp.dot(dhid, up_ref[...], preferred_element_type=F32)
        dcw_ref[...] += gw
        dcb_ref[...] += gb

    rev = lambda i: last - i
    tile = lambda n: pl.BlockSpec((tm, n), lambda i: (rev(i), 0))
    acc = lambda shape: pl.BlockSpec(shape, lambda i: (0, 0))
    return pl.pallas_call(
        body, grid=(_FFN_NT,),
        in_specs=[tile(d), _resident((1, d)), tile(d), tile(w),
                  pl.BlockSpec((CONV_HALO, w), lambda i: (jnp.maximum(rev(i) * _FFN_HALO_PER_TILE - 1, 0), 0)),
                  _resident((w, d)), _resident((FFN_DIM, d)), _resident((8, w)), _resident((1, w))],
        out_specs=[tile(d), acc((1, d)), tile(w), tile(d), acc((8, w)), acc((1, w))],
        out_shape=[jax.ShapeDtypeStruct((SEQ, d), BF16), jax.ShapeDtypeStruct((1, d), F32),
                   jax.ShapeDtypeStruct((SEQ, w), BF16), jax.ShapeDtypeStruct((SEQ, d), F32),
                   jax.ShapeDtypeStruct((8, w), F32), jax.ShapeDtypeStruct((1, w), F32)],
        scratch_shapes=[pltpu.VMEM((CONV_HALO, w), F32)], name=name,
        compiler_params=_cparams("arbitrary"))(dx2, gate, f, hid, hid, up_t, down, conv_w, conv_b)


def _axes():
    return lax.axis_index("x"), lax.axis_index("y"), lax.axis_index("c")


def _handshake(peers):
    barrier = pltpu.get_barrier_semaphore()
    for peer in peers:
        pl.semaphore_signal(barrier, inc=1, device_id=peer, device_id_type=MESH)
    pl.semaphore_wait(barrier, len(peers))


def _allgather_body(x_refs, out_refs, send_sems, recv_sems, local_sems, own_barrier):
    n = len(x_refs)
    x, y, c = _axes()
    me, sibling = (x, y, c), (x, y, 1 - c)
    chips = [(1 - x, y), (x, 1 - y), (1 - x, 1 - y)]
    if own_barrier:
        _handshake([sibling] + [(*chip, c) for chip in chips])

    def slot(a, px, py, pc):
        return out_refs[a].at[4 * px + 2 * py + pc]

    def copy(a, k, block, to, src=None):
        return pltpu.make_async_remote_copy(
            src_ref=slot(a, *block) if src is None else src, dst_ref=slot(a, *block),
            send_sem=send_sems.at[a, k], recv_sem=recv_sems.at[a, k], device_id=to, device_id_type=MESH)

    mines, firsts = [], []
    for a in range(n):
        mines.append(pltpu.make_async_copy(x_refs[a], slot(a, *me), local_sems.at[a]))
        mines[-1].start()
        first = [copy(a, 0, me, sibling, src=x_refs[a])]
        first += [copy(a, 1 + j, me, (*chip, c), src=x_refs[a]) for j, chip in enumerate(chips)]
        for cp in first:
            cp.start()
        firsts += first
    passed = []
    for j, chip in enumerate(chips):
        for a in range(n):
            copy(a, 1 + j, (*chip, c), me).wait_recv()
            passed.append(copy(a, 4 + j, (*chip, c), sibling))
            passed[-1].start()
    for a in range(n):
        copy(a, 0, sibling, me).wait_recv()
    for j, chip in enumerate(chips):
        for a in range(n):
            copy(a, 4 + j, (*chip, 1 - c), me).wait_recv()
    for cp in firsts + passed:
        cp.wait_send()
    for cp in mines:
        cp.wait()


def _allgather_sems(n):
    return [pltpu.SemaphoreType.DMA((n, 7)), pltpu.SemaphoreType.DMA((n, 7)), pltpu.SemaphoreType.DMA((n,))]


def _allgather(xs, *, name):
    n = len(xs)

    def body(*refs):
        _allgather_body(refs[:n], refs[n:2 * n], *refs[2 * n:], own_barrier=False)

    return pl.pallas_call(
        body, out_shape=[jax.ShapeDtypeStruct((N_DEV,) + xb.shape, xb.dtype) for xb in xs],
        in_specs=[_HBM] * n, out_specs=[_HBM] * n, scratch_shapes=_allgather_sems(n), name=name)(*xs)


def _allgather_async(xs, *, name, collective_id):
    n = len(xs)
    x_refs = [jax.new_ref(xb, memory_space=pltpu.MemorySpace.HBM) for xb in xs]
    out_refs = [jax.empty_ref(jax.ShapeDtypeStruct((N_DEV,) + xb.shape, xb.dtype), memory_space=pltpu.MemorySpace.HBM)
                for xb in xs]

    @pl.kernel(mesh=plsc.ScalarSubcoreMesh(axis_name="sequencer", num_cores=1), name=name,
               scratch_types=tuple(_allgather_sems(n)),
               compiler_params=pltpu.CompilerParams(collective_id=collective_id))
    def launch(send_sems, recv_sems, local_sems):
        _allgather_body(x_refs, out_refs, send_sems, recv_sems, local_sems, own_barrier=True)

    launch()
    return [r[...] for r in out_refs]


def _pair_exchange(blocks, *, name, collective_id):
    n = len(blocks)
    hbm = pltpu.MemorySpace.HBM
    in_refs = [jax.new_ref(b, memory_space=hbm) for b in blocks]
    out_refs = [jax.empty_ref(jax.ShapeDtypeStruct((4,) + b.shape[1:], b.dtype), memory_space=hbm) for b in blocks]

    @pl.kernel(mesh=plsc.ScalarSubcoreMesh(axis_name="sequencer", num_cores=1), name=name,
               scratch_types=(pltpu.SemaphoreType.DMA((n, 4)), pltpu.SemaphoreType.DMA((n, 4))),
               compiler_params=pltpu.CompilerParams(collective_id=collective_id))
    def launch(send_sems, recv_sems):
        x, y, c = _axes()
        _handshake([(x, y, 1 - c)])
        copies = [pltpu.make_async_remote_copy(
            src_ref=in_refs[a].at[2 * s + (1 - c)], dst_ref=out_refs[a].at[s], send_sem=send_sems.at[a, s],
            recv_sem=recv_sems.at[a, s], device_id=(x, y, 1 - c), device_id_type=MESH)
            for a in range(n) for s in range(4)]
        for cp in copies:
            cp.start()
        for cp in copies:
            cp.wait_recv()
        for cp in copies:
            cp.wait_send()

    launch()
    return [r[...] for r in out_refs]


def _chip_exchange(parts, *, name, collective_id):
    n = len(parts)
    hbm = pltpu.MemorySpace.HBM
    in_refs = [jax.new_ref(p, memory_space=hbm) for p in parts]
    out_refs = [jax.empty_ref(jax.ShapeDtypeStruct(p.shape, p.dtype), memory_space=hbm) for p in parts]

    @pl.kernel(mesh=plsc.ScalarSubcoreMesh(axis_name="sequencer", num_cores=1), name=name,
               scratch_types=(pltpu.SemaphoreType.DMA((n, 3)), pltpu.SemaphoreType.DMA((n, 3)),
                              pltpu.SemaphoreType.DMA((n,))),
               compiler_params=pltpu.CompilerParams(collective_id=collective_id))
    def launch(send_sems, recv_sems, local_sems):
        x, y, c = _axes()
        my_chip = 2 * x + y
        chips = [(1 - x, y), (x, 1 - y), (1 - x, 1 - y)]
        _handshake([(*chip, c) for chip in chips])
        locals_ = [pltpu.make_async_copy(in_refs[a].at[my_chip], out_refs[a].at[my_chip], local_sems.at[a])
                   for a in range(n)]
        for cp in locals_:
            cp.start()
        copies = [pltpu.make_async_remote_copy(
            src_ref=in_refs[a].at[2 * px + py], dst_ref=out_refs[a].at[my_chip], send_sem=send_sems.at[a, k],
            recv_sem=recv_sems.at[a, k], device_id=(px, py, c), device_id_type=MESH)
            for a in range(n) for k, (px, py) in enumerate(chips)]
        for cp in copies:
            cp.start()
        for cp in copies:
            cp.wait_recv()
        for cp in copies:
            cp.wait_send()
        for cp in locals_:
            cp.wait()

    launch()
    return [r[...] for r in out_refs]


def _pair_sum(core, blocks, from_sibling, *, name):
    n = len(blocks)

    def body(core_ref, *refs):
        for a_ref, b_ref, o_ref in zip(refs[:n], refs[n:2 * n], refs[2 * n:]):
            o_ref[...] = (a_ref[...].astype(F32) + b_ref[...].astype(F32)).astype(o_ref.dtype)

    mine = lambda b: pl.BlockSpec((1,) + b.shape[1:], lambda s, core_ref: (2 * s + core_ref[0], 0, 0))
    slot = lambda b: pl.BlockSpec((1,) + b.shape[1:], lambda s, core_ref: (s, 0, 0))
    return pl.pallas_call(
        body,
        grid_spec=pltpu.PrefetchScalarGridSpec(
            num_scalar_prefetch=1, grid=(4,),
            in_specs=[mine(b) for b in blocks] + [slot(b) for b in blocks], out_specs=[slot(b) for b in blocks]),
        out_shape=[jax.ShapeDtypeStruct(s.shape, s.dtype) for s in from_sibling], name=name,
        compiler_params=_cparams("parallel"))(core, *blocks, *from_sibling)


def _sum_blocks(a, *, name, tr=None):
    n, r, cdim = a.shape
    tr = tr or r

    def body(a_ref, o_ref):
        acc = a_ref[0].astype(F32)
        for k in range(1, n):
            acc = acc + a_ref[k].astype(F32)
        o_ref[...] = acc

    return pl.pallas_call(body, grid=(r // tr,), in_specs=[pl.BlockSpec((n, tr, cdim), lambda i: (0, i, 0))],
                          out_specs=pl.BlockSpec((tr, cdim), lambda i: (i, 0)),
                          out_shape=jax.ShapeDtypeStruct((r, cdim), F32), name=name,
                          compiler_params=_cparams("parallel"))(a)


def _sum_gathered(gathered, *, name):
    n = len(gathered)

    def body(*refs):
        for a_ref, o_ref in zip(refs[:n], refs[n:]):
            acc = a_ref[0]
            for k in range(1, N_DEV):
                acc = acc + a_ref[k]
            o_ref[...] = acc

    return pl.pallas_call(body, out_shape=[jax.ShapeDtypeStruct(g.shape[1:], F32) for g in gathered], name=name,
                          compiler_params=pltpu.CompilerParams(vmem_limit_bytes=VMEM_LIMIT))(*gathered)


_ADA_SHARD = 6 * D_MODEL // N_DEV


def _ada_mod(c_all, ada_w, *, name):
    def body(c_ref, w_ref, o_ref):
        o_ref[0] = jnp.dot(_silu(c_ref[...]).astype(BF16), w_ref[0].astype(BF16), preferred_element_type=F32)

    return pl.pallas_call(
        body, grid=(DEPTH,),
        in_specs=[pl.BlockSpec((N_DEV, D_MODEL), lambda l: (0, 0)),
                  pl.BlockSpec((1, D_MODEL, _ADA_SHARD), lambda l: (l, 0, 0))],
        out_specs=pl.BlockSpec((1, N_DEV, _ADA_SHARD), lambda l: (l, 0, 0)),
        out_shape=jax.ShapeDtypeStruct((DEPTH, N_DEV, _ADA_SHARD), F32), name=name,
        compiler_params=_cparams("parallel"))(c_all, ada_w)


def _ada_wgrad(c_all, dmod_cols, *, name):
    def body(c_ref, d_ref, o_ref):
        o_ref[0] = lax.dot_general(_silu(c_ref[...]), d_ref[0], (((0,), (0,)), ((), ())),
                                   preferred_element_type=F32, precision=lax.Precision.HIGHEST)

    return pl.pallas_call(
        body, grid=(DEPTH,),
        in_specs=[pl.BlockSpec((N_DEV, D_MODEL), lambda l: (0, 0)),
                  pl.BlockSpec((1, N_DEV, _ADA_SHARD), lambda l: (l, 0, 0))],
        out_specs=pl.BlockSpec((1, D_MODEL, _ADA_SHARD), lambda l: (l, 0, 0)),
        out_shape=jax.ShapeDtypeStruct((DEPTH, D_MODEL, _ADA_SHARD), F32), name=name,
        compiler_params=_cparams("parallel"))(c_all, dmod_cols)


def _add_rows(a, b, *, name):
    def body(a_ref, b_ref, o_ref):
        o_ref[...] = a_ref[...] + b_ref[...]

    return pl.pallas_call(body, out_shape=jax.ShapeDtypeStruct(a.shape, a.dtype), name=name)(a, b)


def _adamw_update(w_ref, g_ref, m_ref, v_ref, d_ref, mo_ref, vo_ref):
    gv = g_ref[...]
    mn = ADAM_B1 * m_ref[...] + (1.0 - ADAM_B1) * gv
    vn = ADAM_B2 * v_ref[...] + (1.0 - ADAM_B2) * (gv * gv)
    mo_ref[...] = mn
    vo_ref[...] = vn
    m_hat = mn / (1.0 - ADAM_B1 ** ADAM_STEP)
    v_hat = vn / (1.0 - ADAM_B2 ** ADAM_STEP)
    d_ref[...] = -ADAM_LR * (m_hat / (jnp.sqrt(v_hat) + ADAM_EPS) + ADAM_WD * w_ref[...])


def _adamw_small(ws, gs, ms, vs, *, name):
    n = len(ws)

    def body(*refs):
        ins, outs = refs[:4 * n], refs[4 * n:]
        for i in range(n):
            _adamw_update(ins[i], ins[n + i], ins[2 * n + i], ins[3 * n + i], outs[i], outs[n + i], outs[2 * n + i])

    shapes = [jax.ShapeDtypeStruct(a.shape, F32) for a in ws]
    outs = pl.pallas_call(body, out_shape=shapes * 3, name=name,
                          compiler_params=pltpu.CompilerParams(vmem_limit_bytes=VMEM_LIMIT))(*ws, *gs, *ms, *vs)
    return outs[:n], outs[n:2 * n], outs[2 * n:]


def _adamw(w, g, m, v, *, name, tr):
    r, cdim = w.shape
    body = functools.partial(_adamw_update)

    spec = pl.BlockSpec((tr, cdim), lambda i: (i, 0))
    shp = jax.ShapeDtypeStruct((r, cdim), F32)
    return pl.pallas_call(body, grid=(r // tr,), in_specs=[spec] * 4, out_specs=[spec] * 3, out_shape=[shp] * 3,
                          name=name, compiler_params=_cparams("parallel"))(w, g, m, v)


def _pad_rows(a, rows):
    return jnp.concatenate([a, jnp.zeros((rows - a.shape[0],) + a.shape[1:], a.dtype)], axis=0)


def _pad_lanes(a, lanes):
    return jnp.concatenate([a, jnp.zeros(a.shape[:-1] + (lanes - a.shape[-1],), a.dtype)], axis=-1)


def _permute_w_in(wt):
    return jnp.concatenate([wt[512:1536], wt[:512], wt[1536:1544],
                            jnp.zeros((PROJ_W - IN_W, wt.shape[1]), wt.dtype), wt[1544:]], axis=0)


def _unpermute_w_in(wp):
    return jnp.concatenate([wp[1024:1536], wp[:1024], wp[1536:1544], wp[PROJ_SSD_W:]], axis=0)


def _block_diag(w):
    rows = []
    for g in range(4):
        rows.append(jnp.concatenate([w[g] if k == g else jnp.zeros_like(w[g]) for k in range(4)], axis=1))
    return jnp.concatenate(rows, axis=0)


def _diag_blocks(wbd):
    return jnp.stack([wbd[64 * g:64 * (g + 1), 64 * g:64 * (g + 1)] for g in range(4)], axis=0)


def _layer_params(l, small):
    return dict(
        norm1_g=small["norm1_g"][l][None], norm2_g=small["norm2_g"][l][None],
        conv_w=_pad_rows(small["ssd_conv_w"][l], 8), conv_b=small["ssd_conv_b"][l][None],
        dt_bias=_pad_lanes(small["ssd_dt_bias"][l][None], 128), a_log=_pad_lanes(small["ssd_a_log"][l][None], 128),
        d_skip=_pad_lanes(small["ssd_d"][l][None], 128), ssd_norm_g=small["ssd_norm_g"][l][None],
        pool_bd=_block_diag(small["pool_w"][l]), pool_scale=small["pool_scale"][l][None],
        fcw=_pad_rows(small["ffn_conv_w"][l], 8), fcb=small["ffn_conv_b"][l][None])


def _mod_rows(mod_l):
    return [mod_l[None, D_MODEL * i:D_MODEL * (i + 1)] for i in range(6)]


def _layer_fwd(x, mod_l, p, tabs, l, gather):
    sh1, sc1, g1, sh2, sc2, g2 = _mod_rows(mod_l)
    mix_w = gather(l, "mix", None)
    p.update(w_in=mix_w["w_in"], w_out=mix_w["w_out"])
    proj, h1 = _mm(x, p["w_in"], nt=True, norm=(p["norm1_g"], sc1, sh1), name=f"l{l}_proj")
    ffn_w = gather(l, "ffn", proj)
    p.update(up=ffn_w["ffn_up"], down=ffn_w["ffn_down"])
    mix, hst = _ssd_fwd(proj, p["conv_w"], p["conv_b"], p["dt_bias"], p["a_log"], p["d_skip"], p["ssd_norm_g"],
                        name=f"l{l}_ssd")
    mix = _pool_fwd(proj, p["pool_bd"], p["pool_scale"], mix, name=f"l{l}_pool")
    y_att, lse, mix = _att_fwd(proj, *tabs, mix, name=f"l{l}_att")
    mo, x1 = _mm(mix, p["w_out"], residual=(x, g1), name=f"l{l}_out")
    gather(l + 1, "mix", (x1, p["up"]))
    h2, hid, act, f, x2 = _ffn_fwd(x1, p["norm2_g"], sc2, sh2, g2, p["up"], p["down"], p["fcw"], p["fcb"],
                                   name=f"l{l}_ffn")
    return x2, dict(x=x, h1=h1, proj=proj, hst=hst, y_att=y_att, lse=lse, mix=mix, mo=mo, x1=x1, h2=h2, hid=hid,
                    act=act, f=f)


def _layer_bwd(dx2, sv, mod_l, p, tabs, l, exchange):
    sh1, sc1, g1, sh2, sc2, g2 = _mod_rows(mod_l)
    df, dg2, dhid, dh2, dfcw, dfcb = _ffn_bwd(dx2, g2, sv["f"], sv["hid"], p["up"], p["down"], p["fcw"], p["fcb"],
                                              name=f"l{l}_ffn_b")
    d_down = _wgrad(sv["act"], df, tk=1408, name=f"l{l}_down_bw")
    d_up = _wgrad(dhid, sv["h2"], tk=1408, name=f"l{l}_up_bw")
    finish_ffn = exchange(l, "ffn", dict(ffn_up=d_up, ffn_down=d_down))
    dx1, dn2, dsc2, dsh2, dmix, dmo, dg1 = _mid_bwd(sv["x1"], dh2, dx2, p["norm2_g"], sc2, g1, sv["mo"], p["w_out"],
                                                    name=f"l{l}_mid_b")
    finish_ffn(dx1)
    d_wout = _wgrad(sv["mix"], dmo, name=f"l{l}_out_bw")
    dproj, dcw, dcb, ddb, dal, dd, dng = _ssd_bwd(
        sv["proj"], sv["hst"], dmix, p["conv_w"], p["conv_b"], p["dt_bias"], p["a_log"], p["d_skip"],
        p["ssd_norm_g"], name=f"l{l}_ssd_b")
    dproj, dwbd, dpsc = _pool_bwd(sv["proj"], dmix, p["pool_bd"], p["pool_scale"], dproj, name=f"l{l}_pool_b")
    dproj = _att_bwd(sv["proj"], *tabs, sv["y_att"], sv["lse"], dmix, dproj, name=f"l{l}_att_b")
    d_win = _wgrad(dproj, sv["h1"], tk=1408, name=f"l{l}_proj_bw")
    finish_mix = exchange(l, "mix", dict(w_in=d_win, w_out=d_wout))
    dx0, dn1, dsc1, dsh1 = _norm_mod_bwd(sv["x"], dproj, dx1, p["norm1_g"], sc1, w=p["w_in"], name=f"l{l}_proj_b")
    finish_mix(dx0)
    dx0, _ = lax.optimization_barrier((dx0, (d_win, d_wout, d_up, d_down)))
    dmod = jnp.concatenate([dsh1, dsc1, dg1, dsh2, dsc2, dg2], axis=1)[0]
    small = dict(norm1_g=dn1[0], ssd_conv_w=dcw[:SSD_CONV_K], ssd_conv_b=dcb[0], ssd_dt_bias=ddb[0], ssd_a_log=dal[0],
                 ssd_d=dd[0], ssd_norm_g=dng[0], pool_w=_diag_blocks(dwbd), pool_scale=dpsc[0], norm2_g=dn2[0],
                 ffn_conv_w=dfcw[:FFN_CONV_K], ffn_conv_b=dfcb[0])
    return dx0, dmod, small


def _example_step(x, target, pos_col, inv_freq_lane, mod, gather, small, final_g, exchange):
    tabs = _rope_tables(pos_col, inv_freq_lane, name="rope_tables")
    params, saved = [], []
    for l in range(DEPTH):
        params.append(_layer_params(l, small))
        x, sv = _layer_fwd(x, mod[l], params[l], tabs, l, gather)
        saved.append(sv)
    loss_row, dx, dfg = _final_loss(x, final_g[None], target, name="final_loss")
    dmods, smalls = [None] * DEPTH, [None] * DEPTH
    for l in reversed(range(DEPTH)):
        dx, dmods[l], smalls[l] = _layer_bwd(dx, saved[l], mod[l], params[l], tabs, l, exchange)
    return loss_row, dx, jnp.stack(dmods, axis=0), smalls, dfg[0]


_BIG = ("w_in", "w_out", "ffn_up", "ffn_down")
_SMALL_GRADS = ("norm1_g", "ssd_conv_w", "ssd_conv_b", "ssd_dt_bias", "ssd_a_log", "ssd_d", "ssd_norm_g", "pool_w",
                "pool_scale", "norm2_g", "ffn_conv_w", "ffn_conv_b")
_SMALL_PARAMS = ("ada_b", "norm1_g", "ssd_conv_w", "ssd_conv_b", "ssd_dt_bias", "ssd_a_log", "ssd_d", "ssd_norm_g",
                 "pool_w", "pool_scale", "norm2_g", "ffn_conv_w", "ffn_conv_b", "final_g")
_WEIGHT_ORDER = ("ada_w", "ada_b", "norm1_g", "w_in", "ssd_conv_w", "ssd_conv_b", "ssd_dt_bias", "ssd_a_log", "ssd_d",
                 "ssd_norm_g", "pool_w", "pool_scale", "w_out", "norm2_g", "ffn_up", "ffn_conv_w", "ffn_conv_b",
                 "ffn_down", "final_g")


_COLUMN_SHARDED = ("w_in", "ffn_up")
_GROUPS = (("mix", ("w_in", "w_out")), ("ffn", ("ffn_up", "ffn_down")))


def _big_shares(w, l, names):
    return [(w[name][l].T if name in _COLUMN_SHARDED else w[name][l]).astype(BF16) for name in names]


def _unshard_big(names, gathered):
    out = {}
    for name, g in zip(names, gathered):
        full = g.reshape(N_DEV * g.shape[1], g.shape[2])
        out[name] = _permute_w_in(full) if name == "w_in" else full
    return out


def _shard_big(grads):
    out = []
    for name, g in grads.items():
        g = _unpermute_w_in(g) if name == "w_in" else g
        out.append(g.reshape(N_DEV, g.shape[0] // N_DEV, g.shape[1]))
    return out


def kernel(x, c, positions, ada_w, ada_b, norm1_g, w_in, ssd_conv_w, ssd_conv_b, ssd_dt_bias, ssd_a_log, ssd_d, ssd_norm_g, pool_w, pool_scale, w_out, norm2_g, ffn_up, ffn_conv_w, ffn_conv_b, ffn_down, final_g, loss_target, m_ada_w, m_ada_b, m_norm1_g, m_w_in, m_ssd_conv_w, m_ssd_conv_b, m_ssd_dt_bias, m_ssd_a_log, m_ssd_d, m_ssd_norm_g, m_pool_w, m_pool_scale, m_w_out, m_norm2_g, m_ffn_up, m_ffn_conv_w, m_ffn_conv_b, m_ffn_down, m_final_g, v_ada_w, v_ada_b, v_norm1_g, v_w_in, v_ssd_conv_w, v_ssd_conv_b, v_ssd_dt_bias, v_ssd_a_log, v_ssd_d, v_ssd_norm_g, v_pool_w, v_pool_scale, v_w_out, v_norm2_g, v_ffn_up, v_ffn_conv_w, v_ffn_conv_b, v_ffn_down, v_final_g):
    w = dict(ada_w=ada_w, ada_b=ada_b, norm1_g=norm1_g, w_in=w_in, ssd_conv_w=ssd_conv_w, ssd_conv_b=ssd_conv_b,
             ssd_dt_bias=ssd_dt_bias, ssd_a_log=ssd_a_log, ssd_d=ssd_d, ssd_norm_g=ssd_norm_g, pool_w=pool_w,
             pool_scale=pool_scale, w_out=w_out, norm2_g=norm2_g, ffn_up=ffn_up, ffn_conv_w=ffn_conv_w,
             ffn_conv_b=ffn_conv_b, ffn_down=ffn_down, final_g=final_g)
    m = dict(ada_w=m_ada_w, ada_b=m_ada_b, norm1_g=m_norm1_g, w_in=m_w_in, ssd_conv_w=m_ssd_conv_w,
             ssd_conv_b=m_ssd_conv_b, ssd_dt_bias=m_ssd_dt_bias, ssd_a_log=m_ssd_a_log, ssd_d=m_ssd_d,
             ssd_norm_g=m_ssd_norm_g, pool_w=m_pool_w, pool_scale=m_pool_scale, w_out=m_w_out, norm2_g=m_norm2_g,
             ffn_up=m_ffn_up, ffn_conv_w=m_ffn_conv_w, ffn_conv_b=m_ffn_conv_b, ffn_down=m_ffn_down,
             final_g=m_final_g)
    v = dict(ada_w=v_ada_w, ada_b=v_ada_b, norm1_g=v_norm1_g, w_in=v_w_in, ssd_conv_w=v_ssd_conv_w,
             ssd_conv_b=v_ssd_conv_b, ssd_dt_bias=v_ssd_dt_bias, ssd_a_log=v_ssd_a_log, ssd_d=v_ssd_d,
             ssd_norm_g=v_ssd_norm_g, pool_w=v_pool_w, pool_scale=v_pool_scale, w_out=v_w_out, norm2_g=v_norm2_g,
             ffn_up=v_ffn_up, ffn_conv_w=v_ffn_conv_w, ffn_conv_b=v_ffn_conv_b, ffn_down=v_ffn_down,
             final_g=v_final_g)
    ix, iy, ic = _axes()
    dev = 4 * ix + 2 * iy + ic

    c_all, scw, fcw = _allgather([c, ssd_conv_w.reshape(DEPTH * SSD_CONV_K, -1),
                                  ffn_conv_w.reshape(DEPTH * FFN_CONV_K, -1)], name="gather_small")
    small_all = c_all
    c_all = c_all.reshape(N_DEV, D_MODEL)
    scw = scw.reshape(N_DEV, DEPTH, SSD_CONV_K, -1).transpose(1, 2, 0, 3).reshape(DEPTH, SSD_CONV_K, SSD_CONV_CH)
    fcw = fcw.reshape(N_DEV, DEPTH, FFN_CONV_K, -1).transpose(1, 2, 0, 3).reshape(DEPTH, FFN_CONV_K, 2 * FFN_DIM)

    mod_cols = _ada_mod(c_all, ada_w, name="ada_mod")
    mod_all = _allgather([mod_cols.reshape(DEPTH * N_DEV, _ADA_SHARD)], name="gather_mod")[0]
    mod_all = mod_all.reshape(N_DEV, DEPTH, N_DEV, _ADA_SHARD)
    mod_mine = lax.dynamic_index_in_dim(mod_all, dev, axis=2, keepdims=False)
    mod = _add_rows(mod_mine.transpose(1, 0, 2).reshape(DEPTH, 6 * D_MODEL), ada_b, name="ada_bias")

    fetched = {}

    def gather(l, group, after):
        if l < DEPTH and (l, group) not in fetched:
            names = dict(_GROUPS)[group]
            shares, _ = lax.optimization_barrier((_big_shares(w, l, names), small_all if after is None else after))
            got = _allgather_async(shares, name=f"gather_weights_l{l}_{group}",
                                   collective_id=1 + 2 * l + (group == "ffn"))
            fetched[l, group] = _unshard_big(names, got)
        return fetched.get((l, group))

    core = ic.astype(jnp.int32).reshape(1)
    from_chips = {}

    def exchange(l, group, g):
        cid = 5 + 4 * l + 2 * (group == "mix")
        blocks = _shard_big(g)
        if from_chips:
            blocks, _ = lax.optimization_barrier((blocks, list(from_chips.values())))
        from_sibling = _pair_exchange(blocks, name=f"grads_pair_exchange_l{l}_{group}", collective_id=cid)

        def finish(after):
            theirs, _ = lax.optimization_barrier((from_sibling, after))
            parts = _pair_sum(core, blocks, theirs, name=f"grads_pair_sum_l{l}_{group}")
            got = _chip_exchange(parts, name=f"grads_chip_exchange_l{l}_{group}", collective_id=cid + 1)
            from_chips.update({(l, name): t for name, t in zip(g, got)})

        return finish

    small = dict(norm1_g=norm1_g, norm2_g=norm2_g, ssd_conv_w=scw, ssd_conv_b=ssd_conv_b, ssd_dt_bias=ssd_dt_bias,
                 ssd_a_log=ssd_a_log, ssd_d=ssd_d, ssd_norm_g=ssd_norm_g, pool_w=pool_w, pool_scale=pool_scale,
                 ffn_conv_w=fcw, ffn_conv_b=ffn_conv_b)

    inv_freq = ROPE_THETA ** (-jnp.arange(0, ROT_DIM, 2, dtype=F32) / ROT_DIM)
    lane = jnp.arange(128) % HEAD_LANES
    inv_freq_lane = jnp.where(lane < ROT_DIM, inv_freq[lane % (ROT_DIM // 2)], 0.0)[None, :]
    pos_col = positions.reshape(SEQ, 1).astype(F32)
    loss_row, dx, dmod, g_small, g_final = _example_step(
        x[0], loss_target[0], pos_col, inv_freq_lane, mod, gather, small, final_g, exchange)

    grads = {}
    for name in _BIG:
        per_layer = [_sum_blocks(from_chips[l, name], name=f"grads_chip_sum_l{l}_{name}") for l in range(DEPTH)]
        grads[name] = jnp.stack([g.T if name in _COLUMN_SHARDED else g for g in per_layer], axis=0)

    small_names = list(_SMALL_GRADS)
    stacked = [jnp.stack([g_small[l][name] for l in range(DEPTH)], axis=0) for name in small_names]
    small_parts = [loss_row, dmod] + [s.reshape(-1, s.shape[-1]) for s in stacked] + [g_final[None]]
    gathered = _allgather_async(small_parts, name="gather_small_grads", collective_id=13)
    total = _sum_gathered(gathered, name="sum_small_grads")
    loss = total[0][0, 0]
    grads["ada_b"] = total[1]
    grads.update(zip(small_names, total[2:-1]))
    grads["final_g"] = total[-1][0]
    dmod_cols = lax.dynamic_slice_in_dim(gathered[1], dev * _ADA_SHARD, _ADA_SHARD, axis=2).transpose(1, 0, 2)
    grads["ada_w"] = _ada_wgrad(c_all, dmod_cols, name="ada_wgrad")
    for name in ("ssd_dt_bias", "ssd_a_log", "ssd_d"):
        grads[name] = grads[name][:, :SSD_HEADS]
    grads["pool_w"] = grads["pool_w"].reshape(pool_w.shape)
    grads["ssd_conv_w"] = lax.dynamic_slice_in_dim(
        grads["ssd_conv_w"].reshape(DEPTH, SSD_CONV_K, SSD_CONV_CH), dev * ssd_conv_w.shape[2], ssd_conv_w.shape[2], axis=2)
    grads["ffn_conv_w"] = lax.dynamic_slice_in_dim(
        grads["ffn_conv_w"].reshape(DEPTH, FFN_CONV_K, 2 * FFN_DIM), dev * ffn_conv_w.shape[2], ffn_conv_w.shape[2], axis=2)

    delta, new_m, new_v = {}, {}, {}
    for name, tr in (("ada_w", 512), ("w_in", 512), ("w_out", 256), ("ffn_up", 512), ("ffn_down", 352)):
        shp = w[name].shape
        two_d = lambda a: a.reshape(shp[0] * shp[1], shp[2])
        d_, m_, v_ = _adamw(two_d(w[name]), two_d(grads[name]), two_d(m[name]), two_d(v[name]), tr=tr,
                               name=f"adamw_{name}")
        delta[name], new_m[name], new_v[name] = (t.reshape(shp) for t in (d_, m_, v_))
    two_d = lambda a: a.reshape(-1, a.shape[-1])
    outs = _adamw_small(*[[two_d(t[name]) for name in _SMALL_PARAMS] for t in (w, grads, m, v)], name="adamw_small")
    for name, d_, m_, v_ in zip(_SMALL_PARAMS, *outs):
        delta[name], new_m[name], new_v[name] = (t.reshape(w[name].shape) for t in (d_, m_, v_))

    grad_x = dx[None]
    return (loss, grad_x, *[grads[n].reshape(w[n].shape) for n in _WEIGHT_ORDER],
            *[delta[n] for n in _WEIGHT_ORDER], *[new_m[n] for n in _WEIGHT_ORDER],
            *[new_v[n] for n in _WEIGHT_ORDER])
```

```python
import functools
import math

import jax
import jax.numpy as jnp
from jax import lax
from jax.experimental import pallas as pl
from jax.experimental.pallas import tpu as pltpu
from jax.experimental.pallas import tpu_sc as plsc

F32 = jnp.float32
BF16 = jnp.bfloat16

N_DEV = 8
D_MODEL = 1024
SEQ = 4096
DEPTH = 2
SSD_INNER = 512
SSD_HEADS = 8
SSD_HPG = 4
SSD_STATE = 128
SSD_CHUNK = 256
SSD_CONV_K = 4
SSD_CONV_CH = 1024
POOL_W = 256
POOL_WINDOWS = (2, 4, 8, 16)
ATT_W = 256
ATT_PATTERNS = ((128, 1), (512, 4), (2048, 16))
ATT_BLOCK = 128
ROT_DIM = 16
ROPE_THETA = 500000.0
IN_W = 2568
FFN_DIM = 2816
FFN_CONV_K = 3
NORM_EPS = 1e-6
HEAD_LANES = 64

ADAM_LR = 0.001
ADAM_B1 = 0.9
ADAM_B2 = 0.999
ADAM_EPS = 1e-08
ADAM_WD = 0.01
ADAM_STEP = 10

PROJ_W = 2816
PROJ_SSD_W = 1792
PROJ_Z_BLK = 2
PROJ_DT_BLK = 12
PROJ_POOL_BLK = 7
PROJ_Q_BLK, PROJ_K_BLK, PROJ_V_BLK = 16, 18, 20
MIX_POOL_BLK = 2
MIX_ATT_BLK = 6
VMEM_LIMIT = 56 * 1024 * 1024
ROW_TILE = 512
CONV_HALO = 8
POOL_HALO = 16
ATT_KPAD = ATT_BLOCK * 16
MESH = pl.DeviceIdType.MESH
_HBM = pl.BlockSpec(memory_space=pl.ANY)


def _cparams(*sem):
    return pltpu.CompilerParams(dimension_semantics=sem, vmem_limit_bytes=VMEM_LIMIT)


def _resident(shape):
    return pl.BlockSpec(shape, lambda i: (0,) * len(shape), pipeline_mode=pl.Buffered(1))


def _silu(x):
    return x * jax.nn.sigmoid(x)


def _pick_lane(v, h):
    lane = lax.broadcasted_iota(jnp.int32, v.shape, 1)
    return jnp.sum(jnp.where(lane == h, v, 0.0), axis=1, keepdims=True)


def _pick_row(v, h):
    row = lax.broadcasted_iota(jnp.int32, v.shape, 0)
    return jnp.sum(jnp.where(row == h, v, 0.0), axis=0, keepdims=True)


def _head_of_lane(width):
    return lax.broadcasted_iota(jnp.int32, (1, width), 1) // HEAD_LANES


@functools.partial(jax.custom_vjp, nondiff_argnums=(1, 2))
def _shift_rows(x_ext, s, halo):
    y = x_ext if s == 0 else pltpu.roll(x_ext, s, 0)
    return y[halo:]


def _shift_rows_fwd(x_ext, s, halo):
    return _shift_rows(x_ext, s, halo), None


def _shift_rows_bwd(s, halo, _, g):
    ge = jnp.concatenate([jnp.zeros((halo, g.shape[1]), g.dtype), g], axis=0)
    return (ge if s == 0 else pltpu.roll(ge, ge.shape[0] - s, 0),)


_shift_rows.defvjp(_shift_rows_fwd, _shift_rows_bwd)


@functools.partial(jax.custom_vjp, nondiff_argnums=(1,))
def _roll_rows(x, s):
    return pltpu.roll(x, s, 0)


def _roll_rows_fwd(x, s):
    return _roll_rows(x, s), None


def _roll_rows_bwd(s, _, g):
    return (pltpu.roll(g, g.shape[0] - s, 0),)


_roll_rows.defvjp(_roll_rows_fwd, _roll_rows_bwd)


def _rms_modulate(xv, g, sc, sh):
    r = lax.rsqrt(jnp.mean(xv * xv, axis=-1, keepdims=True) + NORM_EPS)
    return (xv * r * g) * (1.0 + sc) + sh


def _mm(a, w, *, name, nt=False, tm=ROW_TILE, tn=None, out_dtype=F32, norm=None, residual=None):
    t, k = a.shape
    n = w.shape[0] if nt else w.shape[1]
    tn = tn or n
    assert tn == n or (norm is None and residual is None)
    extra_in = list(norm or ()) + list(residual or ())

    def body(*refs):
        a_ref, w_ref = refs[:2]
        ins = refs[2:2 + len(extra_in)]
        outs = refs[2 + len(extra_in):]
        if norm is None:
            av = a_ref[...].astype(BF16)
        else:
            av = _rms_modulate(a_ref[...], ins[0][...], ins[1][...], ins[2][...]).astype(BF16)
            outs[1][...] = av
        if nt:
            acc = lax.dot_general(av, w_ref[...], (((1,), (1,)), ((), ())), preferred_element_type=F32)
        else:
            acc = jnp.dot(av, w_ref[...], preferred_element_type=F32)
        outs[0][...] = acc.astype(out_dtype)
        if residual is not None:
            x_ref, gate_ref = ins[-2:]
            outs[-1][...] = x_ref[...] + gate_ref[...] * acc

    row = lambda width: pl.BlockSpec((1, width), lambda i, j: (0, 0))
    tile = lambda width: pl.BlockSpec((tm, width), lambda i, j: (i, 0))
    w_spec = pl.BlockSpec((tn, k), lambda i, j: (j, 0)) if nt else pl.BlockSpec((k, tn), lambda i, j: (0, j))
    in_specs = [tile(k), w_spec] + ([row(k)] * 3 if norm else []) + ([tile(n), row(n)] if residual else [])
    out_specs = [pl.BlockSpec((tm, tn), lambda i, j: (i, j))] + ([tile(k)] if norm else []) + \
        ([tile(n)] if residual else [])
    out_shape = [jax.ShapeDtypeStruct((t, n), out_dtype)] + \
        ([jax.ShapeDtypeStruct((t, k), BF16)] if norm else []) + \
        ([jax.ShapeDtypeStruct((t, n), F32)] if residual else [])
    outs = pl.pallas_call(
        body, grid=(t // tm, n // tn), in_specs=in_specs, out_specs=out_specs, out_shape=out_shape, name=name,
        compiler_params=_cparams("parallel", "parallel"))(a, w, *extra_in)
    return outs[0] if len(outs) == 1 else outs


def _wgrad(a, b, *, name, tk=None, tn=None, tt=2048, out_dtype=BF16):
    t, k = a.shape
    n = b.shape[1]
    tk = tk or k
    tn = tn or n
    steps = t // tt

    def body(a_ref, b_ref, o_ref, acc_ref):
        s = pl.program_id(2)

        @pl.when(s == 0)
        def _():
            acc_ref[...] = jnp.zeros_like(acc_ref)

        acc_ref[...] += lax.dot_general(a_ref[...].astype(BF16), b_ref[...].astype(BF16),
                                        (((0,), (0,)), ((), ())), preferred_element_type=F32)

        @pl.when(s == steps - 1)
        def _():
            o_ref[...] = acc_ref[...].astype(out_dtype)

    return pl.pallas_call(
        body, grid=(k // tk, n // tn, steps),
        in_specs=[pl.BlockSpec((tt, tk), lambda i, j, s: (s, i)), pl.BlockSpec((tt, tn), lambda i, j, s: (s, j))],
        out_specs=pl.BlockSpec((tk, tn), lambda i, j, s: (i, j)),
        out_shape=jax.ShapeDtypeStruct((k, n), out_dtype),
        scratch_shapes=[pltpu.VMEM((tk, tn), F32)], name=name,
        compiler_params=_cparams("parallel", "parallel", "arbitrary"))(a, b)


def _norm_mod_bwd(x, dh, dres, g, sc, *, name, w=None, tm=ROW_TILE):
    s, d = x.shape
    steps = s // tm

    def body(x_ref, dh_ref, dres_ref, g_ref, sc_ref, *rest):
        w_ref = rest[0] if w is not None else None
        dx_ref, dg_ref, dsc_ref, dsh_ref, da_acc, dsh_acc = rest[-6:]
        i = pl.program_id(0)

        @pl.when(i == 0)
        def _():
            da_acc[...] = jnp.zeros_like(da_acc)
            dsh_acc[...] = jnp.zeros_like(dsh_acc)

        xv = x_ref[...]
        if w is None:
            dhv = dh_ref[...].astype(F32)
        else:
            dhv = jnp.dot(dh_ref[...], w_ref[...], preferred_element_type=F32)
        r = lax.rsqrt(jnp.mean(xv * xv, axis=-1, keepdims=True) + NORM_EPS)
        xhat = xv * r
        gain = g_ref[...] * (1.0 + sc_ref[...])
        dxhat = dhv * gain
        dx_ref[...] = dres_ref[...] + r * (dxhat - xhat * jnp.mean(dxhat * xhat, axis=-1, keepdims=True))
        da_acc[...] += jnp.sum(dhv * xhat, axis=0, keepdims=True)
        dsh_acc[...] += jnp.sum(dhv, axis=0, keepdims=True)

        @pl.when(i == steps - 1)
        def _():
            dg_ref[...] = da_acc[...] * (1.0 + sc_ref[...])
            dsc_ref[...] = da_acc[...] * g_ref[...]
            dsh_ref[...] = dsh_acc[...]

    row = pl.BlockSpec((1, d), lambda i: (0, 0))
    tile = pl.BlockSpec((tm, d), lambda i: (i, 0))
    row_shape = jax.ShapeDtypeStruct((1, d), F32)
    dh_spec = tile if w is None else pl.BlockSpec((tm, dh.shape[1]), lambda i: (i, 0))
    return pl.pallas_call(
        body, grid=(steps,), in_specs=[tile, dh_spec, tile, row, row] + ([] if w is None else [_resident(w.shape)]),
        out_specs=[tile, row, row, row],
        out_shape=[jax.ShapeDtypeStruct((s, d), F32), row_shape, row_shape, row_shape],
        scratch_shapes=[pltpu.VMEM((1, d), F32), pltpu.VMEM((1, d), F32)], name=name,
        compiler_params=_cparams("arbitrary"))(x, dh, dres, g, sc, *([] if w is None else [w]))


def _mid_bwd(x1, dh2, dx2, norm_g, sc, gate, mo, w_out, *, name, tm=ROW_TILE):
    s, d = x1.shape
    steps = s // tm

    def body(x_ref, dh_ref, dres_ref, ng_ref, sc_ref, g_ref, mo_ref, w_ref,
             dx_ref, dng_ref, dsc_ref, dsh_ref, dmix_ref, dmo_ref, dg_ref, da_acc, dsh_acc):
        i = pl.program_id(0)

        @pl.when(i == 0)
        def _():
            da_acc[...] = jnp.zeros_like(da_acc)
            dsh_acc[...] = jnp.zeros_like(dsh_acc)
            dg_ref[...] = jnp.zeros_like(dg_ref)

        xv = x_ref[...]
        dhv = dh_ref[...]
        r = lax.rsqrt(jnp.mean(xv * xv, axis=-1, keepdims=True) + NORM_EPS)
        xhat = xv * r
        dxhat = dhv * (ng_ref[...] * (1.0 + sc_ref[...]))
        dxv = dres_ref[...] + r * (dxhat - xhat * jnp.mean(dxhat * xhat, axis=-1, keepdims=True))
        dx_ref[...] = dxv
        da_acc[...] += jnp.sum(dhv * xhat, axis=0, keepdims=True)
        dsh_acc[...] += jnp.sum(dhv, axis=0, keepdims=True)
        dmo = (g_ref[...] * dxv).astype(BF16)
        dmo_ref[...] = dmo
        dg_ref[...] += jnp.sum(dxv * mo_ref[...], axis=0, keepdims=True)
        dmix_ref[...] = lax.dot_general(dmo, w_ref[...], (((1,), (1,)), ((), ())), preferred_element_type=F32)

        @pl.when(i == steps - 1)
        def _():
            dng_ref[...] = da_acc[...] * (1.0 + sc_ref[...])
            dsc_ref[...] = da_acc[...] * ng_ref[...]
            dsh_ref[...] = dsh_acc[...]

    tile = pl.BlockSpec((tm, d), lambda i: (i, 0))
    row = pl.BlockSpec((1, d), lambda i: (0, 0))
    mix_tile = pl.BlockSpec((tm, w_out.shape[0]), lambda i: (i, 0))
    row_shape = jax.ShapeDtypeStruct((1, d), F32)
    return pl.pallas_call(
        body, grid=(steps,), in_specs=[tile, tile, tile, row, row, row, tile, _resident(w_out.shape)],
        out_specs=[tile, row, row, row, mix_tile, tile, row],
        out_shape=[jax.ShapeDtypeStruct((s, d), F32), row_shape, row_shape, row_shape,
                   jax.ShapeDtypeStruct((s, w_out.shape[0]), F32), jax.ShapeDtypeStruct((s, d), BF16), row_shape],
        scratch_shapes=[pltpu.VMEM((1, d), F32), pltpu.VMEM((1, d), F32)], name=name,
        compiler_params=_cparams("arbitrary"))(x1, dh2, dx2, norm_g, sc, gate, mo, w_out)


def _final_loss(x, g, target, *, name, tm=ROW_TILE):
    s, d = x.shape
    steps = s // tm

    def body(x_ref, g_ref, t_ref, loss_ref, dx_ref, dg_ref, sq_acc):
        i = pl.program_id(0)

        @pl.when(i == 0)
        def _():
            sq_acc[...] = jnp.zeros_like(sq_acc)
            dg_ref[...] = jnp.zeros_like(dg_ref)

        xv = x_ref[...]
        r = lax.rsqrt(jnp.mean(xv * xv, axis=-1, keepdims=True) + NORM_EPS)
        xhat = xv * r
        err = xhat * g_ref[...] - t_ref[...]
        sq_acc[...] += jnp.sum(err * err, axis=0, keepdims=True)
        dy = err * (1.0 / d)
        dg_ref[...] += jnp.sum(dy * xhat, axis=0, keepdims=True)
        dxhat = dy * g_ref[...]
        dx_ref[...] = r * (dxhat - xhat * jnp.mean(dxhat * xhat, axis=-1, keepdims=True))

        @pl.when(i == steps - 1)
        def _():
            total = jnp.sum(sq_acc[...], axis=1, keepdims=True) * (0.5 / d)
            loss_ref[...] = jnp.broadcast_to(total, loss_ref.shape)

    tile = pl.BlockSpec((tm, d), lambda i: (i, 0))
    row = pl.BlockSpec((1, d), lambda i: (0, 0))
    return pl.pallas_call(
        body, grid=(steps,), in_specs=[tile, row, tile],
        out_specs=[pl.BlockSpec((1, 128), lambda i: (0, 0)), tile, row],
        out_shape=[jax.ShapeDtypeStruct((1, 128), F32), jax.ShapeDtypeStruct((s, d), F32),
                   jax.ShapeDtypeStruct((1, d), F32)],
        scratch_shapes=[pltpu.VMEM((1, d), F32)], name=name, compiler_params=_cparams("arbitrary"))(x, g, target)


def _ssd_chunk(z, xbc_ext, dt_raw, conv_w, conv_b, dt_bias, a_log, d_skip, norm_g, h_in):
    q = z.shape[0]
    gw = SSD_HPG * HEAD_LANES
    xc = conv_b
    for k in range(SSD_CONV_K):
        xc = xc + _pick_row(conv_w, k) * _shift_rows(xbc_ext, SSD_CONV_K - 1 - k, CONV_HALO)
    xc = _silu(xc)
    dt = jax.nn.softplus(dt_raw + dt_bias)
    da = dt * (-jnp.exp(a_log))
    ri = lax.broadcasted_iota(jnp.int32, (q, q), 0)
    ci = lax.broadcasted_iota(jnp.int32, (q, q), 1)
    causal = ri >= ci
    tril = causal.astype(F32)
    a_cum = jnp.dot(tril, da, preferred_element_type=F32, precision=lax.Precision.HIGHEST)
    a_cum_t = lax.dot_general(da, tril, (((0,), (1,)), ((), ())), preferred_element_type=F32,
                              precision=lax.Precision.HIGHEST)
    a_last = _pick_row(a_cum, q - 1)
    head = _head_of_lane(gw)
    ys, hs = [], []
    for g in range(2):
        xs = xc[:, gw * g:gw * (g + 1)]
        bm = xc[:, SSD_INNER + SSD_STATE * g:SSD_INNER + SSD_STATE * (g + 1)]
        cm = xc[:, SSD_INNER + 2 * SSD_STATE + SSD_STATE * g:SSD_INNER + 2 * SSD_STATE + SSD_STATE * (g + 1)]
        cb = lax.dot_general(cm.astype(BF16), bm.astype(BF16), (((1,), (1,)), ((), ())), preferred_element_type=F32)
        cols = [_pick_lane(a_cum, SSD_HPG * g + j) for j in range(SSD_HPG)]
        lasts = [_pick_lane(a_last, SSD_HPG * g + j) for j in range(SSD_HPG)]
        dt_exp = sum(jnp.where(head == j, _pick_lane(dt, SSD_HPG * g + j), 0.0) for j in range(SSD_HPG))
        d_exp = sum(jnp.where(head == j, _pick_lane(d_skip, SSD_HPG * g + j), 0.0) for j in range(SSD_HPG))
        e_cum = sum(jnp.where(head == j, jnp.exp(cols[j]), 0.0) for j in range(SSD_HPG))
        c_dec = sum(jnp.where(head == j, jnp.exp(lasts[j]), 0.0) for j in range(SSD_HPG))
        xsdt = (xs * dt_exp).astype(BF16)
        y_diag = jnp.zeros((q, gw), F32)
        st_new = jnp.zeros((SSD_STATE, gw), F32)
        for j in range(SSD_HPG):
            row = _pick_row(a_cum_t, SSD_HPG * g + j)
            lmat = jnp.exp(jnp.where(causal, cols[j] - row, -jnp.inf))
            r = jnp.dot((cb * lmat).astype(BF16), xsdt, preferred_element_type=F32)
            y_diag = y_diag + jnp.where(head == j, r, 0.0)
            bd = (bm * jnp.exp(lasts[j] - cols[j])).astype(BF16)
            st = lax.dot_general(bd, xsdt, (((0,), (0,)), ((), ())), preferred_element_type=F32)
            st_new = st_new + jnp.where(head == j, st, 0.0)
        y_off = jnp.dot(cm.astype(BF16), h_in[g].astype(BF16), preferred_element_type=F32) * e_cum
        hs.append(h_in[g] * c_dec + st_new)
        y = y_diag + y_off + d_exp * xs
        yz = y * _silu(z[:, gw * g:gw * (g + 1)])
        yz = yz * lax.rsqrt(jnp.mean(yz * yz, axis=-1, keepdims=True) + NORM_EPS)
        ys.append(yz * norm_g[:, gw * g:gw * (g + 1)])
    return jnp.concatenate(ys, axis=1), tuple(hs)


_SSD_NCHUNK = SEQ // SSD_CHUNK
_HALO_PER_CHUNK = SSD_CHUNK // CONV_HALO


def _ssd_param_specs(const):
    return [pl.BlockSpec((8, SSD_CONV_CH), const), pl.BlockSpec((1, SSD_CONV_CH), const),
            pl.BlockSpec((1, 128), const), pl.BlockSpec((1, 128), const), pl.BlockSpec((1, 128), const),
            pl.BlockSpec((1, SSD_INNER), const)]


def _ssd_fwd(proj, conv_w, conv_b, dt_bias, a_log, d_skip, norm_g, *, name):
    q = SSD_CHUNK

    def body(z_ref, xbc_ref, halo_ref, dt_ref, cw_ref, cb_ref, db_ref, al_ref, d_ref, ng_ref, y_ref, hs_ref, h_acc):
        i = pl.program_id(0)

        @pl.when(i == 0)
        def _():
            h_acc[...] = jnp.zeros_like(h_acc)

        halo = jnp.where(i == 0, 0.0, halo_ref[...])
        xbc_ext = jnp.concatenate([halo, xbc_ref[...]], axis=0)
        h_in = (h_acc[0], h_acc[1])
        hs_ref[0, 0] = h_in[0]
        hs_ref[0, 1] = h_in[1]
        y, h_out = _ssd_chunk(z_ref[...], xbc_ext, dt_ref[...], cw_ref[...], cb_ref[...], db_ref[...], al_ref[...],
                              d_ref[...], ng_ref[...], h_in)
        y_ref[...] = y.astype(BF16)
        h_acc[0] = h_out[0]
        h_acc[1] = h_out[1]

    const = lambda i: (0, 0)
    return pl.pallas_call(
        body, grid=(_SSD_NCHUNK,),
        in_specs=[pl.BlockSpec((q, SSD_INNER), lambda i: (i, PROJ_Z_BLK)),
                  pl.BlockSpec((q, SSD_CONV_CH), lambda i: (i, 0)),
                  pl.BlockSpec((CONV_HALO, SSD_CONV_CH), lambda i: (jnp.maximum(i * _HALO_PER_CHUNK - 1, 0), 0)),
                  pl.BlockSpec((q, 128), lambda i: (i, PROJ_DT_BLK))] + _ssd_param_specs(const),
        out_specs=[pl.BlockSpec((q, SSD_INNER), lambda i: (i, 0)),
                   pl.BlockSpec((1, 2, SSD_STATE, 256), lambda i: (i, 0, 0, 0))],
        out_shape=[jax.ShapeDtypeStruct((SEQ, D_MODEL), BF16),
                   jax.ShapeDtypeStruct((_SSD_NCHUNK, 2, SSD_STATE, 256), F32)],
        scratch_shapes=[pltpu.VMEM((2, SSD_STATE, 256), F32)], name=name,
        compiler_params=_cparams("arbitrary"))(proj, proj, proj, proj, conv_w, conv_b, dt_bias, a_log, d_skip, norm_g)


def _ssd_bwd(proj, hstates, dmix, conv_w, conv_b, dt_bias, a_log, d_skip, norm_g, *, name):
    q = SSD_CHUNK
    last = _SSD_NCHUNK - 1

    def body(z_ref, xbc_ref, halo_ref, dt_ref, hs_ref, dy_ref, cw_ref, cb_ref, db_ref, al_ref, d_ref, ng_ref,
             dp_ref, dcw_ref, dcb_ref, ddb_ref, dal_ref, dd_ref, dng_ref, dh_acc, dhalo_acc):
        i = pl.program_id(0)

        @pl.when(i == 0)
        def _():
            dh_acc[...] = jnp.zeros_like(dh_acc)
            dhalo_acc[...] = jnp.zeros_like(dhalo_acc)
            for r in (dcw_ref, dcb_ref, ddb_ref, dal_ref, dd_ref, dng_ref):
                r[...] = jnp.zeros_like(r)

        halo = jnp.where(i == last, 0.0, halo_ref[...])
        xbc_ext = jnp.concatenate([halo, xbc_ref[...]], axis=0)
        _, vjp = jax.vjp(_ssd_chunk, z_ref[...], xbc_ext, dt_ref[...], cw_ref[...], cb_ref[...], db_ref[...],
                         al_ref[...], d_ref[...], ng_ref[...], (hs_ref[0, 0], hs_ref[0, 1]))
        gz, gx, gdt, gcw, gcb, gdb, gal, gd, gng, gh = vjp((dy_ref[...], (dh_acc[0], dh_acc[1])))
        dxbc = jnp.concatenate([gx[CONV_HALO:q], gx[q:] + dhalo_acc[...]], axis=0)
        dp_ref[...] = jnp.concatenate([dxbc, gz, gdt, jnp.zeros_like(gdt)], axis=1).astype(BF16)
        dhalo_acc[...] = gx[:CONV_HALO]
        dh_acc[0] = gh[0]
        dh_acc[1] = gh[1]
        dcw_ref[...] += gcw
        dcb_ref[...] += gcb
        ddb_ref[...] += gdb
        dal_ref[...] += gal
        dd_ref[...] += gd
        dng_ref[...] += gng

    const = lambda i: (0, 0)
    rev = lambda i: last - i
    row = lambda n: jax.ShapeDtypeStruct((1, n), F32)
    return pl.pallas_call(
        body, grid=(_SSD_NCHUNK,),
        in_specs=[pl.BlockSpec((q, SSD_INNER), lambda i: (rev(i), PROJ_Z_BLK)),
                  pl.BlockSpec((q, SSD_CONV_CH), lambda i: (rev(i), 0)),
                  pl.BlockSpec((CONV_HALO, SSD_CONV_CH), lambda i: (jnp.maximum(rev(i) * _HALO_PER_CHUNK - 1, 0), 0)),
                  pl.BlockSpec((q, 128), lambda i: (rev(i), PROJ_DT_BLK)),
                  pl.BlockSpec((1, 2, SSD_STATE, 256), lambda i: (rev(i), 0, 0, 0)),
                  pl.BlockSpec((q, SSD_INNER), lambda i: (rev(i), 0))] + _ssd_param_specs(const),
        out_specs=[pl.BlockSpec((q, PROJ_SSD_W), lambda i: (rev(i), 0))] + _ssd_param_specs(const),
        out_shape=[jax.ShapeDtypeStruct((SEQ, PROJ_W), BF16), jax.ShapeDtypeStruct((8, SSD_CONV_CH), F32),
                   row(SSD_CONV_CH), row(128), row(128), row(128), row(SSD_INNER)],
        scratch_shapes=[pltpu.VMEM((2, SSD_STATE, 256), F32), pltpu.VMEM((CONV_HALO, SSD_CONV_CH), F32)], name=name,
        compiler_params=_cparams("arbitrary"))(proj, proj, proj, proj, hstates, dmix, conv_w, conv_b, dt_bias, a_log,
                                                d_skip, norm_g)


def _rope_tables(pos_col, inv_freq_lane, *, name):
    s = pos_col.shape[0]

    def body(p_ref, f_ref, c_ref, s1_ref, s2_ref):
        ang = p_ref[...] * f_ref[...]
        within = lax.broadcasted_iota(jnp.int32, ang.shape, 1) % HEAD_LANES
        half = ROT_DIM // 2
        c_ref[...] = jnp.where(within < ROT_DIM, jnp.cos(ang), 1.0)
        sn = jnp.sin(ang)
        s1_ref[...] = jnp.where(within < half, -sn, 0.0)
        s2_ref[...] = jnp.where((within >= half) & (within < ROT_DIM), sn, 0.0)

    shp = jax.ShapeDtypeStruct((s, 128), F32)
    return pl.pallas_call(body, out_shape=[shp, shp, shp], name=name,
                          compiler_params=pltpu.CompilerParams(vmem_limit_bytes=VMEM_LIMIT))(pos_col, inv_freq_lane)


def _rope(t, c, s1, s2):
    half = ROT_DIM // 2
    return t * c + pltpu.roll(t, 128 - half, 1) * s1 + pltpu.roll(t, half, 1) * s2


def _rope_t(g, c, s1, s2):
    half = ROT_DIM // 2
    return g * c + pltpu.roll(g * s1, half, 1) + pltpu.roll(g * s2, 128 - half, 1)


def _att_valid(b):
    qi = lax.broadcasted_iota(jnp.int32, (ATT_BLOCK, 2 * ATT_BLOCK), 0)
    kj = lax.broadcasted_iota(jnp.int32, (ATT_BLOCK, 2 * ATT_BLOCK), 1)
    rel = qi + ATT_BLOCK - kj
    return (rel >= 0) & (rel <= ATT_BLOCK) & (b * ATT_BLOCK + kj - ATT_BLOCK >= 0)


def _att_slices(i, d):
    if d == 1:
        qstart = pl.multiple_of(i * ATT_BLOCK, ATT_BLOCK)
        return i, pl.ds(qstart, ATT_BLOCK), pl.ds(pl.multiple_of(qstart - ATT_BLOCK + ATT_KPAD, ATT_BLOCK), 2 * ATT_BLOCK)
    r = i % d
    b = i // d
    qstart = r + d * ATT_BLOCK * b
    return b, pl.ds(qstart, ATT_BLOCK, stride=d), pl.ds(qstart - ATT_BLOCK * d + ATT_KPAD, 2 * ATT_BLOCK, stride=d)


_ATT_NBLK = SEQ // ATT_BLOCK
_ATT_SCALE = HEAD_LANES ** -0.5
_ATT_UNROLL_FWD = 8
_ATT_UNROLL = 4


def _att_fwd(proj, cos, sin1, sin2, mix, *, name):
    s = SEQ

    def body(q_ref, k_ref, v_ref, c_ref, s1_ref, s2_ref, _, o_ref, lse_ref, mix_ref, qs, ks, vs, acc, m_s, l_s):
        c, s1, s2 = c_ref[...], s1_ref[...], s2_ref[...]
        qs[...] = _rope(q_ref[...], c, s1, s2) * _ATT_SCALE
        zeros = jnp.zeros((ATT_KPAD, 128), F32)
        ks[pl.ds(0, ATT_KPAD), :] = zeros
        vs[pl.ds(0, ATT_KPAD), :] = zeros
        ks[pl.ds(ATT_KPAD, s), :] = _rope(k_ref[...], c, s1, s2)
        vs[pl.ds(ATT_KPAD, s), :] = v_ref[...]
        head0 = _head_of_lane(128) == 0

        for bi, (_, d) in enumerate(ATT_PATTERNS):
            def blk(i, carry, d=d, first=(bi == 0)):
                b, sq, sk = _att_slices(i, d)
                qb = qs[sq, :]
                kw = ks[sk, :].astype(BF16)
                vw = vs[sk, :].astype(BF16)
                valid = _att_valid(b)
                ms, ls, os_ = [], [], []
                for hh in range(2):
                    qh = jnp.where(head0 if hh == 0 else ~head0, qb, 0.0).astype(BF16)
                    sc = lax.dot_general(qh, kw, (((1,), (1,)), ((), ())), preferred_element_type=F32)
                    sc = jnp.where(valid, sc, -jnp.inf)
                    mb = jnp.max(sc, axis=1, keepdims=True)
                    p = jnp.exp(sc - mb)
                    ms.append(mb)
                    ls.append(jnp.sum(p, axis=1, keepdims=True))
                    os_.append(jnp.dot(p.astype(BF16), vw, preferred_element_type=F32))
                m_b = jnp.where(head0, ms[0], ms[1])
                l_b = jnp.where(head0, ls[0], ls[1])
                o_b = jnp.where(head0, os_[0], os_[1])
                if first:
                    m_s[sq, :] = m_b
                    l_s[sq, :] = l_b
                    acc[sq, :] = o_b
                else:
                    m_old = m_s[sq, :]
                    m_new = jnp.maximum(m_old, m_b)
                    a_old = jnp.exp(m_old - m_new)
                    a_b = jnp.exp(m_b - m_new)
                    m_s[sq, :] = m_new
                    l_s[sq, :] = l_s[sq, :] * a_old + l_b * a_b
                    acc[sq, :] = acc[sq, :] * a_old + o_b * a_b
                return carry

            lax.fori_loop(0, _ATT_NBLK, blk, 0, unroll=_ATT_UNROLL_FWD)

        out = acc[...] / l_s[...]
        o_ref[...] = out
        mix_ref[...] = out.astype(BF16)
        lse_ref[...] = m_s[...] + jnp.log(l_s[...])

    col = lambda base: pl.BlockSpec((s, 128), lambda p: (0, base + p))
    tab = pl.BlockSpec((s, 128), lambda p: (0, 0))
    big = pltpu.VMEM((ATT_KPAD + s, 128), F32)
    tok = pltpu.VMEM((s, 128), F32)
    return pl.pallas_call(
        body, grid=(2,), in_specs=[col(PROJ_Q_BLK), col(PROJ_K_BLK), col(PROJ_V_BLK), tab, tab, tab, _HBM],
        out_specs=[pl.BlockSpec((s, 128), lambda p: (0, p)), pl.BlockSpec((s, 128), lambda p: (0, p)),
                   col(MIX_ATT_BLK)],
        out_shape=[jax.ShapeDtypeStruct((s, ATT_W), F32), jax.ShapeDtypeStruct((s, ATT_W), F32),
                   jax.ShapeDtypeStruct(mix.shape, mix.dtype)],
        input_output_aliases={6: 2}, scratch_shapes=[tok, big, big, tok, tok, tok], name=name,
        compiler_params=_cparams("arbitrary"))(proj, proj, proj, cos, sin1, sin2, mix)


def _att_bwd(proj, cos, sin1, sin2, out, lse, dmix, dproj, *, name):
    s = SEQ

    def body(proj_ref, c_hbm, s1_hbm, s2_hbm, out_hbm, lse_hbm, dmix_hbm, _, dproj_hbm,
             c_ref, s1_ref, s2_ref, o_ref, lse_ref, do_ref, qs, ks, vs, dqs, dks, dvs, staged, sems):
        def start(copies):
            for cp in copies:
                cp.start()
            return copies

        def load(pair):
            lanes = pl.ds(128 * pair, 128)
            rows = pl.ds(ATT_KPAD, s)
            return start([
                pltpu.make_async_copy(proj_ref.at[:, pl.ds(128 * (PROJ_Q_BLK + pair), 128)], qs, sems.at[0]),
                pltpu.make_async_copy(proj_ref.at[:, pl.ds(128 * (PROJ_K_BLK + pair), 128)], ks.at[rows, :], sems.at[1]),
                pltpu.make_async_copy(proj_ref.at[:, pl.ds(128 * (PROJ_V_BLK + pair), 128)], vs.at[rows, :], sems.at[2]),
                pltpu.make_async_copy(out_hbm.at[:, lanes], o_ref, sems.at[3]),
                pltpu.make_async_copy(lse_hbm.at[:, lanes], lse_ref, sems.at[4]),
                pltpu.make_async_copy(dmix_hbm.at[:, pl.ds(128 * (MIX_ATT_BLK + pair), 128)], do_ref, sems.at[5])])

        tables = start([pltpu.make_async_copy(c_hbm, c_ref, sems.at[6]),
                        pltpu.make_async_copy(s1_hbm, s1_ref, sems.at[7]),
                        pltpu.make_async_copy(s2_hbm, s2_ref, sems.at[8])])
        loads = load(0)
        for cp in tables:
            cp.wait()
        head0 = _head_of_lane(128) == 0
        zeros = jnp.zeros((ATT_KPAD, 128), F32)
        for pair in range(2):
            for cp in loads:
                cp.wait()
            c, s1, s2 = c_ref[...], s1_ref[...], s2_ref[...]
            qs[...] = _rope(qs[...], c, s1, s2) * _ATT_SCALE
            ks[pl.ds(0, ATT_KPAD), :] = zeros
            vs[pl.ds(0, ATT_KPAD), :] = zeros
            ks[pl.ds(ATT_KPAD, s), :] = _rope(ks[pl.ds(ATT_KPAD, s), :], c, s1, s2)
            dqs[...] = jnp.zeros_like(dqs)
            dks[...] = jnp.zeros_like(dks)
            dvs[...] = jnp.zeros_like(dvs)

            for _, d in ATT_PATTERNS:
                def blk(i, carry, d=d):
                    b, sq, sk = _att_slices(i, d)
                    qb = qs[sq, :]
                    kw = ks[sk, :].astype(BF16)
                    vw = vs[sk, :].astype(BF16)
                    dob = do_ref[sq, :]
                    lse_b = lse_ref[sq, :]
                    dd = dob * o_ref[sq, :]
                    valid = _att_valid(b)
                    dq_b = jnp.zeros((ATT_BLOCK, 128), F32)
                    dk_w = jnp.zeros((2 * ATT_BLOCK, 128), F32)
                    dv_w = jnp.zeros((2 * ATT_BLOCK, 128), F32)
                    for hh in range(2):
                        hm = head0 if hh == 0 else ~head0
                        qh = jnp.where(hm, qb, 0.0).astype(BF16)
                        doh = jnp.where(hm, dob, 0.0).astype(BF16)
                        lse_h = _pick_lane(lse_b, hh * HEAD_LANES)
                        d_h = jnp.sum(jnp.where(hm, dd, 0.0), axis=1, keepdims=True)
                        sc = lax.dot_general(qh, kw, (((1,), (1,)), ((), ())), preferred_element_type=F32)
                        p = jnp.where(valid, jnp.exp(sc - lse_h), 0.0)
                        dp = lax.dot_general(doh, vw, (((1,), (1,)), ((), ())), preferred_element_type=F32)
                        ds = (p * (dp - d_h)).astype(BF16)
                        dq_b = dq_b + jnp.where(hm, jnp.dot(ds, kw, preferred_element_type=F32), 0.0)
                        dk_w = dk_w + lax.dot_general(ds, qh, (((0,), (0,)), ((), ())), preferred_element_type=F32)
                        dv_w = dv_w + lax.dot_general(p.astype(BF16), doh, (((0,), (0,)), ((), ())),
                                                      preferred_element_type=F32)
                    dqs[sq, :] += dq_b
                    dks[sk, :] += dk_w
                    dvs[sk, :] += dv_w
                    return carry

                lax.fori_loop(0, _ATT_NBLK, blk, 0, unroll=_ATT_UNROLL)

            staged[0] = _rope_t(dqs[...] * _ATT_SCALE, c, s1, s2).astype(BF16)
            staged[1] = _rope_t(dks[pl.ds(ATT_KPAD, s), :], c, s1, s2).astype(BF16)
            staged[2] = dvs[pl.ds(ATT_KPAD, s), :].astype(BF16)
            stores = start([
                pltpu.make_async_copy(staged.at[j], dproj_hbm.at[:, pl.ds(128 * (col + pair), 128)], sems.at[9 + j])
                for j, col in enumerate((PROJ_Q_BLK, PROJ_K_BLK, PROJ_V_BLK))])
            if pair == 0:
                loads = load(1)
            for cp in stores:
                cp.wait()

    big = pltpu.VMEM((ATT_KPAD + s, 128), F32)
    tok = pltpu.VMEM((s, 128), F32)
    return pl.pallas_call(
        body, in_specs=[_HBM] * 8, out_specs=_HBM, out_shape=jax.ShapeDtypeStruct(dproj.shape, dproj.dtype),
        input_output_aliases={7: 0},
        scratch_shapes=[tok] * 6 + [tok, big, big, tok, big, big, pltpu.VMEM((3, s, 128), BF16),
                                    pltpu.SemaphoreType.DMA((12,))], name=name,
        compiler_params=pltpu.CompilerParams(vmem_limit_bytes=VMEM_LIMIT))(
            proj, cos, sin1, sin2, out, lse, dmix, dproj)


_POOL_TM = 512
_POOL_NT = SEQ // _POOL_TM
_POOL_HALO_PER_TILE = _POOL_TM // POOL_HALO


def _pool_tile(u_ext, w_bd, scale, t0):
    s2 = u_ext + _roll_rows(u_ext, 1)
    s4 = s2 + _roll_rows(s2, 2)
    s8 = s4 + _roll_rows(s4, 4)
    s16 = s8 + _roll_rows(s8, 8)
    grp = _head_of_lane(POOL_W)
    sel = jnp.where(grp == 0, s2, jnp.where(grp == 1, s4, jnp.where(grp == 2, s8, s16)))[POOL_HALO:]
    t = sel.shape[0]
    pos = t0 + lax.broadcasted_iota(jnp.int32, (t, POOL_W), 0) + 1
    win = jnp.where(grp == 0, 2, jnp.where(grp == 1, 4, jnp.where(grp == 2, 8, 16)))
    cnt = jnp.minimum(pos, win).astype(F32)
    diff = sel / cnt - u_ext[POOL_HALO:]
    return jnp.dot(diff.astype(BF16), w_bd.astype(BF16), preferred_element_type=F32) * scale


def _pool_fwd(proj, w_bd, scale, mix, *, name):
    tm = _POOL_TM

    def body(u_ref, halo_ref, w_ref, sc_ref, _, y_ref):
        i = pl.program_id(0)
        halo = jnp.where(i == 0, 0.0, halo_ref[...])
        u_ext = jnp.concatenate([halo, u_ref[...]], axis=0)
        y_ref[...] = _pool_tile(u_ext, w_ref[...], sc_ref[...], i * tm).astype(BF16)

    return pl.pallas_call(
        body, grid=(_POOL_NT,),
        in_specs=[pl.BlockSpec((tm, POOL_W), lambda i: (i, PROJ_POOL_BLK)),
                  pl.BlockSpec((POOL_HALO, POOL_W),
                               lambda i: (jnp.maximum(i * _POOL_HALO_PER_TILE - 1, 0), PROJ_POOL_BLK)),
                  pl.BlockSpec((POOL_W, POOL_W), lambda i: (0, 0)), pl.BlockSpec((1, POOL_W), lambda i: (0, 0)), _HBM],
        out_specs=pl.BlockSpec((tm, POOL_W), lambda i: (i, MIX_POOL_BLK)),
        out_shape=jax.ShapeDtypeStruct(mix.shape, mix.dtype), input_output_aliases={4: 0}, name=name,
        compiler_params=_cparams("parallel"))(proj, proj, w_bd, scale, mix)


def _pool_bwd(proj, dmix, w_bd, scale, dproj, *, name):
    tm = _POOL_TM
    last = _POOL_NT - 1

    def body(u_ref, halo_ref, dy_ref, w_ref, sc_ref, _, du_ref, dw_ref, dsc_ref, dhalo_acc):
        i = pl.program_id(0)

        @pl.when(i == 0)
        def _():
            dhalo_acc[...] = jnp.zeros_like(dhalo_acc)
            dw_ref[...] = jnp.zeros_like(dw_ref)
            dsc_ref[...] = jnp.zeros_like(dsc_ref)

        tile = last - i
        halo = jnp.where(tile == 0, 0.0, halo_ref[...])
        u_ext = jnp.concatenate([halo, u_ref[...]], axis=0)
        _, vjp = jax.vjp(functools.partial(_pool_tile, t0=tile * tm), u_ext, w_ref[...], sc_ref[...])
        gu, gw, gs = vjp(dy_ref[...])
        du_ref[...] = jnp.concatenate([gu[POOL_HALO:tm], gu[tm:] + dhalo_acc[...]], axis=0).astype(BF16)
        dhalo_acc[...] = gu[:POOL_HALO]
        dw_ref[...] += gw
        dsc_ref[...] += gs

    rev = lambda i: last - i
    return pl.pallas_call(
        body, grid=(_POOL_NT,),
        in_specs=[pl.BlockSpec((tm, POOL_W), lambda i: (rev(i), PROJ_POOL_BLK)),
                  pl.BlockSpec((POOL_HALO, POOL_W),
                               lambda i: (jnp.maximum(rev(i) * _POOL_HALO_PER_TILE - 1, 0), PROJ_POOL_BLK)),
                  pl.BlockSpec((tm, POOL_W), lambda i: (rev(i), MIX_POOL_BLK)),
                  pl.BlockSpec((POOL_W, POOL_W), lambda i: (0, 0)), pl.BlockSpec((1, POOL_W), lambda i: (0, 0)), _HBM],
        out_specs=[pl.BlockSpec((tm, POOL_W), lambda i: (rev(i), PROJ_POOL_BLK)),
                   pl.BlockSpec((POOL_W, POOL_W), lambda i: (0, 0)), pl.BlockSpec((1, POOL_W), lambda i: (0, 0))],
        out_shape=[jax.ShapeDtypeStruct(dproj.shape, dproj.dtype), jax.ShapeDtypeStruct((POOL_W, POOL_W), F32),
                   jax.ShapeDtypeStruct((1, POOL_W), F32)],
        input_output_aliases={5: 0}, scratch_shapes=[pltpu.VMEM((POOL_HALO, POOL_W), F32)], name=name,
        compiler_params=_cparams("arbitrary"))(proj, proj, dmix, w_bd, scale, dproj)


_FFN_TM = 256
_FFN_NT = SEQ // _FFN_TM
_FFN_HALO_PER_TILE = _FFN_TM // CONV_HALO


def _ffn_act_tile(hid_ext, conv_w, conv_b):
    hc = conv_b
    for k in range(FFN_CONV_K):
        hc = hc + _pick_row(conv_w, k) * _shift_rows(hid_ext, FFN_CONV_K - 1 - k, CONV_HALO)
    return _silu(hc[:, :FFN_DIM]) * hc[:, FFN_DIM:]


def _ffn_fwd(x1, norm_g, sc, sh, gate, up_t, down, conv_w, conv_b, *, name):
    tm = _FFN_TM
    w = 2 * FFN_DIM
    d = D_MODEL

    def body(x_ref, ng_ref, sc_ref, sh_ref, g_ref, up_ref, dn_ref, cw_ref, cb_ref,
             h_ref, hid_ref, act_ref, f_ref, x2_ref, halo_acc):
        i = pl.program_id(0)
        h2 = _rms_modulate(x_ref[...], ng_ref[...], sc_ref[...], sh_ref[...]).astype(BF16)
        h_ref[...] = h2
        hid = lax.dot_general(h2, up_ref[...], (((1,), (1,)), ((), ())), preferred_element_type=F32)
        hid_ref[...] = hid
        halo = jnp.where(i == 0, 0.0, halo_acc[...])
        act = _ffn_act_tile(jnp.concatenate([halo, hid], axis=0), cw_ref[...], cb_ref[...]).astype(BF16)
        halo_acc[...] = hid[tm - CONV_HALO:]
        act_ref[...] = act
        f = jnp.dot(act, dn_ref[...], preferred_element_type=F32)
        f_ref[...] = f
        x2_ref[...] = x_ref[...] + g_ref[...] * f

    tile = lambda n: pl.BlockSpec((tm, n), lambda i: (i, 0))
    return pl.pallas_call(
        body, grid=(_FFN_NT,),
        in_specs=[tile(d)] + [_resident((1, d))] * 4 + [_resident((w, d)), _resident((FFN_DIM, d)),
                                                        _resident((8, w)), _resident((1, w))],
        out_specs=[tile(d), tile(w), tile(FFN_DIM), tile(d), tile(d)],
        out_shape=[jax.ShapeDtypeStruct((SEQ, d), BF16), jax.ShapeDtypeStruct((SEQ, w), F32),
                   jax.ShapeDtypeStruct((SEQ, FFN_DIM), BF16), jax.ShapeDtypeStruct((SEQ, d), F32),
                   jax.ShapeDtypeStruct((SEQ, d), F32)],
        scratch_shapes=[pltpu.VMEM((CONV_HALO, w), F32)], name=name,
        compiler_params=_cparams("arbitrary"))(x1, norm_g, sc, sh, gate, up_t, down, conv_w, conv_b)


def _ffn_bwd(dx2, gate, f, hid, up_t, down, conv_w, conv_b, *, name):
    tm = _FFN_TM
    w = 2 * FFN_DIM
    d = D_MODEL
    last = _FFN_NT - 1

    def body(dx_ref, g_ref, f_ref, h_ref, halo_ref, up_ref, dn_ref, cw_ref, cb_ref,
             df_ref, dg_ref, dh_ref, dh2_ref, dcw_ref, dcb_ref, dhalo_acc):
        i = pl.program_id(0)

        @pl.when(i == 0)
        def _():
            dhalo_acc[...] = jnp.zeros_like(dhalo_acc)
            dcw_ref[...] = jnp.zeros_like(dcw_ref)
            dcb_ref[...] = jnp.zeros_like(dcb_ref)
            dg_ref[...] = jnp.zeros_like(dg_ref)

        dxv = dx_ref[...]
        df = (g_ref[...] * dxv).astype(BF16)
        df_ref[...] = df
        dg_ref[...] += jnp.sum(dxv * f_ref[...], axis=0, keepdims=True)
        dact = lax.dot_general(df, dn_ref[...], (((1,), (1,)), ((), ())), preferred_element_type=F32)
        halo = jnp.where(i == last, 0.0, halo_ref[...])
        hid_ext = jnp.concatenate([halo, h_ref[...]], axis=0)
        _, vjp = jax.vjp(_ffn_act_tile, hid_ext, cw_ref[...], cb_ref[...])
        gh, gw, gb = vjp(dact)
        dhid = jnp.concatenate([gh[CONV_HALO:tm], gh[tm:] + dhalo_acc[...]], axis=0).astype(BF16)
        dhalo_acc[...] = gh[:CONV_HALO]
        dh_ref[...] = dhid
        dh2_ref[...] = jnp.dot(dhid, up_ref[...], preferred_element_type=F32)
        dcw_ref[...] += gw
        dcb_ref[...] += gb

    rev = lambda i: last - i
    tile = lambda n: pl.BlockSpec((tm, n), lambda i: (rev(i), 0))
    acc = lambda shape: pl.BlockSpec(shape, lambda i: (0, 0))
    return pl.pallas_call(
        body, grid=(_FFN_NT,),
        in_specs=[tile(d), _resident((1, d)), tile(d), tile(w),
                  pl.BlockSpec((CONV_HALO, w), lambda i: (jnp.maximum(rev(i) * _FFN_HALO_PER_TILE - 1, 0), 0)),
                  _resident((w, d)), _resident((FFN_DIM, d)), _resident((8, w)), _resident((1, w))],
        out_specs=[tile(d), acc((1, d)), tile(w), tile(d), acc((8, w)), acc((1, w))],
        out_shape=[jax.ShapeDtypeStruct((SEQ, d), BF16), jax.ShapeDtypeStruct((1, d), F32),
                   jax.ShapeDtypeStruct((SEQ, w), BF16), jax.ShapeDtypeStruct((SEQ, d), F32),
                   jax.ShapeDtypeStruct((8, w), F32), jax.ShapeDtypeStruct((1, w), F32)],
        scratch_shapes=[pltpu.VMEM((CONV_HALO, w), F32)], name=name,
        compiler_params=_cparams("arbitrary"))(dx2, gate, f, hid, hid, up_t, down, conv_w, conv_b)


def _axes():
    return lax.axis_index("x"), lax.axis_index("y"), lax.axis_index("c")


def _handshake(peers):
    barrier = pltpu.get_barrier_semaphore()
    for peer in peers:
        pl.semaphore_signal(barrier, inc=1, device_id=peer, device_id_type=MESH)
    pl.semaphore_wait(barrier, len(peers))


def _allgather_body(x_refs, out_refs, send_sems, recv_sems, local_sems, own_barrier):
    n = len(x_refs)
    x, y, c = _axes()
    me, sibling = (x, y, c), (x, y, 1 - c)
    chips = [(1 - x, y), (x, 1 - y), (1 - x, 1 - y)]
    if own_barrier:
        _handshake([sibling] + [(*chip, c) for chip in chips])

    def slot(a, px, py, pc):
        return out_refs[a].at[4 * px + 2 * py + pc]

    def copy(a, k, block, to, src=None):
        return pltpu.make_async_remote_copy(
            src_ref=slot(a, *block) if src is None else src, dst_ref=slot(a, *block),
            send_sem=send_sems.at[a, k], recv_sem=recv_sems.at[a, k], device_id=to, device_id_type=MESH)

    mines, firsts = [], []
    for a in range(n):
        mines.append(pltpu.make_async_copy(x_refs[a], slot(a, *me), local_sems.at[a]))
        mines[-1].start()
        first = [copy(a, 0, me, sibling, src=x_refs[a])]
        first += [copy(a, 1 + j, me, (*chip, c), src=x_refs[a]) for j, chip in enumerate(chips)]
        for cp in first:
            cp.start()
        firsts += first
    passed = []
    for j, chip in enumerate(chips):
        for a in range(n):
            copy(a, 1 + j, (*chip, c), me).wait_recv()
            passed.append(copy(a, 4 + j, (*chip, c), sibling))
            passed[-1].start()
    for a in range(n):
        copy(a, 0, sibling, me).wait_recv()
    for j, chip in enumerate(chips):
        for a in range(n):
            copy(a, 4 + j, (*chip, 1 - c), me).wait_recv()
    for cp in firsts + passed:
        cp.wait_send()
    for cp in mines:
        cp.wait()


def _allgather_sems(n):
    return [pltpu.SemaphoreType.DMA((n, 7)), pltpu.SemaphoreType.DMA((n, 7)), pltpu.SemaphoreType.DMA((n,))]


def _allgather(xs, *, name):
    n = len(xs)

    def body(*refs):
        _allgather_body(refs[:n], refs[n:2 * n], *refs[2 * n:], own_barrier=False)

    return pl.pallas_call(
        body, out_shape=[jax.ShapeDtypeStruct((N_DEV,) + xb.shape, xb.dtype) for xb in xs],
        in_specs=[_HBM] * n, out_specs=[_HBM] * n, scratch_shapes=_allgather_sems(n), name=name)(*xs)


def _allgather_async(xs, *, name, collective_id):
    n = len(xs)
    x_refs = [jax.new_ref(xb, memory_space=pltpu.MemorySpace.HBM) for xb in xs]
    out_refs = [jax.empty_ref(jax.ShapeDtypeStruct((N_DEV,) + xb.shape, xb.dtype), memory_space=pltpu.MemorySpace.HBM)
                for xb in xs]

    @pl.kernel(mesh=plsc.ScalarSubcoreMesh(axis_name="sequencer", num_cores=1), name=name,
               scratch_types=tuple(_allgather_sems(n)),
               compiler_params=pltpu.CompilerParams(collective_id=collective_id))
    def launch(send_sems, recv_sems, local_sems):
        _allgather_body(x_refs, out_refs, send_sems, recv_sems, local_sems, own_barrier=True)

    launch()
    return [r[...] for r in out_refs]


def _pair_exchange(blocks, *, name, collective_id):
    n = len(blocks)
    hbm = pltpu.MemorySpace.HBM
    in_refs = [jax.new_ref(b, memory_space=hbm) for b in blocks]
    out_refs = [jax.empty_ref(jax.ShapeDtypeStruct((4,) + b.shape[1:], b.dtype), memory_space=hbm) for b in blocks]

    @pl.kernel(mesh=plsc.ScalarSubcoreMesh(axis_name="sequencer", num_cores=1), name=name,
               scratch_types=(pltpu.SemaphoreType.DMA((n, 4)), pltpu.SemaphoreType.DMA((n, 4))),
               compiler_params=pltpu.CompilerParams(collective_id=collective_id))
    def launch(send_sems, recv_sems):
        x, y, c = _axes()
        _handshake([(x, y, 1 - c)])
        copies = [pltpu.make_async_remote_copy(
            src_ref=in_refs[a].at[2 * s + (1 - c)], dst_ref=out_refs[a].at[s], send_sem=send_sems.at[a, s],
            recv_sem=recv_sems.at[a, s], device_id=(x, y, 1 - c), device_id_type=MESH)
            for a in range(n) for s in range(4)]
        for cp in copies:
            cp.start()
        for cp in copies:
            cp.wait_recv()
        for cp in copies:
            cp.wait_send()

    launch()
    return [r[...] for r in out_refs]


def _chip_exchange(parts, *, name, collective_id):
    n = len(parts)
    hbm = pltpu.MemorySpace.HBM
    in_refs = [jax.new_ref(p, memory_space=hbm) for p in parts]
    out_refs = [jax.empty_ref(jax.ShapeDtypeStruct(p.shape, p.dtype), memory_space=hbm) for p in parts]

    @pl.kernel(mesh=plsc.ScalarSubcoreMesh(axis_name="sequencer", num_cores=1), name=name,
               scratch_types=(pltpu.SemaphoreType.DMA((n, 3)), pltpu.SemaphoreType.DMA((n, 3)),
                              pltpu.SemaphoreType.DMA((n,))),
               compiler_params=pltpu.CompilerParams(collective_id=collective_id))
    def launch(send_sems, recv_sems, local_sems):
        x, y, c = _axes()
        my_chip = 2 * x + y
        chips = [(1 - x, y), (x, 1 - y), (1 - x, 1 - y)]
        _handshake([(*chip, c) for chip in chips])
        locals_ = [pltpu.make_async_copy(in_refs[a].at[my_chip], out_refs[a].at[my_chip], local_sems.at[a])
                   for a in range(n)]
        for cp in locals_:
            cp.start()
        copies = [pltpu.make_async_remote_copy(
            src_ref=in_refs[a].at[2 * px + py], dst_ref=out_refs[a].at[my_chip], send_sem=send_sems.at[a, k],
            recv_sem=recv_sems.at[a, k], device_id=(px, py, c), device_id_type=MESH)
            for a in range(n) for k, (px, py) in enumerate(chips)]
        for cp in copies:
            cp.start()
        for cp in copies:
            cp.wait_recv()
        for cp in copies:
            cp.wait_send()
        for cp in locals_:
            cp.wait()

    launch()
    return [r[...] for r in out_refs]


def _pair_sum(core, blocks, from_sibling, *, name):
    n = len(blocks)

    def body(core_ref, *refs):
        for a_ref, b_ref, o_ref in zip(refs[:n], refs[n:2 * n], refs[2 * n:]):
            o_ref[...] = (a_ref[...].astype(F32) + b_ref[...].astype(F32)).astype(o_ref.dtype)

    mine = lambda b: pl.BlockSpec((1,) + b.shape[1:], lambda s, core_ref: (2 * s + core_ref[0], 0, 0))
    slot = lambda b: pl.BlockSpec((1,) + b.shape[1:], lambda s, core_ref: (s, 0, 0))
    return pl.pallas_call(
        body,
        grid_spec=pltpu.PrefetchScalarGridSpec(
            num_scalar_prefetch=1, grid=(4,),
            in_specs=[mine(b) for b in blocks] + [slot(b) for b in blocks], out_specs=[slot(b) for b in blocks]),
        out_shape=[jax.ShapeDtypeStruct(s.shape, s.dtype) for s in from_sibling], name=name,
        compiler_params=_cparams("parallel"))(core, *blocks, *from_sibling)


def _sum_blocks(a, *, name, tr=None):
    n, r, cdim = a.shape
    tr = tr or r

    def body(a_ref, o_ref):
        acc = a_ref[0].astype(F32)
        for k in range(1, n):
            acc = acc + a_ref[k].astype(F32)
        o_ref[...] = acc

    return pl.pallas_call(body, grid=(r // tr,), in_specs=[pl.BlockSpec((n, tr, cdim), lambda i: (0, i, 0))],
                          out_specs=pl.BlockSpec((tr, cdim), lambda i: (i, 0)),
                          out_shape=jax.ShapeDtypeStruct((r, cdim), F32), name=name,
                          compiler_params=_cparams("parallel"))(a)


def _sum_gathered(gathered, *, name):
    n = len(gathered)

    def body(*refs):
        for a_ref, o_ref in zip(refs[:n], refs[n:]):
            acc = a_ref[0]
            for k in range(1, N_DEV):
                acc = acc + a_ref[k]
            o_ref[...] = acc

    return pl.pallas_call(body, out_shape=[jax.ShapeDtypeStruct(g.shape[1:], F32) for g in gathered], name=name,
                          compiler_params=pltpu.CompilerParams(vmem_limit_bytes=VMEM_LIMIT))(*gathered)


_ADA_SHARD = 6 * D_MODEL // N_DEV


def _ada_mod(c_all, ada_w, *, name):
    def body(c_ref, w_ref, o_ref):
        o_ref[0] = jnp.dot(_silu(c_ref[...]).astype(BF16), w_ref[0].astype(BF16), preferred_element_type=F32)

    return pl.pallas_call(
        body, grid=(DEPTH,),
        in_specs=[pl.BlockSpec((N_DEV, D_MODEL), lambda l: (0, 0)),
                  pl.BlockSpec((1, D_MODEL, _ADA_SHARD), lambda l: (l, 0, 0))],
        out_specs=pl.BlockSpec((1, N_DEV, _ADA_SHARD), lambda l: (l, 0, 0)),
        out_shape=jax.ShapeDtypeStruct((DEPTH, N_DEV, _ADA_SHARD), F32), name=name,
        compiler_params=_cparams("parallel"))(c_all, ada_w)


def _ada_wgrad(c_all, dmod_cols, *, name):
    def body(c_ref, d_ref, o_ref):
        o_ref[0] = lax.dot_general(_silu(c_ref[...]), d_ref[0], (((0,), (0,)), ((), ())),
                                   preferred_element_type=F32, precision=lax.Precision.HIGHEST)

    return pl.pallas_call(
        body, grid=(DEPTH,),
        in_specs=[pl.BlockSpec((N_DEV, D_MODEL), lambda l: (0, 0)),
                  pl.BlockSpec((1, N_DEV, _ADA_SHARD), lambda l: (l, 0, 0))],
        out_specs=pl.BlockSpec((1, D_MODEL, _ADA_SHARD), lambda l: (l, 0, 0)),
        out_shape=jax.ShapeDtypeStruct((DEPTH, D_MODEL, _ADA_SHARD), F32), name=name,
        compiler_params=_cparams("parallel"))(c_all, dmod_cols)


def _add_rows(a, b, *, name):
    def body(a_ref, b_ref, o_ref):
        o_ref[...] = a_ref[...] + b_ref[...]

    return pl.pallas_call(body, out_shape=jax.ShapeDtypeStruct(a.shape, a.dtype), name=name)(a, b)


def _adamw_update(w_ref, g_ref, m_ref, v_ref, d_ref, mo_ref, vo_ref):
    gv = g_ref[...]
    mn = ADAM_B1 * m_ref[...] + (1.0 - ADAM_B1) * gv
    vn = ADAM_B2 * v_ref[...] + (1.0 - ADAM_B2) * (gv * gv)
    mo_ref[...] = mn
    vo_ref[...] = vn
    m_hat = mn / (1.0 - ADAM_B1 ** ADAM_STEP)
    v_hat = vn / (1.0 - ADAM_B2 ** ADAM_STEP)
    d_ref[...] = -ADAM_LR * (m_hat / (jnp.sqrt(v_hat) + ADAM_EPS) + ADAM_WD * w_ref[...])


def _adamw_small(ws, gs, ms, vs, *, name):
    n = len(ws)

    def body(*refs):
        ins, outs = refs[:4 * n], refs[4 * n:]
        for i in range(n):
            _adamw_update(ins[i], ins[n + i], ins[2 * n + i], ins[3 * n + i], outs[i], outs[n + i], outs[2 * n + i])

    shapes = [jax.ShapeDtypeStruct(a.shape, F32) for a in ws]
    outs = pl.pallas_call(body, out_shape=shapes * 3, name=name,
                          compiler_params=pltpu.CompilerParams(vmem_limit_bytes=VMEM_LIMIT))(*ws, *gs, *ms, *vs)
    return outs[:n], outs[n:2 * n], outs[2 * n:]


def _adamw_stack(w, g, m, v, *, name, tr):
    layers, r, cdim = w.shape
    body = functools.partial(_adamw_update)
    spec = pl.BlockSpec((1, tr, cdim), lambda l, i: (l, i, 0))
    shp = jax.ShapeDtypeStruct(w.shape, F32)
    return pl.pallas_call(body, grid=(layers, r // tr), in_specs=[spec] * 4, out_specs=[spec] * 3,
                          out_shape=[shp] * 3, name=name, compiler_params=_cparams("parallel", "parallel"))(w, g, m, v)


def _adamw(w, g, m, v, *, name, tr):
    r, cdim = w.shape
    body = functools.partial(_adamw_update)

    spec = pl.BlockSpec((tr, cdim), lambda i: (i, 0))
    shp = jax.ShapeDtypeStruct((r, cdim), F32)
    return pl.pallas_call(body, grid=(r // tr,), in_specs=[spec] * 4, out_specs=[spec] * 3, out_shape=[shp] * 3,
                          name=name, compiler_params=_cparams("parallel"))(w, g, m, v)


def _pad_rows(a, rows):
    return jnp.concatenate([a, jnp.zeros((rows - a.shape[0],) + a.shape[1:], a.dtype)], axis=0)


def _pad_lanes(a, lanes):
    return jnp.concatenate([a, jnp.zeros(a.shape[:-1] + (lanes - a.shape[-1],), a.dtype)], axis=-1)


def _permute_w_in(wt):
    return jnp.concatenate([wt[512:1536], wt[:512], wt[1536:1544],
                            jnp.zeros((PROJ_W - IN_W, wt.shape[1]), wt.dtype), wt[1544:]], axis=0)


def _unpermute_w_in(wp):
    return jnp.concatenate([wp[1024:1536], wp[:1024], wp[1536:1544], wp[PROJ_SSD_W:]], axis=0)


def _block_diag(w):
    rows = []
    for g in range(4):
        rows.append(jnp.concatenate([w[g] if k == g else jnp.zeros_like(w[g]) for k in range(4)], axis=1))
    return jnp.concatenate(rows, axis=0)


def _diag_blocks(wbd):
    return jnp.stack([wbd[64 * g:64 * (g + 1), 64 * g:64 * (g + 1)] for g in range(4)], axis=0)


def _layer_params(l, small):
    return dict(
        norm1_g=small["norm1_g"][l][None], norm2_g=small["norm2_g"][l][None],
        conv_w=_pad_rows(small["ssd_conv_w"][l], 8), conv_b=small["ssd_conv_b"][l][None],
        dt_bias=_pad_lanes(small["ssd_dt_bias"][l][None], 128), a_log=_pad_lanes(small["ssd_a_log"][l][None], 128),
        d_skip=_pad_lanes(small["ssd_d"][l][None], 128), ssd_norm_g=small["ssd_norm_g"][l][None],
        pool_bd=_block_diag(small["pool_w"][l]), pool_scale=small["pool_scale"][l][None],
        fcw=_pad_rows(small["ffn_conv_w"][l], 8), fcb=small["ffn_conv_b"][l][None])


def _mod_rows(mod_l):
    return [mod_l[None, D_MODEL * i:D_MODEL * (i + 1)] for i in range(6)]


def _layer_fwd(x, mod_l, p, tabs, l, gather):
    sh1, sc1, g1, sh2, sc2, g2 = _mod_rows(mod_l)
    mix_w = gather(l, "mix", None)
    p.update(w_in=mix_w["w_in"], w_out=mix_w["w_out"])
    proj, h1 = _mm(x, p["w_in"], nt=True, norm=(p["norm1_g"], sc1, sh1), name=f"l{l}_proj")
    ffn_w = gather(l, "ffn", proj)
    p.update(up=ffn_w["ffn_up"], down=ffn_w["ffn_down"])
    mix, hst = _ssd_fwd(proj, p["conv_w"], p["conv_b"], p["dt_bias"], p["a_log"], p["d_skip"], p["ssd_norm_g"],
                        name=f"l{l}_ssd")
    mix = _pool_fwd(proj, p["pool_bd"], p["pool_scale"], mix, name=f"l{l}_pool")
    y_att, lse, mix = _att_fwd(proj, *tabs, mix, name=f"l{l}_att")
    mo, x1 = _mm(mix, p["w_out"], residual=(x, g1), name=f"l{l}_out")
    gather(l + 1, "mix", (x1, p["up"]))
    h2, hid, act, f, x2 = _ffn_fwd(x1, p["norm2_g"], sc2, sh2, g2, p["up"], p["down"], p["fcw"], p["fcb"],
                                   name=f"l{l}_ffn")
    return x2, dict(x=x, h1=h1, proj=proj, hst=hst, y_att=y_att, lse=lse, mix=mix, mo=mo, x1=x1, h2=h2, hid=hid,
                    act=act, f=f)


def _layer_bwd(dx2, sv, mod_l, p, tabs, l, exchange):
    sh1, sc1, g1, sh2, sc2, g2 = _mod_rows(mod_l)
    df, dg2, dhid, dh2, dfcw, dfcb = _ffn_bwd(dx2, g2, sv["f"], sv["hid"], p["up"], p["down"], p["fcw"], p["fcb"],
                                              name=f"l{l}_ffn_b")
    d_down = _wgrad(sv["act"], df, tk=1408, name=f"l{l}_down_bw")
    d_up = _wgrad(dhid, sv["h2"], tk=1408, name=f"l{l}_up_bw")
    finish_ffn = exchange(l, "ffn", dict(ffn_up=d_up, ffn_down=d_down))
    dx1, dn2, dsc2, dsh2, dmix, dmo, dg1 = _mid_bwd(sv["x1"], dh2, dx2, p["norm2_g"], sc2, g1, sv["mo"], p["w_out"],
                                                    name=f"l{l}_mid_b")
    finish_ffn(dx1)
    d_wout = _wgrad(sv["mix"], dmo, name=f"l{l}_out_bw")
    dproj, dcw, dcb, ddb, dal, dd, dng = _ssd_bwd(
        sv["proj"], sv["hst"], dmix, p["conv_w"], p["conv_b"], p["dt_bias"], p["a_log"], p["d_skip"],
        p["ssd_norm_g"], name=f"l{l}_ssd_b")
    dproj, dwbd, dpsc = _pool_bwd(sv["proj"], dmix, p["pool_bd"], p["pool_scale"], dproj, name=f"l{l}_pool_b")
    dproj = _att_bwd(sv["proj"], *tabs, sv["y_att"], sv["lse"], dmix, dproj, name=f"l{l}_att_b")
    d_win = _wgrad(dproj, sv["h1"], tk=1408, name=f"l{l}_proj_bw")
    finish_mix = exchange(l, "mix", dict(w_in=d_win, w_out=d_wout))
    dx0, dn1, dsc1, dsh1 = _norm_mod_bwd(sv["x"], dproj, dx1, p["norm1_g"], sc1, w=p["w_in"], name=f"l{l}_proj_b")
    finish_mix(dx0)
    dx0, _ = lax.optimization_barrier((dx0, (d_win, d_wout, d_up, d_down)))
    dmod = jnp.concatenate([dsh1, dsc1, dg1, dsh2, dsc2, dg2], axis=1)[0]
    small = dict(norm1_g=dn1[0], ssd_conv_w=dcw[:SSD_CONV_K], ssd_conv_b=dcb[0], ssd_dt_bias=ddb[0], ssd_a_log=dal[0],
                 ssd_d=dd[0], ssd_norm_g=dng[0], pool_w=_diag_blocks(dwbd), pool_scale=dpsc[0], norm2_g=dn2[0],
                 ffn_conv_w=dfcw[:FFN_CONV_K], ffn_conv_b=dfcb[0])
    return dx0, dmod, small


def _example_step(x, target, pos_col, inv_freq_lane, mod, gather, small, final_g, exchange):
    tabs = _rope_tables(pos_col, inv_freq_lane, name="rope_tables")
    params, saved = [], []
    for l in range(DEPTH):
        params.append(_layer_params(l, small))
        x, sv = _layer_fwd(x, mod[l], params[l], tabs, l, gather)
        saved.append(sv)
    loss_row, dx, dfg = _final_loss(x, final_g[None], target, name="final_loss")
    dmods, smalls = [None] * DEPTH, [None] * DEPTH
    for l in reversed(range(DEPTH)):
        dx, dmods[l], smalls[l] = _layer_bwd(dx, saved[l], mod[l], params[l], tabs, l, exchange)
    return loss_row, dx, jnp.stack(dmods, axis=0), smalls, dfg[0]


_BIG = ("w_in", "w_out", "ffn_up", "ffn_down")
_SMALL_GRADS = ("norm1_g", "ssd_conv_w", "ssd_conv_b", "ssd_dt_bias", "ssd_a_log", "ssd_d", "ssd_norm_g", "pool_w",
                "pool_scale", "norm2_g", "ffn_conv_w", "ffn_conv_b")
_SMALL_PARAMS = ("ada_b", "norm1_g", "ssd_conv_w", "ssd_conv_b", "ssd_dt_bias", "ssd_a_log", "ssd_d", "ssd_norm_g",
                 "pool_w", "pool_scale", "norm2_g", "ffn_conv_w", "ffn_conv_b", "final_g")
_WEIGHT_ORDER = ("ada_w", "ada_b", "norm1_g", "w_in", "ssd_conv_w", "ssd_conv_b", "ssd_dt_bias", "ssd_a_log", "ssd_d",
                 "ssd_norm_g", "pool_w", "pool_scale", "w_out", "norm2_g", "ffn_up", "ffn_conv_w", "ffn_conv_b",
                 "ffn_down", "final_g")


_COLUMN_SHARDED = ("w_in", "ffn_up")
_GROUPS = (("mix", ("w_in", "w_out")), ("ffn", ("ffn_up", "ffn_down")))


def _big_shares(w, l, names):
    return [(w[name][l].T if name in _COLUMN_SHARDED else w[name][l]).astype(BF16) for name in names]


def _unshard_big(names, gathered):
    out = {}
    for name, g in zip(names, gathered):
        full = g.reshape(N_DEV * g.shape[1], g.shape[2])
        out[name] = _permute_w_in(full) if name == "w_in" else full
    return out


def _shard_big(grads):
    out = []
    for name, g in grads.items():
        g = _unpermute_w_in(g) if name == "w_in" else g
        out.append(g.reshape(N_DEV, g.shape[0] // N_DEV, g.shape[1]))
    return out


def kernel(x, c, positions, ada_w, ada_b, norm1_g, w_in, ssd_conv_w, ssd_conv_b, ssd_dt_bias, ssd_a_log, ssd_d, ssd_norm_g, pool_w, pool_scale, w_out, norm2_g, ffn_up, ffn_conv_w, ffn_conv_b, ffn_down, final_g, loss_target, m_ada_w, m_ada_b, m_norm1_g, m_w_in, m_ssd_conv_w, m_ssd_conv_b, m_ssd_dt_bias, m_ssd_a_log, m_ssd_d, m_ssd_norm_g, m_pool_w, m_pool_scale, m_w_out, m_norm2_g, m_ffn_up, m_ffn_conv_w, m_ffn_conv_b, m_ffn_down, m_final_g, v_ada_w, v_ada_b, v_norm1_g, v_w_in, v_ssd_conv_w, v_ssd_conv_b, v_ssd_dt_bias, v_ssd_a_log, v_ssd_d, v_ssd_norm_g, v_pool_w, v_pool_scale, v_w_out, v_norm2_g, v_ffn_up, v_ffn_conv_w, v_ffn_conv_b, v_ffn_down, v_final_g):
    w = dict(ada_w=ada_w, ada_b=ada_b, norm1_g=norm1_g, w_in=w_in, ssd_conv_w=ssd_conv_w, ssd_conv_b=ssd_conv_b,
             ssd_dt_bias=ssd_dt_bias, ssd_a_log=ssd_a_log, ssd_d=ssd_d, ssd_norm_g=ssd_norm_g, pool_w=pool_w,
             pool_scale=pool_scale, w_out=w_out, norm2_g=norm2_g, ffn_up=ffn_up, ffn_conv_w=ffn_conv_w,
             ffn_conv_b=ffn_conv_b, ffn_down=ffn_down, final_g=final_g)
    m = dict(ada_w=m_ada_w, ada_b=m_ada_b, norm1_g=m_norm1_g, w_in=m_w_in, ssd_conv_w=m_ssd_conv_w,
             ssd_conv_b=m_ssd_conv_b, ssd_dt_bias=m_ssd_dt_bias, ssd_a_log=m_ssd_a_log, ssd_d=m_ssd_d,
             ssd_norm_g=m_ssd_norm_g, pool_w=m_pool_w, pool_scale=m_pool_scale, w_out=m_w_out, norm2_g=m_norm2_g,
             ffn_up=m_ffn_up, ffn_conv_w=m_ffn_conv_w, ffn_conv_b=m_ffn_conv_b, ffn_down=m_ffn_down,
             final_g=m_final_g)
    v = dict(ada_w=v_ada_w, ada_b=v_ada_b, norm1_g=v_norm1_g, w_in=v_w_in, ssd_conv_w=v_ssd_conv_w,
             ssd_conv_b=v_ssd_conv_b, ssd_dt_bias=v_ssd_dt_bias, ssd_a_log=v_ssd_a_log, ssd_d=v_ssd_d,
             ssd_norm_g=v_ssd_norm_g, pool_w=v_pool_w, pool_scale=v_pool_scale, w_out=v_w_out, norm2_g=v_norm2_g,
             ffn_up=v_ffn_up, ffn_conv_w=v_ffn_conv_w, ffn_conv_b=v_ffn_conv_b, ffn_down=v_ffn_down,
             final_g=v_final_g)
    ix, iy, ic = _axes()
    dev = 4 * ix + 2 * iy + ic

    c_all, scw, fcw = _allgather([c, ssd_conv_w.reshape(DEPTH * SSD_CONV_K, -1),
                                  ffn_conv_w.reshape(DEPTH * FFN_CONV_K, -1)], name="gather_small")
    small_all = c_all
    c_all = c_all.reshape(N_DEV, D_MODEL)
    scw = scw.reshape(N_DEV, DEPTH, SSD_CONV_K, -1).transpose(1, 2, 0, 3).reshape(DEPTH, SSD_CONV_K, SSD_CONV_CH)
    fcw = fcw.reshape(N_DEV, DEPTH, FFN_CONV_K, -1).transpose(1, 2, 0, 3).reshape(DEPTH, FFN_CONV_K, 2 * FFN_DIM)

    mod_cols = _ada_mod(c_all, ada_w, name="ada_mod")
    mod_all = _allgather([mod_cols.reshape(DEPTH * N_DEV, _ADA_SHARD)], name="gather_mod")[0]
    mod_all = mod_all.reshape(N_DEV, DEPTH, N_DEV, _ADA_SHARD)
    mod_mine = lax.dynamic_index_in_dim(mod_all, dev, axis=2, keepdims=False)
    mod = _add_rows(mod_mine.transpose(1, 0, 2).reshape(DEPTH, 6 * D_MODEL), ada_b, name="ada_bias")

    fetched = {}

    def gather(l, group, after):
        if l < DEPTH and (l, group) not in fetched:
            names = dict(_GROUPS)[group]
            shares, _ = lax.optimization_barrier((_big_shares(w, l, names), small_all if after is None else after))
            got = _allgather_async(shares, name=f"gather_weights_l{l}_{group}",
                                   collective_id=1 + 2 * l + (group == "ffn"))
            fetched[l, group] = _unshard_big(names, got)
        return fetched.get((l, group))

    core = ic.astype(jnp.int32).reshape(1)
    from_chips = {}

    def exchange(l, group, g):
        cid = 5 + 4 * l + 2 * (group == "mix")
        blocks = _shard_big(g)
        if from_chips:
            blocks, _ = lax.optimization_barrier((blocks, list(from_chips.values())))
        from_sibling = _pair_exchange(blocks, name=f"grads_pair_exchange_l{l}_{group}", collective_id=cid)

        def finish(after):
            theirs, _ = lax.optimization_barrier((from_sibling, after))
            parts = _pair_sum(core, blocks, theirs, name=f"grads_pair_sum_l{l}_{group}")
            got = _chip_exchange(parts, name=f"grads_chip_exchange_l{l}_{group}", collective_id=cid + 1)
            from_chips.update({(l, name): t for name, t in zip(g, got)})

        return finish

    small = dict(norm1_g=norm1_g, norm2_g=norm2_g, ssd_conv_w=scw, ssd_conv_b=ssd_conv_b, ssd_dt_bias=ssd_dt_bias,
                 ssd_a_log=ssd_a_log, ssd_d=ssd_d, ssd_norm_g=ssd_norm_g, pool_w=pool_w, pool_scale=pool_scale,
                 ffn_conv_w=fcw, ffn_conv_b=ffn_conv_b)

    inv_freq = ROPE_THETA ** (-jnp.arange(0, ROT_DIM, 2, dtype=F32) / ROT_DIM)
    lane = jnp.arange(128) % HEAD_LANES
    inv_freq_lane = jnp.where(lane < ROT_DIM, inv_freq[lane % (ROT_DIM // 2)], 0.0)[None, :]
    pos_col = positions.reshape(SEQ, 1).astype(F32)
    loss_row, dx, dmod, g_small, g_final = _example_step(
        x[0], loss_target[0], pos_col, inv_freq_lane, mod, gather, small, final_g, exchange)

    grads = {}
    summed = {}
    for name in _BIG:
        summed[name] = jnp.stack([_sum_blocks(from_chips[l, name], name=f"grads_chip_sum_l{l}_{name}")
                                  for l in range(DEPTH)], axis=0)
        grads[name] = summed[name].transpose(0, 2, 1) if name in _COLUMN_SHARDED else summed[name]

    small_names = list(_SMALL_GRADS)
    stacked = [jnp.stack([g_small[l][name] for l in range(DEPTH)], axis=0) for name in small_names]
    small_parts = [loss_row, dmod] + [s.reshape(-1, s.shape[-1]) for s in stacked] + [g_final[None]]
    gathered = _allgather_async(small_parts, name="gather_small_grads", collective_id=13)
    total = _sum_gathered(gathered, name="sum_small_grads")
    loss = total[0][0, 0]
    grads["ada_b"] = total[1]
    grads.update(zip(small_names, total[2:-1]))
    grads["final_g"] = total[-1][0]
    dmod_cols = lax.dynamic_slice_in_dim(gathered[1], dev * _ADA_SHARD, _ADA_SHARD, axis=2).transpose(1, 0, 2)
    grads["ada_w"] = _ada_wgrad(c_all, dmod_cols, name="ada_wgrad")
    for name in ("ssd_dt_bias", "ssd_a_log", "ssd_d"):
        grads[name] = grads[name][:, :SSD_HEADS]
    grads["pool_w"] = grads["pool_w"].reshape(pool_w.shape)
    grads["ssd_conv_w"] = lax.dynamic_slice_in_dim(
        grads["ssd_conv_w"].reshape(DEPTH, SSD_CONV_K, SSD_CONV_CH), dev * ssd_conv_w.shape[2], ssd_conv_w.shape[2], axis=2)
    grads["ffn_conv_w"] = lax.dynamic_slice_in_dim(
        grads["ffn_conv_w"].reshape(DEPTH, FFN_CONV_K, 2 * FFN_DIM), dev * ffn_conv_w.shape[2], ffn_conv_w.shape[2], axis=2)

    delta, new_m, new_v = {}, {}, {}
    for name, tr in (("ada_w", 512), ("w_out", 256), ("ffn_down", 352)):
        shp = w[name].shape
        two_d = lambda a: a.reshape(shp[0] * shp[1], shp[2])
        d_, m_, v_ = _adamw(two_d(w[name]), two_d(grads[name]), two_d(m[name]), two_d(v[name]), tr=tr,
                               name=f"adamw_{name}")
        delta[name], new_m[name], new_v[name] = (t.reshape(shp) for t in (d_, m_, v_))
    for name, tr in (("w_in", 321), ("ffn_up", 352)):
        flip = lambda a: a.transpose(0, 2, 1)
        outs = _adamw_stack(flip(w[name]), summed[name], flip(m[name]), flip(v[name]), tr=tr, name=f"adamw_{name}")
        delta[name], new_m[name], new_v[name] = (flip(t) for t in outs)
    two_d = lambda a: a.reshape(-1, a.shape[-1])
    outs = _adamw_small(*[[two_d(t[name]) for name in _SMALL_PARAMS] for t in (w, grads, m, v)], name="adamw_small")
    for name, d_, m_, v_ in zip(_SMALL_PARAMS, *outs):
        delta[name], new_m[name], new_v[name] = (t.reshape(w[name].shape) for t in (d_, m_, v_))

    grad_x = dx[None]
    return (loss, grad_x, *[grads[n].reshape(w[n].shape) for n in _WEIGHT_ORDER],
            *[delta[n] for n in _WEIGHT_ORDER], *[new_m[n] for n in _WEIGHT_ORDER],
            *[new_v[n] for n in _WEIGHT_ORDER])
```

```python
import functools
import math

import jax
import jax.numpy as jnp
from jax import lax
from jax.experimental import pallas as pl
from jax.experimental.pallas import tpu as pltpu
from jax.experimental.pallas import tpu_sc as plsc

F32 = jnp.float32
BF16 = jnp.bfloat16

N_DEV = 8
D_MODEL = 1024
SEQ = 4096
DEPTH = 2
SSD_INNER = 512
SSD_HEADS = 8
SSD_HPG = 4
SSD_STATE = 128
SSD_CHUNK = 256
SSD_CONV_K = 4
SSD_CONV_CH = 1024
POOL_W = 256
POOL_WINDOWS = (2, 4, 8, 16)
ATT_W = 256
ATT_PATTERNS = ((128, 1), (512, 4), (2048, 16))
ATT_BLOCK = 128
ROT_DIM = 16
ROPE_THETA = 500000.0
IN_W = 2568
FFN_DIM = 2816
FFN_CONV_K = 3
NORM_EPS = 1e-6
HEAD_LANES = 64

ADAM_LR = 0.001
ADAM_B1 = 0.9
ADAM_B2 = 0.999
ADAM_EPS = 1e-08
ADAM_WD = 0.01
ADAM_STEP = 10

PROJ_W = 2816
PROJ_SSD_W = 1792
PROJ_Z_BLK = 2
PROJ_DT_BLK = 12
PROJ_POOL_BLK = 7
PROJ_Q_BLK, PROJ_K_BLK, PROJ_V_BLK = 16, 18, 20
MIX_POOL_BLK = 2
MIX_ATT_BLK = 6
VMEM_LIMIT = 56 * 1024 * 1024
ROW_TILE = 512
CONV_HALO = 8
POOL_HALO = 16
ATT_KPAD = ATT_BLOCK * 16
MESH = pl.DeviceIdType.MESH
_HBM = pl.BlockSpec(memory_space=pl.ANY)


def _cparams(*sem):
    return pltpu.CompilerParams(dimension_semantics=sem, vmem_limit_bytes=VMEM_LIMIT)


def _resident(shape):
    return pl.BlockSpec(shape, lambda i: (0,) * len(shape), pipeline_mode=pl.Buffered(1))


def _silu(x):
    return x * jax.nn.sigmoid(x)


def _pick_lane(v, h):
    lane = lax.broadcasted_iota(jnp.int32, v.shape, 1)
    return jnp.sum(jnp.where(lane == h, v, 0.0), axis=1, keepdims=True)


def _pick_row(v, h):
    row = lax.broadcasted_iota(jnp.int32, v.shape, 0)
    return jnp.sum(jnp.where(row == h, v, 0.0), axis=0, keepdims=True)


def _head_of_lane(width):
    return lax.broadcasted_iota(jnp.int32, (1, width), 1) // HEAD_LANES


@functools.partial(jax.custom_vjp, nondiff_argnums=(1, 2))
def _shift_rows(x_ext, s, halo):
    y = x_ext if s == 0 else pltpu.roll(x_ext, s, 0)
    return y[halo:]


def _shift_rows_fwd(x_ext, s, halo):
    return _shift_rows(x_ext, s, halo), None


def _shift_rows_bwd(s, halo, _, g):
    ge = jnp.concatenate([jnp.zeros((halo, g.shape[1]), g.dtype), g], axis=0)
    return (ge if s == 0 else pltpu.roll(ge, ge.shape[0] - s, 0),)


_shift_rows.defvjp(_shift_rows_fwd, _shift_rows_bwd)


@functools.partial(jax.custom_vjp, nondiff_argnums=(1,))
def _roll_rows(x, s):
    return pltpu.roll(x, s, 0)


def _roll_rows_fwd(x, s):
    return _roll_rows(x, s), None


def _roll_rows_bwd(s, _, g):
    return (pltpu.roll(g, g.shape[0] - s, 0),)


_roll_rows.defvjp(_roll_rows_fwd, _roll_rows_bwd)


def _rms_modulate(xv, g, sc, sh):
    r = lax.rsqrt(jnp.mean(xv * xv, axis=-1, keepdims=True) + NORM_EPS)
    return (xv * r * g) * (1.0 + sc) + sh


def _mm(a, w, *, name, nt=False, tm=ROW_TILE, tn=None, out_dtype=F32, norm=None, residual=None):
    t, k = a.shape
    n = w.shape[0] if nt else w.shape[1]
    tn = tn or n
    assert tn == n or (norm is None and residual is None)
    extra_in = list(norm or ()) + list(residual or ())

    def body(*refs):
        a_ref, w_ref = refs[:2]
        ins = refs[2:2 + len(extra_in)]
        outs = refs[2 + len(extra_in):]
        if norm is None:
            av = a_ref[...].astype(BF16)
        else:
            av = _rms_modulate(a_ref[...], ins[0][...], ins[1][...], ins[2][...]).astype(BF16)
            outs[1][...] = av
        if nt:
            acc = lax.dot_general(av, w_ref[...], (((1,), (1,)), ((), ())), preferred_element_type=F32)
        else:
            acc = jnp.dot(av, w_ref[...], preferred_element_type=F32)
        outs[0][...] = acc.astype(out_dtype)
        if residual is not None:
            x_ref, gate_ref = ins[-2:]
            outs[-1][...] = x_ref[...] + gate_ref[...] * acc

    row = lambda width: pl.BlockSpec((1, width), lambda i, j: (0, 0))
    tile = lambda width: pl.BlockSpec((tm, width), lambda i, j: (i, 0))
    w_spec = pl.BlockSpec((tn, k), lambda i, j: (j, 0)) if nt else pl.BlockSpec((k, tn), lambda i, j: (0, j))
    in_specs = [tile(k), w_spec] + ([row(k)] * 3 if norm else []) + ([tile(n), row(n)] if residual else [])
    out_specs = [pl.BlockSpec((tm, tn), lambda i, j: (i, j))] + ([tile(k)] if norm else []) + \
        ([tile(n)] if residual else [])
    out_shape = [jax.ShapeDtypeStruct((t, n), out_dtype)] + \
        ([jax.ShapeDtypeStruct((t, k), BF16)] if norm else []) + \
        ([jax.ShapeDtypeStruct((t, n), F32)] if residual else [])
    outs = pl.pallas_call(
        body, grid=(t // tm, n // tn), in_specs=in_specs, out_specs=out_specs, out_shape=out_shape, name=name,
        compiler_params=_cparams("parallel", "parallel"))(a, w, *extra_in)
    return outs[0] if len(outs) == 1 else outs


def _wgrad(a, b, *, name, tk=None, tn=None, tt=2048, out_dtype=BF16):
    t, k = a.shape
    n = b.shape[1]
    tk = tk or k
    tn = tn or n
    steps = t // tt

    def body(a_ref, b_ref, o_ref, acc_ref):
        s = pl.program_id(2)

        @pl.when(s == 0)
        def _():
            acc_ref[...] = jnp.zeros_like(acc_ref)

        acc_ref[...] += lax.dot_general(a_ref[...].astype(BF16), b_ref[...].astype(BF16),
                                        (((0,), (0,)), ((), ())), preferred_element_type=F32)

        @pl.when(s == steps - 1)
        def _():
            o_ref[...] = acc_ref[...].astype(out_dtype)

    return pl.pallas_call(
        body, grid=(k // tk, n // tn, steps),
        in_specs=[pl.BlockSpec((tt, tk), lambda i, j, s: (s, i)), pl.BlockSpec((tt, tn), lambda i, j, s: (s, j))],
        out_specs=pl.BlockSpec((tk, tn), lambda i, j, s: (i, j)),
        out_shape=jax.ShapeDtypeStruct((k, n), out_dtype),
        scratch_shapes=[pltpu.VMEM((tk, tn), F32)], name=name,
        compiler_params=_cparams("parallel", "parallel", "arbitrary"))(a, b)


def _norm_mod_bwd(x, dh, dres, g, sc, *, name, w=None, tm=ROW_TILE):
    s, d = x.shape
    steps = s // tm

    def body(x_ref, dh_ref, dres_ref, g_ref, sc_ref, *rest):
        w_ref = rest[0] if w is not None else None
        dx_ref, dg_ref, dsc_ref, dsh_ref, da_acc, dsh_acc = rest[-6:]
        i = pl.program_id(0)

        @pl.when(i == 0)
        def _():
            da_acc[...] = jnp.zeros_like(da_acc)
            dsh_acc[...] = jnp.zeros_like(dsh_acc)

        xv = x_ref[...]
        if w is None:
            dhv = dh_ref[...].astype(F32)
        else:
            dhv = jnp.dot(dh_ref[...], w_ref[...], preferred_element_type=F32)
        r = lax.rsqrt(jnp.mean(xv * xv, axis=-1, keepdims=True) + NORM_EPS)
        xhat = xv * r
        gain = g_ref[...] * (1.0 + sc_ref[...])
        dxhat = dhv * gain
        dx_ref[...] = dres_ref[...] + r * (dxhat - xhat * jnp.mean(dxhat * xhat, axis=-1, keepdims=True))
        da_acc[...] += jnp.sum(dhv * xhat, axis=0, keepdims=True)
        dsh_acc[...] += jnp.sum(dhv, axis=0, keepdims=True)

        @pl.when(i == steps - 1)
        def _():
            dg_ref[...] = da_acc[...] * (1.0 + sc_ref[...])
            dsc_ref[...] = da_acc[...] * g_ref[...]
            dsh_ref[...] = dsh_acc[...]

    row = pl.BlockSpec((1, d), lambda i: (0, 0))
    tile = pl.BlockSpec((tm, d), lambda i: (i, 0))
    row_shape = jax.ShapeDtypeStruct((1, d), F32)
    dh_spec = tile if w is None else pl.BlockSpec((tm, dh.shape[1]), lambda i: (i, 0))
    return pl.pallas_call(
        body, grid=(steps,), in_specs=[tile, dh_spec, tile, row, row] + ([] if w is None else [_resident(w.shape)]),
        out_specs=[tile, row, row, row],
        out_shape=[jax.ShapeDtypeStruct((s, d), F32), row_shape, row_shape, row_shape],
        scratch_shapes=[pltpu.VMEM((1, d), F32), pltpu.VMEM((1, d), F32)], name=name,
        compiler_params=_cparams("arbitrary"))(x, dh, dres, g, sc, *([] if w is None else [w]))


def _mid_bwd(x1, dh2, dx2, norm_g, sc, gate, mo, w_out, *, name, tm=ROW_TILE):
    s, d = x1.shape
    steps = s // tm

    def body(x_ref, dh_ref, dres_ref, ng_ref, sc_ref, g_ref, mo_ref, w_ref,
             dx_ref, dng_ref, dsc_ref, dsh_ref, dmix_ref, dmo_ref, dg_ref, da_acc, dsh_acc):
        i = pl.program_id(0)

        @pl.when(i == 0)
        def _():
            da_acc[...] = jnp.zeros_like(da_acc)
            dsh_acc[...] = jnp.zeros_like(dsh_acc)
            dg_ref[...] = jnp.zeros_like(dg_ref)

        xv = x_ref[...]
        dhv = dh_ref[...]
        r = lax.rsqrt(jnp.mean(xv * xv, axis=-1, keepdims=True) + NORM_EPS)
        xhat = xv * r
        dxhat = dhv * (ng_ref[...] * (1.0 + sc_ref[...]))
        dxv = dres_ref[...] + r * (dxhat - xhat * jnp.mean(dxhat * xhat, axis=-1, keepdims=True))
        dx_ref[...] = dxv
        da_acc[...] += jnp.sum(dhv * xhat, axis=0, keepdims=True)
        dsh_acc[...] += jnp.sum(dhv, axis=0, keepdims=True)
        dmo = (g_ref[...] * dxv).astype(BF16)
        dmo_ref[...] = dmo
        dg_ref[...] += jnp.sum(dxv * mo_ref[...], axis=0, keepdims=True)
        dmix_ref[...] = lax.dot_general(dmo, w_ref[...], (((1,), (1,)), ((), ())), preferred_element_type=F32)

        @pl.when(i == steps - 1)
        def _():
            dng_ref[...] = da_acc[...] * (1.0 + sc_ref[...])
            dsc_ref[...] = da_acc[...] * ng_ref[...]
            dsh_ref[...] = dsh_acc[...]

    tile = pl.BlockSpec((tm, d), lambda i: (i, 0))
    row = pl.BlockSpec((1, d), lambda i: (0, 0))
    mix_tile = pl.BlockSpec((tm, w_out.shape[0]), lambda i: (i, 0))
    row_shape = jax.ShapeDtypeStruct((1, d), F32)
    return pl.pallas_call(
        body, grid=(steps,), in_specs=[tile, tile, tile, row, row, row, tile, _resident(w_out.shape)],
        out_specs=[tile, row, row, row, mix_tile, tile, row],
        out_shape=[jax.ShapeDtypeStruct((s, d), F32), row_shape, row_shape, row_shape,
                   jax.ShapeDtypeStruct((s, w_out.shape[0]), F32), jax.ShapeDtypeStruct((s, d), BF16), row_shape],
        scratch_shapes=[pltpu.VMEM((1, d), F32), pltpu.VMEM((1, d), F32)], name=name,
        compiler_params=_cparams("arbitrary"))(x1, dh2, dx2, norm_g, sc, gate, mo, w_out)


def _final_loss(x, g, target, *, name, tm=ROW_TILE):
    s, d = x.shape
    steps = s // tm

    def body(x_ref, g_ref, t_ref, loss_ref, dx_ref, dg_ref, sq_acc):
        i = pl.program_id(0)

        @pl.when(i == 0)
        def _():
            sq_acc[...] = jnp.zeros_like(sq_acc)
            dg_ref[...] = jnp.zeros_like(dg_ref)

        xv = x_ref[...]
        r = lax.rsqrt(jnp.mean(xv * xv, axis=-1, keepdims=True) + NORM_EPS)
        xhat = xv * r
        err = xhat * g_ref[...] - t_ref[...]
        sq_acc[...] += jnp.sum(err * err, axis=0, keepdims=True)
        dy = err * (1.0 / d)
        dg_ref[...] += jnp.sum(dy * xhat, axis=0, keepdims=True)
        dxhat = dy * g_ref[...]
        dx_ref[...] = r * (dxhat - xhat * jnp.mean(dxhat * xhat, axis=-1, keepdims=True))

        @pl.when(i == steps - 1)
        def _():
            total = jnp.sum(sq_acc[...], axis=1, keepdims=True) * (0.5 / d)
            loss_ref[...] = jnp.broadcast_to(total, loss_ref.shape)

    tile = pl.BlockSpec((tm, d), lambda i: (i, 0))
    row = pl.BlockSpec((1, d), lambda i: (0, 0))
    return pl.pallas_call(
        body, grid=(steps,), in_specs=[tile, row, tile],
        out_specs=[pl.BlockSpec((1, 128), lambda i: (0, 0)), tile, row],
        out_shape=[jax.ShapeDtypeStruct((1, 128), F32), jax.ShapeDtypeStruct((s, d), F32),
                   jax.ShapeDtypeStruct((1, d), F32)],
        scratch_shapes=[pltpu.VMEM((1, d), F32)], name=name, compiler_params=_cparams("arbitrary"))(x, g, target)


def _ssd_chunk(z, xbc_ext, dt_raw, conv_w, conv_b, dt_bias, a_log, d_skip, norm_g, h_in):
    q = z.shape[0]
    gw = SSD_HPG * HEAD_LANES
    xc = conv_b
    for k in range(SSD_CONV_K):
        xc = xc + _pick_row(conv_w, k) * _shift_rows(xbc_ext, SSD_CONV_K - 1 - k, CONV_HALO)
    xc = _silu(xc)
    dt = jax.nn.softplus(dt_raw + dt_bias)
    da = dt * (-jnp.exp(a_log))
    ri = lax.broadcasted_iota(jnp.int32, (q, q), 0)
    ci = lax.broadcasted_iota(jnp.int32, (q, q), 1)
    causal = ri >= ci
    tril = causal.astype(F32)
    a_cum = jnp.dot(tril, da, preferred_element_type=F32, precision=lax.Precision.HIGHEST)
    a_cum_t = lax.dot_general(da, tril, (((0,), (1,)), ((), ())), preferred_element_type=F32,
                              precision=lax.Precision.HIGHEST)
    a_last = _pick_row(a_cum, q - 1)
    head = _head_of_lane(gw)
    ys, hs = [], []
    for g in range(2):
        xs = xc[:, gw * g:gw * (g + 1)]
        bm = xc[:, SSD_INNER + SSD_STATE * g:SSD_INNER + SSD_STATE * (g + 1)]
        cm = xc[:, SSD_INNER + 2 * SSD_STATE + SSD_STATE * g:SSD_INNER + 2 * SSD_STATE + SSD_STATE * (g + 1)]
        cb = lax.dot_general(cm.astype(BF16), bm.astype(BF16), (((1,), (1,)), ((), ())), preferred_element_type=F32)
        cols = [_pick_lane(a_cum, SSD_HPG * g + j) for j in range(SSD_HPG)]
        lasts = [_pick_lane(a_last, SSD_HPG * g + j) for j in range(SSD_HPG)]
        dt_exp = sum(jnp.where(head == j, _pick_lane(dt, SSD_HPG * g + j), 0.0) for j in range(SSD_HPG))
        d_exp = sum(jnp.where(head == j, _pick_lane(d_skip, SSD_HPG * g + j), 0.0) for j in range(SSD_HPG))
        e_cum = sum(jnp.where(head == j, jnp.exp(cols[j]), 0.0) for j in range(SSD_HPG))
        c_dec = sum(jnp.where(head == j, jnp.exp(lasts[j]), 0.0) for j in range(SSD_HPG))
        xsdt = (xs * dt_exp).astype(BF16)
        y_diag = jnp.zeros((q, gw), F32)
        st_new = jnp.zeros((SSD_STATE, gw), F32)
        for j in range(SSD_HPG):
            row = _pick_row(a_cum_t, SSD_HPG * g + j)
            lmat = jnp.exp(jnp.where(causal, cols[j] - row, -jnp.inf))
            r = jnp.dot((cb * lmat).astype(BF16), xsdt, preferred_element_type=F32)
            y_diag = y_diag + jnp.where(head == j, r, 0.0)
            bd = (bm * jnp.exp(lasts[j] - cols[j])).astype(BF16)
            st = lax.dot_general(bd, xsdt, (((0,), (0,)), ((), ())), preferred_element_type=F32)
            st_new = st_new + jnp.where(head == j, st, 0.0)
        y_off = jnp.dot(cm.astype(BF16), h_in[g].astype(BF16), preferred_element_type=F32) * e_cum
        hs.append(h_in[g] * c_dec + st_new)
        y = y_diag + y_off + d_exp * xs
        yz = y * _silu(z[:, gw * g:gw * (g + 1)])
        yz = yz * lax.rsqrt(jnp.mean(yz * yz, axis=-1, keepdims=True) + NORM_EPS)
        ys.append(yz * norm_g[:, gw * g:gw * (g + 1)])
    return jnp.concatenate(ys, axis=1), tuple(hs)


_SSD_NCHUNK = SEQ // SSD_CHUNK
_HALO_PER_CHUNK = SSD_CHUNK // CONV_HALO


def _ssd_param_specs(const):
    return [pl.BlockSpec((8, SSD_CONV_CH), const), pl.BlockSpec((1, SSD_CONV_CH), const),
            pl.BlockSpec((1, 128), const), pl.BlockSpec((1, 128), const), pl.BlockSpec((1, 128), const),
            pl.BlockSpec((1, SSD_INNER), const)]


def _ssd_fwd(proj, conv_w, conv_b, dt_bias, a_log, d_skip, norm_g, *, name):
    q = SSD_CHUNK

    def body(z_ref, xbc_ref, halo_ref, dt_ref, cw_ref, cb_ref, db_ref, al_ref, d_ref, ng_ref, y_ref, hs_ref, h_acc):
        i = pl.program_id(0)

        @pl.when(i == 0)
        def _():
            h_acc[...] = jnp.zeros_like(h_acc)

        halo = jnp.where(i == 0, 0.0, halo_ref[...])
        xbc_ext = jnp.concatenate([halo, xbc_ref[...]], axis=0)
        h_in = (h_acc[0], h_acc[1])
        hs_ref[0, 0] = h_in[0]
        hs_ref[0, 1] = h_in[1]
        y, h_out = _ssd_chunk(z_ref[...], xbc_ext, dt_ref[...], cw_ref[...], cb_ref[...], db_ref[...], al_ref[...],
                              d_ref[...], ng_ref[...], h_in)
        y_ref[...] = y.astype(BF16)
        h_acc[0] = h_out[0]
        h_acc[1] = h_out[1]

    const = lambda i: (0, 0)
    return pl.pallas_call(
        body, grid=(_SSD_NCHUNK,),
        in_specs=[pl.BlockSpec((q, SSD_INNER), lambda i: (i, PROJ_Z_BLK)),
                  pl.BlockSpec((q, SSD_CONV_CH), lambda i: (i, 0)),
                  pl.BlockSpec((CONV_HALO, SSD_CONV_CH), lambda i: (jnp.maximum(i * _HALO_PER_CHUNK - 1, 0), 0)),
                  pl.BlockSpec((q, 128), lambda i: (i, PROJ_DT_BLK))] + _ssd_param_specs(const),
        out_specs=[pl.BlockSpec((q, SSD_INNER), lambda i: (i, 0)),
                   pl.BlockSpec((1, 2, SSD_STATE, 256), lambda i: (i, 0, 0, 0))],
        out_shape=[jax.ShapeDtypeStruct((SEQ, D_MODEL), BF16),
                   jax.ShapeDtypeStruct((_SSD_NCHUNK, 2, SSD_STATE, 256), F32)],
        scratch_shapes=[pltpu.VMEM((2, SSD_STATE, 256), F32)], name=name,
        compiler_params=_cparams("arbitrary"))(proj, proj, proj, proj, conv_w, conv_b, dt_bias, a_log, d_skip, norm_g)


def _ssd_bwd(proj, hstates, dmix, conv_w, conv_b, dt_bias, a_log, d_skip, norm_g, *, name):
    q = SSD_CHUNK
    last = _SSD_NCHUNK - 1

    def body(z_ref, xbc_ref, halo_ref, dt_ref, hs_ref, dy_ref, cw_ref, cb_ref, db_ref, al_ref, d_ref, ng_ref,
             dp_ref, dcw_ref, dcb_ref, ddb_ref, dal_ref, dd_ref, dng_ref, dh_acc, dhalo_acc):
        i = pl.program_id(0)

        @pl.when(i == 0)
        def _():
            dh_acc[...] = jnp.zeros_like(dh_acc)
            dhalo_acc[...] = jnp.zeros_like(dhalo_acc)
            for r in (dcw_ref, dcb_ref, ddb_ref, dal_ref, dd_ref, dng_ref):
                r[...] = jnp.zeros_like(r)

        halo = jnp.where(i == last, 0.0, halo_ref[...])
        xbc_ext = jnp.concatenate([halo, xbc_ref[...]], axis=0)
        _, vjp = jax.vjp(_ssd_chunk, z_ref[...], xbc_ext, dt_ref[...], cw_ref[...], cb_ref[...], db_ref[...],
                         al_ref[...], d_ref[...], ng_ref[...], (hs_ref[0, 0], hs_ref[0, 1]))
        gz, gx, gdt, gcw, gcb, gdb, gal, gd, gng, gh = vjp((dy_ref[...], (dh_acc[0], dh_acc[1])))
        dxbc = jnp.concatenate([gx[CONV_HALO:q], gx[q:] + dhalo_acc[...]], axis=0)
        dp_ref[...] = jnp.concatenate([dxbc, gz, gdt, jnp.zeros_like(gdt)], axis=1).astype(BF16)
        dhalo_acc[...] = gx[:CONV_HALO]
        dh_acc[0] = gh[0]
        dh_acc[1] = gh[1]
        dcw_ref[...] += gcw
        dcb_ref[...] += gcb
        ddb_ref[...] += gdb
        dal_ref[...] += gal
        dd_ref[...] += gd
        dng_ref[...] += gng

    const = lambda i: (0, 0)
    rev = lambda i: last - i
    row = lambda n: jax.ShapeDtypeStruct((1, n), F32)
    return pl.pallas_call(
        body, grid=(_SSD_NCHUNK,),
        in_specs=[pl.BlockSpec((q, SSD_INNER), lambda i: (rev(i), PROJ_Z_BLK)),
                  pl.BlockSpec((q, SSD_CONV_CH), lambda i: (rev(i), 0)),
                  pl.BlockSpec((CONV_HALO, SSD_CONV_CH), lambda i: (jnp.maximum(rev(i) * _HALO_PER_CHUNK - 1, 0), 0)),
                  pl.BlockSpec((q, 128), lambda i: (rev(i), PROJ_DT_BLK)),
                  pl.BlockSpec((1, 2, SSD_STATE, 256), lambda i: (rev(i), 0, 0, 0)),
                  pl.BlockSpec((q, SSD_INNER), lambda i: (rev(i), 0))] + _ssd_param_specs(const),
        out_specs=[pl.BlockSpec((q, PROJ_SSD_W), lambda i: (rev(i), 0))] + _ssd_param_specs(const),
        out_shape=[jax.ShapeDtypeStruct((SEQ, PROJ_W), BF16), jax.ShapeDtypeStruct((8, SSD_CONV_CH), F32),
                   row(SSD_CONV_CH), row(128), row(128), row(128), row(SSD_INNER)],
        scratch_shapes=[pltpu.VMEM((2, SSD_STATE, 256), F32), pltpu.VMEM((CONV_HALO, SSD_CONV_CH), F32)], name=name,
        compiler_params=_cparams("arbitrary"))(proj, proj, proj, proj, hstates, dmix, conv_w, conv_b, dt_bias, a_log,
                                                d_skip, norm_g)


def _rope_tables(pos_col, inv_freq_lane, *, name):
    s = pos_col.shape[0]

    def body(p_ref, f_ref, c_ref, s1_ref, s2_ref):
        ang = p_ref[...] * f_ref[...]
        within = lax.broadcasted_iota(jnp.int32, ang.shape, 1) % HEAD_LANES
        half = ROT_DIM // 2
        c_ref[...] = jnp.where(within < ROT_DIM, jnp.cos(ang), 1.0)
        sn = jnp.sin(ang)
        s1_ref[...] = jnp.where(within < half, -sn, 0.0)
        s2_ref[...] = jnp.where((within >= half) & (within < ROT_DIM), sn, 0.0)

    shp = jax.ShapeDtypeStruct((s, 128), F32)
    return pl.pallas_call(body, out_shape=[shp, shp, shp], name=name,
                          compiler_params=pltpu.CompilerParams(vmem_limit_bytes=VMEM_LIMIT))(pos_col, inv_freq_lane)


def _rope(t, c, s1, s2):
    half = ROT_DIM // 2
    return t * c + pltpu.roll(t, 128 - half, 1) * s1 + pltpu.roll(t, half, 1) * s2


def _rope_t(g, c, s1, s2):
    half = ROT_DIM // 2
    return g * c + pltpu.roll(g * s1, half, 1) + pltpu.roll(g * s2, 128 - half, 1)


def _att_valid(b):
    qi = lax.broadcasted_iota(jnp.int32, (ATT_BLOCK, 2 * ATT_BLOCK), 0)
    kj = lax.broadcasted_iota(jnp.int32, (ATT_BLOCK, 2 * ATT_BLOCK), 1)
    rel = qi + ATT_BLOCK - kj
    return (rel >= 0) & (rel <= ATT_BLOCK) & (b * ATT_BLOCK + kj - ATT_BLOCK >= 0)


def _att_slices(i, d):
    if d == 1:
        qstart = pl.multiple_of(i * ATT_BLOCK, ATT_BLOCK)
        return i, pl.ds(qstart, ATT_BLOCK), pl.ds(pl.multiple_of(qstart - ATT_BLOCK + ATT_KPAD, ATT_BLOCK), 2 * ATT_BLOCK)
    r = i % d
    b = i // d
    qstart = r + d * ATT_BLOCK * b
    return b, pl.ds(qstart, ATT_BLOCK, stride=d), pl.ds(qstart - ATT_BLOCK * d + ATT_KPAD, 2 * ATT_BLOCK, stride=d)


_ATT_NBLK = SEQ // ATT_BLOCK
_ATT_SCALE = HEAD_LANES ** -0.5
_ATT_UNROLL_FWD = 8
_ATT_UNROLL = 4


def _att_fwd(proj, cos, sin1, sin2, mix, *, name):
    s = SEQ

    def body(q_ref, k_ref, v_ref, c_ref, s1_ref, s2_ref, _, o_ref, lse_ref, mix_ref, qs, ks, vs, acc, m_s, l_s):
        c, s1, s2 = c_ref[...], s1_ref[...], s2_ref[...]
        qs[...] = _rope(q_ref[...], c, s1, s2) * _ATT_SCALE
        zeros = jnp.zeros((ATT_KPAD, 128), F32)
        ks[pl.ds(0, ATT_KPAD), :] = zeros
        vs[pl.ds(0, ATT_KPAD), :] = zeros
        ks[pl.ds(ATT_KPAD, s), :] = _rope(k_ref[...], c, s1, s2)
        vs[pl.ds(ATT_KPAD, s), :] = v_ref[...]
        head0 = _head_of_lane(128) == 0

        for bi, (_, d) in enumerate(ATT_PATTERNS):
            def blk(i, carry, d=d, first=(bi == 0)):
                b, sq, sk = _att_slices(i, d)
                qb = qs[sq, :]
                kw = ks[sk, :].astype(BF16)
                vw = vs[sk, :].astype(BF16)
                valid = _att_valid(b)
                ms, ls, os_ = [], [], []
                for hh in range(2):
                    qh = jnp.where(head0 if hh == 0 else ~head0, qb, 0.0).astype(BF16)
                    sc = lax.dot_general(qh, kw, (((1,), (1,)), ((), ())), preferred_element_type=F32)
                    sc = jnp.where(valid, sc, -jnp.inf)
                    mb = jnp.max(sc, axis=1, keepdims=True)
                    p = jnp.exp(sc - mb)
                    ms.append(mb)
                    ls.append(jnp.sum(p, axis=1, keepdims=True))
                    os_.append(jnp.dot(p.astype(BF16), vw, preferred_element_type=F32))
                m_b = jnp.where(head0, ms[0], ms[1])
                l_b = jnp.where(head0, ls[0], ls[1])
                o_b = jnp.where(head0, os_[0], os_[1])
                if first:
                    m_s[sq, :] = m_b
                    l_s[sq, :] = l_b
                    acc[sq, :] = o_b
                else:
                    m_old = m_s[sq, :]
                    m_new = jnp.maximum(m_old, m_b)
                    a_old = jnp.exp(m_old - m_new)
                    a_b = jnp.exp(m_b - m_new)
                    m_s[sq, :] = m_new
                    l_s[sq, :] = l_s[sq, :] * a_old + l_b * a_b
                    acc[sq, :] = acc[sq, :] * a_old + o_b * a_b
                return carry

            lax.fori_loop(0, _ATT_NBLK, blk, 0, unroll=_ATT_UNROLL_FWD)

        out = acc[...] / l_s[...]
        o_ref[...] = out
        mix_ref[...] = out.astype(BF16)
        lse_ref[...] = m_s[...] + jnp.log(l_s[...])

    col = lambda base: pl.BlockSpec((s, 128), lambda p: (0, base + p))
    tab = pl.BlockSpec((s, 128), lambda p: (0, 0))
    big = pltpu.VMEM((ATT_KPAD + s, 128), F32)
    tok = pltpu.VMEM((s, 128), F32)
    return pl.pallas_call(
        body, grid=(2,), in_specs=[col(PROJ_Q_BLK), col(PROJ_K_BLK), col(PROJ_V_BLK), tab, tab, tab, _HBM],
        out_specs=[pl.BlockSpec((s, 128), lambda p: (0, p)), pl.BlockSpec((s, 128), lambda p: (0, p)),
                   col(MIX_ATT_BLK)],
        out_shape=[jax.ShapeDtypeStruct((s, ATT_W), F32), jax.ShapeDtypeStruct((s, ATT_W), F32),
                   jax.ShapeDtypeStruct(mix.shape, mix.dtype)],
        input_output_aliases={6: 2}, scratch_shapes=[tok, big, big, tok, tok, tok], name=name,
        compiler_params=_cparams("arbitrary"))(proj, proj, proj, cos, sin1, sin2, mix)


def _att_bwd(proj, cos, sin1, sin2, out, lse, dmix, dproj, *, name):
    s = SEQ

    def body(proj_ref, c_hbm, s1_hbm, s2_hbm, out_hbm, lse_hbm, dmix_hbm, _, dproj_hbm,
             c_ref, s1_ref, s2_ref, o_ref, lse_ref, do_ref, qs, ks, vs, dqs, dks, dvs, staged, sems):
        def start(copies):
            for cp in copies:
                cp.start()
            return copies

        def load(pair):
            lanes = pl.ds(128 * pair, 128)
            rows = pl.ds(ATT_KPAD, s)
            return start([
                pltpu.make_async_copy(proj_ref.at[:, pl.ds(128 * (PROJ_Q_BLK + pair), 128)], qs, sems.at[0]),
                pltpu.make_async_copy(proj_ref.at[:, pl.ds(128 * (PROJ_K_BLK + pair), 128)], ks.at[rows, :], sems.at[1]),
                pltpu.make_async_copy(proj_ref.at[:, pl.ds(128 * (PROJ_V_BLK + pair), 128)], vs.at[rows, :], sems.at[2]),
                pltpu.make_async_copy(out_hbm.at[:, lanes], o_ref, sems.at[3]),
                pltpu.make_async_copy(lse_hbm.at[:, lanes], lse_ref, sems.at[4]),
                pltpu.make_async_copy(dmix_hbm.at[:, pl.ds(128 * (MIX_ATT_BLK + pair), 128)], do_ref, sems.at[5])])

        tables = start([pltpu.make_async_copy(c_hbm, c_ref, sems.at[6]),
                        pltpu.make_async_copy(s1_hbm, s1_ref, sems.at[7]),
                        pltpu.make_async_copy(s2_hbm, s2_ref, sems.at[8])])
        loads = load(0)
        for cp in tables:
            cp.wait()
        head0 = _head_of_lane(128) == 0
        zeros = jnp.zeros((ATT_KPAD, 128), F32)
        for pair in range(2):
            for cp in loads:
                cp.wait()
            c, s1, s2 = c_ref[...], s1_ref[...], s2_ref[...]
            qs[...] = _rope(qs[...], c, s1, s2) * _ATT_SCALE
            ks[pl.ds(0, ATT_KPAD), :] = zeros
            vs[pl.ds(0, ATT_KPAD), :] = zeros
            ks[pl.ds(ATT_KPAD, s), :] = _rope(ks[pl.ds(ATT_KPAD, s), :], c, s1, s2)
            dqs[...] = jnp.zeros_like(dqs)
            dks[...] = jnp.zeros_like(dks)
            dvs[...] = jnp.zeros_like(dvs)

            for _, d in ATT_PATTERNS:
                def blk(i, carry, d=d):
                    b, sq, sk = _att_slices(i, d)
                    qb = qs[sq, :]
                    kw = ks[sk, :].astype(BF16)
                    vw = vs[sk, :].astype(BF16)
                    dob = do_ref[sq, :]
                    lse_b = lse_ref[sq, :]
                    dd = dob * o_ref[sq, :]
                    valid = _att_valid(b)
                    dq_b = jnp.zeros((ATT_BLOCK, 128), F32)
                    dk_w = jnp.zeros((2 * ATT_BLOCK, 128), F32)
                    dv_w = jnp.zeros((2 * ATT_BLOCK, 128), F32)
                    for hh in range(2):
                        hm = head0 if hh == 0 else ~head0
                        qh = jnp.where(hm, qb, 0.0).astype(BF16)
                        doh = jnp.where(hm, dob, 0.0).astype(BF16)
                        lse_h = _pick_lane(lse_b, hh * HEAD_LANES)
                        d_h = jnp.sum(jnp.where(hm, dd, 0.0), axis=1, keepdims=True)
                        sc = lax.dot_general(qh, kw, (((1,), (1,)), ((), ())), preferred_element_type=F32)
                        p = jnp.where(valid, jnp.exp(sc - lse_h), 0.0)
                        dp = lax.dot_general(doh, vw, (((1,), (1,)), ((), ())), preferred_element_type=F32)
                        ds = (p * (dp - d_h)).astype(BF16)
                        dq_b = dq_b + jnp.where(hm, jnp.dot(ds, kw, preferred_element_type=F32), 0.0)
                        dk_w = dk_w + lax.dot_general(ds, qh, (((0,), (0,)), ((), ())), preferred_element_type=F32)
                        dv_w = dv_w + lax.dot_general(p.astype(BF16), doh, (((0,), (0,)), ((), ())),
                                                      preferred_element_type=F32)
                    dqs[sq, :] += dq_b
                    dks[sk, :] += dk_w
                    dvs[sk, :] += dv_w
                    return carry

                lax.fori_loop(0, _ATT_NBLK, blk, 0, unroll=_ATT_UNROLL)

            staged[0] = _rope_t(dqs[...] * _ATT_SCALE, c, s1, s2).astype(BF16)
            staged[1] = _rope_t(dks[pl.ds(ATT_KPAD, s), :], c, s1, s2).astype(BF16)
            staged[2] = dvs[pl.ds(ATT_KPAD, s), :].astype(BF16)
            stores = start([
                pltpu.make_async_copy(staged.at[j], dproj_hbm.at[:, pl.ds(128 * (col + pair), 128)], sems.at[9 + j])
                for j, col in enumerate((PROJ_Q_BLK, PROJ_K_BLK, PROJ_V_BLK))])
            if pair == 0:
                loads = load(1)
            for cp in stores:
                cp.wait()

    big = pltpu.VMEM((ATT_KPAD + s, 128), F32)
    tok = pltpu.VMEM((s, 128), F32)
    return pl.pallas_call(
        body, in_specs=[_HBM] * 8, out_specs=_HBM, out_shape=jax.ShapeDtypeStruct(dproj.shape, dproj.dtype),
        input_output_aliases={7: 0},
        scratch_shapes=[tok] * 6 + [tok, big, big, tok, big, big, pltpu.VMEM((3, s, 128), BF16),
                                    pltpu.SemaphoreType.DMA((12,))], name=name,
        compiler_params=pltpu.CompilerParams(vmem_limit_bytes=VMEM_LIMIT))(
            proj, cos, sin1, sin2, out, lse, dmix, dproj)


_POOL_TM = 512
_POOL_NT = SEQ // _POOL_TM
_POOL_HALO_PER_TILE = _POOL_TM // POOL_HALO


def _pool_tile(u_ext, w_bd, scale, t0):
    s2 = u_ext + _roll_rows(u_ext, 1)
    s4 = s2 + _roll_rows(s2, 2)
    s8 = s4 + _roll_rows(s4, 4)
    s16 = s8 + _roll_rows(s8, 8)
    grp = _head_of_lane(POOL_W)
    sel = jnp.where(grp == 0, s2, jnp.where(grp == 1, s4, jnp.where(grp == 2, s8, s16)))[POOL_HALO:]
    t = sel.shape[0]
    pos = t0 + lax.broadcasted_iota(jnp.int32, (t, POOL_W), 0) + 1
    win = jnp.where(grp == 0, 2, jnp.where(grp == 1, 4, jnp.where(grp == 2, 8, 16)))
    cnt = jnp.minimum(pos, win).astype(F32)
    diff = sel / cnt - u_ext[POOL_HALO:]
    return jnp.dot(diff.astype(BF16), w_bd.astype(BF16), preferred_element_type=F32) * scale


def _pool_fwd(proj, w_bd, scale, mix, *, name):
    tm = _POOL_TM

    def body(u_ref, halo_ref, w_ref, sc_ref, _, y_ref):
        i = pl.program_id(0)
        halo = jnp.where(i == 0, 0.0, halo_ref[...])
        u_ext = jnp.concatenate([halo, u_ref[...]], axis=0)
        y_ref[...] = _pool_tile(u_ext, w_ref[...], sc_ref[...], i * tm).astype(BF16)

    return pl.pallas_call(
        body, grid=(_POOL_NT,),
        in_specs=[pl.BlockSpec((tm, POOL_W), lambda i: (i, PROJ_POOL_BLK)),
                  pl.BlockSpec((POOL_HALO, POOL_W),
                               lambda i: (jnp.maximum(i * _POOL_HALO_PER_TILE - 1, 0), PROJ_POOL_BLK)),
                  pl.BlockSpec((POOL_W, POOL_W), lambda i: (0, 0)), pl.BlockSpec((1, POOL_W), lambda i: (0, 0)), _HBM],
        out_specs=pl.BlockSpec((tm, POOL_W), lambda i: (i, MIX_POOL_BLK)),
        out_shape=jax.ShapeDtypeStruct(mix.shape, mix.dtype), input_output_aliases={4: 0}, name=name,
        compiler_params=_cparams("parallel"))(proj, proj, w_bd, scale, mix)


def _pool_bwd(proj, dmix, w_bd, scale, dproj, *, name):
    tm = _POOL_TM
    last = _POOL_NT - 1

    def body(u_ref, halo_ref, dy_ref, w_ref, sc_ref, _, du_ref, dw_ref, dsc_ref, dhalo_acc):
        i = pl.program_id(0)

        @pl.when(i == 0)
        def _():
            dhalo_acc[...] = jnp.zeros_like(dhalo_acc)
            dw_ref[...] = jnp.zeros_like(dw_ref)
            dsc_ref[...] = jnp.zeros_like(dsc_ref)

        tile = last - i
        halo = jnp.where(tile == 0, 0.0, halo_ref[...])
        u_ext = jnp.concatenate([halo, u_ref[...]], axis=0)
        _, vjp = jax.vjp(functools.partial(_pool_tile, t0=tile * tm), u_ext, w_ref[...], sc_ref[...])
        gu, gw, gs = vjp(dy_ref[...])
        du_ref[...] = jnp.concatenate([gu[POOL_HALO:tm], gu[tm:] + dhalo_acc[...]], axis=0).astype(BF16)
        dhalo_acc[...] = gu[:POOL_HALO]
        dw_ref[...] += gw
        dsc_ref[...] += gs

    rev = lambda i: last - i
    return pl.pallas_call(
        body, grid=(_POOL_NT,),
        in_specs=[pl.BlockSpec((tm, POOL_W), lambda i: (rev(i), PROJ_POOL_BLK)),
                  pl.BlockSpec((POOL_HALO, POOL_W),
                               lambda i: (jnp.maximum(rev(i) * _POOL_HALO_PER_TILE - 1, 0), PROJ_POOL_BLK)),
                  pl.BlockSpec((tm, POOL_W), lambda i: (rev(i), MIX_POOL_BLK)),
                  pl.BlockSpec((POOL_W, POOL_W), lambda i: (0, 0)), pl.BlockSpec((1, POOL_W), lambda i: (0, 0)), _HBM],
        out_specs=[pl.BlockSpec((tm, POOL_W), lambda i: (rev(i), PROJ_POOL_BLK)),
                   pl.BlockSpec((POOL_W, POOL_W), lambda i: (0, 0)), pl.BlockSpec((1, POOL_W), lambda i: (0, 0))],
        out_shape=[jax.ShapeDtypeStruct(dproj.shape, dproj.dtype), jax.ShapeDtypeStruct((POOL_W, POOL_W), F32),
                   jax.ShapeDtypeStruct((1, POOL_W), F32)],
        input_output_aliases={5: 0}, scratch_shapes=[pltpu.VMEM((POOL_HALO, POOL_W), F32)], name=name,
        compiler_params=_cparams("arbitrary"))(proj, proj, dmix, w_bd, scale, dproj)


_FFN_TM = 256
_FFN_NT = SEQ // _FFN_TM
_FFN_HALO_PER_TILE = _FFN_TM // CONV_HALO


def _ffn_act_tile(hid_ext, conv_w, conv_b):
    hc = conv_b
    for k in range(FFN_CONV_K):
        hc = hc + _pick_row(conv_w, k) * _shift_rows(hid_ext, FFN_CONV_K - 1 - k, CONV_HALO)
    return _silu(hc[:, :FFN_DIM]) * hc[:, FFN_DIM:]


def _ffn_fwd(x1, norm_g, sc, sh, gate, up_t, down, conv_w, conv_b, *, name):
    tm = _FFN_TM
    w = 2 * FFN_DIM
    d = D_MODEL

    def body(x_ref, ng_ref, sc_ref, sh_ref, g_ref, up_ref, dn_ref, cw_ref, cb_ref,
             h_ref, hid_ref, act_ref, f_ref, x2_ref, halo_acc):
        i = pl.program_id(0)
        h2 = _rms_modulate(x_ref[...], ng_ref[...], sc_ref[...], sh_ref[...]).astype(BF16)
        h_ref[...] = h2
        hid = lax.dot_general(h2, up_ref[...], (((1,), (1,)), ((), ())), preferred_element_type=F32)
        hid_ref[...] = hid
        halo = jnp.where(i == 0, 0.0, halo_acc[...])
        act = _ffn_act_tile(jnp.concatenate([halo, hid], axis=0), cw_ref[...], cb_ref[...]).astype(BF16)
        halo_acc[...] = hid[tm - CONV_HALO:]
        act_ref[...] = act
        f = jnp.dot(act, dn_ref[...], preferred_element_type=F32)
        f_ref[...] = f
        x2_ref[...] = x_ref[...] + g_ref[...] * f

    tile = lambda n: pl.BlockSpec((tm, n), lambda i: (i, 0))
    return pl.pallas_call(
        body, grid=(_FFN_NT,),
        in_specs=[tile(d)] + [_resident((1, d))] * 4 + [_resident((w, d)), _resident((FFN_DIM, d)),
                                                        _resident((8, w)), _resident((1, w))],
        out_specs=[tile(d), tile(w), tile(FFN_DIM), tile(d), tile(d)],
        out_shape=[jax.ShapeDtypeStruct((SEQ, d), BF16), jax.ShapeDtypeStruct((SEQ, w), F32),
                   jax.ShapeDtypeStruct((SEQ, FFN_DIM), BF16), jax.ShapeDtypeStruct((SEQ, d), F32),
                   jax.ShapeDtypeStruct((SEQ, d), F32)],
        scratch_shapes=[pltpu.VMEM((CONV_HALO, w), F32)], name=name,
        compiler_params=_cparams("arbitrary"))(x1, norm_g, sc, sh, gate, up_t, down, conv_w, conv_b)


def _ffn_bwd(dx2, gate, f, hid, up_t, down, conv_w, conv_b, *, name):
    tm = _FFN_TM
    w = 2 * FFN_DIM
    d = D_MODEL
    last = _FFN_NT - 1

    def body(dx_ref, g_ref, f_ref, h_ref, halo_ref, up_ref, dn_ref, cw_ref, cb_ref,
             df_ref, dg_ref, dh_ref, dh2_ref, dcw_ref, dcb_ref, dhalo_acc):
        i = pl.program_id(0)

        @pl.when(i == 0)
        def _():
            dhalo_acc[...] = jnp.zeros_like(dhalo_acc)
            dcw_ref[...] = jnp.zeros_like(dcw_ref)
            dcb_ref[...] = jnp.zeros_like(dcb_ref)
            dg_ref[...] = jnp.zeros_like(dg_ref)

        dxv = dx_ref[...]
        df = (g_ref[...] * dxv).astype(BF16)
        df_ref[...] = df
        dg_ref[...] += jnp.sum(dxv * f_ref[...], axis=0, keepdims=True)
        dact = lax.dot_general(df, dn_ref[...], (((1,), (1,)), ((), ())), preferred_element_type=F32)
        halo = jnp.where(i == last, 0.0, halo_ref[...])
        hid_ext = jnp.concatenate([halo, h_ref[...]], axis=0)
        _, vjp = jax.vjp(_ffn_act_tile, hid_ext, cw_ref[...], cb_ref[...])
        gh, gw, gb = vjp(dact)
        dhid = jnp.concatenate([gh[CONV_HALO:tm], gh[tm:] + dhalo_acc[...]], axis=0).astype(BF16)
        dhalo_acc[...] = gh[:CONV_HALO]
        dh_ref[...] = dhid
        dh2_ref[...] = jnp.dot(dhid, up_ref[...], preferred_element_type=F32)
        dcw_ref[...] += gw
        dcb_ref[...] += gb

    rev = lambda i: last - i
    tile = lambda n: pl.BlockSpec((tm, n), lambda i: (rev(i), 0))
    acc = lambda shape: pl.BlockSpec(shape, lambda i: (0, 0))
    return pl.pallas_call(
        body, grid=(_FFN_NT,),
        in_specs=[tile(d), _resident((1, d)), tile(d), tile(w),
                  pl.BlockSpec((CONV_HALO, w), lambda i: (jnp.maximum(rev(i) * _FFN_HALO_PER_TILE - 1, 0), 0)),
                  _resident((w, d)), _resident((FFN_DIM, d)), _resident((8, w)), _resident((1, w))],
        out_specs=[tile(d), acc((1, d)), tile(w), tile(d), acc((8, w)), acc((1, w))],
        out_shape=[jax.ShapeDtypeStruct((SEQ, d), BF16), jax.ShapeDtypeStruct((1, d), F32),
                   jax.ShapeDtypeStruct((SEQ, w), BF16), jax.ShapeDtypeStruct((SEQ, d), F32),
                   jax.ShapeDtypeStruct((8, w), F32), jax.ShapeDtypeStruct((1, w), F32)],
        scratch_shapes=[pltpu.VMEM((CONV_HALO, w), F32)], name=name,
        compiler_params=_cparams("arbitrary"))(dx2, gate, f, hid, hid, up_t, down, conv_w, conv_b)


def _axes():
    return lax.axis_index("x"), lax.axis_index("y"), lax.axis_index("c")


def _handshake(peers):
    barrier = pltpu.get_barrier_semaphore()
    for peer in peers:
        pl.semaphore_signal(barrier, inc=1, device_id=peer, device_id_type=MESH)
    pl.semaphore_wait(barrier, len(peers))


def _allgather_body(x_refs, out_refs, send_sems, recv_sems, local_sems, own_barrier):
    n = len(x_refs)
    x, y, c = _axes()
    me, sibling = (x, y, c), (x, y, 1 - c)
    chips = [(1 - x, y), (x, 1 - y), (1 - x, 1 - y)]
    if own_barrier:
        _handshake([sibling] + [(*chip, c) for chip in chips])

    def slot(a, px, py, pc):
        return out_refs[a].at[4 * px + 2 * py + pc]

    def copy(a, k, block, to, src=None):
        return pltpu.make_async_remote_copy(
            src_ref=slot(a, *block) if src is None else src, dst_ref=slot(a, *block),
            send_sem=send_sems.at[a, k], recv_sem=recv_sems.at[a, k], device_id=to, device_id_type=MESH)

    mines, firsts = [], []
    for a in range(n):
        mines.append(pltpu.make_async_copy(x_refs[a], slot(a, *me), local_sems.at[a]))
        mines[-1].start()
        first = [copy(a, 0, me, sibling, src=x_refs[a])]
        first += [copy(a, 1 + j, me, (*chip, c), src=x_refs[a]) for j, chip in enumerate(chips)]
        for cp in first:
            cp.start()
        firsts += first
    passed = []
    for j, chip in enumerate(chips):
        for a in range(n):
            copy(a, 1 + j, (*chip, c), me).wait_recv()
            passed.append(copy(a, 4 + j, (*chip, c), sibling))
            passed[-1].start()
    for a in range(n):
        copy(a, 0, sibling, me).wait_recv()
    for j, chip in enumerate(chips):
        for a in range(n):
            copy(a, 4 + j, (*chip, 1 - c), me).wait_recv()
    for cp in firsts + passed:
        cp.wait_send()
    for cp in mines:
        cp.wait()


def _allgather_sems(n):
    return [pltpu.SemaphoreType.DMA((n, 7)), pltpu.SemaphoreType.DMA((n, 7)), pltpu.SemaphoreType.DMA((n,))]


def _allgather(xs, *, name):
    n = len(xs)

    def body(*refs):
        _allgather_body(refs[:n], refs[n:2 * n], *refs[2 * n:], own_barrier=False)

    return pl.pallas_call(
        body, out_shape=[jax.ShapeDtypeStruct((N_DEV,) + xb.shape, xb.dtype) for xb in xs],
        in_specs=[_HBM] * n, out_specs=[_HBM] * n, scratch_shapes=_allgather_sems(n), name=name)(*xs)


def _allgather_async(xs, *, name, collective_id):
    n = len(xs)
    x_refs = [jax.new_ref(xb, memory_space=pltpu.MemorySpace.HBM) for xb in xs]
    out_refs = [jax.empty_ref(jax.ShapeDtypeStruct((N_DEV,) + xb.shape, xb.dtype), memory_space=pltpu.MemorySpace.HBM)
                for xb in xs]

    @pl.kernel(mesh=plsc.ScalarSubcoreMesh(axis_name="sequencer", num_cores=1), name=name,
               scratch_types=tuple(_allgather_sems(n)),
               compiler_params=pltpu.CompilerParams(collective_id=collective_id))
    def launch(send_sems, recv_sems, local_sems):
        _allgather_body(x_refs, out_refs, send_sems, recv_sems, local_sems, own_barrier=True)

    launch()
    return [r[...] for r in out_refs]


def _pair_exchange(blocks, *, name, collective_id):
    n = len(blocks)
    hbm = pltpu.MemorySpace.HBM
    in_refs = [jax.new_ref(b, memory_space=hbm) for b in blocks]
    out_refs = [jax.empty_ref(jax.ShapeDtypeStruct((4,) + b.shape[1:], b.dtype), memory_space=hbm) for b in blocks]

    @pl.kernel(mesh=plsc.ScalarSubcoreMesh(axis_name="sequencer", num_cores=1), name=name,
               scratch_types=(pltpu.SemaphoreType.DMA((n, 4)), pltpu.SemaphoreType.DMA((n, 4))),
               compiler_params=pltpu.CompilerParams(collective_id=collective_id))
    def launch(send_sems, recv_sems):
        x, y, c = _axes()
        _handshake([(x, y, 1 - c)])
        copies = [pltpu.make_async_remote_copy(
            src_ref=in_refs[a].at[2 * s + (1 - c)], dst_ref=out_refs[a].at[s], send_sem=send_sems.at[a, s],
            recv_sem=recv_sems.at[a, s], device_id=(x, y, 1 - c), device_id_type=MESH)
            for a in range(n) for s in range(4)]
        for cp in copies:
            cp.start()
        for cp in copies:
            cp.wait_recv()
        for cp in copies:
            cp.wait_send()

    launch()
    return [r[...] for r in out_refs]


def _chip_exchange(parts, *, name, collective_id):
    n = len(parts)
    hbm = pltpu.MemorySpace.HBM
    in_refs = [jax.new_ref(p, memory_space=hbm) for p in parts]
    out_refs = [jax.empty_ref(jax.ShapeDtypeStruct(p.shape, p.dtype), memory_space=hbm) for p in parts]

    @pl.kernel(mesh=plsc.ScalarSubcoreMesh(axis_name="sequencer", num_cores=1), name=name,
               scratch_types=(pltpu.SemaphoreType.DMA((n, 3)), pltpu.SemaphoreType.DMA((n, 3)),
                              pltpu.SemaphoreType.DMA((n,))),
               compiler_params=pltpu.CompilerParams(collective_id=collective_id))
    def launch(send_sems, recv_sems, local_sems):
        x, y, c = _axes()
        my_chip = 2 * x + y
        chips = [(1 - x, y), (x, 1 - y), (1 - x, 1 - y)]
        _handshake([(*chip, c) for chip in chips])
        locals_ = [pltpu.make_async_copy(in_refs[a].at[my_chip], out_refs[a].at[my_chip], local_sems.at[a])
                   for a in range(n)]
        for cp in locals_:
            cp.start()
        copies = [pltpu.make_async_remote_copy(
            src_ref=in_refs[a].at[2 * px + py], dst_ref=out_refs[a].at[my_chip], send_sem=send_sems.at[a, k],
            recv_sem=recv_sems.at[a, k], device_id=(px, py, c), device_id_type=MESH)
            for a in range(n) for k, (px, py) in enumerate(chips)]
        for cp in copies:
            cp.start()
        for cp in copies:
            cp.wait_recv()
        for cp in copies:
            cp.wait_send()
        for cp in locals_:
            cp.wait()

    launch()
    return [r[...] for r in out_refs]


def _pair_sum(core, blocks, from_sibling, *, name):
    n = len(blocks)

    def body(core_ref, *refs):
        for a_ref, b_ref, o_ref in zip(refs[:n], refs[n:2 * n], refs[2 * n:]):
            o_ref[...] = (a_ref[...].astype(F32) + b_ref[...].astype(F32)).astype(o_ref.dtype)

    mine = lambda b: pl.BlockSpec((1,) + b.shape[1:], lambda s, core_ref: (2 * s + core_ref[0], 0, 0))
    slot = lambda b: pl.BlockSpec((1,) + b.shape[1:], lambda s, core_ref: (s, 0, 0))
    return pl.pallas_call(
        body,
        grid_spec=pltpu.PrefetchScalarGridSpec(
            num_scalar_prefetch=1, grid=(4,),
            in_specs=[mine(b) for b in blocks] + [slot(b) for b in blocks], out_specs=[slot(b) for b in blocks]),
        out_shape=[jax.ShapeDtypeStruct(s.shape, s.dtype) for s in from_sibling], name=name,
        compiler_params=_cparams("parallel"))(core, *blocks, *from_sibling)


def _sum_blocks(a, *, name, tr=None):
    n, r, cdim = a.shape
    tr = tr or r

    def body(a_ref, o_ref):
        acc = a_ref[0].astype(F32)
        for k in range(1, n):
            acc = acc + a_ref[k].astype(F32)
        o_ref[...] = acc

    return pl.pallas_call(body, grid=(r // tr,), in_specs=[pl.BlockSpec((n, tr, cdim), lambda i: (0, i, 0))],
                          out_specs=pl.BlockSpec((tr, cdim), lambda i: (i, 0)),
                          out_shape=jax.ShapeDtypeStruct((r, cdim), F32), name=name,
                          compiler_params=_cparams("parallel"))(a)


def _sum_gathered(gathered, *, name):
    n = len(gathered)

    def body(*refs):
        for a_ref, o_ref in zip(refs[:n], refs[n:]):
            acc = a_ref[0]
            for k in range(1, N_DEV):
                acc = acc + a_ref[k]
            o_ref[...] = acc

    return pl.pallas_call(body, out_shape=[jax.ShapeDtypeStruct(g.shape[1:], F32) for g in gathered], name=name,
                          compiler_params=pltpu.CompilerParams(vmem_limit_bytes=VMEM_LIMIT))(*gathered)


_ADA_SHARD = 6 * D_MODEL // N_DEV


def _ada_mod(c_all, ada_w, *, name):
    def body(c_ref, w_ref, o_ref):
        o_ref[0] = jnp.dot(_silu(c_ref[...]).astype(BF16), w_ref[0].astype(BF16), preferred_element_type=F32)

    return pl.pallas_call(
        body, grid=(DEPTH,),
        in_specs=[pl.BlockSpec((N_DEV, D_MODEL), lambda l: (0, 0)),
                  pl.BlockSpec((1, D_MODEL, _ADA_SHARD), lambda l: (l, 0, 0))],
        out_specs=pl.BlockSpec((1, N_DEV, _ADA_SHARD), lambda l: (l, 0, 0)),
        out_shape=jax.ShapeDtypeStruct((DEPTH, N_DEV, _ADA_SHARD), F32), name=name,
        compiler_params=_cparams("parallel"))(c_all, ada_w)


def _ada_wgrad(c_all, dmod_cols, *, name):
    def body(c_ref, d_ref, o_ref):
        o_ref[0] = lax.dot_general(_silu(c_ref[...]), d_ref[0], (((0,), (0,)), ((), ())),
                                   preferred_element_type=F32, precision=lax.Precision.HIGHEST)

    return pl.pallas_call(
        body, grid=(DEPTH,),
        in_specs=[pl.BlockSpec((N_DEV, D_MODEL), lambda l: (0, 0)),
                  pl.BlockSpec((1, N_DEV, _ADA_SHARD), lambda l: (l, 0, 0))],
        out_specs=pl.BlockSpec((1, D_MODEL, _ADA_SHARD), lambda l: (l, 0, 0)),
        out_shape=jax.ShapeDtypeStruct((DEPTH, D_MODEL, _ADA_SHARD), F32), name=name,
        compiler_params=_cparams("parallel"))(c_all, dmod_cols)


def _add_rows(a, b, *, name):
    def body(a_ref, b_ref, o_ref):
        o_ref[...] = a_ref[...] + b_ref[...]

    return pl.pallas_call(body, out_shape=jax.ShapeDtypeStruct(a.shape, a.dtype), name=name)(a, b)


def _adamw_update(w_ref, g_ref, m_ref, v_ref, d_ref, mo_ref, vo_ref):
    gv = g_ref[...]
    mn = ADAM_B1 * m_ref[...] + (1.0 - ADAM_B1) * gv
    vn = ADAM_B2 * v_ref[...] + (1.0 - ADAM_B2) * (gv * gv)
    mo_ref[...] = mn
    vo_ref[...] = vn
    m_hat = mn / (1.0 - ADAM_B1 ** ADAM_STEP)
    v_hat = vn / (1.0 - ADAM_B2 ** ADAM_STEP)
    d_ref[...] = -ADAM_LR * (m_hat / (jnp.sqrt(v_hat) + ADAM_EPS) + ADAM_WD * w_ref[...])


def _adamw_small(ws, gs, ms, vs, *, name):
    n = len(ws)

    def body(*refs):
        ins, outs = refs[:4 * n], refs[4 * n:]
        for i in range(n):
            _adamw_update(ins[i], ins[n + i], ins[2 * n + i], ins[3 * n + i], outs[i], outs[n + i], outs[2 * n + i])

    shapes = [jax.ShapeDtypeStruct(a.shape, F32) for a in ws]
    outs = pl.pallas_call(body, out_shape=shapes * 3, name=name,
                          compiler_params=pltpu.CompilerParams(vmem_limit_bytes=VMEM_LIMIT))(*ws, *gs, *ms, *vs)
    return outs[:n], outs[n:2 * n], outs[2 * n:]


def _adamw_stack(w, g, m, v, *, name, tr):
    layers, r, cdim = w.shape
    body = functools.partial(_adamw_update)
    spec = pl.BlockSpec((1, tr, cdim), lambda l, i: (l, i, 0))
    shp = jax.ShapeDtypeStruct(w.shape, F32)
    return pl.pallas_call(body, grid=(layers, r // tr), in_specs=[spec] * 4, out_specs=[spec] * 3,
                          out_shape=[shp] * 3, name=name, compiler_params=_cparams("parallel", "parallel"))(w, g, m, v)


def _adamw(w, g, m, v, *, name, tr):
    r, cdim = w.shape
    body = functools.partial(_adamw_update)

    spec = pl.BlockSpec((tr, cdim), lambda i: (i, 0))
    shp = jax.ShapeDtypeStruct((r, cdim), F32)
    return pl.pallas_call(body, grid=(r // tr,), in_specs=[spec] * 4, out_specs=[spec] * 3, out_shape=[shp] * 3,
                          name=name, compiler_params=_cparams("parallel"))(w, g, m, v)


def _pad_rows(a, rows):
    return jnp.concatenate([a, jnp.zeros((rows - a.shape[0],) + a.shape[1:], a.dtype)], axis=0)


def _pad_lanes(a, lanes):
    return jnp.concatenate([a, jnp.zeros(a.shape[:-1] + (lanes - a.shape[-1],), a.dtype)], axis=-1)


def _permute_w_in(wt):
    return jnp.concatenate([wt[512:1536], wt[:512], wt[1536:1544],
                            jnp.zeros((PROJ_W - IN_W, wt.shape[1]), wt.dtype), wt[1544:]], axis=0)


def _unpermute_w_in(wp):
    return jnp.concatenate([wp[1024:1536], wp[:1024], wp[1536:1544], wp[PROJ_SSD_W:]], axis=0)


def _block_diag(w):
    rows = []
    for g in range(4):
        rows.append(jnp.concatenate([w[g] if k == g else jnp.zeros_like(w[g]) for k in range(4)], axis=1))
    return jnp.concatenate(rows, axis=0)


def _diag_blocks(wbd):
    return jnp.stack([wbd[64 * g:64 * (g + 1), 64 * g:64 * (g + 1)] for g in range(4)], axis=0)


def _layer_params(l, small):
    return dict(
        norm1_g=small["norm1_g"][l][None], norm2_g=small["norm2_g"][l][None],
        conv_w=_pad_rows(small["ssd_conv_w"][l], 8), conv_b=small["ssd_conv_b"][l][None],
        dt_bias=_pad_lanes(small["ssd_dt_bias"][l][None], 128), a_log=_pad_lanes(small["ssd_a_log"][l][None], 128),
        d_skip=_pad_lanes(small["ssd_d"][l][None], 128), ssd_norm_g=small["ssd_norm_g"][l][None],
        pool_bd=_block_diag(small["pool_w"][l]), pool_scale=small["pool_scale"][l][None],
        fcw=_pad_rows(small["ffn_conv_w"][l], 8), fcb=small["ffn_conv_b"][l][None])


def _mod_rows(mod_l):
    return [mod_l[None, D_MODEL * i:D_MODEL * (i + 1)] for i in range(6)]


def _layer_fwd(x, mod_l, p, tabs, l, gather):
    sh1, sc1, g1, sh2, sc2, g2 = _mod_rows(mod_l)
    mix_w = gather(l, "mix", None)
    p.update(w_in=mix_w["w_in"], w_out=mix_w["w_out"])
    proj, h1 = _mm(x, p["w_in"], nt=True, norm=(p["norm1_g"], sc1, sh1), name=f"l{l}_proj")
    ffn_w = gather(l, "ffn", proj)
    p.update(up=ffn_w["ffn_up"], down=ffn_w["ffn_down"])
    mix, hst = _ssd_fwd(proj, p["conv_w"], p["conv_b"], p["dt_bias"], p["a_log"], p["d_skip"], p["ssd_norm_g"],
                        name=f"l{l}_ssd")
    mix = _pool_fwd(proj, p["pool_bd"], p["pool_scale"], mix, name=f"l{l}_pool")
    y_att, lse, mix = _att_fwd(proj, *tabs, mix, name=f"l{l}_att")
    mo, x1 = _mm(mix, p["w_out"], residual=(x, g1), name=f"l{l}_out")
    gather(l + 1, "mix", (x1, p["up"]))
    h2, hid, act, f, x2 = _ffn_fwd(x1, p["norm2_g"], sc2, sh2, g2, p["up"], p["down"], p["fcw"], p["fcb"],
                                   name=f"l{l}_ffn")
    return x2, dict(x=x, h1=h1, proj=proj, hst=hst, y_att=y_att, lse=lse, mix=mix, mo=mo, x1=x1, h2=h2, hid=hid,
                    act=act, f=f)


def _layer_bwd(dx2, sv, mod_l, p, tabs, l, exchange):
    sh1, sc1, g1, sh2, sc2, g2 = _mod_rows(mod_l)
    df, dg2, dhid, dh2, dfcw, dfcb = _ffn_bwd(dx2, g2, sv["f"], sv["hid"], p["up"], p["down"], p["fcw"], p["fcb"],
                                              name=f"l{l}_ffn_b")
    d_down = _wgrad(sv["act"], df, tk=1408, name=f"l{l}_down_bw")
    d_up = _wgrad(dhid, sv["h2"], tk=1408, name=f"l{l}_up_bw")
    finish_ffn = exchange(l, "ffn", dict(ffn_up=d_up, ffn_down=d_down))
    dx1, dn2, dsc2, dsh2, dmix, dmo, dg1 = _mid_bwd(sv["x1"], dh2, dx2, p["norm2_g"], sc2, g1, sv["mo"], p["w_out"],
                                                    name=f"l{l}_mid_b")
    finish_ffn(dx1)
    d_wout = _wgrad(sv["mix"], dmo, name=f"l{l}_out_bw")
    dproj, dcw, dcb, ddb, dal, dd, dng = _ssd_bwd(
        sv["proj"], sv["hst"], dmix, p["conv_w"], p["conv_b"], p["dt_bias"], p["a_log"], p["d_skip"],
        p["ssd_norm_g"], name=f"l{l}_ssd_b")
    dproj, dwbd, dpsc = _pool_bwd(sv["proj"], dmix, p["pool_bd"], p["pool_scale"], dproj, name=f"l{l}_pool_b")
    dproj = _att_bwd(sv["proj"], *tabs, sv["y_att"], sv["lse"], dmix, dproj, name=f"l{l}_att_b")
    d_win = _wgrad(dproj, sv["h1"], tk=1408, name=f"l{l}_proj_bw")
    finish_mix = exchange(l, "mix", dict(w_in=d_win, w_out=d_wout))
    dx0, dn1, dsc1, dsh1 = _norm_mod_bwd(sv["x"], dproj, dx1, p["norm1_g"], sc1, w=p["w_in"], name=f"l{l}_proj_b")
    dx0, _ = lax.optimization_barrier((dx0, (d_win, d_wout, d_up, d_down)))
    dmod = jnp.concatenate([dsh1, dsc1, dg1, dsh2, dsc2, dg2], axis=1)[0]
    small = dict(norm1_g=dn1[0], ssd_conv_w=dcw[:SSD_CONV_K], ssd_conv_b=dcb[0], ssd_dt_bias=ddb[0], ssd_a_log=dal[0],
                 ssd_d=dd[0], ssd_norm_g=dng[0], pool_w=_diag_blocks(dwbd), pool_scale=dpsc[0], norm2_g=dn2[0],
                 ffn_conv_w=dfcw[:FFN_CONV_K], ffn_conv_b=dfcb[0])
    return dx0, dmod, small, finish_mix


def _example_step(x, target, pos_col, inv_freq_lane, mod, gather, small, final_g, exchange):
    tabs = _rope_tables(pos_col, inv_freq_lane, name="rope_tables")
    params, saved = [], []
    for l in range(DEPTH):
        params.append(_layer_params(l, small))
        x, sv = _layer_fwd(x, mod[l], params[l], tabs, l, gather)
        saved.append(sv)
    loss_row, dx, dfg = _final_loss(x, final_g[None], target, name="final_loss")
    dmods, smalls = [None] * DEPTH, [None] * DEPTH
    for l in reversed(range(DEPTH)):
        dx, dmods[l], smalls[l], finish_mix = _layer_bwd(dx, saved[l], mod[l], params[l], tabs, l, exchange)
        if l > 0:
            finish_mix(dx)
    return loss_row, dx, jnp.stack(dmods, axis=0), smalls, dfg[0], finish_mix


_BIG = ("w_in", "w_out", "ffn_up", "ffn_down")
_SMALL_GRADS = ("norm1_g", "ssd_conv_w", "ssd_conv_b", "ssd_dt_bias", "ssd_a_log", "ssd_d", "ssd_norm_g", "pool_w",
                "pool_scale", "norm2_g", "ffn_conv_w", "ffn_conv_b")
_SMALL_PARAMS = ("ada_b", "norm1_g", "ssd_conv_w", "ssd_conv_b", "ssd_dt_bias", "ssd_a_log", "ssd_d", "ssd_norm_g",
                 "pool_w", "pool_scale", "norm2_g", "ffn_conv_w", "ffn_conv_b", "final_g")
_WEIGHT_ORDER = ("ada_w", "ada_b", "norm1_g", "w_in", "ssd_conv_w", "ssd_conv_b", "ssd_dt_bias", "ssd_a_log", "ssd_d",
                 "ssd_norm_g", "pool_w", "pool_scale", "w_out", "norm2_g", "ffn_up", "ffn_conv_w", "ffn_conv_b",
                 "ffn_down", "final_g")


_COLUMN_SHARDED = ("w_in", "ffn_up")
_GROUPS = (("mix", ("w_in", "w_out")), ("ffn", ("ffn_up", "ffn_down")))


def _big_shares(w, l, names):
    return [(w[name][l].T if name in _COLUMN_SHARDED else w[name][l]).astype(BF16) for name in names]


def _unshard_big(names, gathered):
    out = {}
    for name, g in zip(names, gathered):
        full = g.reshape(N_DEV * g.shape[1], g.shape[2])
        out[name] = _permute_w_in(full) if name == "w_in" else full
    return out


def _shard_big(grads):
    out = []
    for name, g in grads.items():
        g = _unpermute_w_in(g) if name == "w_in" else g
        out.append(g.reshape(N_DEV, g.shape[0] // N_DEV, g.shape[1]))
    return out


def kernel(x, c, positions, ada_w, ada_b, norm1_g, w_in, ssd_conv_w, ssd_conv_b, ssd_dt_bias, ssd_a_log, ssd_d, ssd_norm_g, pool_w, pool_scale, w_out, norm2_g, ffn_up, ffn_conv_w, ffn_conv_b, ffn_down, final_g, loss_target, m_ada_w, m_ada_b, m_norm1_g, m_w_in, m_ssd_conv_w, m_ssd_conv_b, m_ssd_dt_bias, m_ssd_a_log, m_ssd_d, m_ssd_norm_g, m_pool_w, m_pool_scale, m_w_out, m_norm2_g, m_ffn_up, m_ffn_conv_w, m_ffn_conv_b, m_ffn_down, m_final_g, v_ada_w, v_ada_b, v_norm1_g, v_w_in, v_ssd_conv_w, v_ssd_conv_b, v_ssd_dt_bias, v_ssd_a_log, v_ssd_d, v_ssd_norm_g, v_pool_w, v_pool_scale, v_w_out, v_norm2_g, v_ffn_up, v_ffn_conv_w, v_ffn_conv_b, v_ffn_down, v_final_g):
    w = dict(ada_w=ada_w, ada_b=ada_b, norm1_g=norm1_g, w_in=w_in, ssd_conv_w=ssd_conv_w, ssd_conv_b=ssd_conv_b,
             ssd_dt_bias=ssd_dt_bias, ssd_a_log=ssd_a_log, ssd_d=ssd_d, ssd_norm_g=ssd_norm_g, pool_w=pool_w,
             pool_scale=pool_scale, w_out=w_out, norm2_g=norm2_g, ffn_up=ffn_up, ffn_conv_w=ffn_conv_w,
             ffn_conv_b=ffn_conv_b, ffn_down=ffn_down, final_g=final_g)
    m = dict(ada_w=m_ada_w, ada_b=m_ada_b, norm1_g=m_norm1_g, w_in=m_w_in, ssd_conv_w=m_ssd_conv_w,
             ssd_conv_b=m_ssd_conv_b, ssd_dt_bias=m_ssd_dt_bias, ssd_a_log=m_ssd_a_log, ssd_d=m_ssd_d,
             ssd_norm_g=m_ssd_norm_g, pool_w=m_pool_w, pool_scale=m_pool_scale, w_out=m_w_out, norm2_g=m_norm2_g,
             ffn_up=m_ffn_up, ffn_conv_w=m_ffn_conv_w, ffn_conv_b=m_ffn_conv_b, ffn_down=m_ffn_down,
             final_g=m_final_g)
    v = dict(ada_w=v_ada_w, ada_b=v_ada_b, norm1_g=v_norm1_g, w_in=v_w_in, ssd_conv_w=v_ssd_conv_w,
             ssd_conv_b=v_ssd_conv_b, ssd_dt_bias=v_ssd_dt_bias, ssd_a_log=v_ssd_a_log, ssd_d=v_ssd_d,
             ssd_norm_g=v_ssd_norm_g, pool_w=v_pool_w, pool_scale=v_pool_scale, w_out=v_w_out, norm2_g=v_norm2_g,
             ffn_up=v_ffn_up, ffn_conv_w=v_ffn_conv_w, ffn_conv_b=v_ffn_conv_b, ffn_down=v_ffn_down,
             final_g=v_final_g)
    ix, iy, ic = _axes()
    dev = 4 * ix + 2 * iy + ic

    c_all, scw, fcw = _allgather([c, ssd_conv_w.reshape(DEPTH * SSD_CONV_K, -1),
                                  ffn_conv_w.reshape(DEPTH * FFN_CONV_K, -1)], name="gather_small")
    small_all = c_all
    c_all = c_all.reshape(N_DEV, D_MODEL)
    scw = scw.reshape(N_DEV, DEPTH, SSD_CONV_K, -1).transpose(1, 2, 0, 3).reshape(DEPTH, SSD_CONV_K, SSD_CONV_CH)
    fcw = fcw.reshape(N_DEV, DEPTH, FFN_CONV_K, -1).transpose(1, 2, 0, 3).reshape(DEPTH, FFN_CONV_K, 2 * FFN_DIM)

    mod_cols = _ada_mod(c_all, ada_w, name="ada_mod")
    mod_all = _allgather([mod_cols.reshape(DEPTH * N_DEV, _ADA_SHARD)], name="gather_mod")[0]
    mod_all = mod_all.reshape(N_DEV, DEPTH, N_DEV, _ADA_SHARD)
    mod_mine = lax.dynamic_index_in_dim(mod_all, dev, axis=2, keepdims=False)
    mod = _add_rows(mod_mine.transpose(1, 0, 2).reshape(DEPTH, 6 * D_MODEL), ada_b, name="ada_bias")

    fetched = {}

    def gather(l, group, after):
        if l < DEPTH and (l, group) not in fetched:
            names = dict(_GROUPS)[group]
            shares, _ = lax.optimization_barrier((_big_shares(w, l, names), small_all if after is None else after))
            got = _allgather_async(shares, name=f"gather_weights_l{l}_{group}",
                                   collective_id=1 + 2 * l + (group == "ffn"))
            fetched[l, group] = _unshard_big(names, got)
        return fetched.get((l, group))

    core = ic.astype(jnp.int32).reshape(1)
    from_chips = {}

    def exchange(l, group, g):
        cid = 5 + 4 * l + 2 * (group == "mix")
        blocks = _shard_big(g)
        if from_chips:
            blocks, _ = lax.optimization_barrier((blocks, list(from_chips.values())))
        from_sibling = _pair_exchange(blocks, name=f"grads_pair_exchange_l{l}_{group}", collective_id=cid)

        def finish(after):
            theirs, _ = lax.optimization_barrier((from_sibling, after))
            parts = _pair_sum(core, blocks, theirs, name=f"grads_pair_sum_l{l}_{group}")
            got = _chip_exchange(parts, name=f"grads_chip_exchange_l{l}_{group}", collective_id=cid + 1)
            from_chips.update({(l, name): t for name, t in zip(g, got)})

        return finish

    small = dict(norm1_g=norm1_g, norm2_g=norm2_g, ssd_conv_w=scw, ssd_conv_b=ssd_conv_b, ssd_dt_bias=ssd_dt_bias,
                 ssd_a_log=ssd_a_log, ssd_d=ssd_d, ssd_norm_g=ssd_norm_g, pool_w=pool_w, pool_scale=pool_scale,
                 ffn_conv_w=fcw, ffn_conv_b=ffn_conv_b)

    inv_freq = ROPE_THETA ** (-jnp.arange(0, ROT_DIM, 2, dtype=F32) / ROT_DIM)
    lane = jnp.arange(128) % HEAD_LANES
    inv_freq_lane = jnp.where(lane < ROT_DIM, inv_freq[lane % (ROT_DIM // 2)], 0.0)[None, :]
    pos_col = positions.reshape(SEQ, 1).astype(F32)
    loss_row, dx, dmod, g_small, g_final, finish_last = _example_step(
        x[0], loss_target[0], pos_col, inv_freq_lane, mod, gather, small, final_g, exchange)

    small_names = list(_SMALL_GRADS)
    stacked = [jnp.stack([g_small[l][name] for l in range(DEPTH)], axis=0) for name in small_names]
    small_parts = [loss_row, dmod] + [s.reshape(-1, s.shape[-1]) for s in stacked] + [g_final[None]]
    gathered = _allgather_async(small_parts, name="gather_small_grads", collective_id=13)
    finish_last(gathered[0])

    grads = {}
    summed = {}
    for name in _BIG:
        summed[name] = jnp.stack([_sum_blocks(from_chips[l, name], name=f"grads_chip_sum_l{l}_{name}")
                                  for l in range(DEPTH)], axis=0)
        grads[name] = summed[name].transpose(0, 2, 1) if name in _COLUMN_SHARDED else summed[name]

    total = _sum_gathered(gathered, name="sum_small_grads")
    loss = total[0][0, 0]
    grads["ada_b"] = total[1]
    grads.update(zip(small_names, total[2:-1]))
    grads["final_g"] = total[-1][0]
    dmod_cols = lax.dynamic_slice_in_dim(gathered[1], dev * _ADA_SHARD, _ADA_SHARD, axis=2).transpose(1, 0, 2)
    grads["ada_w"] = _ada_wgrad(c_all, dmod_cols, name="ada_wgrad")
    for name in ("ssd_dt_bias", "ssd_a_log", "ssd_d"):
        grads[name] = grads[name][:, :SSD_HEADS]
    grads["pool_w"] = grads["pool_w"].reshape(pool_w.shape)
    grads["ssd_conv_w"] = lax.dynamic_slice_in_dim(
        grads["ssd_conv_w"].reshape(DEPTH, SSD_CONV_K, SSD_CONV_CH), dev * ssd_conv_w.shape[2], ssd_conv_w.shape[2], axis=2)
    grads["ffn_conv_w"] = lax.dynamic_slice_in_dim(
        grads["ffn_conv_w"].reshape(DEPTH, FFN_CONV_K, 2 * FFN_DIM), dev * ffn_conv_w.shape[2], ffn_conv_w.shape[2], axis=2)

    delta, new_m, new_v = {}, {}, {}
    for name, tr in (("ada_w", 512), ("w_out", 256), ("ffn_down", 352)):
        shp = w[name].shape
        two_d = lambda a: a.reshape(shp[0] * shp[1], shp[2])
        d_, m_, v_ = _adamw(two_d(w[name]), two_d(grads[name]), two_d(m[name]), two_d(v[name]), tr=tr,
                               name=f"adamw_{name}")
        delta[name], new_m[name], new_v[name] = (t.reshape(shp) for t in (d_, m_, v_))
    for name, tr in (("w_in", 321), ("ffn_up", 352)):
        flip = lambda a: a.transpose(0, 2, 1)
        outs = _adamw_stack(flip(w[name]), summed[name], flip(m[name]), flip(v[name]), tr=tr, name=f"adamw_{name}")
        delta[name], new_m[name], new_v[name] = (flip(t) for t in outs)
    two_d = lambda a: a.reshape(-1, a.shape[-1])
    outs = _adamw_small(*[[two_d(t[name]) for name in _SMALL_PARAMS] for t in (w, grads, m, v)], name="adamw_small")
    for name, d_, m_, v_ in zip(_SMALL_PARAMS, *outs):
        delta[name], new_m[name], new_v[name] = (t.reshape(w[name].shape) for t in (d_, m_, v_))

    grad_x = dx[None]
    return (loss, grad_x, *[grads[n].reshape(w[n].shape) for n in _WEIGHT_ORDER],
            *[delta[n] for n in _WEIGHT_ORDER], *[new_m[n] for n in _WEIGHT_ORDER],
            *[new_v[n] for n in _WEIGHT_ORDER])
```

```python
import functools
import math

import jax
import jax.numpy as jnp
from jax import lax
from jax.experimental import pallas as pl
from jax.experimental.pallas import tpu as pltpu
from jax.experimental.pallas import tpu_sc as plsc

F32 = jnp.float32
BF16 = jnp.bfloat16

N_DEV = 8
D_MODEL = 1024
SEQ = 4096
DEPTH = 2
SSD_INNER = 512
SSD_HEADS = 8
SSD_HPG = 4
SSD_STATE = 128
SSD_CHUNK = 256
SSD_CONV_K = 4
SSD_CONV_CH = 1024
POOL_W = 256
POOL_WINDOWS = (2, 4, 8, 16)
ATT_W = 256
ATT_PATTERNS = ((128, 1), (512, 4), (2048, 16))
ATT_BLOCK = 128
ROT_DIM = 16
ROPE_THETA = 500000.0
IN_W = 2568
FFN_DIM = 2816
FFN_CONV_K = 3
NORM_EPS = 1e-6
HEAD_LANES = 64

ADAM_LR = 0.001
ADAM_B1 = 0.9
ADAM_B2 = 0.999
ADAM_EPS = 1e-08
ADAM_WD = 0.01
ADAM_STEP = 10

PROJ_W = 2816
PROJ_SSD_W = 1792
PROJ_Z_BLK = 2
PROJ_DT_BLK = 12
PROJ_POOL_BLK = 7
PROJ_Q_BLK, PROJ_K_BLK, PROJ_V_BLK = 16, 18, 20
MIX_POOL_BLK = 2
MIX_ATT_BLK = 6
VMEM_LIMIT = 56 * 1024 * 1024
ROW_TILE = 512
CONV_HALO = 8
POOL_HALO = 16
ATT_KPAD = ATT_BLOCK * 16
MESH = pl.DeviceIdType.MESH
_HBM = pl.BlockSpec(memory_space=pl.ANY)


def _cparams(*sem):
    return pltpu.CompilerParams(dimension_semantics=sem, vmem_limit_bytes=VMEM_LIMIT)


def _resident(shape):
    return pl.BlockSpec(shape, lambda i: (0,) * len(shape), pipeline_mode=pl.Buffered(1))


def _silu(x):
    return x * jax.nn.sigmoid(x)


def _pick_lane(v, h):
    lane = lax.broadcasted_iota(jnp.int32, v.shape, 1)
    return jnp.sum(jnp.where(lane == h, v, 0.0), axis=1, keepdims=True)


def _pick_row(v, h):
    row = lax.broadcasted_iota(jnp.int32, v.shape, 0)
    return jnp.sum(jnp.where(row == h, v, 0.0), axis=0, keepdims=True)


def _head_of_lane(width):
    return lax.broadcasted_iota(jnp.int32, (1, width), 1) // HEAD_LANES


@functools.partial(jax.custom_vjp, nondiff_argnums=(1, 2))
def _shift_rows(x_ext, s, halo):
    y = x_ext if s == 0 else pltpu.roll(x_ext, s, 0)
    return y[halo:]


def _shift_rows_fwd(x_ext, s, halo):
    return _shift_rows(x_ext, s, halo), None


def _shift_rows_bwd(s, halo, _, g):
    ge = jnp.concatenate([jnp.zeros((halo, g.shape[1]), g.dtype), g], axis=0)
    return (ge if s == 0 else pltpu.roll(ge, ge.shape[0] - s, 0),)


_shift_rows.defvjp(_shift_rows_fwd, _shift_rows_bwd)


@functools.partial(jax.custom_vjp, nondiff_argnums=(1,))
def _roll_rows(x, s):
    return pltpu.roll(x, s, 0)


def _roll_rows_fwd(x, s):
    return _roll_rows(x, s), None


def _roll_rows_bwd(s, _, g):
    return (pltpu.roll(g, g.shape[0] - s, 0),)


_roll_rows.defvjp(_roll_rows_fwd, _roll_rows_bwd)


def _rms_modulate(xv, g, sc, sh):
    r = lax.rsqrt(jnp.mean(xv * xv, axis=-1, keepdims=True) + NORM_EPS)
    return (xv * r * g) * (1.0 + sc) + sh


def _mm(a, w, *, name, nt=False, tm=ROW_TILE, tn=None, out_dtype=F32, norm=None, residual=None):
    t, k = a.shape
    n = w.shape[0] if nt else w.shape[1]
    tn = tn or n
    assert tn == n or (norm is None and residual is None)
    extra_in = list(norm or ()) + list(residual or ())

    def body(*refs):
        a_ref, w_ref = refs[:2]
        ins = refs[2:2 + len(extra_in)]
        outs = refs[2 + len(extra_in):]
        if norm is None:
            av = a_ref[...].astype(BF16)
        else:
            av = _rms_modulate(a_ref[...], ins[0][...], ins[1][...], ins[2][...]).astype(BF16)
            outs[1][...] = av
        if nt:
            acc = lax.dot_general(av, w_ref[...], (((1,), (1,)), ((), ())), preferred_element_type=F32)
        else:
            acc = jnp.dot(av, w_ref[...], preferred_element_type=F32)
        outs[0][...] = acc.astype(out_dtype)
        if residual is not None:
            x_ref, gate_ref = ins[-2:]
            outs[-1][...] = x_ref[...] + gate_ref[...] * acc

    row = lambda width: pl.BlockSpec((1, width), lambda i, j: (0, 0))
    tile = lambda width: pl.BlockSpec((tm, width), lambda i, j: (i, 0))
    w_spec = pl.BlockSpec((tn, k), lambda i, j: (j, 0)) if nt else pl.BlockSpec((k, tn), lambda i, j: (0, j))
    in_specs = [tile(k), w_spec] + ([row(k)] * 3 if norm else []) + ([tile(n), row(n)] if residual else [])
    out_specs = [pl.BlockSpec((tm, tn), lambda i, j: (i, j))] + ([tile(k)] if norm else []) + \
        ([tile(n)] if residual else [])
    out_shape = [jax.ShapeDtypeStruct((t, n), out_dtype)] + \
        ([jax.ShapeDtypeStruct((t, k), BF16)] if norm else []) + \
        ([jax.ShapeDtypeStruct((t, n), F32)] if residual else [])
    outs = pl.pallas_call(
        body, grid=(t // tm, n // tn), in_specs=in_specs, out_specs=out_specs, out_shape=out_shape, name=name,
        compiler_params=_cparams("parallel", "parallel"))(a, w, *extra_in)
    return outs[0] if len(outs) == 1 else outs


def _wgrad(a, b, *, name, tk=None, tn=None, tt=2048, out_dtype=BF16):
    t, k = a.shape
    n = b.shape[1]
    tk = tk or k
    tn = tn or n
    steps = t // tt

    def body(a_ref, b_ref, o_ref, acc_ref):
        s = pl.program_id(2)

        @pl.when(s == 0)
        def _():
            acc_ref[...] = jnp.zeros_like(acc_ref)

        acc_ref[...] += lax.dot_general(a_ref[...].astype(BF16), b_ref[...].astype(BF16),
                                        (((0,), (0,)), ((), ())), preferred_element_type=F32)

        @pl.when(s == steps - 1)
        def _():
            o_ref[...] = acc_ref[...].astype(out_dtype)

    return pl.pallas_call(
        body, grid=(k // tk, n // tn, steps),
        in_specs=[pl.BlockSpec((tt, tk), lambda i, j, s: (s, i)), pl.BlockSpec((tt, tn), lambda i, j, s: (s, j))],
        out_specs=pl.BlockSpec((tk, tn), lambda i, j, s: (i, j)),
        out_shape=jax.ShapeDtypeStruct((k, n), out_dtype),
        scratch_shapes=[pltpu.VMEM((tk, tn), F32)], name=name,
        compiler_params=_cparams("parallel", "parallel", "arbitrary"))(a, b)


def _norm_mod_bwd(x, dh, dres, g, sc, *, name, w=None, tm=ROW_TILE):
    s, d = x.shape
    steps = s // tm

    def body(x_ref, dh_ref, dres_ref, g_ref, sc_ref, *rest):
        w_ref = rest[0] if w is not None else None
        dx_ref, dg_ref, dsc_ref, dsh_ref, da_acc, dsh_acc = rest[-6:]
        i = pl.program_id(0)

        @pl.when(i == 0)
        def _():
            da_acc[...] = jnp.zeros_like(da_acc)
            dsh_acc[...] = jnp.zeros_like(dsh_acc)

        xv = x_ref[...]
        if w is None:
            dhv = dh_ref[...].astype(F32)
        else:
            dhv = jnp.dot(dh_ref[...], w_ref[...], preferred_element_type=F32)
        r = lax.rsqrt(jnp.mean(xv * xv, axis=-1, keepdims=True) + NORM_EPS)
        xhat = xv * r
        gain = g_ref[...] * (1.0 + sc_ref[...])
        dxhat = dhv * gain
        dx_ref[...] = dres_ref[...] + r * (dxhat - xhat * jnp.mean(dxhat * xhat, axis=-1, keepdims=True))
        da_acc[...] += jnp.sum(dhv * xhat, axis=0, keepdims=True)
        dsh_acc[...] += jnp.sum(dhv, axis=0, keepdims=True)

        @pl.when(i == steps - 1)
        def _():
            dg_ref[...] = da_acc[...] * (1.0 + sc_ref[...])
            dsc_ref[...] = da_acc[...] * g_ref[...]
            dsh_ref[...] = dsh_acc[...]

    row = pl.BlockSpec((1, d), lambda i: (0, 0))
    tile = pl.BlockSpec((tm, d), lambda i: (i, 0))
    row_shape = jax.ShapeDtypeStruct((1, d), F32)
    dh_spec = tile if w is None else pl.BlockSpec((tm, dh.shape[1]), lambda i: (i, 0))
    return pl.pallas_call(
        body, grid=(steps,), in_specs=[tile, dh_spec, tile, row, row] + ([] if w is None else [_resident(w.shape)]),
        out_specs=[tile, row, row, row],
        out_shape=[jax.ShapeDtypeStruct((s, d), F32), row_shape, row_shape, row_shape],
        scratch_shapes=[pltpu.VMEM((1, d), F32), pltpu.VMEM((1, d), F32)], name=name,
        compiler_params=_cparams("arbitrary"))(x, dh, dres, g, sc, *([] if w is None else [w]))


def _mid_bwd(x1, dh2, dx2, norm_g, sc, gate, mo, w_out, *, name, tm=ROW_TILE):
    s, d = x1.shape
    steps = s // tm

    def body(x_ref, dh_ref, dres_ref, ng_ref, sc_ref, g_ref, mo_ref, w_ref,
             dx_ref, dng_ref, dsc_ref, dsh_ref, dmix_ref, dmo_ref, dg_ref, da_acc, dsh_acc):
        i = pl.program_id(0)

        @pl.when(i == 0)
        def _():
            da_acc[...] = jnp.zeros_like(da_acc)
            dsh_acc[...] = jnp.zeros_like(dsh_acc)
            dg_ref[...] = jnp.zeros_like(dg_ref)

        xv = x_ref[...]
        dhv = dh_ref[...]
        r = lax.rsqrt(jnp.mean(xv * xv, axis=-1, keepdims=True) + NORM_EPS)
        xhat = xv * r
        dxhat = dhv * (ng_ref[...] * (1.0 + sc_ref[...]))
        dxv = dres_ref[...] + r * (dxhat - xhat * jnp.mean(dxhat * xhat, axis=-1, keepdims=True))
        dx_ref[...] = dxv
        da_acc[...] += jnp.sum(dhv * xhat, axis=0, keepdims=True)
        dsh_acc[...] += jnp.sum(dhv, axis=0, keepdims=True)
        dmo = (g_ref[...] * dxv).astype(BF16)
        dmo_ref[...] = dmo
        dg_ref[...] += jnp.sum(dxv * mo_ref[...], axis=0, keepdims=True)
        dmix_ref[...] = lax.dot_general(dmo, w_ref[...], (((1,), (1,)), ((), ())), preferred_element_type=F32)

        @pl.when(i == steps - 1)
        def _():
            dng_ref[...] = da_acc[...] * (1.0 + sc_ref[...])
            dsc_ref[...] = da_acc[...] * ng_ref[...]
            dsh_ref[...] = dsh_acc[...]

    tile = pl.BlockSpec((tm, d), lambda i: (i, 0))
    row = pl.BlockSpec((1, d), lambda i: (0, 0))
    mix_tile = pl.BlockSpec((tm, w_out.shape[0]), lambda i: (i, 0))
    row_shape = jax.ShapeDtypeStruct((1, d), F32)
    return pl.pallas_call(
        body, grid=(steps,), in_specs=[tile, tile, tile, row, row, row, tile, _resident(w_out.shape)],
        out_specs=[tile, row, row, row, mix_tile, tile, row],
        out_shape=[jax.ShapeDtypeStruct((s, d), F32), row_shape, row_shape, row_shape,
                   jax.ShapeDtypeStruct((s, w_out.shape[0]), F32), jax.ShapeDtypeStruct((s, d), BF16), row_shape],
        scratch_shapes=[pltpu.VMEM((1, d), F32), pltpu.VMEM((1, d), F32)], name=name,
        compiler_params=_cparams("arbitrary"))(x1, dh2, dx2, norm_g, sc, gate, mo, w_out)


def _final_loss(x, g, target, *, name, tm=ROW_TILE):
    s, d = x.shape
    steps = s // tm

    def body(x_ref, g_ref, t_ref, loss_ref, dx_ref, dg_ref, sq_acc):
        i = pl.program_id(0)

        @pl.when(i == 0)
        def _():
            sq_acc[...] = jnp.zeros_like(sq_acc)
            dg_ref[...] = jnp.zeros_like(dg_ref)

        xv = x_ref[...]
        r = lax.rsqrt(jnp.mean(xv * xv, axis=-1, keepdims=True) + NORM_EPS)
        xhat = xv * r
        err = xhat * g_ref[...] - t_ref[...]
        sq_acc[...] += jnp.sum(err * err, axis=0, keepdims=True)
        dy = err * (1.0 / d)
        dg_ref[...] += jnp.sum(dy * xhat, axis=0, keepdims=True)
        dxhat = dy * g_ref[...]
        dx_ref[...] = r * (dxhat - xhat * jnp.mean(dxhat * xhat, axis=-1, keepdims=True))

        @pl.when(i == steps - 1)
        def _():
            total = jnp.sum(sq_acc[...], axis=1, keepdims=True) * (0.5 / d)
            loss_ref[...] = jnp.broadcast_to(total, loss_ref.shape)

    tile = pl.BlockSpec((tm, d), lambda i: (i, 0))
    row = pl.BlockSpec((1, d), lambda i: (0, 0))
    return pl.pallas_call(
        body, grid=(steps,), in_specs=[tile, row, tile],
        out_specs=[pl.BlockSpec((1, 128), lambda i: (0, 0)), tile, row],
        out_shape=[jax.ShapeDtypeStruct((1, 128), F32), jax.ShapeDtypeStruct((s, d), F32),
                   jax.ShapeDtypeStruct((1, d), F32)],
        scratch_shapes=[pltpu.VMEM((1, d), F32)], name=name, compiler_params=_cparams("arbitrary"))(x, g, target)


def _ssd_chunk(z, xbc_ext, dt_raw, conv_w, conv_b, dt_bias, a_log, d_skip, norm_g, h_in):
    q = z.shape[0]
    gw = SSD_HPG * HEAD_LANES
    xc = conv_b
    for k in range(SSD_CONV_K):
        xc = xc + _pick_row(conv_w, k) * _shift_rows(xbc_ext, SSD_CONV_K - 1 - k, CONV_HALO)
    xc = _silu(xc)
    dt = jax.nn.softplus(dt_raw + dt_bias)
    da = dt * (-jnp.exp(a_log))
    ri = lax.broadcasted_iota(jnp.int32, (q, q), 0)
    ci = lax.broadcasted_iota(jnp.int32, (q, q), 1)
    causal = ri >= ci
    tril = causal.astype(F32)
    a_cum = jnp.dot(tril, da, preferred_element_type=F32, precision=lax.Precision.HIGHEST)
    a_cum_t = lax.dot_general(da, tril, (((0,), (1,)), ((), ())), preferred_element_type=F32,
                              precision=lax.Precision.HIGHEST)
    a_last = _pick_row(a_cum, q - 1)
    head = _head_of_lane(gw)
    ys, hs = [], []
    for g in range(2):
        xs = xc[:, gw * g:gw * (g + 1)]
        bm = xc[:, SSD_INNER + SSD_STATE * g:SSD_INNER + SSD_STATE * (g + 1)]
        cm = xc[:, SSD_INNER + 2 * SSD_STATE + SSD_STATE * g:SSD_INNER + 2 * SSD_STATE + SSD_STATE * (g + 1)]
        cb = lax.dot_general(cm.astype(BF16), bm.astype(BF16), (((1,), (1,)), ((), ())), preferred_element_type=F32)
        cols = [_pick_lane(a_cum, SSD_HPG * g + j) for j in range(SSD_HPG)]
        lasts = [_pick_lane(a_last, SSD_HPG * g + j) for j in range(SSD_HPG)]
        dt_exp = sum(jnp.where(head == j, _pick_lane(dt, SSD_HPG * g + j), 0.0) for j in range(SSD_HPG))
        d_exp = sum(jnp.where(head == j, _pick_lane(d_skip, SSD_HPG * g + j), 0.0) for j in range(SSD_HPG))
        e_cum = sum(jnp.where(head == j, jnp.exp(cols[j]), 0.0) for j in range(SSD_HPG))
        c_dec = sum(jnp.where(head == j, jnp.exp(lasts[j]), 0.0) for j in range(SSD_HPG))
        xsdt = (xs * dt_exp).astype(BF16)
        y_diag = jnp.zeros((q, gw), F32)
        st_new = jnp.zeros((SSD_STATE, gw), F32)
        for j in range(SSD_HPG):
            row = _pick_row(a_cum_t, SSD_HPG * g + j)
            lmat = jnp.exp(jnp.where(causal, cols[j] - row, -jnp.inf))
            r = jnp.dot((cb * lmat).astype(BF16), xsdt, preferred_element_type=F32)
            y_diag = y_diag + jnp.where(head == j, r, 0.0)
            bd = (bm * jnp.exp(lasts[j] - cols[j])).astype(BF16)
            st = lax.dot_general(bd, xsdt, (((0,), (0,)), ((), ())), preferred_element_type=F32)
            st_new = st_new + jnp.where(head == j, st, 0.0)
        y_off = jnp.dot(cm.astype(BF16), h_in[g].astype(BF16), preferred_element_type=F32) * e_cum
        hs.append(h_in[g] * c_dec + st_new)
        y = y_diag + y_off + d_exp * xs
        yz = y * _silu(z[:, gw * g:gw * (g + 1)])
        yz = yz * lax.rsqrt(jnp.mean(yz * yz, axis=-1, keepdims=True) + NORM_EPS)
        ys.append(yz * norm_g[:, gw * g:gw * (g + 1)])
    return jnp.concatenate(ys, axis=1), tuple(hs)


_SSD_NCHUNK = SEQ // SSD_CHUNK
_HALO_PER_CHUNK = SSD_CHUNK // CONV_HALO


def _ssd_param_specs(const):
    return [pl.BlockSpec((8, SSD_CONV_CH), const), pl.BlockSpec((1, SSD_CONV_CH), const),
            pl.BlockSpec((1, 128), const), pl.BlockSpec((1, 128), const), pl.BlockSpec((1, 128), const),
            pl.BlockSpec((1, SSD_INNER), const)]


def _ssd_fwd(proj, conv_w, conv_b, dt_bias, a_log, d_skip, norm_g, *, name):
    q = SSD_CHUNK

    def body(z_ref, xbc_ref, halo_ref, dt_ref, cw_ref, cb_ref, db_ref, al_ref, d_ref, ng_ref, y_ref, hs_ref, h_acc):
        i = pl.program_id(0)

        @pl.when(i == 0)
        def _():
            h_acc[...] = jnp.zeros_like(h_acc)

        halo = jnp.where(i == 0, 0.0, halo_ref[...])
        xbc_ext = jnp.concatenate([halo, xbc_ref[...]], axis=0)
        h_in = (h_acc[0], h_acc[1])
        hs_ref[0, 0] = h_in[0]
        hs_ref[0, 1] = h_in[1]
        y, h_out = _ssd_chunk(z_ref[...], xbc_ext, dt_ref[...], cw_ref[...], cb_ref[...], db_ref[...], al_ref[...],
                              d_ref[...], ng_ref[...], h_in)
        y_ref[...] = y.astype(BF16)
        h_acc[0] = h_out[0]
        h_acc[1] = h_out[1]

    const = lambda i: (0, 0)
    return pl.pallas_call(
        body, grid=(_SSD_NCHUNK,),
        in_specs=[pl.BlockSpec((q, SSD_INNER), lambda i: (i, PROJ_Z_BLK)),
                  pl.BlockSpec((q, SSD_CONV_CH), lambda i: (i, 0)),
                  pl.BlockSpec((CONV_HALO, SSD_CONV_CH), lambda i: (jnp.maximum(i * _HALO_PER_CHUNK - 1, 0), 0)),
                  pl.BlockSpec((q, 128), lambda i: (i, PROJ_DT_BLK))] + _ssd_param_specs(const),
        out_specs=[pl.BlockSpec((q, SSD_INNER), lambda i: (i, 0)),
                   pl.BlockSpec((1, 2, SSD_STATE, 256), lambda i: (i, 0, 0, 0))],
        out_shape=[jax.ShapeDtypeStruct((SEQ, D_MODEL), BF16),
                   jax.ShapeDtypeStruct((_SSD_NCHUNK, 2, SSD_STATE, 256), F32)],
        scratch_shapes=[pltpu.VMEM((2, SSD_STATE, 256), F32)], name=name,
        compiler_params=_cparams("arbitrary"))(proj, proj, proj, proj, conv_w, conv_b, dt_bias, a_log, d_skip, norm_g)


def _ssd_bwd(proj, hstates, dmix, conv_w, conv_b, dt_bias, a_log, d_skip, norm_g, *, name):
    q = SSD_CHUNK
    last = _SSD_NCHUNK - 1

    def body(z_ref, xbc_ref, halo_ref, dt_ref, hs_ref, dy_ref, cw_ref, cb_ref, db_ref, al_ref, d_ref, ng_ref,
             dp_ref, dcw_ref, dcb_ref, ddb_ref, dal_ref, dd_ref, dng_ref, dh_acc, dhalo_acc):
        i = pl.program_id(0)

        @pl.when(i == 0)
        def _():
            dh_acc[...] = jnp.zeros_like(dh_acc)
            dhalo_acc[...] = jnp.zeros_like(dhalo_acc)
            for r in (dcw_ref, dcb_ref, ddb_ref, dal_ref, dd_ref, dng_ref):
                r[...] = jnp.zeros_like(r)

        halo = jnp.where(i == last, 0.0, halo_ref[...])
        xbc_ext = jnp.concatenate([halo, xbc_ref[...]], axis=0)
        _, vjp = jax.vjp(_ssd_chunk, z_ref[...], xbc_ext, dt_ref[...], cw_ref[...], cb_ref[...], db_ref[...],
                         al_ref[...], d_ref[...], ng_ref[...], (hs_ref[0, 0], hs_ref[0, 1]))
        gz, gx, gdt, gcw, gcb, gdb, gal, gd, gng, gh = vjp((dy_ref[...], (dh_acc[0], dh_acc[1])))
        dxbc = jnp.concatenate([gx[CONV_HALO:q], gx[q:] + dhalo_acc[...]], axis=0)
        dp_ref[...] = jnp.concatenate([dxbc, gz, gdt, jnp.zeros_like(gdt)], axis=1).astype(BF16)
        dhalo_acc[...] = gx[:CONV_HALO]
        dh_acc[0] = gh[0]
        dh_acc[1] = gh[1]
        dcw_ref[...] += gcw
        dcb_ref[...] += gcb
        ddb_ref[...] += gdb
        dal_ref[...] += gal
        dd_ref[...] += gd
        dng_ref[...] += gng

    const = lambda i: (0, 0)
    rev = lambda i: last - i
    row = lambda n: jax.ShapeDtypeStruct((1, n), F32)
    return pl.pallas_call(
        body, grid=(_SSD_NCHUNK,),
        in_specs=[pl.BlockSpec((q, SSD_INNER), lambda i: (rev(i), PROJ_Z_BLK)),
                  pl.BlockSpec((q, SSD_CONV_CH), lambda i: (rev(i), 0)),
                  pl.BlockSpec((CONV_HALO, SSD_CONV_CH), lambda i: (jnp.maximum(rev(i) * _HALO_PER_CHUNK - 1, 0), 0)),
                  pl.BlockSpec((q, 128), lambda i: (rev(i), PROJ_DT_BLK)),
                  pl.BlockSpec((1, 2, SSD_STATE, 256), lambda i: (rev(i), 0, 0, 0)),
                  pl.BlockSpec((q, SSD_INNER), lambda i: (rev(i), 0))] + _ssd_param_specs(const),
        out_specs=[pl.BlockSpec((q, PROJ_SSD_W), lambda i: (rev(i), 0))] + _ssd_param_specs(const),
        out_shape=[jax.ShapeDtypeStruct((SEQ, PROJ_W), BF16), jax.ShapeDtypeStruct((8, SSD_CONV_CH), F32),
                   row(SSD_CONV_CH), row(128), row(128), row(128), row(SSD_INNER)],
        scratch_shapes=[pltpu.VMEM((2, SSD_STATE, 256), F32), pltpu.VMEM((CONV_HALO, SSD_CONV_CH), F32)], name=name,
        compiler_params=_cparams("arbitrary"))(proj, proj, proj, proj, hstates, dmix, conv_w, conv_b, dt_bias, a_log,
                                                d_skip, norm_g)


def _rope_tables(pos_col, inv_freq_lane, *, name):
    s = pos_col.shape[0]

    def body(p_ref, f_ref, c_ref, s1_ref, s2_ref):
        ang = p_ref[...] * f_ref[...]
        within = lax.broadcasted_iota(jnp.int32, ang.shape, 1) % HEAD_LANES
        half = ROT_DIM // 2
        c_ref[...] = jnp.where(within < ROT_DIM, jnp.cos(ang), 1.0)
        sn = jnp.sin(ang)
        s1_ref[...] = jnp.where(within < half, -sn, 0.0)
        s2_ref[...] = jnp.where((within >= half) & (within < ROT_DIM), sn, 0.0)

    shp = jax.ShapeDtypeStruct((s, 128), F32)
    return pl.pallas_call(body, out_shape=[shp, shp, shp], name=name,
                          compiler_params=pltpu.CompilerParams(vmem_limit_bytes=VMEM_LIMIT))(pos_col, inv_freq_lane)


def _rope(t, c, s1, s2):
    half = ROT_DIM // 2
    return t * c + pltpu.roll(t, 128 - half, 1) * s1 + pltpu.roll(t, half, 1) * s2


def _rope_t(g, c, s1, s2):
    half = ROT_DIM // 2
    return g * c + pltpu.roll(g * s1, half, 1) + pltpu.roll(g * s2, 128 - half, 1)


def _att_valid(b):
    qi = lax.broadcasted_iota(jnp.int32, (ATT_BLOCK, 2 * ATT_BLOCK), 0)
    kj = lax.broadcasted_iota(jnp.int32, (ATT_BLOCK, 2 * ATT_BLOCK), 1)
    rel = qi + ATT_BLOCK - kj
    return (rel >= 0) & (rel <= ATT_BLOCK) & (b * ATT_BLOCK + kj - ATT_BLOCK >= 0)


def _att_slices(i, d):
    if d == 1:
        qstart = pl.multiple_of(i * ATT_BLOCK, ATT_BLOCK)
        return i, pl.ds(qstart, ATT_BLOCK), pl.ds(pl.multiple_of(qstart - ATT_BLOCK + ATT_KPAD, ATT_BLOCK), 2 * ATT_BLOCK)
    r = i % d
    b = i // d
    qstart = r + d * ATT_BLOCK * b
    return b, pl.ds(qstart, ATT_BLOCK, stride=d), pl.ds(qstart - ATT_BLOCK * d + ATT_KPAD, 2 * ATT_BLOCK, stride=d)


_ATT_NBLK = SEQ // ATT_BLOCK
_ATT_SCALE = HEAD_LANES ** -0.5
_ATT_UNROLL_FWD = 8
_ATT_UNROLL = 8


def _att_fwd(proj, cos, sin1, sin2, mix, *, name):
    s = SEQ

    def body(q_ref, k_ref, v_ref, c_ref, s1_ref, s2_ref, _, o_ref, lse_ref, mix_ref, qs, ks, vs, acc, m_s, l_s):
        c, s1, s2 = c_ref[...], s1_ref[...], s2_ref[...]
        qs[...] = _rope(q_ref[...], c, s1, s2) * _ATT_SCALE
        zeros = jnp.zeros((ATT_KPAD, 128), F32)
        ks[pl.ds(0, ATT_KPAD), :] = zeros
        vs[pl.ds(0, ATT_KPAD), :] = zeros
        ks[pl.ds(ATT_KPAD, s), :] = _rope(k_ref[...], c, s1, s2)
        vs[pl.ds(ATT_KPAD, s), :] = v_ref[...]
        head0 = _head_of_lane(128) == 0

        for bi, (_, d) in enumerate(ATT_PATTERNS):
            def blk(i, carry, d=d, first=(bi == 0)):
                b, sq, sk = _att_slices(i, d)
                qb = qs[sq, :]
                kw = ks[sk, :].astype(BF16)
                vw = vs[sk, :].astype(BF16)
                valid = _att_valid(b)
                ms, ls, os_ = [], [], []
                for hh in range(2):
                    qh = jnp.where(head0 if hh == 0 else ~head0, qb, 0.0).astype(BF16)
                    sc = lax.dot_general(qh, kw, (((1,), (1,)), ((), ())), preferred_element_type=F32)
                    sc = jnp.where(valid, sc, -jnp.inf)
                    mb = jnp.max(sc, axis=1, keepdims=True)
                    p = jnp.exp(sc - mb)
                    ms.append(mb)
                    ls.append(jnp.sum(p, axis=1, keepdims=True))
                    os_.append(jnp.dot(p.astype(BF16), vw, preferred_element_type=F32))
                m_b = jnp.where(head0, ms[0], ms[1])
                l_b = jnp.where(head0, ls[0], ls[1])
                o_b = jnp.where(head0, os_[0], os_[1])
                if first:
                    m_s[sq, :] = m_b
                    l_s[sq, :] = l_b
                    acc[sq, :] = o_b
                else:
                    m_old = m_s[sq, :]
                    m_new = jnp.maximum(m_old, m_b)
                    a_old = jnp.exp(m_old - m_new)
                    a_b = jnp.exp(m_b - m_new)
                    m_s[sq, :] = m_new
                    l_s[sq, :] = l_s[sq, :] * a_old + l_b * a_b
                    acc[sq, :] = acc[sq, :] * a_old + o_b * a_b
                return carry

            lax.fori_loop(0, _ATT_NBLK, blk, 0, unroll=_ATT_UNROLL_FWD)

        out = acc[...] / l_s[...]
        o_ref[...] = out
        mix_ref[...] = out.astype(BF16)
        lse_ref[...] = m_s[...] + jnp.log(l_s[...])

    col = lambda base: pl.BlockSpec((s, 128), lambda p: (0, base + p))
    tab = pl.BlockSpec((s, 128), lambda p: (0, 0))
    big = pltpu.VMEM((ATT_KPAD + s, 128), F32)
    tok = pltpu.VMEM((s, 128), F32)
    return pl.pallas_call(
        body, grid=(2,), in_specs=[col(PROJ_Q_BLK), col(PROJ_K_BLK), col(PROJ_V_BLK), tab, tab, tab, _HBM],
        out_specs=[pl.BlockSpec((s, 128), lambda p: (0, p)), pl.BlockSpec((s, 128), lambda p: (0, p)),
                   col(MIX_ATT_BLK)],
        out_shape=[jax.ShapeDtypeStruct((s, ATT_W), F32), jax.ShapeDtypeStruct((s, ATT_W), F32),
                   jax.ShapeDtypeStruct(mix.shape, mix.dtype)],
        input_output_aliases={6: 2}, scratch_shapes=[tok, big, big, tok, tok, tok], name=name,
        compiler_params=_cparams("arbitrary"))(proj, proj, proj, cos, sin1, sin2, mix)


def _att_bwd(proj, cos, sin1, sin2, out, lse, dmix, dproj, *, name):
    s = SEQ

    def body(proj_ref, c_hbm, s1_hbm, s2_hbm, out_hbm, lse_hbm, dmix_hbm, _, dproj_hbm,
             c_ref, s1_ref, s2_ref, o_ref, lse_ref, do_ref, qs, ks, vs, dqs, dks, dvs, staged, sems):
        def start(copies):
            for cp in copies:
                cp.start()
            return copies

        def load(pair):
            lanes = pl.ds(128 * pair, 128)
            rows = pl.ds(ATT_KPAD, s)
            return start([
                pltpu.make_async_copy(proj_ref.at[:, pl.ds(128 * (PROJ_Q_BLK + pair), 128)], qs, sems.at[0]),
                pltpu.make_async_copy(proj_ref.at[:, pl.ds(128 * (PROJ_K_BLK + pair), 128)], ks.at[rows, :], sems.at[1]),
                pltpu.make_async_copy(proj_ref.at[:, pl.ds(128 * (PROJ_V_BLK + pair), 128)], vs.at[rows, :], sems.at[2]),
                pltpu.make_async_copy(out_hbm.at[:, lanes], o_ref, sems.at[3]),
                pltpu.make_async_copy(lse_hbm.at[:, lanes], lse_ref, sems.at[4]),
                pltpu.make_async_copy(dmix_hbm.at[:, pl.ds(128 * (MIX_ATT_BLK + pair), 128)], do_ref, sems.at[5])])

        tables = start([pltpu.make_async_copy(c_hbm, c_ref, sems.at[6]),
                        pltpu.make_async_copy(s1_hbm, s1_ref, sems.at[7]),
                        pltpu.make_async_copy(s2_hbm, s2_ref, sems.at[8])])
        loads = load(0)
        for cp in tables:
            cp.wait()
        head0 = _head_of_lane(128) == 0
        zeros = jnp.zeros((ATT_KPAD, 128), F32)
        for pair in range(2):
            for cp in loads:
                cp.wait()
            c, s1, s2 = c_ref[...], s1_ref[...], s2_ref[...]
            qs[...] = _rope(qs[...], c, s1, s2) * _ATT_SCALE
            ks[pl.ds(0, ATT_KPAD), :] = zeros
            vs[pl.ds(0, ATT_KPAD), :] = zeros
            ks[pl.ds(ATT_KPAD, s), :] = _rope(ks[pl.ds(ATT_KPAD, s), :], c, s1, s2)
            dqs[...] = jnp.zeros_like(dqs)
            dks[...] = jnp.zeros_like(dks)
            dvs[...] = jnp.zeros_like(dvs)

            for _, d in ATT_PATTERNS:
                def blk(i, carry, d=d):
                    b, sq, sk = _att_slices(i, d)
                    qb = qs[sq, :]
                    kw = ks[sk, :].astype(BF16)
                    vw = vs[sk, :].astype(BF16)
                    dob = do_ref[sq, :]
                    lse_b = lse_ref[sq, :]
                    dd = dob * o_ref[sq, :]
                    valid = _att_valid(b)
                    dq_b = jnp.zeros((ATT_BLOCK, 128), F32)
                    dk_w = jnp.zeros((2 * ATT_BLOCK, 128), F32)
                    dv_w = jnp.zeros((2 * ATT_BLOCK, 128), F32)
                    for hh in range(2):
                        hm = head0 if hh == 0 else ~head0
                        qh = jnp.where(hm, qb, 0.0).astype(BF16)
                        doh = jnp.where(hm, dob, 0.0).astype(BF16)
                        lse_h = _pick_lane(lse_b, hh * HEAD_LANES)
                        d_h = jnp.sum(jnp.where(hm, dd, 0.0), axis=1, keepdims=True)
                        sc = lax.dot_general(qh, kw, (((1,), (1,)), ((), ())), preferred_element_type=F32)
                        p = jnp.where(valid, jnp.exp(sc - lse_h), 0.0)
                        dp = lax.dot_general(doh, vw, (((1,), (1,)), ((), ())), preferred_element_type=F32)
                        ds = (p * (dp - d_h)).astype(BF16)
                        dq_b = dq_b + jnp.where(hm, jnp.dot(ds, kw, preferred_element_type=F32), 0.0)
                        dk_w = dk_w + lax.dot_general(ds, qh, (((0,), (0,)), ((), ())), preferred_element_type=F32)
                        dv_w = dv_w + lax.dot_general(p.astype(BF16), doh, (((0,), (0,)), ((), ())),
                                                      preferred_element_type=F32)
                    dqs[sq, :] += dq_b
                    dks[sk, :] += dk_w
                    dvs[sk, :] += dv_w
                    return carry

                lax.fori_loop(0, _ATT_NBLK, blk, 0, unroll=_ATT_UNROLL)

            staged[0] = _rope_t(dqs[...] * _ATT_SCALE, c, s1, s2).astype(BF16)
            staged[1] = _rope_t(dks[pl.ds(ATT_KPAD, s), :], c, s1, s2).astype(BF16)
            staged[2] = dvs[pl.ds(ATT_KPAD, s), :].astype(BF16)
            stores = start([
                pltpu.make_async_copy(staged.at[j], dproj_hbm.at[:, pl.ds(128 * (col + pair), 128)], sems.at[9 + j])
                for j, col in enumerate((PROJ_Q_BLK, PROJ_K_BLK, PROJ_V_BLK))])
            if pair == 0:
                loads = load(1)
            for cp in stores:
                cp.wait()

    big = pltpu.VMEM((ATT_KPAD + s, 128), F32)
    tok = pltpu.VMEM((s, 128), F32)
    return pl.pallas_call(
        body, in_specs=[_HBM] * 8, out_specs=_HBM, out_shape=jax.ShapeDtypeStruct(dproj.shape, dproj.dtype),
        input_output_aliases={7: 0},
        scratch_shapes=[tok] * 6 + [tok, big, big, tok, big, big, pltpu.VMEM((3, s, 128), BF16),
                                    pltpu.SemaphoreType.DMA((12,))], name=name,
        compiler_params=pltpu.CompilerParams(vmem_limit_bytes=VMEM_LIMIT))(
            proj, cos, sin1, sin2, out, lse, dmix, dproj)


_POOL_TM = 512
_POOL_NT = SEQ // _POOL_TM
_POOL_HALO_PER_TILE = _POOL_TM // POOL_HALO


def _pool_tile(u_ext, w_bd, scale, t0):
    s2 = u_ext + _roll_rows(u_ext, 1)
    s4 = s2 + _roll_rows(s2, 2)
    s8 = s4 + _roll_rows(s4, 4)
    s16 = s8 + _roll_rows(s8, 8)
    grp = _head_of_lane(POOL_W)
    sel = jnp.where(grp == 0, s2, jnp.where(grp == 1, s4, jnp.where(grp == 2, s8, s16)))[POOL_HALO:]
    t = sel.shape[0]
    pos = t0 + lax.broadcasted_iota(jnp.int32, (t, POOL_W), 0) + 1
    win = jnp.where(grp == 0, 2, jnp.where(grp == 1, 4, jnp.where(grp == 2, 8, 16)))
    cnt = jnp.minimum(pos, win).astype(F32)
    diff = sel / cnt - u_ext[POOL_HALO:]
    return jnp.dot(diff.astype(BF16), w_bd.astype(BF16), preferred_element_type=F32) * scale


def _pool_fwd(proj, w_bd, scale, mix, *, name):
    tm = _POOL_TM

    def body(u_ref, halo_ref, w_ref, sc_ref, _, y_ref):
        i = pl.program_id(0)
        halo = jnp.where(i == 0, 0.0, halo_ref[...])
        u_ext = jnp.concatenate([halo, u_ref[...]], axis=0)
        y_ref[...] = _pool_tile(u_ext, w_ref[...], sc_ref[...], i * tm).astype(BF16)

    return pl.pallas_call(
        body, grid=(_POOL_NT,),
        in_specs=[pl.BlockSpec((tm, POOL_W), lambda i: (i, PROJ_POOL_BLK)),
                  pl.BlockSpec((POOL_HALO, POOL_W),
                               lambda i: (jnp.maximum(i * _POOL_HALO_PER_TILE - 1, 0), PROJ_POOL_BLK)),
                  pl.BlockSpec((POOL_W, POOL_W), lambda i: (0, 0)), pl.BlockSpec((1, POOL_W), lambda i: (0, 0)), _HBM],
        out_specs=pl.BlockSpec((tm, POOL_W), lambda i: (i, MIX_POOL_BLK)),
        out_shape=jax.ShapeDtypeStruct(mix.shape, mix.dtype), input_output_aliases={4: 0}, name=name,
        compiler_params=_cparams("parallel"))(proj, proj, w_bd, scale, mix)


def _pool_bwd(proj, dmix, w_bd, scale, dproj, *, name):
    tm = _POOL_TM
    last = _POOL_NT - 1

    def body(u_ref, halo_ref, dy_ref, w_ref, sc_ref, _, du_ref, dw_ref, dsc_ref, dhalo_acc):
        i = pl.program_id(0)

        @pl.when(i == 0)
        def _():
            dhalo_acc[...] = jnp.zeros_like(dhalo_acc)
            dw_ref[...] = jnp.zeros_like(dw_ref)
            dsc_ref[...] = jnp.zeros_like(dsc_ref)

        tile = last - i
        halo = jnp.where(tile == 0, 0.0, halo_ref[...])
        u_ext = jnp.concatenate([halo, u_ref[...]], axis=0)
        _, vjp = jax.vjp(functools.partial(_pool_tile, t0=tile * tm), u_ext, w_ref[...], sc_ref[...])
        gu, gw, gs = vjp(dy_ref[...])
        du_ref[...] = jnp.concatenate([gu[POOL_HALO:tm], gu[tm:] + dhalo_acc[...]], axis=0).astype(BF16)
        dhalo_acc[...] = gu[:POOL_HALO]
        dw_ref[...] += gw
        dsc_ref[...] += gs

    rev = lambda i: last - i
    return pl.pallas_call(
        body, grid=(_POOL_NT,),
        in_specs=[pl.BlockSpec((tm, POOL_W), lambda i: (rev(i), PROJ_POOL_BLK)),
                  pl.BlockSpec((POOL_HALO, POOL_W),
                               lambda i: (jnp.maximum(rev(i) * _POOL_HALO_PER_TILE - 1, 0), PROJ_POOL_BLK)),
                  pl.BlockSpec((tm, POOL_W), lambda i: (rev(i), MIX_POOL_BLK)),
                  pl.BlockSpec((POOL_W, POOL_W), lambda i: (0, 0)), pl.BlockSpec((1, POOL_W), lambda i: (0, 0)), _HBM],
        out_specs=[pl.BlockSpec((tm, POOL_W), lambda i: (rev(i), PROJ_POOL_BLK)),
                   pl.BlockSpec((POOL_W, POOL_W), lambda i: (0, 0)), pl.BlockSpec((1, POOL_W), lambda i: (0, 0))],
        out_shape=[jax.ShapeDtypeStruct(dproj.shape, dproj.dtype), jax.ShapeDtypeStruct((POOL_W, POOL_W), F32),
                   jax.ShapeDtypeStruct((1, POOL_W), F32)],
        input_output_aliases={5: 0}, scratch_shapes=[pltpu.VMEM((POOL_HALO, POOL_W), F32)], name=name,
        compiler_params=_cparams("arbitrary"))(proj, proj, dmix, w_bd, scale, dproj)


_FFN_TM = 256
_FFN_NT = SEQ // _FFN_TM
_FFN_HALO_PER_TILE = _FFN_TM // CONV_HALO


def _ffn_act_tile(hid_ext, conv_w, conv_b):
    hc = conv_b
    for k in range(FFN_CONV_K):
        hc = hc + _pick_row(conv_w, k) * _shift_rows(hid_ext, FFN_CONV_K - 1 - k, CONV_HALO)
    return _silu(hc[:, :FFN_DIM]) * hc[:, FFN_DIM:]


def _ffn_fwd(x1, norm_g, sc, sh, gate, up_t, down, conv_w, conv_b, *, name):
    tm = _FFN_TM
    w = 2 * FFN_DIM
    d = D_MODEL

    def body(x_ref, ng_ref, sc_ref, sh_ref, g_ref, up_ref, dn_ref, cw_ref, cb_ref,
             h_ref, hid_ref, act_ref, f_ref, x2_ref, halo_acc):
        i = pl.program_id(0)
        h2 = _rms_modulate(x_ref[...], ng_ref[...], sc_ref[...], sh_ref[...]).astype(BF16)
        h_ref[...] = h2
        hid = lax.dot_general(h2, up_ref[...], (((1,), (1,)), ((), ())), preferred_element_type=F32)
        hid_ref[...] = hid
        halo = jnp.where(i == 0, 0.0, halo_acc[...])
        act = _ffn_act_tile(jnp.concatenate([halo, hid], axis=0), cw_ref[...], cb_ref[...]).astype(BF16)
        halo_acc[...] = hid[tm - CONV_HALO:]
        act_ref[...] = act
        f = jnp.dot(act, dn_ref[...], preferred_element_type=F32)
        f_ref[...] = f
        x2_ref[...] = x_ref[...] + g_ref[...] * f

    tile = lambda n: pl.BlockSpec((tm, n), lambda i: (i, 0))
    return pl.pallas_call(
        body, grid=(_FFN_NT,),
        in_specs=[tile(d)] + [_resident((1, d))] * 4 + [_resident((w, d)), _resident((FFN_DIM, d)),
                                                        _resident((8, w)), _resident((1, w))],
        out_specs=[tile(d), tile(w), tile(FFN_DIM), tile(d), tile(d)],
        out_shape=[jax.ShapeDtypeStruct((SEQ, d), BF16), jax.ShapeDtypeStruct((SEQ, w), F32),
                   jax.ShapeDtypeStruct((SEQ, FFN_DIM), BF16), jax.ShapeDtypeStruct((SEQ, d), F32),
                   jax.ShapeDtypeStruct((SEQ, d), F32)],
        scratch_shapes=[pltpu.VMEM((CONV_HALO, w), F32)], name=name,
        compiler_params=_cparams("arbitrary"))(x1, norm_g, sc, sh, gate, up_t, down, conv_w, conv_b)


def _ffn_bwd(dx2, gate, f, hid, up_t, down, conv_w, conv_b, *, name):
    tm = _FFN_TM
    w = 2 * FFN_DIM
    d = D_MODEL
    last = _FFN_NT - 1

    def body(dx_ref, g_ref, f_ref, h_ref, halo_ref, up_ref, dn_ref, cw_ref, cb_ref,
             df_ref, dg_ref, dh_ref, dh2_ref, dcw_ref, dcb_ref, dhalo_acc):
        i = pl.program_id(0)

        @pl.when(i == 0)
        def _():
            dhalo_acc[...] = jnp.zeros_like(dhalo_acc)
            dcw_ref[...] = jnp.zeros_like(dcw_ref)
            dcb_ref[...] = jnp.zeros_like(dcb_ref)
            dg_ref[...] = jnp.zeros_like(dg_ref)

        dxv = dx_ref[...]
        df = (g_ref[...] * dxv).astype(BF16)
        df_ref[...] = df
        dg_ref[...] += jnp.sum(dxv * f_ref[...], axis=0, keepdims=True)
        dact = lax.dot_general(df, dn_ref[...], (((1,), (1,)), ((), ())), preferred_element_type=F32)
        halo = jnp.where(i == last, 0.0, halo_ref[...])
        hid_ext = jnp.concatenate([halo, h_ref[...]], axis=0)
        _, vjp = jax.vjp(_ffn_act_tile, hid_ext, cw_ref[...], cb_ref[...])
        gh, gw, gb = vjp(dact)
        dhid = jnp.concatenate([gh[CONV_HALO:tm], gh[tm:] + dhalo_acc[...]], axis=0).astype(BF16)
        dhalo_acc[...] = gh[:CONV_HALO]
        dh_ref[...] = dhid
        dh2_ref[...] = jnp.dot(dhid, up_ref[...], preferred_element_type=F32)
        dcw_ref[...] += gw
        dcb_ref[...] += gb

    rev = lambda i: last - i
    tile = lambda n: pl.BlockSpec((tm, n), lambda i: (rev(i), 0))
    acc = lambda shape: pl.BlockSpec(shape, lambda i: (0, 0))
    return pl.pallas_call(
        body, grid=(_FFN_NT,),
        in_specs=[tile(d), _resident((1, d)), tile(d), tile(w),
                  pl.BlockSpec((CONV_HALO, w), lambda i: (jnp.maximum(rev(i) * _FFN_HALO_PER_TILE - 1, 0), 0)),
                  _resident((w, d)), _resident((FFN_DIM, d)), _resident((8, w)), _resident((1, w))],
        out_specs=[tile(d), acc((1, d)), tile(w), tile(d), acc((8, w)), acc((1, w))],
        out_shape=[jax.ShapeDtypeStruct((SEQ, d), BF16), jax.ShapeDtypeStruct((1, d), F32),
                   jax.ShapeDtypeStruct((SEQ, w), BF16), jax.ShapeDtypeStruct((SEQ, d), F32),
                   jax.ShapeDtypeStruct((8, w), F32), jax.ShapeDtypeStruct((1, w), F32)],
        scratch_shapes=[pltpu.VMEM((CONV_HALO, w), F32)], name=name,
        compiler_params=_cparams("arbitrary"))(dx2, gate, f, hid, hid, up_t, down, conv_w, conv_b)


def _axes():
    return lax.axis_index("x"), lax.axis_index("y"), lax.axis_index("c")


def _handshake(peers):
    barrier = pltpu.get_barrier_semaphore()
    for peer in peers:
        pl.semaphore_signal(barrier, inc=1, device_id=peer, device_id_type=MESH)
    pl.semaphore_wait(barrier, len(peers))


def _allgather_body(x_refs, out_refs, send_sems, recv_sems, local_sems, own_barrier):
    n = len(x_refs)
    x, y, c = _axes()
    me, sibling = (x, y, c), (x, y, 1 - c)
    chips = [(1 - x, y), (x, 1 - y), (1 - x, 1 - y)]
    if own_barrier:
        _handshake([sibling] + [(*chip, c) for chip in chips])

    def slot(a, px, py, pc):
        return out_refs[a].at[4 * px + 2 * py + pc]

    def copy(a, k, block, to, src=None):
        return pltpu.make_async_remote_copy(
            src_ref=slot(a, *block) if src is None else src, dst_ref=slot(a, *block),
            send_sem=send_sems.at[a, k], recv_sem=recv_sems.at[a, k], device_id=to, device_id_type=MESH)

    mines, firsts = [], []
    for a in range(n):
        mines.append(pltpu.make_async_copy(x_refs[a], slot(a, *me), local_sems.at[a]))
        mines[-1].start()
        first = [copy(a, 0, me, sibling, src=x_refs[a])]
        first += [copy(a, 1 + j, me, (*chip, c), src=x_refs[a]) for j, chip in enumerate(chips)]
        for cp in first:
            cp.start()
        firsts += first
    passed = []
    for j, chip in enumerate(chips):
        for a in range(n):
            copy(a, 1 + j, (*chip, c), me).wait_recv()
            passed.append(copy(a, 4 + j, (*chip, c), sibling))
            passed[-1].start()
    for a in range(n):
        copy(a, 0, sibling, me).wait_recv()
    for j, chip in enumerate(chips):
        for a in range(n):
            copy(a, 4 + j, (*chip, 1 - c), me).wait_recv()
    for cp in firsts + passed:
        cp.wait_send()
    for cp in mines:
        cp.wait()


def _allgather_sems(n):
    return [pltpu.SemaphoreType.DMA((n, 7)), pltpu.SemaphoreType.DMA((n, 7)), pltpu.SemaphoreType.DMA((n,))]


def _allgather(xs, *, name):
    n = len(xs)

    def body(*refs):
        _allgather_body(refs[:n], refs[n:2 * n], *refs[2 * n:], own_barrier=False)

    return pl.pallas_call(
        body, out_shape=[jax.ShapeDtypeStruct((N_DEV,) + xb.shape, xb.dtype) for xb in xs],
        in_specs=[_HBM] * n, out_specs=[_HBM] * n, scratch_shapes=_allgather_sems(n), name=name)(*xs)


def _allgather_async(xs, *, name, collective_id):
    n = len(xs)
    x_refs = [jax.new_ref(xb, memory_space=pltpu.MemorySpace.HBM) for xb in xs]
    out_refs = [jax.empty_ref(jax.ShapeDtypeStruct((N_DEV,) + xb.shape, xb.dtype), memory_space=pltpu.MemorySpace.HBM)
                for xb in xs]

    @pl.kernel(mesh=plsc.ScalarSubcoreMesh(axis_name="sequencer", num_cores=1), name=name,
               scratch_types=tuple(_allgather_sems(n)),
               compiler_params=pltpu.CompilerParams(collective_id=collective_id))
    def launch(send_sems, recv_sems, local_sems):
        _allgather_body(x_refs, out_refs, send_sems, recv_sems, local_sems, own_barrier=True)

    launch()
    return [r[...] for r in out_refs]


def _pair_exchange(blocks, *, name, collective_id):
    n = len(blocks)
    hbm = pltpu.MemorySpace.HBM
    in_refs = [jax.new_ref(b, memory_space=hbm) for b in blocks]
    out_refs = [jax.empty_ref(jax.ShapeDtypeStruct((4,) + b.shape[1:], b.dtype), memory_space=hbm) for b in blocks]

    @pl.kernel(mesh=plsc.ScalarSubcoreMesh(axis_name="sequencer", num_cores=1), name=name,
               scratch_types=(pltpu.SemaphoreType.DMA((n, 4)), pltpu.SemaphoreType.DMA((n, 4))),
               compiler_params=pltpu.CompilerParams(collective_id=collective_id))
    def launch(send_sems, recv_sems):
        x, y, c = _axes()
        _handshake([(x, y, 1 - c)])
        copies = [pltpu.make_async_remote_copy(
            src_ref=in_refs[a].at[2 * s + (1 - c)], dst_ref=out_refs[a].at[s], send_sem=send_sems.at[a, s],
            recv_sem=recv_sems.at[a, s], device_id=(x, y, 1 - c), device_id_type=MESH)
            for a in range(n) for s in range(4)]
        for cp in copies:
            cp.start()
        for cp in copies:
            cp.wait_recv()
        for cp in copies:
            cp.wait_send()

    launch()
    return [r[...] for r in out_refs]


def _chip_exchange(parts, *, name, collective_id):
    n = len(parts)
    hbm = pltpu.MemorySpace.HBM
    in_refs = [jax.new_ref(p, memory_space=hbm) for p in parts]
    out_refs = [jax.empty_ref(jax.ShapeDtypeStruct(p.shape, p.dtype), memory_space=hbm) for p in parts]

    @pl.kernel(mesh=plsc.ScalarSubcoreMesh(axis_name="sequencer", num_cores=1), name=name,
               scratch_types=(pltpu.SemaphoreType.DMA((n, 3)), pltpu.SemaphoreType.DMA((n, 3)),
                              pltpu.SemaphoreType.DMA((n,))),
               compiler_params=pltpu.CompilerParams(collective_id=collective_id))
    def launch(send_sems, recv_sems, local_sems):
        x, y, c = _axes()
        my_chip = 2 * x + y
        chips = [(1 - x, y), (x, 1 - y), (1 - x, 1 - y)]
        _handshake([(*chip, c) for chip in chips])
        locals_ = [pltpu.make_async_copy(in_refs[a].at[my_chip], out_refs[a].at[my_chip], local_sems.at[a])
                   for a in range(n)]
        for cp in locals_:
            cp.start()
        copies = [pltpu.make_async_remote_copy(
            src_ref=in_refs[a].at[2 * px + py], dst_ref=out_refs[a].at[my_chip], send_sem=send_sems.at[a, k],
            recv_sem=recv_sems.at[a, k], device_id=(px, py, c), device_id_type=MESH)
            for a in range(n) for k, (px, py) in enumerate(chips)]
        for cp in copies:
            cp.start()
        for cp in copies:
            cp.wait_recv()
        for cp in copies:
            cp.wait_send()
        for cp in locals_:
            cp.wait()

    launch()
    return [r[...] for r in out_refs]


def _pair_sum(core, blocks, from_sibling, *, name):
    n = len(blocks)

    def body(core_ref, *refs):
        for a_ref, b_ref, o_ref in zip(refs[:n], refs[n:2 * n], refs[2 * n:]):
            o_ref[...] = (a_ref[...].astype(F32) + b_ref[...].astype(F32)).astype(o_ref.dtype)

    mine = lambda b: pl.BlockSpec((1,) + b.shape[1:], lambda s, core_ref: (2 * s + core_ref[0], 0, 0))
    slot = lambda b: pl.BlockSpec((1,) + b.shape[1:], lambda s, core_ref: (s, 0, 0))
    return pl.pallas_call(
        body,
        grid_spec=pltpu.PrefetchScalarGridSpec(
            num_scalar_prefetch=1, grid=(4,),
            in_specs=[mine(b) for b in blocks] + [slot(b) for b in blocks], out_specs=[slot(b) for b in blocks]),
        out_shape=[jax.ShapeDtypeStruct(s.shape, s.dtype) for s in from_sibling], name=name,
        compiler_params=_cparams("parallel"))(core, *blocks, *from_sibling)


def _sum_blocks(a, *, name, tr=None):
    n, r, cdim = a.shape
    tr = tr or r

    def body(a_ref, o_ref):
        acc = a_ref[0].astype(F32)
        for k in range(1, n):
            acc = acc + a_ref[k].astype(F32)
        o_ref[...] = acc

    return pl.pallas_call(body, grid=(r // tr,), in_specs=[pl.BlockSpec((n, tr, cdim), lambda i: (0, i, 0))],
                          out_specs=pl.BlockSpec((tr, cdim), lambda i: (i, 0)),
                          out_shape=jax.ShapeDtypeStruct((r, cdim), F32), name=name,
                          compiler_params=_cparams("parallel"))(a)


def _sum_gathered(gathered, *, name):
    n = len(gathered)

    def body(*refs):
        for a_ref, o_ref in zip(refs[:n], refs[n:]):
            acc = a_ref[0]
            for k in range(1, N_DEV):
                acc = acc + a_ref[k]
            o_ref[...] = acc

    return pl.pallas_call(body, out_shape=[jax.ShapeDtypeStruct(g.shape[1:], F32) for g in gathered], name=name,
                          compiler_params=pltpu.CompilerParams(vmem_limit_bytes=VMEM_LIMIT))(*gathered)


_ADA_SHARD = 6 * D_MODEL // N_DEV


def _ada_mod(c_all, ada_w, *, name):
    def body(c_ref, w_ref, o_ref):
        o_ref[0] = jnp.dot(_silu(c_ref[...]).astype(BF16), w_ref[0].astype(BF16), preferred_element_type=F32)

    return pl.pallas_call(
        body, grid=(DEPTH,),
        in_specs=[pl.BlockSpec((N_DEV, D_MODEL), lambda l: (0, 0)),
                  pl.BlockSpec((1, D_MODEL, _ADA_SHARD), lambda l: (l, 0, 0))],
        out_specs=pl.BlockSpec((1, N_DEV, _ADA_SHARD), lambda l: (l, 0, 0)),
        out_shape=jax.ShapeDtypeStruct((DEPTH, N_DEV, _ADA_SHARD), F32), name=name,
        compiler_params=_cparams("parallel"))(c_all, ada_w)


def _ada_wgrad(c_all, dmod_cols, *, name):
    def body(c_ref, d_ref, o_ref):
        o_ref[0] = lax.dot_general(_silu(c_ref[...]), d_ref[0], (((0,), (0,)), ((), ())),
                                   preferred_element_type=F32, precision=lax.Precision.HIGHEST)

    return pl.pallas_call(
        body, grid=(DEPTH,),
        in_specs=[pl.BlockSpec((N_DEV, D_MODEL), lambda l: (0, 0)),
                  pl.BlockSpec((1, N_DEV, _ADA_SHARD), lambda l: (l, 0, 0))],
        out_specs=pl.BlockSpec((1, D_MODEL, _ADA_SHARD), lambda l: (l, 0, 0)),
        out_shape=jax.ShapeDtypeStruct((DEPTH, D_MODEL, _ADA_SHARD), F32), name=name,
        compiler_params=_cparams("parallel"))(c_all, dmod_cols)


def _add_rows(a, b, *, name):
    def body(a_ref, b_ref, o_ref):
        o_ref[...] = a_ref[...] + b_ref[...]

    return pl.pallas_call(body, out_shape=jax.ShapeDtypeStruct(a.shape, a.dtype), name=name)(a, b)


def _adamw_update(w_ref, g_ref, m_ref, v_ref, d_ref, mo_ref, vo_ref):
    gv = g_ref[...]
    mn = ADAM_B1 * m_ref[...] + (1.0 - ADAM_B1) * gv
    vn = ADAM_B2 * v_ref[...] + (1.0 - ADAM_B2) * (gv * gv)
    mo_ref[...] = mn
    vo_ref[...] = vn
    m_hat = mn / (1.0 - ADAM_B1 ** ADAM_STEP)
    v_hat = vn / (1.0 - ADAM_B2 ** ADAM_STEP)
    d_ref[...] = -ADAM_LR * (m_hat / (jnp.sqrt(v_hat) + ADAM_EPS) + ADAM_WD * w_ref[...])


def _adamw_small(ws, gs, ms, vs, *, name):
    n = len(ws)

    def body(*refs):
        ins, outs = refs[:4 * n], refs[4 * n:]
        for i in range(n):
            _adamw_update(ins[i], ins[n + i], ins[2 * n + i], ins[3 * n + i], outs[i], outs[n + i], outs[2 * n + i])

    shapes = [jax.ShapeDtypeStruct(a.shape, F32) for a in ws]
    outs = pl.pallas_call(body, out_shape=shapes * 3, name=name,
                          compiler_params=pltpu.CompilerParams(vmem_limit_bytes=VMEM_LIMIT))(*ws, *gs, *ms, *vs)
    return outs[:n], outs[n:2 * n], outs[2 * n:]


def _adamw_stack(w, g, m, v, *, name, tr):
    layers, r, cdim = w.shape
    body = functools.partial(_adamw_update)
    spec = pl.BlockSpec((1, tr, cdim), lambda l, i: (l, i, 0))
    shp = jax.ShapeDtypeStruct(w.shape, F32)
    return pl.pallas_call(body, grid=(layers, r // tr), in_specs=[spec] * 4, out_specs=[spec] * 3,
                          out_shape=[shp] * 3, name=name, compiler_params=_cparams("parallel", "parallel"))(w, g, m, v)


def _adamw(w, g, m, v, *, name, tr):
    r, cdim = w.shape
    body = functools.partial(_adamw_update)

    spec = pl.BlockSpec((tr, cdim), lambda i: (i, 0))
    shp = jax.ShapeDtypeStruct((r, cdim), F32)
    return pl.pallas_call(body, grid=(r // tr,), in_specs=[spec] * 4, out_specs=[spec] * 3, out_shape=[shp] * 3,
                          name=name, compiler_params=_cparams("parallel"))(w, g, m, v)


def _pad_rows(a, rows):
    return jnp.concatenate([a, jnp.zeros((rows - a.shape[0],) + a.shape[1:], a.dtype)], axis=0)


def _pad_lanes(a, lanes):
    return jnp.concatenate([a, jnp.zeros(a.shape[:-1] + (lanes - a.shape[-1],), a.dtype)], axis=-1)


def _permute_w_in(wt):
    return jnp.concatenate([wt[512:1536], wt[:512], wt[1536:1544],
                            jnp.zeros((PROJ_W - IN_W, wt.shape[1]), wt.dtype), wt[1544:]], axis=0)


def _unpermute_w_in(wp):
    return jnp.concatenate([wp[1024:1536], wp[:1024], wp[1536:1544], wp[PROJ_SSD_W:]], axis=0)


def _block_diag(w):
    rows = []
    for g in range(4):
        rows.append(jnp.concatenate([w[g] if k == g else jnp.zeros_like(w[g]) for k in range(4)], axis=1))
    return jnp.concatenate(rows, axis=0)


def _diag_blocks(wbd):
    return jnp.stack([wbd[64 * g:64 * (g + 1), 64 * g:64 * (g + 1)] for g in range(4)], axis=0)


def _layer_params(l, small):
    return dict(
        norm1_g=small["norm1_g"][l][None], norm2_g=small["norm2_g"][l][None],
        conv_w=_pad_rows(small["ssd_conv_w"][l], 8), conv_b=small["ssd_conv_b"][l][None],
        dt_bias=_pad_lanes(small["ssd_dt_bias"][l][None], 128), a_log=_pad_lanes(small["ssd_a_log"][l][None], 128),
        d_skip=_pad_lanes(small["ssd_d"][l][None], 128), ssd_norm_g=small["ssd_norm_g"][l][None],
        pool_bd=_block_diag(small["pool_w"][l]), pool_scale=small["pool_scale"][l][None],
        fcw=_pad_rows(small["ffn_conv_w"][l], 8), fcb=small["ffn_conv_b"][l][None])


def _mod_rows(mod_l):
    return [mod_l[None, D_MODEL * i:D_MODEL * (i + 1)] for i in range(6)]


def _layer_fwd(x, mod_l, p, tabs, l, gather):
    sh1, sc1, g1, sh2, sc2, g2 = _mod_rows(mod_l)
    mix_w = gather(l, "mix", None)
    p.update(w_in=mix_w["w_in"], w_out=mix_w["w_out"])
    proj, h1 = _mm(x, p["w_in"], nt=True, norm=(p["norm1_g"], sc1, sh1), name=f"l{l}_proj")
    ffn_w = gather(l, "ffn", proj)
    p.update(up=ffn_w["ffn_up"], down=ffn_w["ffn_down"])
    mix, hst = _ssd_fwd(proj, p["conv_w"], p["conv_b"], p["dt_bias"], p["a_log"], p["d_skip"], p["ssd_norm_g"],
                        name=f"l{l}_ssd")
    mix = _pool_fwd(proj, p["pool_bd"], p["pool_scale"], mix, name=f"l{l}_pool")
    y_att, lse, mix = _att_fwd(proj, *tabs, mix, name=f"l{l}_att")
    mo, x1 = _mm(mix, p["w_out"], residual=(x, g1), name=f"l{l}_out")
    gather(l + 1, "mix", (x1, p["up"]))
    h2, hid, act, f, x2 = _ffn_fwd(x1, p["norm2_g"], sc2, sh2, g2, p["up"], p["down"], p["fcw"], p["fcb"],
                                   name=f"l{l}_ffn")
    return x2, dict(x=x, h1=h1, proj=proj, hst=hst, y_att=y_att, lse=lse, mix=mix, mo=mo, x1=x1, h2=h2, hid=hid,
                    act=act, f=f)


def _layer_bwd(dx2, sv, mod_l, p, tabs, l, exchange):
    sh1, sc1, g1, sh2, sc2, g2 = _mod_rows(mod_l)
    df, dg2, dhid, dh2, dfcw, dfcb = _ffn_bwd(dx2, g2, sv["f"], sv["hid"], p["up"], p["down"], p["fcw"], p["fcb"],
                                              name=f"l{l}_ffn_b")
    d_down = _wgrad(sv["act"], df, tk=1408, name=f"l{l}_down_bw")
    d_up = _wgrad(dhid, sv["h2"], tk=1408, name=f"l{l}_up_bw")
    finish_ffn = exchange(l, "ffn", dict(ffn_up=d_up, ffn_down=d_down))
    dx1, dn2, dsc2, dsh2, dmix, dmo, dg1 = _mid_bwd(sv["x1"], dh2, dx2, p["norm2_g"], sc2, g1, sv["mo"], p["w_out"],
                                                    name=f"l{l}_mid_b")
    finish_ffn(dx1)
    d_wout = _wgrad(sv["mix"], dmo, name=f"l{l}_out_bw")
    dproj, dcw, dcb, ddb, dal, dd, dng = _ssd_bwd(
        sv["proj"], sv["hst"], dmix, p["conv_w"], p["conv_b"], p["dt_bias"], p["a_log"], p["d_skip"],
        p["ssd_norm_g"], name=f"l{l}_ssd_b")
    dproj, dwbd, dpsc = _pool_bwd(sv["proj"], dmix, p["pool_bd"], p["pool_scale"], dproj, name=f"l{l}_pool_b")
    dproj = _att_bwd(sv["proj"], *tabs, sv["y_att"], sv["lse"], dmix, dproj, name=f"l{l}_att_b")
    d_win = _wgrad(dproj, sv["h1"], tk=1408, name=f"l{l}_proj_bw")
    finish_mix = exchange(l, "mix", dict(w_in=d_win, w_out=d_wout))
    dx0, dn1, dsc1, dsh1 = _norm_mod_bwd(sv["x"], dproj, dx1, p["norm1_g"], sc1, w=p["w_in"], name=f"l{l}_proj_b")
    dx0, _ = lax.optimization_barrier((dx0, (d_win, d_wout, d_up, d_down)))
    dmod = jnp.concatenate([dsh1, dsc1, dg1, dsh2, dsc2, dg2], axis=1)[0]
    small = dict(norm1_g=dn1[0], ssd_conv_w=dcw[:SSD_CONV_K], ssd_conv_b=dcb[0], ssd_dt_bias=ddb[0], ssd_a_log=dal[0],
                 ssd_d=dd[0], ssd_norm_g=dng[0], pool_w=_diag_blocks(dwbd), pool_scale=dpsc[0], norm2_g=dn2[0],
                 ffn_conv_w=dfcw[:FFN_CONV_K], ffn_conv_b=dfcb[0])
    return dx0, dmod, small, finish_mix


def _example_step(x, target, pos_col, inv_freq_lane, mod, gather, small, final_g, exchange):
    tabs = _rope_tables(pos_col, inv_freq_lane, name="rope_tables")
    params, saved = [], []
    for l in range(DEPTH):
        params.append(_layer_params(l, small))
        x, sv = _layer_fwd(x, mod[l], params[l], tabs, l, gather)
        saved.append(sv)
    loss_row, dx, dfg = _final_loss(x, final_g[None], target, name="final_loss")
    dmods, smalls = [None] * DEPTH, [None] * DEPTH
    for l in reversed(range(DEPTH)):
        dx, dmods[l], smalls[l], finish_mix = _layer_bwd(dx, saved[l], mod[l], params[l], tabs, l, exchange)
        if l > 0:
            finish_mix(dx)
    return loss_row, dx, jnp.stack(dmods, axis=0), smalls, dfg[0], finish_mix


_BIG = ("w_in", "w_out", "ffn_up", "ffn_down")
_SMALL_GRADS = ("norm1_g", "ssd_conv_w", "ssd_conv_b", "ssd_dt_bias", "ssd_a_log", "ssd_d", "ssd_norm_g", "pool_w",
                "pool_scale", "norm2_g", "ffn_conv_w", "ffn_conv_b")
_SMALL_PARAMS = ("ada_b", "norm1_g", "ssd_conv_w", "ssd_conv_b", "ssd_dt_bias", "ssd_a_log", "ssd_d", "ssd_norm_g",
                 "pool_w", "pool_scale", "norm2_g", "ffn_conv_w", "ffn_conv_b", "final_g")
_WEIGHT_ORDER = ("ada_w", "ada_b", "norm1_g", "w_in", "ssd_conv_w", "ssd_conv_b", "ssd_dt_bias", "ssd_a_log", "ssd_d",
                 "ssd_norm_g", "pool_w", "pool_scale", "w_out", "norm2_g", "ffn_up", "ffn_conv_w", "ffn_conv_b",
                 "ffn_down", "final_g")


_COLUMN_SHARDED = ("w_in", "ffn_up")
_GROUPS = (("mix", ("w_in", "w_out")), ("ffn", ("ffn_up", "ffn_down")))


def _big_shares(w, l, names):
    return [(w[name][l].T if name in _COLUMN_SHARDED else w[name][l]).astype(BF16) for name in names]


def _unshard_big(names, gathered):
    out = {}
    for name, g in zip(names, gathered):
        full = g.reshape(N_DEV * g.shape[1], g.shape[2])
        out[name] = _permute_w_in(full) if name == "w_in" else full
    return out


def _shard_big(grads):
    out = []
    for name, g in grads.items():
        g = _unpermute_w_in(g) if name == "w_in" else g
        out.append(g.reshape(N_DEV, g.shape[0] // N_DEV, g.shape[1]))
    return out


def kernel(x, c, positions, ada_w, ada_b, norm1_g, w_in, ssd_conv_w, ssd_conv_b, ssd_dt_bias, ssd_a_log, ssd_d, ssd_norm_g, pool_w, pool_scale, w_out, norm2_g, ffn_up, ffn_conv_w, ffn_conv_b, ffn_down, final_g, loss_target, m_ada_w, m_ada_b, m_norm1_g, m_w_in, m_ssd_conv_w, m_ssd_conv_b, m_ssd_dt_bias, m_ssd_a_log, m_ssd_d, m_ssd_norm_g, m_pool_w, m_pool_scale, m_w_out, m_norm2_g, m_ffn_up, m_ffn_conv_w, m_ffn_conv_b, m_ffn_down, m_final_g, v_ada_w, v_ada_b, v_norm1_g, v_w_in, v_ssd_conv_w, v_ssd_conv_b, v_ssd_dt_bias, v_ssd_a_log, v_ssd_d, v_ssd_norm_g, v_pool_w, v_pool_scale, v_w_out, v_norm2_g, v_ffn_up, v_ffn_conv_w, v_ffn_conv_b, v_ffn_down, v_final_g):
    w = dict(ada_w=ada_w, ada_b=ada_b, norm1_g=norm1_g, w_in=w_in, ssd_conv_w=ssd_conv_w, ssd_conv_b=ssd_conv_b,
             ssd_dt_bias=ssd_dt_bias, ssd_a_log=ssd_a_log, ssd_d=ssd_d, ssd_norm_g=ssd_norm_g, pool_w=pool_w,
             pool_scale=pool_scale, w_out=w_out, norm2_g=norm2_g, ffn_up=ffn_up, ffn_conv_w=ffn_conv_w,
             ffn_conv_b=ffn_conv_b, ffn_down=ffn_down, final_g=final_g)
    m = dict(ada_w=m_ada_w, ada_b=m_ada_b, norm1_g=m_norm1_g, w_in=m_w_in, ssd_conv_w=m_ssd_conv_w,
             ssd_conv_b=m_ssd_conv_b, ssd_dt_bias=m_ssd_dt_bias, ssd_a_log=m_ssd_a_log, ssd_d=m_ssd_d,
             ssd_norm_g=m_ssd_norm_g, pool_w=m_pool_w, pool_scale=m_pool_scale, w_out=m_w_out, norm2_g=m_norm2_g,
             ffn_up=m_ffn_up, ffn_conv_w=m_ffn_conv_w, ffn_conv_b=m_ffn_conv_b, ffn_down=m_ffn_down,
             final_g=m_final_g)
    v = dict(ada_w=v_ada_w, ada_b=v_ada_b, norm1_g=v_norm1_g, w_in=v_w_in, ssd_conv_w=v_ssd_conv_w,
             ssd_conv_b=v_ssd_conv_b, ssd_dt_bias=v_ssd_dt_bias, ssd_a_log=v_ssd_a_log, ssd_d=v_ssd_d,
             ssd_norm_g=v_ssd_norm_g, pool_w=v_pool_w, pool_scale=v_pool_scale, w_out=v_w_out, norm2_g=v_norm2_g,
             ffn_up=v_ffn_up, ffn_conv_w=v_ffn_conv_w, ffn_conv_b=v_ffn_conv_b, ffn_down=v_ffn_down,
             final_g=v_final_g)
    ix, iy, ic = _axes()
    dev = 4 * ix + 2 * iy + ic

    c_all, scw, fcw = _allgather([c, ssd_conv_w.reshape(DEPTH * SSD_CONV_K, -1),
                                  ffn_conv_w.reshape(DEPTH * FFN_CONV_K, -1)], name="gather_small")
    small_all = c_all
    c_all = c_all.reshape(N_DEV, D_MODEL)
    scw = scw.reshape(N_DEV, DEPTH, SSD_CONV_K, -1).transpose(1, 2, 0, 3).reshape(DEPTH, SSD_CONV_K, SSD_CONV_CH)
    fcw = fcw.reshape(N_DEV, DEPTH, FFN_CONV_K, -1).transpose(1, 2, 0, 3).reshape(DEPTH, FFN_CONV_K, 2 * FFN_DIM)

    mod_cols = _ada_mod(c_all, ada_w, name="ada_mod")
    mod_all = _allgather([mod_cols.reshape(DEPTH * N_DEV, _ADA_SHARD)], name="gather_mod")[0]
    mod_all = mod_all.reshape(N_DEV, DEPTH, N_DEV, _ADA_SHARD)
    mod_mine = lax.dynamic_index_in_dim(mod_all, dev, axis=2, keepdims=False)
    mod = _add_rows(mod_mine.transpose(1, 0, 2).reshape(DEPTH, 6 * D_MODEL), ada_b, name="ada_bias")

    fetched = {}

    def gather(l, group, after):
        if l < DEPTH and (l, group) not in fetched:
            names = dict(_GROUPS)[group]
            shares, _ = lax.optimization_barrier((_big_shares(w, l, names), small_all if after is None else after))
            got = _allgather_async(shares, name=f"gather_weights_l{l}_{group}",
                                   collective_id=1 + 2 * l + (group == "ffn"))
            fetched[l, group] = _unshard_big(names, got)
        return fetched.get((l, group))

    core = ic.astype(jnp.int32).reshape(1)
    from_chips = {}

    def exchange(l, group, g):
        cid = 5 + 4 * l + 2 * (group == "mix")
        blocks = _shard_big(g)
        if from_chips:
            blocks, _ = lax.optimization_barrier((blocks, list(from_chips.values())))
        from_sibling = _pair_exchange(blocks, name=f"grads_pair_exchange_l{l}_{group}", collective_id=cid)

        def finish(after):
            theirs, _ = lax.optimization_barrier((from_sibling, after))
            parts = _pair_sum(core, blocks, theirs, name=f"grads_pair_sum_l{l}_{group}")
            got = _chip_exchange(parts, name=f"grads_chip_exchange_l{l}_{group}", collective_id=cid + 1)
            from_chips.update({(l, name): t for name, t in zip(g, got)})

        return finish

    small = dict(norm1_g=norm1_g, norm2_g=norm2_g, ssd_conv_w=scw, ssd_conv_b=ssd_conv_b, ssd_dt_bias=ssd_dt_bias,
                 ssd_a_log=ssd_a_log, ssd_d=ssd_d, ssd_norm_g=ssd_norm_g, pool_w=pool_w, pool_scale=pool_scale,
                 ffn_conv_w=fcw, ffn_conv_b=ffn_conv_b)

    inv_freq = ROPE_THETA ** (-jnp.arange(0, ROT_DIM, 2, dtype=F32) / ROT_DIM)
    lane = jnp.arange(128) % HEAD_LANES
    inv_freq_lane = jnp.where(lane < ROT_DIM, inv_freq[lane % (ROT_DIM // 2)], 0.0)[None, :]
    pos_col = positions.reshape(SEQ, 1).astype(F32)
    loss_row, dx, dmod, g_small, g_final, finish_last = _example_step(
        x[0], loss_target[0], pos_col, inv_freq_lane, mod, gather, small, final_g, exchange)

    small_names = list(_SMALL_GRADS)
    stacked = [jnp.stack([g_small[l][name] for l in range(DEPTH)], axis=0) for name in small_names]
    small_parts = [loss_row, dmod] + [s.reshape(-1, s.shape[-1]) for s in stacked] + [g_final[None]]
    gathered = _allgather_async(small_parts, name="gather_small_grads", collective_id=13)
    finish_last(gathered[0])

    grads = {}
    summed = {}
    for name in _BIG:
        summed[name] = jnp.stack([_sum_blocks(from_chips[l, name], name=f"grads_chip_sum_l{l}_{name}")
                                  for l in range(DEPTH)], axis=0)
        grads[name] = summed[name].transpose(0, 2, 1) if name in _COLUMN_SHARDED else summed[name]

    total = _sum_gathered(gathered, name="sum_small_grads")
    loss = total[0][0, 0]
    grads["ada_b"] = total[1]
    grads.update(zip(small_names, total[2:-1]))
    grads["final_g"] = total[-1][0]
    dmod_cols = lax.dynamic_slice_in_dim(gathered[1], dev * _ADA_SHARD, _ADA_SHARD, axis=2).transpose(1, 0, 2)
    grads["ada_w"] = _ada_wgrad(c_all, dmod_cols, name="ada_wgrad")
    for name in ("ssd_dt_bias", "ssd_a_log", "ssd_d"):
        grads[name] = grads[name][:, :SSD_HEADS]
    grads["pool_w"] = grads["pool_w"].reshape(pool_w.shape)
    grads["ssd_conv_w"] = lax.dynamic_slice_in_dim(
        grads["ssd_conv_w"].reshape(DEPTH, SSD_CONV_K, SSD_CONV_CH), dev * ssd_conv_w.shape[2], ssd_conv_w.shape[2], axis=2)
    grads["ffn_conv_w"] = lax.dynamic_slice_in_dim(
        grads["ffn_conv_w"].reshape(DEPTH, FFN_CONV_K, 2 * FFN_DIM), dev * ffn_conv_w.shape[2], ffn_conv_w.shape[2], axis=2)

    delta, new_m, new_v = {}, {}, {}
    for name, tr in (("ada_w", 512), ("w_out", 256), ("ffn_down", 352)):
        shp = w[name].shape
        two_d = lambda a: a.reshape(shp[0] * shp[1], shp[2])
        d_, m_, v_ = _adamw(two_d(w[name]), two_d(grads[name]), two_d(m[name]), two_d(v[name]), tr=tr,
                               name=f"adamw_{name}")
        delta[name], new_m[name], new_v[name] = (t.reshape(shp) for t in (d_, m_, v_))
    for name, tr in (("w_in", 321), ("ffn_up", 352)):
        flip = lambda a: a.transpose(0, 2, 1)
        outs = _adamw_stack(flip(w[name]), summed[name], flip(m[name]), flip(v[name]), tr=tr, name=f"adamw_{name}")
        delta[name], new_m[name], new_v[name] = (flip(t) for t in outs)
    two_d = lambda a: a.reshape(-1, a.shape[-1])
    outs = _adamw_small(*[[two_d(t[name]) for name in _SMALL_PARAMS] for t in (w, grads, m, v)], name="adamw_small")
    for name, d_, m_, v_ in zip(_SMALL_PARAMS, *outs):
        delta[name], new_m[name], new_v[name] = (t.reshape(w[name].shape) for t in (d_, m_, v_))

    grad_x = dx[None]
    return (loss, grad_x, *[grads[n].reshape(w[n].shape) for n in _WEIGHT_ORDER],
            *[delta[n] for n in _WEIGHT_ORDER], *[new_m[n] for n in _WEIGHT_ORDER],
            *[new_v[n] for n in _WEIGHT_ORDER])
```

```python
import functools
import math

import jax
import jax.numpy as jnp
from jax import lax
from jax.experimental import pallas as pl
from jax.experimental.pallas import tpu as pltpu
from jax.experimental.pallas import tpu_sc as plsc

F32 = jnp.float32
BF16 = jnp.bfloat16

N_DEV = 8
D_MODEL = 1024
SEQ = 4096
DEPTH = 2
SSD_INNER = 512
SSD_HEADS = 8
SSD_HPG = 4
SSD_STATE = 128
SSD_CHUNK = 256
SSD_CONV_K = 4
SSD_CONV_CH = 1024
POOL_W = 256
POOL_WINDOWS = (2, 4, 8, 16)
ATT_W = 256
ATT_PATTERNS = ((128, 1), (512, 4), (2048, 16))
ATT_BLOCK = 128
ROT_DIM = 16
ROPE_THETA = 500000.0
IN_W = 2568
FFN_DIM = 2816
FFN_CONV_K = 3
NORM_EPS = 1e-6
HEAD_LANES = 64

ADAM_LR = 0.001
ADAM_B1 = 0.9
ADAM_B2 = 0.999
ADAM_EPS = 1e-08
ADAM_WD = 0.01
ADAM_STEP = 10

PROJ_W = 2816
PROJ_SSD_W = 1792
PROJ_Z_BLK = 2
PROJ_DT_BLK = 12
PROJ_POOL_BLK = 7
PROJ_Q_BLK, PROJ_K_BLK, PROJ_V_BLK = 16, 18, 20
MIX_POOL_BLK = 2
MIX_ATT_BLK = 6
VMEM_LIMIT = 56 * 1024 * 1024
ROW_TILE = 512
CONV_HALO = 8
POOL_HALO = 16
ATT_KPAD = ATT_BLOCK * 16
MESH = pl.DeviceIdType.MESH
_HBM = pl.BlockSpec(memory_space=pl.ANY)


def _cparams(*sem):
    return pltpu.CompilerParams(dimension_semantics=sem, vmem_limit_bytes=VMEM_LIMIT)


def _resident(shape):
    return pl.BlockSpec(shape, lambda i: (0,) * len(shape), pipeline_mode=pl.Buffered(1))


def _silu(x):
    return x * jax.nn.sigmoid(x)


def _pick_lane(v, h):
    lane = lax.broadcasted_iota(jnp.int32, v.shape, 1)
    return jnp.sum(jnp.where(lane == h, v, 0.0), axis=1, keepdims=True)


def _pick_row(v, h):
    row = lax.broadcasted_iota(jnp.int32, v.shape, 0)
    return jnp.sum(jnp.where(row == h, v, 0.0), axis=0, keepdims=True)


def _head_of_lane(width):
    return lax.broadcasted_iota(jnp.int32, (1, width), 1) // HEAD_LANES


@functools.partial(jax.custom_vjp, nondiff_argnums=(1, 2))
def _shift_rows(x_ext, s, halo):
    y = x_ext if s == 0 else pltpu.roll(x_ext, s, 0)
    return y[halo:]


def _shift_rows_fwd(x_ext, s, halo):
    return _shift_rows(x_ext, s, halo), None


def _shift_rows_bwd(s, halo, _, g):
    ge = jnp.concatenate([jnp.zeros((halo, g.shape[1]), g.dtype), g], axis=0)
    return (ge if s == 0 else pltpu.roll(ge, ge.shape[0] - s, 0),)


_shift_rows.defvjp(_shift_rows_fwd, _shift_rows_bwd)


@functools.partial(jax.custom_vjp, nondiff_argnums=(1,))
def _roll_rows(x, s):
    return pltpu.roll(x, s, 0)


def _roll_rows_fwd(x, s):
    return _roll_rows(x, s), None


def _roll_rows_bwd(s, _, g):
    return (pltpu.roll(g, g.shape[0] - s, 0),)


_roll_rows.defvjp(_roll_rows_fwd, _roll_rows_bwd)


def _rms_modulate(xv, g, sc, sh):
    r = lax.rsqrt(jnp.mean(xv * xv, axis=-1, keepdims=True) + NORM_EPS)
    return (xv * r * g) * (1.0 + sc) + sh


def _mm(a, w, *, name, nt=False, tm=ROW_TILE, tn=None, out_dtype=F32, norm=None, residual=None):
    t, k = a.shape
    n = w.shape[0] if nt else w.shape[1]
    tn = tn or n
    assert tn == n or (norm is None and residual is None)
    extra_in = list(norm or ()) + list(residual or ())

    def body(*refs):
        a_ref, w_ref = refs[:2]
        ins = refs[2:2 + len(extra_in)]
        outs = refs[2 + len(extra_in):]
        if norm is None:
            av = a_ref[...].astype(BF16)
        else:
            av = _rms_modulate(a_ref[...], ins[0][...], ins[1][...], ins[2][...]).astype(BF16)
            outs[1][...] = av
        if nt:
            acc = lax.dot_general(av, w_ref[...], (((1,), (1,)), ((), ())), preferred_element_type=F32)
        else:
            acc = jnp.dot(av, w_ref[...], preferred_element_type=F32)
        outs[0][...] = acc.astype(out_dtype)
        if residual is not None:
            x_ref, gate_ref = ins[-2:]
            outs[-1][...] = x_ref[...] + gate_ref[...] * acc

    row = lambda width: pl.BlockSpec((1, width), lambda i, j: (0, 0))
    tile = lambda width: pl.BlockSpec((tm, width), lambda i, j: (i, 0))
    w_spec = pl.BlockSpec((tn, k), lambda i, j: (j, 0)) if nt else pl.BlockSpec((k, tn), lambda i, j: (0, j))
    in_specs = [tile(k), w_spec] + ([row(k)] * 3 if norm else []) + ([tile(n), row(n)] if residual else [])
    out_specs = [pl.BlockSpec((tm, tn), lambda i, j: (i, j))] + ([tile(k)] if norm else []) + \
        ([tile(n)] if residual else [])
    out_shape = [jax.ShapeDtypeStruct((t, n), out_dtype)] + \
        ([jax.ShapeDtypeStruct((t, k), BF16)] if norm else []) + \
        ([jax.ShapeDtypeStruct((t, n), F32)] if residual else [])
    outs = pl.pallas_call(
        body, grid=(t // tm, n // tn), in_specs=in_specs, out_specs=out_specs, out_shape=out_shape, name=name,
        compiler_params=_cparams("parallel", "parallel"))(a, w, *extra_in)
    return outs[0] if len(outs) == 1 else outs


def _wgrad(a, b, *, name, tk=None, tn=None, tt=2048, out_dtype=BF16):
    t, k = a.shape
    n = b.shape[1]
    tk = tk or k
    tn = tn or n
    steps = t // tt

    def body(a_ref, b_ref, o_ref, acc_ref):
        s = pl.program_id(2)

        @pl.when(s == 0)
        def _():
            acc_ref[...] = jnp.zeros_like(acc_ref)

        acc_ref[...] += lax.dot_general(a_ref[...].astype(BF16), b_ref[...].astype(BF16),
                                        (((0,), (0,)), ((), ())), preferred_element_type=F32)

        @pl.when(s == steps - 1)
        def _():
            o_ref[...] = acc_ref[...].astype(out_dtype)

    return pl.pallas_call(
        body, grid=(k // tk, n // tn, steps),
        in_specs=[pl.BlockSpec((tt, tk), lambda i, j, s: (s, i)), pl.BlockSpec((tt, tn), lambda i, j, s: (s, j))],
        out_specs=pl.BlockSpec((tk, tn), lambda i, j, s: (i, j)),
        out_shape=jax.ShapeDtypeStruct((k, n), out_dtype),
        scratch_shapes=[pltpu.VMEM((tk, tn), F32)], name=name,
        compiler_params=_cparams("parallel", "parallel", "arbitrary"))(a, b)


def _norm_mod_bwd(x, dh, dres, g, sc, *, name, w=None, tm=ROW_TILE):
    s, d = x.shape
    steps = s // tm

    def body(x_ref, dh_ref, dres_ref, g_ref, sc_ref, *rest):
        w_ref = rest[0] if w is not None else None
        dx_ref, dg_ref, dsc_ref, dsh_ref, da_acc, dsh_acc = rest[-6:]
        i = pl.program_id(0)

        @pl.when(i == 0)
        def _():
            da_acc[...] = jnp.zeros_like(da_acc)
            dsh_acc[...] = jnp.zeros_like(dsh_acc)

        xv = x_ref[...]
        if w is None:
            dhv = dh_ref[...].astype(F32)
        else:
            dhv = jnp.dot(dh_ref[...], w_ref[...], preferred_element_type=F32)
        r = lax.rsqrt(jnp.mean(xv * xv, axis=-1, keepdims=True) + NORM_EPS)
        xhat = xv * r
        gain = g_ref[...] * (1.0 + sc_ref[...])
        dxhat = dhv * gain
        dx_ref[...] = dres_ref[...] + r * (dxhat - xhat * jnp.mean(dxhat * xhat, axis=-1, keepdims=True))
        da_acc[...] += jnp.sum(dhv * xhat, axis=0, keepdims=True)
        dsh_acc[...] += jnp.sum(dhv, axis=0, keepdims=True)

        @pl.when(i == steps - 1)
        def _():
            dg_ref[...] = da_acc[...] * (1.0 + sc_ref[...])
            dsc_ref[...] = da_acc[...] * g_ref[...]
            dsh_ref[...] = dsh_acc[...]

    row = pl.BlockSpec((1, d), lambda i: (0, 0))
    tile = pl.BlockSpec((tm, d), lambda i: (i, 0))
    row_shape = jax.ShapeDtypeStruct((1, d), F32)
    dh_spec = tile if w is None else pl.BlockSpec((tm, dh.shape[1]), lambda i: (i, 0))
    return pl.pallas_call(
        body, grid=(steps,), in_specs=[tile, dh_spec, tile, row, row] + ([] if w is None else [_resident(w.shape)]),
        out_specs=[tile, row, row, row],
        out_shape=[jax.ShapeDtypeStruct((s, d), F32), row_shape, row_shape, row_shape],
        scratch_shapes=[pltpu.VMEM((1, d), F32), pltpu.VMEM((1, d), F32)], name=name,
        compiler_params=_cparams("arbitrary"))(x, dh, dres, g, sc, *([] if w is None else [w]))


def _mid_bwd(x1, dh2, dx2, norm_g, sc, gate, mo, w_out, *, name, tm=ROW_TILE):
    s, d = x1.shape
    steps = s // tm

    def body(x_ref, dh_ref, dres_ref, ng_ref, sc_ref, g_ref, mo_ref, w_ref,
             dx_ref, dng_ref, dsc_ref, dsh_ref, dmix_ref, dmo_ref, dg_ref, da_acc, dsh_acc):
        i = pl.program_id(0)

        @pl.when(i == 0)
        def _():
            da_acc[...] = jnp.zeros_like(da_acc)
            dsh_acc[...] = jnp.zeros_like(dsh_acc)
            dg_ref[...] = jnp.zeros_like(dg_ref)

        xv = x_ref[...]
        dhv = dh_ref[...]
        r = lax.rsqrt(jnp.mean(xv * xv, axis=-1, keepdims=True) + NORM_EPS)
        xhat = xv * r
        dxhat = dhv * (ng_ref[...] * (1.0 + sc_ref[...]))
        dxv = dres_ref[...] + r * (dxhat - xhat * jnp.mean(dxhat * xhat, axis=-1, keepdims=True))
        dx_ref[...] = dxv
        da_acc[...] += jnp.sum(dhv * xhat, axis=0, keepdims=True)
        dsh_acc[...] += jnp.sum(dhv, axis=0, keepdims=True)
        dmo = (g_ref[...] * dxv).astype(BF16)
        dmo_ref[...] = dmo
        dg_ref[...] += jnp.sum(dxv * mo_ref[...], axis=0, keepdims=True)
        dmix_ref[...] = lax.dot_general(dmo, w_ref[...], (((1,), (1,)), ((), ())), preferred_element_type=F32)

        @pl.when(i == steps - 1)
        def _():
            dng_ref[...] = da_acc[...] * (1.0 + sc_ref[...])
            dsc_ref[...] = da_acc[...] * ng_ref[...]
            dsh_ref[...] = dsh_acc[...]

    tile = pl.BlockSpec((tm, d), lambda i: (i, 0))
    row = pl.BlockSpec((1, d), lambda i: (0, 0))
    mix_tile = pl.BlockSpec((tm, w_out.shape[0]), lambda i: (i, 0))
    row_shape = jax.ShapeDtypeStruct((1, d), F32)
    return pl.pallas_call(
        body, grid=(steps,), in_specs=[tile, tile, tile, row, row, row, tile, _resident(w_out.shape)],
        out_specs=[tile, row, row, row, mix_tile, tile, row],
        out_shape=[jax.ShapeDtypeStruct((s, d), F32), row_shape, row_shape, row_shape,
                   jax.ShapeDtypeStruct((s, w_out.shape[0]), F32), jax.ShapeDtypeStruct((s, d), BF16), row_shape],
        scratch_shapes=[pltpu.VMEM((1, d), F32), pltpu.VMEM((1, d), F32)], name=name,
        compiler_params=_cparams("arbitrary"))(x1, dh2, dx2, norm_g, sc, gate, mo, w_out)


def _final_loss(x, g, target, *, name, tm=ROW_TILE):
    s, d = x.shape
    steps = s // tm

    def body(x_ref, g_ref, t_ref, loss_ref, dx_ref, dg_ref, sq_acc):
        i = pl.program_id(0)

        @pl.when(i == 0)
        def _():
            sq_acc[...] = jnp.zeros_like(sq_acc)
            dg_ref[...] = jnp.zeros_like(dg_ref)

        xv = x_ref[...]
        r = lax.rsqrt(jnp.mean(xv * xv, axis=-1, keepdims=True) + NORM_EPS)
        xhat = xv * r
        err = xhat * g_ref[...] - t_ref[...]
        sq_acc[...] += jnp.sum(err * err, axis=0, keepdims=True)
        dy = err * (1.0 / d)
        dg_ref[...] += jnp.sum(dy * xhat, axis=0, keepdims=True)
        dxhat = dy * g_ref[...]
        dx_ref[...] = r * (dxhat - xhat * jnp.mean(dxhat * xhat, axis=-1, keepdims=True))

        @pl.when(i == steps - 1)
        def _():
            total = jnp.sum(sq_acc[...], axis=1, keepdims=True) * (0.5 / d)
            loss_ref[...] = jnp.broadcast_to(total, loss_ref.shape)

    tile = pl.BlockSpec((tm, d), lambda i: (i, 0))
    row = pl.BlockSpec((1, d), lambda i: (0, 0))
    return pl.pallas_call(
        body, grid=(steps,), in_specs=[tile, row, tile],
        out_specs=[pl.BlockSpec((1, 128), lambda i: (0, 0)), tile, row],
        out_shape=[jax.ShapeDtypeStruct((1, 128), F32), jax.ShapeDtypeStruct((s, d), F32),
                   jax.ShapeDtypeStruct((1, d), F32)],
        scratch_shapes=[pltpu.VMEM((1, d), F32)], name=name, compiler_params=_cparams("arbitrary"))(x, g, target)


def _ssd_chunk(z, xbc_ext, dt_raw, conv_w, conv_b, dt_bias, a_log, d_skip, norm_g, h_in):
    q = z.shape[0]
    gw = SSD_HPG * HEAD_LANES
    xc = conv_b
    for k in range(SSD_CONV_K):
        xc = xc + _pick_row(conv_w, k) * _shift_rows(xbc_ext, SSD_CONV_K - 1 - k, CONV_HALO)
    xc = _silu(xc)
    dt = jax.nn.softplus(dt_raw + dt_bias)
    da = dt * (-jnp.exp(a_log))
    ri = lax.broadcasted_iota(jnp.int32, (q, q), 0)
    ci = lax.broadcasted_iota(jnp.int32, (q, q), 1)
    causal = ri >= ci
    tril = causal.astype(F32)
    a_cum = jnp.dot(tril, da, preferred_element_type=F32, precision=lax.Precision.HIGHEST)
    a_cum_t = lax.dot_general(da, tril, (((0,), (1,)), ((), ())), preferred_element_type=F32,
                              precision=lax.Precision.HIGHEST)
    a_last = _pick_row(a_cum, q - 1)
    head = _head_of_lane(gw)
    ys, hs = [], []
    for g in range(2):
        xs = xc[:, gw * g:gw * (g + 1)]
        bm = xc[:, SSD_INNER + SSD_STATE * g:SSD_INNER + SSD_STATE * (g + 1)]
        cm = xc[:, SSD_INNER + 2 * SSD_STATE + SSD_STATE * g:SSD_INNER + 2 * SSD_STATE + SSD_STATE * (g + 1)]
        cb = lax.dot_general(cm.astype(BF16), bm.astype(BF16), (((1,), (1,)), ((), ())), preferred_element_type=F32)
        cols = [_pick_lane(a_cum, SSD_HPG * g + j) for j in range(SSD_HPG)]
        lasts = [_pick_lane(a_last, SSD_HPG * g + j) for j in range(SSD_HPG)]
        dt_exp = sum(jnp.where(head == j, _pick_lane(dt, SSD_HPG * g + j), 0.0) for j in range(SSD_HPG))
        d_exp = sum(jnp.where(head == j, _pick_lane(d_skip, SSD_HPG * g + j), 0.0) for j in range(SSD_HPG))
        e_cum = sum(jnp.where(head == j, jnp.exp(cols[j]), 0.0) for j in range(SSD_HPG))
        c_dec = sum(jnp.where(head == j, jnp.exp(lasts[j]), 0.0) for j in range(SSD_HPG))
        xsdt = (xs * dt_exp).astype(BF16)
        y_diag = jnp.zeros((q, gw), F32)
        st_new = jnp.zeros((SSD_STATE, gw), F32)
        for j in range(SSD_HPG):
            row = _pick_row(a_cum_t, SSD_HPG * g + j)
            lmat = jnp.exp(jnp.where(causal, cols[j] - row, -jnp.inf))
            r = jnp.dot((cb * lmat).astype(BF16), xsdt, preferred_element_type=F32)
            y_diag = y_diag + jnp.where(head == j, r, 0.0)
            bd = (bm * jnp.exp(lasts[j] - cols[j])).astype(BF16)
            st = lax.dot_general(bd, xsdt, (((0,), (0,)), ((), ())), preferred_element_type=F32)
            st_new = st_new + jnp.where(head == j, st, 0.0)
        y_off = jnp.dot(cm.astype(BF16), h_in[g].astype(BF16), preferred_element_type=F32) * e_cum
        hs.append(h_in[g] * c_dec + st_new)
        y = y_diag + y_off + d_exp * xs
        yz = y * _silu(z[:, gw * g:gw * (g + 1)])
        yz = yz * lax.rsqrt(jnp.mean(yz * yz, axis=-1, keepdims=True) + NORM_EPS)
        ys.append(yz * norm_g[:, gw * g:gw * (g + 1)])
    return jnp.concatenate(ys, axis=1), tuple(hs)


_SSD_NCHUNK = SEQ // SSD_CHUNK
_HALO_PER_CHUNK = SSD_CHUNK // CONV_HALO


def _ssd_param_specs(const):
    return [pl.BlockSpec((8, SSD_CONV_CH), const), pl.BlockSpec((1, SSD_CONV_CH), const),
            pl.BlockSpec((1, 128), const), pl.BlockSpec((1, 128), const), pl.BlockSpec((1, 128), const),
            pl.BlockSpec((1, SSD_INNER), const)]


def _ssd_fwd(proj, conv_w, conv_b, dt_bias, a_log, d_skip, norm_g, *, name):
    q = SSD_CHUNK

    def body(z_ref, xbc_ref, halo_ref, dt_ref, cw_ref, cb_ref, db_ref, al_ref, d_ref, ng_ref, y_ref, hs_ref, h_acc):
        i = pl.program_id(0)

        @pl.when(i == 0)
        def _():
            h_acc[...] = jnp.zeros_like(h_acc)

        halo = jnp.where(i == 0, 0.0, halo_ref[...])
        xbc_ext = jnp.concatenate([halo, xbc_ref[...]], axis=0)
        h_in = (h_acc[0], h_acc[1])
        hs_ref[0, 0] = h_in[0]
        hs_ref[0, 1] = h_in[1]
        y, h_out = _ssd_chunk(z_ref[...], xbc_ext, dt_ref[...], cw_ref[...], cb_ref[...], db_ref[...], al_ref[...],
                              d_ref[...], ng_ref[...], h_in)
        y_ref[...] = y.astype(BF16)
        h_acc[0] = h_out[0]
        h_acc[1] = h_out[1]

    const = lambda i: (0, 0)
    return pl.pallas_call(
        body, grid=(_SSD_NCHUNK,),
        in_specs=[pl.BlockSpec((q, SSD_INNER), lambda i: (i, PROJ_Z_BLK)),
                  pl.BlockSpec((q, SSD_CONV_CH), lambda i: (i, 0)),
                  pl.BlockSpec((CONV_HALO, SSD_CONV_CH), lambda i: (jnp.maximum(i * _HALO_PER_CHUNK - 1, 0), 0)),
                  pl.BlockSpec((q, 128), lambda i: (i, PROJ_DT_BLK))] + _ssd_param_specs(const),
        out_specs=[pl.BlockSpec((q, SSD_INNER), lambda i: (i, 0)),
                   pl.BlockSpec((1, 2, SSD_STATE, 256), lambda i: (i, 0, 0, 0))],
        out_shape=[jax.ShapeDtypeStruct((SEQ, D_MODEL), BF16),
                   jax.ShapeDtypeStruct((_SSD_NCHUNK, 2, SSD_STATE, 256), F32)],
        scratch_shapes=[pltpu.VMEM((2, SSD_STATE, 256), F32)], name=name,
        compiler_params=_cparams("arbitrary"))(proj, proj, proj, proj, conv_w, conv_b, dt_bias, a_log, d_skip, norm_g)


def _ssd_bwd(proj, hstates, dmix, conv_w, conv_b, dt_bias, a_log, d_skip, norm_g, *, name):
    q = SSD_CHUNK
    last = _SSD_NCHUNK - 1

    def body(z_ref, xbc_ref, halo_ref, dt_ref, hs_ref, dy_ref, cw_ref, cb_ref, db_ref, al_ref, d_ref, ng_ref,
             dp_ref, dcw_ref, dcb_ref, ddb_ref, dal_ref, dd_ref, dng_ref, dh_acc, dhalo_acc):
        i = pl.program_id(0)

        @pl.when(i == 0)
        def _():
            dh_acc[...] = jnp.zeros_like(dh_acc)
            dhalo_acc[...] = jnp.zeros_like(dhalo_acc)
            for r in (dcw_ref, dcb_ref, ddb_ref, dal_ref, dd_ref, dng_ref):
                r[...] = jnp.zeros_like(r)

        halo = jnp.where(i == last, 0.0, halo_ref[...])
        xbc_ext = jnp.concatenate([halo, xbc_ref[...]], axis=0)
        _, vjp = jax.vjp(_ssd_chunk, z_ref[...], xbc_ext, dt_ref[...], cw_ref[...], cb_ref[...], db_ref[...],
                         al_ref[...], d_ref[...], ng_ref[...], (hs_ref[0, 0], hs_ref[0, 1]))
        gz, gx, gdt, gcw, gcb, gdb, gal, gd, gng, gh = vjp((dy_ref[...], (dh_acc[0], dh_acc[1])))
        dxbc = jnp.concatenate([gx[CONV_HALO:q], gx[q:] + dhalo_acc[...]], axis=0)
        dp_ref[...] = jnp.concatenate([dxbc, gz, gdt, jnp.zeros_like(gdt)], axis=1).astype(BF16)
        dhalo_acc[...] = gx[:CONV_HALO]
        dh_acc[0] = gh[0]
        dh_acc[1] = gh[1]
        dcw_ref[...] += gcw
        dcb_ref[...] += gcb
        ddb_ref[...] += gdb
        dal_ref[...] += gal
        dd_ref[...] += gd
        dng_ref[...] += gng

    const = lambda i: (0, 0)
    rev = lambda i: last - i
    row = lambda n: jax.ShapeDtypeStruct((1, n), F32)
    return pl.pallas_call(
        body, grid=(_SSD_NCHUNK,),
        in_specs=[pl.BlockSpec((q, SSD_INNER), lambda i: (rev(i), PROJ_Z_BLK)),
                  pl.BlockSpec((q, SSD_CONV_CH), lambda i: (rev(i), 0)),
                  pl.BlockSpec((CONV_HALO, SSD_CONV_CH), lambda i: (jnp.maximum(rev(i) * _HALO_PER_CHUNK - 1, 0), 0)),
                  pl.BlockSpec((q, 128), lambda i: (rev(i), PROJ_DT_BLK)),
                  pl.BlockSpec((1, 2, SSD_STATE, 256), lambda i: (rev(i), 0, 0, 0)),
                  pl.BlockSpec((q, SSD_INNER), lambda i: (rev(i), 0))] + _ssd_param_specs(const),
        out_specs=[pl.BlockSpec((q, PROJ_SSD_W), lambda i: (rev(i), 0))] + _ssd_param_specs(const),
        out_shape=[jax.ShapeDtypeStruct((SEQ, PROJ_W), BF16), jax.ShapeDtypeStruct((8, SSD_CONV_CH), F32),
                   row(SSD_CONV_CH), row(128), row(128), row(128), row(SSD_INNER)],
        scratch_shapes=[pltpu.VMEM((2, SSD_STATE, 256), F32), pltpu.VMEM((CONV_HALO, SSD_CONV_CH), F32)], name=name,
        compiler_params=_cparams("arbitrary"))(proj, proj, proj, proj, hstates, dmix, conv_w, conv_b, dt_bias, a_log,
                                                d_skip, norm_g)


def _rope_tables(pos_col, inv_freq_lane, *, name):
    s = pos_col.shape[0]

    def body(p_ref, f_ref, c_ref, s1_ref, s2_ref):
        ang = p_ref[...] * f_ref[...]
        within = lax.broadcasted_iota(jnp.int32, ang.shape, 1) % HEAD_LANES
        half = ROT_DIM // 2
        c_ref[...] = jnp.where(within < ROT_DIM, jnp.cos(ang), 1.0)
        sn = jnp.sin(ang)
        s1_ref[...] = jnp.where(within < half, -sn, 0.0)
        s2_ref[...] = jnp.where((within >= half) & (within < ROT_DIM), sn, 0.0)

    shp = jax.ShapeDtypeStruct((s, 128), F32)
    return pl.pallas_call(body, out_shape=[shp, shp, shp], name=name,
                          compiler_params=pltpu.CompilerParams(vmem_limit_bytes=VMEM_LIMIT))(pos_col, inv_freq_lane)


def _rope(t, c, s1, s2):
    half = ROT_DIM // 2
    return t * c + pltpu.roll(t, 128 - half, 1) * s1 + pltpu.roll(t, half, 1) * s2


def _rope_t(g, c, s1, s2):
    half = ROT_DIM // 2
    return g * c + pltpu.roll(g * s1, half, 1) + pltpu.roll(g * s2, 128 - half, 1)


def _att_valid(b):
    qi = lax.broadcasted_iota(jnp.int32, (ATT_BLOCK, 2 * ATT_BLOCK), 0)
    kj = lax.broadcasted_iota(jnp.int32, (ATT_BLOCK, 2 * ATT_BLOCK), 1)
    rel = qi + ATT_BLOCK - kj
    return (rel >= 0) & (rel <= ATT_BLOCK) & (b * ATT_BLOCK + kj - ATT_BLOCK >= 0)


def _att_slices(i, d):
    if d == 1:
        qstart = pl.multiple_of(i * ATT_BLOCK, ATT_BLOCK)
        return i, pl.ds(qstart, ATT_BLOCK), pl.ds(pl.multiple_of(qstart - ATT_BLOCK + ATT_KPAD, ATT_BLOCK), 2 * ATT_BLOCK)
    r = i % d
    b = i // d
    qstart = r + d * ATT_BLOCK * b
    return b, pl.ds(qstart, ATT_BLOCK, stride=d), pl.ds(qstart - ATT_BLOCK * d + ATT_KPAD, 2 * ATT_BLOCK, stride=d)


_ATT_NBLK = SEQ // ATT_BLOCK
_ATT_SCALE = HEAD_LANES ** -0.5
_ATT_UNROLL_FWD = 8
_ATT_UNROLL = 4


def _att_fwd(proj, cos, sin1, sin2, mix, *, name):
    s = SEQ

    def body(q_ref, k_ref, v_ref, c_ref, s1_ref, s2_ref, _, o_ref, lse_ref, mix_ref, qs, ks, vs, acc, m_s, l_s):
        c, s1, s2 = c_ref[...], s1_ref[...], s2_ref[...]
        qs[...] = _rope(q_ref[...], c, s1, s2) * _ATT_SCALE
        zeros = jnp.zeros((ATT_KPAD, 128), F32)
        ks[pl.ds(0, ATT_KPAD), :] = zeros
        vs[pl.ds(0, ATT_KPAD), :] = zeros
        ks[pl.ds(ATT_KPAD, s), :] = _rope(k_ref[...], c, s1, s2)
        vs[pl.ds(ATT_KPAD, s), :] = v_ref[...]
        head0 = _head_of_lane(128) == 0

        for bi, (_, d) in enumerate(ATT_PATTERNS):
            def blk(i, carry, d=d, first=(bi == 0)):
                b, sq, sk = _att_slices(i, d)
                qb = qs[sq, :]
                kw = ks[sk, :].astype(BF16)
                vw = vs[sk, :].astype(BF16)
                valid = _att_valid(b)
                ms, ls, os_ = [], [], []
                for hh in range(2):
                    qh = jnp.where(head0 if hh == 0 else ~head0, qb, 0.0).astype(BF16)
                    sc = lax.dot_general(qh, kw, (((1,), (1,)), ((), ())), preferred_element_type=F32)
                    sc = jnp.where(valid, sc, -jnp.inf)
                    mb = jnp.max(sc, axis=1, keepdims=True)
                    p = jnp.exp(sc - mb)
                    ms.append(mb)
                    ls.append(jnp.sum(p, axis=1, keepdims=True))
                    os_.append(jnp.dot(p.astype(BF16), vw, preferred_element_type=F32))
                m_b = jnp.where(head0, ms[0], ms[1])
                l_b = jnp.where(head0, ls[0], ls[1])
                o_b = jnp.where(head0, os_[0], os_[1])
                if first:
                    m_s[sq, :] = m_b
                    l_s[sq, :] = l_b
                    acc[sq, :] = o_b
                else:
                    m_old = m_s[sq, :]
                    m_new = jnp.maximum(m_old, m_b)
                    a_old = jnp.exp(m_old - m_new)
                    a_b = jnp.exp(m_b - m_new)
                    m_s[sq, :] = m_new
                    l_s[sq, :] = l_s[sq, :] * a_old + l_b * a_b
                    acc[sq, :] = acc[sq, :] * a_old + o_b * a_b
                return carry

            lax.fori_loop(0, _ATT_NBLK, blk, 0, unroll=_ATT_UNROLL_FWD)

        out = acc[...] / l_s[...]
        o_ref[...] = out
        mix_ref[...] = out.astype(BF16)
        lse_ref[...] = m_s[...] + jnp.log(l_s[...])

    col = lambda base: pl.BlockSpec((s, 128), lambda p: (0, base + p))
    tab = pl.BlockSpec((s, 128), lambda p: (0, 0))
    big = pltpu.VMEM((ATT_KPAD + s, 128), F32)
    tok = pltpu.VMEM((s, 128), F32)
    return pl.pallas_call(
        body, grid=(2,), in_specs=[col(PROJ_Q_BLK), col(PROJ_K_BLK), col(PROJ_V_BLK), tab, tab, tab, _HBM],
        out_specs=[pl.BlockSpec((s, 128), lambda p: (0, p)), pl.BlockSpec((s, 128), lambda p: (0, p)),
                   col(MIX_ATT_BLK)],
        out_shape=[jax.ShapeDtypeStruct((s, ATT_W), F32), jax.ShapeDtypeStruct((s, ATT_W), F32),
                   jax.ShapeDtypeStruct(mix.shape, mix.dtype)],
        input_output_aliases={6: 2}, scratch_shapes=[tok, big, big, tok, tok, tok], name=name,
        compiler_params=_cparams("arbitrary"))(proj, proj, proj, cos, sin1, sin2, mix)


def _att_bwd(proj, cos, sin1, sin2, out, lse, dmix, dproj, *, name):
    s = SEQ

    def body(proj_ref, c_hbm, s1_hbm, s2_hbm, out_hbm, lse_hbm, dmix_hbm, _, dproj_hbm,
             c_ref, s1_ref, s2_ref, o_ref, lse_ref, do_ref, qs, ks, vs, dqs, dks, dvs, staged, sems):
        def start(copies):
            for cp in copies:
                cp.start()
            return copies

        def load(pair):
            lanes = pl.ds(128 * pair, 128)
            rows = pl.ds(ATT_KPAD, s)
            return start([
                pltpu.make_async_copy(proj_ref.at[:, pl.ds(128 * (PROJ_Q_BLK + pair), 128)], qs, sems.at[0]),
                pltpu.make_async_copy(proj_ref.at[:, pl.ds(128 * (PROJ_K_BLK + pair), 128)], ks.at[rows, :], sems.at[1]),
                pltpu.make_async_copy(proj_ref.at[:, pl.ds(128 * (PROJ_V_BLK + pair), 128)], vs.at[rows, :], sems.at[2]),
                pltpu.make_async_copy(out_hbm.at[:, lanes], o_ref, sems.at[3]),
                pltpu.make_async_copy(lse_hbm.at[:, lanes], lse_ref, sems.at[4]),
                pltpu.make_async_copy(dmix_hbm.at[:, pl.ds(128 * (MIX_ATT_BLK + pair), 128)], do_ref, sems.at[5])])

        tables = start([pltpu.make_async_copy(c_hbm, c_ref, sems.at[6]),
                        pltpu.make_async_copy(s1_hbm, s1_ref, sems.at[7]),
                        pltpu.make_async_copy(s2_hbm, s2_ref, sems.at[8])])
        loads = load(0)
        for cp in tables:
            cp.wait()
        head0 = _head_of_lane(128) == 0
        zeros = jnp.zeros((ATT_KPAD, 128), F32)
        for pair in range(2):
            for cp in loads:
                cp.wait()
            c, s1, s2 = c_ref[...], s1_ref[...], s2_ref[...]
            qs[...] = _rope(qs[...], c, s1, s2) * _ATT_SCALE
            ks[pl.ds(0, ATT_KPAD), :] = zeros
            vs[pl.ds(0, ATT_KPAD), :] = zeros
            ks[pl.ds(ATT_KPAD, s), :] = _rope(ks[pl.ds(ATT_KPAD, s), :], c, s1, s2)
            dqs[...] = jnp.zeros_like(dqs)
            dks[...] = jnp.zeros_like(dks)
            dvs[...] = jnp.zeros_like(dvs)

            for _, d in ATT_PATTERNS:
                def blk(i, carry, d=d):
                    b, sq, sk = _att_slices(i, d)
                    qb = qs[sq, :]
                    kw = ks[sk, :].astype(BF16)
                    vw = vs[sk, :].astype(BF16)
                    dob = do_ref[sq, :]
                    lse_b = lse_ref[sq, :]
                    dd = dob * o_ref[sq, :]
                    valid = _att_valid(b)
                    dq_b = jnp.zeros((ATT_BLOCK, 128), F32)
                    dk_w = jnp.zeros((2 * ATT_BLOCK, 128), F32)
                    dv_w = jnp.zeros((2 * ATT_BLOCK, 128), F32)
                    for hh in range(2):
                        hm = head0 if hh == 0 else ~head0
                        qh = jnp.where(hm, qb, 0.0).astype(BF16)
                        doh = jnp.where(hm, dob, 0.0).astype(BF16)
                        lse_h = _pick_lane(lse_b, hh * HEAD_LANES)
                        d_h = jnp.sum(jnp.where(hm, dd, 0.0), axis=1, keepdims=True)
                        sc = lax.dot_general(qh, kw, (((1,), (1,)), ((), ())), preferred_element_type=F32)
                        p = jnp.where(valid, jnp.exp(sc - lse_h), 0.0)
                        dp = lax.dot_general(doh, vw, (((1,), (1,)), ((), ())), preferred_element_type=F32)
                        ds = (p * (dp - d_h)).astype(BF16)
                        dq_b = dq_b + jnp.where(hm, jnp.dot(ds, kw, preferred_element_type=F32), 0.0)
                        dk_w = dk_w + lax.dot_general(ds, qh, (((0,), (0,)), ((), ())), preferred_element_type=F32)
                        dv_w = dv_w + lax.dot_general(p.astype(BF16), doh, (((0,), (0,)), ((), ())),
                                                      preferred_element_type=F32)
                    dqs[sq, :] += dq_b
                    dks[sk, :] += dk_w
                    dvs[sk, :] += dv_w
                    return carry

                lax.fori_loop(0, _ATT_NBLK, blk, 0, unroll=_ATT_UNROLL)

            staged[0] = _rope_t(dqs[...] * _ATT_SCALE, c, s1, s2).astype(BF16)
            staged[1] = _rope_t(dks[pl.ds(ATT_KPAD, s), :], c, s1, s2).astype(BF16)
            staged[2] = dvs[pl.ds(ATT_KPAD, s), :].astype(BF16)
            stores = start([
                pltpu.make_async_copy(staged.at[j], dproj_hbm.at[:, pl.ds(128 * (col + pair), 128)], sems.at[9 + j])
                for j, col in enumerate((PROJ_Q_BLK, PROJ_K_BLK, PROJ_V_BLK))])
            if pair == 0:
                loads = load(1)
            for cp in stores:
                cp.wait()

    big = pltpu.VMEM((ATT_KPAD + s, 128), F32)
    tok = pltpu.VMEM((s, 128), F32)
    return pl.pallas_call(
        body, in_specs=[_HBM] * 8, out_specs=_HBM, out_shape=jax.ShapeDtypeStruct(dproj.shape, dproj.dtype),
        input_output_aliases={7: 0},
        scratch_shapes=[tok] * 6 + [tok, big, big, tok, big, big, pltpu.VMEM((3, s, 128), BF16),
                                    pltpu.SemaphoreType.DMA((12,))], name=name,
        compiler_params=pltpu.CompilerParams(vmem_limit_bytes=VMEM_LIMIT))(
            proj, cos, sin1, sin2, out, lse, dmix, dproj)


_POOL_TM = 512
_POOL_NT = SEQ // _POOL_TM
_POOL_HALO_PER_TILE = _POOL_TM // POOL_HALO


def _pool_tile(u_ext, w_bd, scale, t0):
    s2 = u_ext + _roll_rows(u_ext, 1)
    s4 = s2 + _roll_rows(s2, 2)
    s8 = s4 + _roll_rows(s4, 4)
    s16 = s8 + _roll_rows(s8, 8)
    grp = _head_of_lane(POOL_W)
    sel = jnp.where(grp == 0, s2, jnp.where(grp == 1, s4, jnp.where(grp == 2, s8, s16)))[POOL_HALO:]
    t = sel.shape[0]
    pos = t0 + lax.broadcasted_iota(jnp.int32, (t, POOL_W), 0) + 1
    win = jnp.where(grp == 0, 2, jnp.where(grp == 1, 4, jnp.where(grp == 2, 8, 16)))
    cnt = jnp.minimum(pos, win).astype(F32)
    diff = sel / cnt - u_ext[POOL_HALO:]
    return jnp.dot(diff.astype(BF16), w_bd.astype(BF16), preferred_element_type=F32) * scale


def _pool_fwd(proj, w_bd, scale, mix, *, name):
    tm = _POOL_TM

    def body(u_ref, halo_ref, w_ref, sc_ref, _, y_ref):
        i = pl.program_id(0)
        halo = jnp.where(i == 0, 0.0, halo_ref[...])
        u_ext = jnp.concatenate([halo, u_ref[...]], axis=0)
        y_ref[...] = _pool_tile(u_ext, w_ref[...], sc_ref[...], i * tm).astype(BF16)

    return pl.pallas_call(
        body, grid=(_POOL_NT,),
        in_specs=[pl.BlockSpec((tm, POOL_W), lambda i: (i, PROJ_POOL_BLK)),
                  pl.BlockSpec((POOL_HALO, POOL_W),
                               lambda i: (jnp.maximum(i * _POOL_HALO_PER_TILE - 1, 0), PROJ_POOL_BLK)),
                  pl.BlockSpec((POOL_W, POOL_W), lambda i: (0, 0)), pl.BlockSpec((1, POOL_W), lambda i: (0, 0)), _HBM],
        out_specs=pl.BlockSpec((tm, POOL_W), lambda i: (i, MIX_POOL_BLK)),
        out_shape=jax.ShapeDtypeStruct(mix.shape, mix.dtype), input_output_aliases={4: 0}, name=name,
        compiler_params=_cparams("parallel"))(proj, proj, w_bd, scale, mix)


def _pool_bwd(proj, dmix, w_bd, scale, dproj, *, name):
    tm = _POOL_TM
    last = _POOL_NT - 1

    def body(u_ref, halo_ref, dy_ref, w_ref, sc_ref, _, du_ref, dw_ref, dsc_ref, dhalo_acc):
        i = pl.program_id(0)

        @pl.when(i == 0)
        def _():
            dhalo_acc[...] = jnp.zeros_like(dhalo_acc)
            dw_ref[...] = jnp.zeros_like(dw_ref)
            dsc_ref[...] = jnp.zeros_like(dsc_ref)

        tile = last - i
        halo = jnp.where(tile == 0, 0.0, halo_ref[...])
        u_ext = jnp.concatenate([halo, u_ref[...]], axis=0)
        _, vjp = jax.vjp(functools.partial(_pool_tile, t0=tile * tm), u_ext, w_ref[...], sc_ref[...])
        gu, gw, gs = vjp(dy_ref[...])
        du_ref[...] = jnp.concatenate([gu[POOL_HALO:tm], gu[tm:] + dhalo_acc[...]], axis=0).astype(BF16)
        dhalo_acc[...] = gu[:POOL_HALO]
        dw_ref[...] += gw
        dsc_ref[...] += gs

    rev = lambda i: last - i
    return pl.pallas_call(
        body, grid=(_POOL_NT,),
        in_specs=[pl.BlockSpec((tm, POOL_W), lambda i: (rev(i), PROJ_POOL_BLK)),
                  pl.BlockSpec((POOL_HALO, POOL_W),
                               lambda i: (jnp.maximum(rev(i) * _POOL_HALO_PER_TILE - 1, 0), PROJ_POOL_BLK)),
                  pl.BlockSpec((tm, POOL_W), lambda i: (rev(i), MIX_POOL_BLK)),
                  pl.BlockSpec((POOL_W, POOL_W), lambda i: (0, 0)), pl.BlockSpec((1, POOL_W), lambda i: (0, 0)), _HBM],
        out_specs=[pl.BlockSpec((tm, POOL_W), lambda i: (rev(i), PROJ_POOL_BLK)),
                   pl.BlockSpec((POOL_W, POOL_W), lambda i: (0, 0)), pl.BlockSpec((1, POOL_W), lambda i: (0, 0))],
        out_shape=[jax.ShapeDtypeStruct(dproj.shape, dproj.dtype), jax.ShapeDtypeStruct((POOL_W, POOL_W), F32),
                   jax.ShapeDtypeStruct((1, POOL_W), F32)],
        input_output_aliases={5: 0}, scratch_shapes=[pltpu.VMEM((POOL_HALO, POOL_W), F32)], name=name,
        compiler_params=_cparams("arbitrary"))(proj, proj, dmix, w_bd, scale, dproj)


_FFN_TM = 256
_FFN_NT = SEQ // _FFN_TM
_FFN_SPLIT = 2
_FFN_HALO_PER_TILE = _FFN_TM // CONV_HALO


def _ffn_act_tile(hid_ext, conv_w, conv_b):
    hc = conv_b
    for k in range(FFN_CONV_K):
        hc = hc + _pick_row(conv_w, k) * _shift_rows(hid_ext, FFN_CONV_K - 1 - k, CONV_HALO)
    half = hid_ext.shape[1] // 2
    return _silu(hc[:, :half]) * hc[:, half:]


def _ffn_fwd(x1, norm_g, sc, sh, gate, up_t, down, conv_w, conv_b, *, name):
    tm = _FFN_TM
    w = 2 * FFN_DIM
    d = D_MODEL

    def body(x_ref, ng_ref, sc_ref, sh_ref, g_ref, up_ref, dn_ref, cw_ref, cb_ref,
             h_ref, hid_ref, act_ref, f_ref, x2_ref, halo_acc):
        i = pl.program_id(0)
        h2 = _rms_modulate(x_ref[...], ng_ref[...], sc_ref[...], sh_ref[...]).astype(BF16)
        h_ref[...] = h2
        hid = lax.dot_general(h2, up_ref[...], (((1,), (1,)), ((), ())), preferred_element_type=F32)
        hid_ref[...] = hid
        halo = jnp.where(i == 0, 0.0, halo_acc[...])
        act = _ffn_act_tile(jnp.concatenate([halo, hid], axis=0), cw_ref[...], cb_ref[...]).astype(BF16)
        halo_acc[...] = hid[tm - CONV_HALO:]
        act_ref[...] = act
        f = jnp.dot(act, dn_ref[...], preferred_element_type=F32)
        f_ref[...] = f
        x2_ref[...] = x_ref[...] + g_ref[...] * f

    tile = lambda n: pl.BlockSpec((tm, n), lambda i: (i, 0))
    return pl.pallas_call(
        body, grid=(_FFN_NT,),
        in_specs=[tile(d)] + [_resident((1, d))] * 4 + [_resident((w, d)), _resident((FFN_DIM, d)),
                                                        _resident((8, w)), _resident((1, w))],
        out_specs=[tile(d), tile(w), tile(FFN_DIM), tile(d), tile(d)],
        out_shape=[jax.ShapeDtypeStruct((SEQ, d), BF16), jax.ShapeDtypeStruct((SEQ, w), F32),
                   jax.ShapeDtypeStruct((SEQ, FFN_DIM), BF16), jax.ShapeDtypeStruct((SEQ, d), F32),
                   jax.ShapeDtypeStruct((SEQ, d), F32)],
        scratch_shapes=[pltpu.VMEM((CONV_HALO, w), F32)], name=name,
        compiler_params=_cparams("arbitrary"))(x1, norm_g, sc, sh, gate, up_t, down, conv_w, conv_b)


def _ffn_bwd(dx2, gate, f, hid, up_t, down, conv_w, conv_b, *, name):
    tm = _FFN_TM
    w = 2 * FFN_DIM
    d = D_MODEL
    last = _FFN_NT - 1

    def body(dx_ref, g_ref, f_ref, h_ref, halo_ref, up_ref, dn_ref, cw_ref, cb_ref,
             df_ref, dg_ref, dh_ref, dh2_ref, dcw_ref, dcb_ref, dhalo_acc):
        i = pl.program_id(0)

        @pl.when(i == 0)
        def _():
            dhalo_acc[...] = jnp.zeros_like(dhalo_acc)
            dcw_ref[...] = jnp.zeros_like(dcw_ref)
            dcb_ref[...] = jnp.zeros_like(dcb_ref)
            dg_ref[...] = jnp.zeros_like(dg_ref)

        dxv = dx_ref[...]
        df = (g_ref[...] * dxv).astype(BF16)
        df_ref[...] = df
        dg_ref[...] += jnp.sum(dxv * f_ref[...], axis=0, keepdims=True)
        hw = FFN_DIM // _FFN_SPLIT
        dh2 = jnp.zeros((tm, d), F32)
        for j in range(_FFN_SPLIT):
            gate_cols, up_cols = pl.ds(hw * j, hw), pl.ds(FFN_DIM + hw * j, hw)
            both = lambda ref, rows=slice(None): jnp.concatenate([ref[rows, gate_cols], ref[rows, up_cols]], axis=1)
            dact = lax.dot_general(df, dn_ref[gate_cols, :], (((1,), (1,)), ((), ())), preferred_element_type=F32)
            halo = jnp.where(i == last, 0.0, both(halo_ref))
            hid_ext = jnp.concatenate([halo, both(h_ref)], axis=0)
            _, vjp = jax.vjp(_ffn_act_tile, hid_ext, both(cw_ref), both(cb_ref))
            gh, gw, gb = vjp(dact)
            dhid = jnp.concatenate([gh[CONV_HALO:tm], gh[tm:] + both(dhalo_acc)], axis=0).astype(BF16)
            for cols, part in ((gate_cols, slice(0, hw)), (up_cols, slice(hw, 2 * hw))):
                dhalo_acc[:, cols] = gh[:CONV_HALO, part]
                dh_ref[:, cols] = dhid[:, part]
                dh2 = dh2 + jnp.dot(dhid[:, part], up_ref[cols, :], preferred_element_type=F32)
                dcw_ref[:, cols] += gw[:, part]
                dcb_ref[:, cols] += gb[:, part]
        dh2_ref[...] = dh2

    rev = lambda i: last - i
    tile = lambda n: pl.BlockSpec((tm, n), lambda i: (rev(i), 0))
    acc = lambda shape: pl.BlockSpec(shape, lambda i: (0, 0))
    return pl.pallas_call(
        body, grid=(_FFN_NT,),
        in_specs=[tile(d), _resident((1, d)), tile(d), tile(w),
                  pl.BlockSpec((CONV_HALO, w), lambda i: (jnp.maximum(rev(i) * _FFN_HALO_PER_TILE - 1, 0), 0)),
                  _resident((w, d)), _resident((FFN_DIM, d)), _resident((8, w)), _resident((1, w))],
        out_specs=[tile(d), acc((1, d)), tile(w), tile(d), acc((8, w)), acc((1, w))],
        out_shape=[jax.ShapeDtypeStruct((SEQ, d), BF16), jax.ShapeDtypeStruct((1, d), F32),
                   jax.ShapeDtypeStruct((SEQ, w), BF16), jax.ShapeDtypeStruct((SEQ, d), F32),
                   jax.ShapeDtypeStruct((8, w), F32), jax.ShapeDtypeStruct((1, w), F32)],
        scratch_shapes=[pltpu.VMEM((CONV_HALO, w), F32)], name=name,
        compiler_params=_cparams("arbitrary"))(dx2, gate, f, hid, hid, up_t, down, conv_w, conv_b)


def _axes():
    return lax.axis_index("x"), lax.axis_index("y"), lax.axis_index("c")


def _handshake(peers):
    barrier = pltpu.get_barrier_semaphore()
    for peer in peers:
        pl.semaphore_signal(barrier, inc=1, device_id=peer, device_id_type=MESH)
    pl.semaphore_wait(barrier, len(peers))


def _allgather_body(x_refs, out_refs, send_sems, recv_sems, local_sems, own_barrier):
    n = len(x_refs)
    x, y, c = _axes()
    me, sibling = (x, y, c), (x, y, 1 - c)
    chips = [(1 - x, y), (x, 1 - y), (1 - x, 1 - y)]
    if own_barrier:
        _handshake([sibling] + [(*chip, c) for chip in chips])

    def slot(a, px, py, pc):
        return out_refs[a].at[4 * px + 2 * py + pc]

    def copy(a, k, block, to, src=None):
        return pltpu.make_async_remote_copy(
            src_ref=slot(a, *block) if src is None else src, dst_ref=slot(a, *block),
            send_sem=send_sems.at[a, k], recv_sem=recv_sems.at[a, k], device_id=to, device_id_type=MESH)

    mines, firsts = [], []
    for a in range(n):
        mines.append(pltpu.make_async_copy(x_refs[a], slot(a, *me), local_sems.at[a]))
        mines[-1].start()
        first = [copy(a, 0, me, sibling, src=x_refs[a])]
        first += [copy(a, 1 + j, me, (*chip, c), src=x_refs[a]) for j, chip in enumerate(chips)]
        for cp in first:
            cp.start()
        firsts += first
    passed = []
    for j, chip in enumerate(chips):
        for a in range(n):
            copy(a, 1 + j, (*chip, c), me).wait_recv()
            passed.append(copy(a, 4 + j, (*chip, c), sibling))
            passed[-1].start()
    for a in range(n):
        copy(a, 0, sibling, me).wait_recv()
    for j, chip in enumerate(chips):
        for a in range(n):
            copy(a, 4 + j, (*chip, 1 - c), me).wait_recv()
    for cp in firsts + passed:
        cp.wait_send()
    for cp in mines:
        cp.wait()


def _allgather_sems(n):
    return [pltpu.SemaphoreType.DMA((n, 7)), pltpu.SemaphoreType.DMA((n, 7)), pltpu.SemaphoreType.DMA((n,))]


def _allgather(xs, *, name):
    n = len(xs)

    def body(*refs):
        _allgather_body(refs[:n], refs[n:2 * n], *refs[2 * n:], own_barrier=False)

    return pl.pallas_call(
        body, out_shape=[jax.ShapeDtypeStruct((N_DEV,) + xb.shape, xb.dtype) for xb in xs],
        in_specs=[_HBM] * n, out_specs=[_HBM] * n, scratch_shapes=_allgather_sems(n), name=name)(*xs)


def _allgather_async(xs, *, name, collective_id):
    n = len(xs)
    x_refs = [jax.new_ref(xb, memory_space=pltpu.MemorySpace.HBM) for xb in xs]
    out_refs = [jax.empty_ref(jax.ShapeDtypeStruct((N_DEV,) + xb.shape, xb.dtype), memory_space=pltpu.MemorySpace.HBM)
                for xb in xs]

    @pl.kernel(mesh=plsc.ScalarSubcoreMesh(axis_name="sequencer", num_cores=1), name=name,
               scratch_types=tuple(_allgather_sems(n)),
               compiler_params=pltpu.CompilerParams(collective_id=collective_id))
    def launch(send_sems, recv_sems, local_sems):
        _allgather_body(x_refs, out_refs, send_sems, recv_sems, local_sems, own_barrier=True)

    launch()
    return [r[...] for r in out_refs]


def _pair_exchange(blocks, *, name, collective_id):
    n = len(blocks)
    hbm = pltpu.MemorySpace.HBM
    in_refs = [jax.new_ref(b, memory_space=hbm) for b in blocks]
    out_refs = [jax.empty_ref(jax.ShapeDtypeStruct((4,) + b.shape[1:], b.dtype), memory_space=hbm) for b in blocks]

    @pl.kernel(mesh=plsc.ScalarSubcoreMesh(axis_name="sequencer", num_cores=1), name=name,
               scratch_types=(pltpu.SemaphoreType.DMA((n, 4)), pltpu.SemaphoreType.DMA((n, 4))),
               compiler_params=pltpu.CompilerParams(collective_id=collective_id))
    def launch(send_sems, recv_sems):
        x, y, c = _axes()
        _handshake([(x, y, 1 - c)])
        copies = [pltpu.make_async_remote_copy(
            src_ref=in_refs[a].at[2 * s + (1 - c)], dst_ref=out_refs[a].at[s], send_sem=send_sems.at[a, s],
            recv_sem=recv_sems.at[a, s], device_id=(x, y, 1 - c), device_id_type=MESH)
            for a in range(n) for s in range(4)]
        for cp in copies:
            cp.start()
        for cp in copies:
            cp.wait_recv()
        for cp in copies:
            cp.wait_send()

    launch()
    return [r[...] for r in out_refs]


def _chip_exchange(parts, *, name, collective_id):
    n = len(parts)
    hbm = pltpu.MemorySpace.HBM
    in_refs = [jax.new_ref(p, memory_space=hbm) for p in parts]
    out_refs = [jax.empty_ref(jax.ShapeDtypeStruct(p.shape, p.dtype), memory_space=hbm) for p in parts]

    @pl.kernel(mesh=plsc.ScalarSubcoreMesh(axis_name="sequencer", num_cores=1), name=name,
               scratch_types=(pltpu.SemaphoreType.DMA((n, 3)), pltpu.SemaphoreType.DMA((n, 3)),
                              pltpu.SemaphoreType.DMA((n,))),
               compiler_params=pltpu.CompilerParams(collective_id=collective_id))
    def launch(send_sems, recv_sems, local_sems):
        x, y, c = _axes()
        my_chip = 2 * x + y
        chips = [(1 - x, y), (x, 1 - y), (1 - x, 1 - y)]
        _handshake([(*chip, c) for chip in chips])
        locals_ = [pltpu.make_async_copy(in_refs[a].at[my_chip], out_refs[a].at[my_chip], local_sems.at[a])
                   for a in range(n)]
        for cp in locals_:
            cp.start()
        copies = [pltpu.make_async_remote_copy(
            src_ref=in_refs[a].at[2 * px + py], dst_ref=out_refs[a].at[my_chip], send_sem=send_sems.at[a, k],
            recv_sem=recv_sems.at[a, k], device_id=(px, py, c), device_id_type=MESH)
            for a in range(n) for k, (px, py) in enumerate(chips)]
        for cp in copies:
            cp.start()
        for cp in copies:
            cp.wait_recv()
        for cp in copies:
            cp.wait_send()
        for cp in locals_:
            cp.wait()

    launch()
    return [r[...] for r in out_refs]


def _pair_sum(core, blocks, from_sibling, *, name):
    n = len(blocks)

    def body(core_ref, *refs):
        for a_ref, b_ref, o_ref in zip(refs[:n], refs[n:2 * n], refs[2 * n:]):
            o_ref[...] = (a_ref[...].astype(F32) + b_ref[...].astype(F32)).astype(o_ref.dtype)

    mine = lambda b: pl.BlockSpec((1,) + b.shape[1:], lambda s, core_ref: (2 * s + core_ref[0], 0, 0))
    slot = lambda b: pl.BlockSpec((1,) + b.shape[1:], lambda s, core_ref: (s, 0, 0))
    return pl.pallas_call(
        body,
        grid_spec=pltpu.PrefetchScalarGridSpec(
            num_scalar_prefetch=1, grid=(4,),
            in_specs=[mine(b) for b in blocks] + [slot(b) for b in blocks], out_specs=[slot(b) for b in blocks]),
        out_shape=[jax.ShapeDtypeStruct(s.shape, s.dtype) for s in from_sibling], name=name,
        compiler_params=_cparams("parallel"))(core, *blocks, *from_sibling)


def _sum_blocks(a, *, name, tr=None):
    n, r, cdim = a.shape
    tr = tr or r

    def body(a_ref, o_ref):
        acc = a_ref[0].astype(F32)
        for k in range(1, n):
            acc = acc + a_ref[k].astype(F32)
        o_ref[...] = acc

    return pl.pallas_call(body, grid=(r // tr,), in_specs=[pl.BlockSpec((n, tr, cdim), lambda i: (0, i, 0))],
                          out_specs=pl.BlockSpec((tr, cdim), lambda i: (i, 0)),
                          out_shape=jax.ShapeDtypeStruct((r, cdim), F32), name=name,
                          compiler_params=_cparams("parallel"))(a)


def _sum_gathered(gathered, *, name):
    n = len(gathered)

    def body(*refs):
        for a_ref, o_ref in zip(refs[:n], refs[n:]):
            acc = a_ref[0]
            for k in range(1, N_DEV):
                acc = acc + a_ref[k]
            o_ref[...] = acc

    return pl.pallas_call(body, out_shape=[jax.ShapeDtypeStruct(g.shape[1:], F32) for g in gathered], name=name,
                          compiler_params=pltpu.CompilerParams(vmem_limit_bytes=VMEM_LIMIT))(*gathered)


_ADA_SHARD = 6 * D_MODEL // N_DEV


def _ada_mod(c_all, ada_w, *, name):
    def body(c_ref, w_ref, o_ref):
        o_ref[0] = jnp.dot(_silu(c_ref[...]).astype(BF16), w_ref[0].astype(BF16), preferred_element_type=F32)

    return pl.pallas_call(
        body, grid=(DEPTH,),
        in_specs=[pl.BlockSpec((N_DEV, D_MODEL), lambda l: (0, 0)),
                  pl.BlockSpec((1, D_MODEL, _ADA_SHARD), lambda l: (l, 0, 0))],
        out_specs=pl.BlockSpec((1, N_DEV, _ADA_SHARD), lambda l: (l, 0, 0)),
        out_shape=jax.ShapeDtypeStruct((DEPTH, N_DEV, _ADA_SHARD), F32), name=name,
        compiler_params=_cparams("parallel"))(c_all, ada_w)


def _ada_wgrad(c_all, dmod_cols, *, name):
    def body(c_ref, d_ref, o_ref):
        o_ref[0] = lax.dot_general(_silu(c_ref[...]), d_ref[0], (((0,), (0,)), ((), ())),
                                   preferred_element_type=F32, precision=lax.Precision.HIGHEST)

    return pl.pallas_call(
        body, grid=(DEPTH,),
        in_specs=[pl.BlockSpec((N_DEV, D_MODEL), lambda l: (0, 0)),
                  pl.BlockSpec((1, N_DEV, _ADA_SHARD), lambda l: (l, 0, 0))],
        out_specs=pl.BlockSpec((1, D_MODEL, _ADA_SHARD), lambda l: (l, 0, 0)),
        out_shape=jax.ShapeDtypeStruct((DEPTH, D_MODEL, _ADA_SHARD), F32), name=name,
        compiler_params=_cparams("parallel"))(c_all, dmod_cols)


def _add_rows(a, b, *, name):
    def body(a_ref, b_ref, o_ref):
        o_ref[...] = a_ref[...] + b_ref[...]

    return pl.pallas_call(body, out_shape=jax.ShapeDtypeStruct(a.shape, a.dtype), name=name)(a, b)


def _adamw_update(w_ref, g_ref, m_ref, v_ref, d_ref, mo_ref, vo_ref):
    gv = g_ref[...]
    mn = ADAM_B1 * m_ref[...] + (1.0 - ADAM_B1) * gv
    vn = ADAM_B2 * v_ref[...] + (1.0 - ADAM_B2) * (gv * gv)
    mo_ref[...] = mn
    vo_ref[...] = vn
    m_hat = mn / (1.0 - ADAM_B1 ** ADAM_STEP)
    v_hat = vn / (1.0 - ADAM_B2 ** ADAM_STEP)
    d_ref[...] = -ADAM_LR * (m_hat / (jnp.sqrt(v_hat) + ADAM_EPS) + ADAM_WD * w_ref[...])


def _adamw_small(ws, gs, ms, vs, *, name):
    n = len(ws)

    def body(*refs):
        ins, outs = refs[:4 * n], refs[4 * n:]
        for i in range(n):
            _adamw_update(ins[i], ins[n + i], ins[2 * n + i], ins[3 * n + i], outs[i], outs[n + i], outs[2 * n + i])

    shapes = [jax.ShapeDtypeStruct(a.shape, F32) for a in ws]
    outs = pl.pallas_call(body, out_shape=shapes * 3, name=name,
                          compiler_params=pltpu.CompilerParams(vmem_limit_bytes=VMEM_LIMIT))(*ws, *gs, *ms, *vs)
    return outs[:n], outs[n:2 * n], outs[2 * n:]


def _adamw_stack(w, g, m, v, *, name, tr):
    layers, r, cdim = w.shape
    body = functools.partial(_adamw_update)
    spec = pl.BlockSpec((1, tr, cdim), lambda l, i: (l, i, 0))
    shp = jax.ShapeDtypeStruct(w.shape, F32)
    return pl.pallas_call(body, grid=(layers, r // tr), in_specs=[spec] * 4, out_specs=[spec] * 3,
                          out_shape=[shp] * 3, name=name, compiler_params=_cparams("parallel", "parallel"))(w, g, m, v)


def _adamw(w, g, m, v, *, name, tr):
    r, cdim = w.shape
    body = functools.partial(_adamw_update)

    spec = pl.BlockSpec((tr, cdim), lambda i: (i, 0))
    shp = jax.ShapeDtypeStruct((r, cdim), F32)
    return pl.pallas_call(body, grid=(r // tr,), in_specs=[spec] * 4, out_specs=[spec] * 3, out_shape=[shp] * 3,
                          name=name, compiler_params=_cparams("parallel"))(w, g, m, v)


def _pad_rows(a, rows):
    return jnp.concatenate([a, jnp.zeros((rows - a.shape[0],) + a.shape[1:], a.dtype)], axis=0)


def _pad_lanes(a, lanes):
    return jnp.concatenate([a, jnp.zeros(a.shape[:-1] + (lanes - a.shape[-1],), a.dtype)], axis=-1)


def _permute_w_in(wt):
    return jnp.concatenate([wt[512:1536], wt[:512], wt[1536:1544],
                            jnp.zeros((PROJ_W - IN_W, wt.shape[1]), wt.dtype), wt[1544:]], axis=0)


def _unpermute_w_in(wp):
    return jnp.concatenate([wp[1024:1536], wp[:1024], wp[1536:1544], wp[PROJ_SSD_W:]], axis=0)


def _block_diag(w):
    rows = []
    for g in range(4):
        rows.append(jnp.concatenate([w[g] if k == g else jnp.zeros_like(w[g]) for k in range(4)], axis=1))
    return jnp.concatenate(rows, axis=0)


def _diag_blocks(wbd):
    return jnp.stack([wbd[64 * g:64 * (g + 1), 64 * g:64 * (g + 1)] for g in range(4)], axis=0)


def _layer_params(l, small):
    return dict(
        norm1_g=small["norm1_g"][l][None], norm2_g=small["norm2_g"][l][None],
        conv_w=_pad_rows(small["ssd_conv_w"][l], 8), conv_b=small["ssd_conv_b"][l][None],
        dt_bias=_pad_lanes(small["ssd_dt_bias"][l][None], 128), a_log=_pad_lanes(small["ssd_a_log"][l][None], 128),
        d_skip=_pad_lanes(small["ssd_d"][l][None], 128), ssd_norm_g=small["ssd_norm_g"][l][None],
        pool_bd=_block_diag(small["pool_w"][l]), pool_scale=small["pool_scale"][l][None],
        fcw=_pad_rows(small["ffn_conv_w"][l], 8), fcb=small["ffn_conv_b"][l][None])


def _mod_rows(mod_l):
    return [mod_l[None, D_MODEL * i:D_MODEL * (i + 1)] for i in range(6)]


def _layer_fwd(x, mod_l, p, tabs, l, gather):
    sh1, sc1, g1, sh2, sc2, g2 = _mod_rows(mod_l)
    mix_w = gather(l, "mix", None)
    p.update(w_in=mix_w["w_in"], w_out=mix_w["w_out"])
    proj, h1 = _mm(x, p["w_in"], nt=True, norm=(p["norm1_g"], sc1, sh1), name=f"l{l}_proj")
    ffn_w = gather(l, "ffn", proj)
    p.update(up=ffn_w["ffn_up"], down=ffn_w["ffn_down"])
    mix, hst = _ssd_fwd(proj, p["conv_w"], p["conv_b"], p["dt_bias"], p["a_log"], p["d_skip"], p["ssd_norm_g"],
                        name=f"l{l}_ssd")
    mix = _pool_fwd(proj, p["pool_bd"], p["pool_scale"], mix, name=f"l{l}_pool")
    y_att, lse, mix = _att_fwd(proj, *tabs, mix, name=f"l{l}_att")
    mo, x1 = _mm(mix, p["w_out"], residual=(x, g1), name=f"l{l}_out")
    gather(l + 1, "mix", (x1, p["up"]))
    h2, hid, act, f, x2 = _ffn_fwd(x1, p["norm2_g"], sc2, sh2, g2, p["up"], p["down"], p["fcw"], p["fcb"],
                                   name=f"l{l}_ffn")
    return x2, dict(x=x, h1=h1, proj=proj, hst=hst, y_att=y_att, lse=lse, mix=mix, mo=mo, x1=x1, h2=h2, hid=hid,
                    act=act, f=f)


def _layer_bwd(dx2, sv, mod_l, p, tabs, l, exchange):
    sh1, sc1, g1, sh2, sc2, g2 = _mod_rows(mod_l)
    df, dg2, dhid, dh2, dfcw, dfcb = _ffn_bwd(dx2, g2, sv["f"], sv["hid"], p["up"], p["down"], p["fcw"], p["fcb"],
                                              name=f"l{l}_ffn_b")
    d_down = _wgrad(sv["act"], df, tk=1408, name=f"l{l}_down_bw")
    d_up = _wgrad(dhid, sv["h2"], tk=1408, name=f"l{l}_up_bw")
    finish_ffn = exchange(l, "ffn", dict(ffn_up=d_up, ffn_down=d_down))
    dx1, dn2, dsc2, dsh2, dmix, dmo, dg1 = _mid_bwd(sv["x1"], dh2, dx2, p["norm2_g"], sc2, g1, sv["mo"], p["w_out"],
                                                    name=f"l{l}_mid_b")
    finish_ffn(dx1)
    d_wout = _wgrad(sv["mix"], dmo, name=f"l{l}_out_bw")
    dproj, dcw, dcb, ddb, dal, dd, dng = _ssd_bwd(
        sv["proj"], sv["hst"], dmix, p["conv_w"], p["conv_b"], p["dt_bias"], p["a_log"], p["d_skip"],
        p["ssd_norm_g"], name=f"l{l}_ssd_b")
    dproj, dwbd, dpsc = _pool_bwd(sv["proj"], dmix, p["pool_bd"], p["pool_scale"], dproj, name=f"l{l}_pool_b")
    dproj = _att_bwd(sv["proj"], *tabs, sv["y_att"], sv["lse"], dmix, dproj, name=f"l{l}_att_b")
    d_win = _wgrad(dproj, sv["h1"], tk=1408, name=f"l{l}_proj_bw")
    finish_mix = exchange(l, "mix", dict(w_in=d_win, w_out=d_wout))
    dx0, dn1, dsc1, dsh1 = _norm_mod_bwd(sv["x"], dproj, dx1, p["norm1_g"], sc1, w=p["w_in"], name=f"l{l}_proj_b")
    dx0, _ = lax.optimization_barrier((dx0, (d_win, d_wout, d_up, d_down)))
    dmod = jnp.concatenate([dsh1, dsc1, dg1, dsh2, dsc2, dg2], axis=1)[0]
    small = dict(norm1_g=dn1[0], ssd_conv_w=dcw[:SSD_CONV_K], ssd_conv_b=dcb[0], ssd_dt_bias=ddb[0], ssd_a_log=dal[0],
                 ssd_d=dd[0], ssd_norm_g=dng[0], pool_w=_diag_blocks(dwbd), pool_scale=dpsc[0], norm2_g=dn2[0],
                 ffn_conv_w=dfcw[:FFN_CONV_K], ffn_conv_b=dfcb[0])
    return dx0, dmod, small, finish_mix


def _example_step(x, target, pos_col, inv_freq_lane, mod, gather, small, final_g, exchange):
    tabs = _rope_tables(pos_col, inv_freq_lane, name="rope_tables")
    params, saved = [], []
    for l in range(DEPTH):
        params.append(_layer_params(l, small))
        x, sv = _layer_fwd(x, mod[l], params[l], tabs, l, gather)
        saved.append(sv)
    loss_row, dx, dfg = _final_loss(x, final_g[None], target, name="final_loss")
    dmods, smalls = [None] * DEPTH, [None] * DEPTH
    for l in reversed(range(DEPTH)):
        dx, dmods[l], smalls[l], finish_mix = _layer_bwd(dx, saved[l], mod[l], params[l], tabs, l, exchange)
        if l > 0:
            finish_mix(dx)
    return loss_row, dx, jnp.stack(dmods, axis=0), smalls, dfg[0], finish_mix


_BIG = ("w_in", "w_out", "ffn_up", "ffn_down")
_SMALL_GRADS = ("norm1_g", "ssd_conv_w", "ssd_conv_b", "ssd_dt_bias", "ssd_a_log", "ssd_d", "ssd_norm_g", "pool_w",
                "pool_scale", "norm2_g", "ffn_conv_w", "ffn_conv_b")
_SMALL_PARAMS = ("ada_b", "norm1_g", "ssd_conv_w", "ssd_conv_b", "ssd_dt_bias", "ssd_a_log", "ssd_d", "ssd_norm_g",
                 "pool_w", "pool_scale", "norm2_g", "ffn_conv_w", "ffn_conv_b", "final_g")
_WEIGHT_ORDER = ("ada_w", "ada_b", "norm1_g", "w_in", "ssd_conv_w", "ssd_conv_b", "ssd_dt_bias", "ssd_a_log", "ssd_d",
                 "ssd_norm_g", "pool_w", "pool_scale", "w_out", "norm2_g", "ffn_up", "ffn_conv_w", "ffn_conv_b",
                 "ffn_down", "final_g")


_COLUMN_SHARDED = ("w_in", "ffn_up")
_GROUPS = (("mix", ("w_in", "w_out")), ("ffn", ("ffn_up", "ffn_down")))


def _big_shares(w, l, names):
    return [(w[name][l].T if name in _COLUMN_SHARDED else w[name][l]).astype(BF16) for name in names]


def _unshard_big(names, gathered):
    out = {}
    for name, g in zip(names, gathered):
        full = g.reshape(N_DEV * g.shape[1], g.shape[2])
        out[name] = _permute_w_in(full) if name == "w_in" else full
    return out


def _shard_big(grads):
    out = []
    for name, g in grads.items():
        g = _unpermute_w_in(g) if name == "w_in" else g
        out.append(g.reshape(N_DEV, g.shape[0] // N_DEV, g.shape[1]))
    return out


def kernel(x, c, positions, ada_w, ada_b, norm1_g, w_in, ssd_conv_w, ssd_conv_b, ssd_dt_bias, ssd_a_log, ssd_d, ssd_norm_g, pool_w, pool_scale, w_out, norm2_g, ffn_up, ffn_conv_w, ffn_conv_b, ffn_down, final_g, loss_target, m_ada_w, m_ada_b, m_norm1_g, m_w_in, m_ssd_conv_w, m_ssd_conv_b, m_ssd_dt_bias, m_ssd_a_log, m_ssd_d, m_ssd_norm_g, m_pool_w, m_pool_scale, m_w_out, m_norm2_g, m_ffn_up, m_ffn_conv_w, m_ffn_conv_b, m_ffn_down, m_final_g, v_ada_w, v_ada_b, v_norm1_g, v_w_in, v_ssd_conv_w, v_ssd_conv_b, v_ssd_dt_bias, v_ssd_a_log, v_ssd_d, v_ssd_norm_g, v_pool_w, v_pool_scale, v_w_out, v_norm2_g, v_ffn_up, v_ffn_conv_w, v_ffn_conv_b, v_ffn_down, v_final_g):
    w = dict(ada_w=ada_w, ada_b=ada_b, norm1_g=norm1_g, w_in=w_in, ssd_conv_w=ssd_conv_w, ssd_conv_b=ssd_conv_b,
             ssd_dt_bias=ssd_dt_bias, ssd_a_log=ssd_a_log, ssd_d=ssd_d, ssd_norm_g=ssd_norm_g, pool_w=pool_w,
             pool_scale=pool_scale, w_out=w_out, norm2_g=norm2_g, ffn_up=ffn_up, ffn_conv_w=ffn_conv_w,
             ffn_conv_b=ffn_conv_b, ffn_down=ffn_down, final_g=final_g)
    m = dict(ada_w=m_ada_w, ada_b=m_ada_b, norm1_g=m_norm1_g, w_in=m_w_in, ssd_conv_w=m_ssd_conv_w,
             ssd_conv_b=m_ssd_conv_b, ssd_dt_bias=m_ssd_dt_bias, ssd_a_log=m_ssd_a_log, ssd_d=m_ssd_d,
             ssd_norm_g=m_ssd_norm_g, pool_w=m_pool_w, pool_scale=m_pool_scale, w_out=m_w_out, norm2_g=m_norm2_g,
             ffn_up=m_ffn_up, ffn_conv_w=m_ffn_conv_w, ffn_conv_b=m_ffn_conv_b, ffn_down=m_ffn_down,
             final_g=m_final_g)
    v = dict(ada_w=v_ada_w, ada_b=v_ada_b, norm1_g=v_norm1_g, w_in=v_w_in, ssd_conv_w=v_ssd_conv_w,
             ssd_conv_b=v_ssd_conv_b, ssd_dt_bias=v_ssd_dt_bias, ssd_a_log=v_ssd_a_log, ssd_d=v_ssd_d,
             ssd_norm_g=v_ssd_norm_g, pool_w=v_pool_w, pool_scale=v_pool_scale, w_out=v_w_out, norm2_g=v_norm2_g,
             ffn_up=v_ffn_up, ffn_conv_w=v_ffn_conv_w, ffn_conv_b=v_ffn_conv_b, ffn_down=v_ffn_down,
             final_g=v_final_g)
    ix, iy, ic = _axes()
    dev = 4 * ix + 2 * iy + ic

    c_all, scw, fcw = _allgather([c, ssd_conv_w.reshape(DEPTH * SSD_CONV_K, -1),
                                  ffn_conv_w.reshape(DEPTH * FFN_CONV_K, -1)], name="gather_small")
    small_all = c_all
    c_all = c_all.reshape(N_DEV, D_MODEL)
    scw = scw.reshape(N_DEV, DEPTH, SSD_CONV_K, -1).transpose(1, 2, 0, 3).reshape(DEPTH, SSD_CONV_K, SSD_CONV_CH)
    fcw = fcw.reshape(N_DEV, DEPTH, FFN_CONV_K, -1).transpose(1, 2, 0, 3).reshape(DEPTH, FFN_CONV_K, 2 * FFN_DIM)

    mod_cols = _ada_mod(c_all, ada_w, name="ada_mod")
    mod_all = _allgather([mod_cols.reshape(DEPTH * N_DEV, _ADA_SHARD)], name="gather_mod")[0]
    mod_all = mod_all.reshape(N_DEV, DEPTH, N_DEV, _ADA_SHARD)
    mod_mine = lax.dynamic_index_in_dim(mod_all, dev, axis=2, keepdims=False)
    mod = _add_rows(mod_mine.transpose(1, 0, 2).reshape(DEPTH, 6 * D_MODEL), ada_b, name="ada_bias")

    fetched = {}

    def gather(l, group, after):
        if l < DEPTH and (l, group) not in fetched:
            names = dict(_GROUPS)[group]
            shares, _ = lax.optimization_barrier((_big_shares(w, l, names), small_all if after is None else after))
            got = _allgather_async(shares, name=f"gather_weights_l{l}_{group}",
                                   collective_id=1 + 2 * l + (group == "ffn"))
            fetched[l, group] = _unshard_big(names, got)
        return fetched.get((l, group))

    core = ic.astype(jnp.int32).reshape(1)
    from_chips = {}

    def exchange(l, group, g):
        cid = 5 + 4 * l + 2 * (group == "mix")
        blocks = _shard_big(g)
        if from_chips:
            blocks, _ = lax.optimization_barrier((blocks, list(from_chips.values())))
        from_sibling = _pair_exchange(blocks, name=f"grads_pair_exchange_l{l}_{group}", collective_id=cid)

        def finish(after):
            theirs, _ = lax.optimization_barrier((from_sibling, after))
            parts = _pair_sum(core, blocks, theirs, name=f"grads_pair_sum_l{l}_{group}")
            got = _chip_exchange(parts, name=f"grads_chip_exchange_l{l}_{group}", collective_id=cid + 1)
            from_chips.update({(l, name): t for name, t in zip(g, got)})

        return finish

    small = dict(norm1_g=norm1_g, norm2_g=norm2_g, ssd_conv_w=scw, ssd_conv_b=ssd_conv_b, ssd_dt_bias=ssd_dt_bias,
                 ssd_a_log=ssd_a_log, ssd_d=ssd_d, ssd_norm_g=ssd_norm_g, pool_w=pool_w, pool_scale=pool_scale,
                 ffn_conv_w=fcw, ffn_conv_b=ffn_conv_b)

    inv_freq = ROPE_THETA ** (-jnp.arange(0, ROT_DIM, 2, dtype=F32) / ROT_DIM)
    lane = jnp.arange(128) % HEAD_LANES
    inv_freq_lane = jnp.where(lane < ROT_DIM, inv_freq[lane % (ROT_DIM // 2)], 0.0)[None, :]
    pos_col = positions.reshape(SEQ, 1).astype(F32)
    loss_row, dx, dmod, g_small, g_final, finish_last = _example_step(
        x[0], loss_target[0], pos_col, inv_freq_lane, mod, gather, small, final_g, exchange)

    small_names = list(_SMALL_GRADS)
    stacked = [jnp.stack([g_small[l][name] for l in range(DEPTH)], axis=0) for name in small_names]
    small_parts = [loss_row, dmod] + [s.reshape(-1, s.shape[-1]) for s in stacked] + [g_final[None]]
    gathered = _allgather_async(small_parts, name="gather_small_grads", collective_id=13)
    finish_last(gathered[0])

    grads = {}
    summed = {}
    for name in _BIG:
        summed[name] = jnp.stack([_sum_blocks(from_chips[l, name], name=f"grads_chip_sum_l{l}_{name}")
                                  for l in range(DEPTH)], axis=0)
        grads[name] = summed[name].transpose(0, 2, 1) if name in _COLUMN_SHARDED else summed[name]

    total = _sum_gathered(gathered, name="sum_small_grads")
    loss = total[0][0, 0]
    grads["ada_b"] = total[1]
    grads.update(zip(small_names, total[2:-1]))
    grads["final_g"] = total[-1][0]
    dmod_cols = lax.dynamic_slice_in_dim(gathered[1], dev * _ADA_SHARD, _ADA_SHARD, axis=2).transpose(1, 0, 2)
    grads["ada_w"] = _ada_wgrad(c_all, dmod_cols, name="ada_wgrad")
    for name in ("ssd_dt_bias", "ssd_a_log", "ssd_d"):
        grads[name] = grads[name][:, :SSD_HEADS]
    grads["pool_w"] = grads["pool_w"].reshape(pool_w.shape)
    grads["ssd_conv_w"] = lax.dynamic_slice_in_dim(
        grads["ssd_conv_w"].reshape(DEPTH, SSD_CONV_K, SSD_CONV_CH), dev * ssd_conv_w.shape[2], ssd_conv_w.shape[2], axis=2)
    grads["ffn_conv_w"] = lax.dynamic_slice_in_dim(
        grads["ffn_conv_w"].reshape(DEPTH, FFN_CONV_K, 2 * FFN_DIM), dev * ffn_conv_w.shape[2], ffn_conv_w.shape[2], axis=2)

    delta, new_m, new_v = {}, {}, {}
    for name, tr in (("ada_w", 512), ("w_out", 256), ("ffn_down", 352)):
        shp = w[name].shape
        two_d = lambda a: a.reshape(shp[0] * shp[1], shp[2])
        d_, m_, v_ = _adamw(two_d(w[name]), two_d(grads[name]), two_d(m[name]), two_d(v[name]), tr=tr,
                               name=f"adamw_{name}")
        delta[name], new_m[name], new_v[name] = (t.reshape(shp) for t in (d_, m_, v_))
    for name, tr in (("w_in", 321), ("ffn_up", 352)):
        flip = lambda a: a.transpose(0, 2, 1)
        outs = _adamw_stack(flip(w[name]), summed[name], flip(m[name]), flip(v[name]), tr=tr, name=f"adamw_{name}")
        delta[name], new_m[name], new_v[name] = (flip(t) for t in outs)
    two_d = lambda a: a.reshape(-1, a.shape[-1])
    outs = _adamw_small(*[[two_d(t[name]) for name in _SMALL_PARAMS] for t in (w, grads, m, v)], name="adamw_small")
    for name, d_, m_, v_ in zip(_SMALL_PARAMS, *outs):
        delta[name], new_m[name], new_v[name] = (t.reshape(w[name].shape) for t in (d_, m_, v_))

    grad_x = dx[None]
    return (loss, grad_x, *[grads[n].reshape(w[n].shape) for n in _WEIGHT_ORDER],
            *[delta[n] for n in _WEIGHT_ORDER], *[new_m[n] for n in _WEIGHT_ORDER],
            *[new_v[n] for n in _WEIGHT_ORDER])
```

```python
import functools
import math

import jax
import jax.numpy as jnp
from jax import lax
from jax.experimental import pallas as pl
from jax.experimental.pallas import tpu as pltpu
from jax.experimental.pallas import tpu_sc as plsc

F32 = jnp.float32
BF16 = jnp.bfloat16

N_DEV = 8
D_MODEL = 1024
SEQ = 4096
DEPTH = 2
SSD_INNER = 512
SSD_HEADS = 8
SSD_HPG = 4
SSD_STATE = 128
SSD_CHUNK = 256
SSD_CONV_K = 4
SSD_CONV_CH = 1024
POOL_W = 256
POOL_WINDOWS = (2, 4, 8, 16)
ATT_W = 256
ATT_PATTERNS = ((128, 1), (512, 4), (2048, 16))
ATT_BLOCK = 128
ROT_DIM = 16
ROPE_THETA = 500000.0
IN_W = 2568
FFN_DIM = 2816
FFN_CONV_K = 3
NORM_EPS = 1e-6
HEAD_LANES = 64

ADAM_LR = 0.001
ADAM_B1 = 0.9
ADAM_B2 = 0.999
ADAM_EPS = 1e-08
ADAM_WD = 0.01
ADAM_STEP = 10

PROJ_W = 2816
PROJ_SSD_W = 1792
PROJ_Z_BLK = 2
PROJ_DT_BLK = 12
PROJ_POOL_BLK = 7
PROJ_Q_BLK, PROJ_K_BLK, PROJ_V_BLK = 16, 18, 20
MIX_POOL_BLK = 2
MIX_ATT_BLK = 6
VMEM_LIMIT = 56 * 1024 * 1024
ROW_TILE = 512
CONV_HALO = 8
POOL_HALO = 16
ATT_KPAD = ATT_BLOCK * 16
MESH = pl.DeviceIdType.MESH
_HBM = pl.BlockSpec(memory_space=pl.ANY)


def _cparams(*sem):
    return pltpu.CompilerParams(dimension_semantics=sem, vmem_limit_bytes=VMEM_LIMIT)


def _resident(shape):
    return pl.BlockSpec(shape, lambda i: (0,) * len(shape), pipeline_mode=pl.Buffered(1))


def _silu(x):
    return x * jax.nn.sigmoid(x)


def _pick_lane(v, h):
    lane = lax.broadcasted_iota(jnp.int32, v.shape, 1)
    return jnp.sum(jnp.where(lane == h, v, 0.0), axis=1, keepdims=True)


def _pick_row(v, h):
    row = lax.broadcasted_iota(jnp.int32, v.shape, 0)
    return jnp.sum(jnp.where(row == h, v, 0.0), axis=0, keepdims=True)


def _head_of_lane(width):
    return lax.broadcasted_iota(jnp.int32, (1, width), 1) // HEAD_LANES


@functools.partial(jax.custom_vjp, nondiff_argnums=(1, 2))
def _shift_rows(x_ext, s, halo):
    y = x_ext if s == 0 else pltpu.roll(x_ext, s, 0)
    return y[halo:]


def _shift_rows_fwd(x_ext, s, halo):
    return _shift_rows(x_ext, s, halo), None


def _shift_rows_bwd(s, halo, _, g):
    ge = jnp.concatenate([jnp.zeros((halo, g.shape[1]), g.dtype), g], axis=0)
    return (ge if s == 0 else pltpu.roll(ge, ge.shape[0] - s, 0),)


_shift_rows.defvjp(_shift_rows_fwd, _shift_rows_bwd)


@functools.partial(jax.custom_vjp, nondiff_argnums=(1,))
def _roll_rows(x, s):
    return pltpu.roll(x, s, 0)


def _roll_rows_fwd(x, s):
    return _roll_rows(x, s), None


def _roll_rows_bwd(s, _, g):
    return (pltpu.roll(g, g.shape[0] - s, 0),)


_roll_rows.defvjp(_roll_rows_fwd, _roll_rows_bwd)


def _rms_modulate(xv, g, sc, sh):
    r = lax.rsqrt(jnp.mean(xv * xv, axis=-1, keepdims=True) + NORM_EPS)
    return (xv * r * g) * (1.0 + sc) + sh


def _mm(a, w, *, name, nt=False, tm=ROW_TILE, tn=None, out_dtype=F32, norm=None, residual=None):
    t, k = a.shape
    n = w.shape[0] if nt else w.shape[1]
    tn = tn or n
    assert tn == n or (norm is None and residual is None)
    extra_in = list(norm or ()) + list(residual or ())

    def body(*refs):
        a_ref, w_ref = refs[:2]
        ins = refs[2:2 + len(extra_in)]
        outs = refs[2 + len(extra_in):]
        if norm is None:
            av = a_ref[...].astype(BF16)
        else:
            av = _rms_modulate(a_ref[...], ins[0][...], ins[1][...], ins[2][...]).astype(BF16)
            outs[1][...] = av
        if nt:
            acc = lax.dot_general(av, w_ref[...], (((1,), (1,)), ((), ())), preferred_element_type=F32)
        else:
            acc = jnp.dot(av, w_ref[...], preferred_element_type=F32)
        outs[0][...] = acc.astype(out_dtype)
        if residual is not None:
            x_ref, gate_ref = ins[-2:]
            outs[-1][...] = x_ref[...] + gate_ref[...] * acc

    row = lambda width: pl.BlockSpec((1, width), lambda i, j: (0, 0))
    tile = lambda width: pl.BlockSpec((tm, width), lambda i, j: (i, 0))
    w_spec = pl.BlockSpec((tn, k), lambda i, j: (j, 0)) if nt else pl.BlockSpec((k, tn), lambda i, j: (0, j))
    in_specs = [tile(k), w_spec] + ([row(k)] * 3 if norm else []) + ([tile(n), row(n)] if residual else [])
    out_specs = [pl.BlockSpec((tm, tn), lambda i, j: (i, j))] + ([tile(k)] if norm else []) + \
        ([tile(n)] if residual else [])
    out_shape = [jax.ShapeDtypeStruct((t, n), out_dtype)] + \
        ([jax.ShapeDtypeStruct((t, k), BF16)] if norm else []) + \
        ([jax.ShapeDtypeStruct((t, n), F32)] if residual else [])
    outs = pl.pallas_call(
        body, grid=(t // tm, n // tn), in_specs=in_specs, out_specs=out_specs, out_shape=out_shape, name=name,
        compiler_params=_cparams("parallel", "parallel"))(a, w, *extra_in)
    return outs[0] if len(outs) == 1 else outs


def _wgrad(a, b, *, name, tk=None, tn=None, tt=2048, out_dtype=BF16):
    t, k = a.shape
    n = b.shape[1]
    tk = tk or k
    tn = tn or n
    steps = t // tt

    def body(a_ref, b_ref, o_ref, acc_ref):
        s = pl.program_id(2)

        @pl.when(s == 0)
        def _():
            acc_ref[...] = jnp.zeros_like(acc_ref)

        acc_ref[...] += lax.dot_general(a_ref[...].astype(BF16), b_ref[...].astype(BF16),
                                        (((0,), (0,)), ((), ())), preferred_element_type=F32)

        @pl.when(s == steps - 1)
        def _():
            o_ref[...] = acc_ref[...].astype(out_dtype)

    return pl.pallas_call(
        body, grid=(k // tk, n // tn, steps),
        in_specs=[pl.BlockSpec((tt, tk), lambda i, j, s: (s, i)), pl.BlockSpec((tt, tn), lambda i, j, s: (s, j))],
        out_specs=pl.BlockSpec((tk, tn), lambda i, j, s: (i, j)),
        out_shape=jax.ShapeDtypeStruct((k, n), out_dtype),
        scratch_shapes=[pltpu.VMEM((tk, tn), F32)], name=name,
        compiler_params=_cparams("parallel", "parallel", "arbitrary"))(a, b)


def _norm_mod_bwd(x, dh, dres, g, sc, *, name, w=None, tm=ROW_TILE):
    s, d = x.shape
    steps = s // tm

    def body(x_ref, dh_ref, dres_ref, g_ref, sc_ref, *rest):
        w_ref = rest[0] if w is not None else None
        dx_ref, dg_ref, dsc_ref, dsh_ref, da_acc, dsh_acc = rest[-6:]
        i = pl.program_id(0)

        @pl.when(i == 0)
        def _():
            da_acc[...] = jnp.zeros_like(da_acc)
            dsh_acc[...] = jnp.zeros_like(dsh_acc)

        xv = x_ref[...]
        if w is None:
            dhv = dh_ref[...].astype(F32)
        else:
            dhv = jnp.dot(dh_ref[...], w_ref[...], preferred_element_type=F32)
        r = lax.rsqrt(jnp.mean(xv * xv, axis=-1, keepdims=True) + NORM_EPS)
        xhat = xv * r
        gain = g_ref[...] * (1.0 + sc_ref[...])
        dxhat = dhv * gain
        dx_ref[...] = dres_ref[...] + r * (dxhat - xhat * jnp.mean(dxhat * xhat, axis=-1, keepdims=True))
        da_acc[...] += jnp.sum(dhv * xhat, axis=0, keepdims=True)
        dsh_acc[...] += jnp.sum(dhv, axis=0, keepdims=True)

        @pl.when(i == steps - 1)
        def _():
            dg_ref[...] = da_acc[...] * (1.0 + sc_ref[...])
            dsc_ref[...] = da_acc[...] * g_ref[...]
            dsh_ref[...] = dsh_acc[...]

    row = pl.BlockSpec((1, d), lambda i: (0, 0))
    tile = pl.BlockSpec((tm, d), lambda i: (i, 0))
    row_shape = jax.ShapeDtypeStruct((1, d), F32)
    dh_spec = tile if w is None else pl.BlockSpec((tm, dh.shape[1]), lambda i: (i, 0))
    return pl.pallas_call(
        body, grid=(steps,), in_specs=[tile, dh_spec, tile, row, row] + ([] if w is None else [_resident(w.shape)]),
        out_specs=[tile, row, row, row],
        out_shape=[jax.ShapeDtypeStruct((s, d), F32), row_shape, row_shape, row_shape],
        scratch_shapes=[pltpu.VMEM((1, d), F32), pltpu.VMEM((1, d), F32)], name=name,
        compiler_params=_cparams("arbitrary"))(x, dh, dres, g, sc, *([] if w is None else [w]))


def _mid_bwd(x1, dh2, dx2, norm_g, sc, gate, mo, w_out, *, name, tm=ROW_TILE):
    s, d = x1.shape
    steps = s // tm

    def body(x_ref, dh_ref, dres_ref, ng_ref, sc_ref, g_ref, mo_ref, w_ref,
             dx_ref, dng_ref, dsc_ref, dsh_ref, dmix_ref, dmo_ref, dg_ref, da_acc, dsh_acc):
        i = pl.program_id(0)

        @pl.when(i == 0)
        def _():
            da_acc[...] = jnp.zeros_like(da_acc)
            dsh_acc[...] = jnp.zeros_like(dsh_acc)
            dg_ref[...] = jnp.zeros_like(dg_ref)

        xv = x_ref[...]
        dhv = dh_ref[...]
        r = lax.rsqrt(jnp.mean(xv * xv, axis=-1, keepdims=True) + NORM_EPS)
        xhat = xv * r
        dxhat = dhv * (ng_ref[...] * (1.0 + sc_ref[...]))
        dxv = dres_ref[...] + r * (dxhat - xhat * jnp.mean(dxhat * xhat, axis=-1, keepdims=True))
        dx_ref[...] = dxv
        da_acc[...] += jnp.sum(dhv * xhat, axis=0, keepdims=True)
        dsh_acc[...] += jnp.sum(dhv, axis=0, keepdims=True)
        dmo = (g_ref[...] * dxv).astype(BF16)
        dmo_ref[...] = dmo
        dg_ref[...] += jnp.sum(dxv * mo_ref[...], axis=0, keepdims=True)
        dmix_ref[...] = lax.dot_general(dmo, w_ref[...], (((1,), (1,)), ((), ())), preferred_element_type=F32)

        @pl.when(i == steps - 1)
        def _():
            dng_ref[...] = da_acc[...] * (1.0 + sc_ref[...])
            dsc_ref[...] = da_acc[...] * ng_ref[...]
            dsh_ref[...] = dsh_acc[...]

    tile = pl.BlockSpec((tm, d), lambda i: (i, 0))
    row = pl.BlockSpec((1, d), lambda i: (0, 0))
    mix_tile = pl.BlockSpec((tm, w_out.shape[0]), lambda i: (i, 0))
    row_shape = jax.ShapeDtypeStruct((1, d), F32)
    return pl.pallas_call(
        body, grid=(steps,), in_specs=[tile, tile, tile, row, row, row, tile, _resident(w_out.shape)],
        out_specs=[tile, row, row, row, mix_tile, tile, row],
        out_shape=[jax.ShapeDtypeStruct((s, d), F32), row_shape, row_shape, row_shape,
                   jax.ShapeDtypeStruct((s, w_out.shape[0]), F32), jax.ShapeDtypeStruct((s, d), BF16), row_shape],
        scratch_shapes=[pltpu.VMEM((1, d), F32), pltpu.VMEM((1, d), F32)], name=name,
        compiler_params=_cparams("arbitrary"))(x1, dh2, dx2, norm_g, sc, gate, mo, w_out)


def _final_loss(x, g, target, *, name, tm=ROW_TILE):
    s, d = x.shape
    steps = s // tm

    def body(x_ref, g_ref, t_ref, loss_ref, dx_ref, dg_ref, sq_acc):
        i = pl.program_id(0)

        @pl.when(i == 0)
        def _():
            sq_acc[...] = jnp.zeros_like(sq_acc)
            dg_ref[...] = jnp.zeros_like(dg_ref)

        xv = x_ref[...]
        r = lax.rsqrt(jnp.mean(xv * xv, axis=-1, keepdims=True) + NORM_EPS)
        xhat = xv * r
        err = xhat * g_ref[...] - t_ref[...]
        sq_acc[...] += jnp.sum(err * err, axis=0, keepdims=True)
        dy = err * (1.0 / d)
        dg_ref[...] += jnp.sum(dy * xhat, axis=0, keepdims=True)
        dxhat = dy * g_ref[...]
        dx_ref[...] = r * (dxhat - xhat * jnp.mean(dxhat * xhat, axis=-1, keepdims=True))

        @pl.when(i == steps - 1)
        def _():
            total = jnp.sum(sq_acc[...], axis=1, keepdims=True) * (0.5 / d)
            loss_ref[...] = jnp.broadcast_to(total, loss_ref.shape)

    tile = pl.BlockSpec((tm, d), lambda i: (i, 0))
    row = pl.BlockSpec((1, d), lambda i: (0, 0))
    return pl.pallas_call(
        body, grid=(steps,), in_specs=[tile, row, tile],
        out_specs=[pl.BlockSpec((1, 128), lambda i: (0, 0)), tile, row],
        out_shape=[jax.ShapeDtypeStruct((1, 128), F32), jax.ShapeDtypeStruct((s, d), F32),
                   jax.ShapeDtypeStruct((1, d), F32)],
        scratch_shapes=[pltpu.VMEM((1, d), F32)], name=name, compiler_params=_cparams("arbitrary"))(x, g, target)


def _ssd_chunk(z, xbc_ext, dt_raw, conv_w, conv_b, dt_bias, a_log, d_skip, norm_g, h_in):
    q = z.shape[0]
    gw = SSD_HPG * HEAD_LANES
    xc = conv_b
    for k in range(SSD_CONV_K):
        xc = xc + _pick_row(conv_w, k) * _shift_rows(xbc_ext, SSD_CONV_K - 1 - k, CONV_HALO)
    xc = _silu(xc)
    dt = jax.nn.softplus(dt_raw + dt_bias)
    da = dt * (-jnp.exp(a_log))
    ri = lax.broadcasted_iota(jnp.int32, (q, q), 0)
    ci = lax.broadcasted_iota(jnp.int32, (q, q), 1)
    causal = ri >= ci
    tril = causal.astype(F32)
    a_cum = jnp.dot(tril, da, preferred_element_type=F32, precision=lax.Precision.HIGHEST)
    a_cum_t = lax.dot_general(da, tril, (((0,), (1,)), ((), ())), preferred_element_type=F32,
                              precision=lax.Precision.HIGHEST)
    a_last = _pick_row(a_cum, q - 1)
    head = _head_of_lane(gw)
    ys, hs = [], []
    for g in range(2):
        xs = xc[:, gw * g:gw * (g + 1)]
        bm = xc[:, SSD_INNER + SSD_STATE * g:SSD_INNER + SSD_STATE * (g + 1)]
        cm = xc[:, SSD_INNER + 2 * SSD_STATE + SSD_STATE * g:SSD_INNER + 2 * SSD_STATE + SSD_STATE * (g + 1)]
        cb = lax.dot_general(cm.astype(BF16), bm.astype(BF16), (((1,), (1,)), ((), ())), preferred_element_type=F32)
        cols = [_pick_lane(a_cum, SSD_HPG * g + j) for j in range(SSD_HPG)]
        lasts = [_pick_lane(a_last, SSD_HPG * g + j) for j in range(SSD_HPG)]
        dt_exp = sum(jnp.where(head == j, _pick_lane(dt, SSD_HPG * g + j), 0.0) for j in range(SSD_HPG))
        d_exp = sum(jnp.where(head == j, _pick_lane(d_skip, SSD_HPG * g + j), 0.0) for j in range(SSD_HPG))
        e_cum = sum(jnp.where(head == j, jnp.exp(cols[j]), 0.0) for j in range(SSD_HPG))
        c_dec = sum(jnp.where(head == j, jnp.exp(lasts[j]), 0.0) for j in range(SSD_HPG))
        xsdt = (xs * dt_exp).astype(BF16)
        y_diag = jnp.zeros((q, gw), F32)
        st_new = jnp.zeros((SSD_STATE, gw), F32)
        for j in range(SSD_HPG):
            row = _pick_row(a_cum_t, SSD_HPG * g + j)
            lmat = jnp.exp(jnp.where(causal, cols[j] - row, -jnp.inf))
            r = jnp.dot((cb * lmat).astype(BF16), xsdt, preferred_element_type=F32)
            y_diag = y_diag + jnp.where(head == j, r, 0.0)
            bd = (bm * jnp.exp(lasts[j] - cols[j])).astype(BF16)
            st = lax.dot_general(bd, xsdt, (((0,), (0,)), ((), ())), preferred_element_type=F32)
            st_new = st_new + jnp.where(head == j, st, 0.0)
        y_off = jnp.dot(cm.astype(BF16), h_in[g].astype(BF16), preferred_element_type=F32) * e_cum
        hs.append(h_in[g] * c_dec + st_new)
        y = y_diag + y_off + d_exp * xs
        yz = y * _silu(z[:, gw * g:gw * (g + 1)])
        yz = yz * lax.rsqrt(jnp.mean(yz * yz, axis=-1, keepdims=True) + NORM_EPS)
        ys.append(yz * norm_g[:, gw * g:gw * (g + 1)])
    return jnp.concatenate(ys, axis=1), tuple(hs)


_SSD_NCHUNK = SEQ // SSD_CHUNK
_HALO_PER_CHUNK = SSD_CHUNK // CONV_HALO


def _ssd_param_specs(const):
    return [pl.BlockSpec((8, SSD_CONV_CH), const), pl.BlockSpec((1, SSD_CONV_CH), const),
            pl.BlockSpec((1, 128), const), pl.BlockSpec((1, 128), const), pl.BlockSpec((1, 128), const),
            pl.BlockSpec((1, SSD_INNER), const)]


def _ssd_fwd(proj, conv_w, conv_b, dt_bias, a_log, d_skip, norm_g, *, name):
    q = SSD_CHUNK

    def body(z_ref, xbc_ref, halo_ref, dt_ref, cw_ref, cb_ref, db_ref, al_ref, d_ref, ng_ref, y_ref, hs_ref, h_acc):
        i = pl.program_id(0)

        @pl.when(i == 0)
        def _():
            h_acc[...] = jnp.zeros_like(h_acc)

        halo = jnp.where(i == 0, 0.0, halo_ref[...])
        xbc_ext = jnp.concatenate([halo, xbc_ref[...]], axis=0)
        h_in = (h_acc[0], h_acc[1])
        hs_ref[0, 0] = h_in[0]
        hs_ref[0, 1] = h_in[1]
        y, h_out = _ssd_chunk(z_ref[...], xbc_ext, dt_ref[...], cw_ref[...], cb_ref[...], db_ref[...], al_ref[...],
                              d_ref[...], ng_ref[...], h_in)
        y_ref[...] = y.astype(BF16)
        h_acc[0] = h_out[0]
        h_acc[1] = h_out[1]

    const = lambda i: (0, 0)
    return pl.pallas_call(
        body, grid=(_SSD_NCHUNK,),
        in_specs=[pl.BlockSpec((q, SSD_INNER), lambda i: (i, PROJ_Z_BLK)),
                  pl.BlockSpec((q, SSD_CONV_CH), lambda i: (i, 0)),
                  pl.BlockSpec((CONV_HALO, SSD_CONV_CH), lambda i: (jnp.maximum(i * _HALO_PER_CHUNK - 1, 0), 0)),
                  pl.BlockSpec((q, 128), lambda i: (i, PROJ_DT_BLK))] + _ssd_param_specs(const),
        out_specs=[pl.BlockSpec((q, SSD_INNER), lambda i: (i, 0)),
                   pl.BlockSpec((1, 2, SSD_STATE, 256), lambda i: (i, 0, 0, 0))],
        out_shape=[jax.ShapeDtypeStruct((SEQ, D_MODEL), BF16),
                   jax.ShapeDtypeStruct((_SSD_NCHUNK, 2, SSD_STATE, 256), F32)],
        scratch_shapes=[pltpu.VMEM((2, SSD_STATE, 256), F32)], name=name,
        compiler_params=_cparams("arbitrary"))(proj, proj, proj, proj, conv_w, conv_b, dt_bias, a_log, d_skip, norm_g)


def _ssd_bwd(proj, hstates, dmix, conv_w, conv_b, dt_bias, a_log, d_skip, norm_g, *, name):
    q = SSD_CHUNK
    last = _SSD_NCHUNK - 1

    def body(z_ref, xbc_ref, halo_ref, dt_ref, hs_ref, dy_ref, cw_ref, cb_ref, db_ref, al_ref, d_ref, ng_ref,
             dp_ref, dcw_ref, dcb_ref, ddb_ref, dal_ref, dd_ref, dng_ref, dh_acc, dhalo_acc):
        i = pl.program_id(0)

        @pl.when(i == 0)
        def _():
            dh_acc[...] = jnp.zeros_like(dh_acc)
            dhalo_acc[...] = jnp.zeros_like(dhalo_acc)
            for r in (dcw_ref, dcb_ref, ddb_ref, dal_ref, dd_ref, dng_ref):
                r[...] = jnp.zeros_like(r)

        halo = jnp.where(i == last, 0.0, halo_ref[...])
        xbc_ext = jnp.concatenate([halo, xbc_ref[...]], axis=0)
        _, vjp = jax.vjp(_ssd_chunk, z_ref[...], xbc_ext, dt_ref[...], cw_ref[...], cb_ref[...], db_ref[...],
                         al_ref[...], d_ref[...], ng_ref[...], (hs_ref[0, 0], hs_ref[0, 1]))
        gz, gx, gdt, gcw, gcb, gdb, gal, gd, gng, gh = vjp((dy_ref[...], (dh_acc[0], dh_acc[1])))
        dxbc = jnp.concatenate([gx[CONV_HALO:q], gx[q:] + dhalo_acc[...]], axis=0)
        dp_ref[...] = jnp.concatenate([dxbc, gz, gdt, jnp.zeros_like(gdt)], axis=1).astype(BF16)
        dhalo_acc[...] = gx[:CONV_HALO]
        dh_acc[0] = gh[0]
        dh_acc[1] = gh[1]
        dcw_ref[...] += gcw
        dcb_ref[...] += gcb
        ddb_ref[...] += gdb
        dal_ref[...] += gal
        dd_ref[...] += gd
        dng_ref[...] += gng

    const = lambda i: (0, 0)
    rev = lambda i: last - i
    row = lambda n: jax.ShapeDtypeStruct((1, n), F32)
    return pl.pallas_call(
        body, grid=(_SSD_NCHUNK,),
        in_specs=[pl.BlockSpec((q, SSD_INNER), lambda i: (rev(i), PROJ_Z_BLK)),
                  pl.BlockSpec((q, SSD_CONV_CH), lambda i: (rev(i), 0)),
                  pl.BlockSpec((CONV_HALO, SSD_CONV_CH), lambda i: (jnp.maximum(rev(i) * _HALO_PER_CHUNK - 1, 0), 0)),
                  pl.BlockSpec((q, 128), lambda i: (rev(i), PROJ_DT_BLK)),
                  pl.BlockSpec((1, 2, SSD_STATE, 256), lambda i: (rev(i), 0, 0, 0)),
                  pl.BlockSpec((q, SSD_INNER), lambda i: (rev(i), 0))] + _ssd_param_specs(const),
        out_specs=[pl.BlockSpec((q, PROJ_SSD_W), lambda i: (rev(i), 0))] + _ssd_param_specs(const),
        out_shape=[jax.ShapeDtypeStruct((SEQ, PROJ_W), BF16), jax.ShapeDtypeStruct((8, SSD_CONV_CH), F32),
                   row(SSD_CONV_CH), row(128), row(128), row(128), row(SSD_INNER)],
        scratch_shapes=[pltpu.VMEM((2, SSD_STATE, 256), F32), pltpu.VMEM((CONV_HALO, SSD_CONV_CH), F32)], name=name,
        compiler_params=_cparams("arbitrary"))(proj, proj, proj, proj, hstates, dmix, conv_w, conv_b, dt_bias, a_log,
                                                d_skip, norm_g)


def _rope_tables(pos_col, inv_freq_lane, *, name):
    s = pos_col.shape[0]

    def body(p_ref, f_ref, c_ref, s1_ref, s2_ref):
        ang = p_ref[...] * f_ref[...]
        within = lax.broadcasted_iota(jnp.int32, ang.shape, 1) % HEAD_LANES
        half = ROT_DIM // 2
        c_ref[...] = jnp.where(within < ROT_DIM, jnp.cos(ang), 1.0)
        sn = jnp.sin(ang)
        s1_ref[...] = jnp.where(within < half, -sn, 0.0)
        s2_ref[...] = jnp.where((within >= half) & (within < ROT_DIM), sn, 0.0)

    shp = jax.ShapeDtypeStruct((s, 128), F32)
    return pl.pallas_call(body, out_shape=[shp, shp, shp], name=name,
                          compiler_params=pltpu.CompilerParams(vmem_limit_bytes=VMEM_LIMIT))(pos_col, inv_freq_lane)


def _rope(t, c, s1, s2):
    half = ROT_DIM // 2
    return t * c + pltpu.roll(t, 128 - half, 1) * s1 + pltpu.roll(t, half, 1) * s2


def _rope_t(g, c, s1, s2):
    half = ROT_DIM // 2
    return g * c + pltpu.roll(g * s1, half, 1) + pltpu.roll(g * s2, 128 - half, 1)


def _att_valid(b):
    qi = lax.broadcasted_iota(jnp.int32, (ATT_BLOCK, 2 * ATT_BLOCK), 0)
    kj = lax.broadcasted_iota(jnp.int32, (ATT_BLOCK, 2 * ATT_BLOCK), 1)
    rel = qi + ATT_BLOCK - kj
    return (rel >= 0) & (rel <= ATT_BLOCK) & (b * ATT_BLOCK + kj - ATT_BLOCK >= 0)


def _att_slices(i, d):
    if d == 1:
        qstart = pl.multiple_of(i * ATT_BLOCK, ATT_BLOCK)
        return i, pl.ds(qstart, ATT_BLOCK), pl.ds(pl.multiple_of(qstart - ATT_BLOCK + ATT_KPAD, ATT_BLOCK), 2 * ATT_BLOCK)
    r = i % d
    b = i // d
    qstart = r + d * ATT_BLOCK * b
    return b, pl.ds(qstart, ATT_BLOCK, stride=d), pl.ds(qstart - ATT_BLOCK * d + ATT_KPAD, 2 * ATT_BLOCK, stride=d)


_ATT_NBLK = SEQ // ATT_BLOCK
_ATT_SCALE = HEAD_LANES ** -0.5
_ATT_UNROLL_FWD = 8
_ATT_UNROLL = 4


def _att_fwd(proj, cos, sin1, sin2, mix, *, name):
    s = SEQ

    def body(q_ref, k_ref, v_ref, c_ref, s1_ref, s2_ref, _, o_ref, lse_ref, mix_ref, qs, ks, vs, acc, m_s, l_s):
        c, s1, s2 = c_ref[...], s1_ref[...], s2_ref[...]
        qs[...] = _rope(q_ref[...], c, s1, s2) * _ATT_SCALE
        zeros = jnp.zeros((ATT_KPAD, 128), F32)
        ks[pl.ds(0, ATT_KPAD), :] = zeros
        vs[pl.ds(0, ATT_KPAD), :] = zeros
        ks[pl.ds(ATT_KPAD, s), :] = _rope(k_ref[...], c, s1, s2)
        vs[pl.ds(ATT_KPAD, s), :] = v_ref[...]
        head0 = _head_of_lane(128) == 0

        for bi, (_, d) in enumerate(ATT_PATTERNS):
            def blk(i, carry, d=d, first=(bi == 0)):
                b, sq, sk = _att_slices(i, d)
                qb = qs[sq, :]
                kw = ks[sk, :].astype(BF16)
                vw = vs[sk, :].astype(BF16)
                valid = _att_valid(b)
                ms, ls, os_ = [], [], []
                for hh in range(2):
                    qh = jnp.where(head0 if hh == 0 else ~head0, qb, 0.0).astype(BF16)
                    sc = lax.dot_general(qh, kw, (((1,), (1,)), ((), ())), preferred_element_type=F32)
                    sc = jnp.where(valid, sc, -jnp.inf)
                    mb = jnp.max(sc, axis=1, keepdims=True)
                    p = jnp.exp(sc - mb)
                    ms.append(mb)
                    ls.append(jnp.sum(p, axis=1, keepdims=True))
                    os_.append(jnp.dot(p.astype(BF16), vw, preferred_element_type=F32))
                m_b = jnp.where(head0, ms[0], ms[1])
                l_b = jnp.where(head0, ls[0], ls[1])
                o_b = jnp.where(head0, os_[0], os_[1])
                if first:
                    m_s[sq, :] = m_b
                    l_s[sq, :] = l_b
                    acc[sq, :] = o_b
                else:
                    m_old = m_s[sq, :]
                    m_new = jnp.maximum(m_old, m_b)
                    a_old = jnp.exp(m_old - m_new)
                    a_b = jnp.exp(m_b - m_new)
                    m_s[sq, :] = m_new
                    l_s[sq, :] = l_s[sq, :] * a_old + l_b * a_b
                    acc[sq, :] = acc[sq, :] * a_old + o_b * a_b
                return carry

            lax.fori_loop(0, _ATT_NBLK, blk, 0, unroll=_ATT_UNROLL_FWD)

        out = acc[...] / l_s[...]
        o_ref[...] = out
        mix_ref[...] = out.astype(BF16)
        lse_ref[...] = m_s[...] + jnp.log(l_s[...])

    col = lambda base: pl.BlockSpec((s, 128), lambda p: (0, base + p))
    tab = pl.BlockSpec((s, 128), lambda p: (0, 0))
    big = pltpu.VMEM((ATT_KPAD + s, 128), F32)
    tok = pltpu.VMEM((s, 128), F32)
    return pl.pallas_call(
        body, grid=(2,), in_specs=[col(PROJ_Q_BLK), col(PROJ_K_BLK), col(PROJ_V_BLK), tab, tab, tab, _HBM],
        out_specs=[pl.BlockSpec((s, 128), lambda p: (0, p)), pl.BlockSpec((s, 128), lambda p: (0, p)),
                   col(MIX_ATT_BLK)],
        out_shape=[jax.ShapeDtypeStruct((s, ATT_W), F32), jax.ShapeDtypeStruct((s, ATT_W), F32),
                   jax.ShapeDtypeStruct(mix.shape, mix.dtype)],
        input_output_aliases={6: 2}, scratch_shapes=[tok, big, big, tok, tok, tok], name=name,
        compiler_params=_cparams("arbitrary"))(proj, proj, proj, cos, sin1, sin2, mix)


def _att_bwd(proj, cos, sin1, sin2, out, lse, dmix, dproj, *, name):
    s = SEQ

    def body(proj_ref, c_hbm, s1_hbm, s2_hbm, out_hbm, lse_hbm, dmix_hbm, _, dproj_hbm,
             c_ref, s1_ref, s2_ref, o_ref, lse_ref, do_ref, qs, ks, vs, dqs, dks, dvs, staged, sems):
        def start(copies):
            for cp in copies:
                cp.start()
            return copies

        def load(pair):
            lanes = pl.ds(128 * pair, 128)
            rows = pl.ds(ATT_KPAD, s)
            return start([
                pltpu.make_async_copy(proj_ref.at[:, pl.ds(128 * (PROJ_Q_BLK + pair), 128)], qs, sems.at[0]),
                pltpu.make_async_copy(proj_ref.at[:, pl.ds(128 * (PROJ_K_BLK + pair), 128)], ks.at[rows, :], sems.at[1]),
                pltpu.make_async_copy(proj_ref.at[:, pl.ds(128 * (PROJ_V_BLK + pair), 128)], vs.at[rows, :], sems.at[2]),
                pltpu.make_async_copy(out_hbm.at[:, lanes], o_ref, sems.at[3]),
                pltpu.make_async_copy(lse_hbm.at[:, lanes], lse_ref, sems.at[4]),
                pltpu.make_async_copy(dmix_hbm.at[:, pl.ds(128 * (MIX_ATT_BLK + pair), 128)], do_ref, sems.at[5])])

        tables = start([pltpu.make_async_copy(c_hbm, c_ref, sems.at[6]),
                        pltpu.make_async_copy(s1_hbm, s1_ref, sems.at[7]),
                        pltpu.make_async_copy(s2_hbm, s2_ref, sems.at[8])])
        loads = load(0)
        for cp in tables:
            cp.wait()
        head0 = _head_of_lane(128) == 0
        zeros = jnp.zeros((ATT_KPAD, 128), F32)
        for pair in range(2):
            for cp in loads:
                cp.wait()
            c, s1, s2 = c_ref[...], s1_ref[...], s2_ref[...]
            qs[...] = _rope(qs[...], c, s1, s2) * _ATT_SCALE
            ks[pl.ds(0, ATT_KPAD), :] = zeros
            vs[pl.ds(0, ATT_KPAD), :] = zeros
            ks[pl.ds(ATT_KPAD, s), :] = _rope(ks[pl.ds(ATT_KPAD, s), :], c, s1, s2)
            dqs[...] = jnp.zeros_like(dqs)
            dks[...] = jnp.zeros_like(dks)
            dvs[...] = jnp.zeros_like(dvs)

            for _, d in ATT_PATTERNS:
                def blk(i, carry, d=d):
                    b, sq, sk = _att_slices(i, d)
                    qb = qs[sq, :]
                    kw = ks[sk, :].astype(BF16)
                    vw = vs[sk, :].astype(BF16)
                    dob = do_ref[sq, :]
                    lse_b = lse_ref[sq, :]
                    dd = dob * o_ref[sq, :]
                    valid = _att_valid(b)
                    dq_b = jnp.zeros((ATT_BLOCK, 128), F32)
                    dk_w = jnp.zeros((2 * ATT_BLOCK, 128), F32)
                    dv_w = jnp.zeros((2 * ATT_BLOCK, 128), F32)
                    for hh in range(2):
                        hm = head0 if hh == 0 else ~head0
                        qh = jnp.where(hm, qb, 0.0).astype(BF16)
                        doh = jnp.where(hm, dob, 0.0).astype(BF16)
                        lse_h = _pick_lane(lse_b, hh * HEAD_LANES)
                        d_h = jnp.sum(jnp.where(hm, dd, 0.0), axis=1, keepdims=True)
                        sc = lax.dot_general(qh, kw, (((1,), (1,)), ((), ())), preferred_element_type=F32)
                        p = jnp.where(valid, jnp.exp(sc - lse_h), 0.0)
                        dp = lax.dot_general(doh, vw, (((1,), (1,)), ((), ())), preferred_element_type=F32)
                        ds = (p * (dp - d_h)).astype(BF16)
                        dq_b = dq_b + jnp.where(hm, jnp.dot(ds, kw, preferred_element_type=F32), 0.0)
                        dk_w = dk_w + lax.dot_general(ds, qh, (((0,), (0,)), ((), ())), preferred_element_type=F32)
                        dv_w = dv_w + lax.dot_general(p.astype(BF16), doh, (((0,), (0,)), ((), ())),
                                                      preferred_element_type=F32)
                    dqs[sq, :] += dq_b
                    dks[sk, :] += dk_w
                    dvs[sk, :] += dv_w
                    return carry

                lax.fori_loop(0, _ATT_NBLK, blk, 0, unroll=_ATT_UNROLL)

            staged[0] = _rope_t(dqs[...] * _ATT_SCALE, c, s1, s2).astype(BF16)
            staged[1] = _rope_t(dks[pl.ds(ATT_KPAD, s), :], c, s1, s2).astype(BF16)
            staged[2] = dvs[pl.ds(ATT_KPAD, s), :].astype(BF16)
            stores = start([
                pltpu.make_async_copy(staged.at[j], dproj_hbm.at[:, pl.ds(128 * (col + pair), 128)], sems.at[9 + j])
                for j, col in enumerate((PROJ_Q_BLK, PROJ_K_BLK, PROJ_V_BLK))])
            if pair == 0:
                loads = load(1)
            for cp in stores:
                cp.wait()

    big = pltpu.VMEM((ATT_KPAD + s, 128), F32)
    tok = pltpu.VMEM((s, 128), F32)
    return pl.pallas_call(
        body, in_specs=[_HBM] * 8, out_specs=_HBM, out_shape=jax.ShapeDtypeStruct(dproj.shape, dproj.dtype),
        input_output_aliases={7: 0},
        scratch_shapes=[tok] * 6 + [tok, big, big, tok, big, big, pltpu.VMEM((3, s, 128), BF16),
                                    pltpu.SemaphoreType.DMA((12,))], name=name,
        compiler_params=pltpu.CompilerParams(vmem_limit_bytes=VMEM_LIMIT))(
            proj, cos, sin1, sin2, out, lse, dmix, dproj)


_POOL_TM = 512
_POOL_NT = SEQ // _POOL_TM
_POOL_HALO_PER_TILE = _POOL_TM // POOL_HALO


def _pool_tile(u_ext, w_bd, scale, t0):
    s2 = u_ext + _roll_rows(u_ext, 1)
    s4 = s2 + _roll_rows(s2, 2)
    s8 = s4 + _roll_rows(s4, 4)
    s16 = s8 + _roll_rows(s8, 8)
    grp = _head_of_lane(POOL_W)
    sel = jnp.where(grp == 0, s2, jnp.where(grp == 1, s4, jnp.where(grp == 2, s8, s16)))[POOL_HALO:]
    t = sel.shape[0]
    pos = t0 + lax.broadcasted_iota(jnp.int32, (t, POOL_W), 0) + 1
    win = jnp.where(grp == 0, 2, jnp.where(grp == 1, 4, jnp.where(grp == 2, 8, 16)))
    cnt = jnp.minimum(pos, win).astype(F32)
    diff = sel / cnt - u_ext[POOL_HALO:]
    return jnp.dot(diff.astype(BF16), w_bd.astype(BF16), preferred_element_type=F32) * scale


def _pool_fwd(proj, w_bd, scale, mix, *, name):
    tm = _POOL_TM

    def body(u_ref, halo_ref, w_ref, sc_ref, _, y_ref):
        i = pl.program_id(0)
        halo = jnp.where(i == 0, 0.0, halo_ref[...])
        u_ext = jnp.concatenate([halo, u_ref[...]], axis=0)
        y_ref[...] = _pool_tile(u_ext, w_ref[...], sc_ref[...], i * tm).astype(BF16)

    return pl.pallas_call(
        body, grid=(_POOL_NT,),
        in_specs=[pl.BlockSpec((tm, POOL_W), lambda i: (i, PROJ_POOL_BLK)),
                  pl.BlockSpec((POOL_HALO, POOL_W),
                               lambda i: (jnp.maximum(i * _POOL_HALO_PER_TILE - 1, 0), PROJ_POOL_BLK)),
                  pl.BlockSpec((POOL_W, POOL_W), lambda i: (0, 0)), pl.BlockSpec((1, POOL_W), lambda i: (0, 0)), _HBM],
        out_specs=pl.BlockSpec((tm, POOL_W), lambda i: (i, MIX_POOL_BLK)),
        out_shape=jax.ShapeDtypeStruct(mix.shape, mix.dtype), input_output_aliases={4: 0}, name=name,
        compiler_params=_cparams("parallel"))(proj, proj, w_bd, scale, mix)


def _pool_bwd(proj, dmix, w_bd, scale, dproj, *, name):
    tm = _POOL_TM
    last = _POOL_NT - 1

    def body(u_ref, halo_ref, dy_ref, w_ref, sc_ref, _, du_ref, dw_ref, dsc_ref, dhalo_acc):
        i = pl.program_id(0)

        @pl.when(i == 0)
        def _():
            dhalo_acc[...] = jnp.zeros_like(dhalo_acc)
            dw_ref[...] = jnp.zeros_like(dw_ref)
            dsc_ref[...] = jnp.zeros_like(dsc_ref)

        tile = last - i
        halo = jnp.where(tile == 0, 0.0, halo_ref[...])
        u_ext = jnp.concatenate([halo, u_ref[...]], axis=0)
        _, vjp = jax.vjp(functools.partial(_pool_tile, t0=tile * tm), u_ext, w_ref[...], sc_ref[...])
        gu, gw, gs = vjp(dy_ref[...])
        du_ref[...] = jnp.concatenate([gu[POOL_HALO:tm], gu[tm:] + dhalo_acc[...]], axis=0).astype(BF16)
        dhalo_acc[...] = gu[:POOL_HALO]
        dw_ref[...] += gw
        dsc_ref[...] += gs

    rev = lambda i: last - i
    return pl.pallas_call(
        body, grid=(_POOL_NT,),
        in_specs=[pl.BlockSpec((tm, POOL_W), lambda i: (rev(i), PROJ_POOL_BLK)),
                  pl.BlockSpec((POOL_HALO, POOL_W),
                               lambda i: (jnp.maximum(rev(i) * _POOL_HALO_PER_TILE - 1, 0), PROJ_POOL_BLK)),
                  pl.BlockSpec((tm, POOL_W), lambda i: (rev(i), MIX_POOL_BLK)),
                  pl.BlockSpec((POOL_W, POOL_W), lambda i: (0, 0)), pl.BlockSpec((1, POOL_W), lambda i: (0, 0)), _HBM],
        out_specs=[pl.BlockSpec((tm, POOL_W), lambda i: (rev(i), PROJ_POOL_BLK)),
                   pl.BlockSpec((POOL_W, POOL_W), lambda i: (0, 0)), pl.BlockSpec((1, POOL_W), lambda i: (0, 0))],
        out_shape=[jax.ShapeDtypeStruct(dproj.shape, dproj.dtype), jax.ShapeDtypeStruct((POOL_W, POOL_W), F32),
                   jax.ShapeDtypeStruct((1, POOL_W), F32)],
        input_output_aliases={5: 0}, scratch_shapes=[pltpu.VMEM((POOL_HALO, POOL_W), F32)], name=name,
        compiler_params=_cparams("arbitrary"))(proj, proj, dmix, w_bd, scale, dproj)


_FFN_TM = 256
_FFN_NT = SEQ // _FFN_TM
_FFN_SPLIT = 2
_FFN_HALO_PER_TILE = _FFN_TM // CONV_HALO


def _ffn_act_tile(hid_ext, conv_w, conv_b):
    hc = conv_b
    for k in range(FFN_CONV_K):
        hc = hc + _pick_row(conv_w, k) * _shift_rows(hid_ext, FFN_CONV_K - 1 - k, CONV_HALO)
    half = hid_ext.shape[1] // 2
    return _silu(hc[:, :half]) * hc[:, half:]


def _ffn_fwd(x1, norm_g, sc, sh, gate, up_t, down, conv_w, conv_b, *, name):
    tm = _FFN_TM
    w = 2 * FFN_DIM
    d = D_MODEL

    def body(x_ref, ng_ref, sc_ref, sh_ref, g_ref, up_ref, dn_ref, cw_ref, cb_ref,
             h_ref, hid_ref, act_ref, f_ref, x2_ref, halo_acc):
        i = pl.program_id(0)
        h2 = _rms_modulate(x_ref[...], ng_ref[...], sc_ref[...], sh_ref[...]).astype(BF16)
        h_ref[...] = h2
        hid = lax.dot_general(h2, up_ref[...], (((1,), (1,)), ((), ())), preferred_element_type=F32)
        hid_ref[...] = hid
        hw = FFN_DIM // _FFN_SPLIT
        f = jnp.zeros((tm, d), F32)
        for j in range(_FFN_SPLIT):
            gate_cols, up_cols = pl.ds(hw * j, hw), pl.ds(FFN_DIM + hw * j, hw)
            both = lambda ref: jnp.concatenate([ref[:, gate_cols], ref[:, up_cols]], axis=1)
            halo = jnp.where(i == 0, 0.0, both(halo_acc))
            act = _ffn_act_tile(jnp.concatenate([halo, both(hid_ref)], axis=0), both(cw_ref), both(cb_ref)).astype(BF16)
            act_ref[:, gate_cols] = act
            f = f + jnp.dot(act, dn_ref[gate_cols, :], preferred_element_type=F32)
        halo_acc[...] = hid[tm - CONV_HALO:]
        f_ref[...] = f
        x2_ref[...] = x_ref[...] + g_ref[...] * f

    tile = lambda n: pl.BlockSpec((tm, n), lambda i: (i, 0))
    return pl.pallas_call(
        body, grid=(_FFN_NT,),
        in_specs=[tile(d)] + [_resident((1, d))] * 4 + [_resident((w, d)), _resident((FFN_DIM, d)),
                                                        _resident((8, w)), _resident((1, w))],
        out_specs=[tile(d), tile(w), tile(FFN_DIM), tile(d), tile(d)],
        out_shape=[jax.ShapeDtypeStruct((SEQ, d), BF16), jax.ShapeDtypeStruct((SEQ, w), F32),
                   jax.ShapeDtypeStruct((SEQ, FFN_DIM), BF16), jax.ShapeDtypeStruct((SEQ, d), F32),
                   jax.ShapeDtypeStruct((SEQ, d), F32)],
        scratch_shapes=[pltpu.VMEM((CONV_HALO, w), F32)], name=name,
        compiler_params=_cparams("arbitrary"))(x1, norm_g, sc, sh, gate, up_t, down, conv_w, conv_b)


def _ffn_bwd(dx2, gate, f, hid, up_t, down, conv_w, conv_b, *, name):
    tm = _FFN_TM
    w = 2 * FFN_DIM
    d = D_MODEL
    last = _FFN_NT - 1

    def body(dx_ref, g_ref, f_ref, h_ref, halo_ref, up_ref, dn_ref, cw_ref, cb_ref,
             df_ref, dg_ref, dh_ref, dh2_ref, dcw_ref, dcb_ref, dhalo_acc):
        i = pl.program_id(0)

        @pl.when(i == 0)
        def _():
            dhalo_acc[...] = jnp.zeros_like(dhalo_acc)
            dcw_ref[...] = jnp.zeros_like(dcw_ref)
            dcb_ref[...] = jnp.zeros_like(dcb_ref)
            dg_ref[...] = jnp.zeros_like(dg_ref)

        dxv = dx_ref[...]
        df = (g_ref[...] * dxv).astype(BF16)
        df_ref[...] = df
        dg_ref[...] += jnp.sum(dxv * f_ref[...], axis=0, keepdims=True)
        hw = FFN_DIM // _FFN_SPLIT
        dh2 = jnp.zeros((tm, d), F32)
        for j in range(_FFN_SPLIT):
            gate_cols, up_cols = pl.ds(hw * j, hw), pl.ds(FFN_DIM + hw * j, hw)
            both = lambda ref, rows=slice(None): jnp.concatenate([ref[rows, gate_cols], ref[rows, up_cols]], axis=1)
            dact = lax.dot_general(df, dn_ref[gate_cols, :], (((1,), (1,)), ((), ())), preferred_element_type=F32)
            halo = jnp.where(i == last, 0.0, both(halo_ref))
            hid_ext = jnp.concatenate([halo, both(h_ref)], axis=0)
            _, vjp = jax.vjp(_ffn_act_tile, hid_ext, both(cw_ref), both(cb_ref))
            gh, gw, gb = vjp(dact)
            dhid = jnp.concatenate([gh[CONV_HALO:tm], gh[tm:] + both(dhalo_acc)], axis=0).astype(BF16)
            for cols, part in ((gate_cols, slice(0, hw)), (up_cols, slice(hw, 2 * hw))):
                dhalo_acc[:, cols] = gh[:CONV_HALO, part]
                dh_ref[:, cols] = dhid[:, part]
                dh2 = dh2 + jnp.dot(dhid[:, part], up_ref[cols, :], preferred_element_type=F32)
                dcw_ref[:, cols] += gw[:, part]
                dcb_ref[:, cols] += gb[:, part]
        dh2_ref[...] = dh2

    rev = lambda i: last - i
    tile = lambda n: pl.BlockSpec((tm, n), lambda i: (rev(i), 0))
    acc = lambda shape: pl.BlockSpec(shape, lambda i: (0, 0))
    return pl.pallas_call(
        body, grid=(_FFN_NT,),
        in_specs=[tile(d), _resident((1, d)), tile(d), tile(w),
                  pl.BlockSpec((CONV_HALO, w), lambda i: (jnp.maximum(rev(i) * _FFN_HALO_PER_TILE - 1, 0), 0)),
                  _resident((w, d)), _resident((FFN_DIM, d)), _resident((8, w)), _resident((1, w))],
        out_specs=[tile(d), acc((1, d)), tile(w), tile(d), acc((8, w)), acc((1, w))],
        out_shape=[jax.ShapeDtypeStruct((SEQ, d), BF16), jax.ShapeDtypeStruct((1, d), F32),
                   jax.ShapeDtypeStruct((SEQ, w), BF16), jax.ShapeDtypeStruct((SEQ, d), F32),
                   jax.ShapeDtypeStruct((8, w), F32), jax.ShapeDtypeStruct((1, w), F32)],
        scratch_shapes=[pltpu.VMEM((CONV_HALO, w), F32)], name=name,
        compiler_params=_cparams("arbitrary"))(dx2, gate, f, hid, hid, up_t, down, conv_w, conv_b)


def _axes():
    return lax.axis_index("x"), lax.axis_index("y"), lax.axis_index("c")


def _handshake(peers):
    barrier = pltpu.get_barrier_semaphore()
    for peer in peers:
        pl.semaphore_signal(barrier, inc=1, device_id=peer, device_id_type=MESH)
    pl.semaphore_wait(barrier, len(peers))


def _allgather_body(x_refs, out_refs, send_sems, recv_sems, local_sems, own_barrier):
    n = len(x_refs)
    x, y, c = _axes()
    me, sibling = (x, y, c), (x, y, 1 - c)
    chips = [(1 - x, y), (x, 1 - y), (1 - x, 1 - y)]
    if own_barrier:
        _handshake([sibling] + [(*chip, c) for chip in chips])

    def slot(a, px, py, pc):
        return out_refs[a].at[4 * px + 2 * py + pc]

    def copy(a, k, block, to, src=None):
        return pltpu.make_async_remote_copy(
            src_ref=slot(a, *block) if src is None else src, dst_ref=slot(a, *block),
            send_sem=send_sems.at[a, k], recv_sem=recv_sems.at[a, k], device_id=to, device_id_type=MESH)

    mines, firsts = [], []
    for a in range(n):
        mines.append(pltpu.make_async_copy(x_refs[a], slot(a, *me), local_sems.at[a]))
        mines[-1].start()
        first = [copy(a, 0, me, sibling, src=x_refs[a])]
        first += [copy(a, 1 + j, me, (*chip, c), src=x_refs[a]) for j, chip in enumerate(chips)]
        for cp in first:
            cp.start()
        firsts += first
    passed = []
    for j, chip in enumerate(chips):
        for a in range(n):
            copy(a, 1 + j, (*chip, c), me).wait_recv()
            passed.append(copy(a, 4 + j, (*chip, c), sibling))
            passed[-1].start()
    for a in range(n):
        copy(a, 0, sibling, me).wait_recv()
    for j, chip in enumerate(chips):
        for a in range(n):
            copy(a, 4 + j, (*chip, 1 - c), me).wait_recv()
    for cp in firsts + passed:
        cp.wait_send()
    for cp in mines:
        cp.wait()


def _allgather_sems(n):
    return [pltpu.SemaphoreType.DMA((n, 7)), pltpu.SemaphoreType.DMA((n, 7)), pltpu.SemaphoreType.DMA((n,))]


def _allgather(xs, *, name):
    n = len(xs)

    def body(*refs):
        _allgather_body(refs[:n], refs[n:2 * n], *refs[2 * n:], own_barrier=False)

    return pl.pallas_call(
        body, out_shape=[jax.ShapeDtypeStruct((N_DEV,) + xb.shape, xb.dtype) for xb in xs],
        in_specs=[_HBM] * n, out_specs=[_HBM] * n, scratch_shapes=_allgather_sems(n), name=name)(*xs)


def _allgather_async(xs, *, name, collective_id):
    n = len(xs)
    x_refs = [jax.new_ref(xb, memory_space=pltpu.MemorySpace.HBM) for xb in xs]
    out_refs = [jax.empty_ref(jax.ShapeDtypeStruct((N_DEV,) + xb.shape, xb.dtype), memory_space=pltpu.MemorySpace.HBM)
                for xb in xs]

    @pl.kernel(mesh=plsc.ScalarSubcoreMesh(axis_name="sequencer", num_cores=1), name=name,
               scratch_types=tuple(_allgather_sems(n)),
               compiler_params=pltpu.CompilerParams(collective_id=collective_id))
    def launch(send_sems, recv_sems, local_sems):
        _allgather_body(x_refs, out_refs, send_sems, recv_sems, local_sems, own_barrier=True)

    launch()
    return [r[...] for r in out_refs]


def _pair_exchange(blocks, *, name, collective_id):
    n = len(blocks)
    hbm = pltpu.MemorySpace.HBM
    in_refs = [jax.new_ref(b, memory_space=hbm) for b in blocks]
    out_refs = [jax.empty_ref(jax.ShapeDtypeStruct((4,) + b.shape[1:], b.dtype), memory_space=hbm) for b in blocks]

    @pl.kernel(mesh=plsc.ScalarSubcoreMesh(axis_name="sequencer", num_cores=1), name=name,
               scratch_types=(pltpu.SemaphoreType.DMA((n, 4)), pltpu.SemaphoreType.DMA((n, 4))),
               compiler_params=pltpu.CompilerParams(collective_id=collective_id))
    def launch(send_sems, recv_sems):
        x, y, c = _axes()
        _handshake([(x, y, 1 - c)])
        copies = [pltpu.make_async_remote_copy(
            src_ref=in_refs[a].at[2 * s + (1 - c)], dst_ref=out_refs[a].at[s], send_sem=send_sems.at[a, s],
            recv_sem=recv_sems.at[a, s], device_id=(x, y, 1 - c), device_id_type=MESH)
            for a in range(n) for s in range(4)]
        for cp in copies:
            cp.start()
        for cp in copies:
            cp.wait_recv()
        for cp in copies:
            cp.wait_send()

    launch()
    return [r[...] for r in out_refs]


def _chip_exchange(parts, *, name, collective_id):
    n = len(parts)
    hbm = pltpu.MemorySpace.HBM
    in_refs = [jax.new_ref(p, memory_space=hbm) for p in parts]
    out_refs = [jax.empty_ref(jax.ShapeDtypeStruct(p.shape, p.dtype), memory_space=hbm) for p in parts]

    @pl.kernel(mesh=plsc.ScalarSubcoreMesh(axis_name="sequencer", num_cores=1), name=name,
               scratch_types=(pltpu.SemaphoreType.DMA((n, 3)), pltpu.SemaphoreType.DMA((n, 3)),
                              pltpu.SemaphoreType.DMA((n,))),
               compiler_params=pltpu.CompilerParams(collective_id=collective_id))
    def launch(send_sems, recv_sems, local_sems):
        x, y, c = _axes()
        my_chip = 2 * x + y
        chips = [(1 - x, y), (x, 1 - y), (1 - x, 1 - y)]
        _handshake([(*chip, c) for chip in chips])
        locals_ = [pltpu.make_async_copy(in_refs[a].at[my_chip], out_refs[a].at[my_chip], local_sems.at[a])
                   for a in range(n)]
        for cp in locals_:
            cp.start()
        copies = [pltpu.make_async_remote_copy(
            src_ref=in_refs[a].at[2 * px + py], dst_ref=out_refs[a].at[my_chip], send_sem=send_sems.at[a, k],
            recv_sem=recv_sems.at[a, k], device_id=(px, py, c), device_id_type=MESH)
            for a in range(n) for k, (px, py) in enumerate(chips)]
        for cp in copies:
            cp.start()
        for cp in copies:
            cp.wait_recv()
        for cp in copies:
            cp.wait_send()
        for cp in locals_:
            cp.wait()

    launch()
    return [r[...] for r in out_refs]


def _pair_sum(core, blocks, from_sibling, *, name):
    n = len(blocks)

    def body(core_ref, *refs):
        for a_ref, b_ref, o_ref in zip(refs[:n], refs[n:2 * n], refs[2 * n:]):
            o_ref[...] = (a_ref[...].astype(F32) + b_ref[...].astype(F32)).astype(o_ref.dtype)

    mine = lambda b: pl.BlockSpec((1,) + b.shape[1:], lambda s, core_ref: (2 * s + core_ref[0], 0, 0))
    slot = lambda b: pl.BlockSpec((1,) + b.shape[1:], lambda s, core_ref: (s, 0, 0))
    return pl.pallas_call(
        body,
        grid_spec=pltpu.PrefetchScalarGridSpec(
            num_scalar_prefetch=1, grid=(4,),
            in_specs=[mine(b) for b in blocks] + [slot(b) for b in blocks], out_specs=[slot(b) for b in blocks]),
        out_shape=[jax.ShapeDtypeStruct(s.shape, s.dtype) for s in from_sibling], name=name,
        compiler_params=_cparams("parallel"))(core, *blocks, *from_sibling)


def _sum_blocks(a, *, name, tr=None):
    n, r, cdim = a.shape
    tr = tr or r

    def body(a_ref, o_ref):
        acc = a_ref[0].astype(F32)
        for k in range(1, n):
            acc = acc + a_ref[k].astype(F32)
        o_ref[...] = acc

    return pl.pallas_call(body, grid=(r // tr,), in_specs=[pl.BlockSpec((n, tr, cdim), lambda i: (0, i, 0))],
                          out_specs=pl.BlockSpec((tr, cdim), lambda i: (i, 0)),
                          out_shape=jax.ShapeDtypeStruct((r, cdim), F32), name=name,
                          compiler_params=_cparams("parallel"))(a)


def _sum_gathered(gathered, *, name):
    n = len(gathered)

    def body(*refs):
        for a_ref, o_ref in zip(refs[:n], refs[n:]):
            acc = a_ref[0]
            for k in range(1, N_DEV):
                acc = acc + a_ref[k]
            o_ref[...] = acc

    return pl.pallas_call(body, out_shape=[jax.ShapeDtypeStruct(g.shape[1:], F32) for g in gathered], name=name,
                          compiler_params=pltpu.CompilerParams(vmem_limit_bytes=VMEM_LIMIT))(*gathered)


_ADA_SHARD = 6 * D_MODEL // N_DEV


def _ada_mod(c_all, ada_w, *, name):
    def body(c_ref, w_ref, o_ref):
        o_ref[0] = jnp.dot(_silu(c_ref[...]).astype(BF16), w_ref[0].astype(BF16), preferred_element_type=F32)

    return pl.pallas_call(
        body, grid=(DEPTH,),
        in_specs=[pl.BlockSpec((N_DEV, D_MODEL), lambda l: (0, 0)),
                  pl.BlockSpec((1, D_MODEL, _ADA_SHARD), lambda l: (l, 0, 0))],
        out_specs=pl.BlockSpec((1, N_DEV, _ADA_SHARD), lambda l: (l, 0, 0)),
        out_shape=jax.ShapeDtypeStruct((DEPTH, N_DEV, _ADA_SHARD), F32), name=name,
        compiler_params=_cparams("parallel"))(c_all, ada_w)


def _ada_wgrad(c_all, dmod_cols, *, name):
    def body(c_ref, d_ref, o_ref):
        o_ref[0] = lax.dot_general(_silu(c_ref[...]), d_ref[0], (((0,), (0,)), ((), ())),
                                   preferred_element_type=F32, precision=lax.Precision.HIGHEST)

    return pl.pallas_call(
        body, grid=(DEPTH,),
        in_specs=[pl.BlockSpec((N_DEV, D_MODEL), lambda l: (0, 0)),
                  pl.BlockSpec((1, N_DEV, _ADA_SHARD), lambda l: (l, 0, 0))],
        out_specs=pl.BlockSpec((1, D_MODEL, _ADA_SHARD), lambda l: (l, 0, 0)),
        out_shape=jax.ShapeDtypeStruct((DEPTH, D_MODEL, _ADA_SHARD), F32), name=name,
        compiler_params=_cparams("parallel"))(c_all, dmod_cols)


def _add_rows(a, b, *, name):
    def body(a_ref, b_ref, o_ref):
        o_ref[...] = a_ref[...] + b_ref[...]

    return pl.pallas_call(body, out_shape=jax.ShapeDtypeStruct(a.shape, a.dtype), name=name)(a, b)


def _adamw_update(w_ref, g_ref, m_ref, v_ref, d_ref, mo_ref, vo_ref):
    gv = g_ref[...]
    mn = ADAM_B1 * m_ref[...] + (1.0 - ADAM_B1) * gv
    vn = ADAM_B2 * v_ref[...] + (1.0 - ADAM_B2) * (gv * gv)
    mo_ref[...] = mn
    vo_ref[...] = vn
    m_hat = mn / (1.0 - ADAM_B1 ** ADAM_STEP)
    v_hat = vn / (1.0 - ADAM_B2 ** ADAM_STEP)
    d_ref[...] = -ADAM_LR * (m_hat / (jnp.sqrt(v_hat) + ADAM_EPS) + ADAM_WD * w_ref[...])


def _adamw_small(ws, gs, ms, vs, *, name):
    n = len(ws)

    def body(*refs):
        ins, outs = refs[:4 * n], refs[4 * n:]
        for i in range(n):
            _adamw_update(ins[i], ins[n + i], ins[2 * n + i], ins[3 * n + i], outs[i], outs[n + i], outs[2 * n + i])

    shapes = [jax.ShapeDtypeStruct(a.shape, F32) for a in ws]
    outs = pl.pallas_call(body, out_shape=shapes * 3, name=name,
                          compiler_params=pltpu.CompilerParams(vmem_limit_bytes=VMEM_LIMIT))(*ws, *gs, *ms, *vs)
    return outs[:n], outs[n:2 * n], outs[2 * n:]


def _adamw_stack(w, g, m, v, *, name, tr):
    layers, r, cdim = w.shape
    body = functools.partial(_adamw_update)
    spec = pl.BlockSpec((1, tr, cdim), lambda l, i: (l, i, 0))
    shp = jax.ShapeDtypeStruct(w.shape, F32)
    return pl.pallas_call(body, grid=(layers, r // tr), in_specs=[spec] * 4, out_specs=[spec] * 3,
                          out_shape=[shp] * 3, name=name, compiler_params=_cparams("parallel", "parallel"))(w, g, m, v)


def _adamw(w, g, m, v, *, name, tr):
    r, cdim = w.shape
    body = functools.partial(_adamw_update)

    spec = pl.BlockSpec((tr, cdim), lambda i: (i, 0))
    shp = jax.ShapeDtypeStruct((r, cdim), F32)
    return pl.pallas_call(body, grid=(r // tr,), in_specs=[spec] * 4, out_specs=[spec] * 3, out_shape=[shp] * 3,
                          name=name, compiler_params=_cparams("parallel"))(w, g, m, v)


def _pad_rows(a, rows):
    return jnp.concatenate([a, jnp.zeros((rows - a.shape[0],) + a.shape[1:], a.dtype)], axis=0)


def _pad_lanes(a, lanes):
    return jnp.concatenate([a, jnp.zeros(a.shape[:-1] + (lanes - a.shape[-1],), a.dtype)], axis=-1)


def _permute_w_in(wt):
    return jnp.concatenate([wt[512:1536], wt[:512], wt[1536:1544],
                            jnp.zeros((PROJ_W - IN_W, wt.shape[1]), wt.dtype), wt[1544:]], axis=0)


def _unpermute_w_in(wp):
    return jnp.concatenate([wp[1024:1536], wp[:1024], wp[1536:1544], wp[PROJ_SSD_W:]], axis=0)


def _block_diag(w):
    rows = []
    for g in range(4):
        rows.append(jnp.concatenate([w[g] if k == g else jnp.zeros_like(w[g]) for k in range(4)], axis=1))
    return jnp.concatenate(rows, axis=0)


def _diag_blocks(wbd):
    return jnp.stack([wbd[64 * g:64 * (g + 1), 64 * g:64 * (g + 1)] for g in range(4)], axis=0)


def _layer_params(l, small):
    return dict(
        norm1_g=small["norm1_g"][l][None], norm2_g=small["norm2_g"][l][None],
        conv_w=_pad_rows(small["ssd_conv_w"][l], 8), conv_b=small["ssd_conv_b"][l][None],
        dt_bias=_pad_lanes(small["ssd_dt_bias"][l][None], 128), a_log=_pad_lanes(small["ssd_a_log"][l][None], 128),
        d_skip=_pad_lanes(small["ssd_d"][l][None], 128), ssd_norm_g=small["ssd_norm_g"][l][None],
        pool_bd=_block_diag(small["pool_w"][l]), pool_scale=small["pool_scale"][l][None],
        fcw=_pad_rows(small["ffn_conv_w"][l], 8), fcb=small["ffn_conv_b"][l][None])


def _mod_rows(mod_l):
    return [mod_l[None, D_MODEL * i:D_MODEL * (i + 1)] for i in range(6)]


def _layer_fwd(x, mod_l, p, tabs, l, gather):
    sh1, sc1, g1, sh2, sc2, g2 = _mod_rows(mod_l)
    mix_w = gather(l, "mix", None)
    p.update(w_in=mix_w["w_in"], w_out=mix_w["w_out"])
    proj, h1 = _mm(x, p["w_in"], nt=True, norm=(p["norm1_g"], sc1, sh1), name=f"l{l}_proj")
    ffn_w = gather(l, "ffn", proj)
    p.update(up=ffn_w["ffn_up"], down=ffn_w["ffn_down"])
    mix, hst = _ssd_fwd(proj, p["conv_w"], p["conv_b"], p["dt_bias"], p["a_log"], p["d_skip"], p["ssd_norm_g"],
                        name=f"l{l}_ssd")
    mix = _pool_fwd(proj, p["pool_bd"], p["pool_scale"], mix, name=f"l{l}_pool")
    y_att, lse, mix = _att_fwd(proj, *tabs, mix, name=f"l{l}_att")
    mo, x1 = _mm(mix, p["w_out"], residual=(x, g1), name=f"l{l}_out")
    gather(l + 1, "mix", (x1, p["up"]))
    h2, hid, act, f, x2 = _ffn_fwd(x1, p["norm2_g"], sc2, sh2, g2, p["up"], p["down"], p["fcw"], p["fcb"],
                                   name=f"l{l}_ffn")
    return x2, dict(x=x, h1=h1, proj=proj, hst=hst, y_att=y_att, lse=lse, mix=mix, mo=mo, x1=x1, h2=h2, hid=hid,
                    act=act, f=f)


def _layer_bwd(dx2, sv, mod_l, p, tabs, l, exchange):
    sh1, sc1, g1, sh2, sc2, g2 = _mod_rows(mod_l)
    df, dg2, dhid, dh2, dfcw, dfcb = _ffn_bwd(dx2, g2, sv["f"], sv["hid"], p["up"], p["down"], p["fcw"], p["fcb"],
                                              name=f"l{l}_ffn_b")
    d_down = _wgrad(sv["act"], df, tk=1408, name=f"l{l}_down_bw")
    d_up = _wgrad(dhid, sv["h2"], tk=1408, name=f"l{l}_up_bw")
    finish_ffn = exchange(l, "ffn", dict(ffn_up=d_up, ffn_down=d_down))
    dx1, dn2, dsc2, dsh2, dmix, dmo, dg1 = _mid_bwd(sv["x1"], dh2, dx2, p["norm2_g"], sc2, g1, sv["mo"], p["w_out"],
                                                    name=f"l{l}_mid_b")
    finish_ffn(dx1)
    d_wout = _wgrad(sv["mix"], dmo, name=f"l{l}_out_bw")
    dproj, dcw, dcb, ddb, dal, dd, dng = _ssd_bwd(
        sv["proj"], sv["hst"], dmix, p["conv_w"], p["conv_b"], p["dt_bias"], p["a_log"], p["d_skip"],
        p["ssd_norm_g"], name=f"l{l}_ssd_b")
    dproj, dwbd, dpsc = _pool_bwd(sv["proj"], dmix, p["pool_bd"], p["pool_scale"], dproj, name=f"l{l}_pool_b")
    dproj = _att_bwd(sv["proj"], *tabs, sv["y_att"], sv["lse"], dmix, dproj, name=f"l{l}_att_b")
    d_win = _wgrad(dproj, sv["h1"], tk=1408, name=f"l{l}_proj_bw")
    finish_mix = exchange(l, "mix", dict(w_in=d_win, w_out=d_wout))
    dx0, dn1, dsc1, dsh1 = _norm_mod_bwd(sv["x"], dproj, dx1, p["norm1_g"], sc1, w=p["w_in"], name=f"l{l}_proj_b")
    dx0, _ = lax.optimization_barrier((dx0, (d_win, d_wout, d_up, d_down)))
    dmod = jnp.concatenate([dsh1, dsc1, dg1, dsh2, dsc2, dg2], axis=1)[0]
    small = dict(norm1_g=dn1[0], ssd_conv_w=dcw[:SSD_CONV_K], ssd_conv_b=dcb[0], ssd_dt_bias=ddb[0], ssd_a_log=dal[0],
                 ssd_d=dd[0], ssd_norm_g=dng[0], pool_w=_diag_blocks(dwbd), pool_scale=dpsc[0], norm2_g=dn2[0],
                 ffn_conv_w=dfcw[:FFN_CONV_K], ffn_conv_b=dfcb[0])
    return dx0, dmod, small, finish_mix


def _example_step(x, target, pos_col, inv_freq_lane, mod, gather, small, final_g, exchange):
    tabs = _rope_tables(pos_col, inv_freq_lane, name="rope_tables")
    params, saved = [], []
    for l in range(DEPTH):
        params.append(_layer_params(l, small))
        x, sv = _layer_fwd(x, mod[l], params[l], tabs, l, gather)
        saved.append(sv)
    loss_row, dx, dfg = _final_loss(x, final_g[None], target, name="final_loss")
    dmods, smalls = [None] * DEPTH, [None] * DEPTH
    for l in reversed(range(DEPTH)):
        dx, dmods[l], smalls[l], finish_mix = _layer_bwd(dx, saved[l], mod[l], params[l], tabs, l, exchange)
        if l > 0:
            finish_mix(dx)
    return loss_row, dx, jnp.stack(dmods, axis=0), smalls, dfg[0], finish_mix


_BIG = ("w_in", "w_out", "ffn_up", "ffn_down")
_SMALL_GRADS = ("norm1_g", "ssd_conv_w", "ssd_conv_b", "ssd_dt_bias", "ssd_a_log", "ssd_d", "ssd_norm_g", "pool_w",
                "pool_scale", "norm2_g", "ffn_conv_w", "ffn_conv_b")
_SMALL_PARAMS = ("ada_b", "norm1_g", "ssd_conv_w", "ssd_conv_b", "ssd_dt_bias", "ssd_a_log", "ssd_d", "ssd_norm_g",
                 "pool_w", "pool_scale", "norm2_g", "ffn_conv_w", "ffn_conv_b", "final_g")
_WEIGHT_ORDER = ("ada_w", "ada_b", "norm1_g", "w_in", "ssd_conv_w", "ssd_conv_b", "ssd_dt_bias", "ssd_a_log", "ssd_d",
                 "ssd_norm_g", "pool_w", "pool_scale", "w_out", "norm2_g", "ffn_up", "ffn_conv_w", "ffn_conv_b",
                 "ffn_down", "final_g")


_COLUMN_SHARDED = ("w_in", "ffn_up")
_GROUPS = (("mix", ("w_in", "w_out")), ("ffn", ("ffn_up", "ffn_down")))


def _big_shares(w, l, names):
    return [(w[name][l].T if name in _COLUMN_SHARDED else w[name][l]).astype(BF16) for name in names]


def _unshard_big(names, gathered):
    out = {}
    for name, g in zip(names, gathered):
        full = g.reshape(N_DEV * g.shape[1], g.shape[2])
        out[name] = _permute_w_in(full) if name == "w_in" else full
    return out


def _shard_big(grads):
    out = []
    for name, g in grads.items():
        g = _unpermute_w_in(g) if name == "w_in" else g
        out.append(g.reshape(N_DEV, g.shape[0] // N_DEV, g.shape[1]))
    return out


def kernel(x, c, positions, ada_w, ada_b, norm1_g, w_in, ssd_conv_w, ssd_conv_b, ssd_dt_bias, ssd_a_log, ssd_d, ssd_norm_g, pool_w, pool_scale, w_out, norm2_g, ffn_up, ffn_conv_w, ffn_conv_b, ffn_down, final_g, loss_target, m_ada_w, m_ada_b, m_norm1_g, m_w_in, m_ssd_conv_w, m_ssd_conv_b, m_ssd_dt_bias, m_ssd_a_log, m_ssd_d, m_ssd_norm_g, m_pool_w, m_pool_scale, m_w_out, m_norm2_g, m_ffn_up, m_ffn_conv_w, m_ffn_conv_b, m_ffn_down, m_final_g, v_ada_w, v_ada_b, v_norm1_g, v_w_in, v_ssd_conv_w, v_ssd_conv_b, v_ssd_dt_bias, v_ssd_a_log, v_ssd_d, v_ssd_norm_g, v_pool_w, v_pool_scale, v_w_out, v_norm2_g, v_ffn_up, v_ffn_conv_w, v_ffn_conv_b, v_ffn_down, v_final_g):
    w = dict(ada_w=ada_w, ada_b=ada_b, norm1_g=norm1_g, w_in=w_in, ssd_conv_w=ssd_conv_w, ssd_conv_b=ssd_conv_b,
             ssd_dt_bias=ssd_dt_bias, ssd_a_log=ssd_a_log, ssd_d=ssd_d, ssd_norm_g=ssd_norm_g, pool_w=pool_w,
             pool_scale=pool_scale, w_out=w_out, norm2_g=norm2_g, ffn_up=ffn_up, ffn_conv_w=ffn_conv_w,
             ffn_conv_b=ffn_conv_b, ffn_down=ffn_down, final_g=final_g)
    m = dict(ada_w=m_ada_w, ada_b=m_ada_b, norm1_g=m_norm1_g, w_in=m_w_in, ssd_conv_w=m_ssd_conv_w,
             ssd_conv_b=m_ssd_conv_b, ssd_dt_bias=m_ssd_dt_bias, ssd_a_log=m_ssd_a_log, ssd_d=m_ssd_d,
             ssd_norm_g=m_ssd_norm_g, pool_w=m_pool_w, pool_scale=m_pool_scale, w_out=m_w_out, norm2_g=m_norm2_g,
             ffn_up=m_ffn_up, ffn_conv_w=m_ffn_conv_w, ffn_conv_b=m_ffn_conv_b, ffn_down=m_ffn_down,
             final_g=m_final_g)
    v = dict(ada_w=v_ada_w, ada_b=v_ada_b, norm1_g=v_norm1_g, w_in=v_w_in, ssd_conv_w=v_ssd_conv_w,
             ssd_conv_b=v_ssd_conv_b, ssd_dt_bias=v_ssd_dt_bias, ssd_a_log=v_ssd_a_log, ssd_d=v_ssd_d,
             ssd_norm_g=v_ssd_norm_g, pool_w=v_pool_w, pool_scale=v_pool_scale, w_out=v_w_out, norm2_g=v_norm2_g,
             ffn_up=v_ffn_up, ffn_conv_w=v_ffn_conv_w, ffn_conv_b=v_ffn_conv_b, ffn_down=v_ffn_down,
             final_g=v_final_g)
    ix, iy, ic = _axes()
    dev = 4 * ix + 2 * iy + ic

    c_all, scw, fcw = _allgather([c, ssd_conv_w.reshape(DEPTH * SSD_CONV_K, -1),
                                  ffn_conv_w.reshape(DEPTH * FFN_CONV_K, -1)], name="gather_small")
    small_all = c_all
    c_all = c_all.reshape(N_DEV, D_MODEL)
    scw = scw.reshape(N_DEV, DEPTH, SSD_CONV_K, -1).transpose(1, 2, 0, 3).reshape(DEPTH, SSD_CONV_K, SSD_CONV_CH)
    fcw = fcw.reshape(N_DEV, DEPTH, FFN_CONV_K, -1).transpose(1, 2, 0, 3).reshape(DEPTH, FFN_CONV_K, 2 * FFN_DIM)

    mod_cols = _ada_mod(c_all, ada_w, name="ada_mod")
    mod_all = _allgather([mod_cols.reshape(DEPTH * N_DEV, _ADA_SHARD)], name="gather_mod")[0]
    mod_all = mod_all.reshape(N_DEV, DEPTH, N_DEV, _ADA_SHARD)
    mod_mine = lax.dynamic_index_in_dim(mod_all, dev, axis=2, keepdims=False)
    mod = _add_rows(mod_mine.transpose(1, 0, 2).reshape(DEPTH, 6 * D_MODEL), ada_b, name="ada_bias")

    fetched = {}

    def gather(l, group, after):
        if l < DEPTH and (l, group) not in fetched:
            names = dict(_GROUPS)[group]
            shares, _ = lax.optimization_barrier((_big_shares(w, l, names), small_all if after is None else after))
            got = _allgather_async(shares, name=f"gather_weights_l{l}_{group}",
                                   collective_id=1 + 2 * l + (group == "ffn"))
            fetched[l, group] = _unshard_big(names, got)
        return fetched.get((l, group))

    core = ic.astype(jnp.int32).reshape(1)
    from_chips = {}

    def exchange(l, group, g):
        cid = 5 + 4 * l + 2 * (group == "mix")
        blocks = _shard_big(g)
        if from_chips:
            blocks, _ = lax.optimization_barrier((blocks, list(from_chips.values())))
        from_sibling = _pair_exchange(blocks, name=f"grads_pair_exchange_l{l}_{group}", collective_id=cid)

        def finish(after):
            theirs, _ = lax.optimization_barrier((from_sibling, after))
            parts = _pair_sum(core, blocks, theirs, name=f"grads_pair_sum_l{l}_{group}")
            got = _chip_exchange(parts, name=f"grads_chip_exchange_l{l}_{group}", collective_id=cid + 1)
            from_chips.update({(l, name): t for name, t in zip(g, got)})

        return finish

    small = dict(norm1_g=norm1_g, norm2_g=norm2_g, ssd_conv_w=scw, ssd_conv_b=ssd_conv_b, ssd_dt_bias=ssd_dt_bias,
                 ssd_a_log=ssd_a_log, ssd_d=ssd_d, ssd_norm_g=ssd_norm_g, pool_w=pool_w, pool_scale=pool_scale,
                 ffn_conv_w=fcw, ffn_conv_b=ffn_conv_b)

    inv_freq = ROPE_THETA ** (-jnp.arange(0, ROT_DIM, 2, dtype=F32) / ROT_DIM)
    lane = jnp.arange(128) % HEAD_LANES
    inv_freq_lane = jnp.where(lane < ROT_DIM, inv_freq[lane % (ROT_DIM // 2)], 0.0)[None, :]
    pos_col = positions.reshape(SEQ, 1).astype(F32)
    loss_row, dx, dmod, g_small, g_final, finish_last = _example_step(
        x[0], loss_target[0], pos_col, inv_freq_lane, mod, gather, small, final_g, exchange)

    small_names = list(_SMALL_GRADS)
    stacked = [jnp.stack([g_small[l][name] for l in range(DEPTH)], axis=0) for name in small_names]
    small_parts = [loss_row, dmod] + [s.reshape(-1, s.shape[-1]) for s in stacked] + [g_final[None]]
    gathered = _allgather_async(small_parts, name="gather_small_grads", collective_id=13)
    finish_last(gathered[0])

    grads = {}
    summed = {}
    for name in _BIG:
        summed[name] = jnp.stack([_sum_blocks(from_chips[l, name], name=f"grads_chip_sum_l{l}_{name}")
                                  for l in range(DEPTH)], axis=0)
        grads[name] = summed[name].transpose(0, 2, 1) if name in _COLUMN_SHARDED else summed[name]

    total = _sum_gathered(gathered, name="sum_small_grads")
    loss = total[0][0, 0]
    grads["ada_b"] = total[1]
    grads.update(zip(small_names, total[2:-1]))
    grads["final_g"] = total[-1][0]
    dmod_cols = lax.dynamic_slice_in_dim(gathered[1], dev * _ADA_SHARD, _ADA_SHARD, axis=2).transpose(1, 0, 2)
    grads["ada_w"] = _ada_wgrad(c_all, dmod_cols, name="ada_wgrad")
    for name in ("ssd_dt_bias", "ssd_a_log", "ssd_d"):
        grads[name] = grads[name][:, :SSD_HEADS]
    grads["pool_w"] = grads["pool_w"].reshape(pool_w.shape)
    grads["ssd_conv_w"] = lax.dynamic_slice_in_dim(
        grads["ssd_conv_w"].reshape(DEPTH, SSD_CONV_K, SSD_CONV_CH), dev * ssd_conv_w.shape[2], ssd_conv_w.shape[2], axis=2)
    grads["ffn_conv_w"] = lax.dynamic_slice_in_dim(
        grads["ffn_conv_w"].reshape(DEPTH, FFN_CONV_K, 2 * FFN_DIM), dev * ffn_conv_w.shape[2], ffn_conv_w.shape[2], axis=2)

    delta, new_m, new_v = {}, {}, {}
    for name, tr in (("ada_w", 512), ("w_out", 256), ("ffn_down", 352)):
        shp = w[name].shape
        two_d = lambda a: a.reshape(shp[0] * shp[1], shp[2])
        d_, m_, v_ = _adamw(two_d(w[name]), two_d(grads[name]), two_d(m[name]), two_d(v[name]), tr=tr,
                               name=f"adamw_{name}")
        delta[name], new_m[name], new_v[name] = (t.reshape(shp) for t in (d_, m_, v_))
    for name, tr in (("w_in", 321), ("ffn_up", 352)):
        flip = lambda a: a.transpose(0, 2, 1)
        outs = _adamw_stack(flip(w[name]), summed[name], flip(m[name]), flip(v[name]), tr=tr, name=f"adamw_{name}")
        delta[name], new_m[name], new_v[name] = (flip(t) for t in outs)
    two_d = lambda a: a.reshape(-1, a.shape[-1])
    outs = _adamw_small(*[[two_d(t[name]) for name in _SMALL_PARAMS] for t in (w, grads, m, v)], name="adamw_small")
    for name, d_, m_, v_ in zip(_SMALL_PARAMS, *outs):
        delta[name], new_m[name], new_v[name] = (t.reshape(w[name].shape) for t in (d_, m_, v_))

    grad_x = dx[None]
    return (loss, grad_x, *[grads[n].reshape(w[n].shape) for n in _WEIGHT_ORDER],
            *[delta[n] for n in _WEIGHT_ORDER], *[new_m[n] for n in _WEIGHT_ORDER],
            *[new_v[n] for n in _WEIGHT_ORDER])
```
